```python
import math
import jax, jax.numpy as jnp
from jax import lax
import numpy as np

D_MODEL = 1024
BATCH = 8
SEQ = 2048
DEPTH = 4

MEM_LEN = 256
EXPAND = 2
MIX_WIDTH = EXPAND * D_MODEL
A_WIDTH = MIX_WIDTH // 2
A_HEAD_DIM = 64
A_HEADS = A_WIDTH // A_HEAD_DIM
A_PATTERNS = ((128, 1), (512, 4), (2048, 16))
A_BLOCK = 128
B_WIDTH = MIX_WIDTH - A_WIDTH
POOL_WINDOWS = (2, 4, 8, 16)
B_GROUP = B_WIDTH // len(POOL_WINDOWS)
C_WIDTH = D_MODEL
C_CHUNK = 128
C_GROUPS = 4
C_GROUP_DIM = C_WIDTH // C_GROUPS
D_WIDTH = D_MODEL // 2
S5_GROUP_DIM = 16
S5_GROUPS = D_WIDTH // S5_GROUP_DIM
S5_STATE = 64
X_HEADS = 4
X_HEAD_DIM = D_MODEL // X_HEADS
N_EVEN = (DEPTH + 1) // 2
N_ODD = DEPTH // 2
EPS = 1e-6
NEG = -1e30

kernel_name = 'hybrid_dilated_pool_sgu_s5_trunk'


def rms_norm(x, g):
    xf = x.astype(jnp.float32)
    y = xf * lax.rsqrt(jnp.mean(xf * xf, axis=-1, keepdims=True) + EPS)
    return (y * g.astype(jnp.float32)).astype(x.dtype)


def _dilated_pattern(q, k, v, window, dilation):
    b, s, h, dh = q.shape
    d = dilation
    L = s // d
    w = window // d
    nb = -(-L // A_BLOCK)
    lp = nb * A_BLOCK
    n = b * d

    def to_dilated(t):
        t = t.reshape(b, L, d, h, dh).transpose(0, 2, 1, 3, 4).reshape(n, L, h, dh)
        return jnp.pad(t, ((0, 0), (0, lp - L), (0, 0), (0, 0)))

    def band(t):
        tp = jnp.pad(t, ((0, 0), (A_BLOCK, 0), (0, 0), (0, 0))).reshape(n, nb + 1, A_BLOCK, h, dh)
        return jnp.concatenate([tp[:, :-1], tp[:, 1:]], axis=2)

    qb = to_dilated(q).reshape(n, nb, A_BLOCK, h, dh)
    kb = band(to_dilated(k))
    vb = band(to_dilated(v))
    i = jnp.arange(A_BLOCK)[:, None]
    j = jnp.arange(2 * A_BLOCK)[None, :]
    dist = i + A_BLOCK - j
    blk = jnp.arange(nb)[:, None, None]
    valid = (dist >= 0) & (dist <= w) & ((j >= A_BLOCK) | (blk > 0))
    sc = jnp.einsum('nbihd,nbjhd->nbhij', qb, kb, preferred_element_type=jnp.float32)
    sc = jnp.where(valid[None, :, None], sc, NEG)
    m = jnp.max(sc, axis=-1, keepdims=True)
    p = jnp.exp(sc - m)
    den = jnp.sum(p, axis=-1, keepdims=True)
    o = jnp.einsum('nbhij,nbjhd->nbihd', (p / den).astype(v.dtype), vb)
    lse = (m + jnp.log(den))[..., 0].transpose(0, 1, 3, 2)

    def from_dilated(t):
        rest = t.shape[3:]
        t = t.reshape((n, lp) + rest)[:, :L]
        return t.reshape((b, d, L) + rest).swapaxes(1, 2).reshape((b, s) + rest)

    return from_dilated(o), from_dilated(lse)


def dilated_attention(q, k, v):
    outs, lses = zip(*[_dilated_pattern(q, k, v, w, d) for (w, d) in A_PATTERNS])
    wts = jax.nn.softmax(jnp.stack(lses, axis=0), axis=0)
    o = jnp.sum(jnp.stack(outs, axis=0).astype(jnp.float32) * wts[..., None], axis=0)
    return o.astype(q.dtype)


def multiscale_pool(v, pool_w, pool_scale):
    b, s, _ = v.shape
    vf = v.astype(jnp.float32)
    c0 = jnp.pad(jnp.cumsum(vf, axis=1), ((0, 0), (1, 0), (0, 0)))
    pos = jnp.arange(1, s + 1, dtype=jnp.float32)[None, :, None]
    groups = []
    for g, w in enumerate(POOL_WINDOWS):
        sl = slice(g * B_GROUP, (g + 1) * B_GROUP)
        cg = c0[..., sl]
        lower = jnp.pad(cg, ((0, 0), (w - 1, 0), (0, 0)))[:, :s]
        mean = (cg[:, 1:] - lower) / jnp.minimum(pos, float(w))
        groups.append(mean - vf[..., sl])
    pooled = jnp.stack(groups, axis=2).astype(v.dtype)
    mixed = jnp.einsum('bsgc,gcd->bsgd', pooled, pool_w).reshape(b, s, B_WIDTH)
    return mixed * pool_scale


def spatial_gating(u, v, ln_g, ln_b, w_s, b_s):
    b, s, _ = u.shape
    vf = v.astype(jnp.float32)
    mu = jnp.mean(vf, axis=-1, keepdims=True)
    var = jnp.mean(jnp.square(vf - mu), axis=-1, keepdims=True)
    vn = ((vf - mu) * lax.rsqrt(var + EPS) * ln_g.astype(jnp.float32) + ln_b.astype(jnp.float32)).astype(v.dtype)
    nc = s // C_CHUNK
    vc = vn.reshape(b, nc, C_CHUNK, C_GROUPS, C_GROUP_DIM)
    mask = jnp.tril(jnp.ones((C_CHUNK, C_CHUNK), dtype=bool))
    w = jnp.where(mask[None], w_s, jnp.zeros_like(w_s))
    mixed = jnp.einsum('gij,bnjgc->bnigc', w, vc) + b_s.T[None, None, :, :, None]
    return u * mixed.reshape(b, s, C_WIDTH)


def _ssm_combine(e1, e2):
    a1r, a1i, b1r, b1i = e1
    a2r, a2i, b2r, b2i = e2
    return (a2r * a1r - a2i * a1i,
            a2r * a1i + a2i * a1r,
            a2r * b1r - a2i * b1i + b2r,
            a2r * b1i + a2i * b1r + b2i)


def s5_ssm(u, a_re, a_im, log_dt, b_re, b_im, c_re, c_im, d_skip, w1, w2):
    bsz, s, _ = u.shape
    f32 = jnp.float32
    uf = u.astype(f32).reshape(bsz, s, S5_GROUPS, S5_GROUP_DIM)
    ar, ai = a_re.astype(f32), a_im.astype(f32)
    dt = jnp.exp(log_dt.astype(f32))[:, None]
    mag = jnp.exp(dt * ar)
    abar_re = mag * jnp.cos(dt * ai)
    abar_im = mag * jnp.sin(dt * ai)
    nr, ni = abar_re - 1.0, abar_im
    inv = 1.0 / (ar * ar + ai * ai)
    coef_re = (nr * ar + ni * ai) * inv
    coef_im = (ni * ar - nr * ai) * inv
    br, bi = b_re.astype(f32), b_im.astype(f32)
    bbar_re = coef_re[..., None] * br - coef_im[..., None] * bi
    bbar_im = coef_re[..., None] * bi + coef_im[..., None] * br
    bu_re = jnp.einsum('bsgh,gph->bsgp', uf, bbar_re)
    bu_im = jnp.einsum('bsgh,gph->bsgp', uf, bbar_im)
    shape_a = (1, s, S5_GROUPS, S5_STATE)
    a_seq_re = jnp.broadcast_to(abar_re[None, None], shape_a)
    a_seq_im = jnp.broadcast_to(abar_im[None, None], shape_a)
    _, _, h_re, h_im = lax.associative_scan(_ssm_combine, (a_seq_re, a_seq_im, bu_re, bu_im), axis=1)
    y = (jnp.einsum('bsgp,ghp->bsgh', h_re, c_re.astype(f32))
         - jnp.einsum('bsgp,ghp->bsgh', h_im, c_im.astype(f32))
         + d_skip.astype(f32).reshape(S5_GROUPS, S5_GROUP_DIM) * uf)
    y = jax.nn.gelu(y.reshape(bsz, s, D_WIDTH)).astype(u.dtype)
    return (y @ w1) * jax.nn.sigmoid(y @ w2)


def memory_cross_attention(h, mem_n, w_q, w_kv, w_o):
    b, s, _ = h.shape
    m = mem_n.shape[1]
    q = (h @ w_q).reshape(b, s, X_HEADS, X_HEAD_DIM)
    kv = (mem_n @ w_kv).reshape(b, m, 2, X_HEADS, X_HEAD_DIM)
    k, v = kv[:, :, 0], kv[:, :, 1]
    sc = jnp.einsum('bshd,bmhd->bhsm', q, k, preferred_element_type=jnp.float32) * (X_HEAD_DIM ** -0.5)
    p = jax.nn.softmax(sc, axis=-1).astype(v.dtype)
    o = jnp.einsum('bhsm,bmhd->bshd', p, v).reshape(b, s, D_MODEL)
    return o @ w_o


def _fwd_setup_inputs(seed: int = 0) -> dict:
    key = jax.random.key(seed)
    ks = iter(jax.random.split(key, 40))

    def nrm(shape, scale):
        return scale * jax.random.normal(next(ks), shape, jnp.float32)

    def gain(shape):
        return 1.0 + nrm(shape, 0.05)

    return {
        'x': nrm((BATCH, SEQ, D_MODEL), 1.0),
        'mem': nrm((BATCH, MEM_LEN, D_MODEL), 1.0),
        'norm_ab': gain((N_EVEN, D_MODEL)),
        'w_in_ab': nrm((N_EVEN, D_MODEL, 4 * A_WIDTH + 2 * B_WIDTH), D_MODEL ** -0.5),
        'pool_w': nrm((N_EVEN, len(POOL_WINDOWS), B_GROUP, B_GROUP), B_GROUP ** -0.5),
        'pool_scale': gain((N_EVEN, B_WIDTH)),
        'w_out_ab': nrm((N_EVEN, A_WIDTH + B_WIDTH, D_MODEL), (A_WIDTH + B_WIDTH) ** -0.5),
        'norm_cd': gain((N_ODD, D_MODEL)),
        'w_in_cd': nrm((N_ODD, D_MODEL, 3 * C_WIDTH + 2 * D_WIDTH), D_MODEL ** -0.5),
        'sgu_ln_g': gain((N_ODD, C_WIDTH)),
        'sgu_ln_b': nrm((N_ODD, C_WIDTH), 0.02),
        'sgu_w': nrm((N_ODD, C_GROUPS, C_CHUNK, C_CHUNK), C_CHUNK ** -0.5),
        'sgu_b': gain((N_ODD, C_GROUPS, C_CHUNK)),
        's5_a_re': -0.5 + nrm((N_ODD, S5_GROUPS, S5_STATE), 0.01),
        's5_a_im': jnp.pi * jnp.arange(S5_STATE, dtype=jnp.float32)[None, None, :] + nrm((N_ODD, S5_GROUPS, S5_STATE), 0.01),
        's5_log_dt': jax.random.uniform(next(ks), (N_ODD, S5_GROUPS), jnp.float32, math.log(1e-3), math.log(1e-1)),
        's5_b_re': nrm((N_ODD, S5_GROUPS, S5_STATE, S5_GROUP_DIM), (2 * S5_GROUP_DIM) ** -0.5),
        's5_b_im': nrm((N_ODD, S5_GROUPS, S5_STATE, S5_GROUP_DIM), (2 * S5_GROUP_DIM) ** -0.5),
        's5_c_re': nrm((N_ODD, S5_GROUPS, S5_GROUP_DIM, S5_STATE), S5_STATE ** -0.5),
        's5_c_im': nrm((N_ODD, S5_GROUPS, S5_GROUP_DIM, S5_STATE), S5_STATE ** -0.5),
        's5_d': nrm((N_ODD, D_WIDTH), 1.0),
        'glu_w1': nrm((N_ODD, D_WIDTH, D_WIDTH), D_WIDTH ** -0.5),
        'glu_w2': nrm((N_ODD, D_WIDTH, D_WIDTH), D_WIDTH ** -0.5),
        'w_out_cd': nrm((N_ODD, C_WIDTH + D_WIDTH, D_MODEL), (C_WIDTH + D_WIDTH) ** -0.5),
        'norm_x': gain((DEPTH, D_MODEL)),
        'w_xq': nrm((DEPTH, D_MODEL, D_MODEL), D_MODEL ** -0.5),
        'w_xkv': nrm((DEPTH, D_MODEL, 2 * D_MODEL), D_MODEL ** -0.5),
        'w_xo': nrm((DEPTH, D_MODEL, D_MODEL), D_MODEL ** -0.5),
        'mem_norm': gain((D_MODEL,)),
        'final_norm': gain((D_MODEL,)),
    }


def _fwd_reference(x, mem, norm_ab, w_in_ab, pool_w, pool_scale, w_out_ab,
              norm_cd, w_in_cd, sgu_ln_g, sgu_ln_b, sgu_w, sgu_b,
              s5_a_re, s5_a_im, s5_log_dt, s5_b_re, s5_b_im, s5_c_re, s5_c_im, s5_d,
              glu_w1, glu_w2, w_out_cd, norm_x, w_xq, w_xkv, w_xo, mem_norm, final_norm):
    b, s, _ = x.shape
    mem_n = rms_norm(mem, mem_norm)
    for layer in range(DEPTH):
        i = layer // 2
        if layer % 2 == 0:
            hn = rms_norm(x, norm_ab[i])
            z = hn @ w_in_ab[i]
            q, k, v, g_a, v_b, g_b = jnp.split(
                z, [A_WIDTH, 2 * A_WIDTH, 3 * A_WIDTH, 4 * A_WIDTH, 4 * A_WIDTH + B_WIDTH], axis=-1)
            q = q.reshape(b, s, A_HEADS, A_HEAD_DIM) * (A_HEAD_DIM ** -0.5)
            k = k.reshape(b, s, A_HEADS, A_HEAD_DIM)
            v = v.reshape(b, s, A_HEADS, A_HEAD_DIM)
            a_out = dilated_attention(q, k, v).reshape(b, s, A_WIDTH) * jax.nn.silu(g_a)
            b_out = multiscale_pool(v_b, pool_w[i], pool_scale[i]) * jax.nn.silu(g_b)
            y = jnp.concatenate([a_out, b_out], axis=-1) @ w_out_ab[i]
        else:
            hn = rms_norm(x, norm_cd[i])
            z = hn @ w_in_cd[i]
            u_c, v_c, g_c, x_d, g_d = jnp.split(
                z, [C_WIDTH, 2 * C_WIDTH, 3 * C_WIDTH, 3 * C_WIDTH + D_WIDTH], axis=-1)
            c_out = spatial_gating(u_c, v_c, sgu_ln_g[i], sgu_ln_b[i], sgu_w[i], sgu_b[i]) * jax.nn.silu(g_c)
            d_out = s5_ssm(x_d, s5_a_re[i], s5_a_im[i], s5_log_dt[i], s5_b_re[i], s5_b_im[i],
                           s5_c_re[i], s5_c_im[i], s5_d[i], glu_w1[i], glu_w2[i]) * jax.nn.silu(g_d)
            y = jnp.concatenate([c_out, d_out], axis=-1) @ w_out_cd[i]
        x = x + y
        x = x + memory_cross_attention(rms_norm(x, norm_x[layer]), mem_n, w_xq[layer], w_xkv[layer], w_xo[layer])
    return rms_norm(x, final_norm)


import jax as _jax
import jax.numpy as _jnp

TWIN_FORMAT = 'train_step'
FWD_PARAMS = ['x', 'mem', 'norm_ab', 'w_in_ab', 'pool_w', 'pool_scale', 'w_out_ab', 'norm_cd', 'w_in_cd', 'sgu_ln_g', 'sgu_ln_b', 'sgu_w', 'sgu_b', 's5_a_re', 's5_a_im', 's5_log_dt', 's5_b_re', 's5_b_im', 's5_c_re', 's5_c_im', 's5_d', 'glu_w1', 'glu_w2', 'w_out_cd', 'norm_x', 'w_xq', 'w_xkv', 'w_xo', 'mem_norm', 'final_norm']
TWIN_WEIGHTS = ['norm_ab', 'w_in_ab', 'pool_w', 'pool_scale', 'w_out_ab', 'norm_cd', 'w_in_cd', 'sgu_ln_g', 'sgu_ln_b', 'sgu_w', 'sgu_b', 's5_a_re', 's5_a_im', 's5_log_dt', 's5_b_re', 's5_b_im', 's5_c_re', 's5_c_im', 's5_d', 'glu_w1', 'glu_w2', 'w_out_cd', 'norm_x', 'w_xq', 'w_xkv', 'w_xo', 'mem_norm', 'final_norm']
TWIN_DIFF_INPUT = 'x'
TWIN_INPUTS = ['x', 'mem', 'norm_ab', 'w_in_ab', 'pool_w', 'pool_scale', 'w_out_ab', 'norm_cd', 'w_in_cd', 'sgu_ln_g', 'sgu_ln_b', 'sgu_w', 'sgu_b', 's5_a_re', 's5_a_im', 's5_log_dt', 's5_b_re', 's5_b_im', 's5_c_re', 's5_c_im', 's5_d', 'glu_w1', 'glu_w2', 'w_out_cd', 'norm_x', 'w_xq', 'w_xkv', 'w_xo', 'mem_norm', 'final_norm', 'loss_target', 'm_norm_ab', 'm_w_in_ab', 'm_pool_w', 'm_pool_scale', 'm_w_out_ab', 'm_norm_cd', 'm_w_in_cd', 'm_sgu_ln_g', 'm_sgu_ln_b', 'm_sgu_w', 'm_sgu_b', 'm_s5_a_re', 'm_s5_a_im', 'm_s5_log_dt', 'm_s5_b_re', 'm_s5_b_im', 'm_s5_c_re', 'm_s5_c_im', 'm_s5_d', 'm_glu_w1', 'm_glu_w2', 'm_w_out_cd', 'm_norm_x', 'm_w_xq', 'm_w_xkv', 'm_w_xo', 'm_mem_norm', 'm_final_norm', 'v_norm_ab', 'v_w_in_ab', 'v_pool_w', 'v_pool_scale', 'v_w_out_ab', 'v_norm_cd', 'v_w_in_cd', 'v_sgu_ln_g', 'v_sgu_ln_b', 'v_sgu_w', 'v_sgu_b', 'v_s5_a_re', 'v_s5_a_im', 'v_s5_log_dt', 'v_s5_b_re', 'v_s5_b_im', 'v_s5_c_re', 'v_s5_c_im', 'v_s5_d', 'v_glu_w1', 'v_glu_w2', 'v_w_out_cd', 'v_norm_x', 'v_w_xq', 'v_w_xkv', 'v_w_xo', 'v_mem_norm', 'v_final_norm']
TWIN_OUTPUTS = ['loss', 'grad_x', 'grad_norm_ab', 'grad_w_in_ab', 'grad_pool_w', 'grad_pool_scale', 'grad_w_out_ab', 'grad_norm_cd', 'grad_w_in_cd', 'grad_sgu_ln_g', 'grad_sgu_ln_b', 'grad_sgu_w', 'grad_sgu_b', 'grad_s5_a_re', 'grad_s5_a_im', 'grad_s5_log_dt', 'grad_s5_b_re', 'grad_s5_b_im', 'grad_s5_c_re', 'grad_s5_c_im', 'grad_s5_d', 'grad_glu_w1', 'grad_glu_w2', 'grad_w_out_cd', 'grad_norm_x', 'grad_w_xq', 'grad_w_xkv', 'grad_w_xo', 'grad_mem_norm', 'grad_final_norm', 'delta_norm_ab', 'delta_w_in_ab', 'delta_pool_w', 'delta_pool_scale', 'delta_w_out_ab', 'delta_norm_cd', 'delta_w_in_cd', 'delta_sgu_ln_g', 'delta_sgu_ln_b', 'delta_sgu_w', 'delta_sgu_b', 'delta_s5_a_re', 'delta_s5_a_im', 'delta_s5_log_dt', 'delta_s5_b_re', 'delta_s5_b_im', 'delta_s5_c_re', 'delta_s5_c_im', 'delta_s5_d', 'delta_glu_w1', 'delta_glu_w2', 'delta_w_out_cd', 'delta_norm_x', 'delta_w_xq', 'delta_w_xkv', 'delta_w_xo', 'delta_mem_norm', 'delta_final_norm', 'new_m_norm_ab', 'new_m_w_in_ab', 'new_m_pool_w', 'new_m_pool_scale', 'new_m_w_out_ab', 'new_m_norm_cd', 'new_m_w_in_cd', 'new_m_sgu_ln_g', 'new_m_sgu_ln_b', 'new_m_sgu_w', 'new_m_sgu_b', 'new_m_s5_a_re', 'new_m_s5_a_im', 'new_m_s5_log_dt', 'new_m_s5_b_re', 'new_m_s5_b_im', 'new_m_s5_c_re', 'new_m_s5_c_im', 'new_m_s5_d', 'new_m_glu_w1', 'new_m_glu_w2', 'new_m_w_out_cd', 'new_m_norm_x', 'new_m_w_xq', 'new_m_w_xkv', 'new_m_w_xo', 'new_m_mem_norm', 'new_m_final_norm', 'new_v_norm_ab', 'new_v_w_in_ab', 'new_v_pool_w', 'new_v_pool_scale', 'new_v_w_out_ab', 'new_v_norm_cd', 'new_v_w_in_cd', 'new_v_sgu_ln_g', 'new_v_sgu_ln_b', 'new_v_sgu_w', 'new_v_sgu_b', 'new_v_s5_a_re', 'new_v_s5_a_im', 'new_v_s5_log_dt', 'new_v_s5_b_re', 'new_v_s5_b_im', 'new_v_s5_c_re', 'new_v_s5_c_im', 'new_v_s5_d', 'new_v_glu_w1', 'new_v_glu_w2', 'new_v_w_out_cd', 'new_v_norm_x', 'new_v_w_xq', 'new_v_w_xkv', 'new_v_w_xo', 'new_v_mem_norm', 'new_v_final_norm']
TWIN_LEAF_KINDS = {'loss': 'loss', 'grad_x': 'grad_x', 'grad_norm_ab': 'grad_w', 'grad_w_in_ab': 'grad_w', 'grad_pool_w': 'grad_w', 'grad_pool_scale': 'grad_w', 'grad_w_out_ab': 'grad_w', 'grad_norm_cd': 'grad_w', 'grad_w_in_cd': 'grad_w', 'grad_sgu_ln_g': 'grad_w', 'grad_sgu_ln_b': 'grad_w', 'grad_sgu_w': 'grad_w', 'grad_sgu_b': 'grad_w', 'grad_s5_a_re': 'grad_w', 'grad_s5_a_im': 'grad_w', 'grad_s5_log_dt': 'grad_w', 'grad_s5_b_re': 'grad_w', 'grad_s5_b_im': 'grad_w', 'grad_s5_c_re': 'grad_w', 'grad_s5_c_im': 'grad_w', 'grad_s5_d': 'grad_w', 'grad_glu_w1': 'grad_w', 'grad_glu_w2': 'grad_w', 'grad_w_out_cd': 'grad_w', 'grad_norm_x': 'grad_w', 'grad_w_xq': 'grad_w', 'grad_w_xkv': 'grad_w', 'grad_w_xo': 'grad_w', 'grad_mem_norm': 'grad_w', 'grad_final_norm': 'grad_w', 'delta_norm_ab': 'delta_w', 'delta_w_in_ab': 'delta_w', 'delta_pool_w': 'delta_w', 'delta_pool_scale': 'delta_w', 'delta_w_out_ab': 'delta_w', 'delta_norm_cd': 'delta_w', 'delta_w_in_cd': 'delta_w', 'delta_sgu_ln_g': 'delta_w', 'delta_sgu_ln_b': 'delta_w', 'delta_sgu_w': 'delta_w', 'delta_sgu_b': 'delta_w', 'delta_s5_a_re': 'delta_w', 'delta_s5_a_im': 'delta_w', 'delta_s5_log_dt': 'delta_w', 'delta_s5_b_re': 'delta_w', 'delta_s5_b_im': 'delta_w', 'delta_s5_c_re': 'delta_w', 'delta_s5_c_im': 'delta_w', 'delta_s5_d': 'delta_w', 'delta_glu_w1': 'delta_w', 'delta_glu_w2': 'delta_w', 'delta_w_out_cd': 'delta_w', 'delta_norm_x': 'delta_w', 'delta_w_xq': 'delta_w', 'delta_w_xkv': 'delta_w', 'delta_w_xo': 'delta_w', 'delta_mem_norm': 'delta_w', 'delta_final_norm': 'delta_w', 'new_m_norm_ab': 'new_m', 'new_m_w_in_ab': 'new_m', 'new_m_pool_w': 'new_m', 'new_m_pool_scale': 'new_m', 'new_m_w_out_ab': 'new_m', 'new_m_norm_cd': 'new_m', 'new_m_w_in_cd': 'new_m', 'new_m_sgu_ln_g': 'new_m', 'new_m_sgu_ln_b': 'new_m', 'new_m_sgu_w': 'new_m', 'new_m_sgu_b': 'new_m', 'new_m_s5_a_re': 'new_m', 'new_m_s5_a_im': 'new_m', 'new_m_s5_log_dt': 'new_m', 'new_m_s5_b_re': 'new_m', 'new_m_s5_b_im': 'new_m', 'new_m_s5_c_re': 'new_m', 'new_m_s5_c_im': 'new_m', 'new_m_s5_d': 'new_m', 'new_m_glu_w1': 'new_m', 'new_m_glu_w2': 'new_m', 'new_m_w_out_cd': 'new_m', 'new_m_norm_x': 'new_m', 'new_m_w_xq': 'new_m', 'new_m_w_xkv': 'new_m', 'new_m_w_xo': 'new_m', 'new_m_mem_norm': 'new_m', 'new_m_final_norm': 'new_m', 'new_v_norm_ab': 'new_v', 'new_v_w_in_ab': 'new_v', 'new_v_pool_w': 'new_v', 'new_v_pool_scale': 'new_v', 'new_v_w_out_ab': 'new_v', 'new_v_norm_cd': 'new_v', 'new_v_w_in_cd': 'new_v', 'new_v_sgu_ln_g': 'new_v', 'new_v_sgu_ln_b': 'new_v', 'new_v_sgu_w': 'new_v', 'new_v_sgu_b': 'new_v', 'new_v_s5_a_re': 'new_v', 'new_v_s5_a_im': 'new_v', 'new_v_s5_log_dt': 'new_v', 'new_v_s5_b_re': 'new_v', 'new_v_s5_b_im': 'new_v', 'new_v_s5_c_re': 'new_v', 'new_v_s5_c_im': 'new_v', 'new_v_s5_d': 'new_v', 'new_v_glu_w1': 'new_v', 'new_v_glu_w2': 'new_v', 'new_v_w_out_cd': 'new_v', 'new_v_norm_x': 'new_v', 'new_v_w_xq': 'new_v', 'new_v_w_xkv': 'new_v', 'new_v_w_xo': 'new_v', 'new_v_mem_norm': 'new_v', 'new_v_final_norm': 'new_v'}


def _forward(args):
    return _fwd_reference(*[args[k] for k in FWD_PARAMS])


def _output_shape():
    out = _jax.eval_shape(lambda: _forward(_fwd_setup_inputs(0)))
    return out.shape, out.dtype

N_MICROBATCH = 1
ADAM_LR = 0.001
ADAM_B1 = 0.9
ADAM_B2 = 0.999
ADAM_EPS = 1e-08
ADAM_WD = 0.01
ADAM_STEP = 10
PER_EXAMPLE_BATCH_AXIS = {'x': 0, 'mem': 0, 'loss_target': 0}
SHARED_INPUTS = []
_WEIGHT_DTYPES = {'norm_ab': _jnp.float32, 'w_in_ab': _jnp.float32, 'pool_w': _jnp.float32, 'pool_scale': _jnp.float32, 'w_out_ab': _jnp.float32, 'norm_cd': _jnp.float32, 'w_in_cd': _jnp.float32, 'sgu_ln_g': _jnp.float32, 'sgu_ln_b': _jnp.float32, 'sgu_w': _jnp.float32, 'sgu_b': _jnp.float32, 's5_a_re': _jnp.float32, 's5_a_im': _jnp.float32, 's5_log_dt': _jnp.float32, 's5_b_re': _jnp.float32, 's5_b_im': _jnp.float32, 's5_c_re': _jnp.float32, 's5_c_im': _jnp.float32, 's5_d': _jnp.float32, 'glu_w1': _jnp.float32, 'glu_w2': _jnp.float32, 'w_out_cd': _jnp.float32, 'norm_x': _jnp.float32, 'w_xq': _jnp.float32, 'w_xkv': _jnp.float32, 'w_xo': _jnp.float32, 'mem_norm': _jnp.float32, 'final_norm': _jnp.float32}
MOMENT_SCALE = {'norm_ab': 7.560531e-02, 'w_in_ab': 3.121876e-02, 'pool_w': 5.064923e-02, 'pool_scale': 5.282594e-02, 'w_out_ab': 5.198813e-02, 'norm_cd': 1.008046e-01, 'w_in_cd': 4.903112e-02, 'sgu_ln_g': 3.632078e-02, 'sgu_ln_b': 3.551333e-02, 'sgu_w': 5.008766e-02, 'sgu_b': 7.155594e-02, 's5_a_re': 1.316140e-03, 's5_a_im': 1.496867e-03, 's5_log_dt': 1.187727e+00, 's5_b_re': 8.615796e-04, 's5_b_im': 8.717569e-04, 's5_c_re': 1.239404e-03, 's5_c_im': 1.250214e-03, 's5_d': 2.000722e-02, 'glu_w1': 1.730103e-02, 'glu_w2': 4.906291e-03, 'w_out_cd': 6.364738e-02, 'norm_x': 1.292558e-02, 'w_xq': 1.269271e-02, 'w_xkv': 1.281549e-02, 'w_xo': 1.303025e-02, 'mem_norm': 3.881308e-02, 'final_norm': 1.601054e+01}


def _to_microbatches(a, axis):
    t = _jnp.moveaxis(a, axis, 0)
    t = t.reshape((N_MICROBATCH, t.shape[0] // N_MICROBATCH) + t.shape[1:])
    return _jnp.moveaxis(t, 1, axis + 1)


def setup_inputs(seed: int = 0) -> dict:
    inp = _fwd_setup_inputs(seed)
    key = _jax.random.fold_in(_jax.random.key(seed), 7919)
    shape, _ = _output_shape()
    out = dict(inp)
    out["loss_target"] = _jax.random.normal(_jax.random.fold_in(key, 0), shape, _jnp.float32)
    for i, name in enumerate(TWIN_WEIGHTS):
        w = inp[name].astype(_jnp.float32)
        if MOMENT_SCALE is None:
            s = _jnp.sqrt(_jnp.mean(_jnp.square(w)) + 1e-30)
        else:
            s = MOMENT_SCALE[name]
        km, kv = _jax.random.split(_jax.random.fold_in(key, i + 1))
        out[name] = w
        out["m_" + name] = s * _jax.random.normal(km, w.shape, _jnp.float32)
        out["v_" + name] = (s * s) * _jax.random.uniform(kv, w.shape, _jnp.float32, 0.5, 1.5)
    if N_MICROBATCH > 1:
        for name, axis in PER_EXAMPLE_BATCH_AXIS.items():
            out[name] = _to_microbatches(out[name], axis)
    return {'x': out['x'], 'mem': out['mem'], 'norm_ab': out['norm_ab'], 'w_in_ab': out['w_in_ab'], 'pool_w': out['pool_w'], 'pool_scale': out['pool_scale'], 'w_out_ab': out['w_out_ab'], 'norm_cd': out['norm_cd'], 'w_in_cd': out['w_in_cd'], 'sgu_ln_g': out['sgu_ln_g'], 'sgu_ln_b': out['sgu_ln_b'], 'sgu_w': out['sgu_w'], 'sgu_b': out['sgu_b'], 's5_a_re': out['s5_a_re'], 's5_a_im': out['s5_a_im'], 's5_log_dt': out['s5_log_dt'], 's5_b_re': out['s5_b_re'], 's5_b_im': out['s5_b_im'], 's5_c_re': out['s5_c_re'], 's5_c_im': out['s5_c_im'], 's5_d': out['s5_d'], 'glu_w1': out['glu_w1'], 'glu_w2': out['glu_w2'], 'w_out_cd': out['w_out_cd'], 'norm_x': out['norm_x'], 'w_xq': out['w_xq'], 'w_xkv': out['w_xkv'], 'w_xo': out['w_xo'], 'mem_norm': out['mem_norm'], 'final_norm': out['final_norm'], 'loss_target': out['loss_target'], 'm_norm_ab': out['m_norm_ab'], 'm_w_in_ab': out['m_w_in_ab'], 'm_pool_w': out['m_pool_w'], 'm_pool_scale': out['m_pool_scale'], 'm_w_out_ab': out['m_w_out_ab'], 'm_norm_cd': out['m_norm_cd'], 'm_w_in_cd': out['m_w_in_cd'], 'm_sgu_ln_g': out['m_sgu_ln_g'], 'm_sgu_ln_b': out['m_sgu_ln_b'], 'm_sgu_w': out['m_sgu_w'], 'm_sgu_b': out['m_sgu_b'], 'm_s5_a_re': out['m_s5_a_re'], 'm_s5_a_im': out['m_s5_a_im'], 'm_s5_log_dt': out['m_s5_log_dt'], 'm_s5_b_re': out['m_s5_b_re'], 'm_s5_b_im': out['m_s5_b_im'], 'm_s5_c_re': out['m_s5_c_re'], 'm_s5_c_im': out['m_s5_c_im'], 'm_s5_d': out['m_s5_d'], 'm_glu_w1': out['m_glu_w1'], 'm_glu_w2': out['m_glu_w2'], 'm_w_out_cd': out['m_w_out_cd'], 'm_norm_x': out['m_norm_x'], 'm_w_xq': out['m_w_xq'], 'm_w_xkv': out['m_w_xkv'], 'm_w_xo': out['m_w_xo'], 'm_mem_norm': out['m_mem_norm'], 'm_final_norm': out['m_final_norm'], 'v_norm_ab': out['v_norm_ab'], 'v_w_in_ab': out['v_w_in_ab'], 'v_pool_w': out['v_pool_w'], 'v_pool_scale': out['v_pool_scale'], 'v_w_out_ab': out['v_w_out_ab'], 'v_norm_cd': out['v_norm_cd'], 'v_w_in_cd': out['v_w_in_cd'], 'v_sgu_ln_g': out['v_sgu_ln_g'], 'v_sgu_ln_b': out['v_sgu_ln_b'], 'v_sgu_w': out['v_sgu_w'], 'v_sgu_b': out['v_sgu_b'], 'v_s5_a_re': out['v_s5_a_re'], 'v_s5_a_im': out['v_s5_a_im'], 'v_s5_log_dt': out['v_s5_log_dt'], 'v_s5_b_re': out['v_s5_b_re'], 'v_s5_b_im': out['v_s5_b_im'], 'v_s5_c_re': out['v_s5_c_re'], 'v_s5_c_im': out['v_s5_c_im'], 'v_s5_d': out['v_s5_d'], 'v_glu_w1': out['v_glu_w1'], 'v_glu_w2': out['v_glu_w2'], 'v_w_out_cd': out['v_w_out_cd'], 'v_norm_x': out['v_norm_x'], 'v_w_xq': out['v_w_xq'], 'v_w_xkv': out['v_w_xkv'], 'v_w_xo': out['v_w_xo'], 'v_mem_norm': out['v_mem_norm'], 'v_final_norm': out['v_final_norm']}


def _loss(weights, diff, rest, loss_target):
    with _jax.named_scope("forward"):
        args = {**rest, TWIN_DIFF_INPUT: diff, **{k: w.astype(_WEIGHT_DTYPES[k]) for k, w in weights.items()}}
        y = _forward(args)
    with _jax.named_scope("loss_head"):
        err = _jnp.square(y.astype(_jnp.float32) - loss_target)
        return 0.5 * _jnp.sum(_jnp.mean(err, axis=-1)) if err.ndim else 0.5 * err


def _adamw(w, g, m, v):
    m = ADAM_B1 * m + (1.0 - ADAM_B1) * g
    v = ADAM_B2 * v + (1.0 - ADAM_B2) * _jnp.square(g)
    m_hat = m / (1.0 - ADAM_B1 ** ADAM_STEP)
    v_hat = v / (1.0 - ADAM_B2 ** ADAM_STEP)
    delta = -ADAM_LR * (m_hat / (_jnp.sqrt(v_hat) + ADAM_EPS) + ADAM_WD * w)
    return delta, m, v


def reference(x, mem, norm_ab, w_in_ab, pool_w, pool_scale, w_out_ab, norm_cd, w_in_cd, sgu_ln_g, sgu_ln_b, sgu_w, sgu_b, s5_a_re, s5_a_im, s5_log_dt, s5_b_re, s5_b_im, s5_c_re, s5_c_im, s5_d, glu_w1, glu_w2, w_out_cd, norm_x, w_xq, w_xkv, w_xo, mem_norm, final_norm, loss_target, m_norm_ab, m_w_in_ab, m_pool_w, m_pool_scale, m_w_out_ab, m_norm_cd, m_w_in_cd, m_sgu_ln_g, m_sgu_ln_b, m_sgu_w, m_sgu_b, m_s5_a_re, m_s5_a_im, m_s5_log_dt, m_s5_b_re, m_s5_b_im, m_s5_c_re, m_s5_c_im, m_s5_d, m_glu_w1, m_glu_w2, m_w_out_cd, m_norm_x, m_w_xq, m_w_xkv, m_w_xo, m_mem_norm, m_final_norm, v_norm_ab, v_w_in_ab, v_pool_w, v_pool_scale, v_w_out_ab, v_norm_cd, v_w_in_cd, v_sgu_ln_g, v_sgu_ln_b, v_sgu_w, v_sgu_b, v_s5_a_re, v_s5_a_im, v_s5_log_dt, v_s5_b_re, v_s5_b_im, v_s5_c_re, v_s5_c_im, v_s5_d, v_glu_w1, v_glu_w2, v_w_out_cd, v_norm_x, v_w_xq, v_w_xkv, v_w_xo, v_mem_norm, v_final_norm):
    given = dict(x=x, mem=mem, norm_ab=norm_ab, w_in_ab=w_in_ab, pool_w=pool_w, pool_scale=pool_scale, w_out_ab=w_out_ab, norm_cd=norm_cd, w_in_cd=w_in_cd, sgu_ln_g=sgu_ln_g, sgu_ln_b=sgu_ln_b, sgu_w=sgu_w, sgu_b=sgu_b, s5_a_re=s5_a_re, s5_a_im=s5_a_im, s5_log_dt=s5_log_dt, s5_b_re=s5_b_re, s5_b_im=s5_b_im, s5_c_re=s5_c_re, s5_c_im=s5_c_im, s5_d=s5_d, glu_w1=glu_w1, glu_w2=glu_w2, w_out_cd=w_out_cd, norm_x=norm_x, w_xq=w_xq, w_xkv=w_xkv, w_xo=w_xo, mem_norm=mem_norm, final_norm=final_norm, loss_target=loss_target, m_norm_ab=m_norm_ab, m_w_in_ab=m_w_in_ab, m_pool_w=m_pool_w, m_pool_scale=m_pool_scale, m_w_out_ab=m_w_out_ab, m_norm_cd=m_norm_cd, m_w_in_cd=m_w_in_cd, m_sgu_ln_g=m_sgu_ln_g, m_sgu_ln_b=m_sgu_ln_b, m_sgu_w=m_sgu_w, m_sgu_b=m_sgu_b, m_s5_a_re=m_s5_a_re, m_s5_a_im=m_s5_a_im, m_s5_log_dt=m_s5_log_dt, m_s5_b_re=m_s5_b_re, m_s5_b_im=m_s5_b_im, m_s5_c_re=m_s5_c_re, m_s5_c_im=m_s5_c_im, m_s5_d=m_s5_d, m_glu_w1=m_glu_w1, m_glu_w2=m_glu_w2, m_w_out_cd=m_w_out_cd, m_norm_x=m_norm_x, m_w_xq=m_w_xq, m_w_xkv=m_w_xkv, m_w_xo=m_w_xo, m_mem_norm=m_mem_norm, m_final_norm=m_final_norm, v_norm_ab=v_norm_ab, v_w_in_ab=v_w_in_ab, v_pool_w=v_pool_w, v_pool_scale=v_pool_scale, v_w_out_ab=v_w_out_ab, v_norm_cd=v_norm_cd, v_w_in_cd=v_w_in_cd, v_sgu_ln_g=v_sgu_ln_g, v_sgu_ln_b=v_sgu_ln_b, v_sgu_w=v_sgu_w, v_sgu_b=v_sgu_b, v_s5_a_re=v_s5_a_re, v_s5_a_im=v_s5_a_im, v_s5_log_dt=v_s5_log_dt, v_s5_b_re=v_s5_b_re, v_s5_b_im=v_s5_b_im, v_s5_c_re=v_s5_c_re, v_s5_c_im=v_s5_c_im, v_s5_d=v_s5_d, v_glu_w1=v_glu_w1, v_glu_w2=v_glu_w2, v_w_out_cd=v_w_out_cd, v_norm_x=v_norm_x, v_w_xq=v_w_xq, v_w_xkv=v_w_xkv, v_w_xo=v_w_xo, v_mem_norm=v_mem_norm, v_final_norm=v_final_norm)
    weights = {n: given[n] for n in TWIN_WEIGHTS}
    shared = {n: given[n] for n in SHARED_INPUTS}
    per_example = {n: given[n] for n in ['x', 'mem']}
    grad_fn = _jax.value_and_grad(_loss, argnums=(0, 1))

    def one_microbatch(ex, loss_target):
        ex = dict(ex)
        diff = ex.pop(TWIN_DIFF_INPUT)
        return grad_fn(weights, diff, {**shared, **ex}, loss_target)

    if N_MICROBATCH == 1:
        loss, (grad_w, grad_x) = one_microbatch(per_example, given["loss_target"])
    else:
        def body(carry, xs):
            loss_sum, grad_sum = carry
            l_k, (gw_k, gx_k) = one_microbatch(xs[0], xs[1])
            with _jax.named_scope("update"):
                return (loss_sum + l_k, _jax.tree.map(_jnp.add, grad_sum, gw_k)), gx_k

        init = (_jnp.zeros((), _jnp.float32), _jax.tree.map(_jnp.zeros_like, weights))
        (loss, grad_w), grad_x = _jax.lax.scan(body, init, (per_example, given["loss_target"]))
    with _jax.named_scope("update"):
        delta_w, new_m, new_v = {}, {}, {}
        for n in TWIN_WEIGHTS:
            delta_w[n], new_m[n], new_v[n] = _adamw(weights[n], grad_w[n], given["m_" + n], given["v_" + n])
    return (loss, grad_x, *[grad_w[n] for n in TWIN_WEIGHTS], *[delta_w[n] for n in TWIN_WEIGHTS],
            *[new_m[n] for n in TWIN_WEIGHTS], *[new_v[n] for n in TWIN_WEIGHTS])
```

```python
import math

import jax
import jax.numpy as jnp
from jax import lax
from jax.experimental import pallas as pl
from jax.experimental.pallas import tpu as pltpu

F32, BF16 = jnp.float32, jnp.bfloat16
S, D = 2048, 1024
MEM = 256
EPS = 1e-6
NEG = -1e30
QB = 128
PATTERNS = (1, 4, 16)
NG, NP, NH = 32, 64, 16
NS = NG * NP
LR, B1, B2, AEPS, WD, STEP = 0.001, 0.9, 0.999, 1e-08, 0.01, 10
MESHID = pl.DeviceIdType.MESH
VMEM_LIMIT = 56 * 1024 * 1024


def _cparams(sem):
    return pltpu.CompilerParams(dimension_semantics=sem, vmem_limit_bytes=VMEM_LIMIT)


def _sig(x):
    return 1.0 / (1.0 + jnp.exp(-x))


def _dot(a, b, dims):
    return lax.dot_general(a, b, (dims, ((), ())), preferred_element_type=F32)


def _nn(a, b):
    return _dot(a, b, ((1,), (0,)))


def _nt(a, b):
    return _dot(a, b, ((1,), (1,)))


def _tn(a, b):
    return _dot(a, b, ((0,), (0,)))


_DIMS = {"nn": ((1,), (0,)), "nt": ((1,), (1,)), "tn": ((0,), (0,))}


def _tile(dim, cc=None, cap=1024):
    for t in (1024, 512, 256, 128):
        if t <= cap and dim % t == 0 and (cc is None or cc % t == 0):
            return t
    return dim


def m2(arr, col_off=0, ncols=None):
    rows, cols = arr.shape
    ncols = cols - col_off if ncols is None else ncols

    def spec(tr, tc, rc):
        assert col_off % tc == 0
        return pl.BlockSpec((tr, tc), lambda *g: (rc(*g)[0], rc(*g)[1] + col_off // tc))
    return (arr, rows, ncols, spec, None if col_off == 0 else col_off)


def m3(arr, i):
    def spec(tr, tc, rc):
        return pl.BlockSpec((None, tr, tc), lambda *g: (i,) + tuple(rc(*g)))
    return (arr, arr.shape[1], arr.shape[2], spec, None)


def mcs(arr, i):
    cs = arr.shape[3]

    def spec(tr, tc, rc):
        n = cs // tc
        return pl.BlockSpec((None, None, tr, tc),
                            lambda *g: (i, rc(*g)[1] // n, rc(*g)[0], rc(*g)[1] % n))
    return (arr, arr.shape[2], 4 * cs, spec, cs)


def out2(rows, cols):
    def spec(tr, tc, rc):
        return pl.BlockSpec((tr, tc), lambda *g: tuple(rc(*g)))
    return ((rows, cols), spec, None)


def outcs(rows, cs):
    def spec(tr, tc, rc):
        n = cs // tc
        return pl.BlockSpec((None, tr, tc), lambda *g: (rc(*g)[1] // n, rc(*g)[0], rc(*g)[1] % n))
    return ((4, rows, cs), spec, cs)


def _both(a, b):
    if a is None:
        return b
    if b is None:
        return a
    return math.gcd(a, b)


def mm(a, b, mode, name, add=None, out=None, out_dtype=F32):
    a_arr, a_r, a_c, a_spec, a_cc = a
    b_arr, b_r, b_c, b_spec, b_cc = b
    if mode == "nn":
        m, k, n = a_r, a_c, b_c
        assert b_r == k
        ccm, cck, ccn = None, a_cc, b_cc
    elif mode == "nt":
        m, k, n = a_r, a_c, b_r
        assert b_c == k
        ccm, cck, ccn = None, _both(a_cc, b_cc), None
    else:
        m, k, n = a_c, a_r, b_c
        assert b_r == k
        ccm, cck, ccn = a_cc, None, b_cc
    out = out2(m, n) if out is None else out
    o_shape, o_spec, o_cc = out
    ccn = _both(ccn, o_cc)
    if add is not None:
        ccn = _both(ccn, add[4])
    tm, tn, tk = _tile(m, ccm), _tile(n, ccn), _tile(k, cck, cap=512)
    nk = k // tk
    if mode == "nn":
        in_specs = [a_spec(tm, tk, lambda i, j, kk: (i, kk)), b_spec(tk, tn, lambda i, j, kk: (kk, j))]
    elif mode == "nt":
        in_specs = [a_spec(tm, tk, lambda i, j, kk: (i, kk)), b_spec(tn, tk, lambda i, j, kk: (j, kk))]
    else:
        in_specs = [a_spec(tk, tm, lambda i, j, kk: (kk, i)), b_spec(tk, tn, lambda i, j, kk: (kk, j))]
    args = [a_arr, b_arr]
    if add is not None:
        in_specs.append(add[3](tm, tn, lambda i, j, kk: (i, j)))
        args.append(add[0])
    dims = _DIMS[mode]
    has_add = add is not None

    def body(*refs):
        a_ref, b_ref = refs[0], refs[1]
        add_ref = refs[2] if has_add else None
        o_ref, acc = refs[-2], refs[-1]
        kk = pl.program_id(2)

        @pl.when(kk == 0)
        def _():
            acc[...] = jnp.zeros_like(acc)

        acc[...] += _dot(a_ref[...].astype(BF16), b_ref[...].astype(BF16), dims)

        @pl.when(kk == nk - 1)
        def _():
            r = acc[...]
            if has_add:
                r = r + add_ref[...].astype(F32)
            o_ref[...] = r.astype(o_ref.dtype)

    return pl.pallas_call(
        body, name=name, grid=(m // tm, n // tn, nk), in_specs=in_specs,
        out_specs=o_spec(tm, tn, lambda i, j, kk: (i, j)),
        out_shape=jax.ShapeDtypeStruct(o_shape, out_dtype),
        scratch_shapes=[pltpu.VMEM((tm, tn), F32)],
        compiler_params=_cparams(("parallel", "parallel", "arbitrary")),
    )(*args)


def rw(fn, ins, outs, name, rows, tr=256, consts=(), accs=()):
    n_in, n_c, n_o, n_a = len(ins), len(consts), len(outs), len(accs)
    in_specs = []
    for arr, off, width in ins:
        assert off % width == 0
        in_specs.append(pl.BlockSpec((tr, width), lambda i, o=off // width: (i, o)))
    for c in consts:
        in_specs.append(pl.BlockSpec(c.shape, lambda i: (0, 0)))
    out_specs = [pl.BlockSpec((tr, w), lambda i: (i, 0)) for w, _ in outs]
    out_specs += [pl.BlockSpec(s, lambda i: (0, 0)) for s in accs]
    out_shape = [jax.ShapeDtypeStruct((rows, w), dt) for w, dt in outs]
    out_shape += [jax.ShapeDtypeStruct(s, F32) for s in accs]

    def body(*refs):
        vals = [r[...] for r in refs[:n_in + n_c]]
        o_refs = refs[n_in + n_c:n_in + n_c + n_o]
        a_refs = refs[n_in + n_c + n_o:]
        res = fn(*vals)
        for r, v in zip(o_refs, res[:n_o]):
            r[...] = v.astype(r.dtype)
        if n_a:
            @pl.when(pl.program_id(0) == 0)
            def _():
                for r in a_refs:
                    r[...] = jnp.zeros_like(r)
            for r, v in zip(a_refs, res[n_o:]):
                r[...] += v

    res = pl.pallas_call(
        body, name=name, grid=(rows // tr,), in_specs=in_specs, out_specs=out_specs,
        out_shape=out_shape,
        compiler_params=_cparams(("arbitrary",) if n_a else ("parallel",)),
    )(*[a for a, _, _ in ins], *consts)
    return res


def _rstd(x):
    return lax.rsqrt(jnp.mean(x * x, axis=-1, keepdims=True) + EPS)


def rms_fwd(x, g, name):
    def fn(xv, gv):
        xv = xv.astype(F32)
        return (xv * _rstd(xv) * gv,)
    return rw(fn, [(x, 0, D)], [(D, BF16)], name, x.shape[0], consts=[g])[0]


def _rms_bwd_math(xv, dy, gv):
    r = _rstd(xv)
    dyg = dy * gv
    dx = r * dyg - xv * (r * r * r / D) * jnp.sum(dyg * xv, axis=-1, keepdims=True)
    dg = jnp.sum(dy * xv * r, axis=0, keepdims=True)
    return dx, dg


def rms_bwd(x, dy, dres, g, name):
    def fn(xv, dyv, drv, gv):
        dx, dg = _rms_bwd_math(xv, dyv, gv)
        return dx + drv, dg
    return rw(fn, [(x, 0, D), (dy, 0, D), (dres, 0, D)], [(D, F32)], name, x.shape[0],
              consts=[g], accs=[(1, D)])


def final_loss(x, tgt, g):
    def fn(xv, tv, gv):
        e = xv * _rstd(xv) * gv - tv
        loss = 0.5 * jnp.sum(jnp.sum(e * e, axis=-1, keepdims=True), axis=0, keepdims=True) / D
        dx, dg = _rms_bwd_math(xv, e / D, gv)
        return dx, loss, dg
    return rw(fn, [(x, 0, D), (tgt, 0, D)], [(D, F32)], "final_loss", S, consts=[g],
              accs=[(1, 1), (1, D)])


def _attn_masks(b):
    ii = lax.broadcasted_iota(jnp.int32, (QB, 2 * QB), 0)
    jj = lax.broadcasted_iota(jnp.int32, (QB, 2 * QB), 1)
    dist = ii + QB - jj
    valid = (dist >= 0) & (dist <= QB) & ((jj >= QB) | (b > 0))
    lane = lax.broadcasted_iota(jnp.int32, (1, 128), 1)
    return valid, lane < 64


def attn_fwd(z, d, name):
    ln = S // d
    nb = ln // QB
    zz = z.reshape(ln, d * 6144)

    def cur(off):
        return pl.BlockSpec((QB, D), lambda r, b: (b, r * 6 + off))

    def prev(off):
        return pl.BlockSpec((QB, D), lambda r, b: (jnp.maximum(b - 1, 0), r * 6 + off))

    def body(q_ref, kp_ref, kc_ref, vp_ref, vc_ref, o_ref, l_ref):
        valid, m0 = _attn_masks(pl.program_id(1))
        for hp in range(8):
            sl = slice(hp * 128, (hp + 1) * 128)
            q = q_ref[:, sl] * 0.125
            k = jnp.concatenate([kp_ref[:, sl], kc_ref[:, sl]], axis=0).astype(BF16)
            v = jnp.concatenate([vp_ref[:, sl], vc_ref[:, sl]], axis=0)
            o_acc = jnp.zeros((QB, 128), F32)
            l_out = jnp.zeros((QB, 128), F32)
            for hh in range(2):
                msk = m0 if hh == 0 else jnp.logical_not(m0)
                qm = jnp.where(msk, q, 0.0).astype(BF16)
                s = jnp.where(valid, _nt(qm, k), NEG)
                mx = jnp.max(s, axis=-1, keepdims=True)
                p = jnp.exp(s - mx)
                den = jnp.sum(p, axis=-1, keepdims=True)
                vm = jnp.where(msk, v, 0.0).astype(BF16)
                o_acc = o_acc + _nn((p / den).astype(BF16), vm)
                l_out = jnp.where(msk, mx + jnp.log(den), l_out)
            o_ref[:, sl] = o_acc
            l_ref[:, sl] = l_out

    o, l = pl.pallas_call(
        body, name=name, grid=(d, nb),
        in_specs=[cur(0), prev(1), cur(1), prev(2), cur(2)],
        out_specs=[pl.BlockSpec((QB, D), lambda r, b: (b, r))] * 2,
        out_shape=[jax.ShapeDtypeStruct((ln, d * D), F32)] * 2,
        compiler_params=_cparams(("parallel", "parallel")),
    )(zz, zz, zz, zz, zz)
    return o.reshape(S, D), l.reshape(S, D)


def attn_combine(os, ls, z):
    def fn(o1, o2, o3, l1, l2, l3, ga):
        mx = jnp.maximum(jnp.maximum(l1, l2), l3)
        e1, e2, e3 = jnp.exp(l1 - mx), jnp.exp(l2 - mx), jnp.exp(l3 - mx)
        tot = e1 + e2 + e3
        o = (o1 * e1 + o2 * e2 + o3 * e3) / tot
        return o, mx + jnp.log(tot), o * (ga * _sig(ga))
    ins = [(a, 0, D) for a in os] + [(a, 0, D) for a in ls] + [(z, 3 * D, D)]
    return rw(fn, ins, [(D, F32), (D, F32), (D, BF16)], "attn_combine", S)


def attn_prep(d_cat, z, o):
    def fn(da, ga, ov):
        sg = _sig(ga)
        return da * (ga * sg), da * ov * (sg * (1.0 + ga * (1.0 - sg)))
    return rw(fn, [(d_cat, 0, D), (z, 3 * D, D), (o, 0, D)], [(D, F32), (D, F32)], "attn_prep", S)


def attn_bwd(z, do, o, lse, d, name):
    ln = S // d
    nb = ln // QB
    zz = z.reshape(ln, d * 6144)
    dov, ov, lv = (t.reshape(ln, d * D) for t in (do, o, lse))

    def qb(b):
        return jnp.minimum(b, nb - 1)

    def cur(off):
        return pl.BlockSpec((QB, D), lambda r, b: (qb(b), r * 6 + off))

    def prev(off):
        return pl.BlockSpec((QB, D), lambda r, b: (jnp.maximum(qb(b) - 1, 0), r * 6 + off))

    act = pl.BlockSpec((QB, D), lambda r, b: (qb(b), r))
    kout = pl.BlockSpec((QB, D), lambda r, b: (jnp.maximum(b - 1, 0), r))

    def body(q_ref, kp_ref, kc_ref, vp_ref, vc_ref, do_ref, o_ref, l_ref,
             dq_ref, dk_ref, dv_ref, ck, cv):
        b = pl.program_id(1)

        @pl.when(b == 0)
        def _():
            ck[...] = jnp.zeros_like(ck)
            cv[...] = jnp.zeros_like(cv)

        @pl.when(b < nb)
        def _():
            valid, m0 = _attn_masks(b)
            for hp in range(8):
                sl = slice(hp * 128, (hp + 1) * 128)
                q = q_ref[:, sl] * 0.125
                kf = jnp.concatenate([kp_ref[:, sl], kc_ref[:, sl]], axis=0)
                k = kf.astype(BF16)
                v = jnp.concatenate([vp_ref[:, sl], vc_ref[:, sl]], axis=0).astype(BF16)
                dof = do_ref[:, sl]
                prod = dof * o_ref[:, sl]
                lp = l_ref[:, sl]
                dq = jnp.zeros((QB, 128), F32)
                dk = jnp.zeros((2 * QB, 128), F32)
                dv = jnp.zeros((2 * QB, 128), F32)
                for hh in range(2):
                    msk = m0 if hh == 0 else jnp.logical_not(m0)
                    qm = jnp.where(msk, q, 0.0).astype(BF16)
                    dom = jnp.where(msk, dof, 0.0).astype(BF16)
                    lh = jnp.max(jnp.where(msk, lp, -jnp.inf), axis=-1, keepdims=True)
                    delta = jnp.sum(jnp.where(msk, prod, 0.0), axis=-1, keepdims=True)
                    s = jnp.where(valid, _nt(qm, k), NEG)
                    p = jnp.exp(s - lh)
                    ds = (p * (_nt(dom, v) - delta)).astype(BF16)
                    dq = dq + _nn(ds, jnp.where(msk, kf, 0.0).astype(BF16))
                    dk = dk + _tn(ds, qm)
                    dv = dv + _tn(p.astype(BF16), dom)
                dq_ref[:, sl] = dq * 0.125
                dk_ref[:, sl] = ck[:, sl] + dk[:QB]
                dv_ref[:, sl] = cv[:, sl] + dv[:QB]
                ck[:, sl] = dk[QB:]
                cv[:, sl] = dv[QB:]

        @pl.when(b == nb)
        def _():
            dk_ref[...] = ck[...]
            dv_ref[...] = cv[...]

    dq, dk, dv = pl.pallas_call(
        body, name=name, grid=(d, nb + 1),
        in_specs=[cur(0), prev(1), cur(1), prev(2), cur(2), act, act, act],
        out_specs=[act, kout, kout],
        out_shape=[jax.ShapeDtypeStruct((ln, d * D), F32)] * 3,
        scratch_shapes=[pltpu.VMEM((QB, D), F32), pltpu.VMEM((QB, D), F32)],
        compiler_params=_cparams(("parallel", "arbitrary")),
    )(zz, zz, zz, zz, zz, dov, ov, lv)
    return dq.reshape(S, D), dk.reshape(S, D), dv.reshape(S, D)


def assemble_dz_even(dqs, dks, dvs, dga, dvb, dgb):
    def body(*refs):
        o_ref = refs[-1]
        for j in range(3):
            a, b, c = refs[3 * j:3 * j + 3]
            o_ref[:, j * D:(j + 1) * D] = a[...] + b[...] + c[...]
        for j in range(3):
            o_ref[:, (3 + j) * D:(4 + j) * D] = refs[9 + j][...]
    tr = 128
    blk = pl.BlockSpec((tr, D), lambda i: (i, 0))
    return pl.pallas_call(
        body, name="assemble_dz_even", grid=(S // tr,), in_specs=[blk] * 12,
        out_specs=pl.BlockSpec((tr, 6 * D), lambda i: (i, 0)),
        out_shape=jax.ShapeDtypeStruct((S, 6 * D), F32),
        compiler_params=_cparams(("parallel",)),
    )(*dqs, *dks, *dvs, dga, dvb, dgb)


def _pool_window(g):
    return jnp.where(g == 0, 2.0, jnp.where(g == 1, 4.0, jnp.where(g == 2, 8.0, 16.0)))


def _pool_sel(g, levels):
    return jnp.where(g == 0, levels[0], jnp.where(g == 1, levels[1], jnp.where(g == 2, levels[2], levels[3])))


def _pool_fwd_math(v, g):
    t = lax.broadcasted_iota(jnp.int32, (S, 1), 0)
    s = v
    levels = []
    for k in (1, 2, 4, 8):
        s = s + jnp.where(t >= k, pltpu.roll(s, k, 0), 0.0)
        levels.append(s)
    cnt = jnp.minimum((t + 1).astype(F32), _pool_window(g))
    return _pool_sel(g, levels) / cnt - v, cnt


def pool_fwd(z, pw, ps):
    def body(v_ref, g_ref, pw_ref, ps_ref, o_ref):
        g = pl.program_id(0)
        pooled, _ = _pool_fwd_math(v_ref[...], g)
        mixed = _nn(pooled.astype(BF16), pw_ref[...].astype(BF16))
        gb = g_ref[...]
        o_ref[...] = (mixed * ps_ref[...] * (gb * _sig(gb))).astype(o_ref.dtype)

    return pl.pallas_call(
        body, name="pool_fwd", grid=(4,),
        in_specs=[pl.BlockSpec((S, 256), lambda g: (0, 16 + g)),
                  pl.BlockSpec((S, 256), lambda g: (0, 20 + g)),
                  pl.BlockSpec((None, 256, 256), lambda g: (g, 0, 0)),
                  pl.BlockSpec((1, 256), lambda g: (0, g))],
        out_specs=pl.BlockSpec((S, 256), lambda g: (0, g)),
        out_shape=jax.ShapeDtypeStruct((S, D), BF16),
        compiler_params=_cparams(("parallel",)),
    )(z, z, pw, ps)


def pool_bwd(z, d_cat, pw, ps):
    def body(v_ref, g_ref, d_ref, pw_ref, ps_ref, dv_ref, dg_ref, dpw_ref, dps_ref):
        g = pl.program_id(0)
        v = v_ref[...]
        pooled, cnt = _pool_fwd_math(v, g)
        pwb = pw_ref[...].astype(BF16)
        pb = pooled.astype(BF16)
        mixed = _nn(pb, pwb)
        gb = g_ref[...]
        sg = _sig(gb)
        dout = d_ref[...]
        sc = ps_ref[...]
        dg_ref[...] = dout * mixed * sc * (sg * (1.0 + gb * (1.0 - sg)))
        dms = dout * (gb * sg)
        dps_ref[...] = jnp.sum(dms * mixed, axis=0, keepdims=True)
        dmx = (dms * sc).astype(BF16)
        dpw_ref[...] = _tn(pb, dmx)
        dpooled = _nt(dmx, pwb)
        t = lax.broadcasted_iota(jnp.int32, (S, 1), 0)
        s = dpooled / cnt
        levels = []
        for k in (1, 2, 4, 8):
            s = s + jnp.where(t < S - k, pltpu.roll(s, S - k, 0), 0.0)
            levels.append(s)
        dv_ref[...] = _pool_sel(g, levels) - dpooled

    return pl.pallas_call(
        body, name="pool_bwd", grid=(4,),
        in_specs=[pl.BlockSpec((S, 256), lambda g: (0, 16 + g)),
                  pl.BlockSpec((S, 256), lambda g: (0, 20 + g)),
                  pl.BlockSpec((S, 256), lambda g: (0, 4 + g)),
                  pl.BlockSpec((None, 256, 256), lambda g: (g, 0, 0)),
                  pl.BlockSpec((1, 256), lambda g: (0, g))],
        out_specs=[pl.BlockSpec((S, 256), lambda g: (0, g)),
                   pl.BlockSpec((S, 256), lambda g: (0, g)),
                   pl.BlockSpec((None, 256, 256), lambda g: (g, 0, 0)),
                   pl.BlockSpec((1, 256), lambda g: (0, g))],
        out_shape=[jax.ShapeDtypeStruct((S, D), F32), jax.ShapeDtypeStruct((S, D), F32),
                   jax.ShapeDtypeStruct((4, 256, 256), F32), jax.ShapeDtypeStruct((1, D), F32)],
        compiler_params=_cparams(("parallel",)),
    )(z, z, d_cat, pw, ps)


CH = 128


def _sgu_common(v, lng, lnb, w_ref):
    mu = jnp.mean(v, axis=-1, keepdims=True)
    vc = v - mu
    rs = lax.rsqrt(jnp.mean(vc * vc, axis=-1, keepdims=True) + EPS)
    xhat = vc * rs
    vn = (xhat * lng + lnb).astype(BF16)
    ri = lax.broadcasted_iota(jnp.int32, (CH, CH), 0)
    ci = lax.broadcasted_iota(jnp.int32, (CH, CH), 1)
    tril = ri >= ci
    ws = [jnp.where(tril, w_ref[g], 0.0).astype(BF16) for g in range(4)]
    return xhat, rs, vn, tril, ws


def _zspec(off):
    return pl.BlockSpec((CH, D), lambda c: (c, off))


def _full(shape):
    return pl.BlockSpec(shape, lambda c: (0,) * len(shape))


def sgu_fwd(z, lng, lnb, w, bfull):
    def body(u_ref, v_ref, g_ref, lng_ref, lnb_ref, w_ref, b_ref, o_ref):
        _, _, vn, _, ws = _sgu_common(v_ref[...], lng_ref[...], lnb_ref[...], w_ref)
        for g in range(4):
            sl = slice(g * 256, (g + 1) * 256)
            mixed = _nn(ws[g], vn[:, sl]) + b_ref[:, sl]
            gc = g_ref[:, sl]
            o_ref[:, sl] = (u_ref[:, sl] * mixed * (gc * _sig(gc))).astype(o_ref.dtype)

    return pl.pallas_call(
        body, name="sgu_fwd", grid=(S // CH,),
        in_specs=[_zspec(0), _zspec(1), _zspec(2), _full((1, D)), _full((1, D)),
                  _full((4, CH, CH)), _full((CH, D))],
        out_specs=pl.BlockSpec((CH, D), lambda c: (c, 0)),
        out_shape=jax.ShapeDtypeStruct((S, D), BF16),
        compiler_params=_cparams(("parallel",)),
    )(z, z, z, lng, lnb, w, bfull)


def sgu_bwd(z, d_cat, lng, lnb, w, bfull):
    def body(u_ref, v_ref, g_ref, d_ref, lng_ref, lnb_ref, w_ref, b_ref,
             du_ref, dv_ref, dg_ref, dw_ref, db_ref, dlg_ref, dlb_ref):
        @pl.when(pl.program_id(0) == 0)
        def _():
            dw_ref[...] = jnp.zeros_like(dw_ref)
            db_ref[...] = jnp.zeros_like(db_ref)
            dlg_ref[...] = jnp.zeros_like(dlg_ref)
            dlb_ref[...] = jnp.zeros_like(dlb_ref)

        lng = lng_ref[...]
        xhat, rs, vn, tril, ws = _sgu_common(v_ref[...], lng, lnb_ref[...], w_ref)
        lane = lax.broadcasted_iota(jnp.int32, (1, 128), 1)
        db = jnp.zeros((CH, 128), F32)
        dvn_parts = []
        for g in range(4):
            sl = slice(g * 256, (g + 1) * 256)
            mixed = _nn(ws[g], vn[:, sl]) + b_ref[:, sl]
            gc = g_ref[:, sl]
            sg = _sig(gc)
            u = u_ref[:, sl]
            dc = d_ref[:, sl]
            du_ref[:, sl] = dc * mixed * (gc * sg)
            dg_ref[:, sl] = dc * u * mixed * (sg * (1.0 + gc * (1.0 - sg)))
            dmx = dc * u * (gc * sg)
            db = db + jnp.where(lane == g, jnp.sum(dmx, axis=-1, keepdims=True), 0.0)
            dmb = dmx.astype(BF16)
            dw_ref[g] += jnp.where(tril, _nt(dmb, vn[:, sl]), 0.0)
            dvn_parts.append(_tn(ws[g], dmb))
        db_ref[...] += db
        dvn = jnp.concatenate(dvn_parts, axis=1)
        dlb_ref[...] += jnp.sum(dvn, axis=0, keepdims=True)
        dlg_ref[...] += jnp.sum(dvn * xhat, axis=0, keepdims=True)
        dxh = dvn * lng
        dv_ref[...] = rs * (dxh - jnp.mean(dxh, axis=-1, keepdims=True)
                            - xhat * jnp.mean(dxh * xhat, axis=-1, keepdims=True))

    row = pl.BlockSpec((CH, D), lambda c: (c, 0))
    return pl.pallas_call(
        body, name="sgu_bwd", grid=(S // CH,),
        in_specs=[_zspec(0), _zspec(1), _zspec(2), row, _full((1, D)), _full((1, D)),
                  _full((4, CH, CH)), _full((CH, D))],
        out_specs=[row, row, row, _full((4, CH, CH)), _full((CH, 128)), _full((1, D)), _full((1, D))],
        out_shape=[jax.ShapeDtypeStruct((S, D), F32)] * 3
        + [jax.ShapeDtypeStruct((4, CH, CH), F32), jax.ShapeDtypeStruct((CH, 128), F32),
           jax.ShapeDtypeStruct((1, D), F32), jax.ShapeDtypeStruct((1, D), F32)],
        compiler_params=_cparams(("arbitrary",)),
    )(z, z, z, d_cat, lng, lnb, w, bfull)


TB = 256


def _cmul(ar, ai, br, bi):
    return ar * br - ai * bi, ar * bi + ai * br


def _scan_consts(ar, ai, reverse):
    a2 = _cmul(ar, ai, ar, ai)
    a4 = _cmul(*a2, *a2)
    row = lax.broadcasted_iota(jnp.int32, (8, NS), 0)
    pr = jnp.zeros((8, NS), F32)
    pi = jnp.zeros((8, NS), F32)
    cr, ci = ar, ai
    for r in range(8):
        sel = row == (7 - r if reverse else r)
        pr = jnp.where(sel, cr, pr)
        pi = jnp.where(sel, ci, pi)
        cr, ci = _cmul(cr, ci, ar, ai)
    return ((ar, ai), a2, a4), (pr, pi), row


def scan_fwd(bu, abr, abi):
    def body(bu_ref, ar_ref, ai_ref, h_ref, car, cai):
        @pl.when(pl.program_id(0) == 0)
        def _():
            car[...] = jnp.zeros_like(car)
            cai[...] = jnp.zeros_like(cai)

        pows, (pr, pi), row = _scan_consts(ar_ref[...], ai_ref[...], False)

        def tile(t, carry):
            c_r, c_i = carry
            rows = pl.ds(pl.multiple_of(t * 8, 8), 8)
            xr = bu_ref[rows, 0:NS]
            xi = bu_ref[rows, NS:2 * NS]
            for k, (kr, ki) in zip((1, 2, 4), pows):
                sr = jnp.where(row >= k, pltpu.roll(xr, k, 0), 0.0)
                si = jnp.where(row >= k, pltpu.roll(xi, k, 0), 0.0)
                xr, xi = xr + kr * sr - ki * si, xi + kr * si + ki * sr
            xr, xi = xr + pr * c_r - pi * c_i, xi + pr * c_i + pi * c_r
            h_ref[rows, 0:NS] = xr
            h_ref[rows, NS:2 * NS] = xi
            return (jnp.broadcast_to(xr[7:8, :], (8, NS)), jnp.broadcast_to(xi[7:8, :], (8, NS)))

        c_r, c_i = lax.fori_loop(0, TB // 8, tile, (car[...], cai[...]))
        car[...] = c_r
        cai[...] = c_i

    return pl.pallas_call(
        body, name="s5_scan_fwd", grid=(S // TB,),
        in_specs=[pl.BlockSpec((TB, 2 * NS), lambda i: (i, 0)),
                  pl.BlockSpec((1, NS), lambda i: (0, 0)), pl.BlockSpec((1, NS), lambda i: (0, 0))],
        out_specs=pl.BlockSpec((TB, 2 * NS), lambda i: (i, 0)),
        out_shape=jax.ShapeDtypeStruct((S, 2 * NS), F32),
        scratch_shapes=[pltpu.VMEM((8, NS), F32), pltpu.VMEM((8, NS), F32)],
        compiler_params=_cparams(("arbitrary",)),
    )(bu, abr, abi)


def scan_bwd(eta, h, abr, abi):
    nt = S // TB

    def body(e_ref, h_ref, ar_ref, ai_ref, l_ref, da_ref, car, cai):
        @pl.when(pl.program_id(0) == 0)
        def _():
            car[...] = jnp.zeros_like(car)
            cai[...] = jnp.zeros_like(cai)
            da_ref[...] = jnp.zeros_like(da_ref)

        pows, (pr, pi), row = _scan_consts(ar_ref[...], -ai_ref[...], True)

        def tile(tt, carry):
            c_r, c_i, acr, aci = carry
            t = TB // 8 - 1 - tt
            rows = pl.ds(pl.multiple_of(t * 8, 8), 8)
            xr = e_ref[rows, 0:NS]
            xi = e_ref[rows, NS:2 * NS]
            for k, (kr, ki) in zip((1, 2, 4), pows):
                sr = jnp.where(row < 8 - k, pltpu.roll(xr, 8 - k, 0), 0.0)
                si = jnp.where(row < 8 - k, pltpu.roll(xi, 8 - k, 0), 0.0)
                xr, xi = xr + kr * sr - ki * si, xi + kr * si + ki * sr
            xr, xi = xr + pr * c_r - pi * c_i, xi + pr * c_i + pi * c_r
            l_ref[rows, 0:NS] = xr
            l_ref[rows, NS:2 * NS] = xi
            nr = jnp.where(row < 7, pltpu.roll(xr, 7, 0), c_r)
            ni = jnp.where(row < 7, pltpu.roll(xi, 7, 0), c_i)
            hr = h_ref[rows, 0:NS]
            hi = h_ref[rows, NS:2 * NS]
            acr = acr + hr * nr + hi * ni
            aci = aci + hr * ni - hi * nr
            return (jnp.broadcast_to(xr[0:1, :], (8, NS)), jnp.broadcast_to(xi[0:1, :], (8, NS)), acr, aci)

        zero = jnp.zeros((8, NS), F32)
        c_r, c_i, acr, aci = lax.fori_loop(0, TB // 8, tile, (car[...], cai[...], zero, zero))
        car[...] = c_r
        cai[...] = c_i
        da_ref[:, 0:NS] += acr
        da_ref[:, NS:2 * NS] += aci

    rev = pl.BlockSpec((TB, 2 * NS), lambda i: (nt - 1 - i, 0))
    return pl.pallas_call(
        body, name="s5_scan_bwd", grid=(nt,),
        in_specs=[rev, rev, pl.BlockSpec((1, NS), lambda i: (0, 0)), pl.BlockSpec((1, NS), lambda i: (0, 0))],
        out_specs=[rev, pl.BlockSpec((8, 2 * NS), lambda i: (0, 0))],
        out_shape=[jax.ShapeDtypeStruct((S, 2 * NS), F32), jax.ShapeDtypeStruct((8, 2 * NS), F32)],
        scratch_shapes=[pltpu.VMEM((8, NS), F32), pltpu.VMEM((8, NS), F32)],
        compiler_params=_cparams(("arbitrary",)),
    )(eta, h, abr, abi)


GC = 0.7978845608028654
GA = 0.044715


def s5_post(hc, z, dskip):
    def fn(hv, xd, dv):
        y = hv + dv * xd
        return y, 0.5 * y * (1.0 + jnp.tanh(GC * (y + GA * y * y * y)))
    return rw(fn, [(hc, 0, 512), (z, 3072, 512)], [(512, F32), (512, BF16)], "s5_post", S, consts=[dskip])


def s5_post_bwd(dyg, ypre, z, dskip):
    def fn(dy, y, xd, dv):
        th = jnp.tanh(GC * (y + GA * y * y * y))
        dg = 0.5 * (1.0 + th) + 0.5 * y * (1.0 - th * th) * GC * (1.0 + 3.0 * GA * y * y)
        dyp = dy * dg
        return dyp, dyp * dv, jnp.sum(dyp * xd, axis=0, keepdims=True)
    return rw(fn, [(dyg, 0, 512), (ypre, 0, 512), (z, 3072, 512)], [(512, F32), (512, F32)],
              "s5_post_bwd", S, consts=[dskip], accs=[(1, 512)])


def glu_fwd(t, z):
    def fn(t1, t2, gd):
        return (t1 * _sig(t2) * (gd * _sig(gd)),)
    return rw(fn, [(t, 0, 512), (t, 512, 512), (z, 3584, 512)], [(512, BF16)], "glu_fwd", S)[0]


def glu_bwd(t, z, d_cat):
    def fn(t1, t2, gd, dd):
        s2, sg = _sig(t2), _sig(gd)
        sl = gd * sg
        return (dd * s2 * sl, dd * t1 * s2 * (1.0 - s2) * sl,
                dd * t1 * s2 * (sg * (1.0 + gd * (1.0 - sg))))
    return rw(fn, [(t, 0, 512), (t, 512, 512), (z, 3584, 512), (d_cat, 1024, 512)],
              [(512, BF16), (512, BF16), (512, F32)], "glu_bwd", S)


def assemble_dz_odd(du, dv, dgc, dxd, dgd):
    def body(a, b, c, d, e, o_ref):
        o_ref[:, 0:D] = a[...]
        o_ref[:, D:2 * D] = b[...]
        o_ref[:, 2 * D:3 * D] = c[...]
        o_ref[:, 3 * D:3 * D + 512] = d[...]
        o_ref[:, 3 * D + 512:4 * D] = e[...]
    tr = 256
    blk = pl.BlockSpec((tr, D), lambda i: (i, 0))
    half = pl.BlockSpec((tr, 512), lambda i: (i, 0))
    return pl.pallas_call(
        body, name="assemble_dz_odd", grid=(S // tr,), in_specs=[blk, blk, blk, half, half],
        out_specs=pl.BlockSpec((tr, 4 * D), lambda i: (i, 0)),
        out_shape=jax.ShapeDtypeStruct((S, 4 * D), F32),
        compiler_params=_cparams(("parallel",)),
    )(du, dv, dgc, dxd, dgd)


TQ = 256


def _xattn_probs(qh, kh):
    s = _nt(qh, kh) * 0.0625
    p = jnp.exp(s - jnp.max(s, axis=-1, keepdims=True))
    return p / jnp.sum(p, axis=-1, keepdims=True)


def xattn_fwd(q, kv):
    def body(q_ref, kv_ref, o_ref):
        for h in range(4):
            sl = slice(h * 256, (h + 1) * 256)
            p = _xattn_probs(q_ref[:, sl].astype(BF16), kv_ref[:, sl].astype(BF16))
            vh = kv_ref[:, D + h * 256:D + (h + 1) * 256].astype(BF16)
            o_ref[:, sl] = _nn(p.astype(BF16), vh).astype(o_ref.dtype)

    return pl.pallas_call(
        body, name="xattn_fwd", grid=(S // TQ,),
        in_specs=[pl.BlockSpec((TQ, D), lambda i: (i, 0)), pl.BlockSpec((MEM, 2 * D), lambda i: (0, 0))],
        out_specs=pl.BlockSpec((TQ, D), lambda i: (i, 0)),
        out_shape=jax.ShapeDtypeStruct((S, D), BF16),
        compiler_params=_cparams(("parallel",)),
    )(q, kv)


def xattn_bwd(q, kv, d_o):
    def body(q_ref, kv_ref, do_ref, dq_ref, dkv_ref):
        @pl.when(pl.program_id(0) == 0)
        def _():
            dkv_ref[...] = jnp.zeros_like(dkv_ref)

        for h in range(4):
            sl = slice(h * 256, (h + 1) * 256)
            vs = slice(D + h * 256, D + (h + 1) * 256)
            qh = q_ref[:, sl].astype(BF16)
            kh = kv_ref[:, sl].astype(BF16)
            vh = kv_ref[:, vs].astype(BF16)
            doh = do_ref[:, sl].astype(BF16)
            p = _xattn_probs(qh, kh)
            dp = _nt(doh, vh)
            ds = (p * (dp - jnp.sum(p * dp, axis=-1, keepdims=True)) * 0.0625).astype(BF16)
            dq_ref[:, sl] = _nn(ds, kh)
            dkv_ref[:, sl] += _tn(ds, qh)
            dkv_ref[:, vs] += _tn(p.astype(BF16), doh)

    return pl.pallas_call(
        body, name="xattn_bwd", grid=(S // TQ,),
        in_specs=[pl.BlockSpec((TQ, D), lambda i: (i, 0)), pl.BlockSpec((MEM, 2 * D), lambda i: (0, 0)),
                  pl.BlockSpec((TQ, D), lambda i: (i, 0))],
        out_specs=[pl.BlockSpec((TQ, D), lambda i: (i, 0)), pl.BlockSpec((MEM, 2 * D), lambda i: (0, 0))],
        out_shape=[jax.ShapeDtypeStruct((S, D), F32), jax.ShapeDtypeStruct((MEM, 2 * D), F32)],
        compiler_params=_cparams(("arbitrary",)),
    )(q, kv, d_o)


def _s5_disc(a_re, a_im, log_dt, b_re, b_im):
    dt = jnp.exp(log_dt)[:, None]
    mag = jnp.exp(dt * a_re)
    abr = mag * jnp.cos(dt * a_im)
    abi = mag * jnp.sin(dt * a_im)
    nr, ni = abr - 1.0, abi
    inv = 1.0 / (a_re * a_re + a_im * a_im)
    cr = (nr * a_re + ni * a_im) * inv
    ci = (ni * a_re - nr * a_im) * inv
    bbr = cr[..., None] * b_re - ci[..., None] * b_im
    bbi = cr[..., None] * b_im + ci[..., None] * b_re
    return abr, abi, bbr, bbi


def _blockdiag(t):
    g, a, b = t.shape
    eye = jnp.eye(g, dtype=t.dtype)
    return (eye[:, None, :, None] * t[:, :, None, :]).reshape(g * a, g * b)


def _blocks(mat, a, b):
    return jnp.einsum("gagb->gab", mat.reshape(NG, a, NG, b))


def _fwd_even(i, x, P):
    hn = rms_fwd(x, P["norm_ab"][i:i + 1], "rms_ab_fwd")
    z = mm(m2(hn), mcs(P["w_in_ab"], i), "nn", "in_ab")
    os, ls = zip(*[attn_fwd(z, d, "attn_fwd_d%d" % d) for d in PATTERNS])
    o, lse, a_out = attn_combine(os, ls, z)
    b_out = pool_fwd(z, P["pool_w"][i], P["pool_scale"][i:i + 1])
    cat = jnp.concatenate([a_out, b_out], axis=1)
    x_mid = mm(m2(cat), m3(P["w_out_ab"], i), "nn", "out_ab", add=m2(x))
    return x_mid, dict(x=x, hn=hn, z=z, o=o, lse=lse, cat=cat)


def _bwd_even(i, dx_mid, sv, P, G):
    z = sv["z"]
    d_cat = mm(m2(dx_mid), m3(P["w_out_ab"], i), "nt", "out_ab_dx")
    G["w_out_ab"][i] = mm(m2(sv["cat"]), m2(dx_mid), "tn", "out_ab_dw").reshape(4, 512, D)
    do, dga = attn_prep(d_cat, z, sv["o"])
    dqs, dks, dvs = zip(*[attn_bwd(z, do, sv["o"], sv["lse"], d, "attn_bwd_d%d" % d) for d in PATTERNS])
    dvb, dgb, dpw, dps = pool_bwd(z, d_cat, P["pool_w"][i], P["pool_scale"][i:i + 1])
    G["pool_w"][i] = dpw.reshape(4, 4, 64, 256).transpose(1, 0, 2, 3).reshape(4, 256, 256)
    G["pool_scale"][i] = dps[0]
    d_z = assemble_dz_even(dqs, dks, dvs, dga, dvb, dgb)
    d_hn = mm(m2(d_z), mcs(P["w_in_ab"], i), "nt", "in_ab_dx")
    G["w_in_ab"][i] = mm(m2(sv["hn"]), m2(d_z), "tn", "in_ab_dw", out=outcs(D, 1536))
    dx, dg = rms_bwd(sv["x"], d_hn, dx_mid, P["norm_ab"][i:i + 1], "rms_ab_bwd")
    G["norm_ab"][i] = dg[0]
    return dx


def _fwd_odd(i, x, P):
    hn = rms_fwd(x, P["norm_cd"][i:i + 1], "rms_cd_fwd")
    z = mm(m2(hn), mcs(P["w_in_cd"], i), "nn", "in_cd")
    bfull = jnp.repeat(P["sgu_b"][i].T, 256, axis=1)
    c_out = sgu_fwd(z, P["sgu_ln_g"][i:i + 1], P["sgu_ln_b"][i:i + 1], P["sgu_w"][i], bfull)
    disc, disc_vjp = jax.vjp(_s5_disc, P["s5_a_re"][i], P["s5_a_im"][i], P["s5_log_dt"][i],
                             P["s5_b_re"][i], P["s5_b_im"][i])
    abr, abi, bbr, bbi = disc
    bbd = jnp.concatenate([_blockdiag(bbr.transpose(0, 2, 1)), _blockdiag(bbi.transpose(0, 2, 1))], axis=1)
    cbd = jnp.concatenate([_blockdiag(P["s5_c_re"][i].transpose(0, 2, 1)),
                           -_blockdiag(P["s5_c_im"][i].transpose(0, 2, 1))], axis=0)
    abr, abi = abr.reshape(1, NS), abi.reshape(1, NS)
    bu = mm(m2(z, 3072, 512), m2(bbd), "nn", "s5_bu")
    h = scan_fwd(bu, abr, abi)
    hc = mm(m2(h), m2(cbd), "nn", "s5_hc")
    dskip = P["s5_d"][i:i + 1]
    ypre, yg = s5_post(hc, z, dskip)
    w12 = jnp.concatenate([P["glu_w1"][i], P["glu_w2"][i]], axis=1)
    t = mm(m2(yg), m2(w12), "nn", "glu_t")
    d_out = glu_fwd(t, z)
    cat = jnp.concatenate([c_out, d_out], axis=1)
    x_mid = mm(m2(cat), m3(P["w_out_cd"], i), "nn", "out_cd", add=m2(x))
    return x_mid, dict(x=x, hn=hn, z=z, bfull=bfull, disc_vjp=disc_vjp, bbd=bbd, cbd=cbd, abr=abr,
                       abi=abi, h=h, ypre=ypre, yg=yg, w12=w12, t=t, cat=cat, dskip=dskip)


def _bwd_odd(i, dx_mid, sv, P, G):
    z = sv["z"]
    d_cat = mm(m2(dx_mid), m3(P["w_out_cd"], i), "nt", "out_cd_dx")
    G["w_out_cd"][i] = mm(m2(sv["cat"]), m2(dx_mid), "tn", "out_cd_dw").reshape(4, 384, D)
    du, dv, dgc, dws, dbs, dlg, dlb = sgu_bwd(z, d_cat, P["sgu_ln_g"][i:i + 1], P["sgu_ln_b"][i:i + 1],
                                               P["sgu_w"][i], sv["bfull"])
    G["sgu_w"][i], G["sgu_b"][i] = dws, dbs[:, :4].T
    G["sgu_ln_g"][i], G["sgu_ln_b"][i] = dlg[0], dlb[0]
    dt1, dt2, dgd = glu_bwd(sv["t"], z, d_cat)
    dt = jnp.concatenate([dt1, dt2], axis=1)
    gw12 = mm(m2(sv["yg"]), m2(dt), "tn", "glu_dw")
    G["glu_w1"][i] = gw12[:, :512].reshape(4, 128, 512)
    G["glu_w2"][i] = gw12[:, 512:].reshape(4, 128, 512)
    dyg = mm(m2(dt), m2(sv["w12"]), "nt", "glu_dx")
    dypre, dxd1, dd = s5_post_bwd(dyg, sv["ypre"], z, sv["dskip"])
    G["s5_d"][i] = dd[0]
    gcbd = mm(m2(sv["h"]), m2(dypre), "tn", "s5_dc")
    G["s5_c_re"][i] = _blocks(gcbd[:NS], NP, NH).transpose(0, 2, 1)
    G["s5_c_im"][i] = -_blocks(gcbd[NS:], NP, NH).transpose(0, 2, 1)
    eta = mm(m2(dypre), m2(sv["cbd"]), "nt", "s5_eta")
    lam, dacc = scan_bwd(eta, sv["h"], sv["abr"], sv["abi"])
    gbbd = mm(m2(z, 3072, 512), m2(lam), "tn", "s5_db")
    dxd = mm(m2(lam), m2(sv["bbd"]), "nt", "s5_dx", add=m2(dxd1))
    dacc = jnp.sum(dacc, axis=0)
    d_bbr = _blocks(gbbd[:, :NS], NH, NP).transpose(0, 2, 1)
    d_bbi = _blocks(gbbd[:, NS:], NH, NP).transpose(0, 2, 1)
    (G["s5_a_re"][i], G["s5_a_im"][i], G["s5_log_dt"][i], G["s5_b_re"][i], G["s5_b_im"][i]) = sv["disc_vjp"](
        (dacc[:NS].reshape(NG, NP), dacc[NS:].reshape(NG, NP), d_bbr, d_bbi))
    d_z = assemble_dz_odd(du, dv, dgc, dxd, dgd)
    d_hn = mm(m2(d_z), mcs(P["w_in_cd"], i), "nt", "in_cd_dx")
    G["w_in_cd"][i] = mm(m2(sv["hn"]), m2(d_z), "tn", "in_cd_dw", out=outcs(D, 1024))
    dx, dg = rms_bwd(sv["x"], d_hn, dx_mid, P["norm_cd"][i:i + 1], "rms_cd_bwd")
    G["norm_cd"][i] = dg[0]
    return dx


def _fwd_x(l, x, mem_n, P):
    hx = rms_fwd(x, P["norm_x"][l:l + 1], "rms_x_fwd")
    q = mm(m2(hx), m3(P["w_xq"], l), "nn", "xq")
    kv = mm(m2(mem_n), mcs(P["w_xkv"], l), "nn", "xkv")
    ox = xattn_fwd(q, kv)
    x_out = mm(m2(ox), m3(P["w_xo"], l), "nn", "xo", add=m2(x))
    return x_out, dict(x=x, hx=hx, q=q, kv=kv, ox=ox)


def _bwd_x(l, dx_out, sv, mem_n, d_memn, P, G):
    d_ox = mm(m2(dx_out), m3(P["w_xo"], l), "nt", "xo_dx")
    G["w_xo"][l] = mm(m2(sv["ox"]), m2(dx_out), "tn", "xo_dw").reshape(4, 256, D)
    dq, dkv = xattn_bwd(sv["q"], sv["kv"], d_ox)
    G["w_xq"][l] = mm(m2(sv["hx"]), m2(dq), "tn", "xq_dw").reshape(4, 256, D)
    d_hx = mm(m2(dq), m3(P["w_xq"], l), "nt", "xq_dx")
    G["w_xkv"][l] = mm(m2(mem_n), m2(dkv), "tn", "xkv_dw", out=outcs(D, 512))
    d_memn = mm(m2(dkv), mcs(P["w_xkv"], l), "nt", "xkv_dx", add=None if d_memn is None else m2(d_memn))
    dx, dg = rms_bwd(sv["x"], d_hx, dx_out, P["norm_x"][l:l + 1], "rms_x_bwd")
    G["norm_x"][l] = dg[0]
    return dx, d_memn


def local_step(x, mem, tgt, P):
    G = {k: [None] * n for k, n in (
        ("norm_ab", 2), ("w_in_ab", 2), ("pool_w", 2), ("pool_scale", 2), ("w_out_ab", 2), ("norm_cd", 2),
        ("w_in_cd", 2), ("sgu_ln_g", 2), ("sgu_ln_b", 2), ("sgu_w", 2), ("sgu_b", 2), ("s5_a_re", 2),
        ("s5_a_im", 2), ("s5_log_dt", 2), ("s5_b_re", 2), ("s5_b_im", 2), ("s5_c_re", 2), ("s5_c_im", 2),
        ("s5_d", 2), ("glu_w1", 2), ("glu_w2", 2), ("w_out_cd", 2), ("norm_x", 4), ("w_xq", 4),
        ("w_xkv", 4), ("w_xo", 4))}
    mem_g = P["mem_norm"].reshape(1, D)
    mem_n = rms_fwd(mem, mem_g, "rms_mem_fwd")
    saved = []
    for layer in range(4):
        i = layer // 2
        x, sv_m = (_fwd_even if layer % 2 == 0 else _fwd_odd)(i, x, P)
        x, sv_x = _fwd_x(layer, x, mem_n, P)
        saved.append((sv_m, sv_x))
    dx, loss, dgf = final_loss(x, tgt, P["final_norm"].reshape(1, D))
    G["final_norm"] = dgf[0]
    d_memn = None
    for layer in reversed(range(4)):
        i = layer // 2
        sv_m, sv_x = saved[layer]
        dx, d_memn = _bwd_x(layer, dx, sv_x, mem_n, d_memn, P, G)
        dx = (_bwd_even if layer % 2 == 0 else _bwd_odd)(i, dx, sv_m, P, G)
    _, dgm = rms_bwd(mem, d_memn, d_memn, mem_g, "rms_mem_bwd")
    G["mem_norm"] = dgm[0]
    return loss, dx, G


ANY = pl.BlockSpec(memory_space=pl.ANY)


def _place():
    x, y, c = lax.axis_index("x"), lax.axis_index("y"), lax.axis_index("c")
    chips = [(1 - x, y), (x, 1 - y), (1 - x, 1 - y)]
    return x, y, c, 2 * x + y, (x, y, 1 - c), chips


def _remote(src, dst, send, recv, k, dev):
    return pltpu.make_async_remote_copy(src_ref=src, dst_ref=dst, send_sem=send.at[k], recv_sem=recv.at[k],
                                        device_id=dev, device_id_type=MESHID)


def allgather_big(shards):
    n = len(shards)

    def body(*refs):
        ins, outs = refs[:n], refs[n:2 * n]
        send, recv, lsem = refs[2 * n:]
        x, y, c, jme, sib, chips = _place()
        local = []
        for a in range(n):
            cp = pltpu.make_async_copy(ins[a], outs[a].at[:, jme], lsem.at[a])
            cp.start()
            local.append(cp)

        def half(a, hc):
            lh = shards[a].shape[0] // 2
            return pl.ds(hc * lh, lh)

        first, passed = [], []
        for a in range(n):
            for k, chip in enumerate(chips):
                cp = _remote(ins[a].at[half(a, c)], outs[a].at[half(a, c), jme], send, recv, a * 6 + k, (*chip, c))
                cp.start()
                first.append(cp)
        for a in range(n):
            for k, chip in enumerate(chips):
                jt = 2 * chip[0] + chip[1]
                piece = outs[a].at[half(a, c), jt]
                _remote(piece, piece, send, recv, a * 6 + k, (*chip, c)).wait_recv()
                fw = _remote(piece, piece, send, recv, a * 6 + 3 + k, sib)
                fw.start()
                passed.append(fw)
        for a in range(n):
            for k, chip in enumerate(chips):
                piece = outs[a].at[half(a, 1 - c), 2 * chip[0] + chip[1]]
                _remote(piece, piece, send, recv, a * 6 + 3 + k, sib).wait_recv()
        for cp in first + passed:
            cp.wait_send()
        for cp in local:
            cp.wait()

    return pl.pallas_call(
        body, name="allgather_big", in_specs=[ANY] * n, out_specs=[ANY] * n,
        out_shape=[jax.ShapeDtypeStruct((s.shape[0], 4) + s.shape[1:], s.dtype) for s in shards],
        scratch_shapes=[pltpu.SemaphoreType.DMA((6 * n,)), pltpu.SemaphoreType.DMA((6 * n,)),
                        pltpu.SemaphoreType.DMA((n,))],
    )(*shards)


def allgather_small(slab):
    def body(in_ref, out_ref, send, recv, lsem):
        x, y, c, jme, sib, chips = _place()
        loc = pltpu.make_async_copy(in_ref, out_ref.at[jme], lsem.at[0])
        loc.start()
        cps = [_remote(in_ref, out_ref.at[jme], send, recv, k, (*chip, c)) for k, chip in enumerate(chips)]
        for cp in cps:
            cp.start()
        for k, chip in enumerate(chips):
            piece = out_ref.at[2 * chip[0] + chip[1]]
            _remote(piece, piece, send, recv, k, (*chip, c)).wait_recv()
        for cp in cps:
            cp.wait_send()
        loc.wait()

    return pl.pallas_call(
        body, name="allgather_small", in_specs=[ANY], out_specs=ANY,
        out_shape=jax.ShapeDtypeStruct((4,) + slab.shape, slab.dtype),
        scratch_shapes=[pltpu.SemaphoreType.DMA((3,)), pltpu.SemaphoreType.DMA((3,)), pltpu.SemaphoreType.DMA((1,))],
    )(slab)


def allreduce_small(v):
    def body(v_ref, o_ref, r0, r1, r2, send, recv):
        x, y, c, jme, sib, chips = _place()
        peers = [sib, (1 - x, y, c), (x, 1 - y, c)]
        o_ref[...] = v_ref[...]
        for k, buf in enumerate((r0, r1, r2)):
            cp = _remote(o_ref, buf, send, recv, k, peers[k])
            cp.start()
            cp.wait()
            o_ref[...] = o_ref[...] + buf[...]

    vm = pl.BlockSpec(memory_space=pltpu.VMEM)
    return pl.pallas_call(
        body, name="allreduce_small", in_specs=[vm], out_specs=vm,
        out_shape=jax.ShapeDtypeStruct(v.shape, v.dtype),
        scratch_shapes=[pltpu.VMEM(v.shape, v.dtype)] * 3 + [pltpu.SemaphoreType.DMA((3,)), pltpu.SemaphoreType.DMA((3,))],
        compiler_params=pltpu.CompilerParams(vmem_limit_bytes=VMEM_LIMIT),
    )(v)


def rs_pair_exchange(gs):
    n = len(gs)

    def body(*refs):
        ins, outs = refs[:n], refs[n:2 * n]
        send, recv = refs[2 * n:]
        x, y, c, jme, sib, chips = _place()
        cps = [_remote(ins[a].at[:, 1 - c], outs[a], send, recv, a, sib) for a in range(n)]
        for cp in cps:
            cp.start()
        for cp in cps:
            cp.wait()

    return pl.pallas_call(
        body, name="rs_pair_exchange", in_specs=[ANY] * n, out_specs=[ANY] * n,
        out_shape=[jax.ShapeDtypeStruct((4,) + g.shape[2:], g.dtype) for g in gs],
        scratch_shapes=[pltpu.SemaphoreType.DMA((n,)), pltpu.SemaphoreType.DMA((n,))],
    )(*gs)


def rs_pair_sum(g4, got, cidx):
    _, _, rh, cols = g4.shape
    tr = rh if rh <= 256 else 256

    def body(c_ref, a_ref, b_ref, o_ref):
        o_ref[...] = (a_ref[...] + b_ref[...]).astype(o_ref.dtype)

    return pl.pallas_call(
        body, name="rs_pair_sum",
        grid_spec=pltpu.PrefetchScalarGridSpec(
            num_scalar_prefetch=1, grid=(4, rh // tr),
            in_specs=[pl.BlockSpec((None, None, tr, cols), lambda j, t, cr: (j, cr[0], t, 0)),
                      pl.BlockSpec((None, tr, cols), lambda j, t, cr: (j, t, 0))],
            out_specs=pl.BlockSpec((None, tr, cols), lambda j, t, cr: (j, t, 0))),
        out_shape=jax.ShapeDtypeStruct((4, rh, cols), BF16),
        compiler_params=_cparams(("parallel", "parallel")),
    )(cidx, g4, got)


def rs_chip_exchange(ps):
    n = len(ps)

    def body(*refs):
        ins, outs = refs[:n], refs[n:2 * n]
        send, recv, lsem = refs[2 * n:]
        x, y, c, jme, sib, chips = _place()
        local = []
        for a in range(n):
            cp = pltpu.make_async_copy(ins[a].at[jme], outs[a].at[jme], lsem.at[a])
            cp.start()
            local.append(cp)
        cps = []
        for a in range(n):
            for k, chip in enumerate(chips):
                cp = _remote(ins[a].at[2 * chip[0] + chip[1]], outs[a].at[jme], send, recv, a * 3 + k, (*chip, c))
                cp.start()
                cps.append(cp)
        for a in range(n):
            for k, chip in enumerate(chips):
                slot = outs[a].at[2 * chip[0] + chip[1]]
                _remote(slot, slot, send, recv, a * 3 + k, (*chip, c)).wait_recv()
        for cp in cps:
            cp.wait_send()
        for cp in local:
            cp.wait()

    return pl.pallas_call(
        body, name="rs_chip_exchange", in_specs=[ANY] * n, out_specs=[ANY] * n,
        out_shape=[jax.ShapeDtypeStruct(p.shape, p.dtype) for p in ps],
        scratch_shapes=[pltpu.SemaphoreType.DMA((3 * n,)), pltpu.SemaphoreType.DMA((3 * n,)),
                        pltpu.SemaphoreType.DMA((n,))],
    )(*ps)


def rs_chip_sum(q):
    _, rh, cols = q.shape
    tr = rh if rh <= 256 else 256

    def body(q_ref, o_ref):
        o_ref[...] = ((q_ref[0].astype(F32) + q_ref[1].astype(F32)) + q_ref[2].astype(F32)) + q_ref[3].astype(F32)

    return pl.pallas_call(
        body, name="rs_chip_sum", grid=(rh // tr,),
        in_specs=[pl.BlockSpec((4, tr, cols), lambda t: (0, t, 0))],
        out_specs=pl.BlockSpec((tr, cols), lambda t: (t, 0)),
        out_shape=jax.ShapeDtypeStruct((rh, cols), F32),
        compiler_params=_cparams(("parallel",)),
    )(q)


def rs_pair_gather(rs, layout):
    n = len(rs)

    def body(*refs):
        ins, outs = refs[:n], refs[n:n + len(layout)]
        send, recv, lsem = refs[n + len(layout):]
        x, y, c, jme, sib, chips = _place()
        local, cps = [], []
        for w, (first, layers) in enumerate(layout):
            for l in range(layers):
                a = first + l
                cp = pltpu.make_async_copy(ins[a], outs[w].at[l, c], lsem.at[a])
                cp.start()
                local.append(cp)
                cp = _remote(ins[a], outs[w].at[l, c], send, recv, a, sib)
                cp.start()
                cps.append(cp)
        for w, (first, layers) in enumerate(layout):
            for l in range(layers):
                slot = outs[w].at[l, 1 - c]
                _remote(slot, slot, send, recv, first + l, sib).wait_recv()
        for cp in cps:
            cp.wait_send()
        for cp in local:
            cp.wait()

    return pl.pallas_call(
        body, name="rs_pair_gather", in_specs=[ANY] * n, out_specs=[ANY] * len(layout),
        out_shape=[jax.ShapeDtypeStruct((layers, 2) + rs[first].shape, F32) for first, layers in layout],
        scratch_shapes=[pltpu.SemaphoreType.DMA((n,)), pltpu.SemaphoreType.DMA((n,)), pltpu.SemaphoreType.DMA((n,))],
    )(*rs)


def _adamw_math(w, g, m, v):
    m = B1 * m + (1.0 - B1) * g
    v = B2 * v + (1.0 - B2) * (g * g)
    m_hat = m / (1.0 - B1 ** STEP)
    v_hat = v / (1.0 - B2 ** STEP)
    return -LR * (m_hat / (jnp.sqrt(v_hat) + AEPS) + WD * w), m, v


def adamw(w, g, m, v, name):
    rows, cols = w.shape
    tr = 256 if rows % 256 == 0 else rows
    return rw(_adamw_math, [(a, 0, cols) for a in (w, g, m, v)], [(cols, F32)] * 3, name, rows, tr=tr)


WEIGHTS = ["norm_ab", "w_in_ab", "pool_w", "pool_scale", "w_out_ab", "norm_cd", "w_in_cd", "sgu_ln_g", "sgu_ln_b",
           "sgu_w", "sgu_b", "s5_a_re", "s5_a_im", "s5_log_dt", "s5_b_re", "s5_b_im", "s5_c_re", "s5_c_im", "s5_d",
           "glu_w1", "glu_w2", "w_out_cd", "norm_x", "w_xq", "w_xkv", "w_xo", "mem_norm", "final_norm"]
INPUTS = ["x", "mem"] + WEIGHTS + ["loss_target"] + ["m_" + n for n in WEIGHTS] + ["v_" + n for n in WEIGHTS]
BIG = ["w_in_ab", "w_out_ab", "w_in_cd", "w_out_cd", "w_xq", "w_xkv", "w_xo", "glu_w1", "glu_w2", "pool_w"]
COL_SHARDED = ("w_in_ab", "w_in_cd", "w_xkv")
SMALL = [n for n in WEIGHTS if n not in BIG]
SMALL_SHARDED = {"norm_cd": 256, "sgu_ln_g": 256, "sgu_ln_b": 256, "s5_d": 128}
PACK = 256 * 128


def _pack(arrs):
    flat = jnp.concatenate([a.reshape(-1) for a in arrs])
    pad = (-flat.shape[0]) % PACK
    return jnp.concatenate([flat, jnp.zeros((pad,), flat.dtype)]).reshape(-1, 128)


def _unpack(packed, shapes):
    flat, out, off = packed.reshape(-1), [], 0
    for s in shapes:
        n = 1
        for d in s:
            n *= d
        out.append(flat[off:off + n].reshape(s))
        off += n
    return out


def _mat(a):
    return a.reshape(a.shape[0], -1, a.shape[-1])


def kernel(*args):
    a = dict(zip(INPUTS, args))
    x_i, y_i, c_i = lax.axis_index("x"), lax.axis_index("y"), lax.axis_index("c")
    j = 2 * x_i + y_i

    gathered = dict(zip(BIG, allgather_big([_mat(a[n]).astype(BF16) for n in BIG])))
    slab = jnp.concatenate([a["norm_cd"], a["sgu_ln_g"], a["sgu_ln_b"],
                            jnp.pad(a["s5_d"], ((0, 0), (0, 128)))], axis=0)
    gslab = allgather_small(slab)
    P = {n: a[n] for n in SMALL}
    for k, n in enumerate(("norm_cd", "sgu_ln_g", "sgu_ln_b", "s5_d")):
        wd = SMALL_SHARDED[n]
        P[n] = gslab[:, 2 * k:2 * k + 2, :wd].transpose(1, 0, 2).reshape(2, 4 * wd)
    for n in BIG:
        g = gathered[n]
        if n in COL_SHARDED:
            P[n] = g
        elif n == "pool_w":
            P[n] = g.reshape(2, 4, 4, 64, 256).transpose(0, 2, 1, 3, 4).reshape(2, 4, 256, 256)
        else:
            P[n] = g.reshape(g.shape[0], 4 * g.shape[2], g.shape[3])

    loss, dx, G = local_step(a["x"][0], a["mem"][0], a["loss_target"][0], P)
    loss = lax.psum(loss[0, 0], ("x", "y", "c"))

    layout, flat = [], []
    for n in BIG:
        layout.append((len(flat), len(G[n])))
        flat += [g.reshape(4, 2, g.shape[1] // 2, g.shape[2]) for g in G[n]]
    got = rs_pair_exchange(flat)
    cidx = jnp.reshape(c_i, (1,)).astype(jnp.int32)
    pair = [rs_pair_sum(g4, r, cidx) for g4, r in zip(flat, got)]
    red = [rs_chip_sum(q) for q in rs_chip_exchange(pair)]
    gbig = dict(zip(BIG, rs_pair_gather(red, layout)))

    outs = {}
    for n in BIG:
        shp = a[n].shape
        g2 = gbig[n].reshape(-1, shp[-1])
        d2, m2_, v2_ = adamw(a[n].reshape(g2.shape), g2, a["m_" + n].reshape(g2.shape),
                             a["v_" + n].reshape(g2.shape), "adamw_" + n)
        outs[n] = tuple(t.reshape(shp) for t in (g2, d2, m2_, v2_))

    gfull = [jnp.stack(G[n]) if isinstance(G[n], list) else G[n] for n in SMALL]
    shapes = [g.shape for g in gfull]
    gsum = _unpack(allreduce_small(_pack(gfull)), shapes)
    gloc = []
    for n, g in zip(SMALL, gsum):
        if n in SMALL_SHARDED:
            g = lax.dynamic_slice_in_dim(g, j * SMALL_SHARDED[n], SMALL_SHARDED[n], axis=1)
        gloc.append(g)
    lshapes = [a[n].shape for n in SMALL]
    packed = [_pack(t) for t in ([a[n] for n in SMALL], gloc, [a["m_" + n] for n in SMALL], [a["v_" + n] for n in SMALL])]
    small = [_unpack(t, lshapes) for t in adamw(*packed, "adamw_small")]
    for k, n in enumerate(SMALL):
        outs[n] = (gloc[k], small[0][k], small[1][k], small[2][k])

    res = [loss, dx[None]]
    for part in range(4):
        res += [outs[n][part] for n in WEIGHTS]
    return tuple(res)
```

```python
import math

import jax
import jax.numpy as jnp
from jax import lax
from jax.experimental import pallas as pl
from jax.experimental.pallas import tpu as pltpu

F32, BF16 = jnp.float32, jnp.bfloat16
S, D = 2048, 1024
MEM = 256
EPS = 1e-6
NEG = -1e30
QB = 128
PATTERNS = (1, 4, 16)
NG, NP, NH = 32, 64, 16
NS = NG * NP
LR, B1, B2, AEPS, WD, STEP = 0.001, 0.9, 0.999, 1e-08, 0.01, 10
MESHID = pl.DeviceIdType.MESH
VMEM_LIMIT = 56 * 1024 * 1024


def _cparams(sem):
    return pltpu.CompilerParams(dimension_semantics=sem, vmem_limit_bytes=VMEM_LIMIT)


def _sig(x):
    return 1.0 / (1.0 + jnp.exp(-x))


def _dot(a, b, dims):
    return lax.dot_general(a, b, (dims, ((), ())), preferred_element_type=F32)


def _nn(a, b):
    return _dot(a, b, ((1,), (0,)))


def _nt(a, b):
    return _dot(a, b, ((1,), (1,)))


def _tn(a, b):
    return _dot(a, b, ((0,), (0,)))


_DIMS = {"nn": ((1,), (0,)), "nt": ((1,), (1,)), "tn": ((0,), (0,))}


def _tile(dim, cc=None, cap=1024):
    for t in (1024, 512, 256, 128):
        if t <= cap and dim % t == 0 and (cc is None or cc % t == 0):
            return t
    return dim


def m2(arr, col_off=0, ncols=None):
    rows, cols = arr.shape
    ncols = cols - col_off if ncols is None else ncols

    def spec(tr, tc, rc):
        assert col_off % tc == 0
        return pl.BlockSpec((tr, tc), lambda *g: (rc(*g)[0], rc(*g)[1] + col_off // tc))
    return (arr, rows, ncols, spec, None if col_off == 0 else col_off)


def m3(arr, i):
    def spec(tr, tc, rc):
        return pl.BlockSpec((None, tr, tc), lambda *g: (i,) + tuple(rc(*g)))
    return (arr, arr.shape[1], arr.shape[2], spec, None)


def mcs(arr, i):
    cs = arr.shape[3]

    def spec(tr, tc, rc):
        n = cs // tc
        return pl.BlockSpec((None, None, tr, tc),
                            lambda *g: (i, rc(*g)[1] // n, rc(*g)[0], rc(*g)[1] % n))
    return (arr, arr.shape[2], 4 * cs, spec, cs)


def out2(rows, cols):
    def spec(tr, tc, rc):
        return pl.BlockSpec((tr, tc), lambda *g: tuple(rc(*g)))
    return ((rows, cols), spec, None)


def outcs(rows, cs):
    def spec(tr, tc, rc):
        n = cs // tc
        return pl.BlockSpec((None, tr, tc), lambda *g: (rc(*g)[1] // n, rc(*g)[0], rc(*g)[1] % n))
    return ((4, rows, cs), spec, cs)


def _both(a, b):
    if a is None:
        return b
    if b is None:
        return a
    return math.gcd(a, b)


def mm(a, b, mode, name, add=None, out=None, out_dtype=F32):
    a_arr, a_r, a_c, a_spec, a_cc = a
    b_arr, b_r, b_c, b_spec, b_cc = b
    if mode == "nn":
        m, k, n = a_r, a_c, b_c
        assert b_r == k
        ccm, cck, ccn = None, a_cc, b_cc
    elif mode == "nt":
        m, k, n = a_r, a_c, b_r
        assert b_c == k
        ccm, cck, ccn = None, _both(a_cc, b_cc), None
    else:
        m, k, n = a_c, a_r, b_c
        assert b_r == k
        ccm, cck, ccn = a_cc, None, b_cc
    out = out2(m, n) if out is None else out
    o_shape, o_spec, o_cc = out
    ccn = _both(ccn, o_cc)
    if add is not None:
        ccn = _both(ccn, add[4])
    tm, tn, tk = _tile(m, ccm), _tile(n, ccn), _tile(k, cck, cap=512)
    nk = k // tk
    if mode == "nn":
        in_specs = [a_spec(tm, tk, lambda i, j, kk: (i, kk)), b_spec(tk, tn, lambda i, j, kk: (kk, j))]
    elif mode == "nt":
        in_specs = [a_spec(tm, tk, lambda i, j, kk: (i, kk)), b_spec(tn, tk, lambda i, j, kk: (j, kk))]
    else:
        in_specs = [a_spec(tk, tm, lambda i, j, kk: (kk, i)), b_spec(tk, tn, lambda i, j, kk: (kk, j))]
    args = [a_arr, b_arr]
    if add is not None:
        in_specs.append(add[3](tm, tn, lambda i, j, kk: (i, j)))
        args.append(add[0])
    dims = _DIMS[mode]
    has_add = add is not None

    def body(*refs):
        a_ref, b_ref = refs[0], refs[1]
        add_ref = refs[2] if has_add else None
        o_ref, acc = refs[-2], refs[-1]
        kk = pl.program_id(2)

        @pl.when(kk == 0)
        def _():
            acc[...] = jnp.zeros_like(acc)

        acc[...] += _dot(a_ref[...].astype(BF16), b_ref[...].astype(BF16), dims)

        @pl.when(kk == nk - 1)
        def _():
            r = acc[...]
            if has_add:
                r = r + add_ref[...].astype(F32)
            o_ref[...] = r.astype(o_ref.dtype)

    return pl.pallas_call(
        body, name=name, grid=(m // tm, n // tn, nk), in_specs=in_specs,
        out_specs=o_spec(tm, tn, lambda i, j, kk: (i, j)),
        out_shape=jax.ShapeDtypeStruct(o_shape, out_dtype),
        scratch_shapes=[pltpu.VMEM((tm, tn), F32)],
        compiler_params=_cparams(("parallel", "parallel", "arbitrary")),
    )(*args)


def rw(fn, ins, outs, name, rows, tr=256, consts=(), accs=()):
    n_in, n_c, n_o, n_a = len(ins), len(consts), len(outs), len(accs)
    in_specs = []
    for arr, off, width in ins:
        assert off % width == 0
        in_specs.append(pl.BlockSpec((tr, width), lambda i, o=off // width: (i, o)))
    for c in consts:
        in_specs.append(pl.BlockSpec(c.shape, lambda i: (0, 0)))
    out_specs = [pl.BlockSpec((tr, w), lambda i: (i, 0)) for w, _ in outs]
    out_specs += [pl.BlockSpec(s, lambda i: (0, 0)) for s in accs]
    out_shape = [jax.ShapeDtypeStruct((rows, w), dt) for w, dt in outs]
    out_shape += [jax.ShapeDtypeStruct(s, F32) for s in accs]

    def body(*refs):
        vals = [r[...] for r in refs[:n_in + n_c]]
        o_refs = refs[n_in + n_c:n_in + n_c + n_o]
        a_refs = refs[n_in + n_c + n_o:]
        res = fn(*vals)
        for r, v in zip(o_refs, res[:n_o]):
            r[...] = v.astype(r.dtype)
        if n_a:
            @pl.when(pl.program_id(0) == 0)
            def _():
                for r in a_refs:
                    r[...] = jnp.zeros_like(r)
            for r, v in zip(a_refs, res[n_o:]):
                r[...] += v

    res = pl.pallas_call(
        body, name=name, grid=(rows // tr,), in_specs=in_specs, out_specs=out_specs,
        out_shape=out_shape,
        compiler_params=_cparams(("arbitrary",) if n_a else ("parallel",)),
    )(*[a for a, _, _ in ins], *consts)
    return res


def _rstd(x):
    return lax.rsqrt(jnp.mean(x * x, axis=-1, keepdims=True) + EPS)


def rms_fwd(x, g, name):
    def fn(xv, gv):
        xv = xv.astype(F32)
        return (xv * _rstd(xv) * gv,)
    return rw(fn, [(x, 0, D)], [(D, BF16)], name, x.shape[0], consts=[g])[0]


def _rms_bwd_math(xv, dy, gv):
    r = _rstd(xv)
    dyg = dy * gv
    dx = r * dyg - xv * (r * r * r / D) * jnp.sum(dyg * xv, axis=-1, keepdims=True)
    dg = jnp.sum(dy * xv * r, axis=0, keepdims=True)
    return dx, dg


def rms_bwd(x, dy, dres, g, name):
    def fn(xv, dyv, drv, gv):
        dx, dg = _rms_bwd_math(xv, dyv, gv)
        return dx + drv, dg
    return rw(fn, [(x, 0, D), (dy, 0, D), (dres, 0, D)], [(D, F32)], name, x.shape[0],
              consts=[g], accs=[(1, D)])


def final_loss(x, tgt, g):
    def fn(xv, tv, gv):
        e = xv * _rstd(xv) * gv - tv
        loss = 0.5 * jnp.sum(jnp.sum(e * e, axis=-1, keepdims=True), axis=0, keepdims=True) / D
        dx, dg = _rms_bwd_math(xv, e / D, gv)
        return dx, loss, dg
    return rw(fn, [(x, 0, D), (tgt, 0, D)], [(D, F32)], "final_loss", S, consts=[g],
              accs=[(1, 1), (1, D)])


def _attn_masks(b):
    ii = lax.broadcasted_iota(jnp.int32, (QB, 2 * QB), 0)
    jj = lax.broadcasted_iota(jnp.int32, (QB, 2 * QB), 1)
    dist = ii + QB - jj
    valid = (dist >= 0) & (dist <= QB) & ((jj >= QB) | (b > 0))
    lane = lax.broadcasted_iota(jnp.int32, (1, 128), 1)
    return valid, lane < 64


def _attn_rows(idx, d):
    if d == 1:
        b = idx
        cur = pl.ds(pl.multiple_of(b * QB, QB), QB)
        prev = pl.ds(pl.multiple_of(jnp.maximum(b - 1, 0) * QB, QB), QB)
    else:
        r, b = lax.rem(idx, d), lax.div(idx, d)
        cur = pl.ds(r + b * (QB * d), QB, stride=d)
        prev = pl.ds(r + jnp.maximum(b - 1, 0) * (QB * d), QB, stride=d)
    return cur, prev, b


NBLK = S // QB


def _colblk(off):
    return pl.BlockSpec((S, 128), lambda hp: (0, off * 8 + hp))


def attn_fwd(z):
    def body(q_ref, k_ref, v_ref, g_ref, o_ref, l_ref, a_ref, os, ls):
        for pi, d in enumerate(PATTERNS):
            def step(idx, carry, pi=pi, d=d):
                cur, prev, b = _attn_rows(idx, d)
                valid, m0 = _attn_masks(b)
                q = q_ref[cur, :] * 0.125
                k = jnp.concatenate([k_ref[prev, :], k_ref[cur, :]], axis=0).astype(BF16)
                v = jnp.concatenate([v_ref[prev, :], v_ref[cur, :]], axis=0)
                o_acc = jnp.zeros((QB, 128), F32)
                l_out = jnp.zeros((QB, 128), F32)
                for hh in range(2):
                    msk = m0 if hh == 0 else jnp.logical_not(m0)
                    qm = jnp.where(msk, q, 0.0).astype(BF16)
                    s = jnp.where(valid, _nt(qm, k), NEG)
                    mx = jnp.max(s, axis=-1, keepdims=True)
                    p = jnp.exp(s - mx)
                    den = jnp.sum(p, axis=-1, keepdims=True)
                    vm = jnp.where(msk, v, 0.0).astype(BF16)
                    o_acc = o_acc + _nn((p / den).astype(BF16), vm)
                    l_out = jnp.where(msk, mx + jnp.log(den), l_out)
                os[pi, cur, :] = o_acc
                ls[pi, cur, :] = l_out
                return carry
            lax.fori_loop(0, NBLK, step, 0)
        l1, l2, l3 = ls[0], ls[1], ls[2]
        mx = jnp.maximum(jnp.maximum(l1, l2), l3)
        e1, e2, e3 = jnp.exp(l1 - mx), jnp.exp(l2 - mx), jnp.exp(l3 - mx)
        tot = e1 + e2 + e3
        o = (os[0] * e1 + os[1] * e2 + os[2] * e3) / tot
        ga = g_ref[...]
        o_ref[...] = o
        l_ref[...] = mx + jnp.log(tot)
        a_ref[...] = (o * (ga * _sig(ga))).astype(a_ref.dtype)

    out = pl.BlockSpec((S, 128), lambda hp: (0, hp))
    return pl.pallas_call(
        body, name="attn_fwd", grid=(8,),
        in_specs=[_colblk(0), _colblk(1), _colblk(2), _colblk(3)], out_specs=[out] * 3,
        out_shape=[jax.ShapeDtypeStruct((S, D), F32), jax.ShapeDtypeStruct((S, D), F32),
                   jax.ShapeDtypeStruct((S, D), BF16)],
        scratch_shapes=[pltpu.VMEM((3, S, 128), F32), pltpu.VMEM((3, S, 128), F32)],
        compiler_params=_cparams(("parallel",)),
    )(z, z, z, z)


def attn_bwd(z, d_cat, o, lse):
    def body(q_ref, k_ref, v_ref, g_ref, da_ref, o_ref, l_ref, dq_ref, dk_ref, dv_ref, dg_ref, do_s, pr_s):
        ga = g_ref[...]
        sg = _sig(ga)
        da = da_ref[...]
        ov = o_ref[...]
        do = da * (ga * sg)
        dg_ref[...] = da * ov * (sg * (1.0 + ga * (1.0 - sg)))
        do_s[...] = do
        pr_s[...] = do * ov
        dq_ref[...] = jnp.zeros_like(dq_ref)
        dk_ref[...] = jnp.zeros_like(dk_ref)
        dv_ref[...] = jnp.zeros_like(dv_ref)
        for d in PATTERNS:
            def step(idx, carry, d=d):
                cur, prev, b = _attn_rows(idx, d)
                valid, m0 = _attn_masks(b)
                q = q_ref[cur, :] * 0.125
                kf = jnp.concatenate([k_ref[prev, :], k_ref[cur, :]], axis=0)
                k = kf.astype(BF16)
                v = jnp.concatenate([v_ref[prev, :], v_ref[cur, :]], axis=0).astype(BF16)
                dof = do_s[cur, :]
                prod = pr_s[cur, :]
                lp = l_ref[cur, :]
                dq = jnp.zeros((QB, 128), F32)
                dk = jnp.zeros((2 * QB, 128), F32)
                dv = jnp.zeros((2 * QB, 128), F32)
                for hh in range(2):
                    msk = m0 if hh == 0 else jnp.logical_not(m0)
                    qm = jnp.where(msk, q, 0.0).astype(BF16)
                    dom = jnp.where(msk, dof, 0.0).astype(BF16)
                    lh = jnp.max(jnp.where(msk, lp, -jnp.inf), axis=-1, keepdims=True)
                    delta = jnp.sum(jnp.where(msk, prod, 0.0), axis=-1, keepdims=True)
                    s = jnp.where(valid, _nt(qm, k), NEG)
                    p = jnp.exp(s - lh)
                    ds = (p * (_nt(dom, v) - delta)).astype(BF16)
                    dq = dq + _nn(ds, jnp.where(msk, kf, 0.0).astype(BF16))
                    dk = dk + _tn(ds, qm)
                    dv = dv + _tn(p.astype(BF16), dom)
                dq_ref[cur, :] = dq_ref[cur, :] + dq * 0.125
                dk_ref[prev, :] = dk_ref[prev, :] + dk[:QB]
                dv_ref[prev, :] = dv_ref[prev, :] + dv[:QB]
                dk_ref[cur, :] = dk_ref[cur, :] + dk[QB:]
                dv_ref[cur, :] = dv_ref[cur, :] + dv[QB:]
                return carry
            lax.fori_loop(0, NBLK, step, 0)

    blk = pl.BlockSpec((S, 128), lambda hp: (0, hp))
    return pl.pallas_call(
        body, name="attn_bwd", grid=(8,),
        in_specs=[_colblk(0), _colblk(1), _colblk(2), _colblk(3), blk, blk, blk], out_specs=[blk] * 4,
        out_shape=[jax.ShapeDtypeStruct((S, D), F32)] * 4,
        scratch_shapes=[pltpu.VMEM((S, 128), F32), pltpu.VMEM((S, 128), F32)],
        compiler_params=_cparams(("parallel",)),
    )(z, z, z, z, d_cat, o, lse)


def assemble_dz_even(parts):
    def body(*refs):
        o_ref = refs[-1]
        for j in range(6):
            o_ref[:, j * D:(j + 1) * D] = refs[j][...]
    tr = 256
    blk = pl.BlockSpec((tr, D), lambda i: (i, 0))
    return pl.pallas_call(
        body, name="assemble_dz_even", grid=(S // tr,), in_specs=[blk] * 6,
        out_specs=pl.BlockSpec((tr, 6 * D), lambda i: (i, 0)),
        out_shape=jax.ShapeDtypeStruct((S, 6 * D), F32),
        compiler_params=_cparams(("parallel",)),
    )(*parts)


def _pool_window(g):
    return jnp.where(g == 0, 2.0, jnp.where(g == 1, 4.0, jnp.where(g == 2, 8.0, 16.0)))


def _pool_sel(g, levels):
    return jnp.where(g == 0, levels[0], jnp.where(g == 1, levels[1], jnp.where(g == 2, levels[2], levels[3])))


def _pool_fwd_math(v, g):
    t = lax.broadcasted_iota(jnp.int32, (S, 1), 0)
    s = v
    levels = []
    for k in (1, 2, 4, 8):
        s = s + jnp.where(t >= k, pltpu.roll(s, k, 0), 0.0)
        levels.append(s)
    cnt = jnp.minimum((t + 1).astype(F32), _pool_window(g))
    return _pool_sel(g, levels) / cnt - v, cnt


def pool_fwd(z, pw, ps):
    def body(v_ref, g_ref, pw_ref, ps_ref, o_ref):
        g = pl.program_id(0)
        pooled, _ = _pool_fwd_math(v_ref[...], g)
        mixed = _nn(pooled.astype(BF16), pw_ref[...].astype(BF16))
        gb = g_ref[...]
        o_ref[...] = (mixed * ps_ref[...] * (gb * _sig(gb))).astype(o_ref.dtype)

    return pl.pallas_call(
        body, name="pool_fwd", grid=(4,),
        in_specs=[pl.BlockSpec((S, 256), lambda g: (0, 16 + g)),
                  pl.BlockSpec((S, 256), lambda g: (0, 20 + g)),
                  pl.BlockSpec((None, 256, 256), lambda g: (g, 0, 0)),
                  pl.BlockSpec((1, 256), lambda g: (0, g))],
        out_specs=pl.BlockSpec((S, 256), lambda g: (0, g)),
        out_shape=jax.ShapeDtypeStruct((S, D), BF16),
        compiler_params=_cparams(("parallel",)),
    )(z, z, pw, ps)


def pool_bwd(z, d_cat, pw, ps):
    def body(v_ref, g_ref, d_ref, pw_ref, ps_ref, dv_ref, dg_ref, dpw_ref, dps_ref):
        g = pl.program_id(0)
        v = v_ref[...]
        pooled, cnt = _pool_fwd_math(v, g)
        pwb = pw_ref[...].astype(BF16)
        pb = pooled.astype(BF16)
        mixed = _nn(pb, pwb)
        gb = g_ref[...]
        sg = _sig(gb)
        dout = d_ref[...]
        sc = ps_ref[...]
        dg_ref[...] = dout * mixed * sc * (sg * (1.0 + gb * (1.0 - sg)))
        dms = dout * (gb * sg)
        dps_ref[...] = jnp.sum(dms * mixed, axis=0, keepdims=True)
        dmx = (dms * sc).astype(BF16)
        dpw_ref[...] = _tn(pb, dmx)
        dpooled = _nt(dmx, pwb)
        t = lax.broadcasted_iota(jnp.int32, (S, 1), 0)
        s = dpooled / cnt
        levels = []
        for k in (1, 2, 4, 8):
            s = s + jnp.where(t < S - k, pltpu.roll(s, S - k, 0), 0.0)
            levels.append(s)
        dv_ref[...] = _pool_sel(g, levels) - dpooled

    return pl.pallas_call(
        body, name="pool_bwd", grid=(4,),
        in_specs=[pl.BlockSpec((S, 256), lambda g: (0, 16 + g)),
                  pl.BlockSpec((S, 256), lambda g: (0, 20 + g)),
                  pl.BlockSpec((S, 256), lambda g: (0, 4 + g)),
                  pl.BlockSpec((None, 256, 256), lambda g: (g, 0, 0)),
                  pl.BlockSpec((1, 256), lambda g: (0, g))],
        out_specs=[pl.BlockSpec((S, 256), lambda g: (0, g)),
                   pl.BlockSpec((S, 256), lambda g: (0, g)),
                   pl.BlockSpec((None, 256, 256), lambda g: (g, 0, 0)),
                   pl.BlockSpec((1, 256), lambda g: (0, g))],
        out_shape=[jax.ShapeDtypeStruct((S, D), F32), jax.ShapeDtypeStruct((S, D), F32),
                   jax.ShapeDtypeStruct((4, 256, 256), F32), jax.ShapeDtypeStruct((1, D), F32)],
        compiler_params=_cparams(("parallel",)),
    )(z, z, d_cat, pw, ps)


CH = 128


def _sgu_common(v, lng, lnb, w_ref):
    mu = jnp.mean(v, axis=-1, keepdims=True)
    vc = v - mu
    rs = lax.rsqrt(jnp.mean(vc * vc, axis=-1, keepdims=True) + EPS)
    xhat = vc * rs
    vn = (xhat * lng + lnb).astype(BF16)
    ri = lax.broadcasted_iota(jnp.int32, (CH, CH), 0)
    ci = lax.broadcasted_iota(jnp.int32, (CH, CH), 1)
    tril = ri >= ci
    ws = [jnp.where(tril, w_ref[g], 0.0).astype(BF16) for g in range(4)]
    return xhat, rs, vn, tril, ws


def _zspec(off):
    return pl.BlockSpec((CH, D), lambda c: (c, off))


def _full(shape):
    return pl.BlockSpec(shape, lambda c: (0,) * len(shape))


def sgu_fwd(z, lng, lnb, w, bfull):
    def body(u_ref, v_ref, g_ref, lng_ref, lnb_ref, w_ref, b_ref, o_ref):
        _, _, vn, _, ws = _sgu_common(v_ref[...], lng_ref[...], lnb_ref[...], w_ref)
        for g in range(4):
            sl = slice(g * 256, (g + 1) * 256)
            mixed = _nn(ws[g], vn[:, sl]) + b_ref[:, sl]
            gc = g_ref[:, sl]
            o_ref[:, sl] = (u_ref[:, sl] * mixed * (gc * _sig(gc))).astype(o_ref.dtype)

    return pl.pallas_call(
        body, name="sgu_fwd", grid=(S // CH,),
        in_specs=[_zspec(0), _zspec(1), _zspec(2), _full((1, D)), _full((1, D)),
                  _full((4, CH, CH)), _full((CH, D))],
        out_specs=pl.BlockSpec((CH, D), lambda c: (c, 0)),
        out_shape=jax.ShapeDtypeStruct((S, D), BF16),
        compiler_params=_cparams(("parallel",)),
    )(z, z, z, lng, lnb, w, bfull)


def sgu_bwd(z, d_cat, lng, lnb, w, bfull):
    def body(u_ref, v_ref, g_ref, d_ref, lng_ref, lnb_ref, w_ref, b_ref,
             du_ref, dv_ref, dg_ref, dw_ref, db_ref, dlg_ref, dlb_ref):
        @pl.when(pl.program_id(0) == 0)
        def _():
            dw_ref[...] = jnp.zeros_like(dw_ref)
            db_ref[...] = jnp.zeros_like(db_ref)
            dlg_ref[...] = jnp.zeros_like(dlg_ref)
            dlb_ref[...] = jnp.zeros_like(dlb_ref)

        lng = lng_ref[...]
        xhat, rs, vn, tril, ws = _sgu_common(v_ref[...], lng, lnb_ref[...], w_ref)
        lane = lax.broadcasted_iota(jnp.int32, (1, 128), 1)
        db = jnp.zeros((CH, 128), F32)
        dvn_parts = []
        for g in range(4):
            sl = slice(g * 256, (g + 1) * 256)
            mixed = _nn(ws[g], vn[:, sl]) + b_ref[:, sl]
            gc = g_ref[:, sl]
            sg = _sig(gc)
            u = u_ref[:, sl]
            dc = d_ref[:, sl]
            du_ref[:, sl] = dc * mixed * (gc * sg)
            dg_ref[:, sl] = dc * u * mixed * (sg * (1.0 + gc * (1.0 - sg)))
            dmx = dc * u * (gc * sg)
            db = db + jnp.where(lane == g, jnp.sum(dmx, axis=-1, keepdims=True), 0.0)
            dmb = dmx.astype(BF16)
            dw_ref[g] += jnp.where(tril, _nt(dmb, vn[:, sl]), 0.0)
            dvn_parts.append(_tn(ws[g], dmb))
        db_ref[...] += db
        dvn = jnp.concatenate(dvn_parts, axis=1)
        dlb_ref[...] += jnp.sum(dvn, axis=0, keepdims=True)
        dlg_ref[...] += jnp.sum(dvn * xhat, axis=0, keepdims=True)
        dxh = dvn * lng
        dv_ref[...] = rs * (dxh - jnp.mean(dxh, axis=-1, keepdims=True)
                            - xhat * jnp.mean(dxh * xhat, axis=-1, keepdims=True))

    row = pl.BlockSpec((CH, D), lambda c: (c, 0))
    return pl.pallas_call(
        body, name="sgu_bwd", grid=(S // CH,),
        in_specs=[_zspec(0), _zspec(1), _zspec(2), row, _full((1, D)), _full((1, D)),
                  _full((4, CH, CH)), _full((CH, D))],
        out_specs=[row, row, row, _full((4, CH, CH)), _full((CH, 128)), _full((1, D)), _full((1, D))],
        out_shape=[jax.ShapeDtypeStruct((S, D), F32)] * 3
        + [jax.ShapeDtypeStruct((4, CH, CH), F32), jax.ShapeDtypeStruct((CH, 128), F32),
           jax.ShapeDtypeStruct((1, D), F32), jax.ShapeDtypeStruct((1, D), F32)],
        compiler_params=_cparams(("arbitrary",)),
    )(z, z, z, d_cat, lng, lnb, w, bfull)


TB = 256


def _cmul(ar, ai, br, bi):
    return ar * br - ai * bi, ar * bi + ai * br


def _scan_consts(ar, ai, reverse):
    a2 = _cmul(ar, ai, ar, ai)
    a4 = _cmul(*a2, *a2)
    row = lax.broadcasted_iota(jnp.int32, (8, NS), 0)
    pr = jnp.zeros((8, NS), F32)
    pi = jnp.zeros((8, NS), F32)
    cr, ci = ar, ai
    for r in range(8):
        sel = row == (7 - r if reverse else r)
        pr = jnp.where(sel, cr, pr)
        pi = jnp.where(sel, ci, pi)
        cr, ci = _cmul(cr, ci, ar, ai)
    return ((ar, ai), a2, a4), (pr, pi), row


def scan_fwd(bu, abr, abi):
    def body(bu_ref, ar_ref, ai_ref, h_ref, car, cai):
        @pl.when(pl.program_id(0) == 0)
        def _():
            car[...] = jnp.zeros_like(car)
            cai[...] = jnp.zeros_like(cai)

        pows, (pr, pi), row = _scan_consts(ar_ref[...], ai_ref[...], False)

        def tile(t, carry):
            c_r, c_i = carry
            rows = pl.ds(pl.multiple_of(t * 8, 8), 8)
            xr = bu_ref[rows, 0:NS]
            xi = bu_ref[rows, NS:2 * NS]
            for k, (kr, ki) in zip((1, 2, 4), pows):
                sr = jnp.where(row >= k, pltpu.roll(xr, k, 0), 0.0)
                si = jnp.where(row >= k, pltpu.roll(xi, k, 0), 0.0)
                xr, xi = xr + kr * sr - ki * si, xi + kr * si + ki * sr
            xr, xi = xr + pr * c_r - pi * c_i, xi + pr * c_i + pi * c_r
            h_ref[rows, 0:NS] = xr
            h_ref[rows, NS:2 * NS] = xi
            return (jnp.broadcast_to(xr[7:8, :], (8, NS)), jnp.broadcast_to(xi[7:8, :], (8, NS)))

        c_r, c_i = lax.fori_loop(0, TB // 8, tile, (car[...], cai[...]))
        car[...] = c_r
        cai[...] = c_i

    return pl.pallas_call(
        body, name="s5_scan_fwd", grid=(S // TB,),
        in_specs=[pl.BlockSpec((TB, 2 * NS), lambda i: (i, 0)),
                  pl.BlockSpec((1, NS), lambda i: (0, 0)), pl.BlockSpec((1, NS), lambda i: (0, 0))],
        out_specs=pl.BlockSpec((TB, 2 * NS), lambda i: (i, 0)),
        out_shape=jax.ShapeDtypeStruct((S, 2 * NS), F32),
        scratch_shapes=[pltpu.VMEM((8, NS), F32), pltpu.VMEM((8, NS), F32)],
        compiler_params=_cparams(("arbitrary",)),
    )(bu, abr, abi)


def scan_bwd(eta, h, abr, abi):
    nt = S // TB

    def body(e_ref, h_ref, ar_ref, ai_ref, l_ref, da_ref, car, cai):
        @pl.when(pl.program_id(0) == 0)
        def _():
            car[...] = jnp.zeros_like(car)
            cai[...] = jnp.zeros_like(cai)
            da_ref[...] = jnp.zeros_like(da_ref)

        pows, (pr, pi), row = _scan_consts(ar_ref[...], -ai_ref[...], True)

        def tile(tt, carry):
            c_r, c_i, acr, aci = carry
            t = TB // 8 - 1 - tt
            rows = pl.ds(pl.multiple_of(t * 8, 8), 8)
            xr = e_ref[rows, 0:NS]
            xi = e_ref[rows, NS:2 * NS]
            for k, (kr, ki) in zip((1, 2, 4), pows):
                sr = jnp.where(row < 8 - k, pltpu.roll(xr, 8 - k, 0), 0.0)
                si = jnp.where(row < 8 - k, pltpu.roll(xi, 8 - k, 0), 0.0)
                xr, xi = xr + kr * sr - ki * si, xi + kr * si + ki * sr
            xr, xi = xr + pr * c_r - pi * c_i, xi + pr * c_i + pi * c_r
            l_ref[rows, 0:NS] = xr
            l_ref[rows, NS:2 * NS] = xi
            nr = jnp.where(row < 7, pltpu.roll(xr, 7, 0), c_r)
            ni = jnp.where(row < 7, pltpu.roll(xi, 7, 0), c_i)
            hr = h_ref[rows, 0:NS]
            hi = h_ref[rows, NS:2 * NS]
            acr = acr + hr * nr + hi * ni
            aci = aci + hr * ni - hi * nr
            return (jnp.broadcast_to(xr[0:1, :], (8, NS)), jnp.broadcast_to(xi[0:1, :], (8, NS)), acr, aci)

        zero = jnp.zeros((8, NS), F32)
        c_r, c_i, acr, aci = lax.fori_loop(0, TB // 8, tile, (car[...], cai[...], zero, zero))
        car[...] = c_r
        cai[...] = c_i
        da_ref[:, 0:NS] += acr
        da_ref[:, NS:2 * NS] += aci

    rev = pl.BlockSpec((TB, 2 * NS), lambda i: (nt - 1 - i, 0))
    return pl.pallas_call(
        body, name="s5_scan_bwd", grid=(nt,),
        in_specs=[rev, rev, pl.BlockSpec((1, NS), lambda i: (0, 0)), pl.BlockSpec((1, NS), lambda i: (0, 0))],
        out_specs=[rev, pl.BlockSpec((8, 2 * NS), lambda i: (0, 0))],
        out_shape=[jax.ShapeDtypeStruct((S, 2 * NS), F32), jax.ShapeDtypeStruct((8, 2 * NS), F32)],
        scratch_shapes=[pltpu.VMEM((8, NS), F32), pltpu.VMEM((8, NS), F32)],
        compiler_params=_cparams(("arbitrary",)),
    )(eta, h, abr, abi)


GC = 0.7978845608028654
GA = 0.044715


def s5_post(hc, z, dskip):
    def fn(hv, xd, dv):
        y = hv + dv * xd
        return y, 0.5 * y * (1.0 + jnp.tanh(GC * (y + GA * y * y * y)))
    return rw(fn, [(hc, 0, 512), (z, 3072, 512)], [(512, F32), (512, BF16)], "s5_post", S, consts=[dskip])


def s5_post_bwd(dyg, ypre, z, dskip):
    def fn(dy, y, xd, dv):
        th = jnp.tanh(GC * (y + GA * y * y * y))
        dg = 0.5 * (1.0 + th) + 0.5 * y * (1.0 - th * th) * GC * (1.0 + 3.0 * GA * y * y)
        dyp = dy * dg
        return dyp, dyp * dv, jnp.sum(dyp * xd, axis=0, keepdims=True)
    return rw(fn, [(dyg, 0, 512), (ypre, 0, 512), (z, 3072, 512)], [(512, F32), (512, F32)],
              "s5_post_bwd", S, consts=[dskip], accs=[(1, 512)])


def glu_fwd(t, z):
    def fn(t1, t2, gd):
        return (t1 * _sig(t2) * (gd * _sig(gd)),)
    return rw(fn, [(t, 0, 512), (t, 512, 512), (z, 3584, 512)], [(512, BF16)], "glu_fwd", S)[0]


def glu_bwd(t, z, d_cat):
    def fn(t1, t2, gd, dd):
        s2, sg = _sig(t2), _sig(gd)
        sl = gd * sg
        return (dd * s2 * sl, dd * t1 * s2 * (1.0 - s2) * sl,
                dd * t1 * s2 * (sg * (1.0 + gd * (1.0 - sg))))
    return rw(fn, [(t, 0, 512), (t, 512, 512), (z, 3584, 512), (d_cat, 1024, 512)],
              [(512, BF16), (512, BF16), (512, F32)], "glu_bwd", S)


def assemble_dz_odd(du, dv, dgc, dxd, dgd):
    def body(a, b, c, d, e, o_ref):
        o_ref[:, 0:D] = a[...]
        o_ref[:, D:2 * D] = b[...]
        o_ref[:, 2 * D:3 * D] = c[...]
        o_ref[:, 3 * D:3 * D + 512] = d[...]
        o_ref[:, 3 * D + 512:4 * D] = e[...]
    tr = 256
    blk = pl.BlockSpec((tr, D), lambda i: (i, 0))
    half = pl.BlockSpec((tr, 512), lambda i: (i, 0))
    return pl.pallas_call(
        body, name="assemble_dz_odd", grid=(S // tr,), in_specs=[blk, blk, blk, half, half],
        out_specs=pl.BlockSpec((tr, 4 * D), lambda i: (i, 0)),
        out_shape=jax.ShapeDtypeStruct((S, 4 * D), F32),
        compiler_params=_cparams(("parallel",)),
    )(du, dv, dgc, dxd, dgd)


TQ = 256


def _xattn_probs(qh, kh):
    s = _nt(qh, kh) * 0.0625
    p = jnp.exp(s - jnp.max(s, axis=-1, keepdims=True))
    return p / jnp.sum(p, axis=-1, keepdims=True)


def xattn_fwd(q, kv):
    def body(q_ref, kv_ref, o_ref):
        for h in range(4):
            sl = slice(h * 256, (h + 1) * 256)
            p = _xattn_probs(q_ref[:, sl].astype(BF16), kv_ref[:, sl].astype(BF16))
            vh = kv_ref[:, D + h * 256:D + (h + 1) * 256].astype(BF16)
            o_ref[:, sl] = _nn(p.astype(BF16), vh).astype(o_ref.dtype)

    return pl.pallas_call(
        body, name="xattn_fwd", grid=(S // TQ,),
        in_specs=[pl.BlockSpec((TQ, D), lambda i: (i, 0)), pl.BlockSpec((MEM, 2 * D), lambda i: (0, 0))],
        out_specs=pl.BlockSpec((TQ, D), lambda i: (i, 0)),
        out_shape=jax.ShapeDtypeStruct((S, D), BF16),
        compiler_params=_cparams(("parallel",)),
    )(q, kv)


def xattn_bwd(q, kv, d_o):
    def body(q_ref, kv_ref, do_ref, dq_ref, dkv_ref):
        @pl.when(pl.program_id(0) == 0)
        def _():
            dkv_ref[...] = jnp.zeros_like(dkv_ref)

        for h in range(4):
            sl = slice(h * 256, (h + 1) * 256)
            vs = slice(D + h * 256, D + (h + 1) * 256)
            qh = q_ref[:, sl].astype(BF16)
            kh = kv_ref[:, sl].astype(BF16)
            vh = kv_ref[:, vs].astype(BF16)
            doh = do_ref[:, sl].astype(BF16)
            p = _xattn_probs(qh, kh)
            dp = _nt(doh, vh)
            ds = (p * (dp - jnp.sum(p * dp, axis=-1, keepdims=True)) * 0.0625).astype(BF16)
            dq_ref[:, sl] = _nn(ds, kh)
            dkv_ref[:, sl] += _tn(ds, qh)
            dkv_ref[:, vs] += _tn(p.astype(BF16), doh)

    return pl.pallas_call(
        body, name="xattn_bwd", grid=(S // TQ,),
        in_specs=[pl.BlockSpec((TQ, D), lambda i: (i, 0)), pl.BlockSpec((MEM, 2 * D), lambda i: (0, 0)),
                  pl.BlockSpec((TQ, D), lambda i: (i, 0))],
        out_specs=[pl.BlockSpec((TQ, D), lambda i: (i, 0)), pl.BlockSpec((MEM, 2 * D), lambda i: (0, 0))],
        out_shape=[jax.ShapeDtypeStruct((S, D), F32), jax.ShapeDtypeStruct((MEM, 2 * D), F32)],
        compiler_params=_cparams(("arbitrary",)),
    )(q, kv, d_o)


def _s5_disc(a_re, a_im, log_dt, b_re, b_im):
    dt = jnp.exp(log_dt)[:, None]
    mag = jnp.exp(dt * a_re)
    abr = mag * jnp.cos(dt * a_im)
    abi = mag * jnp.sin(dt * a_im)
    nr, ni = abr - 1.0, abi
    inv = 1.0 / (a_re * a_re + a_im * a_im)
    cr = (nr * a_re + ni * a_im) * inv
    ci = (ni * a_re - nr * a_im) * inv
    bbr = cr[..., None] * b_re - ci[..., None] * b_im
    bbi = cr[..., None] * b_im + ci[..., None] * b_re
    return abr, abi, bbr, bbi


def _blockdiag(t):
    g, a, b = t.shape
    eye = jnp.eye(g, dtype=t.dtype)
    return (eye[:, None, :, None] * t[:, :, None, :]).reshape(g * a, g * b)


def _blocks(mat, a, b):
    return jnp.einsum("gagb->gab", mat.reshape(NG, a, NG, b))


def _fwd_even(i, x, P):
    hn = rms_fwd(x, P["norm_ab"][i:i + 1], "rms_ab_fwd")
    z = mm(m2(hn), mcs(P["w_in_ab"], i), "nn", "in_ab")
    o, lse, a_out = attn_fwd(z)
    b_out = pool_fwd(z, P["pool_w"][i], P["pool_scale"][i:i + 1])
    cat = jnp.concatenate([a_out, b_out], axis=1)
    x_mid = mm(m2(cat), m3(P["w_out_ab"], i), "nn", "out_ab", add=m2(x))
    return x_mid, dict(x=x, hn=hn, z=z, o=o, lse=lse, cat=cat)


def _bwd_even(i, dx_mid, sv, P, G):
    z = sv["z"]
    d_cat = mm(m2(dx_mid), m3(P["w_out_ab"], i), "nt", "out_ab_dx")
    G["w_out_ab"][i] = mm(m2(sv["cat"]), m2(dx_mid), "tn", "out_ab_dw").reshape(4, 512, D)
    dq, dk, dv, dga = attn_bwd(z, d_cat, sv["o"], sv["lse"])
    dvb, dgb, dpw, dps = pool_bwd(z, d_cat, P["pool_w"][i], P["pool_scale"][i:i + 1])
    G["pool_w"][i] = dpw.reshape(4, 4, 64, 256).transpose(1, 0, 2, 3).reshape(4, 256, 256)
    G["pool_scale"][i] = dps[0]
    d_z = assemble_dz_even((dq, dk, dv, dga, dvb, dgb))
    d_hn = mm(m2(d_z), mcs(P["w_in_ab"], i), "nt", "in_ab_dx")
    G["w_in_ab"][i] = mm(m2(sv["hn"]), m2(d_z), "tn", "in_ab_dw", out=outcs(D, 1536))
    dx, dg = rms_bwd(sv["x"], d_hn, dx_mid, P["norm_ab"][i:i + 1], "rms_ab_bwd")
    G["norm_ab"][i] = dg[0]
    return dx


def _fwd_odd(i, x, P):
    hn = rms_fwd(x, P["norm_cd"][i:i + 1], "rms_cd_fwd")
    z = mm(m2(hn), mcs(P["w_in_cd"], i), "nn", "in_cd")
    bfull = jnp.repeat(P["sgu_b"][i].T, 256, axis=1)
    c_out = sgu_fwd(z, P["sgu_ln_g"][i:i + 1], P["sgu_ln_b"][i:i + 1], P["sgu_w"][i], bfull)
    disc, disc_vjp = jax.vjp(_s5_disc, P["s5_a_re"][i], P["s5_a_im"][i], P["s5_log_dt"][i],
                             P["s5_b_re"][i], P["s5_b_im"][i])
    abr, abi, bbr, bbi = disc
    bbd = jnp.concatenate([_blockdiag(bbr.transpose(0, 2, 1)), _blockdiag(bbi.transpose(0, 2, 1))], axis=1)
    cbd = jnp.concatenate([_blockdiag(P["s5_c_re"][i].transpose(0, 2, 1)),
                           -_blockdiag(P["s5_c_im"][i].transpose(0, 2, 1))], axis=0)
    abr, abi = abr.reshape(1, NS), abi.reshape(1, NS)
    bu = mm(m2(z, 3072, 512), m2(bbd), "nn", "s5_bu")
    h = scan_fwd(bu, abr, abi)
    hc = mm(m2(h), m2(cbd), "nn", "s5_hc")
    dskip = P["s5_d"][i:i + 1]
    ypre, yg = s5_post(hc, z, dskip)
    w12 = jnp.concatenate([P["glu_w1"][i], P["glu_w2"][i]], axis=1)
    t = mm(m2(yg), m2(w12), "nn", "glu_t")
    d_out = glu_fwd(t, z)
    cat = jnp.concatenate([c_out, d_out], axis=1)
    x_mid = mm(m2(cat), m3(P["w_out_cd"], i), "nn", "out_cd", add=m2(x))
    return x_mid, dict(x=x, hn=hn, z=z, bfull=bfull, disc_vjp=disc_vjp, bbd=bbd, cbd=cbd, abr=abr,
                       abi=abi, h=h, ypre=ypre, yg=yg, w12=w12, t=t, cat=cat, dskip=dskip)


def _bwd_odd(i, dx_mid, sv, P, G):
    z = sv["z"]
    d_cat = mm(m2(dx_mid), m3(P["w_out_cd"], i), "nt", "out_cd_dx")
    G["w_out_cd"][i] = mm(m2(sv["cat"]), m2(dx_mid), "tn", "out_cd_dw").reshape(4, 384, D)
    du, dv, dgc, dws, dbs, dlg, dlb = sgu_bwd(z, d_cat, P["sgu_ln_g"][i:i + 1], P["sgu_ln_b"][i:i + 1],
                                               P["sgu_w"][i], sv["bfull"])
    G["sgu_w"][i], G["sgu_b"][i] = dws, dbs[:, :4].T
    G["sgu_ln_g"][i], G["sgu_ln_b"][i] = dlg[0], dlb[0]
    dt1, dt2, dgd = glu_bwd(sv["t"], z, d_cat)
    dt = jnp.concatenate([dt1, dt2], axis=1)
    gw12 = mm(m2(sv["yg"]), m2(dt), "tn", "glu_dw")
    G["glu_w1"][i] = gw12[:, :512].reshape(4, 128, 512)
    G["glu_w2"][i] = gw12[:, 512:].reshape(4, 128, 512)
    dyg = mm(m2(dt), m2(sv["w12"]), "nt", "glu_dx")
    dypre, dxd1, dd = s5_post_bwd(dyg, sv["ypre"], z, sv["dskip"])
    G["s5_d"][i] = dd[0]
    gcbd = mm(m2(sv["h"]), m2(dypre), "tn", "s5_dc")
    G["s5_c_re"][i] = _blocks(gcbd[:NS], NP, NH).transpose(0, 2, 1)
    G["s5_c_im"][i] = -_blocks(gcbd[NS:], NP, NH).transpose(0, 2, 1)
    eta = mm(m2(dypre), m2(sv["cbd"]), "nt", "s5_eta")
    lam, dacc = scan_bwd(eta, sv["h"], sv["abr"], sv["abi"])
    gbbd = mm(m2(z, 3072, 512), m2(lam), "tn", "s5_db")
    dxd = mm(m2(lam), m2(sv["bbd"]), "nt", "s5_dx", add=m2(dxd1))
    dacc = jnp.sum(dacc, axis=0)
    d_bbr = _blocks(gbbd[:, :NS], NH, NP).transpose(0, 2, 1)
    d_bbi = _blocks(gbbd[:, NS:], NH, NP).transpose(0, 2, 1)
    (G["s5_a_re"][i], G["s5_a_im"][i], G["s5_log_dt"][i], G["s5_b_re"][i], G["s5_b_im"][i]) = sv["disc_vjp"](
        (dacc[:NS].reshape(NG, NP), dacc[NS:].reshape(NG, NP), d_bbr, d_bbi))
    d_z = assemble_dz_odd(du, dv, dgc, dxd, dgd)
    d_hn = mm(m2(d_z), mcs(P["w_in_cd"], i), "nt", "in_cd_dx")
    G["w_in_cd"][i] = mm(m2(sv["hn"]), m2(d_z), "tn", "in_cd_dw", out=outcs(D, 1024))
    dx, dg = rms_bwd(sv["x"], d_hn, dx_mid, P["norm_cd"][i:i + 1], "rms_cd_bwd")
    G["norm_cd"][i] = dg[0]
    return dx


def _fwd_x(l, x, mem_n, P):
    hx = rms_fwd(x, P["norm_x"][l:l + 1], "rms_x_fwd")
    q = mm(m2(hx), m3(P["w_xq"], l), "nn", "xq")
    kv = mm(m2(mem_n), mcs(P["w_xkv"], l), "nn", "xkv")
    ox = xattn_fwd(q, kv)
    x_out = mm(m2(ox), m3(P["w_xo"], l), "nn", "xo", add=m2(x))
    return x_out, dict(x=x, hx=hx, q=q, kv=kv, ox=ox)


def _bwd_x(l, dx_out, sv, mem_n, d_memn, P, G):
    d_ox = mm(m2(dx_out), m3(P["w_xo"], l), "nt", "xo_dx")
    G["w_xo"][l] = mm(m2(sv["ox"]), m2(dx_out), "tn", "xo_dw").reshape(4, 256, D)
    dq, dkv = xattn_bwd(sv["q"], sv["kv"], d_ox)
    G["w_xq"][l] = mm(m2(sv["hx"]), m2(dq), "tn", "xq_dw").reshape(4, 256, D)
    d_hx = mm(m2(dq), m3(P["w_xq"], l), "nt", "xq_dx")
    G["w_xkv"][l] = mm(m2(mem_n), m2(dkv), "tn", "xkv_dw", out=outcs(D, 512))
    d_memn = mm(m2(dkv), mcs(P["w_xkv"], l), "nt", "xkv_dx", add=None if d_memn is None else m2(d_memn))
    dx, dg = rms_bwd(sv["x"], d_hx, dx_out, P["norm_x"][l:l + 1], "rms_x_bwd")
    G["norm_x"][l] = dg[0]
    return dx, d_memn


def local_step(x, mem, tgt, P):
    G = {k: [None] * n for k, n in (
        ("norm_ab", 2), ("w_in_ab", 2), ("pool_w", 2), ("pool_scale", 2), ("w_out_ab", 2), ("norm_cd", 2),
        ("w_in_cd", 2), ("sgu_ln_g", 2), ("sgu_ln_b", 2), ("sgu_w", 2), ("sgu_b", 2), ("s5_a_re", 2),
        ("s5_a_im", 2), ("s5_log_dt", 2), ("s5_b_re", 2), ("s5_b_im", 2), ("s5_c_re", 2), ("s5_c_im", 2),
        ("s5_d", 2), ("glu_w1", 2), ("glu_w2", 2), ("w_out_cd", 2), ("norm_x", 4), ("w_xq", 4),
        ("w_xkv", 4), ("w_xo", 4))}
    mem_g = P["mem_norm"].reshape(1, D)
    mem_n = rms_fwd(mem, mem_g, "rms_mem_fwd")
    saved = []
    for layer in range(4):
        i = layer // 2
        x, sv_m = (_fwd_even if layer % 2 == 0 else _fwd_odd)(i, x, P)
        x, sv_x = _fwd_x(layer, x, mem_n, P)
        saved.append((sv_m, sv_x))
    dx, loss, dgf = final_loss(x, tgt, P["final_norm"].reshape(1, D))
    G["final_norm"] = dgf[0]
    d_memn = None
    for layer in reversed(range(4)):
        i = layer // 2
        sv_m, sv_x = saved[layer]
        dx, d_memn = _bwd_x(layer, dx, sv_x, mem_n, d_memn, P, G)
        dx = (_bwd_even if layer % 2 == 0 else _bwd_odd)(i, dx, sv_m, P, G)
    _, dgm = rms_bwd(mem, d_memn, d_memn, mem_g, "rms_mem_bwd")
    G["mem_norm"] = dgm[0]
    return loss, dx, G


ANY = pl.BlockSpec(memory_space=pl.ANY)


def _place():
    x, y, c = lax.axis_index("x"), lax.axis_index("y"), lax.axis_index("c")
    chips = [(1 - x, y), (x, 1 - y), (1 - x, 1 - y)]
    return x, y, c, 2 * x + y, (x, y, 1 - c), chips


def _remote(src, dst, send, recv, k, dev):
    return pltpu.make_async_remote_copy(src_ref=src, dst_ref=dst, send_sem=send.at[k], recv_sem=recv.at[k],
                                        device_id=dev, device_id_type=MESHID)


def allgather_big(shards):
    n = len(shards)

    def body(*refs):
        ins, outs = refs[:n], refs[n:2 * n]
        send, recv = refs[2 * n:]
        x, y, c, jme, sib, chips = _place()

        def half(a, hc):
            lh = shards[a].shape[0] // 2
            return pl.ds(hc * lh, lh)

        first, passed = [], []
        for a in range(n):
            cp = _remote(ins[a], outs[a].at[:, jme], send, recv, a * 7 + 6, sib)
            cp.start()
            first.append(cp)
            for k, chip in enumerate(chips):
                cp = _remote(ins[a].at[half(a, c)], outs[a].at[half(a, c), jme], send, recv, a * 7 + k, (*chip, c))
                cp.start()
                first.append(cp)
        for a in range(n):
            for k, chip in enumerate(chips):
                piece = outs[a].at[half(a, c), 2 * chip[0] + chip[1]]
                _remote(piece, piece, send, recv, a * 7 + k, (*chip, c)).wait_recv()
                fw = _remote(piece, piece, send, recv, a * 7 + 3 + k, sib)
                fw.start()
                passed.append(fw)
        for a in range(n):
            own = outs[a].at[:, jme]
            _remote(own, own, send, recv, a * 7 + 6, sib).wait_recv()
            for k, chip in enumerate(chips):
                piece = outs[a].at[half(a, 1 - c), 2 * chip[0] + chip[1]]
                _remote(piece, piece, send, recv, a * 7 + 3 + k, sib).wait_recv()
        for cp in first + passed:
            cp.wait_send()

    return pl.pallas_call(
        body, name="allgather_big", in_specs=[ANY] * n, out_specs=[ANY] * n,
        out_shape=[jax.ShapeDtypeStruct((s.shape[0], 4) + s.shape[1:], s.dtype) for s in shards],
        scratch_shapes=[pltpu.SemaphoreType.DMA((7 * n,)), pltpu.SemaphoreType.DMA((7 * n,))],
    )(*shards)


def allgather_small(slab):
    def body(in_ref, out_ref, send, recv, lsem):
        x, y, c, jme, sib, chips = _place()
        loc = pltpu.make_async_copy(in_ref, out_ref.at[jme], lsem.at[0])
        loc.start()
        cps = [_remote(in_ref, out_ref.at[jme], send, recv, k, (*chip, c)) for k, chip in enumerate(chips)]
        for cp in cps:
            cp.start()
        for k, chip in enumerate(chips):
            piece = out_ref.at[2 * chip[0] + chip[1]]
            _remote(piece, piece, send, recv, k, (*chip, c)).wait_recv()
        for cp in cps:
            cp.wait_send()
        loc.wait()

    return pl.pallas_call(
        body, name="allgather_small", in_specs=[ANY], out_specs=ANY,
        out_shape=jax.ShapeDtypeStruct((4,) + slab.shape, slab.dtype),
        scratch_shapes=[pltpu.SemaphoreType.DMA((3,)), pltpu.SemaphoreType.DMA((3,)), pltpu.SemaphoreType.DMA((1,))],
    )(slab)


def allreduce_small(v):
    def body(v_ref, o_ref, r0, r1, r2, send, recv):
        x, y, c, jme, sib, chips = _place()
        peers = [sib, (1 - x, y, c), (x, 1 - y, c)]
        o_ref[...] = v_ref[...]
        for k, buf in enumerate((r0, r1, r2)):
            cp = _remote(o_ref, buf, send, recv, k, peers[k])
            cp.start()
            cp.wait()
            o_ref[...] = o_ref[...] + buf[...]

    vm = pl.BlockSpec(memory_space=pltpu.VMEM)
    return pl.pallas_call(
        body, name="allreduce_small", in_specs=[vm], out_specs=vm,
        out_shape=jax.ShapeDtypeStruct(v.shape, v.dtype),
        scratch_shapes=[pltpu.VMEM(v.shape, v.dtype)] * 3 + [pltpu.SemaphoreType.DMA((3,)), pltpu.SemaphoreType.DMA((3,))],
        compiler_params=pltpu.CompilerParams(vmem_limit_bytes=VMEM_LIMIT),
    )(v)


def rs_pair_exchange(gs, layout):
    n, nw = len(gs), len(layout)

    def body(*refs):
        ins, outs = refs[:n], refs[n:n + nw]
        send, recv = refs[n + nw:]
        x, y, c, jme, sib, chips = _place()
        cps = []
        for w, (first, layers) in enumerate(layout):
            for l in range(layers):
                cps.append(_remote(ins[first + l].at[:, 1 - c], outs[w].at[l], send, recv, first + l, sib))
        for cp in cps:
            cp.start()
        for cp in cps:
            cp.wait()

    return pl.pallas_call(
        body, name="rs_pair_exchange", in_specs=[ANY] * n, out_specs=[ANY] * nw,
        out_shape=[jax.ShapeDtypeStruct((layers, 4) + gs[first].shape[2:], F32) for first, layers in layout],
        scratch_shapes=[pltpu.SemaphoreType.DMA((n,)), pltpu.SemaphoreType.DMA((n,))],
    )(*gs)


def rs_pair_sum(g4, got, l, acc, cidx):
    _, _, rh, cols = g4.shape
    tr = rh if rh <= 256 else 256

    def body(c_ref, a_ref, b_ref, *rest):
        o_ref = rest[-1]
        o_ref[...] = (a_ref[...] + b_ref[...]).astype(o_ref.dtype)

    in_specs = [pl.BlockSpec((None, None, tr, cols), lambda j, t, cr: (j, cr[0], t, 0)),
                pl.BlockSpec((None, None, tr, cols), lambda j, t, cr: (l, j, t, 0))]
    args = [cidx, g4, got]
    if acc is not None:
        in_specs.append(ANY)
        args.append(acc)
    return pl.pallas_call(
        body, name="rs_pair_sum",
        grid_spec=pltpu.PrefetchScalarGridSpec(
            num_scalar_prefetch=1, grid=(4, rh // tr), in_specs=in_specs,
            out_specs=pl.BlockSpec((None, None, tr, cols), lambda j, t, cr: (l, j, t, 0))),
        out_shape=jax.ShapeDtypeStruct(got.shape, BF16),
        input_output_aliases={} if acc is None else {3: 0},
        compiler_params=_cparams(("parallel", "parallel")),
    )(*args)


def rs_chip_exchange(ps):
    n = len(ps)

    def body(*refs):
        ins, outs = refs[:n], refs[n:2 * n]
        send, recv = refs[2 * n:]
        x, y, c, jme, sib, chips = _place()
        cps = []
        for a in range(n):
            for k, chip in enumerate(chips):
                cp = _remote(ins[a].at[:, 2 * chip[0] + chip[1]], outs[a].at[:, jme], send, recv, a * 3 + k, (*chip, c))
                cp.start()
                cps.append(cp)
        for a in range(n):
            for k, chip in enumerate(chips):
                slot = outs[a].at[:, 2 * chip[0] + chip[1]]
                _remote(slot, slot, send, recv, a * 3 + k, (*chip, c)).wait_recv()
        for cp in cps:
            cp.wait_send()

    return pl.pallas_call(
        body, name="rs_chip_exchange", in_specs=[ANY] * n, out_specs=[ANY] * n,
        out_shape=[jax.ShapeDtypeStruct(p.shape, p.dtype) for p in ps],
        scratch_shapes=[pltpu.SemaphoreType.DMA((3 * n,)), pltpu.SemaphoreType.DMA((3 * n,))],
    )(*ps)


def rs_chip_sum(q, p, jc):
    layers, _, rh, cols = q.shape
    tr = rh if rh <= 256 else 256

    def body(jc_ref, q_ref, p_ref, o_ref):
        jme = jc_ref[0]
        own = p_ref[...].astype(F32)
        v = [jnp.where(jme == j, own, q_ref[j].astype(F32)) for j in range(4)]
        o_ref[...] = ((v[0] + v[1]) + v[2]) + v[3]

    return pl.pallas_call(
        body, name="rs_chip_sum",
        grid_spec=pltpu.PrefetchScalarGridSpec(
            num_scalar_prefetch=1, grid=(layers, rh // tr),
            in_specs=[pl.BlockSpec((None, 4, tr, cols), lambda l, t, jr: (l, 0, t, 0)),
                      pl.BlockSpec((None, None, tr, cols), lambda l, t, jr: (l, jr[0], t, 0))],
            out_specs=pl.BlockSpec((None, None, tr, cols), lambda l, t, jr: (l, jr[1], t, 0))),
        out_shape=jax.ShapeDtypeStruct((layers, 2, rh, cols), F32),
        compiler_params=_cparams(("parallel", "parallel")),
    )(jc, q, p)


def rs_pair_gather(rs):
    n = len(rs)

    def body(*refs):
        outs = refs[n:2 * n]
        send, recv = refs[2 * n:]
        x, y, c, jme, sib, chips = _place()
        cps = [_remote(outs[a].at[:, c], outs[a].at[:, c], send, recv, a, sib) for a in range(n)]
        for cp in cps:
            cp.start()
        for a in range(n):
            slot = outs[a].at[:, 1 - c]
            _remote(slot, slot, send, recv, a, sib).wait_recv()
        for cp in cps:
            cp.wait_send()

    return pl.pallas_call(
        body, name="rs_pair_gather", in_specs=[ANY] * n, out_specs=[ANY] * n,
        out_shape=[jax.ShapeDtypeStruct(r.shape, r.dtype) for r in rs],
        input_output_aliases={a: a for a in range(n)},
        scratch_shapes=[pltpu.SemaphoreType.DMA((n,)), pltpu.SemaphoreType.DMA((n,))],
    )(*rs)


def _adamw_math(w, g, m, v):
    m = B1 * m + (1.0 - B1) * g
    v = B2 * v + (1.0 - B2) * (g * g)
    m_hat = m / (1.0 - B1 ** STEP)
    v_hat = v / (1.0 - B2 ** STEP)
    return -LR * (m_hat / (jnp.sqrt(v_hat) + AEPS) + WD * w), m, v


def adamw(w, g, m, v, name):
    rows, cols = w.shape
    tr = 256 if rows % 256 == 0 else rows
    return rw(_adamw_math, [(a, 0, cols) for a in (w, g, m, v)], [(cols, F32)] * 3, name, rows, tr=tr)


WEIGHTS = ["norm_ab", "w_in_ab", "pool_w", "pool_scale", "w_out_ab", "norm_cd", "w_in_cd", "sgu_ln_g", "sgu_ln_b",
           "sgu_w", "sgu_b", "s5_a_re", "s5_a_im", "s5_log_dt", "s5_b_re", "s5_b_im", "s5_c_re", "s5_c_im", "s5_d",
           "glu_w1", "glu_w2", "w_out_cd", "norm_x", "w_xq", "w_xkv", "w_xo", "mem_norm", "final_norm"]
INPUTS = ["x", "mem"] + WEIGHTS + ["loss_target"] + ["m_" + n for n in WEIGHTS] + ["v_" + n for n in WEIGHTS]
BIG = ["w_in_ab", "w_out_ab", "w_in_cd", "w_out_cd", "w_xq", "w_xkv", "w_xo", "glu_w1", "glu_w2", "pool_w"]
COL_SHARDED = ("w_in_ab", "w_in_cd", "w_xkv")
SMALL = [n for n in WEIGHTS if n not in BIG]
SMALL_SHARDED = {"norm_cd": 256, "sgu_ln_g": 256, "sgu_ln_b": 256, "s5_d": 128}
PACK = 256 * 128


def _pack(arrs):
    flat = jnp.concatenate([a.reshape(-1) for a in arrs])
    pad = (-flat.shape[0]) % PACK
    return jnp.concatenate([flat, jnp.zeros((pad,), flat.dtype)]).reshape(-1, 128)


def _unpack(packed, shapes):
    flat, out, off = packed.reshape(-1), [], 0
    for s in shapes:
        n = 1
        for d in s:
            n *= d
        out.append(flat[off:off + n].reshape(s))
        off += n
    return out


def _mat(a):
    return a.reshape(a.shape[0], -1, a.shape[-1])


def kernel(*args):
    a = dict(zip(INPUTS, args))
    x_i, y_i, c_i = lax.axis_index("x"), lax.axis_index("y"), lax.axis_index("c")
    j = 2 * x_i + y_i

    gathered = dict(zip(BIG, allgather_big([_mat(a[n]).astype(BF16) for n in BIG])))
    slab = jnp.concatenate([a["norm_cd"], a["sgu_ln_g"], a["sgu_ln_b"],
                            jnp.pad(a["s5_d"], ((0, 0), (0, 128)))], axis=0)
    gslab = allgather_small(slab)
    P = {n: a[n] for n in SMALL}
    for k, n in enumerate(("norm_cd", "sgu_ln_g", "sgu_ln_b", "s5_d")):
        wd = SMALL_SHARDED[n]
        P[n] = gslab[:, 2 * k:2 * k + 2, :wd].transpose(1, 0, 2).reshape(2, 4 * wd)
    for n in BIG:
        g = gathered[n]
        if n in COL_SHARDED:
            P[n] = g
        elif n == "pool_w":
            P[n] = g.reshape(2, 4, 4, 64, 256).transpose(0, 2, 1, 3, 4).reshape(2, 4, 256, 256)
        else:
            P[n] = g.reshape(g.shape[0], 4 * g.shape[2], g.shape[3])

    loss, dx, G = local_step(a["x"][0], a["mem"][0], a["loss_target"][0], P)
    loss = lax.psum(loss[0, 0], ("x", "y", "c"))

    layout, flat = [], []
    for n in BIG:
        layout.append((len(flat), len(G[n])))
        flat += [g.reshape(4, 2, g.shape[1] // 2, g.shape[2]) for g in G[n]]
    got = rs_pair_exchange(flat, layout)
    cidx = jnp.reshape(c_i, (1,)).astype(jnp.int32)
    jc = jnp.stack([j, c_i]).astype(jnp.int32)
    pair = []
    for (first, layers), gw in zip(layout, got):
        acc = None
        for l in range(layers):
            acc = rs_pair_sum(flat[first + l], gw, l, acc, cidx)
        pair.append(acc)
    red = [rs_chip_sum(q, p, jc) for q, p in zip(rs_chip_exchange(pair), pair)]
    gbig = dict(zip(BIG, rs_pair_gather(red)))

    outs = {}
    for n in BIG:
        shp = a[n].shape
        g2 = gbig[n].reshape(-1, shp[-1])
        d2, m2_, v2_ = adamw(a[n].reshape(g2.shape), g2, a["m_" + n].reshape(g2.shape),
                             a["v_" + n].reshape(g2.shape), "adamw_" + n)
        outs[n] = tuple(t.reshape(shp) for t in (g2, d2, m2_, v2_))

    gfull = [jnp.stack(G[n]) if isinstance(G[n], list) else G[n] for n in SMALL]
    shapes = [g.shape for g in gfull]
    gsum = _unpack(allreduce_small(_pack(gfull)), shapes)
    gloc = []
    for n, g in zip(SMALL, gsum):
        if n in SMALL_SHARDED:
            g = lax.dynamic_slice_in_dim(g, j * SMALL_SHARDED[n], SMALL_SHARDED[n], axis=1)
        gloc.append(g)
    lshapes = [a[n].shape for n in SMALL]
    packed = [_pack(t) for t in ([a[n] for n in SMALL], gloc, [a["m_" + n] for n in SMALL], [a["v_" + n] for n in SMALL])]
    small = [_unpack(t, lshapes) for t in adamw(*packed, "adamw_small")]
    for k, n in enumerate(SMALL):
        outs[n] = (gloc[k], small[0][k], small[1][k], small[2][k])

    res = [loss, dx[None]]
    for part in range(4):
        res += [outs[n][part] for n in WEIGHTS]
    return tuple(res)
```

```python
import math

import jax
import jax.numpy as jnp
from jax import lax
from jax.experimental import pallas as pl
from jax.experimental.pallas import tpu as pltpu

F32, BF16 = jnp.float32, jnp.bfloat16
S, D = 2048, 1024
MEM = 256
EPS = 1e-6
NEG = -1e30
QB = 128
PATTERNS = (1, 4, 16)
NG, NP, NH = 32, 64, 16
NS = NG * NP
LR, B1, B2, AEPS, WD, STEP = 0.001, 0.9, 0.999, 1e-08, 0.01, 10
MESHID = pl.DeviceIdType.MESH
VMEM_LIMIT = 56 * 1024 * 1024


def _cparams(sem):
    return pltpu.CompilerParams(dimension_semantics=sem, vmem_limit_bytes=VMEM_LIMIT)


def _sig(x):
    return 1.0 / (1.0 + jnp.exp(-x))


def _dot(a, b, dims):
    return lax.dot_general(a, b, (dims, ((), ())), preferred_element_type=F32)


def _nn(a, b):
    return _dot(a, b, ((1,), (0,)))


def _nt(a, b):
    return _dot(a, b, ((1,), (1,)))


def _tn(a, b):
    return _dot(a, b, ((0,), (0,)))


_DIMS = {"nn": ((1,), (0,)), "nt": ((1,), (1,)), "tn": ((0,), (0,))}


def _tile(dim, cc=None, cap=1024):
    for t in (2048, 1536, 1024, 768, 512, 384, 256, 128):
        if t <= cap and dim % t == 0 and (cc is None or cc % t == 0):
            return t
    return dim


MM_VMEM = 36 * 1024 * 1024


def _mm_tiles(m, n, k, ccm, ccn, cck, a_bytes, b_bytes, o_bytes):
    caps = [1024, 1024, 2048]
    while True:
        tm, tn, tk = _tile(m, ccm, caps[0]), _tile(n, ccn, caps[1]), _tile(k, cck, caps[2])
        need = 2 * (tm * tk * a_bytes + tk * tn * b_bytes + tm * tn * o_bytes) + (tm * tn * 4 if tk < k else 0)
        if need <= MM_VMEM:
            return tm, tn, tk
        if tk > 1024:
            caps[2] = tk // 2
        elif tn >= tm:
            caps[1] = tn // 2
        else:
            caps[0] = tm // 2


def m2(arr, col_off=0, ncols=None):
    rows, cols = arr.shape
    ncols = cols - col_off if ncols is None else ncols

    def spec(tr, tc, rc):
        assert col_off % tc == 0
        return pl.BlockSpec((tr, tc), lambda *g: (rc(*g)[0], rc(*g)[1] + col_off // tc))
    return (arr, rows, ncols, spec, None if col_off == 0 else col_off)


def m3(arr, i):
    def spec(tr, tc, rc):
        return pl.BlockSpec((None, tr, tc), lambda *g: (i,) + tuple(rc(*g)))
    return (arr, arr.shape[1], arr.shape[2], spec, None)


def mcs(arr, i):
    cs = arr.shape[3]

    def spec(tr, tc, rc):
        n = cs // tc
        return pl.BlockSpec((None, None, tr, tc),
                            lambda *g: (i, rc(*g)[1] // n, rc(*g)[0], rc(*g)[1] % n))
    return (arr, arr.shape[2], 4 * cs, spec, cs)


def out2(rows, cols):
    def spec(tr, tc, rc):
        return pl.BlockSpec((tr, tc), lambda *g: tuple(rc(*g)))
    return ((rows, cols), spec, None)


def outcs(rows, cs):
    def spec(tr, tc, rc):
        n = cs // tc
        return pl.BlockSpec((None, tr, tc), lambda *g: (rc(*g)[1] // n, rc(*g)[0], rc(*g)[1] % n))
    return ((4, rows, cs), spec, cs)


def _both(a, b):
    if a is None:
        return b
    if b is None:
        return a
    return math.gcd(a, b)


def mm(a, b, mode, name, add=None, out=None, out_dtype=F32):
    a_arr, a_r, a_c, a_spec, a_cc = a
    b_arr, b_r, b_c, b_spec, b_cc = b
    if mode == "nn":
        m, k, n = a_r, a_c, b_c
        assert b_r == k
        ccm, cck, ccn = None, a_cc, b_cc
    elif mode == "nt":
        m, k, n = a_r, a_c, b_r
        assert b_c == k
        ccm, cck, ccn = None, _both(a_cc, b_cc), None
    else:
        m, k, n = a_c, a_r, b_c
        assert b_r == k
        ccm, cck, ccn = a_cc, None, b_cc
    out = out2(m, n) if out is None else out
    o_shape, o_spec, o_cc = out
    ccn = _both(ccn, o_cc)
    if add is not None:
        ccn = _both(ccn, add[4])
    o_bytes = jnp.dtype(out_dtype).itemsize + (0 if add is None else add[0].dtype.itemsize)
    tm, tn, tk = _mm_tiles(m, n, k, ccm, ccn, cck, a_arr.dtype.itemsize, b_arr.dtype.itemsize, o_bytes)
    nk = k // tk
    if mode == "nn":
        in_specs = [a_spec(tm, tk, lambda i, j, kk: (i, kk)), b_spec(tk, tn, lambda i, j, kk: (kk, j))]
    elif mode == "nt":
        in_specs = [a_spec(tm, tk, lambda i, j, kk: (i, kk)), b_spec(tn, tk, lambda i, j, kk: (j, kk))]
    else:
        in_specs = [a_spec(tk, tm, lambda i, j, kk: (kk, i)), b_spec(tk, tn, lambda i, j, kk: (kk, j))]
    args = [a_arr, b_arr]
    if add is not None:
        in_specs.append(add[3](tm, tn, lambda i, j, kk: (i, j)))
        args.append(add[0])
    dims = _DIMS[mode]
    has_add = add is not None

    def body(*refs):
        a_ref, b_ref = refs[0], refs[1]
        add_ref = refs[2] if has_add else None
        prod = _dot(a_ref[...].astype(BF16), b_ref[...].astype(BF16), dims)
        if nk == 1:
            o_ref = refs[-1]
            if has_add:
                prod = prod + add_ref[...].astype(F32)
            o_ref[...] = prod.astype(o_ref.dtype)
            return
        o_ref, acc = refs[-2], refs[-1]
        kk = pl.program_id(2)

        @pl.when(kk == 0)
        def _():
            acc[...] = prod

        @pl.when(kk > 0)
        def _():
            acc[...] += prod

        @pl.when(kk == nk - 1)
        def _():
            r = acc[...]
            if has_add:
                r = r + add_ref[...].astype(F32)
            o_ref[...] = r.astype(o_ref.dtype)

    return pl.pallas_call(
        body, name=name, grid=(m // tm, n // tn, nk), in_specs=in_specs,
        out_specs=o_spec(tm, tn, lambda i, j, kk: (i, j)),
        out_shape=jax.ShapeDtypeStruct(o_shape, out_dtype),
        scratch_shapes=[pltpu.VMEM((tm, tn), F32)] if nk > 1 else [],
        compiler_params=_cparams(("parallel", "parallel", "arbitrary")),
    )(*args)


def rw(fn, ins, outs, name, rows, tr=256, consts=(), accs=()):
    n_in, n_c, n_o, n_a = len(ins), len(consts), len(outs), len(accs)
    in_specs = []
    for arr, off, width in ins:
        assert off % width == 0
        in_specs.append(pl.BlockSpec((tr, width), lambda i, o=off // width: (i, o)))
    for c in consts:
        in_specs.append(pl.BlockSpec(c.shape, lambda i: (0, 0)))
    out_specs = [pl.BlockSpec((tr, w), lambda i: (i, 0)) for w, _ in outs]
    out_specs += [pl.BlockSpec(s, lambda i: (0, 0)) for s in accs]
    out_shape = [jax.ShapeDtypeStruct((rows, w), dt) for w, dt in outs]
    out_shape += [jax.ShapeDtypeStruct(s, F32) for s in accs]

    def body(*refs):
        vals = [r[...] for r in refs[:n_in + n_c]]
        o_refs = refs[n_in + n_c:n_in + n_c + n_o]
        a_refs = refs[n_in + n_c + n_o:]
        res = fn(*vals)
        for r, v in zip(o_refs, res[:n_o]):
            r[...] = v.astype(r.dtype)
        if n_a:
            @pl.when(pl.program_id(0) == 0)
            def _():
                for r in a_refs:
                    r[...] = jnp.zeros_like(r)
            for r, v in zip(a_refs, res[n_o:]):
                r[...] += v

    res = pl.pallas_call(
        body, name=name, grid=(rows // tr,), in_specs=in_specs, out_specs=out_specs,
        out_shape=out_shape,
        compiler_params=_cparams(("arbitrary",) if n_a else ("parallel",)),
    )(*[a for a, _, _ in ins], *consts)
    return res


def _rstd(x):
    return lax.rsqrt(jnp.mean(x * x, axis=-1, keepdims=True) + EPS)


def rms_fwd(x, g, name):
    def fn(xv, gv):
        xv = xv.astype(F32)
        return (xv * _rstd(xv) * gv,)
    return rw(fn, [(x, 0, D)], [(D, BF16)], name, x.shape[0], consts=[g])[0]


def _rms_bwd_math(xv, dy, gv):
    r = _rstd(xv)
    dyg = dy * gv
    dx = r * dyg - xv * (r * r * r / D) * jnp.sum(dyg * xv, axis=-1, keepdims=True)
    dg = jnp.sum(dy * xv * r, axis=0, keepdims=True)
    return dx, dg


def rms_bwd(x, dy, dres, g, name):
    def fn(xv, dyv, drv, gv):
        dx, dg = _rms_bwd_math(xv, dyv, gv)
        return dx + drv, dg
    return rw(fn, [(x, 0, D), (dy, 0, D), (dres, 0, D)], [(D, F32)], name, x.shape[0],
              consts=[g], accs=[(1, D)])


def final_loss(x, tgt, g):
    def fn(xv, tv, gv):
        e = xv * _rstd(xv) * gv - tv
        loss = 0.5 * jnp.sum(jnp.sum(e * e, axis=-1, keepdims=True), axis=0, keepdims=True) / D
        dx, dg = _rms_bwd_math(xv, e / D, gv)
        return dx, loss, dg
    return rw(fn, [(x, 0, D), (tgt, 0, D)], [(D, F32)], "final_loss", S, consts=[g],
              accs=[(1, 1), (1, D)])


def _attn_bias(bias_ref):
    ii = lax.broadcasted_iota(jnp.int32, (2 * QB, 2 * QB), 0) % QB
    jj = lax.broadcasted_iota(jnp.int32, (2 * QB, 2 * QB), 1)
    dist = ii + QB - jj
    band = (dist >= 0) & (dist <= QB)
    bias_ref[1] = jnp.where(band, 0.0, NEG)
    bias_ref[0] = jnp.where(band & (jj >= QB), 0.0, NEG)


def _two_heads(x, m0):
    return jnp.concatenate([jnp.where(m0, x, 0.0), jnp.where(m0, 0.0, x)], axis=0)


def _per_head(col, m0):
    return jnp.where(m0, col[:QB], col[QB:])


def _attn_rows(idx, d):
    if d == 1:
        b = idx
        cur = pl.ds(pl.multiple_of(b * QB, QB), QB)
        prev = pl.ds(pl.multiple_of(jnp.maximum(b - 1, 0) * QB, QB), QB)
    else:
        r, b = lax.rem(idx, d), lax.div(idx, d)
        cur = pl.ds(r + b * (QB * d), QB, stride=d)
        prev = pl.ds(r + jnp.maximum(b - 1, 0) * (QB * d), QB, stride=d)
    return cur, prev, b


NBLK = S // QB


def _colblk(off):
    return pl.BlockSpec((S, 128), lambda hp: (0, off * 8 + hp))


def attn_fwd(z):
    def body(q_ref, k_ref, v_ref, g_ref, o_ref, l_ref, a_ref, os, ls, bias):
        _attn_bias(bias)
        m0 = lax.broadcasted_iota(jnp.int32, (1, 128), 1) < 64
        for pi, d in enumerate(PATTERNS):
            def step(idx, carry, pi=pi, d=d):
                cur, prev, b = _attn_rows(idx, d)
                qq = _two_heads(q_ref[cur, :] * 0.125, m0).astype(BF16)
                k = jnp.concatenate([k_ref[prev, :], k_ref[cur, :]], axis=0).astype(BF16)
                v = jnp.concatenate([v_ref[prev, :], v_ref[cur, :]], axis=0)
                s = _nt(qq, k) + bias[jnp.minimum(b, 1)]
                mx = jnp.max(s, axis=-1, keepdims=True)
                p = jnp.exp(s - mx)
                den = jnp.sum(p, axis=-1, keepdims=True)
                pb = p.astype(BF16)
                o = _nn(jnp.concatenate([pb[:QB], pb[QB:]], axis=1), _two_heads(v, m0).astype(BF16))
                os[pi, cur, :] = o * _per_head(1.0 / den, m0)
                ls[pi, cur, :] = _per_head(mx + jnp.log(den), m0)
                return carry
            lax.fori_loop(0, NBLK, step, 0, unroll=2)
        l1, l2, l3 = ls[0], ls[1], ls[2]
        mx = jnp.maximum(jnp.maximum(l1, l2), l3)
        e1, e2, e3 = jnp.exp(l1 - mx), jnp.exp(l2 - mx), jnp.exp(l3 - mx)
        tot = e1 + e2 + e3
        o = (os[0] * e1 + os[1] * e2 + os[2] * e3) / tot
        ga = g_ref[...]
        o_ref[...] = o
        l_ref[...] = mx + jnp.log(tot)
        a_ref[...] = (o * (ga * _sig(ga))).astype(a_ref.dtype)

    out = pl.BlockSpec((S, 128), lambda hp: (0, hp))
    return pl.pallas_call(
        body, name="attn_fwd", grid=(8,),
        in_specs=[_colblk(0), _colblk(1), _colblk(2), _colblk(3)], out_specs=[out] * 3,
        out_shape=[jax.ShapeDtypeStruct((S, D), F32), jax.ShapeDtypeStruct((S, D), F32),
                   jax.ShapeDtypeStruct((S, 2 * D), BF16)],
        scratch_shapes=[pltpu.VMEM((3, S, 128), F32), pltpu.VMEM((3, S, 128), F32),
                        pltpu.VMEM((2, 2 * QB, 2 * QB), F32)],
        compiler_params=_cparams(("parallel",)),
    )(z, z, z, z)


def attn_bwd(z, d_cat, o, lse):
    def body(q_ref, k_ref, v_ref, g_ref, da_ref, o_ref, l_ref, dq_ref, dk_ref, dv_ref, dg_ref, do_s, pr_s, bias):
        _attn_bias(bias)
        m0 = lax.broadcasted_iota(jnp.int32, (1, 128), 1) < 64
        ga = g_ref[...]
        sg = _sig(ga)
        da = da_ref[...]
        ov = o_ref[...]
        do = da * (ga * sg)
        dg_ref[...] = da * ov * (sg * (1.0 + ga * (1.0 - sg)))
        do_s[...] = do
        pr_s[...] = do * ov
        dq_ref[...] = jnp.zeros_like(dq_ref)
        dk_ref[...] = jnp.zeros_like(dk_ref)
        dv_ref[...] = jnp.zeros_like(dv_ref)
        for d in PATTERNS:
            def step(idx, carry, d=d):
                cur, prev, b = _attn_rows(idx, d)
                qq = _two_heads(q_ref[cur, :] * 0.125, m0).astype(BF16)
                kf = jnp.concatenate([k_ref[prev, :], k_ref[cur, :]], axis=0)
                k = kf.astype(BF16)
                v = jnp.concatenate([v_ref[prev, :], v_ref[cur, :]], axis=0).astype(BF16)
                dd = _two_heads(do_s[cur, :], m0).astype(BF16)
                lp = l_ref[cur, :]
                lh = jnp.max(jnp.concatenate([jnp.where(m0, lp, -jnp.inf), jnp.where(m0, -jnp.inf, lp)], axis=0),
                             axis=-1, keepdims=True)
                delta = jnp.sum(_two_heads(pr_s[cur, :], m0), axis=-1, keepdims=True)
                p = jnp.exp(_nt(qq, k) + bias[jnp.minimum(b, 1)] - lh)
                ds = (p * (_nt(dd, v) - delta)).astype(BF16)
                dq = _nn(jnp.concatenate([ds[:QB], ds[QB:]], axis=1), _two_heads(kf, m0).astype(BF16))
                dk = _tn(ds, qq)
                dv = _tn(p.astype(BF16), dd)
                dq_ref[cur, :] = dq_ref[cur, :] + dq * 0.125
                dk_ref[prev, :] = dk_ref[prev, :] + dk[:QB]
                dv_ref[prev, :] = dv_ref[prev, :] + dv[:QB]
                dk_ref[cur, :] = dk_ref[cur, :] + dk[QB:]
                dv_ref[cur, :] = dv_ref[cur, :] + dv[QB:]
                return carry
            lax.fori_loop(0, NBLK, step, 0, unroll=2)

    blk = pl.BlockSpec((S, 128), lambda hp: (0, hp))
    return pl.pallas_call(
        body, name="attn_bwd", grid=(8,),
        in_specs=[_colblk(0), _colblk(1), _colblk(2), _colblk(3), blk, blk, blk], out_specs=[blk] * 4,
        out_shape=[jax.ShapeDtypeStruct((S, D), F32)] * 4,
        scratch_shapes=[pltpu.VMEM((S, 128), F32), pltpu.VMEM((S, 128), F32), pltpu.VMEM((2, 2 * QB, 2 * QB), F32)],
        compiler_params=_cparams(("parallel",)),
    )(z, z, z, z, d_cat, o, lse)


def assemble_dz_even(parts):
    def body(*refs):
        o_ref = refs[-1]
        for j in range(6):
            o_ref[:, j * D:(j + 1) * D] = refs[j][...].astype(o_ref.dtype)
    tr = 256
    blk = pl.BlockSpec((tr, D), lambda i: (i, 0))
    return pl.pallas_call(
        body, name="assemble_dz_even", grid=(S // tr,), in_specs=[blk] * 6,
        out_specs=pl.BlockSpec((tr, 6 * D), lambda i: (i, 0)),
        out_shape=jax.ShapeDtypeStruct((S, 6 * D), BF16),
        compiler_params=_cparams(("parallel",)),
    )(*parts)


def _pool_window(g):
    return jnp.where(g == 0, 2.0, jnp.where(g == 1, 4.0, jnp.where(g == 2, 8.0, 16.0)))


def _pool_sel(g, levels):
    return jnp.where(g == 0, levels[0], jnp.where(g == 1, levels[1], jnp.where(g == 2, levels[2], levels[3])))


def _pool_fwd_math(v, g):
    t = lax.broadcasted_iota(jnp.int32, (S, 1), 0)
    s = v
    levels = []
    for k in (1, 2, 4, 8):
        s = s + jnp.where(t >= k, pltpu.roll(s, k, 0), 0.0)
        levels.append(s)
    cnt = jnp.minimum((t + 1).astype(F32), _pool_window(g))
    return _pool_sel(g, levels) / cnt - v, cnt


def pool_fwd(z, pw, ps, cat):
    def body(v_ref, g_ref, pw_ref, ps_ref, cat_ref, o_ref):
        g = pl.program_id(0)
        pooled, _ = _pool_fwd_math(v_ref[...], g)
        mixed = _nn(pooled.astype(BF16), pw_ref[...].astype(BF16))
        gb = g_ref[...]
        o_ref[...] = (mixed * ps_ref[...] * (gb * _sig(gb))).astype(o_ref.dtype)

    return pl.pallas_call(
        body, name="pool_fwd", grid=(4,),
        in_specs=[pl.BlockSpec((S, 256), lambda g: (0, 16 + g)),
                  pl.BlockSpec((S, 256), lambda g: (0, 20 + g)),
                  pl.BlockSpec((None, 256, 256), lambda g: (g, 0, 0)),
                  pl.BlockSpec((1, 256), lambda g: (0, g)), pl.BlockSpec(memory_space=pl.ANY)],
        out_specs=pl.BlockSpec((S, 256), lambda g: (0, 4 + g)),
        out_shape=jax.ShapeDtypeStruct((S, 2 * D), BF16),
        input_output_aliases={4: 0},
        compiler_params=_cparams(("parallel",)),
    )(z, z, pw, ps, cat)


def pool_bwd(z, d_cat, pw, ps):
    def body(v_ref, g_ref, d_ref, pw_ref, ps_ref, dv_ref, dg_ref, dpw_ref, dps_ref):
        g = pl.program_id(0)
        v = v_ref[...]
        pooled, cnt = _pool_fwd_math(v, g)
        pwb = pw_ref[...].astype(BF16)
        pb = pooled.astype(BF16)
        mixed = _nn(pb, pwb)
        gb = g_ref[...]
        sg = _sig(gb)
        dout = d_ref[...]
        sc = ps_ref[...]
        dg_ref[...] = dout * mixed * sc * (sg * (1.0 + gb * (1.0 - sg)))
        dms = dout * (gb * sg)
        dps_ref[...] = jnp.sum(dms * mixed, axis=0, keepdims=True)
        dmx = (dms * sc).astype(BF16)
        dpw_ref[...] = _tn(pb, dmx)
        dpooled = _nt(dmx, pwb)
        t = lax.broadcasted_iota(jnp.int32, (S, 1), 0)
        s = dpooled / cnt
        levels = []
        for k in (1, 2, 4, 8):
            s = s + jnp.where(t < S - k, pltpu.roll(s, S - k, 0), 0.0)
            levels.append(s)
        dv_ref[...] = _pool_sel(g, levels) - dpooled

    return pl.pallas_call(
        body, name="pool_bwd", grid=(4,),
        in_specs=[pl.BlockSpec((S, 256), lambda g: (0, 16 + g)),
                  pl.BlockSpec((S, 256), lambda g: (0, 20 + g)),
                  pl.BlockSpec((S, 256), lambda g: (0, 4 + g)),
                  pl.BlockSpec((None, 256, 256), lambda g: (g, 0, 0)),
                  pl.BlockSpec((1, 256), lambda g: (0, g))],
        out_specs=[pl.BlockSpec((S, 256), lambda g: (0, g)),
                   pl.BlockSpec((S, 256), lambda g: (0, g)),
                   pl.BlockSpec((None, 256, 256), lambda g: (g, 0, 0)),
                   pl.BlockSpec((1, 256), lambda g: (0, g))],
        out_shape=[jax.ShapeDtypeStruct((S, D), F32), jax.ShapeDtypeStruct((S, D), F32),
                   jax.ShapeDtypeStruct((4, 256, 256), F32), jax.ShapeDtypeStruct((1, D), F32)],
        compiler_params=_cparams(("parallel",)),
    )(z, z, d_cat, pw, ps)


CH = 128


def _sgu_common(v, lng, lnb, w_ref):
    mu = jnp.mean(v, axis=-1, keepdims=True)
    vc = v - mu
    rs = lax.rsqrt(jnp.mean(vc * vc, axis=-1, keepdims=True) + EPS)
    xhat = vc * rs
    vn = (xhat * lng + lnb).astype(BF16)
    ri = lax.broadcasted_iota(jnp.int32, (CH, CH), 0)
    ci = lax.broadcasted_iota(jnp.int32, (CH, CH), 1)
    tril = ri >= ci
    ws = [jnp.where(tril, w_ref[g], 0.0).astype(BF16) for g in range(4)]
    return xhat, rs, vn, tril, ws


def _zspec(off):
    return pl.BlockSpec((CH, D), lambda c: (c, off))


def _full(shape):
    return pl.BlockSpec(shape, lambda c: (0,) * len(shape))


def sgu_fwd(z, lng, lnb, w, bfull):
    def body(u_ref, v_ref, g_ref, lng_ref, lnb_ref, w_ref, b_ref, o_ref):
        _, _, vn, _, ws = _sgu_common(v_ref[...], lng_ref[...], lnb_ref[...], w_ref)
        for g in range(4):
            sl = slice(g * 256, (g + 1) * 256)
            mixed = _nn(ws[g], vn[:, sl]) + b_ref[:, sl]
            gc = g_ref[:, sl]
            o_ref[:, sl] = (u_ref[:, sl] * mixed * (gc * _sig(gc))).astype(o_ref.dtype)

    return pl.pallas_call(
        body, name="sgu_fwd", grid=(S // CH,),
        in_specs=[_zspec(0), _zspec(1), _zspec(2), _full((1, D)), _full((1, D)),
                  _full((4, CH, CH)), _full((CH, D))],
        out_specs=pl.BlockSpec((CH, D), lambda c: (c, 0)),
        out_shape=jax.ShapeDtypeStruct((S, D), BF16),
        compiler_params=_cparams(("parallel",)),
    )(z, z, z, lng, lnb, w, bfull)


def sgu_bwd(z, d_cat, lng, lnb, w, bfull):
    def body(u_ref, v_ref, g_ref, d_ref, lng_ref, lnb_ref, w_ref, b_ref,
             du_ref, dv_ref, dg_ref, dw_ref, db_ref, dlg_ref, dlb_ref):
        @pl.when(pl.program_id(0) == 0)
        def _():
            dw_ref[...] = jnp.zeros_like(dw_ref)
            db_ref[...] = jnp.zeros_like(db_ref)
            dlg_ref[...] = jnp.zeros_like(dlg_ref)
            dlb_ref[...] = jnp.zeros_like(dlb_ref)

        lng = lng_ref[...]
        xhat, rs, vn, tril, ws = _sgu_common(v_ref[...], lng, lnb_ref[...], w_ref)
        lane = lax.broadcasted_iota(jnp.int32, (1, 128), 1)
        db = jnp.zeros((CH, 128), F32)
        dvn_parts = []
        for g in range(4):
            sl = slice(g * 256, (g + 1) * 256)
            mixed = _nn(ws[g], vn[:, sl]) + b_ref[:, sl]
            gc = g_ref[:, sl]
            sg = _sig(gc)
            u = u_ref[:, sl]
            dc = d_ref[:, sl]
            du_ref[:, sl] = dc * mixed * (gc * sg)
            dg_ref[:, sl] = dc * u * mixed * (sg * (1.0 + gc * (1.0 - sg)))
            dmx = dc * u * (gc * sg)
            db = db + jnp.where(lane == g, jnp.sum(dmx, axis=-1, keepdims=True), 0.0)
            dmb = dmx.astype(BF16)
            dw_ref[g] += jnp.where(tril, _nt(dmb, vn[:, sl]), 0.0)
            dvn_parts.append(_tn(ws[g], dmb))
        db_ref[...] += db
        dvn = jnp.concatenate(dvn_parts, axis=1)
        dlb_ref[...] += jnp.sum(dvn, axis=0, keepdims=True)
        dlg_ref[...] += jnp.sum(dvn * xhat, axis=0, keepdims=True)
        dxh = dvn * lng
        dv_ref[...] = rs * (dxh - jnp.mean(dxh, axis=-1, keepdims=True)
                            - xhat * jnp.mean(dxh * xhat, axis=-1, keepdims=True))

    row = pl.BlockSpec((CH, D), lambda c: (c, 0))
    return pl.pallas_call(
        body, name="sgu_bwd", grid=(S // CH,),
        in_specs=[_zspec(0), _zspec(1), _zspec(2), row, _full((1, D)), _full((1, D)),
                  _full((4, CH, CH)), _full((CH, D))],
        out_specs=[row, row, row, _full((4, CH, CH)), _full((CH, 128)), _full((1, D)), _full((1, D))],
        out_shape=[jax.ShapeDtypeStruct((S, D), F32)] * 3
        + [jax.ShapeDtypeStruct((4, CH, CH), F32), jax.ShapeDtypeStruct((CH, 128), F32),
           jax.ShapeDtypeStruct((1, D), F32), jax.ShapeDtypeStruct((1, D), F32)],
        compiler_params=_cparams(("arbitrary",)),
    )(z, z, z, d_cat, lng, lnb, w, bfull)


TB = 256


def _cmul(ar, ai, br, bi):
    return ar * br - ai * bi, ar * bi + ai * br


def _scan_consts(ar, ai, reverse):
    a2 = _cmul(ar, ai, ar, ai)
    a4 = _cmul(*a2, *a2)
    row = lax.broadcasted_iota(jnp.int32, (8, NS), 0)
    pr = jnp.zeros((8, NS), F32)
    pi = jnp.zeros((8, NS), F32)
    cr, ci = ar, ai
    for r in range(8):
        sel = row == (7 - r if reverse else r)
        pr = jnp.where(sel, cr, pr)
        pi = jnp.where(sel, ci, pi)
        cr, ci = _cmul(cr, ci, ar, ai)
    return ((ar, ai), a2, a4), (pr, pi), row


def scan_fwd(bu, abr, abi):
    def body(bu_ref, ar_ref, ai_ref, h_ref, car, cai):
        @pl.when(pl.program_id(0) == 0)
        def _():
            car[...] = jnp.zeros_like(car)
            cai[...] = jnp.zeros_like(cai)

        pows, (pr, pi), row = _scan_consts(ar_ref[...], ai_ref[...], False)

        def tile(t, carry):
            c_r, c_i = carry
            rows = pl.ds(pl.multiple_of(t * 8, 8), 8)
            xr = bu_ref[rows, 0:NS]
            xi = bu_ref[rows, NS:2 * NS]
            for k, (kr, ki) in zip((1, 2, 4), pows):
                sr = jnp.where(row >= k, pltpu.roll(xr, k, 0), 0.0)
                si = jnp.where(row >= k, pltpu.roll(xi, k, 0), 0.0)
                xr, xi = xr + kr * sr - ki * si, xi + kr * si + ki * sr
            xr, xi = xr + pr * c_r - pi * c_i, xi + pr * c_i + pi * c_r
            h_ref[rows, 0:NS] = xr
            h_ref[rows, NS:2 * NS] = xi
            return (jnp.broadcast_to(xr[7:8, :], (8, NS)), jnp.broadcast_to(xi[7:8, :], (8, NS)))

        c_r, c_i = lax.fori_loop(0, TB // 8, tile, (car[...], cai[...]))
        car[...] = c_r
        cai[...] = c_i

    return pl.pallas_call(
        body, name="s5_scan_fwd", grid=(S // TB,),
        in_specs=[pl.BlockSpec((TB, 2 * NS), lambda i: (i, 0)),
                  pl.BlockSpec((1, NS), lambda i: (0, 0)), pl.BlockSpec((1, NS), lambda i: (0, 0))],
        out_specs=pl.BlockSpec((TB, 2 * NS), lambda i: (i, 0)),
        out_shape=jax.ShapeDtypeStruct((S, 2 * NS), F32),
        scratch_shapes=[pltpu.VMEM((8, NS), F32), pltpu.VMEM((8, NS), F32)],
        compiler_params=_cparams(("arbitrary",)),
    )(bu, abr, abi)


def scan_bwd(eta, h, abr, abi):
    nt = S // TB

    def body(e_ref, h_ref, ar_ref, ai_ref, l_ref, da_ref, car, cai):
        @pl.when(pl.program_id(0) == 0)
        def _():
            car[...] = jnp.zeros_like(car)
            cai[...] = jnp.zeros_like(cai)
            da_ref[...] = jnp.zeros_like(da_ref)

        pows, (pr, pi), row = _scan_consts(ar_ref[...], -ai_ref[...], True)

        def tile(tt, carry):
            c_r, c_i, acr, aci = carry
            t = TB // 8 - 1 - tt
            rows = pl.ds(pl.multiple_of(t * 8, 8), 8)
            xr = e_ref[rows, 0:NS]
            xi = e_ref[rows, NS:2 * NS]
            for k, (kr, ki) in zip((1, 2, 4), pows):
                sr = jnp.where(row < 8 - k, pltpu.roll(xr, 8 - k, 0), 0.0)
                si = jnp.where(row < 8 - k, pltpu.roll(xi, 8 - k, 0), 0.0)
                xr, xi = xr + kr * sr - ki * si, xi + kr * si + ki * sr
            xr, xi = xr + pr * c_r - pi * c_i, xi + pr * c_i + pi * c_r
            l_ref[rows, 0:NS] = xr
            l_ref[rows, NS:2 * NS] = xi
            nr = jnp.where(row < 7, pltpu.roll(xr, 7, 0), c_r)
            ni = jnp.where(row < 7, pltpu.roll(xi, 7, 0), c_i)
            hr = h_ref[rows, 0:NS]
            hi = h_ref[rows, NS:2 * NS]
            acr = acr + hr * nr + hi * ni
            aci = aci + hr * ni - hi * nr
            return (jnp.broadcast_to(xr[0:1, :], (8, NS)), jnp.broadcast_to(xi[0:1, :], (8, NS)), acr, aci)

        zero = jnp.zeros((8, NS), F32)
        c_r, c_i, acr, aci = lax.fori_loop(0, TB // 8, tile, (car[...], cai[...], zero, zero))
        car[...] = c_r
        cai[...] = c_i
        da_ref[:, 0:NS] += acr
        da_ref[:, NS:2 * NS] += aci

    rev = pl.BlockSpec((TB, 2 * NS), lambda i: (nt - 1 - i, 0))
    return pl.pallas_call(
        body, name="s5_scan_bwd", grid=(nt,),
        in_specs=[rev, rev, pl.BlockSpec((1, NS), lambda i: (0, 0)), pl.BlockSpec((1, NS), lambda i: (0, 0))],
        out_specs=[rev, pl.BlockSpec((8, 2 * NS), lambda i: (0, 0))],
        out_shape=[jax.ShapeDtypeStruct((S, 2 * NS), F32), jax.ShapeDtypeStruct((8, 2 * NS), F32)],
        scratch_shapes=[pltpu.VMEM((8, NS), F32), pltpu.VMEM((8, NS), F32)],
        compiler_params=_cparams(("arbitrary",)),
    )(eta, h, abr, abi)


GC = 0.7978845608028654
GA = 0.044715


def s5_post(hc, z, dskip):
    def fn(hv, xd, dv):
        y = hv + dv * xd
        return y, 0.5 * y * (1.0 + jnp.tanh(GC * (y + GA * y * y * y)))
    return rw(fn, [(hc, 0, 512), (z, 3072, 512)], [(512, F32), (512, BF16)], "s5_post", S, consts=[dskip])


def s5_post_bwd(dyg, ypre, z, dskip):
    def fn(dy, y, xd, dv):
        th = jnp.tanh(GC * (y + GA * y * y * y))
        dg = 0.5 * (1.0 + th) + 0.5 * y * (1.0 - th * th) * GC * (1.0 + 3.0 * GA * y * y)
        dyp = dy * dg
        return dyp, dyp * dv, jnp.sum(dyp * xd, axis=0, keepdims=True)
    return rw(fn, [(dyg, 0, 512), (ypre, 0, 512), (z, 3072, 512)], [(512, BF16), (512, F32)],
              "s5_post_bwd", S, consts=[dskip], accs=[(1, 512)])


def glu_fwd(t, z, c_out):
    def fn(t1, t2, gd, co):
        return (jnp.concatenate([co, (t1 * _sig(t2) * (gd * _sig(gd))).astype(BF16)], axis=1),)
    return rw(fn, [(t, 0, 512), (t, 512, 512), (z, 3584, 512), (c_out, 0, D)], [(D + 512, BF16)], "glu_fwd", S)[0]


def glu_bwd(t, z, d_cat):
    def fn(t1, t2, gd, dd):
        s2, sg = _sig(t2), _sig(gd)
        sl = gd * sg
        return (jnp.concatenate([dd * s2 * sl, dd * t1 * s2 * (1.0 - s2) * sl], axis=1),
                dd * t1 * s2 * (sg * (1.0 + gd * (1.0 - sg))))
    return rw(fn, [(t, 0, 512), (t, 512, 512), (z, 3584, 512), (d_cat, 1024, 512)],
              [(D, BF16), (512, F32)], "glu_bwd", S)


def assemble_dz_odd(du, dv, dgc, dxd, dgd):
    def body(a, b, c, d, e, o_ref):
        o_ref[:, 0:D] = a[...].astype(BF16)
        o_ref[:, D:2 * D] = b[...].astype(BF16)
        o_ref[:, 2 * D:3 * D] = c[...].astype(BF16)
        o_ref[:, 3 * D:3 * D + 512] = d[...].astype(BF16)
        o_ref[:, 3 * D + 512:4 * D] = e[...].astype(BF16)
    tr = 256
    blk = pl.BlockSpec((tr, D), lambda i: (i, 0))
    half = pl.BlockSpec((tr, 512), lambda i: (i, 0))
    return pl.pallas_call(
        body, name="assemble_dz_odd", grid=(S // tr,), in_specs=[blk, blk, blk, half, half],
        out_specs=pl.BlockSpec((tr, 4 * D), lambda i: (i, 0)),
        out_shape=jax.ShapeDtypeStruct((S, 4 * D), BF16),
        compiler_params=_cparams(("parallel",)),
    )(du, dv, dgc, dxd, dgd)


TQ = 256


def _xattn_probs(qh, kh):
    s = _nt(qh, kh) * 0.0625
    p = jnp.exp(s - jnp.max(s, axis=-1, keepdims=True))
    return p / jnp.sum(p, axis=-1, keepdims=True)


def xattn_fwd(q, kv):
    def body(q_ref, kv_ref, o_ref):
        for h in range(4):
            sl = slice(h * 256, (h + 1) * 256)
            p = _xattn_probs(q_ref[:, sl].astype(BF16), kv_ref[:, sl].astype(BF16))
            vh = kv_ref[:, D + h * 256:D + (h + 1) * 256].astype(BF16)
            o_ref[:, sl] = _nn(p.astype(BF16), vh).astype(o_ref.dtype)

    return pl.pallas_call(
        body, name="xattn_fwd", grid=(S // TQ,),
        in_specs=[pl.BlockSpec((TQ, D), lambda i: (i, 0)), pl.BlockSpec((MEM, 2 * D), lambda i: (0, 0))],
        out_specs=pl.BlockSpec((TQ, D), lambda i: (i, 0)),
        out_shape=jax.ShapeDtypeStruct((S, D), BF16),
        compiler_params=_cparams(("parallel",)),
    )(q, kv)


def xattn_bwd(q, kv, d_o):
    def body(q_ref, kv_ref, do_ref, dq_ref, dkv_ref):
        @pl.when(pl.program_id(0) == 0)
        def _():
            dkv_ref[...] = jnp.zeros_like(dkv_ref)

        for h in range(4):
            sl = slice(h * 256, (h + 1) * 256)
            vs = slice(D + h * 256, D + (h + 1) * 256)
            qh = q_ref[:, sl].astype(BF16)
            kh = kv_ref[:, sl].astype(BF16)
            vh = kv_ref[:, vs].astype(BF16)
            doh = do_ref[:, sl].astype(BF16)
            p = _xattn_probs(qh, kh)
            dp = _nt(doh, vh)
            ds = (p * (dp - jnp.sum(p * dp, axis=-1, keepdims=True)) * 0.0625).astype(BF16)
            dq_ref[:, sl] = _nn(ds, kh).astype(dq_ref.dtype)
            dkv_ref[:, sl] += _tn(ds, qh)
            dkv_ref[:, vs] += _tn(p.astype(BF16), doh)

    return pl.pallas_call(
        body, name="xattn_bwd", grid=(S // TQ,),
        in_specs=[pl.BlockSpec((TQ, D), lambda i: (i, 0)), pl.BlockSpec((MEM, 2 * D), lambda i: (0, 0)),
                  pl.BlockSpec((TQ, D), lambda i: (i, 0))],
        out_specs=[pl.BlockSpec((TQ, D), lambda i: (i, 0)), pl.BlockSpec((MEM, 2 * D), lambda i: (0, 0))],
        out_shape=[jax.ShapeDtypeStruct((S, D), BF16), jax.ShapeDtypeStruct((MEM, 2 * D), F32)],
        compiler_params=_cparams(("arbitrary",)),
    )(q, kv, d_o)


def _s5_disc(a_re, a_im, log_dt, b_re, b_im):
    dt = jnp.exp(log_dt)[:, None]
    mag = jnp.exp(dt * a_re)
    abr = mag * jnp.cos(dt * a_im)
    abi = mag * jnp.sin(dt * a_im)
    nr, ni = abr - 1.0, abi
    inv = 1.0 / (a_re * a_re + a_im * a_im)
    cr = (nr * a_re + ni * a_im) * inv
    ci = (ni * a_re - nr * a_im) * inv
    bbr = cr[..., None] * b_re - ci[..., None] * b_im
    bbi = cr[..., None] * b_im + ci[..., None] * b_re
    return abr, abi, bbr, bbi


def _blockdiag(t):
    g, a, b = t.shape
    eye = jnp.eye(g, dtype=t.dtype)
    return (eye[:, None, :, None] * t[:, :, None, :]).reshape(g * a, g * b)


def _blocks(mat, a, b):
    return jnp.einsum("gagb->gab", mat.reshape(NG, a, NG, b))


def _fwd_even(i, x, P):
    hn = rms_fwd(x, P["norm_ab"][i:i + 1], "rms_ab_fwd")
    z = mm(m2(hn), mcs(P["w_in_ab"], i), "nn", "in_ab")
    o, lse, cat = attn_fwd(z)
    cat = pool_fwd(z, P["pool_w"][i], P["pool_scale"][i:i + 1], cat)
    x_mid = mm(m2(cat), m3(P["w_out_ab"], i), "nn", "out_ab", add=m2(x))
    return x_mid, dict(x=x, hn=hn, z=z, o=o, lse=lse, cat=cat)


def _bwd_even(i, dx_mid, sv, P, G):
    z = sv["z"]
    d_cat = mm(m2(dx_mid), m3(P["w_out_ab"], i), "nt", "out_ab_dx")
    G["w_out_ab"][i] = mm(m2(sv["cat"]), m2(dx_mid), "tn", "out_ab_dw").reshape(4, 512, D)
    dq, dk, dv, dga = attn_bwd(z, d_cat, sv["o"], sv["lse"])
    dvb, dgb, dpw, dps = pool_bwd(z, d_cat, P["pool_w"][i], P["pool_scale"][i:i + 1])
    G["pool_w"][i] = dpw.reshape(4, 4, 64, 256).transpose(1, 0, 2, 3).reshape(4, 256, 256)
    G["pool_scale"][i] = dps[0]
    d_z = assemble_dz_even((dq, dk, dv, dga, dvb, dgb))
    d_hn = mm(m2(d_z), mcs(P["w_in_ab"], i), "nt", "in_ab_dx")
    G["w_in_ab"][i] = mm(m2(sv["hn"]), m2(d_z), "tn", "in_ab_dw", out=outcs(D, 1536))
    dx, dg = rms_bwd(sv["x"], d_hn, dx_mid, P["norm_ab"][i:i + 1], "rms_ab_bwd")
    G["norm_ab"][i] = dg[0]
    return dx


def _fwd_odd(i, x, P):
    hn = rms_fwd(x, P["norm_cd"][i:i + 1], "rms_cd_fwd")
    z = mm(m2(hn), mcs(P["w_in_cd"], i), "nn", "in_cd")
    bfull = jnp.repeat(P["sgu_b"][i].T, 256, axis=1)
    c_out = sgu_fwd(z, P["sgu_ln_g"][i:i + 1], P["sgu_ln_b"][i:i + 1], P["sgu_w"][i], bfull)
    disc, disc_vjp = jax.vjp(_s5_disc, P["s5_a_re"][i], P["s5_a_im"][i], P["s5_log_dt"][i],
                             P["s5_b_re"][i], P["s5_b_im"][i])
    abr, abi, bbr, bbi = disc
    bbd = jnp.concatenate([_blockdiag(bbr.transpose(0, 2, 1)), _blockdiag(bbi.transpose(0, 2, 1))], axis=1)
    cbd = jnp.concatenate([_blockdiag(P["s5_c_re"][i].transpose(0, 2, 1)),
                           -_blockdiag(P["s5_c_im"][i].transpose(0, 2, 1))], axis=0)
    abr, abi = abr.reshape(1, NS), abi.reshape(1, NS)
    bu = mm(m2(z, 3072, 512), m2(bbd), "nn", "s5_bu")
    h = scan_fwd(bu, abr, abi)
    hc = mm(m2(h), m2(cbd), "nn", "s5_hc")
    dskip = P["s5_d"][i:i + 1]
    ypre, yg = s5_post(hc, z, dskip)
    w12 = jnp.concatenate([P["glu_w1"][i], P["glu_w2"][i]], axis=1)
    t = mm(m2(yg), m2(w12), "nn", "glu_t")
    cat = glu_fwd(t, z, c_out)
    x_mid = mm(m2(cat), m3(P["w_out_cd"], i), "nn", "out_cd", add=m2(x))
    return x_mid, dict(x=x, hn=hn, z=z, bfull=bfull, disc_vjp=disc_vjp, bbd=bbd, cbd=cbd, abr=abr,
                       abi=abi, h=h, ypre=ypre, yg=yg, w12=w12, t=t, cat=cat, dskip=dskip)


def _bwd_odd(i, dx_mid, sv, P, G):
    z = sv["z"]
    d_cat = mm(m2(dx_mid), m3(P["w_out_cd"], i), "nt", "out_cd_dx")
    G["w_out_cd"][i] = mm(m2(sv["cat"]), m2(dx_mid), "tn", "out_cd_dw").reshape(4, 384, D)
    du, dv, dgc, dws, dbs, dlg, dlb = sgu_bwd(z, d_cat, P["sgu_ln_g"][i:i + 1], P["sgu_ln_b"][i:i + 1],
                                               P["sgu_w"][i], sv["bfull"])
    G["sgu_w"][i], G["sgu_b"][i] = dws, dbs[:, :4].T
    G["sgu_ln_g"][i], G["sgu_ln_b"][i] = dlg[0], dlb[0]
    dt, dgd = glu_bwd(sv["t"], z, d_cat)
    gw12 = mm(m2(sv["yg"]), m2(dt), "tn", "glu_dw")
    G["glu_w1"][i] = gw12[:, :512].reshape(4, 128, 512)
    G["glu_w2"][i] = gw12[:, 512:].reshape(4, 128, 512)
    dyg = mm(m2(dt), m2(sv["w12"]), "nt", "glu_dx")
    dypre, dxd1, dd = s5_post_bwd(dyg, sv["ypre"], z, sv["dskip"])
    G["s5_d"][i] = dd[0]
    gcbd = mm(m2(sv["h"]), m2(dypre), "tn", "s5_dc")
    G["s5_c_re"][i] = _blocks(gcbd[:NS], NP, NH).transpose(0, 2, 1)
    G["s5_c_im"][i] = -_blocks(gcbd[NS:], NP, NH).transpose(0, 2, 1)
    eta = mm(m2(dypre), m2(sv["cbd"]), "nt", "s5_eta")
    lam, dacc = scan_bwd(eta, sv["h"], sv["abr"], sv["abi"])
    gbbd = mm(m2(z, 3072, 512), m2(lam), "tn", "s5_db")
    dxd = mm(m2(lam), m2(sv["bbd"]), "nt", "s5_dx", add=m2(dxd1))
    dacc = jnp.sum(dacc, axis=0)
    d_bbr = _blocks(gbbd[:, :NS], NH, NP).transpose(0, 2, 1)
    d_bbi = _blocks(gbbd[:, NS:], NH, NP).transpose(0, 2, 1)
    (G["s5_a_re"][i], G["s5_a_im"][i], G["s5_log_dt"][i], G["s5_b_re"][i], G["s5_b_im"][i]) = sv["disc_vjp"](
        (dacc[:NS].reshape(NG, NP), dacc[NS:].reshape(NG, NP), d_bbr, d_bbi))
    d_z = assemble_dz_odd(du, dv, dgc, dxd, dgd)
    d_hn = mm(m2(d_z), mcs(P["w_in_cd"], i), "nt", "in_cd_dx")
    G["w_in_cd"][i] = mm(m2(sv["hn"]), m2(d_z), "tn", "in_cd_dw", out=outcs(D, 1024))
    dx, dg = rms_bwd(sv["x"], d_hn, dx_mid, P["norm_cd"][i:i + 1], "rms_cd_bwd")
    G["norm_cd"][i] = dg[0]
    return dx


def _fwd_x(l, x, mem_n, P):
    hx = rms_fwd(x, P["norm_x"][l:l + 1], "rms_x_fwd")
    q = mm(m2(hx), m3(P["w_xq"], l), "nn", "xq", out_dtype=BF16)
    kv = mm(m2(mem_n), mcs(P["w_xkv"], l), "nn", "xkv", out_dtype=BF16)
    ox = xattn_fwd(q, kv)
    x_out = mm(m2(ox), m3(P["w_xo"], l), "nn", "xo", add=m2(x))
    return x_out, dict(x=x, hx=hx, q=q, kv=kv, ox=ox)


def _bwd_x(l, dx_out, sv, mem_n, d_memn, P, G):
    d_ox = mm(m2(dx_out), m3(P["w_xo"], l), "nt", "xo_dx", out_dtype=BF16)
    G["w_xo"][l] = mm(m2(sv["ox"]), m2(dx_out), "tn", "xo_dw").reshape(4, 256, D)
    dq, dkv = xattn_bwd(sv["q"], sv["kv"], d_ox)
    G["w_xq"][l] = mm(m2(sv["hx"]), m2(dq), "tn", "xq_dw").reshape(4, 256, D)
    d_hx = mm(m2(dq), m3(P["w_xq"], l), "nt", "xq_dx")
    G["w_xkv"][l] = mm(m2(mem_n), m2(dkv), "tn", "xkv_dw", out=outcs(D, 512))
    d_memn = mm(m2(dkv), mcs(P["w_xkv"], l), "nt", "xkv_dx", add=None if d_memn is None else m2(d_memn))
    dx, dg = rms_bwd(sv["x"], d_hx, dx_out, P["norm_x"][l:l + 1], "rms_x_bwd")
    G["norm_x"][l] = dg[0]
    return dx, d_memn


def local_step(x, mem, tgt, P):
    G = {k: [None] * n for k, n in (
        ("norm_ab", 2), ("w_in_ab", 2), ("pool_w", 2), ("pool_scale", 2), ("w_out_ab", 2), ("norm_cd", 2),
        ("w_in_cd", 2), ("sgu_ln_g", 2), ("sgu_ln_b", 2), ("sgu_w", 2), ("sgu_b", 2), ("s5_a_re", 2),
        ("s5_a_im", 2), ("s5_log_dt", 2), ("s5_b_re", 2), ("s5_b_im", 2), ("s5_c_re", 2), ("s5_c_im", 2),
        ("s5_d", 2), ("glu_w1", 2), ("glu_w2", 2), ("w_out_cd", 2), ("norm_x", 4), ("w_xq", 4),
        ("w_xkv", 4), ("w_xo", 4))}
    mem_g = P["mem_norm"].reshape(1, D)
    mem_n = rms_fwd(mem, mem_g, "rms_mem_fwd")
    saved = []
    for layer in range(4):
        i = layer // 2
        x, sv_m = (_fwd_even if layer % 2 == 0 else _fwd_odd)(i, x, P)
        x, sv_x = _fwd_x(layer, x, mem_n, P)
        saved.append((sv_m, sv_x))
    dx, loss, dgf = final_loss(x, tgt, P["final_norm"].reshape(1, D))
    G["final_norm"] = dgf[0]
    d_memn = None
    for layer in reversed(range(4)):
        i = layer // 2
        sv_m, sv_x = saved[layer]
        dx, d_memn = _bwd_x(layer, dx, sv_x, mem_n, d_memn, P, G)
        dx = (_bwd_even if layer % 2 == 0 else _bwd_odd)(i, dx, sv_m, P, G)
    _, dgm = rms_bwd(mem, d_memn, d_memn, mem_g, "rms_mem_bwd")
    G["mem_norm"] = dgm[0]
    return loss, dx, G


ANY = pl.BlockSpec(memory_space=pl.ANY)


def _place():
    x, y, c = lax.axis_index("x"), lax.axis_index("y"), lax.axis_index("c")
    chips = [(1 - x, y), (x, 1 - y), (1 - x, 1 - y)]
    return x, y, c, 2 * x + y, (x, y, 1 - c), chips


def _remote(src, dst, send, recv, k, dev):
    return pltpu.make_async_remote_copy(src_ref=src, dst_ref=dst, send_sem=send.at[k], recv_sem=recv.at[k],
                                        device_id=dev, device_id_type=MESHID)


def allgather_big(shards):
    n = len(shards)

    def body(*refs):
        ins, outs = refs[:n], refs[n:2 * n]
        send, recv = refs[2 * n:]
        x, y, c, jme, sib, chips = _place()

        def half(a, hc):
            lh = shards[a].shape[0] // 2
            return pl.ds(hc * lh, lh)

        first, passed = [], []
        for a in range(n):
            cp = _remote(ins[a], outs[a].at[:, jme], send, recv, a * 7 + 6, sib)
            cp.start()
            first.append(cp)
            for k, chip in enumerate(chips):
                cp = _remote(ins[a].at[half(a, c)], outs[a].at[half(a, c), jme], send, recv, a * 7 + k, (*chip, c))
                cp.start()
                first.append(cp)
        for a in range(n):
            for k, chip in enumerate(chips):
                piece = outs[a].at[half(a, c), 2 * chip[0] + chip[1]]
                _remote(piece, piece, send, recv, a * 7 + k, (*chip, c)).wait_recv()
                fw = _remote(piece, piece, send, recv, a * 7 + 3 + k, sib)
                fw.start()
                passed.append(fw)
        for a in range(n):
            own = outs[a].at[:, jme]
            _remote(own, own, send, recv, a * 7 + 6, sib).wait_recv()
            for k, chip in enumerate(chips):
                piece = outs[a].at[half(a, 1 - c), 2 * chip[0] + chip[1]]
                _remote(piece, piece, send, recv, a * 7 + 3 + k, sib).wait_recv()
        for cp in first + passed:
            cp.wait_send()

    return pl.pallas_call(
        body, name="allgather_big", in_specs=[ANY] * n, out_specs=[ANY] * n,
        out_shape=[jax.ShapeDtypeStruct((s.shape[0], 4) + s.shape[1:], s.dtype) for s in shards],
        scratch_shapes=[pltpu.SemaphoreType.DMA((7 * n,)), pltpu.SemaphoreType.DMA((7 * n,))],
    )(*shards)


def allgather_small(slab):
    def body(in_ref, out_ref, send, recv, lsem):
        x, y, c, jme, sib, chips = _place()
        loc = pltpu.make_async_copy(in_ref, out_ref.at[jme], lsem.at[0])
        loc.start()
        cps = [_remote(in_ref, out_ref.at[jme], send, recv, k, (*chip, c)) for k, chip in enumerate(chips)]
        for cp in cps:
            cp.start()
        for k, chip in enumerate(chips):
            piece = out_ref.at[2 * chip[0] + chip[1]]
            _remote(piece, piece, send, recv, k, (*chip, c)).wait_recv()
        for cp in cps:
            cp.wait_send()
        loc.wait()

    return pl.pallas_call(
        body, name="allgather_small", in_specs=[ANY], out_specs=ANY,
        out_shape=jax.ShapeDtypeStruct((4,) + slab.shape, slab.dtype),
        scratch_shapes=[pltpu.SemaphoreType.DMA((3,)), pltpu.SemaphoreType.DMA((3,)), pltpu.SemaphoreType.DMA((1,))],
    )(slab)


def allreduce_small(v):
    def body(v_ref, o_ref, r0, r1, r2, send, recv):
        x, y, c, jme, sib, chips = _place()
        peers = [sib, (1 - x, y, c), (x, 1 - y, c)]
        o_ref[...] = v_ref[...]
        for k, buf in enumerate((r0, r1, r2)):
            cp = _remote(o_ref, buf, send, recv, k, peers[k])
            cp.start()
            cp.wait()
            o_ref[...] = o_ref[...] + buf[...]

    vm = pl.BlockSpec(memory_space=pltpu.VMEM)
    return pl.pallas_call(
        body, name="allreduce_small", in_specs=[vm], out_specs=vm,
        out_shape=jax.ShapeDtypeStruct(v.shape, v.dtype),
        scratch_shapes=[pltpu.VMEM(v.shape, v.dtype)] * 3 + [pltpu.SemaphoreType.DMA((3,)), pltpu.SemaphoreType.DMA((3,))],
        compiler_params=pltpu.CompilerParams(vmem_limit_bytes=VMEM_LIMIT),
    )(v)


def rs_pair_exchange(gs, layout):
    n, nw = len(gs), len(layout)

    def body(*refs):
        ins, outs = refs[:n], refs[n:n + nw]
        send, recv = refs[n + nw:]
        x, y, c, jme, sib, chips = _place()
        cps = []
        for w, (first, layers) in enumerate(layout):
            for l in range(layers):
                cps.append(_remote(ins[first + l].at[:, 1 - c], outs[w].at[l], send, recv, first + l, sib))
        for cp in cps:
            cp.start()
        for cp in cps:
            cp.wait()

    return pl.pallas_call(
        body, name="rs_pair_exchange", in_specs=[ANY] * n, out_specs=[ANY] * nw,
        out_shape=[jax.ShapeDtypeStruct((layers, 4) + gs[first].shape[2:], F32) for first, layers in layout],
        scratch_shapes=[pltpu.SemaphoreType.DMA((n,)), pltpu.SemaphoreType.DMA((n,))],
    )(*gs)


def rs_pair_sum(g4, got, l, acc, cidx):
    _, _, rh, cols = g4.shape
    tr = rh if rh <= 256 else 256

    def body(c_ref, a_ref, b_ref, *rest):
        o_ref = rest[-1]
        o_ref[...] = (a_ref[...] + b_ref[...]).astype(o_ref.dtype)

    in_specs = [pl.BlockSpec((None, None, tr, cols), lambda j, t, cr: (j, cr[0], t, 0)),
                pl.BlockSpec((None, None, tr, cols), lambda j, t, cr: (l, j, t, 0))]
    args = [cidx, g4, got]
    if acc is not None:
        in_specs.append(ANY)
        args.append(acc)
    return pl.pallas_call(
        body, name="rs_pair_sum",
        grid_spec=pltpu.PrefetchScalarGridSpec(
            num_scalar_prefetch=1, grid=(4, rh // tr), in_specs=in_specs,
            out_specs=pl.BlockSpec((None, None, tr, cols), lambda j, t, cr: (l, j, t, 0))),
        out_shape=jax.ShapeDtypeStruct(got.shape, BF16),
        input_output_aliases={} if acc is None else {3: 0},
        compiler_params=_cparams(("parallel", "parallel")),
    )(*args)


def rs_chip_exchange(ps):
    n = len(ps)

    def body(*refs):
        ins, outs = refs[:n], refs[n:2 * n]
        send, recv = refs[2 * n:]
        x, y, c, jme, sib, chips = _place()
        cps = []
        for a in range(n):
            for k, chip in enumerate(chips):
                cp = _remote(ins[a].at[:, 2 * chip[0] + chip[1]], outs[a].at[:, jme], send, recv, a * 3 + k, (*chip, c))
                cp.start()
                cps.append(cp)
        for a in range(n):
            for k, chip in enumerate(chips):
                slot = outs[a].at[:, 2 * chip[0] + chip[1]]
                _remote(slot, slot, send, recv, a * 3 + k, (*chip, c)).wait_recv()
        for cp in cps:
            cp.wait_send()

    return pl.pallas_call(
        body, name="rs_chip_exchange", in_specs=[ANY] * n, out_specs=[ANY] * n,
        out_shape=[jax.ShapeDtypeStruct(p.shape, p.dtype) for p in ps],
        scratch_shapes=[pltpu.SemaphoreType.DMA((3 * n,)), pltpu.SemaphoreType.DMA((3 * n,))],
    )(*ps)


def rs_chip_sum(q, p, jc):
    layers, _, rh, cols = q.shape
    tr = rh if rh <= 256 else 256

    def body(jc_ref, q_ref, p_ref, o_ref):
        jme = jc_ref[0]
        own = p_ref[...].astype(F32)
        v = [jnp.where(jme == j, own, q_ref[j].astype(F32)) for j in range(4)]
        o_ref[...] = ((v[0] + v[1]) + v[2]) + v[3]

    return pl.pallas_call(
        body, name="rs_chip_sum",
        grid_spec=pltpu.PrefetchScalarGridSpec(
            num_scalar_prefetch=1, grid=(layers, rh // tr),
            in_specs=[pl.BlockSpec((None, 4, tr, cols), lambda l, t, jr: (l, 0, t, 0)),
                      pl.BlockSpec((None, None, tr, cols), lambda l, t, jr: (l, jr[0], t, 0))],
            out_specs=pl.BlockSpec((None, None, tr, cols), lambda l, t, jr: (l, jr[1], t, 0))),
        out_shape=jax.ShapeDtypeStruct((layers, 2, rh, cols), F32),
        compiler_params=_cparams(("parallel", "parallel")),
    )(jc, q, p)


def rs_pair_gather(rs):
    n = len(rs)

    def body(*refs):
        outs = refs[n:2 * n]
        send, recv = refs[2 * n:]
        x, y, c, jme, sib, chips = _place()
        cps = [_remote(outs[a].at[:, c], outs[a].at[:, c], send, recv, a, sib) for a in range(n)]
        for cp in cps:
            cp.start()
        for a in range(n):
            slot = outs[a].at[:, 1 - c]
            _remote(slot, slot, send, recv, a, sib).wait_recv()
        for cp in cps:
            cp.wait_send()

    return pl.pallas_call(
        body, name="rs_pair_gather", in_specs=[ANY] * n, out_specs=[ANY] * n,
        out_shape=[jax.ShapeDtypeStruct(r.shape, r.dtype) for r in rs],
        input_output_aliases={a: a for a in range(n)},
        scratch_shapes=[pltpu.SemaphoreType.DMA((n,)), pltpu.SemaphoreType.DMA((n,))],
    )(*rs)


def _adamw_math(w, g, m, v):
    m = B1 * m + (1.0 - B1) * g
    v = B2 * v + (1.0 - B2) * (g * g)
    m_hat = m / (1.0 - B1 ** STEP)
    v_hat = v / (1.0 - B2 ** STEP)
    return -LR * (m_hat / (jnp.sqrt(v_hat) + AEPS) + WD * w), m, v


def adamw(w, g, m, v, name):
    rows, cols = w.shape
    tr = 256 if rows % 256 == 0 else rows
    return rw(_adamw_math, [(a, 0, cols) for a in (w, g, m, v)], [(cols, F32)] * 3, name, rows, tr=tr)


WEIGHTS = ["norm_ab", "w_in_ab", "pool_w", "pool_scale", "w_out_ab", "norm_cd", "w_in_cd", "sgu_ln_g", "sgu_ln_b",
           "sgu_w", "sgu_b", "s5_a_re", "s5_a_im", "s5_log_dt", "s5_b_re", "s5_b_im", "s5_c_re", "s5_c_im", "s5_d",
           "glu_w1", "glu_w2", "w_out_cd", "norm_x", "w_xq", "w_xkv", "w_xo", "mem_norm", "final_norm"]
INPUTS = ["x", "mem"] + WEIGHTS + ["loss_target"] + ["m_" + n for n in WEIGHTS] + ["v_" + n for n in WEIGHTS]
BIG = ["w_in_ab", "w_out_ab", "w_in_cd", "w_out_cd", "w_xq", "w_xkv", "w_xo", "glu_w1", "glu_w2", "pool_w"]
COL_SHARDED = ("w_in_ab", "w_in_cd", "w_xkv")
SMALL = [n for n in WEIGHTS if n not in BIG]
SMALL_SHARDED = {"norm_cd": 256, "sgu_ln_g": 256, "sgu_ln_b": 256, "s5_d": 128}
PACK = 256 * 128


def _pack(arrs):
    flat = jnp.concatenate([a.reshape(-1) for a in arrs])
    pad = (-flat.shape[0]) % PACK
    return jnp.concatenate([flat, jnp.zeros((pad,), flat.dtype)]).reshape(-1, 128)


def _unpack(packed, shapes):
    flat, out, off = packed.reshape(-1), [], 0
    for s in shapes:
        n = 1
        for d in s:
            n *= d
        out.append(flat[off:off + n].reshape(s))
        off += n
    return out


def _mat(a):
    return a.reshape(a.shape[0], -1, a.shape[-1])


def kernel(*args):
    a = dict(zip(INPUTS, args))
    x_i, y_i, c_i = lax.axis_index("x"), lax.axis_index("y"), lax.axis_index("c")
    j = 2 * x_i + y_i

    gathered = dict(zip(BIG, allgather_big([_mat(a[n]).astype(BF16) for n in BIG])))
    slab = jnp.concatenate([a["norm_cd"], a["sgu_ln_g"], a["sgu_ln_b"],
                            jnp.pad(a["s5_d"], ((0, 0), (0, 128)))], axis=0)
    gslab = allgather_small(slab)
    P = {n: a[n] for n in SMALL}
    for k, n in enumerate(("norm_cd", "sgu_ln_g", "sgu_ln_b", "s5_d")):
        wd = SMALL_SHARDED[n]
        P[n] = gslab[:, 2 * k:2 * k + 2, :wd].transpose(1, 0, 2).reshape(2, 4 * wd)
    for n in BIG:
        g = gathered[n]
        if n in COL_SHARDED:
            P[n] = g
        elif n == "pool_w":
            P[n] = g.reshape(2, 4, 4, 64, 256).transpose(0, 2, 1, 3, 4).reshape(2, 4, 256, 256)
        else:
            P[n] = g.reshape(g.shape[0], 4 * g.shape[2], g.shape[3])

    loss, dx, G = local_step(a["x"][0], a["mem"][0], a["loss_target"][0], P)
    loss = lax.psum(loss[0, 0], ("x", "y", "c"))

    layout, flat = [], []
    for n in BIG:
        layout.append((len(flat), len(G[n])))
        flat += [g.reshape(4, 2, g.shape[1] // 2, g.shape[2]) for g in G[n]]
    got = rs_pair_exchange(flat, layout)
    cidx = jnp.reshape(c_i, (1,)).astype(jnp.int32)
    jc = jnp.stack([j, c_i]).astype(jnp.int32)
    pair = []
    for (first, layers), gw in zip(layout, got):
        acc = None
        for l in range(layers):
            acc = rs_pair_sum(flat[first + l], gw, l, acc, cidx)
        pair.append(acc)
    red = [rs_chip_sum(q, p, jc) for q, p in zip(rs_chip_exchange(pair), pair)]
    gbig = dict(zip(BIG, rs_pair_gather(red)))

    outs = {}
    for n in BIG:
        shp = a[n].shape
        g2 = gbig[n].reshape(-1, shp[-1])
        d2, m2_, v2_ = adamw(a[n].reshape(g2.shape), g2, a["m_" + n].reshape(g2.shape),
                             a["v_" + n].reshape(g2.shape), "adamw_" + n)
        outs[n] = tuple(t.reshape(shp) for t in (g2, d2, m2_, v2_))

    gfull = [jnp.stack(G[n]) if isinstance(G[n], list) else G[n] for n in SMALL]
    shapes = [g.shape for g in gfull]
    gsum = _unpack(allreduce_small(_pack(gfull)), shapes)
    gloc = []
    for n, g in zip(SMALL, gsum):
        if n in SMALL_SHARDED:
            g = lax.dynamic_slice_in_dim(g, j * SMALL_SHARDED[n], SMALL_SHARDED[n], axis=1)
        gloc.append(g)
    lshapes = [a[n].shape for n in SMALL]
    packed = [_pack(t) for t in ([a[n] for n in SMALL], gloc, [a["m_" + n] for n in SMALL], [a["v_" + n] for n in SMALL])]
    small = [_unpack(t, lshapes) for t in adamw(*packed, "adamw_small")]
    for k, n in enumerate(SMALL):
        outs[n] = (gloc[k], small[0][k], small[1][k], small[2][k])

    res = [loss, dx[None]]
    for part in range(4):
        res += [outs[n][part] for n in WEIGHTS]
    return tuple(res)
```

```python
import math

import jax
import jax.numpy as jnp
from jax import lax
from jax.experimental import pallas as pl
from jax.experimental.pallas import tpu as pltpu

F32, BF16 = jnp.float32, jnp.bfloat16
S, D = 2048, 1024
MEM = 256
EPS = 1e-6
NEG = -1e30
QB = 128
PATTERNS = (1, 4, 16)
NG, NP, NH = 32, 64, 16
NS = NG * NP
LR, B1, B2, AEPS, WD, STEP = 0.001, 0.9, 0.999, 1e-08, 0.01, 10
MESHID = pl.DeviceIdType.MESH
VMEM_LIMIT = 56 * 1024 * 1024


def _cparams(sem):
    return pltpu.CompilerParams(dimension_semantics=sem, vmem_limit_bytes=VMEM_LIMIT)


def _sig(x):
    return 1.0 / (1.0 + jnp.exp(-x))


def _dot(a, b, dims):
    return lax.dot_general(a, b, (dims, ((), ())), preferred_element_type=F32)


def _nn(a, b):
    return _dot(a, b, ((1,), (0,)))


def _nt(a, b):
    return _dot(a, b, ((1,), (1,)))


def _tn(a, b):
    return _dot(a, b, ((0,), (0,)))


_DIMS = {"nn": ((1,), (0,)), "nt": ((1,), (1,)), "tn": ((0,), (0,))}


def _tile(dim, cc=None, cap=1024):
    for t in (2048, 1536, 1024, 768, 512, 384, 256, 128):
        if t <= cap and dim % t == 0 and (cc is None or cc % t == 0):
            return t
    return dim


MM_VMEM = 36 * 1024 * 1024


def _mm_tiles(m, n, k, ccm, ccn, cck, a_bytes, b_bytes, o_bytes):
    caps = [1024, 1024, 2048]
    while True:
        tm, tn, tk = _tile(m, ccm, caps[0]), _tile(n, ccn, caps[1]), _tile(k, cck, caps[2])
        need = 2 * (tm * tk * a_bytes + tk * tn * b_bytes + tm * tn * o_bytes) + (tm * tn * 4 if tk < k else 0)
        if need <= MM_VMEM:
            return tm, tn, tk
        if tk > 1024:
            caps[2] = tk // 2
        elif tn >= tm:
            caps[1] = tn // 2
        else:
            caps[0] = tm // 2


def m2(arr, col_off=0, ncols=None):
    rows, cols = arr.shape
    ncols = cols - col_off if ncols is None else ncols

    def spec(tr, tc, rc):
        assert col_off % tc == 0
        return pl.BlockSpec((tr, tc), lambda *g: (rc(*g)[0], rc(*g)[1] + col_off // tc))
    return (arr, rows, ncols, spec, None if col_off == 0 else col_off)


def m3(arr, i):
    def spec(tr, tc, rc):
        return pl.BlockSpec((None, tr, tc), lambda *g: (i,) + tuple(rc(*g)))
    return (arr, arr.shape[1], arr.shape[2], spec, None)


def mcs(arr, i):
    cs = arr.shape[3]

    def spec(tr, tc, rc):
        n = cs // tc
        return pl.BlockSpec((None, None, tr, tc),
                            lambda *g: (i, rc(*g)[1] // n, rc(*g)[0], rc(*g)[1] % n))
    return (arr, arr.shape[2], 4 * cs, spec, cs)


def out2(rows, cols):
    def spec(tr, tc, rc):
        return pl.BlockSpec((tr, tc), lambda *g: tuple(rc(*g)))
    return ((rows, cols), spec, None)


def outcs(rows, cs):
    def spec(tr, tc, rc):
        n = cs // tc
        return pl.BlockSpec((None, tr, tc), lambda *g: (rc(*g)[1] // n, rc(*g)[0], rc(*g)[1] % n))
    return ((4, rows, cs), spec, cs)


def _both(a, b):
    if a is None:
        return b
    if b is None:
        return a
    return math.gcd(a, b)


def mm(a, b, mode, name, add=None, out=None, out_dtype=F32):
    a_arr, a_r, a_c, a_spec, a_cc = a
    b_arr, b_r, b_c, b_spec, b_cc = b
    if mode == "nn":
        m, k, n = a_r, a_c, b_c
        assert b_r == k
        ccm, cck, ccn = None, a_cc, b_cc
    elif mode == "nt":
        m, k, n = a_r, a_c, b_r
        assert b_c == k
        ccm, cck, ccn = None, _both(a_cc, b_cc), None
    else:
        m, k, n = a_c, a_r, b_c
        assert b_r == k
        ccm, cck, ccn = a_cc, None, b_cc
    out = out2(m, n) if out is None else out
    o_shape, o_spec, o_cc = out
    ccn = _both(ccn, o_cc)
    if add is not None:
        ccn = _both(ccn, add[4])
    o_bytes = jnp.dtype(out_dtype).itemsize + (0 if add is None else add[0].dtype.itemsize)
    tm, tn, tk = _mm_tiles(m, n, k, ccm, ccn, cck, a_arr.dtype.itemsize, b_arr.dtype.itemsize, o_bytes)
    nk = k // tk
    if mode == "nn":
        in_specs = [a_spec(tm, tk, lambda i, j, kk: (i, kk)), b_spec(tk, tn, lambda i, j, kk: (kk, j))]
    elif mode == "nt":
        in_specs = [a_spec(tm, tk, lambda i, j, kk: (i, kk)), b_spec(tn, tk, lambda i, j, kk: (j, kk))]
    else:
        in_specs = [a_spec(tk, tm, lambda i, j, kk: (kk, i)), b_spec(tk, tn, lambda i, j, kk: (kk, j))]
    args = [a_arr, b_arr]
    if add is not None:
        in_specs.append(add[3](tm, tn, lambda i, j, kk: (i, j)))
        args.append(add[0])
    dims = _DIMS[mode]
    has_add = add is not None

    def body(*refs):
        a_ref, b_ref = refs[0], refs[1]
        add_ref = refs[2] if has_add else None
        prod = _dot(a_ref[...].astype(BF16), b_ref[...].astype(BF16), dims)
        if nk == 1:
            o_ref = refs[-1]
            if has_add:
                prod = prod + add_ref[...].astype(F32)
            o_ref[...] = prod.astype(o_ref.dtype)
            return
        o_ref, acc = refs[-2], refs[-1]
        kk = pl.program_id(2)

        @pl.when(kk == 0)
        def _():
            acc[...] = prod

        @pl.when(kk > 0)
        def _():
            acc[...] += prod

        @pl.when(kk == nk - 1)
        def _():
            r = acc[...]
            if has_add:
                r = r + add_ref[...].astype(F32)
            o_ref[...] = r.astype(o_ref.dtype)

    return pl.pallas_call(
        body, name=name, grid=(m // tm, n // tn, nk), in_specs=in_specs,
        out_specs=o_spec(tm, tn, lambda i, j, kk: (i, j)),
        out_shape=jax.ShapeDtypeStruct(o_shape, out_dtype),
        scratch_shapes=[pltpu.VMEM((tm, tn), F32)] if nk > 1 else [],
        compiler_params=_cparams(("parallel", "parallel", "arbitrary")),
    )(*args)


def rw(fn, ins, outs, name, rows, tr=256, consts=(), accs=()):
    n_in, n_c, n_o, n_a = len(ins), len(consts), len(outs), len(accs)
    in_specs = []
    for arr, off, width in ins:
        assert off % width == 0
        in_specs.append(pl.BlockSpec((tr, width), lambda i, o=off // width: (i, o)))
    for c in consts:
        in_specs.append(pl.BlockSpec(c.shape, lambda i: (0, 0)))
    out_specs = [pl.BlockSpec((tr, w), lambda i: (i, 0)) for w, _ in outs]
    out_specs += [pl.BlockSpec(s, lambda i: (0, 0)) for s in accs]
    out_shape = [jax.ShapeDtypeStruct((rows, w), dt) for w, dt in outs]
    out_shape += [jax.ShapeDtypeStruct(s, F32) for s in accs]

    def body(*refs):
        vals = [r[...] for r in refs[:n_in + n_c]]
        o_refs = refs[n_in + n_c:n_in + n_c + n_o]
        a_refs = refs[n_in + n_c + n_o:]
        res = fn(*vals)
        for r, v in zip(o_refs, res[:n_o]):
            r[...] = v.astype(r.dtype)
        if n_a:
            @pl.when(pl.program_id(0) == 0)
            def _():
                for r in a_refs:
                    r[...] = jnp.zeros_like(r)
            for r, v in zip(a_refs, res[n_o:]):
                r[...] += v

    res = pl.pallas_call(
        body, name=name, grid=(rows // tr,), in_specs=in_specs, out_specs=out_specs,
        out_shape=out_shape,
        compiler_params=_cparams(("arbitrary",) if n_a else ("parallel",)),
    )(*[a for a, _, _ in ins], *consts)
    return res


def _rstd(x):
    return lax.rsqrt(jnp.mean(x * x, axis=-1, keepdims=True) + EPS)


def rms_fwd(x, g, name):
    def fn(xv, gv):
        xv = xv.astype(F32)
        return (xv * _rstd(xv) * gv,)
    return rw(fn, [(x, 0, D)], [(D, BF16)], name, x.shape[0], consts=[g])[0]


def _rms_bwd_math(xv, dy, gv):
    r = _rstd(xv)
    dyg = dy * gv
    dx = r * dyg - xv * (r * r * r / D) * jnp.sum(dyg * xv, axis=-1, keepdims=True)
    dg = jnp.sum(dy * xv * r, axis=0, keepdims=True)
    return dx, dg


def rms_bwd(x, dy, dres, g, name):
    def fn(xv, dyv, drv, gv):
        dx, dg = _rms_bwd_math(xv, dyv, gv)
        return dx + drv, dg
    return rw(fn, [(x, 0, D), (dy, 0, D), (dres, 0, D)], [(D, F32)], name, x.shape[0],
              consts=[g], accs=[(1, D)])


def final_loss(x, tgt, g):
    def fn(xv, tv, gv):
        e = xv * _rstd(xv) * gv - tv
        loss = 0.5 * jnp.sum(jnp.sum(e * e, axis=-1, keepdims=True), axis=0, keepdims=True) / D
        dx, dg = _rms_bwd_math(xv, e / D, gv)
        return dx, loss, dg
    return rw(fn, [(x, 0, D), (tgt, 0, D)], [(D, F32)], "final_loss", S, consts=[g],
              accs=[(1, 1), (1, D)])


def _attn_bias(bias_ref):
    ii = lax.broadcasted_iota(jnp.int32, (2 * QB, 2 * QB), 0) % QB
    jj = lax.broadcasted_iota(jnp.int32, (2 * QB, 2 * QB), 1)
    dist = ii + QB - jj
    band = (dist >= 0) & (dist <= QB)
    bias_ref[1] = jnp.where(band, 0.0, NEG)
    bias_ref[0] = jnp.where(band & (jj >= QB), 0.0, NEG)


def _two_heads(x, m0):
    return jnp.concatenate([jnp.where(m0, x, 0.0), jnp.where(m0, 0.0, x)], axis=0)


def _per_head(col, m0):
    return jnp.where(m0, col[:QB], col[QB:])


def _attn_rows(idx, d):
    if d == 1:
        b = idx
        cur = pl.ds(pl.multiple_of(b * QB, QB), QB)
        prev = pl.ds(pl.multiple_of(jnp.maximum(b - 1, 0) * QB, QB), QB)
    else:
        r, b = lax.rem(idx, d), lax.div(idx, d)
        cur = pl.ds(r + b * (QB * d), QB, stride=d)
        prev = pl.ds(r + jnp.maximum(b - 1, 0) * (QB * d), QB, stride=d)
    return cur, prev, b


NBLK = S // QB
GROUP = 4


def _colblk(off):
    return pl.BlockSpec((S, 128), lambda hp: (0, off * 8 + hp))


def attn_fwd(z):
    def body(q_ref, k_ref, v_ref, g_ref, o_ref, l_ref, a_ref, os, ls, bias):
        _attn_bias(bias)
        m0 = lax.broadcasted_iota(jnp.int32, (1, 128), 1) < 64
        for pi, d in enumerate(PATTERNS):
            def load(idx, d=d):
                cur, prev, b = _attn_rows(idx, d)
                return cur, (q_ref[cur, :], k_ref[prev, :], k_ref[cur, :], v_ref[prev, :], v_ref[cur, :],
                             bias[jnp.minimum(b, 1)])

            def block(q, kp, kc, vp, vc, bs):
                qq = _two_heads(q * 0.125, m0).astype(BF16)
                k = jnp.concatenate([kp, kc], axis=0).astype(BF16)
                s = _nt(qq, k) + bs
                mx = jnp.max(s, axis=-1, keepdims=True)
                p = jnp.exp(s - mx)
                den = jnp.sum(p, axis=-1, keepdims=True)
                pb = p.astype(BF16)
                vv = _two_heads(jnp.concatenate([vp, vc], axis=0), m0).astype(BF16)
                o = _nn(jnp.concatenate([pb[:QB], pb[QB:]], axis=1), vv)
                return o * _per_head(1.0 / den, m0), _per_head(mx + jnp.log(den), m0)

            def step(i, carry, pi=pi):
                loaded = [load(i * GROUP + u) for u in range(GROUP)]
                done = [block(*vals) for _, vals in loaded]
                for (cur, _), (o, l) in zip(loaded, done):
                    os[pi, cur, :] = o
                    ls[pi, cur, :] = l
                return carry
            lax.fori_loop(0, NBLK // GROUP, step, 0)
        l1, l2, l3 = ls[0], ls[1], ls[2]
        mx = jnp.maximum(jnp.maximum(l1, l2), l3)
        e1, e2, e3 = jnp.exp(l1 - mx), jnp.exp(l2 - mx), jnp.exp(l3 - mx)
        tot = e1 + e2 + e3
        o = (os[0] * e1 + os[1] * e2 + os[2] * e3) / tot
        ga = g_ref[...]
        o_ref[...] = o
        l_ref[...] = mx + jnp.log(tot)
        a_ref[...] = (o * (ga * _sig(ga))).astype(a_ref.dtype)

    out = pl.BlockSpec((S, 128), lambda hp: (0, hp))
    return pl.pallas_call(
        body, name="attn_fwd", grid=(8,),
        in_specs=[_colblk(0), _colblk(1), _colblk(2), _colblk(3)], out_specs=[out] * 3,
        out_shape=[jax.ShapeDtypeStruct((S, D), F32), jax.ShapeDtypeStruct((S, D), F32),
                   jax.ShapeDtypeStruct((S, 2 * D), BF16)],
        scratch_shapes=[pltpu.VMEM((3, S, 128), F32), pltpu.VMEM((3, S, 128), F32),
                        pltpu.VMEM((2, 2 * QB, 2 * QB), F32)],
        compiler_params=_cparams(("parallel",)),
    )(z, z, z, z)


def attn_bwd(z, d_cat, o, lse):
    def body(q_ref, k_ref, v_ref, g_ref, da_ref, o_ref, l_ref, dq_ref, dk_ref, dv_ref, dg_ref, do_s, pr_s, bias):
        _attn_bias(bias)
        m0 = lax.broadcasted_iota(jnp.int32, (1, 128), 1) < 64
        ga = g_ref[...]
        sg = _sig(ga)
        da = da_ref[...]
        ov = o_ref[...]
        do = da * (ga * sg)
        dg_ref[...] = da * ov * (sg * (1.0 + ga * (1.0 - sg)))
        do_s[...] = do
        pr_s[...] = do * ov
        dq_ref[...] = jnp.zeros_like(dq_ref)
        dk_ref[...] = jnp.zeros_like(dk_ref)
        dv_ref[...] = jnp.zeros_like(dv_ref)
        for d in PATTERNS:
            def load(idx, d=d):
                cur, prev, b = _attn_rows(idx, d)
                return (cur, prev), (q_ref[cur, :], k_ref[prev, :], k_ref[cur, :], v_ref[prev, :], v_ref[cur, :],
                                     do_s[cur, :], pr_s[cur, :], l_ref[cur, :], bias[jnp.minimum(b, 1)])

            def block(q, kp, kc, vp, vc, dof, prod, lp, bs):
                qq = _two_heads(q * 0.125, m0).astype(BF16)
                kf = jnp.concatenate([kp, kc], axis=0)
                k = kf.astype(BF16)
                v = jnp.concatenate([vp, vc], axis=0).astype(BF16)
                dd = _two_heads(dof, m0).astype(BF16)
                lh = jnp.max(jnp.concatenate([jnp.where(m0, lp, -jnp.inf), jnp.where(m0, -jnp.inf, lp)], axis=0),
                             axis=-1, keepdims=True)
                delta = jnp.sum(_two_heads(prod, m0), axis=-1, keepdims=True)
                p = jnp.exp(_nt(qq, k) + bs - lh)
                ds = (p * (_nt(dd, v) - delta)).astype(BF16)
                dq = _nn(jnp.concatenate([ds[:QB], ds[QB:]], axis=1), _two_heads(kf, m0).astype(BF16))
                return dq * 0.125, _tn(ds, qq), _tn(p.astype(BF16), dd)

            def step(i, carry):
                loaded = [load(i * GROUP + u) for u in range(GROUP)]
                done = [block(*vals) for _, vals in loaded]
                for ((cur, prev), _), (dq, dk, dv) in zip(loaded, done):
                    dq_ref[cur, :] = dq_ref[cur, :] + dq
                    dk_ref[prev, :] = dk_ref[prev, :] + dk[:QB]
                    dv_ref[prev, :] = dv_ref[prev, :] + dv[:QB]
                    dk_ref[cur, :] = dk_ref[cur, :] + dk[QB:]
                    dv_ref[cur, :] = dv_ref[cur, :] + dv[QB:]
                return carry
            lax.fori_loop(0, NBLK // GROUP, step, 0)

    blk = pl.BlockSpec((S, 128), lambda hp: (0, hp))
    return pl.pallas_call(
        body, name="attn_bwd", grid=(8,),
        in_specs=[_colblk(0), _colblk(1), _colblk(2), _colblk(3), blk, blk, blk], out_specs=[blk] * 4,
        out_shape=[jax.ShapeDtypeStruct((S, D), F32)] * 4,
        scratch_shapes=[pltpu.VMEM((S, 128), F32), pltpu.VMEM((S, 128), F32), pltpu.VMEM((2, 2 * QB, 2 * QB), F32)],
        compiler_params=_cparams(("parallel",)),
    )(z, z, z, z, d_cat, o, lse)


def assemble_dz_even(parts):
    def body(*refs):
        o_ref = refs[-1]
        for j in range(6):
            o_ref[:, j * D:(j + 1) * D] = refs[j][...].astype(o_ref.dtype)
    tr = 256
    blk = pl.BlockSpec((tr, D), lambda i: (i, 0))
    return pl.pallas_call(
        body, name="assemble_dz_even", grid=(S // tr,), in_specs=[blk] * 6,
        out_specs=pl.BlockSpec((tr, 6 * D), lambda i: (i, 0)),
        out_shape=jax.ShapeDtypeStruct((S, 6 * D), BF16),
        compiler_params=_cparams(("parallel",)),
    )(*parts)


def _pool_window(g):
    return jnp.where(g == 0, 2.0, jnp.where(g == 1, 4.0, jnp.where(g == 2, 8.0, 16.0)))


def _pool_sel(g, levels):
    return jnp.where(g == 0, levels[0], jnp.where(g == 1, levels[1], jnp.where(g == 2, levels[2], levels[3])))


def _pool_fwd_math(v, g):
    t = lax.broadcasted_iota(jnp.int32, (S, 1), 0)
    s = v
    levels = []
    for k in (1, 2, 4, 8):
        s = s + jnp.where(t >= k, pltpu.roll(s, k, 0), 0.0)
        levels.append(s)
    cnt = jnp.minimum((t + 1).astype(F32), _pool_window(g))
    return _pool_sel(g, levels) / cnt - v, cnt


def pool_fwd(z, pw, ps, cat):
    def body(v_ref, g_ref, pw_ref, ps_ref, cat_ref, o_ref):
        g = pl.program_id(0)
        pooled, _ = _pool_fwd_math(v_ref[...], g)
        mixed = _nn(pooled.astype(BF16), pw_ref[...].astype(BF16))
        gb = g_ref[...]
        o_ref[...] = (mixed * ps_ref[...] * (gb * _sig(gb))).astype(o_ref.dtype)

    return pl.pallas_call(
        body, name="pool_fwd", grid=(4,),
        in_specs=[pl.BlockSpec((S, 256), lambda g: (0, 16 + g)),
                  pl.BlockSpec((S, 256), lambda g: (0, 20 + g)),
                  pl.BlockSpec((None, 256, 256), lambda g: (g, 0, 0)),
                  pl.BlockSpec((1, 256), lambda g: (0, g)), pl.BlockSpec(memory_space=pl.ANY)],
        out_specs=pl.BlockSpec((S, 256), lambda g: (0, 4 + g)),
        out_shape=jax.ShapeDtypeStruct((S, 2 * D), BF16),
        input_output_aliases={4: 0},
        compiler_params=_cparams(("parallel",)),
    )(z, z, pw, ps, cat)


def pool_bwd(z, d_cat, pw, ps):
    def body(v_ref, g_ref, d_ref, pw_ref, ps_ref, dv_ref, dg_ref, dpw_ref, dps_ref):
        g = pl.program_id(0)
        v = v_ref[...]
        pooled, cnt = _pool_fwd_math(v, g)
        pwb = pw_ref[...].astype(BF16)
        pb = pooled.astype(BF16)
        mixed = _nn(pb, pwb)
        gb = g_ref[...]
        sg = _sig(gb)
        dout = d_ref[...]
        sc = ps_ref[...]
        dg_ref[...] = dout * mixed * sc * (sg * (1.0 + gb * (1.0 - sg)))
        dms = dout * (gb * sg)
        dps_ref[...] = jnp.sum(dms * mixed, axis=0, keepdims=True)
        dmx = (dms * sc).astype(BF16)
        dpw_ref[...] = _tn(pb, dmx)
        dpooled = _nt(dmx, pwb)
        t = lax.broadcasted_iota(jnp.int32, (S, 1), 0)
        s = dpooled / cnt
        levels = []
        for k in (1, 2, 4, 8):
            s = s + jnp.where(t < S - k, pltpu.roll(s, S - k, 0), 0.0)
            levels.append(s)
        dv_ref[...] = _pool_sel(g, levels) - dpooled

    return pl.pallas_call(
        body, name="pool_bwd", grid=(4,),
        in_specs=[pl.BlockSpec((S, 256), lambda g: (0, 16 + g)),
                  pl.BlockSpec((S, 256), lambda g: (0, 20 + g)),
                  pl.BlockSpec((S, 256), lambda g: (0, 4 + g)),
                  pl.BlockSpec((None, 256, 256), lambda g: (g, 0, 0)),
                  pl.BlockSpec((1, 256), lambda g: (0, g))],
        out_specs=[pl.BlockSpec((S, 256), lambda g: (0, g)),
                   pl.BlockSpec((S, 256), lambda g: (0, g)),
                   pl.BlockSpec((None, 256, 256), lambda g: (g, 0, 0)),
                   pl.BlockSpec((1, 256), lambda g: (0, g))],
        out_shape=[jax.ShapeDtypeStruct((S, D), F32), jax.ShapeDtypeStruct((S, D), F32),
                   jax.ShapeDtypeStruct((4, 256, 256), F32), jax.ShapeDtypeStruct((1, D), F32)],
        compiler_params=_cparams(("parallel",)),
    )(z, z, d_cat, pw, ps)


CH = 128


def _sgu_common(v, lng, lnb, w_ref):
    mu = jnp.mean(v, axis=-1, keepdims=True)
    vc = v - mu
    rs = lax.rsqrt(jnp.mean(vc * vc, axis=-1, keepdims=True) + EPS)
    xhat = vc * rs
    vn = (xhat * lng + lnb).astype(BF16)
    ri = lax.broadcasted_iota(jnp.int32, (CH, CH), 0)
    ci = lax.broadcasted_iota(jnp.int32, (CH, CH), 1)
    tril = ri >= ci
    ws = [jnp.where(tril, w_ref[g], 0.0).astype(BF16) for g in range(4)]
    return xhat, rs, vn, tril, ws


def _zspec(off):
    return pl.BlockSpec((CH, D), lambda c: (c, off))


def _full(shape):
    return pl.BlockSpec(shape, lambda c: (0,) * len(shape))


def sgu_fwd(z, lng, lnb, w, bfull):
    def body(u_ref, v_ref, g_ref, lng_ref, lnb_ref, w_ref, b_ref, o_ref):
        _, _, vn, _, ws = _sgu_common(v_ref[...], lng_ref[...], lnb_ref[...], w_ref)
        for g in range(4):
            sl = slice(g * 256, (g + 1) * 256)
            mixed = _nn(ws[g], vn[:, sl]) + b_ref[:, sl]
            gc = g_ref[:, sl]
            o_ref[:, sl] = (u_ref[:, sl] * mixed * (gc * _sig(gc))).astype(o_ref.dtype)

    return pl.pallas_call(
        body, name="sgu_fwd", grid=(S // CH,),
        in_specs=[_zspec(0), _zspec(1), _zspec(2), _full((1, D)), _full((1, D)),
                  _full((4, CH, CH)), _full((CH, D))],
        out_specs=pl.BlockSpec((CH, D), lambda c: (c, 0)),
        out_shape=jax.ShapeDtypeStruct((S, D), BF16),
        compiler_params=_cparams(("parallel",)),
    )(z, z, z, lng, lnb, w, bfull)


def sgu_bwd(z, d_cat, lng, lnb, w, bfull):
    def body(u_ref, v_ref, g_ref, d_ref, lng_ref, lnb_ref, w_ref, b_ref,
             du_ref, dv_ref, dg_ref, dw_ref, db_ref, dlg_ref, dlb_ref):
        @pl.when(pl.program_id(0) == 0)
        def _():
            dw_ref[...] = jnp.zeros_like(dw_ref)
            db_ref[...] = jnp.zeros_like(db_ref)
            dlg_ref[...] = jnp.zeros_like(dlg_ref)
            dlb_ref[...] = jnp.zeros_like(dlb_ref)

        lng = lng_ref[...]
        xhat, rs, vn, tril, ws = _sgu_common(v_ref[...], lng, lnb_ref[...], w_ref)
        lane = lax.broadcasted_iota(jnp.int32, (1, 128), 1)
        db = jnp.zeros((CH, 128), F32)
        dvn_parts = []
        for g in range(4):
            sl = slice(g * 256, (g + 1) * 256)
            mixed = _nn(ws[g], vn[:, sl]) + b_ref[:, sl]
            gc = g_ref[:, sl]
            sg = _sig(gc)
            u = u_ref[:, sl]
            dc = d_ref[:, sl]
            du_ref[:, sl] = dc * mixed * (gc * sg)
            dg_ref[:, sl] = dc * u * mixed * (sg * (1.0 + gc * (1.0 - sg)))
            dmx = dc * u * (gc * sg)
            db = db + jnp.where(lane == g, jnp.sum(dmx, axis=-1, keepdims=True), 0.0)
            dmb = dmx.astype(BF16)
            dw_ref[g] += jnp.where(tril, _nt(dmb, vn[:, sl]), 0.0)
            dvn_parts.append(_tn(ws[g], dmb))
        db_ref[...] += db
        dvn = jnp.concatenate(dvn_parts, axis=1)
        dlb_ref[...] += jnp.sum(dvn, axis=0, keepdims=True)
        dlg_ref[...] += jnp.sum(dvn * xhat, axis=0, keepdims=True)
        dxh = dvn * lng
        dv_ref[...] = rs * (dxh - jnp.mean(dxh, axis=-1, keepdims=True)
                            - xhat * jnp.mean(dxh * xhat, axis=-1, keepdims=True))

    row = pl.BlockSpec((CH, D), lambda c: (c, 0))
    return pl.pallas_call(
        body, name="sgu_bwd", grid=(S // CH,),
        in_specs=[_zspec(0), _zspec(1), _zspec(2), row, _full((1, D)), _full((1, D)),
                  _full((4, CH, CH)), _full((CH, D))],
        out_specs=[row, row, row, _full((4, CH, CH)), _full((CH, 128)), _full((1, D)), _full((1, D))],
        out_shape=[jax.ShapeDtypeStruct((S, D), F32)] * 3
        + [jax.ShapeDtypeStruct((4, CH, CH), F32), jax.ShapeDtypeStruct((CH, 128), F32),
           jax.ShapeDtypeStruct((1, D), F32), jax.ShapeDtypeStruct((1, D), F32)],
        compiler_params=_cparams(("arbitrary",)),
    )(z, z, z, d_cat, lng, lnb, w, bfull)


TB = 256


def _cmul(ar, ai, br, bi):
    return ar * br - ai * bi, ar * bi + ai * br


def _scan_consts(ar, ai, reverse):
    a2 = _cmul(ar, ai, ar, ai)
    a4 = _cmul(*a2, *a2)
    row = lax.broadcasted_iota(jnp.int32, (8, NS), 0)
    pr = jnp.zeros((8, NS), F32)
    pi = jnp.zeros((8, NS), F32)
    cr, ci = ar, ai
    for r in range(8):
        sel = row == (7 - r if reverse else r)
        pr = jnp.where(sel, cr, pr)
        pi = jnp.where(sel, ci, pi)
        cr, ci = _cmul(cr, ci, ar, ai)
    return ((ar, ai), a2, a4), (pr, pi), row


def scan_fwd(bu, abr, abi):
    def body(bu_ref, ar_ref, ai_ref, h_ref, car, cai):
        @pl.when(pl.program_id(0) == 0)
        def _():
            car[...] = jnp.zeros_like(car)
            cai[...] = jnp.zeros_like(cai)

        pows, (pr, pi), row = _scan_consts(ar_ref[...], ai_ref[...], False)

        def tile(t, carry):
            c_r, c_i = carry
            rows = pl.ds(pl.multiple_of(t * 8, 8), 8)
            xr = bu_ref[rows, 0:NS]
            xi = bu_ref[rows, NS:2 * NS]
            for k, (kr, ki) in zip((1, 2, 4), pows):
                sr = jnp.where(row >= k, pltpu.roll(xr, k, 0), 0.0)
                si = jnp.where(row >= k, pltpu.roll(xi, k, 0), 0.0)
                xr, xi = xr + kr * sr - ki * si, xi + kr * si + ki * sr
            xr, xi = xr + pr * c_r - pi * c_i, xi + pr * c_i + pi * c_r
            h_ref[rows, 0:NS] = xr
            h_ref[rows, NS:2 * NS] = xi
            return (jnp.broadcast_to(xr[7:8, :], (8, NS)), jnp.broadcast_to(xi[7:8, :], (8, NS)))

        c_r, c_i = lax.fori_loop(0, TB // 8, tile, (car[...], cai[...]))
        car[...] = c_r
        cai[...] = c_i

    return pl.pallas_call(
        body, name="s5_scan_fwd", grid=(S // TB,),
        in_specs=[pl.BlockSpec((TB, 2 * NS), lambda i: (i, 0)),
                  pl.BlockSpec((1, NS), lambda i: (0, 0)), pl.BlockSpec((1, NS), lambda i: (0, 0))],
        out_specs=pl.BlockSpec((TB, 2 * NS), lambda i: (i, 0)),
        out_shape=jax.ShapeDtypeStruct((S, 2 * NS), F32),
        scratch_shapes=[pltpu.VMEM((8, NS), F32), pltpu.VMEM((8, NS), F32)],
        compiler_params=_cparams(("arbitrary",)),
    )(bu, abr, abi)


def scan_bwd(eta, h, abr, abi):
    nt = S // TB

    def body(e_ref, h_ref, ar_ref, ai_ref, l_ref, da_ref, car, cai):
        @pl.when(pl.program_id(0) == 0)
        def _():
            car[...] = jnp.zeros_like(car)
            cai[...] = jnp.zeros_like(cai)
            da_ref[...] = jnp.zeros_like(da_ref)

        pows, (pr, pi), row = _scan_consts(ar_ref[...], -ai_ref[...], True)

        def tile(tt, carry):
            c_r, c_i, acr, aci = carry
            t = TB // 8 - 1 - tt
            rows = pl.ds(pl.multiple_of(t * 8, 8), 8)
            xr = e_ref[rows, 0:NS]
            xi = e_ref[rows, NS:2 * NS]
            for k, (kr, ki) in zip((1, 2, 4), pows):
                sr = jnp.where(row < 8 - k, pltpu.roll(xr, 8 - k, 0), 0.0)
                si = jnp.where(row < 8 - k, pltpu.roll(xi, 8 - k, 0), 0.0)
                xr, xi = xr + kr * sr - ki * si, xi + kr * si + ki * sr
            xr, xi = xr + pr * c_r - pi * c_i, xi + pr * c_i + pi * c_r
            l_ref[rows, 0:NS] = xr
            l_ref[rows, NS:2 * NS] = xi
            nr = jnp.where(row < 7, pltpu.roll(xr, 7, 0), c_r)
            ni = jnp.where(row < 7, pltpu.roll(xi, 7, 0), c_i)
            hr = h_ref[rows, 0:NS]
            hi = h_ref[rows, NS:2 * NS]
            acr = acr + hr * nr + hi * ni
            aci = aci + hr * ni - hi * nr
            return (jnp.broadcast_to(xr[0:1, :], (8, NS)), jnp.broadcast_to(xi[0:1, :], (8, NS)), acr, aci)

        zero = jnp.zeros((8, NS), F32)
        c_r, c_i, acr, aci = lax.fori_loop(0, TB // 8, tile, (car[...], cai[...], zero, zero))
        car[...] = c_r
        cai[...] = c_i
        da_ref[:, 0:NS] += acr
        da_ref[:, NS:2 * NS] += aci

    rev = pl.BlockSpec((TB, 2 * NS), lambda i: (nt - 1 - i, 0))
    return pl.pallas_call(
        body, name="s5_scan_bwd", grid=(nt,),
        in_specs=[rev, rev, pl.BlockSpec((1, NS), lambda i: (0, 0)), pl.BlockSpec((1, NS), lambda i: (0, 0))],
        out_specs=[rev, pl.BlockSpec((8, 2 * NS), lambda i: (0, 0))],
        out_shape=[jax.ShapeDtypeStruct((S, 2 * NS), F32), jax.ShapeDtypeStruct((8, 2 * NS), F32)],
        scratch_shapes=[pltpu.VMEM((8, NS), F32), pltpu.VMEM((8, NS), F32)],
        compiler_params=_cparams(("arbitrary",)),
    )(eta, h, abr, abi)


GC = 0.7978845608028654
GA = 0.044715


def s5_post(hc, z, dskip):
    def fn(hv, xd, dv):
        y = hv + dv * xd
        return y, 0.5 * y * (1.0 + jnp.tanh(GC * (y + GA * y * y * y)))
    return rw(fn, [(hc, 0, 512), (z, 3072, 512)], [(512, F32), (512, BF16)], "s5_post", S, consts=[dskip])


def s5_post_bwd(dyg, ypre, z, dskip):
    def fn(dy, y, xd, dv):
        th = jnp.tanh(GC * (y + GA * y * y * y))
        dg = 0.5 * (1.0 + th) + 0.5 * y * (1.0 - th * th) * GC * (1.0 + 3.0 * GA * y * y)
        dyp = dy * dg
        return dyp, dyp * dv, jnp.sum(dyp * xd, axis=0, keepdims=True)
    return rw(fn, [(dyg, 0, 512), (ypre, 0, 512), (z, 3072, 512)], [(512, BF16), (512, F32)],
              "s5_post_bwd", S, consts=[dskip], accs=[(1, 512)])


def glu_fwd(t, z, c_out):
    def fn(t1, t2, gd, co):
        return (jnp.concatenate([co, (t1 * _sig(t2) * (gd * _sig(gd))).astype(BF16)], axis=1),)
    return rw(fn, [(t, 0, 512), (t, 512, 512), (z, 3584, 512), (c_out, 0, D)], [(D + 512, BF16)], "glu_fwd", S)[0]


def glu_bwd(t, z, d_cat):
    def fn(t1, t2, gd, dd):
        s2, sg = _sig(t2), _sig(gd)
        sl = gd * sg
        return (jnp.concatenate([dd * s2 * sl, dd * t1 * s2 * (1.0 - s2) * sl], axis=1),
                dd * t1 * s2 * (sg * (1.0 + gd * (1.0 - sg))))
    return rw(fn, [(t, 0, 512), (t, 512, 512), (z, 3584, 512), (d_cat, 1024, 512)],
              [(D, BF16), (512, F32)], "glu_bwd", S)


def assemble_dz_odd(du, dv, dgc, dxd, dgd):
    def body(a, b, c, d, e, o_ref):
        o_ref[:, 0:D] = a[...].astype(BF16)
        o_ref[:, D:2 * D] = b[...].astype(BF16)
        o_ref[:, 2 * D:3 * D] = c[...].astype(BF16)
        o_ref[:, 3 * D:3 * D + 512] = d[...].astype(BF16)
        o_ref[:, 3 * D + 512:4 * D] = e[...].astype(BF16)
    tr = 256
    blk = pl.BlockSpec((tr, D), lambda i: (i, 0))
    half = pl.BlockSpec((tr, 512), lambda i: (i, 0))
    return pl.pallas_call(
        body, name="assemble_dz_odd", grid=(S // tr,), in_specs=[blk, blk, blk, half, half],
        out_specs=pl.BlockSpec((tr, 4 * D), lambda i: (i, 0)),
        out_shape=jax.ShapeDtypeStruct((S, 4 * D), BF16),
        compiler_params=_cparams(("parallel",)),
    )(du, dv, dgc, dxd, dgd)


TQ = 256


def _xattn_probs(qh, kh):
    s = _nt(qh, kh) * 0.0625
    p = jnp.exp(s - jnp.max(s, axis=-1, keepdims=True))
    return p / jnp.sum(p, axis=-1, keepdims=True)


def xattn_fwd(q, kv):
    def body(q_ref, kv_ref, o_ref):
        for h in range(4):
            sl = slice(h * 256, (h + 1) * 256)
            p = _xattn_probs(q_ref[:, sl].astype(BF16), kv_ref[:, sl].astype(BF16))
            vh = kv_ref[:, D + h * 256:D + (h + 1) * 256].astype(BF16)
            o_ref[:, sl] = _nn(p.astype(BF16), vh).astype(o_ref.dtype)

    return pl.pallas_call(
        body, name="xattn_fwd", grid=(S // TQ,),
        in_specs=[pl.BlockSpec((TQ, D), lambda i: (i, 0)), pl.BlockSpec((MEM, 2 * D), lambda i: (0, 0))],
        out_specs=pl.BlockSpec((TQ, D), lambda i: (i, 0)),
        out_shape=jax.ShapeDtypeStruct((S, D), BF16),
        compiler_params=_cparams(("parallel",)),
    )(q, kv)


def xattn_bwd(q, kv, d_o):
    def body(q_ref, kv_ref, do_ref, dq_ref, dkv_ref):
        @pl.when(pl.program_id(0) == 0)
        def _():
            dkv_ref[...] = jnp.zeros_like(dkv_ref)

        for h in range(4):
            sl = slice(h * 256, (h + 1) * 256)
            vs = slice(D + h * 256, D + (h + 1) * 256)
            qh = q_ref[:, sl].astype(BF16)
            kh = kv_ref[:, sl].astype(BF16)
            vh = kv_ref[:, vs].astype(BF16)
            doh = do_ref[:, sl].astype(BF16)
            p = _xattn_probs(qh, kh)
            dp = _nt(doh, vh)
            ds = (p * (dp - jnp.sum(p * dp, axis=-1, keepdims=True)) * 0.0625).astype(BF16)
            dq_ref[:, sl] = _nn(ds, kh).astype(dq_ref.dtype)
            dkv_ref[:, sl] += _tn(ds, qh)
            dkv_ref[:, vs] += _tn(p.astype(BF16), doh)

    return pl.pallas_call(
        body, name="xattn_bwd", grid=(S // TQ,),
        in_specs=[pl.BlockSpec((TQ, D), lambda i: (i, 0)), pl.BlockSpec((MEM, 2 * D), lambda i: (0, 0)),
                  pl.BlockSpec((TQ, D), lambda i: (i, 0))],
        out_specs=[pl.BlockSpec((TQ, D), lambda i: (i, 0)), pl.BlockSpec((MEM, 2 * D), lambda i: (0, 0))],
        out_shape=[jax.ShapeDtypeStruct((S, D), BF16), jax.ShapeDtypeStruct((MEM, 2 * D), F32)],
        compiler_params=_cparams(("arbitrary",)),
    )(q, kv, d_o)


def _s5_disc(a_re, a_im, log_dt, b_re, b_im):
    dt = jnp.exp(log_dt)[:, None]
    mag = jnp.exp(dt * a_re)
    abr = mag * jnp.cos(dt * a_im)
    abi = mag * jnp.sin(dt * a_im)
    nr, ni = abr - 1.0, abi
    inv = 1.0 / (a_re * a_re + a_im * a_im)
    cr = (nr * a_re + ni * a_im) * inv
    ci = (ni * a_re - nr * a_im) * inv
    bbr = cr[..., None] * b_re - ci[..., None] * b_im
    bbi = cr[..., None] * b_im + ci[..., None] * b_re
    return abr, abi, bbr, bbi


def _blockdiag(t):
    g, a, b = t.shape
    eye = jnp.eye(g, dtype=t.dtype)
    return (eye[:, None, :, None] * t[:, :, None, :]).reshape(g * a, g * b)


def _blocks(mat, a, b):
    return jnp.einsum("gagb->gab", mat.reshape(NG, a, NG, b))


def _fwd_even(i, x, P):
    hn = rms_fwd(x, P["norm_ab"][i:i + 1], "rms_ab_fwd")
    z = mm(m2(hn), mcs(P["w_in_ab"], i), "nn", "in_ab")
    o, lse, cat = attn_fwd(z)
    cat = pool_fwd(z, P["pool_w"][i], P["pool_scale"][i:i + 1], cat)
    x_mid = mm(m2(cat), m3(P["w_out_ab"], i), "nn", "out_ab", add=m2(x))
    return x_mid, dict(x=x, hn=hn, z=z, o=o, lse=lse, cat=cat)


def _bwd_even(i, dx_mid, sv, P, G):
    z = sv["z"]
    d_cat = mm(m2(dx_mid), m3(P["w_out_ab"], i), "nt", "out_ab_dx")
    G["w_out_ab"][i] = mm(m2(sv["cat"]), m2(dx_mid), "tn", "out_ab_dw").reshape(4, 512, D)
    dq, dk, dv, dga = attn_bwd(z, d_cat, sv["o"], sv["lse"])
    dvb, dgb, dpw, dps = pool_bwd(z, d_cat, P["pool_w"][i], P["pool_scale"][i:i + 1])
    G["pool_w"][i] = dpw.reshape(4, 4, 64, 256).transpose(1, 0, 2, 3).reshape(4, 256, 256)
    G["pool_scale"][i] = dps[0]
    d_z = assemble_dz_even((dq, dk, dv, dga, dvb, dgb))
    d_hn = mm(m2(d_z), mcs(P["w_in_ab"], i), "nt", "in_ab_dx")
    G["w_in_ab"][i] = mm(m2(sv["hn"]), m2(d_z), "tn", "in_ab_dw", out=outcs(D, 1536))
    dx, dg = rms_bwd(sv["x"], d_hn, dx_mid, P["norm_ab"][i:i + 1], "rms_ab_bwd")
    G["norm_ab"][i] = dg[0]
    return dx


def _fwd_odd(i, x, P):
    hn = rms_fwd(x, P["norm_cd"][i:i + 1], "rms_cd_fwd")
    z = mm(m2(hn), mcs(P["w_in_cd"], i), "nn", "in_cd")
    bfull = jnp.repeat(P["sgu_b"][i].T, 256, axis=1)
    c_out = sgu_fwd(z, P["sgu_ln_g"][i:i + 1], P["sgu_ln_b"][i:i + 1], P["sgu_w"][i], bfull)
    disc, disc_vjp = jax.vjp(_s5_disc, P["s5_a_re"][i], P["s5_a_im"][i], P["s5_log_dt"][i],
                             P["s5_b_re"][i], P["s5_b_im"][i])
    abr, abi, bbr, bbi = disc
    bbd = jnp.concatenate([_blockdiag(bbr.transpose(0, 2, 1)), _blockdiag(bbi.transpose(0, 2, 1))], axis=1)
    cbd = jnp.concatenate([_blockdiag(P["s5_c_re"][i].transpose(0, 2, 1)),
                           -_blockdiag(P["s5_c_im"][i].transpose(0, 2, 1))], axis=0)
    abr, abi = abr.reshape(1, NS), abi.reshape(1, NS)
    bu = mm(m2(z, 3072, 512), m2(bbd), "nn", "s5_bu")
    h = scan_fwd(bu, abr, abi)
    hc = mm(m2(h), m2(cbd), "nn", "s5_hc")
    dskip = P["s5_d"][i:i + 1]
    ypre, yg = s5_post(hc, z, dskip)
    w12 = jnp.concatenate([P["glu_w1"][i], P["glu_w2"][i]], axis=1)
    t = mm(m2(yg), m2(w12), "nn", "glu_t")
    cat = glu_fwd(t, z, c_out)
    x_mid = mm(m2(cat), m3(P["w_out_cd"], i), "nn", "out_cd", add=m2(x))
    return x_mid, dict(x=x, hn=hn, z=z, bfull=bfull, disc_vjp=disc_vjp, bbd=bbd, cbd=cbd, abr=abr,
                       abi=abi, h=h, ypre=ypre, yg=yg, w12=w12, t=t, cat=cat, dskip=dskip)


def _bwd_odd(i, dx_mid, sv, P, G):
    z = sv["z"]
    d_cat = mm(m2(dx_mid), m3(P["w_out_cd"], i), "nt", "out_cd_dx")
    G["w_out_cd"][i] = mm(m2(sv["cat"]), m2(dx_mid), "tn", "out_cd_dw").reshape(4, 384, D)
    du, dv, dgc, dws, dbs, dlg, dlb = sgu_bwd(z, d_cat, P["sgu_ln_g"][i:i + 1], P["sgu_ln_b"][i:i + 1],
                                               P["sgu_w"][i], sv["bfull"])
    G["sgu_w"][i], G["sgu_b"][i] = dws, dbs[:, :4].T
    G["sgu_ln_g"][i], G["sgu_ln_b"][i] = dlg[0], dlb[0]
    dt, dgd = glu_bwd(sv["t"], z, d_cat)
    gw12 = mm(m2(sv["yg"]), m2(dt), "tn", "glu_dw")
    G["glu_w1"][i] = gw12[:, :512].reshape(4, 128, 512)
    G["glu_w2"][i] = gw12[:, 512:].reshape(4, 128, 512)
    dyg = mm(m2(dt), m2(sv["w12"]), "nt", "glu_dx")
    dypre, dxd1, dd = s5_post_bwd(dyg, sv["ypre"], z, sv["dskip"])
    G["s5_d"][i] = dd[0]
    gcbd = mm(m2(sv["h"]), m2(dypre), "tn", "s5_dc")
    G["s5_c_re"][i] = _blocks(gcbd[:NS], NP, NH).transpose(0, 2, 1)
    G["s5_c_im"][i] = -_blocks(gcbd[NS:], NP, NH).transpose(0, 2, 1)
    eta = mm(m2(dypre), m2(sv["cbd"]), "nt", "s5_eta")
    lam, dacc = scan_bwd(eta, sv["h"], sv["abr"], sv["abi"])
    gbbd = mm(m2(z, 3072, 512), m2(lam), "tn", "s5_db")
    dxd = mm(m2(lam), m2(sv["bbd"]), "nt", "s5_dx", add=m2(dxd1))
    dacc = jnp.sum(dacc, axis=0)
    d_bbr = _blocks(gbbd[:, :NS], NH, NP).transpose(0, 2, 1)
    d_bbi = _blocks(gbbd[:, NS:], NH, NP).transpose(0, 2, 1)
    (G["s5_a_re"][i], G["s5_a_im"][i], G["s5_log_dt"][i], G["s5_b_re"][i], G["s5_b_im"][i]) = sv["disc_vjp"](
        (dacc[:NS].reshape(NG, NP), dacc[NS:].reshape(NG, NP), d_bbr, d_bbi))
    d_z = assemble_dz_odd(du, dv, dgc, dxd, dgd)
    d_hn = mm(m2(d_z), mcs(P["w_in_cd"], i), "nt", "in_cd_dx")
    G["w_in_cd"][i] = mm(m2(sv["hn"]), m2(d_z), "tn", "in_cd_dw", out=outcs(D, 1024))
    dx, dg = rms_bwd(sv["x"], d_hn, dx_mid, P["norm_cd"][i:i + 1], "rms_cd_bwd")
    G["norm_cd"][i] = dg[0]
    return dx


def _fwd_x(l, x, mem_n, P):
    hx = rms_fwd(x, P["norm_x"][l:l + 1], "rms_x_fwd")
    q = mm(m2(hx), m3(P["w_xq"], l), "nn", "xq", out_dtype=BF16)
    kv = mm(m2(mem_n), mcs(P["w_xkv"], l), "nn", "xkv", out_dtype=BF16)
    ox = xattn_fwd(q, kv)
    x_out = mm(m2(ox), m3(P["w_xo"], l), "nn", "xo", add=m2(x))
    return x_out, dict(x=x, hx=hx, q=q, kv=kv, ox=ox)


def _bwd_x(l, dx_out, sv, mem_n, d_memn, P, G):
    d_ox = mm(m2(dx_out), m3(P["w_xo"], l), "nt", "xo_dx", out_dtype=BF16)
    G["w_xo"][l] = mm(m2(sv["ox"]), m2(dx_out), "tn", "xo_dw").reshape(4, 256, D)
    dq, dkv = xattn_bwd(sv["q"], sv["kv"], d_ox)
    G["w_xq"][l] = mm(m2(sv["hx"]), m2(dq), "tn", "xq_dw").reshape(4, 256, D)
    d_hx = mm(m2(dq), m3(P["w_xq"], l), "nt", "xq_dx")
    G["w_xkv"][l] = mm(m2(mem_n), m2(dkv), "tn", "xkv_dw", out=outcs(D, 512))
    d_memn = mm(m2(dkv), mcs(P["w_xkv"], l), "nt", "xkv_dx", add=None if d_memn is None else m2(d_memn))
    dx, dg = rms_bwd(sv["x"], d_hx, dx_out, P["norm_x"][l:l + 1], "rms_x_bwd")
    G["norm_x"][l] = dg[0]
    return dx, d_memn


def local_step(x, mem, tgt, P):
    G = {k: [None] * n for k, n in (
        ("norm_ab", 2), ("w_in_ab", 2), ("pool_w", 2), ("pool_scale", 2), ("w_out_ab", 2), ("norm_cd", 2),
        ("w_in_cd", 2), ("sgu_ln_g", 2), ("sgu_ln_b", 2), ("sgu_w", 2), ("sgu_b", 2), ("s5_a_re", 2),
        ("s5_a_im", 2), ("s5_log_dt", 2), ("s5_b_re", 2), ("s5_b_im", 2), ("s5_c_re", 2), ("s5_c_im", 2),
        ("s5_d", 2), ("glu_w1", 2), ("glu_w2", 2), ("w_out_cd", 2), ("norm_x", 4), ("w_xq", 4),
        ("w_xkv", 4), ("w_xo", 4))}
    mem_g = P["mem_norm"].reshape(1, D)
    mem_n = rms_fwd(mem, mem_g, "rms_mem_fwd")
    saved = []
    for layer in range(4):
        i = layer // 2
        x, sv_m = (_fwd_even if layer % 2 == 0 else _fwd_odd)(i, x, P)
        x, sv_x = _fwd_x(layer, x, mem_n, P)
        saved.append((sv_m, sv_x))
    dx, loss, dgf = final_loss(x, tgt, P["final_norm"].reshape(1, D))
    G["final_norm"] = dgf[0]
    d_memn = None
    for layer in reversed(range(4)):
        i = layer // 2
        sv_m, sv_x = saved[layer]
        dx, d_memn = _bwd_x(layer, dx, sv_x, mem_n, d_memn, P, G)
        dx = (_bwd_even if layer % 2 == 0 else _bwd_odd)(i, dx, sv_m, P, G)
    _, dgm = rms_bwd(mem, d_memn, d_memn, mem_g, "rms_mem_bwd")
    G["mem_norm"] = dgm[0]
    return loss, dx, G


ANY = pl.BlockSpec(memory_space=pl.ANY)


def _place():
    x, y, c = lax.axis_index("x"), lax.axis_index("y"), lax.axis_index("c")
    chips = [(1 - x, y), (x, 1 - y), (1 - x, 1 - y)]
    return x, y, c, 2 * x + y, (x, y, 1 - c), chips


def _remote(src, dst, send, recv, k, dev):
    return pltpu.make_async_remote_copy(src_ref=src, dst_ref=dst, send_sem=send.at[k], recv_sem=recv.at[k],
                                        device_id=dev, device_id_type=MESHID)


def allgather_big(shards):
    n = len(shards)

    def body(*refs):
        ins, outs = refs[:n], refs[n:2 * n]
        send, recv = refs[2 * n:]
        x, y, c, jme, sib, chips = _place()

        def half(a, hc):
            lh = shards[a].shape[0] // 2
            return pl.ds(hc * lh, lh)

        first, passed = [], []
        for a in range(n):
            cp = _remote(ins[a], outs[a].at[:, jme], send, recv, a * 7 + 6, sib)
            cp.start()
            first.append(cp)
            for k, chip in enumerate(chips):
                cp = _remote(ins[a].at[half(a, c)], outs[a].at[half(a, c), jme], send, recv, a * 7 + k, (*chip, c))
                cp.start()
                first.append(cp)
        for a in range(n):
            for k, chip in enumerate(chips):
                piece = outs[a].at[half(a, c), 2 * chip[0] + chip[1]]
                _remote(piece, piece, send, recv, a * 7 + k, (*chip, c)).wait_recv()
                fw = _remote(piece, piece, send, recv, a * 7 + 3 + k, sib)
                fw.start()
                passed.append(fw)
        for a in range(n):
            own = outs[a].at[:, jme]
            _remote(own, own, send, recv, a * 7 + 6, sib).wait_recv()
            for k, chip in enumerate(chips):
                piece = outs[a].at[half(a, 1 - c), 2 * chip[0] + chip[1]]
                _remote(piece, piece, send, recv, a * 7 + 3 + k, sib).wait_recv()
        for cp in first + passed:
            cp.wait_send()

    return pl.pallas_call(
        body, name="allgather_big", in_specs=[ANY] * n, out_specs=[ANY] * n,
        out_shape=[jax.ShapeDtypeStruct((s.shape[0], 4) + s.shape[1:], s.dtype) for s in shards],
        scratch_shapes=[pltpu.SemaphoreType.DMA((7 * n,)), pltpu.SemaphoreType.DMA((7 * n,))],
    )(*shards)


def allgather_small(slab):
    def body(in_ref, out_ref, send, recv, lsem):
        x, y, c, jme, sib, chips = _place()
        loc = pltpu.make_async_copy(in_ref, out_ref.at[jme], lsem.at[0])
        loc.start()
        cps = [_remote(in_ref, out_ref.at[jme], send, recv, k, (*chip, c)) for k, chip in enumerate(chips)]
        for cp in cps:
            cp.start()
        for k, chip in enumerate(chips):
            piece = out_ref.at[2 * chip[0] + chip[1]]
            _remote(piece, piece, send, recv, k, (*chip, c)).wait_recv()
        for cp in cps:
            cp.wait_send()
        loc.wait()

    return pl.pallas_call(
        body, name="allgather_small", in_specs=[ANY], out_specs=ANY,
        out_shape=jax.ShapeDtypeStruct((4,) + slab.shape, slab.dtype),
        scratch_shapes=[pltpu.SemaphoreType.DMA((3,)), pltpu.SemaphoreType.DMA((3,)), pltpu.SemaphoreType.DMA((1,))],
    )(slab)


def allreduce_small(v):
    def body(v_ref, o_ref, r0, r1, r2, send, recv):
        x, y, c, jme, sib, chips = _place()
        peers = [sib, (1 - x, y, c), (x, 1 - y, c)]
        o_ref[...] = v_ref[...]
        for k, buf in enumerate((r0, r1, r2)):
            cp = _remote(o_ref, buf, send, recv, k, peers[k])
            cp.start()
            cp.wait()
            o_ref[...] = o_ref[...] + buf[...]

    vm = pl.BlockSpec(memory_space=pltpu.VMEM)
    return pl.pallas_call(
        body, name="allreduce_small", in_specs=[vm], out_specs=vm,
        out_shape=jax.ShapeDtypeStruct(v.shape, v.dtype),
        scratch_shapes=[pltpu.VMEM(v.shape, v.dtype)] * 3 + [pltpu.SemaphoreType.DMA((3,)), pltpu.SemaphoreType.DMA((3,))],
        compiler_params=pltpu.CompilerParams(vmem_limit_bytes=VMEM_LIMIT),
    )(v)


def rs_pair_exchange(gs, layout):
    n, nw = len(gs), len(layout)

    def body(*refs):
        ins, outs = refs[:n], refs[n:n + nw]
        send, recv = refs[n + nw:]
        x, y, c, jme, sib, chips = _place()
        cps = []
        for w, (first, layers) in enumerate(layout):
            for l in range(layers):
                cps.append(_remote(ins[first + l].at[:, 1 - c], outs[w].at[l], send, recv, first + l, sib))
        for cp in cps:
            cp.start()
        for cp in cps:
            cp.wait()

    return pl.pallas_call(
        body, name="rs_pair_exchange", in_specs=[ANY] * n, out_specs=[ANY] * nw,
        out_shape=[jax.ShapeDtypeStruct((layers, 4) + gs[first].shape[2:], F32) for first, layers in layout],
        scratch_shapes=[pltpu.SemaphoreType.DMA((n,)), pltpu.SemaphoreType.DMA((n,))],
    )(*gs)


def rs_pair_sum(g4, got, l, acc, cidx):
    _, _, rh, cols = g4.shape
    tr = rh if rh <= 256 else 256

    def body(c_ref, a_ref, b_ref, *rest):
        o_ref = rest[-1]
        o_ref[...] = (a_ref[...] + b_ref[...]).astype(o_ref.dtype)

    in_specs = [pl.BlockSpec((None, None, tr, cols), lambda j, t, cr: (j, cr[0], t, 0)),
                pl.BlockSpec((None, None, tr, cols), lambda j, t, cr: (l, j, t, 0))]
    args = [cidx, g4, got]
    if acc is not None:
        in_specs.append(ANY)
        args.append(acc)
    return pl.pallas_call(
        body, name="rs_pair_sum",
        grid_spec=pltpu.PrefetchScalarGridSpec(
            num_scalar_prefetch=1, grid=(4, rh // tr), in_specs=in_specs,
            out_specs=pl.BlockSpec((None, None, tr, cols), lambda j, t, cr: (l, j, t, 0))),
        out_shape=jax.ShapeDtypeStruct(got.shape, BF16),
        input_output_aliases={} if acc is None else {3: 0},
        compiler_params=_cparams(("parallel", "parallel")),
    )(*args)


def rs_chip_exchange(ps):
    n = len(ps)

    def body(*refs):
        ins, outs = refs[:n], refs[n:2 * n]
        send, recv = refs[2 * n:]
        x, y, c, jme, sib, chips = _place()
        cps = []
        for a in range(n):
            for k, chip in enumerate(chips):
                cp = _remote(ins[a].at[:, 2 * chip[0] + chip[1]], outs[a].at[:, jme], send, recv, a * 3 + k, (*chip, c))
                cp.start()
                cps.append(cp)
        for a in range(n):
            for k, chip in enumerate(chips):
                slot = outs[a].at[:, 2 * chip[0] + chip[1]]
                _remote(slot, slot, send, recv, a * 3 + k, (*chip, c)).wait_recv()
        for cp in cps:
            cp.wait_send()

    return pl.pallas_call(
        body, name="rs_chip_exchange", in_specs=[ANY] * n, out_specs=[ANY] * n,
        out_shape=[jax.ShapeDtypeStruct(p.shape, p.dtype) for p in ps],
        scratch_shapes=[pltpu.SemaphoreType.DMA((3 * n,)), pltpu.SemaphoreType.DMA((3 * n,))],
    )(*ps)


def rs_chip_sum(q, p, jc):
    layers, _, rh, cols = q.shape
    tr = rh if rh <= 256 else 256

    def body(jc_ref, q_ref, p_ref, o_ref):
        jme = jc_ref[0]
        own = p_ref[...].astype(F32)
        v = [jnp.where(jme == j, own, q_ref[j].astype(F32)) for j in range(4)]
        o_ref[...] = ((v[0] + v[1]) + v[2]) + v[3]

    return pl.pallas_call(
        body, name="rs_chip_sum",
        grid_spec=pltpu.PrefetchScalarGridSpec(
            num_scalar_prefetch=1, grid=(layers, rh // tr),
            in_specs=[pl.BlockSpec((None, 4, tr, cols), lambda l, t, jr: (l, 0, t, 0)),
                      pl.BlockSpec((None, None, tr, cols), lambda l, t, jr: (l, jr[0], t, 0))],
            out_specs=pl.BlockSpec((None, None, tr, cols), lambda l, t, jr: (l, jr[1], t, 0))),
        out_shape=jax.ShapeDtypeStruct((layers, 2, rh, cols), F32),
        compiler_params=_cparams(("parallel", "parallel")),
    )(jc, q, p)


def rs_pair_gather(rs):
    n = len(rs)

    def body(*refs):
        outs = refs[n:2 * n]
        send, recv = refs[2 * n:]
        x, y, c, jme, sib, chips = _place()
        cps = [_remote(outs[a].at[:, c], outs[a].at[:, c], send, recv, a, sib) for a in range(n)]
        for cp in cps:
            cp.start()
        for a in range(n):
            slot = outs[a].at[:, 1 - c]
            _remote(slot, slot, send, recv, a, sib).wait_recv()
        for cp in cps:
            cp.wait_send()

    return pl.pallas_call(
        body, name="rs_pair_gather", in_specs=[ANY] * n, out_specs=[ANY] * n,
        out_shape=[jax.ShapeDtypeStruct(r.shape, r.dtype) for r in rs],
        input_output_aliases={a: a for a in range(n)},
        scratch_shapes=[pltpu.SemaphoreType.DMA((n,)), pltpu.SemaphoreType.DMA((n,))],
    )(*rs)


def _adamw_math(w, g, m, v):
    m = B1 * m + (1.0 - B1) * g
    v = B2 * v + (1.0 - B2) * (g * g)
    m_hat = m / (1.0 - B1 ** STEP)
    v_hat = v / (1.0 - B2 ** STEP)
    return -LR * (m_hat / (jnp.sqrt(v_hat) + AEPS) + WD * w), m, v


def adamw(w, g, m, v, name):
    rows, cols = w.shape
    tr = 256 if rows % 256 == 0 else rows
    return rw(_adamw_math, [(a, 0, cols) for a in (w, g, m, v)], [(cols, F32)] * 3, name, rows, tr=tr)


WEIGHTS = ["norm_ab", "w_in_ab", "pool_w", "pool_scale", "w_out_ab", "norm_cd", "w_in_cd", "sgu_ln_g", "sgu_ln_b",
           "sgu_w", "sgu_b", "s5_a_re", "s5_a_im", "s5_log_dt", "s5_b_re", "s5_b_im", "s5_c_re", "s5_c_im", "s5_d",
           "glu_w1", "glu_w2", "w_out_cd", "norm_x", "w_xq", "w_xkv", "w_xo", "mem_norm", "final_norm"]
INPUTS = ["x", "mem"] + WEIGHTS + ["loss_target"] + ["m_" + n for n in WEIGHTS] + ["v_" + n for n in WEIGHTS]
BIG = ["w_in_ab", "w_out_ab", "w_in_cd", "w_out_cd", "w_xq", "w_xkv", "w_xo", "glu_w1", "glu_w2", "pool_w"]
COL_SHARDED = ("w_in_ab", "w_in_cd", "w_xkv")
SMALL = [n for n in WEIGHTS if n not in BIG]
SMALL_SHARDED = {"norm_cd": 256, "sgu_ln_g": 256, "sgu_ln_b": 256, "s5_d": 128}
PACK = 256 * 128


def _pack(arrs):
    flat = jnp.concatenate([a.reshape(-1) for a in arrs])
    pad = (-flat.shape[0]) % PACK
    return jnp.concatenate([flat, jnp.zeros((pad,), flat.dtype)]).reshape(-1, 128)


def _unpack(packed, shapes):
    flat, out, off = packed.reshape(-1), [], 0
    for s in shapes:
        n = 1
        for d in s:
            n *= d
        out.append(flat[off:off + n].reshape(s))
        off += n
    return out


def _mat(a):
    return a.reshape(a.shape[0], -1, a.shape[-1])


def kernel(*args):
    a = dict(zip(INPUTS, args))
    x_i, y_i, c_i = lax.axis_index("x"), lax.axis_index("y"), lax.axis_index("c")
    j = 2 * x_i + y_i

    gathered = dict(zip(BIG, allgather_big([_mat(a[n]).astype(BF16) for n in BIG])))
    slab = jnp.concatenate([a["norm_cd"], a["sgu_ln_g"], a["sgu_ln_b"],
                            jnp.pad(a["s5_d"], ((0, 0), (0, 128)))], axis=0)
    gslab = allgather_small(slab)
    P = {n: a[n] for n in SMALL}
    for k, n in enumerate(("norm_cd", "sgu_ln_g", "sgu_ln_b", "s5_d")):
        wd = SMALL_SHARDED[n]
        P[n] = gslab[:, 2 * k:2 * k + 2, :wd].transpose(1, 0, 2).reshape(2, 4 * wd)
    for n in BIG:
        g = gathered[n]
        if n in COL_SHARDED:
            P[n] = g
        elif n == "pool_w":
            P[n] = g.reshape(2, 4, 4, 64, 256).transpose(0, 2, 1, 3, 4).reshape(2, 4, 256, 256)
        else:
            P[n] = g.reshape(g.shape[0], 4 * g.shape[2], g.shape[3])

    loss, dx, G = local_step(a["x"][0], a["mem"][0], a["loss_target"][0], P)
    loss = lax.psum(loss[0, 0], ("x", "y", "c"))

    layout, flat = [], []
    for n in BIG:
        layout.append((len(flat), len(G[n])))
        flat += [g.reshape(4, 2, g.shape[1] // 2, g.shape[2]) for g in G[n]]
    got = rs_pair_exchange(flat, layout)
    cidx = jnp.reshape(c_i, (1,)).astype(jnp.int32)
    jc = jnp.stack([j, c_i]).astype(jnp.int32)
    pair = []
    for (first, layers), gw in zip(layout, got):
        acc = None
        for l in range(layers):
            acc = rs_pair_sum(flat[first + l], gw, l, acc, cidx)
        pair.append(acc)
    red = [rs_chip_sum(q, p, jc) for q, p in zip(rs_chip_exchange(pair), pair)]
    gbig = dict(zip(BIG, rs_pair_gather(red)))

    outs = {}
    for n in BIG:
        shp = a[n].shape
        g2 = gbig[n].reshape(-1, shp[-1])
        d2, m2_, v2_ = adamw(a[n].reshape(g2.shape), g2, a["m_" + n].reshape(g2.shape),
                             a["v_" + n].reshape(g2.shape), "adamw_" + n)
        outs[n] = tuple(t.reshape(shp) for t in (g2, d2, m2_, v2_))

    gfull = [jnp.stack(G[n]) if isinstance(G[n], list) else G[n] for n in SMALL]
    shapes = [g.shape for g in gfull]
    gsum = _unpack(allreduce_small(_pack(gfull)), shapes)
    gloc = []
    for n, g in zip(SMALL, gsum):
        if n in SMALL_SHARDED:
            g = lax.dynamic_slice_in_dim(g, j * SMALL_SHARDED[n], SMALL_SHARDED[n], axis=1)
        gloc.append(g)
    lshapes = [a[n].shape for n in SMALL]
    packed = [_pack(t) for t in ([a[n] for n in SMALL], gloc, [a["m_" + n] for n in SMALL], [a["v_" + n] for n in SMALL])]
    small = [_unpack(t, lshapes) for t in adamw(*packed, "adamw_small")]
    for k, n in enumerate(SMALL):
        outs[n] = (gloc[k], small[0][k], small[1][k], small[2][k])

    res = [loss, dx[None]]
    for part in range(4):
        res += [outs[n][part] for n in WEIGHTS]
    return tuple(res)
```

```python
import math

import jax
import jax.numpy as jnp
from jax import lax
from jax.experimental import pallas as pl
from jax.experimental.pallas import tpu as pltpu

F32, BF16 = jnp.float32, jnp.bfloat16
S, D = 2048, 1024
MEM = 256
EPS = 1e-6
NEG = -1e30
QB = 128
PATTERNS = (1, 4, 16)
NG, NP, NH = 32, 64, 16
NS = NG * NP
LR, B1, B2, AEPS, WD, STEP = 0.001, 0.9, 0.999, 1e-08, 0.01, 10
MESHID = pl.DeviceIdType.MESH
VMEM_LIMIT = 56 * 1024 * 1024


def _cparams(sem):
    return pltpu.CompilerParams(dimension_semantics=sem, vmem_limit_bytes=VMEM_LIMIT)


def _sig(x):
    return 1.0 / (1.0 + jnp.exp(-x))


def _dot(a, b, dims):
    return lax.dot_general(a, b, (dims, ((), ())), preferred_element_type=F32)


def _nn(a, b):
    return _dot(a, b, ((1,), (0,)))


def _nt(a, b):
    return _dot(a, b, ((1,), (1,)))


def _tn(a, b):
    return _dot(a, b, ((0,), (0,)))


_DIMS = {"nn": ((1,), (0,)), "nt": ((1,), (1,)), "tn": ((0,), (0,))}


def _tile(dim, cc=None, cap=1024):
    for t in (2048, 1536, 1024, 768, 512, 384, 256, 128):
        if t <= cap and dim % t == 0 and (cc is None or cc % t == 0):
            return t
    return dim


MM_VMEM = 36 * 1024 * 1024


def _mm_tiles(m, n, k, ccm, ccn, cck, a_bytes, b_bytes, o_bytes):
    caps = [1024, 1024, 2048]
    while True:
        tm, tn, tk = _tile(m, ccm, caps[0]), _tile(n, ccn, caps[1]), _tile(k, cck, caps[2])
        need = 2 * (tm * tk * a_bytes + tk * tn * b_bytes + tm * tn * o_bytes) + (tm * tn * 4 if tk < k else 0)
        if need <= MM_VMEM:
            return tm, tn, tk
        if tk > 1024:
            caps[2] = tk // 2
        elif tn >= tm:
            caps[1] = tn // 2
        else:
            caps[0] = tm // 2


def m2(arr, col_off=0, ncols=None):
    rows, cols = arr.shape
    ncols = cols - col_off if ncols is None else ncols

    def spec(tr, tc, rc):
        assert col_off % tc == 0
        return pl.BlockSpec((tr, tc), lambda *g: (rc(*g)[0], rc(*g)[1] + col_off // tc))
    return (arr, rows, ncols, spec, None if col_off == 0 else col_off)


def m3(arr, i):
    def spec(tr, tc, rc):
        return pl.BlockSpec((None, tr, tc), lambda *g: (i,) + tuple(rc(*g)))
    return (arr, arr.shape[1], arr.shape[2], spec, None)


def mcs(arr, i):
    cs = arr.shape[3]

    def spec(tr, tc, rc):
        n = cs // tc
        return pl.BlockSpec((None, None, tr, tc),
                            lambda *g: (i, rc(*g)[1] // n, rc(*g)[0], rc(*g)[1] % n))
    return (arr, arr.shape[2], 4 * cs, spec, cs)


def out2(rows, cols):
    def spec(tr, tc, rc):
        return pl.BlockSpec((tr, tc), lambda *g: tuple(rc(*g)))
    return ((rows, cols), spec, None)


def outcs(rows, cs):
    def spec(tr, tc, rc):
        n = cs // tc
        return pl.BlockSpec((None, tr, tc), lambda *g: (rc(*g)[1] // n, rc(*g)[0], rc(*g)[1] % n))
    return ((4, rows, cs), spec, cs)


def _both(a, b):
    if a is None:
        return b
    if b is None:
        return a
    return math.gcd(a, b)


def mm(a, b, mode, name, add=None, out=None, out_dtype=F32):
    a_arr, a_r, a_c, a_spec, a_cc = a
    b_arr, b_r, b_c, b_spec, b_cc = b
    if mode == "nn":
        m, k, n = a_r, a_c, b_c
        assert b_r == k
        ccm, cck, ccn = None, a_cc, b_cc
    elif mode == "nt":
        m, k, n = a_r, a_c, b_r
        assert b_c == k
        ccm, cck, ccn = None, _both(a_cc, b_cc), None
    else:
        m, k, n = a_c, a_r, b_c
        assert b_r == k
        ccm, cck, ccn = a_cc, None, b_cc
    out = out2(m, n) if out is None else out
    o_shape, o_spec, o_cc = out
    ccn = _both(ccn, o_cc)
    if add is not None:
        ccn = _both(ccn, add[4])
    o_bytes = jnp.dtype(out_dtype).itemsize + (0 if add is None else add[0].dtype.itemsize)
    tm, tn, tk = _mm_tiles(m, n, k, ccm, ccn, cck, a_arr.dtype.itemsize, b_arr.dtype.itemsize, o_bytes)
    nk = k // tk
    if mode == "nn":
        in_specs = [a_spec(tm, tk, lambda i, j, kk: (i, kk)), b_spec(tk, tn, lambda i, j, kk: (kk, j))]
    elif mode == "nt":
        in_specs = [a_spec(tm, tk, lambda i, j, kk: (i, kk)), b_spec(tn, tk, lambda i, j, kk: (j, kk))]
    else:
        in_specs = [a_spec(tk, tm, lambda i, j, kk: (kk, i)), b_spec(tk, tn, lambda i, j, kk: (kk, j))]
    args = [a_arr, b_arr]
    if add is not None:
        in_specs.append(add[3](tm, tn, lambda i, j, kk: (i, j)))
        args.append(add[0])
    dims = _DIMS[mode]
    has_add = add is not None

    def body(*refs):
        a_ref, b_ref = refs[0], refs[1]
        add_ref = refs[2] if has_add else None
        prod = _dot(a_ref[...].astype(BF16), b_ref[...].astype(BF16), dims)
        if nk == 1:
            o_ref = refs[-1]
            if has_add:
                prod = prod + add_ref[...].astype(F32)
            o_ref[...] = prod.astype(o_ref.dtype)
            return
        o_ref, acc = refs[-2], refs[-1]
        kk = pl.program_id(2)

        @pl.when(kk == 0)
        def _():
            acc[...] = prod

        @pl.when(kk > 0)
        def _():
            acc[...] += prod

        @pl.when(kk == nk - 1)
        def _():
            r = acc[...]
            if has_add:
                r = r + add_ref[...].astype(F32)
            o_ref[...] = r.astype(o_ref.dtype)

    return pl.pallas_call(
        body, name=name, grid=(m // tm, n // tn, nk), in_specs=in_specs,
        out_specs=o_spec(tm, tn, lambda i, j, kk: (i, j)),
        out_shape=jax.ShapeDtypeStruct(o_shape, out_dtype),
        scratch_shapes=[pltpu.VMEM((tm, tn), F32)] if nk > 1 else [],
        compiler_params=_cparams(("parallel", "parallel", "arbitrary")),
    )(*args)


def rw(fn, ins, outs, name, rows, tr=256, consts=(), accs=()):
    n_in, n_c, n_o, n_a = len(ins), len(consts), len(outs), len(accs)
    in_specs = []
    for arr, off, width in ins:
        assert off % width == 0
        in_specs.append(pl.BlockSpec((tr, width), lambda i, o=off // width: (i, o)))
    for c in consts:
        in_specs.append(pl.BlockSpec(c.shape, lambda i: (0, 0)))
    out_specs = [pl.BlockSpec((tr, w), lambda i: (i, 0)) for w, _ in outs]
    out_specs += [pl.BlockSpec(s, lambda i: (0, 0)) for s in accs]
    out_shape = [jax.ShapeDtypeStruct((rows, w), dt) for w, dt in outs]
    out_shape += [jax.ShapeDtypeStruct(s, F32) for s in accs]

    def body(*refs):
        vals = [r[...] for r in refs[:n_in + n_c]]
        o_refs = refs[n_in + n_c:n_in + n_c + n_o]
        a_refs = refs[n_in + n_c + n_o:]
        res = fn(*vals)
        for r, v in zip(o_refs, res[:n_o]):
            r[...] = v.astype(r.dtype)
        if n_a:
            @pl.when(pl.program_id(0) == 0)
            def _():
                for r in a_refs:
                    r[...] = jnp.zeros_like(r)
            for r, v in zip(a_refs, res[n_o:]):
                r[...] += v

    res = pl.pallas_call(
        body, name=name, grid=(rows // tr,), in_specs=in_specs, out_specs=out_specs,
        out_shape=out_shape,
        compiler_params=_cparams(("arbitrary",) if n_a else ("parallel",)),
    )(*[a for a, _, _ in ins], *consts)
    return res


def _rstd(x):
    return lax.rsqrt(jnp.mean(x * x, axis=-1, keepdims=True) + EPS)


def rms_fwd(x, g, name):
    def fn(xv, gv):
        xv = xv.astype(F32)
        return (xv * _rstd(xv) * gv,)
    return rw(fn, [(x, 0, D)], [(D, BF16)], name, x.shape[0], consts=[g])[0]


def _rms_bwd_math(xv, dy, gv):
    r = _rstd(xv)
    dyg = dy * gv
    dx = r * dyg - xv * (r * r * r / D) * jnp.sum(dyg * xv, axis=-1, keepdims=True)
    dg = jnp.sum(dy * xv * r, axis=0, keepdims=True)
    return dx, dg


def rms_bwd(x, dy, dres, g, name):
    def fn(xv, dyv, drv, gv):
        dx, dg = _rms_bwd_math(xv, dyv, gv)
        return dx + drv, dg
    return rw(fn, [(x, 0, D), (dy, 0, D), (dres, 0, D)], [(D, F32)], name, x.shape[0],
              consts=[g], accs=[(1, D)])


def final_loss(x, tgt, g):
    def fn(xv, tv, gv):
        e = xv * _rstd(xv) * gv - tv
        loss = 0.5 * jnp.sum(jnp.sum(e * e, axis=-1, keepdims=True), axis=0, keepdims=True) / D
        dx, dg = _rms_bwd_math(xv, e / D, gv)
        return dx, loss, dg
    return rw(fn, [(x, 0, D), (tgt, 0, D)], [(D, F32)], "final_loss", S, consts=[g],
              accs=[(1, 1), (1, D)])


def _attn_bias(bias_ref):
    ii = lax.broadcasted_iota(jnp.int32, (2 * QB, 2 * QB), 0) % QB
    jj = lax.broadcasted_iota(jnp.int32, (2 * QB, 2 * QB), 1)
    dist = ii + QB - jj
    band = (dist >= 0) & (dist <= QB)
    bias_ref[1] = jnp.where(band, 0.0, NEG)
    bias_ref[0] = jnp.where(band & (jj >= QB), 0.0, NEG)


def _two_heads(x, m0):
    return jnp.concatenate([jnp.where(m0, x, 0.0), jnp.where(m0, 0.0, x)], axis=0)


def _per_head(col, m0):
    return jnp.where(m0, col[:QB], col[QB:])


def _attn_rows(idx, d):
    if d == 1:
        b = idx
        cur = pl.ds(pl.multiple_of(b * QB, QB), QB)
        prev = pl.ds(pl.multiple_of(jnp.maximum(b - 1, 0) * QB, QB), QB)
    else:
        r, b = lax.rem(idx, d), lax.div(idx, d)
        cur = pl.ds(r + b * (QB * d), QB, stride=d)
        prev = pl.ds(r + jnp.maximum(b - 1, 0) * (QB * d), QB, stride=d)
    return cur, prev, b


NBLK = S // QB
GROUP = 4


def _colblk(off):
    return pl.BlockSpec((S, 128), lambda hp: (0, off * 8 + hp))


def attn_fwd(z):
    def body(q_ref, k_ref, v_ref, g_ref, o_ref, l_ref, a_ref, os, ls, bias):
        _attn_bias(bias)
        m0 = lax.broadcasted_iota(jnp.int32, (1, 128), 1) < 64
        for pi, d in enumerate(PATTERNS):
            def load(idx, d=d):
                cur, prev, b = _attn_rows(idx, d)
                return cur, (q_ref[cur, :], k_ref[prev, :], k_ref[cur, :], v_ref[prev, :], v_ref[cur, :],
                             bias[jnp.minimum(b, 1)])

            def block(q, kp, kc, vp, vc, bs):
                qq = _two_heads(q * 0.125, m0).astype(BF16)
                k = jnp.concatenate([kp, kc], axis=0).astype(BF16)
                s = _nt(qq, k) + bs
                mx = jnp.max(s, axis=-1, keepdims=True)
                p = jnp.exp(s - mx)
                den = jnp.sum(p, axis=-1, keepdims=True)
                pb = p.astype(BF16)
                vv = _two_heads(jnp.concatenate([vp, vc], axis=0), m0).astype(BF16)
                o = _nn(jnp.concatenate([pb[:QB], pb[QB:]], axis=1), vv)
                return o * _per_head(1.0 / den, m0), _per_head(mx + jnp.log(den), m0)

            def step(i, carry, pi=pi):
                loaded = [load(i * GROUP + u) for u in range(GROUP)]
                done = [block(*vals) for _, vals in loaded]
                for (cur, _), (o, l) in zip(loaded, done):
                    os[pi, cur, :] = o
                    ls[pi, cur, :] = l
                return carry
            lax.fori_loop(0, NBLK // GROUP, step, 0)
        l1, l2, l3 = ls[0], ls[1], ls[2]
        mx = jnp.maximum(jnp.maximum(l1, l2), l3)
        e1, e2, e3 = jnp.exp(l1 - mx), jnp.exp(l2 - mx), jnp.exp(l3 - mx)
        tot = e1 + e2 + e3
        o = (os[0] * e1 + os[1] * e2 + os[2] * e3) / tot
        ga = g_ref[...]
        o_ref[...] = o
        l_ref[...] = mx + jnp.log(tot)
        a_ref[...] = (o * (ga * _sig(ga))).astype(a_ref.dtype)

    out = pl.BlockSpec((S, 128), lambda hp: (0, hp))
    return pl.pallas_call(
        body, name="attn_fwd", grid=(8,),
        in_specs=[_colblk(0), _colblk(1), _colblk(2), _colblk(3)], out_specs=[out] * 3,
        out_shape=[jax.ShapeDtypeStruct((S, D), F32), jax.ShapeDtypeStruct((S, D), F32),
                   jax.ShapeDtypeStruct((S, 2 * D), BF16)],
        scratch_shapes=[pltpu.VMEM((3, S, 128), F32), pltpu.VMEM((3, S, 128), F32),
                        pltpu.VMEM((2, 2 * QB, 2 * QB), F32)],
        compiler_params=_cparams(("parallel",)),
    )(z, z, z, z)


def attn_bwd(z, d_cat, o, lse):
    def body(q_ref, k_ref, v_ref, g_ref, da_ref, o_ref, l_ref, dq_ref, dk_ref, dv_ref, dg_ref, do_s, pr_s, bias):
        _attn_bias(bias)
        m0 = lax.broadcasted_iota(jnp.int32, (1, 128), 1) < 64
        ga = g_ref[...]
        sg = _sig(ga)
        da = da_ref[...]
        ov = o_ref[...]
        do = da * (ga * sg)
        dg_ref[...] = da * ov * (sg * (1.0 + ga * (1.0 - sg)))
        do_s[...] = do
        pr_s[...] = do * ov
        dq_ref[...] = jnp.zeros_like(dq_ref)
        dk_ref[...] = jnp.zeros_like(dk_ref)
        dv_ref[...] = jnp.zeros_like(dv_ref)
        for d in PATTERNS:
            def load(idx, d=d):
                cur, prev, b = _attn_rows(idx, d)
                return (cur, prev), (q_ref[cur, :], k_ref[prev, :], k_ref[cur, :], v_ref[prev, :], v_ref[cur, :],
                                     do_s[cur, :], pr_s[cur, :], l_ref[cur, :], bias[jnp.minimum(b, 1)])

            def block(q, kp, kc, vp, vc, dof, prod, lp, bs):
                qq = _two_heads(q * 0.125, m0).astype(BF16)
                kf = jnp.concatenate([kp, kc], axis=0)
                k = kf.astype(BF16)
                v = jnp.concatenate([vp, vc], axis=0).astype(BF16)
                dd = _two_heads(dof, m0).astype(BF16)
                lh = jnp.max(jnp.concatenate([jnp.where(m0, lp, -jnp.inf), jnp.where(m0, -jnp.inf, lp)], axis=0),
                             axis=-1, keepdims=True)
                delta = jnp.sum(_two_heads(prod, m0), axis=-1, keepdims=True)
                p = jnp.exp(_nt(qq, k) + bs - lh)
                ds = (p * (_nt(dd, v) - delta)).astype(BF16)
                dq = _nn(jnp.concatenate([ds[:QB], ds[QB:]], axis=1), _two_heads(kf, m0).astype(BF16))
                return dq * 0.125, _tn(ds, qq), _tn(p.astype(BF16), dd)

            def step(i, carry):
                loaded = [load(i * GROUP + u) for u in range(GROUP)]
                done = [block(*vals) for _, vals in loaded]
                for ((cur, prev), _), (dq, dk, dv) in zip(loaded, done):
                    dq_ref[cur, :] = dq_ref[cur, :] + dq
                    dk_ref[prev, :] = dk_ref[prev, :] + dk[:QB]
                    dv_ref[prev, :] = dv_ref[prev, :] + dv[:QB]
                    dk_ref[cur, :] = dk_ref[cur, :] + dk[QB:]
                    dv_ref[cur, :] = dv_ref[cur, :] + dv[QB:]
                return carry
            lax.fori_loop(0, NBLK // GROUP, step, 0)

    blk = pl.BlockSpec((S, 128), lambda hp: (0, hp))
    return pl.pallas_call(
        body, name="attn_bwd", grid=(8,),
        in_specs=[_colblk(0), _colblk(1), _colblk(2), _colblk(3), blk, blk, blk], out_specs=[blk] * 4,
        out_shape=[jax.ShapeDtypeStruct((S, D), F32)] * 4,
        scratch_shapes=[pltpu.VMEM((S, 128), F32), pltpu.VMEM((S, 128), F32), pltpu.VMEM((2, 2 * QB, 2 * QB), F32)],
        compiler_params=_cparams(("parallel",)),
    )(z, z, z, z, d_cat, o, lse)


def assemble_dz_even(parts):
    def body(*refs):
        o_ref = refs[-1]
        for j in range(6):
            o_ref[:, j * D:(j + 1) * D] = refs[j][...].astype(o_ref.dtype)
    tr = 256
    blk = pl.BlockSpec((tr, D), lambda i: (i, 0))
    return pl.pallas_call(
        body, name="assemble_dz_even", grid=(S // tr,), in_specs=[blk] * 6,
        out_specs=pl.BlockSpec((tr, 6 * D), lambda i: (i, 0)),
        out_shape=jax.ShapeDtypeStruct((S, 6 * D), BF16),
        compiler_params=_cparams(("parallel",)),
    )(*parts)


def _pool_window(g):
    return jnp.where(g == 0, 2.0, jnp.where(g == 1, 4.0, jnp.where(g == 2, 8.0, 16.0)))


def _pool_sel(g, levels):
    return jnp.where(g == 0, levels[0], jnp.where(g == 1, levels[1], jnp.where(g == 2, levels[2], levels[3])))


def _pool_fwd_math(v, g):
    t = lax.broadcasted_iota(jnp.int32, (S, 1), 0)
    s = v
    levels = []
    for k in (1, 2, 4, 8):
        s = s + jnp.where(t >= k, pltpu.roll(s, k, 0), 0.0)
        levels.append(s)
    cnt = jnp.minimum((t + 1).astype(F32), _pool_window(g))
    return _pool_sel(g, levels) / cnt - v, cnt


def pool_fwd(z, pw, ps, cat):
    def body(v_ref, g_ref, pw_ref, ps_ref, cat_ref, o_ref):
        g = pl.program_id(0)
        pooled, _ = _pool_fwd_math(v_ref[...], g)
        mixed = _nn(pooled.astype(BF16), pw_ref[...].astype(BF16))
        gb = g_ref[...]
        o_ref[...] = (mixed * ps_ref[...] * (gb * _sig(gb))).astype(o_ref.dtype)

    return pl.pallas_call(
        body, name="pool_fwd", grid=(4,),
        in_specs=[pl.BlockSpec((S, 256), lambda g: (0, 16 + g)),
                  pl.BlockSpec((S, 256), lambda g: (0, 20 + g)),
                  pl.BlockSpec((None, 256, 256), lambda g: (g, 0, 0)),
                  pl.BlockSpec((1, 256), lambda g: (0, g)), pl.BlockSpec(memory_space=pl.ANY)],
        out_specs=pl.BlockSpec((S, 256), lambda g: (0, 4 + g)),
        out_shape=jax.ShapeDtypeStruct((S, 2 * D), BF16),
        input_output_aliases={4: 0},
        compiler_params=_cparams(("parallel",)),
    )(z, z, pw, ps, cat)


def pool_bwd(z, d_cat, pw, ps):
    def body(v_ref, g_ref, d_ref, pw_ref, ps_ref, dv_ref, dg_ref, dpw_ref, dps_ref):
        g = pl.program_id(0)
        v = v_ref[...]
        pooled, cnt = _pool_fwd_math(v, g)
        pwb = pw_ref[...].astype(BF16)
        pb = pooled.astype(BF16)
        mixed = _nn(pb, pwb)
        gb = g_ref[...]
        sg = _sig(gb)
        dout = d_ref[...]
        sc = ps_ref[...]
        dg_ref[...] = dout * mixed * sc * (sg * (1.0 + gb * (1.0 - sg)))
        dms = dout * (gb * sg)
        dps_ref[...] = jnp.sum(dms * mixed, axis=0, keepdims=True)
        dmx = (dms * sc).astype(BF16)
        dpw_ref[...] = _tn(pb, dmx)
        dpooled = _nt(dmx, pwb)
        t = lax.broadcasted_iota(jnp.int32, (S, 1), 0)
        s = dpooled / cnt
        levels = []
        for k in (1, 2, 4, 8):
            s = s + jnp.where(t < S - k, pltpu.roll(s, S - k, 0), 0.0)
            levels.append(s)
        dv_ref[...] = _pool_sel(g, levels) - dpooled

    return pl.pallas_call(
        body, name="pool_bwd", grid=(4,),
        in_specs=[pl.BlockSpec((S, 256), lambda g: (0, 16 + g)),
                  pl.BlockSpec((S, 256), lambda g: (0, 20 + g)),
                  pl.BlockSpec((S, 256), lambda g: (0, 4 + g)),
                  pl.BlockSpec((None, 256, 256), lambda g: (g, 0, 0)),
                  pl.BlockSpec((1, 256), lambda g: (0, g))],
        out_specs=[pl.BlockSpec((S, 256), lambda g: (0, g)),
                   pl.BlockSpec((S, 256), lambda g: (0, g)),
                   pl.BlockSpec((None, 256, 256), lambda g: (g, 0, 0)),
                   pl.BlockSpec((1, 256), lambda g: (0, g))],
        out_shape=[jax.ShapeDtypeStruct((S, D), F32), jax.ShapeDtypeStruct((S, D), F32),
                   jax.ShapeDtypeStruct((4, 256, 256), F32), jax.ShapeDtypeStruct((1, D), F32)],
        compiler_params=_cparams(("parallel",)),
    )(z, z, d_cat, pw, ps)


CH = 128


def _sgu_common(v, lng, lnb, w_ref):
    mu = jnp.mean(v, axis=-1, keepdims=True)
    vc = v - mu
    rs = lax.rsqrt(jnp.mean(vc * vc, axis=-1, keepdims=True) + EPS)
    xhat = vc * rs
    vn = (xhat * lng + lnb).astype(BF16)
    ri = lax.broadcasted_iota(jnp.int32, (CH, CH), 0)
    ci = lax.broadcasted_iota(jnp.int32, (CH, CH), 1)
    tril = ri >= ci
    ws = [jnp.where(tril, w_ref[g], 0.0).astype(BF16) for g in range(4)]
    return xhat, rs, vn, tril, ws


def _zspec(off):
    return pl.BlockSpec((CH, D), lambda c: (c, off))


def _full(shape):
    return pl.BlockSpec(shape, lambda c: (0,) * len(shape))


def sgu_fwd(z, lng, lnb, w, bfull):
    def body(u_ref, v_ref, g_ref, lng_ref, lnb_ref, w_ref, b_ref, o_ref):
        _, _, vn, _, ws = _sgu_common(v_ref[...], lng_ref[...], lnb_ref[...], w_ref)
        for g in range(4):
            sl = slice(g * 256, (g + 1) * 256)
            mixed = _nn(ws[g], vn[:, sl]) + b_ref[:, sl]
            gc = g_ref[:, sl]
            o_ref[:, sl] = (u_ref[:, sl] * mixed * (gc * _sig(gc))).astype(o_ref.dtype)

    return pl.pallas_call(
        body, name="sgu_fwd", grid=(S // CH,),
        in_specs=[_zspec(0), _zspec(1), _zspec(2), _full((1, D)), _full((1, D)),
                  _full((4, CH, CH)), _full((CH, D))],
        out_specs=pl.BlockSpec((CH, D), lambda c: (c, 0)),
        out_shape=jax.ShapeDtypeStruct((S, D), BF16),
        compiler_params=_cparams(("parallel",)),
    )(z, z, z, lng, lnb, w, bfull)


def sgu_bwd(z, d_cat, lng, lnb, w, bfull):
    def body(u_ref, v_ref, g_ref, d_ref, lng_ref, lnb_ref, w_ref, b_ref,
             du_ref, dv_ref, dg_ref, dw_ref, db_ref, dlg_ref, dlb_ref):
        @pl.when(pl.program_id(0) == 0)
        def _():
            dw_ref[...] = jnp.zeros_like(dw_ref)
            db_ref[...] = jnp.zeros_like(db_ref)
            dlg_ref[...] = jnp.zeros_like(dlg_ref)
            dlb_ref[...] = jnp.zeros_like(dlb_ref)

        lng = lng_ref[...]
        xhat, rs, vn, tril, ws = _sgu_common(v_ref[...], lng, lnb_ref[...], w_ref)
        lane = lax.broadcasted_iota(jnp.int32, (1, 128), 1)
        db = jnp.zeros((CH, 128), F32)
        dvn_parts = []
        for g in range(4):
            sl = slice(g * 256, (g + 1) * 256)
            mixed = _nn(ws[g], vn[:, sl]) + b_ref[:, sl]
            gc = g_ref[:, sl]
            sg = _sig(gc)
            u = u_ref[:, sl]
            dc = d_ref[:, sl]
            du_ref[:, sl] = dc * mixed * (gc * sg)
            dg_ref[:, sl] = dc * u * mixed * (sg * (1.0 + gc * (1.0 - sg)))
            dmx = dc * u * (gc * sg)
            db = db + jnp.where(lane == g, jnp.sum(dmx, axis=-1, keepdims=True), 0.0)
            dmb = dmx.astype(BF16)
            dw_ref[g] += jnp.where(tril, _nt(dmb, vn[:, sl]), 0.0)
            dvn_parts.append(_tn(ws[g], dmb))
        db_ref[...] += db
        dvn = jnp.concatenate(dvn_parts, axis=1)
        dlb_ref[...] += jnp.sum(dvn, axis=0, keepdims=True)
        dlg_ref[...] += jnp.sum(dvn * xhat, axis=0, keepdims=True)
        dxh = dvn * lng
        dv_ref[...] = rs * (dxh - jnp.mean(dxh, axis=-1, keepdims=True)
                            - xhat * jnp.mean(dxh * xhat, axis=-1, keepdims=True))

    row = pl.BlockSpec((CH, D), lambda c: (c, 0))
    return pl.pallas_call(
        body, name="sgu_bwd", grid=(S // CH,),
        in_specs=[_zspec(0), _zspec(1), _zspec(2), row, _full((1, D)), _full((1, D)),
                  _full((4, CH, CH)), _full((CH, D))],
        out_specs=[row, row, row, _full((4, CH, CH)), _full((CH, 128)), _full((1, D)), _full((1, D))],
        out_shape=[jax.ShapeDtypeStruct((S, D), F32)] * 3
        + [jax.ShapeDtypeStruct((4, CH, CH), F32), jax.ShapeDtypeStruct((CH, 128), F32),
           jax.ShapeDtypeStruct((1, D), F32), jax.ShapeDtypeStruct((1, D), F32)],
        compiler_params=_cparams(("arbitrary",)),
    )(z, z, z, d_cat, lng, lnb, w, bfull)


TB = 256


def _cmul(ar, ai, br, bi):
    return ar * br - ai * bi, ar * bi + ai * br


def _scan_consts(ar, ai, reverse):
    a2 = _cmul(ar, ai, ar, ai)
    a4 = _cmul(*a2, *a2)
    row = lax.broadcasted_iota(jnp.int32, (8, NS), 0)
    pr = jnp.zeros((8, NS), F32)
    pi = jnp.zeros((8, NS), F32)
    cr, ci = ar, ai
    for r in range(8):
        sel = row == (7 - r if reverse else r)
        pr = jnp.where(sel, cr, pr)
        pi = jnp.where(sel, ci, pi)
        cr, ci = _cmul(cr, ci, ar, ai)
    return ((ar, ai), a2, a4), (pr, pi), row


def scan_fwd(bu, abr, abi):
    def body(bu_ref, ar_ref, ai_ref, h_ref, car, cai):
        @pl.when(pl.program_id(0) == 0)
        def _():
            car[...] = jnp.zeros_like(car)
            cai[...] = jnp.zeros_like(cai)

        pows, (pr, pi), row = _scan_consts(ar_ref[...], ai_ref[...], False)

        def tile(t, carry):
            c_r, c_i = carry
            rows = pl.ds(pl.multiple_of(t * 8, 8), 8)
            xr = bu_ref[rows, 0:NS]
            xi = bu_ref[rows, NS:2 * NS]
            for k, (kr, ki) in zip((1, 2, 4), pows):
                sr = jnp.where(row >= k, pltpu.roll(xr, k, 0), 0.0)
                si = jnp.where(row >= k, pltpu.roll(xi, k, 0), 0.0)
                xr, xi = xr + kr * sr - ki * si, xi + kr * si + ki * sr
            xr, xi = xr + pr * c_r - pi * c_i, xi + pr * c_i + pi * c_r
            h_ref[rows, 0:NS] = xr
            h_ref[rows, NS:2 * NS] = xi
            return (jnp.broadcast_to(xr[7:8, :], (8, NS)), jnp.broadcast_to(xi[7:8, :], (8, NS)))

        c_r, c_i = lax.fori_loop(0, TB // 8, tile, (car[...], cai[...]))
        car[...] = c_r
        cai[...] = c_i

    return pl.pallas_call(
        body, name="s5_scan_fwd", grid=(S // TB,),
        in_specs=[pl.BlockSpec((TB, 2 * NS), lambda i: (i, 0)),
                  pl.BlockSpec((1, NS), lambda i: (0, 0)), pl.BlockSpec((1, NS), lambda i: (0, 0))],
        out_specs=pl.BlockSpec((TB, 2 * NS), lambda i: (i, 0)),
        out_shape=jax.ShapeDtypeStruct((S, 2 * NS), F32),
        scratch_shapes=[pltpu.VMEM((8, NS), F32), pltpu.VMEM((8, NS), F32)],
        compiler_params=_cparams(("arbitrary",)),
    )(bu, abr, abi)


def scan_bwd(eta, h, abr, abi):
    nt = S // TB

    def body(e_ref, h_ref, ar_ref, ai_ref, l_ref, da_ref, car, cai):
        @pl.when(pl.program_id(0) == 0)
        def _():
            car[...] = jnp.zeros_like(car)
            cai[...] = jnp.zeros_like(cai)
            da_ref[...] = jnp.zeros_like(da_ref)

        pows, (pr, pi), row = _scan_consts(ar_ref[...], -ai_ref[...], True)

        def tile(tt, carry):
            c_r, c_i, acr, aci = carry
            t = TB // 8 - 1 - tt
            rows = pl.ds(pl.multiple_of(t * 8, 8), 8)
            xr = e_ref[rows, 0:NS]
            xi = e_ref[rows, NS:2 * NS]
            for k, (kr, ki) in zip((1, 2, 4), pows):
                sr = jnp.where(row < 8 - k, pltpu.roll(xr, 8 - k, 0), 0.0)
                si = jnp.where(row < 8 - k, pltpu.roll(xi, 8 - k, 0), 0.0)
                xr, xi = xr + kr * sr - ki * si, xi + kr * si + ki * sr
            xr, xi = xr + pr * c_r - pi * c_i, xi + pr * c_i + pi * c_r
            l_ref[rows, 0:NS] = xr
            l_ref[rows, NS:2 * NS] = xi
            nr = jnp.where(row < 7, pltpu.roll(xr, 7, 0), c_r)
            ni = jnp.where(row < 7, pltpu.roll(xi, 7, 0), c_i)
            hr = h_ref[rows, 0:NS]
            hi = h_ref[rows, NS:2 * NS]
            acr = acr + hr * nr + hi * ni
            aci = aci + hr * ni - hi * nr
            return (jnp.broadcast_to(xr[0:1, :], (8, NS)), jnp.broadcast_to(xi[0:1, :], (8, NS)), acr, aci)

        zero = jnp.zeros((8, NS), F32)
        c_r, c_i, acr, aci = lax.fori_loop(0, TB // 8, tile, (car[...], cai[...], zero, zero))
        car[...] = c_r
        cai[...] = c_i
        da_ref[:, 0:NS] += acr
        da_ref[:, NS:2 * NS] += aci

    rev = pl.BlockSpec((TB, 2 * NS), lambda i: (nt - 1 - i, 0))
    return pl.pallas_call(
        body, name="s5_scan_bwd", grid=(nt,),
        in_specs=[rev, rev, pl.BlockSpec((1, NS), lambda i: (0, 0)), pl.BlockSpec((1, NS), lambda i: (0, 0))],
        out_specs=[rev, pl.BlockSpec((8, 2 * NS), lambda i: (0, 0))],
        out_shape=[jax.ShapeDtypeStruct((S, 2 * NS), F32), jax.ShapeDtypeStruct((8, 2 * NS), F32)],
        scratch_shapes=[pltpu.VMEM((8, NS), F32), pltpu.VMEM((8, NS), F32)],
        compiler_params=_cparams(("arbitrary",)),
    )(eta, h, abr, abi)


GC = 0.7978845608028654
GA = 0.044715


def s5_post(hc, z, dskip):
    def fn(hv, xd, dv):
        y = hv + dv * xd
        return y, 0.5 * y * (1.0 + jnp.tanh(GC * (y + GA * y * y * y)))
    return rw(fn, [(hc, 0, 512), (z, 3072, 512)], [(512, F32), (512, BF16)], "s5_post", S, consts=[dskip])


def s5_post_bwd(dyg, ypre, z, dskip):
    def fn(dy, y, xd, dv):
        th = jnp.tanh(GC * (y + GA * y * y * y))
        dg = 0.5 * (1.0 + th) + 0.5 * y * (1.0 - th * th) * GC * (1.0 + 3.0 * GA * y * y)
        dyp = dy * dg
        return dyp, dyp * dv, jnp.sum(dyp * xd, axis=0, keepdims=True)
    return rw(fn, [(dyg, 0, 512), (ypre, 0, 512), (z, 3072, 512)], [(512, BF16), (512, F32)],
              "s5_post_bwd", S, consts=[dskip], accs=[(1, 512)])


def glu_fwd(t, z, c_out):
    def fn(t1, t2, gd, co):
        return (jnp.concatenate([co, (t1 * _sig(t2) * (gd * _sig(gd))).astype(BF16)], axis=1),)
    return rw(fn, [(t, 0, 512), (t, 512, 512), (z, 3584, 512), (c_out, 0, D)], [(D + 512, BF16)], "glu_fwd", S)[0]


def glu_bwd(t, z, d_cat):
    def fn(t1, t2, gd, dd):
        s2, sg = _sig(t2), _sig(gd)
        sl = gd * sg
        return (jnp.concatenate([dd * s2 * sl, dd * t1 * s2 * (1.0 - s2) * sl], axis=1),
                dd * t1 * s2 * (sg * (1.0 + gd * (1.0 - sg))))
    return rw(fn, [(t, 0, 512), (t, 512, 512), (z, 3584, 512), (d_cat, 1024, 512)],
              [(D, BF16), (512, F32)], "glu_bwd", S)


def assemble_dz_odd(du, dv, dgc, dxd, dgd):
    def body(a, b, c, d, e, o_ref):
        o_ref[:, 0:D] = a[...].astype(BF16)
        o_ref[:, D:2 * D] = b[...].astype(BF16)
        o_ref[:, 2 * D:3 * D] = c[...].astype(BF16)
        o_ref[:, 3 * D:3 * D + 512] = d[...].astype(BF16)
        o_ref[:, 3 * D + 512:4 * D] = e[...].astype(BF16)
    tr = 256
    blk = pl.BlockSpec((tr, D), lambda i: (i, 0))
    half = pl.BlockSpec((tr, 512), lambda i: (i, 0))
    return pl.pallas_call(
        body, name="assemble_dz_odd", grid=(S // tr,), in_specs=[blk, blk, blk, half, half],
        out_specs=pl.BlockSpec((tr, 4 * D), lambda i: (i, 0)),
        out_shape=jax.ShapeDtypeStruct((S, 4 * D), BF16),
        compiler_params=_cparams(("parallel",)),
    )(du, dv, dgc, dxd, dgd)


TQ = 256


def _xattn_probs(qh, kh):
    s = _nt(qh, kh) * 0.0625
    p = jnp.exp(s - jnp.max(s, axis=-1, keepdims=True))
    return p / jnp.sum(p, axis=-1, keepdims=True)


def xattn_fwd(q, kv):
    def body(q_ref, kv_ref, o_ref):
        for h in range(4):
            sl = slice(h * 256, (h + 1) * 256)
            p = _xattn_probs(q_ref[:, sl].astype(BF16), kv_ref[:, sl].astype(BF16))
            vh = kv_ref[:, D + h * 256:D + (h + 1) * 256].astype(BF16)
            o_ref[:, sl] = _nn(p.astype(BF16), vh).astype(o_ref.dtype)

    return pl.pallas_call(
        body, name="xattn_fwd", grid=(S // TQ,),
        in_specs=[pl.BlockSpec((TQ, D), lambda i: (i, 0)), pl.BlockSpec((MEM, 2 * D), lambda i: (0, 0))],
        out_specs=pl.BlockSpec((TQ, D), lambda i: (i, 0)),
        out_shape=jax.ShapeDtypeStruct((S, D), BF16),
        compiler_params=_cparams(("parallel",)),
    )(q, kv)


def xattn_bwd(q, kv, d_o):
    def body(q_ref, kv_ref, do_ref, dq_ref, dkv_ref):
        @pl.when(pl.program_id(0) == 0)
        def _():
            dkv_ref[...] = jnp.zeros_like(dkv_ref)

        for h in range(4):
            sl = slice(h * 256, (h + 1) * 256)
            vs = slice(D + h * 256, D + (h + 1) * 256)
            qh = q_ref[:, sl].astype(BF16)
            kh = kv_ref[:, sl].astype(BF16)
            vh = kv_ref[:, vs].astype(BF16)
            doh = do_ref[:, sl].astype(BF16)
            p = _xattn_probs(qh, kh)
            dp = _nt(doh, vh)
            ds = (p * (dp - jnp.sum(p * dp, axis=-1, keepdims=True)) * 0.0625).astype(BF16)
            dq_ref[:, sl] = _nn(ds, kh).astype(dq_ref.dtype)
            dkv_ref[:, sl] += _tn(ds, qh)
            dkv_ref[:, vs] += _tn(p.astype(BF16), doh)

    return pl.pallas_call(
        body, name="xattn_bwd", grid=(S // TQ,),
        in_specs=[pl.BlockSpec((TQ, D), lambda i: (i, 0)), pl.BlockSpec((MEM, 2 * D), lambda i: (0, 0)),
                  pl.BlockSpec((TQ, D), lambda i: (i, 0))],
        out_specs=[pl.BlockSpec((TQ, D), lambda i: (i, 0)), pl.BlockSpec((MEM, 2 * D), lambda i: (0, 0))],
        out_shape=[jax.ShapeDtypeStruct((S, D), BF16), jax.ShapeDtypeStruct((MEM, 2 * D), F32)],
        compiler_params=_cparams(("arbitrary",)),
    )(q, kv, d_o)


def _s5_disc(a_re, a_im, log_dt, b_re, b_im):
    dt = jnp.exp(log_dt)[:, None]
    mag = jnp.exp(dt * a_re)
    abr = mag * jnp.cos(dt * a_im)
    abi = mag * jnp.sin(dt * a_im)
    nr, ni = abr - 1.0, abi
    inv = 1.0 / (a_re * a_re + a_im * a_im)
    cr = (nr * a_re + ni * a_im) * inv
    ci = (ni * a_re - nr * a_im) * inv
    bbr = cr[..., None] * b_re - ci[..., None] * b_im
    bbi = cr[..., None] * b_im + ci[..., None] * b_re
    return abr, abi, bbr, bbi


def _blockdiag(t):
    g, a, b = t.shape
    eye = jnp.eye(g, dtype=t.dtype)
    return (eye[:, None, :, None] * t[:, :, None, :]).reshape(g * a, g * b)


def _blocks(mat, a, b):
    return jnp.einsum("gagb->gab", mat.reshape(NG, a, NG, b))


def _fwd_even(i, x, P, W):
    hn = rms_fwd(x, P["norm_ab"][i:i + 1], "rms_ab_fwd")
    z = mm(m2(hn), W["w_in"], "nn", "in_ab")
    o, lse, cat = attn_fwd(z)
    cat = pool_fwd(z, W["pool_w"], P["pool_scale"][i:i + 1], cat)
    x_mid = mm(m2(cat), W["w_out"], "nn", "out_ab", add=m2(x))
    return x_mid, dict(x=x, hn=hn, z=z, o=o, lse=lse, cat=cat)


def _bwd_even(i, dx_mid, sv, P, W, G, GW):
    z = sv["z"]
    d_cat = mm(m2(dx_mid), W["w_out"], "nt", "out_ab_dx")
    GW["w_out"] = mm(m2(sv["cat"]), m2(dx_mid), "tn", "out_ab_dw").reshape(4, 512, D)
    dq, dk, dv, dga = attn_bwd(z, d_cat, sv["o"], sv["lse"])
    dvb, dgb, dpw, dps = pool_bwd(z, d_cat, W["pool_w"], P["pool_scale"][i:i + 1])
    GW["pool_w"] = dpw.reshape(4, 4, 64, 256).transpose(1, 0, 2, 3).reshape(4, 256, 256)
    G["pool_scale"][i] = dps[0]
    d_z = assemble_dz_even((dq, dk, dv, dga, dvb, dgb))
    d_hn = mm(m2(d_z), W["w_in"], "nt", "in_ab_dx")
    GW["w_in"] = mm(m2(sv["hn"]), m2(d_z), "tn", "in_ab_dw", out=outcs(D, 1536))
    return d_hn, P["norm_ab"][i:i + 1], "norm_ab", "rms_ab_bwd"


def _fwd_odd(i, x, P, W):
    hn = rms_fwd(x, P["norm_cd"][i:i + 1], "rms_cd_fwd")
    z = mm(m2(hn), W["w_in"], "nn", "in_cd")
    bfull = jnp.repeat(P["sgu_b"][i].T, 256, axis=1)
    c_out = sgu_fwd(z, P["sgu_ln_g"][i:i + 1], P["sgu_ln_b"][i:i + 1], P["sgu_w"][i], bfull)
    disc, disc_vjp = jax.vjp(_s5_disc, P["s5_a_re"][i], P["s5_a_im"][i], P["s5_log_dt"][i],
                             P["s5_b_re"][i], P["s5_b_im"][i])
    abr, abi, bbr, bbi = disc
    bbd = jnp.concatenate([_blockdiag(bbr.transpose(0, 2, 1)), _blockdiag(bbi.transpose(0, 2, 1))], axis=1)
    cbd = jnp.concatenate([_blockdiag(P["s5_c_re"][i].transpose(0, 2, 1)),
                           -_blockdiag(P["s5_c_im"][i].transpose(0, 2, 1))], axis=0)
    abr, abi = abr.reshape(1, NS), abi.reshape(1, NS)
    bu = mm(m2(z, 3072, 512), m2(bbd), "nn", "s5_bu")
    h = scan_fwd(bu, abr, abi)
    hc = mm(m2(h), m2(cbd), "nn", "s5_hc")
    dskip = P["s5_d"][i:i + 1]
    ypre, yg = s5_post(hc, z, dskip)
    w12 = W["w12"]
    t = mm(m2(yg), m2(w12), "nn", "glu_t")
    cat = glu_fwd(t, z, c_out)
    x_mid = mm(m2(cat), W["w_out"], "nn", "out_cd", add=m2(x))
    return x_mid, dict(x=x, hn=hn, z=z, bfull=bfull, disc_vjp=disc_vjp, bbd=bbd, cbd=cbd, abr=abr,
                       abi=abi, h=h, ypre=ypre, yg=yg, w12=w12, t=t, cat=cat, dskip=dskip)


def _bwd_odd(i, dx_mid, sv, P, W, G, GW):
    z = sv["z"]
    d_cat = mm(m2(dx_mid), W["w_out"], "nt", "out_cd_dx")
    GW["w_out"] = mm(m2(sv["cat"]), m2(dx_mid), "tn", "out_cd_dw").reshape(4, 384, D)
    du, dv, dgc, dws, dbs, dlg, dlb = sgu_bwd(z, d_cat, P["sgu_ln_g"][i:i + 1], P["sgu_ln_b"][i:i + 1],
                                               P["sgu_w"][i], sv["bfull"])
    G["sgu_w"][i], G["sgu_b"][i] = dws, dbs[:, :4].T
    G["sgu_ln_g"][i], G["sgu_ln_b"][i] = dlg[0], dlb[0]
    dt, dgd = glu_bwd(sv["t"], z, d_cat)
    gw12 = mm(m2(sv["yg"]), m2(dt), "tn", "glu_dw")
    GW["glu_w1"] = gw12[:, :512].reshape(4, 128, 512)
    GW["glu_w2"] = gw12[:, 512:].reshape(4, 128, 512)
    dyg = mm(m2(dt), m2(sv["w12"]), "nt", "glu_dx")
    dypre, dxd1, dd = s5_post_bwd(dyg, sv["ypre"], z, sv["dskip"])
    G["s5_d"][i] = dd[0]
    gcbd = mm(m2(sv["h"]), m2(dypre), "tn", "s5_dc")
    G["s5_c_re"][i] = _blocks(gcbd[:NS], NP, NH).transpose(0, 2, 1)
    G["s5_c_im"][i] = -_blocks(gcbd[NS:], NP, NH).transpose(0, 2, 1)
    eta = mm(m2(dypre), m2(sv["cbd"]), "nt", "s5_eta")
    lam, dacc = scan_bwd(eta, sv["h"], sv["abr"], sv["abi"])
    gbbd = mm(m2(z, 3072, 512), m2(lam), "tn", "s5_db")
    dxd = mm(m2(lam), m2(sv["bbd"]), "nt", "s5_dx", add=m2(dxd1))
    dacc = jnp.sum(dacc, axis=0)
    d_bbr = _blocks(gbbd[:, :NS], NH, NP).transpose(0, 2, 1)
    d_bbi = _blocks(gbbd[:, NS:], NH, NP).transpose(0, 2, 1)
    (G["s5_a_re"][i], G["s5_a_im"][i], G["s5_log_dt"][i], G["s5_b_re"][i], G["s5_b_im"][i]) = sv["disc_vjp"](
        (dacc[:NS].reshape(NG, NP), dacc[NS:].reshape(NG, NP), d_bbr, d_bbi))
    d_z = assemble_dz_odd(du, dv, dgc, dxd, dgd)
    d_hn = mm(m2(d_z), W["w_in"], "nt", "in_cd_dx")
    GW["w_in"] = mm(m2(sv["hn"]), m2(d_z), "tn", "in_cd_dw", out=outcs(D, 1024))
    return d_hn, P["norm_cd"][i:i + 1], "norm_cd", "rms_cd_bwd"


def _fwd_x(l, x, mem_n, P, W):
    hx = rms_fwd(x, P["norm_x"][l:l + 1], "rms_x_fwd")
    q = mm(m2(hx), W["w_xq"], "nn", "xq", out_dtype=BF16)
    kv = mm(m2(mem_n), W["w_xkv"], "nn", "xkv", out_dtype=BF16)
    ox = xattn_fwd(q, kv)
    x_out = mm(m2(ox), W["w_xo"], "nn", "xo", add=m2(x))
    return x_out, dict(x=x, hx=hx, q=q, kv=kv, ox=ox)


def _bwd_x(l, dx_out, sv, mem_n, d_memn, P, W, G, GW):
    d_ox = mm(m2(dx_out), W["w_xo"], "nt", "xo_dx", out_dtype=BF16)
    GW["w_xo"] = mm(m2(sv["ox"]), m2(dx_out), "tn", "xo_dw").reshape(4, 256, D)
    dq, dkv = xattn_bwd(sv["q"], sv["kv"], d_ox)
    GW["w_xq"] = mm(m2(sv["hx"]), m2(dq), "tn", "xq_dw").reshape(4, 256, D)
    d_hx = mm(m2(dq), W["w_xq"], "nt", "xq_dx")
    GW["w_xkv"] = mm(m2(mem_n), m2(dkv), "tn", "xkv_dw", out=outcs(D, 512))
    d_memn = mm(m2(dkv), W["w_xkv"], "nt", "xkv_dx", add=None if d_memn is None else m2(d_memn))
    dx, dg = rms_bwd(sv["x"], d_hx, dx_out, P["norm_x"][l:l + 1], "rms_x_bwd")
    G["norm_x"][l] = dg[0]
    return dx, d_memn


SMALL_LAYERS = (("norm_ab", 2), ("pool_scale", 2), ("norm_cd", 2), ("sgu_ln_g", 2), ("sgu_ln_b", 2), ("sgu_w", 2),
                ("sgu_b", 2), ("s5_a_re", 2), ("s5_a_im", 2), ("s5_log_dt", 2), ("s5_b_re", 2), ("s5_b_im", 2),
                ("s5_c_re", 2), ("s5_c_im", 2), ("s5_d", 2), ("norm_x", 4))


def local_step(x, mem, tgt, P, weights_of, grads_done):
    G = {k: [None] * n for k, n in SMALL_LAYERS}
    mem_g = P["mem_norm"].reshape(1, D)
    mem_n = rms_fwd(mem, mem_g, "rms_mem_fwd")
    saved = []
    for layer in range(4):
        i = layer // 2
        W = weights_of(layer, x)
        x, sv_m = (_fwd_even if layer % 2 == 0 else _fwd_odd)(i, x, P, W)
        x, sv_x = _fwd_x(layer, x, mem_n, P, W)
        saved.append((sv_m, sv_x, W))
    dx, loss, dgf = final_loss(x, tgt, P["final_norm"].reshape(1, D))
    G["final_norm"] = dgf[0]
    d_memn = None
    for layer in reversed(range(4)):
        i = layer // 2
        sv_m, sv_x, W = saved[layer]
        GW = {}
        dx_mid, d_memn = _bwd_x(layer, dx, sv_x, mem_n, d_memn, P, W, G, GW)
        d_hn, g, key, name = (_bwd_even if layer % 2 == 0 else _bwd_odd)(i, dx_mid, sv_m, P, W, G, GW)
        token = grads_done(layer, GW)
        if token is not None:
            g = g + token
        dx, dg = rms_bwd(sv_m["x"], d_hn, dx_mid, g, name)
        G[key][i] = dg[0]
    _, dgm = rms_bwd(mem, d_memn, d_memn, mem_g, "rms_mem_bwd")
    G["mem_norm"] = dgm[0]
    return loss, dx, G


ANY = pl.BlockSpec(memory_space=pl.ANY)


def _place():
    x, y, c = lax.axis_index("x"), lax.axis_index("y"), lax.axis_index("c")
    chips = [(1 - x, y), (x, 1 - y), (1 - x, 1 - y)]
    return x, y, c, 2 * x + y, (x, y, 1 - c), chips


def _remote(src, dst, send, recv, k, dev):
    return pltpu.make_async_remote_copy(src_ref=src, dst_ref=dst, send_sem=send.at[k], recv_sem=recv.at[k],
                                        device_id=dev, device_id_type=MESHID)


def allgather_big(shards):
    n = len(shards)

    def body(*refs):
        ins, outs = refs[:n], refs[n:2 * n]
        send, recv = refs[2 * n:]
        x, y, c, jme, sib, chips = _place()

        def half(a, hc):
            lh = shards[a].shape[0] // 2
            return pl.ds(hc * lh, lh)

        first, passed = [], []
        for a in range(n):
            cp = _remote(ins[a], outs[a].at[:, jme], send, recv, a * 7 + 6, sib)
            cp.start()
            first.append(cp)
            for k, chip in enumerate(chips):
                cp = _remote(ins[a].at[half(a, c)], outs[a].at[half(a, c), jme], send, recv, a * 7 + k, (*chip, c))
                cp.start()
                first.append(cp)
        for a in range(n):
            for k, chip in enumerate(chips):
                piece = outs[a].at[half(a, c), 2 * chip[0] + chip[1]]
                _remote(piece, piece, send, recv, a * 7 + k, (*chip, c)).wait_recv()
                fw = _remote(piece, piece, send, recv, a * 7 + 3 + k, sib)
                fw.start()
                passed.append(fw)
        for a in range(n):
            own = outs[a].at[:, jme]
            _remote(own, own, send, recv, a * 7 + 6, sib).wait_recv()
            for k, chip in enumerate(chips):
                piece = outs[a].at[half(a, 1 - c), 2 * chip[0] + chip[1]]
                _remote(piece, piece, send, recv, a * 7 + 3 + k, sib).wait_recv()
        for cp in first + passed:
            cp.wait_send()

    return pl.pallas_call(
        body, name="allgather_big", in_specs=[ANY] * n, out_specs=[ANY] * n,
        out_shape=[jax.ShapeDtypeStruct((s.shape[0], 4) + s.shape[1:], s.dtype) for s in shards],
        scratch_shapes=[pltpu.SemaphoreType.DMA((7 * n,)), pltpu.SemaphoreType.DMA((7 * n,))],
    )(*shards)


def allgather_small(slab):
    def body(in_ref, out_ref, send, recv, lsem):
        x, y, c, jme, sib, chips = _place()
        loc = pltpu.make_async_copy(in_ref, out_ref.at[jme], lsem.at[0])
        loc.start()
        cps = [_remote(in_ref, out_ref.at[jme], send, recv, k, (*chip, c)) for k, chip in enumerate(chips)]
        for cp in cps:
            cp.start()
        for k, chip in enumerate(chips):
            piece = out_ref.at[2 * chip[0] + chip[1]]
            _remote(piece, piece, send, recv, k, (*chip, c)).wait_recv()
        for cp in cps:
            cp.wait_send()
        loc.wait()

    return pl.pallas_call(
        body, name="allgather_small", in_specs=[ANY], out_specs=ANY,
        out_shape=jax.ShapeDtypeStruct((4,) + slab.shape, slab.dtype),
        scratch_shapes=[pltpu.SemaphoreType.DMA((3,)), pltpu.SemaphoreType.DMA((3,)), pltpu.SemaphoreType.DMA((1,))],
    )(slab)


def allreduce_small(v):
    def body(v_ref, o_ref, r0, r1, r2, send, recv):
        x, y, c, jme, sib, chips = _place()
        peers = [sib, (1 - x, y, c), (x, 1 - y, c)]
        o_ref[...] = v_ref[...]
        for k, buf in enumerate((r0, r1, r2)):
            cp = _remote(o_ref, buf, send, recv, k, peers[k])
            cp.start()
            cp.wait()
            o_ref[...] = o_ref[...] + buf[...]

    vm = pl.BlockSpec(memory_space=pltpu.VMEM)
    return pl.pallas_call(
        body, name="allreduce_small", in_specs=[vm], out_specs=vm,
        out_shape=jax.ShapeDtypeStruct(v.shape, v.dtype),
        scratch_shapes=[pltpu.VMEM(v.shape, v.dtype)] * 3 + [pltpu.SemaphoreType.DMA((3,)), pltpu.SemaphoreType.DMA((3,))],
        compiler_params=pltpu.CompilerParams(vmem_limit_bytes=VMEM_LIMIT),
    )(v)


def rs_pair_exchange(gs):
    n = len(gs)

    def body(*refs):
        ins, outs = refs[:n], refs[n:2 * n]
        send, recv = refs[2 * n:]
        x, y, c, jme, sib, chips = _place()
        cps = [_remote(ins[a].at[:, 1 - c], outs[a], send, recv, a, sib) for a in range(n)]
        for cp in cps:
            cp.start()
        for cp in cps:
            cp.wait()

    return pl.pallas_call(
        body, name="rs_pair_exchange", in_specs=[ANY] * n, out_specs=[ANY] * n,
        out_shape=[jax.ShapeDtypeStruct((4,) + g.shape[2:], F32) for g in gs],
        scratch_shapes=[pltpu.SemaphoreType.DMA((n,)), pltpu.SemaphoreType.DMA((n,))],
    )(*gs)


def rs_pair_sum(g4, got, cidx):
    _, _, rh, cols = g4.shape
    tr = rh if rh <= 256 else 256

    def body(c_ref, a_ref, b_ref, o_ref):
        o_ref[...] = (a_ref[...] + b_ref[...]).astype(o_ref.dtype)

    return pl.pallas_call(
        body, name="rs_pair_sum",
        grid_spec=pltpu.PrefetchScalarGridSpec(
            num_scalar_prefetch=1, grid=(4, rh // tr),
            in_specs=[pl.BlockSpec((None, None, tr, cols), lambda j, t, cr: (j, cr[0], t, 0)),
                      pl.BlockSpec((None, tr, cols), lambda j, t, cr: (j, t, 0))],
            out_specs=pl.BlockSpec((None, tr, cols), lambda j, t, cr: (j, t, 0))),
        out_shape=jax.ShapeDtypeStruct((4, rh, cols), BF16),
        compiler_params=_cparams(("parallel", "parallel")),
    )(cidx, g4, got)


HBM = pl.BlockSpec(memory_space=pltpu.HBM)
SEM = pl.BlockSpec(memory_space=pltpu.SEMAPHORE)
EFFECT = pltpu.SideEffectType.DATAFLOW_SIDE_EFFECTING


def _chip_copies(ps, lands, send, recv):
    x, y, c, jme, sib, chips = _place()
    return [_remote(ps[a].at[2 * chip[0] + chip[1]], lands[a].at[jme], send, recv, a * 3 + k, (*chip, c))
            for a in range(len(ps)) for k, chip in enumerate(chips)]


def rs_chip_start(ps, name):
    n = len(ps)

    def body(*refs):
        ins, lands = refs[:n], refs[n:2 * n]
        send, recv = refs[2 * n], refs[2 * n + 1]
        token = refs[-1]
        for cp in _chip_copies(ins, lands, send, recv):
            cp.start()
        token[...] = jnp.zeros_like(token)

    hbm = lambda t: pltpu.with_memory_space_constraint(t, pltpu.HBM)
    res = pl.pallas_call(
        body, name=name,
        out_shape=(pltpu.SemaphoreType.DMA((3 * n,)), pltpu.SemaphoreType.DMA((3 * n,)),
                   *[pltpu.HBM(p.shape, p.dtype) for p in ps], *[pltpu.HBM(p.shape, p.dtype) for p in ps],
                   jax.ShapeDtypeStruct((8, 128), F32)),
        in_specs=[HBM] * (2 * n), out_specs=(SEM, SEM, *[HBM] * (2 * n), pl.BlockSpec(memory_space=pltpu.VMEM)),
        input_output_aliases={a: 2 + a for a in range(2 * n)},
        compiler_params=pltpu.CompilerParams(has_side_effects=EFFECT),
    )(*[hbm(p) for p in ps], *[hbm(lax.empty(p.shape, p.dtype)) for p in ps])
    return res[0], res[1], list(res[2:2 + n]), list(res[2 + n:2 + 2 * n]), res[-1]


def rs_chip_wait(send, recv, ps, lands, after, name):
    n = len(ps)

    def body(*refs):
        ins, zones = refs[:n], refs[n:2 * n]
        send_r, recv_r = refs[2 * n], refs[2 * n + 1]
        x, y, c, jme, sib, chips = _place()
        for a in range(n):
            for k, chip in enumerate(chips):
                jt = 2 * chip[0] + chip[1]
                cp = _remote(ins[a].at[jt], zones[a].at[jt], send_r, recv_r, a * 3 + k, (*chip, c))
                cp.wait_send()
                cp.wait_recv()

    res = pl.pallas_call(
        body, name=name,
        out_shape=tuple(pltpu.HBM(p.shape, p.dtype) for p in list(ps) + list(lands)),
        in_specs=[HBM] * (2 * n) + [SEM, SEM, ANY], out_specs=tuple([HBM] * (2 * n)),
        input_output_aliases={a: a for a in range(2 * n)},
        compiler_params=pltpu.CompilerParams(has_side_effects=EFFECT),
    )(*ps, *lands, send, recv, after)
    return list(res[n:])


def rs_chip_sum(q, p, l, acc, layers, jc):
    _, rh, cols = q.shape
    tr = rh if rh <= 256 else 256

    def body(jc_ref, q_ref, p_ref, *rest):
        o_ref = rest[-1]
        jme = jc_ref[0]
        own = p_ref[...].astype(F32)
        v = [jnp.where(jme == j, own, q_ref[j].astype(F32)) for j in range(4)]
        o_ref[...] = ((v[0] + v[1]) + v[2]) + v[3]

    in_specs = [pl.BlockSpec((4, tr, cols), lambda t, jr: (0, t, 0)),
                pl.BlockSpec((None, tr, cols), lambda t, jr: (jr[0], t, 0))]
    args = [jc, q, p]
    if acc is not None:
        in_specs.append(ANY)
        args.append(acc)
    return pl.pallas_call(
        body, name="rs_chip_sum",
        grid_spec=pltpu.PrefetchScalarGridSpec(
            num_scalar_prefetch=1, grid=(rh // tr,), in_specs=in_specs,
            out_specs=pl.BlockSpec((None, None, tr, cols), lambda t, jr: (l, jr[1], t, 0))),
        out_shape=jax.ShapeDtypeStruct((layers, 2, rh, cols), F32),
        input_output_aliases={} if acc is None else {3: 0},
        compiler_params=_cparams(("parallel",)),
    )(*args)


def rs_pair_gather(rs):
    n = len(rs)

    def body(*refs):
        outs = refs[n:2 * n]
        send, recv = refs[2 * n:]
        x, y, c, jme, sib, chips = _place()
        cps = [_remote(outs[a].at[:, c], outs[a].at[:, c], send, recv, a, sib) for a in range(n)]
        for cp in cps:
            cp.start()
        for a in range(n):
            slot = outs[a].at[:, 1 - c]
            _remote(slot, slot, send, recv, a, sib).wait_recv()
        for cp in cps:
            cp.wait_send()

    return pl.pallas_call(
        body, name="rs_pair_gather", in_specs=[ANY] * n, out_specs=[ANY] * n,
        out_shape=[jax.ShapeDtypeStruct(r.shape, r.dtype) for r in rs],
        input_output_aliases={a: a for a in range(n)},
        scratch_shapes=[pltpu.SemaphoreType.DMA((n,)), pltpu.SemaphoreType.DMA((n,))],
    )(*rs)


def _adamw_math(w, g, m, v):
    m = B1 * m + (1.0 - B1) * g
    v = B2 * v + (1.0 - B2) * (g * g)
    m_hat = m / (1.0 - B1 ** STEP)
    v_hat = v / (1.0 - B2 ** STEP)
    return -LR * (m_hat / (jnp.sqrt(v_hat) + AEPS) + WD * w), m, v


def adamw(w, g, m, v, name):
    rows, cols = w.shape
    tr = 256 if rows % 256 == 0 else rows
    return rw(_adamw_math, [(a, 0, cols) for a in (w, g, m, v)], [(cols, F32)] * 3, name, rows, tr=tr)


WEIGHTS = ["norm_ab", "w_in_ab", "pool_w", "pool_scale", "w_out_ab", "norm_cd", "w_in_cd", "sgu_ln_g", "sgu_ln_b",
           "sgu_w", "sgu_b", "s5_a_re", "s5_a_im", "s5_log_dt", "s5_b_re", "s5_b_im", "s5_c_re", "s5_c_im", "s5_d",
           "glu_w1", "glu_w2", "w_out_cd", "norm_x", "w_xq", "w_xkv", "w_xo", "mem_norm", "final_norm"]
INPUTS = ["x", "mem"] + WEIGHTS + ["loss_target"] + ["m_" + n for n in WEIGHTS] + ["v_" + n for n in WEIGHTS]
BIG = ["w_in_ab", "w_out_ab", "w_in_cd", "w_out_cd", "w_xq", "w_xkv", "w_xo", "glu_w1", "glu_w2", "pool_w"]
COL_SHARDED = ("w_in_ab", "w_in_cd", "w_xkv")
SMALL = [n for n in WEIGHTS if n not in BIG]
SMALL_SHARDED = {"norm_cd": 256, "sgu_ln_g": 256, "sgu_ln_b": 256, "s5_d": 128}
PACK = 256 * 128


def _pack(arrs):
    flat = jnp.concatenate([a.reshape(-1) for a in arrs])
    pad = (-flat.shape[0]) % PACK
    return jnp.concatenate([flat, jnp.zeros((pad,), flat.dtype)]).reshape(-1, 128)


def _unpack(packed, shapes):
    flat, out, off = packed.reshape(-1), [], 0
    for s in shapes:
        n = 1
        for d in s:
            n *= d
        out.append(flat[off:off + n].reshape(s))
        off += n
    return out


def _weight_of(key, layer):
    if key in ("w_xq", "w_xkv", "w_xo"):
        return key, layer, 4
    kind = "ab" if layer % 2 == 0 else "cd"
    return {"w_in": "w_in_" + kind, "w_out": "w_out_" + kind}.get(key, key), layer // 2, 2


def _mat(a):
    return a.reshape(a.shape[0], -1, a.shape[-1])


def kernel(*args):
    a = dict(zip(INPUTS, args))
    x_i, y_i, c_i = lax.axis_index("x"), lax.axis_index("y"), lax.axis_index("c")
    j = 2 * x_i + y_i

    gathered = dict(zip(BIG, allgather_big([_mat(a[n]).astype(BF16) for n in BIG])))
    slab = jnp.concatenate([a["norm_cd"], a["sgu_ln_g"], a["sgu_ln_b"],
                            jnp.pad(a["s5_d"], ((0, 0), (0, 128)))], axis=0)
    gslab = allgather_small(slab)
    P = {n: a[n] for n in SMALL}
    for k, n in enumerate(("norm_cd", "sgu_ln_g", "sgu_ln_b", "s5_d")):
        wd = SMALL_SHARDED[n]
        P[n] = gslab[:, 2 * k:2 * k + 2, :wd].transpose(1, 0, 2).reshape(2, 4 * wd)
    for n in BIG:
        g = gathered[n]
        if n in COL_SHARDED:
            P[n] = g
        elif n == "pool_w":
            P[n] = g.reshape(2, 4, 4, 64, 256).transpose(0, 2, 1, 3, 4).reshape(2, 4, 256, 256)
        else:
            P[n] = g.reshape(g.shape[0], 4 * g.shape[2], g.shape[3])

    cidx = jnp.reshape(c_i, (1,)).astype(jnp.int32)
    jc = jnp.stack([j, c_i]).astype(jnp.int32)

    def weights_of(layer, x_in):
        i = layer // 2
        if layer % 2 == 0:
            W = {"w_in": mcs(P["w_in_ab"], i), "w_out": m3(P["w_out_ab"], i), "pool_w": P["pool_w"][i]}
        else:
            W = {"w_in": mcs(P["w_in_cd"], i), "w_out": m3(P["w_out_cd"], i),
                 "w12": jnp.concatenate([P["glu_w1"][i], P["glu_w2"][i]], axis=1)}
        W.update(w_xq=m3(P["w_xq"], layer), w_xkv=mcs(P["w_xkv"], layer), w_xo=m3(P["w_xo"], layer))
        return W

    pending = {}

    def grads_done(layer, GW):
        keys = sorted(GW)
        flat = [GW[k].reshape(4, 2, GW[k].shape[1] // 2, GW[k].shape[2]) for k in keys]
        pair = [rs_pair_sum(g4, r, cidx) for g4, r in zip(flat, rs_pair_exchange(flat))]
        send, recv, pair, lands, token = rs_chip_start(pair, "rs_chip_start_%d" % layer)
        pending[layer] = (keys, send, recv, pair, lands)
        return token[0:1, 0:1]

    loss, dx, G = local_step(a["x"][0], a["mem"][0], a["loss_target"][0], P, weights_of, grads_done)
    loss = lax.psum(loss[0, 0], ("x", "y", "c"))

    red = {}
    for layer in (3, 2, 1, 0):
        keys, send, recv, pair, lands = pending[layer]
        lands = rs_chip_wait(send, recv, pair, lands, dx, "rs_chip_wait_%d" % layer)
        for k, q, p in zip(keys, lands, pair):
            n, l, layers = _weight_of(k, layer)
            red[n] = rs_chip_sum(q, p, l, red.get(n), layers, jc)
    gbig = dict(zip(BIG, rs_pair_gather([red[n] for n in BIG])))

    outs = {}
    for n in BIG:
        shp = a[n].shape
        g2 = gbig[n].reshape(-1, shp[-1])
        d2, m2_, v2_ = adamw(a[n].reshape(g2.shape), g2, a["m_" + n].reshape(g2.shape),
                             a["v_" + n].reshape(g2.shape), "adamw_" + n)
        outs[n] = tuple(t.reshape(shp) for t in (g2, d2, m2_, v2_))

    gfull = [jnp.stack(G[n]) if isinstance(G[n], list) else G[n] for n in SMALL]
    shapes = [g.shape for g in gfull]
    gsum = _unpack(allreduce_small(_pack(gfull)), shapes)
    gloc = []
    for n, g in zip(SMALL, gsum):
        if n in SMALL_SHARDED:
            g = lax.dynamic_slice_in_dim(g, j * SMALL_SHARDED[n], SMALL_SHARDED[n], axis=1)
        gloc.append(g)
    lshapes = [a[n].shape for n in SMALL]
    packed = [_pack(t) for t in ([a[n] for n in SMALL], gloc, [a["m_" + n] for n in SMALL], [a["v_" + n] for n in SMALL])]
    small = [_unpack(t, lshapes) for t in adamw(*packed, "adamw_small")]
    for k, n in enumerate(SMALL):
        outs[n] = (gloc[k], small[0][k], small[1][k], small[2][k])

    res = [loss, dx[None]]
    for part in range(4):
        res += [outs[n][part] for n in WEIGHTS]
    return tuple(res)
```

```python
import math

import jax
import jax.numpy as jnp
from jax import lax
from jax.experimental import pallas as pl
from jax.experimental.pallas import tpu as pltpu

F32, BF16 = jnp.float32, jnp.bfloat16
S, D = 2048, 1024
MEM = 256
EPS = 1e-6
NEG = -1e30
QB = 128
PATTERNS = (1, 4, 16)
NG, NP, NH = 32, 64, 16
NS = NG * NP
LR, B1, B2, AEPS, WD, STEP = 0.001, 0.9, 0.999, 1e-08, 0.01, 10
MESHID = pl.DeviceIdType.MESH
VMEM_LIMIT = 56 * 1024 * 1024


def _cparams(sem):
    return pltpu.CompilerParams(dimension_semantics=sem, vmem_limit_bytes=VMEM_LIMIT)


def _sig(x):
    return 1.0 / (1.0 + jnp.exp(-x))


def _dot(a, b, dims):
    return lax.dot_general(a, b, (dims, ((), ())), preferred_element_type=F32)


def _nn(a, b):
    return _dot(a, b, ((1,), (0,)))


def _nt(a, b):
    return _dot(a, b, ((1,), (1,)))


def _tn(a, b):
    return _dot(a, b, ((0,), (0,)))


_DIMS = {"nn": ((1,), (0,)), "nt": ((1,), (1,)), "tn": ((0,), (0,))}


def _tile(dim, cc=None, cap=1024):
    for t in (2048, 1536, 1024, 768, 512, 384, 256, 128):
        if t <= cap and dim % t == 0 and (cc is None or cc % t == 0):
            return t
    return dim


MM_VMEM = 36 * 1024 * 1024


def _mm_tiles(m, n, k, ccm, ccn, cck, a_bytes, b_bytes, o_bytes):
    caps = [1024, 1024, 2048]
    while True:
        tm, tn, tk = _tile(m, ccm, caps[0]), _tile(n, ccn, caps[1]), _tile(k, cck, caps[2])
        need = 2 * (tm * tk * a_bytes + tk * tn * b_bytes + tm * tn * o_bytes) + (tm * tn * 4 if tk < k else 0)
        if need <= MM_VMEM:
            return tm, tn, tk
        if tk > 1024:
            caps[2] = tk // 2
        elif tn >= tm:
            caps[1] = tn // 2
        else:
            caps[0] = tm // 2


def m2(arr, col_off=0, ncols=None):
    rows, cols = arr.shape
    ncols = cols - col_off if ncols is None else ncols

    def spec(tr, tc, rc):
        assert col_off % tc == 0
        return pl.BlockSpec((tr, tc), lambda *g: (rc(*g)[0], rc(*g)[1] + col_off // tc))
    return (arr, rows, ncols, spec, None if col_off == 0 else col_off)


def mcs(arr):
    cs = arr.shape[2]

    def spec(tr, tc, rc):
        n = cs // tc
        return pl.BlockSpec((None, tr, tc), lambda *g: (rc(*g)[1] // n, rc(*g)[0], rc(*g)[1] % n))
    return (arr, arr.shape[1], 4 * cs, spec, cs)


def out2(rows, cols):
    def spec(tr, tc, rc):
        return pl.BlockSpec((tr, tc), lambda *g: tuple(rc(*g)))
    return ((rows, cols), spec, None)


def outcs(rows, cs):
    def spec(tr, tc, rc):
        n = cs // tc
        return pl.BlockSpec((None, tr, tc), lambda *g: (rc(*g)[1] // n, rc(*g)[0], rc(*g)[1] % n))
    return ((4, rows, cs), spec, cs)


def _both(a, b):
    if a is None:
        return b
    if b is None:
        return a
    return math.gcd(a, b)


def mm(a, b, mode, name, add=None, out=None, out_dtype=F32):
    a_arr, a_r, a_c, a_spec, a_cc = a
    b_arr, b_r, b_c, b_spec, b_cc = b
    if mode == "nn":
        m, k, n = a_r, a_c, b_c
        assert b_r == k
        ccm, cck, ccn = None, a_cc, b_cc
    elif mode == "nt":
        m, k, n = a_r, a_c, b_r
        assert b_c == k
        ccm, cck, ccn = None, _both(a_cc, b_cc), None
    else:
        m, k, n = a_c, a_r, b_c
        assert b_r == k
        ccm, cck, ccn = a_cc, None, b_cc
    out = out2(m, n) if out is None else out
    o_shape, o_spec, o_cc = out
    ccn = _both(ccn, o_cc)
    if add is not None:
        ccn = _both(ccn, add[4])
    o_bytes = jnp.dtype(out_dtype).itemsize + (0 if add is None else add[0].dtype.itemsize)
    tm, tn, tk = _mm_tiles(m, n, k, ccm, ccn, cck, a_arr.dtype.itemsize, b_arr.dtype.itemsize, o_bytes)
    nk = k // tk
    if mode == "nn":
        in_specs = [a_spec(tm, tk, lambda i, j, kk: (i, kk)), b_spec(tk, tn, lambda i, j, kk: (kk, j))]
    elif mode == "nt":
        in_specs = [a_spec(tm, tk, lambda i, j, kk: (i, kk)), b_spec(tn, tk, lambda i, j, kk: (j, kk))]
    else:
        in_specs = [a_spec(tk, tm, lambda i, j, kk: (kk, i)), b_spec(tk, tn, lambda i, j, kk: (kk, j))]
    args = [a_arr, b_arr]
    if add is not None:
        in_specs.append(add[3](tm, tn, lambda i, j, kk: (i, j)))
        args.append(add[0])
    dims = _DIMS[mode]
    has_add = add is not None

    def body(*refs):
        a_ref, b_ref = refs[0], refs[1]
        add_ref = refs[2] if has_add else None
        prod = _dot(a_ref[...].astype(BF16), b_ref[...].astype(BF16), dims)
        if nk == 1:
            o_ref = refs[-1]
            if has_add:
                prod = prod + add_ref[...].astype(F32)
            o_ref[...] = prod.astype(o_ref.dtype)
            return
        o_ref, acc = refs[-2], refs[-1]
        kk = pl.program_id(2)

        @pl.when(kk == 0)
        def _():
            acc[...] = prod

        @pl.when(kk > 0)
        def _():
            acc[...] += prod

        @pl.when(kk == nk - 1)
        def _():
            r = acc[...]
            if has_add:
                r = r + add_ref[...].astype(F32)
            o_ref[...] = r.astype(o_ref.dtype)

    return pl.pallas_call(
        body, name=name, grid=(m // tm, n // tn, nk), in_specs=in_specs,
        out_specs=o_spec(tm, tn, lambda i, j, kk: (i, j)),
        out_shape=jax.ShapeDtypeStruct(o_shape, out_dtype),
        scratch_shapes=[pltpu.VMEM((tm, tn), F32)] if nk > 1 else [],
        compiler_params=_cparams(("parallel", "parallel", "arbitrary")),
    )(*args)


def rw(fn, ins, outs, name, rows, tr=256, consts=(), accs=()):
    n_in, n_c, n_o, n_a = len(ins), len(consts), len(outs), len(accs)
    in_specs = []
    for arr, off, width in ins:
        assert off % width == 0
        in_specs.append(pl.BlockSpec((tr, width), lambda i, o=off // width: (i, o)))
    for c in consts:
        in_specs.append(pl.BlockSpec(c.shape, lambda i: (0, 0)))
    out_specs = [pl.BlockSpec((tr, w), lambda i: (i, 0)) for w, _ in outs]
    out_specs += [pl.BlockSpec(s, lambda i: (0, 0)) for s in accs]
    out_shape = [jax.ShapeDtypeStruct((rows, w), dt) for w, dt in outs]
    out_shape += [jax.ShapeDtypeStruct(s, F32) for s in accs]

    def body(*refs):
        vals = [r[...] for r in refs[:n_in + n_c]]
        o_refs = refs[n_in + n_c:n_in + n_c + n_o]
        a_refs = refs[n_in + n_c + n_o:]
        res = fn(*vals)
        for r, v in zip(o_refs, res[:n_o]):
            r[...] = v.astype(r.dtype)
        if n_a:
            @pl.when(pl.program_id(0) == 0)
            def _():
                for r in a_refs:
                    r[...] = jnp.zeros_like(r)
            for r, v in zip(a_refs, res[n_o:]):
                r[...] += v

    res = pl.pallas_call(
        body, name=name, grid=(rows // tr,), in_specs=in_specs, out_specs=out_specs,
        out_shape=out_shape,
        compiler_params=_cparams(("arbitrary",) if n_a else ("parallel",)),
    )(*[a for a, _, _ in ins], *consts)
    return res


def _rstd(x):
    return lax.rsqrt(jnp.mean(x * x, axis=-1, keepdims=True) + EPS)


def rms_fwd(x, g, name):
    def fn(xv, gv):
        xv = xv.astype(F32)
        return (xv * _rstd(xv) * gv,)
    return rw(fn, [(x, 0, D)], [(D, BF16)], name, x.shape[0], consts=[g])[0]


def _rms_bwd_math(xv, dy, gv):
    r = _rstd(xv)
    dyg = dy * gv
    dx = r * dyg - xv * (r * r * r / D) * jnp.sum(dyg * xv, axis=-1, keepdims=True)
    dg = jnp.sum(dy * xv * r, axis=0, keepdims=True)
    return dx, dg


def rms_bwd(x, dy, dres, g, name):
    def fn(xv, dyv, drv, gv):
        dx, dg = _rms_bwd_math(xv, dyv, gv)
        return dx + drv, dg
    return rw(fn, [(x, 0, D), (dy, 0, D), (dres, 0, D)], [(D, F32)], name, x.shape[0],
              consts=[g], accs=[(1, D)])


def final_loss(x, tgt, g):
    def fn(xv, tv, gv):
        e = xv * _rstd(xv) * gv - tv
        loss = 0.5 * jnp.sum(jnp.sum(e * e, axis=-1, keepdims=True), axis=0, keepdims=True) / D
        dx, dg = _rms_bwd_math(xv, e / D, gv)
        return dx, loss, dg
    return rw(fn, [(x, 0, D), (tgt, 0, D)], [(D, F32)], "final_loss", S, consts=[g],
              accs=[(1, 1), (1, D)])


def _attn_bias(bias_ref):
    ii = lax.broadcasted_iota(jnp.int32, (2 * QB, 2 * QB), 0) % QB
    jj = lax.broadcasted_iota(jnp.int32, (2 * QB, 2 * QB), 1)
    dist = ii + QB - jj
    band = (dist >= 0) & (dist <= QB)
    bias_ref[1] = jnp.where(band, 0.0, NEG)
    bias_ref[0] = jnp.where(band & (jj >= QB), 0.0, NEG)


def _two_heads(x, m0):
    return jnp.concatenate([jnp.where(m0, x, 0.0), jnp.where(m0, 0.0, x)], axis=0)


def _per_head(col, m0):
    return jnp.where(m0, col[:QB], col[QB:])


def _attn_rows(idx, d):
    if d == 1:
        b = idx
        cur = pl.ds(pl.multiple_of(b * QB, QB), QB)
        prev = pl.ds(pl.multiple_of(jnp.maximum(b - 1, 0) * QB, QB), QB)
    else:
        r, b = lax.rem(idx, d), lax.div(idx, d)
        cur = pl.ds(r + b * (QB * d), QB, stride=d)
        prev = pl.ds(r + jnp.maximum(b - 1, 0) * (QB * d), QB, stride=d)
    return cur, prev, b


NBLK = S // QB
GROUP = 4


def _colblk(off):
    return pl.BlockSpec((S, 128), lambda hp: (0, off * 8 + hp))


def attn_fwd(z):
    def body(q_ref, k_ref, v_ref, g_ref, o_ref, l_ref, a_ref, os, ls, bias):
        _attn_bias(bias)
        m0 = lax.broadcasted_iota(jnp.int32, (1, 128), 1) < 64
        for pi, d in enumerate(PATTERNS):
            def load(idx, d=d):
                cur, prev, b = _attn_rows(idx, d)
                return cur, (q_ref[cur, :], k_ref[prev, :], k_ref[cur, :], v_ref[prev, :], v_ref[cur, :],
                             bias[jnp.minimum(b, 1)])

            def block(q, kp, kc, vp, vc, bs):
                qq = _two_heads(q * 0.125, m0).astype(BF16)
                k = jnp.concatenate([kp, kc], axis=0).astype(BF16)
                s = _nt(qq, k) + bs
                mx = jnp.max(s, axis=-1, keepdims=True)
                p = jnp.exp(s - mx)
                den = jnp.sum(p, axis=-1, keepdims=True)
                pb = p.astype(BF16)
                vv = _two_heads(jnp.concatenate([vp, vc], axis=0), m0).astype(BF16)
                o = _nn(jnp.concatenate([pb[:QB], pb[QB:]], axis=1), vv)
                return o * _per_head(1.0 / den, m0), _per_head(mx + jnp.log(den), m0)

            def step(i, carry, pi=pi):
                loaded = [load(i * GROUP + u) for u in range(GROUP)]
                done = [block(*vals) for _, vals in loaded]
                for (cur, _), (o, l) in zip(loaded, done):
                    os[pi, cur, :] = o
                    ls[pi, cur, :] = l
                return carry
            lax.fori_loop(0, NBLK // GROUP, step, 0)
        l1, l2, l3 = ls[0], ls[1], ls[2]
        mx = jnp.maximum(jnp.maximum(l1, l2), l3)
        e1, e2, e3 = jnp.exp(l1 - mx), jnp.exp(l2 - mx), jnp.exp(l3 - mx)
        tot = e1 + e2 + e3
        o = (os[0] * e1 + os[1] * e2 + os[2] * e3) / tot
        ga = g_ref[...]
        o_ref[...] = o
        l_ref[...] = mx + jnp.log(tot)
        a_ref[...] = (o * (ga * _sig(ga))).astype(a_ref.dtype)

    out = pl.BlockSpec((S, 128), lambda hp: (0, hp))
    return pl.pallas_call(
        body, name="attn_fwd", grid=(8,),
        in_specs=[_colblk(0), _colblk(1), _colblk(2), _colblk(3)], out_specs=[out] * 3,
        out_shape=[jax.ShapeDtypeStruct((S, D), F32), jax.ShapeDtypeStruct((S, D), F32),
                   jax.ShapeDtypeStruct((S, 2 * D), BF16)],
        scratch_shapes=[pltpu.VMEM((3, S, 128), F32), pltpu.VMEM((3, S, 128), F32),
                        pltpu.VMEM((2, 2 * QB, 2 * QB), F32)],
        compiler_params=_cparams(("parallel",)),
    )(z, z, z, z)


def attn_bwd(z, d_cat, o, lse):
    def body(q_ref, k_ref, v_ref, g_ref, da_ref, o_ref, l_ref, dq_ref, dk_ref, dv_ref, dg_ref, do_s, pr_s, bias):
        _attn_bias(bias)
        m0 = lax.broadcasted_iota(jnp.int32, (1, 128), 1) < 64
        ga = g_ref[...]
        sg = _sig(ga)
        da = da_ref[...]
        ov = o_ref[...]
        do = da * (ga * sg)
        dg_ref[...] = da * ov * (sg * (1.0 + ga * (1.0 - sg)))
        do_s[...] = do
        pr_s[...] = do * ov
        dq_ref[...] = jnp.zeros_like(dq_ref)
        dk_ref[...] = jnp.zeros_like(dk_ref)
        dv_ref[...] = jnp.zeros_like(dv_ref)
        for d in PATTERNS:
            def load(idx, d=d):
                cur, prev, b = _attn_rows(idx, d)
                return (cur, prev), (q_ref[cur, :], k_ref[prev, :], k_ref[cur, :], v_ref[prev, :], v_ref[cur, :],
                                     do_s[cur, :], pr_s[cur, :], l_ref[cur, :], bias[jnp.minimum(b, 1)])

            def block(q, kp, kc, vp, vc, dof, prod, lp, bs):
                qq = _two_heads(q * 0.125, m0).astype(BF16)
                kf = jnp.concatenate([kp, kc], axis=0)
                k = kf.astype(BF16)
                v = jnp.concatenate([vp, vc], axis=0).astype(BF16)
                dd = _two_heads(dof, m0).astype(BF16)
                lh = jnp.max(jnp.concatenate([jnp.where(m0, lp, -jnp.inf), jnp.where(m0, -jnp.inf, lp)], axis=0),
                             axis=-1, keepdims=True)
                delta = jnp.sum(_two_heads(prod, m0), axis=-1, keepdims=True)
                p = jnp.exp(_nt(qq, k) + bs - lh)
                ds = (p * (_nt(dd, v) - delta)).astype(BF16)
                dq = _nn(jnp.concatenate([ds[:QB], ds[QB:]], axis=1), _two_heads(kf, m0).astype(BF16))
                return dq * 0.125, _tn(ds, qq), _tn(p.astype(BF16), dd)

            def step(i, carry):
                loaded = [load(i * GROUP + u) for u in range(GROUP)]
                done = [block(*vals) for _, vals in loaded]
                for ((cur, prev), _), (dq, dk, dv) in zip(loaded, done):
                    dq_ref[cur, :] = dq_ref[cur, :] + dq
                    dk_ref[prev, :] = dk_ref[prev, :] + dk[:QB]
                    dv_ref[prev, :] = dv_ref[prev, :] + dv[:QB]
                    dk_ref[cur, :] = dk_ref[cur, :] + dk[QB:]
                    dv_ref[cur, :] = dv_ref[cur, :] + dv[QB:]
                return carry
            lax.fori_loop(0, NBLK // GROUP, step, 0)

    blk = pl.BlockSpec((S, 128), lambda hp: (0, hp))
    return pl.pallas_call(
        body, name="attn_bwd", grid=(8,),
        in_specs=[_colblk(0), _colblk(1), _colblk(2), _colblk(3), blk, blk, blk], out_specs=[blk] * 4,
        out_shape=[jax.ShapeDtypeStruct((S, D), F32)] * 4,
        scratch_shapes=[pltpu.VMEM((S, 128), F32), pltpu.VMEM((S, 128), F32), pltpu.VMEM((2, 2 * QB, 2 * QB), F32)],
        compiler_params=_cparams(("parallel",)),
    )(z, z, z, z, d_cat, o, lse)


def assemble_dz_even(parts):
    def body(*refs):
        o_ref = refs[-1]
        for j in range(6):
            o_ref[:, j * D:(j + 1) * D] = refs[j][...].astype(o_ref.dtype)
    tr = 256
    blk = pl.BlockSpec((tr, D), lambda i: (i, 0))
    return pl.pallas_call(
        body, name="assemble_dz_even", grid=(S // tr,), in_specs=[blk] * 6,
        out_specs=pl.BlockSpec((tr, 6 * D), lambda i: (i, 0)),
        out_shape=jax.ShapeDtypeStruct((S, 6 * D), BF16),
        compiler_params=_cparams(("parallel",)),
    )(*parts)


def _pool_window(g):
    return jnp.where(g == 0, 2.0, jnp.where(g == 1, 4.0, jnp.where(g == 2, 8.0, 16.0)))


def _pool_sel(g, levels):
    return jnp.where(g == 0, levels[0], jnp.where(g == 1, levels[1], jnp.where(g == 2, levels[2], levels[3])))


def _pool_fwd_math(v, g):
    t = lax.broadcasted_iota(jnp.int32, (S, 1), 0)
    s = v
    levels = []
    for k in (1, 2, 4, 8):
        s = s + jnp.where(t >= k, pltpu.roll(s, k, 0), 0.0)
        levels.append(s)
    cnt = jnp.minimum((t + 1).astype(F32), _pool_window(g))
    return _pool_sel(g, levels) / cnt - v, cnt


def pool_fwd(z, pw, ps, cat):
    def body(v_ref, g_ref, pw_ref, ps_ref, cat_ref, o_ref):
        g = pl.program_id(0)
        pooled, _ = _pool_fwd_math(v_ref[...], g)
        mixed = _nn(pooled.astype(BF16), pw_ref[...].astype(BF16))
        gb = g_ref[...]
        o_ref[...] = (mixed * ps_ref[...] * (gb * _sig(gb))).astype(o_ref.dtype)

    return pl.pallas_call(
        body, name="pool_fwd", grid=(4,),
        in_specs=[pl.BlockSpec((S, 256), lambda g: (0, 16 + g)),
                  pl.BlockSpec((S, 256), lambda g: (0, 20 + g)),
                  pl.BlockSpec((None, 256, 256), lambda g: (g, 0, 0)),
                  pl.BlockSpec((1, 256), lambda g: (0, g)), pl.BlockSpec(memory_space=pl.ANY)],
        out_specs=pl.BlockSpec((S, 256), lambda g: (0, 4 + g)),
        out_shape=jax.ShapeDtypeStruct((S, 2 * D), BF16),
        input_output_aliases={4: 0},
        compiler_params=_cparams(("parallel",)),
    )(z, z, pw, ps, cat)


def pool_bwd(z, d_cat, pw, ps):
    def body(v_ref, g_ref, d_ref, pw_ref, ps_ref, dv_ref, dg_ref, dpw_ref, dps_ref):
        g = pl.program_id(0)
        v = v_ref[...]
        pooled, cnt = _pool_fwd_math(v, g)
        pwb = pw_ref[...].astype(BF16)
        pb = pooled.astype(BF16)
        mixed = _nn(pb, pwb)
        gb = g_ref[...]
        sg = _sig(gb)
        dout = d_ref[...]
        sc = ps_ref[...]
        dg_ref[...] = dout * mixed * sc * (sg * (1.0 + gb * (1.0 - sg)))
        dms = dout * (gb * sg)
        dps_ref[...] = jnp.sum(dms * mixed, axis=0, keepdims=True)
        dmx = (dms * sc).astype(BF16)
        dpw_ref[...] = _tn(pb, dmx)
        dpooled = _nt(dmx, pwb)
        t = lax.broadcasted_iota(jnp.int32, (S, 1), 0)
        s = dpooled / cnt
        levels = []
        for k in (1, 2, 4, 8):
            s = s + jnp.where(t < S - k, pltpu.roll(s, S - k, 0), 0.0)
            levels.append(s)
        dv_ref[...] = _pool_sel(g, levels) - dpooled

    return pl.pallas_call(
        body, name="pool_bwd", grid=(4,),
        in_specs=[pl.BlockSpec((S, 256), lambda g: (0, 16 + g)),
                  pl.BlockSpec((S, 256), lambda g: (0, 20 + g)),
                  pl.BlockSpec((S, 256), lambda g: (0, 4 + g)),
                  pl.BlockSpec((None, 256, 256), lambda g: (g, 0, 0)),
                  pl.BlockSpec((1, 256), lambda g: (0, g))],
        out_specs=[pl.BlockSpec((S, 256), lambda g: (0, g)),
                   pl.BlockSpec((S, 256), lambda g: (0, g)),
                   pl.BlockSpec((None, 256, 256), lambda g: (g, 0, 0)),
                   pl.BlockSpec((1, 256), lambda g: (0, g))],
        out_shape=[jax.ShapeDtypeStruct((S, D), F32), jax.ShapeDtypeStruct((S, D), F32),
                   jax.ShapeDtypeStruct((4, 256, 256), F32), jax.ShapeDtypeStruct((1, D), F32)],
        compiler_params=_cparams(("parallel",)),
    )(z, z, d_cat, pw, ps)


CH = 128


def _sgu_common(v, lng, lnb, w_ref):
    mu = jnp.mean(v, axis=-1, keepdims=True)
    vc = v - mu
    rs = lax.rsqrt(jnp.mean(vc * vc, axis=-1, keepdims=True) + EPS)
    xhat = vc * rs
    vn = (xhat * lng + lnb).astype(BF16)
    ri = lax.broadcasted_iota(jnp.int32, (CH, CH), 0)
    ci = lax.broadcasted_iota(jnp.int32, (CH, CH), 1)
    tril = ri >= ci
    ws = [jnp.where(tril, w_ref[g], 0.0).astype(BF16) for g in range(4)]
    return xhat, rs, vn, tril, ws


def _zspec(off):
    return pl.BlockSpec((CH, D), lambda c: (c, off))


def _full(shape):
    return pl.BlockSpec(shape, lambda c: (0,) * len(shape))


def sgu_fwd(z, lng, lnb, w, bfull):
    def body(u_ref, v_ref, g_ref, lng_ref, lnb_ref, w_ref, b_ref, o_ref):
        _, _, vn, _, ws = _sgu_common(v_ref[...], lng_ref[...], lnb_ref[...], w_ref)
        for g in range(4):
            sl = slice(g * 256, (g + 1) * 256)
            mixed = _nn(ws[g], vn[:, sl]) + b_ref[:, sl]
            gc = g_ref[:, sl]
            o_ref[:, sl] = (u_ref[:, sl] * mixed * (gc * _sig(gc))).astype(o_ref.dtype)

    return pl.pallas_call(
        body, name="sgu_fwd", grid=(S // CH,),
        in_specs=[_zspec(0), _zspec(1), _zspec(2), _full((1, D)), _full((1, D)),
                  _full((4, CH, CH)), _full((CH, D))],
        out_specs=pl.BlockSpec((CH, D), lambda c: (c, 0)),
        out_shape=jax.ShapeDtypeStruct((S, D), BF16),
        compiler_params=_cparams(("parallel",)),
    )(z, z, z, lng, lnb, w, bfull)


def sgu_bwd(z, d_cat, lng, lnb, w, bfull):
    def body(u_ref, v_ref, g_ref, d_ref, lng_ref, lnb_ref, w_ref, b_ref,
             du_ref, dv_ref, dg_ref, dw_ref, db_ref, dlg_ref, dlb_ref):
        @pl.when(pl.program_id(0) == 0)
        def _():
            dw_ref[...] = jnp.zeros_like(dw_ref)
            db_ref[...] = jnp.zeros_like(db_ref)
            dlg_ref[...] = jnp.zeros_like(dlg_ref)
            dlb_ref[...] = jnp.zeros_like(dlb_ref)

        lng = lng_ref[...]
        xhat, rs, vn, tril, ws = _sgu_common(v_ref[...], lng, lnb_ref[...], w_ref)
        lane = lax.broadcasted_iota(jnp.int32, (1, 128), 1)
        db = jnp.zeros((CH, 128), F32)
        dvn_parts = []
        for g in range(4):
            sl = slice(g * 256, (g + 1) * 256)
            mixed = _nn(ws[g], vn[:, sl]) + b_ref[:, sl]
            gc = g_ref[:, sl]
            sg = _sig(gc)
            u = u_ref[:, sl]
            dc = d_ref[:, sl]
            du_ref[:, sl] = dc * mixed * (gc * sg)
            dg_ref[:, sl] = dc * u * mixed * (sg * (1.0 + gc * (1.0 - sg)))
            dmx = dc * u * (gc * sg)
            db = db + jnp.where(lane == g, jnp.sum(dmx, axis=-1, keepdims=True), 0.0)
            dmb = dmx.astype(BF16)
            dw_ref[g] += jnp.where(tril, _nt(dmb, vn[:, sl]), 0.0)
            dvn_parts.append(_tn(ws[g], dmb))
        db_ref[...] += db
        dvn = jnp.concatenate(dvn_parts, axis=1)
        dlb_ref[...] += jnp.sum(dvn, axis=0, keepdims=True)
        dlg_ref[...] += jnp.sum(dvn * xhat, axis=0, keepdims=True)
        dxh = dvn * lng
        dv_ref[...] = rs * (dxh - jnp.mean(dxh, axis=-1, keepdims=True)
                            - xhat * jnp.mean(dxh * xhat, axis=-1, keepdims=True))

    row = pl.BlockSpec((CH, D), lambda c: (c, 0))
    return pl.pallas_call(
        body, name="sgu_bwd", grid=(S // CH,),
        in_specs=[_zspec(0), _zspec(1), _zspec(2), row, _full((1, D)), _full((1, D)),
                  _full((4, CH, CH)), _full((CH, D))],
        out_specs=[row, row, row, _full((4, CH, CH)), _full((CH, 128)), _full((1, D)), _full((1, D))],
        out_shape=[jax.ShapeDtypeStruct((S, D), F32)] * 3
        + [jax.ShapeDtypeStruct((4, CH, CH), F32), jax.ShapeDtypeStruct((CH, 128), F32),
           jax.ShapeDtypeStruct((1, D), F32), jax.ShapeDtypeStruct((1, D), F32)],
        compiler_params=_cparams(("arbitrary",)),
    )(z, z, z, d_cat, lng, lnb, w, bfull)


TB = 256


def _cmul(ar, ai, br, bi):
    return ar * br - ai * bi, ar * bi + ai * br


def _scan_consts(ar, ai, reverse):
    a2 = _cmul(ar, ai, ar, ai)
    a4 = _cmul(*a2, *a2)
    row = lax.broadcasted_iota(jnp.int32, (8, NS), 0)
    pr = jnp.zeros((8, NS), F32)
    pi = jnp.zeros((8, NS), F32)
    cr, ci = ar, ai
    for r in range(8):
        sel = row == (7 - r if reverse else r)
        pr = jnp.where(sel, cr, pr)
        pi = jnp.where(sel, ci, pi)
        cr, ci = _cmul(cr, ci, ar, ai)
    return ((ar, ai), a2, a4), (pr, pi), row


def scan_fwd(bu, abr, abi):
    def body(bu_ref, ar_ref, ai_ref, h_ref, car, cai):
        @pl.when(pl.program_id(0) == 0)
        def _():
            car[...] = jnp.zeros_like(car)
            cai[...] = jnp.zeros_like(cai)

        pows, (pr, pi), row = _scan_consts(ar_ref[...], ai_ref[...], False)

        def tile(t, carry):
            c_r, c_i = carry
            rows = pl.ds(pl.multiple_of(t * 8, 8), 8)
            xr = bu_ref[rows, 0:NS]
            xi = bu_ref[rows, NS:2 * NS]
            for k, (kr, ki) in zip((1, 2, 4), pows):
                sr = jnp.where(row >= k, pltpu.roll(xr, k, 0), 0.0)
                si = jnp.where(row >= k, pltpu.roll(xi, k, 0), 0.0)
                xr, xi = xr + kr * sr - ki * si, xi + kr * si + ki * sr
            xr, xi = xr + pr * c_r - pi * c_i, xi + pr * c_i + pi * c_r
            h_ref[rows, 0:NS] = xr
            h_ref[rows, NS:2 * NS] = xi
            return (jnp.broadcast_to(xr[7:8, :], (8, NS)), jnp.broadcast_to(xi[7:8, :], (8, NS)))

        c_r, c_i = lax.fori_loop(0, TB // 8, tile, (car[...], cai[...]))
        car[...] = c_r
        cai[...] = c_i

    return pl.pallas_call(
        body, name="s5_scan_fwd", grid=(S // TB,),
        in_specs=[pl.BlockSpec((TB, 2 * NS), lambda i: (i, 0)),
                  pl.BlockSpec((1, NS), lambda i: (0, 0)), pl.BlockSpec((1, NS), lambda i: (0, 0))],
        out_specs=pl.BlockSpec((TB, 2 * NS), lambda i: (i, 0)),
        out_shape=jax.ShapeDtypeStruct((S, 2 * NS), F32),
        scratch_shapes=[pltpu.VMEM((8, NS), F32), pltpu.VMEM((8, NS), F32)],
        compiler_params=_cparams(("arbitrary",)),
    )(bu, abr, abi)


def scan_bwd(eta, h, abr, abi):
    nt = S // TB

    def body(e_ref, h_ref, ar_ref, ai_ref, l_ref, da_ref, car, cai):
        @pl.when(pl.program_id(0) == 0)
        def _():
            car[...] = jnp.zeros_like(car)
            cai[...] = jnp.zeros_like(cai)
            da_ref[...] = jnp.zeros_like(da_ref)

        pows, (pr, pi), row = _scan_consts(ar_ref[...], -ai_ref[...], True)

        def tile(tt, carry):
            c_r, c_i, acr, aci = carry
            t = TB // 8 - 1 - tt
            rows = pl.ds(pl.multiple_of(t * 8, 8), 8)
            xr = e_ref[rows, 0:NS]
            xi = e_ref[rows, NS:2 * NS]
            for k, (kr, ki) in zip((1, 2, 4), pows):
                sr = jnp.where(row < 8 - k, pltpu.roll(xr, 8 - k, 0), 0.0)
                si = jnp.where(row < 8 - k, pltpu.roll(xi, 8 - k, 0), 0.0)
                xr, xi = xr + kr * sr - ki * si, xi + kr * si + ki * sr
            xr, xi = xr + pr * c_r - pi * c_i, xi + pr * c_i + pi * c_r
            l_ref[rows, 0:NS] = xr
            l_ref[rows, NS:2 * NS] = xi
            nr = jnp.where(row < 7, pltpu.roll(xr, 7, 0), c_r)
            ni = jnp.where(row < 7, pltpu.roll(xi, 7, 0), c_i)
            hr = h_ref[rows, 0:NS]
            hi = h_ref[rows, NS:2 * NS]
            acr = acr + hr * nr + hi * ni
            aci = aci + hr * ni - hi * nr
            return (jnp.broadcast_to(xr[0:1, :], (8, NS)), jnp.broadcast_to(xi[0:1, :], (8, NS)), acr, aci)

        zero = jnp.zeros((8, NS), F32)
        c_r, c_i, acr, aci = lax.fori_loop(0, TB // 8, tile, (car[...], cai[...], zero, zero))
        car[...] = c_r
        cai[...] = c_i
        da_ref[:, 0:NS] += acr
        da_ref[:, NS:2 * NS] += aci

    rev = pl.BlockSpec((TB, 2 * NS), lambda i: (nt - 1 - i, 0))
    return pl.pallas_call(
        body, name="s5_scan_bwd", grid=(nt,),
        in_specs=[rev, rev, pl.BlockSpec((1, NS), lambda i: (0, 0)), pl.BlockSpec((1, NS), lambda i: (0, 0))],
        out_specs=[rev, pl.BlockSpec((8, 2 * NS), lambda i: (0, 0))],
        out_shape=[jax.ShapeDtypeStruct((S, 2 * NS), F32), jax.ShapeDtypeStruct((8, 2 * NS), F32)],
        scratch_shapes=[pltpu.VMEM((8, NS), F32), pltpu.VMEM((8, NS), F32)],
        compiler_params=_cparams(("arbitrary",)),
    )(eta, h, abr, abi)


GC = 0.7978845608028654
GA = 0.044715


def s5_post(hc, z, dskip):
    def fn(hv, xd, dv):
        y = hv + dv * xd
        return y, 0.5 * y * (1.0 + jnp.tanh(GC * (y + GA * y * y * y)))
    return rw(fn, [(hc, 0, 512), (z, 3072, 512)], [(512, F32), (512, BF16)], "s5_post", S, consts=[dskip])


def s5_post_bwd(dyg, ypre, z, dskip):
    def fn(dy, y, xd, dv):
        th = jnp.tanh(GC * (y + GA * y * y * y))
        dg = 0.5 * (1.0 + th) + 0.5 * y * (1.0 - th * th) * GC * (1.0 + 3.0 * GA * y * y)
        dyp = dy * dg
        return dyp, dyp * dv, jnp.sum(dyp * xd, axis=0, keepdims=True)
    return rw(fn, [(dyg, 0, 512), (ypre, 0, 512), (z, 3072, 512)], [(512, BF16), (512, F32)],
              "s5_post_bwd", S, consts=[dskip], accs=[(1, 512)])


def glu_fwd(t, z, c_out):
    def fn(t1, t2, gd, co):
        return (jnp.concatenate([co, (t1 * _sig(t2) * (gd * _sig(gd))).astype(BF16)], axis=1),)
    return rw(fn, [(t, 0, 512), (t, 512, 512), (z, 3584, 512), (c_out, 0, D)], [(D + 512, BF16)], "glu_fwd", S)[0]


def glu_bwd(t, z, d_cat):
    def fn(t1, t2, gd, dd):
        s2, sg = _sig(t2), _sig(gd)
        sl = gd * sg
        return (jnp.concatenate([dd * s2 * sl, dd * t1 * s2 * (1.0 - s2) * sl], axis=1),
                dd * t1 * s2 * (sg * (1.0 + gd * (1.0 - sg))))
    return rw(fn, [(t, 0, 512), (t, 512, 512), (z, 3584, 512), (d_cat, 1024, 512)],
              [(D, BF16), (512, F32)], "glu_bwd", S)


def assemble_dz_odd(du, dv, dgc, dxd, dgd):
    def body(a, b, c, d, e, o_ref):
        o_ref[:, 0:D] = a[...].astype(BF16)
        o_ref[:, D:2 * D] = b[...].astype(BF16)
        o_ref[:, 2 * D:3 * D] = c[...].astype(BF16)
        o_ref[:, 3 * D:3 * D + 512] = d[...].astype(BF16)
        o_ref[:, 3 * D + 512:4 * D] = e[...].astype(BF16)
    tr = 256
    blk = pl.BlockSpec((tr, D), lambda i: (i, 0))
    half = pl.BlockSpec((tr, 512), lambda i: (i, 0))
    return pl.pallas_call(
        body, name="assemble_dz_odd", grid=(S // tr,), in_specs=[blk, blk, blk, half, half],
        out_specs=pl.BlockSpec((tr, 4 * D), lambda i: (i, 0)),
        out_shape=jax.ShapeDtypeStruct((S, 4 * D), BF16),
        compiler_params=_cparams(("parallel",)),
    )(du, dv, dgc, dxd, dgd)


TQ = 256


def _xattn_probs(qh, kh):
    s = _nt(qh, kh) * 0.0625
    p = jnp.exp(s - jnp.max(s, axis=-1, keepdims=True))
    return p / jnp.sum(p, axis=-1, keepdims=True)


def xattn_fwd(q, kv):
    def body(q_ref, kv_ref, o_ref):
        for h in range(4):
            sl = slice(h * 256, (h + 1) * 256)
            p = _xattn_probs(q_ref[:, sl].astype(BF16), kv_ref[:, sl].astype(BF16))
            vh = kv_ref[:, D + h * 256:D + (h + 1) * 256].astype(BF16)
            o_ref[:, sl] = _nn(p.astype(BF16), vh).astype(o_ref.dtype)

    return pl.pallas_call(
        body, name="xattn_fwd", grid=(S // TQ,),
        in_specs=[pl.BlockSpec((TQ, D), lambda i: (i, 0)), pl.BlockSpec((MEM, 2 * D), lambda i: (0, 0))],
        out_specs=pl.BlockSpec((TQ, D), lambda i: (i, 0)),
        out_shape=jax.ShapeDtypeStruct((S, D), BF16),
        compiler_params=_cparams(("parallel",)),
    )(q, kv)


def xattn_bwd(q, kv, d_o):
    def body(q_ref, kv_ref, do_ref, dq_ref, dkv_ref):
        @pl.when(pl.program_id(0) == 0)
        def _():
            dkv_ref[...] = jnp.zeros_like(dkv_ref)

        for h in range(4):
            sl = slice(h * 256, (h + 1) * 256)
            vs = slice(D + h * 256, D + (h + 1) * 256)
            qh = q_ref[:, sl].astype(BF16)
            kh = kv_ref[:, sl].astype(BF16)
            vh = kv_ref[:, vs].astype(BF16)
            doh = do_ref[:, sl].astype(BF16)
            p = _xattn_probs(qh, kh)
            dp = _nt(doh, vh)
            ds = (p * (dp - jnp.sum(p * dp, axis=-1, keepdims=True)) * 0.0625).astype(BF16)
            dq_ref[:, sl] = _nn(ds, kh).astype(dq_ref.dtype)
            dkv_ref[:, sl] += _tn(ds, qh)
            dkv_ref[:, vs] += _tn(p.astype(BF16), doh)

    return pl.pallas_call(
        body, name="xattn_bwd", grid=(S // TQ,),
        in_specs=[pl.BlockSpec((TQ, D), lambda i: (i, 0)), pl.BlockSpec((MEM, 2 * D), lambda i: (0, 0)),
                  pl.BlockSpec((TQ, D), lambda i: (i, 0))],
        out_specs=[pl.BlockSpec((TQ, D), lambda i: (i, 0)), pl.BlockSpec((MEM, 2 * D), lambda i: (0, 0))],
        out_shape=[jax.ShapeDtypeStruct((S, D), BF16), jax.ShapeDtypeStruct((MEM, 2 * D), F32)],
        compiler_params=_cparams(("arbitrary",)),
    )(q, kv, d_o)


def _s5_disc(a_re, a_im, log_dt, b_re, b_im):
    dt = jnp.exp(log_dt)[:, None]
    mag = jnp.exp(dt * a_re)
    abr = mag * jnp.cos(dt * a_im)
    abi = mag * jnp.sin(dt * a_im)
    nr, ni = abr - 1.0, abi
    inv = 1.0 / (a_re * a_re + a_im * a_im)
    cr = (nr * a_re + ni * a_im) * inv
    ci = (ni * a_re - nr * a_im) * inv
    bbr = cr[..., None] * b_re - ci[..., None] * b_im
    bbi = cr[..., None] * b_im + ci[..., None] * b_re
    return abr, abi, bbr, bbi


def _blockdiag(t):
    g, a, b = t.shape
    eye = jnp.eye(g, dtype=t.dtype)
    return (eye[:, None, :, None] * t[:, :, None, :]).reshape(g * a, g * b)


def _blocks(mat, a, b):
    return jnp.einsum("gagb->gab", mat.reshape(NG, a, NG, b))


def _fwd_even(i, x, P, W):
    hn = rms_fwd(x, P["norm_ab"][i:i + 1], "rms_ab_fwd")
    z = mm(m2(hn), W["w_in"], "nn", "in_ab")
    o, lse, cat = attn_fwd(z)
    cat = pool_fwd(z, W["pool_w"], P["pool_scale"][i:i + 1], cat)
    x_mid = mm(m2(cat), W["w_out"], "nn", "out_ab", add=m2(x))
    return x_mid, dict(x=x, hn=hn, z=z, o=o, lse=lse, cat=cat)


def _bwd_even(i, dx_mid, sv, P, W, G, GW):
    z = sv["z"]
    d_cat = mm(m2(dx_mid), W["w_out"], "nt", "out_ab_dx")
    GW["w_out"] = mm(m2(sv["cat"]), m2(dx_mid), "tn", "out_ab_dw").reshape(4, 512, D)
    dq, dk, dv, dga = attn_bwd(z, d_cat, sv["o"], sv["lse"])
    dvb, dgb, dpw, dps = pool_bwd(z, d_cat, W["pool_w"], P["pool_scale"][i:i + 1])
    GW["pool_w"] = dpw.reshape(4, 4, 64, 256).transpose(1, 0, 2, 3).reshape(4, 256, 256)
    G["pool_scale"][i] = dps[0]
    d_z = assemble_dz_even((dq, dk, dv, dga, dvb, dgb))
    d_hn = mm(m2(d_z), W["w_in"], "nt", "in_ab_dx")
    GW["w_in"] = mm(m2(sv["hn"]), m2(d_z), "tn", "in_ab_dw", out=outcs(D, 1536))
    return d_hn, P["norm_ab"][i:i + 1], "norm_ab", "rms_ab_bwd"


def _fwd_odd(i, x, P, W):
    hn = rms_fwd(x, P["norm_cd"][i:i + 1], "rms_cd_fwd")
    z = mm(m2(hn), W["w_in"], "nn", "in_cd")
    bfull = jnp.repeat(P["sgu_b"][i].T, 256, axis=1)
    c_out = sgu_fwd(z, P["sgu_ln_g"][i:i + 1], P["sgu_ln_b"][i:i + 1], P["sgu_w"][i], bfull)
    disc, disc_vjp = jax.vjp(_s5_disc, P["s5_a_re"][i], P["s5_a_im"][i], P["s5_log_dt"][i],
                             P["s5_b_re"][i], P["s5_b_im"][i])
    abr, abi, bbr, bbi = disc
    bbd = jnp.concatenate([_blockdiag(bbr.transpose(0, 2, 1)), _blockdiag(bbi.transpose(0, 2, 1))], axis=1)
    cbd = jnp.concatenate([_blockdiag(P["s5_c_re"][i].transpose(0, 2, 1)),
                           -_blockdiag(P["s5_c_im"][i].transpose(0, 2, 1))], axis=0)
    abr, abi = abr.reshape(1, NS), abi.reshape(1, NS)
    bu = mm(m2(z, 3072, 512), m2(bbd), "nn", "s5_bu")
    h = scan_fwd(bu, abr, abi)
    hc = mm(m2(h), m2(cbd), "nn", "s5_hc")
    dskip = P["s5_d"][i:i + 1]
    ypre, yg = s5_post(hc, z, dskip)
    w12 = W["w12"]
    t = mm(m2(yg), m2(w12), "nn", "glu_t")
    cat = glu_fwd(t, z, c_out)
    x_mid = mm(m2(cat), W["w_out"], "nn", "out_cd", add=m2(x))
    return x_mid, dict(x=x, hn=hn, z=z, bfull=bfull, disc_vjp=disc_vjp, bbd=bbd, cbd=cbd, abr=abr,
                       abi=abi, h=h, ypre=ypre, yg=yg, w12=w12, t=t, cat=cat, dskip=dskip)


def _bwd_odd(i, dx_mid, sv, P, W, G, GW):
    z = sv["z"]
    d_cat = mm(m2(dx_mid), W["w_out"], "nt", "out_cd_dx")
    GW["w_out"] = mm(m2(sv["cat"]), m2(dx_mid), "tn", "out_cd_dw").reshape(4, 384, D)
    du, dv, dgc, dws, dbs, dlg, dlb = sgu_bwd(z, d_cat, P["sgu_ln_g"][i:i + 1], P["sgu_ln_b"][i:i + 1],
                                               P["sgu_w"][i], sv["bfull"])
    G["sgu_w"][i], G["sgu_b"][i] = dws, dbs[:, :4].T
    G["sgu_ln_g"][i], G["sgu_ln_b"][i] = dlg[0], dlb[0]
    dt, dgd = glu_bwd(sv["t"], z, d_cat)
    gw12 = mm(m2(sv["yg"]), m2(dt), "tn", "glu_dw")
    GW["glu_w1"] = gw12[:, :512].reshape(4, 128, 512)
    GW["glu_w2"] = gw12[:, 512:].reshape(4, 128, 512)
    dyg = mm(m2(dt), m2(sv["w12"]), "nt", "glu_dx")
    dypre, dxd1, dd = s5_post_bwd(dyg, sv["ypre"], z, sv["dskip"])
    G["s5_d"][i] = dd[0]
    gcbd = mm(m2(sv["h"]), m2(dypre), "tn", "s5_dc")
    G["s5_c_re"][i] = _blocks(gcbd[:NS], NP, NH).transpose(0, 2, 1)
    G["s5_c_im"][i] = -_blocks(gcbd[NS:], NP, NH).transpose(0, 2, 1)
    eta = mm(m2(dypre), m2(sv["cbd"]), "nt", "s5_eta")
    lam, dacc = scan_bwd(eta, sv["h"], sv["abr"], sv["abi"])
    gbbd = mm(m2(z, 3072, 512), m2(lam), "tn", "s5_db")
    dxd = mm(m2(lam), m2(sv["bbd"]), "nt", "s5_dx", add=m2(dxd1))
    dacc = jnp.sum(dacc, axis=0)
    d_bbr = _blocks(gbbd[:, :NS], NH, NP).transpose(0, 2, 1)
    d_bbi = _blocks(gbbd[:, NS:], NH, NP).transpose(0, 2, 1)
    (G["s5_a_re"][i], G["s5_a_im"][i], G["s5_log_dt"][i], G["s5_b_re"][i], G["s5_b_im"][i]) = sv["disc_vjp"](
        (dacc[:NS].reshape(NG, NP), dacc[NS:].reshape(NG, NP), d_bbr, d_bbi))
    d_z = assemble_dz_odd(du, dv, dgc, dxd, dgd)
    d_hn = mm(m2(d_z), W["w_in"], "nt", "in_cd_dx")
    GW["w_in"] = mm(m2(sv["hn"]), m2(d_z), "tn", "in_cd_dw", out=outcs(D, 1024))
    return d_hn, P["norm_cd"][i:i + 1], "norm_cd", "rms_cd_bwd"


def _fwd_x(l, x, mem_n, P, W):
    hx = rms_fwd(x, P["norm_x"][l:l + 1], "rms_x_fwd")
    q = mm(m2(hx), W["w_xq"], "nn", "xq", out_dtype=BF16)
    kv = mm(m2(mem_n), W["w_xkv"], "nn", "xkv", out_dtype=BF16)
    ox = xattn_fwd(q, kv)
    x_out = mm(m2(ox), W["w_xo"], "nn", "xo", add=m2(x))
    return x_out, dict(x=x, hx=hx, q=q, kv=kv, ox=ox)


def _bwd_x(l, dx_out, sv, mem_n, d_memn, P, W, G, GW):
    d_ox = mm(m2(dx_out), W["w_xo"], "nt", "xo_dx", out_dtype=BF16)
    GW["w_xo"] = mm(m2(sv["ox"]), m2(dx_out), "tn", "xo_dw").reshape(4, 256, D)
    dq, dkv = xattn_bwd(sv["q"], sv["kv"], d_ox)
    GW["w_xq"] = mm(m2(sv["hx"]), m2(dq), "tn", "xq_dw").reshape(4, 256, D)
    d_hx = mm(m2(dq), W["w_xq"], "nt", "xq_dx")
    GW["w_xkv"] = mm(m2(mem_n), m2(dkv), "tn", "xkv_dw", out=outcs(D, 512))
    d_memn = mm(m2(dkv), W["w_xkv"], "nt", "xkv_dx", add=None if d_memn is None else m2(d_memn))
    dx, dg = rms_bwd(sv["x"], d_hx, dx_out, P["norm_x"][l:l + 1], "rms_x_bwd")
    G["norm_x"][l] = dg[0]
    return dx, d_memn


SMALL_LAYERS = (("norm_ab", 2), ("pool_scale", 2), ("norm_cd", 2), ("sgu_ln_g", 2), ("sgu_ln_b", 2), ("sgu_w", 2),
                ("sgu_b", 2), ("s5_a_re", 2), ("s5_a_im", 2), ("s5_log_dt", 2), ("s5_b_re", 2), ("s5_b_im", 2),
                ("s5_c_re", 2), ("s5_c_im", 2), ("s5_d", 2), ("norm_x", 4))


def local_step(x, mem, tgt, P, weights_of, grads_done):
    G = {k: [None] * n for k, n in SMALL_LAYERS}
    mem_g = P["mem_norm"].reshape(1, D)
    mem_n = rms_fwd(mem, mem_g, "rms_mem_fwd")
    saved = []
    for layer in range(4):
        i = layer // 2
        W = weights_of(layer, x)
        x, sv_m = (_fwd_even if layer % 2 == 0 else _fwd_odd)(i, x, P, W)
        x, sv_x = _fwd_x(layer, x, mem_n, P, W)
        saved.append((sv_m, sv_x, W))
    dx, loss, dgf = final_loss(x, tgt, P["final_norm"].reshape(1, D))
    G["final_norm"] = dgf[0]
    d_memn = None
    for layer in reversed(range(4)):
        i = layer // 2
        sv_m, sv_x, W = saved[layer]
        GW = {}
        dx_mid, d_memn = _bwd_x(layer, dx, sv_x, mem_n, d_memn, P, W, G, GW)
        d_hn, g, key, name = (_bwd_even if layer % 2 == 0 else _bwd_odd)(i, dx_mid, sv_m, P, W, G, GW)
        token = grads_done(layer, GW)
        if token is not None:
            g = g + token
        dx, dg = rms_bwd(sv_m["x"], d_hn, dx_mid, g, name)
        G[key][i] = dg[0]
    _, dgm = rms_bwd(mem, d_memn, d_memn, mem_g, "rms_mem_bwd")
    G["mem_norm"] = dgm[0]
    return loss, dx, G


ANY = pl.BlockSpec(memory_space=pl.ANY)


def _place():
    x, y, c = lax.axis_index("x"), lax.axis_index("y"), lax.axis_index("c")
    chips = [(1 - x, y), (x, 1 - y), (1 - x, 1 - y)]
    return x, y, c, 2 * x + y, (x, y, 1 - c), chips


def _remote(src, dst, send, recv, k, dev):
    return pltpu.make_async_remote_copy(src_ref=src, dst_ref=dst, send_sem=send.at[k], recv_sem=recv.at[k],
                                        device_id=dev, device_id_type=MESHID)


HBM = pl.BlockSpec(memory_space=pltpu.HBM)
SEM = pl.BlockSpec(memory_space=pltpu.SEMAPHORE)
EFFECT = pltpu.SideEffectType.DATAFLOW_SIDE_EFFECTING


def _hbm(t):
    return pltpu.with_memory_space_constraint(t, pltpu.HBM)


def allgather_sync(shards):
    n = len(shards)

    def body(*refs):
        ins, outs = refs[:n], refs[n:2 * n]
        token, send, recv = refs[2 * n:]
        x, y, c, jme, sib, chips = _place()
        first, passed = [], []
        for a in range(n):
            cp = _remote(ins[a], outs[a].at[jme], send, recv, a * 7 + 6, sib)
            cp.start()
            first.append(cp)
            for k, chip in enumerate(chips):
                cp = _remote(ins[a].at[c], outs[a].at[jme, c], send, recv, a * 7 + k, (*chip, c))
                cp.start()
                first.append(cp)
        for a in range(n):
            for k, chip in enumerate(chips):
                piece = outs[a].at[2 * chip[0] + chip[1], c]
                _remote(piece, piece, send, recv, a * 7 + k, (*chip, c)).wait_recv()
                fw = _remote(piece, piece, send, recv, a * 7 + 3 + k, sib)
                fw.start()
                passed.append(fw)
        for a in range(n):
            own = outs[a].at[jme]
            _remote(own, own, send, recv, a * 7 + 6, sib).wait_recv()
            for k, chip in enumerate(chips):
                piece = outs[a].at[2 * chip[0] + chip[1], 1 - c]
                _remote(piece, piece, send, recv, a * 7 + 3 + k, sib).wait_recv()
        for cp in first + passed:
            cp.wait_send()
        token[...] = jnp.zeros_like(token)

    res = pl.pallas_call(
        body, name="allgather_sync", in_specs=[ANY] * n,
        out_specs=[ANY] * n + [pl.BlockSpec(memory_space=pltpu.VMEM)],
        out_shape=[jax.ShapeDtypeStruct((4,) + s.shape, s.dtype) for s in shards] + [jax.ShapeDtypeStruct((8, 128), F32)],
        scratch_shapes=[pltpu.SemaphoreType.DMA((7 * n,)), pltpu.SemaphoreType.DMA((7 * n,))],
    )(*shards)
    return list(res[:n]), res[n]


def _gather_copies(ins, lands, send, recv):
    x, y, c, jme, sib, chips = _place()
    devs = [(*chip, c) for chip in chips] + [sib]
    return [_remote(ins[a], lands[a].at[jme], send, recv, a * 4 + k, dev)
            for a in range(len(ins)) for k, dev in enumerate(devs)]


def allgather_start(shards, after, name):
    n = len(shards)

    def body(*refs):
        ins, lands = refs[:n], refs[n:2 * n]
        send, recv = refs[2 * n + 1], refs[2 * n + 2]
        token = refs[-1]
        for cp in _gather_copies(ins, lands, send, recv):
            cp.start()
        token[...] = jnp.zeros_like(token)

    res = pl.pallas_call(
        body, name=name,
        out_shape=(pltpu.SemaphoreType.DMA((4 * n,)), pltpu.SemaphoreType.DMA((4 * n,)),
                   *[pltpu.HBM(s.shape, s.dtype) for s in shards],
                   *[pltpu.HBM((4,) + s.shape, s.dtype) for s in shards],
                   jax.ShapeDtypeStruct((8, 128), F32)),
        in_specs=[HBM] * (2 * n) + [ANY],
        out_specs=(SEM, SEM, *[HBM] * (2 * n), pl.BlockSpec(memory_space=pltpu.VMEM)),
        input_output_aliases={a: 2 + a for a in range(2 * n)},
        compiler_params=pltpu.CompilerParams(has_side_effects=EFFECT),
    )(*[_hbm(s) for s in shards], *[_hbm(lax.empty((4,) + s.shape, s.dtype)) for s in shards], after)
    return res[0], res[1], list(res[2:2 + n]), list(res[2 + n:2 + 2 * n]), res[-1]


def allgather_wait(send, recv, shards, lands, after, name):
    n = len(shards)

    def body(*refs):
        ins, zones = refs[:n], refs[n:2 * n]
        send_r, recv_r = refs[2 * n], refs[2 * n + 1]
        x, y, c, jme, sib, chips = _place()
        slots = [2 * chip[0] + chip[1] for chip in chips] + [jme]
        for a in range(n):
            for k, slot in enumerate(slots):
                cp = _remote(ins[a], zones[a].at[slot], send_r, recv_r, a * 4 + k, sib)
                cp.wait_send()
                cp.wait_recv()

    res = pl.pallas_call(
        body, name=name,
        out_shape=tuple(pltpu.HBM(t.shape, t.dtype) for t in list(shards) + list(lands)),
        in_specs=[HBM] * (2 * n) + [SEM, SEM, ANY], out_specs=tuple([HBM] * (2 * n)),
        input_output_aliases={a: a for a in range(2 * n)},
        compiler_params=pltpu.CompilerParams(has_side_effects=EFFECT),
    )(*shards, *lands, send, recv, after)
    return list(res[n:])


def allgather_small(slab):
    def body(in_ref, out_ref, send, recv, lsem):
        x, y, c, jme, sib, chips = _place()
        loc = pltpu.make_async_copy(in_ref, out_ref.at[jme], lsem.at[0])
        loc.start()
        cps = [_remote(in_ref, out_ref.at[jme], send, recv, k, (*chip, c)) for k, chip in enumerate(chips)]
        for cp in cps:
            cp.start()
        for k, chip in enumerate(chips):
            piece = out_ref.at[2 * chip[0] + chip[1]]
            _remote(piece, piece, send, recv, k, (*chip, c)).wait_recv()
        for cp in cps:
            cp.wait_send()
        loc.wait()

    return pl.pallas_call(
        body, name="allgather_small", in_specs=[ANY], out_specs=ANY,
        out_shape=jax.ShapeDtypeStruct((4,) + slab.shape, slab.dtype),
        scratch_shapes=[pltpu.SemaphoreType.DMA((3,)), pltpu.SemaphoreType.DMA((3,)), pltpu.SemaphoreType.DMA((1,))],
    )(slab)


def allreduce_small(v):
    def body(v_ref, o_ref, r0, r1, r2, send, recv):
        x, y, c, jme, sib, chips = _place()
        peers = [sib, (1 - x, y, c), (x, 1 - y, c)]
        o_ref[...] = v_ref[...]
        for k, buf in enumerate((r0, r1, r2)):
            cp = _remote(o_ref, buf, send, recv, k, peers[k])
            cp.start()
            cp.wait()
            o_ref[...] = o_ref[...] + buf[...]

    vm = pl.BlockSpec(memory_space=pltpu.VMEM)
    return pl.pallas_call(
        body, name="allreduce_small", in_specs=[vm], out_specs=vm,
        out_shape=jax.ShapeDtypeStruct(v.shape, v.dtype),
        scratch_shapes=[pltpu.VMEM(v.shape, v.dtype)] * 3 + [pltpu.SemaphoreType.DMA((3,)), pltpu.SemaphoreType.DMA((3,))],
        compiler_params=pltpu.CompilerParams(vmem_limit_bytes=VMEM_LIMIT),
    )(v)


def rs_pair_exchange(gs):
    n = len(gs)

    def body(*refs):
        ins, outs = refs[:n], refs[n:2 * n]
        send, recv = refs[2 * n:]
        x, y, c, jme, sib, chips = _place()
        cps = [_remote(ins[a].at[:, 1 - c], outs[a], send, recv, a, sib) for a in range(n)]
        for cp in cps:
            cp.start()
        for cp in cps:
            cp.wait()

    return pl.pallas_call(
        body, name="rs_pair_exchange", in_specs=[ANY] * n, out_specs=[ANY] * n,
        out_shape=[jax.ShapeDtypeStruct((4,) + g.shape[2:], F32) for g in gs],
        scratch_shapes=[pltpu.SemaphoreType.DMA((n,)), pltpu.SemaphoreType.DMA((n,))],
    )(*gs)


def rs_pair_sum(g4, got, cidx):
    _, _, rh, cols = g4.shape
    tr = rh if rh <= 256 else 256

    def body(c_ref, a_ref, b_ref, o_ref):
        o_ref[...] = (a_ref[...] + b_ref[...]).astype(o_ref.dtype)

    return pl.pallas_call(
        body, name="rs_pair_sum",
        grid_spec=pltpu.PrefetchScalarGridSpec(
            num_scalar_prefetch=1, grid=(4, rh // tr),
            in_specs=[pl.BlockSpec((None, None, tr, cols), lambda j, t, cr: (j, cr[0], t, 0)),
                      pl.BlockSpec((None, tr, cols), lambda j, t, cr: (j, t, 0))],
            out_specs=pl.BlockSpec((None, tr, cols), lambda j, t, cr: (j, t, 0))),
        out_shape=jax.ShapeDtypeStruct((4, rh, cols), BF16),
        compiler_params=_cparams(("parallel", "parallel")),
    )(cidx, g4, got)


def _chip_copies(ps, lands, send, recv):
    x, y, c, jme, sib, chips = _place()
    return [_remote(ps[a].at[2 * chip[0] + chip[1]], lands[a].at[jme], send, recv, a * 3 + k, (*chip, c))
            for a in range(len(ps)) for k, chip in enumerate(chips)]


def rs_chip_start(ps, name):
    n = len(ps)

    def body(*refs):
        ins, lands = refs[:n], refs[n:2 * n]
        send, recv = refs[2 * n], refs[2 * n + 1]
        token = refs[-1]
        for cp in _chip_copies(ins, lands, send, recv):
            cp.start()
        token[...] = jnp.zeros_like(token)

    res = pl.pallas_call(
        body, name=name,
        out_shape=(pltpu.SemaphoreType.DMA((3 * n,)), pltpu.SemaphoreType.DMA((3 * n,)),
                   *[pltpu.HBM(p.shape, p.dtype) for p in ps], *[pltpu.HBM(p.shape, p.dtype) for p in ps],
                   jax.ShapeDtypeStruct((8, 128), F32)),
        in_specs=[HBM] * (2 * n), out_specs=(SEM, SEM, *[HBM] * (2 * n), pl.BlockSpec(memory_space=pltpu.VMEM)),
        input_output_aliases={a: 2 + a for a in range(2 * n)},
        compiler_params=pltpu.CompilerParams(has_side_effects=EFFECT),
    )(*[_hbm(p) for p in ps], *[_hbm(lax.empty(p.shape, p.dtype)) for p in ps])
    return res[0], res[1], list(res[2:2 + n]), list(res[2 + n:2 + 2 * n]), res[-1]


def rs_chip_wait(send, recv, ps, lands, after, name):
    n = len(ps)

    def body(*refs):
        ins, zones = refs[:n], refs[n:2 * n]
        send_r, recv_r = refs[2 * n], refs[2 * n + 1]
        x, y, c, jme, sib, chips = _place()
        for a in range(n):
            for k, chip in enumerate(chips):
                jt = 2 * chip[0] + chip[1]
                cp = _remote(ins[a].at[jt], zones[a].at[jt], send_r, recv_r, a * 3 + k, (*chip, c))
                cp.wait_send()
                cp.wait_recv()

    res = pl.pallas_call(
        body, name=name,
        out_shape=tuple(pltpu.HBM(p.shape, p.dtype) for p in list(ps) + list(lands)),
        in_specs=[HBM] * (2 * n) + [SEM, SEM, ANY], out_specs=tuple([HBM] * (2 * n)),
        input_output_aliases={a: a for a in range(2 * n)},
        compiler_params=pltpu.CompilerParams(has_side_effects=EFFECT),
    )(*ps, *lands, send, recv, after)
    return list(res[n:])


def rs_chip_sum(q, p, l, acc, layers, jc):
    _, rh, cols = q.shape
    tr = rh if rh <= 256 else 256

    def body(jc_ref, q_ref, p_ref, *rest):
        o_ref = rest[-1]
        jme = jc_ref[0]
        own = p_ref[...].astype(F32)
        v = [jnp.where(jme == j, own, q_ref[j].astype(F32)) for j in range(4)]
        o_ref[...] = ((v[0] + v[1]) + v[2]) + v[3]

    in_specs = [pl.BlockSpec((4, tr, cols), lambda t, jr: (0, t, 0)),
                pl.BlockSpec((None, tr, cols), lambda t, jr: (jr[0], t, 0))]
    args = [jc, q, p]
    if acc is not None:
        in_specs.append(ANY)
        args.append(acc)
    return pl.pallas_call(
        body, name="rs_chip_sum",
        grid_spec=pltpu.PrefetchScalarGridSpec(
            num_scalar_prefetch=1, grid=(rh // tr,), in_specs=in_specs,
            out_specs=pl.BlockSpec((None, None, tr, cols), lambda t, jr: (l, jr[1], t, 0))),
        out_shape=jax.ShapeDtypeStruct((layers, 2, rh, cols), F32),
        input_output_aliases={} if acc is None else {3: 0},
        compiler_params=_cparams(("parallel",)),
    )(*args)


def rs_pair_gather(rs):
    n = len(rs)

    def body(*refs):
        outs = refs[n:2 * n]
        send, recv = refs[2 * n:]
        x, y, c, jme, sib, chips = _place()
        cps = [_remote(outs[a].at[:, c], outs[a].at[:, c], send, recv, a, sib) for a in range(n)]
        for cp in cps:
            cp.start()
        for a in range(n):
            slot = outs[a].at[:, 1 - c]
            _remote(slot, slot, send, recv, a, sib).wait_recv()
        for cp in cps:
            cp.wait_send()

    return pl.pallas_call(
        body, name="rs_pair_gather", in_specs=[ANY] * n, out_specs=[ANY] * n,
        out_shape=[jax.ShapeDtypeStruct(r.shape, r.dtype) for r in rs],
        input_output_aliases={a: a for a in range(n)},
        scratch_shapes=[pltpu.SemaphoreType.DMA((n,)), pltpu.SemaphoreType.DMA((n,))],
    )(*rs)


def _adamw_math(w, g, m, v):
    m = B1 * m + (1.0 - B1) * g
    v = B2 * v + (1.0 - B2) * (g * g)
    m_hat = m / (1.0 - B1 ** STEP)
    v_hat = v / (1.0 - B2 ** STEP)
    return -LR * (m_hat / (jnp.sqrt(v_hat) + AEPS) + WD * w), m, v


def adamw(w, g, m, v, name):
    rows, cols = w.shape
    tr = 256 if rows % 256 == 0 else rows
    return rw(_adamw_math, [(a, 0, cols) for a in (w, g, m, v)], [(cols, F32)] * 3, name, rows, tr=tr)


WEIGHTS = ["norm_ab", "w_in_ab", "pool_w", "pool_scale", "w_out_ab", "norm_cd", "w_in_cd", "sgu_ln_g", "sgu_ln_b",
           "sgu_w", "sgu_b", "s5_a_re", "s5_a_im", "s5_log_dt", "s5_b_re", "s5_b_im", "s5_c_re", "s5_c_im", "s5_d",
           "glu_w1", "glu_w2", "w_out_cd", "norm_x", "w_xq", "w_xkv", "w_xo", "mem_norm", "final_norm"]
INPUTS = ["x", "mem"] + WEIGHTS + ["loss_target"] + ["m_" + n for n in WEIGHTS] + ["v_" + n for n in WEIGHTS]
BIG = ["w_in_ab", "w_out_ab", "w_in_cd", "w_out_cd", "w_xq", "w_xkv", "w_xo", "glu_w1", "glu_w2", "pool_w"]
COL_SHARDED = ("w_in_ab", "w_in_cd", "w_xkv")
SMALL = [n for n in WEIGHTS if n not in BIG]
SMALL_SHARDED = {"norm_cd": 256, "sgu_ln_g": 256, "sgu_ln_b": 256, "s5_d": 128}
PACK = 256 * 128


def _pack(arrs):
    flat = jnp.concatenate([a.reshape(-1) for a in arrs])
    pad = (-flat.shape[0]) % PACK
    return jnp.concatenate([flat, jnp.zeros((pad,), flat.dtype)]).reshape(-1, 128)


def _unpack(packed, shapes):
    flat, out, off = packed.reshape(-1), [], 0
    for s in shapes:
        n = 1
        for d in s:
            n *= d
        out.append(flat[off:off + n].reshape(s))
        off += n
    return out


LAYER_KEYS = (("w_in", "w_out", "pool_w", "w_xq", "w_xkv", "w_xo"),
              ("w_in", "w_out", "glu_w1", "glu_w2", "w_xq", "w_xkv", "w_xo"))


def _weight_of(key, layer):
    if key in ("w_xq", "w_xkv", "w_xo"):
        return key, layer, 4
    kind = "ab" if layer % 2 == 0 else "cd"
    return {"w_in": "w_in_" + kind, "w_out": "w_out_" + kind}.get(key, key), layer // 2, 2


def kernel(*args):
    a = dict(zip(INPUTS, args))
    x_i, y_i, c_i = lax.axis_index("x"), lax.axis_index("y"), lax.axis_index("c")
    j = 2 * x_i + y_i

    slab = jnp.concatenate([a["norm_cd"], a["sgu_ln_g"], a["sgu_ln_b"],
                            jnp.pad(a["s5_d"], ((0, 0), (0, 128)))], axis=0)
    gslab = allgather_small(slab)
    P = {n: a[n] for n in SMALL}
    for k, n in enumerate(("norm_cd", "sgu_ln_g", "sgu_ln_b", "s5_d")):
        wd = SMALL_SHARDED[n]
        P[n] = gslab[:, 2 * k:2 * k + 2, :wd].transpose(1, 0, 2).reshape(2, 4 * wd)

    def shards_of(layer):
        keys = sorted(k for k in LAYER_KEYS[layer % 2])
        out = []
        for k in keys:
            n, l, _ = _weight_of(k, layer)
            out.append(a[n][l].reshape(-1, a[n].shape[-1]).astype(BF16))
        return keys, out

    keys0, sh0 = shards_of(0)
    g0, token = allgather_sync([s.reshape(2, s.shape[0] // 2, s.shape[1]) for s in sh0])
    gathered = {0: dict(zip(keys0, [g.reshape(4, -1, g.shape[-1]) for g in g0]))}
    started = {}
    for layer in (1, 2, 3):
        keys, sh = shards_of(layer)
        send, recv, sh, lands, token = allgather_start(sh, token, "allgather_start_%d" % layer)
        started[layer] = (keys, send, recv, sh, lands)
    P["norm_ab"] = P["norm_ab"] + token[0:1, 0:1]

    cidx = jnp.reshape(c_i, (1,)).astype(jnp.int32)
    jc = jnp.stack([j, c_i]).astype(jnp.int32)

    def weights_of(layer, x_in):
        if layer not in gathered:
            keys, send, recv, sh, lands = started[layer]
            gathered[layer] = dict(zip(keys, allgather_wait(send, recv, sh, lands, x_in, "allgather_wait_%d" % layer)))
        g = gathered[layer]
        W = {}
        for k, v in g.items():
            if k in ("w_in", "w_xkv"):
                W[k] = mcs(v)
            elif k == "pool_w":
                W[k] = v.reshape(4, 4, 64, 256).transpose(1, 0, 2, 3).reshape(4, 256, 256)
            elif k not in ("glu_w1", "glu_w2"):
                W[k] = m2(v.reshape(-1, v.shape[-1]))
        if layer % 2 == 1:
            W["w12"] = jnp.concatenate([g["glu_w1"].reshape(512, 512), g["glu_w2"].reshape(512, 512)], axis=1)
        return W

    pending = {}

    def grads_done(layer, GW):
        keys = sorted(GW)
        flat = [GW[k].reshape(4, 2, GW[k].shape[1] // 2, GW[k].shape[2]) for k in keys]
        pair = [rs_pair_sum(g4, r, cidx) for g4, r in zip(flat, rs_pair_exchange(flat))]
        send, recv, pair, lands, token = rs_chip_start(pair, "rs_chip_start_%d" % layer)
        pending[layer] = (keys, send, recv, pair, lands)
        return token[0:1, 0:1]

    loss, dx, G = local_step(a["x"][0], a["mem"][0], a["loss_target"][0], P, weights_of, grads_done)
    loss = lax.psum(loss[0, 0], ("x", "y", "c"))

    red = {}
    for layer in (3, 2, 1, 0):
        keys, send, recv, pair, lands = pending[layer]
        lands = rs_chip_wait(send, recv, pair, lands, dx, "rs_chip_wait_%d" % layer)
        for k, q, p in zip(keys, lands, pair):
            n, l, layers = _weight_of(k, layer)
            red[n] = rs_chip_sum(q, p, l, red.get(n), layers, jc)
    gbig = dict(zip(BIG, rs_pair_gather([red[n] for n in BIG])))

    outs = {}
    for n in BIG:
        shp = a[n].shape
        g2 = gbig[n].reshape(-1, shp[-1])
        d2, m2_, v2_ = adamw(a[n].reshape(g2.shape), g2, a["m_" + n].reshape(g2.shape),
                             a["v_" + n].reshape(g2.shape), "adamw_" + n)
        outs[n] = tuple(t.reshape(shp) for t in (g2, d2, m2_, v2_))

    gfull = [jnp.stack(G[n]) if isinstance(G[n], list) else G[n] for n in SMALL]
    shapes = [g.shape for g in gfull]
    gsum = _unpack(allreduce_small(_pack(gfull)), shapes)
    gloc = []
    for n, g in zip(SMALL, gsum):
        if n in SMALL_SHARDED:
            g = lax.dynamic_slice_in_dim(g, j * SMALL_SHARDED[n], SMALL_SHARDED[n], axis=1)
        gloc.append(g)
    lshapes = [a[n].shape for n in SMALL]
    packed = [_pack(t) for t in ([a[n] for n in SMALL], gloc, [a["m_" + n] for n in SMALL], [a["v_" + n] for n in SMALL])]
    small = [_unpack(t, lshapes) for t in adamw(*packed, "adamw_small")]
    for k, n in enumerate(SMALL):
        outs[n] = (gloc[k], small[0][k], small[1][k], small[2][k])

    res = [loss, dx[None]]
    for part in range(4):
        res += [outs[n][part] for n in WEIGHTS]
    return tuple(res)
```

```python
import math

import jax
import jax.numpy as jnp
from jax import lax
from jax.experimental import pallas as pl
from jax.experimental.pallas import tpu as pltpu

F32, BF16 = jnp.float32, jnp.bfloat16
S, D = 2048, 1024
MEM = 256
EPS = 1e-6
NEG = -1e30
QB = 128
PATTERNS = (1, 4, 16)
NG, NP, NH = 32, 64, 16
NS = NG * NP
LR, B1, B2, AEPS, WD, STEP = 0.001, 0.9, 0.999, 1e-08, 0.01, 10
MESHID = pl.DeviceIdType.MESH
VMEM_LIMIT = 56 * 1024 * 1024


def _cparams(sem):
    return pltpu.CompilerParams(dimension_semantics=sem, vmem_limit_bytes=VMEM_LIMIT)


def _sig(x):
    return 1.0 / (1.0 + jnp.exp(-x))


def _dot(a, b, dims):
    return lax.dot_general(a, b, (dims, ((), ())), preferred_element_type=F32)


def _nn(a, b):
    return _dot(a, b, ((1,), (0,)))


def _nt(a, b):
    return _dot(a, b, ((1,), (1,)))


def _tn(a, b):
    return _dot(a, b, ((0,), (0,)))


_DIMS = {"nn": ((1,), (0,)), "nt": ((1,), (1,)), "tn": ((0,), (0,))}


def _tile(dim, cc=None, cap=1024):
    for t in (2048, 1536, 1024, 768, 512, 384, 256, 128):
        if t <= cap and dim % t == 0 and (cc is None or cc % t == 0):
            return t
    return dim


MM_VMEM = 36 * 1024 * 1024


def _mm_tiles(m, n, k, ccm, ccn, cck, a_bytes, b_bytes, o_bytes):
    caps = [1024, 1024, 2048]
    while True:
        tm, tn, tk = _tile(m, ccm, caps[0]), _tile(n, ccn, caps[1]), _tile(k, cck, caps[2])
        need = 2 * (tm * tk * a_bytes + tk * tn * b_bytes + tm * tn * o_bytes) + (tm * tn * 4 if tk < k else 0)
        if need <= MM_VMEM:
            return tm, tn, tk
        if tk > 1024:
            caps[2] = tk // 2
        elif tn >= tm:
            caps[1] = tn // 2
        else:
            caps[0] = tm // 2


def m2(arr, col_off=0, ncols=None):
    rows, cols = arr.shape
    ncols = cols - col_off if ncols is None else ncols

    def spec(tr, tc, rc):
        assert col_off % tc == 0
        return pl.BlockSpec((tr, tc), lambda *g: (rc(*g)[0], rc(*g)[1] + col_off // tc))
    return (arr, rows, ncols, spec, None if col_off == 0 else col_off)


def mcs(arr):
    cs = arr.shape[2]

    def spec(tr, tc, rc):
        n = cs // tc
        return pl.BlockSpec((None, tr, tc), lambda *g: (rc(*g)[1] // n, rc(*g)[0], rc(*g)[1] % n))
    return (arr, arr.shape[1], 4 * cs, spec, cs)


def out2(rows, cols):
    def spec(tr, tc, rc):
        return pl.BlockSpec((tr, tc), lambda *g: tuple(rc(*g)))
    return ((rows, cols), spec, None)


def outcs(rows, cs):
    def spec(tr, tc, rc):
        n = cs // tc
        return pl.BlockSpec((None, tr, tc), lambda *g: (rc(*g)[1] // n, rc(*g)[0], rc(*g)[1] % n))
    return ((4, rows, cs), spec, cs)


def _both(a, b):
    if a is None:
        return b
    if b is None:
        return a
    return math.gcd(a, b)


def mm(a, b, mode, name, add=None, out=None, out_dtype=F32):
    a_arr, a_r, a_c, a_spec, a_cc = a
    b_arr, b_r, b_c, b_spec, b_cc = b
    if mode == "nn":
        m, k, n = a_r, a_c, b_c
        assert b_r == k
        ccm, cck, ccn = None, a_cc, b_cc
    elif mode == "nt":
        m, k, n = a_r, a_c, b_r
        assert b_c == k
        ccm, cck, ccn = None, _both(a_cc, b_cc), None
    else:
        m, k, n = a_c, a_r, b_c
        assert b_r == k
        ccm, cck, ccn = a_cc, None, b_cc
    out = out2(m, n) if out is None else out
    o_shape, o_spec, o_cc = out
    ccn = _both(ccn, o_cc)
    if add is not None:
        ccn = _both(ccn, add[4])
    o_bytes = jnp.dtype(out_dtype).itemsize + (0 if add is None else add[0].dtype.itemsize)
    tm, tn, tk = _mm_tiles(m, n, k, ccm, ccn, cck, a_arr.dtype.itemsize, b_arr.dtype.itemsize, o_bytes)
    nk = k // tk
    if mode == "nn":
        in_specs = [a_spec(tm, tk, lambda i, j, kk: (i, kk)), b_spec(tk, tn, lambda i, j, kk: (kk, j))]
    elif mode == "nt":
        in_specs = [a_spec(tm, tk, lambda i, j, kk: (i, kk)), b_spec(tn, tk, lambda i, j, kk: (j, kk))]
    else:
        in_specs = [a_spec(tk, tm, lambda i, j, kk: (kk, i)), b_spec(tk, tn, lambda i, j, kk: (kk, j))]
    args = [a_arr, b_arr]
    if add is not None:
        in_specs.append(add[3](tm, tn, lambda i, j, kk: (i, j)))
        args.append(add[0])
    dims = _DIMS[mode]
    has_add = add is not None

    def body(*refs):
        a_ref, b_ref = refs[0], refs[1]
        add_ref = refs[2] if has_add else None
        prod = _dot(a_ref[...].astype(BF16), b_ref[...].astype(BF16), dims)
        if nk == 1:
            o_ref = refs[-1]
            if has_add:
                prod = prod + add_ref[...].astype(F32)
            o_ref[...] = prod.astype(o_ref.dtype)
            return
        o_ref, acc = refs[-2], refs[-1]
        kk = pl.program_id(2)

        @pl.when(kk == 0)
        def _():
            acc[...] = prod

        @pl.when(kk > 0)
        def _():
            acc[...] += prod

        @pl.when(kk == nk - 1)
        def _():
            r = acc[...]
            if has_add:
                r = r + add_ref[...].astype(F32)
            o_ref[...] = r.astype(o_ref.dtype)

    return pl.pallas_call(
        body, name=name, grid=(m // tm, n // tn, nk), in_specs=in_specs,
        out_specs=o_spec(tm, tn, lambda i, j, kk: (i, j)),
        out_shape=jax.ShapeDtypeStruct(o_shape, out_dtype),
        scratch_shapes=[pltpu.VMEM((tm, tn), F32)] if nk > 1 else [],
        compiler_params=_cparams(("parallel", "parallel", "arbitrary")),
    )(*args)


def rw(fn, ins, outs, name, rows, tr=256, consts=(), accs=()):
    n_in, n_c, n_o, n_a = len(ins), len(consts), len(outs), len(accs)
    in_specs = []
    for arr, off, width in ins:
        assert off % width == 0
        in_specs.append(pl.BlockSpec((tr, width), lambda i, o=off // width: (i, o)))
    for c in consts:
        in_specs.append(pl.BlockSpec(c.shape, lambda i: (0, 0)))
    out_specs = [pl.BlockSpec((tr, w), lambda i: (i, 0)) for w, _ in outs]
    out_specs += [pl.BlockSpec(s, lambda i: (0, 0)) for s in accs]
    out_shape = [jax.ShapeDtypeStruct((rows, w), dt) for w, dt in outs]
    out_shape += [jax.ShapeDtypeStruct(s, F32) for s in accs]

    def body(*refs):
        vals = [r[...] for r in refs[:n_in + n_c]]
        o_refs = refs[n_in + n_c:n_in + n_c + n_o]
        a_refs = refs[n_in + n_c + n_o:]
        res = fn(*vals)
        for r, v in zip(o_refs, res[:n_o]):
            r[...] = v.astype(r.dtype)
        if n_a:
            @pl.when(pl.program_id(0) == 0)
            def _():
                for r in a_refs:
                    r[...] = jnp.zeros_like(r)
            for r, v in zip(a_refs, res[n_o:]):
                r[...] += v

    res = pl.pallas_call(
        body, name=name, grid=(rows // tr,), in_specs=in_specs, out_specs=out_specs,
        out_shape=out_shape,
        compiler_params=_cparams(("arbitrary",) if n_a else ("parallel",)),
    )(*[a for a, _, _ in ins], *consts)
    return res


def _rstd(x):
    return lax.rsqrt(jnp.mean(x * x, axis=-1, keepdims=True) + EPS)


def rms_fwd(x, g, name):
    def fn(xv, gv):
        xv = xv.astype(F32)
        return (xv * _rstd(xv) * gv,)
    return rw(fn, [(x, 0, D)], [(D, BF16)], name, x.shape[0], consts=[g])[0]


def _rms_bwd_math(xv, dy, gv):
    r = _rstd(xv)
    dyg = dy * gv
    dx = r * dyg - xv * (r * r * r / D) * jnp.sum(dyg * xv, axis=-1, keepdims=True)
    dg = jnp.sum(dy * xv * r, axis=0, keepdims=True)
    return dx, dg


def rms_bwd(x, dy, dres, g, name):
    def fn(xv, dyv, drv, gv):
        dx, dg = _rms_bwd_math(xv, dyv, gv)
        return dx + drv, dg
    return rw(fn, [(x, 0, D), (dy, 0, D), (dres, 0, D)], [(D, F32)], name, x.shape[0],
              consts=[g], accs=[(1, D)])


def final_loss(x, tgt, g):
    def fn(xv, tv, gv):
        e = xv * _rstd(xv) * gv - tv
        loss = 0.5 * jnp.sum(jnp.sum(e * e, axis=-1, keepdims=True), axis=0, keepdims=True) / D
        dx, dg = _rms_bwd_math(xv, e / D, gv)
        return dx, loss, dg
    return rw(fn, [(x, 0, D), (tgt, 0, D)], [(D, F32)], "final_loss", S, consts=[g],
              accs=[(1, 1), (1, D)])


def _attn_bias(bias_ref):
    ii = lax.broadcasted_iota(jnp.int32, (2 * QB, 2 * QB), 0) % QB
    jj = lax.broadcasted_iota(jnp.int32, (2 * QB, 2 * QB), 1)
    dist = ii + QB - jj
    band = (dist >= 0) & (dist <= QB)
    bias_ref[1] = jnp.where(band, 0.0, NEG)
    bias_ref[0] = jnp.where(band & (jj >= QB), 0.0, NEG)


def _two_heads(x, m0):
    return jnp.concatenate([jnp.where(m0, x, 0.0), jnp.where(m0, 0.0, x)], axis=0)


def _per_head(col, m0):
    return jnp.where(m0, col[:QB], col[QB:])


def _attn_rows(idx, d):
    if d == 1:
        b = idx
        cur = pl.ds(pl.multiple_of(b * QB, QB), QB)
        prev = pl.ds(pl.multiple_of(jnp.maximum(b - 1, 0) * QB, QB), QB)
    else:
        r, b = lax.rem(idx, d), lax.div(idx, d)
        cur = pl.ds(r + b * (QB * d), QB, stride=d)
        prev = pl.ds(r + jnp.maximum(b - 1, 0) * (QB * d), QB, stride=d)
    return cur, prev, b


NBLK = S // QB
GROUP = 4


def _colblk(off):
    return pl.BlockSpec((S, 128), lambda hp: (0, off * 8 + hp))


def attn_fwd(z):
    def body(q_ref, k_ref, v_ref, g_ref, o_ref, l_ref, a_ref, os, ls, bias):
        _attn_bias(bias)
        m0 = lax.broadcasted_iota(jnp.int32, (1, 128), 1) < 64
        for pi, d in enumerate(PATTERNS):
            def load(idx, d=d):
                cur, prev, b = _attn_rows(idx, d)
                return cur, (q_ref[cur, :], k_ref[prev, :], k_ref[cur, :], v_ref[prev, :], v_ref[cur, :],
                             bias[jnp.minimum(b, 1)])

            def block(q, kp, kc, vp, vc, bs):
                qq = _two_heads(q * 0.125, m0).astype(BF16)
                k = jnp.concatenate([kp, kc], axis=0).astype(BF16)
                s = _nt(qq, k) + bs
                mx = jnp.max(s, axis=-1, keepdims=True)
                p = jnp.exp(s - mx)
                den = jnp.sum(p, axis=-1, keepdims=True)
                pb = p.astype(BF16)
                vv = _two_heads(jnp.concatenate([vp, vc], axis=0), m0).astype(BF16)
                o = _nn(jnp.concatenate([pb[:QB], pb[QB:]], axis=1), vv)
                return o * _per_head(1.0 / den, m0), _per_head(mx + jnp.log(den), m0)

            def step(i, carry, pi=pi):
                loaded = [load(i * GROUP + u) for u in range(GROUP)]
                done = [block(*vals) for _, vals in loaded]
                for (cur, _), (o, l) in zip(loaded, done):
                    os[pi, cur, :] = o
                    ls[pi, cur, :] = l
                return carry
            lax.fori_loop(0, NBLK // GROUP, step, 0)
        l1, l2, l3 = ls[0], ls[1], ls[2]
        mx = jnp.maximum(jnp.maximum(l1, l2), l3)
        e1, e2, e3 = jnp.exp(l1 - mx), jnp.exp(l2 - mx), jnp.exp(l3 - mx)
        tot = e1 + e2 + e3
        o = (os[0] * e1 + os[1] * e2 + os[2] * e3) / tot
        ga = g_ref[...]
        o_ref[...] = o
        l_ref[...] = mx + jnp.log(tot)
        a_ref[...] = (o * (ga * _sig(ga))).astype(a_ref.dtype)

    out = pl.BlockSpec((S, 128), lambda hp: (0, hp))
    return pl.pallas_call(
        body, name="attn_fwd", grid=(8,),
        in_specs=[_colblk(0), _colblk(1), _colblk(2), _colblk(3)], out_specs=[out] * 3,
        out_shape=[jax.ShapeDtypeStruct((S, D), F32), jax.ShapeDtypeStruct((S, D), F32),
                   jax.ShapeDtypeStruct((S, 2 * D), BF16)],
        scratch_shapes=[pltpu.VMEM((3, S, 128), F32), pltpu.VMEM((3, S, 128), F32),
                        pltpu.VMEM((2, 2 * QB, 2 * QB), F32)],
        compiler_params=_cparams(("parallel",)),
    )(z, z, z, z)


def attn_bwd(z, d_cat, o, lse):
    def body(q_ref, k_ref, v_ref, g_ref, da_ref, o_ref, l_ref, dq_ref, dk_ref, dv_ref, dg_ref, do_s, pr_s, bias):
        _attn_bias(bias)
        m0 = lax.broadcasted_iota(jnp.int32, (1, 128), 1) < 64
        ga = g_ref[...]
        sg = _sig(ga)
        da = da_ref[...]
        ov = o_ref[...]
        do = da * (ga * sg)
        dg_ref[...] = da * ov * (sg * (1.0 + ga * (1.0 - sg)))
        do_s[...] = do
        pr_s[...] = do * ov
        dq_ref[...] = jnp.zeros_like(dq_ref)
        dk_ref[...] = jnp.zeros_like(dk_ref)
        dv_ref[...] = jnp.zeros_like(dv_ref)
        for d in PATTERNS:
            def load(idx, d=d):
                cur, prev, b = _attn_rows(idx, d)
                return (cur, prev), (q_ref[cur, :], k_ref[prev, :], k_ref[cur, :], v_ref[prev, :], v_ref[cur, :],
                                     do_s[cur, :], pr_s[cur, :], l_ref[cur, :], bias[jnp.minimum(b, 1)])

            def block(q, kp, kc, vp, vc, dof, prod, lp, bs):
                qq = _two_heads(q * 0.125, m0).astype(BF16)
                kf = jnp.concatenate([kp, kc], axis=0)
                k = kf.astype(BF16)
                v = jnp.concatenate([vp, vc], axis=0).astype(BF16)
                dd = _two_heads(dof, m0).astype(BF16)
                lh = jnp.max(jnp.concatenate([jnp.where(m0, lp, -jnp.inf), jnp.where(m0, -jnp.inf, lp)], axis=0),
                             axis=-1, keepdims=True)
                delta = jnp.sum(_two_heads(prod, m0), axis=-1, keepdims=True)
                p = jnp.exp(_nt(qq, k) + bs - lh)
                ds = (p * (_nt(dd, v) - delta)).astype(BF16)
                dq = _nn(jnp.concatenate([ds[:QB], ds[QB:]], axis=1), _two_heads(kf, m0).astype(BF16))
                return dq * 0.125, _tn(ds, qq), _tn(p.astype(BF16), dd)

            def step(i, carry):
                loaded = [load(i * GROUP + u) for u in range(GROUP)]
                done = [block(*vals) for _, vals in loaded]
                for ((cur, prev), _), (dq, dk, dv) in zip(loaded, done):
                    dq_ref[cur, :] = dq_ref[cur, :] + dq
                    dk_ref[prev, :] = dk_ref[prev, :] + dk[:QB]
                    dv_ref[prev, :] = dv_ref[prev, :] + dv[:QB]
                    dk_ref[cur, :] = dk_ref[cur, :] + dk[QB:]
                    dv_ref[cur, :] = dv_ref[cur, :] + dv[QB:]
                return carry
            lax.fori_loop(0, NBLK // GROUP, step, 0)

    blk = pl.BlockSpec((S, 128), lambda hp: (0, hp))
    return pl.pallas_call(
        body, name="attn_bwd", grid=(8,),
        in_specs=[_colblk(0), _colblk(1), _colblk(2), _colblk(3), blk, blk, blk], out_specs=[blk] * 4,
        out_shape=[jax.ShapeDtypeStruct((S, D), F32)] * 4,
        scratch_shapes=[pltpu.VMEM((S, 128), F32), pltpu.VMEM((S, 128), F32), pltpu.VMEM((2, 2 * QB, 2 * QB), F32)],
        compiler_params=_cparams(("parallel",)),
    )(z, z, z, z, d_cat, o, lse)


def assemble_dz_even(parts):
    def body(*refs):
        o_ref = refs[-1]
        for j in range(6):
            o_ref[:, j * D:(j + 1) * D] = refs[j][...].astype(o_ref.dtype)
    tr = 256
    blk = pl.BlockSpec((tr, D), lambda i: (i, 0))
    return pl.pallas_call(
        body, name="assemble_dz_even", grid=(S // tr,), in_specs=[blk] * 6,
        out_specs=pl.BlockSpec((tr, 6 * D), lambda i: (i, 0)),
        out_shape=jax.ShapeDtypeStruct((S, 6 * D), BF16),
        compiler_params=_cparams(("parallel",)),
    )(*parts)


def _pool_window(g):
    return jnp.where(g == 0, 2.0, jnp.where(g == 1, 4.0, jnp.where(g == 2, 8.0, 16.0)))


def _pool_sel(g, levels):
    return jnp.where(g == 0, levels[0], jnp.where(g == 1, levels[1], jnp.where(g == 2, levels[2], levels[3])))


def _pool_fwd_math(v, g):
    t = lax.broadcasted_iota(jnp.int32, (S, 1), 0)
    s = v
    levels = []
    for k in (1, 2, 4, 8):
        s = s + jnp.where(t >= k, pltpu.roll(s, k, 0), 0.0)
        levels.append(s)
    cnt = jnp.minimum((t + 1).astype(F32), _pool_window(g))
    return _pool_sel(g, levels) / cnt - v, cnt


def pool_fwd(z, pw, ps, cat):
    def body(v_ref, g_ref, pw_ref, ps_ref, cat_ref, o_ref):
        g = pl.program_id(0)
        pooled, _ = _pool_fwd_math(v_ref[...], g)
        mixed = _nn(pooled.astype(BF16), pw_ref[...].astype(BF16))
        gb = g_ref[...]
        o_ref[...] = (mixed * ps_ref[...] * (gb * _sig(gb))).astype(o_ref.dtype)

    return pl.pallas_call(
        body, name="pool_fwd", grid=(4,),
        in_specs=[pl.BlockSpec((S, 256), lambda g: (0, 16 + g)),
                  pl.BlockSpec((S, 256), lambda g: (0, 20 + g)),
                  pl.BlockSpec((None, 256, 256), lambda g: (g, 0, 0)),
                  pl.BlockSpec((1, 256), lambda g: (0, g)), pl.BlockSpec(memory_space=pl.ANY)],
        out_specs=pl.BlockSpec((S, 256), lambda g: (0, 4 + g)),
        out_shape=jax.ShapeDtypeStruct((S, 2 * D), BF16),
        input_output_aliases={4: 0},
        compiler_params=_cparams(("parallel",)),
    )(z, z, pw, ps, cat)


def pool_bwd(z, d_cat, pw, ps):
    def body(v_ref, g_ref, d_ref, pw_ref, ps_ref, dv_ref, dg_ref, dpw_ref, dps_ref):
        g = pl.program_id(0)
        v = v_ref[...]
        pooled, cnt = _pool_fwd_math(v, g)
        pwb = pw_ref[...].astype(BF16)
        pb = pooled.astype(BF16)
        mixed = _nn(pb, pwb)
        gb = g_ref[...]
        sg = _sig(gb)
        dout = d_ref[...]
        sc = ps_ref[...]
        dg_ref[...] = dout * mixed * sc * (sg * (1.0 + gb * (1.0 - sg)))
        dms = dout * (gb * sg)
        dps_ref[...] = jnp.sum(dms * mixed, axis=0, keepdims=True)
        dmx = (dms * sc).astype(BF16)
        dpw_ref[...] = _tn(pb, dmx)
        dpooled = _nt(dmx, pwb)
        t = lax.broadcasted_iota(jnp.int32, (S, 1), 0)
        s = dpooled / cnt
        levels = []
        for k in (1, 2, 4, 8):
            s = s + jnp.where(t < S - k, pltpu.roll(s, S - k, 0), 0.0)
            levels.append(s)
        dv_ref[...] = _pool_sel(g, levels) - dpooled

    return pl.pallas_call(
        body, name="pool_bwd", grid=(4,),
        in_specs=[pl.BlockSpec((S, 256), lambda g: (0, 16 + g)),
                  pl.BlockSpec((S, 256), lambda g: (0, 20 + g)),
                  pl.BlockSpec((S, 256), lambda g: (0, 4 + g)),
                  pl.BlockSpec((None, 256, 256), lambda g: (g, 0, 0)),
                  pl.BlockSpec((1, 256), lambda g: (0, g))],
        out_specs=[pl.BlockSpec((S, 256), lambda g: (0, g)),
                   pl.BlockSpec((S, 256), lambda g: (0, g)),
                   pl.BlockSpec((None, 256, 256), lambda g: (g, 0, 0)),
                   pl.BlockSpec((1, 256), lambda g: (0, g))],
        out_shape=[jax.ShapeDtypeStruct((S, D), F32), jax.ShapeDtypeStruct((S, D), F32),
                   jax.ShapeDtypeStruct((4, 256, 256), F32), jax.ShapeDtypeStruct((1, D), F32)],
        compiler_params=_cparams(("parallel",)),
    )(z, z, d_cat, pw, ps)


CH = 128


def _sgu_common(v, lng, lnb, w_ref):
    mu = jnp.mean(v, axis=-1, keepdims=True)
    vc = v - mu
    rs = lax.rsqrt(jnp.mean(vc * vc, axis=-1, keepdims=True) + EPS)
    xhat = vc * rs
    vn = (xhat * lng + lnb).astype(BF16)
    ri = lax.broadcasted_iota(jnp.int32, (CH, CH), 0)
    ci = lax.broadcasted_iota(jnp.int32, (CH, CH), 1)
    tril = ri >= ci
    ws = [jnp.where(tril, w_ref[g], 0.0).astype(BF16) for g in range(4)]
    return xhat, rs, vn, tril, ws


def _zspec(off):
    return pl.BlockSpec((CH, D), lambda c: (c, off))


def _full(shape):
    return pl.BlockSpec(shape, lambda c: (0,) * len(shape))


def sgu_fwd(z, lng, lnb, w, bfull):
    def body(u_ref, v_ref, g_ref, lng_ref, lnb_ref, w_ref, b_ref, o_ref):
        _, _, vn, _, ws = _sgu_common(v_ref[...], lng_ref[...], lnb_ref[...], w_ref)
        for g in range(4):
            sl = slice(g * 256, (g + 1) * 256)
            mixed = _nn(ws[g], vn[:, sl]) + b_ref[:, sl]
            gc = g_ref[:, sl]
            o_ref[:, sl] = (u_ref[:, sl] * mixed * (gc * _sig(gc))).astype(o_ref.dtype)

    return pl.pallas_call(
        body, name="sgu_fwd", grid=(S // CH,),
        in_specs=[_zspec(0), _zspec(1), _zspec(2), _full((1, D)), _full((1, D)),
                  _full((4, CH, CH)), _full((CH, D))],
        out_specs=pl.BlockSpec((CH, D), lambda c: (c, 0)),
        out_shape=jax.ShapeDtypeStruct((S, D), BF16),
        compiler_params=_cparams(("parallel",)),
    )(z, z, z, lng, lnb, w, bfull)


def sgu_bwd(z, d_cat, lng, lnb, w, bfull):
    def body(u_ref, v_ref, g_ref, d_ref, lng_ref, lnb_ref, w_ref, b_ref,
             du_ref, dv_ref, dg_ref, dw_ref, db_ref, dlg_ref, dlb_ref):
        @pl.when(pl.program_id(0) == 0)
        def _():
            dw_ref[...] = jnp.zeros_like(dw_ref)
            db_ref[...] = jnp.zeros_like(db_ref)
            dlg_ref[...] = jnp.zeros_like(dlg_ref)
            dlb_ref[...] = jnp.zeros_like(dlb_ref)

        lng = lng_ref[...]
        xhat, rs, vn, tril, ws = _sgu_common(v_ref[...], lng, lnb_ref[...], w_ref)
        lane = lax.broadcasted_iota(jnp.int32, (1, 128), 1)
        db = jnp.zeros((CH, 128), F32)
        dvn_parts = []
        for g in range(4):
            sl = slice(g * 256, (g + 1) * 256)
            mixed = _nn(ws[g], vn[:, sl]) + b_ref[:, sl]
            gc = g_ref[:, sl]
            sg = _sig(gc)
            u = u_ref[:, sl]
            dc = d_ref[:, sl]
            du_ref[:, sl] = dc * mixed * (gc * sg)
            dg_ref[:, sl] = dc * u * mixed * (sg * (1.0 + gc * (1.0 - sg)))
            dmx = dc * u * (gc * sg)
            db = db + jnp.where(lane == g, jnp.sum(dmx, axis=-1, keepdims=True), 0.0)
            dmb = dmx.astype(BF16)
            dw_ref[g] += jnp.where(tril, _nt(dmb, vn[:, sl]), 0.0)
            dvn_parts.append(_tn(ws[g], dmb))
        db_ref[...] += db
        dvn = jnp.concatenate(dvn_parts, axis=1)
        dlb_ref[...] += jnp.sum(dvn, axis=0, keepdims=True)
        dlg_ref[...] += jnp.sum(dvn * xhat, axis=0, keepdims=True)
        dxh = dvn * lng
        dv_ref[...] = rs * (dxh - jnp.mean(dxh, axis=-1, keepdims=True)
                            - xhat * jnp.mean(dxh * xhat, axis=-1, keepdims=True))

    row = pl.BlockSpec((CH, D), lambda c: (c, 0))
    return pl.pallas_call(
        body, name="sgu_bwd", grid=(S // CH,),
        in_specs=[_zspec(0), _zspec(1), _zspec(2), row, _full((1, D)), _full((1, D)),
                  _full((4, CH, CH)), _full((CH, D))],
        out_specs=[row, row, row, _full((4, CH, CH)), _full((CH, 128)), _full((1, D)), _full((1, D))],
        out_shape=[jax.ShapeDtypeStruct((S, D), F32)] * 3
        + [jax.ShapeDtypeStruct((4, CH, CH), F32), jax.ShapeDtypeStruct((CH, 128), F32),
           jax.ShapeDtypeStruct((1, D), F32), jax.ShapeDtypeStruct((1, D), F32)],
        compiler_params=_cparams(("arbitrary",)),
    )(z, z, z, d_cat, lng, lnb, w, bfull)


TB = 256


def _cmul(ar, ai, br, bi):
    return ar * br - ai * bi, ar * bi + ai * br


def _scan_consts(ar, ai, reverse):
    a2 = _cmul(ar, ai, ar, ai)
    a4 = _cmul(*a2, *a2)
    row = lax.broadcasted_iota(jnp.int32, (8, NS), 0)
    pr = jnp.zeros((8, NS), F32)
    pi = jnp.zeros((8, NS), F32)
    cr, ci = ar, ai
    for r in range(8):
        sel = row == (7 - r if reverse else r)
        pr = jnp.where(sel, cr, pr)
        pi = jnp.where(sel, ci, pi)
        cr, ci = _cmul(cr, ci, ar, ai)
    return ((ar, ai), a2, a4), (pr, pi), row


def scan_fwd(bu, abr, abi):
    def body(bu_ref, ar_ref, ai_ref, h_ref, car, cai):
        @pl.when(pl.program_id(0) == 0)
        def _():
            car[...] = jnp.zeros_like(car)
            cai[...] = jnp.zeros_like(cai)

        pows, (pr, pi), row = _scan_consts(ar_ref[...], ai_ref[...], False)

        def tile(t, carry):
            c_r, c_i = carry
            rows = pl.ds(pl.multiple_of(t * 8, 8), 8)
            xr = bu_ref[rows, 0:NS]
            xi = bu_ref[rows, NS:2 * NS]
            for k, (kr, ki) in zip((1, 2, 4), pows):
                sr = jnp.where(row >= k, pltpu.roll(xr, k, 0), 0.0)
                si = jnp.where(row >= k, pltpu.roll(xi, k, 0), 0.0)
                xr, xi = xr + kr * sr - ki * si, xi + kr * si + ki * sr
            xr, xi = xr + pr * c_r - pi * c_i, xi + pr * c_i + pi * c_r
            h_ref[rows, 0:NS] = xr
            h_ref[rows, NS:2 * NS] = xi
            return (jnp.broadcast_to(xr[7:8, :], (8, NS)), jnp.broadcast_to(xi[7:8, :], (8, NS)))

        c_r, c_i = lax.fori_loop(0, TB // 8, tile, (car[...], cai[...]))
        car[...] = c_r
        cai[...] = c_i

    return pl.pallas_call(
        body, name="s5_scan_fwd", grid=(S // TB,),
        in_specs=[pl.BlockSpec((TB, 2 * NS), lambda i: (i, 0)),
                  pl.BlockSpec((1, NS), lambda i: (0, 0)), pl.BlockSpec((1, NS), lambda i: (0, 0))],
        out_specs=pl.BlockSpec((TB, 2 * NS), lambda i: (i, 0)),
        out_shape=jax.ShapeDtypeStruct((S, 2 * NS), F32),
        scratch_shapes=[pltpu.VMEM((8, NS), F32), pltpu.VMEM((8, NS), F32)],
        compiler_params=_cparams(("arbitrary",)),
    )(bu, abr, abi)


def scan_bwd(eta, h, abr, abi):
    nt = S // TB

    def body(e_ref, h_ref, ar_ref, ai_ref, l_ref, da_ref, car, cai):
        @pl.when(pl.program_id(0) == 0)
        def _():
            car[...] = jnp.zeros_like(car)
            cai[...] = jnp.zeros_like(cai)
            da_ref[...] = jnp.zeros_like(da_ref)

        pows, (pr, pi), row = _scan_consts(ar_ref[...], -ai_ref[...], True)

        def tile(tt, carry):
            c_r, c_i, acr, aci = carry
            t = TB // 8 - 1 - tt
            rows = pl.ds(pl.multiple_of(t * 8, 8), 8)
            xr = e_ref[rows, 0:NS]
            xi = e_ref[rows, NS:2 * NS]
            for k, (kr, ki) in zip((1, 2, 4), pows):
                sr = jnp.where(row < 8 - k, pltpu.roll(xr, 8 - k, 0), 0.0)
                si = jnp.where(row < 8 - k, pltpu.roll(xi, 8 - k, 0), 0.0)
                xr, xi = xr + kr * sr - ki * si, xi + kr * si + ki * sr
            xr, xi = xr + pr * c_r - pi * c_i, xi + pr * c_i + pi * c_r
            l_ref[rows, 0:NS] = xr
            l_ref[rows, NS:2 * NS] = xi
            nr = jnp.where(row < 7, pltpu.roll(xr, 7, 0), c_r)
            ni = jnp.where(row < 7, pltpu.roll(xi, 7, 0), c_i)
            hr = h_ref[rows, 0:NS]
            hi = h_ref[rows, NS:2 * NS]
            acr = acr + hr * nr + hi * ni
            aci = aci + hr * ni - hi * nr
            return (jnp.broadcast_to(xr[0:1, :], (8, NS)), jnp.broadcast_to(xi[0:1, :], (8, NS)), acr, aci)

        zero = jnp.zeros((8, NS), F32)
        c_r, c_i, acr, aci = lax.fori_loop(0, TB // 8, tile, (car[...], cai[...], zero, zero))
        car[...] = c_r
        cai[...] = c_i
        da_ref[:, 0:NS] += acr
        da_ref[:, NS:2 * NS] += aci

    rev = pl.BlockSpec((TB, 2 * NS), lambda i: (nt - 1 - i, 0))
    return pl.pallas_call(
        body, name="s5_scan_bwd", grid=(nt,),
        in_specs=[rev, rev, pl.BlockSpec((1, NS), lambda i: (0, 0)), pl.BlockSpec((1, NS), lambda i: (0, 0))],
        out_specs=[rev, pl.BlockSpec((8, 2 * NS), lambda i: (0, 0))],
        out_shape=[jax.ShapeDtypeStruct((S, 2 * NS), F32), jax.ShapeDtypeStruct((8, 2 * NS), F32)],
        scratch_shapes=[pltpu.VMEM((8, NS), F32), pltpu.VMEM((8, NS), F32)],
        compiler_params=_cparams(("arbitrary",)),
    )(eta, h, abr, abi)


GC = 0.7978845608028654
GA = 0.044715


def s5_post(hc, z, dskip):
    def fn(hv, xd, dv):
        y = hv + dv * xd
        return y, 0.5 * y * (1.0 + jnp.tanh(GC * (y + GA * y * y * y)))
    return rw(fn, [(hc, 0, 512), (z, 3072, 512)], [(512, F32), (512, BF16)], "s5_post", S, consts=[dskip])


def s5_post_bwd(dyg, ypre, z, dskip):
    def fn(dy, y, xd, dv):
        th = jnp.tanh(GC * (y + GA * y * y * y))
        dg = 0.5 * (1.0 + th) + 0.5 * y * (1.0 - th * th) * GC * (1.0 + 3.0 * GA * y * y)
        dyp = dy * dg
        return dyp, dyp * dv, jnp.sum(dyp * xd, axis=0, keepdims=True)
    return rw(fn, [(dyg, 0, 512), (ypre, 0, 512), (z, 3072, 512)], [(512, BF16), (512, F32)],
              "s5_post_bwd", S, consts=[dskip], accs=[(1, 512)])


def glu_fwd(t, z, c_out):
    def fn(t1, t2, gd, co):
        return (jnp.concatenate([co, (t1 * _sig(t2) * (gd * _sig(gd))).astype(BF16)], axis=1),)
    return rw(fn, [(t, 0, 512), (t, 512, 512), (z, 3584, 512), (c_out, 0, D)], [(D + 512, BF16)], "glu_fwd", S)[0]


def glu_bwd(t, z, d_cat):
    def fn(t1, t2, gd, dd):
        s2, sg = _sig(t2), _sig(gd)
        sl = gd * sg
        return (jnp.concatenate([dd * s2 * sl, dd * t1 * s2 * (1.0 - s2) * sl], axis=1),
                dd * t1 * s2 * (sg * (1.0 + gd * (1.0 - sg))))
    return rw(fn, [(t, 0, 512), (t, 512, 512), (z, 3584, 512), (d_cat, 1024, 512)],
              [(D, BF16), (512, F32)], "glu_bwd", S)


def assemble_dz_odd(du, dv, dgc, dxd, dgd):
    def body(a, b, c, d, e, o_ref):
        o_ref[:, 0:D] = a[...].astype(BF16)
        o_ref[:, D:2 * D] = b[...].astype(BF16)
        o_ref[:, 2 * D:3 * D] = c[...].astype(BF16)
        o_ref[:, 3 * D:3 * D + 512] = d[...].astype(BF16)
        o_ref[:, 3 * D + 512:4 * D] = e[...].astype(BF16)
    tr = 256
    blk = pl.BlockSpec((tr, D), lambda i: (i, 0))
    half = pl.BlockSpec((tr, 512), lambda i: (i, 0))
    return pl.pallas_call(
        body, name="assemble_dz_odd", grid=(S // tr,), in_specs=[blk, blk, blk, half, half],
        out_specs=pl.BlockSpec((tr, 4 * D), lambda i: (i, 0)),
        out_shape=jax.ShapeDtypeStruct((S, 4 * D), BF16),
        compiler_params=_cparams(("parallel",)),
    )(du, dv, dgc, dxd, dgd)


TQ = 256


def _xattn_probs(qh, kh):
    s = _nt(qh, kh) * 0.0625
    p = jnp.exp(s - jnp.max(s, axis=-1, keepdims=True))
    return p / jnp.sum(p, axis=-1, keepdims=True)


def xattn_fwd(q, kv):
    def body(q_ref, kv_ref, o_ref):
        for h in range(4):
            sl = slice(h * 256, (h + 1) * 256)
            p = _xattn_probs(q_ref[:, sl].astype(BF16), kv_ref[:, sl].astype(BF16))
            vh = kv_ref[:, D + h * 256:D + (h + 1) * 256].astype(BF16)
            o_ref[:, sl] = _nn(p.astype(BF16), vh).astype(o_ref.dtype)

    return pl.pallas_call(
        body, name="xattn_fwd", grid=(S // TQ,),
        in_specs=[pl.BlockSpec((TQ, D), lambda i: (i, 0)), pl.BlockSpec((MEM, 2 * D), lambda i: (0, 0))],
        out_specs=pl.BlockSpec((TQ, D), lambda i: (i, 0)),
        out_shape=jax.ShapeDtypeStruct((S, D), BF16),
        compiler_params=_cparams(("parallel",)),
    )(q, kv)


def xattn_bwd(q, kv, d_o):
    def body(q_ref, kv_ref, do_ref, dq_ref, dkv_ref):
        @pl.when(pl.program_id(0) == 0)
        def _():
            dkv_ref[...] = jnp.zeros_like(dkv_ref)

        for h in range(4):
            sl = slice(h * 256, (h + 1) * 256)
            vs = slice(D + h * 256, D + (h + 1) * 256)
            qh = q_ref[:, sl].astype(BF16)
            kh = kv_ref[:, sl].astype(BF16)
            vh = kv_ref[:, vs].astype(BF16)
            doh = do_ref[:, sl].astype(BF16)
            p = _xattn_probs(qh, kh)
            dp = _nt(doh, vh)
            ds = (p * (dp - jnp.sum(p * dp, axis=-1, keepdims=True)) * 0.0625).astype(BF16)
            dq_ref[:, sl] = _nn(ds, kh).astype(dq_ref.dtype)
            dkv_ref[:, sl] += _tn(ds, qh)
            dkv_ref[:, vs] += _tn(p.astype(BF16), doh)

    return pl.pallas_call(
        body, name="xattn_bwd", grid=(S // TQ,),
        in_specs=[pl.BlockSpec((TQ, D), lambda i: (i, 0)), pl.BlockSpec((MEM, 2 * D), lambda i: (0, 0)),
                  pl.BlockSpec((TQ, D), lambda i: (i, 0))],
        out_specs=[pl.BlockSpec((TQ, D), lambda i: (i, 0)), pl.BlockSpec((MEM, 2 * D), lambda i: (0, 0))],
        out_shape=[jax.ShapeDtypeStruct((S, D), BF16), jax.ShapeDtypeStruct((MEM, 2 * D), F32)],
        compiler_params=_cparams(("arbitrary",)),
    )(q, kv, d_o)


def _s5_disc(a_re, a_im, log_dt, b_re, b_im):
    dt = jnp.exp(log_dt)[:, None]
    mag = jnp.exp(dt * a_re)
    abr = mag * jnp.cos(dt * a_im)
    abi = mag * jnp.sin(dt * a_im)
    nr, ni = abr - 1.0, abi
    inv = 1.0 / (a_re * a_re + a_im * a_im)
    cr = (nr * a_re + ni * a_im) * inv
    ci = (ni * a_re - nr * a_im) * inv
    bbr = cr[..., None] * b_re - ci[..., None] * b_im
    bbi = cr[..., None] * b_im + ci[..., None] * b_re
    return abr, abi, bbr, bbi


def _blockdiag(t):
    g, a, b = t.shape
    eye = jnp.eye(g, dtype=t.dtype)
    return (eye[:, None, :, None] * t[:, :, None, :]).reshape(g * a, g * b)


def _blocks(mat, a, b):
    return jnp.einsum("gagb->gab", mat.reshape(NG, a, NG, b))


def _fwd_even(i, x, P, W):
    hn = rms_fwd(x, P["norm_ab"][i:i + 1], "rms_ab_fwd")
    z = mm(m2(hn), W["w_in"], "nn", "in_ab")
    o, lse, cat = attn_fwd(z)
    cat = pool_fwd(z, W["pool_w"], P["pool_scale"][i:i + 1], cat)
    x_mid = mm(m2(cat), W["w_out"], "nn", "out_ab", add=m2(x))
    return x_mid, dict(x=x, hn=hn, z=z, o=o, lse=lse, cat=cat)


def _bwd_even(i, dx_mid, sv, P, W, G, GW):
    z = sv["z"]
    d_cat = mm(m2(dx_mid), W["w_out"], "nt", "out_ab_dx")
    GW["w_out"] = mm(m2(sv["cat"]), m2(dx_mid), "tn", "out_ab_dw").reshape(4, 512, D)
    dq, dk, dv, dga = attn_bwd(z, d_cat, sv["o"], sv["lse"])
    dvb, dgb, dpw, dps = pool_bwd(z, d_cat, W["pool_w"], P["pool_scale"][i:i + 1])
    GW["pool_w"] = dpw.reshape(4, 4, 64, 256).transpose(1, 0, 2, 3).reshape(4, 256, 256)
    G["pool_scale"][i] = dps[0]
    d_z = assemble_dz_even((dq, dk, dv, dga, dvb, dgb))
    d_hn = mm(m2(d_z), W["w_in"], "nt", "in_ab_dx")
    GW["w_in"] = mm(m2(sv["hn"]), m2(d_z), "tn", "in_ab_dw", out=outcs(D, 1536))
    return d_hn, P["norm_ab"][i:i + 1], "norm_ab", "rms_ab_bwd"


def _fwd_odd(i, x, P, W):
    hn = rms_fwd(x, P["norm_cd"][i:i + 1], "rms_cd_fwd")
    z = mm(m2(hn), W["w_in"], "nn", "in_cd")
    bfull = jnp.repeat(P["sgu_b"][i].T, 256, axis=1)
    c_out = sgu_fwd(z, P["sgu_ln_g"][i:i + 1], P["sgu_ln_b"][i:i + 1], P["sgu_w"][i], bfull)
    disc, disc_vjp = jax.vjp(_s5_disc, P["s5_a_re"][i], P["s5_a_im"][i], P["s5_log_dt"][i],
                             P["s5_b_re"][i], P["s5_b_im"][i])
    abr, abi, bbr, bbi = disc
    bbd = jnp.concatenate([_blockdiag(bbr.transpose(0, 2, 1)), _blockdiag(bbi.transpose(0, 2, 1))], axis=1)
    cbd = jnp.concatenate([_blockdiag(P["s5_c_re"][i].transpose(0, 2, 1)),
                           -_blockdiag(P["s5_c_im"][i].transpose(0, 2, 1))], axis=0)
    abr, abi = abr.reshape(1, NS), abi.reshape(1, NS)
    bu = mm(m2(z, 3072, 512), m2(bbd), "nn", "s5_bu")
    h = scan_fwd(bu, abr, abi)
    hc = mm(m2(h), m2(cbd), "nn", "s5_hc")
    dskip = P["s5_d"][i:i + 1]
    ypre, yg = s5_post(hc, z, dskip)
    w12 = W["w12"]
    t = mm(m2(yg), m2(w12), "nn", "glu_t")
    cat = glu_fwd(t, z, c_out)
    x_mid = mm(m2(cat), W["w_out"], "nn", "out_cd", add=m2(x))
    return x_mid, dict(x=x, hn=hn, z=z, bfull=bfull, disc_vjp=disc_vjp, bbd=bbd, cbd=cbd, abr=abr,
                       abi=abi, h=h, ypre=ypre, yg=yg, w12=w12, t=t, cat=cat, dskip=dskip)


def _bwd_odd(i, dx_mid, sv, P, W, G, GW):
    z = sv["z"]
    d_cat = mm(m2(dx_mid), W["w_out"], "nt", "out_cd_dx")
    GW["w_out"] = mm(m2(sv["cat"]), m2(dx_mid), "tn", "out_cd_dw").reshape(4, 384, D)
    du, dv, dgc, dws, dbs, dlg, dlb = sgu_bwd(z, d_cat, P["sgu_ln_g"][i:i + 1], P["sgu_ln_b"][i:i + 1],
                                               P["sgu_w"][i], sv["bfull"])
    G["sgu_w"][i], G["sgu_b"][i] = dws, dbs[:, :4].T
    G["sgu_ln_g"][i], G["sgu_ln_b"][i] = dlg[0], dlb[0]
    dt, dgd = glu_bwd(sv["t"], z, d_cat)
    gw12 = mm(m2(sv["yg"]), m2(dt), "tn", "glu_dw")
    GW["glu_w1"] = gw12[:, :512].reshape(4, 128, 512)
    GW["glu_w2"] = gw12[:, 512:].reshape(4, 128, 512)
    dyg = mm(m2(dt), m2(sv["w12"]), "nt", "glu_dx")
    dypre, dxd1, dd = s5_post_bwd(dyg, sv["ypre"], z, sv["dskip"])
    G["s5_d"][i] = dd[0]
    gcbd = mm(m2(sv["h"]), m2(dypre), "tn", "s5_dc")
    G["s5_c_re"][i] = _blocks(gcbd[:NS], NP, NH).transpose(0, 2, 1)
    G["s5_c_im"][i] = -_blocks(gcbd[NS:], NP, NH).transpose(0, 2, 1)
    eta = mm(m2(dypre), m2(sv["cbd"]), "nt", "s5_eta")
    lam, dacc = scan_bwd(eta, sv["h"], sv["abr"], sv["abi"])
    gbbd = mm(m2(z, 3072, 512), m2(lam), "tn", "s5_db")
    dxd = mm(m2(lam), m2(sv["bbd"]), "nt", "s5_dx", add=m2(dxd1))
    dacc = jnp.sum(dacc, axis=0)
    d_bbr = _blocks(gbbd[:, :NS], NH, NP).transpose(0, 2, 1)
    d_bbi = _blocks(gbbd[:, NS:], NH, NP).transpose(0, 2, 1)
    (G["s5_a_re"][i], G["s5_a_im"][i], G["s5_log_dt"][i], G["s5_b_re"][i], G["s5_b_im"][i]) = sv["disc_vjp"](
        (dacc[:NS].reshape(NG, NP), dacc[NS:].reshape(NG, NP), d_bbr, d_bbi))
    d_z = assemble_dz_odd(du, dv, dgc, dxd, dgd)
    d_hn = mm(m2(d_z), W["w_in"], "nt", "in_cd_dx")
    GW["w_in"] = mm(m2(sv["hn"]), m2(d_z), "tn", "in_cd_dw", out=outcs(D, 1024))
    return d_hn, P["norm_cd"][i:i + 1], "norm_cd", "rms_cd_bwd"


def _fwd_x(l, x, mem_n, P, W):
    hx = rms_fwd(x, P["norm_x"][l:l + 1], "rms_x_fwd")
    q = mm(m2(hx), W["w_xq"], "nn", "xq", out_dtype=BF16)
    kv = mm(m2(mem_n), W["w_xkv"], "nn", "xkv", out_dtype=BF16)
    ox = xattn_fwd(q, kv)
    x_out = mm(m2(ox), W["w_xo"], "nn", "xo", add=m2(x))
    return x_out, dict(x=x, hx=hx, q=q, kv=kv, ox=ox)


def _bwd_x(l, dx_out, sv, mem_n, d_memn, P, W, G, GW):
    d_ox = mm(m2(dx_out), W["w_xo"], "nt", "xo_dx", out_dtype=BF16)
    GW["w_xo"] = mm(m2(sv["ox"]), m2(dx_out), "tn", "xo_dw").reshape(4, 256, D)
    dq, dkv = xattn_bwd(sv["q"], sv["kv"], d_ox)
    GW["w_xq"] = mm(m2(sv["hx"]), m2(dq), "tn", "xq_dw").reshape(4, 256, D)
    d_hx = mm(m2(dq), W["w_xq"], "nt", "xq_dx")
    GW["w_xkv"] = mm(m2(mem_n), m2(dkv), "tn", "xkv_dw", out=outcs(D, 512))
    d_memn = mm(m2(dkv), W["w_xkv"], "nt", "xkv_dx", add=None if d_memn is None else m2(d_memn))
    dx, dg = rms_bwd(sv["x"], d_hx, dx_out, P["norm_x"][l:l + 1], "rms_x_bwd")
    G["norm_x"][l] = dg[0]
    return dx, d_memn


SMALL_LAYERS = (("norm_ab", 2), ("pool_scale", 2), ("norm_cd", 2), ("sgu_ln_g", 2), ("sgu_ln_b", 2), ("sgu_w", 2),
                ("sgu_b", 2), ("s5_a_re", 2), ("s5_a_im", 2), ("s5_log_dt", 2), ("s5_b_re", 2), ("s5_b_im", 2),
                ("s5_c_re", 2), ("s5_c_im", 2), ("s5_d", 2), ("norm_x", 4))


def local_step(x, mem, tgt, P, weights_of, grads_done):
    G = {k: [None] * n for k, n in SMALL_LAYERS}
    mem_g = P["mem_norm"].reshape(1, D)
    mem_n = rms_fwd(mem, mem_g, "rms_mem_fwd")
    saved = []
    for layer in range(4):
        i = layer // 2
        W = weights_of(layer, x)
        x, sv_m = (_fwd_even if layer % 2 == 0 else _fwd_odd)(i, x, P, W)
        x, sv_x = _fwd_x(layer, x, mem_n, P, W)
        saved.append((sv_m, sv_x, W))
    dx, loss, dgf = final_loss(x, tgt, P["final_norm"].reshape(1, D))
    G["final_norm"] = dgf[0]
    d_memn = None
    for layer in reversed(range(4)):
        i = layer // 2
        sv_m, sv_x, W = saved[layer]
        GW = {}
        dx_mid, d_memn = _bwd_x(layer, dx, sv_x, mem_n, d_memn, P, W, G, GW)
        d_hn, g, key, name = (_bwd_even if layer % 2 == 0 else _bwd_odd)(i, dx_mid, sv_m, P, W, G, GW)
        token = grads_done(layer, GW)
        if token is not None:
            g = g + token
        dx, dg = rms_bwd(sv_m["x"], d_hn, dx_mid, g, name)
        G[key][i] = dg[0]
    _, dgm = rms_bwd(mem, d_memn, d_memn, mem_g, "rms_mem_bwd")
    G["mem_norm"] = dgm[0]
    return loss, dx, G


ANY = pl.BlockSpec(memory_space=pl.ANY)


def _place():
    x, y, c = lax.axis_index("x"), lax.axis_index("y"), lax.axis_index("c")
    chips = [(1 - x, y), (x, 1 - y), (1 - x, 1 - y)]
    return x, y, c, 2 * x + y, (x, y, 1 - c), chips


def _remote(src, dst, send, recv, k, dev):
    return pltpu.make_async_remote_copy(src_ref=src, dst_ref=dst, send_sem=send.at[k], recv_sem=recv.at[k],
                                        device_id=dev, device_id_type=MESHID)


HBM = pl.BlockSpec(memory_space=pltpu.HBM)
SEM = pl.BlockSpec(memory_space=pltpu.SEMAPHORE)
EFFECT = pltpu.SideEffectType.DATAFLOW_SIDE_EFFECTING


def _hbm(t):
    return pltpu.with_memory_space_constraint(t, pltpu.HBM)


def allgather_sync(shards):
    n = len(shards)

    def body(*refs):
        ins, outs = refs[:n], refs[n:2 * n]
        token, send, recv = refs[2 * n:]
        x, y, c, jme, sib, chips = _place()
        first, passed = [], []
        for a in range(n):
            cp = _remote(ins[a], outs[a].at[jme], send, recv, a * 7 + 6, sib)
            cp.start()
            first.append(cp)
            for k, chip in enumerate(chips):
                cp = _remote(ins[a].at[c], outs[a].at[jme, c], send, recv, a * 7 + k, (*chip, c))
                cp.start()
                first.append(cp)
        for a in range(n):
            for k, chip in enumerate(chips):
                piece = outs[a].at[2 * chip[0] + chip[1], c]
                _remote(piece, piece, send, recv, a * 7 + k, (*chip, c)).wait_recv()
                fw = _remote(piece, piece, send, recv, a * 7 + 3 + k, sib)
                fw.start()
                passed.append(fw)
        for a in range(n):
            own = outs[a].at[jme]
            _remote(own, own, send, recv, a * 7 + 6, sib).wait_recv()
            for k, chip in enumerate(chips):
                piece = outs[a].at[2 * chip[0] + chip[1], 1 - c]
                _remote(piece, piece, send, recv, a * 7 + 3 + k, sib).wait_recv()
        for cp in first + passed:
            cp.wait_send()
        token[...] = jnp.zeros_like(token)

    res = pl.pallas_call(
        body, name="allgather_sync", in_specs=[ANY] * n,
        out_specs=[ANY] * n + [pl.BlockSpec(memory_space=pltpu.VMEM)],
        out_shape=[jax.ShapeDtypeStruct((4,) + s.shape, s.dtype) for s in shards] + [jax.ShapeDtypeStruct((8, 128), F32)],
        scratch_shapes=[pltpu.SemaphoreType.DMA((7 * n,)), pltpu.SemaphoreType.DMA((7 * n,))],
    )(*shards)
    return list(res[:n]), res[n]


def _gather_copies(ins, lands, send, recv):
    x, y, c, jme, sib, chips = _place()
    devs = [(*chip, c) for chip in chips] + [sib]
    return [_remote(ins[a], lands[a].at[jme], send, recv, a * 4 + k, dev)
            for a in range(len(ins)) for k, dev in enumerate(devs)]


def allgather_start(shards, after, name):
    n, na = len(shards), len(after)

    def body(*refs):
        ins, lands = refs[:n], refs[n:2 * n]
        send, recv = refs[2 * n + na], refs[2 * n + na + 1]
        token = refs[-1]
        for cp in _gather_copies(ins, lands, send, recv):
            cp.start()
        token[...] = jnp.zeros_like(token)

    res = pl.pallas_call(
        body, name=name,
        out_shape=(pltpu.SemaphoreType.DMA((4 * n,)), pltpu.SemaphoreType.DMA((4 * n,)),
                   *[pltpu.HBM(s.shape, s.dtype) for s in shards],
                   *[pltpu.HBM((4,) + s.shape, s.dtype) for s in shards],
                   jax.ShapeDtypeStruct((8, 128), F32)),
        in_specs=[HBM] * (2 * n) + [ANY] * na,
        out_specs=(SEM, SEM, *[HBM] * (2 * n), pl.BlockSpec(memory_space=pltpu.VMEM)),
        input_output_aliases={a: 2 + a for a in range(2 * n)},
        compiler_params=pltpu.CompilerParams(has_side_effects=EFFECT),
    )(*[_hbm(s) for s in shards], *[_hbm(lax.empty((4,) + s.shape, s.dtype)) for s in shards], *after)
    return res[0], res[1], list(res[2:2 + n]), list(res[2 + n:2 + 2 * n]), res[-1]


def allgather_wait(send, recv, shards, lands, after, name):
    n = len(shards)

    def body(*refs):
        ins, zones = refs[:n], refs[n:2 * n]
        send_r, recv_r = refs[2 * n], refs[2 * n + 1]
        x, y, c, jme, sib, chips = _place()
        slots = [2 * chip[0] + chip[1] for chip in chips] + [jme]
        for a in range(n):
            for k, slot in enumerate(slots):
                cp = _remote(ins[a], zones[a].at[slot], send_r, recv_r, a * 4 + k, sib)
                cp.wait_send()
                cp.wait_recv()

    res = pl.pallas_call(
        body, name=name,
        out_shape=tuple(pltpu.HBM(t.shape, t.dtype) for t in list(shards) + list(lands)),
        in_specs=[HBM] * (2 * n) + [SEM, SEM, ANY], out_specs=tuple([HBM] * (2 * n)),
        input_output_aliases={a: a for a in range(2 * n)},
        compiler_params=pltpu.CompilerParams(has_side_effects=EFFECT),
    )(*shards, *lands, send, recv, after)
    return list(res[n:])


def allgather_small(slab):
    def body(in_ref, out_ref, send, recv, lsem):
        x, y, c, jme, sib, chips = _place()
        loc = pltpu.make_async_copy(in_ref, out_ref.at[jme], lsem.at[0])
        loc.start()
        cps = [_remote(in_ref, out_ref.at[jme], send, recv, k, (*chip, c)) for k, chip in enumerate(chips)]
        for cp in cps:
            cp.start()
        for k, chip in enumerate(chips):
            piece = out_ref.at[2 * chip[0] + chip[1]]
            _remote(piece, piece, send, recv, k, (*chip, c)).wait_recv()
        for cp in cps:
            cp.wait_send()
        loc.wait()

    return pl.pallas_call(
        body, name="allgather_small", in_specs=[ANY], out_specs=ANY,
        out_shape=jax.ShapeDtypeStruct((4,) + slab.shape, slab.dtype),
        scratch_shapes=[pltpu.SemaphoreType.DMA((3,)), pltpu.SemaphoreType.DMA((3,)), pltpu.SemaphoreType.DMA((1,))],
    )(slab)


def allreduce_small(v):
    def body(v_ref, o_ref, r0, r1, r2, send, recv):
        x, y, c, jme, sib, chips = _place()
        peers = [sib, (1 - x, y, c), (x, 1 - y, c)]
        o_ref[...] = v_ref[...]
        for k, buf in enumerate((r0, r1, r2)):
            cp = _remote(o_ref, buf, send, recv, k, peers[k])
            cp.start()
            cp.wait()
            o_ref[...] = o_ref[...] + buf[...]

    vm = pl.BlockSpec(memory_space=pltpu.VMEM)
    return pl.pallas_call(
        body, name="allreduce_small", in_specs=[vm], out_specs=vm,
        out_shape=jax.ShapeDtypeStruct(v.shape, v.dtype),
        scratch_shapes=[pltpu.VMEM(v.shape, v.dtype)] * 3 + [pltpu.SemaphoreType.DMA((3,)), pltpu.SemaphoreType.DMA((3,))],
        compiler_params=pltpu.CompilerParams(vmem_limit_bytes=VMEM_LIMIT),
    )(v)


def rs_pair_exchange(gs):
    n = len(gs)

    def body(*refs):
        ins, outs = refs[:n], refs[n:2 * n]
        send, recv = refs[2 * n:]
        x, y, c, jme, sib, chips = _place()
        cps = [_remote(ins[a].at[:, 1 - c], outs[a], send, recv, a, sib) for a in range(n)]
        for cp in cps:
            cp.start()
        for cp in cps:
            cp.wait()

    return pl.pallas_call(
        body, name="rs_pair_exchange", in_specs=[ANY] * n, out_specs=[ANY] * n,
        out_shape=[jax.ShapeDtypeStruct((4,) + g.shape[2:], F32) for g in gs],
        scratch_shapes=[pltpu.SemaphoreType.DMA((n,)), pltpu.SemaphoreType.DMA((n,))],
    )(*gs)


def rs_pair_sum(g4, got, cidx):
    _, _, rh, cols = g4.shape
    tr = rh if rh <= 256 else 256

    def body(c_ref, a_ref, b_ref, o_ref):
        o_ref[...] = (a_ref[...] + b_ref[...]).astype(o_ref.dtype)

    return pl.pallas_call(
        body, name="rs_pair_sum",
        grid_spec=pltpu.PrefetchScalarGridSpec(
            num_scalar_prefetch=1, grid=(4, rh // tr),
            in_specs=[pl.BlockSpec((None, None, tr, cols), lambda j, t, cr: (j, cr[0], t, 0)),
                      pl.BlockSpec((None, tr, cols), lambda j, t, cr: (j, t, 0))],
            out_specs=pl.BlockSpec((None, tr, cols), lambda j, t, cr: (j, t, 0))),
        out_shape=jax.ShapeDtypeStruct((4, rh, cols), BF16),
        compiler_params=_cparams(("parallel", "parallel")),
    )(cidx, g4, got)


def _chip_copies(ps, lands, send, recv):
    x, y, c, jme, sib, chips = _place()
    return [_remote(ps[a].at[2 * chip[0] + chip[1]], lands[a].at[jme], send, recv, a * 3 + k, (*chip, c))
            for a in range(len(ps)) for k, chip in enumerate(chips)]


def rs_chip_start(ps, name):
    n = len(ps)

    def body(*refs):
        ins, lands = refs[:n], refs[n:2 * n]
        send, recv = refs[2 * n], refs[2 * n + 1]
        token = refs[-1]
        for cp in _chip_copies(ins, lands, send, recv):
            cp.start()
        token[...] = jnp.zeros_like(token)

    res = pl.pallas_call(
        body, name=name,
        out_shape=(pltpu.SemaphoreType.DMA((3 * n,)), pltpu.SemaphoreType.DMA((3 * n,)),
                   *[pltpu.HBM(p.shape, p.dtype) for p in ps], *[pltpu.HBM(p.shape, p.dtype) for p in ps],
                   jax.ShapeDtypeStruct((8, 128), F32)),
        in_specs=[HBM] * (2 * n), out_specs=(SEM, SEM, *[HBM] * (2 * n), pl.BlockSpec(memory_space=pltpu.VMEM)),
        input_output_aliases={a: 2 + a for a in range(2 * n)},
        compiler_params=pltpu.CompilerParams(has_side_effects=EFFECT),
    )(*[_hbm(p) for p in ps], *[_hbm(lax.empty(p.shape, p.dtype)) for p in ps])
    return res[0], res[1], list(res[2:2 + n]), list(res[2 + n:2 + 2 * n]), res[-1]


def rs_chip_wait(send, recv, ps, lands, after, name):
    n = len(ps)

    def body(*refs):
        ins, zones = refs[:n], refs[n:2 * n]
        send_r, recv_r = refs[2 * n], refs[2 * n + 1]
        x, y, c, jme, sib, chips = _place()
        for a in range(n):
            for k, chip in enumerate(chips):
                jt = 2 * chip[0] + chip[1]
                cp = _remote(ins[a].at[jt], zones[a].at[jt], send_r, recv_r, a * 3 + k, (*chip, c))
                cp.wait_send()
                cp.wait_recv()

    res = pl.pallas_call(
        body, name=name,
        out_shape=tuple(pltpu.HBM(p.shape, p.dtype) for p in list(ps) + list(lands)),
        in_specs=[HBM] * (2 * n) + [SEM, SEM, ANY], out_specs=tuple([HBM] * (2 * n)),
        input_output_aliases={a: a for a in range(2 * n)},
        compiler_params=pltpu.CompilerParams(has_side_effects=EFFECT),
    )(*ps, *lands, send, recv, after)
    return list(res[n:])


def rs_chip_sum(q, p, l, acc, layers, jc):
    _, rh, cols = q.shape
    tr = rh if rh <= 256 else 256

    def body(jc_ref, q_ref, p_ref, *rest):
        o_ref = rest[-1]
        jme = jc_ref[0]
        own = p_ref[...].astype(F32)
        v = [jnp.where(jme == j, own, q_ref[j].astype(F32)) for j in range(4)]
        o_ref[...] = ((v[0] + v[1]) + v[2]) + v[3]

    in_specs = [pl.BlockSpec((4, tr, cols), lambda t, jr: (0, t, 0)),
                pl.BlockSpec((None, tr, cols), lambda t, jr: (jr[0], t, 0))]
    args = [jc, q, p]
    if acc is not None:
        in_specs.append(ANY)
        args.append(acc)
    return pl.pallas_call(
        body, name="rs_chip_sum",
        grid_spec=pltpu.PrefetchScalarGridSpec(
            num_scalar_prefetch=1, grid=(rh // tr,), in_specs=in_specs,
            out_specs=pl.BlockSpec((None, None, tr, cols), lambda t, jr: (l, jr[1], t, 0))),
        out_shape=jax.ShapeDtypeStruct((layers, 2, rh, cols), F32),
        input_output_aliases={} if acc is None else {3: 0},
        compiler_params=_cparams(("parallel",)),
    )(*args)


def rs_pair_gather(rs):
    n = len(rs)

    def body(*refs):
        outs = refs[n:2 * n]
        send, recv = refs[2 * n:]
        x, y, c, jme, sib, chips = _place()
        cps = [_remote(outs[a].at[:, c], outs[a].at[:, c], send, recv, a, sib) for a in range(n)]
        for cp in cps:
            cp.start()
        for a in range(n):
            slot = outs[a].at[:, 1 - c]
            _remote(slot, slot, send, recv, a, sib).wait_recv()
        for cp in cps:
            cp.wait_send()

    return pl.pallas_call(
        body, name="rs_pair_gather", in_specs=[ANY] * n, out_specs=[ANY] * n,
        out_shape=[jax.ShapeDtypeStruct(r.shape, r.dtype) for r in rs],
        input_output_aliases={a: a for a in range(n)},
        scratch_shapes=[pltpu.SemaphoreType.DMA((n,)), pltpu.SemaphoreType.DMA((n,))],
    )(*rs)


def _adamw_math(w, g, m, v):
    m = B1 * m + (1.0 - B1) * g
    v = B2 * v + (1.0 - B2) * (g * g)
    m_hat = m / (1.0 - B1 ** STEP)
    v_hat = v / (1.0 - B2 ** STEP)
    return -LR * (m_hat / (jnp.sqrt(v_hat) + AEPS) + WD * w), m, v


def adamw(w, g, m, v, name):
    rows, cols = w.shape
    tr = 256 if rows % 256 == 0 else rows
    return rw(_adamw_math, [(a, 0, cols) for a in (w, g, m, v)], [(cols, F32)] * 3, name, rows, tr=tr)


WEIGHTS = ["norm_ab", "w_in_ab", "pool_w", "pool_scale", "w_out_ab", "norm_cd", "w_in_cd", "sgu_ln_g", "sgu_ln_b",
           "sgu_w", "sgu_b", "s5_a_re", "s5_a_im", "s5_log_dt", "s5_b_re", "s5_b_im", "s5_c_re", "s5_c_im", "s5_d",
           "glu_w1", "glu_w2", "w_out_cd", "norm_x", "w_xq", "w_xkv", "w_xo", "mem_norm", "final_norm"]
INPUTS = ["x", "mem"] + WEIGHTS + ["loss_target"] + ["m_" + n for n in WEIGHTS] + ["v_" + n for n in WEIGHTS]
BIG = ["w_in_ab", "w_out_ab", "w_in_cd", "w_out_cd", "w_xq", "w_xkv", "w_xo", "glu_w1", "glu_w2", "pool_w"]
COL_SHARDED = ("w_in_ab", "w_in_cd", "w_xkv")
SMALL = [n for n in WEIGHTS if n not in BIG]
SMALL_SHARDED = {"norm_cd": 256, "sgu_ln_g": 256, "sgu_ln_b": 256, "s5_d": 128}
PACK = 256 * 128


def _pack(arrs):
    flat = jnp.concatenate([a.reshape(-1) for a in arrs])
    pad = (-flat.shape[0]) % PACK
    return jnp.concatenate([flat, jnp.zeros((pad,), flat.dtype)]).reshape(-1, 128)


def _unpack(packed, shapes):
    flat, out, off = packed.reshape(-1), [], 0
    for s in shapes:
        n = 1
        for d in s:
            n *= d
        out.append(flat[off:off + n].reshape(s))
        off += n
    return out


LAYER_KEYS = (("w_in", "w_out", "pool_w", "w_xq", "w_xkv", "w_xo"),
              ("w_in", "w_out", "glu_w1", "glu_w2", "w_xq", "w_xkv", "w_xo"))


def _weight_of(key, layer):
    if key in ("w_xq", "w_xkv", "w_xo"):
        return key, layer, 4
    kind = "ab" if layer % 2 == 0 else "cd"
    return {"w_in": "w_in_" + kind, "w_out": "w_out_" + kind}.get(key, key), layer // 2, 2


def kernel(*args):
    a = dict(zip(INPUTS, args))
    x_i, y_i, c_i = lax.axis_index("x"), lax.axis_index("y"), lax.axis_index("c")
    j = 2 * x_i + y_i

    slab = jnp.concatenate([a["norm_cd"], a["sgu_ln_g"], a["sgu_ln_b"],
                            jnp.pad(a["s5_d"], ((0, 0), (0, 128)))], axis=0)
    gslab = allgather_small(slab)
    P = {n: a[n] for n in SMALL}
    for k, n in enumerate(("norm_cd", "sgu_ln_g", "sgu_ln_b", "s5_d")):
        wd = SMALL_SHARDED[n]
        P[n] = gslab[:, 2 * k:2 * k + 2, :wd].transpose(1, 0, 2).reshape(2, 4 * wd)

    def shards_of(layer):
        keys = sorted(k for k in LAYER_KEYS[layer % 2])
        out = []
        for k in keys:
            n, l, _ = _weight_of(k, layer)
            out.append(a[n][l].reshape(-1, a[n].shape[-1]).astype(BF16))
        return keys, out

    keys0, sh0 = shards_of(0)
    g0, token = allgather_sync([s.reshape(2, s.shape[0] // 2, s.shape[1]) for s in sh0])
    gathered = {0: dict(zip(keys0, [g.reshape(4, -1, g.shape[-1]) for g in g0]))}
    started = {}
    for layer in (1, 2, 3):
        keys, sh = shards_of(layer)
        send, recv, sh, lands, token = allgather_start(sh, [token, gslab], "allgather_start_%d" % layer)
        started[layer] = (keys, send, recv, sh, lands)
    P["norm_ab"] = P["norm_ab"] + token[0:1, 0:1]

    cidx = jnp.reshape(c_i, (1,)).astype(jnp.int32)
    jc = jnp.stack([j, c_i]).astype(jnp.int32)

    def weights_of(layer, x_in):
        if layer not in gathered:
            keys, send, recv, sh, lands = started[layer]
            gathered[layer] = dict(zip(keys, allgather_wait(send, recv, sh, lands, x_in, "allgather_wait_%d" % layer)))
        g = gathered[layer]
        W = {}
        for k, v in g.items():
            if k in ("w_in", "w_xkv"):
                W[k] = mcs(v)
            elif k == "pool_w":
                W[k] = v.reshape(4, 4, 64, 256).transpose(1, 0, 2, 3).reshape(4, 256, 256)
            elif k not in ("glu_w1", "glu_w2"):
                W[k] = m2(v.reshape(-1, v.shape[-1]))
        if layer % 2 == 1:
            W["w12"] = jnp.concatenate([g["glu_w1"].reshape(512, 512), g["glu_w2"].reshape(512, 512)], axis=1)
        return W

    pending = {}

    def grads_done(layer, GW):
        keys = sorted(GW)
        flat = [GW[k].reshape(4, 2, GW[k].shape[1] // 2, GW[k].shape[2]) for k in keys]
        pair = [rs_pair_sum(g4, r, cidx) for g4, r in zip(flat, rs_pair_exchange(flat))]
        send, recv, pair, lands, token = rs_chip_start(pair, "rs_chip_start_%d" % layer)
        pending[layer] = (keys, send, recv, pair, lands)
        return token[0:1, 0:1]

    loss, dx, G = local_step(a["x"][0], a["mem"][0], a["loss_target"][0], P, weights_of, grads_done)
    loss = lax.psum(loss[0, 0], ("x", "y", "c"))

    red = {}
    for layer in (3, 2, 1, 0):
        keys, send, recv, pair, lands = pending[layer]
        lands = rs_chip_wait(send, recv, pair, lands, dx, "rs_chip_wait_%d" % layer)
        for k, q, p in zip(keys, lands, pair):
            n, l, layers = _weight_of(k, layer)
            red[n] = rs_chip_sum(q, p, l, red.get(n), layers, jc)
    gbig = dict(zip(BIG, rs_pair_gather([red[n] for n in BIG])))

    outs = {}
    for n in BIG:
        shp = a[n].shape
        g2 = gbig[n].reshape(-1, shp[-1])
        d2, m2_, v2_ = adamw(a[n].reshape(g2.shape), g2, a["m_" + n].reshape(g2.shape),
                             a["v_" + n].reshape(g2.shape), "adamw_" + n)
        outs[n] = tuple(t.reshape(shp) for t in (g2, d2, m2_, v2_))

    gfull = [jnp.stack(G[n]) if isinstance(G[n], list) else G[n] for n in SMALL]
    shapes = [g.shape for g in gfull]
    gsum = _unpack(allreduce_small(_pack(gfull)), shapes)
    gloc = []
    for n, g in zip(SMALL, gsum):
        if n in SMALL_SHARDED:
            g = lax.dynamic_slice_in_dim(g, j * SMALL_SHARDED[n], SMALL_SHARDED[n], axis=1)
        gloc.append(g)
    lshapes = [a[n].shape for n in SMALL]
    packed = [_pack(t) for t in ([a[n] for n in SMALL], gloc, [a["m_" + n] for n in SMALL], [a["v_" + n] for n in SMALL])]
    small = [_unpack(t, lshapes) for t in adamw(*packed, "adamw_small")]
    for k, n in enumerate(SMALL):
        outs[n] = (gloc[k], small[0][k], small[1][k], small[2][k])

    res = [loss, dx[None]]
    for part in range(4):
        res += [outs[n][part] for n in WEIGHTS]
    return tuple(res)
```

```python
import math

import jax
import jax.numpy as jnp
from jax import lax
from jax.experimental import pallas as pl
from jax.experimental.pallas import tpu as pltpu

F32, BF16 = jnp.float32, jnp.bfloat16
S, D = 2048, 1024
MEM = 256
EPS = 1e-6
NEG = -1e30
QB = 128
PATTERNS = (1, 4, 16)
NG, NP, NH = 32, 64, 16
NS = NG * NP
LR, B1, B2, AEPS, WD, STEP = 0.001, 0.9, 0.999, 1e-08, 0.01, 10
MESHID = pl.DeviceIdType.MESH
VMEM_LIMIT = 56 * 1024 * 1024


def _cparams(sem):
    return pltpu.CompilerParams(dimension_semantics=sem, vmem_limit_bytes=VMEM_LIMIT)


def _sig(x):
    return 1.0 / (1.0 + jnp.exp(-x))


def _dot(a, b, dims):
    return lax.dot_general(a, b, (dims, ((), ())), preferred_element_type=F32)


def _nn(a, b):
    return _dot(a, b, ((1,), (0,)))


def _nt(a, b):
    return _dot(a, b, ((1,), (1,)))


def _tn(a, b):
    return _dot(a, b, ((0,), (0,)))


_DIMS = {"nn": ((1,), (0,)), "nt": ((1,), (1,)), "tn": ((0,), (0,))}


def _tile(dim, cc=None, cap=1024):
    for t in (2048, 1536, 1024, 768, 512, 384, 256, 128):
        if t <= cap and dim % t == 0 and (cc is None or cc % t == 0):
            return t
    return dim


MM_VMEM = 36 * 1024 * 1024


def _mm_tiles(m, n, k, ccm, ccn, cck, a_bytes, b_bytes, o_bytes):
    caps = [1024, 1024, 2048]
    while True:
        tm, tn, tk = _tile(m, ccm, caps[0]), _tile(n, ccn, caps[1]), _tile(k, cck, caps[2])
        need = 2 * (tm * tk * a_bytes + tk * tn * b_bytes + tm * tn * o_bytes) + (tm * tn * 4 if tk < k else 0)
        if need <= MM_VMEM:
            return tm, tn, tk
        if tk > 1024:
            caps[2] = tk // 2
        elif tn >= tm:
            caps[1] = tn // 2
        else:
            caps[0] = tm // 2


def m2(arr, col_off=0, ncols=None):
    rows, cols = arr.shape
    ncols = cols - col_off if ncols is None else ncols

    def spec(tr, tc, rc):
        assert col_off % tc == 0
        return pl.BlockSpec((tr, tc), lambda *g: (rc(*g)[0], rc(*g)[1] + col_off // tc))
    return (arr, rows, ncols, spec, None if col_off == 0 else col_off)


def mcs(arr):
    cs = arr.shape[2]

    def spec(tr, tc, rc):
        n = cs // tc
        return pl.BlockSpec((None, tr, tc), lambda *g: (rc(*g)[1] // n, rc(*g)[0], rc(*g)[1] % n))
    return (arr, arr.shape[1], 4 * cs, spec, cs)


def out2(rows, cols):
    def spec(tr, tc, rc):
        return pl.BlockSpec((tr, tc), lambda *g: tuple(rc(*g)))
    return ((rows, cols), spec, None)


def outcs(rows, cs):
    def spec(tr, tc, rc):
        n = cs // tc
        return pl.BlockSpec((None, tr, tc), lambda *g: (rc(*g)[1] // n, rc(*g)[0], rc(*g)[1] % n))
    return ((4, rows, cs), spec, cs)


def _both(a, b):
    if a is None:
        return b
    if b is None:
        return a
    return math.gcd(a, b)


def mm(a, b, mode, name, add=None, out=None, out_dtype=F32):
    a_arr, a_r, a_c, a_spec, a_cc = a
    b_arr, b_r, b_c, b_spec, b_cc = b
    if mode == "nn":
        m, k, n = a_r, a_c, b_c
        assert b_r == k
        ccm, cck, ccn = None, a_cc, b_cc
    elif mode == "nt":
        m, k, n = a_r, a_c, b_r
        assert b_c == k
        ccm, cck, ccn = None, _both(a_cc, b_cc), None
    else:
        m, k, n = a_c, a_r, b_c
        assert b_r == k
        ccm, cck, ccn = a_cc, None, b_cc
    out = out2(m, n) if out is None else out
    o_shape, o_spec, o_cc = out
    ccn = _both(ccn, o_cc)
    if add is not None:
        ccn = _both(ccn, add[4])
    o_bytes = jnp.dtype(out_dtype).itemsize + (0 if add is None else add[0].dtype.itemsize)
    tm, tn, tk = _mm_tiles(m, n, k, ccm, ccn, cck, a_arr.dtype.itemsize, b_arr.dtype.itemsize, o_bytes)
    nk = k // tk
    if mode == "nn":
        in_specs = [a_spec(tm, tk, lambda i, j, kk: (i, kk)), b_spec(tk, tn, lambda i, j, kk: (kk, j))]
    elif mode == "nt":
        in_specs = [a_spec(tm, tk, lambda i, j, kk: (i, kk)), b_spec(tn, tk, lambda i, j, kk: (j, kk))]
    else:
        in_specs = [a_spec(tk, tm, lambda i, j, kk: (kk, i)), b_spec(tk, tn, lambda i, j, kk: (kk, j))]
    args = [a_arr, b_arr]
    if add is not None:
        in_specs.append(add[3](tm, tn, lambda i, j, kk: (i, j)))
        args.append(add[0])
    dims = _DIMS[mode]
    has_add = add is not None

    def body(*refs):
        a_ref, b_ref = refs[0], refs[1]
        add_ref = refs[2] if has_add else None
        prod = _dot(a_ref[...].astype(BF16), b_ref[...].astype(BF16), dims)
        if nk == 1:
            o_ref = refs[-1]
            if has_add:
                prod = prod + add_ref[...].astype(F32)
            o_ref[...] = prod.astype(o_ref.dtype)
            return
        o_ref, acc = refs[-2], refs[-1]
        kk = pl.program_id(2)

        @pl.when(kk == 0)
        def _():
            acc[...] = prod

        @pl.when(kk > 0)
        def _():
            acc[...] += prod

        @pl.when(kk == nk - 1)
        def _():
            r = acc[...]
            if has_add:
                r = r + add_ref[...].astype(F32)
            o_ref[...] = r.astype(o_ref.dtype)

    return pl.pallas_call(
        body, name=name, grid=(m // tm, n // tn, nk), in_specs=in_specs,
        out_specs=o_spec(tm, tn, lambda i, j, kk: (i, j)),
        out_shape=jax.ShapeDtypeStruct(o_shape, out_dtype),
        scratch_shapes=[pltpu.VMEM((tm, tn), F32)] if nk > 1 else [],
        compiler_params=_cparams(("parallel", "parallel", "arbitrary")),
    )(*args)


def rw(fn, ins, outs, name, rows, tr=256, consts=(), accs=()):
    n_in, n_c, n_o, n_a = len(ins), len(consts), len(outs), len(accs)
    in_specs = []
    for arr, off, width in ins:
        assert off % width == 0
        in_specs.append(pl.BlockSpec((tr, width), lambda i, o=off // width: (i, o)))
    for c in consts:
        in_specs.append(pl.BlockSpec(c.shape, lambda i: (0, 0)))
    out_specs = [pl.BlockSpec((tr, w), lambda i: (i, 0)) for w, _ in outs]
    out_specs += [pl.BlockSpec(s, lambda i: (0, 0)) for s in accs]
    out_shape = [jax.ShapeDtypeStruct((rows, w), dt) for w, dt in outs]
    out_shape += [jax.ShapeDtypeStruct(s, F32) for s in accs]

    def body(*refs):
        vals = [r[...] for r in refs[:n_in + n_c]]
        o_refs = refs[n_in + n_c:n_in + n_c + n_o]
        a_refs = refs[n_in + n_c + n_o:]
        res = fn(*vals)
        for r, v in zip(o_refs, res[:n_o]):
            r[...] = v.astype(r.dtype)
        if n_a:
            @pl.when(pl.program_id(0) == 0)
            def _():
                for r in a_refs:
                    r[...] = jnp.zeros_like(r)
            for r, v in zip(a_refs, res[n_o:]):
                r[...] += v

    res = pl.pallas_call(
        body, name=name, grid=(rows // tr,), in_specs=in_specs, out_specs=out_specs,
        out_shape=out_shape,
        compiler_params=_cparams(("arbitrary",) if n_a else ("parallel",)),
    )(*[a for a, _, _ in ins], *consts)
    return res


def _rstd(x):
    return lax.rsqrt(jnp.mean(x * x, axis=-1, keepdims=True) + EPS)


def rms_fwd(x, g, name):
    def fn(xv, gv):
        xv = xv.astype(F32)
        return (xv * _rstd(xv) * gv,)
    return rw(fn, [(x, 0, D)], [(D, BF16)], name, x.shape[0], consts=[g])[0]


def _rms_bwd_math(xv, dy, gv):
    r = _rstd(xv)
    dyg = dy * gv
    dx = r * dyg - xv * (r * r * r / D) * jnp.sum(dyg * xv, axis=-1, keepdims=True)
    dg = jnp.sum(dy * xv * r, axis=0, keepdims=True)
    return dx, dg


def rms_bwd(x, dy, dres, g, name):
    def fn(xv, dyv, drv, gv):
        dx, dg = _rms_bwd_math(xv, dyv, gv)
        return dx + drv, dg
    return rw(fn, [(x, 0, D), (dy, 0, D), (dres, 0, D)], [(D, F32)], name, x.shape[0],
              consts=[g], accs=[(1, D)])


def final_loss(x, tgt, g):
    def fn(xv, tv, gv):
        e = xv * _rstd(xv) * gv - tv
        loss = 0.5 * jnp.sum(jnp.sum(e * e, axis=-1, keepdims=True), axis=0, keepdims=True) / D
        dx, dg = _rms_bwd_math(xv, e / D, gv)
        return dx, loss, dg
    return rw(fn, [(x, 0, D), (tgt, 0, D)], [(D, F32)], "final_loss", S, consts=[g],
              accs=[(1, 1), (1, D)])


def _attn_bias(bias_ref):
    ii = lax.broadcasted_iota(jnp.int32, (2 * QB, 2 * QB), 0) % QB
    jj = lax.broadcasted_iota(jnp.int32, (2 * QB, 2 * QB), 1)
    dist = ii + QB - jj
    band = (dist >= 0) & (dist <= QB)
    bias_ref[1] = jnp.where(band, 0.0, NEG)
    bias_ref[0] = jnp.where(band & (jj >= QB), 0.0, NEG)


def _two_heads(x, m0):
    return jnp.concatenate([jnp.where(m0, x, 0.0), jnp.where(m0, 0.0, x)], axis=0)


def _per_head(col, m0):
    return jnp.where(m0, col[:QB], col[QB:])


def _attn_rows(idx, d):
    if d == 1:
        b = idx
        cur = pl.ds(pl.multiple_of(b * QB, QB), QB)
        prev = pl.ds(pl.multiple_of(jnp.maximum(b - 1, 0) * QB, QB), QB)
    else:
        r, b = lax.rem(idx, d), lax.div(idx, d)
        cur = pl.ds(r + b * (QB * d), QB, stride=d)
        prev = pl.ds(r + jnp.maximum(b - 1, 0) * (QB * d), QB, stride=d)
    return cur, prev, b


NBLK = S // QB
GROUP = 4


def _colblk(off):
    return pl.BlockSpec((S, 128), lambda hp: (0, off * 8 + hp))


def attn_fwd(z):
    def body(q_ref, k_ref, v_ref, g_ref, o_ref, l_ref, a_ref, os, ls, bias):
        _attn_bias(bias)
        m0 = lax.broadcasted_iota(jnp.int32, (1, 128), 1) < 64
        for pi, d in enumerate(PATTERNS):
            def load(idx, d=d):
                cur, prev, b = _attn_rows(idx, d)
                return cur, (q_ref[cur, :], k_ref[prev, :], k_ref[cur, :], v_ref[prev, :], v_ref[cur, :],
                             bias[jnp.minimum(b, 1)])

            def block(q, kp, kc, vp, vc, bs):
                qq = _two_heads(q * 0.125, m0).astype(BF16)
                k = jnp.concatenate([kp, kc], axis=0).astype(BF16)
                s = _nt(qq, k) + bs
                mx = jnp.max(s, axis=-1, keepdims=True)
                p = jnp.exp(s - mx)
                den = jnp.sum(p, axis=-1, keepdims=True)
                pb = p.astype(BF16)
                vv = _two_heads(jnp.concatenate([vp, vc], axis=0), m0).astype(BF16)
                o = _nn(jnp.concatenate([pb[:QB], pb[QB:]], axis=1), vv)
                return o * _per_head(1.0 / den, m0), _per_head(mx + jnp.log(den), m0)

            def step(i, carry, pi=pi):
                loaded = [load(i * GROUP + u) for u in range(GROUP)]
                done = [block(*vals) for _, vals in loaded]
                for (cur, _), (o, l) in zip(loaded, done):
                    os[pi, cur, :] = o
                    ls[pi, cur, :] = l
                return carry
            lax.fori_loop(0, NBLK // GROUP, step, 0)
        l1, l2, l3 = ls[0], ls[1], ls[2]
        mx = jnp.maximum(jnp.maximum(l1, l2), l3)
        e1, e2, e3 = jnp.exp(l1 - mx), jnp.exp(l2 - mx), jnp.exp(l3 - mx)
        tot = e1 + e2 + e3
        o = (os[0] * e1 + os[1] * e2 + os[2] * e3) / tot
        ga = g_ref[...]
        o_ref[...] = o
        l_ref[...] = mx + jnp.log(tot)
        a_ref[...] = (o * (ga * _sig(ga))).astype(a_ref.dtype)

    out = pl.BlockSpec((S, 128), lambda hp: (0, hp))
    return pl.pallas_call(
        body, name="attn_fwd", grid=(8,),
        in_specs=[_colblk(0), _colblk(1), _colblk(2), _colblk(3)], out_specs=[out] * 3,
        out_shape=[jax.ShapeDtypeStruct((S, D), F32), jax.ShapeDtypeStruct((S, D), F32),
                   jax.ShapeDtypeStruct((S, 2 * D), BF16)],
        scratch_shapes=[pltpu.VMEM((3, S, 128), F32), pltpu.VMEM((3, S, 128), F32),
                        pltpu.VMEM((2, 2 * QB, 2 * QB), F32)],
        compiler_params=_cparams(("parallel",)),
    )(z, z, z, z)


def attn_bwd(z, d_cat, o, lse):
    def body(q_ref, k_ref, v_ref, g_ref, da_ref, o_ref, l_ref, dq_ref, dk_ref, dv_ref, dg_ref, do_s, pr_s, bias):
        _attn_bias(bias)
        m0 = lax.broadcasted_iota(jnp.int32, (1, 128), 1) < 64
        ga = g_ref[...]
        sg = _sig(ga)
        da = da_ref[...]
        ov = o_ref[...]
        do = da * (ga * sg)
        dg_ref[...] = da * ov * (sg * (1.0 + ga * (1.0 - sg)))
        do_s[...] = do
        pr_s[...] = do * ov
        dq_ref[...] = jnp.zeros_like(dq_ref)
        dk_ref[...] = jnp.zeros_like(dk_ref)
        dv_ref[...] = jnp.zeros_like(dv_ref)
        for d in PATTERNS:
            def load(idx, d=d):
                cur, prev, b = _attn_rows(idx, d)
                return (cur, prev), (q_ref[cur, :], k_ref[prev, :], k_ref[cur, :], v_ref[prev, :], v_ref[cur, :],
                                     do_s[cur, :], pr_s[cur, :], l_ref[cur, :], bias[jnp.minimum(b, 1)])

            def block(q, kp, kc, vp, vc, dof, prod, lp, bs):
                qq = _two_heads(q * 0.125, m0).astype(BF16)
                kf = jnp.concatenate([kp, kc], axis=0)
                k = kf.astype(BF16)
                v = jnp.concatenate([vp, vc], axis=0).astype(BF16)
                dd = _two_heads(dof, m0).astype(BF16)
                lh = jnp.max(jnp.concatenate([jnp.where(m0, lp, -jnp.inf), jnp.where(m0, -jnp.inf, lp)], axis=0),
                             axis=-1, keepdims=True)
                delta = jnp.sum(_two_heads(prod, m0), axis=-1, keepdims=True)
                p = jnp.exp(_nt(qq, k) + bs - lh)
                ds = (p * (_nt(dd, v) - delta)).astype(BF16)
                dq = _nn(jnp.concatenate([ds[:QB], ds[QB:]], axis=1), _two_heads(kf, m0).astype(BF16))
                return dq * 0.125, _tn(ds, qq), _tn(p.astype(BF16), dd)

            def step(i, carry):
                loaded = [load(i * GROUP + u) for u in range(GROUP)]
                done = [block(*vals) for _, vals in loaded]
                for ((cur, prev), _), (dq, dk, dv) in zip(loaded, done):
                    dq_ref[cur, :] = dq_ref[cur, :] + dq
                    dk_ref[prev, :] = dk_ref[prev, :] + dk[:QB]
                    dv_ref[prev, :] = dv_ref[prev, :] + dv[:QB]
                    dk_ref[cur, :] = dk_ref[cur, :] + dk[QB:]
                    dv_ref[cur, :] = dv_ref[cur, :] + dv[QB:]
                return carry
            lax.fori_loop(0, NBLK // GROUP, step, 0)

    blk = pl.BlockSpec((S, 128), lambda hp: (0, hp))
    return pl.pallas_call(
        body, name="attn_bwd", grid=(8,),
        in_specs=[_colblk(0), _colblk(1), _colblk(2), _colblk(3), blk, blk, blk], out_specs=[blk] * 4,
        out_shape=[jax.ShapeDtypeStruct((S, D), F32)] * 4,
        scratch_shapes=[pltpu.VMEM((S, 128), F32), pltpu.VMEM((S, 128), F32), pltpu.VMEM((2, 2 * QB, 2 * QB), F32)],
        compiler_params=_cparams(("parallel",)),
    )(z, z, z, z, d_cat, o, lse)


def assemble_dz_even(parts):
    def body(*refs):
        o_ref = refs[-1]
        for j in range(6):
            o_ref[:, j * D:(j + 1) * D] = refs[j][...].astype(o_ref.dtype)
    tr = 256
    blk = pl.BlockSpec((tr, D), lambda i: (i, 0))
    return pl.pallas_call(
        body, name="assemble_dz_even", grid=(S // tr,), in_specs=[blk] * 6,
        out_specs=pl.BlockSpec((tr, 6 * D), lambda i: (i, 0)),
        out_shape=jax.ShapeDtypeStruct((S, 6 * D), BF16),
        compiler_params=_cparams(("parallel",)),
    )(*parts)


def _pool_window(g):
    return jnp.where(g == 0, 2.0, jnp.where(g == 1, 4.0, jnp.where(g == 2, 8.0, 16.0)))


def _pool_sel(g, levels):
    return jnp.where(g == 0, levels[0], jnp.where(g == 1, levels[1], jnp.where(g == 2, levels[2], levels[3])))


def _pool_fwd_math(v, g):
    t = lax.broadcasted_iota(jnp.int32, (S, 1), 0)
    s = v
    levels = []
    for k in (1, 2, 4, 8):
        s = s + jnp.where(t >= k, pltpu.roll(s, k, 0), 0.0)
        levels.append(s)
    cnt = jnp.minimum((t + 1).astype(F32), _pool_window(g))
    return _pool_sel(g, levels) / cnt - v, cnt


def pool_fwd(z, pw, ps, cat):
    def body(v_ref, g_ref, pw_ref, ps_ref, cat_ref, o_ref):
        g = pl.program_id(0)
        pooled, _ = _pool_fwd_math(v_ref[...], g)
        mixed = _nn(pooled.astype(BF16), pw_ref[...].astype(BF16))
        gb = g_ref[...]
        o_ref[...] = (mixed * ps_ref[...] * (gb * _sig(gb))).astype(o_ref.dtype)

    return pl.pallas_call(
        body, name="pool_fwd", grid=(4,),
        in_specs=[pl.BlockSpec((S, 256), lambda g: (0, 16 + g)),
                  pl.BlockSpec((S, 256), lambda g: (0, 20 + g)),
                  pl.BlockSpec((None, 256, 256), lambda g: (g, 0, 0)),
                  pl.BlockSpec((1, 256), lambda g: (0, g)), pl.BlockSpec(memory_space=pl.ANY)],
        out_specs=pl.BlockSpec((S, 256), lambda g: (0, 4 + g)),
        out_shape=jax.ShapeDtypeStruct((S, 2 * D), BF16),
        input_output_aliases={4: 0},
        compiler_params=_cparams(("parallel",)),
    )(z, z, pw, ps, cat)


def pool_bwd(z, d_cat, pw, ps):
    def body(v_ref, g_ref, d_ref, pw_ref, ps_ref, dv_ref, dg_ref, dpw_ref, dps_ref):
        g = pl.program_id(0)
        v = v_ref[...]
        pooled, cnt = _pool_fwd_math(v, g)
        pwb = pw_ref[...].astype(BF16)
        pb = pooled.astype(BF16)
        mixed = _nn(pb, pwb)
        gb = g_ref[...]
        sg = _sig(gb)
        dout = d_ref[...]
        sc = ps_ref[...]
        dg_ref[...] = dout * mixed * sc * (sg * (1.0 + gb * (1.0 - sg)))
        dms = dout * (gb * sg)
        dps_ref[...] = jnp.sum(dms * mixed, axis=0, keepdims=True)
        dmx = (dms * sc).astype(BF16)
        dpw_ref[...] = _tn(pb, dmx)
        dpooled = _nt(dmx, pwb)
        t = lax.broadcasted_iota(jnp.int32, (S, 1), 0)
        s = dpooled / cnt
        levels = []
        for k in (1, 2, 4, 8):
            s = s + jnp.where(t < S - k, pltpu.roll(s, S - k, 0), 0.0)
            levels.append(s)
        dv_ref[...] = _pool_sel(g, levels) - dpooled

    return pl.pallas_call(
        body, name="pool_bwd", grid=(4,),
        in_specs=[pl.BlockSpec((S, 256), lambda g: (0, 16 + g)),
                  pl.BlockSpec((S, 256), lambda g: (0, 20 + g)),
                  pl.BlockSpec((S, 256), lambda g: (0, 4 + g)),
                  pl.BlockSpec((None, 256, 256), lambda g: (g, 0, 0)),
                  pl.BlockSpec((1, 256), lambda g: (0, g))],
        out_specs=[pl.BlockSpec((S, 256), lambda g: (0, g)),
                   pl.BlockSpec((S, 256), lambda g: (0, g)),
                   pl.BlockSpec((None, 256, 256), lambda g: (g, 0, 0)),
                   pl.BlockSpec((1, 256), lambda g: (0, g))],
        out_shape=[jax.ShapeDtypeStruct((S, D), F32), jax.ShapeDtypeStruct((S, D), F32),
                   jax.ShapeDtypeStruct((4, 256, 256), F32), jax.ShapeDtypeStruct((1, D), F32)],
        compiler_params=_cparams(("parallel",)),
    )(z, z, d_cat, pw, ps)


CH = 128


def _sgu_common(v, lng, lnb, w_ref):
    mu = jnp.mean(v, axis=-1, keepdims=True)
    vc = v - mu
    rs = lax.rsqrt(jnp.mean(vc * vc, axis=-1, keepdims=True) + EPS)
    xhat = vc * rs
    vn = (xhat * lng + lnb).astype(BF16)
    ri = lax.broadcasted_iota(jnp.int32, (CH, CH), 0)
    ci = lax.broadcasted_iota(jnp.int32, (CH, CH), 1)
    tril = ri >= ci
    ws = [jnp.where(tril, w_ref[g], 0.0).astype(BF16) for g in range(4)]
    return xhat, rs, vn, tril, ws


def _zspec(off):
    return pl.BlockSpec((CH, D), lambda c: (c, off))


def _full(shape):
    return pl.BlockSpec(shape, lambda c: (0,) * len(shape))


def sgu_fwd(z, lng, lnb, w, bfull):
    def body(u_ref, v_ref, g_ref, lng_ref, lnb_ref, w_ref, b_ref, o_ref):
        _, _, vn, _, ws = _sgu_common(v_ref[...], lng_ref[...], lnb_ref[...], w_ref)
        for g in range(4):
            sl = slice(g * 256, (g + 1) * 256)
            mixed = _nn(ws[g], vn[:, sl]) + b_ref[:, sl]
            gc = g_ref[:, sl]
            o_ref[:, sl] = (u_ref[:, sl] * mixed * (gc * _sig(gc))).astype(o_ref.dtype)

    return pl.pallas_call(
        body, name="sgu_fwd", grid=(S // CH,),
        in_specs=[_zspec(0), _zspec(1), _zspec(2), _full((1, D)), _full((1, D)),
                  _full((4, CH, CH)), _full((CH, D))],
        out_specs=pl.BlockSpec((CH, D), lambda c: (c, 0)),
        out_shape=jax.ShapeDtypeStruct((S, D), BF16),
        compiler_params=_cparams(("parallel",)),
    )(z, z, z, lng, lnb, w, bfull)


def sgu_bwd(z, d_cat, lng, lnb, w, bfull):
    def body(u_ref, v_ref, g_ref, d_ref, lng_ref, lnb_ref, w_ref, b_ref,
             du_ref, dv_ref, dg_ref, dw_ref, db_ref, dlg_ref, dlb_ref):
        @pl.when(pl.program_id(0) == 0)
        def _():
            dw_ref[...] = jnp.zeros_like(dw_ref)
            db_ref[...] = jnp.zeros_like(db_ref)
            dlg_ref[...] = jnp.zeros_like(dlg_ref)
            dlb_ref[...] = jnp.zeros_like(dlb_ref)

        lng = lng_ref[...]
        xhat, rs, vn, tril, ws = _sgu_common(v_ref[...], lng, lnb_ref[...], w_ref)
        lane = lax.broadcasted_iota(jnp.int32, (1, 128), 1)
        db = jnp.zeros((CH, 128), F32)
        dvn_parts = []
        for g in range(4):
            sl = slice(g * 256, (g + 1) * 256)
            mixed = _nn(ws[g], vn[:, sl]) + b_ref[:, sl]
            gc = g_ref[:, sl]
            sg = _sig(gc)
            u = u_ref[:, sl]
            dc = d_ref[:, sl]
            du_ref[:, sl] = dc * mixed * (gc * sg)
            dg_ref[:, sl] = dc * u * mixed * (sg * (1.0 + gc * (1.0 - sg)))
            dmx = dc * u * (gc * sg)
            db = db + jnp.where(lane == g, jnp.sum(dmx, axis=-1, keepdims=True), 0.0)
            dmb = dmx.astype(BF16)
            dw_ref[g] += jnp.where(tril, _nt(dmb, vn[:, sl]), 0.0)
            dvn_parts.append(_tn(ws[g], dmb))
        db_ref[...] += db
        dvn = jnp.concatenate(dvn_parts, axis=1)
        dlb_ref[...] += jnp.sum(dvn, axis=0, keepdims=True)
        dlg_ref[...] += jnp.sum(dvn * xhat, axis=0, keepdims=True)
        dxh = dvn * lng
        dv_ref[...] = rs * (dxh - jnp.mean(dxh, axis=-1, keepdims=True)
                            - xhat * jnp.mean(dxh * xhat, axis=-1, keepdims=True))

    row = pl.BlockSpec((CH, D), lambda c: (c, 0))
    return pl.pallas_call(
        body, name="sgu_bwd", grid=(S // CH,),
        in_specs=[_zspec(0), _zspec(1), _zspec(2), row, _full((1, D)), _full((1, D)),
                  _full((4, CH, CH)), _full((CH, D))],
        out_specs=[row, row, row, _full((4, CH, CH)), _full((CH, 128)), _full((1, D)), _full((1, D))],
        out_shape=[jax.ShapeDtypeStruct((S, D), F32)] * 3
        + [jax.ShapeDtypeStruct((4, CH, CH), F32), jax.ShapeDtypeStruct((CH, 128), F32),
           jax.ShapeDtypeStruct((1, D), F32), jax.ShapeDtypeStruct((1, D), F32)],
        compiler_params=_cparams(("arbitrary",)),
    )(z, z, z, d_cat, lng, lnb, w, bfull)


TB = 256


def _cmul(ar, ai, br, bi):
    return ar * br - ai * bi, ar * bi + ai * br


def _scan_consts(ar, ai, reverse):
    a2 = _cmul(ar, ai, ar, ai)
    a4 = _cmul(*a2, *a2)
    row = lax.broadcasted_iota(jnp.int32, (8, NS), 0)
    pr = jnp.zeros((8, NS), F32)
    pi = jnp.zeros((8, NS), F32)
    cr, ci = ar, ai
    for r in range(8):
        sel = row == (7 - r if reverse else r)
        pr = jnp.where(sel, cr, pr)
        pi = jnp.where(sel, ci, pi)
        cr, ci = _cmul(cr, ci, ar, ai)
    return ((ar, ai), a2, a4), (pr, pi), row


def scan_fwd(bu, abr, abi):
    def body(bu_ref, ar_ref, ai_ref, h_ref, car, cai):
        @pl.when(pl.program_id(0) == 0)
        def _():
            car[...] = jnp.zeros_like(car)
            cai[...] = jnp.zeros_like(cai)

        pows, (pr, pi), row = _scan_consts(ar_ref[...], ai_ref[...], False)

        def tile(t, carry):
            c_r, c_i = carry
            rows = pl.ds(pl.multiple_of(t * 8, 8), 8)
            xr = bu_ref[rows, 0:NS]
            xi = bu_ref[rows, NS:2 * NS]
            for k, (kr, ki) in zip((1, 2, 4), pows):
                sr = jnp.where(row >= k, pltpu.roll(xr, k, 0), 0.0)
                si = jnp.where(row >= k, pltpu.roll(xi, k, 0), 0.0)
                xr, xi = xr + kr * sr - ki * si, xi + kr * si + ki * sr
            xr, xi = xr + pr * c_r - pi * c_i, xi + pr * c_i + pi * c_r
            h_ref[rows, 0:NS] = xr
            h_ref[rows, NS:2 * NS] = xi
            return (jnp.broadcast_to(xr[7:8, :], (8, NS)), jnp.broadcast_to(xi[7:8, :], (8, NS)))

        c_r, c_i = lax.fori_loop(0, TB // 8, tile, (car[...], cai[...]))
        car[...] = c_r
        cai[...] = c_i

    return pl.pallas_call(
        body, name="s5_scan_fwd", grid=(S // TB,),
        in_specs=[pl.BlockSpec((TB, 2 * NS), lambda i: (i, 0)),
                  pl.BlockSpec((1, NS), lambda i: (0, 0)), pl.BlockSpec((1, NS), lambda i: (0, 0))],
        out_specs=pl.BlockSpec((TB, 2 * NS), lambda i: (i, 0)),
        out_shape=jax.ShapeDtypeStruct((S, 2 * NS), F32),
        scratch_shapes=[pltpu.VMEM((8, NS), F32), pltpu.VMEM((8, NS), F32)],
        compiler_params=_cparams(("arbitrary",)),
    )(bu, abr, abi)


def scan_bwd(eta, h, abr, abi):
    nt = S // TB

    def body(e_ref, h_ref, ar_ref, ai_ref, l_ref, da_ref, car, cai):
        @pl.when(pl.program_id(0) == 0)
        def _():
            car[...] = jnp.zeros_like(car)
            cai[...] = jnp.zeros_like(cai)
            da_ref[...] = jnp.zeros_like(da_ref)

        pows, (pr, pi), row = _scan_consts(ar_ref[...], -ai_ref[...], True)

        def tile(tt, carry):
            c_r, c_i, acr, aci = carry
            t = TB // 8 - 1 - tt
            rows = pl.ds(pl.multiple_of(t * 8, 8), 8)
            xr = e_ref[rows, 0:NS]
            xi = e_ref[rows, NS:2 * NS]
            for k, (kr, ki) in zip((1, 2, 4), pows):
                sr = jnp.where(row < 8 - k, pltpu.roll(xr, 8 - k, 0), 0.0)
                si = jnp.where(row < 8 - k, pltpu.roll(xi, 8 - k, 0), 0.0)
                xr, xi = xr + kr * sr - ki * si, xi + kr * si + ki * sr
            xr, xi = xr + pr * c_r - pi * c_i, xi + pr * c_i + pi * c_r
            l_ref[rows, 0:NS] = xr
            l_ref[rows, NS:2 * NS] = xi
            nr = jnp.where(row < 7, pltpu.roll(xr, 7, 0), c_r)
            ni = jnp.where(row < 7, pltpu.roll(xi, 7, 0), c_i)
            hr = h_ref[rows, 0:NS]
            hi = h_ref[rows, NS:2 * NS]
            acr = acr + hr * nr + hi * ni
            aci = aci + hr * ni - hi * nr
            return (jnp.broadcast_to(xr[0:1, :], (8, NS)), jnp.broadcast_to(xi[0:1, :], (8, NS)), acr, aci)

        zero = jnp.zeros((8, NS), F32)
        c_r, c_i, acr, aci = lax.fori_loop(0, TB // 8, tile, (car[...], cai[...], zero, zero))
        car[...] = c_r
        cai[...] = c_i
        da_ref[:, 0:NS] += acr
        da_ref[:, NS:2 * NS] += aci

    rev = pl.BlockSpec((TB, 2 * NS), lambda i: (nt - 1 - i, 0))
    return pl.pallas_call(
        body, name="s5_scan_bwd", grid=(nt,),
        in_specs=[rev, rev, pl.BlockSpec((1, NS), lambda i: (0, 0)), pl.BlockSpec((1, NS), lambda i: (0, 0))],
        out_specs=[rev, pl.BlockSpec((8, 2 * NS), lambda i: (0, 0))],
        out_shape=[jax.ShapeDtypeStruct((S, 2 * NS), F32), jax.ShapeDtypeStruct((8, 2 * NS), F32)],
        scratch_shapes=[pltpu.VMEM((8, NS), F32), pltpu.VMEM((8, NS), F32)],
        compiler_params=_cparams(("arbitrary",)),
    )(eta, h, abr, abi)


GC = 0.7978845608028654
GA = 0.044715


def s5_post(hc, z, dskip):
    def fn(hv, xd, dv):
        y = hv + dv * xd
        return y, 0.5 * y * (1.0 + jnp.tanh(GC * (y + GA * y * y * y)))
    return rw(fn, [(hc, 0, 512), (z, 3072, 512)], [(512, F32), (512, BF16)], "s5_post", S, consts=[dskip])


def s5_post_bwd(dyg, ypre, z, dskip):
    def fn(dy, y, xd, dv):
        th = jnp.tanh(GC * (y + GA * y * y * y))
        dg = 0.5 * (1.0 + th) + 0.5 * y * (1.0 - th * th) * GC * (1.0 + 3.0 * GA * y * y)
        dyp = dy * dg
        return dyp, dyp * dv, jnp.sum(dyp * xd, axis=0, keepdims=True)
    return rw(fn, [(dyg, 0, 512), (ypre, 0, 512), (z, 3072, 512)], [(512, BF16), (512, F32)],
              "s5_post_bwd", S, consts=[dskip], accs=[(1, 512)])


def glu_fwd(t, z, c_out):
    def fn(t1, t2, gd, co):
        return (jnp.concatenate([co, (t1 * _sig(t2) * (gd * _sig(gd))).astype(BF16)], axis=1),)
    return rw(fn, [(t, 0, 512), (t, 512, 512), (z, 3584, 512), (c_out, 0, D)], [(D + 512, BF16)], "glu_fwd", S)[0]


def glu_bwd(t, z, d_cat):
    def fn(t1, t2, gd, dd):
        s2, sg = _sig(t2), _sig(gd)
        sl = gd * sg
        return (jnp.concatenate([dd * s2 * sl, dd * t1 * s2 * (1.0 - s2) * sl], axis=1),
                dd * t1 * s2 * (sg * (1.0 + gd * (1.0 - sg))))
    return rw(fn, [(t, 0, 512), (t, 512, 512), (z, 3584, 512), (d_cat, 1024, 512)],
              [(D, BF16), (512, F32)], "glu_bwd", S)


def assemble_dz_odd(du, dv, dgc, dxd, dgd):
    def body(a, b, c, d, e, o_ref):
        o_ref[:, 0:D] = a[...].astype(BF16)
        o_ref[:, D:2 * D] = b[...].astype(BF16)
        o_ref[:, 2 * D:3 * D] = c[...].astype(BF16)
        o_ref[:, 3 * D:3 * D + 512] = d[...].astype(BF16)
        o_ref[:, 3 * D + 512:4 * D] = e[...].astype(BF16)
    tr = 256
    blk = pl.BlockSpec((tr, D), lambda i: (i, 0))
    half = pl.BlockSpec((tr, 512), lambda i: (i, 0))
    return pl.pallas_call(
        body, name="assemble_dz_odd", grid=(S // tr,), in_specs=[blk, blk, blk, half, half],
        out_specs=pl.BlockSpec((tr, 4 * D), lambda i: (i, 0)),
        out_shape=jax.ShapeDtypeStruct((S, 4 * D), BF16),
        compiler_params=_cparams(("parallel",)),
    )(du, dv, dgc, dxd, dgd)


TQ = 256


def _xattn_probs(qh, kh):
    s = _nt(qh, kh) * 0.0625
    p = jnp.exp(s - jnp.max(s, axis=-1, keepdims=True))
    return p / jnp.sum(p, axis=-1, keepdims=True)


def xattn_fwd(q, kv):
    def body(q_ref, kv_ref, o_ref):
        for h in range(4):
            sl = slice(h * 256, (h + 1) * 256)
            p = _xattn_probs(q_ref[:, sl].astype(BF16), kv_ref[:, sl].astype(BF16))
            vh = kv_ref[:, D + h * 256:D + (h + 1) * 256].astype(BF16)
            o_ref[:, sl] = _nn(p.astype(BF16), vh).astype(o_ref.dtype)

    return pl.pallas_call(
        body, name="xattn_fwd", grid=(S // TQ,),
        in_specs=[pl.BlockSpec((TQ, D), lambda i: (i, 0)), pl.BlockSpec((MEM, 2 * D), lambda i: (0, 0))],
        out_specs=pl.BlockSpec((TQ, D), lambda i: (i, 0)),
        out_shape=jax.ShapeDtypeStruct((S, D), BF16),
        compiler_params=_cparams(("parallel",)),
    )(q, kv)


def xattn_bwd(q, kv, d_o):
    def body(q_ref, kv_ref, do_ref, dq_ref, dkv_ref):
        @pl.when(pl.program_id(0) == 0)
        def _():
            dkv_ref[...] = jnp.zeros_like(dkv_ref)

        for h in range(4):
            sl = slice(h * 256, (h + 1) * 256)
            vs = slice(D + h * 256, D + (h + 1) * 256)
            qh = q_ref[:, sl].astype(BF16)
            kh = kv_ref[:, sl].astype(BF16)
            vh = kv_ref[:, vs].astype(BF16)
            doh = do_ref[:, sl].astype(BF16)
            p = _xattn_probs(qh, kh)
            dp = _nt(doh, vh)
            ds = (p * (dp - jnp.sum(p * dp, axis=-1, keepdims=True)) * 0.0625).astype(BF16)
            dq_ref[:, sl] = _nn(ds, kh).astype(dq_ref.dtype)
            dkv_ref[:, sl] += _tn(ds, qh)
            dkv_ref[:, vs] += _tn(p.astype(BF16), doh)

    return pl.pallas_call(
        body, name="xattn_bwd", grid=(S // TQ,),
        in_specs=[pl.BlockSpec((TQ, D), lambda i: (i, 0)), pl.BlockSpec((MEM, 2 * D), lambda i: (0, 0)),
                  pl.BlockSpec((TQ, D), lambda i: (i, 0))],
        out_specs=[pl.BlockSpec((TQ, D), lambda i: (i, 0)), pl.BlockSpec((MEM, 2 * D), lambda i: (0, 0))],
        out_shape=[jax.ShapeDtypeStruct((S, D), BF16), jax.ShapeDtypeStruct((MEM, 2 * D), F32)],
        compiler_params=_cparams(("arbitrary",)),
    )(q, kv, d_o)


def _s5_disc(a_re, a_im, log_dt, b_re, b_im):
    dt = jnp.exp(log_dt)[:, None]
    mag = jnp.exp(dt * a_re)
    abr = mag * jnp.cos(dt * a_im)
    abi = mag * jnp.sin(dt * a_im)
    nr, ni = abr - 1.0, abi
    inv = 1.0 / (a_re * a_re + a_im * a_im)
    cr = (nr * a_re + ni * a_im) * inv
    ci = (ni * a_re - nr * a_im) * inv
    bbr = cr[..., None] * b_re - ci[..., None] * b_im
    bbi = cr[..., None] * b_im + ci[..., None] * b_re
    return abr, abi, bbr, bbi


def _blockdiag(t):
    g, a, b = t.shape
    eye = jnp.eye(g, dtype=t.dtype)
    return (eye[:, None, :, None] * t[:, :, None, :]).reshape(g * a, g * b)


def _blocks(mat, a, b):
    return jnp.einsum("gagb->gab", mat.reshape(NG, a, NG, b))


def _fwd_even(i, x, P, W):
    hn = rms_fwd(x, P["norm_ab"][i:i + 1], "rms_ab_fwd")
    z = mm(m2(hn), W["w_in"], "nn", "in_ab")
    o, lse, cat = attn_fwd(z)
    if "more" in W:
        W.update(W.pop("more")(cat))
    cat = pool_fwd(z, W["pool_w"], P["pool_scale"][i:i + 1], cat)
    x_mid = mm(m2(cat), W["w_out"], "nn", "out_ab", add=m2(x))
    return x_mid, dict(x=x, hn=hn, z=z, o=o, lse=lse, cat=cat)


def _bwd_even(i, dx_mid, sv, P, W, G, GW):
    z = sv["z"]
    d_cat = mm(m2(dx_mid), W["w_out"], "nt", "out_ab_dx")
    GW["w_out"] = mm(m2(sv["cat"]), m2(dx_mid), "tn", "out_ab_dw").reshape(4, 512, D)
    dq, dk, dv, dga = attn_bwd(z, d_cat, sv["o"], sv["lse"])
    dvb, dgb, dpw, dps = pool_bwd(z, d_cat, W["pool_w"], P["pool_scale"][i:i + 1])
    GW["pool_w"] = dpw.reshape(4, 4, 64, 256).transpose(1, 0, 2, 3).reshape(4, 256, 256)
    G["pool_scale"][i] = dps[0]
    d_z = assemble_dz_even((dq, dk, dv, dga, dvb, dgb))
    d_hn = mm(m2(d_z), W["w_in"], "nt", "in_ab_dx")
    GW["w_in"] = mm(m2(sv["hn"]), m2(d_z), "tn", "in_ab_dw", out=outcs(D, 1536))
    return d_hn, P["norm_ab"][i:i + 1], "norm_ab", "rms_ab_bwd"


def _fwd_odd(i, x, P, W):
    hn = rms_fwd(x, P["norm_cd"][i:i + 1], "rms_cd_fwd")
    z = mm(m2(hn), W["w_in"], "nn", "in_cd")
    bfull = jnp.repeat(P["sgu_b"][i].T, 256, axis=1)
    c_out = sgu_fwd(z, P["sgu_ln_g"][i:i + 1], P["sgu_ln_b"][i:i + 1], P["sgu_w"][i], bfull)
    disc, disc_vjp = jax.vjp(_s5_disc, P["s5_a_re"][i], P["s5_a_im"][i], P["s5_log_dt"][i],
                             P["s5_b_re"][i], P["s5_b_im"][i])
    abr, abi, bbr, bbi = disc
    bbd = jnp.concatenate([_blockdiag(bbr.transpose(0, 2, 1)), _blockdiag(bbi.transpose(0, 2, 1))], axis=1)
    cbd = jnp.concatenate([_blockdiag(P["s5_c_re"][i].transpose(0, 2, 1)),
                           -_blockdiag(P["s5_c_im"][i].transpose(0, 2, 1))], axis=0)
    abr, abi = abr.reshape(1, NS), abi.reshape(1, NS)
    bu = mm(m2(z, 3072, 512), m2(bbd), "nn", "s5_bu")
    h = scan_fwd(bu, abr, abi)
    hc = mm(m2(h), m2(cbd), "nn", "s5_hc")
    dskip = P["s5_d"][i:i + 1]
    ypre, yg = s5_post(hc, z, dskip)
    w12 = W["w12"]
    t = mm(m2(yg), m2(w12), "nn", "glu_t")
    cat = glu_fwd(t, z, c_out)
    x_mid = mm(m2(cat), W["w_out"], "nn", "out_cd", add=m2(x))
    return x_mid, dict(x=x, hn=hn, z=z, bfull=bfull, disc_vjp=disc_vjp, bbd=bbd, cbd=cbd, abr=abr,
                       abi=abi, h=h, ypre=ypre, yg=yg, w12=w12, t=t, cat=cat, dskip=dskip)


def _bwd_odd(i, dx_mid, sv, P, W, G, GW):
    z = sv["z"]
    d_cat = mm(m2(dx_mid), W["w_out"], "nt", "out_cd_dx")
    GW["w_out"] = mm(m2(sv["cat"]), m2(dx_mid), "tn", "out_cd_dw").reshape(4, 384, D)
    du, dv, dgc, dws, dbs, dlg, dlb = sgu_bwd(z, d_cat, P["sgu_ln_g"][i:i + 1], P["sgu_ln_b"][i:i + 1],
                                               P["sgu_w"][i], sv["bfull"])
    G["sgu_w"][i], G["sgu_b"][i] = dws, dbs[:, :4].T
    G["sgu_ln_g"][i], G["sgu_ln_b"][i] = dlg[0], dlb[0]
    dt, dgd = glu_bwd(sv["t"], z, d_cat)
    gw12 = mm(m2(sv["yg"]), m2(dt), "tn", "glu_dw")
    GW["glu_w1"] = gw12[:, :512].reshape(4, 128, 512)
    GW["glu_w2"] = gw12[:, 512:].reshape(4, 128, 512)
    dyg = mm(m2(dt), m2(sv["w12"]), "nt", "glu_dx")
    dypre, dxd1, dd = s5_post_bwd(dyg, sv["ypre"], z, sv["dskip"])
    G["s5_d"][i] = dd[0]
    gcbd = mm(m2(sv["h"]), m2(dypre), "tn", "s5_dc")
    G["s5_c_re"][i] = _blocks(gcbd[:NS], NP, NH).transpose(0, 2, 1)
    G["s5_c_im"][i] = -_blocks(gcbd[NS:], NP, NH).transpose(0, 2, 1)
    eta = mm(m2(dypre), m2(sv["cbd"]), "nt", "s5_eta")
    lam, dacc = scan_bwd(eta, sv["h"], sv["abr"], sv["abi"])
    gbbd = mm(m2(z, 3072, 512), m2(lam), "tn", "s5_db")
    dxd = mm(m2(lam), m2(sv["bbd"]), "nt", "s5_dx", add=m2(dxd1))
    dacc = jnp.sum(dacc, axis=0)
    d_bbr = _blocks(gbbd[:, :NS], NH, NP).transpose(0, 2, 1)
    d_bbi = _blocks(gbbd[:, NS:], NH, NP).transpose(0, 2, 1)
    (G["s5_a_re"][i], G["s5_a_im"][i], G["s5_log_dt"][i], G["s5_b_re"][i], G["s5_b_im"][i]) = sv["disc_vjp"](
        (dacc[:NS].reshape(NG, NP), dacc[NS:].reshape(NG, NP), d_bbr, d_bbi))
    d_z = assemble_dz_odd(du, dv, dgc, dxd, dgd)
    d_hn = mm(m2(d_z), W["w_in"], "nt", "in_cd_dx")
    GW["w_in"] = mm(m2(sv["hn"]), m2(d_z), "tn", "in_cd_dw", out=outcs(D, 1024))
    return d_hn, P["norm_cd"][i:i + 1], "norm_cd", "rms_cd_bwd"


def _fwd_x(l, x, mem_n, P, W):
    hx = rms_fwd(x, P["norm_x"][l:l + 1], "rms_x_fwd")
    q = mm(m2(hx), W["w_xq"], "nn", "xq", out_dtype=BF16)
    kv = mm(m2(mem_n), W["w_xkv"], "nn", "xkv", out_dtype=BF16)
    ox = xattn_fwd(q, kv)
    x_out = mm(m2(ox), W["w_xo"], "nn", "xo", add=m2(x))
    return x_out, dict(x=x, hx=hx, q=q, kv=kv, ox=ox)


def _bwd_x(l, dx_out, sv, mem_n, d_memn, P, W, G, GW):
    d_ox = mm(m2(dx_out), W["w_xo"], "nt", "xo_dx", out_dtype=BF16)
    GW["w_xo"] = mm(m2(sv["ox"]), m2(dx_out), "tn", "xo_dw").reshape(4, 256, D)
    dq, dkv = xattn_bwd(sv["q"], sv["kv"], d_ox)
    GW["w_xq"] = mm(m2(sv["hx"]), m2(dq), "tn", "xq_dw").reshape(4, 256, D)
    d_hx = mm(m2(dq), W["w_xq"], "nt", "xq_dx")
    GW["w_xkv"] = mm(m2(mem_n), m2(dkv), "tn", "xkv_dw", out=outcs(D, 512))
    d_memn = mm(m2(dkv), W["w_xkv"], "nt", "xkv_dx", add=None if d_memn is None else m2(d_memn))
    dx, dg = rms_bwd(sv["x"], d_hx, dx_out, P["norm_x"][l:l + 1], "rms_x_bwd")
    G["norm_x"][l] = dg[0]
    return dx, d_memn


SMALL_LAYERS = (("norm_ab", 2), ("pool_scale", 2), ("norm_cd", 2), ("sgu_ln_g", 2), ("sgu_ln_b", 2), ("sgu_w", 2),
                ("sgu_b", 2), ("s5_a_re", 2), ("s5_a_im", 2), ("s5_log_dt", 2), ("s5_b_re", 2), ("s5_b_im", 2),
                ("s5_c_re", 2), ("s5_c_im", 2), ("s5_d", 2), ("norm_x", 4))


def local_step(x, mem, tgt, P, weights_of, grads_done):
    G = {k: [None] * n for k, n in SMALL_LAYERS}
    mem_g = P["mem_norm"].reshape(1, D)
    mem_n = rms_fwd(mem, mem_g, "rms_mem_fwd")
    saved = []
    for layer in range(4):
        i = layer // 2
        W = weights_of(layer, x)
        x, sv_m = (_fwd_even if layer % 2 == 0 else _fwd_odd)(i, x, P, W)
        x, sv_x = _fwd_x(layer, x, mem_n, P, W)
        saved.append((sv_m, sv_x, W))
    dx, loss, dgf = final_loss(x, tgt, P["final_norm"].reshape(1, D))
    G["final_norm"] = dgf[0]
    d_memn = None
    for layer in reversed(range(4)):
        i = layer // 2
        sv_m, sv_x, W = saved[layer]
        GW = {}
        dx_mid, d_memn = _bwd_x(layer, dx, sv_x, mem_n, d_memn, P, W, G, GW)
        d_hn, g, key, name = (_bwd_even if layer % 2 == 0 else _bwd_odd)(i, dx_mid, sv_m, P, W, G, GW)
        token = grads_done(layer, GW)
        if token is not None:
            g = g + token
        dx, dg = rms_bwd(sv_m["x"], d_hn, dx_mid, g, name)
        G[key][i] = dg[0]
    _, dgm = rms_bwd(mem, d_memn, d_memn, mem_g, "rms_mem_bwd")
    G["mem_norm"] = dgm[0]
    return loss, dx, G


ANY = pl.BlockSpec(memory_space=pl.ANY)


def _place():
    x, y, c = lax.axis_index("x"), lax.axis_index("y"), lax.axis_index("c")
    chips = [(1 - x, y), (x, 1 - y), (1 - x, 1 - y)]
    return x, y, c, 2 * x + y, (x, y, 1 - c), chips


def _remote(src, dst, send, recv, k, dev):
    return pltpu.make_async_remote_copy(src_ref=src, dst_ref=dst, send_sem=send.at[k], recv_sem=recv.at[k],
                                        device_id=dev, device_id_type=MESHID)


HBM = pl.BlockSpec(memory_space=pltpu.HBM)
SEM = pl.BlockSpec(memory_space=pltpu.SEMAPHORE)
EFFECT = pltpu.SideEffectType.DATAFLOW_SIDE_EFFECTING


def _hbm(t):
    return pltpu.with_memory_space_constraint(t, pltpu.HBM)


def allgather_sync(shards):
    n = len(shards)

    def body(*refs):
        ins, outs = refs[:n], refs[n:2 * n]
        token, send, recv = refs[2 * n:]
        x, y, c, jme, sib, chips = _place()
        first, passed = [], []
        for a in range(n):
            cp = _remote(ins[a], outs[a].at[jme], send, recv, a * 7 + 6, sib)
            cp.start()
            first.append(cp)
            for k, chip in enumerate(chips):
                cp = _remote(ins[a].at[c], outs[a].at[jme, c], send, recv, a * 7 + k, (*chip, c))
                cp.start()
                first.append(cp)
        for a in range(n):
            for k, chip in enumerate(chips):
                piece = outs[a].at[2 * chip[0] + chip[1], c]
                _remote(piece, piece, send, recv, a * 7 + k, (*chip, c)).wait_recv()
                fw = _remote(piece, piece, send, recv, a * 7 + 3 + k, sib)
                fw.start()
                passed.append(fw)
        for a in range(n):
            own = outs[a].at[jme]
            _remote(own, own, send, recv, a * 7 + 6, sib).wait_recv()
            for k, chip in enumerate(chips):
                piece = outs[a].at[2 * chip[0] + chip[1], 1 - c]
                _remote(piece, piece, send, recv, a * 7 + 3 + k, sib).wait_recv()
        for cp in first + passed:
            cp.wait_send()
        token[...] = jnp.zeros_like(token)

    res = pl.pallas_call(
        body, name="allgather_sync", in_specs=[ANY] * n,
        out_specs=[ANY] * n + [pl.BlockSpec(memory_space=pltpu.VMEM)],
        out_shape=[jax.ShapeDtypeStruct((4,) + s.shape, s.dtype) for s in shards] + [jax.ShapeDtypeStruct((8, 128), F32)],
        scratch_shapes=[pltpu.SemaphoreType.DMA((7 * n,)), pltpu.SemaphoreType.DMA((7 * n,))],
    )(*shards)
    return list(res[:n]), res[n]


def _gather_copies(ins, lands, send, recv):
    x, y, c, jme, sib, chips = _place()
    devs = [(*chip, c) for chip in chips] + [sib]
    return [_remote(ins[a], lands[a].at[jme], send, recv, a * 4 + k, dev)
            for a in range(len(ins)) for k, dev in enumerate(devs)]


def allgather_start(shards, after, name):
    n, na = len(shards), len(after)

    def body(*refs):
        ins, lands = refs[:n], refs[n:2 * n]
        send, recv = refs[2 * n + na], refs[2 * n + na + 1]
        token = refs[-1]
        for cp in _gather_copies(ins, lands, send, recv):
            cp.start()
        token[...] = jnp.zeros_like(token)

    res = pl.pallas_call(
        body, name=name,
        out_shape=(pltpu.SemaphoreType.DMA((4 * n,)), pltpu.SemaphoreType.DMA((4 * n,)),
                   *[pltpu.HBM(s.shape, s.dtype) for s in shards],
                   *[pltpu.HBM((4,) + s.shape, s.dtype) for s in shards],
                   jax.ShapeDtypeStruct((8, 128), F32)),
        in_specs=[HBM] * (2 * n) + [ANY] * na,
        out_specs=(SEM, SEM, *[HBM] * (2 * n), pl.BlockSpec(memory_space=pltpu.VMEM)),
        input_output_aliases={a: 2 + a for a in range(2 * n)},
        compiler_params=pltpu.CompilerParams(has_side_effects=EFFECT),
    )(*[_hbm(s) for s in shards], *[_hbm(lax.empty((4,) + s.shape, s.dtype)) for s in shards], *after)
    return res[0], res[1], list(res[2:2 + n]), list(res[2 + n:2 + 2 * n]), res[-1]


def allgather_wait(send, recv, shards, lands, after, name):
    n = len(shards)

    def body(*refs):
        ins, zones = refs[:n], refs[n:2 * n]
        send_r, recv_r = refs[2 * n], refs[2 * n + 1]
        x, y, c, jme, sib, chips = _place()
        slots = [2 * chip[0] + chip[1] for chip in chips] + [jme]
        for a in range(n):
            for k, slot in enumerate(slots):
                cp = _remote(ins[a], zones[a].at[slot], send_r, recv_r, a * 4 + k, sib)
                cp.wait_send()
                cp.wait_recv()

    res = pl.pallas_call(
        body, name=name,
        out_shape=tuple(pltpu.HBM(t.shape, t.dtype) for t in list(shards) + list(lands)),
        in_specs=[HBM] * (2 * n) + [SEM, SEM, ANY], out_specs=tuple([HBM] * (2 * n)),
        input_output_aliases={a: a for a in range(2 * n)},
        compiler_params=pltpu.CompilerParams(has_side_effects=EFFECT),
    )(*shards, *lands, send, recv, after)
    return list(res[n:])


def allgather_small(slab):
    def body(in_ref, out_ref, send, recv, lsem):
        x, y, c, jme, sib, chips = _place()
        loc = pltpu.make_async_copy(in_ref, out_ref.at[jme], lsem.at[0])
        loc.start()
        cps = [_remote(in_ref, out_ref.at[jme], send, recv, k, (*chip, c)) for k, chip in enumerate(chips)]
        for cp in cps:
            cp.start()
        for k, chip in enumerate(chips):
            piece = out_ref.at[2 * chip[0] + chip[1]]
            _remote(piece, piece, send, recv, k, (*chip, c)).wait_recv()
        for cp in cps:
            cp.wait_send()
        loc.wait()

    return pl.pallas_call(
        body, name="allgather_small", in_specs=[ANY], out_specs=ANY,
        out_shape=jax.ShapeDtypeStruct((4,) + slab.shape, slab.dtype),
        scratch_shapes=[pltpu.SemaphoreType.DMA((3,)), pltpu.SemaphoreType.DMA((3,)), pltpu.SemaphoreType.DMA((1,))],
    )(slab)


def allreduce_small(v):
    def body(v_ref, o_ref, r0, r1, r2, send, recv):
        x, y, c, jme, sib, chips = _place()
        peers = [sib, (1 - x, y, c), (x, 1 - y, c)]
        o_ref[...] = v_ref[...]
        for k, buf in enumerate((r0, r1, r2)):
            cp = _remote(o_ref, buf, send, recv, k, peers[k])
            cp.start()
            cp.wait()
            o_ref[...] = o_ref[...] + buf[...]

    vm = pl.BlockSpec(memory_space=pltpu.VMEM)
    return pl.pallas_call(
        body, name="allreduce_small", in_specs=[vm], out_specs=vm,
        out_shape=jax.ShapeDtypeStruct(v.shape, v.dtype),
        scratch_shapes=[pltpu.VMEM(v.shape, v.dtype)] * 3 + [pltpu.SemaphoreType.DMA((3,)), pltpu.SemaphoreType.DMA((3,))],
        compiler_params=pltpu.CompilerParams(vmem_limit_bytes=VMEM_LIMIT),
    )(v)


def rs_pair_exchange(gs):
    n = len(gs)

    def body(*refs):
        ins, outs = refs[:n], refs[n:2 * n]
        send, recv = refs[2 * n:]
        x, y, c, jme, sib, chips = _place()
        cps = [_remote(ins[a].at[:, 1 - c], outs[a], send, recv, a, sib) for a in range(n)]
        for cp in cps:
            cp.start()
        for cp in cps:
            cp.wait()

    return pl.pallas_call(
        body, name="rs_pair_exchange", in_specs=[ANY] * n, out_specs=[ANY] * n,
        out_shape=[jax.ShapeDtypeStruct((4,) + g.shape[2:], F32) for g in gs],
        scratch_shapes=[pltpu.SemaphoreType.DMA((n,)), pltpu.SemaphoreType.DMA((n,))],
    )(*gs)


def rs_pair_sum(g4, got, cidx):
    _, _, rh, cols = g4.shape
    tr = rh if rh <= 256 else 256

    def body(c_ref, a_ref, b_ref, o_ref):
        o_ref[...] = (a_ref[...] + b_ref[...]).astype(o_ref.dtype)

    return pl.pallas_call(
        body, name="rs_pair_sum",
        grid_spec=pltpu.PrefetchScalarGridSpec(
            num_scalar_prefetch=1, grid=(4, rh // tr),
            in_specs=[pl.BlockSpec((None, None, tr, cols), lambda j, t, cr: (j, cr[0], t, 0)),
                      pl.BlockSpec((None, tr, cols), lambda j, t, cr: (j, t, 0))],
            out_specs=pl.BlockSpec((None, tr, cols), lambda j, t, cr: (j, t, 0))),
        out_shape=jax.ShapeDtypeStruct((4, rh, cols), BF16),
        compiler_params=_cparams(("parallel", "parallel")),
    )(cidx, g4, got)


def _chip_copies(ps, lands, send, recv):
    x, y, c, jme, sib, chips = _place()
    return [_remote(ps[a].at[2 * chip[0] + chip[1]], lands[a].at[jme], send, recv, a * 3 + k, (*chip, c))
            for a in range(len(ps)) for k, chip in enumerate(chips)]


def rs_chip_start(ps, name):
    n = len(ps)

    def body(*refs):
        ins, lands = refs[:n], refs[n:2 * n]
        send, recv = refs[2 * n], refs[2 * n + 1]
        token = refs[-1]
        for cp in _chip_copies(ins, lands, send, recv):
            cp.start()
        token[...] = jnp.zeros_like(token)

    res = pl.pallas_call(
        body, name=name,
        out_shape=(pltpu.SemaphoreType.DMA((3 * n,)), pltpu.SemaphoreType.DMA((3 * n,)),
                   *[pltpu.HBM(p.shape, p.dtype) for p in ps], *[pltpu.HBM(p.shape, p.dtype) for p in ps],
                   jax.ShapeDtypeStruct((8, 128), F32)),
        in_specs=[HBM] * (2 * n), out_specs=(SEM, SEM, *[HBM] * (2 * n), pl.BlockSpec(memory_space=pltpu.VMEM)),
        input_output_aliases={a: 2 + a for a in range(2 * n)},
        compiler_params=pltpu.CompilerParams(has_side_effects=EFFECT),
    )(*[_hbm(p) for p in ps], *[_hbm(lax.empty(p.shape, p.dtype)) for p in ps])
    return res[0], res[1], list(res[2:2 + n]), list(res[2 + n:2 + 2 * n]), res[-1]


def rs_chip_wait(send, recv, ps, lands, after, name):
    n = len(ps)

    def body(*refs):
        ins, zones = refs[:n], refs[n:2 * n]
        send_r, recv_r = refs[2 * n], refs[2 * n + 1]
        x, y, c, jme, sib, chips = _place()
        for a in range(n):
            for k, chip in enumerate(chips):
                jt = 2 * chip[0] + chip[1]
                cp = _remote(ins[a].at[jt], zones[a].at[jt], send_r, recv_r, a * 3 + k, (*chip, c))
                cp.wait_send()
                cp.wait_recv()

    res = pl.pallas_call(
        body, name=name,
        out_shape=tuple(pltpu.HBM(p.shape, p.dtype) for p in list(ps) + list(lands)),
        in_specs=[HBM] * (2 * n) + [SEM, SEM, ANY], out_specs=tuple([HBM] * (2 * n)),
        input_output_aliases={a: a for a in range(2 * n)},
        compiler_params=pltpu.CompilerParams(has_side_effects=EFFECT),
    )(*ps, *lands, send, recv, after)
    return list(res[n:])


def rs_chip_sum(q, p, l, acc, layers, jc):
    _, rh, cols = q.shape
    tr = rh if rh <= 256 else 256

    def body(jc_ref, q_ref, p_ref, *rest):
        o_ref = rest[-1]
        jme = jc_ref[0]
        own = p_ref[...].astype(F32)
        v = [jnp.where(jme == j, own, q_ref[j].astype(F32)) for j in range(4)]
        o_ref[...] = ((v[0] + v[1]) + v[2]) + v[3]

    in_specs = [pl.BlockSpec((4, tr, cols), lambda t, jr: (0, t, 0)),
                pl.BlockSpec((None, tr, cols), lambda t, jr: (jr[0], t, 0))]
    args = [jc, q, p]
    if acc is not None:
        in_specs.append(ANY)
        args.append(acc)
    return pl.pallas_call(
        body, name="rs_chip_sum",
        grid_spec=pltpu.PrefetchScalarGridSpec(
            num_scalar_prefetch=1, grid=(rh // tr,), in_specs=in_specs,
            out_specs=pl.BlockSpec((None, None, tr, cols), lambda t, jr: (l, jr[1], t, 0))),
        out_shape=jax.ShapeDtypeStruct((layers, 2, rh, cols), F32),
        input_output_aliases={} if acc is None else {3: 0},
        compiler_params=_cparams(("parallel",)),
    )(*args)


def rs_pair_gather(rs):
    n = len(rs)

    def body(*refs):
        outs = refs[n:2 * n]
        send, recv = refs[2 * n:]
        x, y, c, jme, sib, chips = _place()
        cps = [_remote(outs[a].at[:, c], outs[a].at[:, c], send, recv, a, sib) for a in range(n)]
        for cp in cps:
            cp.start()
        for a in range(n):
            slot = outs[a].at[:, 1 - c]
            _remote(slot, slot, send, recv, a, sib).wait_recv()
        for cp in cps:
            cp.wait_send()

    return pl.pallas_call(
        body, name="rs_pair_gather", in_specs=[ANY] * n, out_specs=[ANY] * n,
        out_shape=[jax.ShapeDtypeStruct(r.shape, r.dtype) for r in rs],
        input_output_aliases={a: a for a in range(n)},
        scratch_shapes=[pltpu.SemaphoreType.DMA((n,)), pltpu.SemaphoreType.DMA((n,))],
    )(*rs)


def _adamw_math(w, g, m, v):
    m = B1 * m + (1.0 - B1) * g
    v = B2 * v + (1.0 - B2) * (g * g)
    m_hat = m / (1.0 - B1 ** STEP)
    v_hat = v / (1.0 - B2 ** STEP)
    return -LR * (m_hat / (jnp.sqrt(v_hat) + AEPS) + WD * w), m, v


def adamw(w, g, m, v, name):
    rows, cols = w.shape
    tr = 256 if rows % 256 == 0 else rows
    return rw(_adamw_math, [(a, 0, cols) for a in (w, g, m, v)], [(cols, F32)] * 3, name, rows, tr=tr)


WEIGHTS = ["norm_ab", "w_in_ab", "pool_w", "pool_scale", "w_out_ab", "norm_cd", "w_in_cd", "sgu_ln_g", "sgu_ln_b",
           "sgu_w", "sgu_b", "s5_a_re", "s5_a_im", "s5_log_dt", "s5_b_re", "s5_b_im", "s5_c_re", "s5_c_im", "s5_d",
           "glu_w1", "glu_w2", "w_out_cd", "norm_x", "w_xq", "w_xkv", "w_xo", "mem_norm", "final_norm"]
INPUTS = ["x", "mem"] + WEIGHTS + ["loss_target"] + ["m_" + n for n in WEIGHTS] + ["v_" + n for n in WEIGHTS]
BIG = ["w_in_ab", "w_out_ab", "w_in_cd", "w_out_cd", "w_xq", "w_xkv", "w_xo", "glu_w1", "glu_w2", "pool_w"]
COL_SHARDED = ("w_in_ab", "w_in_cd", "w_xkv")
SMALL = [n for n in WEIGHTS if n not in BIG]
SMALL_SHARDED = {"norm_cd": 256, "sgu_ln_g": 256, "sgu_ln_b": 256, "s5_d": 128}
PACK = 256 * 128


def _pack(arrs):
    flat = jnp.concatenate([a.reshape(-1) for a in arrs])
    pad = (-flat.shape[0]) % PACK
    return jnp.concatenate([flat, jnp.zeros((pad,), flat.dtype)]).reshape(-1, 128)


def _unpack(packed, shapes):
    flat, out, off = packed.reshape(-1), [], 0
    for s in shapes:
        n = 1
        for d in s:
            n *= d
        out.append(flat[off:off + n].reshape(s))
        off += n
    return out


LAYER_KEYS = (("w_in", "w_out", "pool_w", "w_xq", "w_xkv", "w_xo"),
              ("w_in", "w_out", "glu_w1", "glu_w2", "w_xq", "w_xkv", "w_xo"))


def _weight_of(key, layer):
    if key in ("w_xq", "w_xkv", "w_xo"):
        return key, layer, 4
    kind = "ab" if layer % 2 == 0 else "cd"
    return {"w_in": "w_in_" + kind, "w_out": "w_out_" + kind}.get(key, key), layer // 2, 2


def kernel(*args):
    a = dict(zip(INPUTS, args))
    x_i, y_i, c_i = lax.axis_index("x"), lax.axis_index("y"), lax.axis_index("c")
    j = 2 * x_i + y_i

    slab = jnp.concatenate([a["norm_cd"], a["sgu_ln_g"], a["sgu_ln_b"],
                            jnp.pad(a["s5_d"], ((0, 0), (0, 128)))], axis=0)
    gslab = allgather_small(slab)
    P = {n: a[n] for n in SMALL}
    for k, n in enumerate(("norm_cd", "sgu_ln_g", "sgu_ln_b", "s5_d")):
        wd = SMALL_SHARDED[n]
        P[n] = gslab[:, 2 * k:2 * k + 2, :wd].transpose(1, 0, 2).reshape(2, 4 * wd)

    def shards_of(layer):
        keys = sorted(k for k in LAYER_KEYS[layer % 2])
        out = []
        for k in keys:
            n, l, _ = _weight_of(k, layer)
            out.append(a[n][l].reshape(-1, a[n].shape[-1]).astype(BF16))
        return keys, out

    keys0, sh0 = shards_of(0)
    first = keys0.index("w_in")
    g_in, token = allgather_sync([sh0[first].reshape(2, sh0[first].shape[0] // 2, sh0[first].shape[1])])
    w_in0 = g_in[0].reshape(4, -1, g_in[0].shape[-1])
    started = {}
    for layer in (0, 1, 2, 3):
        keys, sh = shards_of(layer)
        if layer == 0:
            keys, sh = [k for k in keys if k != "w_in"], [s for k, s in zip(keys0, sh0) if k != "w_in"]
        send, recv, sh, lands, token = allgather_start(sh, [token, gslab], "allgather_start_%d" % layer)
        started[layer] = (keys, send, recv, sh, lands)
    P["norm_ab"] = P["norm_ab"] + token[0:1, 0:1]

    cidx = jnp.reshape(c_i, (1,)).astype(jnp.int32)
    jc = jnp.stack([j, c_i]).astype(jnp.int32)

    def views(g):
        W = {}
        for k, v in g.items():
            if k in ("w_in", "w_xkv"):
                W[k] = mcs(v)
            elif k == "pool_w":
                W[k] = v.reshape(4, 4, 64, 256).transpose(1, 0, 2, 3).reshape(4, 256, 256)
            elif k not in ("glu_w1", "glu_w2"):
                W[k] = m2(v.reshape(-1, v.shape[-1]))
        if "glu_w1" in g:
            W["w12"] = jnp.concatenate([g["glu_w1"].reshape(512, 512), g["glu_w2"].reshape(512, 512)], axis=1)
        return W

    def arrived(layer, after):
        keys, send, recv, sh, lands = started[layer]
        return views(dict(zip(keys, allgather_wait(send, recv, sh, lands, after, "allgather_wait_%d" % layer))))

    def weights_of(layer, x_in):
        if layer == 0:
            W = views({"w_in": w_in0})
            W["more"] = lambda after: arrived(0, after)
            return W
        return arrived(layer, x_in)

    pending = {}

    def grads_done(layer, GW):
        keys = sorted(GW)
        flat = [GW[k].reshape(4, 2, GW[k].shape[1] // 2, GW[k].shape[2]) for k in keys]
        pair = [rs_pair_sum(g4, r, cidx) for g4, r in zip(flat, rs_pair_exchange(flat))]
        send, recv, pair, lands, token = rs_chip_start(pair, "rs_chip_start_%d" % layer)
        pending[layer] = (keys, send, recv, pair, lands)
        return token[0:1, 0:1]

    loss, dx, G = local_step(a["x"][0], a["mem"][0], a["loss_target"][0], P, weights_of, grads_done)
    loss = lax.psum(loss[0, 0], ("x", "y", "c"))

    red = {}
    for layer in (3, 2, 1, 0):
        keys, send, recv, pair, lands = pending[layer]
        lands = rs_chip_wait(send, recv, pair, lands, dx, "rs_chip_wait_%d" % layer)
        for k, q, p in zip(keys, lands, pair):
            n, l, layers = _weight_of(k, layer)
            red[n] = rs_chip_sum(q, p, l, red.get(n), layers, jc)
    gbig = dict(zip(BIG, rs_pair_gather([red[n] for n in BIG])))

    outs = {}
    for n in BIG:
        shp = a[n].shape
        g2 = gbig[n].reshape(-1, shp[-1])
        d2, m2_, v2_ = adamw(a[n].reshape(g2.shape), g2, a["m_" + n].reshape(g2.shape),
                             a["v_" + n].reshape(g2.shape), "adamw_" + n)
        outs[n] = tuple(t.reshape(shp) for t in (g2, d2, m2_, v2_))

    gfull = [jnp.stack(G[n]) if isinstance(G[n], list) else G[n] for n in SMALL]
    shapes = [g.shape for g in gfull]
    gsum = _unpack(allreduce_small(_pack(gfull)), shapes)
    gloc = []
    for n, g in zip(SMALL, gsum):
        if n in SMALL_SHARDED:
            g = lax.dynamic_slice_in_dim(g, j * SMALL_SHARDED[n], SMALL_SHARDED[n], axis=1)
        gloc.append(g)
    for n, g in zip(SMALL, gloc):
        shp = a[n].shape
        two = (-1, shp[-1]) if len(shp) > 1 else (1, shp[0])
        upd = adamw(a[n].reshape(two), g.reshape(two), a["m_" + n].reshape(two), a["v_" + n].reshape(two), "adamw_" + n)
        outs[n] = (g,) + tuple(t.reshape(shp) for t in upd)

    res = [loss, dx[None]]
    for part in range(4):
        res += [outs[n][part] for n in WEIGHTS]
    return tuple(res)
```

```python
import math

import jax
import jax.numpy as jnp
from jax import lax
from jax.experimental import pallas as pl
from jax.experimental.pallas import tpu as pltpu

F32, BF16 = jnp.float32, jnp.bfloat16
S, D = 2048, 1024
MEM = 256
EPS = 1e-6
NEG = -1e30
QB = 128
PATTERNS = (1, 4, 16)
NG, NP, NH = 32, 64, 16
NS = NG * NP
LR, B1, B2, AEPS, WD, STEP = 0.001, 0.9, 0.999, 1e-08, 0.01, 10
MESHID = pl.DeviceIdType.MESH
VMEM_LIMIT = 56 * 1024 * 1024


def _cparams(sem):
    return pltpu.CompilerParams(dimension_semantics=sem, vmem_limit_bytes=VMEM_LIMIT)


def _sig(x):
    return 1.0 / (1.0 + jnp.exp(-x))


def _dot(a, b, dims):
    return lax.dot_general(a, b, (dims, ((), ())), preferred_element_type=F32)


def _nn(a, b):
    return _dot(a, b, ((1,), (0,)))


def _nt(a, b):
    return _dot(a, b, ((1,), (1,)))


def _tn(a, b):
    return _dot(a, b, ((0,), (0,)))


_DIMS = {"nn": ((1,), (0,)), "nt": ((1,), (1,)), "tn": ((0,), (0,))}


def _tile(dim, cc=None, cap=1024):
    for t in (2048, 1536, 1024, 768, 512, 384, 256, 128):
        if t <= cap and dim % t == 0 and (cc is None or cc % t == 0):
            return t
    return dim


MM_VMEM = 36 * 1024 * 1024


def _mm_tiles(m, n, k, ccm, ccn, cck, a_bytes, b_bytes, o_bytes):
    caps = [1024, 1024, 2048]
    while True:
        tm, tn, tk = _tile(m, ccm, caps[0]), _tile(n, ccn, caps[1]), _tile(k, cck, caps[2])
        need = 2 * (tm * tk * a_bytes + tk * tn * b_bytes + tm * tn * o_bytes) + (tm * tn * 4 if tk < k else 0)
        if need <= MM_VMEM:
            return tm, tn, tk
        if tk > 1024:
            caps[2] = tk // 2
        elif tn >= tm:
            caps[1] = tn // 2
        else:
            caps[0] = tm // 2


def m2(arr, col_off=0, ncols=None):
    rows, cols = arr.shape
    ncols = cols - col_off if ncols is None else ncols

    def spec(tr, tc, rc):
        assert col_off % tc == 0
        return pl.BlockSpec((tr, tc), lambda *g: (rc(*g)[0], rc(*g)[1] + col_off // tc))
    return (arr, rows, ncols, spec, None if col_off == 0 else col_off)


def mcs(arr):
    cs = arr.shape[2]

    def spec(tr, tc, rc):
        n = cs // tc
        return pl.BlockSpec((None, tr, tc), lambda *g: (rc(*g)[1] // n, rc(*g)[0], rc(*g)[1] % n))
    return (arr, arr.shape[1], 4 * cs, spec, cs)


def out2(rows, cols):
    def spec(tr, tc, rc):
        return pl.BlockSpec((tr, tc), lambda *g: tuple(rc(*g)))
    return ((rows, cols), spec, None)


def outcs(rows, cs):
    def spec(tr, tc, rc):
        n = cs // tc
        return pl.BlockSpec((None, tr, tc), lambda *g: (rc(*g)[1] // n, rc(*g)[0], rc(*g)[1] % n))
    return ((4, rows, cs), spec, cs)


def _both(a, b):
    if a is None:
        return b
    if b is None:
        return a
    return math.gcd(a, b)


def mm(a, b, mode, name, add=None, out=None, out_dtype=F32):
    a_arr, a_r, a_c, a_spec, a_cc = a
    b_arr, b_r, b_c, b_spec, b_cc = b
    if mode == "nn":
        m, k, n = a_r, a_c, b_c
        assert b_r == k
        ccm, cck, ccn = None, a_cc, b_cc
    elif mode == "nt":
        m, k, n = a_r, a_c, b_r
        assert b_c == k
        ccm, cck, ccn = None, _both(a_cc, b_cc), None
    else:
        m, k, n = a_c, a_r, b_c
        assert b_r == k
        ccm, cck, ccn = a_cc, None, b_cc
    out = out2(m, n) if out is None else out
    o_shape, o_spec, o_cc = out
    ccn = _both(ccn, o_cc)
    if add is not None:
        ccn = _both(ccn, add[4])
    o_bytes = jnp.dtype(out_dtype).itemsize + (0 if add is None else add[0].dtype.itemsize)
    tm, tn, tk = _mm_tiles(m, n, k, ccm, ccn, cck, a_arr.dtype.itemsize, b_arr.dtype.itemsize, o_bytes)
    nk = k // tk
    if mode == "nn":
        in_specs = [a_spec(tm, tk, lambda i, j, kk: (i, kk)), b_spec(tk, tn, lambda i, j, kk: (kk, j))]
    elif mode == "nt":
        in_specs = [a_spec(tm, tk, lambda i, j, kk: (i, kk)), b_spec(tn, tk, lambda i, j, kk: (j, kk))]
    else:
        in_specs = [a_spec(tk, tm, lambda i, j, kk: (kk, i)), b_spec(tk, tn, lambda i, j, kk: (kk, j))]
    args = [a_arr, b_arr]
    if add is not None:
        in_specs.append(add[3](tm, tn, lambda i, j, kk: (i, j)))
        args.append(add[0])
    dims = _DIMS[mode]
    has_add = add is not None

    def body(*refs):
        a_ref, b_ref = refs[0], refs[1]
        add_ref = refs[2] if has_add else None
        prod = _dot(a_ref[...].astype(BF16), b_ref[...].astype(BF16), dims)
        if nk == 1:
            o_ref = refs[-1]
            if has_add:
                prod = prod + add_ref[...].astype(F32)
            o_ref[...] = prod.astype(o_ref.dtype)
            return
        o_ref, acc = refs[-2], refs[-1]
        kk = pl.program_id(2)

        @pl.when(kk == 0)
        def _():
            acc[...] = prod

        @pl.when(kk > 0)
        def _():
            acc[...] += prod

        @pl.when(kk == nk - 1)
        def _():
            r = acc[...]
            if has_add:
                r = r + add_ref[...].astype(F32)
            o_ref[...] = r.astype(o_ref.dtype)

    return pl.pallas_call(
        body, name=name, grid=(m // tm, n // tn, nk), in_specs=in_specs,
        out_specs=o_spec(tm, tn, lambda i, j, kk: (i, j)),
        out_shape=jax.ShapeDtypeStruct(o_shape, out_dtype),
        scratch_shapes=[pltpu.VMEM((tm, tn), F32)] if nk > 1 else [],
        compiler_params=_cparams(("parallel", "parallel", "arbitrary")),
    )(*args)


def rw(fn, ins, outs, name, rows, tr=256, consts=(), accs=()):
    n_in, n_c, n_o, n_a = len(ins), len(consts), len(outs), len(accs)
    in_specs = []
    for arr, off, width in ins:
        assert off % width == 0
        in_specs.append(pl.BlockSpec((tr, width), lambda i, o=off // width: (i, o)))
    for c in consts:
        in_specs.append(pl.BlockSpec(c.shape, lambda i: (0, 0)))
    out_specs = [pl.BlockSpec((tr, w), lambda i: (i, 0)) for w, _ in outs]
    out_specs += [pl.BlockSpec(s, lambda i: (0, 0)) for s in accs]
    out_shape = [jax.ShapeDtypeStruct((rows, w), dt) for w, dt in outs]
    out_shape += [jax.ShapeDtypeStruct(s, F32) for s in accs]

    def body(*refs):
        vals = [r[...] for r in refs[:n_in + n_c]]
        o_refs = refs[n_in + n_c:n_in + n_c + n_o]
        a_refs = refs[n_in + n_c + n_o:]
        res = fn(*vals)
        for r, v in zip(o_refs, res[:n_o]):
            r[...] = v.astype(r.dtype)
        if n_a:
            @pl.when(pl.program_id(0) == 0)
            def _():
                for r in a_refs:
                    r[...] = jnp.zeros_like(r)
            for r, v in zip(a_refs, res[n_o:]):
                r[...] += v

    res = pl.pallas_call(
        body, name=name, grid=(rows // tr,), in_specs=in_specs, out_specs=out_specs,
        out_shape=out_shape,
        compiler_params=_cparams(("arbitrary",) if n_a else ("parallel",)),
    )(*[a for a, _, _ in ins], *consts)
    return res


def _rstd(x):
    return lax.rsqrt(jnp.mean(x * x, axis=-1, keepdims=True) + EPS)


def rms_fwd(x, g, name):
    def fn(xv, gv):
        xv = xv.astype(F32)
        return (xv * _rstd(xv) * gv,)
    return rw(fn, [(x, 0, D)], [(D, BF16)], name, x.shape[0], consts=[g])[0]


def _rms_bwd_math(xv, dy, gv):
    r = _rstd(xv)
    dyg = dy * gv
    dx = r * dyg - xv * (r * r * r / D) * jnp.sum(dyg * xv, axis=-1, keepdims=True)
    dg = jnp.sum(dy * xv * r, axis=0, keepdims=True)
    return dx, dg


def rms_bwd(x, dy, dres, g, name):
    def fn(xv, dyv, drv, gv):
        dx, dg = _rms_bwd_math(xv, dyv, gv)
        return dx + drv, dg
    return rw(fn, [(x, 0, D), (dy, 0, D), (dres, 0, D)], [(D, F32)], name, x.shape[0],
              consts=[g], accs=[(1, D)])


def final_loss(x, tgt, g):
    def fn(xv, tv, gv):
        e = xv * _rstd(xv) * gv - tv
        loss = 0.5 * jnp.sum(jnp.sum(e * e, axis=-1, keepdims=True), axis=0, keepdims=True) / D
        dx, dg = _rms_bwd_math(xv, e / D, gv)
        return dx, loss, dg
    return rw(fn, [(x, 0, D), (tgt, 0, D)], [(D, F32)], "final_loss", S, consts=[g],
              accs=[(1, 1), (1, D)])


def _attn_bias(bias_ref):
    ii = lax.broadcasted_iota(jnp.int32, (2 * QB, 2 * QB), 0) % QB
    jj = lax.broadcasted_iota(jnp.int32, (2 * QB, 2 * QB), 1)
    dist = ii + QB - jj
    band = (dist >= 0) & (dist <= QB)
    bias_ref[1] = jnp.where(band, 0.0, NEG)
    bias_ref[0] = jnp.where(band & (jj >= QB), 0.0, NEG)


def _two_heads(x, m0):
    return jnp.concatenate([jnp.where(m0, x, 0.0), jnp.where(m0, 0.0, x)], axis=0)


def _per_head(col, m0):
    return jnp.where(m0, col[:QB], col[QB:])


def _attn_rows(idx, d):
    if d == 1:
        b = idx
        cur = pl.ds(pl.multiple_of(b * QB, QB), QB)
        prev = pl.ds(pl.multiple_of(jnp.maximum(b - 1, 0) * QB, QB), QB)
    else:
        r, b = lax.rem(idx, d), lax.div(idx, d)
        cur = pl.ds(r + b * (QB * d), QB, stride=d)
        prev = pl.ds(r + jnp.maximum(b - 1, 0) * (QB * d), QB, stride=d)
    return cur, prev, b


NBLK = S // QB
GROUP = 4


def _colblk(off):
    return pl.BlockSpec((S, 128), lambda hp: (0, off * 8 + hp))


def attn_fwd(z):
    def body(q_ref, k_ref, v_ref, g_ref, o_ref, l_ref, a_ref, os, ls, bias):
        _attn_bias(bias)
        m0 = lax.broadcasted_iota(jnp.int32, (1, 128), 1) < 64
        for pi, d in enumerate(PATTERNS):
            def load(idx, d=d):
                cur, prev, b = _attn_rows(idx, d)
                return cur, (q_ref[cur, :], k_ref[prev, :], k_ref[cur, :], v_ref[prev, :], v_ref[cur, :],
                             bias[jnp.minimum(b, 1)])

            def block(q, kp, kc, vp, vc, bs):
                qq = _two_heads(q * 0.125, m0).astype(BF16)
                k = jnp.concatenate([kp, kc], axis=0).astype(BF16)
                s = _nt(qq, k) + bs
                mx = jnp.max(s, axis=-1, keepdims=True)
                p = jnp.exp(s - mx)
                den = jnp.sum(p, axis=-1, keepdims=True)
                pb = p.astype(BF16)
                vv = _two_heads(jnp.concatenate([vp, vc], axis=0), m0).astype(BF16)
                o = _nn(jnp.concatenate([pb[:QB], pb[QB:]], axis=1), vv)
                return o * _per_head(1.0 / den, m0), _per_head(mx + jnp.log(den), m0)

            def step(i, carry, pi=pi):
                loaded = [load(i * GROUP + u) for u in range(GROUP)]
                done = [block(*vals) for _, vals in loaded]
                for (cur, _), (o, l) in zip(loaded, done):
                    os[pi, cur, :] = o
                    ls[pi, cur, :] = l
                return carry
            lax.fori_loop(0, NBLK // GROUP, step, 0)
        l1, l2, l3 = ls[0], ls[1], ls[2]
        mx = jnp.maximum(jnp.maximum(l1, l2), l3)
        e1, e2, e3 = jnp.exp(l1 - mx), jnp.exp(l2 - mx), jnp.exp(l3 - mx)
        tot = e1 + e2 + e3
        o = (os[0] * e1 + os[1] * e2 + os[2] * e3) / tot
        ga = g_ref[...]
        o_ref[...] = o
        l_ref[...] = mx + jnp.log(tot)
        a_ref[...] = (o * (ga * _sig(ga))).astype(a_ref.dtype)

    out = pl.BlockSpec((S, 128), lambda hp: (0, hp))
    return pl.pallas_call(
        body, name="attn_fwd", grid=(8,),
        in_specs=[_colblk(0), _colblk(1), _colblk(2), _colblk(3)], out_specs=[out] * 3,
        out_shape=[jax.ShapeDtypeStruct((S, D), F32), jax.ShapeDtypeStruct((S, D), F32),
                   jax.ShapeDtypeStruct((S, 2 * D), BF16)],
        scratch_shapes=[pltpu.VMEM((3, S, 128), F32), pltpu.VMEM((3, S, 128), F32),
                        pltpu.VMEM((2, 2 * QB, 2 * QB), F32)],
        compiler_params=_cparams(("parallel",)),
    )(z, z, z, z)


def attn_bwd(z, d_cat, o, lse):
    def body(q_ref, k_ref, v_ref, g_ref, da_ref, o_ref, l_ref, dq_ref, dk_ref, dv_ref, dg_ref, do_s, pr_s, bias):
        _attn_bias(bias)
        m0 = lax.broadcasted_iota(jnp.int32, (1, 128), 1) < 64
        ga = g_ref[...]
        sg = _sig(ga)
        da = da_ref[...]
        ov = o_ref[...]
        do = da * (ga * sg)
        dg_ref[...] = da * ov * (sg * (1.0 + ga * (1.0 - sg)))
        do_s[...] = do
        pr_s[...] = do * ov
        dq_ref[...] = jnp.zeros_like(dq_ref)
        dk_ref[...] = jnp.zeros_like(dk_ref)
        dv_ref[...] = jnp.zeros_like(dv_ref)
        for d in PATTERNS:
            def load(idx, d=d):
                cur, prev, b = _attn_rows(idx, d)
                return (cur, prev), (q_ref[cur, :], k_ref[prev, :], k_ref[cur, :], v_ref[prev, :], v_ref[cur, :],
                                     do_s[cur, :], pr_s[cur, :], l_ref[cur, :], bias[jnp.minimum(b, 1)])

            def block(q, kp, kc, vp, vc, dof, prod, lp, bs):
                qq = _two_heads(q * 0.125, m0).astype(BF16)
                kf = jnp.concatenate([kp, kc], axis=0)
                k = kf.astype(BF16)
                v = jnp.concatenate([vp, vc], axis=0).astype(BF16)
                dd = _two_heads(dof, m0).astype(BF16)
                lh = jnp.max(jnp.concatenate([jnp.where(m0, lp, -jnp.inf), jnp.where(m0, -jnp.inf, lp)], axis=0),
                             axis=-1, keepdims=True)
                delta = jnp.sum(_two_heads(prod, m0), axis=-1, keepdims=True)
                p = jnp.exp(_nt(qq, k) + bs - lh)
                ds = (p * (_nt(dd, v) - delta)).astype(BF16)
                dq = _nn(jnp.concatenate([ds[:QB], ds[QB:]], axis=1), _two_heads(kf, m0).astype(BF16))
                return dq * 0.125, _tn(ds, qq), _tn(p.astype(BF16), dd)

            def step(i, carry):
                loaded = [load(i * GROUP + u) for u in range(GROUP)]
                done = [block(*vals) for _, vals in loaded]
                for ((cur, prev), _), (dq, dk, dv) in zip(loaded, done):
                    dq_ref[cur, :] = dq_ref[cur, :] + dq
                    dk_ref[prev, :] = dk_ref[prev, :] + dk[:QB]
                    dv_ref[prev, :] = dv_ref[prev, :] + dv[:QB]
                    dk_ref[cur, :] = dk_ref[cur, :] + dk[QB:]
                    dv_ref[cur, :] = dv_ref[cur, :] + dv[QB:]
                return carry
            lax.fori_loop(0, NBLK // GROUP, step, 0)

    blk = pl.BlockSpec((S, 128), lambda hp: (0, hp))
    return pl.pallas_call(
        body, name="attn_bwd", grid=(8,),
        in_specs=[_colblk(0), _colblk(1), _colblk(2), _colblk(3), blk, blk, blk], out_specs=[blk] * 4,
        out_shape=[jax.ShapeDtypeStruct((S, D), F32)] * 4,
        scratch_shapes=[pltpu.VMEM((S, 128), F32), pltpu.VMEM((S, 128), F32), pltpu.VMEM((2, 2 * QB, 2 * QB), F32)],
        compiler_params=_cparams(("parallel",)),
    )(z, z, z, z, d_cat, o, lse)


def assemble_dz_even(parts):
    def body(*refs):
        o_ref = refs[-1]
        for j in range(6):
            o_ref[:, j * D:(j + 1) * D] = refs[j][...].astype(o_ref.dtype)
    tr = 256
    blk = pl.BlockSpec((tr, D), lambda i: (i, 0))
    return pl.pallas_call(
        body, name="assemble_dz_even", grid=(S // tr,), in_specs=[blk] * 6,
        out_specs=pl.BlockSpec((tr, 6 * D), lambda i: (i, 0)),
        out_shape=jax.ShapeDtypeStruct((S, 6 * D), BF16),
        compiler_params=_cparams(("parallel",)),
    )(*parts)


def _pool_window(g):
    return jnp.where(g == 0, 2.0, jnp.where(g == 1, 4.0, jnp.where(g == 2, 8.0, 16.0)))


def _pool_sel(g, levels):
    return jnp.where(g == 0, levels[0], jnp.where(g == 1, levels[1], jnp.where(g == 2, levels[2], levels[3])))


def _pool_fwd_math(v, g):
    t = lax.broadcasted_iota(jnp.int32, (S, 1), 0)
    s = v
    levels = []
    for k in (1, 2, 4, 8):
        s = s + jnp.where(t >= k, pltpu.roll(s, k, 0), 0.0)
        levels.append(s)
    cnt = jnp.minimum((t + 1).astype(F32), _pool_window(g))
    return _pool_sel(g, levels) / cnt - v, cnt


def pool_fwd(z, pw, ps, cat):
    def body(v_ref, g_ref, pw_ref, ps_ref, cat_ref, o_ref):
        g = pl.program_id(0)
        pooled, _ = _pool_fwd_math(v_ref[...], g)
        mixed = _nn(pooled.astype(BF16), pw_ref[...].astype(BF16))
        gb = g_ref[...]
        o_ref[...] = (mixed * ps_ref[...] * (gb * _sig(gb))).astype(o_ref.dtype)

    return pl.pallas_call(
        body, name="pool_fwd", grid=(4,),
        in_specs=[pl.BlockSpec((S, 256), lambda g: (0, 16 + g)),
                  pl.BlockSpec((S, 256), lambda g: (0, 20 + g)),
                  pl.BlockSpec((None, 256, 256), lambda g: (g, 0, 0)),
                  pl.BlockSpec((1, 256), lambda g: (0, g)), pl.BlockSpec(memory_space=pl.ANY)],
        out_specs=pl.BlockSpec((S, 256), lambda g: (0, 4 + g)),
        out_shape=jax.ShapeDtypeStruct((S, 2 * D), BF16),
        input_output_aliases={4: 0},
        compiler_params=_cparams(("parallel",)),
    )(z, z, pw, ps, cat)


def pool_bwd(z, d_cat, pw, ps):
    def body(v_ref, g_ref, d_ref, pw_ref, ps_ref, dv_ref, dg_ref, dpw_ref, dps_ref):
        g = pl.program_id(0)
        v = v_ref[...]
        pooled, cnt = _pool_fwd_math(v, g)
        pwb = pw_ref[...].astype(BF16)
        pb = pooled.astype(BF16)
        mixed = _nn(pb, pwb)
        gb = g_ref[...]
        sg = _sig(gb)
        dout = d_ref[...]
        sc = ps_ref[...]
        dg_ref[...] = dout * mixed * sc * (sg * (1.0 + gb * (1.0 - sg)))
        dms = dout * (gb * sg)
        dps_ref[...] = jnp.sum(dms * mixed, axis=0, keepdims=True)
        dmx = (dms * sc).astype(BF16)
        dpw_ref[...] = _tn(pb, dmx)
        dpooled = _nt(dmx, pwb)
        t = lax.broadcasted_iota(jnp.int32, (S, 1), 0)
        s = dpooled / cnt
        levels = []
        for k in (1, 2, 4, 8):
            s = s + jnp.where(t < S - k, pltpu.roll(s, S - k, 0), 0.0)
            levels.append(s)
        dv_ref[...] = _pool_sel(g, levels) - dpooled

    return pl.pallas_call(
        body, name="pool_bwd", grid=(4,),
        in_specs=[pl.BlockSpec((S, 256), lambda g: (0, 16 + g)),
                  pl.BlockSpec((S, 256), lambda g: (0, 20 + g)),
                  pl.BlockSpec((S, 256), lambda g: (0, 4 + g)),
                  pl.BlockSpec((None, 256, 256), lambda g: (g, 0, 0)),
                  pl.BlockSpec((1, 256), lambda g: (0, g))],
        out_specs=[pl.BlockSpec((S, 256), lambda g: (0, g)),
                   pl.BlockSpec((S, 256), lambda g: (0, g)),
                   pl.BlockSpec((None, 256, 256), lambda g: (g, 0, 0)),
                   pl.BlockSpec((1, 256), lambda g: (0, g))],
        out_shape=[jax.ShapeDtypeStruct((S, D), F32), jax.ShapeDtypeStruct((S, D), F32),
                   jax.ShapeDtypeStruct((4, 256, 256), F32), jax.ShapeDtypeStruct((1, D), F32)],
        compiler_params=_cparams(("parallel",)),
    )(z, z, d_cat, pw, ps)


CH = 128


def _sgu_common(v, lng, lnb, w_ref):
    mu = jnp.mean(v, axis=-1, keepdims=True)
    vc = v - mu
    rs = lax.rsqrt(jnp.mean(vc * vc, axis=-1, keepdims=True) + EPS)
    xhat = vc * rs
    vn = (xhat * lng + lnb).astype(BF16)
    ri = lax.broadcasted_iota(jnp.int32, (CH, CH), 0)
    ci = lax.broadcasted_iota(jnp.int32, (CH, CH), 1)
    tril = ri >= ci
    ws = [jnp.where(tril, w_ref[g], 0.0).astype(BF16) for g in range(4)]
    return xhat, rs, vn, tril, ws


def _zspec(off):
    return pl.BlockSpec((CH, D), lambda c: (c, off))


def _full(shape):
    return pl.BlockSpec(shape, lambda c: (0,) * len(shape))


def sgu_fwd(z, lng, lnb, w, bfull):
    def body(u_ref, v_ref, g_ref, lng_ref, lnb_ref, w_ref, b_ref, o_ref):
        _, _, vn, _, ws = _sgu_common(v_ref[...], lng_ref[...], lnb_ref[...], w_ref)
        for g in range(4):
            sl = slice(g * 256, (g + 1) * 256)
            mixed = _nn(ws[g], vn[:, sl]) + b_ref[:, sl]
            gc = g_ref[:, sl]
            o_ref[:, sl] = (u_ref[:, sl] * mixed * (gc * _sig(gc))).astype(o_ref.dtype)

    return pl.pallas_call(
        body, name="sgu_fwd", grid=(S // CH,),
        in_specs=[_zspec(0), _zspec(1), _zspec(2), _full((1, D)), _full((1, D)),
                  _full((4, CH, CH)), _full((CH, D))],
        out_specs=pl.BlockSpec((CH, D), lambda c: (c, 0)),
        out_shape=jax.ShapeDtypeStruct((S, D), BF16),
        compiler_params=_cparams(("parallel",)),
    )(z, z, z, lng, lnb, w, bfull)


def sgu_bwd(z, d_cat, lng, lnb, w, bfull):
    def body(u_ref, v_ref, g_ref, d_ref, lng_ref, lnb_ref, w_ref, b_ref,
             du_ref, dv_ref, dg_ref, dw_ref, db_ref, dlg_ref, dlb_ref):
        @pl.when(pl.program_id(0) == 0)
        def _():
            dw_ref[...] = jnp.zeros_like(dw_ref)
            db_ref[...] = jnp.zeros_like(db_ref)
            dlg_ref[...] = jnp.zeros_like(dlg_ref)
            dlb_ref[...] = jnp.zeros_like(dlb_ref)

        lng = lng_ref[...]
        xhat, rs, vn, tril, ws = _sgu_common(v_ref[...], lng, lnb_ref[...], w_ref)
        lane = lax.broadcasted_iota(jnp.int32, (1, 128), 1)
        db = jnp.zeros((CH, 128), F32)
        dvn_parts = []
        for g in range(4):
            sl = slice(g * 256, (g + 1) * 256)
            mixed = _nn(ws[g], vn[:, sl]) + b_ref[:, sl]
            gc = g_ref[:, sl]
            sg = _sig(gc)
            u = u_ref[:, sl]
            dc = d_ref[:, sl]
            du_ref[:, sl] = dc * mixed * (gc * sg)
            dg_ref[:, sl] = dc * u * mixed * (sg * (1.0 + gc * (1.0 - sg)))
            dmx = dc * u * (gc * sg)
            db = db + jnp.where(lane == g, jnp.sum(dmx, axis=-1, keepdims=True), 0.0)
            dmb = dmx.astype(BF16)
            dw_ref[g] += jnp.where(tril, _nt(dmb, vn[:, sl]), 0.0)
            dvn_parts.append(_tn(ws[g], dmb))
        db_ref[...] += db
        dvn = jnp.concatenate(dvn_parts, axis=1)
        dlb_ref[...] += jnp.sum(dvn, axis=0, keepdims=True)
        dlg_ref[...] += jnp.sum(dvn * xhat, axis=0, keepdims=True)
        dxh = dvn * lng
        dv_ref[...] = rs * (dxh - jnp.mean(dxh, axis=-1, keepdims=True)
                            - xhat * jnp.mean(dxh * xhat, axis=-1, keepdims=True))

    row = pl.BlockSpec((CH, D), lambda c: (c, 0))
    return pl.pallas_call(
        body, name="sgu_bwd", grid=(S // CH,),
        in_specs=[_zspec(0), _zspec(1), _zspec(2), row, _full((1, D)), _full((1, D)),
                  _full((4, CH, CH)), _full((CH, D))],
        out_specs=[row, row, row, _full((4, CH, CH)), _full((CH, 128)), _full((1, D)), _full((1, D))],
        out_shape=[jax.ShapeDtypeStruct((S, D), F32)] * 3
        + [jax.ShapeDtypeStruct((4, CH, CH), F32), jax.ShapeDtypeStruct((CH, 128), F32),
           jax.ShapeDtypeStruct((1, D), F32), jax.ShapeDtypeStruct((1, D), F32)],
        compiler_params=_cparams(("arbitrary",)),
    )(z, z, z, d_cat, lng, lnb, w, bfull)


TB = 256


def _cmul(ar, ai, br, bi):
    return ar * br - ai * bi, ar * bi + ai * br


def _scan_consts(ar, ai, reverse):
    a2 = _cmul(ar, ai, ar, ai)
    a4 = _cmul(*a2, *a2)
    row = lax.broadcasted_iota(jnp.int32, (8, NS), 0)
    pr = jnp.zeros((8, NS), F32)
    pi = jnp.zeros((8, NS), F32)
    cr, ci = ar, ai
    for r in range(8):
        sel = row == (7 - r if reverse else r)
        pr = jnp.where(sel, cr, pr)
        pi = jnp.where(sel, ci, pi)
        cr, ci = _cmul(cr, ci, ar, ai)
    return ((ar, ai), a2, a4), (pr, pi), row


def scan_fwd(bu, abr, abi):
    def body(bu_ref, ar_ref, ai_ref, h_ref, car, cai):
        @pl.when(pl.program_id(0) == 0)
        def _():
            car[...] = jnp.zeros_like(car)
            cai[...] = jnp.zeros_like(cai)

        pows, (pr, pi), row = _scan_consts(ar_ref[...], ai_ref[...], False)

        def tile(t, carry):
            c_r, c_i = carry
            rows = pl.ds(pl.multiple_of(t * 8, 8), 8)
            xr = bu_ref[rows, 0:NS]
            xi = bu_ref[rows, NS:2 * NS]
            for k, (kr, ki) in zip((1, 2, 4), pows):
                sr = jnp.where(row >= k, pltpu.roll(xr, k, 0), 0.0)
                si = jnp.where(row >= k, pltpu.roll(xi, k, 0), 0.0)
                xr, xi = xr + kr * sr - ki * si, xi + kr * si + ki * sr
            xr, xi = xr + pr * c_r - pi * c_i, xi + pr * c_i + pi * c_r
            h_ref[rows, 0:NS] = xr
            h_ref[rows, NS:2 * NS] = xi
            return (jnp.broadcast_to(xr[7:8, :], (8, NS)), jnp.broadcast_to(xi[7:8, :], (8, NS)))

        c_r, c_i = lax.fori_loop(0, TB // 8, tile, (car[...], cai[...]))
        car[...] = c_r
        cai[...] = c_i

    return pl.pallas_call(
        body, name="s5_scan_fwd", grid=(S // TB,),
        in_specs=[pl.BlockSpec((TB, 2 * NS), lambda i: (i, 0)),
                  pl.BlockSpec((1, NS), lambda i: (0, 0)), pl.BlockSpec((1, NS), lambda i: (0, 0))],
        out_specs=pl.BlockSpec((TB, 2 * NS), lambda i: (i, 0)),
        out_shape=jax.ShapeDtypeStruct((S, 2 * NS), F32),
        scratch_shapes=[pltpu.VMEM((8, NS), F32), pltpu.VMEM((8, NS), F32)],
        compiler_params=_cparams(("arbitrary",)),
    )(bu, abr, abi)


def scan_bwd(eta, h, abr, abi):
    nt = S // TB

    def body(e_ref, h_ref, ar_ref, ai_ref, l_ref, da_ref, car, cai):
        @pl.when(pl.program_id(0) == 0)
        def _():
            car[...] = jnp.zeros_like(car)
            cai[...] = jnp.zeros_like(cai)
            da_ref[...] = jnp.zeros_like(da_ref)

        pows, (pr, pi), row = _scan_consts(ar_ref[...], -ai_ref[...], True)

        def tile(tt, carry):
            c_r, c_i, acr, aci = carry
            t = TB // 8 - 1 - tt
            rows = pl.ds(pl.multiple_of(t * 8, 8), 8)
            xr = e_ref[rows, 0:NS]
            xi = e_ref[rows, NS:2 * NS]
            for k, (kr, ki) in zip((1, 2, 4), pows):
                sr = jnp.where(row < 8 - k, pltpu.roll(xr, 8 - k, 0), 0.0)
                si = jnp.where(row < 8 - k, pltpu.roll(xi, 8 - k, 0), 0.0)
                xr, xi = xr + kr * sr - ki * si, xi + kr * si + ki * sr
            xr, xi = xr + pr * c_r - pi * c_i, xi + pr * c_i + pi * c_r
            l_ref[rows, 0:NS] = xr
            l_ref[rows, NS:2 * NS] = xi
            nr = jnp.where(row < 7, pltpu.roll(xr, 7, 0), c_r)
            ni = jnp.where(row < 7, pltpu.roll(xi, 7, 0), c_i)
            hr = h_ref[rows, 0:NS]
            hi = h_ref[rows, NS:2 * NS]
            acr = acr + hr * nr + hi * ni
            aci = aci + hr * ni - hi * nr
            return (jnp.broadcast_to(xr[0:1, :], (8, NS)), jnp.broadcast_to(xi[0:1, :], (8, NS)), acr, aci)

        zero = jnp.zeros((8, NS), F32)
        c_r, c_i, acr, aci = lax.fori_loop(0, TB // 8, tile, (car[...], cai[...], zero, zero))
        car[...] = c_r
        cai[...] = c_i
        da_ref[:, 0:NS] += acr
        da_ref[:, NS:2 * NS] += aci

    rev = pl.BlockSpec((TB, 2 * NS), lambda i: (nt - 1 - i, 0))
    return pl.pallas_call(
        body, name="s5_scan_bwd", grid=(nt,),
        in_specs=[rev, rev, pl.BlockSpec((1, NS), lambda i: (0, 0)), pl.BlockSpec((1, NS), lambda i: (0, 0))],
        out_specs=[rev, pl.BlockSpec((8, 2 * NS), lambda i: (0, 0))],
        out_shape=[jax.ShapeDtypeStruct((S, 2 * NS), F32), jax.ShapeDtypeStruct((8, 2 * NS), F32)],
        scratch_shapes=[pltpu.VMEM((8, NS), F32), pltpu.VMEM((8, NS), F32)],
        compiler_params=_cparams(("arbitrary",)),
    )(eta, h, abr, abi)


GC = 0.7978845608028654
GA = 0.044715


def s5_post(hc, z, dskip):
    def fn(hv, xd, dv):
        y = hv + dv * xd
        return y, 0.5 * y * (1.0 + jnp.tanh(GC * (y + GA * y * y * y)))
    return rw(fn, [(hc, 0, 512), (z, 3072, 512)], [(512, F32), (512, BF16)], "s5_post", S, consts=[dskip])


def s5_post_bwd(dyg, ypre, z, dskip):
    def fn(dy, y, xd, dv):
        th = jnp.tanh(GC * (y + GA * y * y * y))
        dg = 0.5 * (1.0 + th) + 0.5 * y * (1.0 - th * th) * GC * (1.0 + 3.0 * GA * y * y)
        dyp = dy * dg
        return dyp, dyp * dv, jnp.sum(dyp * xd, axis=0, keepdims=True)
    return rw(fn, [(dyg, 0, 512), (ypre, 0, 512), (z, 3072, 512)], [(512, BF16), (512, F32)],
              "s5_post_bwd", S, consts=[dskip], accs=[(1, 512)])


def glu_fwd(t, z, c_out):
    def fn(t1, t2, gd, co):
        return (jnp.concatenate([co, (t1 * _sig(t2) * (gd * _sig(gd))).astype(BF16)], axis=1),)
    return rw(fn, [(t, 0, 512), (t, 512, 512), (z, 3584, 512), (c_out, 0, D)], [(D + 512, BF16)], "glu_fwd", S)[0]


def glu_bwd(t, z, d_cat):
    def fn(t1, t2, gd, dd):
        s2, sg = _sig(t2), _sig(gd)
        sl = gd * sg
        return (jnp.concatenate([dd * s2 * sl, dd * t1 * s2 * (1.0 - s2) * sl], axis=1),
                dd * t1 * s2 * (sg * (1.0 + gd * (1.0 - sg))))
    return rw(fn, [(t, 0, 512), (t, 512, 512), (z, 3584, 512), (d_cat, 1024, 512)],
              [(D, BF16), (512, F32)], "glu_bwd", S)


def assemble_dz_odd(du, dv, dgc, dxd, dgd):
    def body(a, b, c, d, e, o_ref):
        o_ref[:, 0:D] = a[...].astype(BF16)
        o_ref[:, D:2 * D] = b[...].astype(BF16)
        o_ref[:, 2 * D:3 * D] = c[...].astype(BF16)
        o_ref[:, 3 * D:3 * D + 512] = d[...].astype(BF16)
        o_ref[:, 3 * D + 512:4 * D] = e[...].astype(BF16)
    tr = 256
    blk = pl.BlockSpec((tr, D), lambda i: (i, 0))
    half = pl.BlockSpec((tr, 512), lambda i: (i, 0))
    return pl.pallas_call(
        body, name="assemble_dz_odd", grid=(S // tr,), in_specs=[blk, blk, blk, half, half],
        out_specs=pl.BlockSpec((tr, 4 * D), lambda i: (i, 0)),
        out_shape=jax.ShapeDtypeStruct((S, 4 * D), BF16),
        compiler_params=_cparams(("parallel",)),
    )(du, dv, dgc, dxd, dgd)


TQ = 256


def _xattn_probs(qh, kh):
    s = _nt(qh, kh) * 0.0625
    p = jnp.exp(s - jnp.max(s, axis=-1, keepdims=True))
    return p / jnp.sum(p, axis=-1, keepdims=True)


def xattn_fwd(q, kv):
    def body(q_ref, kv_ref, o_ref):
        for h in range(4):
            sl = slice(h * 256, (h + 1) * 256)
            p = _xattn_probs(q_ref[:, sl].astype(BF16), kv_ref[:, sl].astype(BF16))
            vh = kv_ref[:, D + h * 256:D + (h + 1) * 256].astype(BF16)
            o_ref[:, sl] = _nn(p.astype(BF16), vh).astype(o_ref.dtype)

    return pl.pallas_call(
        body, name="xattn_fwd", grid=(S // TQ,),
        in_specs=[pl.BlockSpec((TQ, D), lambda i: (i, 0)), pl.BlockSpec((MEM, 2 * D), lambda i: (0, 0))],
        out_specs=pl.BlockSpec((TQ, D), lambda i: (i, 0)),
        out_shape=jax.ShapeDtypeStruct((S, D), BF16),
        compiler_params=_cparams(("parallel",)),
    )(q, kv)


def xattn_bwd(q, kv, d_o):
    def body(q_ref, kv_ref, do_ref, dq_ref, dkv_ref):
        @pl.when(pl.program_id(0) == 0)
        def _():
            dkv_ref[...] = jnp.zeros_like(dkv_ref)

        for h in range(4):
            sl = slice(h * 256, (h + 1) * 256)
            vs = slice(D + h * 256, D + (h + 1) * 256)
            qh = q_ref[:, sl].astype(BF16)
            kh = kv_ref[:, sl].astype(BF16)
            vh = kv_ref[:, vs].astype(BF16)
            doh = do_ref[:, sl].astype(BF16)
            p = _xattn_probs(qh, kh)
            dp = _nt(doh, vh)
            ds = (p * (dp - jnp.sum(p * dp, axis=-1, keepdims=True)) * 0.0625).astype(BF16)
            dq_ref[:, sl] = _nn(ds, kh).astype(dq_ref.dtype)
            dkv_ref[:, sl] += _tn(ds, qh)
            dkv_ref[:, vs] += _tn(p.astype(BF16), doh)

    return pl.pallas_call(
        body, name="xattn_bwd", grid=(S // TQ,),
        in_specs=[pl.BlockSpec((TQ, D), lambda i: (i, 0)), pl.BlockSpec((MEM, 2 * D), lambda i: (0, 0)),
                  pl.BlockSpec((TQ, D), lambda i: (i, 0))],
        out_specs=[pl.BlockSpec((TQ, D), lambda i: (i, 0)), pl.BlockSpec((MEM, 2 * D), lambda i: (0, 0))],
        out_shape=[jax.ShapeDtypeStruct((S, D), BF16), jax.ShapeDtypeStruct((MEM, 2 * D), F32)],
        compiler_params=_cparams(("arbitrary",)),
    )(q, kv, d_o)


def _s5_disc(a_re, a_im, log_dt, b_re, b_im):
    dt = jnp.exp(log_dt)[:, None]
    mag = jnp.exp(dt * a_re)
    abr = mag * jnp.cos(dt * a_im)
    abi = mag * jnp.sin(dt * a_im)
    nr, ni = abr - 1.0, abi
    inv = 1.0 / (a_re * a_re + a_im * a_im)
    cr = (nr * a_re + ni * a_im) * inv
    ci = (ni * a_re - nr * a_im) * inv
    bbr = cr[..., None] * b_re - ci[..., None] * b_im
    bbi = cr[..., None] * b_im + ci[..., None] * b_re
    return abr, abi, bbr, bbi


def _blockdiag(t):
    g, a, b = t.shape
    eye = jnp.eye(g, dtype=t.dtype)
    return (eye[:, None, :, None] * t[:, :, None, :]).reshape(g * a, g * b)


def _blocks(mat, a, b):
    return jnp.einsum("gagb->gab", mat.reshape(NG, a, NG, b))


def _fwd_even(i, x, P, W):
    hn = rms_fwd(x, P["norm_ab"][i:i + 1], "rms_ab_fwd")
    z = mm(m2(hn), W["w_in"], "nn", "in_ab")
    o, lse, cat = attn_fwd(z)
    if "more" in W:
        W.update(W.pop("more")(cat))
    cat = pool_fwd(z, W["pool_w"], P["pool_scale"][i:i + 1], cat)
    x_mid = mm(m2(cat), W["w_out"], "nn", "out_ab", add=m2(x))
    return x_mid, dict(x=x, hn=hn, z=z, o=o, lse=lse, cat=cat)


def _bwd_even(i, dx_mid, sv, P, W, G, GW):
    z = sv["z"]
    d_cat = mm(m2(dx_mid), W["w_out"], "nt", "out_ab_dx")
    GW["w_out"] = mm(m2(sv["cat"]), m2(dx_mid), "tn", "out_ab_dw").reshape(4, 512, D)
    dq, dk, dv, dga = attn_bwd(z, d_cat, sv["o"], sv["lse"])
    dvb, dgb, dpw, dps = pool_bwd(z, d_cat, W["pool_w"], P["pool_scale"][i:i + 1])
    GW["pool_w"] = dpw.reshape(4, 4, 64, 256).transpose(1, 0, 2, 3).reshape(4, 256, 256)
    G["pool_scale"][i] = dps[0]
    d_z = assemble_dz_even((dq, dk, dv, dga, dvb, dgb))
    d_hn = mm(m2(d_z), W["w_in"], "nt", "in_ab_dx")
    GW["w_in"] = mm(m2(sv["hn"]), m2(d_z), "tn", "in_ab_dw", out=outcs(D, 1536))
    return d_hn, P["norm_ab"][i:i + 1], "norm_ab", "rms_ab_bwd"


def _fwd_odd(i, x, P, W):
    hn = rms_fwd(x, P["norm_cd"][i:i + 1], "rms_cd_fwd")
    z = mm(m2(hn), W["w_in"], "nn", "in_cd")
    bfull = jnp.repeat(P["sgu_b"][i].T, 256, axis=1)
    c_out = sgu_fwd(z, P["sgu_ln_g"][i:i + 1], P["sgu_ln_b"][i:i + 1], P["sgu_w"][i], bfull)
    disc, disc_vjp = jax.vjp(_s5_disc, P["s5_a_re"][i], P["s5_a_im"][i], P["s5_log_dt"][i],
                             P["s5_b_re"][i], P["s5_b_im"][i])
    abr, abi, bbr, bbi = disc
    bbd = jnp.concatenate([_blockdiag(bbr.transpose(0, 2, 1)), _blockdiag(bbi.transpose(0, 2, 1))], axis=1)
    cbd = jnp.concatenate([_blockdiag(P["s5_c_re"][i].transpose(0, 2, 1)),
                           -_blockdiag(P["s5_c_im"][i].transpose(0, 2, 1))], axis=0)
    abr, abi = abr.reshape(1, NS), abi.reshape(1, NS)
    bu = mm(m2(z, 3072, 512), m2(bbd), "nn", "s5_bu")
    h = scan_fwd(bu, abr, abi)
    hc = mm(m2(h), m2(cbd), "nn", "s5_hc")
    dskip = P["s5_d"][i:i + 1]
    ypre, yg = s5_post(hc, z, dskip)
    if "more" in W:
        W.update(W.pop("more")(yg))
    w12 = W["w12"]
    t = mm(m2(yg), m2(w12), "nn", "glu_t")
    cat = glu_fwd(t, z, c_out)
    x_mid = mm(m2(cat), W["w_out"], "nn", "out_cd", add=m2(x))
    return x_mid, dict(x=x, hn=hn, z=z, bfull=bfull, disc_vjp=disc_vjp, bbd=bbd, cbd=cbd, abr=abr,
                       abi=abi, h=h, ypre=ypre, yg=yg, w12=w12, t=t, cat=cat, dskip=dskip)


def _bwd_odd(i, dx_mid, sv, P, W, G, GW):
    z = sv["z"]
    d_cat = mm(m2(dx_mid), W["w_out"], "nt", "out_cd_dx")
    GW["w_out"] = mm(m2(sv["cat"]), m2(dx_mid), "tn", "out_cd_dw").reshape(4, 384, D)
    du, dv, dgc, dws, dbs, dlg, dlb = sgu_bwd(z, d_cat, P["sgu_ln_g"][i:i + 1], P["sgu_ln_b"][i:i + 1],
                                               P["sgu_w"][i], sv["bfull"])
    G["sgu_w"][i], G["sgu_b"][i] = dws, dbs[:, :4].T
    G["sgu_ln_g"][i], G["sgu_ln_b"][i] = dlg[0], dlb[0]
    dt, dgd = glu_bwd(sv["t"], z, d_cat)
    gw12 = mm(m2(sv["yg"]), m2(dt), "tn", "glu_dw")
    GW["glu_w1"] = gw12[:, :512].reshape(4, 128, 512)
    GW["glu_w2"] = gw12[:, 512:].reshape(4, 128, 512)
    dyg = mm(m2(dt), m2(sv["w12"]), "nt", "glu_dx")
    dypre, dxd1, dd = s5_post_bwd(dyg, sv["ypre"], z, sv["dskip"])
    G["s5_d"][i] = dd[0]
    gcbd = mm(m2(sv["h"]), m2(dypre), "tn", "s5_dc")
    G["s5_c_re"][i] = _blocks(gcbd[:NS], NP, NH).transpose(0, 2, 1)
    G["s5_c_im"][i] = -_blocks(gcbd[NS:], NP, NH).transpose(0, 2, 1)
    eta = mm(m2(dypre), m2(sv["cbd"]), "nt", "s5_eta")
    lam, dacc = scan_bwd(eta, sv["h"], sv["abr"], sv["abi"])
    gbbd = mm(m2(z, 3072, 512), m2(lam), "tn", "s5_db")
    dxd = mm(m2(lam), m2(sv["bbd"]), "nt", "s5_dx", add=m2(dxd1))
    dacc = jnp.sum(dacc, axis=0)
    d_bbr = _blocks(gbbd[:, :NS], NH, NP).transpose(0, 2, 1)
    d_bbi = _blocks(gbbd[:, NS:], NH, NP).transpose(0, 2, 1)
    (G["s5_a_re"][i], G["s5_a_im"][i], G["s5_log_dt"][i], G["s5_b_re"][i], G["s5_b_im"][i]) = sv["disc_vjp"](
        (dacc[:NS].reshape(NG, NP), dacc[NS:].reshape(NG, NP), d_bbr, d_bbi))
    d_z = assemble_dz_odd(du, dv, dgc, dxd, dgd)
    d_hn = mm(m2(d_z), W["w_in"], "nt", "in_cd_dx")
    GW["w_in"] = mm(m2(sv["hn"]), m2(d_z), "tn", "in_cd_dw", out=outcs(D, 1024))
    return d_hn, P["norm_cd"][i:i + 1], "norm_cd", "rms_cd_bwd"


def _fwd_x(l, x, mem_n, P, W):
    hx = rms_fwd(x, P["norm_x"][l:l + 1], "rms_x_fwd")
    q = mm(m2(hx), W["w_xq"], "nn", "xq", out_dtype=BF16)
    kv = mm(m2(mem_n), W["w_xkv"], "nn", "xkv", out_dtype=BF16)
    ox = xattn_fwd(q, kv)
    x_out = mm(m2(ox), W["w_xo"], "nn", "xo", add=m2(x))
    return x_out, dict(x=x, hx=hx, q=q, kv=kv, ox=ox)


def _bwd_x(l, dx_out, sv, mem_n, d_memn, P, W, G, GW):
    d_ox = mm(m2(dx_out), W["w_xo"], "nt", "xo_dx", out_dtype=BF16)
    GW["w_xo"] = mm(m2(sv["ox"]), m2(dx_out), "tn", "xo_dw").reshape(4, 256, D)
    dq, dkv = xattn_bwd(sv["q"], sv["kv"], d_ox)
    GW["w_xq"] = mm(m2(sv["hx"]), m2(dq), "tn", "xq_dw").reshape(4, 256, D)
    d_hx = mm(m2(dq), W["w_xq"], "nt", "xq_dx")
    GW["w_xkv"] = mm(m2(mem_n), m2(dkv), "tn", "xkv_dw", out=outcs(D, 512))
    d_memn = mm(m2(dkv), W["w_xkv"], "nt", "xkv_dx", add=None if d_memn is None else m2(d_memn))
    dx, dg = rms_bwd(sv["x"], d_hx, dx_out, P["norm_x"][l:l + 1], "rms_x_bwd")
    G["norm_x"][l] = dg[0]
    return dx, d_memn


SMALL_LAYERS = (("norm_ab", 2), ("pool_scale", 2), ("norm_cd", 2), ("sgu_ln_g", 2), ("sgu_ln_b", 2), ("sgu_w", 2),
                ("sgu_b", 2), ("s5_a_re", 2), ("s5_a_im", 2), ("s5_log_dt", 2), ("s5_b_re", 2), ("s5_b_im", 2),
                ("s5_c_re", 2), ("s5_c_im", 2), ("s5_d", 2), ("norm_x", 4))


def local_step(x, mem, tgt, P, weights_of, grads_done):
    G = {k: [None] * n for k, n in SMALL_LAYERS}
    mem_g = P["mem_norm"].reshape(1, D)
    mem_n = rms_fwd(mem, mem_g, "rms_mem_fwd")
    saved = []
    for layer in range(4):
        i = layer // 2
        W = weights_of(layer, x)
        x, sv_m = (_fwd_even if layer % 2 == 0 else _fwd_odd)(i, x, P, W)
        x, sv_x = _fwd_x(layer, x, mem_n, P, W)
        saved.append((sv_m, sv_x, W))
    dx, loss, dgf = final_loss(x, tgt, P["final_norm"].reshape(1, D))
    G["final_norm"] = dgf[0]
    d_memn = None
    for layer in reversed(range(4)):
        i = layer // 2
        sv_m, sv_x, W = saved[layer]
        GW = {}
        dx_mid, d_memn = _bwd_x(layer, dx, sv_x, mem_n, d_memn, P, W, G, GW)
        d_hn, g, key, name = (_bwd_even if layer % 2 == 0 else _bwd_odd)(i, dx_mid, sv_m, P, W, G, GW)
        token = grads_done(layer, GW)
        if token is not None:
            g = g + token
        dx, dg = rms_bwd(sv_m["x"], d_hn, dx_mid, g, name)
        G[key][i] = dg[0]
    _, dgm = rms_bwd(mem, d_memn, d_memn, mem_g, "rms_mem_bwd")
    G["mem_norm"] = dgm[0]
    return loss, dx, G


ANY = pl.BlockSpec(memory_space=pl.ANY)


def _place():
    x, y, c = lax.axis_index("x"), lax.axis_index("y"), lax.axis_index("c")
    chips = [(1 - x, y), (x, 1 - y), (1 - x, 1 - y)]
    return x, y, c, 2 * x + y, (x, y, 1 - c), chips


def _remote(src, dst, send, recv, k, dev):
    return pltpu.make_async_remote_copy(src_ref=src, dst_ref=dst, send_sem=send.at[k], recv_sem=recv.at[k],
                                        device_id=dev, device_id_type=MESHID)


HBM = pl.BlockSpec(memory_space=pltpu.HBM)
SEM = pl.BlockSpec(memory_space=pltpu.SEMAPHORE)
EFFECT = pltpu.SideEffectType.DATAFLOW_SIDE_EFFECTING


def _hbm(t):
    return pltpu.with_memory_space_constraint(t, pltpu.HBM)


def allgather_sync(shards):
    n = len(shards)

    def body(*refs):
        ins, outs = refs[:n], refs[n:2 * n]
        token, send, recv = refs[2 * n:]
        x, y, c, jme, sib, chips = _place()
        first, passed = [], []
        for a in range(n):
            cp = _remote(ins[a], outs[a].at[jme], send, recv, a * 7 + 6, sib)
            cp.start()
            first.append(cp)
            for k, chip in enumerate(chips):
                cp = _remote(ins[a].at[c], outs[a].at[jme, c], send, recv, a * 7 + k, (*chip, c))
                cp.start()
                first.append(cp)
        for a in range(n):
            for k, chip in enumerate(chips):
                piece = outs[a].at[2 * chip[0] + chip[1], c]
                _remote(piece, piece, send, recv, a * 7 + k, (*chip, c)).wait_recv()
                fw = _remote(piece, piece, send, recv, a * 7 + 3 + k, sib)
                fw.start()
                passed.append(fw)
        for a in range(n):
            own = outs[a].at[jme]
            _remote(own, own, send, recv, a * 7 + 6, sib).wait_recv()
            for k, chip in enumerate(chips):
                piece = outs[a].at[2 * chip[0] + chip[1], 1 - c]
                _remote(piece, piece, send, recv, a * 7 + 3 + k, sib).wait_recv()
        for cp in first + passed:
            cp.wait_send()
        token[...] = jnp.zeros_like(token)

    res = pl.pallas_call(
        body, name="allgather_sync", in_specs=[ANY] * n,
        out_specs=[ANY] * n + [pl.BlockSpec(memory_space=pltpu.VMEM)],
        out_shape=[jax.ShapeDtypeStruct((4,) + s.shape, s.dtype) for s in shards] + [jax.ShapeDtypeStruct((8, 128), F32)],
        scratch_shapes=[pltpu.SemaphoreType.DMA((7 * n,)), pltpu.SemaphoreType.DMA((7 * n,))],
    )(*shards)
    return list(res[:n]), res[n]


def _gather_copies(ins, lands, send, recv):
    x, y, c, jme, sib, chips = _place()
    devs = [(*chip, c) for chip in chips] + [sib]
    return [_remote(ins[a], lands[a].at[jme], send, recv, a * 4 + k, dev)
            for a in range(len(ins)) for k, dev in enumerate(devs)]


def allgather_start(shards, after, name):
    n, na = len(shards), len(after)

    def body(*refs):
        ins, lands = refs[:n], refs[n:2 * n]
        send, recv = refs[2 * n + na], refs[2 * n + na + 1]
        token = refs[-1]
        for cp in _gather_copies(ins, lands, send, recv):
            cp.start()
        token[...] = jnp.zeros_like(token)

    res = pl.pallas_call(
        body, name=name,
        out_shape=(pltpu.SemaphoreType.DMA((4 * n,)), pltpu.SemaphoreType.DMA((4 * n,)),
                   *[pltpu.HBM(s.shape, s.dtype) for s in shards],
                   *[pltpu.HBM((4,) + s.shape, s.dtype) for s in shards],
                   jax.ShapeDtypeStruct((8, 128), F32)),
        in_specs=[HBM] * (2 * n) + [ANY] * na,
        out_specs=(SEM, SEM, *[HBM] * (2 * n), pl.BlockSpec(memory_space=pltpu.VMEM)),
        input_output_aliases={a: 2 + a for a in range(2 * n)},
        compiler_params=pltpu.CompilerParams(has_side_effects=EFFECT),
    )(*[_hbm(s) for s in shards], *[_hbm(lax.empty((4,) + s.shape, s.dtype)) for s in shards], *after)
    return res[0], res[1], list(res[2:2 + n]), list(res[2 + n:2 + 2 * n]), res[-1]


def allgather_wait(send, recv, shards, lands, after, name):
    n = len(shards)

    def body(*refs):
        ins, zones = refs[:n], refs[n:2 * n]
        send_r, recv_r = refs[2 * n], refs[2 * n + 1]
        x, y, c, jme, sib, chips = _place()
        slots = [2 * chip[0] + chip[1] for chip in chips] + [jme]
        for a in range(n):
            for k, slot in enumerate(slots):
                cp = _remote(ins[a], zones[a].at[slot], send_r, recv_r, a * 4 + k, sib)
                cp.wait_send()
                cp.wait_recv()

    res = pl.pallas_call(
        body, name=name,
        out_shape=tuple(pltpu.HBM(t.shape, t.dtype) for t in list(shards) + list(lands)),
        in_specs=[HBM] * (2 * n) + [SEM, SEM, ANY], out_specs=tuple([HBM] * (2 * n)),
        input_output_aliases={a: a for a in range(2 * n)},
        compiler_params=pltpu.CompilerParams(has_side_effects=EFFECT),
    )(*shards, *lands, send, recv, after)
    return list(res[n:])


def allgather_small(slab):
    def body(in_ref, out_ref, send, recv, lsem):
        x, y, c, jme, sib, chips = _place()
        loc = pltpu.make_async_copy(in_ref, out_ref.at[jme], lsem.at[0])
        loc.start()
        cps = [_remote(in_ref, out_ref.at[jme], send, recv, k, (*chip, c)) for k, chip in enumerate(chips)]
        for cp in cps:
            cp.start()
        for k, chip in enumerate(chips):
            piece = out_ref.at[2 * chip[0] + chip[1]]
            _remote(piece, piece, send, recv, k, (*chip, c)).wait_recv()
        for cp in cps:
            cp.wait_send()
        loc.wait()

    return pl.pallas_call(
        body, name="allgather_small", in_specs=[ANY], out_specs=ANY,
        out_shape=jax.ShapeDtypeStruct((4,) + slab.shape, slab.dtype),
        scratch_shapes=[pltpu.SemaphoreType.DMA((3,)), pltpu.SemaphoreType.DMA((3,)), pltpu.SemaphoreType.DMA((1,))],
    )(slab)


def allreduce_small(v):
    def body(v_ref, o_ref, r0, r1, r2, send, recv):
        x, y, c, jme, sib, chips = _place()
        peers = [sib, (1 - x, y, c), (x, 1 - y, c)]
        o_ref[...] = v_ref[...]
        for k, buf in enumerate((r0, r1, r2)):
            cp = _remote(o_ref, buf, send, recv, k, peers[k])
            cp.start()
            cp.wait()
            o_ref[...] = o_ref[...] + buf[...]

    vm = pl.BlockSpec(memory_space=pltpu.VMEM)
    return pl.pallas_call(
        body, name="allreduce_small", in_specs=[vm], out_specs=vm,
        out_shape=jax.ShapeDtypeStruct(v.shape, v.dtype),
        scratch_shapes=[pltpu.VMEM(v.shape, v.dtype)] * 3 + [pltpu.SemaphoreType.DMA((3,)), pltpu.SemaphoreType.DMA((3,))],
        compiler_params=pltpu.CompilerParams(vmem_limit_bytes=VMEM_LIMIT),
    )(v)


def rs_pair_exchange(gs):
    n = len(gs)

    def body(*refs):
        ins, outs = refs[:n], refs[n:2 * n]
        send, recv = refs[2 * n:]
        x, y, c, jme, sib, chips = _place()
        cps = [_remote(ins[a].at[:, 1 - c], outs[a], send, recv, a, sib) for a in range(n)]
        for cp in cps:
            cp.start()
        for cp in cps:
            cp.wait()

    return pl.pallas_call(
        body, name="rs_pair_exchange", in_specs=[ANY] * n, out_specs=[ANY] * n,
        out_shape=[jax.ShapeDtypeStruct((4,) + g.shape[2:], F32) for g in gs],
        scratch_shapes=[pltpu.SemaphoreType.DMA((n,)), pltpu.SemaphoreType.DMA((n,))],
    )(*gs)


def rs_pair_sum(g4, got, cidx):
    _, _, rh, cols = g4.shape
    tr = rh if rh <= 256 else 256

    def body(c_ref, a_ref, b_ref, o_ref):
        o_ref[...] = (a_ref[...] + b_ref[...]).astype(o_ref.dtype)

    return pl.pallas_call(
        body, name="rs_pair_sum",
        grid_spec=pltpu.PrefetchScalarGridSpec(
            num_scalar_prefetch=1, grid=(4, rh // tr),
            in_specs=[pl.BlockSpec((None, None, tr, cols), lambda j, t, cr: (j, cr[0], t, 0)),
                      pl.BlockSpec((None, tr, cols), lambda j, t, cr: (j, t, 0))],
            out_specs=pl.BlockSpec((None, tr, cols), lambda j, t, cr: (j, t, 0))),
        out_shape=jax.ShapeDtypeStruct((4, rh, cols), BF16),
        compiler_params=_cparams(("parallel", "parallel")),
    )(cidx, g4, got)


def _chip_copies(ps, lands, send, recv):
    x, y, c, jme, sib, chips = _place()
    return [_remote(ps[a].at[2 * chip[0] + chip[1]], lands[a].at[jme], send, recv, a * 3 + k, (*chip, c))
            for a in range(len(ps)) for k, chip in enumerate(chips)]


def rs_chip_start(ps, name):
    n = len(ps)

    def body(*refs):
        ins, lands = refs[:n], refs[n:2 * n]
        send, recv = refs[2 * n], refs[2 * n + 1]
        token = refs[-1]
        for cp in _chip_copies(ins, lands, send, recv):
            cp.start()
        token[...] = jnp.zeros_like(token)

    res = pl.pallas_call(
        body, name=name,
        out_shape=(pltpu.SemaphoreType.DMA((3 * n,)), pltpu.SemaphoreType.DMA((3 * n,)),
                   *[pltpu.HBM(p.shape, p.dtype) for p in ps], *[pltpu.HBM(p.shape, p.dtype) for p in ps],
                   jax.ShapeDtypeStruct((8, 128), F32)),
        in_specs=[HBM] * (2 * n), out_specs=(SEM, SEM, *[HBM] * (2 * n), pl.BlockSpec(memory_space=pltpu.VMEM)),
        input_output_aliases={a: 2 + a for a in range(2 * n)},
        compiler_params=pltpu.CompilerParams(has_side_effects=EFFECT),
    )(*[_hbm(p) for p in ps], *[_hbm(lax.empty(p.shape, p.dtype)) for p in ps])
    return res[0], res[1], list(res[2:2 + n]), list(res[2 + n:2 + 2 * n]), res[-1]


def rs_chip_wait(send, recv, ps, lands, after, name):
    n = len(ps)

    def body(*refs):
        ins, zones = refs[:n], refs[n:2 * n]
        send_r, recv_r = refs[2 * n], refs[2 * n + 1]
        x, y, c, jme, sib, chips = _place()
        for a in range(n):
            for k, chip in enumerate(chips):
                jt = 2 * chip[0] + chip[1]
                cp = _remote(ins[a].at[jt], zones[a].at[jt], send_r, recv_r, a * 3 + k, (*chip, c))
                cp.wait_send()
                cp.wait_recv()

    res = pl.pallas_call(
        body, name=name,
        out_shape=tuple(pltpu.HBM(p.shape, p.dtype) for p in list(ps) + list(lands)),
        in_specs=[HBM] * (2 * n) + [SEM, SEM, ANY], out_specs=tuple([HBM] * (2 * n)),
        input_output_aliases={a: a for a in range(2 * n)},
        compiler_params=pltpu.CompilerParams(has_side_effects=EFFECT),
    )(*ps, *lands, send, recv, after)
    return list(res[n:])


def rs_chip_sum(q, p, l, acc, layers, jc):
    _, rh, cols = q.shape
    tr = rh if rh <= 256 else 256

    def body(jc_ref, q_ref, p_ref, *rest):
        o_ref = rest[-1]
        jme = jc_ref[0]
        own = p_ref[...].astype(F32)
        v = [jnp.where(jme == j, own, q_ref[j].astype(F32)) for j in range(4)]
        o_ref[...] = ((v[0] + v[1]) + v[2]) + v[3]

    in_specs = [pl.BlockSpec((4, tr, cols), lambda t, jr: (0, t, 0)),
                pl.BlockSpec((None, tr, cols), lambda t, jr: (jr[0], t, 0))]
    args = [jc, q, p]
    if acc is not None:
        in_specs.append(ANY)
        args.append(acc)
    return pl.pallas_call(
        body, name="rs_chip_sum",
        grid_spec=pltpu.PrefetchScalarGridSpec(
            num_scalar_prefetch=1, grid=(rh // tr,), in_specs=in_specs,
            out_specs=pl.BlockSpec((None, None, tr, cols), lambda t, jr: (l, jr[1], t, 0))),
        out_shape=jax.ShapeDtypeStruct((layers, 2, rh, cols), F32),
        input_output_aliases={} if acc is None else {3: 0},
        compiler_params=_cparams(("parallel",)),
    )(*args)


def rs_pair_gather(rs):
    n = len(rs)

    def body(*refs):
        outs = refs[n:2 * n]
        send, recv = refs[2 * n:]
        x, y, c, jme, sib, chips = _place()
        cps = [_remote(outs[a].at[:, c], outs[a].at[:, c], send, recv, a, sib) for a in range(n)]
        for cp in cps:
            cp.start()
        for a in range(n):
            slot = outs[a].at[:, 1 - c]
            _remote(slot, slot, send, recv, a, sib).wait_recv()
        for cp in cps:
            cp.wait_send()

    return pl.pallas_call(
        body, name="rs_pair_gather", in_specs=[ANY] * n, out_specs=[ANY] * n,
        out_shape=[jax.ShapeDtypeStruct(r.shape, r.dtype) for r in rs],
        input_output_aliases={a: a for a in range(n)},
        scratch_shapes=[pltpu.SemaphoreType.DMA((n,)), pltpu.SemaphoreType.DMA((n,))],
    )(*rs)


def _adamw_math(w, g, m, v):
    m = B1 * m + (1.0 - B1) * g
    v = B2 * v + (1.0 - B2) * (g * g)
    m_hat = m / (1.0 - B1 ** STEP)
    v_hat = v / (1.0 - B2 ** STEP)
    return -LR * (m_hat / (jnp.sqrt(v_hat) + AEPS) + WD * w), m, v


def adamw(w, g, m, v, name):
    rows, cols = w.shape
    tr = 256 if rows % 256 == 0 else rows
    return rw(_adamw_math, [(a, 0, cols) for a in (w, g, m, v)], [(cols, F32)] * 3, name, rows, tr=tr)


WEIGHTS = ["norm_ab", "w_in_ab", "pool_w", "pool_scale", "w_out_ab", "norm_cd", "w_in_cd", "sgu_ln_g", "sgu_ln_b",
           "sgu_w", "sgu_b", "s5_a_re", "s5_a_im", "s5_log_dt", "s5_b_re", "s5_b_im", "s5_c_re", "s5_c_im", "s5_d",
           "glu_w1", "glu_w2", "w_out_cd", "norm_x", "w_xq", "w_xkv", "w_xo", "mem_norm", "final_norm"]
INPUTS = ["x", "mem"] + WEIGHTS + ["loss_target"] + ["m_" + n for n in WEIGHTS] + ["v_" + n for n in WEIGHTS]
BIG = ["w_in_ab", "w_out_ab", "w_in_cd", "w_out_cd", "w_xq", "w_xkv", "w_xo", "glu_w1", "glu_w2", "pool_w"]
COL_SHARDED = ("w_in_ab", "w_in_cd", "w_xkv")
SMALL = [n for n in WEIGHTS if n not in BIG]
SMALL_SHARDED = {"norm_cd": 256, "sgu_ln_g": 256, "sgu_ln_b": 256, "s5_d": 128}
PACK = 256 * 128


def _pack(arrs):
    flat = jnp.concatenate([a.reshape(-1) for a in arrs])
    pad = (-flat.shape[0]) % PACK
    return jnp.concatenate([flat, jnp.zeros((pad,), flat.dtype)]).reshape(-1, 128)


def _unpack(packed, shapes):
    flat, out, off = packed.reshape(-1), [], 0
    for s in shapes:
        n = 1
        for d in s:
            n *= d
        out.append(flat[off:off + n].reshape(s))
        off += n
    return out


LAYER_KEYS = (("w_in", "w_out", "pool_w", "w_xq", "w_xkv", "w_xo"),
              ("w_in", "w_out", "glu_w1", "glu_w2", "w_xq", "w_xkv", "w_xo"))


def _weight_of(key, layer):
    if key in ("w_xq", "w_xkv", "w_xo"):
        return key, layer, 4
    kind = "ab" if layer % 2 == 0 else "cd"
    return {"w_in": "w_in_" + kind, "w_out": "w_out_" + kind}.get(key, key), layer // 2, 2


def kernel(*args):
    a = dict(zip(INPUTS, args))
    x_i, y_i, c_i = lax.axis_index("x"), lax.axis_index("y"), lax.axis_index("c")
    j = 2 * x_i + y_i

    slab = jnp.concatenate([a["norm_cd"], a["sgu_ln_g"], a["sgu_ln_b"],
                            jnp.pad(a["s5_d"], ((0, 0), (0, 128)))], axis=0)
    gslab = allgather_small(slab)
    P = {n: a[n] for n in SMALL}
    for k, n in enumerate(("norm_cd", "sgu_ln_g", "sgu_ln_b", "s5_d")):
        wd = SMALL_SHARDED[n]
        P[n] = gslab[:, 2 * k:2 * k + 2, :wd].transpose(1, 0, 2).reshape(2, 4 * wd)

    def shards_of(layer):
        keys = sorted(k for k in LAYER_KEYS[layer % 2])
        out = []
        for k in keys:
            n, l, _ = _weight_of(k, layer)
            out.append(a[n][l].reshape(-1, a[n].shape[-1]).astype(BF16))
        return keys, out

    keys0, sh0 = shards_of(0)
    first = keys0.index("w_in")
    g_in, token = allgather_sync([sh0[first].reshape(2, sh0[first].shape[0] // 2, sh0[first].shape[1])])
    w_in0 = g_in[0].reshape(4, -1, g_in[0].shape[-1])
    started = {}
    for layer in (0, 1, 2, 3):
        keys, sh = (keys0, sh0) if layer == 0 else shards_of(layer)
        parts = [("", keys, sh)]
        if layer < 2:
            rest = [(k, s) for k, s in zip(keys, sh) if k != "w_in"]
            parts = [("in", ["w_in"], [sh[keys.index("w_in")]])] * (layer > 0) + [("", *map(list, zip(*rest)))]
        for tag, pk, ps in parts:
            send, recv, ps, lands, token = allgather_start(ps, [token, gslab], "allgather_start_%d%s" % (layer, tag))
            started[(layer, tag)] = (pk, send, recv, ps, lands)
    P["norm_ab"] = P["norm_ab"] + token[0:1, 0:1]

    cidx = jnp.reshape(c_i, (1,)).astype(jnp.int32)
    jc = jnp.stack([j, c_i]).astype(jnp.int32)

    def views(g):
        W = {}
        for k, v in g.items():
            if k in ("w_in", "w_xkv"):
                W[k] = mcs(v)
            elif k == "pool_w":
                W[k] = v.reshape(4, 4, 64, 256).transpose(1, 0, 2, 3).reshape(4, 256, 256)
            elif k not in ("glu_w1", "glu_w2"):
                W[k] = m2(v.reshape(-1, v.shape[-1]))
        if "glu_w1" in g:
            W["w12"] = jnp.concatenate([g["glu_w1"].reshape(512, 512), g["glu_w2"].reshape(512, 512)], axis=1)
        return W

    def arrived(layer, tag, after):
        keys, send, recv, sh, lands = started[(layer, tag)]
        return views(dict(zip(keys, allgather_wait(send, recv, sh, lands, after, "allgather_wait_%d%s" % (layer, tag)))))

    def weights_of(layer, x_in):
        if layer >= 2:
            return arrived(layer, "", x_in)
        W = views({"w_in": w_in0}) if layer == 0 else arrived(layer, "in", x_in)
        W["more"] = lambda after: arrived(layer, "", after)
        return W

    pending = {}

    def grads_done(layer, GW):
        keys = sorted(GW)
        flat = [GW[k].reshape(4, 2, GW[k].shape[1] // 2, GW[k].shape[2]) for k in keys]
        pair = [rs_pair_sum(g4, r, cidx) for g4, r in zip(flat, rs_pair_exchange(flat))]
        send, recv, pair, lands, token = rs_chip_start(pair, "rs_chip_start_%d" % layer)
        pending[layer] = (keys, send, recv, pair, lands)
        return token[0:1, 0:1]

    loss, dx, G = local_step(a["x"][0], a["mem"][0], a["loss_target"][0], P, weights_of, grads_done)
    loss = lax.psum(loss[0, 0], ("x", "y", "c"))

    red = {}
    for layer in (3, 2, 1, 0):
        keys, send, recv, pair, lands = pending[layer]
        lands = rs_chip_wait(send, recv, pair, lands, dx, "rs_chip_wait_%d" % layer)
        for k, q, p in zip(keys, lands, pair):
            n, l, layers = _weight_of(k, layer)
            red[n] = rs_chip_sum(q, p, l, red.get(n), layers, jc)
    gbig = dict(zip(BIG, rs_pair_gather([red[n] for n in BIG])))

    outs = {}
    for n in BIG:
        shp = a[n].shape
        g2 = gbig[n].reshape(-1, shp[-1])
        d2, m2_, v2_ = adamw(a[n].reshape(g2.shape), g2, a["m_" + n].reshape(g2.shape),
                             a["v_" + n].reshape(g2.shape), "adamw_" + n)
        outs[n] = tuple(t.reshape(shp) for t in (g2, d2, m2_, v2_))

    gfull = [jnp.stack(G[n]) if isinstance(G[n], list) else G[n] for n in SMALL]
    shapes = [g.shape for g in gfull]
    gsum = _unpack(allreduce_small(_pack(gfull)), shapes)
    gloc = []
    for n, g in zip(SMALL, gsum):
        if n in SMALL_SHARDED:
            g = lax.dynamic_slice_in_dim(g, j * SMALL_SHARDED[n], SMALL_SHARDED[n], axis=1)
        gloc.append(g)
    for n, g in zip(SMALL, gloc):
        shp = a[n].shape
        two = (-1, shp[-1]) if len(shp) > 1 else (1, shp[0])
        upd = adamw(a[n].reshape(two), g.reshape(two), a["m_" + n].reshape(two), a["v_" + n].reshape(two), "adamw_" + n)
        outs[n] = (g,) + tuple(t.reshape(shp) for t in upd)

    res = [loss, dx[None]]
    for part in range(4):
        res += [outs[n][part] for n in WEIGHTS]
    return tuple(res)
```

```python
import math

import jax
import jax.numpy as jnp
from jax import lax
from jax.experimental import pallas as pl
from jax.experimental.pallas import tpu as pltpu

F32, BF16 = jnp.float32, jnp.bfloat16
S, D = 2048, 1024
MEM = 256
EPS = 1e-6
NEG = -1e30
QB = 128
PATTERNS = (1, 4, 16)
NG, NP, NH = 32, 64, 16
NS = NG * NP
LR, B1, B2, AEPS, WD, STEP = 0.001, 0.9, 0.999, 1e-08, 0.01, 10
MESHID = pl.DeviceIdType.MESH
VMEM_LIMIT = 56 * 1024 * 1024


def _cparams(sem):
    return pltpu.CompilerParams(dimension_semantics=sem, vmem_limit_bytes=VMEM_LIMIT)


def _sig(x):
    return 1.0 / (1.0 + jnp.exp(-x))


def _dot(a, b, dims):
    return lax.dot_general(a, b, (dims, ((), ())), preferred_element_type=F32)


def _nn(a, b):
    return _dot(a, b, ((1,), (0,)))


def _nt(a, b):
    return _dot(a, b, ((1,), (1,)))


def _tn(a, b):
    return _dot(a, b, ((0,), (0,)))


_DIMS = {"nn": ((1,), (0,)), "nt": ((1,), (1,)), "tn": ((0,), (0,))}


def _tile(dim, cc=None, cap=1024):
    for t in (2048, 1536, 1024, 768, 512, 384, 256, 128):
        if t <= cap and dim % t == 0 and (cc is None or cc % t == 0):
            return t
    return dim


MM_VMEM = 36 * 1024 * 1024


def _mm_tiles(m, n, k, ccm, ccn, cck, a_bytes, b_bytes, o_bytes):
    caps = [1024, 1024, 2048]
    while True:
        tm, tn, tk = _tile(m, ccm, caps[0]), _tile(n, ccn, caps[1]), _tile(k, cck, caps[2])
        need = 2 * (tm * tk * a_bytes + tk * tn * b_bytes + tm * tn * o_bytes) + (tm * tn * 4 if tk < k else 0)
        if need <= MM_VMEM:
            return tm, tn, tk
        if tk > 1024:
            caps[2] = tk // 2
        elif tn >= tm:
            caps[1] = tn // 2
        else:
            caps[0] = tm // 2


def m2(arr, col_off=0, ncols=None):
    rows, cols = arr.shape
    ncols = cols - col_off if ncols is None else ncols

    def spec(tr, tc, rc):
        assert col_off % tc == 0
        return pl.BlockSpec((tr, tc), lambda *g: (rc(*g)[0], rc(*g)[1] + col_off // tc))
    return (arr, rows, ncols, spec, None if col_off == 0 else col_off)


def mcs(arr):
    cs = arr.shape[2]

    def spec(tr, tc, rc):
        n = cs // tc
        return pl.BlockSpec((None, tr, tc), lambda *g: (rc(*g)[1] // n, rc(*g)[0], rc(*g)[1] % n))
    return (arr, arr.shape[1], 4 * cs, spec, cs)


def out2(rows, cols):
    def spec(tr, tc, rc):
        return pl.BlockSpec((tr, tc), lambda *g: tuple(rc(*g)))
    return ((rows, cols), spec, None)


def outcs(rows, cs):
    def spec(tr, tc, rc):
        n = cs // tc
        return pl.BlockSpec((None, tr, tc), lambda *g: (rc(*g)[1] // n, rc(*g)[0], rc(*g)[1] % n))
    return ((4, rows, cs), spec, cs)


def _both(a, b):
    if a is None:
        return b
    if b is None:
        return a
    return math.gcd(a, b)


def mm(a, b, mode, name, add=None, out=None, out_dtype=F32):
    a_arr, a_r, a_c, a_spec, a_cc = a
    b_arr, b_r, b_c, b_spec, b_cc = b
    if mode == "nn":
        m, k, n = a_r, a_c, b_c
        assert b_r == k
        ccm, cck, ccn = None, a_cc, b_cc
    elif mode == "nt":
        m, k, n = a_r, a_c, b_r
        assert b_c == k
        ccm, cck, ccn = None, _both(a_cc, b_cc), None
    else:
        m, k, n = a_c, a_r, b_c
        assert b_r == k
        ccm, cck, ccn = a_cc, None, b_cc
    out = out2(m, n) if out is None else out
    o_shape, o_spec, o_cc = out
    ccn = _both(ccn, o_cc)
    if add is not None:
        ccn = _both(ccn, add[4])
    o_bytes = jnp.dtype(out_dtype).itemsize + (0 if add is None else add[0].dtype.itemsize)
    tm, tn, tk = _mm_tiles(m, n, k, ccm, ccn, cck, a_arr.dtype.itemsize, b_arr.dtype.itemsize, o_bytes)
    nk = k // tk
    if mode == "nn":
        in_specs = [a_spec(tm, tk, lambda i, j, kk: (i, kk)), b_spec(tk, tn, lambda i, j, kk: (kk, j))]
    elif mode == "nt":
        in_specs = [a_spec(tm, tk, lambda i, j, kk: (i, kk)), b_spec(tn, tk, lambda i, j, kk: (j, kk))]
    else:
        in_specs = [a_spec(tk, tm, lambda i, j, kk: (kk, i)), b_spec(tk, tn, lambda i, j, kk: (kk, j))]
    args = [a_arr, b_arr]
    if add is not None:
        in_specs.append(add[3](tm, tn, lambda i, j, kk: (i, j)))
        args.append(add[0])
    dims = _DIMS[mode]
    has_add = add is not None

    def body(*refs):
        a_ref, b_ref = refs[0], refs[1]
        add_ref = refs[2] if has_add else None
        prod = _dot(a_ref[...].astype(BF16), b_ref[...].astype(BF16), dims)
        if nk == 1:
            o_ref = refs[-1]
            if has_add:
                prod = prod + add_ref[...].astype(F32)
            o_ref[...] = prod.astype(o_ref.dtype)
            return
        o_ref, acc = refs[-2], refs[-1]
        kk = pl.program_id(2)

        @pl.when(kk == 0)
        def _():
            acc[...] = prod

        @pl.when(kk > 0)
        def _():
            acc[...] += prod

        @pl.when(kk == nk - 1)
        def _():
            r = acc[...]
            if has_add:
                r = r + add_ref[...].astype(F32)
            o_ref[...] = r.astype(o_ref.dtype)

    return pl.pallas_call(
        body, name=name, grid=(m // tm, n // tn, nk), in_specs=in_specs,
        out_specs=o_spec(tm, tn, lambda i, j, kk: (i, j)),
        out_shape=jax.ShapeDtypeStruct(o_shape, out_dtype),
        scratch_shapes=[pltpu.VMEM((tm, tn), F32)] if nk > 1 else [],
        compiler_params=_cparams(("parallel", "parallel", "arbitrary")),
    )(*args)


def rw(fn, ins, outs, name, rows, tr=256, consts=(), accs=()):
    n_in, n_c, n_o, n_a = len(ins), len(consts), len(outs), len(accs)
    in_specs = []
    for arr, off, width in ins:
        assert off % width == 0
        in_specs.append(pl.BlockSpec((tr, width), lambda i, o=off // width: (i, o)))
    for c in consts:
        in_specs.append(pl.BlockSpec(c.shape, lambda i: (0, 0)))
    out_specs = [pl.BlockSpec((tr, w), lambda i: (i, 0)) for w, _ in outs]
    out_specs += [pl.BlockSpec(s, lambda i: (0, 0)) for s in accs]
    out_shape = [jax.ShapeDtypeStruct((rows, w), dt) for w, dt in outs]
    out_shape += [jax.ShapeDtypeStruct(s, F32) for s in accs]

    def body(*refs):
        vals = [r[...] for r in refs[:n_in + n_c]]
        o_refs = refs[n_in + n_c:n_in + n_c + n_o]
        a_refs = refs[n_in + n_c + n_o:]
        res = fn(*vals)
        for r, v in zip(o_refs, res[:n_o]):
            r[...] = v.astype(r.dtype)
        if n_a:
            @pl.when(pl.program_id(0) == 0)
            def _():
                for r in a_refs:
                    r[...] = jnp.zeros_like(r)
            for r, v in zip(a_refs, res[n_o:]):
                r[...] += v

    res = pl.pallas_call(
        body, name=name, grid=(rows // tr,), in_specs=in_specs, out_specs=out_specs,
        out_shape=out_shape,
        compiler_params=_cparams(("arbitrary",) if n_a else ("parallel",)),
    )(*[a for a, _, _ in ins], *consts)
    return res


def _rstd(x):
    return lax.rsqrt(jnp.mean(x * x, axis=-1, keepdims=True) + EPS)


def rms_fwd(x, g, name):
    def fn(xv, gv):
        xv = xv.astype(F32)
        return (xv * _rstd(xv) * gv,)
    return rw(fn, [(x, 0, D)], [(D, BF16)], name, x.shape[0], consts=[g])[0]


def _rms_bwd_math(xv, dy, gv):
    r = _rstd(xv)
    dyg = dy * gv
    dx = r * dyg - xv * (r * r * r / D) * jnp.sum(dyg * xv, axis=-1, keepdims=True)
    dg = jnp.sum(dy * xv * r, axis=0, keepdims=True)
    return dx, dg


def rms_bwd(x, dy, dres, g, name):
    def fn(xv, dyv, drv, gv):
        dx, dg = _rms_bwd_math(xv, dyv, gv)
        return dx + drv, dg
    return rw(fn, [(x, 0, D), (dy, 0, D), (dres, 0, D)], [(D, F32)], name, x.shape[0],
              consts=[g], accs=[(1, D)])


def final_loss(x, tgt, g):
    def fn(xv, tv, gv):
        e = xv * _rstd(xv) * gv - tv
        loss = 0.5 * jnp.sum(jnp.sum(e * e, axis=-1, keepdims=True), axis=0, keepdims=True) / D
        dx, dg = _rms_bwd_math(xv, e / D, gv)
        return dx, loss, dg
    return rw(fn, [(x, 0, D), (tgt, 0, D)], [(D, F32)], "final_loss", S, consts=[g],
              accs=[(1, 1), (1, D)])


def _attn_bias(bias_ref):
    ii = lax.broadcasted_iota(jnp.int32, (2 * QB, 2 * QB), 0) % QB
    jj = lax.broadcasted_iota(jnp.int32, (2 * QB, 2 * QB), 1)
    dist = ii + QB - jj
    band = (dist >= 0) & (dist <= QB)
    bias_ref[1] = jnp.where(band, 0.0, NEG)
    bias_ref[0] = jnp.where(band & (jj >= QB), 0.0, NEG)


def _two_heads(x, m0):
    return jnp.concatenate([jnp.where(m0, x, 0.0), jnp.where(m0, 0.0, x)], axis=0)


def _per_head(col, m0):
    return jnp.where(m0, col[:QB], col[QB:])


def _attn_rows(idx, d):
    if d == 1:
        b = idx
        cur = pl.ds(pl.multiple_of(b * QB, QB), QB)
        prev = pl.ds(pl.multiple_of(jnp.maximum(b - 1, 0) * QB, QB), QB)
    else:
        r, b = lax.rem(idx, d), lax.div(idx, d)
        cur = pl.ds(r + b * (QB * d), QB, stride=d)
        prev = pl.ds(r + jnp.maximum(b - 1, 0) * (QB * d), QB, stride=d)
    return cur, prev, b


NBLK = S // QB
GROUP = 4


def _colblk(off):
    return pl.BlockSpec((S, 128), lambda hp: (0, off * 8 + hp))


def attn_fwd(z):
    def body(q_ref, k_ref, v_ref, g_ref, o_ref, l_ref, a_ref, os, ls, bias):
        _attn_bias(bias)
        m0 = lax.broadcasted_iota(jnp.int32, (1, 128), 1) < 64
        for pi, d in enumerate(PATTERNS):
            def load(idx, d=d):
                cur, prev, b = _attn_rows(idx, d)
                return cur, (q_ref[cur, :], k_ref[prev, :], k_ref[cur, :], v_ref[prev, :], v_ref[cur, :],
                             bias[jnp.minimum(b, 1)])

            def block(q, kp, kc, vp, vc, bs):
                qq = _two_heads(q * 0.125, m0).astype(BF16)
                k = jnp.concatenate([kp, kc], axis=0).astype(BF16)
                s = _nt(qq, k) + bs
                mx = jnp.max(s, axis=-1, keepdims=True)
                p = jnp.exp(s - mx)
                den = jnp.sum(p, axis=-1, keepdims=True)
                pb = p.astype(BF16)
                vv = _two_heads(jnp.concatenate([vp, vc], axis=0), m0).astype(BF16)
                o = _nn(jnp.concatenate([pb[:QB], pb[QB:]], axis=1), vv)
                return o * _per_head(1.0 / den, m0), _per_head(mx + jnp.log(den), m0)

            def step(i, carry, pi=pi):
                loaded = [load(i * GROUP + u) for u in range(GROUP)]
                done = [block(*vals) for _, vals in loaded]
                for (cur, _), (o, l) in zip(loaded, done):
                    os[pi, cur, :] = o
                    ls[pi, cur, :] = l
                return carry
            lax.fori_loop(0, NBLK // GROUP, step, 0)
        l1, l2, l3 = ls[0], ls[1], ls[2]
        mx = jnp.maximum(jnp.maximum(l1, l2), l3)
        e1, e2, e3 = jnp.exp(l1 - mx), jnp.exp(l2 - mx), jnp.exp(l3 - mx)
        tot = e1 + e2 + e3
        o = (os[0] * e1 + os[1] * e2 + os[2] * e3) / tot
        ga = g_ref[...]
        o_ref[...] = o
        l_ref[...] = mx + jnp.log(tot)
        a_ref[...] = (o * (ga * _sig(ga))).astype(a_ref.dtype)

    out = pl.BlockSpec((S, 128), lambda hp: (0, hp))
    return pl.pallas_call(
        body, name="attn_fwd", grid=(8,),
        in_specs=[_colblk(0), _colblk(1), _colblk(2), _colblk(3)], out_specs=[out] * 3,
        out_shape=[jax.ShapeDtypeStruct((S, D), F32), jax.ShapeDtypeStruct((S, D), F32),
                   jax.ShapeDtypeStruct((S, 2 * D), BF16)],
        scratch_shapes=[pltpu.VMEM((3, S, 128), F32), pltpu.VMEM((3, S, 128), F32),
                        pltpu.VMEM((2, 2 * QB, 2 * QB), F32)],
        compiler_params=_cparams(("parallel",)),
    )(z, z, z, z)


def attn_bwd(z, d_cat, o, lse):
    def body(q_ref, k_ref, v_ref, g_ref, da_ref, o_ref, l_ref, dq_ref, dk_ref, dv_ref, dg_ref, do_s, pr_s, bias):
        _attn_bias(bias)
        m0 = lax.broadcasted_iota(jnp.int32, (1, 128), 1) < 64
        ga = g_ref[...]
        sg = _sig(ga)
        da = da_ref[...]
        ov = o_ref[...]
        do = da * (ga * sg)
        dg_ref[...] = da * ov * (sg * (1.0 + ga * (1.0 - sg)))
        do_s[...] = do
        pr_s[...] = do * ov
        dq_ref[...] = jnp.zeros_like(dq_ref)
        dk_ref[...] = jnp.zeros_like(dk_ref)
        dv_ref[...] = jnp.zeros_like(dv_ref)
        for d in PATTERNS:
            def load(idx, d=d):
                cur, prev, b = _attn_rows(idx, d)
                return (cur, prev), (q_ref[cur, :], k_ref[prev, :], k_ref[cur, :], v_ref[prev, :], v_ref[cur, :],
                                     do_s[cur, :], pr_s[cur, :], l_ref[cur, :], bias[jnp.minimum(b, 1)])

            def block(q, kp, kc, vp, vc, dof, prod, lp, bs):
                qq = _two_heads(q * 0.125, m0).astype(BF16)
                kf = jnp.concatenate([kp, kc], axis=0)
                k = kf.astype(BF16)
                v = jnp.concatenate([vp, vc], axis=0).astype(BF16)
                dd = _two_heads(dof, m0).astype(BF16)
                lh = jnp.max(jnp.concatenate([jnp.where(m0, lp, -jnp.inf), jnp.where(m0, -jnp.inf, lp)], axis=0),
                             axis=-1, keepdims=True)
                delta = jnp.sum(_two_heads(prod, m0), axis=-1, keepdims=True)
                p = jnp.exp(_nt(qq, k) + bs - lh)
                ds = (p * (_nt(dd, v) - delta)).astype(BF16)
                dq = _nn(jnp.concatenate([ds[:QB], ds[QB:]], axis=1), _two_heads(kf, m0).astype(BF16))
                return dq * 0.125, _tn(ds, qq), _tn(p.astype(BF16), dd)

            def step(i, carry):
                loaded = [load(i * GROUP + u) for u in range(GROUP)]
                done = [block(*vals) for _, vals in loaded]
                for ((cur, prev), _), (dq, dk, dv) in zip(loaded, done):
                    dq_ref[cur, :] = dq_ref[cur, :] + dq
                    dk_ref[prev, :] = dk_ref[prev, :] + dk[:QB]
                    dv_ref[prev, :] = dv_ref[prev, :] + dv[:QB]
                    dk_ref[cur, :] = dk_ref[cur, :] + dk[QB:]
                    dv_ref[cur, :] = dv_ref[cur, :] + dv[QB:]
                return carry
            lax.fori_loop(0, NBLK // GROUP, step, 0)

    blk = pl.BlockSpec((S, 128), lambda hp: (0, hp))
    return pl.pallas_call(
        body, name="attn_bwd", grid=(8,),
        in_specs=[_colblk(0), _colblk(1), _colblk(2), _colblk(3), blk, blk, blk], out_specs=[blk] * 4,
        out_shape=[jax.ShapeDtypeStruct((S, D), F32)] * 4,
        scratch_shapes=[pltpu.VMEM((S, 128), F32), pltpu.VMEM((S, 128), F32), pltpu.VMEM((2, 2 * QB, 2 * QB), F32)],
        compiler_params=_cparams(("parallel",)),
    )(z, z, z, z, d_cat, o, lse)


def assemble_dz_even(parts):
    def body(*refs):
        o_ref = refs[-1]
        for j in range(6):
            o_ref[:, j * D:(j + 1) * D] = refs[j][...].astype(o_ref.dtype)
    tr = 256
    blk = pl.BlockSpec((tr, D), lambda i: (i, 0))
    return pl.pallas_call(
        body, name="assemble_dz_even", grid=(S // tr,), in_specs=[blk] * 6,
        out_specs=pl.BlockSpec((tr, 6 * D), lambda i: (i, 0)),
        out_shape=jax.ShapeDtypeStruct((S, 6 * D), BF16),
        compiler_params=_cparams(("parallel",)),
    )(*parts)


def _pool_window(g):
    return jnp.where(g == 0, 2.0, jnp.where(g == 1, 4.0, jnp.where(g == 2, 8.0, 16.0)))


def _pool_sel(g, levels):
    return jnp.where(g == 0, levels[0], jnp.where(g == 1, levels[1], jnp.where(g == 2, levels[2], levels[3])))


def _pool_fwd_math(v, g):
    t = lax.broadcasted_iota(jnp.int32, (S, 1), 0)
    s = v
    levels = []
    for k in (1, 2, 4, 8):
        s = s + jnp.where(t >= k, pltpu.roll(s, k, 0), 0.0)
        levels.append(s)
    cnt = jnp.minimum((t + 1).astype(F32), _pool_window(g))
    return _pool_sel(g, levels) / cnt - v, cnt


def pool_fwd(z, pw, ps, cat):
    def body(v_ref, g_ref, pw_ref, ps_ref, cat_ref, o_ref):
        g = pl.program_id(0)
        pooled, _ = _pool_fwd_math(v_ref[...], g)
        mixed = _nn(pooled.astype(BF16), pw_ref[...].astype(BF16))
        gb = g_ref[...]
        o_ref[...] = (mixed * ps_ref[...] * (gb * _sig(gb))).astype(o_ref.dtype)

    return pl.pallas_call(
        body, name="pool_fwd", grid=(4,),
        in_specs=[pl.BlockSpec((S, 256), lambda g: (0, 16 + g)),
                  pl.BlockSpec((S, 256), lambda g: (0, 20 + g)),
                  pl.BlockSpec((None, 256, 256), lambda g: (g, 0, 0)),
                  pl.BlockSpec((1, 256), lambda g: (0, g)), pl.BlockSpec(memory_space=pl.ANY)],
        out_specs=pl.BlockSpec((S, 256), lambda g: (0, 4 + g)),
        out_shape=jax.ShapeDtypeStruct((S, 2 * D), BF16),
        input_output_aliases={4: 0},
        compiler_params=_cparams(("parallel",)),
    )(z, z, pw, ps, cat)


def pool_bwd(z, d_cat, pw, ps):
    def body(v_ref, g_ref, d_ref, pw_ref, ps_ref, dv_ref, dg_ref, dpw_ref, dps_ref):
        g = pl.program_id(0)
        v = v_ref[...]
        pooled, cnt = _pool_fwd_math(v, g)
        pwb = pw_ref[...].astype(BF16)
        pb = pooled.astype(BF16)
        mixed = _nn(pb, pwb)
        gb = g_ref[...]
        sg = _sig(gb)
        dout = d_ref[...]
        sc = ps_ref[...]
        dg_ref[...] = dout * mixed * sc * (sg * (1.0 + gb * (1.0 - sg)))
        dms = dout * (gb * sg)
        dps_ref[...] = jnp.sum(dms * mixed, axis=0, keepdims=True)
        dmx = (dms * sc).astype(BF16)
        dpw_ref[...] = _tn(pb, dmx)
        dpooled = _nt(dmx, pwb)
        t = lax.broadcasted_iota(jnp.int32, (S, 1), 0)
        s = dpooled / cnt
        levels = []
        for k in (1, 2, 4, 8):
            s = s + jnp.where(t < S - k, pltpu.roll(s, S - k, 0), 0.0)
            levels.append(s)
        dv_ref[...] = _pool_sel(g, levels) - dpooled

    return pl.pallas_call(
        body, name="pool_bwd", grid=(4,),
        in_specs=[pl.BlockSpec((S, 256), lambda g: (0, 16 + g)),
                  pl.BlockSpec((S, 256), lambda g: (0, 20 + g)),
                  pl.BlockSpec((S, 256), lambda g: (0, 4 + g)),
                  pl.BlockSpec((None, 256, 256), lambda g: (g, 0, 0)),
                  pl.BlockSpec((1, 256), lambda g: (0, g))],
        out_specs=[pl.BlockSpec((S, 256), lambda g: (0, g)),
                   pl.BlockSpec((S, 256), lambda g: (0, g)),
                   pl.BlockSpec((None, 256, 256), lambda g: (g, 0, 0)),
                   pl.BlockSpec((1, 256), lambda g: (0, g))],
        out_shape=[jax.ShapeDtypeStruct((S, D), F32), jax.ShapeDtypeStruct((S, D), F32),
                   jax.ShapeDtypeStruct((4, 256, 256), F32), jax.ShapeDtypeStruct((1, D), F32)],
        compiler_params=_cparams(("parallel",)),
    )(z, z, d_cat, pw, ps)


CH = 128


def _sgu_common(v, lng, lnb, w_ref):
    mu = jnp.mean(v, axis=-1, keepdims=True)
    vc = v - mu
    rs = lax.rsqrt(jnp.mean(vc * vc, axis=-1, keepdims=True) + EPS)
    xhat = vc * rs
    vn = (xhat * lng + lnb).astype(BF16)
    ri = lax.broadcasted_iota(jnp.int32, (CH, CH), 0)
    ci = lax.broadcasted_iota(jnp.int32, (CH, CH), 1)
    tril = ri >= ci
    ws = [jnp.where(tril, w_ref[g], 0.0).astype(BF16) for g in range(4)]
    return xhat, rs, vn, tril, ws


def _zspec(off):
    return pl.BlockSpec((CH, D), lambda c: (c, off))


def _full(shape):
    return pl.BlockSpec(shape, lambda c: (0,) * len(shape))


def sgu_fwd(z, lng, lnb, w, bfull):
    def body(u_ref, v_ref, g_ref, lng_ref, lnb_ref, w_ref, b_ref, o_ref):
        _, _, vn, _, ws = _sgu_common(v_ref[...], lng_ref[...], lnb_ref[...], w_ref)
        for g in range(4):
            sl = slice(g * 256, (g + 1) * 256)
            mixed = _nn(ws[g], vn[:, sl]) + b_ref[:, sl]
            gc = g_ref[:, sl]
            o_ref[:, sl] = (u_ref[:, sl] * mixed * (gc * _sig(gc))).astype(o_ref.dtype)

    return pl.pallas_call(
        body, name="sgu_fwd", grid=(S // CH,),
        in_specs=[_zspec(0), _zspec(1), _zspec(2), _full((1, D)), _full((1, D)),
                  _full((4, CH, CH)), _full((CH, D))],
        out_specs=pl.BlockSpec((CH, D), lambda c: (c, 0)),
        out_shape=jax.ShapeDtypeStruct((S, D), BF16),
        compiler_params=_cparams(("parallel",)),
    )(z, z, z, lng, lnb, w, bfull)


def sgu_bwd(z, d_cat, lng, lnb, w, bfull):
    def body(u_ref, v_ref, g_ref, d_ref, lng_ref, lnb_ref, w_ref, b_ref,
             du_ref, dv_ref, dg_ref, dw_ref, db_ref, dlg_ref, dlb_ref):
        @pl.when(pl.program_id(0) == 0)
        def _():
            dw_ref[...] = jnp.zeros_like(dw_ref)
            db_ref[...] = jnp.zeros_like(db_ref)
            dlg_ref[...] = jnp.zeros_like(dlg_ref)
            dlb_ref[...] = jnp.zeros_like(dlb_ref)

        lng = lng_ref[...]
        xhat, rs, vn, tril, ws = _sgu_common(v_ref[...], lng, lnb_ref[...], w_ref)
        lane = lax.broadcasted_iota(jnp.int32, (1, 128), 1)
        db = jnp.zeros((CH, 128), F32)
        dvn_parts = []
        for g in range(4):
            sl = slice(g * 256, (g + 1) * 256)
            mixed = _nn(ws[g], vn[:, sl]) + b_ref[:, sl]
            gc = g_ref[:, sl]
            sg = _sig(gc)
            u = u_ref[:, sl]
            dc = d_ref[:, sl]
            du_ref[:, sl] = dc * mixed * (gc * sg)
            dg_ref[:, sl] = dc * u * mixed * (sg * (1.0 + gc * (1.0 - sg)))
            dmx = dc * u * (gc * sg)
            db = db + jnp.where(lane == g, jnp.sum(dmx, axis=-1, keepdims=True), 0.0)
            dmb = dmx.astype(BF16)
            dw_ref[g] += jnp.where(tril, _nt(dmb, vn[:, sl]), 0.0)
            dvn_parts.append(_tn(ws[g], dmb))
        db_ref[...] += db
        dvn = jnp.concatenate(dvn_parts, axis=1)
        dlb_ref[...] += jnp.sum(dvn, axis=0, keepdims=True)
        dlg_ref[...] += jnp.sum(dvn * xhat, axis=0, keepdims=True)
        dxh = dvn * lng
        dv_ref[...] = rs * (dxh - jnp.mean(dxh, axis=-1, keepdims=True)
                            - xhat * jnp.mean(dxh * xhat, axis=-1, keepdims=True))

    row = pl.BlockSpec((CH, D), lambda c: (c, 0))
    return pl.pallas_call(
        body, name="sgu_bwd", grid=(S // CH,),
        in_specs=[_zspec(0), _zspec(1), _zspec(2), row, _full((1, D)), _full((1, D)),
                  _full((4, CH, CH)), _full((CH, D))],
        out_specs=[row, row, row, _full((4, CH, CH)), _full((CH, 128)), _full((1, D)), _full((1, D))],
        out_shape=[jax.ShapeDtypeStruct((S, D), F32)] * 3
        + [jax.ShapeDtypeStruct((4, CH, CH), F32), jax.ShapeDtypeStruct((CH, 128), F32),
           jax.ShapeDtypeStruct((1, D), F32), jax.ShapeDtypeStruct((1, D), F32)],
        compiler_params=_cparams(("arbitrary",)),
    )(z, z, z, d_cat, lng, lnb, w, bfull)


TB = 256


def _cmul(ar, ai, br, bi):
    return ar * br - ai * bi, ar * bi + ai * br


def _scan_consts(ar, ai, reverse):
    a2 = _cmul(ar, ai, ar, ai)
    a4 = _cmul(*a2, *a2)
    row = lax.broadcasted_iota(jnp.int32, (8, NS), 0)
    pr = jnp.zeros((8, NS), F32)
    pi = jnp.zeros((8, NS), F32)
    cr, ci = ar, ai
    for r in range(8):
        sel = row == (7 - r if reverse else r)
        pr = jnp.where(sel, cr, pr)
        pi = jnp.where(sel, ci, pi)
        cr, ci = _cmul(cr, ci, ar, ai)
    return ((ar, ai), a2, a4), (pr, pi), row


def scan_fwd(bu, abr, abi):
    def body(bu_ref, ar_ref, ai_ref, h_ref, car, cai):
        @pl.when(pl.program_id(0) == 0)
        def _():
            car[...] = jnp.zeros_like(car)
            cai[...] = jnp.zeros_like(cai)

        pows, (pr, pi), row = _scan_consts(ar_ref[...], ai_ref[...], False)

        def tile(t, carry):
            c_r, c_i = carry
            rows = pl.ds(pl.multiple_of(t * 8, 8), 8)
            xr = bu_ref[rows, 0:NS]
            xi = bu_ref[rows, NS:2 * NS]
            for k, (kr, ki) in zip((1, 2, 4), pows):
                sr = jnp.where(row >= k, pltpu.roll(xr, k, 0), 0.0)
                si = jnp.where(row >= k, pltpu.roll(xi, k, 0), 0.0)
                xr, xi = xr + kr * sr - ki * si, xi + kr * si + ki * sr
            xr, xi = xr + pr * c_r - pi * c_i, xi + pr * c_i + pi * c_r
            h_ref[rows, 0:NS] = xr
            h_ref[rows, NS:2 * NS] = xi
            return (jnp.broadcast_to(xr[7:8, :], (8, NS)), jnp.broadcast_to(xi[7:8, :], (8, NS)))

        c_r, c_i = lax.fori_loop(0, TB // 8, tile, (car[...], cai[...]))
        car[...] = c_r
        cai[...] = c_i

    return pl.pallas_call(
        body, name="s5_scan_fwd", grid=(S // TB,),
        in_specs=[pl.BlockSpec((TB, 2 * NS), lambda i: (i, 0)),
                  pl.BlockSpec((1, NS), lambda i: (0, 0)), pl.BlockSpec((1, NS), lambda i: (0, 0))],
        out_specs=pl.BlockSpec((TB, 2 * NS), lambda i: (i, 0)),
        out_shape=jax.ShapeDtypeStruct((S, 2 * NS), F32),
        scratch_shapes=[pltpu.VMEM((8, NS), F32), pltpu.VMEM((8, NS), F32)],
        compiler_params=_cparams(("arbitrary",)),
    )(bu, abr, abi)


def scan_bwd(eta, h, abr, abi):
    nt = S // TB

    def body(e_ref, h_ref, ar_ref, ai_ref, l_ref, da_ref, car, cai):
        @pl.when(pl.program_id(0) == 0)
        def _():
            car[...] = jnp.zeros_like(car)
            cai[...] = jnp.zeros_like(cai)
            da_ref[...] = jnp.zeros_like(da_ref)

        pows, (pr, pi), row = _scan_consts(ar_ref[...], -ai_ref[...], True)

        def tile(tt, carry):
            c_r, c_i, acr, aci = carry
            t = TB // 8 - 1 - tt
            rows = pl.ds(pl.multiple_of(t * 8, 8), 8)
            xr = e_ref[rows, 0:NS]
            xi = e_ref[rows, NS:2 * NS]
            for k, (kr, ki) in zip((1, 2, 4), pows):
                sr = jnp.where(row < 8 - k, pltpu.roll(xr, 8 - k, 0), 0.0)
                si = jnp.where(row < 8 - k, pltpu.roll(xi, 8 - k, 0), 0.0)
                xr, xi = xr + kr * sr - ki * si, xi + kr * si + ki * sr
            xr, xi = xr + pr * c_r - pi * c_i, xi + pr * c_i + pi * c_r
            l_ref[rows, 0:NS] = xr
            l_ref[rows, NS:2 * NS] = xi
            nr = jnp.where(row < 7, pltpu.roll(xr, 7, 0), c_r)
            ni = jnp.where(row < 7, pltpu.roll(xi, 7, 0), c_i)
            hr = h_ref[rows, 0:NS]
            hi = h_ref[rows, NS:2 * NS]
            acr = acr + hr * nr + hi * ni
            aci = aci + hr * ni - hi * nr
            return (jnp.broadcast_to(xr[0:1, :], (8, NS)), jnp.broadcast_to(xi[0:1, :], (8, NS)), acr, aci)

        zero = jnp.zeros((8, NS), F32)
        c_r, c_i, acr, aci = lax.fori_loop(0, TB // 8, tile, (car[...], cai[...], zero, zero))
        car[...] = c_r
        cai[...] = c_i
        da_ref[:, 0:NS] += acr
        da_ref[:, NS:2 * NS] += aci

    rev = pl.BlockSpec((TB, 2 * NS), lambda i: (nt - 1 - i, 0))
    return pl.pallas_call(
        body, name="s5_scan_bwd", grid=(nt,),
        in_specs=[rev, rev, pl.BlockSpec((1, NS), lambda i: (0, 0)), pl.BlockSpec((1, NS), lambda i: (0, 0))],
        out_specs=[rev, pl.BlockSpec((8, 2 * NS), lambda i: (0, 0))],
        out_shape=[jax.ShapeDtypeStruct((S, 2 * NS), F32), jax.ShapeDtypeStruct((8, 2 * NS), F32)],
        scratch_shapes=[pltpu.VMEM((8, NS), F32), pltpu.VMEM((8, NS), F32)],
        compiler_params=_cparams(("arbitrary",)),
    )(eta, h, abr, abi)


GC = 0.7978845608028654
GA = 0.044715


def s5_post(hc, z, dskip):
    def fn(hv, xd, dv):
        y = hv + dv * xd
        return y, 0.5 * y * (1.0 + jnp.tanh(GC * (y + GA * y * y * y)))
    return rw(fn, [(hc, 0, 512), (z, 3072, 512)], [(512, F32), (512, BF16)], "s5_post", S, consts=[dskip])


def s5_post_bwd(dyg, ypre, z, dskip):
    def fn(dy, y, xd, dv):
        th = jnp.tanh(GC * (y + GA * y * y * y))
        dg = 0.5 * (1.0 + th) + 0.5 * y * (1.0 - th * th) * GC * (1.0 + 3.0 * GA * y * y)
        dyp = dy * dg
        return dyp, dyp * dv, jnp.sum(dyp * xd, axis=0, keepdims=True)
    return rw(fn, [(dyg, 0, 512), (ypre, 0, 512), (z, 3072, 512)], [(512, BF16), (512, F32)],
              "s5_post_bwd", S, consts=[dskip], accs=[(1, 512)])


def glu_fwd(t, z, c_out):
    def fn(t1, t2, gd, co):
        return (jnp.concatenate([co, (t1 * _sig(t2) * (gd * _sig(gd))).astype(BF16)], axis=1),)
    return rw(fn, [(t, 0, 512), (t, 512, 512), (z, 3584, 512), (c_out, 0, D)], [(D + 512, BF16)], "glu_fwd", S)[0]


def glu_bwd(t, z, d_cat):
    def fn(t1, t2, gd, dd):
        s2, sg = _sig(t2), _sig(gd)
        sl = gd * sg
        return (jnp.concatenate([dd * s2 * sl, dd * t1 * s2 * (1.0 - s2) * sl], axis=1),
                dd * t1 * s2 * (sg * (1.0 + gd * (1.0 - sg))))
    return rw(fn, [(t, 0, 512), (t, 512, 512), (z, 3584, 512), (d_cat, 1024, 512)],
              [(D, BF16), (512, F32)], "glu_bwd", S)


def assemble_dz_odd(du, dv, dgc, dxd, dgd):
    def body(a, b, c, d, e, o_ref):
        o_ref[:, 0:D] = a[...].astype(BF16)
        o_ref[:, D:2 * D] = b[...].astype(BF16)
        o_ref[:, 2 * D:3 * D] = c[...].astype(BF16)
        o_ref[:, 3 * D:3 * D + 512] = d[...].astype(BF16)
        o_ref[:, 3 * D + 512:4 * D] = e[...].astype(BF16)
    tr = 256
    blk = pl.BlockSpec((tr, D), lambda i: (i, 0))
    half = pl.BlockSpec((tr, 512), lambda i: (i, 0))
    return pl.pallas_call(
        body, name="assemble_dz_odd", grid=(S // tr,), in_specs=[blk, blk, blk, half, half],
        out_specs=pl.BlockSpec((tr, 4 * D), lambda i: (i, 0)),
        out_shape=jax.ShapeDtypeStruct((S, 4 * D), BF16),
        compiler_params=_cparams(("parallel",)),
    )(du, dv, dgc, dxd, dgd)


TQ = 256


def _xattn_probs(qh, kh):
    s = _nt(qh, kh) * 0.0625
    p = jnp.exp(s - jnp.max(s, axis=-1, keepdims=True))
    return p / jnp.sum(p, axis=-1, keepdims=True)


def xattn_fwd(q, kv):
    def body(q_ref, kv_ref, o_ref):
        for h in range(4):
            sl = slice(h * 256, (h + 1) * 256)
            p = _xattn_probs(q_ref[:, sl].astype(BF16), kv_ref[:, sl].astype(BF16))
            vh = kv_ref[:, D + h * 256:D + (h + 1) * 256].astype(BF16)
            o_ref[:, sl] = _nn(p.astype(BF16), vh).astype(o_ref.dtype)

    return pl.pallas_call(
        body, name="xattn_fwd", grid=(S // TQ,),
        in_specs=[pl.BlockSpec((TQ, D), lambda i: (i, 0)), pl.BlockSpec((MEM, 2 * D), lambda i: (0, 0))],
        out_specs=pl.BlockSpec((TQ, D), lambda i: (i, 0)),
        out_shape=jax.ShapeDtypeStruct((S, D), BF16),
        compiler_params=_cparams(("parallel",)),
    )(q, kv)


def xattn_bwd(q, kv, d_o):
    def body(q_ref, kv_ref, do_ref, dq_ref, dkv_ref):
        @pl.when(pl.program_id(0) == 0)
        def _():
            dkv_ref[...] = jnp.zeros_like(dkv_ref)

        for h in range(4):
            sl = slice(h * 256, (h + 1) * 256)
            vs = slice(D + h * 256, D + (h + 1) * 256)
            qh = q_ref[:, sl].astype(BF16)
            kh = kv_ref[:, sl].astype(BF16)
            vh = kv_ref[:, vs].astype(BF16)
            doh = do_ref[:, sl].astype(BF16)
            p = _xattn_probs(qh, kh)
            dp = _nt(doh, vh)
            ds = (p * (dp - jnp.sum(p * dp, axis=-1, keepdims=True)) * 0.0625).astype(BF16)
            dq_ref[:, sl] = _nn(ds, kh).astype(dq_ref.dtype)
            dkv_ref[:, sl] += _tn(ds, qh)
            dkv_ref[:, vs] += _tn(p.astype(BF16), doh)

    return pl.pallas_call(
        body, name="xattn_bwd", grid=(S // TQ,),
        in_specs=[pl.BlockSpec((TQ, D), lambda i: (i, 0)), pl.BlockSpec((MEM, 2 * D), lambda i: (0, 0)),
                  pl.BlockSpec((TQ, D), lambda i: (i, 0))],
        out_specs=[pl.BlockSpec((TQ, D), lambda i: (i, 0)), pl.BlockSpec((MEM, 2 * D), lambda i: (0, 0))],
        out_shape=[jax.ShapeDtypeStruct((S, D), BF16), jax.ShapeDtypeStruct((MEM, 2 * D), F32)],
        compiler_params=_cparams(("arbitrary",)),
    )(q, kv, d_o)


def _s5_disc(a_re, a_im, log_dt, b_re, b_im):
    dt = jnp.exp(log_dt)[:, None]
    mag = jnp.exp(dt * a_re)
    abr = mag * jnp.cos(dt * a_im)
    abi = mag * jnp.sin(dt * a_im)
    nr, ni = abr - 1.0, abi
    inv = 1.0 / (a_re * a_re + a_im * a_im)
    cr = (nr * a_re + ni * a_im) * inv
    ci = (ni * a_re - nr * a_im) * inv
    bbr = cr[..., None] * b_re - ci[..., None] * b_im
    bbi = cr[..., None] * b_im + ci[..., None] * b_re
    return abr, abi, bbr, bbi


def _blockdiag(t):
    g, a, b = t.shape
    eye = jnp.eye(g, dtype=t.dtype)
    return (eye[:, None, :, None] * t[:, :, None, :]).reshape(g * a, g * b)


def _blocks(mat, a, b):
    return jnp.einsum("gagb->gab", mat.reshape(NG, a, NG, b))


def _fwd_even(i, x, P, W):
    hn = rms_fwd(x, P["norm_ab"][i:i + 1], "rms_ab_fwd")
    z = mm(m2(hn), W["w_in"], "nn", "in_ab")
    o, lse, cat = attn_fwd(z)
    if "more" in W:
        W.update(W.pop("more")(cat))
    cat = pool_fwd(z, W["pool_w"], P["pool_scale"][i:i + 1], cat)
    x_mid = mm(m2(cat), W["w_out"], "nn", "out_ab", add=m2(x))
    return x_mid, dict(x=x, hn=hn, z=z, o=o, lse=lse, cat=cat)


def _bwd_even(i, dx_mid, sv, P, W, G, GW):
    z = sv["z"]
    d_cat = mm(m2(dx_mid), W["w_out"], "nt", "out_ab_dx")
    GW["w_out"] = mm(m2(sv["cat"]), m2(dx_mid), "tn", "out_ab_dw").reshape(4, 512, D)
    dq, dk, dv, dga = attn_bwd(z, d_cat, sv["o"], sv["lse"])
    dvb, dgb, dpw, dps = pool_bwd(z, d_cat, W["pool_w"], P["pool_scale"][i:i + 1])
    GW["pool_w"] = dpw.reshape(4, 4, 64, 256).transpose(1, 0, 2, 3).reshape(4, 256, 256)
    G["pool_scale"][i] = dps[0]
    d_z = assemble_dz_even((dq, dk, dv, dga, dvb, dgb))
    d_hn = mm(m2(d_z), W["w_in"], "nt", "in_ab_dx")
    GW["w_in"] = mm(m2(sv["hn"]), m2(d_z), "tn", "in_ab_dw", out=outcs(D, 1536))
    return d_hn, P["norm_ab"][i:i + 1], "norm_ab", "rms_ab_bwd"


def _fwd_odd(i, x, P, W):
    hn = rms_fwd(x, P["norm_cd"][i:i + 1], "rms_cd_fwd")
    z = mm(m2(hn), W["w_in"], "nn", "in_cd")
    bfull = jnp.repeat(P["sgu_b"][i].T, 256, axis=1)
    c_out = sgu_fwd(z, P["sgu_ln_g"][i:i + 1], P["sgu_ln_b"][i:i + 1], P["sgu_w"][i], bfull)
    disc, disc_vjp = jax.vjp(_s5_disc, P["s5_a_re"][i], P["s5_a_im"][i], P["s5_log_dt"][i],
                             P["s5_b_re"][i], P["s5_b_im"][i])
    abr, abi, bbr, bbi = disc
    bbd = jnp.concatenate([_blockdiag(bbr.transpose(0, 2, 1)), _blockdiag(bbi.transpose(0, 2, 1))], axis=1)
    cbd = jnp.concatenate([_blockdiag(P["s5_c_re"][i].transpose(0, 2, 1)),
                           -_blockdiag(P["s5_c_im"][i].transpose(0, 2, 1))], axis=0)
    abr, abi = abr.reshape(1, NS), abi.reshape(1, NS)
    bu = mm(m2(z, 3072, 512), m2(bbd), "nn", "s5_bu")
    h = scan_fwd(bu, abr, abi)
    hc = mm(m2(h), m2(cbd), "nn", "s5_hc")
    dskip = P["s5_d"][i:i + 1]
    ypre, yg = s5_post(hc, z, dskip)
    if "more" in W:
        W.update(W.pop("more")(yg))
    w12 = W["w12"]
    t = mm(m2(yg), m2(w12), "nn", "glu_t")
    cat = glu_fwd(t, z, c_out)
    x_mid = mm(m2(cat), W["w_out"], "nn", "out_cd", add=m2(x))
    return x_mid, dict(x=x, hn=hn, z=z, bfull=bfull, disc_vjp=disc_vjp, bbd=bbd, cbd=cbd, abr=abr,
                       abi=abi, h=h, ypre=ypre, yg=yg, w12=w12, t=t, cat=cat, dskip=dskip)


def _bwd_odd(i, dx_mid, sv, P, W, G, GW):
    z = sv["z"]
    d_cat = mm(m2(dx_mid), W["w_out"], "nt", "out_cd_dx")
    GW["w_out"] = mm(m2(sv["cat"]), m2(dx_mid), "tn", "out_cd_dw").reshape(4, 384, D)
    du, dv, dgc, dws, dbs, dlg, dlb = sgu_bwd(z, d_cat, P["sgu_ln_g"][i:i + 1], P["sgu_ln_b"][i:i + 1],
                                               P["sgu_w"][i], sv["bfull"])
    G["sgu_w"][i], G["sgu_b"][i] = dws, dbs[:, :4].T
    G["sgu_ln_g"][i], G["sgu_ln_b"][i] = dlg[0], dlb[0]
    dt, dgd = glu_bwd(sv["t"], z, d_cat)
    gw12 = mm(m2(sv["yg"]), m2(dt), "tn", "glu_dw")
    GW["glu_w1"] = gw12[:, :512].reshape(4, 128, 512)
    GW["glu_w2"] = gw12[:, 512:].reshape(4, 128, 512)
    dyg = mm(m2(dt), m2(sv["w12"]), "nt", "glu_dx")
    dypre, dxd1, dd = s5_post_bwd(dyg, sv["ypre"], z, sv["dskip"])
    G["s5_d"][i] = dd[0]
    gcbd = mm(m2(sv["h"]), m2(dypre), "tn", "s5_dc")
    G["s5_c_re"][i] = _blocks(gcbd[:NS], NP, NH).transpose(0, 2, 1)
    G["s5_c_im"][i] = -_blocks(gcbd[NS:], NP, NH).transpose(0, 2, 1)
    eta = mm(m2(dypre), m2(sv["cbd"]), "nt", "s5_eta")
    lam, dacc = scan_bwd(eta, sv["h"], sv["abr"], sv["abi"])
    gbbd = mm(m2(z, 3072, 512), m2(lam), "tn", "s5_db")
    dxd = mm(m2(lam), m2(sv["bbd"]), "nt", "s5_dx", add=m2(dxd1))
    dacc = jnp.sum(dacc, axis=0)
    d_bbr = _blocks(gbbd[:, :NS], NH, NP).transpose(0, 2, 1)
    d_bbi = _blocks(gbbd[:, NS:], NH, NP).transpose(0, 2, 1)
    (G["s5_a_re"][i], G["s5_a_im"][i], G["s5_log_dt"][i], G["s5_b_re"][i], G["s5_b_im"][i]) = sv["disc_vjp"](
        (dacc[:NS].reshape(NG, NP), dacc[NS:].reshape(NG, NP), d_bbr, d_bbi))
    d_z = assemble_dz_odd(du, dv, dgc, dxd, dgd)
    d_hn = mm(m2(d_z), W["w_in"], "nt", "in_cd_dx")
    GW["w_in"] = mm(m2(sv["hn"]), m2(d_z), "tn", "in_cd_dw", out=outcs(D, 1024))
    return d_hn, P["norm_cd"][i:i + 1], "norm_cd", "rms_cd_bwd"


def _fwd_x(l, x, mem_n, P, W):
    hx = rms_fwd(x, P["norm_x"][l:l + 1], "rms_x_fwd")
    q = mm(m2(hx), W["w_xq"], "nn", "xq", out_dtype=BF16)
    kv = mm(m2(mem_n), W["w_xkv"], "nn", "xkv", out_dtype=BF16)
    ox = xattn_fwd(q, kv)
    x_out = mm(m2(ox), W["w_xo"], "nn", "xo", add=m2(x))
    return x_out, dict(x=x, hx=hx, q=q, kv=kv, ox=ox)


def _bwd_x(l, dx_out, sv, mem_n, d_memn, P, W, G, GW):
    d_ox = mm(m2(dx_out), W["w_xo"], "nt", "xo_dx", out_dtype=BF16)
    GW["w_xo"] = mm(m2(sv["ox"]), m2(dx_out), "tn", "xo_dw").reshape(4, 256, D)
    dq, dkv = xattn_bwd(sv["q"], sv["kv"], d_ox)
    GW["w_xq"] = mm(m2(sv["hx"]), m2(dq), "tn", "xq_dw").reshape(4, 256, D)
    d_hx = mm(m2(dq), W["w_xq"], "nt", "xq_dx")
    GW["w_xkv"] = mm(m2(mem_n), m2(dkv), "tn", "xkv_dw", out=outcs(D, 512))
    d_memn = mm(m2(dkv), W["w_xkv"], "nt", "xkv_dx", add=None if d_memn is None else m2(d_memn))
    dx, dg = rms_bwd(sv["x"], d_hx, dx_out, P["norm_x"][l:l + 1], "rms_x_bwd")
    G["norm_x"][l] = dg[0]
    return dx, d_memn


SMALL_LAYERS = (("norm_ab", 2), ("pool_scale", 2), ("norm_cd", 2), ("sgu_ln_g", 2), ("sgu_ln_b", 2), ("sgu_w", 2),
                ("sgu_b", 2), ("s5_a_re", 2), ("s5_a_im", 2), ("s5_log_dt", 2), ("s5_b_re", 2), ("s5_b_im", 2),
                ("s5_c_re", 2), ("s5_c_im", 2), ("s5_d", 2), ("norm_x", 4))


def local_step(x, mem, tgt, P, weights_of, grads_done):
    G = {k: [None] * n for k, n in SMALL_LAYERS}
    mem_g = P["mem_norm"].reshape(1, D)
    mem_n = rms_fwd(mem, mem_g, "rms_mem_fwd")
    saved = []
    for layer in range(4):
        i = layer // 2
        W = weights_of(layer, x)
        x, sv_m = (_fwd_even if layer % 2 == 0 else _fwd_odd)(i, x, P, W)
        x, sv_x = _fwd_x(layer, x, mem_n, P, W)
        saved.append((sv_m, sv_x, W))
    dx, loss, dgf = final_loss(x, tgt, P["final_norm"].reshape(1, D))
    G["final_norm"] = dgf[0]
    d_memn = None
    for layer in reversed(range(4)):
        i = layer // 2
        sv_m, sv_x, W = saved[layer]
        GW = {}
        dx_mid, d_memn = _bwd_x(layer, dx, sv_x, mem_n, d_memn, P, W, G, GW)
        d_hn, g, key, name = (_bwd_even if layer % 2 == 0 else _bwd_odd)(i, dx_mid, sv_m, P, W, G, GW)
        token = grads_done(layer, GW)
        if token is not None:
            g = g + token
        dx, dg = rms_bwd(sv_m["x"], d_hn, dx_mid, g, name)
        G[key][i] = dg[0]
    _, dgm = rms_bwd(mem, d_memn, d_memn, mem_g, "rms_mem_bwd")
    G["mem_norm"] = dgm[0]
    return loss, dx, G


ANY = pl.BlockSpec(memory_space=pl.ANY)


def _place():
    x, y, c = lax.axis_index("x"), lax.axis_index("y"), lax.axis_index("c")
    chips = [(1 - x, y), (x, 1 - y), (1 - x, 1 - y)]
    return x, y, c, 2 * x + y, (x, y, 1 - c), chips


def _remote(src, dst, send, recv, k, dev):
    return pltpu.make_async_remote_copy(src_ref=src, dst_ref=dst, send_sem=send.at[k], recv_sem=recv.at[k],
                                        device_id=dev, device_id_type=MESHID)


HBM = pl.BlockSpec(memory_space=pltpu.HBM)
SEM = pl.BlockSpec(memory_space=pltpu.SEMAPHORE)
EFFECT = pltpu.SideEffectType.DATAFLOW_SIDE_EFFECTING


def _hbm(t):
    return pltpu.with_memory_space_constraint(t, pltpu.HBM)


def allgather_sync(shards):
    n = len(shards)

    def body(*refs):
        ins, outs = refs[:n], refs[n:2 * n]
        token, send, recv = refs[2 * n:]
        x, y, c, jme, sib, chips = _place()
        first, passed = [], []
        for a in range(n):
            cp = _remote(ins[a], outs[a].at[jme], send, recv, a * 7 + 6, sib)
            cp.start()
            first.append(cp)
            for k, chip in enumerate(chips):
                cp = _remote(ins[a].at[c], outs[a].at[jme, c], send, recv, a * 7 + k, (*chip, c))
                cp.start()
                first.append(cp)
        for a in range(n):
            for k, chip in enumerate(chips):
                piece = outs[a].at[2 * chip[0] + chip[1], c]
                _remote(piece, piece, send, recv, a * 7 + k, (*chip, c)).wait_recv()
                fw = _remote(piece, piece, send, recv, a * 7 + 3 + k, sib)
                fw.start()
                passed.append(fw)
        for a in range(n):
            own = outs[a].at[jme]
            _remote(own, own, send, recv, a * 7 + 6, sib).wait_recv()
            for k, chip in enumerate(chips):
                piece = outs[a].at[2 * chip[0] + chip[1], 1 - c]
                _remote(piece, piece, send, recv, a * 7 + 3 + k, sib).wait_recv()
        for cp in first + passed:
            cp.wait_send()
        token[...] = jnp.zeros_like(token)

    res = pl.pallas_call(
        body, name="allgather_sync", in_specs=[ANY] * n,
        out_specs=[ANY] * n + [pl.BlockSpec(memory_space=pltpu.VMEM)],
        out_shape=[jax.ShapeDtypeStruct((4,) + s.shape, s.dtype) for s in shards] + [jax.ShapeDtypeStruct((8, 128), F32)],
        scratch_shapes=[pltpu.SemaphoreType.DMA((7 * n,)), pltpu.SemaphoreType.DMA((7 * n,))],
    )(*shards)
    return list(res[:n]), res[n]


def _gather_copies(ins, lands, send, recv):
    x, y, c, jme, sib, chips = _place()
    devs = [(*chip, c) for chip in chips] + [sib]
    return [_remote(ins[a], lands[a].at[jme], send, recv, a * 4 + k, dev)
            for a in range(len(ins)) for k, dev in enumerate(devs)]


def allgather_start(shards, after, name):
    n, na = len(shards), len(after)

    def body(*refs):
        ins, lands = refs[:n], refs[n:2 * n]
        send, recv = refs[2 * n + na], refs[2 * n + na + 1]
        token = refs[-1]
        for cp in _gather_copies(ins, lands, send, recv):
            cp.start()
        token[...] = jnp.zeros_like(token)

    res = pl.pallas_call(
        body, name=name,
        out_shape=(pltpu.SemaphoreType.DMA((4 * n,)), pltpu.SemaphoreType.DMA((4 * n,)),
                   *[pltpu.HBM(s.shape, s.dtype) for s in shards],
                   *[pltpu.HBM((4,) + s.shape, s.dtype) for s in shards],
                   jax.ShapeDtypeStruct((8, 128), F32)),
        in_specs=[HBM] * (2 * n) + [ANY] * na,
        out_specs=(SEM, SEM, *[HBM] * (2 * n), pl.BlockSpec(memory_space=pltpu.VMEM)),
        input_output_aliases={a: 2 + a for a in range(2 * n)},
        compiler_params=pltpu.CompilerParams(has_side_effects=EFFECT),
    )(*[_hbm(s) for s in shards], *[_hbm(lax.empty((4,) + s.shape, s.dtype)) for s in shards], *after)
    return res[0], res[1], list(res[2:2 + n]), list(res[2 + n:2 + 2 * n]), res[-1]


def allgather_wait(send, recv, shards, lands, after, name):
    n = len(shards)

    def body(*refs):
        ins, zones = refs[:n], refs[n:2 * n]
        send_r, recv_r = refs[2 * n], refs[2 * n + 1]
        x, y, c, jme, sib, chips = _place()
        slots = [2 * chip[0] + chip[1] for chip in chips] + [jme]
        for a in range(n):
            for k, slot in enumerate(slots):
                cp = _remote(ins[a], zones[a].at[slot], send_r, recv_r, a * 4 + k, sib)
                cp.wait_send()
                cp.wait_recv()

    res = pl.pallas_call(
        body, name=name,
        out_shape=tuple(pltpu.HBM(t.shape, t.dtype) for t in list(shards) + list(lands)),
        in_specs=[HBM] * (2 * n) + [SEM, SEM, ANY], out_specs=tuple([HBM] * (2 * n)),
        input_output_aliases={a: a for a in range(2 * n)},
        compiler_params=pltpu.CompilerParams(has_side_effects=EFFECT),
    )(*shards, *lands, send, recv, after)
    return list(res[n:])


def allgather_small(slab):
    def body(in_ref, out_ref, send, recv, lsem):
        x, y, c, jme, sib, chips = _place()
        loc = pltpu.make_async_copy(in_ref, out_ref.at[jme], lsem.at[0])
        loc.start()
        cps = [_remote(in_ref, out_ref.at[jme], send, recv, k, (*chip, c)) for k, chip in enumerate(chips)]
        for cp in cps:
            cp.start()
        for k, chip in enumerate(chips):
            piece = out_ref.at[2 * chip[0] + chip[1]]
            _remote(piece, piece, send, recv, k, (*chip, c)).wait_recv()
        for cp in cps:
            cp.wait_send()
        loc.wait()

    return pl.pallas_call(
        body, name="allgather_small", in_specs=[ANY], out_specs=ANY,
        out_shape=jax.ShapeDtypeStruct((4,) + slab.shape, slab.dtype),
        scratch_shapes=[pltpu.SemaphoreType.DMA((3,)), pltpu.SemaphoreType.DMA((3,)), pltpu.SemaphoreType.DMA((1,))],
    )(slab)


def allreduce_small(v):
    def body(v_ref, o_ref, r0, r1, r2, send, recv):
        x, y, c, jme, sib, chips = _place()
        peers = [sib, (1 - x, y, c), (x, 1 - y, c)]
        o_ref[...] = v_ref[...]
        for k, buf in enumerate((r0, r1, r2)):
            cp = _remote(o_ref, buf, send, recv, k, peers[k])
            cp.start()
            cp.wait()
            o_ref[...] = o_ref[...] + buf[...]

    vm = pl.BlockSpec(memory_space=pltpu.VMEM)
    return pl.pallas_call(
        body, name="allreduce_small", in_specs=[vm], out_specs=vm,
        out_shape=jax.ShapeDtypeStruct(v.shape, v.dtype),
        scratch_shapes=[pltpu.VMEM(v.shape, v.dtype)] * 3 + [pltpu.SemaphoreType.DMA((3,)), pltpu.SemaphoreType.DMA((3,))],
        compiler_params=pltpu.CompilerParams(vmem_limit_bytes=VMEM_LIMIT),
    )(v)


def _pair_copies(gs, lands, send, recv):
    x, y, c, jme, sib, chips = _place()
    return [_remote(gs[a].at[:, 1 - c], lands[a], send, recv, a, sib) for a in range(len(gs))]


def rs_pair_start(gs, name):
    n = len(gs)

    def body(*refs):
        ins, lands = refs[:n], refs[n:2 * n]
        send, recv = refs[2 * n], refs[2 * n + 1]
        token = refs[-1]
        for cp in _pair_copies(ins, lands, send, recv):
            cp.start()
        token[...] = jnp.zeros_like(token)

    shapes = [(4,) + g.shape[2:] for g in gs]
    res = pl.pallas_call(
        body, name=name,
        out_shape=(pltpu.SemaphoreType.DMA((n,)), pltpu.SemaphoreType.DMA((n,)),
                   *[pltpu.HBM(g.shape, g.dtype) for g in gs], *[pltpu.HBM(s, F32) for s in shapes],
                   jax.ShapeDtypeStruct((8, 128), F32)),
        in_specs=[HBM] * (2 * n), out_specs=(SEM, SEM, *[HBM] * (2 * n), pl.BlockSpec(memory_space=pltpu.VMEM)),
        input_output_aliases={a: 2 + a for a in range(2 * n)},
        compiler_params=pltpu.CompilerParams(has_side_effects=EFFECT),
    )(*[_hbm(g) for g in gs], *[_hbm(lax.empty(s, F32)) for s in shapes])
    return res[0], res[1], list(res[2:2 + n]), list(res[2 + n:2 + 2 * n]), res[-1]


def rs_pair_wait(send, recv, gs, lands, after, name):
    n = len(gs)

    def body(*refs):
        ins, zones = refs[:n], refs[n:2 * n]
        for cp in _pair_copies(ins, zones, refs[2 * n], refs[2 * n + 1]):
            cp.wait_send()
            cp.wait_recv()

    res = pl.pallas_call(
        body, name=name,
        out_shape=tuple(pltpu.HBM(t.shape, t.dtype) for t in list(gs) + list(lands)),
        in_specs=[HBM] * (2 * n) + [SEM, SEM, ANY], out_specs=tuple([HBM] * (2 * n)),
        input_output_aliases={a: a for a in range(2 * n)},
        compiler_params=pltpu.CompilerParams(has_side_effects=EFFECT),
    )(*gs, *lands, send, recv, after)
    return list(res[:n]), list(res[n:])


def rs_pair_sum(g4, got, cidx):
    _, _, rh, cols = g4.shape
    tr = rh if rh <= 256 else 256

    def body(c_ref, a_ref, b_ref, o_ref):
        o_ref[...] = (a_ref[...] + b_ref[...]).astype(o_ref.dtype)

    return pl.pallas_call(
        body, name="rs_pair_sum",
        grid_spec=pltpu.PrefetchScalarGridSpec(
            num_scalar_prefetch=1, grid=(4, rh // tr),
            in_specs=[pl.BlockSpec((None, None, tr, cols), lambda j, t, cr: (j, cr[0], t, 0)),
                      pl.BlockSpec((None, tr, cols), lambda j, t, cr: (j, t, 0))],
            out_specs=pl.BlockSpec((None, tr, cols), lambda j, t, cr: (j, t, 0))),
        out_shape=jax.ShapeDtypeStruct((4, rh, cols), BF16),
        compiler_params=_cparams(("parallel", "parallel")),
    )(cidx, g4, got)


def _chip_copies(ps, lands, send, recv):
    x, y, c, jme, sib, chips = _place()
    return [_remote(ps[a].at[2 * chip[0] + chip[1]], lands[a].at[jme], send, recv, a * 3 + k, (*chip, c))
            for a in range(len(ps)) for k, chip in enumerate(chips)]


def rs_chip_start(ps, name):
    n = len(ps)

    def body(*refs):
        ins, lands = refs[:n], refs[n:2 * n]
        send, recv = refs[2 * n], refs[2 * n + 1]
        token = refs[-1]
        for cp in _chip_copies(ins, lands, send, recv):
            cp.start()
        token[...] = jnp.zeros_like(token)

    res = pl.pallas_call(
        body, name=name,
        out_shape=(pltpu.SemaphoreType.DMA((3 * n,)), pltpu.SemaphoreType.DMA((3 * n,)),
                   *[pltpu.HBM(p.shape, p.dtype) for p in ps], *[pltpu.HBM(p.shape, p.dtype) for p in ps],
                   jax.ShapeDtypeStruct((8, 128), F32)),
        in_specs=[HBM] * (2 * n), out_specs=(SEM, SEM, *[HBM] * (2 * n), pl.BlockSpec(memory_space=pltpu.VMEM)),
        input_output_aliases={a: 2 + a for a in range(2 * n)},
        compiler_params=pltpu.CompilerParams(has_side_effects=EFFECT),
    )(*[_hbm(p) for p in ps], *[_hbm(lax.empty(p.shape, p.dtype)) for p in ps])
    return res[0], res[1], list(res[2:2 + n]), list(res[2 + n:2 + 2 * n]), res[-1]


def rs_chip_wait(send, recv, ps, lands, after, name):
    n = len(ps)

    def body(*refs):
        ins, zones = refs[:n], refs[n:2 * n]
        send_r, recv_r = refs[2 * n], refs[2 * n + 1]
        x, y, c, jme, sib, chips = _place()
        for a in range(n):
            for k, chip in enumerate(chips):
                jt = 2 * chip[0] + chip[1]
                cp = _remote(ins[a].at[jt], zones[a].at[jt], send_r, recv_r, a * 3 + k, (*chip, c))
                cp.wait_send()
                cp.wait_recv()

    res = pl.pallas_call(
        body, name=name,
        out_shape=tuple(pltpu.HBM(p.shape, p.dtype) for p in list(ps) + list(lands)),
        in_specs=[HBM] * (2 * n) + [SEM, SEM, ANY], out_specs=tuple([HBM] * (2 * n)),
        input_output_aliases={a: a for a in range(2 * n)},
        compiler_params=pltpu.CompilerParams(has_side_effects=EFFECT),
    )(*ps, *lands, send, recv, after)
    return list(res[n:])


def rs_chip_sum(q, p, l, acc, layers, jc):
    _, rh, cols = q.shape
    tr = rh if rh <= 256 else 256

    def body(jc_ref, q_ref, p_ref, *rest):
        o_ref = rest[-1]
        jme = jc_ref[0]
        own = p_ref[...].astype(F32)
        v = [jnp.where(jme == j, own, q_ref[j].astype(F32)) for j in range(4)]
        o_ref[...] = ((v[0] + v[1]) + v[2]) + v[3]

    in_specs = [pl.BlockSpec((4, tr, cols), lambda t, jr: (0, t, 0)),
                pl.BlockSpec((None, tr, cols), lambda t, jr: (jr[0], t, 0))]
    args = [jc, q, p]
    if acc is not None:
        in_specs.append(ANY)
        args.append(acc)
    return pl.pallas_call(
        body, name="rs_chip_sum",
        grid_spec=pltpu.PrefetchScalarGridSpec(
            num_scalar_prefetch=1, grid=(rh // tr,), in_specs=in_specs,
            out_specs=pl.BlockSpec((None, None, tr, cols), lambda t, jr: (l, jr[1], t, 0))),
        out_shape=jax.ShapeDtypeStruct((layers, 2, rh, cols), F32),
        input_output_aliases={} if acc is None else {3: 0},
        compiler_params=_cparams(("parallel",)),
    )(*args)


def rs_pair_gather(rs):
    n = len(rs)

    def body(*refs):
        outs = refs[n:2 * n]
        send, recv = refs[2 * n:]
        x, y, c, jme, sib, chips = _place()
        cps = [_remote(outs[a].at[:, c], outs[a].at[:, c], send, recv, a, sib) for a in range(n)]
        for cp in cps:
            cp.start()
        for a in range(n):
            slot = outs[a].at[:, 1 - c]
            _remote(slot, slot, send, recv, a, sib).wait_recv()
        for cp in cps:
            cp.wait_send()

    return pl.pallas_call(
        body, name="rs_pair_gather", in_specs=[ANY] * n, out_specs=[ANY] * n,
        out_shape=[jax.ShapeDtypeStruct(r.shape, r.dtype) for r in rs],
        input_output_aliases={a: a for a in range(n)},
        scratch_shapes=[pltpu.SemaphoreType.DMA((n,)), pltpu.SemaphoreType.DMA((n,))],
    )(*rs)


def _adamw_math(w, g, m, v):
    m = B1 * m + (1.0 - B1) * g
    v = B2 * v + (1.0 - B2) * (g * g)
    m_hat = m / (1.0 - B1 ** STEP)
    v_hat = v / (1.0 - B2 ** STEP)
    return -LR * (m_hat / (jnp.sqrt(v_hat) + AEPS) + WD * w), m, v


def adamw(w, g, m, v, name):
    rows, cols = w.shape
    tr = 256 if rows % 256 == 0 else rows
    return rw(_adamw_math, [(a, 0, cols) for a in (w, g, m, v)], [(cols, F32)] * 3, name, rows, tr=tr)


WEIGHTS = ["norm_ab", "w_in_ab", "pool_w", "pool_scale", "w_out_ab", "norm_cd", "w_in_cd", "sgu_ln_g", "sgu_ln_b",
           "sgu_w", "sgu_b", "s5_a_re", "s5_a_im", "s5_log_dt", "s5_b_re", "s5_b_im", "s5_c_re", "s5_c_im", "s5_d",
           "glu_w1", "glu_w2", "w_out_cd", "norm_x", "w_xq", "w_xkv", "w_xo", "mem_norm", "final_norm"]
INPUTS = ["x", "mem"] + WEIGHTS + ["loss_target"] + ["m_" + n for n in WEIGHTS] + ["v_" + n for n in WEIGHTS]
BIG = ["w_in_ab", "w_out_ab", "w_in_cd", "w_out_cd", "w_xq", "w_xkv", "w_xo", "glu_w1", "glu_w2", "pool_w"]
COL_SHARDED = ("w_in_ab", "w_in_cd", "w_xkv")
SMALL = [n for n in WEIGHTS if n not in BIG]
SMALL_SHARDED = {"norm_cd": 256, "sgu_ln_g": 256, "sgu_ln_b": 256, "s5_d": 128}
PACK = 256 * 128


def _pack(arrs):
    flat = jnp.concatenate([a.reshape(-1) for a in arrs])
    pad = (-flat.shape[0]) % PACK
    return jnp.concatenate([flat, jnp.zeros((pad,), flat.dtype)]).reshape(-1, 128)


def _unpack(packed, shapes):
    flat, out, off = packed.reshape(-1), [], 0
    for s in shapes:
        n = 1
        for d in s:
            n *= d
        out.append(flat[off:off + n].reshape(s))
        off += n
    return out


LAYER_KEYS = (("w_in", "w_out", "pool_w", "w_xq", "w_xkv", "w_xo"),
              ("w_in", "w_out", "glu_w1", "glu_w2", "w_xq", "w_xkv", "w_xo"))


def _weight_of(key, layer):
    if key in ("w_xq", "w_xkv", "w_xo"):
        return key, layer, 4
    kind = "ab" if layer % 2 == 0 else "cd"
    return {"w_in": "w_in_" + kind, "w_out": "w_out_" + kind}.get(key, key), layer // 2, 2


def kernel(*args):
    a = dict(zip(INPUTS, args))
    x_i, y_i, c_i = lax.axis_index("x"), lax.axis_index("y"), lax.axis_index("c")
    j = 2 * x_i + y_i

    slab = jnp.concatenate([a["norm_cd"], a["sgu_ln_g"], a["sgu_ln_b"],
                            jnp.pad(a["s5_d"], ((0, 0), (0, 128)))], axis=0)
    gslab = allgather_small(slab)
    P = {n: a[n] for n in SMALL}
    for k, n in enumerate(("norm_cd", "sgu_ln_g", "sgu_ln_b", "s5_d")):
        wd = SMALL_SHARDED[n]
        P[n] = gslab[:, 2 * k:2 * k + 2, :wd].transpose(1, 0, 2).reshape(2, 4 * wd)

    def shards_of(layer):
        keys = sorted(k for k in LAYER_KEYS[layer % 2])
        out = []
        for k in keys:
            n, l, _ = _weight_of(k, layer)
            out.append(a[n][l].reshape(-1, a[n].shape[-1]).astype(BF16))
        return keys, out

    keys0, sh0 = shards_of(0)
    first = keys0.index("w_in")
    g_in, token = allgather_sync([sh0[first].reshape(2, sh0[first].shape[0] // 2, sh0[first].shape[1])])
    w_in0 = g_in[0].reshape(4, -1, g_in[0].shape[-1])
    started = {}
    for layer in (0, 1, 2, 3):
        keys, sh = (keys0, sh0) if layer == 0 else shards_of(layer)
        rest = [(k, s) for k, s in zip(keys, sh) if k != "w_in"]
        parts = [("in", ["w_in"], [sh[keys.index("w_in")]])] * (layer > 0) + [("", *map(list, zip(*rest)))]
        for tag, pk, ps in parts:
            send, recv, ps, lands, token = allgather_start(ps, [token, gslab], "allgather_start_%d%s" % (layer, tag))
            started[(layer, tag)] = (pk, send, recv, ps, lands)
    P["norm_ab"] = P["norm_ab"] + token[0:1, 0:1]

    cidx = jnp.reshape(c_i, (1,)).astype(jnp.int32)
    jc = jnp.stack([j, c_i]).astype(jnp.int32)

    def views(g):
        W = {}
        for k, v in g.items():
            if k in ("w_in", "w_xkv"):
                W[k] = mcs(v)
            elif k == "pool_w":
                W[k] = v.reshape(4, 4, 64, 256).transpose(1, 0, 2, 3).reshape(4, 256, 256)
            elif k not in ("glu_w1", "glu_w2"):
                W[k] = m2(v.reshape(-1, v.shape[-1]))
        if "glu_w1" in g:
            W["w12"] = jnp.concatenate([g["glu_w1"].reshape(512, 512), g["glu_w2"].reshape(512, 512)], axis=1)
        return W

    def arrived(layer, tag, after):
        keys, send, recv, sh, lands = started[(layer, tag)]
        return views(dict(zip(keys, allgather_wait(send, recv, sh, lands, after, "allgather_wait_%d%s" % (layer, tag)))))

    def weights_of(layer, x_in):
        W = views({"w_in": w_in0}) if layer == 0 else arrived(layer, "in", x_in)
        W["more"] = lambda after: arrived(layer, "", after)
        return W

    halves, pending = {}, {}

    def finish_pair(layer, after):
        keys, send, recv, flat, lands = halves.pop(layer)
        flat, got = rs_pair_wait(send, recv, flat, lands, after, "rs_pair_wait_%d" % layer)
        pair = [rs_pair_sum(g4, r, cidx) for g4, r in zip(flat, got)]
        send, recv, pair, lands, token = rs_chip_start(pair, "rs_chip_start_%d" % layer)
        pending[layer] = (keys, send, recv, pair, lands)
        return token

    def grads_done(layer, GW):
        keys = sorted(GW)
        flat = [GW[k].reshape(4, 2, GW[k].shape[1] // 2, GW[k].shape[2]) for k in keys]
        send, recv, flat, lands, token = rs_pair_start(flat, "rs_pair_start_%d" % layer)
        halves[layer] = (keys, send, recv, flat, lands)
        if layer + 1 in halves:
            token = token + finish_pair(layer + 1, token)
        return token[0:1, 0:1]

    loss, dx, G = local_step(a["x"][0], a["mem"][0], a["loss_target"][0], P, weights_of, grads_done)
    loss = lax.psum(loss[0, 0], ("x", "y", "c"))
    finish_pair(0, dx)

    red = {}
    for layer in (3, 2, 1, 0):
        keys, send, recv, pair, lands = pending[layer]
        lands = rs_chip_wait(send, recv, pair, lands, dx, "rs_chip_wait_%d" % layer)
        for k, q, p in zip(keys, lands, pair):
            n, l, layers = _weight_of(k, layer)
            red[n] = rs_chip_sum(q, p, l, red.get(n), layers, jc)
    gbig = dict(zip(BIG, rs_pair_gather([red[n] for n in BIG])))

    outs = {}
    for n in BIG:
        shp = a[n].shape
        g2 = gbig[n].reshape(-1, shp[-1])
        d2, m2_, v2_ = adamw(a[n].reshape(g2.shape), g2, a["m_" + n].reshape(g2.shape),
                             a["v_" + n].reshape(g2.shape), "adamw_" + n)
        outs[n] = tuple(t.reshape(shp) for t in (g2, d2, m2_, v2_))

    gfull = [jnp.stack(G[n]) if isinstance(G[n], list) else G[n] for n in SMALL]
    shapes = [g.shape for g in gfull]
    gsum = _unpack(allreduce_small(_pack(gfull)), shapes)
    gloc = []
    for n, g in zip(SMALL, gsum):
        if n in SMALL_SHARDED:
            g = lax.dynamic_slice_in_dim(g, j * SMALL_SHARDED[n], SMALL_SHARDED[n], axis=1)
        gloc.append(g)
    for n, g in zip(SMALL, gloc):
        shp = a[n].shape
        two = (-1, shp[-1]) if len(shp) > 1 else (1, shp[0])
        upd = adamw(a[n].reshape(two), g.reshape(two), a["m_" + n].reshape(two), a["v_" + n].reshape(two), "adamw_" + n)
        outs[n] = (g,) + tuple(t.reshape(shp) for t in upd)

    res = [loss, dx[None]]
    for part in range(4):
        res += [outs[n][part] for n in WEIGHTS]
    return tuple(res)
```

```python
import math

import jax
import jax.numpy as jnp
from jax import lax
from jax.experimental import pallas as pl
from jax.experimental.pallas import tpu as pltpu

F32, BF16 = jnp.float32, jnp.bfloat16
S, D = 2048, 1024
MEM = 256
EPS = 1e-6
NEG = -1e30
QB = 128
PATTERNS = (1, 4, 16)
NG, NP, NH = 32, 64, 16
NS = NG * NP
LR, B1, B2, AEPS, WD, STEP = 0.001, 0.9, 0.999, 1e-08, 0.01, 10
MESHID = pl.DeviceIdType.MESH
VMEM_LIMIT = 56 * 1024 * 1024


def _cparams(sem):
    return pltpu.CompilerParams(dimension_semantics=sem, vmem_limit_bytes=VMEM_LIMIT)


def _sig(x):
    return 1.0 / (1.0 + jnp.exp(-x))


def _dot(a, b, dims):
    return lax.dot_general(a, b, (dims, ((), ())), preferred_element_type=F32)


def _nn(a, b):
    return _dot(a, b, ((1,), (0,)))


def _nt(a, b):
    return _dot(a, b, ((1,), (1,)))


def _tn(a, b):
    return _dot(a, b, ((0,), (0,)))


_DIMS = {"nn": ((1,), (0,)), "nt": ((1,), (1,)), "tn": ((0,), (0,))}


def _tile(dim, cc=None, cap=1024):
    for t in (2048, 1536, 1024, 768, 512, 384, 256, 128):
        if t <= cap and dim % t == 0 and (cc is None or cc % t == 0):
            return t
    return dim


MM_VMEM = 36 * 1024 * 1024


def _mm_tiles(m, n, k, ccm, ccn, cck, a_bytes, b_bytes, o_bytes):
    caps = [1024, 1024, 2048]
    while True:
        tm, tn, tk = _tile(m, ccm, caps[0]), _tile(n, ccn, caps[1]), _tile(k, cck, caps[2])
        need = 2 * (tm * tk * a_bytes + tk * tn * b_bytes + tm * tn * o_bytes) + (tm * tn * 4 if tk < k else 0)
        if need <= MM_VMEM:
            return tm, tn, tk
        if tk > 1024:
            caps[2] = tk // 2
        elif tn >= tm:
            caps[1] = tn // 2
        else:
            caps[0] = tm // 2


def m2(arr, col_off=0, ncols=None):
    rows, cols = arr.shape
    ncols = cols - col_off if ncols is None else ncols

    def spec(tr, tc, rc):
        assert col_off % tc == 0
        return pl.BlockSpec((tr, tc), lambda *g: (rc(*g)[0], rc(*g)[1] + col_off // tc))
    return (arr, rows, ncols, spec, None if col_off == 0 else col_off)


def mcs(arr):
    cs = arr.shape[2]

    def spec(tr, tc, rc):
        n = cs // tc
        return pl.BlockSpec((None, tr, tc), lambda *g: (rc(*g)[1] // n, rc(*g)[0], rc(*g)[1] % n))
    return (arr, arr.shape[1], 4 * cs, spec, cs)


def out2(rows, cols):
    def spec(tr, tc, rc):
        return pl.BlockSpec((tr, tc), lambda *g: tuple(rc(*g)))
    return ((rows, cols), spec, None)


def outcs(rows, cs):
    def spec(tr, tc, rc):
        n = cs // tc
        return pl.BlockSpec((None, tr, tc), lambda *g: (rc(*g)[1] // n, rc(*g)[0], rc(*g)[1] % n))
    return ((4, rows, cs), spec, cs)


def _both(a, b):
    if a is None:
        return b
    if b is None:
        return a
    return math.gcd(a, b)


def mm(a, b, mode, name, add=None, out=None, out_dtype=F32):
    a_arr, a_r, a_c, a_spec, a_cc = a
    b_arr, b_r, b_c, b_spec, b_cc = b
    if mode == "nn":
        m, k, n = a_r, a_c, b_c
        assert b_r == k
        ccm, cck, ccn = None, a_cc, b_cc
    elif mode == "nt":
        m, k, n = a_r, a_c, b_r
        assert b_c == k
        ccm, cck, ccn = None, _both(a_cc, b_cc), None
    else:
        m, k, n = a_c, a_r, b_c
        assert b_r == k
        ccm, cck, ccn = a_cc, None, b_cc
    out = out2(m, n) if out is None else out
    o_shape, o_spec, o_cc = out
    ccn = _both(ccn, o_cc)
    if add is not None:
        ccn = _both(ccn, add[4])
    o_bytes = jnp.dtype(out_dtype).itemsize + (0 if add is None else add[0].dtype.itemsize)
    tm, tn, tk = _mm_tiles(m, n, k, ccm, ccn, cck, a_arr.dtype.itemsize, b_arr.dtype.itemsize, o_bytes)
    nk = k // tk
    if mode == "nn":
        in_specs = [a_spec(tm, tk, lambda i, j, kk: (i, kk)), b_spec(tk, tn, lambda i, j, kk: (kk, j))]
    elif mode == "nt":
        in_specs = [a_spec(tm, tk, lambda i, j, kk: (i, kk)), b_spec(tn, tk, lambda i, j, kk: (j, kk))]
    else:
        in_specs = [a_spec(tk, tm, lambda i, j, kk: (kk, i)), b_spec(tk, tn, lambda i, j, kk: (kk, j))]
    args = [a_arr, b_arr]
    if add is not None:
        in_specs.append(add[3](tm, tn, lambda i, j, kk: (i, j)))
        args.append(add[0])
    dims = _DIMS[mode]
    has_add = add is not None

    def body(*refs):
        a_ref, b_ref = refs[0], refs[1]
        add_ref = refs[2] if has_add else None
        prod = _dot(a_ref[...].astype(BF16), b_ref[...].astype(BF16), dims)
        if nk == 1:
            o_ref = refs[-1]
            if has_add:
                prod = prod + add_ref[...].astype(F32)
            o_ref[...] = prod.astype(o_ref.dtype)
            return
        o_ref, acc = refs[-2], refs[-1]
        kk = pl.program_id(2)

        @pl.when(kk == 0)
        def _():
            acc[...] = prod

        @pl.when(kk > 0)
        def _():
            acc[...] += prod

        @pl.when(kk == nk - 1)
        def _():
            r = acc[...]
            if has_add:
                r = r + add_ref[...].astype(F32)
            o_ref[...] = r.astype(o_ref.dtype)

    return pl.pallas_call(
        body, name=name, grid=(m // tm, n // tn, nk), in_specs=in_specs,
        out_specs=o_spec(tm, tn, lambda i, j, kk: (i, j)),
        out_shape=jax.ShapeDtypeStruct(o_shape, out_dtype),
        scratch_shapes=[pltpu.VMEM((tm, tn), F32)] if nk > 1 else [],
        compiler_params=_cparams(("parallel", "parallel", "arbitrary")),
    )(*args)


def rw(fn, ins, outs, name, rows, tr=256, consts=(), accs=()):
    n_in, n_c, n_o, n_a = len(ins), len(consts), len(outs), len(accs)
    in_specs = []
    for arr, off, width in ins:
        assert off % width == 0
        in_specs.append(pl.BlockSpec((tr, width), lambda i, o=off // width: (i, o)))
    for c in consts:
        in_specs.append(pl.BlockSpec(c.shape, lambda i: (0, 0)))
    out_specs = [pl.BlockSpec((tr, w), lambda i: (i, 0)) for w, _ in outs]
    out_specs += [pl.BlockSpec(s, lambda i: (0, 0)) for s in accs]
    out_shape = [jax.ShapeDtypeStruct((rows, w), dt) for w, dt in outs]
    out_shape += [jax.ShapeDtypeStruct(s, F32) for s in accs]

    def body(*refs):
        vals = [r[...] for r in refs[:n_in + n_c]]
        o_refs = refs[n_in + n_c:n_in + n_c + n_o]
        a_refs = refs[n_in + n_c + n_o:]
        res = fn(*vals)
        for r, v in zip(o_refs, res[:n_o]):
            r[...] = v.astype(r.dtype)
        if n_a:
            @pl.when(pl.program_id(0) == 0)
            def _():
                for r in a_refs:
                    r[...] = jnp.zeros_like(r)
            for r, v in zip(a_refs, res[n_o:]):
                r[...] += v

    res = pl.pallas_call(
        body, name=name, grid=(rows // tr,), in_specs=in_specs, out_specs=out_specs,
        out_shape=out_shape,
        compiler_params=_cparams(("arbitrary",) if n_a else ("parallel",)),
    )(*[a for a, _, _ in ins], *consts)
    return res


def _rstd(x):
    return lax.rsqrt(jnp.mean(x * x, axis=-1, keepdims=True) + EPS)


def rms_fwd(x, g, name):
    def fn(xv, gv):
        xv = xv.astype(F32)
        return (xv * _rstd(xv) * gv,)
    return rw(fn, [(x, 0, D)], [(D, BF16)], name, x.shape[0], consts=[g])[0]


def _rms_bwd_math(xv, dy, gv):
    r = _rstd(xv)
    dyg = dy * gv
    dx = r * dyg - xv * (r * r * r / D) * jnp.sum(dyg * xv, axis=-1, keepdims=True)
    dg = jnp.sum(dy * xv * r, axis=0, keepdims=True)
    return dx, dg


def rms_bwd(x, dy, dres, g, name):
    def fn(xv, dyv, drv, gv):
        dx, dg = _rms_bwd_math(xv, dyv, gv)
        return dx + drv, dg
    return rw(fn, [(x, 0, D), (dy, 0, D), (dres, 0, D)], [(D, F32)], name, x.shape[0],
              consts=[g], accs=[(1, D)])


def final_loss(x, tgt, g):
    def fn(xv, tv, gv):
        e = xv * _rstd(xv) * gv - tv
        loss = 0.5 * jnp.sum(jnp.sum(e * e, axis=-1, keepdims=True), axis=0, keepdims=True) / D
        dx, dg = _rms_bwd_math(xv, e / D, gv)
        return dx, loss, dg
    return rw(fn, [(x, 0, D), (tgt, 0, D)], [(D, F32)], "final_loss", S, consts=[g],
              accs=[(1, 1), (1, D)])


def _attn_bias(bias_ref):
    ii = lax.broadcasted_iota(jnp.int32, (2 * QB, 2 * QB), 0) % QB
    jj = lax.broadcasted_iota(jnp.int32, (2 * QB, 2 * QB), 1)
    dist = ii + QB - jj
    band = (dist >= 0) & (dist <= QB)
    bias_ref[1] = jnp.where(band, 0.0, NEG)
    bias_ref[0] = jnp.where(band & (jj >= QB), 0.0, NEG)


def _two_heads(x, m0):
    return jnp.concatenate([jnp.where(m0, x, 0.0), jnp.where(m0, 0.0, x)], axis=0)


def _per_head(col, m0):
    return jnp.where(m0, col[:QB], col[QB:])


def _attn_rows(idx, d):
    if d == 1:
        b = idx
        cur = pl.ds(pl.multiple_of(b * QB, QB), QB)
        prev = pl.ds(pl.multiple_of(jnp.maximum(b - 1, 0) * QB, QB), QB)
    else:
        r, b = lax.rem(idx, d), lax.div(idx, d)
        cur = pl.ds(r + b * (QB * d), QB, stride=d)
        prev = pl.ds(r + jnp.maximum(b - 1, 0) * (QB * d), QB, stride=d)
    return cur, prev, b


NBLK = S // QB
GROUP = 4


def _colblk(off):
    return pl.BlockSpec((S, 128), lambda hp: (0, off * 8 + hp))


def attn_fwd(z):
    def body(q_ref, k_ref, v_ref, g_ref, o_ref, l_ref, a_ref, os, ls, bias):
        _attn_bias(bias)
        m0 = lax.broadcasted_iota(jnp.int32, (1, 128), 1) < 64
        for pi, d in enumerate(PATTERNS):
            def load(idx, d=d):
                cur, prev, b = _attn_rows(idx, d)
                return cur, (q_ref[cur, :], k_ref[prev, :], k_ref[cur, :], v_ref[prev, :], v_ref[cur, :],
                             bias[jnp.minimum(b, 1)])

            def block(q, kp, kc, vp, vc, bs):
                qq = _two_heads(q * 0.125, m0).astype(BF16)
                k = jnp.concatenate([kp, kc], axis=0).astype(BF16)
                s = _nt(qq, k) + bs
                mx = jnp.max(s, axis=-1, keepdims=True)
                p = jnp.exp(s - mx)
                den = jnp.sum(p, axis=-1, keepdims=True)
                pb = p.astype(BF16)
                vv = _two_heads(jnp.concatenate([vp, vc], axis=0), m0).astype(BF16)
                o = _nn(jnp.concatenate([pb[:QB], pb[QB:]], axis=1), vv)
                return o * _per_head(1.0 / den, m0), _per_head(mx + jnp.log(den), m0)

            def step(i, carry, pi=pi):
                loaded = [load(i * GROUP + u) for u in range(GROUP)]
                done = [block(*vals) for _, vals in loaded]
                for (cur, _), (o, l) in zip(loaded, done):
                    os[pi, cur, :] = o
                    ls[pi, cur, :] = l
                return carry
            lax.fori_loop(0, NBLK // GROUP, step, 0)
        l1, l2, l3 = ls[0], ls[1], ls[2]
        mx = jnp.maximum(jnp.maximum(l1, l2), l3)
        e1, e2, e3 = jnp.exp(l1 - mx), jnp.exp(l2 - mx), jnp.exp(l3 - mx)
        tot = e1 + e2 + e3
        o = (os[0] * e1 + os[1] * e2 + os[2] * e3) / tot
        ga = g_ref[...]
        o_ref[...] = o
        l_ref[...] = mx + jnp.log(tot)
        a_ref[...] = (o * (ga * _sig(ga))).astype(a_ref.dtype)

    out = pl.BlockSpec((S, 128), lambda hp: (0, hp))
    return pl.pallas_call(
        body, name="attn_fwd", grid=(8,),
        in_specs=[_colblk(0), _colblk(1), _colblk(2), _colblk(3)], out_specs=[out] * 3,
        out_shape=[jax.ShapeDtypeStruct((S, D), F32), jax.ShapeDtypeStruct((S, D), F32),
                   jax.ShapeDtypeStruct((S, 2 * D), BF16)],
        scratch_shapes=[pltpu.VMEM((3, S, 128), F32), pltpu.VMEM((3, S, 128), F32),
                        pltpu.VMEM((2, 2 * QB, 2 * QB), F32)],
        compiler_params=_cparams(("parallel",)),
    )(z, z, z, z)


def attn_bwd(z, d_cat, o, lse):
    def body(q_ref, k_ref, v_ref, g_ref, da_ref, o_ref, l_ref, dq_ref, dk_ref, dv_ref, dg_ref, do_s, pr_s, bias):
        _attn_bias(bias)
        m0 = lax.broadcasted_iota(jnp.int32, (1, 128), 1) < 64
        ga = g_ref[...]
        sg = _sig(ga)
        da = da_ref[...]
        ov = o_ref[...]
        do = da * (ga * sg)
        dg_ref[...] = da * ov * (sg * (1.0 + ga * (1.0 - sg)))
        do_s[...] = do
        pr_s[...] = do * ov
        dq_ref[...] = jnp.zeros_like(dq_ref)
        dk_ref[...] = jnp.zeros_like(dk_ref)
        dv_ref[...] = jnp.zeros_like(dv_ref)
        for d in PATTERNS:
            def load(idx, d=d):
                cur, prev, b = _attn_rows(idx, d)
                return (cur, prev), (q_ref[cur, :], k_ref[prev, :], k_ref[cur, :], v_ref[prev, :], v_ref[cur, :],
                                     do_s[cur, :], pr_s[cur, :], l_ref[cur, :], bias[jnp.minimum(b, 1)])

            def block(q, kp, kc, vp, vc, dof, prod, lp, bs):
                qq = _two_heads(q * 0.125, m0).astype(BF16)
                kf = jnp.concatenate([kp, kc], axis=0)
                k = kf.astype(BF16)
                v = jnp.concatenate([vp, vc], axis=0).astype(BF16)
                dd = _two_heads(dof, m0).astype(BF16)
                lh = jnp.max(jnp.concatenate([jnp.where(m0, lp, -jnp.inf), jnp.where(m0, -jnp.inf, lp)], axis=0),
                             axis=-1, keepdims=True)
                delta = jnp.sum(_two_heads(prod, m0), axis=-1, keepdims=True)
                p = jnp.exp(_nt(qq, k) + bs - lh)
                ds = (p * (_nt(dd, v) - delta)).astype(BF16)
                dq = _nn(jnp.concatenate([ds[:QB], ds[QB:]], axis=1), _two_heads(kf, m0).astype(BF16))
                return dq * 0.125, _tn(ds, qq), _tn(p.astype(BF16), dd)

            def step(i, carry):
                loaded = [load(i * GROUP + u) for u in range(GROUP)]
                done = [block(*vals) for _, vals in loaded]
                for ((cur, prev), _), (dq, dk, dv) in zip(loaded, done):
                    dq_ref[cur, :] = dq_ref[cur, :] + dq
                    dk_ref[prev, :] = dk_ref[prev, :] + dk[:QB]
                    dv_ref[prev, :] = dv_ref[prev, :] + dv[:QB]
                    dk_ref[cur, :] = dk_ref[cur, :] + dk[QB:]
                    dv_ref[cur, :] = dv_ref[cur, :] + dv[QB:]
                return carry
            lax.fori_loop(0, NBLK // GROUP, step, 0)

    blk = pl.BlockSpec((S, 128), lambda hp: (0, hp))
    return pl.pallas_call(
        body, name="attn_bwd", grid=(8,),
        in_specs=[_colblk(0), _colblk(1), _colblk(2), _colblk(3), blk, blk, blk], out_specs=[blk] * 4,
        out_shape=[jax.ShapeDtypeStruct((S, D), F32)] * 4,
        scratch_shapes=[pltpu.VMEM((S, 128), F32), pltpu.VMEM((S, 128), F32), pltpu.VMEM((2, 2 * QB, 2 * QB), F32)],
        compiler_params=_cparams(("parallel",)),
    )(z, z, z, z, d_cat, o, lse)


def assemble_dz_even(parts):
    def body(*refs):
        o_ref = refs[-1]
        for j in range(6):
            o_ref[:, j * D:(j + 1) * D] = refs[j][...].astype(o_ref.dtype)
    tr = 256
    blk = pl.BlockSpec((tr, D), lambda i: (i, 0))
    return pl.pallas_call(
        body, name="assemble_dz_even", grid=(S // tr,), in_specs=[blk] * 6,
        out_specs=pl.BlockSpec((tr, 6 * D), lambda i: (i, 0)),
        out_shape=jax.ShapeDtypeStruct((S, 6 * D), BF16),
        compiler_params=_cparams(("parallel",)),
    )(*parts)


def _pool_window(g):
    return jnp.where(g == 0, 2.0, jnp.where(g == 1, 4.0, jnp.where(g == 2, 8.0, 16.0)))


def _pool_sel(g, levels):
    return jnp.where(g == 0, levels[0], jnp.where(g == 1, levels[1], jnp.where(g == 2, levels[2], levels[3])))


def _pool_fwd_math(v, g):
    t = lax.broadcasted_iota(jnp.int32, (S, 1), 0)
    s = v
    levels = []
    for k in (1, 2, 4, 8):
        s = s + jnp.where(t >= k, pltpu.roll(s, k, 0), 0.0)
        levels.append(s)
    cnt = jnp.minimum((t + 1).astype(F32), _pool_window(g))
    return _pool_sel(g, levels) / cnt - v, cnt


def pool_fwd(z, pw, ps, cat):
    def body(v_ref, g_ref, pw_ref, ps_ref, cat_ref, o_ref):
        g = pl.program_id(0)
        pooled, _ = _pool_fwd_math(v_ref[...], g)
        mixed = _nn(pooled.astype(BF16), pw_ref[...].astype(BF16))
        gb = g_ref[...]
        o_ref[...] = (mixed * ps_ref[...] * (gb * _sig(gb))).astype(o_ref.dtype)

    return pl.pallas_call(
        body, name="pool_fwd", grid=(4,),
        in_specs=[pl.BlockSpec((S, 256), lambda g: (0, 16 + g)),
                  pl.BlockSpec((S, 256), lambda g: (0, 20 + g)),
                  pl.BlockSpec((None, 256, 256), lambda g: (g, 0, 0)),
                  pl.BlockSpec((1, 256), lambda g: (0, g)), pl.BlockSpec(memory_space=pl.ANY)],
        out_specs=pl.BlockSpec((S, 256), lambda g: (0, 4 + g)),
        out_shape=jax.ShapeDtypeStruct((S, 2 * D), BF16),
        input_output_aliases={4: 0},
        compiler_params=_cparams(("parallel",)),
    )(z, z, pw, ps, cat)


def pool_bwd(z, d_cat, pw, ps):
    def body(v_ref, g_ref, d_ref, pw_ref, ps_ref, dv_ref, dg_ref, dpw_ref, dps_ref):
        g = pl.program_id(0)
        v = v_ref[...]
        pooled, cnt = _pool_fwd_math(v, g)
        pwb = pw_ref[...].astype(BF16)
        pb = pooled.astype(BF16)
        mixed = _nn(pb, pwb)
        gb = g_ref[...]
        sg = _sig(gb)
        dout = d_ref[...]
        sc = ps_ref[...]
        dg_ref[...] = dout * mixed * sc * (sg * (1.0 + gb * (1.0 - sg)))
        dms = dout * (gb * sg)
        dps_ref[...] = jnp.sum(dms * mixed, axis=0, keepdims=True)
        dmx = (dms * sc).astype(BF16)
        dpw_ref[...] = _tn(pb, dmx)
        dpooled = _nt(dmx, pwb)
        t = lax.broadcasted_iota(jnp.int32, (S, 1), 0)
        s = dpooled / cnt
        levels = []
        for k in (1, 2, 4, 8):
            s = s + jnp.where(t < S - k, pltpu.roll(s, S - k, 0), 0.0)
            levels.append(s)
        dv_ref[...] = _pool_sel(g, levels) - dpooled

    return pl.pallas_call(
        body, name="pool_bwd", grid=(4,),
        in_specs=[pl.BlockSpec((S, 256), lambda g: (0, 16 + g)),
                  pl.BlockSpec((S, 256), lambda g: (0, 20 + g)),
                  pl.BlockSpec((S, 256), lambda g: (0, 4 + g)),
                  pl.BlockSpec((None, 256, 256), lambda g: (g, 0, 0)),
                  pl.BlockSpec((1, 256), lambda g: (0, g))],
        out_specs=[pl.BlockSpec((S, 256), lambda g: (0, g)),
                   pl.BlockSpec((S, 256), lambda g: (0, g)),
                   pl.BlockSpec((None, 256, 256), lambda g: (g, 0, 0)),
                   pl.BlockSpec((1, 256), lambda g: (0, g))],
        out_shape=[jax.ShapeDtypeStruct((S, D), F32), jax.ShapeDtypeStruct((S, D), F32),
                   jax.ShapeDtypeStruct((4, 256, 256), F32), jax.ShapeDtypeStruct((1, D), F32)],
        compiler_params=_cparams(("parallel",)),
    )(z, z, d_cat, pw, ps)


CH = 128


def _sgu_common(v, lng, lnb, w_ref):
    mu = jnp.mean(v, axis=-1, keepdims=True)
    vc = v - mu
    rs = lax.rsqrt(jnp.mean(vc * vc, axis=-1, keepdims=True) + EPS)
    xhat = vc * rs
    vn = (xhat * lng + lnb).astype(BF16)
    ri = lax.broadcasted_iota(jnp.int32, (CH, CH), 0)
    ci = lax.broadcasted_iota(jnp.int32, (CH, CH), 1)
    tril = ri >= ci
    ws = [jnp.where(tril, w_ref[g], 0.0).astype(BF16) for g in range(4)]
    return xhat, rs, vn, tril, ws


def _zspec(off):
    return pl.BlockSpec((CH, D), lambda c: (c, off))


def _full(shape):
    return pl.BlockSpec(shape, lambda c: (0,) * len(shape))


def sgu_fwd(z, lng, lnb, w, bfull):
    def body(u_ref, v_ref, g_ref, lng_ref, lnb_ref, w_ref, b_ref, o_ref):
        _, _, vn, _, ws = _sgu_common(v_ref[...], lng_ref[...], lnb_ref[...], w_ref)
        for g in range(4):
            sl = slice(g * 256, (g + 1) * 256)
            mixed = _nn(ws[g], vn[:, sl]) + b_ref[:, sl]
            gc = g_ref[:, sl]
            o_ref[:, sl] = (u_ref[:, sl] * mixed * (gc * _sig(gc))).astype(o_ref.dtype)

    return pl.pallas_call(
        body, name="sgu_fwd", grid=(S // CH,),
        in_specs=[_zspec(0), _zspec(1), _zspec(2), _full((1, D)), _full((1, D)),
                  _full((4, CH, CH)), _full((CH, D))],
        out_specs=pl.BlockSpec((CH, D), lambda c: (c, 0)),
        out_shape=jax.ShapeDtypeStruct((S, D), BF16),
        compiler_params=_cparams(("parallel",)),
    )(z, z, z, lng, lnb, w, bfull)


def sgu_bwd(z, d_cat, lng, lnb, w, bfull):
    def body(u_ref, v_ref, g_ref, d_ref, lng_ref, lnb_ref, w_ref, b_ref,
             du_ref, dv_ref, dg_ref, dw_ref, db_ref, dlg_ref, dlb_ref):
        @pl.when(pl.program_id(0) == 0)
        def _():
            dw_ref[...] = jnp.zeros_like(dw_ref)
            db_ref[...] = jnp.zeros_like(db_ref)
            dlg_ref[...] = jnp.zeros_like(dlg_ref)
            dlb_ref[...] = jnp.zeros_like(dlb_ref)

        lng = lng_ref[...]
        xhat, rs, vn, tril, ws = _sgu_common(v_ref[...], lng, lnb_ref[...], w_ref)
        lane = lax.broadcasted_iota(jnp.int32, (1, 128), 1)
        db = jnp.zeros((CH, 128), F32)
        dvn_parts = []
        for g in range(4):
            sl = slice(g * 256, (g + 1) * 256)
            mixed = _nn(ws[g], vn[:, sl]) + b_ref[:, sl]
            gc = g_ref[:, sl]
            sg = _sig(gc)
            u = u_ref[:, sl]
            dc = d_ref[:, sl]
            du_ref[:, sl] = dc * mixed * (gc * sg)
            dg_ref[:, sl] = dc * u * mixed * (sg * (1.0 + gc * (1.0 - sg)))
            dmx = dc * u * (gc * sg)
            db = db + jnp.where(lane == g, jnp.sum(dmx, axis=-1, keepdims=True), 0.0)
            dmb = dmx.astype(BF16)
            dw_ref[g] += jnp.where(tril, _nt(dmb, vn[:, sl]), 0.0)
            dvn_parts.append(_tn(ws[g], dmb))
        db_ref[...] += db
        dvn = jnp.concatenate(dvn_parts, axis=1)
        dlb_ref[...] += jnp.sum(dvn, axis=0, keepdims=True)
        dlg_ref[...] += jnp.sum(dvn * xhat, axis=0, keepdims=True)
        dxh = dvn * lng
        dv_ref[...] = rs * (dxh - jnp.mean(dxh, axis=-1, keepdims=True)
                            - xhat * jnp.mean(dxh * xhat, axis=-1, keepdims=True))

    row = pl.BlockSpec((CH, D), lambda c: (c, 0))
    return pl.pallas_call(
        body, name="sgu_bwd", grid=(S // CH,),
        in_specs=[_zspec(0), _zspec(1), _zspec(2), row, _full((1, D)), _full((1, D)),
                  _full((4, CH, CH)), _full((CH, D))],
        out_specs=[row, row, row, _full((4, CH, CH)), _full((CH, 128)), _full((1, D)), _full((1, D))],
        out_shape=[jax.ShapeDtypeStruct((S, D), F32)] * 3
        + [jax.ShapeDtypeStruct((4, CH, CH), F32), jax.ShapeDtypeStruct((CH, 128), F32),
           jax.ShapeDtypeStruct((1, D), F32), jax.ShapeDtypeStruct((1, D), F32)],
        compiler_params=_cparams(("arbitrary",)),
    )(z, z, z, d_cat, lng, lnb, w, bfull)


TB = 256


def _cmul(ar, ai, br, bi):
    return ar * br - ai * bi, ar * bi + ai * br


def _scan_consts(ar, ai, reverse):
    a2 = _cmul(ar, ai, ar, ai)
    a4 = _cmul(*a2, *a2)
    row = lax.broadcasted_iota(jnp.int32, (8, NS), 0)
    pr = jnp.zeros((8, NS), F32)
    pi = jnp.zeros((8, NS), F32)
    cr, ci = ar, ai
    for r in range(8):
        sel = row == (7 - r if reverse else r)
        pr = jnp.where(sel, cr, pr)
        pi = jnp.where(sel, ci, pi)
        cr, ci = _cmul(cr, ci, ar, ai)
    return ((ar, ai), a2, a4), (pr, pi), row


def scan_fwd(bu, abr, abi):
    def body(bu_ref, ar_ref, ai_ref, h_ref, car, cai):
        @pl.when(pl.program_id(0) == 0)
        def _():
            car[...] = jnp.zeros_like(car)
            cai[...] = jnp.zeros_like(cai)

        pows, (pr, pi), row = _scan_consts(ar_ref[...], ai_ref[...], False)

        def tile(t, carry):
            c_r, c_i = carry
            rows = pl.ds(pl.multiple_of(t * 8, 8), 8)
            xr = bu_ref[rows, 0:NS]
            xi = bu_ref[rows, NS:2 * NS]
            for k, (kr, ki) in zip((1, 2, 4), pows):
                sr = jnp.where(row >= k, pltpu.roll(xr, k, 0), 0.0)
                si = jnp.where(row >= k, pltpu.roll(xi, k, 0), 0.0)
                xr, xi = xr + kr * sr - ki * si, xi + kr * si + ki * sr
            xr, xi = xr + pr * c_r - pi * c_i, xi + pr * c_i + pi * c_r
            h_ref[rows, 0:NS] = xr
            h_ref[rows, NS:2 * NS] = xi
            return (jnp.broadcast_to(xr[7:8, :], (8, NS)), jnp.broadcast_to(xi[7:8, :], (8, NS)))

        c_r, c_i = lax.fori_loop(0, TB // 8, tile, (car[...], cai[...]))
        car[...] = c_r
        cai[...] = c_i

    return pl.pallas_call(
        body, name="s5_scan_fwd", grid=(S // TB,),
        in_specs=[pl.BlockSpec((TB, 2 * NS), lambda i: (i, 0)),
                  pl.BlockSpec((1, NS), lambda i: (0, 0)), pl.BlockSpec((1, NS), lambda i: (0, 0))],
        out_specs=pl.BlockSpec((TB, 2 * NS), lambda i: (i, 0)),
        out_shape=jax.ShapeDtypeStruct((S, 2 * NS), F32),
        scratch_shapes=[pltpu.VMEM((8, NS), F32), pltpu.VMEM((8, NS), F32)],
        compiler_params=_cparams(("arbitrary",)),
    )(bu, abr, abi)


def scan_bwd(eta, h, abr, abi):
    nt = S // TB

    def body(e_ref, h_ref, ar_ref, ai_ref, l_ref, da_ref, car, cai):
        @pl.when(pl.program_id(0) == 0)
        def _():
            car[...] = jnp.zeros_like(car)
            cai[...] = jnp.zeros_like(cai)
            da_ref[...] = jnp.zeros_like(da_ref)

        pows, (pr, pi), row = _scan_consts(ar_ref[...], -ai_ref[...], True)

        def tile(tt, carry):
            c_r, c_i, acr, aci = carry
            t = TB // 8 - 1 - tt
            rows = pl.ds(pl.multiple_of(t * 8, 8), 8)
            xr = e_ref[rows, 0:NS]
            xi = e_ref[rows, NS:2 * NS]
            for k, (kr, ki) in zip((1, 2, 4), pows):
                sr = jnp.where(row < 8 - k, pltpu.roll(xr, 8 - k, 0), 0.0)
                si = jnp.where(row < 8 - k, pltpu.roll(xi, 8 - k, 0), 0.0)
                xr, xi = xr + kr * sr - ki * si, xi + kr * si + ki * sr
            xr, xi = xr + pr * c_r - pi * c_i, xi + pr * c_i + pi * c_r
            l_ref[rows, 0:NS] = xr
            l_ref[rows, NS:2 * NS] = xi
            nr = jnp.where(row < 7, pltpu.roll(xr, 7, 0), c_r)
            ni = jnp.where(row < 7, pltpu.roll(xi, 7, 0), c_i)
            hr = h_ref[rows, 0:NS]
            hi = h_ref[rows, NS:2 * NS]
            acr = acr + hr * nr + hi * ni
            aci = aci + hr * ni - hi * nr
            return (jnp.broadcast_to(xr[0:1, :], (8, NS)), jnp.broadcast_to(xi[0:1, :], (8, NS)), acr, aci)

        zero = jnp.zeros((8, NS), F32)
        c_r, c_i, acr, aci = lax.fori_loop(0, TB // 8, tile, (car[...], cai[...], zero, zero))
        car[...] = c_r
        cai[...] = c_i
        da_ref[:, 0:NS] += acr
        da_ref[:, NS:2 * NS] += aci

    rev = pl.BlockSpec((TB, 2 * NS), lambda i: (nt - 1 - i, 0))
    return pl.pallas_call(
        body, name="s5_scan_bwd", grid=(nt,),
        in_specs=[rev, rev, pl.BlockSpec((1, NS), lambda i: (0, 0)), pl.BlockSpec((1, NS), lambda i: (0, 0))],
        out_specs=[rev, pl.BlockSpec((8, 2 * NS), lambda i: (0, 0))],
        out_shape=[jax.ShapeDtypeStruct((S, 2 * NS), F32), jax.ShapeDtypeStruct((8, 2 * NS), F32)],
        scratch_shapes=[pltpu.VMEM((8, NS), F32), pltpu.VMEM((8, NS), F32)],
        compiler_params=_cparams(("arbitrary",)),
    )(eta, h, abr, abi)


GC = 0.7978845608028654
GA = 0.044715


def s5_post(hc, z, dskip):
    def fn(hv, xd, dv):
        y = hv + dv * xd
        return y, 0.5 * y * (1.0 + jnp.tanh(GC * (y + GA * y * y * y)))
    return rw(fn, [(hc, 0, 512), (z, 3072, 512)], [(512, F32), (512, BF16)], "s5_post", S, consts=[dskip])


def s5_post_bwd(dyg, ypre, z, dskip):
    def fn(dy, y, xd, dv):
        th = jnp.tanh(GC * (y + GA * y * y * y))
        dg = 0.5 * (1.0 + th) + 0.5 * y * (1.0 - th * th) * GC * (1.0 + 3.0 * GA * y * y)
        dyp = dy * dg
        return dyp, dyp * dv, jnp.sum(dyp * xd, axis=0, keepdims=True)
    return rw(fn, [(dyg, 0, 512), (ypre, 0, 512), (z, 3072, 512)], [(512, BF16), (512, F32)],
              "s5_post_bwd", S, consts=[dskip], accs=[(1, 512)])


def glu_fwd(t, z, c_out):
    def fn(t1, t2, gd, co):
        return (jnp.concatenate([co, (t1 * _sig(t2) * (gd * _sig(gd))).astype(BF16)], axis=1),)
    return rw(fn, [(t, 0, 512), (t, 512, 512), (z, 3584, 512), (c_out, 0, D)], [(D + 512, BF16)], "glu_fwd", S)[0]


def glu_bwd(t, z, d_cat):
    def fn(t1, t2, gd, dd):
        s2, sg = _sig(t2), _sig(gd)
        sl = gd * sg
        return (jnp.concatenate([dd * s2 * sl, dd * t1 * s2 * (1.0 - s2) * sl], axis=1),
                dd * t1 * s2 * (sg * (1.0 + gd * (1.0 - sg))))
    return rw(fn, [(t, 0, 512), (t, 512, 512), (z, 3584, 512), (d_cat, 1024, 512)],
              [(D, BF16), (512, F32)], "glu_bwd", S)


def assemble_dz_odd(du, dv, dgc, dxd, dgd):
    def body(a, b, c, d, e, o_ref):
        o_ref[:, 0:D] = a[...].astype(BF16)
        o_ref[:, D:2 * D] = b[...].astype(BF16)
        o_ref[:, 2 * D:3 * D] = c[...].astype(BF16)
        o_ref[:, 3 * D:3 * D + 512] = d[...].astype(BF16)
        o_ref[:, 3 * D + 512:4 * D] = e[...].astype(BF16)
    tr = 256
    blk = pl.BlockSpec((tr, D), lambda i: (i, 0))
    half = pl.BlockSpec((tr, 512), lambda i: (i, 0))
    return pl.pallas_call(
        body, name="assemble_dz_odd", grid=(S // tr,), in_specs=[blk, blk, blk, half, half],
        out_specs=pl.BlockSpec((tr, 4 * D), lambda i: (i, 0)),
        out_shape=jax.ShapeDtypeStruct((S, 4 * D), BF16),
        compiler_params=_cparams(("parallel",)),
    )(du, dv, dgc, dxd, dgd)


TQ = 256


def _xattn_probs(qh, kh):
    s = _nt(qh, kh) * 0.0625
    p = jnp.exp(s - jnp.max(s, axis=-1, keepdims=True))
    return p / jnp.sum(p, axis=-1, keepdims=True)


def xattn_fwd(q, kv):
    def body(q_ref, kv_ref, o_ref):
        for h in range(4):
            sl = slice(h * 256, (h + 1) * 256)
            p = _xattn_probs(q_ref[:, sl].astype(BF16), kv_ref[:, sl].astype(BF16))
            vh = kv_ref[:, D + h * 256:D + (h + 1) * 256].astype(BF16)
            o_ref[:, sl] = _nn(p.astype(BF16), vh).astype(o_ref.dtype)

    return pl.pallas_call(
        body, name="xattn_fwd", grid=(S // TQ,),
        in_specs=[pl.BlockSpec((TQ, D), lambda i: (i, 0)), pl.BlockSpec((MEM, 2 * D), lambda i: (0, 0))],
        out_specs=pl.BlockSpec((TQ, D), lambda i: (i, 0)),
        out_shape=jax.ShapeDtypeStruct((S, D), BF16),
        compiler_params=_cparams(("parallel",)),
    )(q, kv)


def xattn_bwd(q, kv, d_o):
    def body(q_ref, kv_ref, do_ref, dq_ref, dkv_ref):
        @pl.when(pl.program_id(0) == 0)
        def _():
            dkv_ref[...] = jnp.zeros_like(dkv_ref)

        for h in range(4):
            sl = slice(h * 256, (h + 1) * 256)
            vs = slice(D + h * 256, D + (h + 1) * 256)
            qh = q_ref[:, sl].astype(BF16)
            kh = kv_ref[:, sl].astype(BF16)
            vh = kv_ref[:, vs].astype(BF16)
            doh = do_ref[:, sl].astype(BF16)
            p = _xattn_probs(qh, kh)
            dp = _nt(doh, vh)
            ds = (p * (dp - jnp.sum(p * dp, axis=-1, keepdims=True)) * 0.0625).astype(BF16)
            dq_ref[:, sl] = _nn(ds, kh).astype(dq_ref.dtype)
            dkv_ref[:, sl] += _tn(ds, qh)
            dkv_ref[:, vs] += _tn(p.astype(BF16), doh)

    return pl.pallas_call(
        body, name="xattn_bwd", grid=(S // TQ,),
        in_specs=[pl.BlockSpec((TQ, D), lambda i: (i, 0)), pl.BlockSpec((MEM, 2 * D), lambda i: (0, 0)),
                  pl.BlockSpec((TQ, D), lambda i: (i, 0))],
        out_specs=[pl.BlockSpec((TQ, D), lambda i: (i, 0)), pl.BlockSpec((MEM, 2 * D), lambda i: (0, 0))],
        out_shape=[jax.ShapeDtypeStruct((S, D), BF16), jax.ShapeDtypeStruct((MEM, 2 * D), F32)],
        compiler_params=_cparams(("arbitrary",)),
    )(q, kv, d_o)


def _s5_disc(a_re, a_im, log_dt, b_re, b_im):
    dt = jnp.exp(log_dt)[:, None]
    mag = jnp.exp(dt * a_re)
    abr = mag * jnp.cos(dt * a_im)
    abi = mag * jnp.sin(dt * a_im)
    nr, ni = abr - 1.0, abi
    inv = 1.0 / (a_re * a_re + a_im * a_im)
    cr = (nr * a_re + ni * a_im) * inv
    ci = (ni * a_re - nr * a_im) * inv
    bbr = cr[..., None] * b_re - ci[..., None] * b_im
    bbi = cr[..., None] * b_im + ci[..., None] * b_re
    return abr, abi, bbr, bbi


def _blockdiag(t):
    g, a, b = t.shape
    eye = jnp.eye(g, dtype=t.dtype)
    return (eye[:, None, :, None] * t[:, :, None, :]).reshape(g * a, g * b)


def _blocks(mat, a, b):
    return jnp.einsum("gagb->gab", mat.reshape(NG, a, NG, b))


def _fwd_even(i, x, P, W):
    hn = rms_fwd(x, P["norm_ab"][i:i + 1], "rms_ab_fwd")
    z = mm(m2(hn), W["w_in"], "nn", "in_ab")
    o, lse, cat = attn_fwd(z)
    if "more" in W:
        W.update(W.pop("more")(cat))
    cat = pool_fwd(z, W["pool_w"], P["pool_scale"][i:i + 1], cat)
    x_mid = mm(m2(cat), W["w_out"], "nn", "out_ab", add=m2(x))
    return x_mid, dict(x=x, hn=hn, z=z, o=o, lse=lse, cat=cat)


def _bwd_even(i, dx_mid, sv, P, W, G, GW):
    z = sv["z"]
    d_cat = mm(m2(dx_mid), W["w_out"], "nt", "out_ab_dx")
    GW["w_out"] = mm(m2(sv["cat"]), m2(dx_mid), "tn", "out_ab_dw").reshape(4, 512, D)
    dq, dk, dv, dga = attn_bwd(z, d_cat, sv["o"], sv["lse"])
    dvb, dgb, dpw, dps = pool_bwd(z, d_cat, W["pool_w"], P["pool_scale"][i:i + 1])
    GW["pool_w"] = dpw.reshape(4, 4, 64, 256).transpose(1, 0, 2, 3).reshape(4, 256, 256)
    G["pool_scale"][i] = dps[0]
    d_z = assemble_dz_even((dq, dk, dv, dga, dvb, dgb))
    d_hn = mm(m2(d_z), W["w_in"], "nt", "in_ab_dx")
    GW["w_in"] = mm(m2(sv["hn"]), m2(d_z), "tn", "in_ab_dw", out=outcs(D, 1536))
    return d_hn, P["norm_ab"][i:i + 1], "norm_ab", "rms_ab_bwd"


def _fwd_odd(i, x, P, W):
    hn = rms_fwd(x, P["norm_cd"][i:i + 1], "rms_cd_fwd")
    z = mm(m2(hn), W["w_in"], "nn", "in_cd")
    bfull = jnp.repeat(P["sgu_b"][i].T, 256, axis=1)
    c_out = sgu_fwd(z, P["sgu_ln_g"][i:i + 1], P["sgu_ln_b"][i:i + 1], P["sgu_w"][i], bfull)
    disc, disc_vjp = jax.vjp(_s5_disc, P["s5_a_re"][i], P["s5_a_im"][i], P["s5_log_dt"][i],
                             P["s5_b_re"][i], P["s5_b_im"][i])
    abr, abi, bbr, bbi = disc
    bbd = jnp.concatenate([_blockdiag(bbr.transpose(0, 2, 1)), _blockdiag(bbi.transpose(0, 2, 1))], axis=1)
    cbd = jnp.concatenate([_blockdiag(P["s5_c_re"][i].transpose(0, 2, 1)),
                           -_blockdiag(P["s5_c_im"][i].transpose(0, 2, 1))], axis=0)
    abr, abi = abr.reshape(1, NS), abi.reshape(1, NS)
    bu = mm(m2(z, 3072, 512), m2(bbd), "nn", "s5_bu")
    h = scan_fwd(bu, abr, abi)
    hc = mm(m2(h), m2(cbd), "nn", "s5_hc")
    dskip = P["s5_d"][i:i + 1]
    ypre, yg = s5_post(hc, z, dskip)
    if "more" in W:
        W.update(W.pop("more")(yg))
    w12 = W["w12"]
    t = mm(m2(yg), m2(w12), "nn", "glu_t")
    cat = glu_fwd(t, z, c_out)
    x_mid = mm(m2(cat), W["w_out"], "nn", "out_cd", add=m2(x))
    return x_mid, dict(x=x, hn=hn, z=z, bfull=bfull, disc_vjp=disc_vjp, bbd=bbd, cbd=cbd, abr=abr,
                       abi=abi, h=h, ypre=ypre, yg=yg, w12=w12, t=t, cat=cat, dskip=dskip)


def _bwd_odd(i, dx_mid, sv, P, W, G, GW):
    z = sv["z"]
    d_cat = mm(m2(dx_mid), W["w_out"], "nt", "out_cd_dx")
    GW["w_out"] = mm(m2(sv["cat"]), m2(dx_mid), "tn", "out_cd_dw").reshape(4, 384, D)
    du, dv, dgc, dws, dbs, dlg, dlb = sgu_bwd(z, d_cat, P["sgu_ln_g"][i:i + 1], P["sgu_ln_b"][i:i + 1],
                                               P["sgu_w"][i], sv["bfull"])
    G["sgu_w"][i], G["sgu_b"][i] = dws, dbs[:, :4].T
    G["sgu_ln_g"][i], G["sgu_ln_b"][i] = dlg[0], dlb[0]
    dt, dgd = glu_bwd(sv["t"], z, d_cat)
    gw12 = mm(m2(sv["yg"]), m2(dt), "tn", "glu_dw")
    GW["glu_w1"] = gw12[:, :512].reshape(4, 128, 512)
    GW["glu_w2"] = gw12[:, 512:].reshape(4, 128, 512)
    dyg = mm(m2(dt), m2(sv["w12"]), "nt", "glu_dx")
    dypre, dxd1, dd = s5_post_bwd(dyg, sv["ypre"], z, sv["dskip"])
    G["s5_d"][i] = dd[0]
    gcbd = mm(m2(sv["h"]), m2(dypre), "tn", "s5_dc")
    G["s5_c_re"][i] = _blocks(gcbd[:NS], NP, NH).transpose(0, 2, 1)
    G["s5_c_im"][i] = -_blocks(gcbd[NS:], NP, NH).transpose(0, 2, 1)
    eta = mm(m2(dypre), m2(sv["cbd"]), "nt", "s5_eta")
    lam, dacc = scan_bwd(eta, sv["h"], sv["abr"], sv["abi"])
    gbbd = mm(m2(z, 3072, 512), m2(lam), "tn", "s5_db")
    dxd = mm(m2(lam), m2(sv["bbd"]), "nt", "s5_dx", add=m2(dxd1))
    dacc = jnp.sum(dacc, axis=0)
    d_bbr = _blocks(gbbd[:, :NS], NH, NP).transpose(0, 2, 1)
    d_bbi = _blocks(gbbd[:, NS:], NH, NP).transpose(0, 2, 1)
    (G["s5_a_re"][i], G["s5_a_im"][i], G["s5_log_dt"][i], G["s5_b_re"][i], G["s5_b_im"][i]) = sv["disc_vjp"](
        (dacc[:NS].reshape(NG, NP), dacc[NS:].reshape(NG, NP), d_bbr, d_bbi))
    d_z = assemble_dz_odd(du, dv, dgc, dxd, dgd)
    d_hn = mm(m2(d_z), W["w_in"], "nt", "in_cd_dx")
    GW["w_in"] = mm(m2(sv["hn"]), m2(d_z), "tn", "in_cd_dw", out=outcs(D, 1024))
    return d_hn, P["norm_cd"][i:i + 1], "norm_cd", "rms_cd_bwd"


def _fwd_x(l, x, mem_n, P, W):
    hx = rms_fwd(x, P["norm_x"][l:l + 1], "rms_x_fwd")
    q = mm(m2(hx), W["w_xq"], "nn", "xq", out_dtype=BF16)
    kv = mm(m2(mem_n), W["w_xkv"], "nn", "xkv", out_dtype=BF16)
    ox = xattn_fwd(q, kv)
    x_out = mm(m2(ox), W["w_xo"], "nn", "xo", add=m2(x))
    return x_out, dict(x=x, hx=hx, q=q, kv=kv, ox=ox)


def _bwd_x(l, dx_out, sv, mem_n, d_memn, P, W, G, GW):
    d_ox = mm(m2(dx_out), W["w_xo"], "nt", "xo_dx", out_dtype=BF16)
    GW["w_xo"] = mm(m2(sv["ox"]), m2(dx_out), "tn", "xo_dw").reshape(4, 256, D)
    dq, dkv = xattn_bwd(sv["q"], sv["kv"], d_ox)
    GW["w_xq"] = mm(m2(sv["hx"]), m2(dq), "tn", "xq_dw").reshape(4, 256, D)
    d_hx = mm(m2(dq), W["w_xq"], "nt", "xq_dx")
    GW["w_xkv"] = mm(m2(mem_n), m2(dkv), "tn", "xkv_dw", out=outcs(D, 512))
    d_memn = mm(m2(dkv), W["w_xkv"], "nt", "xkv_dx", add=None if d_memn is None else m2(d_memn))
    dx, dg = rms_bwd(sv["x"], d_hx, dx_out, P["norm_x"][l:l + 1], "rms_x_bwd")
    G["norm_x"][l] = dg[0]
    return dx, d_memn


SMALL_LAYERS = (("norm_ab", 2), ("pool_scale", 2), ("norm_cd", 2), ("sgu_ln_g", 2), ("sgu_ln_b", 2), ("sgu_w", 2),
                ("sgu_b", 2), ("s5_a_re", 2), ("s5_a_im", 2), ("s5_log_dt", 2), ("s5_b_re", 2), ("s5_b_im", 2),
                ("s5_c_re", 2), ("s5_c_im", 2), ("s5_d", 2), ("norm_x", 4))


def local_step(x, mem, tgt, P, weights_of, grads_done):
    G = {k: [None] * n for k, n in SMALL_LAYERS}
    mem_g = P["mem_norm"].reshape(1, D)
    mem_n = rms_fwd(mem, mem_g, "rms_mem_fwd")
    saved = []
    for layer in range(4):
        i = layer // 2
        W = weights_of(layer, x)
        x, sv_m = (_fwd_even if layer % 2 == 0 else _fwd_odd)(i, x, P, W)
        x, sv_x = _fwd_x(layer, x, mem_n, P, W)
        saved.append((sv_m, sv_x, W))
    dx, loss, dgf = final_loss(x, tgt, P["final_norm"].reshape(1, D))
    G["final_norm"] = dgf[0]
    d_memn = None
    for layer in reversed(range(4)):
        i = layer // 2
        sv_m, sv_x, W = saved[layer]
        GW = {}
        dx_mid, d_memn = _bwd_x(layer, dx, sv_x, mem_n, d_memn, P, W, G, GW)
        d_hn, g, key, name = (_bwd_even if layer % 2 == 0 else _bwd_odd)(i, dx_mid, sv_m, P, W, G, GW)
        token = grads_done(layer, GW)
        if token is not None:
            g = g + token
        dx, dg = rms_bwd(sv_m["x"], d_hn, dx_mid, g, name)
        G[key][i] = dg[0]
    _, dgm = rms_bwd(mem, d_memn, d_memn, mem_g, "rms_mem_bwd")
    G["mem_norm"] = dgm[0]
    return loss, dx, G


ANY = pl.BlockSpec(memory_space=pl.ANY)


def _place():
    x, y, c = lax.axis_index("x"), lax.axis_index("y"), lax.axis_index("c")
    chips = [(1 - x, y), (x, 1 - y), (1 - x, 1 - y)]
    return x, y, c, 2 * x + y, (x, y, 1 - c), chips


def _remote(src, dst, send, recv, k, dev):
    return pltpu.make_async_remote_copy(src_ref=src, dst_ref=dst, send_sem=send.at[k], recv_sem=recv.at[k],
                                        device_id=dev, device_id_type=MESHID)


HBM = pl.BlockSpec(memory_space=pltpu.HBM)
SEM = pl.BlockSpec(memory_space=pltpu.SEMAPHORE)
EFFECT = pltpu.SideEffectType.DATAFLOW_SIDE_EFFECTING


def _hbm(t):
    return pltpu.with_memory_space_constraint(t, pltpu.HBM)


def allgather_sync(shards):
    n = len(shards)

    def body(*refs):
        ins, outs = refs[:n], refs[n:2 * n]
        token, send, recv = refs[2 * n:]
        x, y, c, jme, sib, chips = _place()
        first, passed = [], []
        for a in range(n):
            cp = _remote(ins[a], outs[a].at[jme], send, recv, a * 7 + 6, sib)
            cp.start()
            first.append(cp)
            for k, chip in enumerate(chips):
                cp = _remote(ins[a].at[c], outs[a].at[jme, c], send, recv, a * 7 + k, (*chip, c))
                cp.start()
                first.append(cp)
        for a in range(n):
            for k, chip in enumerate(chips):
                piece = outs[a].at[2 * chip[0] + chip[1], c]
                _remote(piece, piece, send, recv, a * 7 + k, (*chip, c)).wait_recv()
                fw = _remote(piece, piece, send, recv, a * 7 + 3 + k, sib)
                fw.start()
                passed.append(fw)
        for a in range(n):
            own = outs[a].at[jme]
            _remote(own, own, send, recv, a * 7 + 6, sib).wait_recv()
            for k, chip in enumerate(chips):
                piece = outs[a].at[2 * chip[0] + chip[1], 1 - c]
                _remote(piece, piece, send, recv, a * 7 + 3 + k, sib).wait_recv()
        for cp in first + passed:
            cp.wait_send()
        token[...] = jnp.zeros_like(token)

    res = pl.pallas_call(
        body, name="allgather_sync", in_specs=[ANY] * n,
        out_specs=[ANY] * n + [pl.BlockSpec(memory_space=pltpu.VMEM)],
        out_shape=[jax.ShapeDtypeStruct((4,) + s.shape, s.dtype) for s in shards] + [jax.ShapeDtypeStruct((8, 128), F32)],
        scratch_shapes=[pltpu.SemaphoreType.DMA((7 * n,)), pltpu.SemaphoreType.DMA((7 * n,))],
    )(*shards)
    return list(res[:n]), res[n]


def _gather_copies(ins, lands, send, recv):
    x, y, c, jme, sib, chips = _place()
    devs = [(*chip, c) for chip in chips] + [sib]
    return [_remote(ins[a], lands[a].at[jme], send, recv, a * 4 + k, dev)
            for a in range(len(ins)) for k, dev in enumerate(devs)]


def allgather_start(shards, after, name):
    n, na = len(shards), len(after)

    def body(*refs):
        ins, lands = refs[:n], refs[n:2 * n]
        send, recv = refs[2 * n + na], refs[2 * n + na + 1]
        token = refs[-1]
        for cp in _gather_copies(ins, lands, send, recv):
            cp.start()
        token[...] = jnp.zeros_like(token)

    res = pl.pallas_call(
        body, name=name,
        out_shape=(pltpu.SemaphoreType.DMA((4 * n,)), pltpu.SemaphoreType.DMA((4 * n,)),
                   *[pltpu.HBM(s.shape, s.dtype) for s in shards],
                   *[pltpu.HBM((4,) + s.shape, s.dtype) for s in shards],
                   jax.ShapeDtypeStruct((8, 128), F32)),
        in_specs=[HBM] * (2 * n) + [ANY] * na,
        out_specs=(SEM, SEM, *[HBM] * (2 * n), pl.BlockSpec(memory_space=pltpu.VMEM)),
        input_output_aliases={a: 2 + a for a in range(2 * n)},
        compiler_params=pltpu.CompilerParams(has_side_effects=EFFECT),
    )(*[_hbm(s) for s in shards], *[_hbm(lax.empty((4,) + s.shape, s.dtype)) for s in shards], *after)
    return res[0], res[1], list(res[2:2 + n]), list(res[2 + n:2 + 2 * n]), res[-1]


def allgather_wait(send, recv, shards, lands, after, name):
    n = len(shards)

    def body(*refs):
        ins, zones = refs[:n], refs[n:2 * n]
        send_r, recv_r = refs[2 * n], refs[2 * n + 1]
        x, y, c, jme, sib, chips = _place()
        slots = [2 * chip[0] + chip[1] for chip in chips] + [jme]
        for a in range(n):
            for k, slot in enumerate(slots):
                cp = _remote(ins[a], zones[a].at[slot], send_r, recv_r, a * 4 + k, sib)
                cp.wait_send()
                cp.wait_recv()

    res = pl.pallas_call(
        body, name=name,
        out_shape=tuple(pltpu.HBM(t.shape, t.dtype) for t in list(shards) + list(lands)),
        in_specs=[HBM] * (2 * n) + [SEM, SEM, ANY], out_specs=tuple([HBM] * (2 * n)),
        input_output_aliases={a: a for a in range(2 * n)},
        compiler_params=pltpu.CompilerParams(has_side_effects=EFFECT),
    )(*shards, *lands, send, recv, after)
    return list(res[n:])


def allgather_small(slab):
    def body(in_ref, out_ref, send, recv, lsem):
        x, y, c, jme, sib, chips = _place()
        loc = pltpu.make_async_copy(in_ref, out_ref.at[jme], lsem.at[0])
        loc.start()
        cps = [_remote(in_ref, out_ref.at[jme], send, recv, k, (*chip, c)) for k, chip in enumerate(chips)]
        for cp in cps:
            cp.start()
        for k, chip in enumerate(chips):
            piece = out_ref.at[2 * chip[0] + chip[1]]
            _remote(piece, piece, send, recv, k, (*chip, c)).wait_recv()
        for cp in cps:
            cp.wait_send()
        loc.wait()

    return pl.pallas_call(
        body, name="allgather_small", in_specs=[ANY], out_specs=ANY,
        out_shape=jax.ShapeDtypeStruct((4,) + slab.shape, slab.dtype),
        scratch_shapes=[pltpu.SemaphoreType.DMA((3,)), pltpu.SemaphoreType.DMA((3,)), pltpu.SemaphoreType.DMA((1,))],
    )(slab)


def allreduce_small(v):
    def body(v_ref, o_ref, r0, r1, r2, send, recv):
        x, y, c, jme, sib, chips = _place()
        peers = [sib, (1 - x, y, c), (x, 1 - y, c)]
        o_ref[...] = v_ref[...]
        for k, buf in enumerate((r0, r1, r2)):
            cp = _remote(o_ref, buf, send, recv, k, peers[k])
            cp.start()
            cp.wait()
            o_ref[...] = o_ref[...] + buf[...]

    vm = pl.BlockSpec(memory_space=pltpu.VMEM)
    return pl.pallas_call(
        body, name="allreduce_small", in_specs=[vm], out_specs=vm,
        out_shape=jax.ShapeDtypeStruct(v.shape, v.dtype),
        scratch_shapes=[pltpu.VMEM(v.shape, v.dtype)] * 3 + [pltpu.SemaphoreType.DMA((3,)), pltpu.SemaphoreType.DMA((3,))],
        compiler_params=pltpu.CompilerParams(vmem_limit_bytes=VMEM_LIMIT),
    )(v)


def _pair_copies(gs, lands, send, recv):
    x, y, c, jme, sib, chips = _place()
    return [_remote(gs[a].at[:, 1 - c], lands[a], send, recv, a, sib) for a in range(len(gs))]


def rs_pair_start(gs, name):
    n = len(gs)

    def body(*refs):
        ins, lands = refs[:n], refs[n:2 * n]
        send, recv = refs[2 * n], refs[2 * n + 1]
        token = refs[-1]
        for cp in _pair_copies(ins, lands, send, recv):
            cp.start()
        token[...] = jnp.zeros_like(token)

    shapes = [(4,) + g.shape[2:] for g in gs]
    res = pl.pallas_call(
        body, name=name,
        out_shape=(pltpu.SemaphoreType.DMA((n,)), pltpu.SemaphoreType.DMA((n,)),
                   *[pltpu.HBM(g.shape, g.dtype) for g in gs], *[pltpu.HBM(s, F32) for s in shapes],
                   jax.ShapeDtypeStruct((8, 128), F32)),
        in_specs=[HBM] * (2 * n), out_specs=(SEM, SEM, *[HBM] * (2 * n), pl.BlockSpec(memory_space=pltpu.VMEM)),
        input_output_aliases={a: 2 + a for a in range(2 * n)},
        compiler_params=pltpu.CompilerParams(has_side_effects=EFFECT),
    )(*[_hbm(g) for g in gs], *[_hbm(lax.empty(s, F32)) for s in shapes])
    return res[0], res[1], list(res[2:2 + n]), list(res[2 + n:2 + 2 * n]), res[-1]


def rs_pair_wait(send, recv, gs, lands, after, name):
    n = len(gs)

    def body(*refs):
        ins, zones = refs[:n], refs[n:2 * n]
        for cp in _pair_copies(ins, zones, refs[2 * n], refs[2 * n + 1]):
            cp.wait_send()
            cp.wait_recv()

    res = pl.pallas_call(
        body, name=name,
        out_shape=tuple(pltpu.HBM(t.shape, t.dtype) for t in list(gs) + list(lands)),
        in_specs=[HBM] * (2 * n) + [SEM, SEM, ANY], out_specs=tuple([HBM] * (2 * n)),
        input_output_aliases={a: a for a in range(2 * n)},
        compiler_params=pltpu.CompilerParams(has_side_effects=EFFECT),
    )(*gs, *lands, send, recv, after)
    return list(res[:n]), list(res[n:])


def rs_pair_sum(g4, got, cidx):
    _, _, rh, cols = g4.shape
    tr = rh if rh <= 256 else 256

    def body(c_ref, a_ref, b_ref, o_ref):
        o_ref[...] = (a_ref[...] + b_ref[...]).astype(o_ref.dtype)

    return pl.pallas_call(
        body, name="rs_pair_sum",
        grid_spec=pltpu.PrefetchScalarGridSpec(
            num_scalar_prefetch=1, grid=(4, rh // tr),
            in_specs=[pl.BlockSpec((None, None, tr, cols), lambda j, t, cr: (j, cr[0], t, 0)),
                      pl.BlockSpec((None, tr, cols), lambda j, t, cr: (j, t, 0))],
            out_specs=pl.BlockSpec((None, tr, cols), lambda j, t, cr: (j, t, 0))),
        out_shape=jax.ShapeDtypeStruct((4, rh, cols), BF16),
        compiler_params=_cparams(("parallel", "parallel")),
    )(cidx, g4, got)


def _chip_copies(ps, lands, send, recv):
    x, y, c, jme, sib, chips = _place()
    return [_remote(ps[a].at[2 * chip[0] + chip[1]], lands[a].at[jme], send, recv, a * 3 + k, (*chip, c))
            for a in range(len(ps)) for k, chip in enumerate(chips)]


def rs_chip_start(ps, name):
    n = len(ps)

    def body(*refs):
        ins, lands = refs[:n], refs[n:2 * n]
        send, recv = refs[2 * n], refs[2 * n + 1]
        token = refs[-1]
        for cp in _chip_copies(ins, lands, send, recv):
            cp.start()
        token[...] = jnp.zeros_like(token)

    res = pl.pallas_call(
        body, name=name,
        out_shape=(pltpu.SemaphoreType.DMA((3 * n,)), pltpu.SemaphoreType.DMA((3 * n,)),
                   *[pltpu.HBM(p.shape, p.dtype) for p in ps], *[pltpu.HBM(p.shape, p.dtype) for p in ps],
                   jax.ShapeDtypeStruct((8, 128), F32)),
        in_specs=[HBM] * (2 * n), out_specs=(SEM, SEM, *[HBM] * (2 * n), pl.BlockSpec(memory_space=pltpu.VMEM)),
        input_output_aliases={a: 2 + a for a in range(2 * n)},
        compiler_params=pltpu.CompilerParams(has_side_effects=EFFECT),
    )(*[_hbm(p) for p in ps], *[_hbm(lax.empty(p.shape, p.dtype)) for p in ps])
    return res[0], res[1], list(res[2:2 + n]), list(res[2 + n:2 + 2 * n]), res[-1]


def rs_chip_wait(send, recv, ps, lands, after, name):
    n = len(ps)

    def body(*refs):
        ins, zones = refs[:n], refs[n:2 * n]
        send_r, recv_r = refs[2 * n], refs[2 * n + 1]
        x, y, c, jme, sib, chips = _place()
        for a in range(n):
            for k, chip in enumerate(chips):
                jt = 2 * chip[0] + chip[1]
                cp = _remote(ins[a].at[jt], zones[a].at[jt], send_r, recv_r, a * 3 + k, (*chip, c))
                cp.wait_send()
                cp.wait_recv()

    res = pl.pallas_call(
        body, name=name,
        out_shape=tuple(pltpu.HBM(p.shape, p.dtype) for p in list(ps) + list(lands)),
        in_specs=[HBM] * (2 * n) + [SEM, SEM, ANY], out_specs=tuple([HBM] * (2 * n)),
        input_output_aliases={a: a for a in range(2 * n)},
        compiler_params=pltpu.CompilerParams(has_side_effects=EFFECT),
    )(*ps, *lands, send, recv, after)
    return list(res[n:])


def rs_chip_sum(q, p, l, acc, layers, jc):
    _, rh, cols = q.shape
    tr = rh if rh <= 256 else 256

    def body(jc_ref, q_ref, p_ref, *rest):
        o_ref = rest[-1]
        jme = jc_ref[0]
        own = p_ref[...].astype(F32)
        v = [jnp.where(jme == j, own, q_ref[j].astype(F32)) for j in range(4)]
        o_ref[...] = ((v[0] + v[1]) + v[2]) + v[3]

    in_specs = [pl.BlockSpec((4, tr, cols), lambda t, jr: (0, t, 0)),
                pl.BlockSpec((None, tr, cols), lambda t, jr: (jr[0], t, 0))]
    args = [jc, q, p]
    if acc is not None:
        in_specs.append(ANY)
        args.append(acc)
    return pl.pallas_call(
        body, name="rs_chip_sum",
        grid_spec=pltpu.PrefetchScalarGridSpec(
            num_scalar_prefetch=1, grid=(rh // tr,), in_specs=in_specs,
            out_specs=pl.BlockSpec((None, None, tr, cols), lambda t, jr: (l, jr[1], t, 0))),
        out_shape=jax.ShapeDtypeStruct((layers, 2, rh, cols), F32),
        input_output_aliases={} if acc is None else {3: 0},
        compiler_params=_cparams(("parallel",)),
    )(*args)


def rs_pair_gather(rs):
    n = len(rs)

    def body(*refs):
        outs = refs[n:2 * n]
        send, recv = refs[2 * n:]
        x, y, c, jme, sib, chips = _place()
        cps = [_remote(outs[a].at[:, c], outs[a].at[:, c], send, recv, a, sib) for a in range(n)]
        for cp in cps:
            cp.start()
        for a in range(n):
            slot = outs[a].at[:, 1 - c]
            _remote(slot, slot, send, recv, a, sib).wait_recv()
        for cp in cps:
            cp.wait_send()

    return pl.pallas_call(
        body, name="rs_pair_gather", in_specs=[ANY] * n, out_specs=[ANY] * n,
        out_shape=[jax.ShapeDtypeStruct(r.shape, r.dtype) for r in rs],
        input_output_aliases={a: a for a in range(n)},
        scratch_shapes=[pltpu.SemaphoreType.DMA((n,)), pltpu.SemaphoreType.DMA((n,))],
    )(*rs)


def _adamw_math(w, g, m, v):
    m = B1 * m + (1.0 - B1) * g
    v = B2 * v + (1.0 - B2) * (g * g)
    m_hat = m / (1.0 - B1 ** STEP)
    v_hat = v / (1.0 - B2 ** STEP)
    return -LR * (m_hat / (jnp.sqrt(v_hat) + AEPS) + WD * w), m, v


def adamw(w, g, m, v, name):
    rows, cols = w.shape
    tr = 256 if rows % 256 == 0 else rows
    return rw(_adamw_math, [(a, 0, cols) for a in (w, g, m, v)], [(cols, F32)] * 3, name, rows, tr=tr)


WEIGHTS = ["norm_ab", "w_in_ab", "pool_w", "pool_scale", "w_out_ab", "norm_cd", "w_in_cd", "sgu_ln_g", "sgu_ln_b",
           "sgu_w", "sgu_b", "s5_a_re", "s5_a_im", "s5_log_dt", "s5_b_re", "s5_b_im", "s5_c_re", "s5_c_im", "s5_d",
           "glu_w1", "glu_w2", "w_out_cd", "norm_x", "w_xq", "w_xkv", "w_xo", "mem_norm", "final_norm"]
INPUTS = ["x", "mem"] + WEIGHTS + ["loss_target"] + ["m_" + n for n in WEIGHTS] + ["v_" + n for n in WEIGHTS]
BIG = ["w_in_ab", "w_out_ab", "w_in_cd", "w_out_cd", "w_xq", "w_xkv", "w_xo", "glu_w1", "glu_w2", "pool_w"]
COL_SHARDED = ("w_in_ab", "w_in_cd", "w_xkv")
SMALL = [n for n in WEIGHTS if n not in BIG]
SMALL_SHARDED = {"norm_cd": 256, "sgu_ln_g": 256, "sgu_ln_b": 256, "s5_d": 128}
PACK = 256 * 128


def _pack(arrs):
    flat = jnp.concatenate([a.reshape(-1) for a in arrs])
    pad = (-flat.shape[0]) % PACK
    return jnp.concatenate([flat, jnp.zeros((pad,), flat.dtype)]).reshape(-1, 128)


def _unpack(packed, shapes):
    flat, out, off = packed.reshape(-1), [], 0
    for s in shapes:
        n = 1
        for d in s:
            n *= d
        out.append(flat[off:off + n].reshape(s))
        off += n
    return out


LAYER_KEYS = (("w_in", "w_out", "pool_w", "w_xq", "w_xkv", "w_xo"),
              ("w_in", "w_out", "glu_w1", "glu_w2", "w_xq", "w_xkv", "w_xo"))


def _weight_of(key, layer):
    if key in ("w_xq", "w_xkv", "w_xo"):
        return key, layer, 4
    kind = "ab" if layer % 2 == 0 else "cd"
    return {"w_in": "w_in_" + kind, "w_out": "w_out_" + kind}.get(key, key), layer // 2, 2


def kernel(*args):
    a = dict(zip(INPUTS, args))
    x_i, y_i, c_i = lax.axis_index("x"), lax.axis_index("y"), lax.axis_index("c")
    j = 2 * x_i + y_i

    slab = jnp.concatenate([a["norm_cd"], a["sgu_ln_g"], a["sgu_ln_b"],
                            jnp.pad(a["s5_d"], ((0, 0), (0, 128)))], axis=0)
    gslab = allgather_small(slab)
    P = {n: a[n] for n in SMALL}
    for k, n in enumerate(("norm_cd", "sgu_ln_g", "sgu_ln_b", "s5_d")):
        wd = SMALL_SHARDED[n]
        P[n] = gslab[:, 2 * k:2 * k + 2, :wd].transpose(1, 0, 2).reshape(2, 4 * wd)

    def shards_of(layer):
        keys = sorted(k for k in LAYER_KEYS[layer % 2])
        out = []
        for k in keys:
            n, l, _ = _weight_of(k, layer)
            out.append(a[n][l].reshape(-1, a[n].shape[-1]).astype(BF16))
        return keys, out

    keys0, sh0 = shards_of(0)
    first = keys0.index("w_in")
    g_in, token = allgather_sync([sh0[first].reshape(2, sh0[first].shape[0] // 2, sh0[first].shape[1])])
    w_in0 = g_in[0].reshape(4, -1, g_in[0].shape[-1])
    started = {}
    for layer in (0, 1, 2, 3):
        keys, sh = (keys0, sh0) if layer == 0 else shards_of(layer)
        rest = [(k, s) for k, s in zip(keys, sh) if k != "w_in"]
        parts = [("in", ["w_in"], [sh[keys.index("w_in")]])] * (layer > 0) + [("", *map(list, zip(*rest)))]
        for tag, pk, ps in parts:
            send, recv, ps, lands, token = allgather_start(ps, [token, gslab], "allgather_start_%d%s" % (layer, tag))
            started[(layer, tag)] = (pk, send, recv, ps, lands)
    P["norm_ab"] = P["norm_ab"] + token[0:1, 0:1]

    cidx = jnp.reshape(c_i, (1,)).astype(jnp.int32)
    jc = jnp.stack([j, c_i]).astype(jnp.int32)

    def views(g):
        W = {}
        for k, v in g.items():
            if k in ("w_in", "w_xkv"):
                W[k] = mcs(v)
            elif k == "pool_w":
                W[k] = v.reshape(4, 4, 64, 256).transpose(1, 0, 2, 3).reshape(4, 256, 256)
            elif k not in ("glu_w1", "glu_w2"):
                W[k] = m2(v.reshape(-1, v.shape[-1]))
        if "glu_w1" in g:
            W["w12"] = jnp.concatenate([g["glu_w1"].reshape(512, 512), g["glu_w2"].reshape(512, 512)], axis=1)
        return W

    def arrived(layer, tag, after):
        keys, send, recv, sh, lands = started[(layer, tag)]
        return views(dict(zip(keys, allgather_wait(send, recv, sh, lands, after, "allgather_wait_%d%s" % (layer, tag)))))

    def weights_of(layer, x_in):
        W = views({"w_in": w_in0}) if layer == 0 else arrived(layer, "in", x_in)
        W["more"] = lambda after: arrived(layer, "", after)
        return W

    halves, pending = {}, {}

    def finish_pair(layer, after):
        keys, send, recv, flat, lands = halves.pop(layer)
        flat, got = rs_pair_wait(send, recv, flat, lands, after, "rs_pair_wait_%d" % layer)
        pair = [rs_pair_sum(g4, r, cidx) for g4, r in zip(flat, got)]
        send, recv, pair, lands, token = rs_chip_start(pair, "rs_chip_start_%d" % layer)
        pending[layer] = (keys, send, recv, pair, lands)
        return token

    def grads_done(layer, GW):
        keys = sorted(GW)
        flat = [GW[k].reshape(4, 2, GW[k].shape[1] // 2, GW[k].shape[2]) for k in keys]
        send, recv, flat, lands, token = rs_pair_start(flat, "rs_pair_start_%d" % layer)
        halves[layer] = (keys, send, recv, flat, lands)
        if layer + 1 in halves:
            token = token + finish_pair(layer + 1, token)
        return token[0:1, 0:1]

    loss, dx, G = local_step(a["x"][0], a["mem"][0], a["loss_target"][0], P, weights_of, grads_done)
    loss = lax.psum(loss[0, 0], ("x", "y", "c"))
    finish_pair(0, dx)
    outs = {}

    def update_big(names, red):
        for n, g in zip(names, rs_pair_gather([red[n] for n in names])):
            shp = a[n].shape
            g2 = g.reshape(-1, shp[-1])
            upd = adamw(a[n].reshape(g2.shape), g2, a["m_" + n].reshape(g2.shape), a["v_" + n].reshape(g2.shape),
                        "adamw_" + n)
            outs[n] = tuple(t.reshape(shp) for t in (g2,) + tuple(upd))

    def reduce_layer(layer, red, after):
        keys, send, recv, pair, lands = pending[layer]
        lands = rs_chip_wait(send, recv, pair, lands, after, "rs_chip_wait_%d" % layer)
        for k, q, p in zip(keys, lands, pair):
            n, l, layers = _weight_of(k, layer)
            red[n] = rs_chip_sum(q, p, l, red.get(n), layers, jc)

    red = {}
    for layer in (3, 2, 1):
        reduce_layer(layer, red, dx)
    odd_only = [n for n in BIG if n.endswith("_cd") or n.startswith("glu")]
    update_big(odd_only, red)

    gfull = [jnp.stack(G[n]) if isinstance(G[n], list) else G[n] for n in SMALL]
    shapes = [g.shape for g in gfull]
    gsum = _unpack(allreduce_small(_pack(gfull)), shapes)
    gloc = []
    for n, g in zip(SMALL, gsum):
        if n in SMALL_SHARDED:
            g = lax.dynamic_slice_in_dim(g, j * SMALL_SHARDED[n], SMALL_SHARDED[n], axis=1)
        gloc.append(g)
    for n, g in zip(SMALL, gloc):
        shp = a[n].shape
        two = (-1, shp[-1]) if len(shp) > 1 else (1, shp[0])
        upd = adamw(a[n].reshape(two), g.reshape(two), a["m_" + n].reshape(two), a["v_" + n].reshape(two), "adamw_" + n)
        outs[n] = (g,) + tuple(t.reshape(shp) for t in upd)

    reduce_layer(0, red, outs[SMALL[-1]][1])
    update_big([n for n in BIG if n not in odd_only], red)

    res = [loss, dx[None]]
    for part in range(4):
        res += [outs[n][part] for n in WEIGHTS]
    return tuple(res)
```

```python
import math

import jax
import jax.numpy as jnp
from jax import lax
from jax.experimental import pallas as pl
from jax.experimental.pallas import tpu as pltpu

F32, BF16 = jnp.float32, jnp.bfloat16
S, D = 2048, 1024
MEM = 256
EPS = 1e-6
NEG = -1e30
QB = 128
PATTERNS = (1, 4, 16)
NG, NP, NH = 32, 64, 16
NS = NG * NP
LR, B1, B2, AEPS, WD, STEP = 0.001, 0.9, 0.999, 1e-08, 0.01, 10
MESHID = pl.DeviceIdType.MESH
VMEM_LIMIT = 56 * 1024 * 1024


def _cparams(sem):
    return pltpu.CompilerParams(dimension_semantics=sem, vmem_limit_bytes=VMEM_LIMIT)


def _sig(x):
    return 1.0 / (1.0 + jnp.exp(-x))


def _dot(a, b, dims):
    return lax.dot_general(a, b, (dims, ((), ())), preferred_element_type=F32)


def _nn(a, b):
    return _dot(a, b, ((1,), (0,)))


def _nt(a, b):
    return _dot(a, b, ((1,), (1,)))


def _tn(a, b):
    return _dot(a, b, ((0,), (0,)))


_DIMS = {"nn": ((1,), (0,)), "nt": ((1,), (1,)), "tn": ((0,), (0,))}


def _tile(dim, cc=None, cap=1024):
    for t in (2048, 1536, 1024, 768, 512, 384, 256, 128):
        if t <= cap and dim % t == 0 and (cc is None or cc % t == 0):
            return t
    return dim


MM_VMEM = 36 * 1024 * 1024


def _mm_tiles(m, n, k, ccm, ccn, cck, a_bytes, b_bytes, o_bytes):
    caps = [1024, 1024, 2048]
    while True:
        tm, tn, tk = _tile(m, ccm, caps[0]), _tile(n, ccn, caps[1]), _tile(k, cck, caps[2])
        need = 2 * (tm * tk * a_bytes + tk * tn * b_bytes + tm * tn * o_bytes) + (tm * tn * 4 if tk < k else 0)
        if need <= MM_VMEM:
            return tm, tn, tk
        if tk > 1024:
            caps[2] = tk // 2
        elif tn >= tm:
            caps[1] = tn // 2
        else:
            caps[0] = tm // 2


def m2(arr, col_off=0, ncols=None):
    rows, cols = arr.shape
    ncols = cols - col_off if ncols is None else ncols

    def spec(tr, tc, rc):
        assert col_off % tc == 0
        return pl.BlockSpec((tr, tc), lambda *g: (rc(*g)[0], rc(*g)[1] + col_off // tc))
    return (arr, rows, ncols, spec, None if col_off == 0 else col_off)


def mcs(arr):
    cs = arr.shape[2]

    def spec(tr, tc, rc):
        n = cs // tc
        return pl.BlockSpec((None, tr, tc), lambda *g: (rc(*g)[1] // n, rc(*g)[0], rc(*g)[1] % n))
    return (arr, arr.shape[1], 4 * cs, spec, cs)


def out2(rows, cols):
    def spec(tr, tc, rc):
        return pl.BlockSpec((tr, tc), lambda *g: tuple(rc(*g)))
    return ((rows, cols), spec, None)


def outcs(rows, cs):
    def spec(tr, tc, rc):
        n = cs // tc
        return pl.BlockSpec((None, tr, tc), lambda *g: (rc(*g)[1] // n, rc(*g)[0], rc(*g)[1] % n))
    return ((4, rows, cs), spec, cs)


def _both(a, b):
    if a is None:
        return b
    if b is None:
        return a
    return math.gcd(a, b)


def mm(a, b, mode, name, add=None, out=None, out_dtype=F32):
    a_arr, a_r, a_c, a_spec, a_cc = a
    b_arr, b_r, b_c, b_spec, b_cc = b
    if mode == "nn":
        m, k, n = a_r, a_c, b_c
        assert b_r == k
        ccm, cck, ccn = None, a_cc, b_cc
    elif mode == "nt":
        m, k, n = a_r, a_c, b_r
        assert b_c == k
        ccm, cck, ccn = None, _both(a_cc, b_cc), None
    else:
        m, k, n = a_c, a_r, b_c
        assert b_r == k
        ccm, cck, ccn = a_cc, None, b_cc
    out = out2(m, n) if out is None else out
    o_shape, o_spec, o_cc = out
    ccn = _both(ccn, o_cc)
    if add is not None:
        ccn = _both(ccn, add[4])
    o_bytes = jnp.dtype(out_dtype).itemsize + (0 if add is None else add[0].dtype.itemsize)
    tm, tn, tk = _mm_tiles(m, n, k, ccm, ccn, cck, a_arr.dtype.itemsize, b_arr.dtype.itemsize, o_bytes)
    nk = k // tk
    if mode == "nn":
        in_specs = [a_spec(tm, tk, lambda i, j, kk: (i, kk)), b_spec(tk, tn, lambda i, j, kk: (kk, j))]
    elif mode == "nt":
        in_specs = [a_spec(tm, tk, lambda i, j, kk: (i, kk)), b_spec(tn, tk, lambda i, j, kk: (j, kk))]
    else:
        in_specs = [a_spec(tk, tm, lambda i, j, kk: (kk, i)), b_spec(tk, tn, lambda i, j, kk: (kk, j))]
    args = [a_arr, b_arr]
    if add is not None:
        in_specs.append(add[3](tm, tn, lambda i, j, kk: (i, j)))
        args.append(add[0])
    dims = _DIMS[mode]
    has_add = add is not None

    def body(*refs):
        a_ref, b_ref = refs[0], refs[1]
        add_ref = refs[2] if has_add else None
        prod = _dot(a_ref[...].astype(BF16), b_ref[...].astype(BF16), dims)
        if nk == 1:
            o_ref = refs[-1]
            if has_add:
                prod = prod + add_ref[...].astype(F32)
            o_ref[...] = prod.astype(o_ref.dtype)
            return
        o_ref, acc = refs[-2], refs[-1]
        kk = pl.program_id(2)

        @pl.when(kk == 0)
        def _():
            acc[...] = prod

        @pl.when(kk > 0)
        def _():
            acc[...] += prod

        @pl.when(kk == nk - 1)
        def _():
            r = acc[...]
            if has_add:
                r = r + add_ref[...].astype(F32)
            o_ref[...] = r.astype(o_ref.dtype)

    return pl.pallas_call(
        body, name=name, grid=(m // tm, n // tn, nk), in_specs=in_specs,
        out_specs=o_spec(tm, tn, lambda i, j, kk: (i, j)),
        out_shape=jax.ShapeDtypeStruct(o_shape, out_dtype),
        scratch_shapes=[pltpu.VMEM((tm, tn), F32)] if nk > 1 else [],
        compiler_params=_cparams(("parallel", "parallel", "arbitrary")),
    )(*args)


def rw(fn, ins, outs, name, rows, tr=256, consts=(), accs=()):
    n_in, n_c, n_o, n_a = len(ins), len(consts), len(outs), len(accs)
    in_specs = []
    for arr, off, width in ins:
        assert off % width == 0
        in_specs.append(pl.BlockSpec((tr, width), lambda i, o=off // width: (i, o)))
    for c in consts:
        in_specs.append(pl.BlockSpec(c.shape, lambda i: (0, 0)))
    out_specs = [pl.BlockSpec((tr, w), lambda i: (i, 0)) for w, _ in outs]
    out_specs += [pl.BlockSpec(s, lambda i: (0, 0)) for s in accs]
    out_shape = [jax.ShapeDtypeStruct((rows, w), dt) for w, dt in outs]
    out_shape += [jax.ShapeDtypeStruct(s, F32) for s in accs]

    def body(*refs):
        vals = [r[...] for r in refs[:n_in + n_c]]
        o_refs = refs[n_in + n_c:n_in + n_c + n_o]
        a_refs = refs[n_in + n_c + n_o:]
        res = fn(*vals)
        for r, v in zip(o_refs, res[:n_o]):
            r[...] = v.astype(r.dtype)
        if n_a:
            @pl.when(pl.program_id(0) == 0)
            def _():
                for r in a_refs:
                    r[...] = jnp.zeros_like(r)
            for r, v in zip(a_refs, res[n_o:]):
                r[...] += v

    res = pl.pallas_call(
        body, name=name, grid=(rows // tr,), in_specs=in_specs, out_specs=out_specs,
        out_shape=out_shape,
        compiler_params=_cparams(("arbitrary",) if n_a else ("parallel",)),
    )(*[a for a, _, _ in ins], *consts)
    return res


def _rstd(x):
    return lax.rsqrt(jnp.mean(x * x, axis=-1, keepdims=True) + EPS)


def rms_fwd(x, g, name):
    def fn(xv, gv):
        xv = xv.astype(F32)
        return (xv * _rstd(xv) * gv,)
    return rw(fn, [(x, 0, D)], [(D, BF16)], name, x.shape[0], consts=[g])[0]


def _rms_bwd_math(xv, dy, gv):
    r = _rstd(xv)
    dyg = dy * gv
    dx = r * dyg - xv * (r * r * r / D) * jnp.sum(dyg * xv, axis=-1, keepdims=True)
    dg = jnp.sum(dy * xv * r, axis=0, keepdims=True)
    return dx, dg


def rms_bwd(x, dy, dres, g, name):
    def fn(xv, dyv, drv, gv):
        dx, dg = _rms_bwd_math(xv, dyv, gv)
        return dx + drv, dg
    return rw(fn, [(x, 0, D), (dy, 0, D), (dres, 0, D)], [(D, F32)], name, x.shape[0],
              consts=[g], accs=[(1, D)])


def final_loss(x, tgt, g):
    def fn(xv, tv, gv):
        e = xv * _rstd(xv) * gv - tv
        loss = 0.5 * jnp.sum(jnp.sum(e * e, axis=-1, keepdims=True), axis=0, keepdims=True) / D
        dx, dg = _rms_bwd_math(xv, e / D, gv)
        return dx, loss, dg
    return rw(fn, [(x, 0, D), (tgt, 0, D)], [(D, F32)], "final_loss", S, consts=[g],
              accs=[(1, 1), (1, D)])


def _attn_bias(bias_ref):
    ii = lax.broadcasted_iota(jnp.int32, (2 * QB, 2 * QB), 0) % QB
    jj = lax.broadcasted_iota(jnp.int32, (2 * QB, 2 * QB), 1)
    dist = ii + QB - jj
    band = (dist >= 0) & (dist <= QB)
    bias_ref[1] = jnp.where(band, 0.0, NEG)
    bias_ref[0] = jnp.where(band & (jj >= QB), 0.0, NEG)


def _two_heads(x, m0):
    return jnp.concatenate([jnp.where(m0, x, 0.0), jnp.where(m0, 0.0, x)], axis=0)


def _per_head(col, m0):
    return jnp.where(m0, col[:QB], col[QB:])


def _attn_rows(idx, d):
    if d == 1:
        b = idx
        cur = pl.ds(pl.multiple_of(b * QB, QB), QB)
        prev = pl.ds(pl.multiple_of(jnp.maximum(b - 1, 0) * QB, QB), QB)
    else:
        r, b = lax.rem(idx, d), lax.div(idx, d)
        cur = pl.ds(r + b * (QB * d), QB, stride=d)
        prev = pl.ds(r + jnp.maximum(b - 1, 0) * (QB * d), QB, stride=d)
    return cur, prev, b


NBLK = S // QB
GROUP = 4


def _colblk(off):
    return pl.BlockSpec((S, 128), lambda hp: (0, off * 8 + hp))


def attn_fwd(z):
    def body(q_ref, k_ref, v_ref, g_ref, o_ref, l_ref, a_ref, os, ls, bias):
        _attn_bias(bias)
        m0 = lax.broadcasted_iota(jnp.int32, (1, 128), 1) < 64
        for pi, d in enumerate(PATTERNS):
            def load(idx, d=d):
                cur, prev, b = _attn_rows(idx, d)
                return cur, (q_ref[cur, :], k_ref[prev, :], k_ref[cur, :], v_ref[prev, :], v_ref[cur, :],
                             bias[jnp.minimum(b, 1)])

            def block(q, kp, kc, vp, vc, bs):
                qq = _two_heads(q * 0.125, m0).astype(BF16)
                k = jnp.concatenate([kp, kc], axis=0).astype(BF16)
                s = _nt(qq, k) + bs
                mx = jnp.max(s, axis=-1, keepdims=True)
                p = jnp.exp(s - mx)
                den = jnp.sum(p, axis=-1, keepdims=True)
                pb = p.astype(BF16)
                vv = _two_heads(jnp.concatenate([vp, vc], axis=0), m0).astype(BF16)
                o = _nn(jnp.concatenate([pb[:QB], pb[QB:]], axis=1), vv)
                return o * _per_head(1.0 / den, m0), _per_head(mx + jnp.log(den), m0)

            def step(i, carry, pi=pi):
                loaded = [load(i * GROUP + u) for u in range(GROUP)]
                done = [block(*vals) for _, vals in loaded]
                for (cur, _), (o, l) in zip(loaded, done):
                    os[pi, cur, :] = o
                    ls[pi, cur, :] = l
                return carry
            lax.fori_loop(0, NBLK // GROUP, step, 0)
        l1, l2, l3 = ls[0], ls[1], ls[2]
        mx = jnp.maximum(jnp.maximum(l1, l2), l3)
        e1, e2, e3 = jnp.exp(l1 - mx), jnp.exp(l2 - mx), jnp.exp(l3 - mx)
        tot = e1 + e2 + e3
        o = (os[0] * e1 + os[1] * e2 + os[2] * e3) / tot
        ga = g_ref[...]
        o_ref[...] = o
        l_ref[...] = mx + jnp.log(tot)
        a_ref[...] = (o * (ga * _sig(ga))).astype(a_ref.dtype)

    out = pl.BlockSpec((S, 128), lambda hp: (0, hp))
    return pl.pallas_call(
        body, name="attn_fwd", grid=(8,),
        in_specs=[_colblk(0), _colblk(1), _colblk(2), _colblk(3)], out_specs=[out] * 3,
        out_shape=[jax.ShapeDtypeStruct((S, D), F32), jax.ShapeDtypeStruct((S, D), F32),
                   jax.ShapeDtypeStruct((S, 2 * D), BF16)],
        scratch_shapes=[pltpu.VMEM((3, S, 128), F32), pltpu.VMEM((3, S, 128), F32),
                        pltpu.VMEM((2, 2 * QB, 2 * QB), F32)],
        compiler_params=_cparams(("parallel",)),
    )(z, z, z, z)


def attn_bwd(z, d_cat, o, lse):
    def body(q_ref, k_ref, v_ref, g_ref, da_ref, o_ref, l_ref, dq_ref, dk_ref, dv_ref, dg_ref, do_s, pr_s, bias):
        _attn_bias(bias)
        m0 = lax.broadcasted_iota(jnp.int32, (1, 128), 1) < 64
        ga = g_ref[...]
        sg = _sig(ga)
        da = da_ref[...]
        ov = o_ref[...]
        do = da * (ga * sg)
        dg_ref[...] = da * ov * (sg * (1.0 + ga * (1.0 - sg)))
        do_s[...] = do
        pr_s[...] = do * ov
        dq_ref[...] = jnp.zeros_like(dq_ref)
        dk_ref[...] = jnp.zeros_like(dk_ref)
        dv_ref[...] = jnp.zeros_like(dv_ref)
        for d in PATTERNS:
            def load(idx, d=d):
                cur, prev, b = _attn_rows(idx, d)
                return (cur, prev), (q_ref[cur, :], k_ref[prev, :], k_ref[cur, :], v_ref[prev, :], v_ref[cur, :],
                                     do_s[cur, :], pr_s[cur, :], l_ref[cur, :], bias[jnp.minimum(b, 1)])

            def block(q, kp, kc, vp, vc, dof, prod, lp, bs):
                qq = _two_heads(q * 0.125, m0).astype(BF16)
                kf = jnp.concatenate([kp, kc], axis=0)
                k = kf.astype(BF16)
                v = jnp.concatenate([vp, vc], axis=0).astype(BF16)
                dd = _two_heads(dof, m0).astype(BF16)
                lh = jnp.max(jnp.concatenate([jnp.where(m0, lp, -jnp.inf), jnp.where(m0, -jnp.inf, lp)], axis=0),
                             axis=-1, keepdims=True)
                delta = jnp.sum(_two_heads(prod, m0), axis=-1, keepdims=True)
                p = jnp.exp(_nt(qq, k) + bs - lh)
                ds = (p * (_nt(dd, v) - delta)).astype(BF16)
                dq = _nn(jnp.concatenate([ds[:QB], ds[QB:]], axis=1), _two_heads(kf, m0).astype(BF16))
                return dq * 0.125, _tn(ds, qq), _tn(p.astype(BF16), dd)

            def step(i, carry):
                loaded = [load(i * GROUP + u) for u in range(GROUP)]
                done = [block(*vals) for _, vals in loaded]
                for ((cur, prev), _), (dq, dk, dv) in zip(loaded, done):
                    dq_ref[cur, :] = dq_ref[cur, :] + dq
                    dk_ref[prev, :] = dk_ref[prev, :] + dk[:QB]
                    dv_ref[prev, :] = dv_ref[prev, :] + dv[:QB]
                    dk_ref[cur, :] = dk_ref[cur, :] + dk[QB:]
                    dv_ref[cur, :] = dv_ref[cur, :] + dv[QB:]
                return carry
            lax.fori_loop(0, NBLK // GROUP, step, 0)

    blk = pl.BlockSpec((S, 128), lambda hp: (0, hp))
    return pl.pallas_call(
        body, name="attn_bwd", grid=(8,),
        in_specs=[_colblk(0), _colblk(1), _colblk(2), _colblk(3), blk, blk, blk], out_specs=[blk] * 4,
        out_shape=[jax.ShapeDtypeStruct((S, D), F32)] * 4,
        scratch_shapes=[pltpu.VMEM((S, 128), F32), pltpu.VMEM((S, 128), F32), pltpu.VMEM((2, 2 * QB, 2 * QB), F32)],
        compiler_params=_cparams(("parallel",)),
    )(z, z, z, z, d_cat, o, lse)


def assemble_dz_even(parts):
    def body(*refs):
        o_ref = refs[-1]
        for j in range(6):
            o_ref[:, j * D:(j + 1) * D] = refs[j][...].astype(o_ref.dtype)
    tr = 256
    blk = pl.BlockSpec((tr, D), lambda i: (i, 0))
    return pl.pallas_call(
        body, name="assemble_dz_even", grid=(S // tr,), in_specs=[blk] * 6,
        out_specs=pl.BlockSpec((tr, 6 * D), lambda i: (i, 0)),
        out_shape=jax.ShapeDtypeStruct((S, 6 * D), BF16),
        compiler_params=_cparams(("parallel",)),
    )(*parts)


def _pool_window(g):
    return jnp.where(g == 0, 2.0, jnp.where(g == 1, 4.0, jnp.where(g == 2, 8.0, 16.0)))


def _pool_sel(g, levels):
    return jnp.where(g == 0, levels[0], jnp.where(g == 1, levels[1], jnp.where(g == 2, levels[2], levels[3])))


def _pool_fwd_math(v, g):
    t = lax.broadcasted_iota(jnp.int32, (S, 1), 0)
    s = v
    levels = []
    for k in (1, 2, 4, 8):
        s = s + jnp.where(t >= k, pltpu.roll(s, k, 0), 0.0)
        levels.append(s)
    cnt = jnp.minimum((t + 1).astype(F32), _pool_window(g))
    return _pool_sel(g, levels) / cnt - v, cnt


def pool_fwd(z, pw, ps, cat):
    def body(v_ref, g_ref, pw_ref, ps_ref, cat_ref, o_ref):
        g = pl.program_id(0)
        pooled, _ = _pool_fwd_math(v_ref[...], g)
        mixed = _nn(pooled.astype(BF16), pw_ref[...].astype(BF16))
        gb = g_ref[...]
        o_ref[...] = (mixed * ps_ref[...] * (gb * _sig(gb))).astype(o_ref.dtype)

    return pl.pallas_call(
        body, name="pool_fwd", grid=(4,),
        in_specs=[pl.BlockSpec((S, 256), lambda g: (0, 16 + g)),
                  pl.BlockSpec((S, 256), lambda g: (0, 20 + g)),
                  pl.BlockSpec((None, 256, 256), lambda g: (g, 0, 0)),
                  pl.BlockSpec((1, 256), lambda g: (0, g)), pl.BlockSpec(memory_space=pl.ANY)],
        out_specs=pl.BlockSpec((S, 256), lambda g: (0, 4 + g)),
        out_shape=jax.ShapeDtypeStruct((S, 2 * D), BF16),
        input_output_aliases={4: 0},
        compiler_params=_cparams(("parallel",)),
    )(z, z, pw, ps, cat)


def pool_bwd(z, d_cat, pw, ps):
    def body(v_ref, g_ref, d_ref, pw_ref, ps_ref, dv_ref, dg_ref, dpw_ref, dps_ref):
        g = pl.program_id(0)
        v = v_ref[...]
        pooled, cnt = _pool_fwd_math(v, g)
        pwb = pw_ref[...].astype(BF16)
        pb = pooled.astype(BF16)
        mixed = _nn(pb, pwb)
        gb = g_ref[...]
        sg = _sig(gb)
        dout = d_ref[...]
        sc = ps_ref[...]
        dg_ref[...] = dout * mixed * sc * (sg * (1.0 + gb * (1.0 - sg)))
        dms = dout * (gb * sg)
        dps_ref[...] = jnp.sum(dms * mixed, axis=0, keepdims=True)
        dmx = (dms * sc).astype(BF16)
        dpw_ref[...] = _tn(pb, dmx)
        dpooled = _nt(dmx, pwb)
        t = lax.broadcasted_iota(jnp.int32, (S, 1), 0)
        s = dpooled / cnt
        levels = []
        for k in (1, 2, 4, 8):
            s = s + jnp.where(t < S - k, pltpu.roll(s, S - k, 0), 0.0)
            levels.append(s)
        dv_ref[...] = _pool_sel(g, levels) - dpooled

    return pl.pallas_call(
        body, name="pool_bwd", grid=(4,),
        in_specs=[pl.BlockSpec((S, 256), lambda g: (0, 16 + g)),
                  pl.BlockSpec((S, 256), lambda g: (0, 20 + g)),
                  pl.BlockSpec((S, 256), lambda g: (0, 4 + g)),
                  pl.BlockSpec((None, 256, 256), lambda g: (g, 0, 0)),
                  pl.BlockSpec((1, 256), lambda g: (0, g))],
        out_specs=[pl.BlockSpec((S, 256), lambda g: (0, g)),
                   pl.BlockSpec((S, 256), lambda g: (0, g)),
                   pl.BlockSpec((None, 256, 256), lambda g: (g, 0, 0)),
                   pl.BlockSpec((1, 256), lambda g: (0, g))],
        out_shape=[jax.ShapeDtypeStruct((S, D), F32), jax.ShapeDtypeStruct((S, D), F32),
                   jax.ShapeDtypeStruct((4, 256, 256), F32), jax.ShapeDtypeStruct((1, D), F32)],
        compiler_params=_cparams(("parallel",)),
    )(z, z, d_cat, pw, ps)


CH = 128


def _sgu_common(v, lng, lnb, w_ref):
    mu = jnp.mean(v, axis=-1, keepdims=True)
    vc = v - mu
    rs = lax.rsqrt(jnp.mean(vc * vc, axis=-1, keepdims=True) + EPS)
    xhat = vc * rs
    vn = (xhat * lng + lnb).astype(BF16)
    ri = lax.broadcasted_iota(jnp.int32, (CH, CH), 0)
    ci = lax.broadcasted_iota(jnp.int32, (CH, CH), 1)
    tril = ri >= ci
    ws = [jnp.where(tril, w_ref[g], 0.0).astype(BF16) for g in range(4)]
    return xhat, rs, vn, tril, ws


def _zspec(off):
    return pl.BlockSpec((CH, D), lambda c: (c, off))


def _full(shape):
    return pl.BlockSpec(shape, lambda c: (0,) * len(shape))


def sgu_fwd(z, lng, lnb, w, bfull):
    def body(u_ref, v_ref, g_ref, lng_ref, lnb_ref, w_ref, b_ref, o_ref):
        _, _, vn, _, ws = _sgu_common(v_ref[...], lng_ref[...], lnb_ref[...], w_ref)
        for g in range(4):
            sl = slice(g * 256, (g + 1) * 256)
            mixed = _nn(ws[g], vn[:, sl]) + b_ref[:, sl]
            gc = g_ref[:, sl]
            o_ref[:, sl] = (u_ref[:, sl] * mixed * (gc * _sig(gc))).astype(o_ref.dtype)

    return pl.pallas_call(
        body, name="sgu_fwd", grid=(S // CH,),
        in_specs=[_zspec(0), _zspec(1), _zspec(2), _full((1, D)), _full((1, D)),
                  _full((4, CH, CH)), _full((CH, D))],
        out_specs=pl.BlockSpec((CH, D), lambda c: (c, 0)),
        out_shape=jax.ShapeDtypeStruct((S, D), BF16),
        compiler_params=_cparams(("parallel",)),
    )(z, z, z, lng, lnb, w, bfull)


def sgu_bwd(z, d_cat, lng, lnb, w, bfull):
    def body(u_ref, v_ref, g_ref, d_ref, lng_ref, lnb_ref, w_ref, b_ref,
             du_ref, dv_ref, dg_ref, dw_ref, db_ref, dlg_ref, dlb_ref):
        @pl.when(pl.program_id(0) == 0)
        def _():
            dw_ref[...] = jnp.zeros_like(dw_ref)
            db_ref[...] = jnp.zeros_like(db_ref)
            dlg_ref[...] = jnp.zeros_like(dlg_ref)
            dlb_ref[...] = jnp.zeros_like(dlb_ref)

        lng = lng_ref[...]
        xhat, rs, vn, tril, ws = _sgu_common(v_ref[...], lng, lnb_ref[...], w_ref)
        lane = lax.broadcasted_iota(jnp.int32, (1, 128), 1)
        db = jnp.zeros((CH, 128), F32)
        dvn_parts = []
        for g in range(4):
            sl = slice(g * 256, (g + 1) * 256)
            mixed = _nn(ws[g], vn[:, sl]) + b_ref[:, sl]
            gc = g_ref[:, sl]
            sg = _sig(gc)
            u = u_ref[:, sl]
            dc = d_ref[:, sl]
            du_ref[:, sl] = dc * mixed * (gc * sg)
            dg_ref[:, sl] = dc * u * mixed * (sg * (1.0 + gc * (1.0 - sg)))
            dmx = dc * u * (gc * sg)
            db = db + jnp.where(lane == g, jnp.sum(dmx, axis=-1, keepdims=True), 0.0)
            dmb = dmx.astype(BF16)
            dw_ref[g] += jnp.where(tril, _nt(dmb, vn[:, sl]), 0.0)
            dvn_parts.append(_tn(ws[g], dmb))
        db_ref[...] += db
        dvn = jnp.concatenate(dvn_parts, axis=1)
        dlb_ref[...] += jnp.sum(dvn, axis=0, keepdims=True)
        dlg_ref[...] += jnp.sum(dvn * xhat, axis=0, keepdims=True)
        dxh = dvn * lng
        dv_ref[...] = rs * (dxh - jnp.mean(dxh, axis=-1, keepdims=True)
                            - xhat * jnp.mean(dxh * xhat, axis=-1, keepdims=True))

    row = pl.BlockSpec((CH, D), lambda c: (c, 0))
    return pl.pallas_call(
        body, name="sgu_bwd", grid=(S // CH,),
        in_specs=[_zspec(0), _zspec(1), _zspec(2), row, _full((1, D)), _full((1, D)),
                  _full((4, CH, CH)), _full((CH, D))],
        out_specs=[row, row, row, _full((4, CH, CH)), _full((CH, 128)), _full((1, D)), _full((1, D))],
        out_shape=[jax.ShapeDtypeStruct((S, D), F32)] * 3
        + [jax.ShapeDtypeStruct((4, CH, CH), F32), jax.ShapeDtypeStruct((CH, 128), F32),
           jax.ShapeDtypeStruct((1, D), F32), jax.ShapeDtypeStruct((1, D), F32)],
        compiler_params=_cparams(("arbitrary",)),
    )(z, z, z, d_cat, lng, lnb, w, bfull)


TB = 256


def _cmul(ar, ai, br, bi):
    return ar * br - ai * bi, ar * bi + ai * br


def _scan_consts(ar, ai, reverse):
    a2 = _cmul(ar, ai, ar, ai)
    a4 = _cmul(*a2, *a2)
    row = lax.broadcasted_iota(jnp.int32, (8, NS), 0)
    pr = jnp.zeros((8, NS), F32)
    pi = jnp.zeros((8, NS), F32)
    cr, ci = ar, ai
    for r in range(8):
        sel = row == (7 - r if reverse else r)
        pr = jnp.where(sel, cr, pr)
        pi = jnp.where(sel, ci, pi)
        cr, ci = _cmul(cr, ci, ar, ai)
    return ((ar, ai), a2, a4), (pr, pi), row


def scan_fwd(bu, abr, abi):
    def body(bu_ref, ar_ref, ai_ref, h_ref, car, cai):
        @pl.when(pl.program_id(0) == 0)
        def _():
            car[...] = jnp.zeros_like(car)
            cai[...] = jnp.zeros_like(cai)

        pows, (pr, pi), row = _scan_consts(ar_ref[...], ai_ref[...], False)

        def tile(t, carry):
            c_r, c_i = carry
            rows = pl.ds(pl.multiple_of(t * 8, 8), 8)
            xr = bu_ref[rows, 0:NS]
            xi = bu_ref[rows, NS:2 * NS]
            for k, (kr, ki) in zip((1, 2, 4), pows):
                sr = jnp.where(row >= k, pltpu.roll(xr, k, 0), 0.0)
                si = jnp.where(row >= k, pltpu.roll(xi, k, 0), 0.0)
                xr, xi = xr + kr * sr - ki * si, xi + kr * si + ki * sr
            xr, xi = xr + pr * c_r - pi * c_i, xi + pr * c_i + pi * c_r
            h_ref[rows, 0:NS] = xr
            h_ref[rows, NS:2 * NS] = xi
            return (jnp.broadcast_to(xr[7:8, :], (8, NS)), jnp.broadcast_to(xi[7:8, :], (8, NS)))

        c_r, c_i = lax.fori_loop(0, TB // 8, tile, (car[...], cai[...]))
        car[...] = c_r
        cai[...] = c_i

    return pl.pallas_call(
        body, name="s5_scan_fwd", grid=(S // TB,),
        in_specs=[pl.BlockSpec((TB, 2 * NS), lambda i: (i, 0)),
                  pl.BlockSpec((1, NS), lambda i: (0, 0)), pl.BlockSpec((1, NS), lambda i: (0, 0))],
        out_specs=pl.BlockSpec((TB, 2 * NS), lambda i: (i, 0)),
        out_shape=jax.ShapeDtypeStruct((S, 2 * NS), F32),
        scratch_shapes=[pltpu.VMEM((8, NS), F32), pltpu.VMEM((8, NS), F32)],
        compiler_params=_cparams(("arbitrary",)),
    )(bu, abr, abi)


def scan_bwd(eta, h, abr, abi):
    nt = S // TB

    def body(e_ref, h_ref, ar_ref, ai_ref, l_ref, da_ref, car, cai):
        @pl.when(pl.program_id(0) == 0)
        def _():
            car[...] = jnp.zeros_like(car)
            cai[...] = jnp.zeros_like(cai)
            da_ref[...] = jnp.zeros_like(da_ref)

        pows, (pr, pi), row = _scan_consts(ar_ref[...], -ai_ref[...], True)

        def tile(tt, carry):
            c_r, c_i, acr, aci = carry
            t = TB // 8 - 1 - tt
            rows = pl.ds(pl.multiple_of(t * 8, 8), 8)
            xr = e_ref[rows, 0:NS]
            xi = e_ref[rows, NS:2 * NS]
            for k, (kr, ki) in zip((1, 2, 4), pows):
                sr = jnp.where(row < 8 - k, pltpu.roll(xr, 8 - k, 0), 0.0)
                si = jnp.where(row < 8 - k, pltpu.roll(xi, 8 - k, 0), 0.0)
                xr, xi = xr + kr * sr - ki * si, xi + kr * si + ki * sr
            xr, xi = xr + pr * c_r - pi * c_i, xi + pr * c_i + pi * c_r
            l_ref[rows, 0:NS] = xr
            l_ref[rows, NS:2 * NS] = xi
            nr = jnp.where(row < 7, pltpu.roll(xr, 7, 0), c_r)
            ni = jnp.where(row < 7, pltpu.roll(xi, 7, 0), c_i)
            hr = h_ref[rows, 0:NS]
            hi = h_ref[rows, NS:2 * NS]
            acr = acr + hr * nr + hi * ni
            aci = aci + hr * ni - hi * nr
            return (jnp.broadcast_to(xr[0:1, :], (8, NS)), jnp.broadcast_to(xi[0:1, :], (8, NS)), acr, aci)

        zero = jnp.zeros((8, NS), F32)
        c_r, c_i, acr, aci = lax.fori_loop(0, TB // 8, tile, (car[...], cai[...], zero, zero))
        car[...] = c_r
        cai[...] = c_i
        da_ref[:, 0:NS] += acr
        da_ref[:, NS:2 * NS] += aci

    rev = pl.BlockSpec((TB, 2 * NS), lambda i: (nt - 1 - i, 0))
    return pl.pallas_call(
        body, name="s5_scan_bwd", grid=(nt,),
        in_specs=[rev, rev, pl.BlockSpec((1, NS), lambda i: (0, 0)), pl.BlockSpec((1, NS), lambda i: (0, 0))],
        out_specs=[rev, pl.BlockSpec((8, 2 * NS), lambda i: (0, 0))],
        out_shape=[jax.ShapeDtypeStruct((S, 2 * NS), F32), jax.ShapeDtypeStruct((8, 2 * NS), F32)],
        scratch_shapes=[pltpu.VMEM((8, NS), F32), pltpu.VMEM((8, NS), F32)],
        compiler_params=_cparams(("arbitrary",)),
    )(eta, h, abr, abi)


GC = 0.7978845608028654
GA = 0.044715


def s5_post(hc, z, dskip):
    def fn(hv, xd, dv):
        y = hv + dv * xd
        return y, 0.5 * y * (1.0 + jnp.tanh(GC * (y + GA * y * y * y)))
    return rw(fn, [(hc, 0, 512), (z, 3072, 512)], [(512, F32), (512, BF16)], "s5_post", S, consts=[dskip])


def s5_post_bwd(dyg, ypre, z, dskip):
    def fn(dy, y, xd, dv):
        th = jnp.tanh(GC * (y + GA * y * y * y))
        dg = 0.5 * (1.0 + th) + 0.5 * y * (1.0 - th * th) * GC * (1.0 + 3.0 * GA * y * y)
        dyp = dy * dg
        return dyp, dyp * dv, jnp.sum(dyp * xd, axis=0, keepdims=True)
    return rw(fn, [(dyg, 0, 512), (ypre, 0, 512), (z, 3072, 512)], [(512, BF16), (512, F32)],
              "s5_post_bwd", S, consts=[dskip], accs=[(1, 512)])


def glu_fwd(t, z, c_out):
    def fn(t1, t2, gd, co):
        return (jnp.concatenate([co, (t1 * _sig(t2) * (gd * _sig(gd))).astype(BF16)], axis=1),)
    return rw(fn, [(t, 0, 512), (t, 512, 512), (z, 3584, 512), (c_out, 0, D)], [(D + 512, BF16)], "glu_fwd", S)[0]


def glu_bwd(t, z, d_cat):
    def fn(t1, t2, gd, dd):
        s2, sg = _sig(t2), _sig(gd)
        sl = gd * sg
        return (jnp.concatenate([dd * s2 * sl, dd * t1 * s2 * (1.0 - s2) * sl], axis=1),
                dd * t1 * s2 * (sg * (1.0 + gd * (1.0 - sg))))
    return rw(fn, [(t, 0, 512), (t, 512, 512), (z, 3584, 512), (d_cat, 1024, 512)],
              [(D, BF16), (512, F32)], "glu_bwd", S)


def assemble_dz_odd(du, dv, dgc, dxd, dgd):
    def body(a, b, c, d, e, o_ref):
        o_ref[:, 0:D] = a[...].astype(BF16)
        o_ref[:, D:2 * D] = b[...].astype(BF16)
        o_ref[:, 2 * D:3 * D] = c[...].astype(BF16)
        o_ref[:, 3 * D:3 * D + 512] = d[...].astype(BF16)
        o_ref[:, 3 * D + 512:4 * D] = e[...].astype(BF16)
    tr = 256
    blk = pl.BlockSpec((tr, D), lambda i: (i, 0))
    half = pl.BlockSpec((tr, 512), lambda i: (i, 0))
    return pl.pallas_call(
        body, name="assemble_dz_odd", grid=(S // tr,), in_specs=[blk, blk, blk, half, half],
        out_specs=pl.BlockSpec((tr, 4 * D), lambda i: (i, 0)),
        out_shape=jax.ShapeDtypeStruct((S, 4 * D), BF16),
        compiler_params=_cparams(("parallel",)),
    )(du, dv, dgc, dxd, dgd)


TQ = 256


def _xattn_probs(qh, kh):
    s = _nt(qh, kh) * 0.0625
    p = jnp.exp(s - jnp.max(s, axis=-1, keepdims=True))
    return p / jnp.sum(p, axis=-1, keepdims=True)


def xattn_fwd(q, kv):
    def body(q_ref, kv_ref, o_ref):
        for h in range(4):
            sl = slice(h * 256, (h + 1) * 256)
            p = _xattn_probs(q_ref[:, sl].astype(BF16), kv_ref[:, sl].astype(BF16))
            vh = kv_ref[:, D + h * 256:D + (h + 1) * 256].astype(BF16)
            o_ref[:, sl] = _nn(p.astype(BF16), vh).astype(o_ref.dtype)

    return pl.pallas_call(
        body, name="xattn_fwd", grid=(S // TQ,),
        in_specs=[pl.BlockSpec((TQ, D), lambda i: (i, 0)), pl.BlockSpec((MEM, 2 * D), lambda i: (0, 0))],
        out_specs=pl.BlockSpec((TQ, D), lambda i: (i, 0)),
        out_shape=jax.ShapeDtypeStruct((S, D), BF16),
        compiler_params=_cparams(("parallel",)),
    )(q, kv)


def xattn_bwd(q, kv, d_o):
    def body(q_ref, kv_ref, do_ref, dq_ref, dkv_ref):
        @pl.when(pl.program_id(0) == 0)
        def _():
            dkv_ref[...] = jnp.zeros_like(dkv_ref)

        for h in range(4):
            sl = slice(h * 256, (h + 1) * 256)
            vs = slice(D + h * 256, D + (h + 1) * 256)
            qh = q_ref[:, sl].astype(BF16)
            kh = kv_ref[:, sl].astype(BF16)
            vh = kv_ref[:, vs].astype(BF16)
            doh = do_ref[:, sl].astype(BF16)
            p = _xattn_probs(qh, kh)
            dp = _nt(doh, vh)
            ds = (p * (dp - jnp.sum(p * dp, axis=-1, keepdims=True)) * 0.0625).astype(BF16)
            dq_ref[:, sl] = _nn(ds, kh).astype(dq_ref.dtype)
            dkv_ref[:, sl] += _tn(ds, qh)
            dkv_ref[:, vs] += _tn(p.astype(BF16), doh)

    return pl.pallas_call(
        body, name="xattn_bwd", grid=(S // TQ,),
        in_specs=[pl.BlockSpec((TQ, D), lambda i: (i, 0)), pl.BlockSpec((MEM, 2 * D), lambda i: (0, 0)),
                  pl.BlockSpec((TQ, D), lambda i: (i, 0))],
        out_specs=[pl.BlockSpec((TQ, D), lambda i: (i, 0)), pl.BlockSpec((MEM, 2 * D), lambda i: (0, 0))],
        out_shape=[jax.ShapeDtypeStruct((S, D), BF16), jax.ShapeDtypeStruct((MEM, 2 * D), F32)],
        compiler_params=_cparams(("arbitrary",)),
    )(q, kv, d_o)


def _s5_disc(a_re, a_im, log_dt, b_re, b_im):
    dt = jnp.exp(log_dt)[:, None]
    mag = jnp.exp(dt * a_re)
    abr = mag * jnp.cos(dt * a_im)
    abi = mag * jnp.sin(dt * a_im)
    nr, ni = abr - 1.0, abi
    inv = 1.0 / (a_re * a_re + a_im * a_im)
    cr = (nr * a_re + ni * a_im) * inv
    ci = (ni * a_re - nr * a_im) * inv
    bbr = cr[..., None] * b_re - ci[..., None] * b_im
    bbi = cr[..., None] * b_im + ci[..., None] * b_re
    return abr, abi, bbr, bbi


def _blockdiag(t):
    g, a, b = t.shape
    eye = jnp.eye(g, dtype=t.dtype)
    return (eye[:, None, :, None] * t[:, :, None, :]).reshape(g * a, g * b)


def _blocks(mat, a, b):
    return jnp.einsum("gagb->gab", mat.reshape(NG, a, NG, b))


def _fwd_even(i, x, P, W):
    hn = rms_fwd(x, P["norm_ab"][i:i + 1], "rms_ab_fwd")
    z = mm(m2(hn), W["w_in"], "nn", "in_ab")
    o, lse, cat = attn_fwd(z)
    if "more" in W:
        W.update(W.pop("more")(cat))
    cat = pool_fwd(z, W["pool_w"], P["pool_scale"][i:i + 1], cat)
    x_mid = mm(m2(cat), W["w_out"], "nn", "out_ab", add=m2(x))
    return x_mid, dict(x=x, hn=hn, z=z, o=o, lse=lse, cat=cat)


def _bwd_even(i, dx_mid, sv, P, W, G, GW):
    z = sv["z"]
    d_cat = mm(m2(dx_mid), W["w_out"], "nt", "out_ab_dx")
    GW["w_out"] = mm(m2(sv["cat"]), m2(dx_mid), "tn", "out_ab_dw").reshape(4, 512, D)
    dq, dk, dv, dga = attn_bwd(z, d_cat, sv["o"], sv["lse"])
    dvb, dgb, dpw, dps = pool_bwd(z, d_cat, W["pool_w"], P["pool_scale"][i:i + 1])
    GW["pool_w"] = dpw.reshape(4, 4, 64, 256).transpose(1, 0, 2, 3).reshape(4, 256, 256)
    G["pool_scale"][i] = dps[0]
    d_z = assemble_dz_even((dq, dk, dv, dga, dvb, dgb))
    d_hn = mm(m2(d_z), W["w_in"], "nt", "in_ab_dx")
    GW["w_in"] = mm(m2(sv["hn"]), m2(d_z), "tn", "in_ab_dw", out=outcs(D, 1536))
    return d_hn, P["norm_ab"][i:i + 1], "norm_ab", "rms_ab_bwd"


def _fwd_odd(i, x, P, W):
    hn = rms_fwd(x, P["norm_cd"][i:i + 1], "rms_cd_fwd")
    z = mm(m2(hn), W["w_in"], "nn", "in_cd")
    bfull = jnp.repeat(P["sgu_b"][i].T, 256, axis=1)
    c_out = sgu_fwd(z, P["sgu_ln_g"][i:i + 1], P["sgu_ln_b"][i:i + 1], P["sgu_w"][i], bfull)
    disc, disc_vjp = jax.vjp(_s5_disc, P["s5_a_re"][i], P["s5_a_im"][i], P["s5_log_dt"][i],
                             P["s5_b_re"][i], P["s5_b_im"][i])
    abr, abi, bbr, bbi = disc
    bbd = jnp.concatenate([_blockdiag(bbr.transpose(0, 2, 1)), _blockdiag(bbi.transpose(0, 2, 1))], axis=1)
    cbd = jnp.concatenate([_blockdiag(P["s5_c_re"][i].transpose(0, 2, 1)),
                           -_blockdiag(P["s5_c_im"][i].transpose(0, 2, 1))], axis=0)
    abr, abi = abr.reshape(1, NS), abi.reshape(1, NS)
    bu = mm(m2(z, 3072, 512), m2(bbd), "nn", "s5_bu")
    h = scan_fwd(bu, abr, abi)
    hc = mm(m2(h), m2(cbd), "nn", "s5_hc")
    dskip = P["s5_d"][i:i + 1]
    ypre, yg = s5_post(hc, z, dskip)
    if "more" in W:
        W.update(W.pop("more")(yg))
    w12 = W["w12"]
    t = mm(m2(yg), m2(w12), "nn", "glu_t")
    cat = glu_fwd(t, z, c_out)
    x_mid = mm(m2(cat), W["w_out"], "nn", "out_cd", add=m2(x))
    return x_mid, dict(x=x, hn=hn, z=z, bfull=bfull, disc_vjp=disc_vjp, bbd=bbd, cbd=cbd, abr=abr,
                       abi=abi, h=h, ypre=ypre, yg=yg, w12=w12, t=t, cat=cat, dskip=dskip)


def _bwd_odd(i, dx_mid, sv, P, W, G, GW):
    z = sv["z"]
    d_cat = mm(m2(dx_mid), W["w_out"], "nt", "out_cd_dx")
    GW["w_out"] = mm(m2(sv["cat"]), m2(dx_mid), "tn", "out_cd_dw").reshape(4, 384, D)
    du, dv, dgc, dws, dbs, dlg, dlb = sgu_bwd(z, d_cat, P["sgu_ln_g"][i:i + 1], P["sgu_ln_b"][i:i + 1],
                                               P["sgu_w"][i], sv["bfull"])
    G["sgu_w"][i], G["sgu_b"][i] = dws, dbs[:, :4].T
    G["sgu_ln_g"][i], G["sgu_ln_b"][i] = dlg[0], dlb[0]
    dt, dgd = glu_bwd(sv["t"], z, d_cat)
    gw12 = mm(m2(sv["yg"]), m2(dt), "tn", "glu_dw")
    GW["glu_w1"] = gw12[:, :512].reshape(4, 128, 512)
    GW["glu_w2"] = gw12[:, 512:].reshape(4, 128, 512)
    dyg = mm(m2(dt), m2(sv["w12"]), "nt", "glu_dx")
    dypre, dxd1, dd = s5_post_bwd(dyg, sv["ypre"], z, sv["dskip"])
    G["s5_d"][i] = dd[0]
    gcbd = mm(m2(sv["h"]), m2(dypre), "tn", "s5_dc")
    G["s5_c_re"][i] = _blocks(gcbd[:NS], NP, NH).transpose(0, 2, 1)
    G["s5_c_im"][i] = -_blocks(gcbd[NS:], NP, NH).transpose(0, 2, 1)
    eta = mm(m2(dypre), m2(sv["cbd"]), "nt", "s5_eta")
    lam, dacc = scan_bwd(eta, sv["h"], sv["abr"], sv["abi"])
    gbbd = mm(m2(z, 3072, 512), m2(lam), "tn", "s5_db")
    dxd = mm(m2(lam), m2(sv["bbd"]), "nt", "s5_dx", add=m2(dxd1))
    dacc = jnp.sum(dacc, axis=0)
    d_bbr = _blocks(gbbd[:, :NS], NH, NP).transpose(0, 2, 1)
    d_bbi = _blocks(gbbd[:, NS:], NH, NP).transpose(0, 2, 1)
    (G["s5_a_re"][i], G["s5_a_im"][i], G["s5_log_dt"][i], G["s5_b_re"][i], G["s5_b_im"][i]) = sv["disc_vjp"](
        (dacc[:NS].reshape(NG, NP), dacc[NS:].reshape(NG, NP), d_bbr, d_bbi))
    d_z = assemble_dz_odd(du, dv, dgc, dxd, dgd)
    d_hn = mm(m2(d_z), W["w_in"], "nt", "in_cd_dx")
    GW["w_in"] = mm(m2(sv["hn"]), m2(d_z), "tn", "in_cd_dw", out=outcs(D, 1024))
    return d_hn, P["norm_cd"][i:i + 1], "norm_cd", "rms_cd_bwd"


def _fwd_x(l, x, mem_n, P, W):
    hx = rms_fwd(x, P["norm_x"][l:l + 1], "rms_x_fwd")
    q = mm(m2(hx), W["w_xq"], "nn", "xq", out_dtype=BF16)
    kv = mm(m2(mem_n), W["w_xkv"], "nn", "xkv", out_dtype=BF16)
    ox = xattn_fwd(q, kv)
    x_out = mm(m2(ox), W["w_xo"], "nn", "xo", add=m2(x))
    return x_out, dict(x=x, hx=hx, q=q, kv=kv, ox=ox)


def _bwd_x(l, dx_out, sv, mem_n, d_memn, P, W, G, GW):
    d_ox = mm(m2(dx_out), W["w_xo"], "nt", "xo_dx", out_dtype=BF16)
    GW["w_xo"] = mm(m2(sv["ox"]), m2(dx_out), "tn", "xo_dw").reshape(4, 256, D)
    dq, dkv = xattn_bwd(sv["q"], sv["kv"], d_ox)
    GW["w_xq"] = mm(m2(sv["hx"]), m2(dq), "tn", "xq_dw").reshape(4, 256, D)
    d_hx = mm(m2(dq), W["w_xq"], "nt", "xq_dx")
    GW["w_xkv"] = mm(m2(mem_n), m2(dkv), "tn", "xkv_dw", out=outcs(D, 512))
    d_memn = mm(m2(dkv), W["w_xkv"], "nt", "xkv_dx", add=None if d_memn is None else m2(d_memn))
    dx, dg = rms_bwd(sv["x"], d_hx, dx_out, P["norm_x"][l:l + 1], "rms_x_bwd")
    G["norm_x"][l] = dg[0]
    return dx, d_memn


SMALL_LAYERS = (("norm_ab", 2), ("pool_scale", 2), ("norm_cd", 2), ("sgu_ln_g", 2), ("sgu_ln_b", 2), ("sgu_w", 2),
                ("sgu_b", 2), ("s5_a_re", 2), ("s5_a_im", 2), ("s5_log_dt", 2), ("s5_b_re", 2), ("s5_b_im", 2),
                ("s5_c_re", 2), ("s5_c_im", 2), ("s5_d", 2), ("norm_x", 4))


def local_step(x, mem, tgt, P, weights_of, grads_done):
    G = {k: [None] * n for k, n in SMALL_LAYERS}
    mem_g = P["mem_norm"].reshape(1, D)
    mem_n = rms_fwd(mem, mem_g, "rms_mem_fwd")
    saved = []
    for layer in range(4):
        i = layer // 2
        W = weights_of(layer, x)
        x, sv_m = (_fwd_even if layer % 2 == 0 else _fwd_odd)(i, x, P, W)
        x, sv_x = _fwd_x(layer, x, mem_n, P, W)
        saved.append((sv_m, sv_x, W))
    dx, loss, dgf = final_loss(x, tgt, P["final_norm"].reshape(1, D))
    G["final_norm"] = dgf[0]
    d_memn = None
    for layer in reversed(range(4)):
        i = layer // 2
        sv_m, sv_x, W = saved[layer]
        GW = {}
        dx_mid, d_memn = _bwd_x(layer, dx, sv_x, mem_n, d_memn, P, W, G, GW)
        d_hn, g, key, name = (_bwd_even if layer % 2 == 0 else _bwd_odd)(i, dx_mid, sv_m, P, W, G, GW)
        token = grads_done(layer, GW)
        if token is not None:
            g = g + token
        dx, dg = rms_bwd(sv_m["x"], d_hn, dx_mid, g, name)
        G[key][i] = dg[0]
    _, dgm = rms_bwd(mem, d_memn, d_memn, mem_g, "rms_mem_bwd")
    G["mem_norm"] = dgm[0]
    return loss, dx, G


ANY = pl.BlockSpec(memory_space=pl.ANY)


def _place():
    x, y, c = lax.axis_index("x"), lax.axis_index("y"), lax.axis_index("c")
    chips = [(1 - x, y), (x, 1 - y), (1 - x, 1 - y)]
    return x, y, c, 2 * x + y, (x, y, 1 - c), chips


def _remote(src, dst, send, recv, k, dev):
    return pltpu.make_async_remote_copy(src_ref=src, dst_ref=dst, send_sem=send.at[k], recv_sem=recv.at[k],
                                        device_id=dev, device_id_type=MESHID)


HBM = pl.BlockSpec(memory_space=pltpu.HBM)
SEM = pl.BlockSpec(memory_space=pltpu.SEMAPHORE)
EFFECT = pltpu.SideEffectType.DATAFLOW_SIDE_EFFECTING


def _hbm(t):
    return pltpu.with_memory_space_constraint(t, pltpu.HBM)


def allgather_sync(shards):
    n = len(shards)

    def body(*refs):
        ins, outs = refs[:n], refs[n:2 * n]
        token, send, recv = refs[2 * n:]
        x, y, c, jme, sib, chips = _place()
        first, passed = [], []
        for a in range(n):
            cp = _remote(ins[a], outs[a].at[jme], send, recv, a * 7 + 6, sib)
            cp.start()
            first.append(cp)
            for k, chip in enumerate(chips):
                cp = _remote(ins[a].at[c], outs[a].at[jme, c], send, recv, a * 7 + k, (*chip, c))
                cp.start()
                first.append(cp)
        for a in range(n):
            for k, chip in enumerate(chips):
                piece = outs[a].at[2 * chip[0] + chip[1], c]
                _remote(piece, piece, send, recv, a * 7 + k, (*chip, c)).wait_recv()
                fw = _remote(piece, piece, send, recv, a * 7 + 3 + k, sib)
                fw.start()
                passed.append(fw)
        for a in range(n):
            own = outs[a].at[jme]
            _remote(own, own, send, recv, a * 7 + 6, sib).wait_recv()
            for k, chip in enumerate(chips):
                piece = outs[a].at[2 * chip[0] + chip[1], 1 - c]
                _remote(piece, piece, send, recv, a * 7 + 3 + k, sib).wait_recv()
        for cp in first + passed:
            cp.wait_send()
        token[...] = jnp.zeros_like(token)

    res = pl.pallas_call(
        body, name="allgather_sync", in_specs=[ANY] * n,
        out_specs=[ANY] * n + [pl.BlockSpec(memory_space=pltpu.VMEM)],
        out_shape=[jax.ShapeDtypeStruct((4,) + s.shape, s.dtype) for s in shards] + [jax.ShapeDtypeStruct((8, 128), F32)],
        scratch_shapes=[pltpu.SemaphoreType.DMA((7 * n,)), pltpu.SemaphoreType.DMA((7 * n,))],
    )(*shards)
    return list(res[:n]), res[n]


def _gather_copies(ins, lands, send, recv):
    x, y, c, jme, sib, chips = _place()
    devs = [(*chip, c) for chip in chips] + [sib]
    return [_remote(ins[a], lands[a].at[jme], send, recv, a * 4 + k, dev)
            for a in range(len(ins)) for k, dev in enumerate(devs)]


def allgather_start(shards, after, name):
    n, na = len(shards), len(after)

    def body(*refs):
        ins, lands = refs[:n], refs[n:2 * n]
        send, recv = refs[2 * n + na], refs[2 * n + na + 1]
        token = refs[-1]
        for cp in _gather_copies(ins, lands, send, recv):
            cp.start()
        token[...] = jnp.zeros_like(token)

    res = pl.pallas_call(
        body, name=name,
        out_shape=(pltpu.SemaphoreType.DMA((4 * n,)), pltpu.SemaphoreType.DMA((4 * n,)),
                   *[pltpu.HBM(s.shape, s.dtype) for s in shards],
                   *[pltpu.HBM((4,) + s.shape, s.dtype) for s in shards],
                   jax.ShapeDtypeStruct((8, 128), F32)),
        in_specs=[HBM] * (2 * n) + [ANY] * na,
        out_specs=(SEM, SEM, *[HBM] * (2 * n), pl.BlockSpec(memory_space=pltpu.VMEM)),
        input_output_aliases={a: 2 + a for a in range(2 * n)},
        compiler_params=pltpu.CompilerParams(has_side_effects=EFFECT),
    )(*[_hbm(s) for s in shards], *[_hbm(lax.empty((4,) + s.shape, s.dtype)) for s in shards], *after)
    return res[0], res[1], list(res[2:2 + n]), list(res[2 + n:2 + 2 * n]), res[-1]


def allgather_wait(send, recv, shards, lands, after, name):
    n = len(shards)

    def body(*refs):
        ins, zones = refs[:n], refs[n:2 * n]
        send_r, recv_r = refs[2 * n], refs[2 * n + 1]
        x, y, c, jme, sib, chips = _place()
        slots = [2 * chip[0] + chip[1] for chip in chips] + [jme]
        for a in range(n):
            for k, slot in enumerate(slots):
                cp = _remote(ins[a], zones[a].at[slot], send_r, recv_r, a * 4 + k, sib)
                cp.wait_send()
                cp.wait_recv()

    res = pl.pallas_call(
        body, name=name,
        out_shape=tuple(pltpu.HBM(t.shape, t.dtype) for t in list(shards) + list(lands)),
        in_specs=[HBM] * (2 * n) + [SEM, SEM, ANY], out_specs=tuple([HBM] * (2 * n)),
        input_output_aliases={a: a for a in range(2 * n)},
        compiler_params=pltpu.CompilerParams(has_side_effects=EFFECT),
    )(*shards, *lands, send, recv, after)
    return list(res[n:])


def allgather_small(slab):
    def body(in_ref, out_ref, send, recv, lsem):
        x, y, c, jme, sib, chips = _place()
        loc = pltpu.make_async_copy(in_ref, out_ref.at[jme], lsem.at[0])
        loc.start()
        cps = [_remote(in_ref, out_ref.at[jme], send, recv, k, (*chip, c)) for k, chip in enumerate(chips)]
        for cp in cps:
            cp.start()
        for k, chip in enumerate(chips):
            piece = out_ref.at[2 * chip[0] + chip[1]]
            _remote(piece, piece, send, recv, k, (*chip, c)).wait_recv()
        for cp in cps:
            cp.wait_send()
        loc.wait()

    return pl.pallas_call(
        body, name="allgather_small", in_specs=[ANY], out_specs=ANY,
        out_shape=jax.ShapeDtypeStruct((4,) + slab.shape, slab.dtype),
        scratch_shapes=[pltpu.SemaphoreType.DMA((3,)), pltpu.SemaphoreType.DMA((3,)), pltpu.SemaphoreType.DMA((1,))],
    )(slab)


def allreduce_small(v):
    def body(v_ref, o_ref, r0, r1, r2, send, recv):
        x, y, c, jme, sib, chips = _place()
        peers = [sib, (1 - x, y, c), (x, 1 - y, c)]
        o_ref[...] = v_ref[...]
        for k, buf in enumerate((r0, r1, r2)):
            cp = _remote(o_ref, buf, send, recv, k, peers[k])
            cp.start()
            cp.wait()
            o_ref[...] = o_ref[...] + buf[...]

    vm = pl.BlockSpec(memory_space=pltpu.VMEM)
    return pl.pallas_call(
        body, name="allreduce_small", in_specs=[vm], out_specs=vm,
        out_shape=jax.ShapeDtypeStruct(v.shape, v.dtype),
        scratch_shapes=[pltpu.VMEM(v.shape, v.dtype)] * 3 + [pltpu.SemaphoreType.DMA((3,)), pltpu.SemaphoreType.DMA((3,))],
        compiler_params=pltpu.CompilerParams(vmem_limit_bytes=VMEM_LIMIT),
    )(v)


def _pair_copies(gs, lands, send, recv):
    x, y, c, jme, sib, chips = _place()
    return [_remote(gs[a].at[:, 1 - c], lands[a], send, recv, a, sib) for a in range(len(gs))]


def rs_pair_start(gs, name):
    n = len(gs)

    def body(*refs):
        ins, lands = refs[:n], refs[n:2 * n]
        send, recv = refs[2 * n], refs[2 * n + 1]
        token = refs[-1]
        for cp in _pair_copies(ins, lands, send, recv):
            cp.start()
        token[...] = jnp.zeros_like(token)

    shapes = [(4,) + g.shape[2:] for g in gs]
    res = pl.pallas_call(
        body, name=name,
        out_shape=(pltpu.SemaphoreType.DMA((n,)), pltpu.SemaphoreType.DMA((n,)),
                   *[pltpu.HBM(g.shape, g.dtype) for g in gs], *[pltpu.HBM(s, F32) for s in shapes],
                   jax.ShapeDtypeStruct((8, 128), F32)),
        in_specs=[HBM] * (2 * n), out_specs=(SEM, SEM, *[HBM] * (2 * n), pl.BlockSpec(memory_space=pltpu.VMEM)),
        input_output_aliases={a: 2 + a for a in range(2 * n)},
        compiler_params=pltpu.CompilerParams(has_side_effects=EFFECT),
    )(*[_hbm(g) for g in gs], *[_hbm(lax.empty(s, F32)) for s in shapes])
    return res[0], res[1], list(res[2:2 + n]), list(res[2 + n:2 + 2 * n]), res[-1]


def rs_pair_wait(send, recv, gs, lands, after, name):
    n = len(gs)

    def body(*refs):
        ins, zones = refs[:n], refs[n:2 * n]
        for cp in _pair_copies(ins, zones, refs[2 * n], refs[2 * n + 1]):
            cp.wait_send()
            cp.wait_recv()

    res = pl.pallas_call(
        body, name=name,
        out_shape=tuple(pltpu.HBM(t.shape, t.dtype) for t in list(gs) + list(lands)),
        in_specs=[HBM] * (2 * n) + [SEM, SEM, ANY], out_specs=tuple([HBM] * (2 * n)),
        input_output_aliases={a: a for a in range(2 * n)},
        compiler_params=pltpu.CompilerParams(has_side_effects=EFFECT),
    )(*gs, *lands, send, recv, after)
    return list(res[:n]), list(res[n:])


def rs_pair_sum(g4, got, cidx):
    _, _, rh, cols = g4.shape
    tr = rh if rh <= 256 else 256

    def body(c_ref, a_ref, b_ref, o_ref):
        o_ref[...] = (a_ref[...] + b_ref[...]).astype(o_ref.dtype)

    return pl.pallas_call(
        body, name="rs_pair_sum",
        grid_spec=pltpu.PrefetchScalarGridSpec(
            num_scalar_prefetch=1, grid=(4, rh // tr),
            in_specs=[pl.BlockSpec((None, None, tr, cols), lambda j, t, cr: (j, cr[0], t, 0)),
                      pl.BlockSpec((None, tr, cols), lambda j, t, cr: (j, t, 0))],
            out_specs=pl.BlockSpec((None, tr, cols), lambda j, t, cr: (j, t, 0))),
        out_shape=jax.ShapeDtypeStruct((4, rh, cols), BF16),
        compiler_params=_cparams(("parallel", "parallel")),
    )(cidx, g4, got)


def _chip_copies(ps, lands, send, recv):
    x, y, c, jme, sib, chips = _place()
    return [_remote(ps[a].at[2 * chip[0] + chip[1]], lands[a].at[jme], send, recv, a * 3 + k, (*chip, c))
            for a in range(len(ps)) for k, chip in enumerate(chips)]


def rs_chip_start(ps, name):
    n = len(ps)

    def body(*refs):
        ins, lands = refs[:n], refs[n:2 * n]
        send, recv = refs[2 * n], refs[2 * n + 1]
        token = refs[-1]
        for cp in _chip_copies(ins, lands, send, recv):
            cp.start()
        token[...] = jnp.zeros_like(token)

    res = pl.pallas_call(
        body, name=name,
        out_shape=(pltpu.SemaphoreType.DMA((3 * n,)), pltpu.SemaphoreType.DMA((3 * n,)),
                   *[pltpu.HBM(p.shape, p.dtype) for p in ps], *[pltpu.HBM(p.shape, p.dtype) for p in ps],
                   jax.ShapeDtypeStruct((8, 128), F32)),
        in_specs=[HBM] * (2 * n), out_specs=(SEM, SEM, *[HBM] * (2 * n), pl.BlockSpec(memory_space=pltpu.VMEM)),
        input_output_aliases={a: 2 + a for a in range(2 * n)},
        compiler_params=pltpu.CompilerParams(has_side_effects=EFFECT),
    )(*[_hbm(p) for p in ps], *[_hbm(lax.empty(p.shape, p.dtype)) for p in ps])
    return res[0], res[1], list(res[2:2 + n]), list(res[2 + n:2 + 2 * n]), res[-1]


def rs_chip_wait(send, recv, ps, lands, after, name):
    n = len(ps)

    def body(*refs):
        ins, zones = refs[:n], refs[n:2 * n]
        send_r, recv_r = refs[2 * n], refs[2 * n + 1]
        x, y, c, jme, sib, chips = _place()
        for a in range(n):
            for k, chip in enumerate(chips):
                jt = 2 * chip[0] + chip[1]
                cp = _remote(ins[a].at[jt], zones[a].at[jt], send_r, recv_r, a * 3 + k, (*chip, c))
                cp.wait_send()
                cp.wait_recv()

    res = pl.pallas_call(
        body, name=name,
        out_shape=tuple(pltpu.HBM(p.shape, p.dtype) for p in list(ps) + list(lands)),
        in_specs=[HBM] * (2 * n) + [SEM, SEM] + [ANY] * len(after), out_specs=tuple([HBM] * (2 * n)),
        input_output_aliases={a: a for a in range(2 * n)},
        compiler_params=pltpu.CompilerParams(has_side_effects=EFFECT),
    )(*ps, *lands, send, recv, *after)
    return list(res[n:])


def rs_chip_sum(q, p, l, acc, layers, jc):
    _, rh, cols = q.shape
    tr = rh if rh <= 256 else 256

    def body(jc_ref, q_ref, p_ref, *rest):
        o_ref = rest[-1]
        jme = jc_ref[0]
        own = p_ref[...].astype(F32)
        v = [jnp.where(jme == j, own, q_ref[j].astype(F32)) for j in range(4)]
        o_ref[...] = ((v[0] + v[1]) + v[2]) + v[3]

    in_specs = [pl.BlockSpec((4, tr, cols), lambda t, jr: (0, t, 0)),
                pl.BlockSpec((None, tr, cols), lambda t, jr: (jr[0], t, 0))]
    args = [jc, q, p]
    if acc is not None:
        in_specs.append(ANY)
        args.append(acc)
    return pl.pallas_call(
        body, name="rs_chip_sum",
        grid_spec=pltpu.PrefetchScalarGridSpec(
            num_scalar_prefetch=1, grid=(rh // tr,), in_specs=in_specs,
            out_specs=pl.BlockSpec((None, None, tr, cols), lambda t, jr: (l, jr[1], t, 0))),
        out_shape=jax.ShapeDtypeStruct((layers, 2, rh, cols), F32),
        input_output_aliases={} if acc is None else {3: 0},
        compiler_params=_cparams(("parallel",)),
    )(*args)


def rs_pair_gather(rs):
    n = len(rs)

    def body(*refs):
        outs = refs[n:2 * n]
        send, recv = refs[2 * n:]
        x, y, c, jme, sib, chips = _place()
        cps = [_remote(outs[a].at[:, c], outs[a].at[:, c], send, recv, a, sib) for a in range(n)]
        for cp in cps:
            cp.start()
        for a in range(n):
            slot = outs[a].at[:, 1 - c]
            _remote(slot, slot, send, recv, a, sib).wait_recv()
        for cp in cps:
            cp.wait_send()

    return pl.pallas_call(
        body, name="rs_pair_gather", in_specs=[ANY] * n, out_specs=[ANY] * n,
        out_shape=[jax.ShapeDtypeStruct(r.shape, r.dtype) for r in rs],
        input_output_aliases={a: a for a in range(n)},
        scratch_shapes=[pltpu.SemaphoreType.DMA((n,)), pltpu.SemaphoreType.DMA((n,))],
    )(*rs)


def _adamw_math(w, g, m, v):
    m = B1 * m + (1.0 - B1) * g
    v = B2 * v + (1.0 - B2) * (g * g)
    m_hat = m / (1.0 - B1 ** STEP)
    v_hat = v / (1.0 - B2 ** STEP)
    return -LR * (m_hat / (jnp.sqrt(v_hat) + AEPS) + WD * w), m, v


def adamw(w, g, m, v, name):
    rows, cols = w.shape
    tr = 256 if rows % 256 == 0 else rows
    return rw(_adamw_math, [(a, 0, cols) for a in (w, g, m, v)], [(cols, F32)] * 3, name, rows, tr=tr)


WEIGHTS = ["norm_ab", "w_in_ab", "pool_w", "pool_scale", "w_out_ab", "norm_cd", "w_in_cd", "sgu_ln_g", "sgu_ln_b",
           "sgu_w", "sgu_b", "s5_a_re", "s5_a_im", "s5_log_dt", "s5_b_re", "s5_b_im", "s5_c_re", "s5_c_im", "s5_d",
           "glu_w1", "glu_w2", "w_out_cd", "norm_x", "w_xq", "w_xkv", "w_xo", "mem_norm", "final_norm"]
INPUTS = ["x", "mem"] + WEIGHTS + ["loss_target"] + ["m_" + n for n in WEIGHTS] + ["v_" + n for n in WEIGHTS]
BIG = ["w_in_ab", "w_out_ab", "w_in_cd", "w_out_cd", "w_xq", "w_xkv", "w_xo", "glu_w1", "glu_w2", "pool_w"]
COL_SHARDED = ("w_in_ab", "w_in_cd", "w_xkv")
SMALL = [n for n in WEIGHTS if n not in BIG]
SMALL_SHARDED = {"norm_cd": 256, "sgu_ln_g": 256, "sgu_ln_b": 256, "s5_d": 128}
PACK = 256 * 128


def _pack(arrs):
    flat = jnp.concatenate([a.reshape(-1) for a in arrs])
    pad = (-flat.shape[0]) % PACK
    return jnp.concatenate([flat, jnp.zeros((pad,), flat.dtype)]).reshape(-1, 128)


def _unpack(packed, shapes):
    flat, out, off = packed.reshape(-1), [], 0
    for s in shapes:
        n = 1
        for d in s:
            n *= d
        out.append(flat[off:off + n].reshape(s))
        off += n
    return out


LAYER_KEYS = (("w_in", "w_out", "pool_w", "w_xq", "w_xkv", "w_xo"),
              ("w_in", "w_out", "glu_w1", "glu_w2", "w_xq", "w_xkv", "w_xo"))


def _weight_of(key, layer):
    if key in ("w_xq", "w_xkv", "w_xo"):
        return key, layer, 4
    kind = "ab" if layer % 2 == 0 else "cd"
    return {"w_in": "w_in_" + kind, "w_out": "w_out_" + kind}.get(key, key), layer // 2, 2


def kernel(*args):
    a = dict(zip(INPUTS, args))
    x_i, y_i, c_i = lax.axis_index("x"), lax.axis_index("y"), lax.axis_index("c")
    j = 2 * x_i + y_i

    slab = jnp.concatenate([a["norm_cd"], a["sgu_ln_g"], a["sgu_ln_b"],
                            jnp.pad(a["s5_d"], ((0, 0), (0, 128)))], axis=0)
    gslab = allgather_small(slab)
    P = {n: a[n] for n in SMALL}
    for k, n in enumerate(("norm_cd", "sgu_ln_g", "sgu_ln_b", "s5_d")):
        wd = SMALL_SHARDED[n]
        P[n] = gslab[:, 2 * k:2 * k + 2, :wd].transpose(1, 0, 2).reshape(2, 4 * wd)

    def shards_of(layer):
        keys = sorted(k for k in LAYER_KEYS[layer % 2])
        out = []
        for k in keys:
            n, l, _ = _weight_of(k, layer)
            out.append(a[n][l].reshape(-1, a[n].shape[-1]).astype(BF16))
        return keys, out

    keys0, sh0 = shards_of(0)
    first = keys0.index("w_in")
    g_in, token = allgather_sync([sh0[first].reshape(2, sh0[first].shape[0] // 2, sh0[first].shape[1])])
    w_in0 = g_in[0].reshape(4, -1, g_in[0].shape[-1])
    started = {}
    for layer in (0, 1, 2, 3):
        keys, sh = (keys0, sh0) if layer == 0 else shards_of(layer)
        rest = [(k, s) for k, s in zip(keys, sh) if k != "w_in"]
        parts = [("in", ["w_in"], [sh[keys.index("w_in")]])] * (layer > 0) + [("", *map(list, zip(*rest)))]
        for tag, pk, ps in parts:
            send, recv, ps, lands, token = allgather_start(ps, [token, gslab], "allgather_start_%d%s" % (layer, tag))
            started[(layer, tag)] = (pk, send, recv, ps, lands)
    P["norm_ab"] = P["norm_ab"] + token[0:1, 0:1]

    cidx = jnp.reshape(c_i, (1,)).astype(jnp.int32)
    jc = jnp.stack([j, c_i]).astype(jnp.int32)

    def views(g):
        W = {}
        for k, v in g.items():
            if k in ("w_in", "w_xkv"):
                W[k] = mcs(v)
            elif k == "pool_w":
                W[k] = v.reshape(4, 4, 64, 256).transpose(1, 0, 2, 3).reshape(4, 256, 256)
            elif k not in ("glu_w1", "glu_w2"):
                W[k] = m2(v.reshape(-1, v.shape[-1]))
        if "glu_w1" in g:
            W["w12"] = jnp.concatenate([g["glu_w1"].reshape(512, 512), g["glu_w2"].reshape(512, 512)], axis=1)
        return W

    def arrived(layer, tag, after):
        keys, send, recv, sh, lands = started[(layer, tag)]
        return views(dict(zip(keys, allgather_wait(send, recv, sh, lands, after, "allgather_wait_%d%s" % (layer, tag)))))

    def weights_of(layer, x_in):
        W = views({"w_in": w_in0}) if layer == 0 else arrived(layer, "in", x_in)
        W["more"] = lambda after: arrived(layer, "", after)
        return W

    halves, pending = {}, {}

    def finish_pair(layer, after):
        keys, send, recv, flat, lands = halves.pop(layer)
        flat, got = rs_pair_wait(send, recv, flat, lands, after, "rs_pair_wait_%d" % layer)
        pair = [rs_pair_sum(g4, r, cidx) for g4, r in zip(flat, got)]
        send, recv, pair, lands, token = rs_chip_start(pair, "rs_chip_start_%d" % layer)
        pending[layer] = (keys, send, recv, pair, lands)
        return token

    def grads_done(layer, GW):
        keys = sorted(GW)
        flat = [GW[k].reshape(4, 2, GW[k].shape[1] // 2, GW[k].shape[2]) for k in keys]
        send, recv, flat, lands, token = rs_pair_start(flat, "rs_pair_start_%d" % layer)
        halves[layer] = (keys, send, recv, flat, lands)
        if layer + 1 in halves:
            token = token + finish_pair(layer + 1, token)
        return token[0:1, 0:1]

    loss, dx, G = local_step(a["x"][0], a["mem"][0], a["loss_target"][0], P, weights_of, grads_done)
    loss = lax.psum(loss[0, 0], ("x", "y", "c"))
    finish_pair(0, dx)
    outs = {}

    def update_big(names, red):
        for n, g in zip(names, rs_pair_gather([red[n] for n in names])):
            shp = a[n].shape
            g2 = g.reshape(-1, shp[-1])
            upd = adamw(a[n].reshape(g2.shape), g2, a["m_" + n].reshape(g2.shape), a["v_" + n].reshape(g2.shape),
                        "adamw_" + n)
            outs[n] = tuple(t.reshape(shp) for t in (g2,) + tuple(upd))

    def reduce_layer(layer, red, after):
        keys, send, recv, pair, lands = pending[layer]
        lands = rs_chip_wait(send, recv, pair, lands, after, "rs_chip_wait_%d" % layer)
        for k, q, p in zip(keys, lands, pair):
            n, l, layers = _weight_of(k, layer)
            red[n] = rs_chip_sum(q, p, l, red.get(n), layers, jc)

    red = {}
    for layer in (3, 2, 1):
        reduce_layer(layer, red, [dx])
    odd_only = [n for n in BIG if n.endswith("_cd") or n.startswith("glu")]
    update_big(odd_only, red)

    gfull = [jnp.stack(G[n]) if isinstance(G[n], list) else G[n] for n in SMALL]
    shapes = [g.shape for g in gfull]
    gsum = _unpack(allreduce_small(_pack(gfull)), shapes)
    gloc = []
    for n, g in zip(SMALL, gsum):
        if n in SMALL_SHARDED:
            g = lax.dynamic_slice_in_dim(g, j * SMALL_SHARDED[n], SMALL_SHARDED[n], axis=1)
        gloc.append(g)
    for n, g in zip(SMALL, gloc):
        shp = a[n].shape
        two = (-1, shp[-1]) if len(shp) > 1 else (1, shp[0])
        upd = adamw(a[n].reshape(two), g.reshape(two), a["m_" + n].reshape(two), a["v_" + n].reshape(two), "adamw_" + n)
        outs[n] = (g,) + tuple(t.reshape(shp) for t in upd)

    behind = [outs[n][1] for n in odd_only + SMALL[-1:]] + [red[n] for n in BIG if n not in odd_only]
    reduce_layer(0, red, behind)
    update_big([n for n in BIG if n not in odd_only], red)

    res = [loss, dx[None]]
    for part in range(4):
        res += [outs[n][part] for n in WEIGHTS]
    return tuple(res)
```

```python
import math

import jax
import jax.numpy as jnp
from jax import lax
from jax.experimental import pallas as pl
from jax.experimental.pallas import tpu as pltpu

F32, BF16 = jnp.float32, jnp.bfloat16
S, D = 2048, 1024
MEM = 256
EPS = 1e-6
NEG = -1e30
QB = 128
PATTERNS = (1, 4, 16)
NG, NP, NH = 32, 64, 16
NS = NG * NP
LR, B1, B2, AEPS, WD, STEP = 0.001, 0.9, 0.999, 1e-08, 0.01, 10
MESHID = pl.DeviceIdType.MESH
VMEM_LIMIT = 56 * 1024 * 1024


def _cparams(sem):
    return pltpu.CompilerParams(dimension_semantics=sem, vmem_limit_bytes=VMEM_LIMIT)


def _sig(x):
    return 1.0 / (1.0 + jnp.exp(-x))


def _dot(a, b, dims):
    return lax.dot_general(a, b, (dims, ((), ())), preferred_element_type=F32)


def _nn(a, b):
    return _dot(a, b, ((1,), (0,)))


def _nt(a, b):
    return _dot(a, b, ((1,), (1,)))


def _tn(a, b):
    return _dot(a, b, ((0,), (0,)))


_DIMS = {"nn": ((1,), (0,)), "nt": ((1,), (1,)), "tn": ((0,), (0,))}


def _tile(dim, cc=None, cap=1024):
    for t in (2048, 1536, 1024, 768, 512, 384, 256, 128):
        if t <= cap and dim % t == 0 and (cc is None or cc % t == 0):
            return t
    return dim


MM_VMEM = 36 * 1024 * 1024


def _mm_tiles(m, n, k, ccm, ccn, cck, a_bytes, b_bytes, o_bytes):
    caps = [1024, 1024, 2048]
    while True:
        tm, tn, tk = _tile(m, ccm, caps[0]), _tile(n, ccn, caps[1]), _tile(k, cck, caps[2])
        need = 2 * (tm * tk * a_bytes + tk * tn * b_bytes + tm * tn * o_bytes) + (tm * tn * 4 if tk < k else 0)
        if need <= MM_VMEM:
            return tm, tn, tk
        if tk > 1024:
            caps[2] = tk // 2
        elif tn >= tm:
            caps[1] = tn // 2
        else:
            caps[0] = tm // 2


def m2(arr, col_off=0, ncols=None):
    rows, cols = arr.shape
    ncols = cols - col_off if ncols is None else ncols

    def spec(tr, tc, rc):
        assert col_off % tc == 0
        return pl.BlockSpec((tr, tc), lambda *g: (rc(*g)[0], rc(*g)[1] + col_off // tc))
    return (arr, rows, ncols, spec, None if col_off == 0 else col_off)


def mcs(arr):
    cs = arr.shape[2]

    def spec(tr, tc, rc):
        n = cs // tc
        return pl.BlockSpec((None, tr, tc), lambda *g: (rc(*g)[1] // n, rc(*g)[0], rc(*g)[1] % n))
    return (arr, arr.shape[1], 4 * cs, spec, cs)


def out2(rows, cols):
    def spec(tr, tc, rc):
        return pl.BlockSpec((tr, tc), lambda *g: tuple(rc(*g)))
    return ((rows, cols), spec, None)


def outcs(rows, cs):
    def spec(tr, tc, rc):
        n = cs // tc
        return pl.BlockSpec((None, tr, tc), lambda *g: (rc(*g)[1] // n, rc(*g)[0], rc(*g)[1] % n))
    return ((4, rows, cs), spec, cs)


def _both(a, b):
    if a is None:
        return b
    if b is None:
        return a
    return math.gcd(a, b)


def mm(a, b, mode, name, add=None, out=None, out_dtype=F32):
    a_arr, a_r, a_c, a_spec, a_cc = a
    b_arr, b_r, b_c, b_spec, b_cc = b
    if mode == "nn":
        m, k, n = a_r, a_c, b_c
        assert b_r == k
        ccm, cck, ccn = None, a_cc, b_cc
    elif mode == "nt":
        m, k, n = a_r, a_c, b_r
        assert b_c == k
        ccm, cck, ccn = None, _both(a_cc, b_cc), None
    else:
        m, k, n = a_c, a_r, b_c
        assert b_r == k
        ccm, cck, ccn = a_cc, None, b_cc
    out = out2(m, n) if out is None else out
    o_shape, o_spec, o_cc = out
    ccn = _both(ccn, o_cc)
    if add is not None:
        ccn = _both(ccn, add[4])
    o_bytes = jnp.dtype(out_dtype).itemsize + (0 if add is None else add[0].dtype.itemsize)
    tm, tn, tk = _mm_tiles(m, n, k, ccm, ccn, cck, a_arr.dtype.itemsize, b_arr.dtype.itemsize, o_bytes)
    nk = k // tk
    if mode == "nn":
        in_specs = [a_spec(tm, tk, lambda i, j, kk: (i, kk)), b_spec(tk, tn, lambda i, j, kk: (kk, j))]
    elif mode == "nt":
        in_specs = [a_spec(tm, tk, lambda i, j, kk: (i, kk)), b_spec(tn, tk, lambda i, j, kk: (j, kk))]
    else:
        in_specs = [a_spec(tk, tm, lambda i, j, kk: (kk, i)), b_spec(tk, tn, lambda i, j, kk: (kk, j))]
    args = [a_arr, b_arr]
    if add is not None:
        in_specs.append(add[3](tm, tn, lambda i, j, kk: (i, j)))
        args.append(add[0])
    dims = _DIMS[mode]
    has_add = add is not None

    def body(*refs):
        a_ref, b_ref = refs[0], refs[1]
        add_ref = refs[2] if has_add else None
        prod = _dot(a_ref[...].astype(BF16), b_ref[...].astype(BF16), dims)
        if nk == 1:
            o_ref = refs[-1]
            if has_add:
                prod = prod + add_ref[...].astype(F32)
            o_ref[...] = prod.astype(o_ref.dtype)
            return
        o_ref, acc = refs[-2], refs[-1]
        kk = pl.program_id(2)

        @pl.when(kk == 0)
        def _():
            acc[...] = prod

        @pl.when(kk > 0)
        def _():
            acc[...] += prod

        @pl.when(kk == nk - 1)
        def _():
            r = acc[...]
            if has_add:
                r = r + add_ref[...].astype(F32)
            o_ref[...] = r.astype(o_ref.dtype)

    return pl.pallas_call(
        body, name=name, grid=(m // tm, n // tn, nk), in_specs=in_specs,
        out_specs=o_spec(tm, tn, lambda i, j, kk: (i, j)),
        out_shape=jax.ShapeDtypeStruct(o_shape, out_dtype),
        scratch_shapes=[pltpu.VMEM((tm, tn), F32)] if nk > 1 else [],
        compiler_params=_cparams(("parallel", "parallel", "arbitrary")),
    )(*args)


def rw(fn, ins, outs, name, rows, tr=256, consts=(), accs=()):
    n_in, n_c, n_o, n_a = len(ins), len(consts), len(outs), len(accs)
    in_specs = []
    for arr, off, width in ins:
        assert off % width == 0
        in_specs.append(pl.BlockSpec((tr, width), lambda i, o=off // width: (i, o)))
    for c in consts:
        in_specs.append(pl.BlockSpec(c.shape, lambda i: (0, 0)))
    out_specs = [pl.BlockSpec((tr, w), lambda i: (i, 0)) for w, _ in outs]
    out_specs += [pl.BlockSpec(s, lambda i: (0, 0)) for s in accs]
    out_shape = [jax.ShapeDtypeStruct((rows, w), dt) for w, dt in outs]
    out_shape += [jax.ShapeDtypeStruct(s, F32) for s in accs]

    def body(*refs):
        vals = [r[...] for r in refs[:n_in + n_c]]
        o_refs = refs[n_in + n_c:n_in + n_c + n_o]
        a_refs = refs[n_in + n_c + n_o:]
        res = fn(*vals)
        for r, v in zip(o_refs, res[:n_o]):
            r[...] = v.astype(r.dtype)
        if n_a:
            @pl.when(pl.program_id(0) == 0)
            def _():
                for r in a_refs:
                    r[...] = jnp.zeros_like(r)
            for r, v in zip(a_refs, res[n_o:]):
                r[...] += v

    res = pl.pallas_call(
        body, name=name, grid=(rows // tr,), in_specs=in_specs, out_specs=out_specs,
        out_shape=out_shape,
        compiler_params=_cparams(("arbitrary",) if n_a else ("parallel",)),
    )(*[a for a, _, _ in ins], *consts)
    return res


def _rstd(x):
    return lax.rsqrt(jnp.mean(x * x, axis=-1, keepdims=True) + EPS)


def rms_fwd(x, g, name):
    def fn(xv, gv):
        xv = xv.astype(F32)
        return (xv * _rstd(xv) * gv,)
    return rw(fn, [(x, 0, D)], [(D, BF16)], name, x.shape[0], consts=[g])[0]


def _rms_bwd_math(xv, dy, gv):
    r = _rstd(xv)
    dyg = dy * gv
    dx = r * dyg - xv * (r * r * r / D) * jnp.sum(dyg * xv, axis=-1, keepdims=True)
    dg = jnp.sum(dy * xv * r, axis=0, keepdims=True)
    return dx, dg


def rms_bwd(x, dy, dres, g, name):
    def fn(xv, dyv, drv, gv):
        dx, dg = _rms_bwd_math(xv, dyv, gv)
        return dx + drv, dg
    return rw(fn, [(x, 0, D), (dy, 0, D), (dres, 0, D)], [(D, F32)], name, x.shape[0],
              consts=[g], accs=[(1, D)])


def final_loss(x, tgt, g):
    def fn(xv, tv, gv):
        e = xv * _rstd(xv) * gv - tv
        loss = 0.5 * jnp.sum(jnp.sum(e * e, axis=-1, keepdims=True), axis=0, keepdims=True) / D
        dx, dg = _rms_bwd_math(xv, e / D, gv)
        return dx, loss, dg
    return rw(fn, [(x, 0, D), (tgt, 0, D)], [(D, F32)], "final_loss", S, consts=[g],
              accs=[(1, 1), (1, D)])


def _attn_bias(bias_ref):
    ii = lax.broadcasted_iota(jnp.int32, (2 * QB, 2 * QB), 0) % QB
    jj = lax.broadcasted_iota(jnp.int32, (2 * QB, 2 * QB), 1)
    dist = ii + QB - jj
    band = (dist >= 0) & (dist <= QB)
    bias_ref[1] = jnp.where(band, 0.0, NEG)
    bias_ref[0] = jnp.where(band & (jj >= QB), 0.0, NEG)


def _two_heads(x, m0):
    return jnp.concatenate([jnp.where(m0, x, 0.0), jnp.where(m0, 0.0, x)], axis=0)


def _per_head(col, m0):
    return jnp.where(m0, col[:QB], col[QB:])


def _attn_rows(idx, d):
    if d == 1:
        b = idx
        cur = pl.ds(pl.multiple_of(b * QB, QB), QB)
        prev = pl.ds(pl.multiple_of(jnp.maximum(b - 1, 0) * QB, QB), QB)
    else:
        r, b = lax.rem(idx, d), lax.div(idx, d)
        cur = pl.ds(r + b * (QB * d), QB, stride=d)
        prev = pl.ds(r + jnp.maximum(b - 1, 0) * (QB * d), QB, stride=d)
    return cur, prev, b


NBLK = S // QB
GROUP = 4


def _colblk(off):
    return pl.BlockSpec((S, 128), lambda hp: (0, off * 8 + hp))


def attn_fwd(z):
    def body(q_ref, k_ref, v_ref, g_ref, o_ref, l_ref, a_ref, os, ls, bias):
        _attn_bias(bias)
        m0 = lax.broadcasted_iota(jnp.int32, (1, 128), 1) < 64
        for pi, d in enumerate(PATTERNS):
            def load(idx, d=d):
                cur, prev, b = _attn_rows(idx, d)
                return cur, (q_ref[cur, :], k_ref[prev, :], k_ref[cur, :], v_ref[prev, :], v_ref[cur, :],
                             bias[jnp.minimum(b, 1)])

            def block(q, kp, kc, vp, vc, bs):
                qq = _two_heads(q * 0.125, m0).astype(BF16)
                k = jnp.concatenate([kp, kc], axis=0).astype(BF16)
                s = _nt(qq, k) + bs
                mx = jnp.max(s, axis=-1, keepdims=True)
                p = jnp.exp(s - mx)
                den = jnp.sum(p, axis=-1, keepdims=True)
                pb = p.astype(BF16)
                vv = _two_heads(jnp.concatenate([vp, vc], axis=0), m0).astype(BF16)
                o = _nn(jnp.concatenate([pb[:QB], pb[QB:]], axis=1), vv)
                return o * _per_head(1.0 / den, m0), _per_head(mx + jnp.log(den), m0)

            def step(i, carry, pi=pi):
                loaded = [load(i * GROUP + u) for u in range(GROUP)]
                done = [block(*vals) for _, vals in loaded]
                for (cur, _), (o, l) in zip(loaded, done):
                    os[pi, cur, :] = o
                    ls[pi, cur, :] = l
                return carry
            lax.fori_loop(0, NBLK // GROUP, step, 0)
        l1, l2, l3 = ls[0], ls[1], ls[2]
        mx = jnp.maximum(jnp.maximum(l1, l2), l3)
        e1, e2, e3 = jnp.exp(l1 - mx), jnp.exp(l2 - mx), jnp.exp(l3 - mx)
        tot = e1 + e2 + e3
        o = (os[0] * e1 + os[1] * e2 + os[2] * e3) / tot
        ga = g_ref[...]
        o_ref[...] = o
        l_ref[...] = mx + jnp.log(tot)
        a_ref[...] = (o * (ga * _sig(ga))).astype(a_ref.dtype)

    out = pl.BlockSpec((S, 128), lambda hp: (0, hp))
    return pl.pallas_call(
        body, name="attn_fwd", grid=(8,),
        in_specs=[_colblk(0), _colblk(1), _colblk(2), _colblk(3)], out_specs=[out] * 3,
        out_shape=[jax.ShapeDtypeStruct((S, D), F32), jax.ShapeDtypeStruct((S, D), F32),
                   jax.ShapeDtypeStruct((S, 2 * D), BF16)],
        scratch_shapes=[pltpu.VMEM((3, S, 128), F32), pltpu.VMEM((3, S, 128), F32),
                        pltpu.VMEM((2, 2 * QB, 2 * QB), F32)],
        compiler_params=_cparams(("parallel",)),
    )(z, z, z, z)


def attn_bwd(z, d_cat, o, lse):
    def body(q_ref, k_ref, v_ref, g_ref, da_ref, o_ref, l_ref, dq_ref, dk_ref, dv_ref, dg_ref, do_s, pr_s, bias):
        _attn_bias(bias)
        m0 = lax.broadcasted_iota(jnp.int32, (1, 128), 1) < 64
        ga = g_ref[...]
        sg = _sig(ga)
        da = da_ref[...]
        ov = o_ref[...]
        do = da * (ga * sg)
        dg_ref[...] = da * ov * (sg * (1.0 + ga * (1.0 - sg)))
        do_s[...] = do
        pr_s[...] = do * ov
        dq_ref[...] = jnp.zeros_like(dq_ref)
        dk_ref[...] = jnp.zeros_like(dk_ref)
        dv_ref[...] = jnp.zeros_like(dv_ref)
        for d in PATTERNS:
            def load(idx, d=d):
                cur, prev, b = _attn_rows(idx, d)
                return (cur, prev), (q_ref[cur, :], k_ref[prev, :], k_ref[cur, :], v_ref[prev, :], v_ref[cur, :],
                                     do_s[cur, :], pr_s[cur, :], l_ref[cur, :], bias[jnp.minimum(b, 1)])

            def block(q, kp, kc, vp, vc, dof, prod, lp, bs):
                qq = _two_heads(q * 0.125, m0).astype(BF16)
                kf = jnp.concatenate([kp, kc], axis=0)
                k = kf.astype(BF16)
                v = jnp.concatenate([vp, vc], axis=0).astype(BF16)
                dd = _two_heads(dof, m0).astype(BF16)
                lh = jnp.max(jnp.concatenate([jnp.where(m0, lp, -jnp.inf), jnp.where(m0, -jnp.inf, lp)], axis=0),
                             axis=-1, keepdims=True)
                delta = jnp.sum(_two_heads(prod, m0), axis=-1, keepdims=True)
                p = jnp.exp(_nt(qq, k) + bs - lh)
                ds = (p * (_nt(dd, v) - delta)).astype(BF16)
                dq = _nn(jnp.concatenate([ds[:QB], ds[QB:]], axis=1), _two_heads(kf, m0).astype(BF16))
                return dq * 0.125, _tn(ds, qq), _tn(p.astype(BF16), dd)

            def step(i, carry):
                loaded = [load(i * GROUP + u) for u in range(GROUP)]
                done = [block(*vals) for _, vals in loaded]
                for ((cur, prev), _), (dq, dk, dv) in zip(loaded, done):
                    dq_ref[cur, :] = dq_ref[cur, :] + dq
                    dk_ref[prev, :] = dk_ref[prev, :] + dk[:QB]
                    dv_ref[prev, :] = dv_ref[prev, :] + dv[:QB]
                    dk_ref[cur, :] = dk_ref[cur, :] + dk[QB:]
                    dv_ref[cur, :] = dv_ref[cur, :] + dv[QB:]
                return carry
            lax.fori_loop(0, NBLK // GROUP, step, 0)

    blk = pl.BlockSpec((S, 128), lambda hp: (0, hp))
    return pl.pallas_call(
        body, name="attn_bwd", grid=(8,),
        in_specs=[_colblk(0), _colblk(1), _colblk(2), _colblk(3), blk, blk, blk], out_specs=[blk] * 4,
        out_shape=[jax.ShapeDtypeStruct((S, D), F32)] * 4,
        scratch_shapes=[pltpu.VMEM((S, 128), F32), pltpu.VMEM((S, 128), F32), pltpu.VMEM((2, 2 * QB, 2 * QB), F32)],
        compiler_params=_cparams(("parallel",)),
    )(z, z, z, z, d_cat, o, lse)


def assemble_dz_even(parts):
    def body(*refs):
        o_ref = refs[-1]
        for j in range(6):
            o_ref[:, j * D:(j + 1) * D] = refs[j][...].astype(o_ref.dtype)
    tr = 256
    blk = pl.BlockSpec((tr, D), lambda i: (i, 0))
    return pl.pallas_call(
        body, name="assemble_dz_even", grid=(S // tr,), in_specs=[blk] * 6,
        out_specs=pl.BlockSpec((tr, 6 * D), lambda i: (i, 0)),
        out_shape=jax.ShapeDtypeStruct((S, 6 * D), BF16),
        compiler_params=_cparams(("parallel",)),
    )(*parts)


def _pool_window(g):
    return jnp.where(g == 0, 2.0, jnp.where(g == 1, 4.0, jnp.where(g == 2, 8.0, 16.0)))


def _pool_sel(g, levels):
    return jnp.where(g == 0, levels[0], jnp.where(g == 1, levels[1], jnp.where(g == 2, levels[2], levels[3])))


def _pool_fwd_math(v, g):
    t = lax.broadcasted_iota(jnp.int32, (S, 1), 0)
    s = v
    levels = []
    for k in (1, 2, 4, 8):
        s = s + jnp.where(t >= k, pltpu.roll(s, k, 0), 0.0)
        levels.append(s)
    cnt = jnp.minimum((t + 1).astype(F32), _pool_window(g))
    return _pool_sel(g, levels) / cnt - v, cnt


def pool_fwd(z, pw, ps, cat):
    def body(v_ref, g_ref, pw_ref, ps_ref, cat_ref, o_ref):
        g = pl.program_id(0)
        pooled, _ = _pool_fwd_math(v_ref[...], g)
        mixed = _nn(pooled.astype(BF16), pw_ref[...].astype(BF16))
        gb = g_ref[...]
        o_ref[...] = (mixed * ps_ref[...] * (gb * _sig(gb))).astype(o_ref.dtype)

    return pl.pallas_call(
        body, name="pool_fwd", grid=(4,),
        in_specs=[pl.BlockSpec((S, 256), lambda g: (0, 16 + g)),
                  pl.BlockSpec((S, 256), lambda g: (0, 20 + g)),
                  pl.BlockSpec((None, 256, 256), lambda g: (g, 0, 0)),
                  pl.BlockSpec((1, 256), lambda g: (0, g)), pl.BlockSpec(memory_space=pl.ANY)],
        out_specs=pl.BlockSpec((S, 256), lambda g: (0, 4 + g)),
        out_shape=jax.ShapeDtypeStruct((S, 2 * D), BF16),
        input_output_aliases={4: 0},
        compiler_params=_cparams(("parallel",)),
    )(z, z, pw, ps, cat)


def pool_bwd(z, d_cat, pw, ps):
    def body(v_ref, g_ref, d_ref, pw_ref, ps_ref, dv_ref, dg_ref, dpw_ref, dps_ref):
        g = pl.program_id(0)
        v = v_ref[...]
        pooled, cnt = _pool_fwd_math(v, g)
        pwb = pw_ref[...].astype(BF16)
        pb = pooled.astype(BF16)
        mixed = _nn(pb, pwb)
        gb = g_ref[...]
        sg = _sig(gb)
        dout = d_ref[...]
        sc = ps_ref[...]
        dg_ref[...] = dout * mixed * sc * (sg * (1.0 + gb * (1.0 - sg)))
        dms = dout * (gb * sg)
        dps_ref[...] = jnp.sum(dms * mixed, axis=0, keepdims=True)
        dmx = (dms * sc).astype(BF16)
        dpw_ref[...] = _tn(pb, dmx)
        dpooled = _nt(dmx, pwb)
        t = lax.broadcasted_iota(jnp.int32, (S, 1), 0)
        s = dpooled / cnt
        levels = []
        for k in (1, 2, 4, 8):
            s = s + jnp.where(t < S - k, pltpu.roll(s, S - k, 0), 0.0)
            levels.append(s)
        dv_ref[...] = _pool_sel(g, levels) - dpooled

    return pl.pallas_call(
        body, name="pool_bwd", grid=(4,),
        in_specs=[pl.BlockSpec((S, 256), lambda g: (0, 16 + g)),
                  pl.BlockSpec((S, 256), lambda g: (0, 20 + g)),
                  pl.BlockSpec((S, 256), lambda g: (0, 4 + g)),
                  pl.BlockSpec((None, 256, 256), lambda g: (g, 0, 0)),
                  pl.BlockSpec((1, 256), lambda g: (0, g))],
        out_specs=[pl.BlockSpec((S, 256), lambda g: (0, g)),
                   pl.BlockSpec((S, 256), lambda g: (0, g)),
                   pl.BlockSpec((None, 256, 256), lambda g: (g, 0, 0)),
                   pl.BlockSpec((1, 256), lambda g: (0, g))],
        out_shape=[jax.ShapeDtypeStruct((S, D), F32), jax.ShapeDtypeStruct((S, D), F32),
                   jax.ShapeDtypeStruct((4, 256, 256), F32), jax.ShapeDtypeStruct((1, D), F32)],
        compiler_params=_cparams(("parallel",)),
    )(z, z, d_cat, pw, ps)


CH = 128


def _sgu_common(v, lng, lnb, w_ref):
    mu = jnp.mean(v, axis=-1, keepdims=True)
    vc = v - mu
    rs = lax.rsqrt(jnp.mean(vc * vc, axis=-1, keepdims=True) + EPS)
    xhat = vc * rs
    vn = (xhat * lng + lnb).astype(BF16)
    ri = lax.broadcasted_iota(jnp.int32, (CH, CH), 0)
    ci = lax.broadcasted_iota(jnp.int32, (CH, CH), 1)
    tril = ri >= ci
    ws = [jnp.where(tril, w_ref[g], 0.0).astype(BF16) for g in range(4)]
    return xhat, rs, vn, tril, ws


def _zspec(off):
    return pl.BlockSpec((CH, D), lambda c: (c, off))


def _full(shape):
    return pl.BlockSpec(shape, lambda c: (0,) * len(shape))


def sgu_fwd(z, lng, lnb, w, bfull):
    def body(u_ref, v_ref, g_ref, lng_ref, lnb_ref, w_ref, b_ref, o_ref):
        _, _, vn, _, ws = _sgu_common(v_ref[...], lng_ref[...], lnb_ref[...], w_ref)
        for g in range(4):
            sl = slice(g * 256, (g + 1) * 256)
            mixed = _nn(ws[g], vn[:, sl]) + b_ref[:, sl]
            gc = g_ref[:, sl]
            o_ref[:, sl] = (u_ref[:, sl] * mixed * (gc * _sig(gc))).astype(o_ref.dtype)

    return pl.pallas_call(
        body, name="sgu_fwd", grid=(S // CH,),
        in_specs=[_zspec(0), _zspec(1), _zspec(2), _full((1, D)), _full((1, D)),
                  _full((4, CH, CH)), _full((CH, D))],
        out_specs=pl.BlockSpec((CH, D), lambda c: (c, 0)),
        out_shape=jax.ShapeDtypeStruct((S, D), BF16),
        compiler_params=_cparams(("parallel",)),
    )(z, z, z, lng, lnb, w, bfull)


def sgu_bwd(z, d_cat, lng, lnb, w, bfull):
    def body(u_ref, v_ref, g_ref, d_ref, lng_ref, lnb_ref, w_ref, b_ref,
             du_ref, dv_ref, dg_ref, dw_ref, db_ref, dlg_ref, dlb_ref):
        @pl.when(pl.program_id(0) == 0)
        def _():
            dw_ref[...] = jnp.zeros_like(dw_ref)
            db_ref[...] = jnp.zeros_like(db_ref)
            dlg_ref[...] = jnp.zeros_like(dlg_ref)
            dlb_ref[...] = jnp.zeros_like(dlb_ref)

        lng = lng_ref[...]
        xhat, rs, vn, tril, ws = _sgu_common(v_ref[...], lng, lnb_ref[...], w_ref)
        lane = lax.broadcasted_iota(jnp.int32, (1, 128), 1)
        db = jnp.zeros((CH, 128), F32)
        dvn_parts = []
        for g in range(4):
            sl = slice(g * 256, (g + 1) * 256)
            mixed = _nn(ws[g], vn[:, sl]) + b_ref[:, sl]
            gc = g_ref[:, sl]
            sg = _sig(gc)
            u = u_ref[:, sl]
            dc = d_ref[:, sl]
            du_ref[:, sl] = dc * mixed * (gc * sg)
            dg_ref[:, sl] = dc * u * mixed * (sg * (1.0 + gc * (1.0 - sg)))
            dmx = dc * u * (gc * sg)
            db = db + jnp.where(lane == g, jnp.sum(dmx, axis=-1, keepdims=True), 0.0)
            dmb = dmx.astype(BF16)
            dw_ref[g] += jnp.where(tril, _nt(dmb, vn[:, sl]), 0.0)
            dvn_parts.append(_tn(ws[g], dmb))
        db_ref[...] += db
        dvn = jnp.concatenate(dvn_parts, axis=1)
        dlb_ref[...] += jnp.sum(dvn, axis=0, keepdims=True)
        dlg_ref[...] += jnp.sum(dvn * xhat, axis=0, keepdims=True)
        dxh = dvn * lng
        dv_ref[...] = rs * (dxh - jnp.mean(dxh, axis=-1, keepdims=True)
                            - xhat * jnp.mean(dxh * xhat, axis=-1, keepdims=True))

    row = pl.BlockSpec((CH, D), lambda c: (c, 0))
    return pl.pallas_call(
        body, name="sgu_bwd", grid=(S // CH,),
        in_specs=[_zspec(0), _zspec(1), _zspec(2), row, _full((1, D)), _full((1, D)),
                  _full((4, CH, CH)), _full((CH, D))],
        out_specs=[row, row, row, _full((4, CH, CH)), _full((CH, 128)), _full((1, D)), _full((1, D))],
        out_shape=[jax.ShapeDtypeStruct((S, D), F32)] * 3
        + [jax.ShapeDtypeStruct((4, CH, CH), F32), jax.ShapeDtypeStruct((CH, 128), F32),
           jax.ShapeDtypeStruct((1, D), F32), jax.ShapeDtypeStruct((1, D), F32)],
        compiler_params=_cparams(("arbitrary",)),
    )(z, z, z, d_cat, lng, lnb, w, bfull)


TB = 256


def _cmul(ar, ai, br, bi):
    return ar * br - ai * bi, ar * bi + ai * br


def _scan_consts(ar, ai, reverse):
    a2 = _cmul(ar, ai, ar, ai)
    a4 = _cmul(*a2, *a2)
    row = lax.broadcasted_iota(jnp.int32, (8, NS), 0)
    pr = jnp.zeros((8, NS), F32)
    pi = jnp.zeros((8, NS), F32)
    cr, ci = ar, ai
    for r in range(8):
        sel = row == (7 - r if reverse else r)
        pr = jnp.where(sel, cr, pr)
        pi = jnp.where(sel, ci, pi)
        cr, ci = _cmul(cr, ci, ar, ai)
    return ((ar, ai), a2, a4), (pr, pi), row


def scan_fwd(bu, abr, abi):
    def body(bu_ref, ar_ref, ai_ref, h_ref, car, cai):
        @pl.when(pl.program_id(0) == 0)
        def _():
            car[...] = jnp.zeros_like(car)
            cai[...] = jnp.zeros_like(cai)

        pows, (pr, pi), row = _scan_consts(ar_ref[...], ai_ref[...], False)

        def tile(t, carry):
            c_r, c_i = carry
            rows = pl.ds(pl.multiple_of(t * 8, 8), 8)
            xr = bu_ref[rows, 0:NS]
            xi = bu_ref[rows, NS:2 * NS]
            for k, (kr, ki) in zip((1, 2, 4), pows):
                sr = jnp.where(row >= k, pltpu.roll(xr, k, 0), 0.0)
                si = jnp.where(row >= k, pltpu.roll(xi, k, 0), 0.0)
                xr, xi = xr + kr * sr - ki * si, xi + kr * si + ki * sr
            xr, xi = xr + pr * c_r - pi * c_i, xi + pr * c_i + pi * c_r
            h_ref[rows, 0:NS] = xr
            h_ref[rows, NS:2 * NS] = xi
            return (jnp.broadcast_to(xr[7:8, :], (8, NS)), jnp.broadcast_to(xi[7:8, :], (8, NS)))

        c_r, c_i = lax.fori_loop(0, TB // 8, tile, (car[...], cai[...]))
        car[...] = c_r
        cai[...] = c_i

    return pl.pallas_call(
        body, name="s5_scan_fwd", grid=(S // TB,),
        in_specs=[pl.BlockSpec((TB, 2 * NS), lambda i: (i, 0)),
                  pl.BlockSpec((1, NS), lambda i: (0, 0)), pl.BlockSpec((1, NS), lambda i: (0, 0))],
        out_specs=pl.BlockSpec((TB, 2 * NS), lambda i: (i, 0)),
        out_shape=jax.ShapeDtypeStruct((S, 2 * NS), F32),
        scratch_shapes=[pltpu.VMEM((8, NS), F32), pltpu.VMEM((8, NS), F32)],
        compiler_params=_cparams(("arbitrary",)),
    )(bu, abr, abi)


def scan_bwd(eta, h, abr, abi):
    nt = S // TB

    def body(e_ref, h_ref, ar_ref, ai_ref, l_ref, da_ref, car, cai):
        @pl.when(pl.program_id(0) == 0)
        def _():
            car[...] = jnp.zeros_like(car)
            cai[...] = jnp.zeros_like(cai)
            da_ref[...] = jnp.zeros_like(da_ref)

        pows, (pr, pi), row = _scan_consts(ar_ref[...], -ai_ref[...], True)

        def tile(tt, carry):
            c_r, c_i, acr, aci = carry
            t = TB // 8 - 1 - tt
            rows = pl.ds(pl.multiple_of(t * 8, 8), 8)
            xr = e_ref[rows, 0:NS]
            xi = e_ref[rows, NS:2 * NS]
            for k, (kr, ki) in zip((1, 2, 4), pows):
                sr = jnp.where(row < 8 - k, pltpu.roll(xr, 8 - k, 0), 0.0)
                si = jnp.where(row < 8 - k, pltpu.roll(xi, 8 - k, 0), 0.0)
                xr, xi = xr + kr * sr - ki * si, xi + kr * si + ki * sr
            xr, xi = xr + pr * c_r - pi * c_i, xi + pr * c_i + pi * c_r
            l_ref[rows, 0:NS] = xr
            l_ref[rows, NS:2 * NS] = xi
            nr = jnp.where(row < 7, pltpu.roll(xr, 7, 0), c_r)
            ni = jnp.where(row < 7, pltpu.roll(xi, 7, 0), c_i)
            hr = h_ref[rows, 0:NS]
            hi = h_ref[rows, NS:2 * NS]
            acr = acr + hr * nr + hi * ni
            aci = aci + hr * ni - hi * nr
            return (jnp.broadcast_to(xr[0:1, :], (8, NS)), jnp.broadcast_to(xi[0:1, :], (8, NS)), acr, aci)

        zero = jnp.zeros((8, NS), F32)
        c_r, c_i, acr, aci = lax.fori_loop(0, TB // 8, tile, (car[...], cai[...], zero, zero))
        car[...] = c_r
        cai[...] = c_i
        da_ref[:, 0:NS] += acr
        da_ref[:, NS:2 * NS] += aci

    rev = pl.BlockSpec((TB, 2 * NS), lambda i: (nt - 1 - i, 0))
    return pl.pallas_call(
        body, name="s5_scan_bwd", grid=(nt,),
        in_specs=[rev, rev, pl.BlockSpec((1, NS), lambda i: (0, 0)), pl.BlockSpec((1, NS), lambda i: (0, 0))],
        out_specs=[rev, pl.BlockSpec((8, 2 * NS), lambda i: (0, 0))],
        out_shape=[jax.ShapeDtypeStruct((S, 2 * NS), F32), jax.ShapeDtypeStruct((8, 2 * NS), F32)],
        scratch_shapes=[pltpu.VMEM((8, NS), F32), pltpu.VMEM((8, NS), F32)],
        compiler_params=_cparams(("arbitrary",)),
    )(eta, h, abr, abi)


GC = 0.7978845608028654
GA = 0.044715


def s5_post(hc, z, dskip):
    def fn(hv, xd, dv):
        y = hv + dv * xd
        return y, 0.5 * y * (1.0 + jnp.tanh(GC * (y + GA * y * y * y)))
    return rw(fn, [(hc, 0, 512), (z, 3072, 512)], [(512, F32), (512, BF16)], "s5_post", S, consts=[dskip])


def s5_post_bwd(dyg, ypre, z, dskip):
    def fn(dy, y, xd, dv):
        th = jnp.tanh(GC * (y + GA * y * y * y))
        dg = 0.5 * (1.0 + th) + 0.5 * y * (1.0 - th * th) * GC * (1.0 + 3.0 * GA * y * y)
        dyp = dy * dg
        return dyp, dyp * dv, jnp.sum(dyp * xd, axis=0, keepdims=True)
    return rw(fn, [(dyg, 0, 512), (ypre, 0, 512), (z, 3072, 512)], [(512, BF16), (512, F32)],
              "s5_post_bwd", S, consts=[dskip], accs=[(1, 512)])


def glu_fwd(t, z, c_out):
    def fn(t1, t2, gd, co):
        return (jnp.concatenate([co, (t1 * _sig(t2) * (gd * _sig(gd))).astype(BF16)], axis=1),)
    return rw(fn, [(t, 0, 512), (t, 512, 512), (z, 3584, 512), (c_out, 0, D)], [(D + 512, BF16)], "glu_fwd", S)[0]


def glu_bwd(t, z, d_cat):
    def fn(t1, t2, gd, dd):
        s2, sg = _sig(t2), _sig(gd)
        sl = gd * sg
        return (jnp.concatenate([dd * s2 * sl, dd * t1 * s2 * (1.0 - s2) * sl], axis=1),
                dd * t1 * s2 * (sg * (1.0 + gd * (1.0 - sg))))
    return rw(fn, [(t, 0, 512), (t, 512, 512), (z, 3584, 512), (d_cat, 1024, 512)],
              [(D, BF16), (512, F32)], "glu_bwd", S)


def assemble_dz_odd(du, dv, dgc, dxd, dgd):
    def body(a, b, c, d, e, o_ref):
        o_ref[:, 0:D] = a[...].astype(BF16)
        o_ref[:, D:2 * D] = b[...].astype(BF16)
        o_ref[:, 2 * D:3 * D] = c[...].astype(BF16)
        o_ref[:, 3 * D:3 * D + 512] = d[...].astype(BF16)
        o_ref[:, 3 * D + 512:4 * D] = e[...].astype(BF16)
    tr = 256
    blk = pl.BlockSpec((tr, D), lambda i: (i, 0))
    half = pl.BlockSpec((tr, 512), lambda i: (i, 0))
    return pl.pallas_call(
        body, name="assemble_dz_odd", grid=(S // tr,), in_specs=[blk, blk, blk, half, half],
        out_specs=pl.BlockSpec((tr, 4 * D), lambda i: (i, 0)),
        out_shape=jax.ShapeDtypeStruct((S, 4 * D), BF16),
        compiler_params=_cparams(("parallel",)),
    )(du, dv, dgc, dxd, dgd)


TQ = 256


def _xattn_probs(qh, kh):
    s = _nt(qh, kh) * 0.0625
    p = jnp.exp(s - jnp.max(s, axis=-1, keepdims=True))
    return p / jnp.sum(p, axis=-1, keepdims=True)


def xattn_fwd(q, kv):
    def body(q_ref, kv_ref, o_ref):
        for h in range(4):
            sl = slice(h * 256, (h + 1) * 256)
            p = _xattn_probs(q_ref[:, sl].astype(BF16), kv_ref[:, sl].astype(BF16))
            vh = kv_ref[:, D + h * 256:D + (h + 1) * 256].astype(BF16)
            o_ref[:, sl] = _nn(p.astype(BF16), vh).astype(o_ref.dtype)

    return pl.pallas_call(
        body, name="xattn_fwd", grid=(S // TQ,),
        in_specs=[pl.BlockSpec((TQ, D), lambda i: (i, 0)), pl.BlockSpec((MEM, 2 * D), lambda i: (0, 0))],
        out_specs=pl.BlockSpec((TQ, D), lambda i: (i, 0)),
        out_shape=jax.ShapeDtypeStruct((S, D), BF16),
        compiler_params=_cparams(("parallel",)),
    )(q, kv)


def xattn_bwd(q, kv, d_o):
    def body(q_ref, kv_ref, do_ref, dq_ref, dkv_ref):
        @pl.when(pl.program_id(0) == 0)
        def _():
            dkv_ref[...] = jnp.zeros_like(dkv_ref)

        for h in range(4):
            sl = slice(h * 256, (h + 1) * 256)
            vs = slice(D + h * 256, D + (h + 1) * 256)
            qh = q_ref[:, sl].astype(BF16)
            kh = kv_ref[:, sl].astype(BF16)
            vh = kv_ref[:, vs].astype(BF16)
            doh = do_ref[:, sl].astype(BF16)
            p = _xattn_probs(qh, kh)
            dp = _nt(doh, vh)
            ds = (p * (dp - jnp.sum(p * dp, axis=-1, keepdims=True)) * 0.0625).astype(BF16)
            dq_ref[:, sl] = _nn(ds, kh).astype(dq_ref.dtype)
            dkv_ref[:, sl] += _tn(ds, qh)
            dkv_ref[:, vs] += _tn(p.astype(BF16), doh)

    return pl.pallas_call(
        body, name="xattn_bwd", grid=(S // TQ,),
        in_specs=[pl.BlockSpec((TQ, D), lambda i: (i, 0)), pl.BlockSpec((MEM, 2 * D), lambda i: (0, 0)),
                  pl.BlockSpec((TQ, D), lambda i: (i, 0))],
        out_specs=[pl.BlockSpec((TQ, D), lambda i: (i, 0)), pl.BlockSpec((MEM, 2 * D), lambda i: (0, 0))],
        out_shape=[jax.ShapeDtypeStruct((S, D), BF16), jax.ShapeDtypeStruct((MEM, 2 * D), F32)],
        compiler_params=_cparams(("arbitrary",)),
    )(q, kv, d_o)


def _s5_disc(a_re, a_im, log_dt, b_re, b_im):
    dt = jnp.exp(log_dt)[:, None]
    mag = jnp.exp(dt * a_re)
    abr = mag * jnp.cos(dt * a_im)
    abi = mag * jnp.sin(dt * a_im)
    nr, ni = abr - 1.0, abi
    inv = 1.0 / (a_re * a_re + a_im * a_im)
    cr = (nr * a_re + ni * a_im) * inv
    ci = (ni * a_re - nr * a_im) * inv
    bbr = cr[..., None] * b_re - ci[..., None] * b_im
    bbi = cr[..., None] * b_im + ci[..., None] * b_re
    return abr, abi, bbr, bbi


VM = pl.BlockSpec(memory_space=pltpu.VMEM)


def s5_embed(bt_re, bt_im, ct_re, ct_im):
    def body(br, bi, cr, ci, b_ref, c_ref):
        b_ref[...] = jnp.zeros_like(b_ref)
        c_ref[...] = jnp.zeros_like(c_ref)
        for g in range(NG):
            rows, cols = slice(g * NH, (g + 1) * NH), slice(g * NP, (g + 1) * NP)
            b_ref[rows, cols] = br[g]
            b_ref[rows, NS + g * NP:NS + (g + 1) * NP] = bi[g]
            c_ref[cols, rows] = cr[g]
            c_ref[NS + g * NP:NS + (g + 1) * NP, rows] = -ci[g]

    return pl.pallas_call(
        body, name="s5_embed", in_specs=[VM] * 4, out_specs=[VM] * 2,
        out_shape=[jax.ShapeDtypeStruct((NG * NH, 2 * NS), F32), jax.ShapeDtypeStruct((2 * NS, NG * NH), F32)],
        compiler_params=pltpu.CompilerParams(vmem_limit_bytes=VMEM_LIMIT),
    )(bt_re, bt_im, ct_re, ct_im)


def s5_extract(gb, gc):
    def body(gb_ref, gc_ref, br, bi, cr, ci):
        for g in range(NG):
            rows, cols = slice(g * NH, (g + 1) * NH), slice(g * NP, (g + 1) * NP)
            br[g] = gb_ref[rows, cols]
            bi[g] = gb_ref[rows, NS + g * NP:NS + (g + 1) * NP]
            cr[g] = gc_ref[cols, rows]
            ci[g] = -gc_ref[NS + g * NP:NS + (g + 1) * NP, rows]

    return pl.pallas_call(
        body, name="s5_extract", in_specs=[VM] * 2, out_specs=[VM] * 4,
        out_shape=[jax.ShapeDtypeStruct((NG, NH, NP), F32)] * 2 + [jax.ShapeDtypeStruct((NG, NP, NH), F32)] * 2,
        compiler_params=pltpu.CompilerParams(vmem_limit_bytes=VMEM_LIMIT),
    )(gb, gc)


def _fwd_even(i, x, P, W):
    hn = rms_fwd(x, P["norm_ab"][i:i + 1], "rms_ab_fwd")
    z = mm(m2(hn), W["w_in"], "nn", "in_ab")
    o, lse, cat = attn_fwd(z)
    if "more" in W:
        W.update(W.pop("more")(cat))
    cat = pool_fwd(z, W["pool_w"], P["pool_scale"][i:i + 1], cat)
    x_mid = mm(m2(cat), W["w_out"], "nn", "out_ab", add=m2(x))
    return x_mid, dict(x=x, hn=hn, z=z, o=o, lse=lse, cat=cat)


def _bwd_even(i, dx_mid, sv, P, W, G, GW):
    z = sv["z"]
    d_cat = mm(m2(dx_mid), W["w_out"], "nt", "out_ab_dx")
    GW["w_out"] = mm(m2(sv["cat"]), m2(dx_mid), "tn", "out_ab_dw").reshape(4, 512, D)
    dq, dk, dv, dga = attn_bwd(z, d_cat, sv["o"], sv["lse"])
    dvb, dgb, dpw, dps = pool_bwd(z, d_cat, W["pool_w"], P["pool_scale"][i:i + 1])
    GW["pool_w"] = dpw.reshape(4, 4, 64, 256).transpose(1, 0, 2, 3).reshape(4, 256, 256)
    G["pool_scale"][i] = dps[0]
    d_z = assemble_dz_even((dq, dk, dv, dga, dvb, dgb))
    d_hn = mm(m2(d_z), W["w_in"], "nt", "in_ab_dx")
    GW["w_in"] = mm(m2(sv["hn"]), m2(d_z), "tn", "in_ab_dw", out=outcs(D, 1536))
    return d_hn, P["norm_ab"][i:i + 1], "norm_ab", "rms_ab_bwd"


def _fwd_odd(i, x, P, W):
    hn = rms_fwd(x, P["norm_cd"][i:i + 1], "rms_cd_fwd")
    z = mm(m2(hn), W["w_in"], "nn", "in_cd")
    bfull = jnp.repeat(P["sgu_b"][i].T, 256, axis=1)
    c_out = sgu_fwd(z, P["sgu_ln_g"][i:i + 1], P["sgu_ln_b"][i:i + 1], P["sgu_w"][i], bfull)
    disc, disc_vjp = jax.vjp(_s5_disc, P["s5_a_re"][i], P["s5_a_im"][i], P["s5_log_dt"][i],
                             P["s5_b_re"][i], P["s5_b_im"][i])
    abr, abi, bbr, bbi = disc
    bbd, cbd = s5_embed(bbr.transpose(0, 2, 1), bbi.transpose(0, 2, 1),
                        P["s5_c_re"][i].transpose(0, 2, 1), P["s5_c_im"][i].transpose(0, 2, 1))
    abr, abi = abr.reshape(1, NS), abi.reshape(1, NS)
    bu = mm(m2(z, 3072, 512), m2(bbd), "nn", "s5_bu")
    h = scan_fwd(bu, abr, abi)
    hc = mm(m2(h), m2(cbd), "nn", "s5_hc")
    dskip = P["s5_d"][i:i + 1]
    ypre, yg = s5_post(hc, z, dskip)
    if "more" in W:
        W.update(W.pop("more")(yg))
    w12 = W["w12"]
    t = mm(m2(yg), m2(w12), "nn", "glu_t")
    cat = glu_fwd(t, z, c_out)
    x_mid = mm(m2(cat), W["w_out"], "nn", "out_cd", add=m2(x))
    return x_mid, dict(x=x, hn=hn, z=z, bfull=bfull, disc_vjp=disc_vjp, bbd=bbd, cbd=cbd, abr=abr,
                       abi=abi, h=h, ypre=ypre, yg=yg, w12=w12, t=t, cat=cat, dskip=dskip)


def _bwd_odd(i, dx_mid, sv, P, W, G, GW):
    z = sv["z"]
    d_cat = mm(m2(dx_mid), W["w_out"], "nt", "out_cd_dx")
    GW["w_out"] = mm(m2(sv["cat"]), m2(dx_mid), "tn", "out_cd_dw").reshape(4, 384, D)
    du, dv, dgc, dws, dbs, dlg, dlb = sgu_bwd(z, d_cat, P["sgu_ln_g"][i:i + 1], P["sgu_ln_b"][i:i + 1],
                                               P["sgu_w"][i], sv["bfull"])
    G["sgu_w"][i], G["sgu_b"][i] = dws, dbs[:, :4].T
    G["sgu_ln_g"][i], G["sgu_ln_b"][i] = dlg[0], dlb[0]
    dt, dgd = glu_bwd(sv["t"], z, d_cat)
    gw12 = mm(m2(sv["yg"]), m2(dt), "tn", "glu_dw")
    GW["glu_w1"] = gw12[:, :512].reshape(4, 128, 512)
    GW["glu_w2"] = gw12[:, 512:].reshape(4, 128, 512)
    dyg = mm(m2(dt), m2(sv["w12"]), "nt", "glu_dx")
    dypre, dxd1, dd = s5_post_bwd(dyg, sv["ypre"], z, sv["dskip"])
    G["s5_d"][i] = dd[0]
    gcbd = mm(m2(sv["h"]), m2(dypre), "tn", "s5_dc")
    eta = mm(m2(dypre), m2(sv["cbd"]), "nt", "s5_eta")
    lam, dacc = scan_bwd(eta, sv["h"], sv["abr"], sv["abi"])
    gbbd = mm(m2(z, 3072, 512), m2(lam), "tn", "s5_db")
    dxd = mm(m2(lam), m2(sv["bbd"]), "nt", "s5_dx", add=m2(dxd1))
    dacc = jnp.sum(dacc, axis=0)
    dbt_re, dbt_im, dct_re, dct_im = s5_extract(gbbd, gcbd)
    G["s5_c_re"][i], G["s5_c_im"][i] = dct_re.transpose(0, 2, 1), dct_im.transpose(0, 2, 1)
    d_bbr, d_bbi = dbt_re.transpose(0, 2, 1), dbt_im.transpose(0, 2, 1)
    (G["s5_a_re"][i], G["s5_a_im"][i], G["s5_log_dt"][i], G["s5_b_re"][i], G["s5_b_im"][i]) = sv["disc_vjp"](
        (dacc[:NS].reshape(NG, NP), dacc[NS:].reshape(NG, NP), d_bbr, d_bbi))
    d_z = assemble_dz_odd(du, dv, dgc, dxd, dgd)
    d_hn = mm(m2(d_z), W["w_in"], "nt", "in_cd_dx")
    GW["w_in"] = mm(m2(sv["hn"]), m2(d_z), "tn", "in_cd_dw", out=outcs(D, 1024))
    return d_hn, P["norm_cd"][i:i + 1], "norm_cd", "rms_cd_bwd"


def _fwd_x(l, x, mem_n, P, W):
    hx = rms_fwd(x, P["norm_x"][l:l + 1], "rms_x_fwd")
    q = mm(m2(hx), W["w_xq"], "nn", "xq", out_dtype=BF16)
    kv = mm(m2(mem_n), W["w_xkv"], "nn", "xkv", out_dtype=BF16)
    ox = xattn_fwd(q, kv)
    x_out = mm(m2(ox), W["w_xo"], "nn", "xo", add=m2(x))
    return x_out, dict(x=x, hx=hx, q=q, kv=kv, ox=ox)


def _bwd_x(l, dx_out, sv, mem_n, d_memn, P, W, G, GW):
    d_ox = mm(m2(dx_out), W["w_xo"], "nt", "xo_dx", out_dtype=BF16)
    GW["w_xo"] = mm(m2(sv["ox"]), m2(dx_out), "tn", "xo_dw").reshape(4, 256, D)
    dq, dkv = xattn_bwd(sv["q"], sv["kv"], d_ox)
    GW["w_xq"] = mm(m2(sv["hx"]), m2(dq), "tn", "xq_dw").reshape(4, 256, D)
    d_hx = mm(m2(dq), W["w_xq"], "nt", "xq_dx")
    GW["w_xkv"] = mm(m2(mem_n), m2(dkv), "tn", "xkv_dw", out=outcs(D, 512))
    d_memn = mm(m2(dkv), W["w_xkv"], "nt", "xkv_dx", add=None if d_memn is None else m2(d_memn))
    dx, dg = rms_bwd(sv["x"], d_hx, dx_out, P["norm_x"][l:l + 1], "rms_x_bwd")
    G["norm_x"][l] = dg[0]
    return dx, d_memn


SMALL_LAYERS = (("norm_ab", 2), ("pool_scale", 2), ("norm_cd", 2), ("sgu_ln_g", 2), ("sgu_ln_b", 2), ("sgu_w", 2),
                ("sgu_b", 2), ("s5_a_re", 2), ("s5_a_im", 2), ("s5_log_dt", 2), ("s5_b_re", 2), ("s5_b_im", 2),
                ("s5_c_re", 2), ("s5_c_im", 2), ("s5_d", 2), ("norm_x", 4))


def local_step(x, mem, tgt, P, weights_of, grads_done):
    G = {k: [None] * n for k, n in SMALL_LAYERS}
    mem_g = P["mem_norm"].reshape(1, D)
    mem_n = rms_fwd(mem, mem_g, "rms_mem_fwd")
    saved = []
    for layer in range(4):
        i = layer // 2
        W = weights_of(layer, x)
        x, sv_m = (_fwd_even if layer % 2 == 0 else _fwd_odd)(i, x, P, W)
        x, sv_x = _fwd_x(layer, x, mem_n, P, W)
        saved.append((sv_m, sv_x, W))
    dx, loss, dgf = final_loss(x, tgt, P["final_norm"].reshape(1, D))
    G["final_norm"] = dgf[0]
    d_memn = None
    for layer in reversed(range(4)):
        i = layer // 2
        sv_m, sv_x, W = saved[layer]
        GW = {}
        dx_mid, d_memn = _bwd_x(layer, dx, sv_x, mem_n, d_memn, P, W, G, GW)
        d_hn, g, key, name = (_bwd_even if layer % 2 == 0 else _bwd_odd)(i, dx_mid, sv_m, P, W, G, GW)
        token = grads_done(layer, GW)
        if token is not None:
            g = g + token
        dx, dg = rms_bwd(sv_m["x"], d_hn, dx_mid, g, name)
        G[key][i] = dg[0]
    _, dgm = rms_bwd(mem, d_memn, d_memn, mem_g, "rms_mem_bwd")
    G["mem_norm"] = dgm[0]
    return loss, dx, G


ANY = pl.BlockSpec(memory_space=pl.ANY)


def _place():
    x, y, c = lax.axis_index("x"), lax.axis_index("y"), lax.axis_index("c")
    chips = [(1 - x, y), (x, 1 - y), (1 - x, 1 - y)]
    return x, y, c, 2 * x + y, (x, y, 1 - c), chips


def _remote(src, dst, send, recv, k, dev):
    return pltpu.make_async_remote_copy(src_ref=src, dst_ref=dst, send_sem=send.at[k], recv_sem=recv.at[k],
                                        device_id=dev, device_id_type=MESHID)


HBM = pl.BlockSpec(memory_space=pltpu.HBM)
SEM = pl.BlockSpec(memory_space=pltpu.SEMAPHORE)
EFFECT = pltpu.SideEffectType.DATAFLOW_SIDE_EFFECTING


def _hbm(t):
    return pltpu.with_memory_space_constraint(t, pltpu.HBM)


def allgather_sync(shards):
    n = len(shards)

    def body(*refs):
        ins, outs = refs[:n], refs[n:2 * n]
        token, send, recv = refs[2 * n:]
        x, y, c, jme, sib, chips = _place()
        first, passed = [], []
        for a in range(n):
            cp = _remote(ins[a], outs[a].at[jme], send, recv, a * 7 + 6, sib)
            cp.start()
            first.append(cp)
            for k, chip in enumerate(chips):
                cp = _remote(ins[a].at[c], outs[a].at[jme, c], send, recv, a * 7 + k, (*chip, c))
                cp.start()
                first.append(cp)
        for a in range(n):
            for k, chip in enumerate(chips):
                piece = outs[a].at[2 * chip[0] + chip[1], c]
                _remote(piece, piece, send, recv, a * 7 + k, (*chip, c)).wait_recv()
                fw = _remote(piece, piece, send, recv, a * 7 + 3 + k, sib)
                fw.start()
                passed.append(fw)
        for a in range(n):
            own = outs[a].at[jme]
            _remote(own, own, send, recv, a * 7 + 6, sib).wait_recv()
            for k, chip in enumerate(chips):
                piece = outs[a].at[2 * chip[0] + chip[1], 1 - c]
                _remote(piece, piece, send, recv, a * 7 + 3 + k, sib).wait_recv()
        for cp in first + passed:
            cp.wait_send()
        token[...] = jnp.zeros_like(token)

    res = pl.pallas_call(
        body, name="allgather_sync", in_specs=[ANY] * n,
        out_specs=[ANY] * n + [pl.BlockSpec(memory_space=pltpu.VMEM)],
        out_shape=[jax.ShapeDtypeStruct((4,) + s.shape, s.dtype) for s in shards] + [jax.ShapeDtypeStruct((8, 128), F32)],
        scratch_shapes=[pltpu.SemaphoreType.DMA((7 * n,)), pltpu.SemaphoreType.DMA((7 * n,))],
    )(*shards)
    return list(res[:n]), res[n]


def _gather_copies(ins, lands, send, recv):
    x, y, c, jme, sib, chips = _place()
    devs = [(*chip, c) for chip in chips] + [sib]
    return [_remote(ins[a], lands[a].at[jme], send, recv, a * 4 + k, dev)
            for a in range(len(ins)) for k, dev in enumerate(devs)]


def allgather_start(shards, after, name):
    n, na = len(shards), len(after)

    def body(*refs):
        ins, lands = refs[:n], refs[n:2 * n]
        send, recv = refs[2 * n + na], refs[2 * n + na + 1]
        token = refs[-1]
        for cp in _gather_copies(ins, lands, send, recv):
            cp.start()
        token[...] = jnp.zeros_like(token)

    res = pl.pallas_call(
        body, name=name,
        out_shape=(pltpu.SemaphoreType.DMA((4 * n,)), pltpu.SemaphoreType.DMA((4 * n,)),
                   *[pltpu.HBM(s.shape, s.dtype) for s in shards],
                   *[pltpu.HBM((4,) + s.shape, s.dtype) for s in shards],
                   jax.ShapeDtypeStruct((8, 128), F32)),
        in_specs=[HBM] * (2 * n) + [ANY] * na,
        out_specs=(SEM, SEM, *[HBM] * (2 * n), pl.BlockSpec(memory_space=pltpu.VMEM)),
        input_output_aliases={a: 2 + a for a in range(2 * n)},
        compiler_params=pltpu.CompilerParams(has_side_effects=EFFECT),
    )(*[_hbm(s) for s in shards], *[_hbm(lax.empty((4,) + s.shape, s.dtype)) for s in shards], *after)
    return res[0], res[1], list(res[2:2 + n]), list(res[2 + n:2 + 2 * n]), res[-1]


def allgather_wait(send, recv, shards, lands, after, name):
    n = len(shards)

    def body(*refs):
        ins, zones = refs[:n], refs[n:2 * n]
        send_r, recv_r = refs[2 * n], refs[2 * n + 1]
        x, y, c, jme, sib, chips = _place()
        slots = [2 * chip[0] + chip[1] for chip in chips] + [jme]
        for a in range(n):
            for k, slot in enumerate(slots):
                cp = _remote(ins[a], zones[a].at[slot], send_r, recv_r, a * 4 + k, sib)
                cp.wait_send()
                cp.wait_recv()

    res = pl.pallas_call(
        body, name=name,
        out_shape=tuple(pltpu.HBM(t.shape, t.dtype) for t in list(shards) + list(lands)),
        in_specs=[HBM] * (2 * n) + [SEM, SEM, ANY], out_specs=tuple([HBM] * (2 * n)),
        input_output_aliases={a: a for a in range(2 * n)},
        compiler_params=pltpu.CompilerParams(has_side_effects=EFFECT),
    )(*shards, *lands, send, recv, after)
    return list(res[n:])


def allgather_small(slab):
    def body(in_ref, out_ref, send, recv, lsem):
        x, y, c, jme, sib, chips = _place()
        loc = pltpu.make_async_copy(in_ref, out_ref.at[jme], lsem.at[0])
        loc.start()
        cps = [_remote(in_ref, out_ref.at[jme], send, recv, k, (*chip, c)) for k, chip in enumerate(chips)]
        for cp in cps:
            cp.start()
        for k, chip in enumerate(chips):
            piece = out_ref.at[2 * chip[0] + chip[1]]
            _remote(piece, piece, send, recv, k, (*chip, c)).wait_recv()
        for cp in cps:
            cp.wait_send()
        loc.wait()

    return pl.pallas_call(
        body, name="allgather_small", in_specs=[ANY], out_specs=ANY,
        out_shape=jax.ShapeDtypeStruct((4,) + slab.shape, slab.dtype),
        scratch_shapes=[pltpu.SemaphoreType.DMA((3,)), pltpu.SemaphoreType.DMA((3,)), pltpu.SemaphoreType.DMA((1,))],
    )(slab)


def allreduce_small(v):
    def body(v_ref, o_ref, r0, r1, r2, send, recv):
        x, y, c, jme, sib, chips = _place()
        peers = [sib, (1 - x, y, c), (x, 1 - y, c)]
        o_ref[...] = v_ref[...]
        for k, buf in enumerate((r0, r1, r2)):
            cp = _remote(o_ref, buf, send, recv, k, peers[k])
            cp.start()
            cp.wait()
            o_ref[...] = o_ref[...] + buf[...]

    vm = pl.BlockSpec(memory_space=pltpu.VMEM)
    return pl.pallas_call(
        body, name="allreduce_small", in_specs=[vm], out_specs=vm,
        out_shape=jax.ShapeDtypeStruct(v.shape, v.dtype),
        scratch_shapes=[pltpu.VMEM(v.shape, v.dtype)] * 3 + [pltpu.SemaphoreType.DMA((3,)), pltpu.SemaphoreType.DMA((3,))],
        compiler_params=pltpu.CompilerParams(vmem_limit_bytes=VMEM_LIMIT),
    )(v)


def _pair_copies(gs, lands, send, recv):
    x, y, c, jme, sib, chips = _place()
    return [_remote(gs[a].at[:, 1 - c], lands[a], send, recv, a, sib) for a in range(len(gs))]


def rs_pair_start(gs, name):
    n = len(gs)

    def body(*refs):
        ins, lands = refs[:n], refs[n:2 * n]
        send, recv = refs[2 * n], refs[2 * n + 1]
        token = refs[-1]
        for cp in _pair_copies(ins, lands, send, recv):
            cp.start()
        token[...] = jnp.zeros_like(token)

    shapes = [(4,) + g.shape[2:] for g in gs]
    res = pl.pallas_call(
        body, name=name,
        out_shape=(pltpu.SemaphoreType.DMA((n,)), pltpu.SemaphoreType.DMA((n,)),
                   *[pltpu.HBM(g.shape, g.dtype) for g in gs], *[pltpu.HBM(s, F32) for s in shapes],
                   jax.ShapeDtypeStruct((8, 128), F32)),
        in_specs=[HBM] * (2 * n), out_specs=(SEM, SEM, *[HBM] * (2 * n), pl.BlockSpec(memory_space=pltpu.VMEM)),
        input_output_aliases={a: 2 + a for a in range(2 * n)},
        compiler_params=pltpu.CompilerParams(has_side_effects=EFFECT),
    )(*[_hbm(g) for g in gs], *[_hbm(lax.empty(s, F32)) for s in shapes])
    return res[0], res[1], list(res[2:2 + n]), list(res[2 + n:2 + 2 * n]), res[-1]


def rs_pair_wait(send, recv, gs, lands, after, name):
    n = len(gs)

    def body(*refs):
        ins, zones = refs[:n], refs[n:2 * n]
        for cp in _pair_copies(ins, zones, refs[2 * n], refs[2 * n + 1]):
            cp.wait_send()
            cp.wait_recv()

    res = pl.pallas_call(
        body, name=name,
        out_shape=tuple(pltpu.HBM(t.shape, t.dtype) for t in list(gs) + list(lands)),
        in_specs=[HBM] * (2 * n) + [SEM, SEM, ANY], out_specs=tuple([HBM] * (2 * n)),
        input_output_aliases={a: a for a in range(2 * n)},
        compiler_params=pltpu.CompilerParams(has_side_effects=EFFECT),
    )(*gs, *lands, send, recv, after)
    return list(res[:n]), list(res[n:])


def rs_pair_sum(g4, got, cidx):
    _, _, rh, cols = g4.shape
    tr = rh if rh <= 256 else 256

    def body(c_ref, a_ref, b_ref, o_ref):
        o_ref[...] = (a_ref[...] + b_ref[...]).astype(o_ref.dtype)

    return pl.pallas_call(
        body, name="rs_pair_sum",
        grid_spec=pltpu.PrefetchScalarGridSpec(
            num_scalar_prefetch=1, grid=(4, rh // tr),
            in_specs=[pl.BlockSpec((None, None, tr, cols), lambda j, t, cr: (j, cr[0], t, 0)),
                      pl.BlockSpec((None, tr, cols), lambda j, t, cr: (j, t, 0))],
            out_specs=pl.BlockSpec((None, tr, cols), lambda j, t, cr: (j, t, 0))),
        out_shape=jax.ShapeDtypeStruct((4, rh, cols), BF16),
        compiler_params=_cparams(("parallel", "parallel")),
    )(cidx, g4, got)


def _chip_copies(ps, lands, send, recv):
    x, y, c, jme, sib, chips = _place()
    return [_remote(ps[a].at[2 * chip[0] + chip[1]], lands[a].at[jme], send, recv, a * 3 + k, (*chip, c))
            for a in range(len(ps)) for k, chip in enumerate(chips)]


def rs_chip_start(ps, name):
    n = len(ps)

    def body(*refs):
        ins, lands = refs[:n], refs[n:2 * n]
        send, recv = refs[2 * n], refs[2 * n + 1]
        token = refs[-1]
        for cp in _chip_copies(ins, lands, send, recv):
            cp.start()
        token[...] = jnp.zeros_like(token)

    res = pl.pallas_call(
        body, name=name,
        out_shape=(pltpu.SemaphoreType.DMA((3 * n,)), pltpu.SemaphoreType.DMA((3 * n,)),
                   *[pltpu.HBM(p.shape, p.dtype) for p in ps], *[pltpu.HBM(p.shape, p.dtype) for p in ps],
                   jax.ShapeDtypeStruct((8, 128), F32)),
        in_specs=[HBM] * (2 * n), out_specs=(SEM, SEM, *[HBM] * (2 * n), pl.BlockSpec(memory_space=pltpu.VMEM)),
        input_output_aliases={a: 2 + a for a in range(2 * n)},
        compiler_params=pltpu.CompilerParams(has_side_effects=EFFECT),
    )(*[_hbm(p) for p in ps], *[_hbm(lax.empty(p.shape, p.dtype)) for p in ps])
    return res[0], res[1], list(res[2:2 + n]), list(res[2 + n:2 + 2 * n]), res[-1]


def rs_chip_wait(send, recv, ps, lands, after, name):
    n = len(ps)

    def body(*refs):
        ins, zones = refs[:n], refs[n:2 * n]
        send_r, recv_r = refs[2 * n], refs[2 * n + 1]
        x, y, c, jme, sib, chips = _place()
        for a in range(n):
            for k, chip in enumerate(chips):
                jt = 2 * chip[0] + chip[1]
                cp = _remote(ins[a].at[jt], zones[a].at[jt], send_r, recv_r, a * 3 + k, (*chip, c))
                cp.wait_send()
                cp.wait_recv()

    res = pl.pallas_call(
        body, name=name,
        out_shape=tuple(pltpu.HBM(p.shape, p.dtype) for p in list(ps) + list(lands)),
        in_specs=[HBM] * (2 * n) + [SEM, SEM] + [ANY] * len(after), out_specs=tuple([HBM] * (2 * n)),
        input_output_aliases={a: a for a in range(2 * n)},
        compiler_params=pltpu.CompilerParams(has_side_effects=EFFECT),
    )(*ps, *lands, send, recv, *after)
    return list(res[n:])


def rs_chip_sum(q, p, l, acc, layers, jc):
    _, rh, cols = q.shape
    tr = rh if rh <= 256 else 256

    def body(jc_ref, q_ref, p_ref, *rest):
        o_ref = rest[-1]
        jme = jc_ref[0]
        own = p_ref[...].astype(F32)
        v = [jnp.where(jme == j, own, q_ref[j].astype(F32)) for j in range(4)]
        o_ref[...] = ((v[0] + v[1]) + v[2]) + v[3]

    in_specs = [pl.BlockSpec((4, tr, cols), lambda t, jr: (0, t, 0)),
                pl.BlockSpec((None, tr, cols), lambda t, jr: (jr[0], t, 0))]
    args = [jc, q, p]
    if acc is not None:
        in_specs.append(ANY)
        args.append(acc)
    return pl.pallas_call(
        body, name="rs_chip_sum",
        grid_spec=pltpu.PrefetchScalarGridSpec(
            num_scalar_prefetch=1, grid=(rh // tr,), in_specs=in_specs,
            out_specs=pl.BlockSpec((None, None, tr, cols), lambda t, jr: (l, jr[1], t, 0))),
        out_shape=jax.ShapeDtypeStruct((layers, 2, rh, cols), F32),
        input_output_aliases={} if acc is None else {3: 0},
        compiler_params=_cparams(("parallel",)),
    )(*args)


def rs_pair_gather(rs):
    n = len(rs)

    def body(*refs):
        outs = refs[n:2 * n]
        send, recv = refs[2 * n:]
        x, y, c, jme, sib, chips = _place()
        cps = [_remote(outs[a].at[:, c], outs[a].at[:, c], send, recv, a, sib) for a in range(n)]
        for cp in cps:
            cp.start()
        for a in range(n):
            slot = outs[a].at[:, 1 - c]
            _remote(slot, slot, send, recv, a, sib).wait_recv()
        for cp in cps:
            cp.wait_send()

    return pl.pallas_call(
        body, name="rs_pair_gather", in_specs=[ANY] * n, out_specs=[ANY] * n,
        out_shape=[jax.ShapeDtypeStruct(r.shape, r.dtype) for r in rs],
        input_output_aliases={a: a for a in range(n)},
        scratch_shapes=[pltpu.SemaphoreType.DMA((n,)), pltpu.SemaphoreType.DMA((n,))],
    )(*rs)


def _adamw_math(w, g, m, v):
    m = B1 * m + (1.0 - B1) * g
    v = B2 * v + (1.0 - B2) * (g * g)
    m_hat = m / (1.0 - B1 ** STEP)
    v_hat = v / (1.0 - B2 ** STEP)
    return -LR * (m_hat / (jnp.sqrt(v_hat) + AEPS) + WD * w), m, v


def adamw(w, g, m, v, name):
    rows, cols = w.shape
    tr = 256 if rows % 256 == 0 else rows
    return rw(_adamw_math, [(a, 0, cols) for a in (w, g, m, v)], [(cols, F32)] * 3, name, rows, tr=tr)


WEIGHTS = ["norm_ab", "w_in_ab", "pool_w", "pool_scale", "w_out_ab", "norm_cd", "w_in_cd", "sgu_ln_g", "sgu_ln_b",
           "sgu_w", "sgu_b", "s5_a_re", "s5_a_im", "s5_log_dt", "s5_b_re", "s5_b_im", "s5_c_re", "s5_c_im", "s5_d",
           "glu_w1", "glu_w2", "w_out_cd", "norm_x", "w_xq", "w_xkv", "w_xo", "mem_norm", "final_norm"]
INPUTS = ["x", "mem"] + WEIGHTS + ["loss_target"] + ["m_" + n for n in WEIGHTS] + ["v_" + n for n in WEIGHTS]
BIG = ["w_in_ab", "w_out_ab", "w_in_cd", "w_out_cd", "w_xq", "w_xkv", "w_xo", "glu_w1", "glu_w2", "pool_w"]
COL_SHARDED = ("w_in_ab", "w_in_cd", "w_xkv")
SMALL = [n for n in WEIGHTS if n not in BIG]
SMALL_SHARDED = {"norm_cd": 256, "sgu_ln_g": 256, "sgu_ln_b": 256, "s5_d": 128}
PACK = 256 * 128


def _pack(arrs):
    flat = jnp.concatenate([a.reshape(-1) for a in arrs])
    pad = (-flat.shape[0]) % PACK
    return jnp.concatenate([flat, jnp.zeros((pad,), flat.dtype)]).reshape(-1, 128)


def _unpack(packed, shapes):
    flat, out, off = packed.reshape(-1), [], 0
    for s in shapes:
        n = 1
        for d in s:
            n *= d
        out.append(flat[off:off + n].reshape(s))
        off += n
    return out


LAYER_KEYS = (("w_in", "w_out", "pool_w", "w_xq", "w_xkv", "w_xo"),
              ("w_in", "w_out", "glu_w1", "glu_w2", "w_xq", "w_xkv", "w_xo"))


def _weight_of(key, layer):
    if key in ("w_xq", "w_xkv", "w_xo"):
        return key, layer, 4
    kind = "ab" if layer % 2 == 0 else "cd"
    return {"w_in": "w_in_" + kind, "w_out": "w_out_" + kind}.get(key, key), layer // 2, 2


def kernel(*args):
    a = dict(zip(INPUTS, args))
    x_i, y_i, c_i = lax.axis_index("x"), lax.axis_index("y"), lax.axis_index("c")
    j = 2 * x_i + y_i

    slab = jnp.concatenate([a["norm_cd"], a["sgu_ln_g"], a["sgu_ln_b"],
                            jnp.pad(a["s5_d"], ((0, 0), (0, 128)))], axis=0)
    gslab = allgather_small(slab)
    P = {n: a[n] for n in SMALL}
    for k, n in enumerate(("norm_cd", "sgu_ln_g", "sgu_ln_b", "s5_d")):
        wd = SMALL_SHARDED[n]
        P[n] = gslab[:, 2 * k:2 * k + 2, :wd].transpose(1, 0, 2).reshape(2, 4 * wd)

    def shards_of(layer):
        keys = sorted(k for k in LAYER_KEYS[layer % 2])
        out = []
        for k in keys:
            n, l, _ = _weight_of(k, layer)
            out.append(a[n][l].reshape(-1, a[n].shape[-1]).astype(BF16))
        return keys, out

    keys0, sh0 = shards_of(0)
    first = keys0.index("w_in")
    g_in, token = allgather_sync([sh0[first].reshape(2, sh0[first].shape[0] // 2, sh0[first].shape[1])])
    w_in0 = g_in[0].reshape(4, -1, g_in[0].shape[-1])
    started = {}
    for layer in (0, 1, 2, 3):
        keys, sh = (keys0, sh0) if layer == 0 else shards_of(layer)
        rest = [(k, s) for k, s in zip(keys, sh) if k != "w_in"]
        parts = [("in", ["w_in"], [sh[keys.index("w_in")]])] * (layer > 0) + [("", *map(list, zip(*rest)))]
        for tag, pk, ps in parts:
            send, recv, ps, lands, token = allgather_start(ps, [token, gslab], "allgather_start_%d%s" % (layer, tag))
            started[(layer, tag)] = (pk, send, recv, ps, lands)
    P["norm_ab"] = P["norm_ab"] + token[0:1, 0:1]

    cidx = jnp.reshape(c_i, (1,)).astype(jnp.int32)
    jc = jnp.stack([j, c_i]).astype(jnp.int32)

    def views(g):
        W = {}
        for k, v in g.items():
            if k in ("w_in", "w_xkv"):
                W[k] = mcs(v)
            elif k == "pool_w":
                W[k] = v.reshape(4, 4, 64, 256).transpose(1, 0, 2, 3).reshape(4, 256, 256)
            elif k not in ("glu_w1", "glu_w2"):
                W[k] = m2(v.reshape(-1, v.shape[-1]))
        if "glu_w1" in g:
            W["w12"] = jnp.concatenate([g["glu_w1"].reshape(512, 512), g["glu_w2"].reshape(512, 512)], axis=1)
        return W

    def arrived(layer, tag, after):
        keys, send, recv, sh, lands = started[(layer, tag)]
        return views(dict(zip(keys, allgather_wait(send, recv, sh, lands, after, "allgather_wait_%d%s" % (layer, tag)))))

    def weights_of(layer, x_in):
        W = views({"w_in": w_in0}) if layer == 0 else arrived(layer, "in", x_in)
        W["more"] = lambda after: arrived(layer, "", after)
        return W

    halves, pending = {}, {}

    def finish_pair(layer, after):
        keys, send, recv, flat, lands = halves.pop(layer)
        flat, got = rs_pair_wait(send, recv, flat, lands, after, "rs_pair_wait_%d" % layer)
        pair = [rs_pair_sum(g4, r, cidx) for g4, r in zip(flat, got)]
        send, recv, pair, lands, token = rs_chip_start(pair, "rs_chip_start_%d" % layer)
        pending[layer] = (keys, send, recv, pair, lands)
        return token

    def grads_done(layer, GW):
        keys = sorted(GW)
        flat = [GW[k].reshape(4, 2, GW[k].shape[1] // 2, GW[k].shape[2]) for k in keys]
        send, recv, flat, lands, token = rs_pair_start(flat, "rs_pair_start_%d" % layer)
        halves[layer] = (keys, send, recv, flat, lands)
        if layer + 1 in halves:
            token = token + finish_pair(layer + 1, token)
        return token[0:1, 0:1]

    loss, dx, G = local_step(a["x"][0], a["mem"][0], a["loss_target"][0], P, weights_of, grads_done)
    loss = lax.psum(loss[0, 0], ("x", "y", "c"))
    finish_pair(0, dx)
    outs = {}

    def update_big(names, red):
        for n, g in zip(names, rs_pair_gather([red[n] for n in names])):
            shp = a[n].shape
            g2 = g.reshape(-1, shp[-1])
            upd = adamw(a[n].reshape(g2.shape), g2, a["m_" + n].reshape(g2.shape), a["v_" + n].reshape(g2.shape),
                        "adamw_" + n)
            outs[n] = tuple(t.reshape(shp) for t in (g2,) + tuple(upd))

    def reduce_layer(layer, red, after):
        keys, send, recv, pair, lands = pending[layer]
        lands = rs_chip_wait(send, recv, pair, lands, after, "rs_chip_wait_%d" % layer)
        for k, q, p in zip(keys, lands, pair):
            n, l, layers = _weight_of(k, layer)
            red[n] = rs_chip_sum(q, p, l, red.get(n), layers, jc)

    red = {}
    for layer in (3, 2, 1):
        reduce_layer(layer, red, [dx])
    odd_only = [n for n in BIG if n.endswith("_cd") or n.startswith("glu")]
    update_big(odd_only, red)

    gfull = [jnp.stack(G[n]) if isinstance(G[n], list) else G[n] for n in SMALL]
    shapes = [g.shape for g in gfull]
    gsum = _unpack(allreduce_small(_pack(gfull)), shapes)
    gloc = []
    for n, g in zip(SMALL, gsum):
        if n in SMALL_SHARDED:
            g = lax.dynamic_slice_in_dim(g, j * SMALL_SHARDED[n], SMALL_SHARDED[n], axis=1)
        gloc.append(g)
    for n, g in zip(SMALL, gloc):
        shp = a[n].shape
        two = (-1, shp[-1]) if len(shp) > 1 else (1, shp[0])
        upd = adamw(a[n].reshape(two), g.reshape(two), a["m_" + n].reshape(two), a["v_" + n].reshape(two), "adamw_" + n)
        outs[n] = (g,) + tuple(t.reshape(shp) for t in upd)

    behind = [outs[n][1] for n in odd_only + SMALL[-1:]] + [red[n] for n in BIG if n not in odd_only]
    reduce_layer(0, red, behind)
    update_big([n for n in BIG if n not in odd_only], red)

    res = [loss, dx[None]]
    for part in range(4):
        res += [outs[n][part] for n in WEIGHTS]
    return tuple(res)
```

```python
import math

import jax
import jax.numpy as jnp
from jax import lax
from jax.experimental import pallas as pl
from jax.experimental.pallas import tpu as pltpu

F32, BF16 = jnp.float32, jnp.bfloat16
S, D = 2048, 1024
MEM = 256
EPS = 1e-6
NEG = -1e30
QB = 128
PATTERNS = (1, 4, 16)
NG, NP, NH = 32, 64, 16
NS = NG * NP
LR, B1, B2, AEPS, WD, STEP = 0.001, 0.9, 0.999, 1e-08, 0.01, 10
MESHID = pl.DeviceIdType.MESH
VMEM_LIMIT = 56 * 1024 * 1024


def _cparams(sem):
    return pltpu.CompilerParams(dimension_semantics=sem, vmem_limit_bytes=VMEM_LIMIT)


def _sig(x):
    return 1.0 / (1.0 + jnp.exp(-x))


def _dot(a, b, dims):
    return lax.dot_general(a, b, (dims, ((), ())), preferred_element_type=F32)


def _nn(a, b):
    return _dot(a, b, ((1,), (0,)))


def _nt(a, b):
    return _dot(a, b, ((1,), (1,)))


def _tn(a, b):
    return _dot(a, b, ((0,), (0,)))


_DIMS = {"nn": ((1,), (0,)), "nt": ((1,), (1,)), "tn": ((0,), (0,))}


def _tile(dim, cc=None, cap=1024):
    for t in (2048, 1536, 1024, 768, 512, 384, 256, 128):
        if t <= cap and dim % t == 0 and (cc is None or cc % t == 0):
            return t
    return dim


MM_VMEM = 36 * 1024 * 1024


def _mm_tiles(m, n, k, ccm, ccn, cck, a_bytes, b_bytes, o_bytes):
    caps = [1024, 1024, 2048]
    while True:
        tm, tn, tk = _tile(m, ccm, caps[0]), _tile(n, ccn, caps[1]), _tile(k, cck, caps[2])
        need = 2 * (tm * tk * a_bytes + tk * tn * b_bytes + tm * tn * o_bytes) + (tm * tn * 4 if tk < k else 0)
        if need <= MM_VMEM:
            return tm, tn, tk
        if tk > 1024:
            caps[2] = tk // 2
        elif tn >= tm:
            caps[1] = tn // 2
        else:
            caps[0] = tm // 2


def m2(arr, col_off=0, ncols=None):
    rows, cols = arr.shape
    ncols = cols - col_off if ncols is None else ncols

    def spec(tr, tc, rc):
        assert col_off % tc == 0
        return pl.BlockSpec((tr, tc), lambda *g: (rc(*g)[0], rc(*g)[1] + col_off // tc))
    return (arr, rows, ncols, spec, None if col_off == 0 else col_off)


def mcs(arr):
    cs = arr.shape[2]

    def spec(tr, tc, rc):
        n = cs // tc
        return pl.BlockSpec((None, tr, tc), lambda *g: (rc(*g)[1] // n, rc(*g)[0], rc(*g)[1] % n))
    return (arr, arr.shape[1], 4 * cs, spec, cs)


def out2(rows, cols):
    def spec(tr, tc, rc):
        return pl.BlockSpec((tr, tc), lambda *g: tuple(rc(*g)))
    return ((rows, cols), spec, None)


def outcs(rows, cs):
    def spec(tr, tc, rc):
        n = cs // tc
        return pl.BlockSpec((None, tr, tc), lambda *g: (rc(*g)[1] // n, rc(*g)[0], rc(*g)[1] % n))
    return ((4, rows, cs), spec, cs)


def _both(a, b):
    if a is None:
        return b
    if b is None:
        return a
    return math.gcd(a, b)


def mm(a, b, mode, name, add=None, out=None, out_dtype=F32):
    a_arr, a_r, a_c, a_spec, a_cc = a
    b_arr, b_r, b_c, b_spec, b_cc = b
    if mode == "nn":
        m, k, n = a_r, a_c, b_c
        assert b_r == k
        ccm, cck, ccn = None, a_cc, b_cc
    elif mode == "nt":
        m, k, n = a_r, a_c, b_r
        assert b_c == k
        ccm, cck, ccn = None, _both(a_cc, b_cc), None
    else:
        m, k, n = a_c, a_r, b_c
        assert b_r == k
        ccm, cck, ccn = a_cc, None, b_cc
    out = out2(m, n) if out is None else out
    o_shape, o_spec, o_cc = out
    ccn = _both(ccn, o_cc)
    if add is not None:
        ccn = _both(ccn, add[4])
    o_bytes = jnp.dtype(out_dtype).itemsize + (0 if add is None else add[0].dtype.itemsize)
    tm, tn, tk = _mm_tiles(m, n, k, ccm, ccn, cck, a_arr.dtype.itemsize, b_arr.dtype.itemsize, o_bytes)
    nk = k // tk
    if mode == "nn":
        in_specs = [a_spec(tm, tk, lambda i, j, kk: (i, kk)), b_spec(tk, tn, lambda i, j, kk: (kk, j))]
    elif mode == "nt":
        in_specs = [a_spec(tm, tk, lambda i, j, kk: (i, kk)), b_spec(tn, tk, lambda i, j, kk: (j, kk))]
    else:
        in_specs = [a_spec(tk, tm, lambda i, j, kk: (kk, i)), b_spec(tk, tn, lambda i, j, kk: (kk, j))]
    args = [a_arr, b_arr]
    if add is not None:
        in_specs.append(add[3](tm, tn, lambda i, j, kk: (i, j)))
        args.append(add[0])
    dims = _DIMS[mode]
    has_add = add is not None

    def body(*refs):
        a_ref, b_ref = refs[0], refs[1]
        add_ref = refs[2] if has_add else None
        prod = _dot(a_ref[...].astype(BF16), b_ref[...].astype(BF16), dims)
        if nk == 1:
            o_ref = refs[-1]
            if has_add:
                prod = prod + add_ref[...].astype(F32)
            o_ref[...] = prod.astype(o_ref.dtype)
            return
        o_ref, acc = refs[-2], refs[-1]
        kk = pl.program_id(2)

        @pl.when(kk == 0)
        def _():
            acc[...] = prod

        @pl.when(kk > 0)
        def _():
            acc[...] += prod

        @pl.when(kk == nk - 1)
        def _():
            r = acc[...]
            if has_add:
                r = r + add_ref[...].astype(F32)
            o_ref[...] = r.astype(o_ref.dtype)

    return pl.pallas_call(
        body, name=name, grid=(m // tm, n // tn, nk), in_specs=in_specs,
        out_specs=o_spec(tm, tn, lambda i, j, kk: (i, j)),
        out_shape=jax.ShapeDtypeStruct(o_shape, out_dtype),
        scratch_shapes=[pltpu.VMEM((tm, tn), F32)] if nk > 1 else [],
        compiler_params=_cparams(("parallel", "parallel", "arbitrary")),
    )(*args)


def rw(fn, ins, outs, name, rows, tr=256, consts=(), accs=()):
    n_in, n_c, n_o, n_a = len(ins), len(consts), len(outs), len(accs)
    in_specs = []
    for arr, off, width in ins:
        assert off % width == 0
        in_specs.append(pl.BlockSpec((tr, width), lambda i, o=off // width: (i, o)))
    for c in consts:
        in_specs.append(pl.BlockSpec(c.shape, lambda i: (0, 0)))
    out_specs = [pl.BlockSpec((tr, w), lambda i: (i, 0)) for w, _ in outs]
    out_specs += [pl.BlockSpec(s, lambda i: (0, 0)) for s in accs]
    out_shape = [jax.ShapeDtypeStruct((rows, w), dt) for w, dt in outs]
    out_shape += [jax.ShapeDtypeStruct(s, F32) for s in accs]

    def body(*refs):
        vals = [r[...] for r in refs[:n_in + n_c]]
        o_refs = refs[n_in + n_c:n_in + n_c + n_o]
        a_refs = refs[n_in + n_c + n_o:]
        res = fn(*vals)
        for r, v in zip(o_refs, res[:n_o]):
            r[...] = v.astype(r.dtype)
        if n_a:
            @pl.when(pl.program_id(0) == 0)
            def _():
                for r in a_refs:
                    r[...] = jnp.zeros_like(r)
            for r, v in zip(a_refs, res[n_o:]):
                r[...] += v

    res = pl.pallas_call(
        body, name=name, grid=(rows // tr,), in_specs=in_specs, out_specs=out_specs,
        out_shape=out_shape,
        compiler_params=_cparams(("arbitrary",) if n_a else ("parallel",)),
    )(*[a for a, _, _ in ins], *consts)
    return res


def _rstd(x):
    return lax.rsqrt(jnp.mean(x * x, axis=-1, keepdims=True) + EPS)


def rms_fwd(x, g, name):
    def fn(xv, gv):
        xv = xv.astype(F32)
        return (xv * _rstd(xv) * gv,)
    return rw(fn, [(x, 0, D)], [(D, BF16)], name, x.shape[0], consts=[g])[0]


def _rms_bwd_math(xv, dy, gv):
    r = _rstd(xv)
    dyg = dy * gv
    dx = r * dyg - xv * (r * r * r / D) * jnp.sum(dyg * xv, axis=-1, keepdims=True)
    dg = jnp.sum(dy * xv * r, axis=0, keepdims=True)
    return dx, dg


def rms_bwd(x, dy, dres, g, name):
    def fn(xv, dyv, drv, gv):
        dx, dg = _rms_bwd_math(xv, dyv, gv)
        return dx + drv, dg
    return rw(fn, [(x, 0, D), (dy, 0, D), (dres, 0, D)], [(D, F32)], name, x.shape[0],
              consts=[g], accs=[(1, D)])


def final_loss(x, tgt, g):
    def fn(xv, tv, gv):
        e = xv * _rstd(xv) * gv - tv
        loss = 0.5 * jnp.sum(jnp.sum(e * e, axis=-1, keepdims=True), axis=0, keepdims=True) / D
        dx, dg = _rms_bwd_math(xv, e / D, gv)
        return dx, loss, dg
    return rw(fn, [(x, 0, D), (tgt, 0, D)], [(D, F32)], "final_loss", S, consts=[g],
              accs=[(1, 1), (1, D)])


def _attn_bias(bias_ref):
    ii = lax.broadcasted_iota(jnp.int32, (2 * QB, 2 * QB), 0) % QB
    jj = lax.broadcasted_iota(jnp.int32, (2 * QB, 2 * QB), 1)
    dist = ii + QB - jj
    band = (dist >= 0) & (dist <= QB)
    bias_ref[1] = jnp.where(band, 0.0, NEG)
    bias_ref[0] = jnp.where(band & (jj >= QB), 0.0, NEG)


def _two_heads(x, m0):
    return jnp.concatenate([jnp.where(m0, x, 0.0), jnp.where(m0, 0.0, x)], axis=0)


def _per_head(col, m0):
    return jnp.where(m0, col[:QB], col[QB:])


def _attn_rows(idx, d):
    if d == 1:
        b = idx
        cur = pl.ds(pl.multiple_of(b * QB, QB), QB)
        prev = pl.ds(pl.multiple_of(jnp.maximum(b - 1, 0) * QB, QB), QB)
    else:
        r, b = lax.rem(idx, d), lax.div(idx, d)
        cur = pl.ds(r + b * (QB * d), QB, stride=d)
        prev = pl.ds(r + jnp.maximum(b - 1, 0) * (QB * d), QB, stride=d)
    return cur, prev, b


NBLK = S // QB
GROUP = 4


def _colblk(off):
    return pl.BlockSpec((S, 128), lambda hp: (0, off * 8 + hp))


def attn_fwd(z):
    def body(q_ref, k_ref, v_ref, g_ref, o_ref, l_ref, a_ref, os, ls, bias):
        _attn_bias(bias)
        m0 = lax.broadcasted_iota(jnp.int32, (1, 128), 1) < 64
        for pi, d in enumerate(PATTERNS):
            lone = S // d == QB

            def load(idx, d=d, lone=lone):
                cur, prev, b = _attn_rows(idx, d)
                if lone:
                    return cur, (q_ref[cur, :], None, k_ref[cur, :], None, v_ref[cur, :], bias[1, :, QB:])
                return cur, (q_ref[cur, :], k_ref[prev, :], k_ref[cur, :], v_ref[prev, :], v_ref[cur, :],
                             bias[jnp.minimum(b, 1)])

            def block(q, kp, kc, vp, vc, bs):
                qq = _two_heads(q * 0.125, m0).astype(BF16)
                k = (kc if kp is None else jnp.concatenate([kp, kc], axis=0)).astype(BF16)
                s = _nt(qq, k) + bs
                mx = jnp.max(s, axis=-1, keepdims=True)
                p = jnp.exp(s - mx)
                den = jnp.sum(p, axis=-1, keepdims=True)
                pb = p.astype(BF16)
                vv = _two_heads(vc if vp is None else jnp.concatenate([vp, vc], axis=0), m0).astype(BF16)
                o = _nn(jnp.concatenate([pb[:QB], pb[QB:]], axis=1), vv)
                return o * _per_head(1.0 / den, m0), _per_head(mx + jnp.log(den), m0)

            def step(i, carry, pi=pi):
                loaded = [load(i * GROUP + u) for u in range(GROUP)]
                done = [block(*vals) for _, vals in loaded]
                for (cur, _), (o, l) in zip(loaded, done):
                    os[pi, cur, :] = o
                    ls[pi, cur, :] = l
                return carry
            lax.fori_loop(0, NBLK // GROUP, step, 0)
        l1, l2, l3 = ls[0], ls[1], ls[2]
        mx = jnp.maximum(jnp.maximum(l1, l2), l3)
        e1, e2, e3 = jnp.exp(l1 - mx), jnp.exp(l2 - mx), jnp.exp(l3 - mx)
        tot = e1 + e2 + e3
        o = (os[0] * e1 + os[1] * e2 + os[2] * e3) / tot
        ga = g_ref[...]
        o_ref[...] = o
        l_ref[...] = mx + jnp.log(tot)
        a_ref[...] = (o * (ga * _sig(ga))).astype(a_ref.dtype)

    out = pl.BlockSpec((S, 128), lambda hp: (0, hp))
    return pl.pallas_call(
        body, name="attn_fwd", grid=(8,),
        in_specs=[_colblk(0), _colblk(1), _colblk(2), _colblk(3)], out_specs=[out] * 3,
        out_shape=[jax.ShapeDtypeStruct((S, D), F32), jax.ShapeDtypeStruct((S, D), F32),
                   jax.ShapeDtypeStruct((S, 2 * D), BF16)],
        scratch_shapes=[pltpu.VMEM((3, S, 128), F32), pltpu.VMEM((3, S, 128), F32),
                        pltpu.VMEM((2, 2 * QB, 2 * QB), F32)],
        compiler_params=_cparams(("parallel",)),
    )(z, z, z, z)


def attn_bwd(z, d_cat, o, lse):
    def body(q_ref, k_ref, v_ref, g_ref, da_ref, o_ref, l_ref, dq_ref, dk_ref, dv_ref, dg_ref, do_s, pr_s, bias):
        _attn_bias(bias)
        m0 = lax.broadcasted_iota(jnp.int32, (1, 128), 1) < 64
        ga = g_ref[...]
        sg = _sig(ga)
        da = da_ref[...]
        ov = o_ref[...]
        do = da * (ga * sg)
        dg_ref[...] = da * ov * (sg * (1.0 + ga * (1.0 - sg)))
        do_s[...] = do
        pr_s[...] = do * ov
        dq_ref[...] = jnp.zeros_like(dq_ref)
        dk_ref[...] = jnp.zeros_like(dk_ref)
        dv_ref[...] = jnp.zeros_like(dv_ref)
        for d in PATTERNS:
            lone = S // d == QB

            def load(idx, d=d, lone=lone):
                cur, prev, b = _attn_rows(idx, d)
                if lone:
                    return (cur, None), (q_ref[cur, :], None, k_ref[cur, :], None, v_ref[cur, :],
                                         do_s[cur, :], pr_s[cur, :], l_ref[cur, :], bias[1, :, QB:])
                return (cur, prev), (q_ref[cur, :], k_ref[prev, :], k_ref[cur, :], v_ref[prev, :], v_ref[cur, :],
                                     do_s[cur, :], pr_s[cur, :], l_ref[cur, :], bias[jnp.minimum(b, 1)])

            def block(q, kp, kc, vp, vc, dof, prod, lp, bs):
                qq = _two_heads(q * 0.125, m0).astype(BF16)
                kf = kc if kp is None else jnp.concatenate([kp, kc], axis=0)
                k = kf.astype(BF16)
                v = (vc if vp is None else jnp.concatenate([vp, vc], axis=0)).astype(BF16)
                dd = _two_heads(dof, m0).astype(BF16)
                lh = jnp.max(jnp.concatenate([jnp.where(m0, lp, -jnp.inf), jnp.where(m0, -jnp.inf, lp)], axis=0),
                             axis=-1, keepdims=True)
                delta = jnp.sum(_two_heads(prod, m0), axis=-1, keepdims=True)
                p = jnp.exp(_nt(qq, k) + bs - lh)
                ds = (p * (_nt(dd, v) - delta)).astype(BF16)
                dq = _nn(jnp.concatenate([ds[:QB], ds[QB:]], axis=1), _two_heads(kf, m0).astype(BF16))
                return dq * 0.125, _tn(ds, qq), _tn(p.astype(BF16), dd)

            def step(i, carry):
                loaded = [load(i * GROUP + u) for u in range(GROUP)]
                done = [block(*vals) for _, vals in loaded]
                for ((cur, prev), _), (dq, dk, dv) in zip(loaded, done):
                    dq_ref[cur, :] = dq_ref[cur, :] + dq
                    if prev is not None:
                        dk_ref[prev, :] = dk_ref[prev, :] + dk[:QB]
                        dv_ref[prev, :] = dv_ref[prev, :] + dv[:QB]
                    dk_ref[cur, :] = dk_ref[cur, :] + dk[-QB:]
                    dv_ref[cur, :] = dv_ref[cur, :] + dv[-QB:]
                return carry
            lax.fori_loop(0, NBLK // GROUP, step, 0)

    blk = pl.BlockSpec((S, 128), lambda hp: (0, hp))
    return pl.pallas_call(
        body, name="attn_bwd", grid=(8,),
        in_specs=[_colblk(0), _colblk(1), _colblk(2), _colblk(3), blk, blk, blk], out_specs=[blk] * 4,
        out_shape=[jax.ShapeDtypeStruct((S, D), F32)] * 4,
        scratch_shapes=[pltpu.VMEM((S, 128), F32), pltpu.VMEM((S, 128), F32), pltpu.VMEM((2, 2 * QB, 2 * QB), F32)],
        compiler_params=_cparams(("parallel",)),
    )(z, z, z, z, d_cat, o, lse)


def assemble_dz_even(parts):
    def body(*refs):
        o_ref = refs[-1]
        for j in range(6):
            o_ref[:, j * D:(j + 1) * D] = refs[j][...].astype(o_ref.dtype)
    tr = 256
    blk = pl.BlockSpec((tr, D), lambda i: (i, 0))
    return pl.pallas_call(
        body, name="assemble_dz_even", grid=(S // tr,), in_specs=[blk] * 6,
        out_specs=pl.BlockSpec((tr, 6 * D), lambda i: (i, 0)),
        out_shape=jax.ShapeDtypeStruct((S, 6 * D), BF16),
        compiler_params=_cparams(("parallel",)),
    )(*parts)


def _pool_window(g):
    return jnp.where(g == 0, 2.0, jnp.where(g == 1, 4.0, jnp.where(g == 2, 8.0, 16.0)))


def _pool_sel(g, levels):
    return jnp.where(g == 0, levels[0], jnp.where(g == 1, levels[1], jnp.where(g == 2, levels[2], levels[3])))


def _pool_fwd_math(v, g):
    t = lax.broadcasted_iota(jnp.int32, (S, 1), 0)
    s = v
    levels = []
    for k in (1, 2, 4, 8):
        s = s + jnp.where(t >= k, pltpu.roll(s, k, 0), 0.0)
        levels.append(s)
    cnt = jnp.minimum((t + 1).astype(F32), _pool_window(g))
    return _pool_sel(g, levels) / cnt - v, cnt


def pool_fwd(z, pw, ps, cat):
    def body(v_ref, g_ref, pw_ref, ps_ref, cat_ref, o_ref):
        g = pl.program_id(0)
        pooled, _ = _pool_fwd_math(v_ref[...], g)
        mixed = _nn(pooled.astype(BF16), pw_ref[...].astype(BF16))
        gb = g_ref[...]
        o_ref[...] = (mixed * ps_ref[...] * (gb * _sig(gb))).astype(o_ref.dtype)

    return pl.pallas_call(
        body, name="pool_fwd", grid=(4,),
        in_specs=[pl.BlockSpec((S, 256), lambda g: (0, 16 + g)),
                  pl.BlockSpec((S, 256), lambda g: (0, 20 + g)),
                  pl.BlockSpec((None, 256, 256), lambda g: (g, 0, 0)),
                  pl.BlockSpec((1, 256), lambda g: (0, g)), pl.BlockSpec(memory_space=pl.ANY)],
        out_specs=pl.BlockSpec((S, 256), lambda g: (0, 4 + g)),
        out_shape=jax.ShapeDtypeStruct((S, 2 * D), BF16),
        input_output_aliases={4: 0},
        compiler_params=_cparams(("parallel",)),
    )(z, z, pw, ps, cat)


def pool_bwd(z, d_cat, pw, ps):
    def body(v_ref, g_ref, d_ref, pw_ref, ps_ref, dv_ref, dg_ref, dpw_ref, dps_ref):
        g = pl.program_id(0)
        v = v_ref[...]
        pooled, cnt = _pool_fwd_math(v, g)
        pwb = pw_ref[...].astype(BF16)
        pb = pooled.astype(BF16)
        mixed = _nn(pb, pwb)
        gb = g_ref[...]
        sg = _sig(gb)
        dout = d_ref[...]
        sc = ps_ref[...]
        dg_ref[...] = dout * mixed * sc * (sg * (1.0 + gb * (1.0 - sg)))
        dms = dout * (gb * sg)
        dps_ref[...] = jnp.sum(dms * mixed, axis=0, keepdims=True)
        dmx = (dms * sc).astype(BF16)
        dpw_ref[...] = _tn(pb, dmx)
        dpooled = _nt(dmx, pwb)
        t = lax.broadcasted_iota(jnp.int32, (S, 1), 0)
        s = dpooled / cnt
        levels = []
        for k in (1, 2, 4, 8):
            s = s + jnp.where(t < S - k, pltpu.roll(s, S - k, 0), 0.0)
            levels.append(s)
        dv_ref[...] = _pool_sel(g, levels) - dpooled

    return pl.pallas_call(
        body, name="pool_bwd", grid=(4,),
        in_specs=[pl.BlockSpec((S, 256), lambda g: (0, 16 + g)),
                  pl.BlockSpec((S, 256), lambda g: (0, 20 + g)),
                  pl.BlockSpec((S, 256), lambda g: (0, 4 + g)),
                  pl.BlockSpec((None, 256, 256), lambda g: (g, 0, 0)),
                  pl.BlockSpec((1, 256), lambda g: (0, g))],
        out_specs=[pl.BlockSpec((S, 256), lambda g: (0, g)),
                   pl.BlockSpec((S, 256), lambda g: (0, g)),
                   pl.BlockSpec((None, 256, 256), lambda g: (g, 0, 0)),
                   pl.BlockSpec((1, 256), lambda g: (0, g))],
        out_shape=[jax.ShapeDtypeStruct((S, D), F32), jax.ShapeDtypeStruct((S, D), F32),
                   jax.ShapeDtypeStruct((4, 256, 256), F32), jax.ShapeDtypeStruct((1, D), F32)],
        compiler_params=_cparams(("parallel",)),
    )(z, z, d_cat, pw, ps)


CH = 128


def _sgu_common(v, lng, lnb, w_ref):
    mu = jnp.mean(v, axis=-1, keepdims=True)
    vc = v - mu
    rs = lax.rsqrt(jnp.mean(vc * vc, axis=-1, keepdims=True) + EPS)
    xhat = vc * rs
    vn = (xhat * lng + lnb).astype(BF16)
    ri = lax.broadcasted_iota(jnp.int32, (CH, CH), 0)
    ci = lax.broadcasted_iota(jnp.int32, (CH, CH), 1)
    tril = ri >= ci
    ws = [jnp.where(tril, w_ref[g], 0.0).astype(BF16) for g in range(4)]
    return xhat, rs, vn, tril, ws


def _zspec(off):
    return pl.BlockSpec((CH, D), lambda c: (c, off))


def _full(shape):
    return pl.BlockSpec(shape, lambda c: (0,) * len(shape))


def sgu_fwd(z, lng, lnb, w, bfull):
    def body(u_ref, v_ref, g_ref, lng_ref, lnb_ref, w_ref, b_ref, o_ref):
        _, _, vn, _, ws = _sgu_common(v_ref[...], lng_ref[...], lnb_ref[...], w_ref)
        for g in range(4):
            sl = slice(g * 256, (g + 1) * 256)
            mixed = _nn(ws[g], vn[:, sl]) + b_ref[:, sl]
            gc = g_ref[:, sl]
            o_ref[:, sl] = (u_ref[:, sl] * mixed * (gc * _sig(gc))).astype(o_ref.dtype)

    return pl.pallas_call(
        body, name="sgu_fwd", grid=(S // CH,),
        in_specs=[_zspec(0), _zspec(1), _zspec(2), _full((1, D)), _full((1, D)),
                  _full((4, CH, CH)), _full((CH, D))],
        out_specs=pl.BlockSpec((CH, D), lambda c: (c, 0)),
        out_shape=jax.ShapeDtypeStruct((S, D), BF16),
        compiler_params=_cparams(("parallel",)),
    )(z, z, z, lng, lnb, w, bfull)


def sgu_bwd(z, d_cat, lng, lnb, w, bfull):
    def body(u_ref, v_ref, g_ref, d_ref, lng_ref, lnb_ref, w_ref, b_ref,
             du_ref, dv_ref, dg_ref, dw_ref, db_ref, dlg_ref, dlb_ref):
        @pl.when(pl.program_id(0) == 0)
        def _():
            dw_ref[...] = jnp.zeros_like(dw_ref)
            db_ref[...] = jnp.zeros_like(db_ref)
            dlg_ref[...] = jnp.zeros_like(dlg_ref)
            dlb_ref[...] = jnp.zeros_like(dlb_ref)

        lng = lng_ref[...]
        xhat, rs, vn, tril, ws = _sgu_common(v_ref[...], lng, lnb_ref[...], w_ref)
        lane = lax.broadcasted_iota(jnp.int32, (1, 128), 1)
        db = jnp.zeros((CH, 128), F32)
        dvn_parts = []
        for g in range(4):
            sl = slice(g * 256, (g + 1) * 256)
            mixed = _nn(ws[g], vn[:, sl]) + b_ref[:, sl]
            gc = g_ref[:, sl]
            sg = _sig(gc)
            u = u_ref[:, sl]
            dc = d_ref[:, sl]
            du_ref[:, sl] = dc * mixed * (gc * sg)
            dg_ref[:, sl] = dc * u * mixed * (sg * (1.0 + gc * (1.0 - sg)))
            dmx = dc * u * (gc * sg)
            db = db + jnp.where(lane == g, jnp.sum(dmx, axis=-1, keepdims=True), 0.0)
            dmb = dmx.astype(BF16)
            dw_ref[g] += jnp.where(tril, _nt(dmb, vn[:, sl]), 0.0)
            dvn_parts.append(_tn(ws[g], dmb))
        db_ref[...] += db
        dvn = jnp.concatenate(dvn_parts, axis=1)
        dlb_ref[...] += jnp.sum(dvn, axis=0, keepdims=True)
        dlg_ref[...] += jnp.sum(dvn * xhat, axis=0, keepdims=True)
        dxh = dvn * lng
        dv_ref[...] = rs * (dxh - jnp.mean(dxh, axis=-1, keepdims=True)
                            - xhat * jnp.mean(dxh * xhat, axis=-1, keepdims=True))

    row = pl.BlockSpec((CH, D), lambda c: (c, 0))
    return pl.pallas_call(
        body, name="sgu_bwd", grid=(S // CH,),
        in_specs=[_zspec(0), _zspec(1), _zspec(2), row, _full((1, D)), _full((1, D)),
                  _full((4, CH, CH)), _full((CH, D))],
        out_specs=[row, row, row, _full((4, CH, CH)), _full((CH, 128)), _full((1, D)), _full((1, D))],
        out_shape=[jax.ShapeDtypeStruct((S, D), F32)] * 3
        + [jax.ShapeDtypeStruct((4, CH, CH), F32), jax.ShapeDtypeStruct((CH, 128), F32),
           jax.ShapeDtypeStruct((1, D), F32), jax.ShapeDtypeStruct((1, D), F32)],
        compiler_params=_cparams(("arbitrary",)),
    )(z, z, z, d_cat, lng, lnb, w, bfull)


TB = 256


def _cmul(ar, ai, br, bi):
    return ar * br - ai * bi, ar * bi + ai * br


def _scan_consts(ar, ai, reverse):
    a2 = _cmul(ar, ai, ar, ai)
    a4 = _cmul(*a2, *a2)
    row = lax.broadcasted_iota(jnp.int32, (8, NS), 0)
    pr = jnp.zeros((8, NS), F32)
    pi = jnp.zeros((8, NS), F32)
    cr, ci = ar, ai
    for r in range(8):
        sel = row == (7 - r if reverse else r)
        pr = jnp.where(sel, cr, pr)
        pi = jnp.where(sel, ci, pi)
        cr, ci = _cmul(cr, ci, ar, ai)
    return ((ar, ai), a2, a4), (pr, pi), row


def scan_fwd(bu, abr, abi):
    def body(bu_ref, ar_ref, ai_ref, h_ref, car, cai):
        @pl.when(pl.program_id(0) == 0)
        def _():
            car[...] = jnp.zeros_like(car)
            cai[...] = jnp.zeros_like(cai)

        pows, (pr, pi), row = _scan_consts(ar_ref[...], ai_ref[...], False)

        def tile(t, carry):
            c_r, c_i = carry
            rows = pl.ds(pl.multiple_of(t * 8, 8), 8)
            xr = bu_ref[rows, 0:NS]
            xi = bu_ref[rows, NS:2 * NS]
            for k, (kr, ki) in zip((1, 2, 4), pows):
                sr = jnp.where(row >= k, pltpu.roll(xr, k, 0), 0.0)
                si = jnp.where(row >= k, pltpu.roll(xi, k, 0), 0.0)
                xr, xi = xr + kr * sr - ki * si, xi + kr * si + ki * sr
            xr, xi = xr + pr * c_r - pi * c_i, xi + pr * c_i + pi * c_r
            h_ref[rows, 0:NS] = xr
            h_ref[rows, NS:2 * NS] = xi
            return (jnp.broadcast_to(xr[7:8, :], (8, NS)), jnp.broadcast_to(xi[7:8, :], (8, NS)))

        c_r, c_i = lax.fori_loop(0, TB // 8, tile, (car[...], cai[...]))
        car[...] = c_r
        cai[...] = c_i

    return pl.pallas_call(
        body, name="s5_scan_fwd", grid=(S // TB,),
        in_specs=[pl.BlockSpec((TB, 2 * NS), lambda i: (i, 0)),
                  pl.BlockSpec((1, NS), lambda i: (0, 0)), pl.BlockSpec((1, NS), lambda i: (0, 0))],
        out_specs=pl.BlockSpec((TB, 2 * NS), lambda i: (i, 0)),
        out_shape=jax.ShapeDtypeStruct((S, 2 * NS), F32),
        scratch_shapes=[pltpu.VMEM((8, NS), F32), pltpu.VMEM((8, NS), F32)],
        compiler_params=_cparams(("arbitrary",)),
    )(bu, abr, abi)


def scan_bwd(eta, h, abr, abi):
    nt = S // TB

    def body(e_ref, h_ref, ar_ref, ai_ref, l_ref, da_ref, car, cai):
        @pl.when(pl.program_id(0) == 0)
        def _():
            car[...] = jnp.zeros_like(car)
            cai[...] = jnp.zeros_like(cai)
            da_ref[...] = jnp.zeros_like(da_ref)

        pows, (pr, pi), row = _scan_consts(ar_ref[...], -ai_ref[...], True)

        def tile(tt, carry):
            c_r, c_i, acr, aci = carry
            t = TB // 8 - 1 - tt
            rows = pl.ds(pl.multiple_of(t * 8, 8), 8)
            xr = e_ref[rows, 0:NS]
            xi = e_ref[rows, NS:2 * NS]
            for k, (kr, ki) in zip((1, 2, 4), pows):
                sr = jnp.where(row < 8 - k, pltpu.roll(xr, 8 - k, 0), 0.0)
                si = jnp.where(row < 8 - k, pltpu.roll(xi, 8 - k, 0), 0.0)
                xr, xi = xr + kr * sr - ki * si, xi + kr * si + ki * sr
            xr, xi = xr + pr * c_r - pi * c_i, xi + pr * c_i + pi * c_r
            l_ref[rows, 0:NS] = xr
            l_ref[rows, NS:2 * NS] = xi
            nr = jnp.where(row < 7, pltpu.roll(xr, 7, 0), c_r)
            ni = jnp.where(row < 7, pltpu.roll(xi, 7, 0), c_i)
            hr = h_ref[rows, 0:NS]
            hi = h_ref[rows, NS:2 * NS]
            acr = acr + hr * nr + hi * ni
            aci = aci + hr * ni - hi * nr
            return (jnp.broadcast_to(xr[0:1, :], (8, NS)), jnp.broadcast_to(xi[0:1, :], (8, NS)), acr, aci)

        zero = jnp.zeros((8, NS), F32)
        c_r, c_i, acr, aci = lax.fori_loop(0, TB // 8, tile, (car[...], cai[...], zero, zero))
        car[...] = c_r
        cai[...] = c_i
        da_ref[:, 0:NS] += acr
        da_ref[:, NS:2 * NS] += aci

    rev = pl.BlockSpec((TB, 2 * NS), lambda i: (nt - 1 - i, 0))
    return pl.pallas_call(
        body, name="s5_scan_bwd", grid=(nt,),
        in_specs=[rev, rev, pl.BlockSpec((1, NS), lambda i: (0, 0)), pl.BlockSpec((1, NS), lambda i: (0, 0))],
        out_specs=[rev, pl.BlockSpec((8, 2 * NS), lambda i: (0, 0))],
        out_shape=[jax.ShapeDtypeStruct((S, 2 * NS), F32), jax.ShapeDtypeStruct((8, 2 * NS), F32)],
        scratch_shapes=[pltpu.VMEM((8, NS), F32), pltpu.VMEM((8, NS), F32)],
        compiler_params=_cparams(("arbitrary",)),
    )(eta, h, abr, abi)


GC = 0.7978845608028654
GA = 0.044715


def s5_post(hc, z, dskip):
    def fn(hv, xd, dv):
        y = hv + dv * xd
        return y, 0.5 * y * (1.0 + jnp.tanh(GC * (y + GA * y * y * y)))
    return rw(fn, [(hc, 0, 512), (z, 3072, 512)], [(512, F32), (512, BF16)], "s5_post", S, consts=[dskip])


def s5_post_bwd(dyg, ypre, z, dskip):
    def fn(dy, y, xd, dv):
        th = jnp.tanh(GC * (y + GA * y * y * y))
        dg = 0.5 * (1.0 + th) + 0.5 * y * (1.0 - th * th) * GC * (1.0 + 3.0 * GA * y * y)
        dyp = dy * dg
        return dyp, dyp * dv, jnp.sum(dyp * xd, axis=0, keepdims=True)
    return rw(fn, [(dyg, 0, 512), (ypre, 0, 512), (z, 3072, 512)], [(512, BF16), (512, F32)],
              "s5_post_bwd", S, consts=[dskip], accs=[(1, 512)])


def glu_fwd(t, z, c_out):
    def fn(t1, t2, gd, co):
        return (jnp.concatenate([co, (t1 * _sig(t2) * (gd * _sig(gd))).astype(BF16)], axis=1),)
    return rw(fn, [(t, 0, 512), (t, 512, 512), (z, 3584, 512), (c_out, 0, D)], [(D + 512, BF16)], "glu_fwd", S)[0]


def glu_bwd(t, z, d_cat):
    def fn(t1, t2, gd, dd):
        s2, sg = _sig(t2), _sig(gd)
        sl = gd * sg
        return (jnp.concatenate([dd * s2 * sl, dd * t1 * s2 * (1.0 - s2) * sl], axis=1),
                dd * t1 * s2 * (sg * (1.0 + gd * (1.0 - sg))))
    return rw(fn, [(t, 0, 512), (t, 512, 512), (z, 3584, 512), (d_cat, 1024, 512)],
              [(D, BF16), (512, F32)], "glu_bwd", S)


def assemble_dz_odd(du, dv, dgc, dxd, dgd):
    def body(a, b, c, d, e, o_ref):
        o_ref[:, 0:D] = a[...].astype(BF16)
        o_ref[:, D:2 * D] = b[...].astype(BF16)
        o_ref[:, 2 * D:3 * D] = c[...].astype(BF16)
        o_ref[:, 3 * D:3 * D + 512] = d[...].astype(BF16)
        o_ref[:, 3 * D + 512:4 * D] = e[...].astype(BF16)
    tr = 256
    blk = pl.BlockSpec((tr, D), lambda i: (i, 0))
    half = pl.BlockSpec((tr, 512), lambda i: (i, 0))
    return pl.pallas_call(
        body, name="assemble_dz_odd", grid=(S // tr,), in_specs=[blk, blk, blk, half, half],
        out_specs=pl.BlockSpec((tr, 4 * D), lambda i: (i, 0)),
        out_shape=jax.ShapeDtypeStruct((S, 4 * D), BF16),
        compiler_params=_cparams(("parallel",)),
    )(du, dv, dgc, dxd, dgd)


TQ = 256


def _xattn_probs(qh, kh):
    s = _nt(qh, kh) * 0.0625
    p = jnp.exp(s - jnp.max(s, axis=-1, keepdims=True))
    return p / jnp.sum(p, axis=-1, keepdims=True)


def xattn_fwd(q, kv):
    def body(q_ref, kv_ref, o_ref):
        for h in range(4):
            sl = slice(h * 256, (h + 1) * 256)
            p = _xattn_probs(q_ref[:, sl].astype(BF16), kv_ref[:, sl].astype(BF16))
            vh = kv_ref[:, D + h * 256:D + (h + 1) * 256].astype(BF16)
            o_ref[:, sl] = _nn(p.astype(BF16), vh).astype(o_ref.dtype)

    return pl.pallas_call(
        body, name="xattn_fwd", grid=(S // TQ,),
        in_specs=[pl.BlockSpec((TQ, D), lambda i: (i, 0)), pl.BlockSpec((MEM, 2 * D), lambda i: (0, 0))],
        out_specs=pl.BlockSpec((TQ, D), lambda i: (i, 0)),
        out_shape=jax.ShapeDtypeStruct((S, D), BF16),
        compiler_params=_cparams(("parallel",)),
    )(q, kv)


def xattn_bwd(q, kv, d_o):
    def body(q_ref, kv_ref, do_ref, dq_ref, dkv_ref):
        @pl.when(pl.program_id(0) == 0)
        def _():
            dkv_ref[...] = jnp.zeros_like(dkv_ref)

        for h in range(4):
            sl = slice(h * 256, (h + 1) * 256)
            vs = slice(D + h * 256, D + (h + 1) * 256)
            qh = q_ref[:, sl].astype(BF16)
            kh = kv_ref[:, sl].astype(BF16)
            vh = kv_ref[:, vs].astype(BF16)
            doh = do_ref[:, sl].astype(BF16)
            p = _xattn_probs(qh, kh)
            dp = _nt(doh, vh)
            ds = (p * (dp - jnp.sum(p * dp, axis=-1, keepdims=True)) * 0.0625).astype(BF16)
            dq_ref[:, sl] = _nn(ds, kh).astype(dq_ref.dtype)
            dkv_ref[:, sl] += _tn(ds, qh)
            dkv_ref[:, vs] += _tn(p.astype(BF16), doh)

    return pl.pallas_call(
        body, name="xattn_bwd", grid=(S // TQ,),
        in_specs=[pl.BlockSpec((TQ, D), lambda i: (i, 0)), pl.BlockSpec((MEM, 2 * D), lambda i: (0, 0)),
                  pl.BlockSpec((TQ, D), lambda i: (i, 0))],
        out_specs=[pl.BlockSpec((TQ, D), lambda i: (i, 0)), pl.BlockSpec((MEM, 2 * D), lambda i: (0, 0))],
        out_shape=[jax.ShapeDtypeStruct((S, D), BF16), jax.ShapeDtypeStruct((MEM, 2 * D), F32)],
        compiler_params=_cparams(("arbitrary",)),
    )(q, kv, d_o)


def _s5_disc(a_re, a_im, log_dt, b_re, b_im):
    dt = jnp.exp(log_dt)[:, None]
    mag = jnp.exp(dt * a_re)
    abr = mag * jnp.cos(dt * a_im)
    abi = mag * jnp.sin(dt * a_im)
    nr, ni = abr - 1.0, abi
    inv = 1.0 / (a_re * a_re + a_im * a_im)
    cr = (nr * a_re + ni * a_im) * inv
    ci = (ni * a_re - nr * a_im) * inv
    bbr = cr[..., None] * b_re - ci[..., None] * b_im
    bbi = cr[..., None] * b_im + ci[..., None] * b_re
    return abr, abi, bbr, bbi


VM = pl.BlockSpec(memory_space=pltpu.VMEM)


def s5_embed(bt_re, bt_im, ct_re, ct_im):
    def body(br, bi, cr, ci, b_ref, c_ref):
        b_ref[...] = jnp.zeros_like(b_ref)
        c_ref[...] = jnp.zeros_like(c_ref)
        for g in range(NG):
            rows, cols = slice(g * NH, (g + 1) * NH), slice(g * NP, (g + 1) * NP)
            b_ref[rows, cols] = br[g]
            b_ref[rows, NS + g * NP:NS + (g + 1) * NP] = bi[g]
            c_ref[cols, rows] = cr[g]
            c_ref[NS + g * NP:NS + (g + 1) * NP, rows] = -ci[g]

    return pl.pallas_call(
        body, name="s5_embed", in_specs=[VM] * 4, out_specs=[VM] * 2,
        out_shape=[jax.ShapeDtypeStruct((NG * NH, 2 * NS), F32), jax.ShapeDtypeStruct((2 * NS, NG * NH), F32)],
        compiler_params=pltpu.CompilerParams(vmem_limit_bytes=VMEM_LIMIT),
    )(bt_re, bt_im, ct_re, ct_im)


def s5_extract(gb, gc):
    def body(gb_ref, gc_ref, br, bi, cr, ci):
        for g in range(NG):
            rows, cols = slice(g * NH, (g + 1) * NH), slice(g * NP, (g + 1) * NP)
            br[g] = gb_ref[rows, cols]
            bi[g] = gb_ref[rows, NS + g * NP:NS + (g + 1) * NP]
            cr[g] = gc_ref[cols, rows]
            ci[g] = -gc_ref[NS + g * NP:NS + (g + 1) * NP, rows]

    return pl.pallas_call(
        body, name="s5_extract", in_specs=[VM] * 2, out_specs=[VM] * 4,
        out_shape=[jax.ShapeDtypeStruct((NG, NH, NP), F32)] * 2 + [jax.ShapeDtypeStruct((NG, NP, NH), F32)] * 2,
        compiler_params=pltpu.CompilerParams(vmem_limit_bytes=VMEM_LIMIT),
    )(gb, gc)


def _fwd_even(i, x, P, W):
    hn = rms_fwd(x, P["norm_ab"][i:i + 1], "rms_ab_fwd")
    z = mm(m2(hn), W["w_in"], "nn", "in_ab")
    o, lse, cat = attn_fwd(z)
    if "more" in W:
        W.update(W.pop("more")(cat))
    cat = pool_fwd(z, W["pool_w"], P["pool_scale"][i:i + 1], cat)
    x_mid = mm(m2(cat), W["w_out"], "nn", "out_ab", add=m2(x))
    return x_mid, dict(x=x, hn=hn, z=z, o=o, lse=lse, cat=cat)


def _bwd_even(i, dx_mid, sv, P, W, G, GW):
    z = sv["z"]
    d_cat = mm(m2(dx_mid), W["w_out"], "nt", "out_ab_dx")
    GW["w_out"] = mm(m2(sv["cat"]), m2(dx_mid), "tn", "out_ab_dw").reshape(4, 512, D)
    dq, dk, dv, dga = attn_bwd(z, d_cat, sv["o"], sv["lse"])
    dvb, dgb, dpw, dps = pool_bwd(z, d_cat, W["pool_w"], P["pool_scale"][i:i + 1])
    GW["pool_w"] = dpw.reshape(4, 4, 64, 256).transpose(1, 0, 2, 3).reshape(4, 256, 256)
    G["pool_scale"][i] = dps[0]
    d_z = assemble_dz_even((dq, dk, dv, dga, dvb, dgb))
    d_hn = mm(m2(d_z), W["w_in"], "nt", "in_ab_dx")
    GW["w_in"] = mm(m2(sv["hn"]), m2(d_z), "tn", "in_ab_dw", out=outcs(D, 1536))
    return d_hn, P["norm_ab"][i:i + 1], "norm_ab", "rms_ab_bwd"


def _fwd_odd(i, x, P, W):
    hn = rms_fwd(x, P["norm_cd"][i:i + 1], "rms_cd_fwd")
    z = mm(m2(hn), W["w_in"], "nn", "in_cd")
    bfull = jnp.repeat(P["sgu_b"][i].T, 256, axis=1)
    c_out = sgu_fwd(z, P["sgu_ln_g"][i:i + 1], P["sgu_ln_b"][i:i + 1], P["sgu_w"][i], bfull)
    disc, disc_vjp = jax.vjp(_s5_disc, P["s5_a_re"][i], P["s5_a_im"][i], P["s5_log_dt"][i],
                             P["s5_b_re"][i], P["s5_b_im"][i])
    abr, abi, bbr, bbi = disc
    bbd, cbd = s5_embed(bbr.transpose(0, 2, 1), bbi.transpose(0, 2, 1),
                        P["s5_c_re"][i].transpose(0, 2, 1), P["s5_c_im"][i].transpose(0, 2, 1))
    abr, abi = abr.reshape(1, NS), abi.reshape(1, NS)
    bu = mm(m2(z, 3072, 512), m2(bbd), "nn", "s5_bu")
    h = scan_fwd(bu, abr, abi)
    hc = mm(m2(h), m2(cbd), "nn", "s5_hc")
    dskip = P["s5_d"][i:i + 1]
    ypre, yg = s5_post(hc, z, dskip)
    if "more" in W:
        W.update(W.pop("more")(yg))
    w12 = W["w12"]
    t = mm(m2(yg), m2(w12), "nn", "glu_t")
    cat = glu_fwd(t, z, c_out)
    x_mid = mm(m2(cat), W["w_out"], "nn", "out_cd", add=m2(x))
    return x_mid, dict(x=x, hn=hn, z=z, bfull=bfull, disc_vjp=disc_vjp, bbd=bbd, cbd=cbd, abr=abr,
                       abi=abi, h=h, ypre=ypre, yg=yg, w12=w12, t=t, cat=cat, dskip=dskip)


def _bwd_odd(i, dx_mid, sv, P, W, G, GW):
    z = sv["z"]
    d_cat = mm(m2(dx_mid), W["w_out"], "nt", "out_cd_dx")
    GW["w_out"] = mm(m2(sv["cat"]), m2(dx_mid), "tn", "out_cd_dw").reshape(4, 384, D)
    du, dv, dgc, dws, dbs, dlg, dlb = sgu_bwd(z, d_cat, P["sgu_ln_g"][i:i + 1], P["sgu_ln_b"][i:i + 1],
                                               P["sgu_w"][i], sv["bfull"])
    G["sgu_w"][i], G["sgu_b"][i] = dws, dbs[:, :4].T
    G["sgu_ln_g"][i], G["sgu_ln_b"][i] = dlg[0], dlb[0]
    dt, dgd = glu_bwd(sv["t"], z, d_cat)
    gw12 = mm(m2(sv["yg"]), m2(dt), "tn", "glu_dw")
    GW["glu_w1"] = gw12[:, :512].reshape(4, 128, 512)
    GW["glu_w2"] = gw12[:, 512:].reshape(4, 128, 512)
    dyg = mm(m2(dt), m2(sv["w12"]), "nt", "glu_dx")
    dypre, dxd1, dd = s5_post_bwd(dyg, sv["ypre"], z, sv["dskip"])
    G["s5_d"][i] = dd[0]
    gcbd = mm(m2(sv["h"]), m2(dypre), "tn", "s5_dc")
    eta = mm(m2(dypre), m2(sv["cbd"]), "nt", "s5_eta")
    lam, dacc = scan_bwd(eta, sv["h"], sv["abr"], sv["abi"])
    gbbd = mm(m2(z, 3072, 512), m2(lam), "tn", "s5_db")
    dxd = mm(m2(lam), m2(sv["bbd"]), "nt", "s5_dx", add=m2(dxd1))
    dacc = jnp.sum(dacc, axis=0)
    dbt_re, dbt_im, dct_re, dct_im = s5_extract(gbbd, gcbd)
    G["s5_c_re"][i], G["s5_c_im"][i] = dct_re.transpose(0, 2, 1), dct_im.transpose(0, 2, 1)
    d_bbr, d_bbi = dbt_re.transpose(0, 2, 1), dbt_im.transpose(0, 2, 1)
    (G["s5_a_re"][i], G["s5_a_im"][i], G["s5_log_dt"][i], G["s5_b_re"][i], G["s5_b_im"][i]) = sv["disc_vjp"](
        (dacc[:NS].reshape(NG, NP), dacc[NS:].reshape(NG, NP), d_bbr, d_bbi))
    d_z = assemble_dz_odd(du, dv, dgc, dxd, dgd)
    d_hn = mm(m2(d_z), W["w_in"], "nt", "in_cd_dx")
    GW["w_in"] = mm(m2(sv["hn"]), m2(d_z), "tn", "in_cd_dw", out=outcs(D, 1024))
    return d_hn, P["norm_cd"][i:i + 1], "norm_cd", "rms_cd_bwd"


def _fwd_x(l, x, mem_n, P, W):
    hx = rms_fwd(x, P["norm_x"][l:l + 1], "rms_x_fwd")
    q = mm(m2(hx), W["w_xq"], "nn", "xq", out_dtype=BF16)
    kv = mm(m2(mem_n), W["w_xkv"], "nn", "xkv", out_dtype=BF16)
    ox = xattn_fwd(q, kv)
    x_out = mm(m2(ox), W["w_xo"], "nn", "xo", add=m2(x))
    return x_out, dict(x=x, hx=hx, q=q, kv=kv, ox=ox)


def _bwd_x(l, dx_out, sv, mem_n, d_memn, P, W, G, GW):
    d_ox = mm(m2(dx_out), W["w_xo"], "nt", "xo_dx", out_dtype=BF16)
    GW["w_xo"] = mm(m2(sv["ox"]), m2(dx_out), "tn", "xo_dw").reshape(4, 256, D)
    dq, dkv = xattn_bwd(sv["q"], sv["kv"], d_ox)
    GW["w_xq"] = mm(m2(sv["hx"]), m2(dq), "tn", "xq_dw").reshape(4, 256, D)
    d_hx = mm(m2(dq), W["w_xq"], "nt", "xq_dx")
    GW["w_xkv"] = mm(m2(mem_n), m2(dkv), "tn", "xkv_dw", out=outcs(D, 512))
    d_memn = mm(m2(dkv), W["w_xkv"], "nt", "xkv_dx", add=None if d_memn is None else m2(d_memn))
    dx, dg = rms_bwd(sv["x"], d_hx, dx_out, P["norm_x"][l:l + 1], "rms_x_bwd")
    G["norm_x"][l] = dg[0]
    return dx, d_memn


SMALL_LAYERS = (("norm_ab", 2), ("pool_scale", 2), ("norm_cd", 2), ("sgu_ln_g", 2), ("sgu_ln_b", 2), ("sgu_w", 2),
                ("sgu_b", 2), ("s5_a_re", 2), ("s5_a_im", 2), ("s5_log_dt", 2), ("s5_b_re", 2), ("s5_b_im", 2),
                ("s5_c_re", 2), ("s5_c_im", 2), ("s5_d", 2), ("norm_x", 4))


def local_step(x, mem, tgt, P, weights_of, grads_done):
    G = {k: [None] * n for k, n in SMALL_LAYERS}
    mem_g = P["mem_norm"].reshape(1, D)
    mem_n = rms_fwd(mem, mem_g, "rms_mem_fwd")
    saved = []
    for layer in range(4):
        i = layer // 2
        W = weights_of(layer, x)
        x, sv_m = (_fwd_even if layer % 2 == 0 else _fwd_odd)(i, x, P, W)
        x, sv_x = _fwd_x(layer, x, mem_n, P, W)
        saved.append((sv_m, sv_x, W))
    dx, loss, dgf = final_loss(x, tgt, P["final_norm"].reshape(1, D))
    G["final_norm"] = dgf[0]
    d_memn = None
    for layer in reversed(range(4)):
        i = layer // 2
        sv_m, sv_x, W = saved[layer]
        GW = {}
        dx_mid, d_memn = _bwd_x(layer, dx, sv_x, mem_n, d_memn, P, W, G, GW)
        d_hn, g, key, name = (_bwd_even if layer % 2 == 0 else _bwd_odd)(i, dx_mid, sv_m, P, W, G, GW)
        token = grads_done(layer, GW)
        if token is not None:
            g = g + token
        dx, dg = rms_bwd(sv_m["x"], d_hn, dx_mid, g, name)
        G[key][i] = dg[0]
    _, dgm = rms_bwd(mem, d_memn, d_memn, mem_g, "rms_mem_bwd")
    G["mem_norm"] = dgm[0]
    return loss, dx, G


ANY = pl.BlockSpec(memory_space=pl.ANY)


def _place():
    x, y, c = lax.axis_index("x"), lax.axis_index("y"), lax.axis_index("c")
    chips = [(1 - x, y), (x, 1 - y), (1 - x, 1 - y)]
    return x, y, c, 2 * x + y, (x, y, 1 - c), chips


def _remote(src, dst, send, recv, k, dev):
    return pltpu.make_async_remote_copy(src_ref=src, dst_ref=dst, send_sem=send.at[k], recv_sem=recv.at[k],
                                        device_id=dev, device_id_type=MESHID)


HBM = pl.BlockSpec(memory_space=pltpu.HBM)
SEM = pl.BlockSpec(memory_space=pltpu.SEMAPHORE)
EFFECT = pltpu.SideEffectType.DATAFLOW_SIDE_EFFECTING


def _hbm(t):
    return pltpu.with_memory_space_constraint(t, pltpu.HBM)


def allgather_sync(shards):
    n = len(shards)

    def body(*refs):
        ins, outs = refs[:n], refs[n:2 * n]
        token, send, recv = refs[2 * n:]
        x, y, c, jme, sib, chips = _place()
        first, passed = [], []
        for a in range(n):
            cp = _remote(ins[a], outs[a].at[jme], send, recv, a * 7 + 6, sib)
            cp.start()
            first.append(cp)
            for k, chip in enumerate(chips):
                cp = _remote(ins[a].at[c], outs[a].at[jme, c], send, recv, a * 7 + k, (*chip, c))
                cp.start()
                first.append(cp)
        for a in range(n):
            for k, chip in enumerate(chips):
                piece = outs[a].at[2 * chip[0] + chip[1], c]
                _remote(piece, piece, send, recv, a * 7 + k, (*chip, c)).wait_recv()
                fw = _remote(piece, piece, send, recv, a * 7 + 3 + k, sib)
                fw.start()
                passed.append(fw)
        for a in range(n):
            own = outs[a].at[jme]
            _remote(own, own, send, recv, a * 7 + 6, sib).wait_recv()
            for k, chip in enumerate(chips):
                piece = outs[a].at[2 * chip[0] + chip[1], 1 - c]
                _remote(piece, piece, send, recv, a * 7 + 3 + k, sib).wait_recv()
        for cp in first + passed:
            cp.wait_send()
        token[...] = jnp.zeros_like(token)

    res = pl.pallas_call(
        body, name="allgather_sync", in_specs=[ANY] * n,
        out_specs=[ANY] * n + [pl.BlockSpec(memory_space=pltpu.VMEM)],
        out_shape=[jax.ShapeDtypeStruct((4,) + s.shape, s.dtype) for s in shards] + [jax.ShapeDtypeStruct((8, 128), F32)],
        scratch_shapes=[pltpu.SemaphoreType.DMA((7 * n,)), pltpu.SemaphoreType.DMA((7 * n,))],
    )(*shards)
    return list(res[:n]), res[n]


def _gather_copies(ins, lands, send, recv):
    x, y, c, jme, sib, chips = _place()
    devs = [(*chip, c) for chip in chips] + [sib]
    return [_remote(ins[a], lands[a].at[jme], send, recv, a * 4 + k, dev)
            for a in range(len(ins)) for k, dev in enumerate(devs)]


def allgather_start(shards, after, name):
    n, na = len(shards), len(after)

    def body(*refs):
        ins, lands = refs[:n], refs[n:2 * n]
        send, recv = refs[2 * n + na], refs[2 * n + na + 1]
        token = refs[-1]
        for cp in _gather_copies(ins, lands, send, recv):
            cp.start()
        token[...] = jnp.zeros_like(token)

    res = pl.pallas_call(
        body, name=name,
        out_shape=(pltpu.SemaphoreType.DMA((4 * n,)), pltpu.SemaphoreType.DMA((4 * n,)),
                   *[pltpu.HBM(s.shape, s.dtype) for s in shards],
                   *[pltpu.HBM((4,) + s.shape, s.dtype) for s in shards],
                   jax.ShapeDtypeStruct((8, 128), F32)),
        in_specs=[HBM] * (2 * n) + [ANY] * na,
        out_specs=(SEM, SEM, *[HBM] * (2 * n), pl.BlockSpec(memory_space=pltpu.VMEM)),
        input_output_aliases={a: 2 + a for a in range(2 * n)},
        compiler_params=pltpu.CompilerParams(has_side_effects=EFFECT),
    )(*[_hbm(s) for s in shards], *[_hbm(lax.empty((4,) + s.shape, s.dtype)) for s in shards], *after)
    return res[0], res[1], list(res[2:2 + n]), list(res[2 + n:2 + 2 * n]), res[-1]


def allgather_wait(send, recv, shards, lands, after, name):
    n = len(shards)

    def body(*refs):
        ins, zones = refs[:n], refs[n:2 * n]
        send_r, recv_r = refs[2 * n], refs[2 * n + 1]
        x, y, c, jme, sib, chips = _place()
        slots = [2 * chip[0] + chip[1] for chip in chips] + [jme]
        for a in range(n):
            for k, slot in enumerate(slots):
                cp = _remote(ins[a], zones[a].at[slot], send_r, recv_r, a * 4 + k, sib)
                cp.wait_send()
                cp.wait_recv()

    res = pl.pallas_call(
        body, name=name,
        out_shape=tuple(pltpu.HBM(t.shape, t.dtype) for t in list(shards) + list(lands)),
        in_specs=[HBM] * (2 * n) + [SEM, SEM, ANY], out_specs=tuple([HBM] * (2 * n)),
        input_output_aliases={a: a for a in range(2 * n)},
        compiler_params=pltpu.CompilerParams(has_side_effects=EFFECT),
    )(*shards, *lands, send, recv, after)
    return list(res[n:])


def allgather_small(slab):
    def body(in_ref, out_ref, send, recv, lsem):
        x, y, c, jme, sib, chips = _place()
        loc = pltpu.make_async_copy(in_ref, out_ref.at[jme], lsem.at[0])
        loc.start()
        cps = [_remote(in_ref, out_ref.at[jme], send, recv, k, (*chip, c)) for k, chip in enumerate(chips)]
        for cp in cps:
            cp.start()
        for k, chip in enumerate(chips):
            piece = out_ref.at[2 * chip[0] + chip[1]]
            _remote(piece, piece, send, recv, k, (*chip, c)).wait_recv()
        for cp in cps:
            cp.wait_send()
        loc.wait()

    return pl.pallas_call(
        body, name="allgather_small", in_specs=[ANY], out_specs=ANY,
        out_shape=jax.ShapeDtypeStruct((4,) + slab.shape, slab.dtype),
        scratch_shapes=[pltpu.SemaphoreType.DMA((3,)), pltpu.SemaphoreType.DMA((3,)), pltpu.SemaphoreType.DMA((1,))],
    )(slab)


def allreduce_small(v):
    def body(v_ref, o_ref, r0, r1, r2, send, recv):
        x, y, c, jme, sib, chips = _place()
        peers = [sib, (1 - x, y, c), (x, 1 - y, c)]
        o_ref[...] = v_ref[...]
        for k, buf in enumerate((r0, r1, r2)):
            cp = _remote(o_ref, buf, send, recv, k, peers[k])
            cp.start()
            cp.wait()
            o_ref[...] = o_ref[...] + buf[...]

    vm = pl.BlockSpec(memory_space=pltpu.VMEM)
    return pl.pallas_call(
        body, name="allreduce_small", in_specs=[vm], out_specs=vm,
        out_shape=jax.ShapeDtypeStruct(v.shape, v.dtype),
        scratch_shapes=[pltpu.VMEM(v.shape, v.dtype)] * 3 + [pltpu.SemaphoreType.DMA((3,)), pltpu.SemaphoreType.DMA((3,))],
        compiler_params=pltpu.CompilerParams(vmem_limit_bytes=VMEM_LIMIT),
    )(v)


def _pair_copies(gs, lands, send, recv):
    x, y, c, jme, sib, chips = _place()
    return [_remote(gs[a].at[:, 1 - c], lands[a], send, recv, a, sib) for a in range(len(gs))]


def rs_pair_start(gs, name):
    n = len(gs)

    def body(*refs):
        ins, lands = refs[:n], refs[n:2 * n]
        send, recv = refs[2 * n], refs[2 * n + 1]
        token = refs[-1]
        for cp in _pair_copies(ins, lands, send, recv):
            cp.start()
        token[...] = jnp.zeros_like(token)

    shapes = [(4,) + g.shape[2:] for g in gs]
    res = pl.pallas_call(
        body, name=name,
        out_shape=(pltpu.SemaphoreType.DMA((n,)), pltpu.SemaphoreType.DMA((n,)),
                   *[pltpu.HBM(g.shape, g.dtype) for g in gs], *[pltpu.HBM(s, F32) for s in shapes],
                   jax.ShapeDtypeStruct((8, 128), F32)),
        in_specs=[HBM] * (2 * n), out_specs=(SEM, SEM, *[HBM] * (2 * n), pl.BlockSpec(memory_space=pltpu.VMEM)),
        input_output_aliases={a: 2 + a for a in range(2 * n)},
        compiler_params=pltpu.CompilerParams(has_side_effects=EFFECT),
    )(*[_hbm(g) for g in gs], *[_hbm(lax.empty(s, F32)) for s in shapes])
    return res[0], res[1], list(res[2:2 + n]), list(res[2 + n:2 + 2 * n]), res[-1]


def rs_pair_wait(send, recv, gs, lands, after, name):
    n = len(gs)

    def body(*refs):
        ins, zones = refs[:n], refs[n:2 * n]
        for cp in _pair_copies(ins, zones, refs[2 * n], refs[2 * n + 1]):
            cp.wait_send()
            cp.wait_recv()

    res = pl.pallas_call(
        body, name=name,
        out_shape=tuple(pltpu.HBM(t.shape, t.dtype) for t in list(gs) + list(lands)),
        in_specs=[HBM] * (2 * n) + [SEM, SEM, ANY], out_specs=tuple([HBM] * (2 * n)),
        input_output_aliases={a: a for a in range(2 * n)},
        compiler_params=pltpu.CompilerParams(has_side_effects=EFFECT),
    )(*gs, *lands, send, recv, after)
    return list(res[:n]), list(res[n:])


def rs_pair_sum(g4, got, cidx):
    _, _, rh, cols = g4.shape
    tr = rh if rh <= 256 else 256

    def body(c_ref, a_ref, b_ref, o_ref):
        o_ref[...] = (a_ref[...] + b_ref[...]).astype(o_ref.dtype)

    return pl.pallas_call(
        body, name="rs_pair_sum",
        grid_spec=pltpu.PrefetchScalarGridSpec(
            num_scalar_prefetch=1, grid=(4, rh // tr),
            in_specs=[pl.BlockSpec((None, None, tr, cols), lambda j, t, cr: (j, cr[0], t, 0)),
                      pl.BlockSpec((None, tr, cols), lambda j, t, cr: (j, t, 0))],
            out_specs=pl.BlockSpec((None, tr, cols), lambda j, t, cr: (j, t, 0))),
        out_shape=jax.ShapeDtypeStruct((4, rh, cols), BF16),
        compiler_params=_cparams(("parallel", "parallel")),
    )(cidx, g4, got)


def _chip_copies(ps, lands, send, recv):
    x, y, c, jme, sib, chips = _place()
    return [_remote(ps[a].at[2 * chip[0] + chip[1]], lands[a].at[jme], send, recv, a * 3 + k, (*chip, c))
            for a in range(len(ps)) for k, chip in enumerate(chips)]


def rs_chip_start(ps, name):
    n = len(ps)

    def body(*refs):
        ins, lands = refs[:n], refs[n:2 * n]
        send, recv = refs[2 * n], refs[2 * n + 1]
        token = refs[-1]
        for cp in _chip_copies(ins, lands, send, recv):
            cp.start()
        token[...] = jnp.zeros_like(token)

    res = pl.pallas_call(
        body, name=name,
        out_shape=(pltpu.SemaphoreType.DMA((3 * n,)), pltpu.SemaphoreType.DMA((3 * n,)),
                   *[pltpu.HBM(p.shape, p.dtype) for p in ps], *[pltpu.HBM(p.shape, p.dtype) for p in ps],
                   jax.ShapeDtypeStruct((8, 128), F32)),
        in_specs=[HBM] * (2 * n), out_specs=(SEM, SEM, *[HBM] * (2 * n), pl.BlockSpec(memory_space=pltpu.VMEM)),
        input_output_aliases={a: 2 + a for a in range(2 * n)},
        compiler_params=pltpu.CompilerParams(has_side_effects=EFFECT),
    )(*[_hbm(p) for p in ps], *[_hbm(lax.empty(p.shape, p.dtype)) for p in ps])
    return res[0], res[1], list(res[2:2 + n]), list(res[2 + n:2 + 2 * n]), res[-1]


def rs_chip_wait(send, recv, ps, lands, after, name):
    n = len(ps)

    def body(*refs):
        ins, zones = refs[:n], refs[n:2 * n]
        send_r, recv_r = refs[2 * n], refs[2 * n + 1]
        x, y, c, jme, sib, chips = _place()
        for a in range(n):
            for k, chip in enumerate(chips):
                jt = 2 * chip[0] + chip[1]
                cp = _remote(ins[a].at[jt], zones[a].at[jt], send_r, recv_r, a * 3 + k, (*chip, c))
                cp.wait_send()
                cp.wait_recv()

    res = pl.pallas_call(
        body, name=name,
        out_shape=tuple(pltpu.HBM(p.shape, p.dtype) for p in list(ps) + list(lands)),
        in_specs=[HBM] * (2 * n) + [SEM, SEM] + [ANY] * len(after), out_specs=tuple([HBM] * (2 * n)),
        input_output_aliases={a: a for a in range(2 * n)},
        compiler_params=pltpu.CompilerParams(has_side_effects=EFFECT),
    )(*ps, *lands, send, recv, *after)
    return list(res[n:])


def rs_chip_sum(q, p, l, acc, layers, jc):
    _, rh, cols = q.shape
    tr = rh if rh <= 256 else 256

    def body(jc_ref, q_ref, p_ref, *rest):
        o_ref = rest[-1]
        jme = jc_ref[0]
        own = p_ref[...].astype(F32)
        v = [jnp.where(jme == j, own, q_ref[j].astype(F32)) for j in range(4)]
        o_ref[...] = ((v[0] + v[1]) + v[2]) + v[3]

    in_specs = [pl.BlockSpec((4, tr, cols), lambda t, jr: (0, t, 0)),
                pl.BlockSpec((None, tr, cols), lambda t, jr: (jr[0], t, 0))]
    args = [jc, q, p]
    if acc is not None:
        in_specs.append(ANY)
        args.append(acc)
    return pl.pallas_call(
        body, name="rs_chip_sum",
        grid_spec=pltpu.PrefetchScalarGridSpec(
            num_scalar_prefetch=1, grid=(rh // tr,), in_specs=in_specs,
            out_specs=pl.BlockSpec((None, None, tr, cols), lambda t, jr: (l, jr[1], t, 0))),
        out_shape=jax.ShapeDtypeStruct((layers, 2, rh, cols), F32),
        input_output_aliases={} if acc is None else {3: 0},
        compiler_params=_cparams(("parallel",)),
    )(*args)


def rs_pair_gather(rs):
    n = len(rs)

    def body(*refs):
        outs = refs[n:2 * n]
        send, recv = refs[2 * n:]
        x, y, c, jme, sib, chips = _place()
        cps = [_remote(outs[a].at[:, c], outs[a].at[:, c], send, recv, a, sib) for a in range(n)]
        for cp in cps:
            cp.start()
        for a in range(n):
            slot = outs[a].at[:, 1 - c]
            _remote(slot, slot, send, recv, a, sib).wait_recv()
        for cp in cps:
            cp.wait_send()

    return pl.pallas_call(
        body, name="rs_pair_gather", in_specs=[ANY] * n, out_specs=[ANY] * n,
        out_shape=[jax.ShapeDtypeStruct(r.shape, r.dtype) for r in rs],
        input_output_aliases={a: a for a in range(n)},
        scratch_shapes=[pltpu.SemaphoreType.DMA((n,)), pltpu.SemaphoreType.DMA((n,))],
    )(*rs)


def _adamw_math(w, g, m, v):
    m = B1 * m + (1.0 - B1) * g
    v = B2 * v + (1.0 - B2) * (g * g)
    m_hat = m / (1.0 - B1 ** STEP)
    v_hat = v / (1.0 - B2 ** STEP)
    return -LR * (m_hat / (jnp.sqrt(v_hat) + AEPS) + WD * w), m, v


def adamw(w, g, m, v, name, with_grad=False):
    rows, cols = w.shape
    tr = 256 if rows % 256 == 0 else rows
    fn = (lambda wv, gv, mv, vv: (gv,) + _adamw_math(wv, gv, mv, vv)) if with_grad else _adamw_math
    return rw(fn, [(a, 0, cols) for a in (w, g, m, v)], [(cols, F32)] * (4 if with_grad else 3), name, rows, tr=tr)


WEIGHTS = ["norm_ab", "w_in_ab", "pool_w", "pool_scale", "w_out_ab", "norm_cd", "w_in_cd", "sgu_ln_g", "sgu_ln_b",
           "sgu_w", "sgu_b", "s5_a_re", "s5_a_im", "s5_log_dt", "s5_b_re", "s5_b_im", "s5_c_re", "s5_c_im", "s5_d",
           "glu_w1", "glu_w2", "w_out_cd", "norm_x", "w_xq", "w_xkv", "w_xo", "mem_norm", "final_norm"]
INPUTS = ["x", "mem"] + WEIGHTS + ["loss_target"] + ["m_" + n for n in WEIGHTS] + ["v_" + n for n in WEIGHTS]
BIG = ["w_in_ab", "w_out_ab", "w_in_cd", "w_out_cd", "w_xq", "w_xkv", "w_xo", "glu_w1", "glu_w2", "pool_w"]
COL_SHARDED = ("w_in_ab", "w_in_cd", "w_xkv")
SMALL = [n for n in WEIGHTS if n not in BIG]
SMALL_SHARDED = {"norm_cd": 256, "sgu_ln_g": 256, "sgu_ln_b": 256, "s5_d": 128}
PACK = 256 * 128


def _pack(arrs):
    flat = jnp.concatenate([a.reshape(-1) for a in arrs])
    pad = (-flat.shape[0]) % PACK
    return jnp.concatenate([flat, jnp.zeros((pad,), flat.dtype)]).reshape(-1, 128)


def _unpack(packed, shapes):
    flat, out, off = packed.reshape(-1), [], 0
    for s in shapes:
        n = 1
        for d in s:
            n *= d
        out.append(flat[off:off + n].reshape(s))
        off += n
    return out


LAYER_KEYS = (("w_in", "w_out", "pool_w", "w_xq", "w_xkv", "w_xo"),
              ("w_in", "w_out", "glu_w1", "glu_w2", "w_xq", "w_xkv", "w_xo"))


def _weight_of(key, layer):
    if key in ("w_xq", "w_xkv", "w_xo"):
        return key, layer, 4
    kind = "ab" if layer % 2 == 0 else "cd"
    return {"w_in": "w_in_" + kind, "w_out": "w_out_" + kind}.get(key, key), layer // 2, 2


def kernel(*args):
    a = dict(zip(INPUTS, args))
    x_i, y_i, c_i = lax.axis_index("x"), lax.axis_index("y"), lax.axis_index("c")
    j = 2 * x_i + y_i

    slab = jnp.concatenate([a["norm_cd"], a["sgu_ln_g"], a["sgu_ln_b"],
                            jnp.pad(a["s5_d"], ((0, 0), (0, 128)))], axis=0)
    gslab = allgather_small(slab)
    P = {n: a[n] for n in SMALL}
    for k, n in enumerate(("norm_cd", "sgu_ln_g", "sgu_ln_b", "s5_d")):
        wd = SMALL_SHARDED[n]
        P[n] = gslab[:, 2 * k:2 * k + 2, :wd].transpose(1, 0, 2).reshape(2, 4 * wd)

    def shards_of(layer):
        keys = sorted(k for k in LAYER_KEYS[layer % 2])
        out = []
        for k in keys:
            n, l, _ = _weight_of(k, layer)
            out.append(a[n][l].reshape(-1, a[n].shape[-1]).astype(BF16))
        return keys, out

    keys0, sh0 = shards_of(0)
    first = keys0.index("w_in")
    g_in, token = allgather_sync([sh0[first].reshape(2, sh0[first].shape[0] // 2, sh0[first].shape[1])])
    w_in0 = g_in[0].reshape(4, -1, g_in[0].shape[-1])
    started = {}
    for layer in (0, 1, 2, 3):
        keys, sh = (keys0, sh0) if layer == 0 else shards_of(layer)
        rest = [(k, s) for k, s in zip(keys, sh) if k != "w_in"]
        parts = [("in", ["w_in"], [sh[keys.index("w_in")]])] * (layer > 0) + [("", *map(list, zip(*rest)))]
        for tag, pk, ps in parts:
            send, recv, ps, lands, token = allgather_start(ps, [token, gslab], "allgather_start_%d%s" % (layer, tag))
            started[(layer, tag)] = (pk, send, recv, ps, lands)
    P["norm_ab"] = P["norm_ab"] + token[0:1, 0:1]

    cidx = jnp.reshape(c_i, (1,)).astype(jnp.int32)
    jc = jnp.stack([j, c_i]).astype(jnp.int32)

    def views(g):
        W = {}
        for k, v in g.items():
            if k in ("w_in", "w_xkv"):
                W[k] = mcs(v)
            elif k == "pool_w":
                W[k] = v.reshape(4, 4, 64, 256).transpose(1, 0, 2, 3).reshape(4, 256, 256)
            elif k not in ("glu_w1", "glu_w2"):
                W[k] = m2(v.reshape(-1, v.shape[-1]))
        if "glu_w1" in g:
            W["w12"] = jnp.concatenate([g["glu_w1"].reshape(512, 512), g["glu_w2"].reshape(512, 512)], axis=1)
        return W

    def arrived(layer, tag, after):
        keys, send, recv, sh, lands = started[(layer, tag)]
        return views(dict(zip(keys, allgather_wait(send, recv, sh, lands, after, "allgather_wait_%d%s" % (layer, tag)))))

    def weights_of(layer, x_in):
        W = views({"w_in": w_in0}) if layer == 0 else arrived(layer, "in", x_in)
        W["more"] = lambda after: arrived(layer, "", after)
        return W

    halves, pending = {}, {}

    def finish_pair(layer, after):
        keys, send, recv, flat, lands = halves.pop(layer)
        flat, got = rs_pair_wait(send, recv, flat, lands, after, "rs_pair_wait_%d" % layer)
        pair = [rs_pair_sum(g4, r, cidx) for g4, r in zip(flat, got)]
        send, recv, pair, lands, token = rs_chip_start(pair, "rs_chip_start_%d" % layer)
        pending[layer] = (keys, send, recv, pair, lands)
        return token

    def grads_done(layer, GW):
        keys = sorted(GW)
        flat = [GW[k].reshape(4, 2, GW[k].shape[1] // 2, GW[k].shape[2]) for k in keys]
        send, recv, flat, lands, token = rs_pair_start(flat, "rs_pair_start_%d" % layer)
        halves[layer] = (keys, send, recv, flat, lands)
        if layer + 1 in halves:
            token = token + finish_pair(layer + 1, token)
        return token[0:1, 0:1]

    loss, dx, G = local_step(a["x"][0], a["mem"][0], a["loss_target"][0], P, weights_of, grads_done)
    loss = lax.psum(loss[0, 0], ("x", "y", "c"))
    finish_pair(0, dx)
    outs = {}

    def update_big(names, red):
        for n, g in zip(names, rs_pair_gather([red[n] for n in names])):
            shp = a[n].shape
            g2 = g.reshape(-1, shp[-1])
            upd = adamw(a[n].reshape(g2.shape), g2, a["m_" + n].reshape(g2.shape), a["v_" + n].reshape(g2.shape),
                        "adamw_" + n, with_grad=True)
            outs[n] = tuple(t.reshape(shp) for t in upd)

    def reduce_layer(layer, red, after):
        keys, send, recv, pair, lands = pending[layer]
        lands = rs_chip_wait(send, recv, pair, lands, after, "rs_chip_wait_%d" % layer)
        for k, q, p in zip(keys, lands, pair):
            n, l, layers = _weight_of(k, layer)
            red[n] = rs_chip_sum(q, p, l, red.get(n), layers, jc)

    red = {}
    for layer in (3, 2, 1):
        reduce_layer(layer, red, [dx])
    odd_only = [n for n in BIG if n.endswith("_cd") or n.startswith("glu")]
    update_big(odd_only, red)

    gfull = [jnp.stack(G[n]) if isinstance(G[n], list) else G[n] for n in SMALL]
    shapes = [g.shape for g in gfull]
    gsum = _unpack(allreduce_small(_pack(gfull)), shapes)
    gloc = []
    for n, g in zip(SMALL, gsum):
        if n in SMALL_SHARDED:
            g = lax.dynamic_slice_in_dim(g, j * SMALL_SHARDED[n], SMALL_SHARDED[n], axis=1)
        gloc.append(g)
    for n, g in zip(SMALL, gloc):
        shp = a[n].shape
        two = (-1, shp[-1]) if len(shp) > 1 else (1, shp[0])
        upd = adamw(a[n].reshape(two), g.reshape(two), a["m_" + n].reshape(two), a["v_" + n].reshape(two), "adamw_" + n)
        outs[n] = (g,) + tuple(t.reshape(shp) for t in upd)

    behind = [outs[n][1] for n in odd_only + SMALL[-1:]] + [red[n] for n in BIG if n not in odd_only]
    reduce_layer(0, red, behind)
    update_big([n for n in BIG if n not in odd_only], red)

    res = [loss, dx[None]]
    for part in range(4):
        res += [outs[n][part] for n in WEIGHTS]
    return tuple(res)
```

```python
import math

import jax
import jax.numpy as jnp
from jax import lax
from jax.experimental import pallas as pl
from jax.experimental.pallas import tpu as pltpu

F32, BF16 = jnp.float32, jnp.bfloat16
S, D = 2048, 1024
MEM = 256
EPS = 1e-6
NEG = -1e30
QB = 128
PATTERNS = (1, 4, 16)
NG, NP, NH = 32, 64, 16
NS = NG * NP
LR, B1, B2, AEPS, WD, STEP = 0.001, 0.9, 0.999, 1e-08, 0.01, 10
MESHID = pl.DeviceIdType.MESH
VMEM_LIMIT = 56 * 1024 * 1024


def _cparams(sem):
    return pltpu.CompilerParams(dimension_semantics=sem, vmem_limit_bytes=VMEM_LIMIT)


def _sig(x):
    return 1.0 / (1.0 + jnp.exp(-x))


def _dot(a, b, dims):
    return lax.dot_general(a, b, (dims, ((), ())), preferred_element_type=F32)


def _nn(a, b):
    return _dot(a, b, ((1,), (0,)))


def _nt(a, b):
    return _dot(a, b, ((1,), (1,)))


def _tn(a, b):
    return _dot(a, b, ((0,), (0,)))


_DIMS = {"nn": ((1,), (0,)), "nt": ((1,), (1,)), "tn": ((0,), (0,))}


def _tile(dim, cc=None, cap=1024):
    for t in (2048, 1536, 1024, 768, 512, 384, 256, 128):
        if t <= cap and dim % t == 0 and (cc is None or cc % t == 0):
            return t
    return dim


MM_VMEM = 36 * 1024 * 1024
MM_STEPS = 8


def _mm_tiles(m, n, k, ccm, ccn, cck, a_bytes, b_bytes, o_bytes):
    caps = [1024, 1024, 2048]
    while True:
        tm, tn, tk = _tile(m, ccm, caps[0]), _tile(n, ccn, caps[1]), _tile(k, cck, caps[2])
        need = 2 * (tm * tk * a_bytes + tk * tn * b_bytes + tm * tn * o_bytes) + (tm * tn * 4 if tk < k else 0)
        few = (m // tm) * (n // tn) * (k // tk) < MM_STEPS and max(tm, tn) > 512
        if need <= MM_VMEM and not few:
            return tm, tn, tk
        if need <= MM_VMEM:
            caps[1 if tn >= tm else 0] = max(tm, tn) // 2
            continue
        if tk > 1024:
            caps[2] = tk // 2
        elif tn >= tm:
            caps[1] = tn // 2
        else:
            caps[0] = tm // 2


def m2(arr, col_off=0, ncols=None):
    rows, cols = arr.shape
    ncols = cols - col_off if ncols is None else ncols

    def spec(tr, tc, rc):
        assert col_off % tc == 0
        return pl.BlockSpec((tr, tc), lambda *g: (rc(*g)[0], rc(*g)[1] + col_off // tc))
    return (arr, rows, ncols, spec, None if col_off == 0 else col_off)


def mcs(arr):
    cs = arr.shape[2]

    def spec(tr, tc, rc):
        n = cs // tc
        return pl.BlockSpec((None, tr, tc), lambda *g: (rc(*g)[1] // n, rc(*g)[0], rc(*g)[1] % n))
    return (arr, arr.shape[1], 4 * cs, spec, cs)


def out2(rows, cols):
    def spec(tr, tc, rc):
        return pl.BlockSpec((tr, tc), lambda *g: tuple(rc(*g)))
    return ((rows, cols), spec, None)


def outcs(rows, cs):
    def spec(tr, tc, rc):
        n = cs // tc
        return pl.BlockSpec((None, tr, tc), lambda *g: (rc(*g)[1] // n, rc(*g)[0], rc(*g)[1] % n))
    return ((4, rows, cs), spec, cs)


def _both(a, b):
    if a is None:
        return b
    if b is None:
        return a
    return math.gcd(a, b)


def mm(a, b, mode, name, add=None, out=None, out_dtype=F32):
    a_arr, a_r, a_c, a_spec, a_cc = a
    b_arr, b_r, b_c, b_spec, b_cc = b
    if mode == "nn":
        m, k, n = a_r, a_c, b_c
        assert b_r == k
        ccm, cck, ccn = None, a_cc, b_cc
    elif mode == "nt":
        m, k, n = a_r, a_c, b_r
        assert b_c == k
        ccm, cck, ccn = None, _both(a_cc, b_cc), None
    else:
        m, k, n = a_c, a_r, b_c
        assert b_r == k
        ccm, cck, ccn = a_cc, None, b_cc
    out = out2(m, n) if out is None else out
    o_shape, o_spec, o_cc = out
    ccn = _both(ccn, o_cc)
    if add is not None:
        ccn = _both(ccn, add[4])
    o_bytes = jnp.dtype(out_dtype).itemsize + (0 if add is None else add[0].dtype.itemsize)
    tm, tn, tk = _mm_tiles(m, n, k, ccm, ccn, cck, a_arr.dtype.itemsize, b_arr.dtype.itemsize, o_bytes)
    nk = k // tk
    if mode == "nn":
        in_specs = [a_spec(tm, tk, lambda i, j, kk: (i, kk)), b_spec(tk, tn, lambda i, j, kk: (kk, j))]
    elif mode == "nt":
        in_specs = [a_spec(tm, tk, lambda i, j, kk: (i, kk)), b_spec(tn, tk, lambda i, j, kk: (j, kk))]
    else:
        in_specs = [a_spec(tk, tm, lambda i, j, kk: (kk, i)), b_spec(tk, tn, lambda i, j, kk: (kk, j))]
    args = [a_arr, b_arr]
    if add is not None:
        in_specs.append(add[3](tm, tn, lambda i, j, kk: (i, j)))
        args.append(add[0])
    return _mm_call(args, in_specs, o_spec(tm, tn, lambda i, j, kk: (i, j)), jax.ShapeDtypeStruct(o_shape, out_dtype),
                    mode, (m // tm, n // tn, nk), (tm, tn), add is not None, name)


def _mm_call(args, in_specs, out_spec, out_shape, mode, grid, tile, has_add, name):
    dims = _DIMS[mode]
    nk = grid[2]
    tm, tn = tile

    def body(*refs):
        a_ref, b_ref = refs[0], refs[1]
        add_ref = refs[2] if has_add else None
        prod = _dot(a_ref[...].astype(BF16), b_ref[...].astype(BF16), dims)
        if nk == 1:
            o_ref = refs[-1]
            if has_add:
                prod = prod + add_ref[...].astype(F32)
            o_ref[...] = prod.astype(o_ref.dtype)
            return
        o_ref, acc = refs[-2], refs[-1]
        kk = pl.program_id(2)

        @pl.when(kk == 0)
        def _():
            acc[...] = prod

        @pl.when(kk > 0)
        def _():
            acc[...] += prod

        @pl.when(kk == nk - 1)
        def _():
            r = acc[...]
            if has_add:
                r = r + add_ref[...].astype(F32)
            o_ref[...] = r.astype(o_ref.dtype)

    return pl.pallas_call(
        body, name=name, grid=grid, in_specs=in_specs, out_specs=out_spec, out_shape=out_shape,
        scratch_shapes=[pltpu.VMEM((tm, tn), F32)] if nk > 1 else [],
        compiler_params=_cparams(("parallel", "parallel", "arbitrary")),
    )(*args)


def mm_band(a, b, mode, name, grid, blocks, maps, out_shape, add=None, out_dtype=F32):
    in_specs = [pl.BlockSpec(blocks[0], maps[0]), pl.BlockSpec(blocks[1], maps[1])]
    args = [a, b]
    if add is not None:
        in_specs.append(pl.BlockSpec(blocks[2], maps[2]))
        args.append(add)
    return _mm_call(args, in_specs, pl.BlockSpec(blocks[2], maps[2]), jax.ShapeDtypeStruct(out_shape, out_dtype),
                    mode, grid, blocks[2], add is not None, name)


def rw(fn, ins, outs, name, rows, tr=256, consts=(), accs=()):
    n_in, n_c, n_o, n_a = len(ins), len(consts), len(outs), len(accs)
    in_specs = []
    for arr, off, width in ins:
        assert off % width == 0
        in_specs.append(pl.BlockSpec((tr, width), lambda i, o=off // width: (i, o)))
    for c in consts:
        in_specs.append(pl.BlockSpec(c.shape, lambda i: (0, 0)))
    out_specs = [pl.BlockSpec((tr, w), lambda i: (i, 0)) for w, _ in outs]
    out_specs += [pl.BlockSpec(s, lambda i: (0, 0)) for s in accs]
    out_shape = [jax.ShapeDtypeStruct((rows, w), dt) for w, dt in outs]
    out_shape += [jax.ShapeDtypeStruct(s, F32) for s in accs]

    def body(*refs):
        vals = [r[...] for r in refs[:n_in + n_c]]
        o_refs = refs[n_in + n_c:n_in + n_c + n_o]
        a_refs = refs[n_in + n_c + n_o:]
        res = fn(*vals)
        for r, v in zip(o_refs, res[:n_o]):
            r[...] = v.astype(r.dtype)
        if n_a:
            @pl.when(pl.program_id(0) == 0)
            def _():
                for r in a_refs:
                    r[...] = jnp.zeros_like(r)
            for r, v in zip(a_refs, res[n_o:]):
                r[...] += v

    res = pl.pallas_call(
        body, name=name, grid=(rows // tr,), in_specs=in_specs, out_specs=out_specs,
        out_shape=out_shape,
        compiler_params=_cparams(("arbitrary",) if n_a else ("parallel",)),
    )(*[a for a, _, _ in ins], *consts)
    return res


def _rstd(x):
    return lax.rsqrt(jnp.mean(x * x, axis=-1, keepdims=True) + EPS)


def rms_fwd(x, g, name):
    def fn(xv, gv):
        xv = xv.astype(F32)
        return (xv * _rstd(xv) * gv,)
    return rw(fn, [(x, 0, D)], [(D, BF16)], name, x.shape[0], consts=[g])[0]


def _rms_bwd_math(xv, dy, gv):
    r = _rstd(xv)
    dyg = dy * gv
    dx = r * dyg - xv * (r * r * r / D) * jnp.sum(dyg * xv, axis=-1, keepdims=True)
    dg = jnp.sum(dy * xv * r, axis=0, keepdims=True)
    return dx, dg


def rms_bwd(x, dy, dres, g, name):
    def fn(xv, dyv, drv, gv):
        dx, dg = _rms_bwd_math(xv, dyv, gv)
        return dx + drv, dg
    return rw(fn, [(x, 0, D), (dy, 0, D), (dres, 0, D)], [(D, F32)], name, x.shape[0],
              consts=[g], accs=[(1, D)])


def final_loss(x, tgt, g):
    def fn(xv, tv, gv):
        e = xv * _rstd(xv) * gv - tv
        loss = 0.5 * jnp.sum(jnp.sum(e * e, axis=-1, keepdims=True), axis=0, keepdims=True) / D
        dx, dg = _rms_bwd_math(xv, e / D, gv)
        return dx, loss, dg
    return rw(fn, [(x, 0, D), (tgt, 0, D)], [(D, F32)], "final_loss", S, consts=[g],
              accs=[(1, 1), (1, D)])


def _attn_bias(bias_ref):
    ii = lax.broadcasted_iota(jnp.int32, (2 * QB, 2 * QB), 0) % QB
    jj = lax.broadcasted_iota(jnp.int32, (2 * QB, 2 * QB), 1)
    dist = ii + QB - jj
    band = (dist >= 0) & (dist <= QB)
    bias_ref[1] = jnp.where(band, 0.0, NEG)
    bias_ref[0] = jnp.where(band & (jj >= QB), 0.0, NEG)


def _two_heads(x, m0):
    return jnp.concatenate([jnp.where(m0, x, 0.0), jnp.where(m0, 0.0, x)], axis=0)


def _per_head(col, m0):
    return jnp.where(m0, col[:QB], col[QB:])


def _attn_rows(idx, d):
    if d == 1:
        b = idx
        cur = pl.ds(pl.multiple_of(b * QB, QB), QB)
        prev = pl.ds(pl.multiple_of(jnp.maximum(b - 1, 0) * QB, QB), QB)
    else:
        r, b = lax.rem(idx, d), lax.div(idx, d)
        cur = pl.ds(r + b * (QB * d), QB, stride=d)
        prev = pl.ds(r + jnp.maximum(b - 1, 0) * (QB * d), QB, stride=d)
    return cur, prev, b


NBLK = S // QB
GROUP = 4


def _colblk(off):
    return pl.BlockSpec((S, 128), lambda hp: (0, off * 8 + hp))


def attn_fwd(z):
    def body(q_ref, k_ref, v_ref, g_ref, o_ref, l_ref, a_ref, os, ls, bias):
        _attn_bias(bias)
        m0 = lax.broadcasted_iota(jnp.int32, (1, 128), 1) < 64
        for pi, d in enumerate(PATTERNS):
            lone = S // d == QB

            def load(idx, d=d, lone=lone):
                cur, prev, b = _attn_rows(idx, d)
                if lone:
                    return cur, (q_ref[cur, :], None, k_ref[cur, :], None, v_ref[cur, :], bias[1, :, QB:])
                return cur, (q_ref[cur, :], k_ref[prev, :], k_ref[cur, :], v_ref[prev, :], v_ref[cur, :],
                             bias[jnp.minimum(b, 1)])

            def block(q, kp, kc, vp, vc, bs):
                qq = _two_heads(q * 0.125, m0).astype(BF16)
                k = (kc if kp is None else jnp.concatenate([kp, kc], axis=0)).astype(BF16)
                s = _nt(qq, k) + bs
                mx = jnp.max(s, axis=-1, keepdims=True)
                p = jnp.exp(s - mx)
                den = jnp.sum(p, axis=-1, keepdims=True)
                pb = p.astype(BF16)
                vv = _two_heads(vc if vp is None else jnp.concatenate([vp, vc], axis=0), m0).astype(BF16)
                o = _nn(jnp.concatenate([pb[:QB], pb[QB:]], axis=1), vv)
                return o * _per_head(1.0 / den, m0), _per_head(mx + jnp.log(den), m0)

            def step(i, carry, pi=pi):
                loaded = [load(i * GROUP + u) for u in range(GROUP)]
                done = [block(*vals) for _, vals in loaded]
                for (cur, _), (o, l) in zip(loaded, done):
                    os[pi, cur, :] = o
                    ls[pi, cur, :] = l
                return carry
            lax.fori_loop(0, NBLK // GROUP, step, 0)
        l1, l2, l3 = ls[0], ls[1], ls[2]
        mx = jnp.maximum(jnp.maximum(l1, l2), l3)
        e1, e2, e3 = jnp.exp(l1 - mx), jnp.exp(l2 - mx), jnp.exp(l3 - mx)
        tot = e1 + e2 + e3
        o = (os[0] * e1 + os[1] * e2 + os[2] * e3) / tot
        ga = g_ref[...]
        o_ref[...] = o
        l_ref[...] = mx + jnp.log(tot)
        a_ref[...] = (o * (ga * _sig(ga))).astype(a_ref.dtype)

    out = pl.BlockSpec((S, 128), lambda hp: (0, hp))
    return pl.pallas_call(
        body, name="attn_fwd", grid=(8,),
        in_specs=[_colblk(0), _colblk(1), _colblk(2), _colblk(3)], out_specs=[out] * 3,
        out_shape=[jax.ShapeDtypeStruct((S, D), F32), jax.ShapeDtypeStruct((S, D), F32),
                   jax.ShapeDtypeStruct((S, 2 * D), BF16)],
        scratch_shapes=[pltpu.VMEM((3, S, 128), F32), pltpu.VMEM((3, S, 128), F32),
                        pltpu.VMEM((2, 2 * QB, 2 * QB), F32)],
        compiler_params=_cparams(("parallel",)),
    )(z, z, z, z)


def attn_bwd(z, d_cat, o, lse):
    def body(q_ref, k_ref, v_ref, g_ref, da_ref, o_ref, l_ref, dq_ref, dk_ref, dv_ref, dg_ref, do_s, pr_s, bias):
        _attn_bias(bias)
        m0 = lax.broadcasted_iota(jnp.int32, (1, 128), 1) < 64
        ga = g_ref[...]
        sg = _sig(ga)
        da = da_ref[...]
        ov = o_ref[...]
        do = da * (ga * sg)
        dg_ref[...] = da * ov * (sg * (1.0 + ga * (1.0 - sg)))
        do_s[...] = do
        pr_s[...] = do * ov
        dq_ref[...] = jnp.zeros_like(dq_ref)
        dk_ref[...] = jnp.zeros_like(dk_ref)
        dv_ref[...] = jnp.zeros_like(dv_ref)
        for d in PATTERNS:
            lone = S // d == QB

            def load(idx, d=d, lone=lone):
                cur, prev, b = _attn_rows(idx, d)
                if lone:
                    return (cur, None), (q_ref[cur, :], None, k_ref[cur, :], None, v_ref[cur, :],
                                         do_s[cur, :], pr_s[cur, :], l_ref[cur, :], bias[1, :, QB:])
                return (cur, prev), (q_ref[cur, :], k_ref[prev, :], k_ref[cur, :], v_ref[prev, :], v_ref[cur, :],
                                     do_s[cur, :], pr_s[cur, :], l_ref[cur, :], bias[jnp.minimum(b, 1)])

            def block(q, kp, kc, vp, vc, dof, prod, lp, bs):
                qq = _two_heads(q * 0.125, m0).astype(BF16)
                kf = kc if kp is None else jnp.concatenate([kp, kc], axis=0)
                k = kf.astype(BF16)
                v = (vc if vp is None else jnp.concatenate([vp, vc], axis=0)).astype(BF16)
                dd = _two_heads(dof, m0).astype(BF16)
                lh = jnp.max(jnp.concatenate([jnp.where(m0, lp, -jnp.inf), jnp.where(m0, -jnp.inf, lp)], axis=0),
                             axis=-1, keepdims=True)
                delta = jnp.sum(_two_heads(prod, m0), axis=-1, keepdims=True)
                p = jnp.exp(_nt(qq, k) + bs - lh)
                ds = (p * (_nt(dd, v) - delta)).astype(BF16)
                dq = _nn(jnp.concatenate([ds[:QB], ds[QB:]], axis=1), _two_heads(kf, m0).astype(BF16))
                return dq * 0.125, _tn(ds, qq), _tn(p.astype(BF16), dd)

            def step(i, carry):
                loaded = [load(i * GROUP + u) for u in range(GROUP)]
                done = [block(*vals) for _, vals in loaded]
                for ((cur, prev), _), (dq, dk, dv) in zip(loaded, done):
                    dq_ref[cur, :] = dq_ref[cur, :] + dq
                    if prev is not None:
                        dk_ref[prev, :] = dk_ref[prev, :] + dk[:QB]
                        dv_ref[prev, :] = dv_ref[prev, :] + dv[:QB]
                    dk_ref[cur, :] = dk_ref[cur, :] + dk[-QB:]
                    dv_ref[cur, :] = dv_ref[cur, :] + dv[-QB:]
                return carry
            lax.fori_loop(0, NBLK // GROUP, step, 0)

    blk = pl.BlockSpec((S, 128), lambda hp: (0, hp))
    return pl.pallas_call(
        body, name="attn_bwd", grid=(8,),
        in_specs=[_colblk(0), _colblk(1), _colblk(2), _colblk(3), blk, blk, blk], out_specs=[blk] * 4,
        out_shape=[jax.ShapeDtypeStruct((S, D), F32)] * 4,
        scratch_shapes=[pltpu.VMEM((S, 128), F32), pltpu.VMEM((S, 128), F32), pltpu.VMEM((2, 2 * QB, 2 * QB), F32)],
        compiler_params=_cparams(("parallel",)),
    )(z, z, z, z, d_cat, o, lse)


def assemble_dz_even(parts):
    def body(*refs):
        o_ref = refs[-1]
        for j in range(6):
            o_ref[:, j * D:(j + 1) * D] = refs[j][...].astype(o_ref.dtype)
    tr = 256
    blk = pl.BlockSpec((tr, D), lambda i: (i, 0))
    return pl.pallas_call(
        body, name="assemble_dz_even", grid=(S // tr,), in_specs=[blk] * 6,
        out_specs=pl.BlockSpec((tr, 6 * D), lambda i: (i, 0)),
        out_shape=jax.ShapeDtypeStruct((S, 6 * D), BF16),
        compiler_params=_cparams(("parallel",)),
    )(*parts)


def _pool_window(g):
    return jnp.where(g == 0, 2.0, jnp.where(g == 1, 4.0, jnp.where(g == 2, 8.0, 16.0)))


def _pool_sel(g, levels):
    return jnp.where(g == 0, levels[0], jnp.where(g == 1, levels[1], jnp.where(g == 2, levels[2], levels[3])))


def _pool_fwd_math(v, g):
    t = lax.broadcasted_iota(jnp.int32, (S, 1), 0)
    s = v
    levels = []
    for k in (1, 2, 4, 8):
        s = s + jnp.where(t >= k, pltpu.roll(s, k, 0), 0.0)
        levels.append(s)
    cnt = jnp.minimum((t + 1).astype(F32), _pool_window(g))
    return _pool_sel(g, levels) / cnt - v, cnt


def pool_fwd(z, pw, ps, cat):
    def body(v_ref, g_ref, pw_ref, ps_ref, cat_ref, o_ref):
        g = pl.program_id(0)
        pooled, _ = _pool_fwd_math(v_ref[...], g)
        mixed = _nn(pooled.astype(BF16), pw_ref[...].astype(BF16))
        gb = g_ref[...]
        o_ref[...] = (mixed * ps_ref[...] * (gb * _sig(gb))).astype(o_ref.dtype)

    return pl.pallas_call(
        body, name="pool_fwd", grid=(4,),
        in_specs=[pl.BlockSpec((S, 256), lambda g: (0, 16 + g)),
                  pl.BlockSpec((S, 256), lambda g: (0, 20 + g)),
                  pl.BlockSpec((None, 256, 256), lambda g: (g, 0, 0)),
                  pl.BlockSpec((1, 256), lambda g: (0, g)), pl.BlockSpec(memory_space=pl.ANY)],
        out_specs=pl.BlockSpec((S, 256), lambda g: (0, 4 + g)),
        out_shape=jax.ShapeDtypeStruct((S, 2 * D), BF16),
        input_output_aliases={4: 0},
        compiler_params=_cparams(("parallel",)),
    )(z, z, pw, ps, cat)


def pool_bwd(z, d_cat, pw, ps):
    def body(v_ref, g_ref, d_ref, pw_ref, ps_ref, dv_ref, dg_ref, dpw_ref, dps_ref):
        g = pl.program_id(0)
        v = v_ref[...]
        pooled, cnt = _pool_fwd_math(v, g)
        pwb = pw_ref[...].astype(BF16)
        pb = pooled.astype(BF16)
        mixed = _nn(pb, pwb)
        gb = g_ref[...]
        sg = _sig(gb)
        dout = d_ref[...]
        sc = ps_ref[...]
        dg_ref[...] = dout * mixed * sc * (sg * (1.0 + gb * (1.0 - sg)))
        dms = dout * (gb * sg)
        dps_ref[...] = jnp.sum(dms * mixed, axis=0, keepdims=True)
        dmx = (dms * sc).astype(BF16)
        dpw_ref[...] = _tn(pb, dmx)
        dpooled = _nt(dmx, pwb)
        t = lax.broadcasted_iota(jnp.int32, (S, 1), 0)
        s = dpooled / cnt
        levels = []
        for k in (1, 2, 4, 8):
            s = s + jnp.where(t < S - k, pltpu.roll(s, S - k, 0), 0.0)
            levels.append(s)
        dv_ref[...] = _pool_sel(g, levels) - dpooled

    return pl.pallas_call(
        body, name="pool_bwd", grid=(4,),
        in_specs=[pl.BlockSpec((S, 256), lambda g: (0, 16 + g)),
                  pl.BlockSpec((S, 256), lambda g: (0, 20 + g)),
                  pl.BlockSpec((S, 256), lambda g: (0, 4 + g)),
                  pl.BlockSpec((None, 256, 256), lambda g: (g, 0, 0)),
                  pl.BlockSpec((1, 256), lambda g: (0, g))],
        out_specs=[pl.BlockSpec((S, 256), lambda g: (0, g)),
                   pl.BlockSpec((S, 256), lambda g: (0, g)),
                   pl.BlockSpec((None, 256, 256), lambda g: (g, 0, 0)),
                   pl.BlockSpec((1, 256), lambda g: (0, g))],
        out_shape=[jax.ShapeDtypeStruct((S, D), F32), jax.ShapeDtypeStruct((S, D), F32),
                   jax.ShapeDtypeStruct((4, 256, 256), F32), jax.ShapeDtypeStruct((1, D), F32)],
        compiler_params=_cparams(("parallel",)),
    )(z, z, d_cat, pw, ps)


CH = 128


def _sgu_common(v, lng, lnb, w_ref):
    mu = jnp.mean(v, axis=-1, keepdims=True)
    vc = v - mu
    rs = lax.rsqrt(jnp.mean(vc * vc, axis=-1, keepdims=True) + EPS)
    xhat = vc * rs
    vn = (xhat * lng + lnb).astype(BF16)
    ri = lax.broadcasted_iota(jnp.int32, (CH, CH), 0)
    ci = lax.broadcasted_iota(jnp.int32, (CH, CH), 1)
    tril = ri >= ci
    ws = [jnp.where(tril, w_ref[g], 0.0).astype(BF16) for g in range(4)]
    return xhat, rs, vn, tril, ws


def _zspec(off):
    return pl.BlockSpec((CH, D), lambda c: (c, off))


def _full(shape):
    return pl.BlockSpec(shape, lambda c: (0,) * len(shape))


def sgu_fwd(z, lng, lnb, w, bfull):
    def body(u_ref, v_ref, g_ref, lng_ref, lnb_ref, w_ref, b_ref, o_ref):
        _, _, vn, _, ws = _sgu_common(v_ref[...], lng_ref[...], lnb_ref[...], w_ref)
        for g in range(4):
            sl = slice(g * 256, (g + 1) * 256)
            mixed = _nn(ws[g], vn[:, sl]) + b_ref[:, sl]
            gc = g_ref[:, sl]
            o_ref[:, sl] = (u_ref[:, sl] * mixed * (gc * _sig(gc))).astype(o_ref.dtype)

    return pl.pallas_call(
        body, name="sgu_fwd", grid=(S // CH,),
        in_specs=[_zspec(0), _zspec(1), _zspec(2), _full((1, D)), _full((1, D)),
                  _full((4, CH, CH)), _full((CH, D))],
        out_specs=pl.BlockSpec((CH, D), lambda c: (c, 0)),
        out_shape=jax.ShapeDtypeStruct((S, D), BF16),
        compiler_params=_cparams(("parallel",)),
    )(z, z, z, lng, lnb, w, bfull)


def sgu_bwd(z, d_cat, lng, lnb, w, bfull):
    def body(u_ref, v_ref, g_ref, d_ref, lng_ref, lnb_ref, w_ref, b_ref,
             du_ref, dv_ref, dg_ref, dw_ref, db_ref, dlg_ref, dlb_ref):
        @pl.when(pl.program_id(0) == 0)
        def _():
            dw_ref[...] = jnp.zeros_like(dw_ref)
            db_ref[...] = jnp.zeros_like(db_ref)
            dlg_ref[...] = jnp.zeros_like(dlg_ref)
            dlb_ref[...] = jnp.zeros_like(dlb_ref)

        lng = lng_ref[...]
        xhat, rs, vn, tril, ws = _sgu_common(v_ref[...], lng, lnb_ref[...], w_ref)
        lane = lax.broadcasted_iota(jnp.int32, (1, 128), 1)
        db = jnp.zeros((CH, 128), F32)
        dvn_parts = []
        for g in range(4):
            sl = slice(g * 256, (g + 1) * 256)
            mixed = _nn(ws[g], vn[:, sl]) + b_ref[:, sl]
            gc = g_ref[:, sl]
            sg = _sig(gc)
            u = u_ref[:, sl]
            dc = d_ref[:, sl]
            du_ref[:, sl] = dc * mixed * (gc * sg)
            dg_ref[:, sl] = dc * u * mixed * (sg * (1.0 + gc * (1.0 - sg)))
            dmx = dc * u * (gc * sg)
            db = db + jnp.where(lane == g, jnp.sum(dmx, axis=-1, keepdims=True), 0.0)
            dmb = dmx.astype(BF16)
            dw_ref[g] += jnp.where(tril, _nt(dmb, vn[:, sl]), 0.0)
            dvn_parts.append(_tn(ws[g], dmb))
        db_ref[...] += db
        dvn = jnp.concatenate(dvn_parts, axis=1)
        dlb_ref[...] += jnp.sum(dvn, axis=0, keepdims=True)
        dlg_ref[...] += jnp.sum(dvn * xhat, axis=0, keepdims=True)
        dxh = dvn * lng
        dv_ref[...] = rs * (dxh - jnp.mean(dxh, axis=-1, keepdims=True)
                            - xhat * jnp.mean(dxh * xhat, axis=-1, keepdims=True))

    row = pl.BlockSpec((CH, D), lambda c: (c, 0))
    return pl.pallas_call(
        body, name="sgu_bwd", grid=(S // CH,),
        in_specs=[_zspec(0), _zspec(1), _zspec(2), row, _full((1, D)), _full((1, D)),
                  _full((4, CH, CH)), _full((CH, D))],
        out_specs=[row, row, row, _full((4, CH, CH)), _full((CH, 128)), _full((1, D)), _full((1, D))],
        out_shape=[jax.ShapeDtypeStruct((S, D), F32)] * 3
        + [jax.ShapeDtypeStruct((4, CH, CH), F32), jax.ShapeDtypeStruct((CH, 128), F32),
           jax.ShapeDtypeStruct((1, D), F32), jax.ShapeDtypeStruct((1, D), F32)],
        compiler_params=_cparams(("arbitrary",)),
    )(z, z, z, d_cat, lng, lnb, w, bfull)


TB = 256


def _cmul(ar, ai, br, bi):
    return ar * br - ai * bi, ar * bi + ai * br


def _scan_consts(ar, ai, reverse):
    a2 = _cmul(ar, ai, ar, ai)
    a4 = _cmul(*a2, *a2)
    row = lax.broadcasted_iota(jnp.int32, (8, NS), 0)
    pr = jnp.zeros((8, NS), F32)
    pi = jnp.zeros((8, NS), F32)
    cr, ci = ar, ai
    for r in range(8):
        sel = row == (7 - r if reverse else r)
        pr = jnp.where(sel, cr, pr)
        pi = jnp.where(sel, ci, pi)
        cr, ci = _cmul(cr, ci, ar, ai)
    return ((ar, ai), a2, a4), (pr, pi), row


def scan_fwd(bu, abr, abi):
    def body(bu_ref, ar_ref, ai_ref, h_ref, car, cai):
        @pl.when(pl.program_id(0) == 0)
        def _():
            car[...] = jnp.zeros_like(car)
            cai[...] = jnp.zeros_like(cai)

        pows, (pr, pi), row = _scan_consts(ar_ref[...], ai_ref[...], False)

        def tile(t, carry):
            c_r, c_i = carry
            rows = pl.ds(pl.multiple_of(t * 8, 8), 8)
            xr = bu_ref[rows, 0:NS]
            xi = bu_ref[rows, NS:2 * NS]
            for k, (kr, ki) in zip((1, 2, 4), pows):
                sr = jnp.where(row >= k, pltpu.roll(xr, k, 0), 0.0)
                si = jnp.where(row >= k, pltpu.roll(xi, k, 0), 0.0)
                xr, xi = xr + kr * sr - ki * si, xi + kr * si + ki * sr
            xr, xi = xr + pr * c_r - pi * c_i, xi + pr * c_i + pi * c_r
            h_ref[rows, 0:NS] = xr
            h_ref[rows, NS:2 * NS] = xi
            return (jnp.broadcast_to(xr[7:8, :], (8, NS)), jnp.broadcast_to(xi[7:8, :], (8, NS)))

        c_r, c_i = lax.fori_loop(0, TB // 8, tile, (car[...], cai[...]))
        car[...] = c_r
        cai[...] = c_i

    return pl.pallas_call(
        body, name="s5_scan_fwd", grid=(S // TB,),
        in_specs=[pl.BlockSpec((TB, 2 * NS), lambda i: (i, 0)),
                  pl.BlockSpec((1, NS), lambda i: (0, 0)), pl.BlockSpec((1, NS), lambda i: (0, 0))],
        out_specs=pl.BlockSpec((TB, 2 * NS), lambda i: (i, 0)),
        out_shape=jax.ShapeDtypeStruct((S, 2 * NS), F32),
        scratch_shapes=[pltpu.VMEM((8, NS), F32), pltpu.VMEM((8, NS), F32)],
        compiler_params=_cparams(("arbitrary",)),
    )(bu, abr, abi)


def scan_bwd(eta, h, abr, abi):
    nt = S // TB

    def body(e_ref, h_ref, ar_ref, ai_ref, l_ref, da_ref, car, cai):
        @pl.when(pl.program_id(0) == 0)
        def _():
            car[...] = jnp.zeros_like(car)
            cai[...] = jnp.zeros_like(cai)
            da_ref[...] = jnp.zeros_like(da_ref)

        pows, (pr, pi), row = _scan_consts(ar_ref[...], -ai_ref[...], True)

        def tile(tt, carry):
            c_r, c_i, acr, aci = carry
            t = TB // 8 - 1 - tt
            rows = pl.ds(pl.multiple_of(t * 8, 8), 8)
            xr = e_ref[rows, 0:NS]
            xi = e_ref[rows, NS:2 * NS]
            for k, (kr, ki) in zip((1, 2, 4), pows):
                sr = jnp.where(row < 8 - k, pltpu.roll(xr, 8 - k, 0), 0.0)
                si = jnp.where(row < 8 - k, pltpu.roll(xi, 8 - k, 0), 0.0)
                xr, xi = xr + kr * sr - ki * si, xi + kr * si + ki * sr
            xr, xi = xr + pr * c_r - pi * c_i, xi + pr * c_i + pi * c_r
            l_ref[rows, 0:NS] = xr
            l_ref[rows, NS:2 * NS] = xi
            nr = jnp.where(row < 7, pltpu.roll(xr, 7, 0), c_r)
            ni = jnp.where(row < 7, pltpu.roll(xi, 7, 0), c_i)
            hr = h_ref[rows, 0:NS]
            hi = h_ref[rows, NS:2 * NS]
            acr = acr + hr * nr + hi * ni
            aci = aci + hr * ni - hi * nr
            return (jnp.broadcast_to(xr[0:1, :], (8, NS)), jnp.broadcast_to(xi[0:1, :], (8, NS)), acr, aci)

        zero = jnp.zeros((8, NS), F32)
        c_r, c_i, acr, aci = lax.fori_loop(0, TB // 8, tile, (car[...], cai[...], zero, zero))
        car[...] = c_r
        cai[...] = c_i
        da_ref[:, 0:NS] += acr
        da_ref[:, NS:2 * NS] += aci

    rev = pl.BlockSpec((TB, 2 * NS), lambda i: (nt - 1 - i, 0))
    return pl.pallas_call(
        body, name="s5_scan_bwd", grid=(nt,),
        in_specs=[rev, rev, pl.BlockSpec((1, NS), lambda i: (0, 0)), pl.BlockSpec((1, NS), lambda i: (0, 0))],
        out_specs=[rev, pl.BlockSpec((8, 2 * NS), lambda i: (0, 0))],
        out_shape=[jax.ShapeDtypeStruct((S, 2 * NS), F32), jax.ShapeDtypeStruct((8, 2 * NS), F32)],
        scratch_shapes=[pltpu.VMEM((8, NS), F32), pltpu.VMEM((8, NS), F32)],
        compiler_params=_cparams(("arbitrary",)),
    )(eta, h, abr, abi)


GC = 0.7978845608028654
GA = 0.044715


def s5_post(hc, z, dskip):
    def fn(hv, xd, dv):
        y = hv + dv * xd
        return y, 0.5 * y * (1.0 + jnp.tanh(GC * (y + GA * y * y * y)))
    return rw(fn, [(hc, 0, 512), (z, 3072, 512)], [(512, F32), (512, BF16)], "s5_post", S, consts=[dskip])


def s5_post_bwd(dyg, ypre, z, dskip):
    def fn(dy, y, xd, dv):
        th = jnp.tanh(GC * (y + GA * y * y * y))
        dg = 0.5 * (1.0 + th) + 0.5 * y * (1.0 - th * th) * GC * (1.0 + 3.0 * GA * y * y)
        dyp = dy * dg
        return dyp, dyp * dv, jnp.sum(dyp * xd, axis=0, keepdims=True)
    return rw(fn, [(dyg, 0, 512), (ypre, 0, 512), (z, 3072, 512)], [(512, BF16), (512, F32)],
              "s5_post_bwd", S, consts=[dskip], accs=[(1, 512)])


def glu_fwd(t, z, c_out):
    def fn(t1, t2, gd, co):
        return (jnp.concatenate([co, (t1 * _sig(t2) * (gd * _sig(gd))).astype(BF16)], axis=1),)
    return rw(fn, [(t, 0, 512), (t, 512, 512), (z, 3584, 512), (c_out, 0, D)], [(D + 512, BF16)], "glu_fwd", S)[0]


def glu_bwd(t, z, d_cat):
    def fn(t1, t2, gd, dd):
        s2, sg = _sig(t2), _sig(gd)
        sl = gd * sg
        return (jnp.concatenate([dd * s2 * sl, dd * t1 * s2 * (1.0 - s2) * sl], axis=1),
                dd * t1 * s2 * (sg * (1.0 + gd * (1.0 - sg))))
    return rw(fn, [(t, 0, 512), (t, 512, 512), (z, 3584, 512), (d_cat, 1024, 512)],
              [(D, BF16), (512, F32)], "glu_bwd", S)


def assemble_dz_odd(du, dv, dgc, dxd, dgd):
    def body(a, b, c, d, e, o_ref):
        o_ref[:, 0:D] = a[...].astype(BF16)
        o_ref[:, D:2 * D] = b[...].astype(BF16)
        o_ref[:, 2 * D:3 * D] = c[...].astype(BF16)
        o_ref[:, 3 * D:3 * D + 512] = d[...].astype(BF16)
        o_ref[:, 3 * D + 512:4 * D] = e[...].astype(BF16)
    tr = 256
    blk = pl.BlockSpec((tr, D), lambda i: (i, 0))
    half = pl.BlockSpec((tr, 512), lambda i: (i, 0))
    return pl.pallas_call(
        body, name="assemble_dz_odd", grid=(S // tr,), in_specs=[blk, blk, blk, half, half],
        out_specs=pl.BlockSpec((tr, 4 * D), lambda i: (i, 0)),
        out_shape=jax.ShapeDtypeStruct((S, 4 * D), BF16),
        compiler_params=_cparams(("parallel",)),
    )(du, dv, dgc, dxd, dgd)


TQ = 256


def _xattn_probs(qh, kh):
    s = _nt(qh, kh) * 0.0625
    p = jnp.exp(s - jnp.max(s, axis=-1, keepdims=True))
    return p / jnp.sum(p, axis=-1, keepdims=True)


def xattn_fwd(q, kv):
    def body(q_ref, kv_ref, o_ref):
        for h in range(4):
            sl = slice(h * 256, (h + 1) * 256)
            p = _xattn_probs(q_ref[:, sl].astype(BF16), kv_ref[:, sl].astype(BF16))
            vh = kv_ref[:, D + h * 256:D + (h + 1) * 256].astype(BF16)
            o_ref[:, sl] = _nn(p.astype(BF16), vh).astype(o_ref.dtype)

    return pl.pallas_call(
        body, name="xattn_fwd", grid=(S // TQ,),
        in_specs=[pl.BlockSpec((TQ, D), lambda i: (i, 0)), pl.BlockSpec((MEM, 2 * D), lambda i: (0, 0))],
        out_specs=pl.BlockSpec((TQ, D), lambda i: (i, 0)),
        out_shape=jax.ShapeDtypeStruct((S, D), BF16),
        compiler_params=_cparams(("parallel",)),
    )(q, kv)


def xattn_bwd(q, kv, d_o):
    def body(q_ref, kv_ref, do_ref, dq_ref, dkv_ref):
        @pl.when(pl.program_id(0) == 0)
        def _():
            dkv_ref[...] = jnp.zeros_like(dkv_ref)

        for h in range(4):
            sl = slice(h * 256, (h + 1) * 256)
            vs = slice(D + h * 256, D + (h + 1) * 256)
            qh = q_ref[:, sl].astype(BF16)
            kh = kv_ref[:, sl].astype(BF16)
            vh = kv_ref[:, vs].astype(BF16)
            doh = do_ref[:, sl].astype(BF16)
            p = _xattn_probs(qh, kh)
            dp = _nt(doh, vh)
            ds = (p * (dp - jnp.sum(p * dp, axis=-1, keepdims=True)) * 0.0625).astype(BF16)
            dq_ref[:, sl] = _nn(ds, kh).astype(dq_ref.dtype)
            dkv_ref[:, sl] += _tn(ds, qh)
            dkv_ref[:, vs] += _tn(p.astype(BF16), doh)

    return pl.pallas_call(
        body, name="xattn_bwd", grid=(S // TQ,),
        in_specs=[pl.BlockSpec((TQ, D), lambda i: (i, 0)), pl.BlockSpec((MEM, 2 * D), lambda i: (0, 0)),
                  pl.BlockSpec((TQ, D), lambda i: (i, 0))],
        out_specs=[pl.BlockSpec((TQ, D), lambda i: (i, 0)), pl.BlockSpec((MEM, 2 * D), lambda i: (0, 0))],
        out_shape=[jax.ShapeDtypeStruct((S, D), BF16), jax.ShapeDtypeStruct((MEM, 2 * D), F32)],
        compiler_params=_cparams(("arbitrary",)),
    )(q, kv, d_o)


def _s5_disc(a_re, a_im, log_dt, b_re, b_im):
    dt = jnp.exp(log_dt)[:, None]
    mag = jnp.exp(dt * a_re)
    abr = mag * jnp.cos(dt * a_im)
    abi = mag * jnp.sin(dt * a_im)
    nr, ni = abr - 1.0, abi
    inv = 1.0 / (a_re * a_re + a_im * a_im)
    cr = (nr * a_re + ni * a_im) * inv
    ci = (ni * a_re - nr * a_im) * inv
    bbr = cr[..., None] * b_re - ci[..., None] * b_im
    bbi = cr[..., None] * b_im + ci[..., None] * b_re
    return abr, abi, bbr, bbi


VM = pl.BlockSpec(memory_space=pltpu.VMEM)


def s5_embed(bt_re, bt_im, ct_re, ct_im):
    def body(br, bi, cr, ci, b_ref, c_ref):
        b_ref[...] = jnp.zeros_like(b_ref)
        c_ref[...] = jnp.zeros_like(c_ref)
        for g in range(NG):
            rows, cols = slice(g * NH, (g + 1) * NH), slice(g * NP, (g + 1) * NP)
            b_ref[rows, cols] = br[g]
            b_ref[rows, NS + g * NP:NS + (g + 1) * NP] = bi[g]
            c_ref[cols, rows] = cr[g]
            c_ref[NS + g * NP:NS + (g + 1) * NP, rows] = -ci[g]

    return pl.pallas_call(
        body, name="s5_embed", in_specs=[VM] * 4, out_specs=[VM] * 2,
        out_shape=[jax.ShapeDtypeStruct((NG * NH, 2 * NS), F32), jax.ShapeDtypeStruct((2 * NS, NG * NH), F32)],
        compiler_params=pltpu.CompilerParams(vmem_limit_bytes=VMEM_LIMIT),
    )(bt_re, bt_im, ct_re, ct_im)


def s5_extract(gb, gc):
    def body(gb_ref, gc_ref, br, bi, cr, ci):
        for g in range(NG):
            rows, cols = slice(g * NH, (g + 1) * NH), slice(g * NP, (g + 1) * NP)
            br[g] = gb_ref[rows, cols]
            bi[g] = gb_ref[rows, NS + g * NP:NS + (g + 1) * NP]
            cr[g] = gc_ref[cols, rows]
            ci[g] = -gc_ref[NS + g * NP:NS + (g + 1) * NP, rows]

    return pl.pallas_call(
        body, name="s5_extract", in_specs=[VM] * 2, out_specs=[VM] * 4,
        out_shape=[jax.ShapeDtypeStruct((NG, NH, NP), F32)] * 2 + [jax.ShapeDtypeStruct((NG, NP, NH), F32)] * 2,
        compiler_params=pltpu.CompilerParams(vmem_limit_bytes=VMEM_LIMIT),
    )(gb, gc)


HC, HS = NG * NH // 2, NS // 2
TS = 1024


def s5_to_states(x, w, mode, name, z_off=0):
    if mode == "nn":
        wb, wm = (HC, HS), lambda i, j, kk: (j % 2, j)
    else:
        wb, wm = (HS, HC), lambda i, j, kk: (j, j % 2)
    return mm_band(x, w, mode, name, (S // TS, 4, 1), ((TS, HC), wb, (TS, HS)),
                   (lambda i, j, kk: (i, z_off + j % 2), wm, lambda i, j, kk: (i, j)), (S, 2 * NS))


def s5_to_channels(x, w, mode, name, add=None):
    if mode == "nn":
        wb, wm = (HS, HC), lambda i, j, kk: (j + 2 * kk, j)
    else:
        wb, wm = (HC, HS), lambda i, j, kk: (j, j + 2 * kk)
    return mm_band(x, w, mode, name, (S // TS, 2, 2), ((TS, HS), wb, (TS, HC)),
                   (lambda i, j, kk: (i, j + 2 * kk), wm, lambda i, j, kk: (i, j)), (S, NG * NH), add=add)


def s5_outer(a, b, name, states_first, z_off=0):
    if states_first:
        return mm_band(a, b, "tn", name, (4, 1, 1), ((S, HS), (S, HC), (HS, HC)),
                       (lambda i, j, kk: (0, i), lambda i, j, kk: (0, i % 2), lambda i, j, kk: (i, i % 2)),
                       (2 * NS, NG * NH))
    return mm_band(a, b, "tn", name, (1, 4, 1), ((S, HC), (S, HS), (HC, HS)),
                   (lambda i, j, kk: (0, z_off + j % 2), lambda i, j, kk: (0, j), lambda i, j, kk: (j % 2, j)),
                   (NG * NH, 2 * NS))


def _fwd_even(i, x, P, W):
    hn = rms_fwd(x, P["norm_ab"][i:i + 1], "rms_ab_fwd")
    z = mm(m2(hn), W["w_in"], "nn", "in_ab")
    o, lse, cat = attn_fwd(z)
    if "more" in W:
        W.update(W.pop("more")(cat))
    cat = pool_fwd(z, W["pool_w"], P["pool_scale"][i:i + 1], cat)
    x_mid = mm(m2(cat), W["w_out"], "nn", "out_ab", add=m2(x))
    return x_mid, dict(x=x, hn=hn, z=z, o=o, lse=lse, cat=cat)


def _bwd_even(i, dx_mid, sv, P, W, G, GW):
    z = sv["z"]
    d_cat = mm(m2(dx_mid), W["w_out"], "nt", "out_ab_dx")
    GW["w_out"] = mm(m2(sv["cat"]), m2(dx_mid), "tn", "out_ab_dw").reshape(4, 512, D)
    dq, dk, dv, dga = attn_bwd(z, d_cat, sv["o"], sv["lse"])
    dvb, dgb, dpw, dps = pool_bwd(z, d_cat, W["pool_w"], P["pool_scale"][i:i + 1])
    GW["pool_w"] = dpw.reshape(4, 4, 64, 256).transpose(1, 0, 2, 3).reshape(4, 256, 256)
    G["pool_scale"][i] = dps[0]
    d_z = assemble_dz_even((dq, dk, dv, dga, dvb, dgb))
    d_hn = mm(m2(d_z), W["w_in"], "nt", "in_ab_dx")
    GW["w_in"] = mm(m2(sv["hn"]), m2(d_z), "tn", "in_ab_dw", out=outcs(D, 1536))
    return d_hn, P["norm_ab"][i:i + 1], "norm_ab", "rms_ab_bwd"


def _fwd_odd(i, x, P, W):
    hn = rms_fwd(x, P["norm_cd"][i:i + 1], "rms_cd_fwd")
    z = mm(m2(hn), W["w_in"], "nn", "in_cd")
    bfull = jnp.repeat(P["sgu_b"][i].T, 256, axis=1)
    c_out = sgu_fwd(z, P["sgu_ln_g"][i:i + 1], P["sgu_ln_b"][i:i + 1], P["sgu_w"][i], bfull)
    disc, disc_vjp = jax.vjp(_s5_disc, P["s5_a_re"][i], P["s5_a_im"][i], P["s5_log_dt"][i],
                             P["s5_b_re"][i], P["s5_b_im"][i])
    abr, abi, bbr, bbi = disc
    bbd, cbd = s5_embed(bbr.transpose(0, 2, 1), bbi.transpose(0, 2, 1),
                        P["s5_c_re"][i].transpose(0, 2, 1), P["s5_c_im"][i].transpose(0, 2, 1))
    abr, abi = abr.reshape(1, NS), abi.reshape(1, NS)
    bu = s5_to_states(z, bbd, "nn", "s5_bu", z_off=3072 // HC)
    h = scan_fwd(bu, abr, abi)
    hc = s5_to_channels(h, cbd, "nn", "s5_hc")
    dskip = P["s5_d"][i:i + 1]
    ypre, yg = s5_post(hc, z, dskip)
    if "more" in W:
        W.update(W.pop("more")(yg))
    w12 = W["w12"]
    t = mm(m2(yg), m2(w12), "nn", "glu_t")
    cat = glu_fwd(t, z, c_out)
    x_mid = mm(m2(cat), W["w_out"], "nn", "out_cd", add=m2(x))
    return x_mid, dict(x=x, hn=hn, z=z, bfull=bfull, disc_vjp=disc_vjp, bbd=bbd, cbd=cbd, abr=abr,
                       abi=abi, h=h, ypre=ypre, yg=yg, w12=w12, t=t, cat=cat, dskip=dskip)


def _bwd_odd(i, dx_mid, sv, P, W, G, GW):
    z = sv["z"]
    d_cat = mm(m2(dx_mid), W["w_out"], "nt", "out_cd_dx")
    GW["w_out"] = mm(m2(sv["cat"]), m2(dx_mid), "tn", "out_cd_dw").reshape(4, 384, D)
    du, dv, dgc, dws, dbs, dlg, dlb = sgu_bwd(z, d_cat, P["sgu_ln_g"][i:i + 1], P["sgu_ln_b"][i:i + 1],
                                               P["sgu_w"][i], sv["bfull"])
    G["sgu_w"][i], G["sgu_b"][i] = dws, dbs[:, :4].T
    G["sgu_ln_g"][i], G["sgu_ln_b"][i] = dlg[0], dlb[0]
    dt, dgd = glu_bwd(sv["t"], z, d_cat)
    gw12 = mm(m2(sv["yg"]), m2(dt), "tn", "glu_dw")
    GW["glu_w1"] = gw12[:, :512].reshape(4, 128, 512)
    GW["glu_w2"] = gw12[:, 512:].reshape(4, 128, 512)
    dyg = mm(m2(dt), m2(sv["w12"]), "nt", "glu_dx")
    dypre, dxd1, dd = s5_post_bwd(dyg, sv["ypre"], z, sv["dskip"])
    G["s5_d"][i] = dd[0]
    gcbd = s5_outer(sv["h"], dypre, "s5_dc", states_first=True)
    eta = s5_to_states(dypre, sv["cbd"], "nt", "s5_eta")
    lam, dacc = scan_bwd(eta, sv["h"], sv["abr"], sv["abi"])
    gbbd = s5_outer(z, lam, "s5_db", states_first=False, z_off=3072 // HC)
    dxd = s5_to_channels(lam, sv["bbd"], "nt", "s5_dx", add=dxd1)
    dacc = jnp.sum(dacc, axis=0)
    dbt_re, dbt_im, dct_re, dct_im = s5_extract(gbbd, gcbd)
    G["s5_c_re"][i], G["s5_c_im"][i] = dct_re.transpose(0, 2, 1), dct_im.transpose(0, 2, 1)
    d_bbr, d_bbi = dbt_re.transpose(0, 2, 1), dbt_im.transpose(0, 2, 1)
    (G["s5_a_re"][i], G["s5_a_im"][i], G["s5_log_dt"][i], G["s5_b_re"][i], G["s5_b_im"][i]) = sv["disc_vjp"](
        (dacc[:NS].reshape(NG, NP), dacc[NS:].reshape(NG, NP), d_bbr, d_bbi))
    d_z = assemble_dz_odd(du, dv, dgc, dxd, dgd)
    d_hn = mm(m2(d_z), W["w_in"], "nt", "in_cd_dx")
    GW["w_in"] = mm(m2(sv["hn"]), m2(d_z), "tn", "in_cd_dw", out=outcs(D, 1024))
    return d_hn, P["norm_cd"][i:i + 1], "norm_cd", "rms_cd_bwd"


def _fwd_x(l, x, mem_n, P, W):
    hx = rms_fwd(x, P["norm_x"][l:l + 1], "rms_x_fwd")
    q = mm(m2(hx), W["w_xq"], "nn", "xq", out_dtype=BF16)
    kv = mm(m2(mem_n), W["w_xkv"], "nn", "xkv", out_dtype=BF16)
    ox = xattn_fwd(q, kv)
    x_out = mm(m2(ox), W["w_xo"], "nn", "xo", add=m2(x))
    return x_out, dict(x=x, hx=hx, q=q, kv=kv, ox=ox)


def _bwd_x(l, dx_out, sv, mem_n, d_memn, P, W, G, GW):
    d_ox = mm(m2(dx_out), W["w_xo"], "nt", "xo_dx", out_dtype=BF16)
    GW["w_xo"] = mm(m2(sv["ox"]), m2(dx_out), "tn", "xo_dw").reshape(4, 256, D)
    dq, dkv = xattn_bwd(sv["q"], sv["kv"], d_ox)
    GW["w_xq"] = mm(m2(sv["hx"]), m2(dq), "tn", "xq_dw").reshape(4, 256, D)
    d_hx = mm(m2(dq), W["w_xq"], "nt", "xq_dx")
    GW["w_xkv"] = mm(m2(mem_n), m2(dkv), "tn", "xkv_dw", out=outcs(D, 512))
    d_memn = mm(m2(dkv), W["w_xkv"], "nt", "xkv_dx", add=None if d_memn is None else m2(d_memn))
    dx, dg = rms_bwd(sv["x"], d_hx, dx_out, P["norm_x"][l:l + 1], "rms_x_bwd")
    G["norm_x"][l] = dg[0]
    return dx, d_memn


SMALL_LAYERS = (("norm_ab", 2), ("pool_scale", 2), ("norm_cd", 2), ("sgu_ln_g", 2), ("sgu_ln_b", 2), ("sgu_w", 2),
                ("sgu_b", 2), ("s5_a_re", 2), ("s5_a_im", 2), ("s5_log_dt", 2), ("s5_b_re", 2), ("s5_b_im", 2),
                ("s5_c_re", 2), ("s5_c_im", 2), ("s5_d", 2), ("norm_x", 4))


def local_step(x, mem, tgt, P, weights_of, grads_done):
    G = {k: [None] * n for k, n in SMALL_LAYERS}
    mem_g = P["mem_norm"].reshape(1, D)
    mem_n = rms_fwd(mem, mem_g, "rms_mem_fwd")
    saved = []
    for layer in range(4):
        i = layer // 2
        W = weights_of(layer, x)
        x, sv_m = (_fwd_even if layer % 2 == 0 else _fwd_odd)(i, x, P, W)
        x, sv_x = _fwd_x(layer, x, mem_n, P, W)
        saved.append((sv_m, sv_x, W))
    dx, loss, dgf = final_loss(x, tgt, P["final_norm"].reshape(1, D))
    G["final_norm"] = dgf[0]
    d_memn = None
    for layer in reversed(range(4)):
        i = layer // 2
        sv_m, sv_x, W = saved[layer]
        GW = {}
        dx_mid, d_memn = _bwd_x(layer, dx, sv_x, mem_n, d_memn, P, W, G, GW)
        d_hn, g, key, name = (_bwd_even if layer % 2 == 0 else _bwd_odd)(i, dx_mid, sv_m, P, W, G, GW)
        token = grads_done(layer, GW)
        if token is not None:
            g = g + token
        dx, dg = rms_bwd(sv_m["x"], d_hn, dx_mid, g, name)
        G[key][i] = dg[0]
    _, dgm = rms_bwd(mem, d_memn, d_memn, mem_g, "rms_mem_bwd")
    G["mem_norm"] = dgm[0]
    return loss, dx, G


ANY = pl.BlockSpec(memory_space=pl.ANY)


def _place():
    x, y, c = lax.axis_index("x"), lax.axis_index("y"), lax.axis_index("c")
    chips = [(1 - x, y), (x, 1 - y), (1 - x, 1 - y)]
    return x, y, c, 2 * x + y, (x, y, 1 - c), chips


def _remote(src, dst, send, recv, k, dev):
    return pltpu.make_async_remote_copy(src_ref=src, dst_ref=dst, send_sem=send.at[k], recv_sem=recv.at[k],
                                        device_id=dev, device_id_type=MESHID)


HBM = pl.BlockSpec(memory_space=pltpu.HBM)
SEM = pl.BlockSpec(memory_space=pltpu.SEMAPHORE)
EFFECT = pltpu.SideEffectType.DATAFLOW_SIDE_EFFECTING


def _hbm(t):
    return pltpu.with_memory_space_constraint(t, pltpu.HBM)


def allgather_sync(shards):
    n = len(shards)

    def body(*refs):
        ins, outs = refs[:n], refs[n:2 * n]
        token, send, recv = refs[2 * n:]
        x, y, c, jme, sib, chips = _place()
        first, passed = [], []
        for a in range(n):
            cp = _remote(ins[a], outs[a].at[jme], send, recv, a * 7 + 6, sib)
            cp.start()
            first.append(cp)
            for k, chip in enumerate(chips):
                cp = _remote(ins[a].at[c], outs[a].at[jme, c], send, recv, a * 7 + k, (*chip, c))
                cp.start()
                first.append(cp)
        for a in range(n):
            for k, chip in enumerate(chips):
                piece = outs[a].at[2 * chip[0] + chip[1], c]
                _remote(piece, piece, send, recv, a * 7 + k, (*chip, c)).wait_recv()
                fw = _remote(piece, piece, send, recv, a * 7 + 3 + k, sib)
                fw.start()
                passed.append(fw)
        for a in range(n):
            own = outs[a].at[jme]
            _remote(own, own, send, recv, a * 7 + 6, sib).wait_recv()
            for k, chip in enumerate(chips):
                piece = outs[a].at[2 * chip[0] + chip[1], 1 - c]
                _remote(piece, piece, send, recv, a * 7 + 3 + k, sib).wait_recv()
        for cp in first + passed:
            cp.wait_send()
        token[...] = jnp.zeros_like(token)

    res = pl.pallas_call(
        body, name="allgather_sync", in_specs=[ANY] * n,
        out_specs=[ANY] * n + [pl.BlockSpec(memory_space=pltpu.VMEM)],
        out_shape=[jax.ShapeDtypeStruct((4,) + s.shape, s.dtype) for s in shards] + [jax.ShapeDtypeStruct((8, 128), F32)],
        scratch_shapes=[pltpu.SemaphoreType.DMA((7 * n,)), pltpu.SemaphoreType.DMA((7 * n,))],
    )(*shards)
    return list(res[:n]), res[n]


def _gather_copies(ins, lands, send, recv):
    x, y, c, jme, sib, chips = _place()
    devs = [(*chip, c) for chip in chips] + [sib]
    return [_remote(ins[a], lands[a].at[jme], send, recv, a * 4 + k, dev)
            for a in range(len(ins)) for k, dev in enumerate(devs)]


def allgather_start(shards, after, name):
    n, na = len(shards), len(after)

    def body(*refs):
        ins, lands = refs[:n], refs[n:2 * n]
        send, recv = refs[2 * n + na], refs[2 * n + na + 1]
        token = refs[-1]
        for cp in _gather_copies(ins, lands, send, recv):
            cp.start()
        token[...] = jnp.zeros_like(token)

    res = pl.pallas_call(
        body, name=name,
        out_shape=(pltpu.SemaphoreType.DMA((4 * n,)), pltpu.SemaphoreType.DMA((4 * n,)),
                   *[pltpu.HBM(s.shape, s.dtype) for s in shards],
                   *[pltpu.HBM((4,) + s.shape, s.dtype) for s in shards],
                   jax.ShapeDtypeStruct((8, 128), F32)),
        in_specs=[HBM] * (2 * n) + [ANY] * na,
        out_specs=(SEM, SEM, *[HBM] * (2 * n), pl.BlockSpec(memory_space=pltpu.VMEM)),
        input_output_aliases={a: 2 + a for a in range(2 * n)},
        compiler_params=pltpu.CompilerParams(has_side_effects=EFFECT),
    )(*[_hbm(s) for s in shards], *[_hbm(lax.empty((4,) + s.shape, s.dtype)) for s in shards], *after)
    return res[0], res[1], list(res[2:2 + n]), list(res[2 + n:2 + 2 * n]), res[-1]


def allgather_wait(send, recv, shards, lands, after, name):
    n = len(shards)

    def body(*refs):
        ins, zones = refs[:n], refs[n:2 * n]
        send_r, recv_r = refs[2 * n], refs[2 * n + 1]
        x, y, c, jme, sib, chips = _place()
        slots = [2 * chip[0] + chip[1] for chip in chips] + [jme]
        for a in range(n):
            for k, slot in enumerate(slots):
                cp = _remote(ins[a], zones[a].at[slot], send_r, recv_r, a * 4 + k, sib)
                cp.wait_send()
                cp.wait_recv()

    res = pl.pallas_call(
        body, name=name,
        out_shape=tuple(pltpu.HBM(t.shape, t.dtype) for t in list(shards) + list(lands)),
        in_specs=[HBM] * (2 * n) + [SEM, SEM, ANY], out_specs=tuple([HBM] * (2 * n)),
        input_output_aliases={a: a for a in range(2 * n)},
        compiler_params=pltpu.CompilerParams(has_side_effects=EFFECT),
    )(*shards, *lands, send, recv, after)
    return list(res[n:])


def allgather_small(slab):
    def body(in_ref, out_ref, send, recv, lsem):
        x, y, c, jme, sib, chips = _place()
        loc = pltpu.make_async_copy(in_ref, out_ref.at[jme], lsem.at[0])
        loc.start()
        cps = [_remote(in_ref, out_ref.at[jme], send, recv, k, (*chip, c)) for k, chip in enumerate(chips)]
        for cp in cps:
            cp.start()
        for k, chip in enumerate(chips):
            piece = out_ref.at[2 * chip[0] + chip[1]]
            _remote(piece, piece, send, recv, k, (*chip, c)).wait_recv()
        for cp in cps:
            cp.wait_send()
        loc.wait()

    return pl.pallas_call(
        body, name="allgather_small", in_specs=[ANY], out_specs=ANY,
        out_shape=jax.ShapeDtypeStruct((4,) + slab.shape, slab.dtype),
        scratch_shapes=[pltpu.SemaphoreType.DMA((3,)), pltpu.SemaphoreType.DMA((3,)), pltpu.SemaphoreType.DMA((1,))],
    )(slab)


def allreduce_small(v):
    def body(v_ref, o_ref, r0, r1, r2, send, recv):
        x, y, c, jme, sib, chips = _place()
        peers = [sib, (1 - x, y, c), (x, 1 - y, c)]
        o_ref[...] = v_ref[...]
        for k, buf in enumerate((r0, r1, r2)):
            cp = _remote(o_ref, buf, send, recv, k, peers[k])
            cp.start()
            cp.wait()
            o_ref[...] = o_ref[...] + buf[...]

    vm = pl.BlockSpec(memory_space=pltpu.VMEM)
    return pl.pallas_call(
        body, name="allreduce_small", in_specs=[vm], out_specs=vm,
        out_shape=jax.ShapeDtypeStruct(v.shape, v.dtype),
        scratch_shapes=[pltpu.VMEM(v.shape, v.dtype)] * 3 + [pltpu.SemaphoreType.DMA((3,)), pltpu.SemaphoreType.DMA((3,))],
        compiler_params=pltpu.CompilerParams(vmem_limit_bytes=VMEM_LIMIT),
    )(v)


def _pair_copies(gs, lands, send, recv):
    x, y, c, jme, sib, chips = _place()
    return [_remote(gs[a].at[:, 1 - c], lands[a], send, recv, a, sib) for a in range(len(gs))]


def rs_pair_start(gs, name):
    n = len(gs)

    def body(*refs):
        ins, lands = refs[:n], refs[n:2 * n]
        send, recv = refs[2 * n], refs[2 * n + 1]
        token = refs[-1]
        for cp in _pair_copies(ins, lands, send, recv):
            cp.start()
        token[...] = jnp.zeros_like(token)

    shapes = [(4,) + g.shape[2:] for g in gs]
    res = pl.pallas_call(
        body, name=name,
        out_shape=(pltpu.SemaphoreType.DMA((n,)), pltpu.SemaphoreType.DMA((n,)),
                   *[pltpu.HBM(g.shape, g.dtype) for g in gs], *[pltpu.HBM(s, F32) for s in shapes],
                   jax.ShapeDtypeStruct((8, 128), F32)),
        in_specs=[HBM] * (2 * n), out_specs=(SEM, SEM, *[HBM] * (2 * n), pl.BlockSpec(memory_space=pltpu.VMEM)),
        input_output_aliases={a: 2 + a for a in range(2 * n)},
        compiler_params=pltpu.CompilerParams(has_side_effects=EFFECT),
    )(*[_hbm(g) for g in gs], *[_hbm(lax.empty(s, F32)) for s in shapes])
    return res[0], res[1], list(res[2:2 + n]), list(res[2 + n:2 + 2 * n]), res[-1]


def rs_pair_wait(send, recv, gs, lands, after, name):
    n = len(gs)

    def body(*refs):
        ins, zones = refs[:n], refs[n:2 * n]
        for cp in _pair_copies(ins, zones, refs[2 * n], refs[2 * n + 1]):
            cp.wait_send()
            cp.wait_recv()

    res = pl.pallas_call(
        body, name=name,
        out_shape=tuple(pltpu.HBM(t.shape, t.dtype) for t in list(gs) + list(lands)),
        in_specs=[HBM] * (2 * n) + [SEM, SEM, ANY], out_specs=tuple([HBM] * (2 * n)),
        input_output_aliases={a: a for a in range(2 * n)},
        compiler_params=pltpu.CompilerParams(has_side_effects=EFFECT),
    )(*gs, *lands, send, recv, after)
    return list(res[:n]), list(res[n:])


def rs_pair_sum(g4, got, cidx):
    _, _, rh, cols = g4.shape
    tr = rh if rh <= 256 else 256

    def body(c_ref, a_ref, b_ref, o_ref):
        o_ref[...] = (a_ref[...] + b_ref[...]).astype(o_ref.dtype)

    return pl.pallas_call(
        body, name="rs_pair_sum",
        grid_spec=pltpu.PrefetchScalarGridSpec(
            num_scalar_prefetch=1, grid=(4, rh // tr),
            in_specs=[pl.BlockSpec((None, None, tr, cols), lambda j, t, cr: (j, cr[0], t, 0)),
                      pl.BlockSpec((None, tr, cols), lambda j, t, cr: (j, t, 0))],
            out_specs=pl.BlockSpec((None, tr, cols), lambda j, t, cr: (j, t, 0))),
        out_shape=jax.ShapeDtypeStruct((4, rh, cols), BF16),
        compiler_params=_cparams(("parallel", "parallel")),
    )(cidx, g4, got)


def _chip_copies(ps, lands, send, recv):
    x, y, c, jme, sib, chips = _place()
    return [_remote(ps[a].at[2 * chip[0] + chip[1]], lands[a].at[jme], send, recv, a * 3 + k, (*chip, c))
            for a in range(len(ps)) for k, chip in enumerate(chips)]


def rs_chip_start(ps, name):
    n = len(ps)

    def body(*refs):
        ins, lands = refs[:n], refs[n:2 * n]
        send, recv = refs[2 * n], refs[2 * n + 1]
        token = refs[-1]
        for cp in _chip_copies(ins, lands, send, recv):
            cp.start()
        token[...] = jnp.zeros_like(token)

    res = pl.pallas_call(
        body, name=name,
        out_shape=(pltpu.SemaphoreType.DMA((3 * n,)), pltpu.SemaphoreType.DMA((3 * n,)),
                   *[pltpu.HBM(p.shape, p.dtype) for p in ps], *[pltpu.HBM(p.shape, p.dtype) for p in ps],
                   jax.ShapeDtypeStruct((8, 128), F32)),
        in_specs=[HBM] * (2 * n), out_specs=(SEM, SEM, *[HBM] * (2 * n), pl.BlockSpec(memory_space=pltpu.VMEM)),
        input_output_aliases={a: 2 + a for a in range(2 * n)},
        compiler_params=pltpu.CompilerParams(has_side_effects=EFFECT),
    )(*[_hbm(p) for p in ps], *[_hbm(lax.empty(p.shape, p.dtype)) for p in ps])
    return res[0], res[1], list(res[2:2 + n]), list(res[2 + n:2 + 2 * n]), res[-1]


def rs_chip_wait(send, recv, ps, lands, after, name):
    n = len(ps)

    def body(*refs):
        ins, zones = refs[:n], refs[n:2 * n]
        send_r, recv_r = refs[2 * n], refs[2 * n + 1]
        x, y, c, jme, sib, chips = _place()
        for a in range(n):
            for k, chip in enumerate(chips):
                jt = 2 * chip[0] + chip[1]
                cp = _remote(ins[a].at[jt], zones[a].at[jt], send_r, recv_r, a * 3 + k, (*chip, c))
                cp.wait_send()
                cp.wait_recv()

    res = pl.pallas_call(
        body, name=name,
        out_shape=tuple(pltpu.HBM(p.shape, p.dtype) for p in list(ps) + list(lands)),
        in_specs=[HBM] * (2 * n) + [SEM, SEM] + [ANY] * len(after), out_specs=tuple([HBM] * (2 * n)),
        input_output_aliases={a: a for a in range(2 * n)},
        compiler_params=pltpu.CompilerParams(has_side_effects=EFFECT),
    )(*ps, *lands, send, recv, *after)
    return list(res[n:])


def rs_chip_sum(q, p, l, acc, layers, jc):
    _, rh, cols = q.shape
    tr = rh if rh <= 256 else 256

    def body(jc_ref, q_ref, p_ref, *rest):
        o_ref = rest[-1]
        jme = jc_ref[0]
        own = p_ref[...].astype(F32)
        v = [jnp.where(jme == j, own, q_ref[j].astype(F32)) for j in range(4)]
        o_ref[...] = ((v[0] + v[1]) + v[2]) + v[3]

    in_specs = [pl.BlockSpec((4, tr, cols), lambda t, jr: (0, t, 0)),
                pl.BlockSpec((None, tr, cols), lambda t, jr: (jr[0], t, 0))]
    args = [jc, q, p]
    if acc is not None:
        in_specs.append(ANY)
        args.append(acc)
    return pl.pallas_call(
        body, name="rs_chip_sum",
        grid_spec=pltpu.PrefetchScalarGridSpec(
            num_scalar_prefetch=1, grid=(rh // tr,), in_specs=in_specs,
            out_specs=pl.BlockSpec((None, None, tr, cols), lambda t, jr: (l, jr[1], t, 0))),
        out_shape=jax.ShapeDtypeStruct((layers, 2, rh, cols), F32),
        input_output_aliases={} if acc is None else {3: 0},
        compiler_params=_cparams(("parallel",)),
    )(*args)


def rs_pair_gather(rs):
    n = len(rs)

    def body(*refs):
        outs = refs[n:2 * n]
        send, recv = refs[2 * n:]
        x, y, c, jme, sib, chips = _place()
        cps = [_remote(outs[a].at[:, c], outs[a].at[:, c], send, recv, a, sib) for a in range(n)]
        for cp in cps:
            cp.start()
        for a in range(n):
            slot = outs[a].at[:, 1 - c]
            _remote(slot, slot, send, recv, a, sib).wait_recv()
        for cp in cps:
            cp.wait_send()

    return pl.pallas_call(
        body, name="rs_pair_gather", in_specs=[ANY] * n, out_specs=[ANY] * n,
        out_shape=[jax.ShapeDtypeStruct(r.shape, r.dtype) for r in rs],
        input_output_aliases={a: a for a in range(n)},
        scratch_shapes=[pltpu.SemaphoreType.DMA((n,)), pltpu.SemaphoreType.DMA((n,))],
    )(*rs)


def _adamw_math(w, g, m, v):
    m = B1 * m + (1.0 - B1) * g
    v = B2 * v + (1.0 - B2) * (g * g)
    m_hat = m / (1.0 - B1 ** STEP)
    v_hat = v / (1.0 - B2 ** STEP)
    return -LR * (m_hat / (jnp.sqrt(v_hat) + AEPS) + WD * w), m, v


def adamw(w, g, m, v, name, with_grad=False):
    rows, cols = w.shape
    tr = 256 if rows % 256 == 0 else rows
    fn = (lambda wv, gv, mv, vv: (gv,) + _adamw_math(wv, gv, mv, vv)) if with_grad else _adamw_math
    return rw(fn, [(a, 0, cols) for a in (w, g, m, v)], [(cols, F32)] * (4 if with_grad else 3), name, rows, tr=tr)


WEIGHTS = ["norm_ab", "w_in_ab", "pool_w", "pool_scale", "w_out_ab", "norm_cd", "w_in_cd", "sgu_ln_g", "sgu_ln_b",
           "sgu_w", "sgu_b", "s5_a_re", "s5_a_im", "s5_log_dt", "s5_b_re", "s5_b_im", "s5_c_re", "s5_c_im", "s5_d",
           "glu_w1", "glu_w2", "w_out_cd", "norm_x", "w_xq", "w_xkv", "w_xo", "mem_norm", "final_norm"]
INPUTS = ["x", "mem"] + WEIGHTS + ["loss_target"] + ["m_" + n for n in WEIGHTS] + ["v_" + n for n in WEIGHTS]
BIG = ["w_in_ab", "w_out_ab", "w_in_cd", "w_out_cd", "w_xq", "w_xkv", "w_xo", "glu_w1", "glu_w2", "pool_w"]
COL_SHARDED = ("w_in_ab", "w_in_cd", "w_xkv")
SMALL = [n for n in WEIGHTS if n not in BIG]
SMALL_SHARDED = {"norm_cd": 256, "sgu_ln_g": 256, "sgu_ln_b": 256, "s5_d": 128}
PACK = 256 * 128


def _pack(arrs):
    flat = jnp.concatenate([a.reshape(-1) for a in arrs])
    pad = (-flat.shape[0]) % PACK
    return jnp.concatenate([flat, jnp.zeros((pad,), flat.dtype)]).reshape(-1, 128)


def _unpack(packed, shapes):
    flat, out, off = packed.reshape(-1), [], 0
    for s in shapes:
        n = 1
        for d in s:
            n *= d
        out.append(flat[off:off + n].reshape(s))
        off += n
    return out


LAYER_KEYS = (("w_in", "w_out", "pool_w", "w_xq", "w_xkv", "w_xo"),
              ("w_in", "w_out", "glu_w1", "glu_w2", "w_xq", "w_xkv", "w_xo"))


def _weight_of(key, layer):
    if key in ("w_xq", "w_xkv", "w_xo"):
        return key, layer, 4
    kind = "ab" if layer % 2 == 0 else "cd"
    return {"w_in": "w_in_" + kind, "w_out": "w_out_" + kind}.get(key, key), layer // 2, 2


def kernel(*args):
    a = dict(zip(INPUTS, args))
    x_i, y_i, c_i = lax.axis_index("x"), lax.axis_index("y"), lax.axis_index("c")
    j = 2 * x_i + y_i

    slab = jnp.concatenate([a["norm_cd"], a["sgu_ln_g"], a["sgu_ln_b"],
                            jnp.pad(a["s5_d"], ((0, 0), (0, 128)))], axis=0)
    gslab = allgather_small(slab)
    P = {n: a[n] for n in SMALL}
    for k, n in enumerate(("norm_cd", "sgu_ln_g", "sgu_ln_b", "s5_d")):
        wd = SMALL_SHARDED[n]
        P[n] = gslab[:, 2 * k:2 * k + 2, :wd].transpose(1, 0, 2).reshape(2, 4 * wd)

    def shards_of(layer):
        keys = sorted(k for k in LAYER_KEYS[layer % 2])
        out = []
        for k in keys:
            n, l, _ = _weight_of(k, layer)
            out.append(a[n][l].reshape(-1, a[n].shape[-1]).astype(BF16))
        return keys, out

    keys0, sh0 = shards_of(0)
    first = keys0.index("w_in")
    g_in, token = allgather_sync([sh0[first].reshape(2, sh0[first].shape[0] // 2, sh0[first].shape[1])])
    w_in0 = g_in[0].reshape(4, -1, g_in[0].shape[-1])
    started = {}
    for layer in (0, 1, 2, 3):
        keys, sh = (keys0, sh0) if layer == 0 else shards_of(layer)
        rest = [(k, s) for k, s in zip(keys, sh) if k != "w_in"]
        parts = [("in", ["w_in"], [sh[keys.index("w_in")]])] * (layer > 0) + [("", *map(list, zip(*rest)))]
        for tag, pk, ps in parts:
            send, recv, ps, lands, token = allgather_start(ps, [token, gslab], "allgather_start_%d%s" % (layer, tag))
            started[(layer, tag)] = (pk, send, recv, ps, lands)
    P["norm_ab"] = P["norm_ab"] + token[0:1, 0:1]

    cidx = jnp.reshape(c_i, (1,)).astype(jnp.int32)
    jc = jnp.stack([j, c_i]).astype(jnp.int32)

    def views(g):
        W = {}
        for k, v in g.items():
            if k in ("w_in", "w_xkv"):
                W[k] = mcs(v)
            elif k == "pool_w":
                W[k] = v.reshape(4, 4, 64, 256).transpose(1, 0, 2, 3).reshape(4, 256, 256)
            elif k not in ("glu_w1", "glu_w2"):
                W[k] = m2(v.reshape(-1, v.shape[-1]))
        if "glu_w1" in g:
            W["w12"] = jnp.concatenate([g["glu_w1"].reshape(512, 512), g["glu_w2"].reshape(512, 512)], axis=1)
        return W

    def arrived(layer, tag, after):
        keys, send, recv, sh, lands = started[(layer, tag)]
        return views(dict(zip(keys, allgather_wait(send, recv, sh, lands, after, "allgather_wait_%d%s" % (layer, tag)))))

    def weights_of(layer, x_in):
        W = views({"w_in": w_in0}) if layer == 0 else arrived(layer, "in", x_in)
        W["more"] = lambda after: arrived(layer, "", after)
        return W

    halves, pending = {}, {}

    def finish_pair(layer, after):
        keys, send, recv, flat, lands = halves.pop(layer)
        flat, got = rs_pair_wait(send, recv, flat, lands, after, "rs_pair_wait_%d" % layer)
        pair = [rs_pair_sum(g4, r, cidx) for g4, r in zip(flat, got)]
        send, recv, pair, lands, token = rs_chip_start(pair, "rs_chip_start_%d" % layer)
        pending[layer] = (keys, send, recv, pair, lands)
        return token

    def grads_done(layer, GW):
        keys = sorted(GW)
        flat = [GW[k].reshape(4, 2, GW[k].shape[1] // 2, GW[k].shape[2]) for k in keys]
        send, recv, flat, lands, token = rs_pair_start(flat, "rs_pair_start_%d" % layer)
        halves[layer] = (keys, send, recv, flat, lands)
        if layer + 1 in halves:
            token = token + finish_pair(layer + 1, token)
        return token[0:1, 0:1]

    loss, dx, G = local_step(a["x"][0], a["mem"][0], a["loss_target"][0], P, weights_of, grads_done)
    loss = lax.psum(loss[0, 0], ("x", "y", "c"))
    finish_pair(0, dx)
    outs = {}

    def update_big(names, red):
        for n, g in zip(names, rs_pair_gather([red[n] for n in names])):
            shp = a[n].shape
            g2 = g.reshape(-1, shp[-1])
            upd = adamw(a[n].reshape(g2.shape), g2, a["m_" + n].reshape(g2.shape), a["v_" + n].reshape(g2.shape),
                        "adamw_" + n, with_grad=True)
            outs[n] = tuple(t.reshape(shp) for t in upd)

    def reduce_layer(layer, red, after):
        keys, send, recv, pair, lands = pending[layer]
        lands = rs_chip_wait(send, recv, pair, lands, after, "rs_chip_wait_%d" % layer)
        for k, q, p in zip(keys, lands, pair):
            n, l, layers = _weight_of(k, layer)
            red[n] = rs_chip_sum(q, p, l, red.get(n), layers, jc)

    red = {}
    for layer in (3, 2, 1):
        reduce_layer(layer, red, [dx])
    odd_only = [n for n in BIG if n.endswith("_cd") or n.startswith("glu")]
    update_big(odd_only, red)

    gfull = [jnp.stack(G[n]) if isinstance(G[n], list) else G[n] for n in SMALL]
    shapes = [g.shape for g in gfull]
    gsum = _unpack(allreduce_small(_pack(gfull)), shapes)
    gloc = []
    for n, g in zip(SMALL, gsum):
        if n in SMALL_SHARDED:
            g = lax.dynamic_slice_in_dim(g, j * SMALL_SHARDED[n], SMALL_SHARDED[n], axis=1)
        gloc.append(g)
    for n, g in zip(SMALL, gloc):
        shp = a[n].shape
        two = (-1, shp[-1]) if len(shp) > 1 else (1, shp[0])
        upd = adamw(a[n].reshape(two), g.reshape(two), a["m_" + n].reshape(two), a["v_" + n].reshape(two), "adamw_" + n)
        outs[n] = (g,) + tuple(t.reshape(shp) for t in upd)

    behind = [outs[n][1] for n in odd_only + SMALL[-1:]] + [red[n] for n in BIG if n not in odd_only]
    reduce_layer(0, red, behind)
    update_big([n for n in BIG if n not in odd_only], red)

    res = [loss, dx[None]]
    for part in range(4):
        res += [outs[n][part] for n in WEIGHTS]
    return tuple(res)
```

```python
import math

import jax
import jax.numpy as jnp
from jax import lax
from jax.experimental import pallas as pl
from jax.experimental.pallas import tpu as pltpu

F32, BF16 = jnp.float32, jnp.bfloat16
S, D = 2048, 1024
MEM = 256
EPS = 1e-6
NEG = -1e30
QB = 128
PATTERNS = (1, 4, 16)
NG, NP, NH = 32, 64, 16
NS = NG * NP
LR, B1, B2, AEPS, WD, STEP = 0.001, 0.9, 0.999, 1e-08, 0.01, 10
MESHID = pl.DeviceIdType.MESH
VMEM_LIMIT = 56 * 1024 * 1024


def _cparams(sem):
    return pltpu.CompilerParams(dimension_semantics=sem, vmem_limit_bytes=VMEM_LIMIT)


def _sig(x):
    return 1.0 / (1.0 + jnp.exp(-x))


def _dot(a, b, dims):
    return lax.dot_general(a, b, (dims, ((), ())), preferred_element_type=F32)


def _nn(a, b):
    return _dot(a, b, ((1,), (0,)))


def _nt(a, b):
    return _dot(a, b, ((1,), (1,)))


def _tn(a, b):
    return _dot(a, b, ((0,), (0,)))


_DIMS = {"nn": ((1,), (0,)), "nt": ((1,), (1,)), "tn": ((0,), (0,))}


def _tile(dim, cc=None, cap=1024):
    for t in (2048, 1536, 1024, 768, 512, 384, 256, 128):
        if t <= cap and dim % t == 0 and (cc is None or cc % t == 0):
            return t
    return dim


MM_VMEM = 36 * 1024 * 1024


def _mm_tiles(m, n, k, ccm, ccn, cck, a_bytes, b_bytes, o_bytes):
    caps = [1024, 1024, 2048]
    while True:
        tm, tn, tk = _tile(m, ccm, caps[0]), _tile(n, ccn, caps[1]), _tile(k, cck, caps[2])
        need = 2 * (tm * tk * a_bytes + tk * tn * b_bytes + tm * tn * o_bytes) + (tm * tn * 4 if tk < k else 0)
        if need <= MM_VMEM:
            return tm, tn, tk
        if tk > 1024:
            caps[2] = tk // 2
        elif tn >= tm:
            caps[1] = tn // 2
        else:
            caps[0] = tm // 2


def m2(arr, col_off=0, ncols=None):
    rows, cols = arr.shape
    ncols = cols - col_off if ncols is None else ncols

    def spec(tr, tc, rc):
        assert col_off % tc == 0
        return pl.BlockSpec((tr, tc), lambda *g: (rc(*g)[0], rc(*g)[1] + col_off // tc))
    return (arr, rows, ncols, spec, None if col_off == 0 else col_off)


def mcs(arr):
    cs = arr.shape[2]

    def spec(tr, tc, rc):
        n = cs // tc
        return pl.BlockSpec((None, tr, tc), lambda *g: (rc(*g)[1] // n, rc(*g)[0], rc(*g)[1] % n))
    return (arr, arr.shape[1], 4 * cs, spec, cs)


def out2(rows, cols):
    def spec(tr, tc, rc):
        return pl.BlockSpec((tr, tc), lambda *g: tuple(rc(*g)))
    return ((rows, cols), spec, None)


def outcs(rows, cs):
    def spec(tr, tc, rc):
        n = cs // tc
        return pl.BlockSpec((None, tr, tc), lambda *g: (rc(*g)[1] // n, rc(*g)[0], rc(*g)[1] % n))
    return ((4, rows, cs), spec, cs)


def _both(a, b):
    if a is None:
        return b
    if b is None:
        return a
    return math.gcd(a, b)


def mm(a, b, mode, name, add=None, out=None, out_dtype=F32):
    a_arr, a_r, a_c, a_spec, a_cc = a
    b_arr, b_r, b_c, b_spec, b_cc = b
    if mode == "nn":
        m, k, n = a_r, a_c, b_c
        assert b_r == k
        ccm, cck, ccn = None, a_cc, b_cc
    elif mode == "nt":
        m, k, n = a_r, a_c, b_r
        assert b_c == k
        ccm, cck, ccn = None, _both(a_cc, b_cc), None
    else:
        m, k, n = a_c, a_r, b_c
        assert b_r == k
        ccm, cck, ccn = a_cc, None, b_cc
    out = out2(m, n) if out is None else out
    o_shape, o_spec, o_cc = out
    ccn = _both(ccn, o_cc)
    if add is not None:
        ccn = _both(ccn, add[4])
    o_bytes = jnp.dtype(out_dtype).itemsize + (0 if add is None else add[0].dtype.itemsize)
    tm, tn, tk = _mm_tiles(m, n, k, ccm, ccn, cck, a_arr.dtype.itemsize, b_arr.dtype.itemsize, o_bytes)
    nk = k // tk
    if mode == "nn":
        in_specs = [a_spec(tm, tk, lambda i, j, kk: (i, kk)), b_spec(tk, tn, lambda i, j, kk: (kk, j))]
    elif mode == "nt":
        in_specs = [a_spec(tm, tk, lambda i, j, kk: (i, kk)), b_spec(tn, tk, lambda i, j, kk: (j, kk))]
    else:
        in_specs = [a_spec(tk, tm, lambda i, j, kk: (kk, i)), b_spec(tk, tn, lambda i, j, kk: (kk, j))]
    args = [a_arr, b_arr]
    if add is not None:
        in_specs.append(add[3](tm, tn, lambda i, j, kk: (i, j)))
        args.append(add[0])
    return _mm_call(args, in_specs, o_spec(tm, tn, lambda i, j, kk: (i, j)), jax.ShapeDtypeStruct(o_shape, out_dtype),
                    mode, (m // tm, n // tn, nk), (tm, tn), add is not None, name)


def _mm_call(args, in_specs, out_spec, out_shape, mode, grid, tile, has_add, name):
    dims = _DIMS[mode]
    nk = grid[2]
    tm, tn = tile

    def body(*refs):
        a_ref, b_ref = refs[0], refs[1]
        add_ref = refs[2] if has_add else None
        prod = _dot(a_ref[...].astype(BF16), b_ref[...].astype(BF16), dims)
        if nk == 1:
            o_ref = refs[-1]
            if has_add:
                prod = prod + add_ref[...].astype(F32)
            o_ref[...] = prod.astype(o_ref.dtype)
            return
        o_ref, acc = refs[-2], refs[-1]
        kk = pl.program_id(2)

        @pl.when(kk == 0)
        def _():
            acc[...] = prod

        @pl.when(kk > 0)
        def _():
            acc[...] += prod

        @pl.when(kk == nk - 1)
        def _():
            r = acc[...]
            if has_add:
                r = r + add_ref[...].astype(F32)
            o_ref[...] = r.astype(o_ref.dtype)

    return pl.pallas_call(
        body, name=name, grid=grid, in_specs=in_specs, out_specs=out_spec, out_shape=out_shape,
        scratch_shapes=[pltpu.VMEM((tm, tn), F32)] if nk > 1 else [],
        compiler_params=_cparams(("parallel", "parallel", "arbitrary")),
    )(*args)


def mm_band(a, b, mode, name, grid, blocks, maps, out_shape, add=None, out_dtype=F32):
    in_specs = [pl.BlockSpec(blocks[0], maps[0]), pl.BlockSpec(blocks[1], maps[1])]
    args = [a, b]
    if add is not None:
        in_specs.append(pl.BlockSpec(blocks[2], maps[2]))
        args.append(add)
    return _mm_call(args, in_specs, pl.BlockSpec(blocks[2], maps[2]), jax.ShapeDtypeStruct(out_shape, out_dtype),
                    mode, grid, blocks[2], add is not None, name)


def rw(fn, ins, outs, name, rows, tr=256, consts=(), accs=()):
    n_in, n_c, n_o, n_a = len(ins), len(consts), len(outs), len(accs)
    in_specs = []
    for arr, off, width in ins:
        assert off % width == 0
        in_specs.append(pl.BlockSpec((tr, width), lambda i, o=off // width: (i, o)))
    for c in consts:
        in_specs.append(pl.BlockSpec(c.shape, lambda i: (0, 0)))
    out_specs = [pl.BlockSpec((tr, w), lambda i: (i, 0)) for w, _ in outs]
    out_specs += [pl.BlockSpec(s, lambda i: (0, 0)) for s in accs]
    out_shape = [jax.ShapeDtypeStruct((rows, w), dt) for w, dt in outs]
    out_shape += [jax.ShapeDtypeStruct(s, F32) for s in accs]

    def body(*refs):
        vals = [r[...] for r in refs[:n_in + n_c]]
        o_refs = refs[n_in + n_c:n_in + n_c + n_o]
        a_refs = refs[n_in + n_c + n_o:]
        res = fn(*vals)
        for r, v in zip(o_refs, res[:n_o]):
            r[...] = v.astype(r.dtype)
        if n_a:
            @pl.when(pl.program_id(0) == 0)
            def _():
                for r in a_refs:
                    r[...] = jnp.zeros_like(r)
            for r, v in zip(a_refs, res[n_o:]):
                r[...] += v

    res = pl.pallas_call(
        body, name=name, grid=(rows // tr,), in_specs=in_specs, out_specs=out_specs,
        out_shape=out_shape,
        compiler_params=_cparams(("arbitrary",) if n_a else ("parallel",)),
    )(*[a for a, _, _ in ins], *consts)
    return res


def _rstd(x):
    return lax.rsqrt(jnp.mean(x * x, axis=-1, keepdims=True) + EPS)


def rms_fwd(x, g, name):
    def fn(xv, gv):
        xv = xv.astype(F32)
        return (xv * _rstd(xv) * gv,)
    return rw(fn, [(x, 0, D)], [(D, BF16)], name, x.shape[0], consts=[g])[0]


def _rms_bwd_math(xv, dy, gv):
    r = _rstd(xv)
    dyg = dy * gv
    dx = r * dyg - xv * (r * r * r / D) * jnp.sum(dyg * xv, axis=-1, keepdims=True)
    dg = jnp.sum(dy * xv * r, axis=0, keepdims=True)
    return dx, dg


def rms_bwd(x, dy, dres, g, name):
    def fn(xv, dyv, drv, gv):
        dx, dg = _rms_bwd_math(xv, dyv, gv)
        return dx + drv, dg
    return rw(fn, [(x, 0, D), (dy, 0, D), (dres, 0, D)], [(D, F32)], name, x.shape[0],
              consts=[g], accs=[(1, D)])


def final_loss(x, tgt, g):
    def fn(xv, tv, gv):
        e = xv * _rstd(xv) * gv - tv
        loss = 0.5 * jnp.sum(jnp.sum(e * e, axis=-1, keepdims=True), axis=0, keepdims=True) / D
        dx, dg = _rms_bwd_math(xv, e / D, gv)
        return dx, loss, dg
    return rw(fn, [(x, 0, D), (tgt, 0, D)], [(D, F32)], "final_loss", S, consts=[g],
              accs=[(1, 1), (1, D)])


def _attn_bias(bias_ref):
    ii = lax.broadcasted_iota(jnp.int32, (2 * QB, 2 * QB), 0) % QB
    jj = lax.broadcasted_iota(jnp.int32, (2 * QB, 2 * QB), 1)
    dist = ii + QB - jj
    band = (dist >= 0) & (dist <= QB)
    bias_ref[1] = jnp.where(band, 0.0, NEG)
    bias_ref[0] = jnp.where(band & (jj >= QB), 0.0, NEG)


def _two_heads(x, m0):
    return jnp.concatenate([jnp.where(m0, x, 0.0), jnp.where(m0, 0.0, x)], axis=0)


def _per_head(col, m0):
    return jnp.where(m0, col[:QB], col[QB:])


def _attn_rows(idx, d):
    if d == 1:
        b = idx
        cur = pl.ds(pl.multiple_of(b * QB, QB), QB)
        prev = pl.ds(pl.multiple_of(jnp.maximum(b - 1, 0) * QB, QB), QB)
    else:
        r, b = lax.rem(idx, d), lax.div(idx, d)
        cur = pl.ds(r + b * (QB * d), QB, stride=d)
        prev = pl.ds(r + jnp.maximum(b - 1, 0) * (QB * d), QB, stride=d)
    return cur, prev, b


NBLK = S // QB
GROUP = 4


def _colblk(off):
    return pl.BlockSpec((S, 128), lambda hp: (0, off * 8 + hp))


def attn_fwd(z):
    def body(q_ref, k_ref, v_ref, g_ref, o_ref, l_ref, a_ref, os, ls, bias):
        _attn_bias(bias)
        m0 = lax.broadcasted_iota(jnp.int32, (1, 128), 1) < 64
        for pi, d in enumerate(PATTERNS):
            lone = S // d == QB

            def load(idx, d=d, lone=lone):
                cur, prev, b = _attn_rows(idx, d)
                if lone:
                    return cur, (q_ref[cur, :], None, k_ref[cur, :], None, v_ref[cur, :], bias[1, :, QB:])
                return cur, (q_ref[cur, :], k_ref[prev, :], k_ref[cur, :], v_ref[prev, :], v_ref[cur, :],
                             bias[jnp.minimum(b, 1)])

            def block(q, kp, kc, vp, vc, bs):
                qq = _two_heads(q * 0.125, m0).astype(BF16)
                k = (kc if kp is None else jnp.concatenate([kp, kc], axis=0)).astype(BF16)
                s = _nt(qq, k) + bs
                mx = jnp.max(s, axis=-1, keepdims=True)
                p = jnp.exp(s - mx)
                den = jnp.sum(p, axis=-1, keepdims=True)
                pb = p.astype(BF16)
                vv = _two_heads(vc if vp is None else jnp.concatenate([vp, vc], axis=0), m0).astype(BF16)
                o = _nn(jnp.concatenate([pb[:QB], pb[QB:]], axis=1), vv)
                return o * _per_head(1.0 / den, m0), _per_head(mx + jnp.log(den), m0)

            def step(i, carry, pi=pi):
                loaded = [load(i * GROUP + u) for u in range(GROUP)]
                done = [block(*vals) for _, vals in loaded]
                for (cur, _), (o, l) in zip(loaded, done):
                    os[pi, cur, :] = o
                    ls[pi, cur, :] = l
                return carry
            lax.fori_loop(0, NBLK // GROUP, step, 0)
        l1, l2, l3 = ls[0], ls[1], ls[2]
        mx = jnp.maximum(jnp.maximum(l1, l2), l3)
        e1, e2, e3 = jnp.exp(l1 - mx), jnp.exp(l2 - mx), jnp.exp(l3 - mx)
        tot = e1 + e2 + e3
        o = (os[0] * e1 + os[1] * e2 + os[2] * e3) / tot
        ga = g_ref[...]
        o_ref[...] = o
        l_ref[...] = mx + jnp.log(tot)
        a_ref[...] = (o * (ga * _sig(ga))).astype(a_ref.dtype)

    out = pl.BlockSpec((S, 128), lambda hp: (0, hp))
    return pl.pallas_call(
        body, name="attn_fwd", grid=(8,),
        in_specs=[_colblk(0), _colblk(1), _colblk(2), _colblk(3)], out_specs=[out] * 3,
        out_shape=[jax.ShapeDtypeStruct((S, D), F32), jax.ShapeDtypeStruct((S, D), F32),
                   jax.ShapeDtypeStruct((S, 2 * D), BF16)],
        scratch_shapes=[pltpu.VMEM((3, S, 128), F32), pltpu.VMEM((3, S, 128), F32),
                        pltpu.VMEM((2, 2 * QB, 2 * QB), F32)],
        compiler_params=_cparams(("parallel",)),
    )(z, z, z, z)


def attn_bwd(z, d_cat, o, lse):
    def body(q_ref, k_ref, v_ref, g_ref, da_ref, o_ref, l_ref, dq_ref, dk_ref, dv_ref, dg_ref, do_s, pr_s, bias):
        _attn_bias(bias)
        m0 = lax.broadcasted_iota(jnp.int32, (1, 128), 1) < 64
        ga = g_ref[...]
        sg = _sig(ga)
        da = da_ref[...]
        ov = o_ref[...]
        do = da * (ga * sg)
        dg_ref[...] = da * ov * (sg * (1.0 + ga * (1.0 - sg)))
        do_s[...] = do
        pr_s[...] = do * ov
        dq_ref[...] = jnp.zeros_like(dq_ref)
        dk_ref[...] = jnp.zeros_like(dk_ref)
        dv_ref[...] = jnp.zeros_like(dv_ref)
        for d in PATTERNS:
            lone = S // d == QB

            def load(idx, d=d, lone=lone):
                cur, prev, b = _attn_rows(idx, d)
                if lone:
                    return (cur, None), (q_ref[cur, :], None, k_ref[cur, :], None, v_ref[cur, :],
                                         do_s[cur, :], pr_s[cur, :], l_ref[cur, :], bias[1, :, QB:])
                return (cur, prev), (q_ref[cur, :], k_ref[prev, :], k_ref[cur, :], v_ref[prev, :], v_ref[cur, :],
                                     do_s[cur, :], pr_s[cur, :], l_ref[cur, :], bias[jnp.minimum(b, 1)])

            def block(q, kp, kc, vp, vc, dof, prod, lp, bs):
                qq = _two_heads(q * 0.125, m0).astype(BF16)
                kf = kc if kp is None else jnp.concatenate([kp, kc], axis=0)
                k = kf.astype(BF16)
                v = (vc if vp is None else jnp.concatenate([vp, vc], axis=0)).astype(BF16)
                dd = _two_heads(dof, m0).astype(BF16)
                lh = jnp.max(jnp.concatenate([jnp.where(m0, lp, -jnp.inf), jnp.where(m0, -jnp.inf, lp)], axis=0),
                             axis=-1, keepdims=True)
                delta = jnp.sum(_two_heads(prod, m0), axis=-1, keepdims=True)
                p = jnp.exp(_nt(qq, k) + bs - lh)
                ds = (p * (_nt(dd, v) - delta)).astype(BF16)
                dq = _nn(jnp.concatenate([ds[:QB], ds[QB:]], axis=1), _two_heads(kf, m0).astype(BF16))
                return dq * 0.125, _tn(ds, qq), _tn(p.astype(BF16), dd)

            def step(i, carry):
                loaded = [load(i * GROUP + u) for u in range(GROUP)]
                done = [block(*vals) for _, vals in loaded]
                for ((cur, prev), _), (dq, dk, dv) in zip(loaded, done):
                    dq_ref[cur, :] = dq_ref[cur, :] + dq
                    if prev is not None:
                        dk_ref[prev, :] = dk_ref[prev, :] + dk[:QB]
                        dv_ref[prev, :] = dv_ref[prev, :] + dv[:QB]
                    dk_ref[cur, :] = dk_ref[cur, :] + dk[-QB:]
                    dv_ref[cur, :] = dv_ref[cur, :] + dv[-QB:]
                return carry
            lax.fori_loop(0, NBLK // GROUP, step, 0)

    blk = pl.BlockSpec((S, 128), lambda hp: (0, hp))
    return pl.pallas_call(
        body, name="attn_bwd", grid=(8,),
        in_specs=[_colblk(0), _colblk(1), _colblk(2), _colblk(3), blk, blk, blk], out_specs=[blk] * 4,
        out_shape=[jax.ShapeDtypeStruct((S, D), F32)] * 4,
        scratch_shapes=[pltpu.VMEM((S, 128), F32), pltpu.VMEM((S, 128), F32), pltpu.VMEM((2, 2 * QB, 2 * QB), F32)],
        compiler_params=_cparams(("parallel",)),
    )(z, z, z, z, d_cat, o, lse)


def assemble_dz_even(parts):
    def body(*refs):
        o_ref = refs[-1]
        for j in range(6):
            o_ref[:, j * D:(j + 1) * D] = refs[j][...].astype(o_ref.dtype)
    tr = 256
    blk = pl.BlockSpec((tr, D), lambda i: (i, 0))
    return pl.pallas_call(
        body, name="assemble_dz_even", grid=(S // tr,), in_specs=[blk] * 6,
        out_specs=pl.BlockSpec((tr, 6 * D), lambda i: (i, 0)),
        out_shape=jax.ShapeDtypeStruct((S, 6 * D), BF16),
        compiler_params=_cparams(("parallel",)),
    )(*parts)


def _pool_window(g):
    return jnp.where(g == 0, 2.0, jnp.where(g == 1, 4.0, jnp.where(g == 2, 8.0, 16.0)))


def _pool_sel(g, levels):
    return jnp.where(g == 0, levels[0], jnp.where(g == 1, levels[1], jnp.where(g == 2, levels[2], levels[3])))


def _pool_fwd_math(v, g):
    t = lax.broadcasted_iota(jnp.int32, (S, 1), 0)
    s = v
    levels = []
    for k in (1, 2, 4, 8):
        s = s + jnp.where(t >= k, pltpu.roll(s, k, 0), 0.0)
        levels.append(s)
    cnt = jnp.minimum((t + 1).astype(F32), _pool_window(g))
    return _pool_sel(g, levels) / cnt - v, cnt


def pool_fwd(z, pw, ps, cat):
    def body(v_ref, g_ref, pw_ref, ps_ref, cat_ref, o_ref):
        g = pl.program_id(0)
        pooled, _ = _pool_fwd_math(v_ref[...], g)
        mixed = _nn(pooled.astype(BF16), pw_ref[...].astype(BF16))
        gb = g_ref[...]
        o_ref[...] = (mixed * ps_ref[...] * (gb * _sig(gb))).astype(o_ref.dtype)

    return pl.pallas_call(
        body, name="pool_fwd", grid=(4,),
        in_specs=[pl.BlockSpec((S, 256), lambda g: (0, 16 + g)),
                  pl.BlockSpec((S, 256), lambda g: (0, 20 + g)),
                  pl.BlockSpec((None, 256, 256), lambda g: (g, 0, 0)),
                  pl.BlockSpec((1, 256), lambda g: (0, g)), pl.BlockSpec(memory_space=pl.ANY)],
        out_specs=pl.BlockSpec((S, 256), lambda g: (0, 4 + g)),
        out_shape=jax.ShapeDtypeStruct((S, 2 * D), BF16),
        input_output_aliases={4: 0},
        compiler_params=_cparams(("parallel",)),
    )(z, z, pw, ps, cat)


def pool_bwd(z, d_cat, pw, ps):
    def body(v_ref, g_ref, d_ref, pw_ref, ps_ref, dv_ref, dg_ref, dpw_ref, dps_ref):
        g = pl.program_id(0)
        v = v_ref[...]
        pooled, cnt = _pool_fwd_math(v, g)
        pwb = pw_ref[...].astype(BF16)
        pb = pooled.astype(BF16)
        mixed = _nn(pb, pwb)
        gb = g_ref[...]
        sg = _sig(gb)
        dout = d_ref[...]
        sc = ps_ref[...]
        dg_ref[...] = dout * mixed * sc * (sg * (1.0 + gb * (1.0 - sg)))
        dms = dout * (gb * sg)
        dps_ref[...] = jnp.sum(dms * mixed, axis=0, keepdims=True)
        dmx = (dms * sc).astype(BF16)
        dpw_ref[...] = _tn(pb, dmx)
        dpooled = _nt(dmx, pwb)
        t = lax.broadcasted_iota(jnp.int32, (S, 1), 0)
        s = dpooled / cnt
        levels = []
        for k in (1, 2, 4, 8):
            s = s + jnp.where(t < S - k, pltpu.roll(s, S - k, 0), 0.0)
            levels.append(s)
        dv_ref[...] = _pool_sel(g, levels) - dpooled

    return pl.pallas_call(
        body, name="pool_bwd", grid=(4,),
        in_specs=[pl.BlockSpec((S, 256), lambda g: (0, 16 + g)),
                  pl.BlockSpec((S, 256), lambda g: (0, 20 + g)),
                  pl.BlockSpec((S, 256), lambda g: (0, 4 + g)),
                  pl.BlockSpec((None, 256, 256), lambda g: (g, 0, 0)),
                  pl.BlockSpec((1, 256), lambda g: (0, g))],
        out_specs=[pl.BlockSpec((S, 256), lambda g: (0, g)),
                   pl.BlockSpec((S, 256), lambda g: (0, g)),
                   pl.BlockSpec((None, 256, 256), lambda g: (g, 0, 0)),
                   pl.BlockSpec((1, 256), lambda g: (0, g))],
        out_shape=[jax.ShapeDtypeStruct((S, D), F32), jax.ShapeDtypeStruct((S, D), F32),
                   jax.ShapeDtypeStruct((4, 256, 256), F32), jax.ShapeDtypeStruct((1, D), F32)],
        compiler_params=_cparams(("parallel",)),
    )(z, z, d_cat, pw, ps)


CH = 128


def _sgu_common(v, lng, lnb, w_ref):
    mu = jnp.mean(v, axis=-1, keepdims=True)
    vc = v - mu
    rs = lax.rsqrt(jnp.mean(vc * vc, axis=-1, keepdims=True) + EPS)
    xhat = vc * rs
    vn = (xhat * lng + lnb).astype(BF16)
    ri = lax.broadcasted_iota(jnp.int32, (CH, CH), 0)
    ci = lax.broadcasted_iota(jnp.int32, (CH, CH), 1)
    tril = ri >= ci
    ws = [jnp.where(tril, w_ref[g], 0.0).astype(BF16) for g in range(4)]
    return xhat, rs, vn, tril, ws


def _zspec(off):
    return pl.BlockSpec((CH, D), lambda c: (c, off))


def _full(shape):
    return pl.BlockSpec(shape, lambda c: (0,) * len(shape))


def sgu_fwd(z, lng, lnb, w, bfull):
    def body(u_ref, v_ref, g_ref, lng_ref, lnb_ref, w_ref, b_ref, o_ref):
        _, _, vn, _, ws = _sgu_common(v_ref[...], lng_ref[...], lnb_ref[...], w_ref)
        for g in range(4):
            sl = slice(g * 256, (g + 1) * 256)
            mixed = _nn(ws[g], vn[:, sl]) + b_ref[:, sl]
            gc = g_ref[:, sl]
            o_ref[:, sl] = (u_ref[:, sl] * mixed * (gc * _sig(gc))).astype(o_ref.dtype)

    return pl.pallas_call(
        body, name="sgu_fwd", grid=(S // CH,),
        in_specs=[_zspec(0), _zspec(1), _zspec(2), _full((1, D)), _full((1, D)),
                  _full((4, CH, CH)), _full((CH, D))],
        out_specs=pl.BlockSpec((CH, D), lambda c: (c, 0)),
        out_shape=jax.ShapeDtypeStruct((S, D), BF16),
        compiler_params=_cparams(("parallel",)),
    )(z, z, z, lng, lnb, w, bfull)


def sgu_bwd(z, d_cat, lng, lnb, w, bfull):
    def body(u_ref, v_ref, g_ref, d_ref, lng_ref, lnb_ref, w_ref, b_ref,
             du_ref, dv_ref, dg_ref, dw_ref, db_ref, dlg_ref, dlb_ref):
        @pl.when(pl.program_id(0) == 0)
        def _():
            dw_ref[...] = jnp.zeros_like(dw_ref)
            db_ref[...] = jnp.zeros_like(db_ref)
            dlg_ref[...] = jnp.zeros_like(dlg_ref)
            dlb_ref[...] = jnp.zeros_like(dlb_ref)

        lng = lng_ref[...]
        xhat, rs, vn, tril, ws = _sgu_common(v_ref[...], lng, lnb_ref[...], w_ref)
        lane = lax.broadcasted_iota(jnp.int32, (1, 128), 1)
        db = jnp.zeros((CH, 128), F32)
        dvn_parts = []
        for g in range(4):
            sl = slice(g * 256, (g + 1) * 256)
            mixed = _nn(ws[g], vn[:, sl]) + b_ref[:, sl]
            gc = g_ref[:, sl]
            sg = _sig(gc)
            u = u_ref[:, sl]
            dc = d_ref[:, sl]
            du_ref[:, sl] = dc * mixed * (gc * sg)
            dg_ref[:, sl] = dc * u * mixed * (sg * (1.0 + gc * (1.0 - sg)))
            dmx = dc * u * (gc * sg)
            db = db + jnp.where(lane == g, jnp.sum(dmx, axis=-1, keepdims=True), 0.0)
            dmb = dmx.astype(BF16)
            dw_ref[g] += jnp.where(tril, _nt(dmb, vn[:, sl]), 0.0)
            dvn_parts.append(_tn(ws[g], dmb))
        db_ref[...] += db
        dvn = jnp.concatenate(dvn_parts, axis=1)
        dlb_ref[...] += jnp.sum(dvn, axis=0, keepdims=True)
        dlg_ref[...] += jnp.sum(dvn * xhat, axis=0, keepdims=True)
        dxh = dvn * lng
        dv_ref[...] = rs * (dxh - jnp.mean(dxh, axis=-1, keepdims=True)
                            - xhat * jnp.mean(dxh * xhat, axis=-1, keepdims=True))

    row = pl.BlockSpec((CH, D), lambda c: (c, 0))
    return pl.pallas_call(
        body, name="sgu_bwd", grid=(S // CH,),
        in_specs=[_zspec(0), _zspec(1), _zspec(2), row, _full((1, D)), _full((1, D)),
                  _full((4, CH, CH)), _full((CH, D))],
        out_specs=[row, row, row, _full((4, CH, CH)), _full((CH, 128)), _full((1, D)), _full((1, D))],
        out_shape=[jax.ShapeDtypeStruct((S, D), F32)] * 3
        + [jax.ShapeDtypeStruct((4, CH, CH), F32), jax.ShapeDtypeStruct((CH, 128), F32),
           jax.ShapeDtypeStruct((1, D), F32), jax.ShapeDtypeStruct((1, D), F32)],
        compiler_params=_cparams(("arbitrary",)),
    )(z, z, z, d_cat, lng, lnb, w, bfull)


TB = 256


def _cmul(ar, ai, br, bi):
    return ar * br - ai * bi, ar * bi + ai * br


def _scan_consts(ar, ai, reverse):
    a2 = _cmul(ar, ai, ar, ai)
    a4 = _cmul(*a2, *a2)
    row = lax.broadcasted_iota(jnp.int32, (8, NS), 0)
    pr = jnp.zeros((8, NS), F32)
    pi = jnp.zeros((8, NS), F32)
    cr, ci = ar, ai
    for r in range(8):
        sel = row == (7 - r if reverse else r)
        pr = jnp.where(sel, cr, pr)
        pi = jnp.where(sel, ci, pi)
        cr, ci = _cmul(cr, ci, ar, ai)
    return ((ar, ai), a2, a4), (pr, pi), row


def scan_fwd(bu, abr, abi):
    def body(bu_ref, ar_ref, ai_ref, h_ref, car, cai):
        @pl.when(pl.program_id(0) == 0)
        def _():
            car[...] = jnp.zeros_like(car)
            cai[...] = jnp.zeros_like(cai)

        pows, (pr, pi), row = _scan_consts(ar_ref[...], ai_ref[...], False)

        def tile(t, carry):
            c_r, c_i = carry
            rows = pl.ds(pl.multiple_of(t * 8, 8), 8)
            xr = bu_ref[rows, 0:NS]
            xi = bu_ref[rows, NS:2 * NS]
            for k, (kr, ki) in zip((1, 2, 4), pows):
                sr = jnp.where(row >= k, pltpu.roll(xr, k, 0), 0.0)
                si = jnp.where(row >= k, pltpu.roll(xi, k, 0), 0.0)
                xr, xi = xr + kr * sr - ki * si, xi + kr * si + ki * sr
            xr, xi = xr + pr * c_r - pi * c_i, xi + pr * c_i + pi * c_r
            h_ref[rows, 0:NS] = xr
            h_ref[rows, NS:2 * NS] = xi
            return (jnp.broadcast_to(xr[7:8, :], (8, NS)), jnp.broadcast_to(xi[7:8, :], (8, NS)))

        c_r, c_i = lax.fori_loop(0, TB // 8, tile, (car[...], cai[...]))
        car[...] = c_r
        cai[...] = c_i

    return pl.pallas_call(
        body, name="s5_scan_fwd", grid=(S // TB,),
        in_specs=[pl.BlockSpec((TB, 2 * NS), lambda i: (i, 0)),
                  pl.BlockSpec((1, NS), lambda i: (0, 0)), pl.BlockSpec((1, NS), lambda i: (0, 0))],
        out_specs=pl.BlockSpec((TB, 2 * NS), lambda i: (i, 0)),
        out_shape=jax.ShapeDtypeStruct((S, 2 * NS), F32),
        scratch_shapes=[pltpu.VMEM((8, NS), F32), pltpu.VMEM((8, NS), F32)],
        compiler_params=_cparams(("arbitrary",)),
    )(bu, abr, abi)


def scan_bwd(eta, h, abr, abi):
    nt = S // TB

    def body(e_ref, h_ref, ar_ref, ai_ref, l_ref, da_ref, car, cai):
        @pl.when(pl.program_id(0) == 0)
        def _():
            car[...] = jnp.zeros_like(car)
            cai[...] = jnp.zeros_like(cai)
            da_ref[...] = jnp.zeros_like(da_ref)

        pows, (pr, pi), row = _scan_consts(ar_ref[...], -ai_ref[...], True)

        def tile(tt, carry):
            c_r, c_i, acr, aci = carry
            t = TB // 8 - 1 - tt
            rows = pl.ds(pl.multiple_of(t * 8, 8), 8)
            xr = e_ref[rows, 0:NS]
            xi = e_ref[rows, NS:2 * NS]
            for k, (kr, ki) in zip((1, 2, 4), pows):
                sr = jnp.where(row < 8 - k, pltpu.roll(xr, 8 - k, 0), 0.0)
                si = jnp.where(row < 8 - k, pltpu.roll(xi, 8 - k, 0), 0.0)
                xr, xi = xr + kr * sr - ki * si, xi + kr * si + ki * sr
            xr, xi = xr + pr * c_r - pi * c_i, xi + pr * c_i + pi * c_r
            l_ref[rows, 0:NS] = xr
            l_ref[rows, NS:2 * NS] = xi
            nr = jnp.where(row < 7, pltpu.roll(xr, 7, 0), c_r)
            ni = jnp.where(row < 7, pltpu.roll(xi, 7, 0), c_i)
            hr = h_ref[rows, 0:NS]
            hi = h_ref[rows, NS:2 * NS]
            acr = acr + hr * nr + hi * ni
            aci = aci + hr * ni - hi * nr
            return (jnp.broadcast_to(xr[0:1, :], (8, NS)), jnp.broadcast_to(xi[0:1, :], (8, NS)), acr, aci)

        zero = jnp.zeros((8, NS), F32)
        c_r, c_i, acr, aci = lax.fori_loop(0, TB // 8, tile, (car[...], cai[...], zero, zero))
        car[...] = c_r
        cai[...] = c_i
        da_ref[:, 0:NS] += acr
        da_ref[:, NS:2 * NS] += aci

    rev = pl.BlockSpec((TB, 2 * NS), lambda i: (nt - 1 - i, 0))
    return pl.pallas_call(
        body, name="s5_scan_bwd", grid=(nt,),
        in_specs=[rev, rev, pl.BlockSpec((1, NS), lambda i: (0, 0)), pl.BlockSpec((1, NS), lambda i: (0, 0))],
        out_specs=[rev, pl.BlockSpec((8, 2 * NS), lambda i: (0, 0))],
        out_shape=[jax.ShapeDtypeStruct((S, 2 * NS), F32), jax.ShapeDtypeStruct((8, 2 * NS), F32)],
        scratch_shapes=[pltpu.VMEM((8, NS), F32), pltpu.VMEM((8, NS), F32)],
        compiler_params=_cparams(("arbitrary",)),
    )(eta, h, abr, abi)


GC = 0.7978845608028654
GA = 0.044715


def s5_post(hc, z, dskip):
    def fn(hv, xd, dv):
        y = hv + dv * xd
        return y, 0.5 * y * (1.0 + jnp.tanh(GC * (y + GA * y * y * y)))
    return rw(fn, [(hc, 0, 512), (z, 3072, 512)], [(512, F32), (512, BF16)], "s5_post", S, consts=[dskip])


def s5_post_bwd(dyg, ypre, z, dskip):
    def fn(dy, y, xd, dv):
        th = jnp.tanh(GC * (y + GA * y * y * y))
        dg = 0.5 * (1.0 + th) + 0.5 * y * (1.0 - th * th) * GC * (1.0 + 3.0 * GA * y * y)
        dyp = dy * dg
        return dyp, dyp * dv, jnp.sum(dyp * xd, axis=0, keepdims=True)
    return rw(fn, [(dyg, 0, 512), (ypre, 0, 512), (z, 3072, 512)], [(512, BF16), (512, F32)],
              "s5_post_bwd", S, consts=[dskip], accs=[(1, 512)])


def glu_fwd(t, z, c_out):
    def fn(t1, t2, gd, co):
        return (jnp.concatenate([co, (t1 * _sig(t2) * (gd * _sig(gd))).astype(BF16)], axis=1),)
    return rw(fn, [(t, 0, 512), (t, 512, 512), (z, 3584, 512), (c_out, 0, D)], [(D + 512, BF16)], "glu_fwd", S)[0]


def glu_bwd(t, z, d_cat):
    def fn(t1, t2, gd, dd):
        s2, sg = _sig(t2), _sig(gd)
        sl = gd * sg
        return (jnp.concatenate([dd * s2 * sl, dd * t1 * s2 * (1.0 - s2) * sl], axis=1),
                dd * t1 * s2 * (sg * (1.0 + gd * (1.0 - sg))))
    return rw(fn, [(t, 0, 512), (t, 512, 512), (z, 3584, 512), (d_cat, 1024, 512)],
              [(D, BF16), (512, F32)], "glu_bwd", S)


def assemble_dz_odd(du, dv, dgc, dxd, dgd):
    def body(a, b, c, d, e, o_ref):
        o_ref[:, 0:D] = a[...].astype(BF16)
        o_ref[:, D:2 * D] = b[...].astype(BF16)
        o_ref[:, 2 * D:3 * D] = c[...].astype(BF16)
        o_ref[:, 3 * D:3 * D + 512] = d[...].astype(BF16)
        o_ref[:, 3 * D + 512:4 * D] = e[...].astype(BF16)
    tr = 256
    blk = pl.BlockSpec((tr, D), lambda i: (i, 0))
    half = pl.BlockSpec((tr, 512), lambda i: (i, 0))
    return pl.pallas_call(
        body, name="assemble_dz_odd", grid=(S // tr,), in_specs=[blk, blk, blk, half, half],
        out_specs=pl.BlockSpec((tr, 4 * D), lambda i: (i, 0)),
        out_shape=jax.ShapeDtypeStruct((S, 4 * D), BF16),
        compiler_params=_cparams(("parallel",)),
    )(du, dv, dgc, dxd, dgd)


TQ = 256


def _xattn_probs(qh, kh):
    s = _nt(qh, kh) * 0.0625
    p = jnp.exp(s - jnp.max(s, axis=-1, keepdims=True))
    return p / jnp.sum(p, axis=-1, keepdims=True)


def xattn_fwd(q, kv):
    def body(q_ref, kv_ref, o_ref):
        for h in range(4):
            sl = slice(h * 256, (h + 1) * 256)
            p = _xattn_probs(q_ref[:, sl].astype(BF16), kv_ref[:, sl].astype(BF16))
            vh = kv_ref[:, D + h * 256:D + (h + 1) * 256].astype(BF16)
            o_ref[:, sl] = _nn(p.astype(BF16), vh).astype(o_ref.dtype)

    return pl.pallas_call(
        body, name="xattn_fwd", grid=(S // TQ,),
        in_specs=[pl.BlockSpec((TQ, D), lambda i: (i, 0)), pl.BlockSpec((MEM, 2 * D), lambda i: (0, 0))],
        out_specs=pl.BlockSpec((TQ, D), lambda i: (i, 0)),
        out_shape=jax.ShapeDtypeStruct((S, D), BF16),
        compiler_params=_cparams(("parallel",)),
    )(q, kv)


def xattn_bwd(q, kv, d_o):
    def body(q_ref, kv_ref, do_ref, dq_ref, dkv_ref):
        @pl.when(pl.program_id(0) == 0)
        def _():
            dkv_ref[...] = jnp.zeros_like(dkv_ref)

        for h in range(4):
            sl = slice(h * 256, (h + 1) * 256)
            vs = slice(D + h * 256, D + (h + 1) * 256)
            qh = q_ref[:, sl].astype(BF16)
            kh = kv_ref[:, sl].astype(BF16)
            vh = kv_ref[:, vs].astype(BF16)
            doh = do_ref[:, sl].astype(BF16)
            p = _xattn_probs(qh, kh)
            dp = _nt(doh, vh)
            ds = (p * (dp - jnp.sum(p * dp, axis=-1, keepdims=True)) * 0.0625).astype(BF16)
            dq_ref[:, sl] = _nn(ds, kh).astype(dq_ref.dtype)
            dkv_ref[:, sl] += _tn(ds, qh)
            dkv_ref[:, vs] += _tn(p.astype(BF16), doh)

    return pl.pallas_call(
        body, name="xattn_bwd", grid=(S // TQ,),
        in_specs=[pl.BlockSpec((TQ, D), lambda i: (i, 0)), pl.BlockSpec((MEM, 2 * D), lambda i: (0, 0)),
                  pl.BlockSpec((TQ, D), lambda i: (i, 0))],
        out_specs=[pl.BlockSpec((TQ, D), lambda i: (i, 0)), pl.BlockSpec((MEM, 2 * D), lambda i: (0, 0))],
        out_shape=[jax.ShapeDtypeStruct((S, D), BF16), jax.ShapeDtypeStruct((MEM, 2 * D), F32)],
        compiler_params=_cparams(("arbitrary",)),
    )(q, kv, d_o)


def _s5_disc(a_re, a_im, log_dt, b_re, b_im):
    dt = jnp.exp(log_dt)[:, None]
    mag = jnp.exp(dt * a_re)
    abr = mag * jnp.cos(dt * a_im)
    abi = mag * jnp.sin(dt * a_im)
    nr, ni = abr - 1.0, abi
    inv = 1.0 / (a_re * a_re + a_im * a_im)
    cr = (nr * a_re + ni * a_im) * inv
    ci = (ni * a_re - nr * a_im) * inv
    bbr = cr[..., None] * b_re - ci[..., None] * b_im
    bbi = cr[..., None] * b_im + ci[..., None] * b_re
    return abr, abi, bbr, bbi


VM = pl.BlockSpec(memory_space=pltpu.VMEM)


def s5_embed(bt_re, bt_im, ct_re, ct_im):
    def body(br, bi, cr, ci, b_ref, c_ref):
        b_ref[...] = jnp.zeros_like(b_ref)
        c_ref[...] = jnp.zeros_like(c_ref)
        for g in range(NG):
            rows, cols = slice(g * NH, (g + 1) * NH), slice(g * NP, (g + 1) * NP)
            b_ref[rows, cols] = br[g]
            b_ref[rows, NS + g * NP:NS + (g + 1) * NP] = bi[g]
            c_ref[cols, rows] = cr[g]
            c_ref[NS + g * NP:NS + (g + 1) * NP, rows] = -ci[g]

    return pl.pallas_call(
        body, name="s5_embed", in_specs=[VM] * 4, out_specs=[VM] * 2,
        out_shape=[jax.ShapeDtypeStruct((NG * NH, 2 * NS), F32), jax.ShapeDtypeStruct((2 * NS, NG * NH), F32)],
        compiler_params=pltpu.CompilerParams(vmem_limit_bytes=VMEM_LIMIT),
    )(bt_re, bt_im, ct_re, ct_im)


def s5_extract(gb, gc):
    def body(gb_ref, gc_ref, br, bi, cr, ci):
        for g in range(NG):
            rows, cols = slice(g * NH, (g + 1) * NH), slice(g * NP, (g + 1) * NP)
            br[g] = gb_ref[rows, cols]
            bi[g] = gb_ref[rows, NS + g * NP:NS + (g + 1) * NP]
            cr[g] = gc_ref[cols, rows]
            ci[g] = -gc_ref[NS + g * NP:NS + (g + 1) * NP, rows]

    return pl.pallas_call(
        body, name="s5_extract", in_specs=[VM] * 2, out_specs=[VM] * 4,
        out_shape=[jax.ShapeDtypeStruct((NG, NH, NP), F32)] * 2 + [jax.ShapeDtypeStruct((NG, NP, NH), F32)] * 2,
        compiler_params=pltpu.CompilerParams(vmem_limit_bytes=VMEM_LIMIT),
    )(gb, gc)


HC, HS = NG * NH // 2, NS // 2
TS = 1024


def s5_to_states(x, w, mode, name, z_off=0):
    if mode == "nn":
        wb, wm = (HC, HS), lambda i, j, kk: (j % 2, j)
    else:
        wb, wm = (HS, HC), lambda i, j, kk: (j, j % 2)
    return mm_band(x, w, mode, name, (S // TS, 4, 1), ((TS, HC), wb, (TS, HS)),
                   (lambda i, j, kk: (i, z_off + j % 2), wm, lambda i, j, kk: (i, j)), (S, 2 * NS))


def s5_to_channels(x, w, mode, name, add=None):
    if mode == "nn":
        wb, wm = (HS, HC), lambda i, j, kk: (j + 2 * kk, j)
    else:
        wb, wm = (HC, HS), lambda i, j, kk: (j, j + 2 * kk)
    return mm_band(x, w, mode, name, (S // TS, 2, 2), ((TS, HS), wb, (TS, HC)),
                   (lambda i, j, kk: (i, j + 2 * kk), wm, lambda i, j, kk: (i, j)), (S, NG * NH), add=add)


def s5_outer(a, b, name, states_first, z_off=0):
    if states_first:
        return mm_band(a, b, "tn", name, (4, 1, 1), ((S, HS), (S, HC), (HS, HC)),
                       (lambda i, j, kk: (0, i), lambda i, j, kk: (0, i % 2), lambda i, j, kk: (i, i % 2)),
                       (2 * NS, NG * NH))
    return mm_band(a, b, "tn", name, (1, 4, 1), ((S, HC), (S, HS), (HC, HS)),
                   (lambda i, j, kk: (0, z_off + j % 2), lambda i, j, kk: (0, j), lambda i, j, kk: (j % 2, j)),
                   (NG * NH, 2 * NS))


def _fwd_even(i, x, P, W):
    hn = rms_fwd(x, P["norm_ab"][i:i + 1], "rms_ab_fwd")
    z = mm(m2(hn), W["w_in"], "nn", "in_ab")
    o, lse, cat = attn_fwd(z)
    if "more" in W:
        W.update(W.pop("more")(cat))
    cat = pool_fwd(z, W["pool_w"], P["pool_scale"][i:i + 1], cat)
    x_mid = mm(m2(cat), W["w_out"], "nn", "out_ab", add=m2(x))
    return x_mid, dict(x=x, hn=hn, z=z, o=o, lse=lse, cat=cat)


def _bwd_even(i, dx_mid, sv, P, W, G, GW):
    z = sv["z"]
    d_cat = mm(m2(dx_mid), W["w_out"], "nt", "out_ab_dx")
    GW["w_out"] = mm(m2(sv["cat"]), m2(dx_mid), "tn", "out_ab_dw").reshape(4, 512, D)
    dq, dk, dv, dga = attn_bwd(z, d_cat, sv["o"], sv["lse"])
    dvb, dgb, dpw, dps = pool_bwd(z, d_cat, W["pool_w"], P["pool_scale"][i:i + 1])
    GW["pool_w"] = dpw.reshape(4, 4, 64, 256).transpose(1, 0, 2, 3).reshape(4, 256, 256)
    G["pool_scale"][i] = dps[0]
    d_z = assemble_dz_even((dq, dk, dv, dga, dvb, dgb))
    d_hn = mm(m2(d_z), W["w_in"], "nt", "in_ab_dx")
    GW["w_in"] = mm(m2(sv["hn"]), m2(d_z), "tn", "in_ab_dw", out=outcs(D, 1536))
    return d_hn, P["norm_ab"][i:i + 1], "norm_ab", "rms_ab_bwd"


def _fwd_odd(i, x, P, W):
    hn = rms_fwd(x, P["norm_cd"][i:i + 1], "rms_cd_fwd")
    z = mm(m2(hn), W["w_in"], "nn", "in_cd")
    bfull = jnp.repeat(P["sgu_b"][i].T, 256, axis=1)
    c_out = sgu_fwd(z, P["sgu_ln_g"][i:i + 1], P["sgu_ln_b"][i:i + 1], P["sgu_w"][i], bfull)
    disc, disc_vjp = jax.vjp(_s5_disc, P["s5_a_re"][i], P["s5_a_im"][i], P["s5_log_dt"][i],
                             P["s5_b_re"][i], P["s5_b_im"][i])
    abr, abi, bbr, bbi = disc
    bbd, cbd = s5_embed(bbr.transpose(0, 2, 1), bbi.transpose(0, 2, 1),
                        P["s5_c_re"][i].transpose(0, 2, 1), P["s5_c_im"][i].transpose(0, 2, 1))
    abr, abi = abr.reshape(1, NS), abi.reshape(1, NS)
    bu = s5_to_states(z, bbd, "nn", "s5_bu", z_off=3072 // HC)
    h = scan_fwd(bu, abr, abi)
    hc = s5_to_channels(h, cbd, "nn", "s5_hc")
    dskip = P["s5_d"][i:i + 1]
    ypre, yg = s5_post(hc, z, dskip)
    if "more" in W:
        W.update(W.pop("more")(yg))
    w12 = W["w12"]
    t = mm(m2(yg), m2(w12), "nn", "glu_t")
    cat = glu_fwd(t, z, c_out)
    x_mid = mm(m2(cat), W["w_out"], "nn", "out_cd", add=m2(x))
    return x_mid, dict(x=x, hn=hn, z=z, bfull=bfull, disc_vjp=disc_vjp, bbd=bbd, cbd=cbd, abr=abr,
                       abi=abi, h=h, ypre=ypre, yg=yg, w12=w12, t=t, cat=cat, dskip=dskip)


def _bwd_odd(i, dx_mid, sv, P, W, G, GW):
    z = sv["z"]
    d_cat = mm(m2(dx_mid), W["w_out"], "nt", "out_cd_dx")
    GW["w_out"] = mm(m2(sv["cat"]), m2(dx_mid), "tn", "out_cd_dw").reshape(4, 384, D)
    du, dv, dgc, dws, dbs, dlg, dlb = sgu_bwd(z, d_cat, P["sgu_ln_g"][i:i + 1], P["sgu_ln_b"][i:i + 1],
                                               P["sgu_w"][i], sv["bfull"])
    G["sgu_w"][i], G["sgu_b"][i] = dws, dbs[:, :4].T
    G["sgu_ln_g"][i], G["sgu_ln_b"][i] = dlg[0], dlb[0]
    dt, dgd = glu_bwd(sv["t"], z, d_cat)
    gw12 = mm(m2(sv["yg"]), m2(dt), "tn", "glu_dw")
    GW["glu_w1"] = gw12[:, :512].reshape(4, 128, 512)
    GW["glu_w2"] = gw12[:, 512:].reshape(4, 128, 512)
    dyg = mm(m2(dt), m2(sv["w12"]), "nt", "glu_dx")
    dypre, dxd1, dd = s5_post_bwd(dyg, sv["ypre"], z, sv["dskip"])
    G["s5_d"][i] = dd[0]
    gcbd = s5_outer(sv["h"], dypre, "s5_dc", states_first=True)
    eta = s5_to_states(dypre, sv["cbd"], "nt", "s5_eta")
    lam, dacc = scan_bwd(eta, sv["h"], sv["abr"], sv["abi"])
    gbbd = s5_outer(z, lam, "s5_db", states_first=False, z_off=3072 // HC)
    dxd = s5_to_channels(lam, sv["bbd"], "nt", "s5_dx", add=dxd1)
    dacc = jnp.sum(dacc, axis=0)
    dbt_re, dbt_im, dct_re, dct_im = s5_extract(gbbd, gcbd)
    G["s5_c_re"][i], G["s5_c_im"][i] = dct_re.transpose(0, 2, 1), dct_im.transpose(0, 2, 1)
    d_bbr, d_bbi = dbt_re.transpose(0, 2, 1), dbt_im.transpose(0, 2, 1)
    (G["s5_a_re"][i], G["s5_a_im"][i], G["s5_log_dt"][i], G["s5_b_re"][i], G["s5_b_im"][i]) = sv["disc_vjp"](
        (dacc[:NS].reshape(NG, NP), dacc[NS:].reshape(NG, NP), d_bbr, d_bbi))
    d_z = assemble_dz_odd(du, dv, dgc, dxd, dgd)
    d_hn = mm(m2(d_z), W["w_in"], "nt", "in_cd_dx")
    GW["w_in"] = mm(m2(sv["hn"]), m2(d_z), "tn", "in_cd_dw", out=outcs(D, 1024))
    return d_hn, P["norm_cd"][i:i + 1], "norm_cd", "rms_cd_bwd"


def _fwd_x(l, x, mem_n, P, W):
    hx = rms_fwd(x, P["norm_x"][l:l + 1], "rms_x_fwd")
    q = mm(m2(hx), W["w_xq"], "nn", "xq", out_dtype=BF16)
    kv = mm(m2(mem_n), W["w_xkv"], "nn", "xkv", out_dtype=BF16)
    ox = xattn_fwd(q, kv)
    x_out = mm(m2(ox), W["w_xo"], "nn", "xo", add=m2(x))
    return x_out, dict(x=x, hx=hx, q=q, kv=kv, ox=ox)


def _bwd_x(l, dx_out, sv, mem_n, d_memn, P, W, G, GW):
    d_ox = mm(m2(dx_out), W["w_xo"], "nt", "xo_dx", out_dtype=BF16)
    GW["w_xo"] = mm(m2(sv["ox"]), m2(dx_out), "tn", "xo_dw").reshape(4, 256, D)
    dq, dkv = xattn_bwd(sv["q"], sv["kv"], d_ox)
    GW["w_xq"] = mm(m2(sv["hx"]), m2(dq), "tn", "xq_dw").reshape(4, 256, D)
    d_hx = mm(m2(dq), W["w_xq"], "nt", "xq_dx")
    GW["w_xkv"] = mm(m2(mem_n), m2(dkv), "tn", "xkv_dw", out=outcs(D, 512))
    d_memn = mm(m2(dkv), W["w_xkv"], "nt", "xkv_dx", add=None if d_memn is None else m2(d_memn))
    dx, dg = rms_bwd(sv["x"], d_hx, dx_out, P["norm_x"][l:l + 1], "rms_x_bwd")
    G["norm_x"][l] = dg[0]
    return dx, d_memn


SMALL_LAYERS = (("norm_ab", 2), ("pool_scale", 2), ("norm_cd", 2), ("sgu_ln_g", 2), ("sgu_ln_b", 2), ("sgu_w", 2),
                ("sgu_b", 2), ("s5_a_re", 2), ("s5_a_im", 2), ("s5_log_dt", 2), ("s5_b_re", 2), ("s5_b_im", 2),
                ("s5_c_re", 2), ("s5_c_im", 2), ("s5_d", 2), ("norm_x", 4))


def local_step(x, mem, tgt, P, weights_of, grads_done):
    G = {k: [None] * n for k, n in SMALL_LAYERS}
    mem_g = P["mem_norm"].reshape(1, D)
    mem_n = rms_fwd(mem, mem_g, "rms_mem_fwd")
    saved = []
    for layer in range(4):
        i = layer // 2
        W = weights_of(layer, x)
        x, sv_m = (_fwd_even if layer % 2 == 0 else _fwd_odd)(i, x, P, W)
        x, sv_x = _fwd_x(layer, x, mem_n, P, W)
        saved.append((sv_m, sv_x, W))
    dx, loss, dgf = final_loss(x, tgt, P["final_norm"].reshape(1, D))
    G["final_norm"] = dgf[0]
    d_memn = None
    for layer in reversed(range(4)):
        i = layer // 2
        sv_m, sv_x, W = saved[layer]
        GW = {}
        dx_mid, d_memn = _bwd_x(layer, dx, sv_x, mem_n, d_memn, P, W, G, GW)
        d_hn, g, key, name = (_bwd_even if layer % 2 == 0 else _bwd_odd)(i, dx_mid, sv_m, P, W, G, GW)
        token = grads_done(layer, GW)
        if token is not None:
            g = g + token
        dx, dg = rms_bwd(sv_m["x"], d_hn, dx_mid, g, name)
        G[key][i] = dg[0]
    _, dgm = rms_bwd(mem, d_memn, d_memn, mem_g, "rms_mem_bwd")
    G["mem_norm"] = dgm[0]
    return loss, dx, G


ANY = pl.BlockSpec(memory_space=pl.ANY)


def _place():
    x, y, c = lax.axis_index("x"), lax.axis_index("y"), lax.axis_index("c")
    chips = [(1 - x, y), (x, 1 - y), (1 - x, 1 - y)]
    return x, y, c, 2 * x + y, (x, y, 1 - c), chips


def _remote(src, dst, send, recv, k, dev):
    return pltpu.make_async_remote_copy(src_ref=src, dst_ref=dst, send_sem=send.at[k], recv_sem=recv.at[k],
                                        device_id=dev, device_id_type=MESHID)


HBM = pl.BlockSpec(memory_space=pltpu.HBM)
SEM = pl.BlockSpec(memory_space=pltpu.SEMAPHORE)
EFFECT = pltpu.SideEffectType.DATAFLOW_SIDE_EFFECTING


def _hbm(t):
    return pltpu.with_memory_space_constraint(t, pltpu.HBM)


def allgather_sync(shards):
    n = len(shards)

    def body(*refs):
        ins, outs = refs[:n], refs[n:2 * n]
        token, send, recv = refs[2 * n:]
        x, y, c, jme, sib, chips = _place()
        first, passed = [], []
        for a in range(n):
            cp = _remote(ins[a], outs[a].at[jme], send, recv, a * 7 + 6, sib)
            cp.start()
            first.append(cp)
            for k, chip in enumerate(chips):
                cp = _remote(ins[a].at[c], outs[a].at[jme, c], send, recv, a * 7 + k, (*chip, c))
                cp.start()
                first.append(cp)
        for a in range(n):
            for k, chip in enumerate(chips):
                piece = outs[a].at[2 * chip[0] + chip[1], c]
                _remote(piece, piece, send, recv, a * 7 + k, (*chip, c)).wait_recv()
                fw = _remote(piece, piece, send, recv, a * 7 + 3 + k, sib)
                fw.start()
                passed.append(fw)
        for a in range(n):
            own = outs[a].at[jme]
            _remote(own, own, send, recv, a * 7 + 6, sib).wait_recv()
            for k, chip in enumerate(chips):
                piece = outs[a].at[2 * chip[0] + chip[1], 1 - c]
                _remote(piece, piece, send, recv, a * 7 + 3 + k, sib).wait_recv()
        for cp in first + passed:
            cp.wait_send()
        token[...] = jnp.zeros_like(token)

    res = pl.pallas_call(
        body, name="allgather_sync", in_specs=[ANY] * n,
        out_specs=[ANY] * n + [pl.BlockSpec(memory_space=pltpu.VMEM)],
        out_shape=[jax.ShapeDtypeStruct((4,) + s.shape, s.dtype) for s in shards] + [jax.ShapeDtypeStruct((8, 128), F32)],
        scratch_shapes=[pltpu.SemaphoreType.DMA((7 * n,)), pltpu.SemaphoreType.DMA((7 * n,))],
    )(*shards)
    return list(res[:n]), res[n]


def _gather_copies(ins, lands, send, recv):
    x, y, c, jme, sib, chips = _place()
    devs = [(*chip, c) for chip in chips] + [sib]
    return [_remote(ins[a], lands[a].at[jme], send, recv, a * 4 + k, dev)
            for a in range(len(ins)) for k, dev in enumerate(devs)]


def allgather_start(shards, after, name):
    n, na = len(shards), len(after)

    def body(*refs):
        ins, lands = refs[:n], refs[n:2 * n]
        send, recv = refs[2 * n + na], refs[2 * n + na + 1]
        token = refs[-1]
        for cp in _gather_copies(ins, lands, send, recv):
            cp.start()
        token[...] = jnp.zeros_like(token)

    res = pl.pallas_call(
        body, name=name,
        out_shape=(pltpu.SemaphoreType.DMA((4 * n,)), pltpu.SemaphoreType.DMA((4 * n,)),
                   *[pltpu.HBM(s.shape, s.dtype) for s in shards],
                   *[pltpu.HBM((4,) + s.shape, s.dtype) for s in shards],
                   jax.ShapeDtypeStruct((8, 128), F32)),
        in_specs=[HBM] * (2 * n) + [ANY] * na,
        out_specs=(SEM, SEM, *[HBM] * (2 * n), pl.BlockSpec(memory_space=pltpu.VMEM)),
        input_output_aliases={a: 2 + a for a in range(2 * n)},
        compiler_params=pltpu.CompilerParams(has_side_effects=EFFECT),
    )(*[_hbm(s) for s in shards], *[_hbm(lax.empty((4,) + s.shape, s.dtype)) for s in shards], *after)
    return res[0], res[1], list(res[2:2 + n]), list(res[2 + n:2 + 2 * n]), res[-1]


def allgather_wait(send, recv, shards, lands, after, name):
    n = len(shards)

    def body(*refs):
        ins, zones = refs[:n], refs[n:2 * n]
        send_r, recv_r = refs[2 * n], refs[2 * n + 1]
        x, y, c, jme, sib, chips = _place()
        slots = [2 * chip[0] + chip[1] for chip in chips] + [jme]
        for a in range(n):
            for k, slot in enumerate(slots):
                cp = _remote(ins[a], zones[a].at[slot], send_r, recv_r, a * 4 + k, sib)
                cp.wait_send()
                cp.wait_recv()

    res = pl.pallas_call(
        body, name=name,
        out_shape=tuple(pltpu.HBM(t.shape, t.dtype) for t in list(shards) + list(lands)),
        in_specs=[HBM] * (2 * n) + [SEM, SEM, ANY], out_specs=tuple([HBM] * (2 * n)),
        input_output_aliases={a: a for a in range(2 * n)},
        compiler_params=pltpu.CompilerParams(has_side_effects=EFFECT),
    )(*shards, *lands, send, recv, after)
    return list(res[n:])


def allgather_small(slab):
    def body(in_ref, out_ref, send, recv, lsem):
        x, y, c, jme, sib, chips = _place()
        loc = pltpu.make_async_copy(in_ref, out_ref.at[jme], lsem.at[0])
        loc.start()
        cps = [_remote(in_ref, out_ref.at[jme], send, recv, k, (*chip, c)) for k, chip in enumerate(chips)]
        for cp in cps:
            cp.start()
        for k, chip in enumerate(chips):
            piece = out_ref.at[2 * chip[0] + chip[1]]
            _remote(piece, piece, send, recv, k, (*chip, c)).wait_recv()
        for cp in cps:
            cp.wait_send()
        loc.wait()

    return pl.pallas_call(
        body, name="allgather_small", in_specs=[ANY], out_specs=ANY,
        out_shape=jax.ShapeDtypeStruct((4,) + slab.shape, slab.dtype),
        scratch_shapes=[pltpu.SemaphoreType.DMA((3,)), pltpu.SemaphoreType.DMA((3,)), pltpu.SemaphoreType.DMA((1,))],
    )(slab)


def allreduce_small(v):
    def body(v_ref, o_ref, r0, r1, r2, send, recv):
        x, y, c, jme, sib, chips = _place()
        peers = [sib, (1 - x, y, c), (x, 1 - y, c)]
        o_ref[...] = v_ref[...]
        for k, buf in enumerate((r0, r1, r2)):
            cp = _remote(o_ref, buf, send, recv, k, peers[k])
            cp.start()
            cp.wait()
            o_ref[...] = o_ref[...] + buf[...]

    vm = pl.BlockSpec(memory_space=pltpu.VMEM)
    return pl.pallas_call(
        body, name="allreduce_small", in_specs=[vm], out_specs=vm,
        out_shape=jax.ShapeDtypeStruct(v.shape, v.dtype),
        scratch_shapes=[pltpu.VMEM(v.shape, v.dtype)] * 3 + [pltpu.SemaphoreType.DMA((3,)), pltpu.SemaphoreType.DMA((3,))],
        compiler_params=pltpu.CompilerParams(vmem_limit_bytes=VMEM_LIMIT),
    )(v)


def _pair_copies(gs, lands, send, recv):
    x, y, c, jme, sib, chips = _place()
    return [_remote(gs[a].at[:, 1 - c], lands[a], send, recv, a, sib) for a in range(len(gs))]


def rs_pair_start(gs, name):
    n = len(gs)

    def body(*refs):
        ins, lands = refs[:n], refs[n:2 * n]
        send, recv = refs[2 * n], refs[2 * n + 1]
        token = refs[-1]
        for cp in _pair_copies(ins, lands, send, recv):
            cp.start()
        token[...] = jnp.zeros_like(token)

    shapes = [(4,) + g.shape[2:] for g in gs]
    res = pl.pallas_call(
        body, name=name,
        out_shape=(pltpu.SemaphoreType.DMA((n,)), pltpu.SemaphoreType.DMA((n,)),
                   *[pltpu.HBM(g.shape, g.dtype) for g in gs], *[pltpu.HBM(s, F32) for s in shapes],
                   jax.ShapeDtypeStruct((8, 128), F32)),
        in_specs=[HBM] * (2 * n), out_specs=(SEM, SEM, *[HBM] * (2 * n), pl.BlockSpec(memory_space=pltpu.VMEM)),
        input_output_aliases={a: 2 + a for a in range(2 * n)},
        compiler_params=pltpu.CompilerParams(has_side_effects=EFFECT),
    )(*[_hbm(g) for g in gs], *[_hbm(lax.empty(s, F32)) for s in shapes])
    return res[0], res[1], list(res[2:2 + n]), list(res[2 + n:2 + 2 * n]), res[-1]


def rs_pair_wait(send, recv, gs, lands, after, name):
    n = len(gs)

    def body(*refs):
        ins, zones = refs[:n], refs[n:2 * n]
        for cp in _pair_copies(ins, zones, refs[2 * n], refs[2 * n + 1]):
            cp.wait_send()
            cp.wait_recv()

    res = pl.pallas_call(
        body, name=name,
        out_shape=tuple(pltpu.HBM(t.shape, t.dtype) for t in list(gs) + list(lands)),
        in_specs=[HBM] * (2 * n) + [SEM, SEM, ANY], out_specs=tuple([HBM] * (2 * n)),
        input_output_aliases={a: a for a in range(2 * n)},
        compiler_params=pltpu.CompilerParams(has_side_effects=EFFECT),
    )(*gs, *lands, send, recv, after)
    return list(res[:n]), list(res[n:])


def rs_pair_sum(g4, got, cidx):
    _, _, rh, cols = g4.shape
    tr = rh if rh <= 256 else 256

    def body(c_ref, a_ref, b_ref, o_ref):
        o_ref[...] = (a_ref[...] + b_ref[...]).astype(o_ref.dtype)

    return pl.pallas_call(
        body, name="rs_pair_sum",
        grid_spec=pltpu.PrefetchScalarGridSpec(
            num_scalar_prefetch=1, grid=(4, rh // tr),
            in_specs=[pl.BlockSpec((None, None, tr, cols), lambda j, t, cr: (j, cr[0], t, 0)),
                      pl.BlockSpec((None, tr, cols), lambda j, t, cr: (j, t, 0))],
            out_specs=pl.BlockSpec((None, tr, cols), lambda j, t, cr: (j, t, 0))),
        out_shape=jax.ShapeDtypeStruct((4, rh, cols), BF16),
        compiler_params=_cparams(("parallel", "parallel")),
    )(cidx, g4, got)


def _chip_copies(ps, lands, send, recv):
    x, y, c, jme, sib, chips = _place()
    return [_remote(ps[a].at[2 * chip[0] + chip[1]], lands[a].at[jme], send, recv, a * 3 + k, (*chip, c))
            for a in range(len(ps)) for k, chip in enumerate(chips)]


def rs_chip_start(ps, name):
    n = len(ps)

    def body(*refs):
        ins, lands = refs[:n], refs[n:2 * n]
        send, recv = refs[2 * n], refs[2 * n + 1]
        token = refs[-1]
        for cp in _chip_copies(ins, lands, send, recv):
            cp.start()
        token[...] = jnp.zeros_like(token)

    res = pl.pallas_call(
        body, name=name,
        out_shape=(pltpu.SemaphoreType.DMA((3 * n,)), pltpu.SemaphoreType.DMA((3 * n,)),
                   *[pltpu.HBM(p.shape, p.dtype) for p in ps], *[pltpu.HBM(p.shape, p.dtype) for p in ps],
                   jax.ShapeDtypeStruct((8, 128), F32)),
        in_specs=[HBM] * (2 * n), out_specs=(SEM, SEM, *[HBM] * (2 * n), pl.BlockSpec(memory_space=pltpu.VMEM)),
        input_output_aliases={a: 2 + a for a in range(2 * n)},
        compiler_params=pltpu.CompilerParams(has_side_effects=EFFECT),
    )(*[_hbm(p) for p in ps], *[_hbm(lax.empty(p.shape, p.dtype)) for p in ps])
    return res[0], res[1], list(res[2:2 + n]), list(res[2 + n:2 + 2 * n]), res[-1]


def rs_chip_wait(send, recv, ps, lands, after, name):
    n = len(ps)

    def body(*refs):
        ins, zones = refs[:n], refs[n:2 * n]
        send_r, recv_r = refs[2 * n], refs[2 * n + 1]
        x, y, c, jme, sib, chips = _place()
        for a in range(n):
            for k, chip in enumerate(chips):
                jt = 2 * chip[0] + chip[1]
                cp = _remote(ins[a].at[jt], zones[a].at[jt], send_r, recv_r, a * 3 + k, (*chip, c))
                cp.wait_send()
                cp.wait_recv()

    res = pl.pallas_call(
        body, name=name,
        out_shape=tuple(pltpu.HBM(p.shape, p.dtype) for p in list(ps) + list(lands)),
        in_specs=[HBM] * (2 * n) + [SEM, SEM] + [ANY] * len(after), out_specs=tuple([HBM] * (2 * n)),
        input_output_aliases={a: a for a in range(2 * n)},
        compiler_params=pltpu.CompilerParams(has_side_effects=EFFECT),
    )(*ps, *lands, send, recv, *after)
    return list(res[n:])


def rs_chip_sum(q, p, l, acc, layers, jc):
    _, rh, cols = q.shape
    tr = rh if rh <= 256 else 256

    def body(jc_ref, q_ref, p_ref, *rest):
        o_ref = rest[-1]
        jme = jc_ref[0]
        own = p_ref[...].astype(F32)
        v = [jnp.where(jme == j, own, q_ref[j].astype(F32)) for j in range(4)]
        o_ref[...] = ((v[0] + v[1]) + v[2]) + v[3]

    in_specs = [pl.BlockSpec((4, tr, cols), lambda t, jr: (0, t, 0)),
                pl.BlockSpec((None, tr, cols), lambda t, jr: (jr[0], t, 0))]
    args = [jc, q, p]
    if acc is not None:
        in_specs.append(ANY)
        args.append(acc)
    return pl.pallas_call(
        body, name="rs_chip_sum",
        grid_spec=pltpu.PrefetchScalarGridSpec(
            num_scalar_prefetch=1, grid=(rh // tr,), in_specs=in_specs,
            out_specs=pl.BlockSpec((None, None, tr, cols), lambda t, jr: (l, jr[1], t, 0))),
        out_shape=jax.ShapeDtypeStruct((layers, 2, rh, cols), F32),
        input_output_aliases={} if acc is None else {3: 0},
        compiler_params=_cparams(("parallel",)),
    )(*args)


def rs_pair_gather(rs):
    n = len(rs)

    def body(*refs):
        outs = refs[n:2 * n]
        send, recv = refs[2 * n:]
        x, y, c, jme, sib, chips = _place()
        cps = [_remote(outs[a].at[:, c], outs[a].at[:, c], send, recv, a, sib) for a in range(n)]
        for cp in cps:
            cp.start()
        for a in range(n):
            slot = outs[a].at[:, 1 - c]
            _remote(slot, slot, send, recv, a, sib).wait_recv()
        for cp in cps:
            cp.wait_send()

    return pl.pallas_call(
        body, name="rs_pair_gather", in_specs=[ANY] * n, out_specs=[ANY] * n,
        out_shape=[jax.ShapeDtypeStruct(r.shape, r.dtype) for r in rs],
        input_output_aliases={a: a for a in range(n)},
        scratch_shapes=[pltpu.SemaphoreType.DMA((n,)), pltpu.SemaphoreType.DMA((n,))],
    )(*rs)


def _adamw_math(w, g, m, v):
    m = B1 * m + (1.0 - B1) * g
    v = B2 * v + (1.0 - B2) * (g * g)
    m_hat = m / (1.0 - B1 ** STEP)
    v_hat = v / (1.0 - B2 ** STEP)
    return -LR * (m_hat / (jnp.sqrt(v_hat) + AEPS) + WD * w), m, v


def adamw(w, g, m, v, name, with_grad=False):
    rows, cols = w.shape
    tr = 256 if rows % 256 == 0 else rows
    fn = (lambda wv, gv, mv, vv: (gv,) + _adamw_math(wv, gv, mv, vv)) if with_grad else _adamw_math
    return rw(fn, [(a, 0, cols) for a in (w, g, m, v)], [(cols, F32)] * (4 if with_grad else 3), name, rows, tr=tr)


WEIGHTS = ["norm_ab", "w_in_ab", "pool_w", "pool_scale", "w_out_ab", "norm_cd", "w_in_cd", "sgu_ln_g", "sgu_ln_b",
           "sgu_w", "sgu_b", "s5_a_re", "s5_a_im", "s5_log_dt", "s5_b_re", "s5_b_im", "s5_c_re", "s5_c_im", "s5_d",
           "glu_w1", "glu_w2", "w_out_cd", "norm_x", "w_xq", "w_xkv", "w_xo", "mem_norm", "final_norm"]
INPUTS = ["x", "mem"] + WEIGHTS + ["loss_target"] + ["m_" + n for n in WEIGHTS] + ["v_" + n for n in WEIGHTS]
BIG = ["w_in_ab", "w_out_ab", "w_in_cd", "w_out_cd", "w_xq", "w_xkv", "w_xo", "glu_w1", "glu_w2", "pool_w"]
COL_SHARDED = ("w_in_ab", "w_in_cd", "w_xkv")
SMALL = [n for n in WEIGHTS if n not in BIG]
SMALL_SHARDED = {"norm_cd": 256, "sgu_ln_g": 256, "sgu_ln_b": 256, "s5_d": 128}
PACK = 256 * 128


def _pack(arrs):
    flat = jnp.concatenate([a.reshape(-1) for a in arrs])
    pad = (-flat.shape[0]) % PACK
    return jnp.concatenate([flat, jnp.zeros((pad,), flat.dtype)]).reshape(-1, 128)


def _unpack(packed, shapes):
    flat, out, off = packed.reshape(-1), [], 0
    for s in shapes:
        n = 1
        for d in s:
            n *= d
        out.append(flat[off:off + n].reshape(s))
        off += n
    return out


LAYER_KEYS = (("w_in", "w_out", "pool_w", "w_xq", "w_xkv", "w_xo"),
              ("w_in", "w_out", "glu_w1", "glu_w2", "w_xq", "w_xkv", "w_xo"))


def _weight_of(key, layer):
    if key in ("w_xq", "w_xkv", "w_xo"):
        return key, layer, 4
    kind = "ab" if layer % 2 == 0 else "cd"
    return {"w_in": "w_in_" + kind, "w_out": "w_out_" + kind}.get(key, key), layer // 2, 2


def kernel(*args):
    a = dict(zip(INPUTS, args))
    x_i, y_i, c_i = lax.axis_index("x"), lax.axis_index("y"), lax.axis_index("c")
    j = 2 * x_i + y_i

    slab = jnp.concatenate([a["norm_cd"], a["sgu_ln_g"], a["sgu_ln_b"],
                            jnp.pad(a["s5_d"], ((0, 0), (0, 128)))], axis=0)
    gslab = allgather_small(slab)
    P = {n: a[n] for n in SMALL}
    for k, n in enumerate(("norm_cd", "sgu_ln_g", "sgu_ln_b", "s5_d")):
        wd = SMALL_SHARDED[n]
        P[n] = gslab[:, 2 * k:2 * k + 2, :wd].transpose(1, 0, 2).reshape(2, 4 * wd)

    def shards_of(layer):
        keys = sorted(k for k in LAYER_KEYS[layer % 2])
        out = []
        for k in keys:
            n, l, _ = _weight_of(k, layer)
            out.append(a[n][l].reshape(-1, a[n].shape[-1]).astype(BF16))
        return keys, out

    keys0, sh0 = shards_of(0)
    first = keys0.index("w_in")
    g_in, token = allgather_sync([sh0[first].reshape(2, sh0[first].shape[0] // 2, sh0[first].shape[1])])
    w_in0 = g_in[0].reshape(4, -1, g_in[0].shape[-1])
    started = {}
    for layer in (0, 1, 2, 3):
        keys, sh = (keys0, sh0) if layer == 0 else shards_of(layer)
        rest = [(k, s) for k, s in zip(keys, sh) if k != "w_in"]
        parts = [("in", ["w_in"], [sh[keys.index("w_in")]])] * (layer > 0) + [("", *map(list, zip(*rest)))]
        for tag, pk, ps in parts:
            send, recv, ps, lands, token = allgather_start(ps, [token, gslab], "allgather_start_%d%s" % (layer, tag))
            started[(layer, tag)] = (pk, send, recv, ps, lands)
    P["norm_ab"] = P["norm_ab"] + token[0:1, 0:1]

    cidx = jnp.reshape(c_i, (1,)).astype(jnp.int32)
    jc = jnp.stack([j, c_i]).astype(jnp.int32)

    def views(g):
        W = {}
        for k, v in g.items():
            if k in ("w_in", "w_xkv"):
                W[k] = mcs(v)
            elif k == "pool_w":
                W[k] = v.reshape(4, 4, 64, 256).transpose(1, 0, 2, 3).reshape(4, 256, 256)
            elif k not in ("glu_w1", "glu_w2"):
                W[k] = m2(v.reshape(-1, v.shape[-1]))
        if "glu_w1" in g:
            W["w12"] = jnp.concatenate([g["glu_w1"].reshape(512, 512), g["glu_w2"].reshape(512, 512)], axis=1)
        return W

    def arrived(layer, tag, after):
        keys, send, recv, sh, lands = started[(layer, tag)]
        return views(dict(zip(keys, allgather_wait(send, recv, sh, lands, after, "allgather_wait_%d%s" % (layer, tag)))))

    def weights_of(layer, x_in):
        W = views({"w_in": w_in0}) if layer == 0 else arrived(layer, "in", x_in)
        W["more"] = lambda after: arrived(layer, "", after)
        return W

    halves, pending = {}, {}

    def finish_pair(layer, after):
        keys, send, recv, flat, lands = halves.pop(layer)
        flat, got = rs_pair_wait(send, recv, flat, lands, after, "rs_pair_wait_%d" % layer)
        pair = [rs_pair_sum(g4, r, cidx) for g4, r in zip(flat, got)]
        send, recv, pair, lands, token = rs_chip_start(pair, "rs_chip_start_%d" % layer)
        pending[layer] = (keys, send, recv, pair, lands)
        return token

    def grads_done(layer, GW):
        keys = sorted(GW)
        flat = [GW[k].reshape(4, 2, GW[k].shape[1] // 2, GW[k].shape[2]) for k in keys]
        send, recv, flat, lands, token = rs_pair_start(flat, "rs_pair_start_%d" % layer)
        halves[layer] = (keys, send, recv, flat, lands)
        if layer + 1 in halves:
            token = token + finish_pair(layer + 1, token)
        return token[0:1, 0:1]

    loss, dx, G = local_step(a["x"][0], a["mem"][0], a["loss_target"][0], P, weights_of, grads_done)
    loss = lax.psum(loss[0, 0], ("x", "y", "c"))
    finish_pair(0, dx)
    outs = {}

    def update_big(names, red):
        for n, g in zip(names, rs_pair_gather([red[n] for n in names])):
            shp = a[n].shape
            g2 = g.reshape(-1, shp[-1])
            upd = adamw(a[n].reshape(g2.shape), g2, a["m_" + n].reshape(g2.shape), a["v_" + n].reshape(g2.shape),
                        "adamw_" + n, with_grad=True)
            outs[n] = tuple(t.reshape(shp) for t in upd)

    def reduce_layer(layer, red, after):
        keys, send, recv, pair, lands = pending[layer]
        lands = rs_chip_wait(send, recv, pair, lands, after, "rs_chip_wait_%d" % layer)
        for k, q, p in zip(keys, lands, pair):
            n, l, layers = _weight_of(k, layer)
            red[n] = rs_chip_sum(q, p, l, red.get(n), layers, jc)

    red = {}
    for layer in (3, 2, 1):
        reduce_layer(layer, red, [dx])
    odd_only = [n for n in BIG if n.endswith("_cd") or n.startswith("glu")]
    update_big(odd_only, red)

    gfull = [jnp.stack(G[n]) if isinstance(G[n], list) else G[n] for n in SMALL]
    shapes = [g.shape for g in gfull]
    gsum = _unpack(allreduce_small(_pack(gfull)), shapes)
    gloc = []
    for n, g in zip(SMALL, gsum):
        if n in SMALL_SHARDED:
            g = lax.dynamic_slice_in_dim(g, j * SMALL_SHARDED[n], SMALL_SHARDED[n], axis=1)
        gloc.append(g)
    for n, g in zip(SMALL, gloc):
        shp = a[n].shape
        two = (-1, shp[-1]) if len(shp) > 1 else (1, shp[0])
        upd = adamw(a[n].reshape(two), g.reshape(two), a["m_" + n].reshape(two), a["v_" + n].reshape(two), "adamw_" + n)
        outs[n] = (g,) + tuple(t.reshape(shp) for t in upd)

    behind = [outs[n][1] for n in odd_only + SMALL[-1:]] + [red[n] for n in BIG if n not in odd_only]
    reduce_layer(0, red, behind)
    update_big([n for n in BIG if n not in odd_only], red)

    res = [loss, dx[None]]
    for part in range(4):
        res += [outs[n][part] for n in WEIGHTS]
    return tuple(res)
```

```python
import math

import jax
import jax.numpy as jnp
from jax import lax
from jax.experimental import pallas as pl
from jax.experimental.pallas import tpu as pltpu

F32, BF16 = jnp.float32, jnp.bfloat16
S, D = 2048, 1024
MEM = 256
EPS = 1e-6
NEG = -1e30
QB = 128
PATTERNS = (1, 4, 16)
NG, NP, NH = 32, 64, 16
NS = NG * NP
LR, B1, B2, AEPS, WD, STEP = 0.001, 0.9, 0.999, 1e-08, 0.01, 10
MESHID = pl.DeviceIdType.MESH
VMEM_LIMIT = 56 * 1024 * 1024


def _cparams(sem):
    return pltpu.CompilerParams(dimension_semantics=sem, vmem_limit_bytes=VMEM_LIMIT)


def _sig(x):
    return 1.0 / (1.0 + jnp.exp(-x))


def _dot(a, b, dims):
    return lax.dot_general(a, b, (dims, ((), ())), preferred_element_type=F32)


def _nn(a, b):
    return _dot(a, b, ((1,), (0,)))


def _nt(a, b):
    return _dot(a, b, ((1,), (1,)))


def _tn(a, b):
    return _dot(a, b, ((0,), (0,)))


_DIMS = {"nn": ((1,), (0,)), "nt": ((1,), (1,)), "tn": ((0,), (0,))}


def _tile(dim, cc=None, cap=1024):
    for t in (2048, 1536, 1024, 768, 512, 384, 256, 128):
        if t <= cap and dim % t == 0 and (cc is None or cc % t == 0):
            return t
    return dim


MM_VMEM = 36 * 1024 * 1024


def _mm_tiles(m, n, k, ccm, ccn, cck, a_bytes, b_bytes, o_bytes):
    caps = [1024, 1024, 2048]
    while True:
        tm, tn, tk = _tile(m, ccm, caps[0]), _tile(n, ccn, caps[1]), _tile(k, cck, caps[2])
        need = 2 * (tm * tk * a_bytes + tk * tn * b_bytes + tm * tn * o_bytes) + (tm * tn * 4 if tk < k else 0)
        if need <= MM_VMEM:
            return tm, tn, tk
        if tk > 1024:
            caps[2] = tk // 2
        elif tn >= tm:
            caps[1] = tn // 2
        else:
            caps[0] = tm // 2


def m2(arr, col_off=0, ncols=None):
    rows, cols = arr.shape
    ncols = cols - col_off if ncols is None else ncols

    def spec(tr, tc, rc):
        assert col_off % tc == 0
        return pl.BlockSpec((tr, tc), lambda *g: (rc(*g)[0], rc(*g)[1] + col_off // tc))
    return (arr, rows, ncols, spec, None if col_off == 0 else col_off)


def mcs(arr):
    cs = arr.shape[2]

    def spec(tr, tc, rc):
        n = cs // tc
        return pl.BlockSpec((None, tr, tc), lambda *g: (rc(*g)[1] // n, rc(*g)[0], rc(*g)[1] % n))
    return (arr, arr.shape[1], 4 * cs, spec, cs)


def out2(rows, cols):
    def spec(tr, tc, rc):
        return pl.BlockSpec((tr, tc), lambda *g: tuple(rc(*g)))
    return ((rows, cols), spec, None)


def outcs(rows, cs):
    def spec(tr, tc, rc):
        n = cs // tc
        return pl.BlockSpec((None, tr, tc), lambda *g: (rc(*g)[1] // n, rc(*g)[0], rc(*g)[1] % n))
    return ((4, rows, cs), spec, cs)


def _both(a, b):
    if a is None:
        return b
    if b is None:
        return a
    return math.gcd(a, b)


def mm(a, b, mode, name, add=None, out=None, out_dtype=F32):
    a_arr, a_r, a_c, a_spec, a_cc = a
    b_arr, b_r, b_c, b_spec, b_cc = b
    if mode == "nn":
        m, k, n = a_r, a_c, b_c
        assert b_r == k
        ccm, cck, ccn = None, a_cc, b_cc
    elif mode == "nt":
        m, k, n = a_r, a_c, b_r
        assert b_c == k
        ccm, cck, ccn = None, _both(a_cc, b_cc), None
    else:
        m, k, n = a_c, a_r, b_c
        assert b_r == k
        ccm, cck, ccn = a_cc, None, b_cc
    out = out2(m, n) if out is None else out
    o_shape, o_spec, o_cc = out
    ccn = _both(ccn, o_cc)
    if add is not None:
        ccn = _both(ccn, add[4])
    o_bytes = jnp.dtype(out_dtype).itemsize + (0 if add is None else add[0].dtype.itemsize)
    tm, tn, tk = _mm_tiles(m, n, k, ccm, ccn, cck, a_arr.dtype.itemsize, b_arr.dtype.itemsize, o_bytes)
    nk = k // tk
    if mode == "nn":
        in_specs = [a_spec(tm, tk, lambda i, j, kk: (i, kk)), b_spec(tk, tn, lambda i, j, kk: (kk, j))]
    elif mode == "nt":
        in_specs = [a_spec(tm, tk, lambda i, j, kk: (i, kk)), b_spec(tn, tk, lambda i, j, kk: (j, kk))]
    else:
        in_specs = [a_spec(tk, tm, lambda i, j, kk: (kk, i)), b_spec(tk, tn, lambda i, j, kk: (kk, j))]
    args = [a_arr, b_arr]
    if add is not None:
        in_specs.append(add[3](tm, tn, lambda i, j, kk: (i, j)))
        args.append(add[0])
    return _mm_call(args, in_specs, o_spec(tm, tn, lambda i, j, kk: (i, j)), jax.ShapeDtypeStruct(o_shape, out_dtype),
                    mode, (m // tm, n // tn, nk), (tm, tn), add is not None, name)


def _mm_call(args, in_specs, out_spec, out_shape, mode, grid, tile, has_add, name):
    dims = _DIMS[mode]
    nk = grid[2]
    tm, tn = tile

    def body(*refs):
        a_ref, b_ref = refs[0], refs[1]
        add_ref = refs[2] if has_add else None
        prod = _dot(a_ref[...].astype(BF16), b_ref[...].astype(BF16), dims)
        if nk == 1:
            o_ref = refs[-1]
            if has_add:
                prod = prod + add_ref[...].astype(F32)
            o_ref[...] = prod.astype(o_ref.dtype)
            return
        o_ref, acc = refs[-2], refs[-1]
        kk = pl.program_id(2)

        @pl.when(kk == 0)
        def _():
            acc[...] = prod

        @pl.when(kk > 0)
        def _():
            acc[...] += prod

        @pl.when(kk == nk - 1)
        def _():
            r = acc[...]
            if has_add:
                r = r + add_ref[...].astype(F32)
            o_ref[...] = r.astype(o_ref.dtype)

    return pl.pallas_call(
        body, name=name, grid=grid, in_specs=in_specs, out_specs=out_spec, out_shape=out_shape,
        scratch_shapes=[pltpu.VMEM((tm, tn), F32)] if nk > 1 else [],
        compiler_params=_cparams(("parallel", "parallel", "arbitrary")),
    )(*args)


def mm_band(a, b, mode, name, grid, blocks, maps, out_shape, add=None, out_dtype=F32):
    in_specs = [pl.BlockSpec(blocks[0], maps[0]), pl.BlockSpec(blocks[1], maps[1])]
    args = [a, b]
    if add is not None:
        in_specs.append(pl.BlockSpec(blocks[2], maps[2]))
        args.append(add)
    return _mm_call(args, in_specs, pl.BlockSpec(blocks[2], maps[2]), jax.ShapeDtypeStruct(out_shape, out_dtype),
                    mode, grid, blocks[2], add is not None, name)


def rw(fn, ins, outs, name, rows, tr=256, consts=(), accs=()):
    n_in, n_c, n_o, n_a = len(ins), len(consts), len(outs), len(accs)
    in_specs = []
    for arr, off, width in ins:
        assert off % width == 0
        in_specs.append(pl.BlockSpec((tr, width), lambda i, o=off // width: (i, o)))
    for c in consts:
        in_specs.append(pl.BlockSpec(c.shape, lambda i: (0, 0)))
    out_specs = [pl.BlockSpec((tr, w), lambda i: (i, 0)) for w, _ in outs]
    out_specs += [pl.BlockSpec(s, lambda i: (0, 0)) for s in accs]
    out_shape = [jax.ShapeDtypeStruct((rows, w), dt) for w, dt in outs]
    out_shape += [jax.ShapeDtypeStruct(s, F32) for s in accs]

    def body(*refs):
        vals = [r[...] for r in refs[:n_in + n_c]]
        o_refs = refs[n_in + n_c:n_in + n_c + n_o]
        a_refs = refs[n_in + n_c + n_o:]
        res = fn(*vals)
        for r, v in zip(o_refs, res[:n_o]):
            r[...] = v.astype(r.dtype)
        if n_a:
            @pl.when(pl.program_id(0) == 0)
            def _():
                for r in a_refs:
                    r[...] = jnp.zeros_like(r)
            for r, v in zip(a_refs, res[n_o:]):
                r[...] += v

    res = pl.pallas_call(
        body, name=name, grid=(rows // tr,), in_specs=in_specs, out_specs=out_specs,
        out_shape=out_shape,
        compiler_params=_cparams(("arbitrary",) if n_a else ("parallel",)),
    )(*[a for a, _, _ in ins], *consts)
    return res


def _rstd(x):
    return lax.rsqrt(jnp.mean(x * x, axis=-1, keepdims=True) + EPS)


def rms_fwd(x, g, name):
    def fn(xv, gv):
        xv = xv.astype(F32)
        return (xv * _rstd(xv) * gv,)
    return rw(fn, [(x, 0, D)], [(D, BF16)], name, x.shape[0], consts=[g])[0]


def _rms_bwd_math(xv, dy, gv):
    r = _rstd(xv)
    dyg = dy * gv
    dx = r * dyg - xv * (r * r * r / D) * jnp.sum(dyg * xv, axis=-1, keepdims=True)
    dg = jnp.sum(dy * xv * r, axis=0, keepdims=True)
    return dx, dg


def rms_bwd(x, dy, dres, g, name):
    def fn(xv, dyv, drv, gv):
        dx, dg = _rms_bwd_math(xv, dyv, gv)
        return dx + drv, dg
    return rw(fn, [(x, 0, D), (dy, 0, D), (dres, 0, D)], [(D, F32)], name, x.shape[0],
              consts=[g], accs=[(1, D)])


def final_loss(x, tgt, g):
    def fn(xv, tv, gv):
        e = xv * _rstd(xv) * gv - tv
        loss = 0.5 * jnp.sum(jnp.sum(e * e, axis=-1, keepdims=True), axis=0, keepdims=True) / D
        dx, dg = _rms_bwd_math(xv, e / D, gv)
        return dx, loss, dg
    return rw(fn, [(x, 0, D), (tgt, 0, D)], [(D, F32)], "final_loss", S, consts=[g],
              accs=[(1, 1), (1, D)])


def _attn_bias(bias_ref):
    ii = lax.broadcasted_iota(jnp.int32, (2 * QB, 2 * QB), 0) % QB
    jj = lax.broadcasted_iota(jnp.int32, (2 * QB, 2 * QB), 1)
    dist = ii + QB - jj
    band = (dist >= 0) & (dist <= QB)
    bias_ref[1] = jnp.where(band, 0.0, NEG)
    bias_ref[0] = jnp.where(band & (jj >= QB), 0.0, NEG)


def _two_heads(x, m0):
    return jnp.concatenate([jnp.where(m0, x, 0.0), jnp.where(m0, 0.0, x)], axis=0)


def _per_head(col, m0):
    return jnp.where(m0, col[:QB], col[QB:])


def _attn_rows(idx, d):
    if d == 1:
        b = idx
        cur = pl.ds(pl.multiple_of(b * QB, QB), QB)
        prev = pl.ds(pl.multiple_of(jnp.maximum(b - 1, 0) * QB, QB), QB)
    else:
        r, b = lax.rem(idx, d), lax.div(idx, d)
        cur = pl.ds(r + b * (QB * d), QB, stride=d)
        prev = pl.ds(r + jnp.maximum(b - 1, 0) * (QB * d), QB, stride=d)
    return cur, prev, b


NBLK = S // QB
GROUP = 4


def _colblk(off):
    return pl.BlockSpec((S, 128), lambda hp: (0, off * 8 + hp))


def attn_fwd(z):
    def body(q_ref, k_ref, v_ref, g_ref, o_ref, l_ref, a_ref, os, ls, bias):
        _attn_bias(bias)
        m0 = lax.broadcasted_iota(jnp.int32, (1, 128), 1) < 64
        for pi, d in enumerate(PATTERNS):
            lone = S // d == QB

            def load(idx, d=d, lone=lone):
                cur, prev, b = _attn_rows(idx, d)
                if lone:
                    return cur, (q_ref[cur, :], None, k_ref[cur, :], None, v_ref[cur, :], bias[1, :, QB:])
                return cur, (q_ref[cur, :], k_ref[prev, :], k_ref[cur, :], v_ref[prev, :], v_ref[cur, :],
                             bias[jnp.minimum(b, 1)])

            def block(q, kp, kc, vp, vc, bs):
                qq = _two_heads(q * 0.125, m0).astype(BF16)
                k = (kc if kp is None else jnp.concatenate([kp, kc], axis=0)).astype(BF16)
                s = _nt(qq, k) + bs
                mx = jnp.max(s, axis=-1, keepdims=True)
                p = jnp.exp(s - mx)
                den = jnp.sum(p, axis=-1, keepdims=True)
                pb = p.astype(BF16)
                vv = _two_heads(vc if vp is None else jnp.concatenate([vp, vc], axis=0), m0).astype(BF16)
                o = _nn(jnp.concatenate([pb[:QB], pb[QB:]], axis=1), vv)
                return o * _per_head(1.0 / den, m0), _per_head(mx + jnp.log(den), m0)

            def step(i, carry, pi=pi):
                loaded = [load(i * GROUP + u) for u in range(GROUP)]
                done = [block(*vals) for _, vals in loaded]
                for (cur, _), (o, l) in zip(loaded, done):
                    os[pi, cur, :] = o
                    ls[pi, cur, :] = l
                return carry
            lax.fori_loop(0, NBLK // GROUP, step, 0)
        l1, l2, l3 = ls[0], ls[1], ls[2]
        mx = jnp.maximum(jnp.maximum(l1, l2), l3)
        e1, e2, e3 = jnp.exp(l1 - mx), jnp.exp(l2 - mx), jnp.exp(l3 - mx)
        tot = e1 + e2 + e3
        o = (os[0] * e1 + os[1] * e2 + os[2] * e3) / tot
        ga = g_ref[...]
        o_ref[...] = o
        l_ref[...] = mx + jnp.log(tot)
        a_ref[...] = (o * (ga * _sig(ga))).astype(a_ref.dtype)

    out = pl.BlockSpec((S, 128), lambda hp: (0, hp))
    return pl.pallas_call(
        body, name="attn_fwd", grid=(8,),
        in_specs=[_colblk(0), _colblk(1), _colblk(2), _colblk(3)], out_specs=[out] * 3,
        out_shape=[jax.ShapeDtypeStruct((S, D), F32), jax.ShapeDtypeStruct((S, D), F32),
                   jax.ShapeDtypeStruct((S, 2 * D), BF16)],
        scratch_shapes=[pltpu.VMEM((3, S, 128), F32), pltpu.VMEM((3, S, 128), F32),
                        pltpu.VMEM((2, 2 * QB, 2 * QB), F32)],
        compiler_params=_cparams(("parallel",)),
    )(z, z, z, z)


def attn_bwd(z, d_cat, o, lse):
    def body(q_ref, k_ref, v_ref, g_ref, da_ref, o_ref, l_ref, dq_ref, dk_ref, dv_ref, dg_ref, do_s, pr_s, bias):
        _attn_bias(bias)
        m0 = lax.broadcasted_iota(jnp.int32, (1, 128), 1) < 64
        ga = g_ref[...]
        sg = _sig(ga)
        da = da_ref[...]
        ov = o_ref[...]
        do = da * (ga * sg)
        dg_ref[...] = da * ov * (sg * (1.0 + ga * (1.0 - sg)))
        do_s[...] = do
        pr_s[...] = do * ov
        dq_ref[...] = jnp.zeros_like(dq_ref)
        dk_ref[...] = jnp.zeros_like(dk_ref)
        dv_ref[...] = jnp.zeros_like(dv_ref)
        for d in PATTERNS:
            lone = S // d == QB

            def load(idx, d=d, lone=lone):
                cur, prev, b = _attn_rows(idx, d)
                if lone:
                    return (cur, None), (q_ref[cur, :], None, k_ref[cur, :], None, v_ref[cur, :],
                                         do_s[cur, :], pr_s[cur, :], l_ref[cur, :], bias[1, :, QB:])
                return (cur, prev), (q_ref[cur, :], k_ref[prev, :], k_ref[cur, :], v_ref[prev, :], v_ref[cur, :],
                                     do_s[cur, :], pr_s[cur, :], l_ref[cur, :], bias[jnp.minimum(b, 1)])

            def block(q, kp, kc, vp, vc, dof, prod, lp, bs):
                qq = _two_heads(q * 0.125, m0).astype(BF16)
                kf = kc if kp is None else jnp.concatenate([kp, kc], axis=0)
                k = kf.astype(BF16)
                v = (vc if vp is None else jnp.concatenate([vp, vc], axis=0)).astype(BF16)
                dd = _two_heads(dof, m0).astype(BF16)
                lh = jnp.max(jnp.concatenate([jnp.where(m0, lp, -jnp.inf), jnp.where(m0, -jnp.inf, lp)], axis=0),
                             axis=-1, keepdims=True)
                delta = jnp.sum(_two_heads(prod, m0), axis=-1, keepdims=True)
                p = jnp.exp(_nt(qq, k) + bs - lh)
                ds = (p * (_nt(dd, v) - delta)).astype(BF16)
                dq = _nn(jnp.concatenate([ds[:QB], ds[QB:]], axis=1), _two_heads(kf, m0).astype(BF16))
                return dq * 0.125, _tn(ds, qq), _tn(p.astype(BF16), dd)

            def step(i, carry):
                loaded = [load(i * GROUP + u) for u in range(GROUP)]
                done = [block(*vals) for _, vals in loaded]
                for ((cur, prev), _), (dq, dk, dv) in zip(loaded, done):
                    dq_ref[cur, :] = dq_ref[cur, :] + dq
                    if prev is not None:
                        dk_ref[prev, :] = dk_ref[prev, :] + dk[:QB]
                        dv_ref[prev, :] = dv_ref[prev, :] + dv[:QB]
                    dk_ref[cur, :] = dk_ref[cur, :] + dk[-QB:]
                    dv_ref[cur, :] = dv_ref[cur, :] + dv[-QB:]
                return carry
            lax.fori_loop(0, NBLK // GROUP, step, 0)

    blk = pl.BlockSpec((S, 128), lambda hp: (0, hp))
    return pl.pallas_call(
        body, name="attn_bwd", grid=(8,),
        in_specs=[_colblk(0), _colblk(1), _colblk(2), _colblk(3), blk, blk, blk], out_specs=[blk] * 4,
        out_shape=[jax.ShapeDtypeStruct((S, D), F32)] * 4,
        scratch_shapes=[pltpu.VMEM((S, 128), F32), pltpu.VMEM((S, 128), F32), pltpu.VMEM((2, 2 * QB, 2 * QB), F32)],
        compiler_params=_cparams(("parallel",)),
    )(z, z, z, z, d_cat, o, lse)


def assemble_dz_even(parts):
    def body(*refs):
        o_ref = refs[-1]
        for j in range(6):
            o_ref[:, j * D:(j + 1) * D] = refs[j][...].astype(o_ref.dtype)
    tr = 256
    blk = pl.BlockSpec((tr, D), lambda i: (i, 0))
    return pl.pallas_call(
        body, name="assemble_dz_even", grid=(S // tr,), in_specs=[blk] * 6,
        out_specs=pl.BlockSpec((tr, 6 * D), lambda i: (i, 0)),
        out_shape=jax.ShapeDtypeStruct((S, 6 * D), BF16),
        compiler_params=_cparams(("parallel",)),
    )(*parts)


def _pool_window(g):
    return jnp.where(g == 0, 2.0, jnp.where(g == 1, 4.0, jnp.where(g == 2, 8.0, 16.0)))


def _pool_sel(g, levels):
    return jnp.where(g == 0, levels[0], jnp.where(g == 1, levels[1], jnp.where(g == 2, levels[2], levels[3])))


def _pool_fwd_math(v, g):
    t = lax.broadcasted_iota(jnp.int32, (S, 1), 0)
    s = v
    levels = []
    for k in (1, 2, 4, 8):
        s = s + jnp.where(t >= k, pltpu.roll(s, k, 0), 0.0)
        levels.append(s)
    cnt = jnp.minimum((t + 1).astype(F32), _pool_window(g))
    return _pool_sel(g, levels) / cnt - v, cnt


def pool_fwd(z, pw, ps, cat):
    def body(v_ref, g_ref, pw_ref, ps_ref, cat_ref, o_ref):
        g = pl.program_id(0)
        pooled, _ = _pool_fwd_math(v_ref[...], g)
        mixed = _nn(pooled.astype(BF16), pw_ref[...].astype(BF16))
        gb = g_ref[...]
        o_ref[...] = (mixed * ps_ref[...] * (gb * _sig(gb))).astype(o_ref.dtype)

    return pl.pallas_call(
        body, name="pool_fwd", grid=(4,),
        in_specs=[pl.BlockSpec((S, 256), lambda g: (0, 16 + g)),
                  pl.BlockSpec((S, 256), lambda g: (0, 20 + g)),
                  pl.BlockSpec((None, 256, 256), lambda g: (g, 0, 0)),
                  pl.BlockSpec((1, 256), lambda g: (0, g)), pl.BlockSpec(memory_space=pl.ANY)],
        out_specs=pl.BlockSpec((S, 256), lambda g: (0, 4 + g)),
        out_shape=jax.ShapeDtypeStruct((S, 2 * D), BF16),
        input_output_aliases={4: 0},
        compiler_params=_cparams(("parallel",)),
    )(z, z, pw, ps, cat)


def pool_bwd(z, d_cat, pw, ps):
    def body(v_ref, g_ref, d_ref, pw_ref, ps_ref, dv_ref, dg_ref, dpw_ref, dps_ref):
        g = pl.program_id(0)
        v = v_ref[...]
        pooled, cnt = _pool_fwd_math(v, g)
        pwb = pw_ref[...].astype(BF16)
        pb = pooled.astype(BF16)
        mixed = _nn(pb, pwb)
        gb = g_ref[...]
        sg = _sig(gb)
        dout = d_ref[...]
        sc = ps_ref[...]
        dg_ref[...] = dout * mixed * sc * (sg * (1.0 + gb * (1.0 - sg)))
        dms = dout * (gb * sg)
        dps_ref[...] = jnp.sum(dms * mixed, axis=0, keepdims=True)
        dmx = (dms * sc).astype(BF16)
        dpw_ref[...] = _tn(pb, dmx)
        dpooled = _nt(dmx, pwb)
        t = lax.broadcasted_iota(jnp.int32, (S, 1), 0)
        s = dpooled / cnt
        levels = []
        for k in (1, 2, 4, 8):
            s = s + jnp.where(t < S - k, pltpu.roll(s, S - k, 0), 0.0)
            levels.append(s)
        dv_ref[...] = _pool_sel(g, levels) - dpooled

    return pl.pallas_call(
        body, name="pool_bwd", grid=(4,),
        in_specs=[pl.BlockSpec((S, 256), lambda g: (0, 16 + g)),
                  pl.BlockSpec((S, 256), lambda g: (0, 20 + g)),
                  pl.BlockSpec((S, 256), lambda g: (0, 4 + g)),
                  pl.BlockSpec((None, 256, 256), lambda g: (g, 0, 0)),
                  pl.BlockSpec((1, 256), lambda g: (0, g))],
        out_specs=[pl.BlockSpec((S, 256), lambda g: (0, g)),
                   pl.BlockSpec((S, 256), lambda g: (0, g)),
                   pl.BlockSpec((None, 256, 256), lambda g: (g, 0, 0)),
                   pl.BlockSpec((1, 256), lambda g: (0, g))],
        out_shape=[jax.ShapeDtypeStruct((S, D), F32), jax.ShapeDtypeStruct((S, D), F32),
                   jax.ShapeDtypeStruct((4, 256, 256), F32), jax.ShapeDtypeStruct((1, D), F32)],
        compiler_params=_cparams(("parallel",)),
    )(z, z, d_cat, pw, ps)


CH = 128


def _sgu_common(v, lng, lnb, w_ref):
    mu = jnp.mean(v, axis=-1, keepdims=True)
    vc = v - mu
    rs = lax.rsqrt(jnp.mean(vc * vc, axis=-1, keepdims=True) + EPS)
    xhat = vc * rs
    vn = (xhat * lng + lnb).astype(BF16)
    ri = lax.broadcasted_iota(jnp.int32, (CH, CH), 0)
    ci = lax.broadcasted_iota(jnp.int32, (CH, CH), 1)
    tril = ri >= ci
    ws = [jnp.where(tril, w_ref[g], 0.0).astype(BF16) for g in range(4)]
    return xhat, rs, vn, tril, ws


def _zspec(off):
    return pl.BlockSpec((CH, D), lambda c: (c, off))


def _full(shape):
    return pl.BlockSpec(shape, lambda c: (0,) * len(shape))


def sgu_fwd(z, lng, lnb, w, bfull):
    def body(u_ref, v_ref, g_ref, lng_ref, lnb_ref, w_ref, b_ref, o_ref):
        _, _, vn, _, ws = _sgu_common(v_ref[...], lng_ref[...], lnb_ref[...], w_ref)
        for g in range(4):
            sl = slice(g * 256, (g + 1) * 256)
            mixed = _nn(ws[g], vn[:, sl]) + b_ref[:, sl]
            gc = g_ref[:, sl]
            o_ref[:, sl] = (u_ref[:, sl] * mixed * (gc * _sig(gc))).astype(o_ref.dtype)

    return pl.pallas_call(
        body, name="sgu_fwd", grid=(S // CH,),
        in_specs=[_zspec(0), _zspec(1), _zspec(2), _full((1, D)), _full((1, D)),
                  _full((4, CH, CH)), _full((CH, D))],
        out_specs=pl.BlockSpec((CH, D), lambda c: (c, 0)),
        out_shape=jax.ShapeDtypeStruct((S, D), BF16),
        compiler_params=_cparams(("parallel",)),
    )(z, z, z, lng, lnb, w, bfull)


def sgu_bwd(z, d_cat, lng, lnb, w, bfull):
    def body(u_ref, v_ref, g_ref, d_ref, lng_ref, lnb_ref, w_ref, b_ref,
             du_ref, dv_ref, dg_ref, dw_ref, db_ref, dlg_ref, dlb_ref):
        @pl.when(pl.program_id(0) == 0)
        def _():
            dw_ref[...] = jnp.zeros_like(dw_ref)
            db_ref[...] = jnp.zeros_like(db_ref)
            dlg_ref[...] = jnp.zeros_like(dlg_ref)
            dlb_ref[...] = jnp.zeros_like(dlb_ref)

        lng = lng_ref[...]
        xhat, rs, vn, tril, ws = _sgu_common(v_ref[...], lng, lnb_ref[...], w_ref)
        lane = lax.broadcasted_iota(jnp.int32, (1, 128), 1)
        db = jnp.zeros((CH, 128), F32)
        dvn_parts = []
        for g in range(4):
            sl = slice(g * 256, (g + 1) * 256)
            mixed = _nn(ws[g], vn[:, sl]) + b_ref[:, sl]
            gc = g_ref[:, sl]
            sg = _sig(gc)
            u = u_ref[:, sl]
            dc = d_ref[:, sl]
            du_ref[:, sl] = dc * mixed * (gc * sg)
            dg_ref[:, sl] = dc * u * mixed * (sg * (1.0 + gc * (1.0 - sg)))
            dmx = dc * u * (gc * sg)
            db = db + jnp.where(lane == g, jnp.sum(dmx, axis=-1, keepdims=True), 0.0)
            dmb = dmx.astype(BF16)
            dw_ref[g] += jnp.where(tril, _nt(dmb, vn[:, sl]), 0.0)
            dvn_parts.append(_tn(ws[g], dmb))
        db_ref[...] += db
        dvn = jnp.concatenate(dvn_parts, axis=1)
        dlb_ref[...] += jnp.sum(dvn, axis=0, keepdims=True)
        dlg_ref[...] += jnp.sum(dvn * xhat, axis=0, keepdims=True)
        dxh = dvn * lng
        dv_ref[...] = rs * (dxh - jnp.mean(dxh, axis=-1, keepdims=True)
                            - xhat * jnp.mean(dxh * xhat, axis=-1, keepdims=True))

    row = pl.BlockSpec((CH, D), lambda c: (c, 0))
    return pl.pallas_call(
        body, name="sgu_bwd", grid=(S // CH,),
        in_specs=[_zspec(0), _zspec(1), _zspec(2), row, _full((1, D)), _full((1, D)),
                  _full((4, CH, CH)), _full((CH, D))],
        out_specs=[row, row, row, _full((4, CH, CH)), _full((CH, 128)), _full((1, D)), _full((1, D))],
        out_shape=[jax.ShapeDtypeStruct((S, D), F32)] * 3
        + [jax.ShapeDtypeStruct((4, CH, CH), F32), jax.ShapeDtypeStruct((CH, 128), F32),
           jax.ShapeDtypeStruct((1, D), F32), jax.ShapeDtypeStruct((1, D), F32)],
        compiler_params=_cparams(("arbitrary",)),
    )(z, z, z, d_cat, lng, lnb, w, bfull)


TB = 256


def _cmul(ar, ai, br, bi):
    return ar * br - ai * bi, ar * bi + ai * br


def _scan_consts(ar, ai, reverse):
    a2 = _cmul(ar, ai, ar, ai)
    a4 = _cmul(*a2, *a2)
    row = lax.broadcasted_iota(jnp.int32, (8, NS), 0)
    pr = jnp.zeros((8, NS), F32)
    pi = jnp.zeros((8, NS), F32)
    cr, ci = ar, ai
    for r in range(8):
        sel = row == (7 - r if reverse else r)
        pr = jnp.where(sel, cr, pr)
        pi = jnp.where(sel, ci, pi)
        cr, ci = _cmul(cr, ci, ar, ai)
    return ((ar, ai), a2, a4), (pr, pi), row


def scan_fwd(bu, abr, abi):
    def body(bu_ref, ar_ref, ai_ref, h_ref, car, cai):
        @pl.when(pl.program_id(0) == 0)
        def _():
            car[...] = jnp.zeros_like(car)
            cai[...] = jnp.zeros_like(cai)

        pows, (pr, pi), row = _scan_consts(ar_ref[...], ai_ref[...], False)

        def tile(t, carry):
            c_r, c_i = carry
            rows = pl.ds(pl.multiple_of(t * 8, 8), 8)
            xr = bu_ref[rows, 0:NS]
            xi = bu_ref[rows, NS:2 * NS]
            for k, (kr, ki) in zip((1, 2, 4), pows):
                sr = jnp.where(row >= k, pltpu.roll(xr, k, 0), 0.0)
                si = jnp.where(row >= k, pltpu.roll(xi, k, 0), 0.0)
                xr, xi = xr + kr * sr - ki * si, xi + kr * si + ki * sr
            xr, xi = xr + pr * c_r - pi * c_i, xi + pr * c_i + pi * c_r
            h_ref[rows, 0:NS] = xr
            h_ref[rows, NS:2 * NS] = xi
            return (jnp.broadcast_to(xr[7:8, :], (8, NS)), jnp.broadcast_to(xi[7:8, :], (8, NS)))

        c_r, c_i = lax.fori_loop(0, TB // 8, tile, (car[...], cai[...]))
        car[...] = c_r
        cai[...] = c_i

    return pl.pallas_call(
        body, name="s5_scan_fwd", grid=(S // TB,),
        in_specs=[pl.BlockSpec((TB, 2 * NS), lambda i: (i, 0)),
                  pl.BlockSpec((1, NS), lambda i: (0, 0)), pl.BlockSpec((1, NS), lambda i: (0, 0))],
        out_specs=pl.BlockSpec((TB, 2 * NS), lambda i: (i, 0)),
        out_shape=jax.ShapeDtypeStruct((S, 2 * NS), F32),
        scratch_shapes=[pltpu.VMEM((8, NS), F32), pltpu.VMEM((8, NS), F32)],
        compiler_params=_cparams(("arbitrary",)),
    )(bu, abr, abi)


def scan_bwd(eta, h, abr, abi):
    nt = S // TB

    def body(e_ref, h_ref, ar_ref, ai_ref, l_ref, da_ref, car, cai):
        @pl.when(pl.program_id(0) == 0)
        def _():
            car[...] = jnp.zeros_like(car)
            cai[...] = jnp.zeros_like(cai)
            da_ref[...] = jnp.zeros_like(da_ref)

        pows, (pr, pi), row = _scan_consts(ar_ref[...], -ai_ref[...], True)

        def tile(tt, carry):
            c_r, c_i, acr, aci = carry
            t = TB // 8 - 1 - tt
            rows = pl.ds(pl.multiple_of(t * 8, 8), 8)
            xr = e_ref[rows, 0:NS]
            xi = e_ref[rows, NS:2 * NS]
            for k, (kr, ki) in zip((1, 2, 4), pows):
                sr = jnp.where(row < 8 - k, pltpu.roll(xr, 8 - k, 0), 0.0)
                si = jnp.where(row < 8 - k, pltpu.roll(xi, 8 - k, 0), 0.0)
                xr, xi = xr + kr * sr - ki * si, xi + kr * si + ki * sr
            xr, xi = xr + pr * c_r - pi * c_i, xi + pr * c_i + pi * c_r
            l_ref[rows, 0:NS] = xr
            l_ref[rows, NS:2 * NS] = xi
            nr = jnp.where(row < 7, pltpu.roll(xr, 7, 0), c_r)
            ni = jnp.where(row < 7, pltpu.roll(xi, 7, 0), c_i)
            hr = h_ref[rows, 0:NS]
            hi = h_ref[rows, NS:2 * NS]
            acr = acr + hr * nr + hi * ni
            aci = aci + hr * ni - hi * nr
            return (jnp.broadcast_to(xr[0:1, :], (8, NS)), jnp.broadcast_to(xi[0:1, :], (8, NS)), acr, aci)

        zero = jnp.zeros((8, NS), F32)
        c_r, c_i, acr, aci = lax.fori_loop(0, TB // 8, tile, (car[...], cai[...], zero, zero))
        car[...] = c_r
        cai[...] = c_i
        da_ref[:, 0:NS] += acr
        da_ref[:, NS:2 * NS] += aci

    rev = pl.BlockSpec((TB, 2 * NS), lambda i: (nt - 1 - i, 0))
    return pl.pallas_call(
        body, name="s5_scan_bwd", grid=(nt,),
        in_specs=[rev, rev, pl.BlockSpec((1, NS), lambda i: (0, 0)), pl.BlockSpec((1, NS), lambda i: (0, 0))],
        out_specs=[rev, pl.BlockSpec((8, 2 * NS), lambda i: (0, 0))],
        out_shape=[jax.ShapeDtypeStruct((S, 2 * NS), F32), jax.ShapeDtypeStruct((8, 2 * NS), F32)],
        scratch_shapes=[pltpu.VMEM((8, NS), F32), pltpu.VMEM((8, NS), F32)],
        compiler_params=_cparams(("arbitrary",)),
    )(eta, h, abr, abi)


GC = 0.7978845608028654
GA = 0.044715


def s5_post(hc, z, dskip):
    def fn(hv, xd, dv):
        y = hv + dv * xd
        return y, 0.5 * y * (1.0 + jnp.tanh(GC * (y + GA * y * y * y)))
    return rw(fn, [(hc, 0, 512), (z, 3072, 512)], [(512, F32), (512, BF16)], "s5_post", S, consts=[dskip])


def s5_post_bwd(dyg, ypre, z, dskip):
    def fn(dy, y, xd, dv):
        th = jnp.tanh(GC * (y + GA * y * y * y))
        dg = 0.5 * (1.0 + th) + 0.5 * y * (1.0 - th * th) * GC * (1.0 + 3.0 * GA * y * y)
        dyp = dy * dg
        return dyp, dyp * dv, jnp.sum(dyp * xd, axis=0, keepdims=True)
    return rw(fn, [(dyg, 0, 512), (ypre, 0, 512), (z, 3072, 512)], [(512, BF16), (512, F32)],
              "s5_post_bwd", S, consts=[dskip], accs=[(1, 512)])


def glu_fwd(t, z, c_out):
    def fn(t1, t2, gd, co):
        return (jnp.concatenate([co, (t1 * _sig(t2) * (gd * _sig(gd))).astype(BF16)], axis=1),)
    return rw(fn, [(t, 0, 512), (t, 512, 512), (z, 3584, 512), (c_out, 0, D)], [(D + 512, BF16)], "glu_fwd", S)[0]


def glu_bwd(t, z, d_cat):
    def fn(t1, t2, gd, dd):
        s2, sg = _sig(t2), _sig(gd)
        sl = gd * sg
        return (jnp.concatenate([dd * s2 * sl, dd * t1 * s2 * (1.0 - s2) * sl], axis=1),
                dd * t1 * s2 * (sg * (1.0 + gd * (1.0 - sg))))
    return rw(fn, [(t, 0, 512), (t, 512, 512), (z, 3584, 512), (d_cat, 1024, 512)],
              [(D, BF16), (512, F32)], "glu_bwd", S)


def assemble_dz_odd(du, dv, dgc, dxd, dgd):
    def body(a, b, c, d, e, o_ref):
        o_ref[:, 0:D] = a[...].astype(BF16)
        o_ref[:, D:2 * D] = b[...].astype(BF16)
        o_ref[:, 2 * D:3 * D] = c[...].astype(BF16)
        o_ref[:, 3 * D:3 * D + 512] = d[...].astype(BF16)
        o_ref[:, 3 * D + 512:4 * D] = e[...].astype(BF16)
    tr = 256
    blk = pl.BlockSpec((tr, D), lambda i: (i, 0))
    half = pl.BlockSpec((tr, 512), lambda i: (i, 0))
    return pl.pallas_call(
        body, name="assemble_dz_odd", grid=(S // tr,), in_specs=[blk, blk, blk, half, half],
        out_specs=pl.BlockSpec((tr, 4 * D), lambda i: (i, 0)),
        out_shape=jax.ShapeDtypeStruct((S, 4 * D), BF16),
        compiler_params=_cparams(("parallel",)),
    )(du, dv, dgc, dxd, dgd)


TQ = 256


def _xattn_probs(qh, kh):
    s = _nt(qh, kh) * 0.0625
    p = jnp.exp(s - jnp.max(s, axis=-1, keepdims=True))
    return p / jnp.sum(p, axis=-1, keepdims=True)


def xattn_fwd(q, kv):
    def body(q_ref, kv_ref, o_ref):
        for h in range(4):
            sl = slice(h * 256, (h + 1) * 256)
            p = _xattn_probs(q_ref[:, sl].astype(BF16), kv_ref[:, sl].astype(BF16))
            vh = kv_ref[:, D + h * 256:D + (h + 1) * 256].astype(BF16)
            o_ref[:, sl] = _nn(p.astype(BF16), vh).astype(o_ref.dtype)

    return pl.pallas_call(
        body, name="xattn_fwd", grid=(S // TQ,),
        in_specs=[pl.BlockSpec((TQ, D), lambda i: (i, 0)), pl.BlockSpec((MEM, 2 * D), lambda i: (0, 0))],
        out_specs=pl.BlockSpec((TQ, D), lambda i: (i, 0)),
        out_shape=jax.ShapeDtypeStruct((S, D), BF16),
        compiler_params=_cparams(("parallel",)),
    )(q, kv)


def xattn_bwd(q, kv, d_o):
    def body(q_ref, kv_ref, do_ref, dq_ref, dkv_ref):
        @pl.when(pl.program_id(0) == 0)
        def _():
            dkv_ref[...] = jnp.zeros_like(dkv_ref)

        for h in range(4):
            sl = slice(h * 256, (h + 1) * 256)
            vs = slice(D + h * 256, D + (h + 1) * 256)
            qh = q_ref[:, sl].astype(BF16)
            kh = kv_ref[:, sl].astype(BF16)
            vh = kv_ref[:, vs].astype(BF16)
            doh = do_ref[:, sl].astype(BF16)
            p = _xattn_probs(qh, kh)
            dp = _nt(doh, vh)
            ds = (p * (dp - jnp.sum(p * dp, axis=-1, keepdims=True)) * 0.0625).astype(BF16)
            dq_ref[:, sl] = _nn(ds, kh).astype(dq_ref.dtype)
            dkv_ref[:, sl] += _tn(ds, qh)
            dkv_ref[:, vs] += _tn(p.astype(BF16), doh)

    return pl.pallas_call(
        body, name="xattn_bwd", grid=(S // TQ,),
        in_specs=[pl.BlockSpec((TQ, D), lambda i: (i, 0)), pl.BlockSpec((MEM, 2 * D), lambda i: (0, 0)),
                  pl.BlockSpec((TQ, D), lambda i: (i, 0))],
        out_specs=[pl.BlockSpec((TQ, D), lambda i: (i, 0)), pl.BlockSpec((MEM, 2 * D), lambda i: (0, 0))],
        out_shape=[jax.ShapeDtypeStruct((S, D), BF16), jax.ShapeDtypeStruct((MEM, 2 * D), F32)],
        compiler_params=_cparams(("arbitrary",)),
    )(q, kv, d_o)


def _s5_disc(a_re, a_im, log_dt, b_re, b_im):
    dt = jnp.exp(log_dt)[:, None]
    mag = jnp.exp(dt * a_re)
    abr = mag * jnp.cos(dt * a_im)
    abi = mag * jnp.sin(dt * a_im)
    nr, ni = abr - 1.0, abi
    inv = 1.0 / (a_re * a_re + a_im * a_im)
    cr = (nr * a_re + ni * a_im) * inv
    ci = (ni * a_re - nr * a_im) * inv
    bbr = cr[..., None] * b_re - ci[..., None] * b_im
    bbi = cr[..., None] * b_im + ci[..., None] * b_re
    return abr, abi, bbr, bbi


VM = pl.BlockSpec(memory_space=pltpu.VMEM)


def s5_embed(bt_re, bt_im, ct_re, ct_im):
    def body(br, bi, cr, ci, b_ref, c_ref):
        b_ref[...] = jnp.zeros_like(b_ref)
        c_ref[...] = jnp.zeros_like(c_ref)
        for g in range(NG):
            rows, cols = slice(g * NH, (g + 1) * NH), slice(g * NP, (g + 1) * NP)
            b_ref[rows, cols] = br[g]
            b_ref[rows, NS + g * NP:NS + (g + 1) * NP] = bi[g]
            c_ref[cols, rows] = cr[g]
            c_ref[NS + g * NP:NS + (g + 1) * NP, rows] = -ci[g]

    return pl.pallas_call(
        body, name="s5_embed", in_specs=[VM] * 4, out_specs=[VM] * 2,
        out_shape=[jax.ShapeDtypeStruct((NG * NH, 2 * NS), F32), jax.ShapeDtypeStruct((2 * NS, NG * NH), F32)],
        compiler_params=pltpu.CompilerParams(vmem_limit_bytes=VMEM_LIMIT),
    )(bt_re, bt_im, ct_re, ct_im)


def s5_extract(gb, gc):
    def body(gb_ref, gc_ref, br, bi, cr, ci):
        for g in range(NG):
            rows, cols = slice(g * NH, (g + 1) * NH), slice(g * NP, (g + 1) * NP)
            br[g] = gb_ref[rows, cols]
            bi[g] = gb_ref[rows, NS + g * NP:NS + (g + 1) * NP]
            cr[g] = gc_ref[cols, rows]
            ci[g] = -gc_ref[NS + g * NP:NS + (g + 1) * NP, rows]

    return pl.pallas_call(
        body, name="s5_extract", in_specs=[VM] * 2, out_specs=[VM] * 4,
        out_shape=[jax.ShapeDtypeStruct((NG, NH, NP), F32)] * 2 + [jax.ShapeDtypeStruct((NG, NP, NH), F32)] * 2,
        compiler_params=pltpu.CompilerParams(vmem_limit_bytes=VMEM_LIMIT),
    )(gb, gc)


HC, HS = NG * NH // 2, NS // 2
TS = 1024


def s5_to_states(x, w, mode, name, z_off=0):
    if mode == "nn":
        wb, wm = (HC, HS), lambda i, j, kk: (j % 2, j)
    else:
        wb, wm = (HS, HC), lambda i, j, kk: (j, j % 2)
    return mm_band(x, w, mode, name, (S // TS, 4, 1), ((TS, HC), wb, (TS, HS)),
                   (lambda i, j, kk: (i, z_off + j % 2), wm, lambda i, j, kk: (i, j)), (S, 2 * NS))


def s5_to_channels(x, w, mode, name, add=None):
    if mode == "nn":
        wb, wm = (HS, HC), lambda i, j, kk: (j + 2 * kk, j)
    else:
        wb, wm = (HC, HS), lambda i, j, kk: (j, j + 2 * kk)
    return mm_band(x, w, mode, name, (S // TS, 2, 2), ((TS, HS), wb, (TS, HC)),
                   (lambda i, j, kk: (i, j + 2 * kk), wm, lambda i, j, kk: (i, j)), (S, NG * NH), add=add)


def s5_outer(a, b, name, states_first, z_off=0):
    if states_first:
        return mm_band(a, b, "tn", name, (4, 1, 1), ((S, HS), (S, HC), (HS, HC)),
                       (lambda i, j, kk: (0, i), lambda i, j, kk: (0, i % 2), lambda i, j, kk: (i, i % 2)),
                       (2 * NS, NG * NH))
    return mm_band(a, b, "tn", name, (1, 4, 1), ((S, HC), (S, HS), (HC, HS)),
                   (lambda i, j, kk: (0, z_off + j % 2), lambda i, j, kk: (0, j), lambda i, j, kk: (j % 2, j)),
                   (NG * NH, 2 * NS))


def _fwd_even(i, x, P, W):
    hn = rms_fwd(x, P["norm_ab"][i:i + 1], "rms_ab_fwd")
    z = mm(m2(hn), W["w_in"], "nn", "in_ab")
    o, lse, cat = attn_fwd(z)
    if "more" in W:
        W.update(W.pop("more")(cat))
    cat = pool_fwd(z, W["pool_w"], P["pool_scale"][i:i + 1], cat)
    x_mid = mm(m2(cat), W["w_out"], "nn", "out_ab", add=m2(x))
    return x_mid, dict(x=x, hn=hn, z=z, o=o, lse=lse, cat=cat)


def _bwd_even(i, dx_mid, sv, P, W, G, GW):
    z = sv["z"]
    d_cat = mm(m2(dx_mid), W["w_out"], "nt", "out_ab_dx")
    GW["w_out"] = mm(m2(sv["cat"]), m2(dx_mid), "tn", "out_ab_dw").reshape(4, 512, D)
    dq, dk, dv, dga = attn_bwd(z, d_cat, sv["o"], sv["lse"])
    dvb, dgb, dpw, dps = pool_bwd(z, d_cat, W["pool_w"], P["pool_scale"][i:i + 1])
    GW["pool_w"] = dpw.reshape(4, 4, 64, 256).transpose(1, 0, 2, 3).reshape(4, 256, 256)
    G["pool_scale"][i] = dps[0]
    d_z = assemble_dz_even((dq, dk, dv, dga, dvb, dgb))
    d_hn = mm(m2(d_z), W["w_in"], "nt", "in_ab_dx")
    GW["w_in"] = mm(m2(sv["hn"]), m2(d_z), "tn", "in_ab_dw", out=outcs(D, 1536))
    return d_hn, P["norm_ab"][i:i + 1], "norm_ab", "rms_ab_bwd"


def _fwd_odd(i, x, P, W):
    hn = rms_fwd(x, P["norm_cd"][i:i + 1], "rms_cd_fwd")
    z = mm(m2(hn), W["w_in"], "nn", "in_cd")
    bfull = jnp.repeat(P["sgu_b"][i].T, 256, axis=1)
    c_out = sgu_fwd(z, P["sgu_ln_g"][i:i + 1], P["sgu_ln_b"][i:i + 1], P["sgu_w"][i], bfull)
    disc, disc_vjp = jax.vjp(_s5_disc, P["s5_a_re"][i], P["s5_a_im"][i], P["s5_log_dt"][i],
                             P["s5_b_re"][i], P["s5_b_im"][i])
    abr, abi, bbr, bbi = disc
    bbd, cbd = s5_embed(bbr.transpose(0, 2, 1), bbi.transpose(0, 2, 1),
                        P["s5_c_re"][i].transpose(0, 2, 1), P["s5_c_im"][i].transpose(0, 2, 1))
    abr, abi = abr.reshape(1, NS), abi.reshape(1, NS)
    bu = s5_to_states(z, bbd, "nn", "s5_bu", z_off=3072 // HC)
    h = scan_fwd(bu, abr, abi)
    hc = s5_to_channels(h, cbd, "nn", "s5_hc")
    dskip = P["s5_d"][i:i + 1]
    ypre, yg = s5_post(hc, z, dskip)
    if "more" in W:
        W.update(W.pop("more")(yg))
    w12 = W["w12"]
    t = mm(m2(yg), m2(w12), "nn", "glu_t")
    cat = glu_fwd(t, z, c_out)
    x_mid = mm(m2(cat), W["w_out"], "nn", "out_cd", add=m2(x))
    return x_mid, dict(x=x, hn=hn, z=z, bfull=bfull, disc_vjp=disc_vjp, bbd=bbd, cbd=cbd, abr=abr,
                       abi=abi, h=h, ypre=ypre, yg=yg, w12=w12, t=t, cat=cat, dskip=dskip)


def _bwd_odd(i, dx_mid, sv, P, W, G, GW):
    z = sv["z"]
    d_cat = mm(m2(dx_mid), W["w_out"], "nt", "out_cd_dx")
    GW["w_out"] = mm(m2(sv["cat"]), m2(dx_mid), "tn", "out_cd_dw").reshape(4, 384, D)
    du, dv, dgc, dws, dbs, dlg, dlb = sgu_bwd(z, d_cat, P["sgu_ln_g"][i:i + 1], P["sgu_ln_b"][i:i + 1],
                                               P["sgu_w"][i], sv["bfull"])
    G["sgu_w"][i], G["sgu_b"][i] = dws, dbs[:, :4].T
    G["sgu_ln_g"][i], G["sgu_ln_b"][i] = dlg[0], dlb[0]
    dt, dgd = glu_bwd(sv["t"], z, d_cat)
    gw12 = mm(m2(sv["yg"]), m2(dt), "tn", "glu_dw")
    GW["glu_w1"] = gw12[:, :512].reshape(4, 128, 512)
    GW["glu_w2"] = gw12[:, 512:].reshape(4, 128, 512)
    dyg = mm(m2(dt), m2(sv["w12"]), "nt", "glu_dx")
    dypre, dxd1, dd = s5_post_bwd(dyg, sv["ypre"], z, sv["dskip"])
    G["s5_d"][i] = dd[0]
    gcbd = s5_outer(sv["h"], dypre, "s5_dc", states_first=True)
    eta = s5_to_states(dypre, sv["cbd"], "nt", "s5_eta")
    lam, dacc = scan_bwd(eta, sv["h"], sv["abr"], sv["abi"])
    gbbd = s5_outer(z, lam, "s5_db", states_first=False, z_off=3072 // HC)
    dxd = s5_to_channels(lam, sv["bbd"], "nt", "s5_dx", add=dxd1)
    dacc = jnp.sum(dacc, axis=0)
    dbt_re, dbt_im, dct_re, dct_im = s5_extract(gbbd, gcbd)
    G["s5_c_re"][i], G["s5_c_im"][i] = dct_re.transpose(0, 2, 1), dct_im.transpose(0, 2, 1)
    d_bbr, d_bbi = dbt_re.transpose(0, 2, 1), dbt_im.transpose(0, 2, 1)
    (G["s5_a_re"][i], G["s5_a_im"][i], G["s5_log_dt"][i], G["s5_b_re"][i], G["s5_b_im"][i]) = sv["disc_vjp"](
        (dacc[:NS].reshape(NG, NP), dacc[NS:].reshape(NG, NP), d_bbr, d_bbi))
    d_z = assemble_dz_odd(du, dv, dgc, dxd, dgd)
    d_hn = mm(m2(d_z), W["w_in"], "nt", "in_cd_dx")
    GW["w_in"] = mm(m2(sv["hn"]), m2(d_z), "tn", "in_cd_dw", out=outcs(D, 1024))
    return d_hn, P["norm_cd"][i:i + 1], "norm_cd", "rms_cd_bwd"


def _fwd_x(l, x, mem_n, P, W):
    hx = rms_fwd(x, P["norm_x"][l:l + 1], "rms_x_fwd")
    q = mm(m2(hx), W["w_xq"], "nn", "xq", out_dtype=BF16)
    kv = mm(m2(mem_n), W["w_xkv"], "nn", "xkv", out_dtype=BF16)
    ox = xattn_fwd(q, kv)
    x_out = mm(m2(ox), W["w_xo"], "nn", "xo", add=m2(x))
    return x_out, dict(x=x, hx=hx, q=q, kv=kv, ox=ox)


def _bwd_x(l, dx_out, sv, mem_n, d_memn, P, W, G, GW):
    d_ox = mm(m2(dx_out), W["w_xo"], "nt", "xo_dx", out_dtype=BF16)
    GW["w_xo"] = mm(m2(sv["ox"]), m2(dx_out), "tn", "xo_dw").reshape(4, 256, D)
    dq, dkv = xattn_bwd(sv["q"], sv["kv"], d_ox)
    GW["w_xq"] = mm(m2(sv["hx"]), m2(dq), "tn", "xq_dw").reshape(4, 256, D)
    d_hx = mm(m2(dq), W["w_xq"], "nt", "xq_dx")
    GW["w_xkv"] = mm(m2(mem_n), m2(dkv), "tn", "xkv_dw", out=outcs(D, 512))
    d_memn = mm(m2(dkv), W["w_xkv"], "nt", "xkv_dx", add=None if d_memn is None else m2(d_memn))
    dx, dg = rms_bwd(sv["x"], d_hx, dx_out, P["norm_x"][l:l + 1], "rms_x_bwd")
    G["norm_x"][l] = dg[0]
    return dx, d_memn


SMALL_LAYERS = (("norm_ab", 2), ("pool_scale", 2), ("norm_cd", 2), ("sgu_ln_g", 2), ("sgu_ln_b", 2), ("sgu_w", 2),
                ("sgu_b", 2), ("s5_a_re", 2), ("s5_a_im", 2), ("s5_log_dt", 2), ("s5_b_re", 2), ("s5_b_im", 2),
                ("s5_c_re", 2), ("s5_c_im", 2), ("s5_d", 2), ("norm_x", 4))


def local_step(x, mem, tgt, P, weights_of, grads_done):
    G = {k: [None] * n for k, n in SMALL_LAYERS}
    mem_g = P["mem_norm"].reshape(1, D)
    mem_n = rms_fwd(mem, mem_g, "rms_mem_fwd")
    saved = []
    for layer in range(4):
        i = layer // 2
        W = weights_of(layer, x)
        x, sv_m = (_fwd_even if layer % 2 == 0 else _fwd_odd)(i, x, P, W)
        x, sv_x = _fwd_x(layer, x, mem_n, P, W)
        saved.append((sv_m, sv_x, W))
    dx, loss, dgf = final_loss(x, tgt, P["final_norm"].reshape(1, D))
    G["final_norm"] = dgf[0]
    d_memn = None
    for layer in reversed(range(4)):
        i = layer // 2
        sv_m, sv_x, W = saved[layer]
        GW = {}
        dx_mid, d_memn = _bwd_x(layer, dx, sv_x, mem_n, d_memn, P, W, G, GW)
        d_hn, g, key, name = (_bwd_even if layer % 2 == 0 else _bwd_odd)(i, dx_mid, sv_m, P, W, G, GW)
        token = grads_done(layer, GW)
        if token is not None:
            g = g + token
        dx, dg = rms_bwd(sv_m["x"], d_hn, dx_mid, g, name)
        G[key][i] = dg[0]
    _, dgm = rms_bwd(mem, d_memn, d_memn, mem_g, "rms_mem_bwd")
    G["mem_norm"] = dgm[0]
    return loss, dx, G


ANY = pl.BlockSpec(memory_space=pl.ANY)


def _place():
    x, y, c = lax.axis_index("x"), lax.axis_index("y"), lax.axis_index("c")
    chips = [(1 - x, y), (x, 1 - y), (1 - x, 1 - y)]
    return x, y, c, 2 * x + y, (x, y, 1 - c), chips


def _remote(src, dst, send, recv, k, dev):
    return pltpu.make_async_remote_copy(src_ref=src, dst_ref=dst, send_sem=send.at[k], recv_sem=recv.at[k],
                                        device_id=dev, device_id_type=MESHID)


HBM = pl.BlockSpec(memory_space=pltpu.HBM)
SEM = pl.BlockSpec(memory_space=pltpu.SEMAPHORE)
EFFECT = pltpu.SideEffectType.DATAFLOW_SIDE_EFFECTING


def _hbm(t):
    return pltpu.with_memory_space_constraint(t, pltpu.HBM)


def allgather_sync(shards):
    n = len(shards)

    def body(*refs):
        ins, outs = refs[:n], refs[n:2 * n]
        token, send, recv = refs[2 * n:]
        x, y, c, jme, sib, chips = _place()
        first, passed = [], []
        for a in range(n):
            cp = _remote(ins[a], outs[a].at[jme], send, recv, a * 7 + 6, sib)
            cp.start()
            first.append(cp)
            for k, chip in enumerate(chips):
                cp = _remote(ins[a].at[c], outs[a].at[jme, c], send, recv, a * 7 + k, (*chip, c))
                cp.start()
                first.append(cp)
        for a in range(n):
            for k, chip in enumerate(chips):
                piece = outs[a].at[2 * chip[0] + chip[1], c]
                _remote(piece, piece, send, recv, a * 7 + k, (*chip, c)).wait_recv()
                fw = _remote(piece, piece, send, recv, a * 7 + 3 + k, sib)
                fw.start()
                passed.append(fw)
        for a in range(n):
            own = outs[a].at[jme]
            _remote(own, own, send, recv, a * 7 + 6, sib).wait_recv()
            for k, chip in enumerate(chips):
                piece = outs[a].at[2 * chip[0] + chip[1], 1 - c]
                _remote(piece, piece, send, recv, a * 7 + 3 + k, sib).wait_recv()
        for cp in first + passed:
            cp.wait_send()
        token[...] = jnp.zeros_like(token)

    res = pl.pallas_call(
        body, name="allgather_sync", in_specs=[ANY] * n,
        out_specs=[ANY] * n + [pl.BlockSpec(memory_space=pltpu.VMEM)],
        out_shape=[jax.ShapeDtypeStruct((4,) + s.shape, s.dtype) for s in shards] + [jax.ShapeDtypeStruct((8, 128), F32)],
        scratch_shapes=[pltpu.SemaphoreType.DMA((7 * n,)), pltpu.SemaphoreType.DMA((7 * n,))],
    )(*shards)
    return list(res[:n]), res[n]


def _gather_copies(ins, lands, send, recv):
    x, y, c, jme, sib, chips = _place()
    devs = [(*chip, c) for chip in chips] + [sib]
    return [_remote(ins[a], lands[a].at[jme], send, recv, a * 4 + k, dev)
            for a in range(len(ins)) for k, dev in enumerate(devs)]


def allgather_start(shards, after, name):
    n, na = len(shards), len(after)

    def body(*refs):
        ins, lands = refs[:n], refs[n:2 * n]
        send, recv = refs[2 * n + na], refs[2 * n + na + 1]
        token = refs[-1]
        for cp in _gather_copies(ins, lands, send, recv):
            cp.start()
        token[...] = jnp.zeros_like(token)

    res = pl.pallas_call(
        body, name=name,
        out_shape=(pltpu.SemaphoreType.DMA((4 * n,)), pltpu.SemaphoreType.DMA((4 * n,)),
                   *[pltpu.HBM(s.shape, s.dtype) for s in shards],
                   *[pltpu.HBM((4,) + s.shape, s.dtype) for s in shards],
                   jax.ShapeDtypeStruct((8, 128), F32)),
        in_specs=[HBM] * (2 * n) + [ANY] * na,
        out_specs=(SEM, SEM, *[HBM] * (2 * n), pl.BlockSpec(memory_space=pltpu.VMEM)),
        input_output_aliases={a: 2 + a for a in range(2 * n)},
        compiler_params=pltpu.CompilerParams(has_side_effects=EFFECT),
    )(*[_hbm(s) for s in shards], *[_hbm(lax.empty((4,) + s.shape, s.dtype)) for s in shards], *after)
    return res[0], res[1], list(res[2:2 + n]), list(res[2 + n:2 + 2 * n]), res[-1]


def allgather_wait(send, recv, shards, lands, after, name):
    n = len(shards)

    def body(*refs):
        ins, zones = refs[:n], refs[n:2 * n]
        send_r, recv_r = refs[2 * n], refs[2 * n + 1]
        x, y, c, jme, sib, chips = _place()
        slots = [2 * chip[0] + chip[1] for chip in chips] + [jme]
        for a in range(n):
            for k, slot in enumerate(slots):
                cp = _remote(ins[a], zones[a].at[slot], send_r, recv_r, a * 4 + k, sib)
                cp.wait_send()
                cp.wait_recv()

    res = pl.pallas_call(
        body, name=name,
        out_shape=tuple(pltpu.HBM(t.shape, t.dtype) for t in list(shards) + list(lands)),
        in_specs=[HBM] * (2 * n) + [SEM, SEM, ANY], out_specs=tuple([HBM] * (2 * n)),
        input_output_aliases={a: a for a in range(2 * n)},
        compiler_params=pltpu.CompilerParams(has_side_effects=EFFECT),
    )(*shards, *lands, send, recv, after)
    return list(res[n:])


def allgather_small(slab):
    def body(in_ref, out_ref, send, recv, lsem):
        x, y, c, jme, sib, chips = _place()
        loc = pltpu.make_async_copy(in_ref, out_ref.at[jme], lsem.at[0])
        loc.start()
        cps = [_remote(in_ref, out_ref.at[jme], send, recv, k, (*chip, c)) for k, chip in enumerate(chips)]
        for cp in cps:
            cp.start()
        for k, chip in enumerate(chips):
            piece = out_ref.at[2 * chip[0] + chip[1]]
            _remote(piece, piece, send, recv, k, (*chip, c)).wait_recv()
        for cp in cps:
            cp.wait_send()
        loc.wait()

    return pl.pallas_call(
        body, name="allgather_small", in_specs=[ANY], out_specs=ANY,
        out_shape=jax.ShapeDtypeStruct((4,) + slab.shape, slab.dtype),
        scratch_shapes=[pltpu.SemaphoreType.DMA((3,)), pltpu.SemaphoreType.DMA((3,)), pltpu.SemaphoreType.DMA((1,))],
    )(slab)


def allreduce_small(v):
    def body(v_ref, o_ref, r0, r1, r2, send, recv):
        x, y, c, jme, sib, chips = _place()
        peers = [sib, (1 - x, y, c), (x, 1 - y, c)]
        o_ref[...] = v_ref[...]
        for k, buf in enumerate((r0, r1, r2)):
            cp = _remote(o_ref, buf, send, recv, k, peers[k])
            cp.start()
            cp.wait()
            o_ref[...] = o_ref[...] + buf[...]

    vm = pl.BlockSpec(memory_space=pltpu.VMEM)
    return pl.pallas_call(
        body, name="allreduce_small", in_specs=[vm], out_specs=vm,
        out_shape=jax.ShapeDtypeStruct(v.shape, v.dtype),
        scratch_shapes=[pltpu.VMEM(v.shape, v.dtype)] * 3 + [pltpu.SemaphoreType.DMA((3,)), pltpu.SemaphoreType.DMA((3,))],
        compiler_params=pltpu.CompilerParams(vmem_limit_bytes=VMEM_LIMIT),
    )(v)


def _pair_copies(gs, lands, send, recv):
    x, y, c, jme, sib, chips = _place()
    return [_remote(gs[a].at[:, 1 - c], lands[a], send, recv, a, sib) for a in range(len(gs))]


def rs_pair_start(gs, name):
    n = len(gs)

    def body(*refs):
        ins, lands = refs[:n], refs[n:2 * n]
        send, recv = refs[2 * n], refs[2 * n + 1]
        token = refs[-1]
        for cp in _pair_copies(ins, lands, send, recv):
            cp.start()
        token[...] = jnp.zeros_like(token)

    shapes = [(4,) + g.shape[2:] for g in gs]
    res = pl.pallas_call(
        body, name=name,
        out_shape=(pltpu.SemaphoreType.DMA((n,)), pltpu.SemaphoreType.DMA((n,)),
                   *[pltpu.HBM(g.shape, g.dtype) for g in gs], *[pltpu.HBM(s, F32) for s in shapes],
                   jax.ShapeDtypeStruct((8, 128), F32)),
        in_specs=[HBM] * (2 * n), out_specs=(SEM, SEM, *[HBM] * (2 * n), pl.BlockSpec(memory_space=pltpu.VMEM)),
        input_output_aliases={a: 2 + a for a in range(2 * n)},
        compiler_params=pltpu.CompilerParams(has_side_effects=EFFECT),
    )(*[_hbm(g) for g in gs], *[_hbm(lax.empty(s, F32)) for s in shapes])
    return res[0], res[1], list(res[2:2 + n]), list(res[2 + n:2 + 2 * n]), res[-1]


def rs_pair_wait(send, recv, gs, lands, after, name):
    n = len(gs)

    def body(*refs):
        ins, zones = refs[:n], refs[n:2 * n]
        for cp in _pair_copies(ins, zones, refs[2 * n], refs[2 * n + 1]):
            cp.wait_send()
            cp.wait_recv()

    res = pl.pallas_call(
        body, name=name,
        out_shape=tuple(pltpu.HBM(t.shape, t.dtype) for t in list(gs) + list(lands)),
        in_specs=[HBM] * (2 * n) + [SEM, SEM, ANY], out_specs=tuple([HBM] * (2 * n)),
        input_output_aliases={a: a for a in range(2 * n)},
        compiler_params=pltpu.CompilerParams(has_side_effects=EFFECT),
    )(*gs, *lands, send, recv, after)
    return list(res[:n]), list(res[n:])


SUM_ROWS = 256


def rs_pair_sum(g4s, gots, cidx):
    n = len(g4s)
    tiles = [(min(g.shape[2], SUM_ROWS), g.shape[3]) for g in g4s]
    nts = [g.shape[2] // tr for g, (tr, _) in zip(g4s, tiles)]

    def at(a, s):
        s = jnp.minimum(s, 4 * nts[a] - 1)
        return s // nts[a], s % nts[a]

    def body(c_ref, *refs):
        for a in range(n):
            refs[2 * n + a][...] = (refs[a][...] + refs[n + a][...]).astype(BF16)

    in_specs = [pl.BlockSpec((None, None) + tiles[a], lambda s, cr, a=a: (at(a, s)[0], cr[0], at(a, s)[1], 0))
                for a in range(n)]
    in_specs += [pl.BlockSpec((None,) + tiles[a], lambda s, cr, a=a: (*at(a, s), 0)) for a in range(n)]
    return pl.pallas_call(
        body, name="rs_pair_sum",
        grid_spec=pltpu.PrefetchScalarGridSpec(
            num_scalar_prefetch=1, grid=(4 * max(nts),), in_specs=in_specs,
            out_specs=[pl.BlockSpec((None,) + tiles[a], lambda s, cr, a=a: (*at(a, s), 0)) for a in range(n)]),
        out_shape=[jax.ShapeDtypeStruct((4,) + g.shape[2:], BF16) for g in g4s],
        compiler_params=_cparams(("arbitrary",)),
    )(cidx, *g4s, *gots)


def _chip_copies(ps, lands, send, recv):
    x, y, c, jme, sib, chips = _place()
    return [_remote(ps[a].at[2 * chip[0] + chip[1]], lands[a].at[jme], send, recv, a * 3 + k, (*chip, c))
            for a in range(len(ps)) for k, chip in enumerate(chips)]


def rs_chip_start(ps, name):
    n = len(ps)

    def body(*refs):
        ins, lands = refs[:n], refs[n:2 * n]
        send, recv = refs[2 * n], refs[2 * n + 1]
        token = refs[-1]
        for cp in _chip_copies(ins, lands, send, recv):
            cp.start()
        token[...] = jnp.zeros_like(token)

    res = pl.pallas_call(
        body, name=name,
        out_shape=(pltpu.SemaphoreType.DMA((3 * n,)), pltpu.SemaphoreType.DMA((3 * n,)),
                   *[pltpu.HBM(p.shape, p.dtype) for p in ps], *[pltpu.HBM(p.shape, p.dtype) for p in ps],
                   jax.ShapeDtypeStruct((8, 128), F32)),
        in_specs=[HBM] * (2 * n), out_specs=(SEM, SEM, *[HBM] * (2 * n), pl.BlockSpec(memory_space=pltpu.VMEM)),
        input_output_aliases={a: 2 + a for a in range(2 * n)},
        compiler_params=pltpu.CompilerParams(has_side_effects=EFFECT),
    )(*[_hbm(p) for p in ps], *[_hbm(lax.empty(p.shape, p.dtype)) for p in ps])
    return res[0], res[1], list(res[2:2 + n]), list(res[2 + n:2 + 2 * n]), res[-1]


def rs_chip_wait(send, recv, ps, lands, after, name):
    n = len(ps)

    def body(*refs):
        ins, zones = refs[:n], refs[n:2 * n]
        send_r, recv_r = refs[2 * n], refs[2 * n + 1]
        x, y, c, jme, sib, chips = _place()
        for a in range(n):
            for k, chip in enumerate(chips):
                jt = 2 * chip[0] + chip[1]
                cp = _remote(ins[a].at[jt], zones[a].at[jt], send_r, recv_r, a * 3 + k, (*chip, c))
                cp.wait_send()
                cp.wait_recv()

    res = pl.pallas_call(
        body, name=name,
        out_shape=tuple(pltpu.HBM(p.shape, p.dtype) for p in list(ps) + list(lands)),
        in_specs=[HBM] * (2 * n) + [SEM, SEM] + [ANY] * len(after), out_specs=tuple([HBM] * (2 * n)),
        input_output_aliases={a: a for a in range(2 * n)},
        compiler_params=pltpu.CompilerParams(has_side_effects=EFFECT),
    )(*ps, *lands, send, recv, *after)
    return list(res[n:])


def rs_chip_sum(qs, ps, ls, accs, layers, jc):
    n = len(qs)
    tiles = [(min(q.shape[1], SUM_ROWS), q.shape[2]) for q in qs]
    nts = [q.shape[1] // tr for q, (tr, _) in zip(qs, tiles)]

    def at(a, s):
        return jnp.minimum(s, nts[a] - 1)

    def body(jc_ref, *refs):
        jme = jc_ref[0]
        for a in range(n):
            q_ref, p_ref, o_ref = refs[a], refs[n + a], refs[len(refs) - n + a]
            own = p_ref[...].astype(F32)
            v = [jnp.where(jme == j, own, q_ref[j].astype(F32)) for j in range(4)]
            o_ref[...] = ((v[0] + v[1]) + v[2]) + v[3]

    in_specs = [pl.BlockSpec((4,) + tiles[a], lambda s, jr, a=a: (0, at(a, s), 0)) for a in range(n)]
    in_specs += [pl.BlockSpec((None,) + tiles[a], lambda s, jr, a=a: (jr[0], at(a, s), 0)) for a in range(n)]
    args, aliases = [jc, *qs, *ps], {}
    for a in range(n):
        if accs[a] is not None:
            aliases[len(args)] = a
            in_specs.append(ANY)
            args.append(accs[a])
    return pl.pallas_call(
        body, name="rs_chip_sum",
        grid_spec=pltpu.PrefetchScalarGridSpec(
            num_scalar_prefetch=1, grid=(max(nts),), in_specs=in_specs,
            out_specs=[pl.BlockSpec((None, None) + tiles[a], lambda s, jr, a=a: (ls[a], jr[1], at(a, s), 0))
                       for a in range(n)]),
        out_shape=[jax.ShapeDtypeStruct((layers[a], 2) + qs[a].shape[1:], F32) for a in range(n)],
        input_output_aliases=aliases,
        compiler_params=_cparams(("arbitrary",)),
    )(*args)


def rs_pair_gather(rs):
    n = len(rs)

    def body(*refs):
        outs = refs[n:2 * n]
        send, recv = refs[2 * n:]
        x, y, c, jme, sib, chips = _place()
        cps = [_remote(outs[a].at[:, c], outs[a].at[:, c], send, recv, a, sib) for a in range(n)]
        for cp in cps:
            cp.start()
        for a in range(n):
            slot = outs[a].at[:, 1 - c]
            _remote(slot, slot, send, recv, a, sib).wait_recv()
        for cp in cps:
            cp.wait_send()

    return pl.pallas_call(
        body, name="rs_pair_gather", in_specs=[ANY] * n, out_specs=[ANY] * n,
        out_shape=[jax.ShapeDtypeStruct(r.shape, r.dtype) for r in rs],
        input_output_aliases={a: a for a in range(n)},
        scratch_shapes=[pltpu.SemaphoreType.DMA((n,)), pltpu.SemaphoreType.DMA((n,))],
    )(*rs)


def _adamw_math(w, g, m, v):
    m = B1 * m + (1.0 - B1) * g
    v = B2 * v + (1.0 - B2) * (g * g)
    m_hat = m / (1.0 - B1 ** STEP)
    v_hat = v / (1.0 - B2 ** STEP)
    return -LR * (m_hat / (jnp.sqrt(v_hat) + AEPS) + WD * w), m, v


def adamw(w, g, m, v, name, with_grad=False):
    rows, cols = w.shape
    tr = 256 if rows % 256 == 0 else rows
    fn = (lambda wv, gv, mv, vv: (gv,) + _adamw_math(wv, gv, mv, vv)) if with_grad else _adamw_math
    return rw(fn, [(a, 0, cols) for a in (w, g, m, v)], [(cols, F32)] * (4 if with_grad else 3), name, rows, tr=tr)


WEIGHTS = ["norm_ab", "w_in_ab", "pool_w", "pool_scale", "w_out_ab", "norm_cd", "w_in_cd", "sgu_ln_g", "sgu_ln_b",
           "sgu_w", "sgu_b", "s5_a_re", "s5_a_im", "s5_log_dt", "s5_b_re", "s5_b_im", "s5_c_re", "s5_c_im", "s5_d",
           "glu_w1", "glu_w2", "w_out_cd", "norm_x", "w_xq", "w_xkv", "w_xo", "mem_norm", "final_norm"]
INPUTS = ["x", "mem"] + WEIGHTS + ["loss_target"] + ["m_" + n for n in WEIGHTS] + ["v_" + n for n in WEIGHTS]
BIG = ["w_in_ab", "w_out_ab", "w_in_cd", "w_out_cd", "w_xq", "w_xkv", "w_xo", "glu_w1", "glu_w2", "pool_w"]
COL_SHARDED = ("w_in_ab", "w_in_cd", "w_xkv")
SMALL = [n for n in WEIGHTS if n not in BIG]
SMALL_SHARDED = {"norm_cd": 256, "sgu_ln_g": 256, "sgu_ln_b": 256, "s5_d": 128}
PACK = 256 * 128


def _pack(arrs):
    flat = jnp.concatenate([a.reshape(-1) for a in arrs])
    pad = (-flat.shape[0]) % PACK
    return jnp.concatenate([flat, jnp.zeros((pad,), flat.dtype)]).reshape(-1, 128)


def _unpack(packed, shapes):
    flat, out, off = packed.reshape(-1), [], 0
    for s in shapes:
        n = 1
        for d in s:
            n *= d
        out.append(flat[off:off + n].reshape(s))
        off += n
    return out


LAYER_KEYS = (("w_in", "w_out", "pool_w", "w_xq", "w_xkv", "w_xo"),
              ("w_in", "w_out", "glu_w1", "glu_w2", "w_xq", "w_xkv", "w_xo"))


def _weight_of(key, layer):
    if key in ("w_xq", "w_xkv", "w_xo"):
        return key, layer, 4
    kind = "ab" if layer % 2 == 0 else "cd"
    return {"w_in": "w_in_" + kind, "w_out": "w_out_" + kind}.get(key, key), layer // 2, 2


def kernel(*args):
    a = dict(zip(INPUTS, args))
    x_i, y_i, c_i = lax.axis_index("x"), lax.axis_index("y"), lax.axis_index("c")
    j = 2 * x_i + y_i

    slab = jnp.concatenate([a["norm_cd"], a["sgu_ln_g"], a["sgu_ln_b"],
                            jnp.pad(a["s5_d"], ((0, 0), (0, 128)))], axis=0)
    gslab = allgather_small(slab)
    P = {n: a[n] for n in SMALL}
    for k, n in enumerate(("norm_cd", "sgu_ln_g", "sgu_ln_b", "s5_d")):
        wd = SMALL_SHARDED[n]
        P[n] = gslab[:, 2 * k:2 * k + 2, :wd].transpose(1, 0, 2).reshape(2, 4 * wd)

    def shards_of(layer):
        keys = sorted(k for k in LAYER_KEYS[layer % 2])
        out = []
        for k in keys:
            n, l, _ = _weight_of(k, layer)
            out.append(a[n][l].reshape(-1, a[n].shape[-1]).astype(BF16))
        return keys, out

    keys0, sh0 = shards_of(0)
    first = keys0.index("w_in")
    g_in, token = allgather_sync([sh0[first].reshape(2, sh0[first].shape[0] // 2, sh0[first].shape[1])])
    w_in0 = g_in[0].reshape(4, -1, g_in[0].shape[-1])
    started = {}
    for layer in (0, 1, 2, 3):
        keys, sh = (keys0, sh0) if layer == 0 else shards_of(layer)
        rest = [(k, s) for k, s in zip(keys, sh) if k != "w_in"]
        parts = [("in", ["w_in"], [sh[keys.index("w_in")]])] * (layer > 0) + [("", *map(list, zip(*rest)))]
        for tag, pk, ps in parts:
            send, recv, ps, lands, token = allgather_start(ps, [token, gslab], "allgather_start_%d%s" % (layer, tag))
            started[(layer, tag)] = (pk, send, recv, ps, lands)
    P["norm_ab"] = P["norm_ab"] + token[0:1, 0:1]

    cidx = jnp.reshape(c_i, (1,)).astype(jnp.int32)
    jc = jnp.stack([j, c_i]).astype(jnp.int32)

    def views(g):
        W = {}
        for k, v in g.items():
            if k in ("w_in", "w_xkv"):
                W[k] = mcs(v)
            elif k == "pool_w":
                W[k] = v.reshape(4, 4, 64, 256).transpose(1, 0, 2, 3).reshape(4, 256, 256)
            elif k not in ("glu_w1", "glu_w2"):
                W[k] = m2(v.reshape(-1, v.shape[-1]))
        if "glu_w1" in g:
            W["w12"] = jnp.concatenate([g["glu_w1"].reshape(512, 512), g["glu_w2"].reshape(512, 512)], axis=1)
        return W

    def arrived(layer, tag, after):
        keys, send, recv, sh, lands = started[(layer, tag)]
        return views(dict(zip(keys, allgather_wait(send, recv, sh, lands, after, "allgather_wait_%d%s" % (layer, tag)))))

    def weights_of(layer, x_in):
        W = views({"w_in": w_in0}) if layer == 0 else arrived(layer, "in", x_in)
        W["more"] = lambda after: arrived(layer, "", after)
        return W

    halves, pending = {}, {}

    def finish_pair(layer, after):
        keys, send, recv, flat, lands = halves.pop(layer)
        flat, got = rs_pair_wait(send, recv, flat, lands, after, "rs_pair_wait_%d" % layer)
        pair = rs_pair_sum(flat, got, cidx)
        send, recv, pair, lands, token = rs_chip_start(pair, "rs_chip_start_%d" % layer)
        pending[layer] = (keys, send, recv, pair, lands)
        return token

    def grads_done(layer, GW):
        keys = sorted(GW)
        flat = [GW[k].reshape(4, 2, GW[k].shape[1] // 2, GW[k].shape[2]) for k in keys]
        send, recv, flat, lands, token = rs_pair_start(flat, "rs_pair_start_%d" % layer)
        halves[layer] = (keys, send, recv, flat, lands)
        if layer + 1 in halves:
            token = token + finish_pair(layer + 1, token)
        return token[0:1, 0:1]

    loss, dx, G = local_step(a["x"][0], a["mem"][0], a["loss_target"][0], P, weights_of, grads_done)
    loss = lax.psum(loss[0, 0], ("x", "y", "c"))
    finish_pair(0, dx)
    outs = {}

    def update_big(names, red):
        for n, g in zip(names, rs_pair_gather([red[n] for n in names])):
            shp = a[n].shape
            g2 = g.reshape(-1, shp[-1])
            upd = adamw(a[n].reshape(g2.shape), g2, a["m_" + n].reshape(g2.shape), a["v_" + n].reshape(g2.shape),
                        "adamw_" + n, with_grad=True)
            outs[n] = tuple(t.reshape(shp) for t in upd)

    def reduce_layer(layer, red, after):
        keys, send, recv, pair, lands = pending[layer]
        lands = rs_chip_wait(send, recv, pair, lands, after, "rs_chip_wait_%d" % layer)
        which = [_weight_of(k, layer) for k in keys]
        sums = rs_chip_sum(lands, pair, [l for _, l, _ in which], [red.get(n) for n, _, _ in which],
                           [layers for _, _, layers in which], jc)
        red.update(zip([n for n, _, _ in which], sums))

    red = {}
    for layer in (3, 2, 1):
        reduce_layer(layer, red, [dx])
    odd_only = [n for n in BIG if n.endswith("_cd") or n.startswith("glu")]
    update_big(odd_only, red)

    gfull = [jnp.stack(G[n]) if isinstance(G[n], list) else G[n] for n in SMALL]
    shapes = [g.shape for g in gfull]
    gsum = _unpack(allreduce_small(_pack(gfull)), shapes)
    gloc = []
    for n, g in zip(SMALL, gsum):
        if n in SMALL_SHARDED:
            g = lax.dynamic_slice_in_dim(g, j * SMALL_SHARDED[n], SMALL_SHARDED[n], axis=1)
        gloc.append(g)
    for n, g in zip(SMALL, gloc):
        shp = a[n].shape
        two = (-1, shp[-1]) if len(shp) > 1 else (1, shp[0])
        upd = adamw(a[n].reshape(two), g.reshape(two), a["m_" + n].reshape(two), a["v_" + n].reshape(two), "adamw_" + n)
        outs[n] = (g,) + tuple(t.reshape(shp) for t in upd)

    behind = [outs[n][1] for n in odd_only + SMALL[-1:]] + [red[n] for n in BIG if n not in odd_only]
    reduce_layer(0, red, behind)
    update_big([n for n in BIG if n not in odd_only], red)

    res = [loss, dx[None]]
    for part in range(4):
        res += [outs[n][part] for n in WEIGHTS]
    return tuple(res)
```

```python
import math

import jax
import jax.numpy as jnp
from jax import lax
from jax.experimental import pallas as pl
from jax.experimental.pallas import tpu as pltpu

F32, BF16 = jnp.float32, jnp.bfloat16
S, D = 2048, 1024
MEM = 256
EPS = 1e-6
NEG = -1e30
QB = 128
PATTERNS = (1, 4, 16)
NG, NP, NH = 32, 64, 16
NS = NG * NP
LR, B1, B2, AEPS, WD, STEP = 0.001, 0.9, 0.999, 1e-08, 0.01, 10
MESHID = pl.DeviceIdType.MESH
VMEM_LIMIT = 56 * 1024 * 1024


def _cparams(sem):
    return pltpu.CompilerParams(dimension_semantics=sem, vmem_limit_bytes=VMEM_LIMIT)


def _sig(x):
    return 1.0 / (1.0 + jnp.exp(-x))


def _dot(a, b, dims):
    return lax.dot_general(a, b, (dims, ((), ())), preferred_element_type=F32)


def _nn(a, b):
    return _dot(a, b, ((1,), (0,)))


def _nt(a, b):
    return _dot(a, b, ((1,), (1,)))


def _tn(a, b):
    return _dot(a, b, ((0,), (0,)))


_DIMS = {"nn": ((1,), (0,)), "nt": ((1,), (1,)), "tn": ((0,), (0,))}


def _tile(dim, cc=None, cap=1024):
    for t in (2048, 1536, 1024, 768, 512, 384, 256, 128):
        if t <= cap and dim % t == 0 and (cc is None or cc % t == 0):
            return t
    return dim


MM_VMEM = 36 * 1024 * 1024


def _mm_tiles(m, n, k, ccm, ccn, cck, a_bytes, b_bytes, o_bytes):
    caps = [1024, 1024, 2048]
    while True:
        tm, tn, tk = _tile(m, ccm, caps[0]), _tile(n, ccn, caps[1]), _tile(k, cck, caps[2])
        need = 2 * (tm * tk * a_bytes + tk * tn * b_bytes + tm * tn * o_bytes) + (tm * tn * 4 if tk < k else 0)
        if need <= MM_VMEM:
            return tm, tn, tk
        if tk > 1024:
            caps[2] = tk // 2
        elif tn >= tm:
            caps[1] = tn // 2
        else:
            caps[0] = tm // 2


def m2(arr, col_off=0, ncols=None):
    rows, cols = arr.shape
    ncols = cols - col_off if ncols is None else ncols

    def spec(tr, tc, rc):
        assert col_off % tc == 0
        return pl.BlockSpec((tr, tc), lambda *g: (rc(*g)[0], rc(*g)[1] + col_off // tc))
    return (arr, rows, ncols, spec, None if col_off == 0 else col_off)


def mcs(arr):
    cs = arr.shape[2]

    def spec(tr, tc, rc):
        n = cs // tc
        return pl.BlockSpec((None, tr, tc), lambda *g: (rc(*g)[1] // n, rc(*g)[0], rc(*g)[1] % n))
    return (arr, arr.shape[1], 4 * cs, spec, cs)


def out2(rows, cols):
    def spec(tr, tc, rc):
        return pl.BlockSpec((tr, tc), lambda *g: tuple(rc(*g)))
    return ((rows, cols), spec, None)


def outcs(rows, cs):
    def spec(tr, tc, rc):
        n = cs // tc
        return pl.BlockSpec((None, tr, tc), lambda *g: (rc(*g)[1] // n, rc(*g)[0], rc(*g)[1] % n))
    return ((4, rows, cs), spec, cs)


def _both(a, b):
    if a is None:
        return b
    if b is None:
        return a
    return math.gcd(a, b)


def mm(a, b, mode, name, add=None, out=None, out_dtype=F32):
    a_arr, a_r, a_c, a_spec, a_cc = a
    b_arr, b_r, b_c, b_spec, b_cc = b
    if mode == "nn":
        m, k, n = a_r, a_c, b_c
        assert b_r == k
        ccm, cck, ccn = None, a_cc, b_cc
    elif mode == "nt":
        m, k, n = a_r, a_c, b_r
        assert b_c == k
        ccm, cck, ccn = None, _both(a_cc, b_cc), None
    else:
        m, k, n = a_c, a_r, b_c
        assert b_r == k
        ccm, cck, ccn = a_cc, None, b_cc
    out = out2(m, n) if out is None else out
    o_shape, o_spec, o_cc = out
    ccn = _both(ccn, o_cc)
    if add is not None:
        ccn = _both(ccn, add[4])
    o_bytes = jnp.dtype(out_dtype).itemsize + (0 if add is None else add[0].dtype.itemsize)
    tm, tn, tk = _mm_tiles(m, n, k, ccm, ccn, cck, a_arr.dtype.itemsize, b_arr.dtype.itemsize, o_bytes)
    nk = k // tk
    if mode == "nn":
        in_specs = [a_spec(tm, tk, lambda i, j, kk: (i, kk)), b_spec(tk, tn, lambda i, j, kk: (kk, j))]
    elif mode == "nt":
        in_specs = [a_spec(tm, tk, lambda i, j, kk: (i, kk)), b_spec(tn, tk, lambda i, j, kk: (j, kk))]
    else:
        in_specs = [a_spec(tk, tm, lambda i, j, kk: (kk, i)), b_spec(tk, tn, lambda i, j, kk: (kk, j))]
    args = [a_arr, b_arr]
    if add is not None:
        in_specs.append(add[3](tm, tn, lambda i, j, kk: (i, j)))
        args.append(add[0])
    return _mm_call(args, in_specs, o_spec(tm, tn, lambda i, j, kk: (i, j)), jax.ShapeDtypeStruct(o_shape, out_dtype),
                    mode, (m // tm, n // tn, nk), (tm, tn), add is not None, name)


def _mm_call(args, in_specs, out_spec, out_shape, mode, grid, tile, has_add, name):
    dims = _DIMS[mode]
    nk = grid[2]
    tm, tn = tile

    def body(*refs):
        a_ref, b_ref = refs[0], refs[1]
        add_ref = refs[2] if has_add else None
        prod = _dot(a_ref[...].astype(BF16), b_ref[...].astype(BF16), dims)
        if nk == 1:
            o_ref = refs[-1]
            if has_add:
                prod = prod + add_ref[...].astype(F32)
            o_ref[...] = prod.astype(o_ref.dtype)
            return
        o_ref, acc = refs[-2], refs[-1]
        kk = pl.program_id(2)

        @pl.when(kk == 0)
        def _():
            acc[...] = prod

        @pl.when(kk > 0)
        def _():
            acc[...] += prod

        @pl.when(kk == nk - 1)
        def _():
            r = acc[...]
            if has_add:
                r = r + add_ref[...].astype(F32)
            o_ref[...] = r.astype(o_ref.dtype)

    return pl.pallas_call(
        body, name=name, grid=grid, in_specs=in_specs, out_specs=out_spec, out_shape=out_shape,
        scratch_shapes=[pltpu.VMEM((tm, tn), F32)] if nk > 1 else [],
        compiler_params=_cparams(("parallel", "parallel", "arbitrary")),
    )(*args)


def mm_band(a, b, mode, name, grid, blocks, maps, out_shape, add=None, out_dtype=F32):
    in_specs = [pl.BlockSpec(blocks[0], maps[0]), pl.BlockSpec(blocks[1], maps[1])]
    args = [a, b]
    if add is not None:
        in_specs.append(pl.BlockSpec(blocks[2], maps[2]))
        args.append(add)
    return _mm_call(args, in_specs, pl.BlockSpec(blocks[2], maps[2]), jax.ShapeDtypeStruct(out_shape, out_dtype),
                    mode, grid, blocks[2], add is not None, name)


def rw(fn, ins, outs, name, rows, tr=256, consts=(), accs=()):
    n_in, n_c, n_o, n_a = len(ins), len(consts), len(outs), len(accs)
    in_specs = []
    for arr, off, width in ins:
        assert off % width == 0
        in_specs.append(pl.BlockSpec((tr, width), lambda i, o=off // width: (i, o)))
    for c in consts:
        in_specs.append(pl.BlockSpec(c.shape, lambda i: (0, 0)))
    out_specs = [pl.BlockSpec((tr, w), lambda i: (i, 0)) for w, _ in outs]
    out_specs += [pl.BlockSpec(s, lambda i: (0, 0)) for s in accs]
    out_shape = [jax.ShapeDtypeStruct((rows, w), dt) for w, dt in outs]
    out_shape += [jax.ShapeDtypeStruct(s, F32) for s in accs]

    def body(*refs):
        vals = [r[...] for r in refs[:n_in + n_c]]
        o_refs = refs[n_in + n_c:n_in + n_c + n_o]
        a_refs = refs[n_in + n_c + n_o:]
        res = fn(*vals)
        for r, v in zip(o_refs, res[:n_o]):
            r[...] = v.astype(r.dtype)
        if n_a:
            @pl.when(pl.program_id(0) == 0)
            def _():
                for r in a_refs:
                    r[...] = jnp.zeros_like(r)
            for r, v in zip(a_refs, res[n_o:]):
                r[...] += v

    res = pl.pallas_call(
        body, name=name, grid=(rows // tr,), in_specs=in_specs, out_specs=out_specs,
        out_shape=out_shape,
        compiler_params=_cparams(("arbitrary",) if n_a else ("parallel",)),
    )(*[a for a, _, _ in ins], *consts)
    return res


def _rstd(x):
    return lax.rsqrt(jnp.mean(x * x, axis=-1, keepdims=True) + EPS)


def rms_fwd(x, g, name):
    def fn(xv, gv):
        xv = xv.astype(F32)
        return (xv * _rstd(xv) * gv,)
    return rw(fn, [(x, 0, D)], [(D, BF16)], name, x.shape[0], consts=[g])[0]


def _rms_bwd_math(xv, dy, gv):
    r = _rstd(xv)
    dyg = dy * gv
    dx = r * dyg - xv * (r * r * r / D) * jnp.sum(dyg * xv, axis=-1, keepdims=True)
    dg = jnp.sum(dy * xv * r, axis=0, keepdims=True)
    return dx, dg


def rms_bwd(x, dy, dres, g, name):
    def fn(xv, dyv, drv, gv):
        dx, dg = _rms_bwd_math(xv, dyv, gv)
        return dx + drv, dg
    return rw(fn, [(x, 0, D), (dy, 0, D), (dres, 0, D)], [(D, F32)], name, x.shape[0],
              consts=[g], accs=[(1, D)])


def final_loss(x, tgt, g):
    def fn(xv, tv, gv):
        e = xv * _rstd(xv) * gv - tv
        loss = 0.5 * jnp.sum(jnp.sum(e * e, axis=-1, keepdims=True), axis=0, keepdims=True) / D
        dx, dg = _rms_bwd_math(xv, e / D, gv)
        return dx, loss, dg
    return rw(fn, [(x, 0, D), (tgt, 0, D)], [(D, F32)], "final_loss", S, consts=[g],
              accs=[(1, 1), (1, D)])


def _attn_bias(bias_ref):
    ii = lax.broadcasted_iota(jnp.int32, (2 * QB, 2 * QB), 0) % QB
    jj = lax.broadcasted_iota(jnp.int32, (2 * QB, 2 * QB), 1)
    dist = ii + QB - jj
    band = (dist >= 0) & (dist <= QB)
    bias_ref[1] = jnp.where(band, 0.0, NEG)
    bias_ref[0] = jnp.where(band & (jj >= QB), 0.0, NEG)


def _two_heads(x, m0):
    return jnp.concatenate([jnp.where(m0, x, 0.0), jnp.where(m0, 0.0, x)], axis=0)


def _per_head(col, m0):
    return jnp.where(m0, col[:QB], col[QB:])


def _attn_rows(idx, d):
    if d == 1:
        b = idx
        cur = pl.ds(pl.multiple_of(b * QB, QB), QB)
        prev = pl.ds(pl.multiple_of(jnp.maximum(b - 1, 0) * QB, QB), QB)
    else:
        r, b = lax.rem(idx, d), lax.div(idx, d)
        cur = pl.ds(r + b * (QB * d), QB, stride=d)
        prev = pl.ds(r + jnp.maximum(b - 1, 0) * (QB * d), QB, stride=d)
    return cur, prev, b


NBLK = S // QB
GROUP = 4


def _colblk(off):
    return pl.BlockSpec((S, 128), lambda hp: (0, off * 8 + hp))


def attn_fwd(z):
    def body(q_ref, k_ref, v_ref, g_ref, o_ref, l_ref, a_ref, os, ls, bias):
        _attn_bias(bias)
        m0 = lax.broadcasted_iota(jnp.int32, (1, 128), 1) < 64
        for pi, d in enumerate(PATTERNS):
            lone = S // d == QB

            def load(idx, d=d, lone=lone):
                cur, prev, b = _attn_rows(idx, d)
                if lone:
                    return cur, (q_ref[cur, :], None, k_ref[cur, :], None, v_ref[cur, :], bias[1, :, QB:])
                return cur, (q_ref[cur, :], k_ref[prev, :], k_ref[cur, :], v_ref[prev, :], v_ref[cur, :],
                             bias[jnp.minimum(b, 1)])

            def block(q, kp, kc, vp, vc, bs):
                qq = _two_heads(q * 0.125, m0).astype(BF16)
                k = (kc if kp is None else jnp.concatenate([kp, kc], axis=0)).astype(BF16)
                s = _nt(qq, k) + bs
                mx = jnp.max(s, axis=-1, keepdims=True)
                p = jnp.exp(s - mx)
                den = jnp.sum(p, axis=-1, keepdims=True)
                pb = p.astype(BF16)
                vv = _two_heads(vc if vp is None else jnp.concatenate([vp, vc], axis=0), m0).astype(BF16)
                o = _nn(jnp.concatenate([pb[:QB], pb[QB:]], axis=1), vv)
                return o * _per_head(1.0 / den, m0), _per_head(mx + jnp.log(den), m0)

            def step(i, carry, pi=pi):
                loaded = [load(i * GROUP + u) for u in range(GROUP)]
                done = [block(*vals) for _, vals in loaded]
                for (cur, _), (o, l) in zip(loaded, done):
                    os[pi, cur, :] = o
                    ls[pi, cur, :] = l
                return carry
            lax.fori_loop(0, NBLK // GROUP, step, 0)
        l1, l2, l3 = ls[0], ls[1], ls[2]
        mx = jnp.maximum(jnp.maximum(l1, l2), l3)
        e1, e2, e3 = jnp.exp(l1 - mx), jnp.exp(l2 - mx), jnp.exp(l3 - mx)
        tot = e1 + e2 + e3
        o = (os[0] * e1 + os[1] * e2 + os[2] * e3) / tot
        ga = g_ref[...]
        o_ref[...] = o
        l_ref[...] = mx + jnp.log(tot)
        a_ref[...] = (o * (ga * _sig(ga))).astype(a_ref.dtype)

    out = pl.BlockSpec((S, 128), lambda hp: (0, hp))
    return pl.pallas_call(
        body, name="attn_fwd", grid=(8,),
        in_specs=[_colblk(0), _colblk(1), _colblk(2), _colblk(3)], out_specs=[out] * 3,
        out_shape=[jax.ShapeDtypeStruct((S, D), F32), jax.ShapeDtypeStruct((S, D), F32),
                   jax.ShapeDtypeStruct((S, 2 * D), BF16)],
        scratch_shapes=[pltpu.VMEM((3, S, 128), F32), pltpu.VMEM((3, S, 128), F32),
                        pltpu.VMEM((2, 2 * QB, 2 * QB), F32)],
        compiler_params=_cparams(("parallel",)),
    )(z, z, z, z)


def attn_bwd(z, d_cat, o, lse):
    def body(q_ref, k_ref, v_ref, g_ref, da_ref, o_ref, l_ref, dq_ref, dk_ref, dv_ref, dg_ref, do_s, pr_s, bias):
        _attn_bias(bias)
        m0 = lax.broadcasted_iota(jnp.int32, (1, 128), 1) < 64
        ga = g_ref[...]
        sg = _sig(ga)
        da = da_ref[...]
        ov = o_ref[...]
        do = da * (ga * sg)
        dg_ref[...] = da * ov * (sg * (1.0 + ga * (1.0 - sg)))
        do_s[...] = do
        pr_s[...] = do * ov
        dq_ref[...] = jnp.zeros_like(dq_ref)
        dk_ref[...] = jnp.zeros_like(dk_ref)
        dv_ref[...] = jnp.zeros_like(dv_ref)
        for d in PATTERNS:
            lone = S // d == QB

            def load(idx, d=d, lone=lone):
                cur, prev, b = _attn_rows(idx, d)
                if lone:
                    return (cur, None), (q_ref[cur, :], None, k_ref[cur, :], None, v_ref[cur, :],
                                         do_s[cur, :], pr_s[cur, :], l_ref[cur, :], bias[1, :, QB:])
                return (cur, prev), (q_ref[cur, :], k_ref[prev, :], k_ref[cur, :], v_ref[prev, :], v_ref[cur, :],
                                     do_s[cur, :], pr_s[cur, :], l_ref[cur, :], bias[jnp.minimum(b, 1)])

            def block(q, kp, kc, vp, vc, dof, prod, lp, bs):
                qq = _two_heads(q * 0.125, m0).astype(BF16)
                kf = kc if kp is None else jnp.concatenate([kp, kc], axis=0)
                k = kf.astype(BF16)
                v = (vc if vp is None else jnp.concatenate([vp, vc], axis=0)).astype(BF16)
                dd = _two_heads(dof, m0).astype(BF16)
                lh = jnp.max(jnp.concatenate([jnp.where(m0, lp, -jnp.inf), jnp.where(m0, -jnp.inf, lp)], axis=0),
                             axis=-1, keepdims=True)
                delta = jnp.sum(_two_heads(prod, m0), axis=-1, keepdims=True)
                p = jnp.exp(_nt(qq, k) + bs - lh)
                ds = (p * (_nt(dd, v) - delta)).astype(BF16)
                dq = _nn(jnp.concatenate([ds[:QB], ds[QB:]], axis=1), _two_heads(kf, m0).astype(BF16))
                return dq * 0.125, _tn(ds, qq), _tn(p.astype(BF16), dd)

            def step(i, carry):
                loaded = [load(i * GROUP + u) for u in range(GROUP)]
                done = [block(*vals) for _, vals in loaded]
                for ((cur, prev), _), (dq, dk, dv) in zip(loaded, done):
                    dq_ref[cur, :] = dq_ref[cur, :] + dq
                    if prev is not None:
                        dk_ref[prev, :] = dk_ref[prev, :] + dk[:QB]
                        dv_ref[prev, :] = dv_ref[prev, :] + dv[:QB]
                    dk_ref[cur, :] = dk_ref[cur, :] + dk[-QB:]
                    dv_ref[cur, :] = dv_ref[cur, :] + dv[-QB:]
                return carry
            lax.fori_loop(0, NBLK // GROUP, step, 0)

    blk = pl.BlockSpec((S, 128), lambda hp: (0, hp))
    return pl.pallas_call(
        body, name="attn_bwd", grid=(8,),
        in_specs=[_colblk(0), _colblk(1), _colblk(2), _colblk(3), blk, blk, blk], out_specs=[blk] * 4,
        out_shape=[jax.ShapeDtypeStruct((S, D), F32)] * 4,
        scratch_shapes=[pltpu.VMEM((S, 128), F32), pltpu.VMEM((S, 128), F32), pltpu.VMEM((2, 2 * QB, 2 * QB), F32)],
        compiler_params=_cparams(("parallel",)),
    )(z, z, z, z, d_cat, o, lse)


def assemble_dz_even(parts):
    def body(*refs):
        o_ref = refs[-1]
        for j in range(6):
            o_ref[:, j * D:(j + 1) * D] = refs[j][...].astype(o_ref.dtype)
    tr = 256
    blk = pl.BlockSpec((tr, D), lambda i: (i, 0))
    return pl.pallas_call(
        body, name="assemble_dz_even", grid=(S // tr,), in_specs=[blk] * 6,
        out_specs=pl.BlockSpec((tr, 6 * D), lambda i: (i, 0)),
        out_shape=jax.ShapeDtypeStruct((S, 6 * D), BF16),
        compiler_params=_cparams(("parallel",)),
    )(*parts)


def _pool_window(g):
    return jnp.where(g == 0, 2.0, jnp.where(g == 1, 4.0, jnp.where(g == 2, 8.0, 16.0)))


def _pool_sel(g, levels):
    return jnp.where(g == 0, levels[0], jnp.where(g == 1, levels[1], jnp.where(g == 2, levels[2], levels[3])))


def _pool_fwd_math(v, g):
    t = lax.broadcasted_iota(jnp.int32, (S, 1), 0)
    s = v
    levels = []
    for k in (1, 2, 4, 8):
        s = s + jnp.where(t >= k, pltpu.roll(s, k, 0), 0.0)
        levels.append(s)
    cnt = jnp.minimum((t + 1).astype(F32), _pool_window(g))
    return _pool_sel(g, levels) / cnt - v, cnt


def pool_fwd(z, pw, ps, cat):
    def body(v_ref, g_ref, pw_ref, ps_ref, cat_ref, o_ref):
        g = pl.program_id(0)
        pooled, _ = _pool_fwd_math(v_ref[...], g)
        mixed = _nn(pooled.astype(BF16), pw_ref[...].astype(BF16))
        gb = g_ref[...]
        o_ref[...] = (mixed * ps_ref[...] * (gb * _sig(gb))).astype(o_ref.dtype)

    return pl.pallas_call(
        body, name="pool_fwd", grid=(4,),
        in_specs=[pl.BlockSpec((S, 256), lambda g: (0, 16 + g)),
                  pl.BlockSpec((S, 256), lambda g: (0, 20 + g)),
                  pl.BlockSpec((None, 256, 256), lambda g: (g, 0, 0)),
                  pl.BlockSpec((1, 256), lambda g: (0, g)), pl.BlockSpec(memory_space=pl.ANY)],
        out_specs=pl.BlockSpec((S, 256), lambda g: (0, 4 + g)),
        out_shape=jax.ShapeDtypeStruct((S, 2 * D), BF16),
        input_output_aliases={4: 0},
        compiler_params=_cparams(("parallel",)),
    )(z, z, pw, ps, cat)


def pool_bwd(z, d_cat, pw, ps):
    def body(v_ref, g_ref, d_ref, pw_ref, ps_ref, dv_ref, dg_ref, dpw_ref, dps_ref):
        g = pl.program_id(0)
        v = v_ref[...]
        pooled, cnt = _pool_fwd_math(v, g)
        pwb = pw_ref[...].astype(BF16)
        pb = pooled.astype(BF16)
        mixed = _nn(pb, pwb)
        gb = g_ref[...]
        sg = _sig(gb)
        dout = d_ref[...]
        sc = ps_ref[...]
        dg_ref[...] = dout * mixed * sc * (sg * (1.0 + gb * (1.0 - sg)))
        dms = dout * (gb * sg)
        dps_ref[...] = jnp.sum(dms * mixed, axis=0, keepdims=True)
        dmx = (dms * sc).astype(BF16)
        dpw_ref[...] = _tn(pb, dmx)
        dpooled = _nt(dmx, pwb)
        t = lax.broadcasted_iota(jnp.int32, (S, 1), 0)
        s = dpooled / cnt
        levels = []
        for k in (1, 2, 4, 8):
            s = s + jnp.where(t < S - k, pltpu.roll(s, S - k, 0), 0.0)
            levels.append(s)
        dv_ref[...] = _pool_sel(g, levels) - dpooled

    return pl.pallas_call(
        body, name="pool_bwd", grid=(4,),
        in_specs=[pl.BlockSpec((S, 256), lambda g: (0, 16 + g)),
                  pl.BlockSpec((S, 256), lambda g: (0, 20 + g)),
                  pl.BlockSpec((S, 256), lambda g: (0, 4 + g)),
                  pl.BlockSpec((None, 256, 256), lambda g: (g, 0, 0)),
                  pl.BlockSpec((1, 256), lambda g: (0, g))],
        out_specs=[pl.BlockSpec((S, 256), lambda g: (0, g)),
                   pl.BlockSpec((S, 256), lambda g: (0, g)),
                   pl.BlockSpec((None, 256, 256), lambda g: (g, 0, 0)),
                   pl.BlockSpec((1, 256), lambda g: (0, g))],
        out_shape=[jax.ShapeDtypeStruct((S, D), F32), jax.ShapeDtypeStruct((S, D), F32),
                   jax.ShapeDtypeStruct((4, 256, 256), F32), jax.ShapeDtypeStruct((1, D), F32)],
        compiler_params=_cparams(("parallel",)),
    )(z, z, d_cat, pw, ps)


CH = 128


def _sgu_common(v, lng, lnb, w_ref):
    mu = jnp.mean(v, axis=-1, keepdims=True)
    vc = v - mu
    rs = lax.rsqrt(jnp.mean(vc * vc, axis=-1, keepdims=True) + EPS)
    xhat = vc * rs
    vn = (xhat * lng + lnb).astype(BF16)
    ri = lax.broadcasted_iota(jnp.int32, (CH, CH), 0)
    ci = lax.broadcasted_iota(jnp.int32, (CH, CH), 1)
    tril = ri >= ci
    ws = [jnp.where(tril, w_ref[g], 0.0).astype(BF16) for g in range(4)]
    return xhat, rs, vn, tril, ws


def _zspec(off):
    return pl.BlockSpec((CH, D), lambda c: (c, off))


def _full(shape):
    return pl.BlockSpec(shape, lambda c: (0,) * len(shape))


def sgu_fwd(z, lng, lnb, w, bfull):
    def body(u_ref, v_ref, g_ref, lng_ref, lnb_ref, w_ref, b_ref, o_ref):
        _, _, vn, _, ws = _sgu_common(v_ref[...], lng_ref[...], lnb_ref[...], w_ref)
        for g in range(4):
            sl = slice(g * 256, (g + 1) * 256)
            mixed = _nn(ws[g], vn[:, sl]) + b_ref[:, sl]
            gc = g_ref[:, sl]
            o_ref[:, sl] = (u_ref[:, sl] * mixed * (gc * _sig(gc))).astype(o_ref.dtype)

    return pl.pallas_call(
        body, name="sgu_fwd", grid=(S // CH,),
        in_specs=[_zspec(0), _zspec(1), _zspec(2), _full((1, D)), _full((1, D)),
                  _full((4, CH, CH)), _full((CH, D))],
        out_specs=pl.BlockSpec((CH, D), lambda c: (c, 0)),
        out_shape=jax.ShapeDtypeStruct((S, D), BF16),
        compiler_params=_cparams(("parallel",)),
    )(z, z, z, lng, lnb, w, bfull)


def sgu_bwd(z, d_cat, lng, lnb, w, bfull):
    def body(u_ref, v_ref, g_ref, d_ref, lng_ref, lnb_ref, w_ref, b_ref,
             du_ref, dv_ref, dg_ref, dw_ref, db_ref, dlg_ref, dlb_ref):
        @pl.when(pl.program_id(0) == 0)
        def _():
            dw_ref[...] = jnp.zeros_like(dw_ref)
            db_ref[...] = jnp.zeros_like(db_ref)
            dlg_ref[...] = jnp.zeros_like(dlg_ref)
            dlb_ref[...] = jnp.zeros_like(dlb_ref)

        lng = lng_ref[...]
        xhat, rs, vn, tril, ws = _sgu_common(v_ref[...], lng, lnb_ref[...], w_ref)
        lane = lax.broadcasted_iota(jnp.int32, (1, 128), 1)
        db = jnp.zeros((CH, 128), F32)
        dvn_parts = []
        for g in range(4):
            sl = slice(g * 256, (g + 1) * 256)
            mixed = _nn(ws[g], vn[:, sl]) + b_ref[:, sl]
            gc = g_ref[:, sl]
            sg = _sig(gc)
            u = u_ref[:, sl]
            dc = d_ref[:, sl]
            du_ref[:, sl] = dc * mixed * (gc * sg)
            dg_ref[:, sl] = dc * u * mixed * (sg * (1.0 + gc * (1.0 - sg)))
            dmx = dc * u * (gc * sg)
            db = db + jnp.where(lane == g, jnp.sum(dmx, axis=-1, keepdims=True), 0.0)
            dmb = dmx.astype(BF16)
            dw_ref[g] += jnp.where(tril, _nt(dmb, vn[:, sl]), 0.0)
            dvn_parts.append(_tn(ws[g], dmb))
        db_ref[...] += db
        dvn = jnp.concatenate(dvn_parts, axis=1)
        dlb_ref[...] += jnp.sum(dvn, axis=0, keepdims=True)
        dlg_ref[...] += jnp.sum(dvn * xhat, axis=0, keepdims=True)
        dxh = dvn * lng
        dv_ref[...] = rs * (dxh - jnp.mean(dxh, axis=-1, keepdims=True)
                            - xhat * jnp.mean(dxh * xhat, axis=-1, keepdims=True))

    row = pl.BlockSpec((CH, D), lambda c: (c, 0))
    return pl.pallas_call(
        body, name="sgu_bwd", grid=(S // CH,),
        in_specs=[_zspec(0), _zspec(1), _zspec(2), row, _full((1, D)), _full((1, D)),
                  _full((4, CH, CH)), _full((CH, D))],
        out_specs=[row, row, row, _full((4, CH, CH)), _full((CH, 128)), _full((1, D)), _full((1, D))],
        out_shape=[jax.ShapeDtypeStruct((S, D), F32)] * 3
        + [jax.ShapeDtypeStruct((4, CH, CH), F32), jax.ShapeDtypeStruct((CH, 128), F32),
           jax.ShapeDtypeStruct((1, D), F32), jax.ShapeDtypeStruct((1, D), F32)],
        compiler_params=_cparams(("arbitrary",)),
    )(z, z, z, d_cat, lng, lnb, w, bfull)


TB = 256


def _cmul(ar, ai, br, bi):
    return ar * br - ai * bi, ar * bi + ai * br


def _scan_consts(ar, ai, reverse):
    a2 = _cmul(ar, ai, ar, ai)
    a4 = _cmul(*a2, *a2)
    row = lax.broadcasted_iota(jnp.int32, (8, NS), 0)
    pr = jnp.zeros((8, NS), F32)
    pi = jnp.zeros((8, NS), F32)
    cr, ci = ar, ai
    for r in range(8):
        sel = row == (7 - r if reverse else r)
        pr = jnp.where(sel, cr, pr)
        pi = jnp.where(sel, ci, pi)
        cr, ci = _cmul(cr, ci, ar, ai)
    return ((ar, ai), a2, a4), (pr, pi), row


def scan_fwd(bu, abr, abi):
    def body(bu_ref, ar_ref, ai_ref, h_ref, car, cai):
        @pl.when(pl.program_id(0) == 0)
        def _():
            car[...] = jnp.zeros_like(car)
            cai[...] = jnp.zeros_like(cai)

        pows, (pr, pi), row = _scan_consts(ar_ref[...], ai_ref[...], False)

        def tile(t, carry):
            c_r, c_i = carry
            rows = pl.ds(pl.multiple_of(t * 8, 8), 8)
            xr = bu_ref[rows, 0:NS]
            xi = bu_ref[rows, NS:2 * NS]
            for k, (kr, ki) in zip((1, 2, 4), pows):
                sr = jnp.where(row >= k, pltpu.roll(xr, k, 0), 0.0)
                si = jnp.where(row >= k, pltpu.roll(xi, k, 0), 0.0)
                xr, xi = xr + kr * sr - ki * si, xi + kr * si + ki * sr
            xr, xi = xr + pr * c_r - pi * c_i, xi + pr * c_i + pi * c_r
            h_ref[rows, 0:NS] = xr
            h_ref[rows, NS:2 * NS] = xi
            return (jnp.broadcast_to(xr[7:8, :], (8, NS)), jnp.broadcast_to(xi[7:8, :], (8, NS)))

        c_r, c_i = lax.fori_loop(0, TB // 8, tile, (car[...], cai[...]))
        car[...] = c_r
        cai[...] = c_i

    return pl.pallas_call(
        body, name="s5_scan_fwd", grid=(S // TB,),
        in_specs=[pl.BlockSpec((TB, 2 * NS), lambda i: (i, 0)),
                  pl.BlockSpec((1, NS), lambda i: (0, 0)), pl.BlockSpec((1, NS), lambda i: (0, 0))],
        out_specs=pl.BlockSpec((TB, 2 * NS), lambda i: (i, 0)),
        out_shape=jax.ShapeDtypeStruct((S, 2 * NS), F32),
        scratch_shapes=[pltpu.VMEM((8, NS), F32), pltpu.VMEM((8, NS), F32)],
        compiler_params=_cparams(("arbitrary",)),
    )(bu, abr, abi)


def scan_bwd(eta, h, abr, abi):
    nt = S // TB

    def body(e_ref, h_ref, ar_ref, ai_ref, l_ref, da_ref, car, cai):
        @pl.when(pl.program_id(0) == 0)
        def _():
            car[...] = jnp.zeros_like(car)
            cai[...] = jnp.zeros_like(cai)
            da_ref[...] = jnp.zeros_like(da_ref)

        pows, (pr, pi), row = _scan_consts(ar_ref[...], -ai_ref[...], True)

        def tile(tt, carry):
            c_r, c_i, acr, aci = carry
            t = TB // 8 - 1 - tt
            rows = pl.ds(pl.multiple_of(t * 8, 8), 8)
            xr = e_ref[rows, 0:NS]
            xi = e_ref[rows, NS:2 * NS]
            for k, (kr, ki) in zip((1, 2, 4), pows):
                sr = jnp.where(row < 8 - k, pltpu.roll(xr, 8 - k, 0), 0.0)
                si = jnp.where(row < 8 - k, pltpu.roll(xi, 8 - k, 0), 0.0)
                xr, xi = xr + kr * sr - ki * si, xi + kr * si + ki * sr
            xr, xi = xr + pr * c_r - pi * c_i, xi + pr * c_i + pi * c_r
            l_ref[rows, 0:NS] = xr
            l_ref[rows, NS:2 * NS] = xi
            nr = jnp.where(row < 7, pltpu.roll(xr, 7, 0), c_r)
            ni = jnp.where(row < 7, pltpu.roll(xi, 7, 0), c_i)
            hr = h_ref[rows, 0:NS]
            hi = h_ref[rows, NS:2 * NS]
            acr = acr + hr * nr + hi * ni
            aci = aci + hr * ni - hi * nr
            return (jnp.broadcast_to(xr[0:1, :], (8, NS)), jnp.broadcast_to(xi[0:1, :], (8, NS)), acr, aci)

        zero = jnp.zeros((8, NS), F32)
        c_r, c_i, acr, aci = lax.fori_loop(0, TB // 8, tile, (car[...], cai[...], zero, zero))
        car[...] = c_r
        cai[...] = c_i
        da_ref[:, 0:NS] += acr
        da_ref[:, NS:2 * NS] += aci

    rev = pl.BlockSpec((TB, 2 * NS), lambda i: (nt - 1 - i, 0))
    return pl.pallas_call(
        body, name="s5_scan_bwd", grid=(nt,),
        in_specs=[rev, rev, pl.BlockSpec((1, NS), lambda i: (0, 0)), pl.BlockSpec((1, NS), lambda i: (0, 0))],
        out_specs=[rev, pl.BlockSpec((8, 2 * NS), lambda i: (0, 0))],
        out_shape=[jax.ShapeDtypeStruct((S, 2 * NS), F32), jax.ShapeDtypeStruct((8, 2 * NS), F32)],
        scratch_shapes=[pltpu.VMEM((8, NS), F32), pltpu.VMEM((8, NS), F32)],
        compiler_params=_cparams(("arbitrary",)),
    )(eta, h, abr, abi)


GC = 0.7978845608028654
GA = 0.044715


def s5_post(hc, z, dskip):
    def fn(hv, xd, dv):
        y = hv + dv * xd
        return y, 0.5 * y * (1.0 + jnp.tanh(GC * (y + GA * y * y * y)))
    return rw(fn, [(hc, 0, 512), (z, 3072, 512)], [(512, F32), (512, BF16)], "s5_post", S, consts=[dskip])


def s5_post_bwd(dyg, ypre, z, dskip):
    def fn(dy, y, xd, dv):
        th = jnp.tanh(GC * (y + GA * y * y * y))
        dg = 0.5 * (1.0 + th) + 0.5 * y * (1.0 - th * th) * GC * (1.0 + 3.0 * GA * y * y)
        dyp = dy * dg
        return dyp, dyp * dv, jnp.sum(dyp * xd, axis=0, keepdims=True)
    return rw(fn, [(dyg, 0, 512), (ypre, 0, 512), (z, 3072, 512)], [(512, BF16), (512, F32)],
              "s5_post_bwd", S, consts=[dskip], accs=[(1, 512)])


def glu_fwd(t, z, c_out):
    def fn(t1, t2, gd, co):
        return (jnp.concatenate([co, (t1 * _sig(t2) * (gd * _sig(gd))).astype(BF16)], axis=1),)
    return rw(fn, [(t, 0, 512), (t, 512, 512), (z, 3584, 512), (c_out, 0, D)], [(D + 512, BF16)], "glu_fwd", S)[0]


def glu_bwd(t, z, d_cat):
    def fn(t1, t2, gd, dd):
        s2, sg = _sig(t2), _sig(gd)
        sl = gd * sg
        return (jnp.concatenate([dd * s2 * sl, dd * t1 * s2 * (1.0 - s2) * sl], axis=1),
                dd * t1 * s2 * (sg * (1.0 + gd * (1.0 - sg))))
    return rw(fn, [(t, 0, 512), (t, 512, 512), (z, 3584, 512), (d_cat, 1024, 512)],
              [(D, BF16), (512, F32)], "glu_bwd", S)


def assemble_dz_odd(du, dv, dgc, dxd, dgd):
    def body(a, b, c, d, e, o_ref):
        o_ref[:, 0:D] = a[...].astype(BF16)
        o_ref[:, D:2 * D] = b[...].astype(BF16)
        o_ref[:, 2 * D:3 * D] = c[...].astype(BF16)
        o_ref[:, 3 * D:3 * D + 512] = d[...].astype(BF16)
        o_ref[:, 3 * D + 512:4 * D] = e[...].astype(BF16)
    tr = 256
    blk = pl.BlockSpec((tr, D), lambda i: (i, 0))
    half = pl.BlockSpec((tr, 512), lambda i: (i, 0))
    return pl.pallas_call(
        body, name="assemble_dz_odd", grid=(S // tr,), in_specs=[blk, blk, blk, half, half],
        out_specs=pl.BlockSpec((tr, 4 * D), lambda i: (i, 0)),
        out_shape=jax.ShapeDtypeStruct((S, 4 * D), BF16),
        compiler_params=_cparams(("parallel",)),
    )(du, dv, dgc, dxd, dgd)


TQ = 256


def _xattn_probs(qh, kh):
    s = _nt(qh, kh) * 0.0625
    p = jnp.exp(s - jnp.max(s, axis=-1, keepdims=True))
    return p / jnp.sum(p, axis=-1, keepdims=True)


def xattn_fwd(q, kv):
    def body(q_ref, kv_ref, o_ref):
        for h in range(4):
            sl = slice(h * 256, (h + 1) * 256)
            p = _xattn_probs(q_ref[:, sl].astype(BF16), kv_ref[:, sl].astype(BF16))
            vh = kv_ref[:, D + h * 256:D + (h + 1) * 256].astype(BF16)
            o_ref[:, sl] = _nn(p.astype(BF16), vh).astype(o_ref.dtype)

    return pl.pallas_call(
        body, name="xattn_fwd", grid=(S // TQ,),
        in_specs=[pl.BlockSpec((TQ, D), lambda i: (i, 0)), pl.BlockSpec((MEM, 2 * D), lambda i: (0, 0))],
        out_specs=pl.BlockSpec((TQ, D), lambda i: (i, 0)),
        out_shape=jax.ShapeDtypeStruct((S, D), BF16),
        compiler_params=_cparams(("parallel",)),
    )(q, kv)


def xattn_bwd(q, kv, d_o):
    def body(q_ref, kv_ref, do_ref, dq_ref, dkv_ref):
        @pl.when(pl.program_id(0) == 0)
        def _():
            dkv_ref[...] = jnp.zeros_like(dkv_ref)

        for h in range(4):
            sl = slice(h * 256, (h + 1) * 256)
            vs = slice(D + h * 256, D + (h + 1) * 256)
            qh = q_ref[:, sl].astype(BF16)
            kh = kv_ref[:, sl].astype(BF16)
            vh = kv_ref[:, vs].astype(BF16)
            doh = do_ref[:, sl].astype(BF16)
            p = _xattn_probs(qh, kh)
            dp = _nt(doh, vh)
            ds = (p * (dp - jnp.sum(p * dp, axis=-1, keepdims=True)) * 0.0625).astype(BF16)
            dq_ref[:, sl] = _nn(ds, kh).astype(dq_ref.dtype)
            dkv_ref[:, sl] += _tn(ds, qh)
            dkv_ref[:, vs] += _tn(p.astype(BF16), doh)

    return pl.pallas_call(
        body, name="xattn_bwd", grid=(S // TQ,),
        in_specs=[pl.BlockSpec((TQ, D), lambda i: (i, 0)), pl.BlockSpec((MEM, 2 * D), lambda i: (0, 0)),
                  pl.BlockSpec((TQ, D), lambda i: (i, 0))],
        out_specs=[pl.BlockSpec((TQ, D), lambda i: (i, 0)), pl.BlockSpec((MEM, 2 * D), lambda i: (0, 0))],
        out_shape=[jax.ShapeDtypeStruct((S, D), BF16), jax.ShapeDtypeStruct((MEM, 2 * D), F32)],
        compiler_params=_cparams(("arbitrary",)),
    )(q, kv, d_o)


def _s5_disc(a_re, a_im, log_dt, b_re, b_im):
    dt = jnp.exp(log_dt)[:, None]
    mag = jnp.exp(dt * a_re)
    abr = mag * jnp.cos(dt * a_im)
    abi = mag * jnp.sin(dt * a_im)
    nr, ni = abr - 1.0, abi
    inv = 1.0 / (a_re * a_re + a_im * a_im)
    cr = (nr * a_re + ni * a_im) * inv
    ci = (ni * a_re - nr * a_im) * inv
    bbr = cr[..., None] * b_re - ci[..., None] * b_im
    bbi = cr[..., None] * b_im + ci[..., None] * b_re
    return abr, abi, bbr, bbi


VM = pl.BlockSpec(memory_space=pltpu.VMEM)


def s5_embed(bt_re, bt_im, ct_re, ct_im):
    def body(br, bi, cr, ci, b_ref, c_ref):
        b_ref[...] = jnp.zeros_like(b_ref)
        c_ref[...] = jnp.zeros_like(c_ref)
        for g in range(NG):
            rows, cols = slice(g * NH, (g + 1) * NH), slice(g * NP, (g + 1) * NP)
            b_ref[rows, cols] = br[g]
            b_ref[rows, NS + g * NP:NS + (g + 1) * NP] = bi[g]
            c_ref[cols, rows] = cr[g]
            c_ref[NS + g * NP:NS + (g + 1) * NP, rows] = -ci[g]

    return pl.pallas_call(
        body, name="s5_embed", in_specs=[VM] * 4, out_specs=[VM] * 2,
        out_shape=[jax.ShapeDtypeStruct((NG * NH, 2 * NS), F32), jax.ShapeDtypeStruct((2 * NS, NG * NH), F32)],
        compiler_params=pltpu.CompilerParams(vmem_limit_bytes=VMEM_LIMIT),
    )(bt_re, bt_im, ct_re, ct_im)


def s5_extract(gb, gc):
    def body(gb_ref, gc_ref, br, bi, cr, ci):
        for g in range(NG):
            rows, cols = slice(g * NH, (g + 1) * NH), slice(g * NP, (g + 1) * NP)
            br[g] = gb_ref[rows, cols]
            bi[g] = gb_ref[rows, NS + g * NP:NS + (g + 1) * NP]
            cr[g] = gc_ref[cols, rows]
            ci[g] = -gc_ref[NS + g * NP:NS + (g + 1) * NP, rows]

    return pl.pallas_call(
        body, name="s5_extract", in_specs=[VM] * 2, out_specs=[VM] * 4,
        out_shape=[jax.ShapeDtypeStruct((NG, NH, NP), F32)] * 2 + [jax.ShapeDtypeStruct((NG, NP, NH), F32)] * 2,
        compiler_params=pltpu.CompilerParams(vmem_limit_bytes=VMEM_LIMIT),
    )(gb, gc)


HC, HS = NG * NH // 2, NS // 2
TS = 1024


def s5_to_states(x, w, mode, name, z_off=0):
    if mode == "nn":
        wb, wm = (HC, HS), lambda i, j, kk: (j % 2, j)
    else:
        wb, wm = (HS, HC), lambda i, j, kk: (j, j % 2)
    return mm_band(x, w, mode, name, (S // TS, 4, 1), ((TS, HC), wb, (TS, HS)),
                   (lambda i, j, kk: (i, z_off + j % 2), wm, lambda i, j, kk: (i, j)), (S, 2 * NS))


def s5_to_channels(x, w, mode, name, add=None):
    if mode == "nn":
        wb, wm = (HS, HC), lambda i, j, kk: (j + 2 * kk, j)
    else:
        wb, wm = (HC, HS), lambda i, j, kk: (j, j + 2 * kk)
    return mm_band(x, w, mode, name, (S // TS, 2, 2), ((TS, HS), wb, (TS, HC)),
                   (lambda i, j, kk: (i, j + 2 * kk), wm, lambda i, j, kk: (i, j)), (S, NG * NH), add=add)


def s5_outer(a, b, name, states_first, z_off=0):
    if states_first:
        return mm_band(a, b, "tn", name, (4, 1, 1), ((S, HS), (S, HC), (HS, HC)),
                       (lambda i, j, kk: (0, i), lambda i, j, kk: (0, i % 2), lambda i, j, kk: (i, i % 2)),
                       (2 * NS, NG * NH))
    return mm_band(a, b, "tn", name, (1, 4, 1), ((S, HC), (S, HS), (HC, HS)),
                   (lambda i, j, kk: (0, z_off + j % 2), lambda i, j, kk: (0, j), lambda i, j, kk: (j % 2, j)),
                   (NG * NH, 2 * NS))


def _fwd_even(i, x, P, W):
    hn = rms_fwd(x, P["norm_ab"][i:i + 1], "rms_ab_fwd")
    z = mm(m2(hn), W["w_in"], "nn", "in_ab")
    o, lse, cat = attn_fwd(z)
    if "more" in W:
        W.update(W.pop("more")(cat))
    cat = pool_fwd(z, W["pool_w"], P["pool_scale"][i:i + 1], cat)
    x_mid = mm(m2(cat), W["w_out"], "nn", "out_ab", add=m2(x))
    return x_mid, dict(x=x, hn=hn, z=z, o=o, lse=lse, cat=cat)


def _bwd_even(i, dx_mid, sv, P, W, G, GW):
    z = sv["z"]
    d_cat = mm(m2(dx_mid), W["w_out"], "nt", "out_ab_dx")
    GW["w_out"] = mm(m2(sv["cat"]), m2(dx_mid), "tn", "out_ab_dw").reshape(4, 512, D)
    dq, dk, dv, dga = attn_bwd(z, d_cat, sv["o"], sv["lse"])
    dvb, dgb, dpw, dps = pool_bwd(z, d_cat, W["pool_w"], P["pool_scale"][i:i + 1])
    GW["pool_w"] = dpw.reshape(4, 4, 64, 256).transpose(1, 0, 2, 3).reshape(4, 256, 256)
    G["pool_scale"][i] = dps[0]
    d_z = assemble_dz_even((dq, dk, dv, dga, dvb, dgb))
    d_hn = mm(m2(d_z), W["w_in"], "nt", "in_ab_dx")
    GW["w_in"] = mm(m2(sv["hn"]), m2(d_z), "tn", "in_ab_dw", out=outcs(D, 1536))
    return d_hn, P["norm_ab"][i:i + 1], "norm_ab", "rms_ab_bwd"


def _fwd_odd(i, x, P, W):
    hn = rms_fwd(x, P["norm_cd"][i:i + 1], "rms_cd_fwd")
    z = mm(m2(hn), W["w_in"], "nn", "in_cd")
    bfull = jnp.repeat(P["sgu_b"][i].T, 256, axis=1)
    c_out = sgu_fwd(z, P["sgu_ln_g"][i:i + 1], P["sgu_ln_b"][i:i + 1], P["sgu_w"][i], bfull)
    disc, disc_vjp = jax.vjp(_s5_disc, P["s5_a_re"][i], P["s5_a_im"][i], P["s5_log_dt"][i],
                             P["s5_b_re"][i], P["s5_b_im"][i])
    abr, abi, bbr, bbi = disc
    bbd, cbd = s5_embed(bbr.transpose(0, 2, 1), bbi.transpose(0, 2, 1),
                        P["s5_c_re"][i].transpose(0, 2, 1), P["s5_c_im"][i].transpose(0, 2, 1))
    abr, abi = abr.reshape(1, NS), abi.reshape(1, NS)
    bu = s5_to_states(z, bbd, "nn", "s5_bu", z_off=3072 // HC)
    h = scan_fwd(bu, abr, abi)
    hc = s5_to_channels(h, cbd, "nn", "s5_hc")
    dskip = P["s5_d"][i:i + 1]
    ypre, yg = s5_post(hc, z, dskip)
    if "more" in W:
        W.update(W.pop("more")(yg))
    w12 = W["w12"]
    t = mm(m2(yg), m2(w12), "nn", "glu_t")
    cat = glu_fwd(t, z, c_out)
    x_mid = mm(m2(cat), W["w_out"], "nn", "out_cd", add=m2(x))
    return x_mid, dict(x=x, hn=hn, z=z, bfull=bfull, disc_vjp=disc_vjp, bbd=bbd, cbd=cbd, abr=abr,
                       abi=abi, h=h, ypre=ypre, yg=yg, w12=w12, t=t, cat=cat, dskip=dskip)


def _bwd_odd(i, dx_mid, sv, P, W, G, GW):
    z = sv["z"]
    d_cat = mm(m2(dx_mid), W["w_out"], "nt", "out_cd_dx")
    GW["w_out"] = mm(m2(sv["cat"]), m2(dx_mid), "tn", "out_cd_dw").reshape(4, 384, D)
    du, dv, dgc, dws, dbs, dlg, dlb = sgu_bwd(z, d_cat, P["sgu_ln_g"][i:i + 1], P["sgu_ln_b"][i:i + 1],
                                               P["sgu_w"][i], sv["bfull"])
    G["sgu_w"][i], G["sgu_b"][i] = dws, dbs[:, :4].T
    G["sgu_ln_g"][i], G["sgu_ln_b"][i] = dlg[0], dlb[0]
    dt, dgd = glu_bwd(sv["t"], z, d_cat)
    gw12 = mm(m2(sv["yg"]), m2(dt), "tn", "glu_dw")
    GW["glu_w1"] = gw12[:, :512].reshape(4, 128, 512)
    GW["glu_w2"] = gw12[:, 512:].reshape(4, 128, 512)
    dyg = mm(m2(dt), m2(sv["w12"]), "nt", "glu_dx")
    dypre, dxd1, dd = s5_post_bwd(dyg, sv["ypre"], z, sv["dskip"])
    G["s5_d"][i] = dd[0]
    gcbd = s5_outer(sv["h"], dypre, "s5_dc", states_first=True)
    eta = s5_to_states(dypre, sv["cbd"], "nt", "s5_eta")
    lam, dacc = scan_bwd(eta, sv["h"], sv["abr"], sv["abi"])
    gbbd = s5_outer(z, lam, "s5_db", states_first=False, z_off=3072 // HC)
    dxd = s5_to_channels(lam, sv["bbd"], "nt", "s5_dx", add=dxd1)
    dacc = jnp.sum(dacc, axis=0)
    dbt_re, dbt_im, dct_re, dct_im = s5_extract(gbbd, gcbd)
    G["s5_c_re"][i], G["s5_c_im"][i] = dct_re.transpose(0, 2, 1), dct_im.transpose(0, 2, 1)
    d_bbr, d_bbi = dbt_re.transpose(0, 2, 1), dbt_im.transpose(0, 2, 1)
    (G["s5_a_re"][i], G["s5_a_im"][i], G["s5_log_dt"][i], G["s5_b_re"][i], G["s5_b_im"][i]) = sv["disc_vjp"](
        (dacc[:NS].reshape(NG, NP), dacc[NS:].reshape(NG, NP), d_bbr, d_bbi))
    d_z = assemble_dz_odd(du, dv, dgc, dxd, dgd)
    d_hn = mm(m2(d_z), W["w_in"], "nt", "in_cd_dx")
    GW["w_in"] = mm(m2(sv["hn"]), m2(d_z), "tn", "in_cd_dw", out=outcs(D, 1024))
    return d_hn, P["norm_cd"][i:i + 1], "norm_cd", "rms_cd_bwd"


def _fwd_x(l, x, mem_n, P, W):
    hx = rms_fwd(x, P["norm_x"][l:l + 1], "rms_x_fwd")
    q = mm(m2(hx), W["w_xq"], "nn", "xq", out_dtype=BF16)
    kv = mm(m2(mem_n), W["w_xkv"], "nn", "xkv", out_dtype=BF16)
    ox = xattn_fwd(q, kv)
    x_out = mm(m2(ox), W["w_xo"], "nn", "xo", add=m2(x))
    return x_out, dict(x=x, hx=hx, q=q, kv=kv, ox=ox)


def _bwd_x(l, dx_out, sv, mem_n, d_memn, P, W, G, GW):
    d_ox = mm(m2(dx_out), W["w_xo"], "nt", "xo_dx", out_dtype=BF16)
    GW["w_xo"] = mm(m2(sv["ox"]), m2(dx_out), "tn", "xo_dw").reshape(4, 256, D)
    dq, dkv = xattn_bwd(sv["q"], sv["kv"], d_ox)
    GW["w_xq"] = mm(m2(sv["hx"]), m2(dq), "tn", "xq_dw").reshape(4, 256, D)
    d_hx = mm(m2(dq), W["w_xq"], "nt", "xq_dx")
    GW["w_xkv"] = mm(m2(mem_n), m2(dkv), "tn", "xkv_dw", out=outcs(D, 512))
    d_memn = mm(m2(dkv), W["w_xkv"], "nt", "xkv_dx", add=None if d_memn is None else m2(d_memn))
    dx, dg = rms_bwd(sv["x"], d_hx, dx_out, P["norm_x"][l:l + 1], "rms_x_bwd")
    G["norm_x"][l] = dg[0]
    return dx, d_memn


SMALL_LAYERS = (("norm_ab", 2), ("pool_scale", 2), ("norm_cd", 2), ("sgu_ln_g", 2), ("sgu_ln_b", 2), ("sgu_w", 2),
                ("sgu_b", 2), ("s5_a_re", 2), ("s5_a_im", 2), ("s5_log_dt", 2), ("s5_b_re", 2), ("s5_b_im", 2),
                ("s5_c_re", 2), ("s5_c_im", 2), ("s5_d", 2), ("norm_x", 4))


def local_step(x, mem, tgt, P, weights_of, grads_done):
    G = {k: [None] * n for k, n in SMALL_LAYERS}
    mem_g = P["mem_norm"].reshape(1, D)
    mem_n = rms_fwd(mem, mem_g, "rms_mem_fwd")
    saved = []
    for layer in range(4):
        i = layer // 2
        W = weights_of(layer, x)
        x, sv_m = (_fwd_even if layer % 2 == 0 else _fwd_odd)(i, x, P, W)
        x, sv_x = _fwd_x(layer, x, mem_n, P, W)
        saved.append((sv_m, sv_x, W))
    dx, loss, dgf = final_loss(x, tgt, P["final_norm"].reshape(1, D))
    G["final_norm"] = dgf[0]
    d_memn = None
    for layer in reversed(range(4)):
        i = layer // 2
        sv_m, sv_x, W = saved[layer]
        GW = {}
        dx_mid, d_memn = _bwd_x(layer, dx, sv_x, mem_n, d_memn, P, W, G, GW)
        d_hn, g, key, name = (_bwd_even if layer % 2 == 0 else _bwd_odd)(i, dx_mid, sv_m, P, W, G, GW)
        token = grads_done(layer, GW)
        if token is not None:
            g = g + token
        dx, dg = rms_bwd(sv_m["x"], d_hn, dx_mid, g, name)
        G[key][i] = dg[0]
    _, dgm = rms_bwd(mem, d_memn, d_memn, mem_g, "rms_mem_bwd")
    G["mem_norm"] = dgm[0]
    return loss, dx, G


ANY = pl.BlockSpec(memory_space=pl.ANY)


def _place():
    x, y, c = lax.axis_index("x"), lax.axis_index("y"), lax.axis_index("c")
    chips = [(1 - x, y), (x, 1 - y), (1 - x, 1 - y)]
    return x, y, c, 2 * x + y, (x, y, 1 - c), chips


def _remote(src, dst, send, recv, k, dev):
    return pltpu.make_async_remote_copy(src_ref=src, dst_ref=dst, send_sem=send.at[k], recv_sem=recv.at[k],
                                        device_id=dev, device_id_type=MESHID)


HBM = pl.BlockSpec(memory_space=pltpu.HBM)
SEM = pl.BlockSpec(memory_space=pltpu.SEMAPHORE)
EFFECT = pltpu.SideEffectType.DATAFLOW_SIDE_EFFECTING


def _hbm(t):
    return pltpu.with_memory_space_constraint(t, pltpu.HBM)


def allgather_sync(shards):
    n = len(shards)

    def body(*refs):
        ins, outs = refs[:n], refs[n:2 * n]
        token, send, recv = refs[2 * n:]
        x, y, c, jme, sib, chips = _place()
        first, passed = [], []
        for a in range(n):
            cp = _remote(ins[a], outs[a].at[jme], send, recv, a * 7 + 6, sib)
            cp.start()
            first.append(cp)
            for k, chip in enumerate(chips):
                cp = _remote(ins[a].at[c], outs[a].at[jme, c], send, recv, a * 7 + k, (*chip, c))
                cp.start()
                first.append(cp)
        for a in range(n):
            for k, chip in enumerate(chips):
                piece = outs[a].at[2 * chip[0] + chip[1], c]
                _remote(piece, piece, send, recv, a * 7 + k, (*chip, c)).wait_recv()
                fw = _remote(piece, piece, send, recv, a * 7 + 3 + k, sib)
                fw.start()
                passed.append(fw)
        for a in range(n):
            own = outs[a].at[jme]
            _remote(own, own, send, recv, a * 7 + 6, sib).wait_recv()
            for k, chip in enumerate(chips):
                piece = outs[a].at[2 * chip[0] + chip[1], 1 - c]
                _remote(piece, piece, send, recv, a * 7 + 3 + k, sib).wait_recv()
        for cp in first + passed:
            cp.wait_send()
        token[...] = jnp.zeros_like(token)

    res = pl.pallas_call(
        body, name="allgather_sync", in_specs=[ANY] * n,
        out_specs=[ANY] * n + [pl.BlockSpec(memory_space=pltpu.VMEM)],
        out_shape=[jax.ShapeDtypeStruct((4,) + s.shape, s.dtype) for s in shards] + [jax.ShapeDtypeStruct((8, 128), F32)],
        scratch_shapes=[pltpu.SemaphoreType.DMA((7 * n,)), pltpu.SemaphoreType.DMA((7 * n,))],
    )(*shards)
    return list(res[:n]), res[n]


def _gather_copies(ins, lands, send, recv):
    x, y, c, jme, sib, chips = _place()
    devs = [(*chip, c) for chip in chips] + [sib]
    return [_remote(ins[a], lands[a].at[jme], send, recv, a * 4 + k, dev)
            for a in range(len(ins)) for k, dev in enumerate(devs)]


def allgather_start(shards, after, name):
    n, na = len(shards), len(after)

    def body(*refs):
        ins, lands = refs[:n], refs[n:2 * n]
        send, recv = refs[2 * n + na], refs[2 * n + na + 1]
        token = refs[-1]
        for cp in _gather_copies(ins, lands, send, recv):
            cp.start()
        token[...] = jnp.zeros_like(token)

    res = pl.pallas_call(
        body, name=name,
        out_shape=(pltpu.SemaphoreType.DMA((4 * n,)), pltpu.SemaphoreType.DMA((4 * n,)),
                   *[pltpu.HBM(s.shape, s.dtype) for s in shards],
                   *[pltpu.HBM((4,) + s.shape, s.dtype) for s in shards],
                   jax.ShapeDtypeStruct((8, 128), F32)),
        in_specs=[HBM] * (2 * n) + [ANY] * na,
        out_specs=(SEM, SEM, *[HBM] * (2 * n), pl.BlockSpec(memory_space=pltpu.VMEM)),
        input_output_aliases={a: 2 + a for a in range(2 * n)},
        compiler_params=pltpu.CompilerParams(has_side_effects=EFFECT),
    )(*[_hbm(s) for s in shards], *[_hbm(lax.empty((4,) + s.shape, s.dtype)) for s in shards], *after)
    return res[0], res[1], list(res[2:2 + n]), list(res[2 + n:2 + 2 * n]), res[-1]


def allgather_wait(send, recv, shards, lands, after, name):
    n = len(shards)

    def body(*refs):
        ins, zones = refs[:n], refs[n:2 * n]
        send_r, recv_r = refs[2 * n], refs[2 * n + 1]
        x, y, c, jme, sib, chips = _place()
        slots = [2 * chip[0] + chip[1] for chip in chips] + [jme]
        for a in range(n):
            for k, slot in enumerate(slots):
                cp = _remote(ins[a], zones[a].at[slot], send_r, recv_r, a * 4 + k, sib)
                cp.wait_send()
                cp.wait_recv()

    res = pl.pallas_call(
        body, name=name,
        out_shape=tuple(pltpu.HBM(t.shape, t.dtype) for t in list(shards) + list(lands)),
        in_specs=[HBM] * (2 * n) + [SEM, SEM, ANY], out_specs=tuple([HBM] * (2 * n)),
        input_output_aliases={a: a for a in range(2 * n)},
        compiler_params=pltpu.CompilerParams(has_side_effects=EFFECT),
    )(*shards, *lands, send, recv, after)
    return list(res[n:])


def allgather_small(slab):
    def body(in_ref, out_ref, send, recv, lsem):
        x, y, c, jme, sib, chips = _place()
        loc = pltpu.make_async_copy(in_ref, out_ref.at[jme], lsem.at[0])
        loc.start()
        cps = [_remote(in_ref, out_ref.at[jme], send, recv, k, (*chip, c)) for k, chip in enumerate(chips)]
        for cp in cps:
            cp.start()
        for k, chip in enumerate(chips):
            piece = out_ref.at[2 * chip[0] + chip[1]]
            _remote(piece, piece, send, recv, k, (*chip, c)).wait_recv()
        for cp in cps:
            cp.wait_send()
        loc.wait()

    return pl.pallas_call(
        body, name="allgather_small", in_specs=[ANY], out_specs=ANY,
        out_shape=jax.ShapeDtypeStruct((4,) + slab.shape, slab.dtype),
        scratch_shapes=[pltpu.SemaphoreType.DMA((3,)), pltpu.SemaphoreType.DMA((3,)), pltpu.SemaphoreType.DMA((1,))],
    )(slab)


def allreduce_small(v):
    def body(v_ref, o_ref, r0, r1, r2, send, recv):
        x, y, c, jme, sib, chips = _place()
        peers = [sib, (1 - x, y, c), (x, 1 - y, c)]
        o_ref[...] = v_ref[...]
        for k, buf in enumerate((r0, r1, r2)):
            cp = _remote(o_ref, buf, send, recv, k, peers[k])
            cp.start()
            cp.wait()
            o_ref[...] = o_ref[...] + buf[...]

    vm = pl.BlockSpec(memory_space=pltpu.VMEM)
    return pl.pallas_call(
        body, name="allreduce_small", in_specs=[vm], out_specs=vm,
        out_shape=jax.ShapeDtypeStruct(v.shape, v.dtype),
        scratch_shapes=[pltpu.VMEM(v.shape, v.dtype)] * 3 + [pltpu.SemaphoreType.DMA((3,)), pltpu.SemaphoreType.DMA((3,))],
        compiler_params=pltpu.CompilerParams(vmem_limit_bytes=VMEM_LIMIT),
    )(v)


def _pair_copies(gs, lands, send, recv):
    x, y, c, jme, sib, chips = _place()
    return [_remote(gs[a].at[:, 1 - c], lands[a], send, recv, a, sib) for a in range(len(gs))]


def rs_pair_start(gs, name):
    n = len(gs)

    def body(*refs):
        ins, lands = refs[:n], refs[n:2 * n]
        send, recv = refs[2 * n], refs[2 * n + 1]
        token = refs[-1]
        for cp in _pair_copies(ins, lands, send, recv):
            cp.start()
        token[...] = jnp.zeros_like(token)

    shapes = [(4,) + g.shape[2:] for g in gs]
    res = pl.pallas_call(
        body, name=name,
        out_shape=(pltpu.SemaphoreType.DMA((n,)), pltpu.SemaphoreType.DMA((n,)),
                   *[pltpu.HBM(g.shape, g.dtype) for g in gs], *[pltpu.HBM(s, F32) for s in shapes],
                   jax.ShapeDtypeStruct((8, 128), F32)),
        in_specs=[HBM] * (2 * n), out_specs=(SEM, SEM, *[HBM] * (2 * n), pl.BlockSpec(memory_space=pltpu.VMEM)),
        input_output_aliases={a: 2 + a for a in range(2 * n)},
        compiler_params=pltpu.CompilerParams(has_side_effects=EFFECT),
    )(*[_hbm(g) for g in gs], *[_hbm(lax.empty(s, F32)) for s in shapes])
    return res[0], res[1], list(res[2:2 + n]), list(res[2 + n:2 + 2 * n]), res[-1]


def rs_pair_wait(send, recv, gs, lands, after, name):
    n = len(gs)

    def body(*refs):
        ins, zones = refs[:n], refs[n:2 * n]
        for cp in _pair_copies(ins, zones, refs[2 * n], refs[2 * n + 1]):
            cp.wait_send()
            cp.wait_recv()

    res = pl.pallas_call(
        body, name=name,
        out_shape=tuple(pltpu.HBM(t.shape, t.dtype) for t in list(gs) + list(lands)),
        in_specs=[HBM] * (2 * n) + [SEM, SEM, ANY], out_specs=tuple([HBM] * (2 * n)),
        input_output_aliases={a: a for a in range(2 * n)},
        compiler_params=pltpu.CompilerParams(has_side_effects=EFFECT),
    )(*gs, *lands, send, recv, after)
    return list(res[:n]), list(res[n:])


SUM_ROWS = 256


def rs_pair_sum(g4s, gots, cidx):
    n = len(g4s)
    tiles = [(min(g.shape[2], SUM_ROWS), g.shape[3]) for g in g4s]
    nts = [g.shape[2] // tr for g, (tr, _) in zip(g4s, tiles)]

    def at(a, s):
        s = jnp.minimum(s, 4 * nts[a] - 1)
        return s // nts[a], s % nts[a]

    def body(c_ref, *refs):
        for a in range(n):
            refs[2 * n + a][...] = (refs[a][...] + refs[n + a][...]).astype(BF16)

    in_specs = [pl.BlockSpec((None, None) + tiles[a], lambda s, cr, a=a: (at(a, s)[0], cr[0], at(a, s)[1], 0))
                for a in range(n)]
    in_specs += [pl.BlockSpec((None,) + tiles[a], lambda s, cr, a=a: (*at(a, s), 0)) for a in range(n)]
    return pl.pallas_call(
        body, name="rs_pair_sum",
        grid_spec=pltpu.PrefetchScalarGridSpec(
            num_scalar_prefetch=1, grid=(4 * max(nts),), in_specs=in_specs,
            out_specs=[pl.BlockSpec((None,) + tiles[a], lambda s, cr, a=a: (*at(a, s), 0)) for a in range(n)]),
        out_shape=[jax.ShapeDtypeStruct((4,) + g.shape[2:], BF16) for g in g4s],
        compiler_params=_cparams(("arbitrary",)),
    )(cidx, *g4s, *gots)


def _chip_copies(ps, lands, send, recv):
    x, y, c, jme, sib, chips = _place()
    return [_remote(ps[a].at[2 * chip[0] + chip[1]], lands[a].at[jme], send, recv, a * 3 + k, (*chip, c))
            for a in range(len(ps)) for k, chip in enumerate(chips)]


def rs_chip_start(ps, name):
    n = len(ps)

    def body(*refs):
        ins, lands = refs[:n], refs[n:2 * n]
        send, recv = refs[2 * n], refs[2 * n + 1]
        token = refs[-1]
        for cp in _chip_copies(ins, lands, send, recv):
            cp.start()
        token[...] = jnp.zeros_like(token)

    res = pl.pallas_call(
        body, name=name,
        out_shape=(pltpu.SemaphoreType.DMA((3 * n,)), pltpu.SemaphoreType.DMA((3 * n,)),
                   *[pltpu.HBM(p.shape, p.dtype) for p in ps], *[pltpu.HBM(p.shape, p.dtype) for p in ps],
                   jax.ShapeDtypeStruct((8, 128), F32)),
        in_specs=[HBM] * (2 * n), out_specs=(SEM, SEM, *[HBM] * (2 * n), pl.BlockSpec(memory_space=pltpu.VMEM)),
        input_output_aliases={a: 2 + a for a in range(2 * n)},
        compiler_params=pltpu.CompilerParams(has_side_effects=EFFECT),
    )(*[_hbm(p) for p in ps], *[_hbm(lax.empty(p.shape, p.dtype)) for p in ps])
    return res[0], res[1], list(res[2:2 + n]), list(res[2 + n:2 + 2 * n]), res[-1]


def rs_chip_wait(send, recv, ps, lands, after, name):
    n = len(ps)

    def body(*refs):
        ins, zones = refs[:n], refs[n:2 * n]
        send_r, recv_r = refs[2 * n], refs[2 * n + 1]
        x, y, c, jme, sib, chips = _place()
        for a in range(n):
            for k, chip in enumerate(chips):
                jt = 2 * chip[0] + chip[1]
                cp = _remote(ins[a].at[jt], zones[a].at[jt], send_r, recv_r, a * 3 + k, (*chip, c))
                cp.wait_send()
                cp.wait_recv()

    res = pl.pallas_call(
        body, name=name,
        out_shape=tuple(pltpu.HBM(p.shape, p.dtype) for p in list(ps) + list(lands)),
        in_specs=[HBM] * (2 * n) + [SEM, SEM] + [ANY] * len(after), out_specs=tuple([HBM] * (2 * n)),
        input_output_aliases={a: a for a in range(2 * n)},
        compiler_params=pltpu.CompilerParams(has_side_effects=EFFECT),
    )(*ps, *lands, send, recv, *after)
    return list(res[n:])


def rs_chip_sum(qs, ps, ls, accs, layers, jc):
    n = len(qs)
    tiles = [(min(q.shape[1], SUM_ROWS), q.shape[2]) for q in qs]
    nts = [q.shape[1] // tr for q, (tr, _) in zip(qs, tiles)]

    def at(a, s):
        return jnp.minimum(s, nts[a] - 1)

    def body(jc_ref, *refs):
        jme = jc_ref[0]
        for a in range(n):
            q_ref, p_ref, o_ref = refs[a], refs[n + a], refs[len(refs) - n + a]
            own = p_ref[...].astype(F32)
            v = [jnp.where(jme == j, own, q_ref[j].astype(F32)) for j in range(4)]
            o_ref[...] = ((v[0] + v[1]) + v[2]) + v[3]

    in_specs = [pl.BlockSpec((4,) + tiles[a], lambda s, jr, a=a: (0, at(a, s), 0)) for a in range(n)]
    in_specs += [pl.BlockSpec((None,) + tiles[a], lambda s, jr, a=a: (jr[0], at(a, s), 0)) for a in range(n)]
    args, aliases = [jc, *qs, *ps], {}
    for a in range(n):
        if accs[a] is not None:
            aliases[len(args)] = a
            in_specs.append(ANY)
            args.append(accs[a])
    return pl.pallas_call(
        body, name="rs_chip_sum",
        grid_spec=pltpu.PrefetchScalarGridSpec(
            num_scalar_prefetch=1, grid=(max(nts),), in_specs=in_specs,
            out_specs=[pl.BlockSpec((None, None) + tiles[a], lambda s, jr, a=a: (ls[a], jr[1], at(a, s), 0))
                       for a in range(n)]),
        out_shape=[jax.ShapeDtypeStruct((layers[a], 2) + qs[a].shape[1:], F32) for a in range(n)],
        input_output_aliases=aliases,
        compiler_params=_cparams(("arbitrary",)),
    )(*args)


def rs_pair_gather(rs):
    n = len(rs)

    def body(*refs):
        outs = refs[n:2 * n]
        send, recv = refs[2 * n:]
        x, y, c, jme, sib, chips = _place()
        cps = [_remote(outs[a].at[:, c], outs[a].at[:, c], send, recv, a, sib) for a in range(n)]
        for cp in cps:
            cp.start()
        for a in range(n):
            slot = outs[a].at[:, 1 - c]
            _remote(slot, slot, send, recv, a, sib).wait_recv()
        for cp in cps:
            cp.wait_send()

    return pl.pallas_call(
        body, name="rs_pair_gather", in_specs=[ANY] * n, out_specs=[ANY] * n,
        out_shape=[jax.ShapeDtypeStruct(r.shape, r.dtype) for r in rs],
        input_output_aliases={a: a for a in range(n)},
        scratch_shapes=[pltpu.SemaphoreType.DMA((n,)), pltpu.SemaphoreType.DMA((n,))],
    )(*rs)


def _adamw_math(w, g, m, v):
    m = B1 * m + (1.0 - B1) * g
    v = B2 * v + (1.0 - B2) * (g * g)
    m_hat = m / (1.0 - B1 ** STEP)
    v_hat = v / (1.0 - B2 ** STEP)
    return -LR * (m_hat / (jnp.sqrt(v_hat) + AEPS) + WD * w), m, v


def adamw(w, g, m, v, name, with_grad=False):
    rows, cols = w.shape
    tr = 256 if rows % 256 == 0 else rows
    fn = (lambda wv, gv, mv, vv: (gv,) + _adamw_math(wv, gv, mv, vv)) if with_grad else _adamw_math
    return rw(fn, [(a, 0, cols) for a in (w, g, m, v)], [(cols, F32)] * (4 if with_grad else 3), name, rows, tr=tr)


def adamw_small(ws, gs, ms, vs):
    n = len(ws)

    def body(*refs):
        for a in range(n):
            res = _adamw_math(*[refs[k * n + a][...] for k in range(4)])
            for k in range(3):
                refs[(4 + k) * n + a][...] = res[k]

    res = pl.pallas_call(
        body, name="adamw_small", in_specs=[VM] * (4 * n), out_specs=[VM] * (3 * n),
        out_shape=[jax.ShapeDtypeStruct(w.shape, F32) for _ in range(3) for w in ws],
        compiler_params=pltpu.CompilerParams(vmem_limit_bytes=VMEM_LIMIT),
    )(*ws, *gs, *ms, *vs)
    return [(res[a], res[n + a], res[2 * n + a]) for a in range(n)]


WEIGHTS = ["norm_ab", "w_in_ab", "pool_w", "pool_scale", "w_out_ab", "norm_cd", "w_in_cd", "sgu_ln_g", "sgu_ln_b",
           "sgu_w", "sgu_b", "s5_a_re", "s5_a_im", "s5_log_dt", "s5_b_re", "s5_b_im", "s5_c_re", "s5_c_im", "s5_d",
           "glu_w1", "glu_w2", "w_out_cd", "norm_x", "w_xq", "w_xkv", "w_xo", "mem_norm", "final_norm"]
INPUTS = ["x", "mem"] + WEIGHTS + ["loss_target"] + ["m_" + n for n in WEIGHTS] + ["v_" + n for n in WEIGHTS]
BIG = ["w_in_ab", "w_out_ab", "w_in_cd", "w_out_cd", "w_xq", "w_xkv", "w_xo", "glu_w1", "glu_w2", "pool_w"]
COL_SHARDED = ("w_in_ab", "w_in_cd", "w_xkv")
SMALL = [n for n in WEIGHTS if n not in BIG]
SMALL_SHARDED = {"norm_cd": 256, "sgu_ln_g": 256, "sgu_ln_b": 256, "s5_d": 128}
PACK = 256 * 128


def _pack(arrs):
    flat = jnp.concatenate([a.reshape(-1) for a in arrs])
    pad = (-flat.shape[0]) % PACK
    return jnp.concatenate([flat, jnp.zeros((pad,), flat.dtype)]).reshape(-1, 128)


def _unpack(packed, shapes):
    flat, out, off = packed.reshape(-1), [], 0
    for s in shapes:
        n = 1
        for d in s:
            n *= d
        out.append(flat[off:off + n].reshape(s))
        off += n
    return out


LAYER_KEYS = (("w_in", "w_out", "pool_w", "w_xq", "w_xkv", "w_xo"),
              ("w_in", "w_out", "glu_w1", "glu_w2", "w_xq", "w_xkv", "w_xo"))


def _weight_of(key, layer):
    if key in ("w_xq", "w_xkv", "w_xo"):
        return key, layer, 4
    kind = "ab" if layer % 2 == 0 else "cd"
    return {"w_in": "w_in_" + kind, "w_out": "w_out_" + kind}.get(key, key), layer // 2, 2


def kernel(*args):
    a = dict(zip(INPUTS, args))
    x_i, y_i, c_i = lax.axis_index("x"), lax.axis_index("y"), lax.axis_index("c")
    j = 2 * x_i + y_i

    slab = jnp.concatenate([a["norm_cd"], a["sgu_ln_g"], a["sgu_ln_b"],
                            jnp.pad(a["s5_d"], ((0, 0), (0, 128)))], axis=0)
    gslab = allgather_small(slab)
    P = {n: a[n] for n in SMALL}
    for k, n in enumerate(("norm_cd", "sgu_ln_g", "sgu_ln_b", "s5_d")):
        wd = SMALL_SHARDED[n]
        P[n] = gslab[:, 2 * k:2 * k + 2, :wd].transpose(1, 0, 2).reshape(2, 4 * wd)

    def shards_of(layer):
        keys = sorted(k for k in LAYER_KEYS[layer % 2])
        out = []
        for k in keys:
            n, l, _ = _weight_of(k, layer)
            out.append(a[n][l].reshape(-1, a[n].shape[-1]).astype(BF16))
        return keys, out

    keys0, sh0 = shards_of(0)
    first = keys0.index("w_in")
    g_in, token = allgather_sync([sh0[first].reshape(2, sh0[first].shape[0] // 2, sh0[first].shape[1])])
    w_in0 = g_in[0].reshape(4, -1, g_in[0].shape[-1])
    started = {}
    for layer in (0, 1, 2, 3):
        keys, sh = (keys0, sh0) if layer == 0 else shards_of(layer)
        rest = [(k, s) for k, s in zip(keys, sh) if k != "w_in"]
        parts = [("in", ["w_in"], [sh[keys.index("w_in")]])] * (layer > 0) + [("", *map(list, zip(*rest)))]
        for tag, pk, ps in parts:
            send, recv, ps, lands, token = allgather_start(ps, [token, gslab], "allgather_start_%d%s" % (layer, tag))
            started[(layer, tag)] = (pk, send, recv, ps, lands)
    P["norm_ab"] = P["norm_ab"] + token[0:1, 0:1]

    cidx = jnp.reshape(c_i, (1,)).astype(jnp.int32)
    jc = jnp.stack([j, c_i]).astype(jnp.int32)

    def views(g):
        W = {}
        for k, v in g.items():
            if k in ("w_in", "w_xkv"):
                W[k] = mcs(v)
            elif k == "pool_w":
                W[k] = v.reshape(4, 4, 64, 256).transpose(1, 0, 2, 3).reshape(4, 256, 256)
            elif k not in ("glu_w1", "glu_w2"):
                W[k] = m2(v.reshape(-1, v.shape[-1]))
        if "glu_w1" in g:
            W["w12"] = jnp.concatenate([g["glu_w1"].reshape(512, 512), g["glu_w2"].reshape(512, 512)], axis=1)
        return W

    def arrived(layer, tag, after):
        keys, send, recv, sh, lands = started[(layer, tag)]
        return views(dict(zip(keys, allgather_wait(send, recv, sh, lands, after, "allgather_wait_%d%s" % (layer, tag)))))

    def weights_of(layer, x_in):
        W = views({"w_in": w_in0}) if layer == 0 else arrived(layer, "in", x_in)
        W["more"] = lambda after: arrived(layer, "", after)
        return W

    halves, pending = {}, {}

    def finish_pair(layer, after):
        keys, send, recv, flat, lands = halves.pop(layer)
        flat, got = rs_pair_wait(send, recv, flat, lands, after, "rs_pair_wait_%d" % layer)
        pair = rs_pair_sum(flat, got, cidx)
        send, recv, pair, lands, token = rs_chip_start(pair, "rs_chip_start_%d" % layer)
        pending[layer] = (keys, send, recv, pair, lands)
        return token

    def grads_done(layer, GW):
        keys = sorted(GW)
        flat = [GW[k].reshape(4, 2, GW[k].shape[1] // 2, GW[k].shape[2]) for k in keys]
        send, recv, flat, lands, token = rs_pair_start(flat, "rs_pair_start_%d" % layer)
        halves[layer] = (keys, send, recv, flat, lands)
        if layer + 1 in halves:
            token = token + finish_pair(layer + 1, token)
        return token[0:1, 0:1]

    loss, dx, G = local_step(a["x"][0], a["mem"][0], a["loss_target"][0], P, weights_of, grads_done)
    loss = lax.psum(loss[0, 0], ("x", "y", "c"))
    finish_pair(0, dx)
    outs = {}

    def update_big(names, red):
        for n, g in zip(names, rs_pair_gather([red[n] for n in names])):
            shp = a[n].shape
            g2 = g.reshape(-1, shp[-1])
            upd = adamw(a[n].reshape(g2.shape), g2, a["m_" + n].reshape(g2.shape), a["v_" + n].reshape(g2.shape),
                        "adamw_" + n, with_grad=True)
            outs[n] = tuple(t.reshape(shp) for t in upd)

    def reduce_layer(layer, red, after):
        keys, send, recv, pair, lands = pending[layer]
        lands = rs_chip_wait(send, recv, pair, lands, after, "rs_chip_wait_%d" % layer)
        which = [_weight_of(k, layer) for k in keys]
        sums = rs_chip_sum(lands, pair, [l for _, l, _ in which], [red.get(n) for n, _, _ in which],
                           [layers for _, _, layers in which], jc)
        red.update(zip([n for n, _, _ in which], sums))

    red = {}
    for layer in (3, 2, 1):
        reduce_layer(layer, red, [dx])
    odd_only = [n for n in BIG if n.endswith("_cd") or n.startswith("glu")]
    update_big(odd_only, red)

    gfull = [jnp.stack(G[n]) if isinstance(G[n], list) else G[n] for n in SMALL]
    shapes = [g.shape for g in gfull]
    gsum = _unpack(allreduce_small(_pack(gfull)), shapes)
    gloc = []
    for n, g in zip(SMALL, gsum):
        if n in SMALL_SHARDED:
            g = lax.dynamic_slice_in_dim(g, j * SMALL_SHARDED[n], SMALL_SHARDED[n], axis=1)
        gloc.append(g)
    two = [(-1, a[n].shape[-1]) if a[n].ndim > 1 else (1, a[n].shape[0]) for n in SMALL]
    upds = adamw_small(*[[t.reshape(s) for t, s in zip(ts, two)]
                         for ts in ([a[n] for n in SMALL], gloc, [a["m_" + n] for n in SMALL],
                                    [a["v_" + n] for n in SMALL])])
    for n, g, upd in zip(SMALL, gloc, upds):
        outs[n] = (g,) + tuple(t.reshape(a[n].shape) for t in upd)

    behind = [outs[n][1] for n in odd_only + SMALL[-1:]] + [red[n] for n in BIG if n not in odd_only]
    reduce_layer(0, red, behind)
    update_big([n for n in BIG if n not in odd_only], red)

    res = [loss, dx[None]]
    for part in range(4):
        res += [outs[n][part] for n in WEIGHTS]
    return tuple(res)
```

```python
import math

import jax
import jax.numpy as jnp
from jax import lax
from jax.experimental import pallas as pl
from jax.experimental.pallas import tpu as pltpu

F32, BF16 = jnp.float32, jnp.bfloat16
S, D = 2048, 1024
MEM = 256
EPS = 1e-6
NEG = -1e30
QB = 128
PATTERNS = (1, 4, 16)
NG, NP, NH = 32, 64, 16
NS = NG * NP
LR, B1, B2, AEPS, WD, STEP = 0.001, 0.9, 0.999, 1e-08, 0.01, 10
MESHID = pl.DeviceIdType.MESH
VMEM_LIMIT = 56 * 1024 * 1024


def _cparams(sem):
    return pltpu.CompilerParams(dimension_semantics=sem, vmem_limit_bytes=VMEM_LIMIT)


def _sig(x):
    return 1.0 / (1.0 + jnp.exp(-x))


def _dot(a, b, dims):
    return lax.dot_general(a, b, (dims, ((), ())), preferred_element_type=F32)


def _nn(a, b):
    return _dot(a, b, ((1,), (0,)))


def _nt(a, b):
    return _dot(a, b, ((1,), (1,)))


def _tn(a, b):
    return _dot(a, b, ((0,), (0,)))


_DIMS = {"nn": ((1,), (0,)), "nt": ((1,), (1,)), "tn": ((0,), (0,))}


def _tile(dim, cc=None, cap=1024):
    for t in (2048, 1536, 1024, 768, 512, 384, 256, 128):
        if t <= cap and dim % t == 0 and (cc is None or cc % t == 0):
            return t
    return dim


MM_VMEM = 36 * 1024 * 1024


def _mm_tiles(m, n, k, ccm, ccn, cck, a_bytes, b_bytes, o_bytes):
    caps = [1024, 1024, 2048]
    while True:
        tm, tn, tk = _tile(m, ccm, caps[0]), _tile(n, ccn, caps[1]), _tile(k, cck, caps[2])
        need = 2 * (tm * tk * a_bytes + tk * tn * b_bytes + tm * tn * o_bytes) + (tm * tn * 4 if tk < k else 0)
        if need <= MM_VMEM:
            return tm, tn, tk
        if tk > 1024:
            caps[2] = tk // 2
        elif tn >= tm:
            caps[1] = tn // 2
        else:
            caps[0] = tm // 2


def m2(arr, col_off=0, ncols=None):
    rows, cols = arr.shape
    ncols = cols - col_off if ncols is None else ncols

    def spec(tr, tc, rc):
        assert col_off % tc == 0
        return pl.BlockSpec((tr, tc), lambda *g: (rc(*g)[0], rc(*g)[1] + col_off // tc))
    return (arr, rows, ncols, spec, None if col_off == 0 else col_off)


def mcs(arr):
    cs = arr.shape[2]

    def spec(tr, tc, rc):
        n = cs // tc
        return pl.BlockSpec((None, tr, tc), lambda *g: (rc(*g)[1] // n, rc(*g)[0], rc(*g)[1] % n))
    return (arr, arr.shape[1], 4 * cs, spec, cs)


def out2(rows, cols):
    def spec(tr, tc, rc):
        return pl.BlockSpec((tr, tc), lambda *g: tuple(rc(*g)))
    return ((rows, cols), spec, None)


def outcs(rows, cs):
    def spec(tr, tc, rc):
        n = cs // tc
        return pl.BlockSpec((None, tr, tc), lambda *g: (rc(*g)[1] // n, rc(*g)[0], rc(*g)[1] % n))
    return ((4, rows, cs), spec, cs)


def _both(a, b):
    if a is None:
        return b
    if b is None:
        return a
    return math.gcd(a, b)


def mm(a, b, mode, name, add=None, out=None, out_dtype=F32):
    a_arr, a_r, a_c, a_spec, a_cc = a
    b_arr, b_r, b_c, b_spec, b_cc = b
    if mode == "nn":
        m, k, n = a_r, a_c, b_c
        assert b_r == k
        ccm, cck, ccn = None, a_cc, b_cc
    elif mode == "nt":
        m, k, n = a_r, a_c, b_r
        assert b_c == k
        ccm, cck, ccn = None, _both(a_cc, b_cc), None
    else:
        m, k, n = a_c, a_r, b_c
        assert b_r == k
        ccm, cck, ccn = a_cc, None, b_cc
    out = out2(m, n) if out is None else out
    o_shape, o_spec, o_cc = out
    ccn = _both(ccn, o_cc)
    if add is not None:
        ccn = _both(ccn, add[4])
    o_bytes = jnp.dtype(out_dtype).itemsize + (0 if add is None else add[0].dtype.itemsize)
    tm, tn, tk = _mm_tiles(m, n, k, ccm, ccn, cck, a_arr.dtype.itemsize, b_arr.dtype.itemsize, o_bytes)
    nk = k // tk
    if mode == "nn":
        in_specs = [a_spec(tm, tk, lambda i, j, kk: (i, kk)), b_spec(tk, tn, lambda i, j, kk: (kk, j))]
    elif mode == "nt":
        in_specs = [a_spec(tm, tk, lambda i, j, kk: (i, kk)), b_spec(tn, tk, lambda i, j, kk: (j, kk))]
    else:
        in_specs = [a_spec(tk, tm, lambda i, j, kk: (kk, i)), b_spec(tk, tn, lambda i, j, kk: (kk, j))]
    args = [a_arr, b_arr]
    if add is not None:
        in_specs.append(add[3](tm, tn, lambda i, j, kk: (i, j)))
        args.append(add[0])
    return _mm_call(args, in_specs, o_spec(tm, tn, lambda i, j, kk: (i, j)), jax.ShapeDtypeStruct(o_shape, out_dtype),
                    mode, (m // tm, n // tn, nk), (tm, tn), add is not None, name)


def _mm_call(args, in_specs, out_spec, out_shape, mode, grid, tile, has_add, name):
    dims = _DIMS[mode]
    nk = grid[2]
    tm, tn = tile

    def body(*refs):
        a_ref, b_ref = refs[0], refs[1]
        add_ref = refs[2] if has_add else None
        prod = _dot(a_ref[...].astype(BF16), b_ref[...].astype(BF16), dims)
        if nk == 1:
            o_ref = refs[-1]
            if has_add:
                prod = prod + add_ref[...].astype(F32)
            o_ref[...] = prod.astype(o_ref.dtype)
            return
        o_ref, acc = refs[-2], refs[-1]
        kk = pl.program_id(2)

        @pl.when(kk == 0)
        def _():
            acc[...] = prod

        @pl.when(kk > 0)
        def _():
            acc[...] += prod

        @pl.when(kk == nk - 1)
        def _():
            r = acc[...]
            if has_add:
                r = r + add_ref[...].astype(F32)
            o_ref[...] = r.astype(o_ref.dtype)

    return pl.pallas_call(
        body, name=name, grid=grid, in_specs=in_specs, out_specs=out_spec, out_shape=out_shape,
        scratch_shapes=[pltpu.VMEM((tm, tn), F32)] if nk > 1 else [],
        compiler_params=_cparams(("parallel", "parallel", "arbitrary")),
    )(*args)


def mm_band(a, b, mode, name, grid, blocks, maps, out_shape, add=None, out_dtype=F32):
    in_specs = [pl.BlockSpec(blocks[0], maps[0]), pl.BlockSpec(blocks[1], maps[1])]
    args = [a, b]
    if add is not None:
        in_specs.append(pl.BlockSpec(blocks[2], maps[2]))
        args.append(add)
    return _mm_call(args, in_specs, pl.BlockSpec(blocks[2], maps[2]), jax.ShapeDtypeStruct(out_shape, out_dtype),
                    mode, grid, blocks[2], add is not None, name)


def rw(fn, ins, outs, name, rows, tr=256, consts=(), accs=()):
    n_in, n_c, n_o, n_a = len(ins), len(consts), len(outs), len(accs)
    in_specs = []
    for arr, off, width in ins:
        assert off % width == 0
        in_specs.append(pl.BlockSpec((tr, width), lambda i, o=off // width: (i, o)))
    for c in consts:
        in_specs.append(pl.BlockSpec(c.shape, lambda i: (0, 0)))
    out_specs = [pl.BlockSpec((tr, w), lambda i: (i, 0)) for w, _ in outs]
    out_specs += [pl.BlockSpec(s, lambda i: (0, 0)) for s in accs]
    out_shape = [jax.ShapeDtypeStruct((rows, w), dt) for w, dt in outs]
    out_shape += [jax.ShapeDtypeStruct(s, F32) for s in accs]

    def body(*refs):
        vals = [r[...] for r in refs[:n_in + n_c]]
        o_refs = refs[n_in + n_c:n_in + n_c + n_o]
        a_refs = refs[n_in + n_c + n_o:]
        res = fn(*vals)
        for r, v in zip(o_refs, res[:n_o]):
            r[...] = v.astype(r.dtype)
        if n_a:
            @pl.when(pl.program_id(0) == 0)
            def _():
                for r in a_refs:
                    r[...] = jnp.zeros_like(r)
            for r, v in zip(a_refs, res[n_o:]):
                r[...] += v

    res = pl.pallas_call(
        body, name=name, grid=(rows // tr,), in_specs=in_specs, out_specs=out_specs,
        out_shape=out_shape,
        compiler_params=_cparams(("arbitrary",) if n_a else ("parallel",)),
    )(*[a for a, _, _ in ins], *consts)
    return res


def _rstd(x):
    return lax.rsqrt(jnp.mean(x * x, axis=-1, keepdims=True) + EPS)


def rms_fwd(x, g, name):
    def fn(xv, gv):
        xv = xv.astype(F32)
        return (xv * _rstd(xv) * gv,)
    return rw(fn, [(x, 0, D)], [(D, BF16)], name, x.shape[0], consts=[g])[0]


def _rms_bwd_math(xv, dy, gv):
    r = _rstd(xv)
    dyg = dy * gv
    dx = r * dyg - xv * (r * r * r / D) * jnp.sum(dyg * xv, axis=-1, keepdims=True)
    dg = jnp.sum(dy * xv * r, axis=0, keepdims=True)
    return dx, dg


def rms_bwd(x, dy, dres, g, name):
    def fn(xv, dyv, drv, gv):
        dx, dg = _rms_bwd_math(xv, dyv, gv)
        return dx + drv, dg
    return rw(fn, [(x, 0, D), (dy, 0, D), (dres, 0, D)], [(D, F32)], name, x.shape[0],
              consts=[g], accs=[(1, D)])


def final_loss(x, tgt, g):
    def fn(xv, tv, gv):
        e = xv * _rstd(xv) * gv - tv
        loss = 0.5 * jnp.sum(jnp.sum(e * e, axis=-1, keepdims=True), axis=0, keepdims=True) / D
        dx, dg = _rms_bwd_math(xv, e / D, gv)
        return dx, loss, dg
    return rw(fn, [(x, 0, D), (tgt, 0, D)], [(D, F32)], "final_loss", S, consts=[g],
              accs=[(1, 1), (1, D)])


def _attn_bias(bias_ref):
    ii = lax.broadcasted_iota(jnp.int32, (2 * QB, 2 * QB), 0) % QB
    jj = lax.broadcasted_iota(jnp.int32, (2 * QB, 2 * QB), 1)
    dist = ii + QB - jj
    band = (dist >= 0) & (dist <= QB)
    bias_ref[1] = jnp.where(band, 0.0, NEG)
    bias_ref[0] = jnp.where(band & (jj >= QB), 0.0, NEG)


def _two_heads(x, m0):
    return jnp.concatenate([jnp.where(m0, x, 0.0), jnp.where(m0, 0.0, x)], axis=0)


def _per_head(col, m0):
    return jnp.where(m0, col[:QB], col[QB:])


def _attn_rows(idx, d):
    if d == 1:
        b = idx
        cur = pl.ds(pl.multiple_of(b * QB, QB), QB)
        prev = pl.ds(pl.multiple_of(jnp.maximum(b - 1, 0) * QB, QB), QB)
    else:
        r, b = lax.rem(idx, d), lax.div(idx, d)
        cur = pl.ds(r + b * (QB * d), QB, stride=d)
        prev = pl.ds(r + jnp.maximum(b - 1, 0) * (QB * d), QB, stride=d)
    return cur, prev, b


NBLK = S // QB
GROUP = 16
GROUP_FWD = 16


def _colblk(off):
    return pl.BlockSpec((S, 128), lambda hp: (0, off * 8 + hp))


def attn_fwd(z):
    def body(q_ref, k_ref, v_ref, g_ref, o_ref, l_ref, a_ref, os, ls, bias):
        _attn_bias(bias)
        m0 = lax.broadcasted_iota(jnp.int32, (1, 128), 1) < 64
        for pi, d in enumerate(PATTERNS):
            lone = S // d == QB

            def load(idx, d=d, lone=lone):
                cur, prev, b = _attn_rows(idx, d)
                if lone:
                    return cur, (q_ref[cur, :], None, k_ref[cur, :], None, v_ref[cur, :], bias[1, :, QB:])
                return cur, (q_ref[cur, :], k_ref[prev, :], k_ref[cur, :], v_ref[prev, :], v_ref[cur, :],
                             bias[jnp.minimum(b, 1)])

            def block(q, kp, kc, vp, vc, bs):
                qq = _two_heads(q * 0.125, m0).astype(BF16)
                k = (kc if kp is None else jnp.concatenate([kp, kc], axis=0)).astype(BF16)
                s = _nt(qq, k) + bs
                mx = jnp.max(s, axis=-1, keepdims=True)
                p = jnp.exp(s - mx)
                den = jnp.sum(p, axis=-1, keepdims=True)
                pb = p.astype(BF16)
                vv = _two_heads(vc if vp is None else jnp.concatenate([vp, vc], axis=0), m0).astype(BF16)
                o = _nn(jnp.concatenate([pb[:QB], pb[QB:]], axis=1), vv)
                return o * _per_head(1.0 / den, m0), _per_head(mx + jnp.log(den), m0)

            def step(i, carry, pi=pi):
                loaded = [load(i * GROUP_FWD + u) for u in range(GROUP_FWD)]
                done = [block(*vals) for _, vals in loaded]
                for (cur, _), (o, l) in zip(loaded, done):
                    os[pi, cur, :] = o
                    ls[pi, cur, :] = l
                return carry
            lax.fori_loop(0, NBLK // GROUP_FWD, step, 0)
        l1, l2, l3 = ls[0], ls[1], ls[2]
        mx = jnp.maximum(jnp.maximum(l1, l2), l3)
        e1, e2, e3 = jnp.exp(l1 - mx), jnp.exp(l2 - mx), jnp.exp(l3 - mx)
        tot = e1 + e2 + e3
        o = (os[0] * e1 + os[1] * e2 + os[2] * e3) / tot
        ga = g_ref[...]
        o_ref[...] = o
        l_ref[...] = mx + jnp.log(tot)
        a_ref[...] = (o * (ga * _sig(ga))).astype(a_ref.dtype)

    out = pl.BlockSpec((S, 128), lambda hp: (0, hp))
    return pl.pallas_call(
        body, name="attn_fwd", grid=(8,),
        in_specs=[_colblk(0), _colblk(1), _colblk(2), _colblk(3)], out_specs=[out] * 3,
        out_shape=[jax.ShapeDtypeStruct((S, D), F32), jax.ShapeDtypeStruct((S, D), F32),
                   jax.ShapeDtypeStruct((S, 2 * D), BF16)],
        scratch_shapes=[pltpu.VMEM((3, S, 128), F32), pltpu.VMEM((3, S, 128), F32),
                        pltpu.VMEM((2, 2 * QB, 2 * QB), F32)],
        compiler_params=_cparams(("parallel",)),
    )(z, z, z, z)


def attn_bwd(z, d_cat, o, lse):
    def body(q_ref, k_ref, v_ref, g_ref, da_ref, o_ref, l_ref, dq_ref, dk_ref, dv_ref, dg_ref, do_s, pr_s, bias):
        _attn_bias(bias)
        m0 = lax.broadcasted_iota(jnp.int32, (1, 128), 1) < 64
        ga = g_ref[...]
        sg = _sig(ga)
        da = da_ref[...]
        ov = o_ref[...]
        do = da * (ga * sg)
        dg_ref[...] = da * ov * (sg * (1.0 + ga * (1.0 - sg)))
        do_s[...] = do
        pr_s[...] = do * ov
        dq_ref[...] = jnp.zeros_like(dq_ref)
        dk_ref[...] = jnp.zeros_like(dk_ref)
        dv_ref[...] = jnp.zeros_like(dv_ref)
        for d in PATTERNS:
            lone = S // d == QB

            def load(idx, d=d, lone=lone):
                cur, prev, b = _attn_rows(idx, d)
                if lone:
                    return (cur, None), (q_ref[cur, :], None, k_ref[cur, :], None, v_ref[cur, :],
                                         do_s[cur, :], pr_s[cur, :], l_ref[cur, :], bias[1, :, QB:])
                return (cur, prev), (q_ref[cur, :], k_ref[prev, :], k_ref[cur, :], v_ref[prev, :], v_ref[cur, :],
                                     do_s[cur, :], pr_s[cur, :], l_ref[cur, :], bias[jnp.minimum(b, 1)])

            def block(q, kp, kc, vp, vc, dof, prod, lp, bs):
                qq = _two_heads(q * 0.125, m0).astype(BF16)
                kf = kc if kp is None else jnp.concatenate([kp, kc], axis=0)
                k = kf.astype(BF16)
                v = (vc if vp is None else jnp.concatenate([vp, vc], axis=0)).astype(BF16)
                dd = _two_heads(dof, m0).astype(BF16)
                lh = jnp.max(jnp.concatenate([jnp.where(m0, lp, -jnp.inf), jnp.where(m0, -jnp.inf, lp)], axis=0),
                             axis=-1, keepdims=True)
                delta = jnp.sum(_two_heads(prod, m0), axis=-1, keepdims=True)
                p = jnp.exp(_nt(qq, k) + bs - lh)
                ds = (p * (_nt(dd, v) - delta)).astype(BF16)
                dq = _nn(jnp.concatenate([ds[:QB], ds[QB:]], axis=1), _two_heads(kf, m0).astype(BF16))
                return dq * 0.125, _tn(ds, qq), _tn(p.astype(BF16), dd)

            def step(i, carry):
                loaded = [load(i * GROUP + u) for u in range(GROUP)]
                done = [block(*vals) for _, vals in loaded]
                for ((cur, prev), _), (dq, dk, dv) in zip(loaded, done):
                    dq_ref[cur, :] = dq_ref[cur, :] + dq
                    if prev is not None:
                        dk_ref[prev, :] = dk_ref[prev, :] + dk[:QB]
                        dv_ref[prev, :] = dv_ref[prev, :] + dv[:QB]
                    dk_ref[cur, :] = dk_ref[cur, :] + dk[-QB:]
                    dv_ref[cur, :] = dv_ref[cur, :] + dv[-QB:]
                return carry
            lax.fori_loop(0, NBLK // GROUP, step, 0)

    blk = pl.BlockSpec((S, 128), lambda hp: (0, hp))
    return pl.pallas_call(
        body, name="attn_bwd", grid=(8,),
        in_specs=[_colblk(0), _colblk(1), _colblk(2), _colblk(3), blk, blk, blk], out_specs=[blk] * 4,
        out_shape=[jax.ShapeDtypeStruct((S, D), F32)] * 4,
        scratch_shapes=[pltpu.VMEM((S, 128), F32), pltpu.VMEM((S, 128), F32), pltpu.VMEM((2, 2 * QB, 2 * QB), F32)],
        compiler_params=_cparams(("parallel",)),
    )(z, z, z, z, d_cat, o, lse)


def assemble_dz_even(parts):
    def body(*refs):
        o_ref = refs[-1]
        for j in range(6):
            o_ref[:, j * D:(j + 1) * D] = refs[j][...].astype(o_ref.dtype)
    tr = 256
    blk = pl.BlockSpec((tr, D), lambda i: (i, 0))
    return pl.pallas_call(
        body, name="assemble_dz_even", grid=(S // tr,), in_specs=[blk] * 6,
        out_specs=pl.BlockSpec((tr, 6 * D), lambda i: (i, 0)),
        out_shape=jax.ShapeDtypeStruct((S, 6 * D), BF16),
        compiler_params=_cparams(("parallel",)),
    )(*parts)


def _pool_window(g):
    return jnp.where(g == 0, 2.0, jnp.where(g == 1, 4.0, jnp.where(g == 2, 8.0, 16.0)))


def _pool_sel(g, levels):
    return jnp.where(g == 0, levels[0], jnp.where(g == 1, levels[1], jnp.where(g == 2, levels[2], levels[3])))


def _pool_fwd_math(v, g):
    t = lax.broadcasted_iota(jnp.int32, (S, 1), 0)
    s = v
    levels = []
    for k in (1, 2, 4, 8):
        s = s + jnp.where(t >= k, pltpu.roll(s, k, 0), 0.0)
        levels.append(s)
    cnt = jnp.minimum((t + 1).astype(F32), _pool_window(g))
    return _pool_sel(g, levels) / cnt - v, cnt


def pool_fwd(z, pw, ps, cat):
    def body(v_ref, g_ref, pw_ref, ps_ref, cat_ref, o_ref):
        g = pl.program_id(0)
        pooled, _ = _pool_fwd_math(v_ref[...], g)
        mixed = _nn(pooled.astype(BF16), pw_ref[...].astype(BF16))
        gb = g_ref[...]
        o_ref[...] = (mixed * ps_ref[...] * (gb * _sig(gb))).astype(o_ref.dtype)

    return pl.pallas_call(
        body, name="pool_fwd", grid=(4,),
        in_specs=[pl.BlockSpec((S, 256), lambda g: (0, 16 + g)),
                  pl.BlockSpec((S, 256), lambda g: (0, 20 + g)),
                  pl.BlockSpec((None, 256, 256), lambda g: (g, 0, 0)),
                  pl.BlockSpec((1, 256), lambda g: (0, g)), pl.BlockSpec(memory_space=pl.ANY)],
        out_specs=pl.BlockSpec((S, 256), lambda g: (0, 4 + g)),
        out_shape=jax.ShapeDtypeStruct((S, 2 * D), BF16),
        input_output_aliases={4: 0},
        compiler_params=_cparams(("parallel",)),
    )(z, z, pw, ps, cat)


def pool_bwd(z, d_cat, pw, ps):
    def body(v_ref, g_ref, d_ref, pw_ref, ps_ref, dv_ref, dg_ref, dpw_ref, dps_ref):
        g = pl.program_id(0)
        v = v_ref[...]
        pooled, cnt = _pool_fwd_math(v, g)
        pwb = pw_ref[...].astype(BF16)
        pb = pooled.astype(BF16)
        mixed = _nn(pb, pwb)
        gb = g_ref[...]
        sg = _sig(gb)
        dout = d_ref[...]
        sc = ps_ref[...]
        dg_ref[...] = dout * mixed * sc * (sg * (1.0 + gb * (1.0 - sg)))
        dms = dout * (gb * sg)
        dps_ref[...] = jnp.sum(dms * mixed, axis=0, keepdims=True)
        dmx = (dms * sc).astype(BF16)
        dpw_ref[...] = _tn(pb, dmx)
        dpooled = _nt(dmx, pwb)
        t = lax.broadcasted_iota(jnp.int32, (S, 1), 0)
        s = dpooled / cnt
        levels = []
        for k in (1, 2, 4, 8):
            s = s + jnp.where(t < S - k, pltpu.roll(s, S - k, 0), 0.0)
            levels.append(s)
        dv_ref[...] = _pool_sel(g, levels) - dpooled

    return pl.pallas_call(
        body, name="pool_bwd", grid=(4,),
        in_specs=[pl.BlockSpec((S, 256), lambda g: (0, 16 + g)),
                  pl.BlockSpec((S, 256), lambda g: (0, 20 + g)),
                  pl.BlockSpec((S, 256), lambda g: (0, 4 + g)),
                  pl.BlockSpec((None, 256, 256), lambda g: (g, 0, 0)),
                  pl.BlockSpec((1, 256), lambda g: (0, g))],
        out_specs=[pl.BlockSpec((S, 256), lambda g: (0, g)),
                   pl.BlockSpec((S, 256), lambda g: (0, g)),
                   pl.BlockSpec((None, 256, 256), lambda g: (g, 0, 0)),
                   pl.BlockSpec((1, 256), lambda g: (0, g))],
        out_shape=[jax.ShapeDtypeStruct((S, D), F32), jax.ShapeDtypeStruct((S, D), F32),
                   jax.ShapeDtypeStruct((4, 256, 256), F32), jax.ShapeDtypeStruct((1, D), F32)],
        compiler_params=_cparams(("parallel",)),
    )(z, z, d_cat, pw, ps)


CH = 128


def _sgu_common(v, lng, lnb, w_ref):
    mu = jnp.mean(v, axis=-1, keepdims=True)
    vc = v - mu
    rs = lax.rsqrt(jnp.mean(vc * vc, axis=-1, keepdims=True) + EPS)
    xhat = vc * rs
    vn = (xhat * lng + lnb).astype(BF16)
    ri = lax.broadcasted_iota(jnp.int32, (CH, CH), 0)
    ci = lax.broadcasted_iota(jnp.int32, (CH, CH), 1)
    tril = ri >= ci
    ws = [jnp.where(tril, w_ref[g], 0.0).astype(BF16) for g in range(4)]
    return xhat, rs, vn, tril, ws


def _zspec(off):
    return pl.BlockSpec((CH, D), lambda c: (c, off))


def _full(shape):
    return pl.BlockSpec(shape, lambda c: (0,) * len(shape))


def sgu_fwd(z, lng, lnb, w, bfull):
    def body(u_ref, v_ref, g_ref, lng_ref, lnb_ref, w_ref, b_ref, o_ref):
        _, _, vn, _, ws = _sgu_common(v_ref[...], lng_ref[...], lnb_ref[...], w_ref)
        for g in range(4):
            sl = slice(g * 256, (g + 1) * 256)
            mixed = _nn(ws[g], vn[:, sl]) + b_ref[:, sl]
            gc = g_ref[:, sl]
            o_ref[:, sl] = (u_ref[:, sl] * mixed * (gc * _sig(gc))).astype(o_ref.dtype)

    return pl.pallas_call(
        body, name="sgu_fwd", grid=(S // CH,),
        in_specs=[_zspec(0), _zspec(1), _zspec(2), _full((1, D)), _full((1, D)),
                  _full((4, CH, CH)), _full((CH, D))],
        out_specs=pl.BlockSpec((CH, D), lambda c: (c, 0)),
        out_shape=jax.ShapeDtypeStruct((S, D), BF16),
        compiler_params=_cparams(("parallel",)),
    )(z, z, z, lng, lnb, w, bfull)


def sgu_bwd(z, d_cat, lng, lnb, w, bfull):
    def body(u_ref, v_ref, g_ref, d_ref, lng_ref, lnb_ref, w_ref, b_ref,
             du_ref, dv_ref, dg_ref, dw_ref, db_ref, dlg_ref, dlb_ref):
        @pl.when(pl.program_id(0) == 0)
        def _():
            dw_ref[...] = jnp.zeros_like(dw_ref)
            db_ref[...] = jnp.zeros_like(db_ref)
            dlg_ref[...] = jnp.zeros_like(dlg_ref)
            dlb_ref[...] = jnp.zeros_like(dlb_ref)

        lng = lng_ref[...]
        xhat, rs, vn, tril, ws = _sgu_common(v_ref[...], lng, lnb_ref[...], w_ref)
        lane = lax.broadcasted_iota(jnp.int32, (1, 128), 1)
        db = jnp.zeros((CH, 128), F32)
        dvn_parts = []
        for g in range(4):
            sl = slice(g * 256, (g + 1) * 256)
            mixed = _nn(ws[g], vn[:, sl]) + b_ref[:, sl]
            gc = g_ref[:, sl]
            sg = _sig(gc)
            u = u_ref[:, sl]
            dc = d_ref[:, sl]
            du_ref[:, sl] = dc * mixed * (gc * sg)
            dg_ref[:, sl] = dc * u * mixed * (sg * (1.0 + gc * (1.0 - sg)))
            dmx = dc * u * (gc * sg)
            db = db + jnp.where(lane == g, jnp.sum(dmx, axis=-1, keepdims=True), 0.0)
            dmb = dmx.astype(BF16)
            dw_ref[g] += jnp.where(tril, _nt(dmb, vn[:, sl]), 0.0)
            dvn_parts.append(_tn(ws[g], dmb))
        db_ref[...] += db
        dvn = jnp.concatenate(dvn_parts, axis=1)
        dlb_ref[...] += jnp.sum(dvn, axis=0, keepdims=True)
        dlg_ref[...] += jnp.sum(dvn * xhat, axis=0, keepdims=True)
        dxh = dvn * lng
        dv_ref[...] = rs * (dxh - jnp.mean(dxh, axis=-1, keepdims=True)
                            - xhat * jnp.mean(dxh * xhat, axis=-1, keepdims=True))

    row = pl.BlockSpec((CH, D), lambda c: (c, 0))
    return pl.pallas_call(
        body, name="sgu_bwd", grid=(S // CH,),
        in_specs=[_zspec(0), _zspec(1), _zspec(2), row, _full((1, D)), _full((1, D)),
                  _full((4, CH, CH)), _full((CH, D))],
        out_specs=[row, row, row, _full((4, CH, CH)), _full((CH, 128)), _full((1, D)), _full((1, D))],
        out_shape=[jax.ShapeDtypeStruct((S, D), F32)] * 3
        + [jax.ShapeDtypeStruct((4, CH, CH), F32), jax.ShapeDtypeStruct((CH, 128), F32),
           jax.ShapeDtypeStruct((1, D), F32), jax.ShapeDtypeStruct((1, D), F32)],
        compiler_params=_cparams(("arbitrary",)),
    )(z, z, z, d_cat, lng, lnb, w, bfull)


TB = 256


def _cmul(ar, ai, br, bi):
    return ar * br - ai * bi, ar * bi + ai * br


def _scan_consts(ar, ai, reverse):
    a2 = _cmul(ar, ai, ar, ai)
    a4 = _cmul(*a2, *a2)
    row = lax.broadcasted_iota(jnp.int32, (8, NS), 0)
    pr = jnp.zeros((8, NS), F32)
    pi = jnp.zeros((8, NS), F32)
    cr, ci = ar, ai
    for r in range(8):
        sel = row == (7 - r if reverse else r)
        pr = jnp.where(sel, cr, pr)
        pi = jnp.where(sel, ci, pi)
        cr, ci = _cmul(cr, ci, ar, ai)
    return ((ar, ai), a2, a4), (pr, pi), row


def scan_fwd(bu, abr, abi):
    def body(bu_ref, ar_ref, ai_ref, h_ref, car, cai):
        @pl.when(pl.program_id(0) == 0)
        def _():
            car[...] = jnp.zeros_like(car)
            cai[...] = jnp.zeros_like(cai)

        pows, (pr, pi), row = _scan_consts(ar_ref[...], ai_ref[...], False)

        def tile(t, carry):
            c_r, c_i = carry
            rows = pl.ds(pl.multiple_of(t * 8, 8), 8)
            xr = bu_ref[rows, 0:NS]
            xi = bu_ref[rows, NS:2 * NS]
            for k, (kr, ki) in zip((1, 2, 4), pows):
                sr = jnp.where(row >= k, pltpu.roll(xr, k, 0), 0.0)
                si = jnp.where(row >= k, pltpu.roll(xi, k, 0), 0.0)
                xr, xi = xr + kr * sr - ki * si, xi + kr * si + ki * sr
            xr, xi = xr + pr * c_r - pi * c_i, xi + pr * c_i + pi * c_r
            h_ref[rows, 0:NS] = xr
            h_ref[rows, NS:2 * NS] = xi
            return (jnp.broadcast_to(xr[7:8, :], (8, NS)), jnp.broadcast_to(xi[7:8, :], (8, NS)))

        c_r, c_i = lax.fori_loop(0, TB // 8, tile, (car[...], cai[...]))
        car[...] = c_r
        cai[...] = c_i

    return pl.pallas_call(
        body, name="s5_scan_fwd", grid=(S // TB,),
        in_specs=[pl.BlockSpec((TB, 2 * NS), lambda i: (i, 0)),
                  pl.BlockSpec((1, NS), lambda i: (0, 0)), pl.BlockSpec((1, NS), lambda i: (0, 0))],
        out_specs=pl.BlockSpec((TB, 2 * NS), lambda i: (i, 0)),
        out_shape=jax.ShapeDtypeStruct((S, 2 * NS), F32),
        scratch_shapes=[pltpu.VMEM((8, NS), F32), pltpu.VMEM((8, NS), F32)],
        compiler_params=_cparams(("arbitrary",)),
    )(bu, abr, abi)


def scan_bwd(eta, h, abr, abi):
    nt = S // TB

    def body(e_ref, h_ref, ar_ref, ai_ref, l_ref, da_ref, car, cai):
        @pl.when(pl.program_id(0) == 0)
        def _():
            car[...] = jnp.zeros_like(car)
            cai[...] = jnp.zeros_like(cai)
            da_ref[...] = jnp.zeros_like(da_ref)

        pows, (pr, pi), row = _scan_consts(ar_ref[...], -ai_ref[...], True)

        def tile(tt, carry):
            c_r, c_i, acr, aci = carry
            t = TB // 8 - 1 - tt
            rows = pl.ds(pl.multiple_of(t * 8, 8), 8)
            xr = e_ref[rows, 0:NS]
            xi = e_ref[rows, NS:2 * NS]
            for k, (kr, ki) in zip((1, 2, 4), pows):
                sr = jnp.where(row < 8 - k, pltpu.roll(xr, 8 - k, 0), 0.0)
                si = jnp.where(row < 8 - k, pltpu.roll(xi, 8 - k, 0), 0.0)
                xr, xi = xr + kr * sr - ki * si, xi + kr * si + ki * sr
            xr, xi = xr + pr * c_r - pi * c_i, xi + pr * c_i + pi * c_r
            l_ref[rows, 0:NS] = xr
            l_ref[rows, NS:2 * NS] = xi
            nr = jnp.where(row < 7, pltpu.roll(xr, 7, 0), c_r)
            ni = jnp.where(row < 7, pltpu.roll(xi, 7, 0), c_i)
            hr = h_ref[rows, 0:NS]
            hi = h_ref[rows, NS:2 * NS]
            acr = acr + hr * nr + hi * ni
            aci = aci + hr * ni - hi * nr
            return (jnp.broadcast_to(xr[0:1, :], (8, NS)), jnp.broadcast_to(xi[0:1, :], (8, NS)), acr, aci)

        zero = jnp.zeros((8, NS), F32)
        c_r, c_i, acr, aci = lax.fori_loop(0, TB // 8, tile, (car[...], cai[...], zero, zero))
        car[...] = c_r
        cai[...] = c_i
        da_ref[:, 0:NS] += acr
        da_ref[:, NS:2 * NS] += aci

    rev = pl.BlockSpec((TB, 2 * NS), lambda i: (nt - 1 - i, 0))
    return pl.pallas_call(
        body, name="s5_scan_bwd", grid=(nt,),
        in_specs=[rev, rev, pl.BlockSpec((1, NS), lambda i: (0, 0)), pl.BlockSpec((1, NS), lambda i: (0, 0))],
        out_specs=[rev, pl.BlockSpec((8, 2 * NS), lambda i: (0, 0))],
        out_shape=[jax.ShapeDtypeStruct((S, 2 * NS), F32), jax.ShapeDtypeStruct((8, 2 * NS), F32)],
        scratch_shapes=[pltpu.VMEM((8, NS), F32), pltpu.VMEM((8, NS), F32)],
        compiler_params=_cparams(("arbitrary",)),
    )(eta, h, abr, abi)


GC = 0.7978845608028654
GA = 0.044715


def s5_post(hc, z, dskip):
    def fn(hv, xd, dv):
        y = hv + dv * xd
        return y, 0.5 * y * (1.0 + jnp.tanh(GC * (y + GA * y * y * y)))
    return rw(fn, [(hc, 0, 512), (z, 3072, 512)], [(512, F32), (512, BF16)], "s5_post", S, consts=[dskip])


def s5_post_bwd(dyg, ypre, z, dskip):
    def fn(dy, y, xd, dv):
        th = jnp.tanh(GC * (y + GA * y * y * y))
        dg = 0.5 * (1.0 + th) + 0.5 * y * (1.0 - th * th) * GC * (1.0 + 3.0 * GA * y * y)
        dyp = dy * dg
        return dyp, dyp * dv, jnp.sum(dyp * xd, axis=0, keepdims=True)
    return rw(fn, [(dyg, 0, 512), (ypre, 0, 512), (z, 3072, 512)], [(512, BF16), (512, F32)],
              "s5_post_bwd", S, consts=[dskip], accs=[(1, 512)])


def glu_fwd(t, z, c_out):
    def fn(t1, t2, gd, co):
        return (jnp.concatenate([co, (t1 * _sig(t2) * (gd * _sig(gd))).astype(BF16)], axis=1),)
    return rw(fn, [(t, 0, 512), (t, 512, 512), (z, 3584, 512), (c_out, 0, D)], [(D + 512, BF16)], "glu_fwd", S)[0]


def glu_bwd(t, z, d_cat):
    def fn(t1, t2, gd, dd):
        s2, sg = _sig(t2), _sig(gd)
        sl = gd * sg
        return (jnp.concatenate([dd * s2 * sl, dd * t1 * s2 * (1.0 - s2) * sl], axis=1),
                dd * t1 * s2 * (sg * (1.0 + gd * (1.0 - sg))))
    return rw(fn, [(t, 0, 512), (t, 512, 512), (z, 3584, 512), (d_cat, 1024, 512)],
              [(D, BF16), (512, F32)], "glu_bwd", S)


def assemble_dz_odd(du, dv, dgc, dxd, dgd):
    def body(a, b, c, d, e, o_ref):
        o_ref[:, 0:D] = a[...].astype(BF16)
        o_ref[:, D:2 * D] = b[...].astype(BF16)
        o_ref[:, 2 * D:3 * D] = c[...].astype(BF16)
        o_ref[:, 3 * D:3 * D + 512] = d[...].astype(BF16)
        o_ref[:, 3 * D + 512:4 * D] = e[...].astype(BF16)
    tr = 256
    blk = pl.BlockSpec((tr, D), lambda i: (i, 0))
    half = pl.BlockSpec((tr, 512), lambda i: (i, 0))
    return pl.pallas_call(
        body, name="assemble_dz_odd", grid=(S // tr,), in_specs=[blk, blk, blk, half, half],
        out_specs=pl.BlockSpec((tr, 4 * D), lambda i: (i, 0)),
        out_shape=jax.ShapeDtypeStruct((S, 4 * D), BF16),
        compiler_params=_cparams(("parallel",)),
    )(du, dv, dgc, dxd, dgd)


TQ = 256


def _xattn_probs(qh, kh):
    s = _nt(qh, kh) * 0.0625
    p = jnp.exp(s - jnp.max(s, axis=-1, keepdims=True))
    return p / jnp.sum(p, axis=-1, keepdims=True)


def xattn_fwd(q, kv):
    def body(q_ref, kv_ref, o_ref):
        for h in range(4):
            sl = slice(h * 256, (h + 1) * 256)
            p = _xattn_probs(q_ref[:, sl].astype(BF16), kv_ref[:, sl].astype(BF16))
            vh = kv_ref[:, D + h * 256:D + (h + 1) * 256].astype(BF16)
            o_ref[:, sl] = _nn(p.astype(BF16), vh).astype(o_ref.dtype)

    return pl.pallas_call(
        body, name="xattn_fwd", grid=(S // TQ,),
        in_specs=[pl.BlockSpec((TQ, D), lambda i: (i, 0)), pl.BlockSpec((MEM, 2 * D), lambda i: (0, 0))],
        out_specs=pl.BlockSpec((TQ, D), lambda i: (i, 0)),
        out_shape=jax.ShapeDtypeStruct((S, D), BF16),
        compiler_params=_cparams(("parallel",)),
    )(q, kv)


def xattn_bwd(q, kv, d_o):
    def body(q_ref, kv_ref, do_ref, dq_ref, dkv_ref):
        @pl.when(pl.program_id(0) == 0)
        def _():
            dkv_ref[...] = jnp.zeros_like(dkv_ref)

        for h in range(4):
            sl = slice(h * 256, (h + 1) * 256)
            vs = slice(D + h * 256, D + (h + 1) * 256)
            qh = q_ref[:, sl].astype(BF16)
            kh = kv_ref[:, sl].astype(BF16)
            vh = kv_ref[:, vs].astype(BF16)
            doh = do_ref[:, sl].astype(BF16)
            p = _xattn_probs(qh, kh)
            dp = _nt(doh, vh)
            ds = (p * (dp - jnp.sum(p * dp, axis=-1, keepdims=True)) * 0.0625).astype(BF16)
            dq_ref[:, sl] = _nn(ds, kh).astype(dq_ref.dtype)
            dkv_ref[:, sl] += _tn(ds, qh)
            dkv_ref[:, vs] += _tn(p.astype(BF16), doh)

    return pl.pallas_call(
        body, name="xattn_bwd", grid=(S // TQ,),
        in_specs=[pl.BlockSpec((TQ, D), lambda i: (i, 0)), pl.BlockSpec((MEM, 2 * D), lambda i: (0, 0)),
                  pl.BlockSpec((TQ, D), lambda i: (i, 0))],
        out_specs=[pl.BlockSpec((TQ, D), lambda i: (i, 0)), pl.BlockSpec((MEM, 2 * D), lambda i: (0, 0))],
        out_shape=[jax.ShapeDtypeStruct((S, D), BF16), jax.ShapeDtypeStruct((MEM, 2 * D), F32)],
        compiler_params=_cparams(("arbitrary",)),
    )(q, kv, d_o)


def _s5_disc(a_re, a_im, log_dt, b_re, b_im):
    dt = jnp.exp(log_dt)[:, None]
    mag = jnp.exp(dt * a_re)
    abr = mag * jnp.cos(dt * a_im)
    abi = mag * jnp.sin(dt * a_im)
    nr, ni = abr - 1.0, abi
    inv = 1.0 / (a_re * a_re + a_im * a_im)
    cr = (nr * a_re + ni * a_im) * inv
    ci = (ni * a_re - nr * a_im) * inv
    bbr = cr[..., None] * b_re - ci[..., None] * b_im
    bbi = cr[..., None] * b_im + ci[..., None] * b_re
    return abr, abi, bbr, bbi


VM = pl.BlockSpec(memory_space=pltpu.VMEM)


def s5_embed(bt_re, bt_im, ct_re, ct_im):
    def body(br, bi, cr, ci, b_ref, c_ref):
        b_ref[...] = jnp.zeros_like(b_ref)
        c_ref[...] = jnp.zeros_like(c_ref)
        for g in range(NG):
            rows, cols = slice(g * NH, (g + 1) * NH), slice(g * NP, (g + 1) * NP)
            b_ref[rows, cols] = br[g]
            b_ref[rows, NS + g * NP:NS + (g + 1) * NP] = bi[g]
            c_ref[cols, rows] = cr[g]
            c_ref[NS + g * NP:NS + (g + 1) * NP, rows] = -ci[g]

    return pl.pallas_call(
        body, name="s5_embed", in_specs=[VM] * 4, out_specs=[VM] * 2,
        out_shape=[jax.ShapeDtypeStruct((NG * NH, 2 * NS), F32), jax.ShapeDtypeStruct((2 * NS, NG * NH), F32)],
        compiler_params=pltpu.CompilerParams(vmem_limit_bytes=VMEM_LIMIT),
    )(bt_re, bt_im, ct_re, ct_im)


def s5_extract(gb, gc):
    def body(gb_ref, gc_ref, br, bi, cr, ci):
        for g in range(NG):
            rows, cols = slice(g * NH, (g + 1) * NH), slice(g * NP, (g + 1) * NP)
            br[g] = gb_ref[rows, cols]
            bi[g] = gb_ref[rows, NS + g * NP:NS + (g + 1) * NP]
            cr[g] = gc_ref[cols, rows]
            ci[g] = -gc_ref[NS + g * NP:NS + (g + 1) * NP, rows]

    return pl.pallas_call(
        body, name="s5_extract", in_specs=[VM] * 2, out_specs=[VM] * 4,
        out_shape=[jax.ShapeDtypeStruct((NG, NH, NP), F32)] * 2 + [jax.ShapeDtypeStruct((NG, NP, NH), F32)] * 2,
        compiler_params=pltpu.CompilerParams(vmem_limit_bytes=VMEM_LIMIT),
    )(gb, gc)


HC, HS = NG * NH // 2, NS // 2
TS = 1024


def s5_to_states(x, w, mode, name, z_off=0):
    if mode == "nn":
        wb, wm = (HC, HS), lambda i, j, kk: (j % 2, j)
    else:
        wb, wm = (HS, HC), lambda i, j, kk: (j, j % 2)
    return mm_band(x, w, mode, name, (S // TS, 4, 1), ((TS, HC), wb, (TS, HS)),
                   (lambda i, j, kk: (i, z_off + j % 2), wm, lambda i, j, kk: (i, j)), (S, 2 * NS))


def s5_to_channels(x, w, mode, name, add=None):
    if mode == "nn":
        wb, wm = (HS, HC), lambda i, j, kk: (j + 2 * kk, j)
    else:
        wb, wm = (HC, HS), lambda i, j, kk: (j, j + 2 * kk)
    return mm_band(x, w, mode, name, (S // TS, 2, 2), ((TS, HS), wb, (TS, HC)),
                   (lambda i, j, kk: (i, j + 2 * kk), wm, lambda i, j, kk: (i, j)), (S, NG * NH), add=add)


def s5_outer(a, b, name, states_first, z_off=0):
    if states_first:
        return mm_band(a, b, "tn", name, (4, 1, 1), ((S, HS), (S, HC), (HS, HC)),
                       (lambda i, j, kk: (0, i), lambda i, j, kk: (0, i % 2), lambda i, j, kk: (i, i % 2)),
                       (2 * NS, NG * NH))
    return mm_band(a, b, "tn", name, (1, 4, 1), ((S, HC), (S, HS), (HC, HS)),
                   (lambda i, j, kk: (0, z_off + j % 2), lambda i, j, kk: (0, j), lambda i, j, kk: (j % 2, j)),
                   (NG * NH, 2 * NS))


def _fwd_even(i, x, P, W):
    hn = rms_fwd(x, P["norm_ab"][i:i + 1], "rms_ab_fwd")
    z = mm(m2(hn), W["w_in"], "nn", "in_ab")
    o, lse, cat = attn_fwd(z)
    if "more" in W:
        W.update(W.pop("more")(cat))
    cat = pool_fwd(z, W["pool_w"], P["pool_scale"][i:i + 1], cat)
    x_mid = mm(m2(cat), W["w_out"], "nn", "out_ab", add=m2(x))
    return x_mid, dict(x=x, hn=hn, z=z, o=o, lse=lse, cat=cat)


def _bwd_even(i, dx_mid, sv, P, W, G, GW):
    z = sv["z"]
    d_cat = mm(m2(dx_mid), W["w_out"], "nt", "out_ab_dx")
    GW["w_out"] = mm(m2(sv["cat"]), m2(dx_mid), "tn", "out_ab_dw").reshape(4, 512, D)
    dq, dk, dv, dga = attn_bwd(z, d_cat, sv["o"], sv["lse"])
    dvb, dgb, dpw, dps = pool_bwd(z, d_cat, W["pool_w"], P["pool_scale"][i:i + 1])
    GW["pool_w"] = dpw.reshape(4, 4, 64, 256).transpose(1, 0, 2, 3).reshape(4, 256, 256)
    G["pool_scale"][i] = dps[0]
    d_z = assemble_dz_even((dq, dk, dv, dga, dvb, dgb))
    d_hn = mm(m2(d_z), W["w_in"], "nt", "in_ab_dx")
    GW["w_in"] = mm(m2(sv["hn"]), m2(d_z), "tn", "in_ab_dw", out=outcs(D, 1536))
    return d_hn, P["norm_ab"][i:i + 1], "norm_ab", "rms_ab_bwd"


def _fwd_odd(i, x, P, W):
    hn = rms_fwd(x, P["norm_cd"][i:i + 1], "rms_cd_fwd")
    z = mm(m2(hn), W["w_in"], "nn", "in_cd")
    bfull = jnp.repeat(P["sgu_b"][i].T, 256, axis=1)
    c_out = sgu_fwd(z, P["sgu_ln_g"][i:i + 1], P["sgu_ln_b"][i:i + 1], P["sgu_w"][i], bfull)
    disc, disc_vjp = jax.vjp(_s5_disc, P["s5_a_re"][i], P["s5_a_im"][i], P["s5_log_dt"][i],
                             P["s5_b_re"][i], P["s5_b_im"][i])
    abr, abi, bbr, bbi = disc
    bbd, cbd = s5_embed(bbr.transpose(0, 2, 1), bbi.transpose(0, 2, 1),
                        P["s5_c_re"][i].transpose(0, 2, 1), P["s5_c_im"][i].transpose(0, 2, 1))
    abr, abi = abr.reshape(1, NS), abi.reshape(1, NS)
    bu = s5_to_states(z, bbd, "nn", "s5_bu", z_off=3072 // HC)
    h = scan_fwd(bu, abr, abi)
    hc = s5_to_channels(h, cbd, "nn", "s5_hc")
    dskip = P["s5_d"][i:i + 1]
    ypre, yg = s5_post(hc, z, dskip)
    if "more" in W:
        W.update(W.pop("more")(yg))
    w12 = W["w12"]
    t = mm(m2(yg), m2(w12), "nn", "glu_t")
    cat = glu_fwd(t, z, c_out)
    x_mid = mm(m2(cat), W["w_out"], "nn", "out_cd", add=m2(x))
    return x_mid, dict(x=x, hn=hn, z=z, bfull=bfull, disc_vjp=disc_vjp, bbd=bbd, cbd=cbd, abr=abr,
                       abi=abi, h=h, ypre=ypre, yg=yg, w12=w12, t=t, cat=cat, dskip=dskip)


def _bwd_odd(i, dx_mid, sv, P, W, G, GW):
    z = sv["z"]
    d_cat = mm(m2(dx_mid), W["w_out"], "nt", "out_cd_dx")
    GW["w_out"] = mm(m2(sv["cat"]), m2(dx_mid), "tn", "out_cd_dw").reshape(4, 384, D)
    du, dv, dgc, dws, dbs, dlg, dlb = sgu_bwd(z, d_cat, P["sgu_ln_g"][i:i + 1], P["sgu_ln_b"][i:i + 1],
                                               P["sgu_w"][i], sv["bfull"])
    G["sgu_w"][i], G["sgu_b"][i] = dws, dbs[:, :4].T
    G["sgu_ln_g"][i], G["sgu_ln_b"][i] = dlg[0], dlb[0]
    dt, dgd = glu_bwd(sv["t"], z, d_cat)
    gw12 = mm(m2(sv["yg"]), m2(dt), "tn", "glu_dw")
    GW["glu_w1"] = gw12[:, :512].reshape(4, 128, 512)
    GW["glu_w2"] = gw12[:, 512:].reshape(4, 128, 512)
    dyg = mm(m2(dt), m2(sv["w12"]), "nt", "glu_dx")
    dypre, dxd1, dd = s5_post_bwd(dyg, sv["ypre"], z, sv["dskip"])
    G["s5_d"][i] = dd[0]
    gcbd = s5_outer(sv["h"], dypre, "s5_dc", states_first=True)
    eta = s5_to_states(dypre, sv["cbd"], "nt", "s5_eta")
    lam, dacc = scan_bwd(eta, sv["h"], sv["abr"], sv["abi"])
    gbbd = s5_outer(z, lam, "s5_db", states_first=False, z_off=3072 // HC)
    dxd = s5_to_channels(lam, sv["bbd"], "nt", "s5_dx", add=dxd1)
    dacc = jnp.sum(dacc, axis=0)
    dbt_re, dbt_im, dct_re, dct_im = s5_extract(gbbd, gcbd)
    G["s5_c_re"][i], G["s5_c_im"][i] = dct_re.transpose(0, 2, 1), dct_im.transpose(0, 2, 1)
    d_bbr, d_bbi = dbt_re.transpose(0, 2, 1), dbt_im.transpose(0, 2, 1)
    (G["s5_a_re"][i], G["s5_a_im"][i], G["s5_log_dt"][i], G["s5_b_re"][i], G["s5_b_im"][i]) = sv["disc_vjp"](
        (dacc[:NS].reshape(NG, NP), dacc[NS:].reshape(NG, NP), d_bbr, d_bbi))
    d_z = assemble_dz_odd(du, dv, dgc, dxd, dgd)
    d_hn = mm(m2(d_z), W["w_in"], "nt", "in_cd_dx")
    GW["w_in"] = mm(m2(sv["hn"]), m2(d_z), "tn", "in_cd_dw", out=outcs(D, 1024))
    return d_hn, P["norm_cd"][i:i + 1], "norm_cd", "rms_cd_bwd"


def _fwd_x(l, x, mem_n, P, W):
    hx = rms_fwd(x, P["norm_x"][l:l + 1], "rms_x_fwd")
    q = mm(m2(hx), W["w_xq"], "nn", "xq", out_dtype=BF16)
    kv = mm(m2(mem_n), W["w_xkv"], "nn", "xkv", out_dtype=BF16)
    ox = xattn_fwd(q, kv)
    x_out = mm(m2(ox), W["w_xo"], "nn", "xo", add=m2(x))
    return x_out, dict(x=x, hx=hx, q=q, kv=kv, ox=ox)


def _bwd_x(l, dx_out, sv, mem_n, d_memn, P, W, G, GW):
    d_ox = mm(m2(dx_out), W["w_xo"], "nt", "xo_dx", out_dtype=BF16)
    GW["w_xo"] = mm(m2(sv["ox"]), m2(dx_out), "tn", "xo_dw").reshape(4, 256, D)
    dq, dkv = xattn_bwd(sv["q"], sv["kv"], d_ox)
    GW["w_xq"] = mm(m2(sv["hx"]), m2(dq), "tn", "xq_dw").reshape(4, 256, D)
    d_hx = mm(m2(dq), W["w_xq"], "nt", "xq_dx")
    GW["w_xkv"] = mm(m2(mem_n), m2(dkv), "tn", "xkv_dw", out=outcs(D, 512))
    d_memn = mm(m2(dkv), W["w_xkv"], "nt", "xkv_dx", add=None if d_memn is None else m2(d_memn))
    dx, dg = rms_bwd(sv["x"], d_hx, dx_out, P["norm_x"][l:l + 1], "rms_x_bwd")
    G["norm_x"][l] = dg[0]
    return dx, d_memn


SMALL_LAYERS = (("norm_ab", 2), ("pool_scale", 2), ("norm_cd", 2), ("sgu_ln_g", 2), ("sgu_ln_b", 2), ("sgu_w", 2),
                ("sgu_b", 2), ("s5_a_re", 2), ("s5_a_im", 2), ("s5_log_dt", 2), ("s5_b_re", 2), ("s5_b_im", 2),
                ("s5_c_re", 2), ("s5_c_im", 2), ("s5_d", 2), ("norm_x", 4))


def local_step(x, mem, tgt, P, weights_of, grads_done):
    G = {k: [None] * n for k, n in SMALL_LAYERS}
    mem_g = P["mem_norm"].reshape(1, D)
    mem_n = rms_fwd(mem, mem_g, "rms_mem_fwd")
    saved = []
    for layer in range(4):
        i = layer // 2
        W = weights_of(layer, x)
        x, sv_m = (_fwd_even if layer % 2 == 0 else _fwd_odd)(i, x, P, W)
        x, sv_x = _fwd_x(layer, x, mem_n, P, W)
        saved.append((sv_m, sv_x, W))
    dx, loss, dgf = final_loss(x, tgt, P["final_norm"].reshape(1, D))
    G["final_norm"] = dgf[0]
    d_memn = None
    for layer in reversed(range(4)):
        i = layer // 2
        sv_m, sv_x, W = saved[layer]
        GW = {}
        dx_mid, d_memn = _bwd_x(layer, dx, sv_x, mem_n, d_memn, P, W, G, GW)
        d_hn, g, key, name = (_bwd_even if layer % 2 == 0 else _bwd_odd)(i, dx_mid, sv_m, P, W, G, GW)
        token = grads_done(layer, GW)
        if token is not None:
            g = g + token
        dx, dg = rms_bwd(sv_m["x"], d_hn, dx_mid, g, name)
        G[key][i] = dg[0]
    _, dgm = rms_bwd(mem, d_memn, d_memn, mem_g, "rms_mem_bwd")
    G["mem_norm"] = dgm[0]
    return loss, dx, G


ANY = pl.BlockSpec(memory_space=pl.ANY)


def _place():
    x, y, c = lax.axis_index("x"), lax.axis_index("y"), lax.axis_index("c")
    chips = [(1 - x, y), (x, 1 - y), (1 - x, 1 - y)]
    return x, y, c, 2 * x + y, (x, y, 1 - c), chips


def _remote(src, dst, send, recv, k, dev):
    return pltpu.make_async_remote_copy(src_ref=src, dst_ref=dst, send_sem=send.at[k], recv_sem=recv.at[k],
                                        device_id=dev, device_id_type=MESHID)


HBM = pl.BlockSpec(memory_space=pltpu.HBM)
SEM = pl.BlockSpec(memory_space=pltpu.SEMAPHORE)
EFFECT = pltpu.SideEffectType.DATAFLOW_SIDE_EFFECTING


def _hbm(t):
    return pltpu.with_memory_space_constraint(t, pltpu.HBM)


def allgather_sync(shards):
    n = len(shards)

    def body(*refs):
        ins, outs = refs[:n], refs[n:2 * n]
        token, send, recv = refs[2 * n:]
        x, y, c, jme, sib, chips = _place()
        first, passed = [], []
        for a in range(n):
            cp = _remote(ins[a], outs[a].at[jme], send, recv, a * 7 + 6, sib)
            cp.start()
            first.append(cp)
            for k, chip in enumerate(chips):
                cp = _remote(ins[a].at[c], outs[a].at[jme, c], send, recv, a * 7 + k, (*chip, c))
                cp.start()
                first.append(cp)
        for a in range(n):
            for k, chip in enumerate(chips):
                piece = outs[a].at[2 * chip[0] + chip[1], c]
                _remote(piece, piece, send, recv, a * 7 + k, (*chip, c)).wait_recv()
                fw = _remote(piece, piece, send, recv, a * 7 + 3 + k, sib)
                fw.start()
                passed.append(fw)
        for a in range(n):
            own = outs[a].at[jme]
            _remote(own, own, send, recv, a * 7 + 6, sib).wait_recv()
            for k, chip in enumerate(chips):
                piece = outs[a].at[2 * chip[0] + chip[1], 1 - c]
                _remote(piece, piece, send, recv, a * 7 + 3 + k, sib).wait_recv()
        for cp in first + passed:
            cp.wait_send()
        token[...] = jnp.zeros_like(token)

    res = pl.pallas_call(
        body, name="allgather_sync", in_specs=[ANY] * n,
        out_specs=[ANY] * n + [pl.BlockSpec(memory_space=pltpu.VMEM)],
        out_shape=[jax.ShapeDtypeStruct((4,) + s.shape, s.dtype) for s in shards] + [jax.ShapeDtypeStruct((8, 128), F32)],
        scratch_shapes=[pltpu.SemaphoreType.DMA((7 * n,)), pltpu.SemaphoreType.DMA((7 * n,))],
    )(*shards)
    return list(res[:n]), res[n]


def _gather_copies(ins, lands, send, recv):
    x, y, c, jme, sib, chips = _place()
    devs = [(*chip, c) for chip in chips] + [sib]
    return [_remote(ins[a], lands[a].at[jme], send, recv, a * 4 + k, dev)
            for a in range(len(ins)) for k, dev in enumerate(devs)]


def allgather_start(shards, after, name):
    n, na = len(shards), len(after)

    def body(*refs):
        ins, lands = refs[:n], refs[n:2 * n]
        send, recv = refs[2 * n + na], refs[2 * n + na + 1]
        token = refs[-1]
        for cp in _gather_copies(ins, lands, send, recv):
            cp.start()
        token[...] = jnp.zeros_like(token)

    res = pl.pallas_call(
        body, name=name,
        out_shape=(pltpu.SemaphoreType.DMA((4 * n,)), pltpu.SemaphoreType.DMA((4 * n,)),
                   *[pltpu.HBM(s.shape, s.dtype) for s in shards],
                   *[pltpu.HBM((4,) + s.shape, s.dtype) for s in shards],
                   jax.ShapeDtypeStruct((8, 128), F32)),
        in_specs=[HBM] * (2 * n) + [ANY] * na,
        out_specs=(SEM, SEM, *[HBM] * (2 * n), pl.BlockSpec(memory_space=pltpu.VMEM)),
        input_output_aliases={a: 2 + a for a in range(2 * n)},
        compiler_params=pltpu.CompilerParams(has_side_effects=EFFECT),
    )(*[_hbm(s) for s in shards], *[_hbm(lax.empty((4,) + s.shape, s.dtype)) for s in shards], *after)
    return res[0], res[1], list(res[2:2 + n]), list(res[2 + n:2 + 2 * n]), res[-1]


def allgather_wait(send, recv, shards, lands, after, name):
    n = len(shards)

    def body(*refs):
        ins, zones = refs[:n], refs[n:2 * n]
        send_r, recv_r = refs[2 * n], refs[2 * n + 1]
        x, y, c, jme, sib, chips = _place()
        slots = [2 * chip[0] + chip[1] for chip in chips] + [jme]
        for a in range(n):
            for k, slot in enumerate(slots):
                cp = _remote(ins[a], zones[a].at[slot], send_r, recv_r, a * 4 + k, sib)
                cp.wait_send()
                cp.wait_recv()

    res = pl.pallas_call(
        body, name=name,
        out_shape=tuple(pltpu.HBM(t.shape, t.dtype) for t in list(shards) + list(lands)),
        in_specs=[HBM] * (2 * n) + [SEM, SEM, ANY], out_specs=tuple([HBM] * (2 * n)),
        input_output_aliases={a: a for a in range(2 * n)},
        compiler_params=pltpu.CompilerParams(has_side_effects=EFFECT),
    )(*shards, *lands, send, recv, after)
    return list(res[n:])


def allgather_small(slab):
    def body(in_ref, out_ref, send, recv, lsem):
        x, y, c, jme, sib, chips = _place()
        loc = pltpu.make_async_copy(in_ref, out_ref.at[jme], lsem.at[0])
        loc.start()
        cps = [_remote(in_ref, out_ref.at[jme], send, recv, k, (*chip, c)) for k, chip in enumerate(chips)]
        for cp in cps:
            cp.start()
        for k, chip in enumerate(chips):
            piece = out_ref.at[2 * chip[0] + chip[1]]
            _remote(piece, piece, send, recv, k, (*chip, c)).wait_recv()
        for cp in cps:
            cp.wait_send()
        loc.wait()

    return pl.pallas_call(
        body, name="allgather_small", in_specs=[ANY], out_specs=ANY,
        out_shape=jax.ShapeDtypeStruct((4,) + slab.shape, slab.dtype),
        scratch_shapes=[pltpu.SemaphoreType.DMA((3,)), pltpu.SemaphoreType.DMA((3,)), pltpu.SemaphoreType.DMA((1,))],
    )(slab)


def allreduce_small(v):
    def body(v_ref, o_ref, r0, r1, r2, send, recv):
        x, y, c, jme, sib, chips = _place()
        peers = [sib, (1 - x, y, c), (x, 1 - y, c)]
        o_ref[...] = v_ref[...]
        for k, buf in enumerate((r0, r1, r2)):
            cp = _remote(o_ref, buf, send, recv, k, peers[k])
            cp.start()
            cp.wait()
            o_ref[...] = o_ref[...] + buf[...]

    vm = pl.BlockSpec(memory_space=pltpu.VMEM)
    return pl.pallas_call(
        body, name="allreduce_small", in_specs=[vm], out_specs=vm,
        out_shape=jax.ShapeDtypeStruct(v.shape, v.dtype),
        scratch_shapes=[pltpu.VMEM(v.shape, v.dtype)] * 3 + [pltpu.SemaphoreType.DMA((3,)), pltpu.SemaphoreType.DMA((3,))],
        compiler_params=pltpu.CompilerParams(vmem_limit_bytes=VMEM_LIMIT),
    )(v)


def _pair_copies(gs, lands, send, recv):
    x, y, c, jme, sib, chips = _place()
    return [_remote(gs[a].at[:, 1 - c], lands[a], send, recv, a, sib) for a in range(len(gs))]


def rs_pair_start(gs, name):
    n = len(gs)

    def body(*refs):
        ins, lands = refs[:n], refs[n:2 * n]
        send, recv = refs[2 * n], refs[2 * n + 1]
        token = refs[-1]
        for cp in _pair_copies(ins, lands, send, recv):
            cp.start()
        token[...] = jnp.zeros_like(token)

    shapes = [(4,) + g.shape[2:] for g in gs]
    res = pl.pallas_call(
        body, name=name,
        out_shape=(pltpu.SemaphoreType.DMA((n,)), pltpu.SemaphoreType.DMA((n,)),
                   *[pltpu.HBM(g.shape, g.dtype) for g in gs], *[pltpu.HBM(s, F32) for s in shapes],
                   jax.ShapeDtypeStruct((8, 128), F32)),
        in_specs=[HBM] * (2 * n), out_specs=(SEM, SEM, *[HBM] * (2 * n), pl.BlockSpec(memory_space=pltpu.VMEM)),
        input_output_aliases={a: 2 + a for a in range(2 * n)},
        compiler_params=pltpu.CompilerParams(has_side_effects=EFFECT),
    )(*[_hbm(g) for g in gs], *[_hbm(lax.empty(s, F32)) for s in shapes])
    return res[0], res[1], list(res[2:2 + n]), list(res[2 + n:2 + 2 * n]), res[-1]


def rs_pair_wait(send, recv, gs, lands, after, name):
    n = len(gs)

    def body(*refs):
        ins, zones = refs[:n], refs[n:2 * n]
        for cp in _pair_copies(ins, zones, refs[2 * n], refs[2 * n + 1]):
            cp.wait_send()
            cp.wait_recv()

    res = pl.pallas_call(
        body, name=name,
        out_shape=tuple(pltpu.HBM(t.shape, t.dtype) for t in list(gs) + list(lands)),
        in_specs=[HBM] * (2 * n) + [SEM, SEM, ANY], out_specs=tuple([HBM] * (2 * n)),
        input_output_aliases={a: a for a in range(2 * n)},
        compiler_params=pltpu.CompilerParams(has_side_effects=EFFECT),
    )(*gs, *lands, send, recv, after)
    return list(res[:n]), list(res[n:])


SUM_ROWS = 256


def rs_pair_sum(g4s, gots, cidx):
    n = len(g4s)
    tiles = [(min(g.shape[2], SUM_ROWS), g.shape[3]) for g in g4s]
    nts = [g.shape[2] // tr for g, (tr, _) in zip(g4s, tiles)]

    def at(a, s):
        s = jnp.minimum(s, 4 * nts[a] - 1)
        return s // nts[a], s % nts[a]

    def body(c_ref, *refs):
        for a in range(n):
            refs[2 * n + a][...] = (refs[a][...] + refs[n + a][...]).astype(BF16)

    in_specs = [pl.BlockSpec((None, None) + tiles[a], lambda s, cr, a=a: (at(a, s)[0], cr[0], at(a, s)[1], 0))
                for a in range(n)]
    in_specs += [pl.BlockSpec((None,) + tiles[a], lambda s, cr, a=a: (*at(a, s), 0)) for a in range(n)]
    return pl.pallas_call(
        body, name="rs_pair_sum",
        grid_spec=pltpu.PrefetchScalarGridSpec(
            num_scalar_prefetch=1, grid=(4 * max(nts),), in_specs=in_specs,
            out_specs=[pl.BlockSpec((None,) + tiles[a], lambda s, cr, a=a: (*at(a, s), 0)) for a in range(n)]),
        out_shape=[jax.ShapeDtypeStruct((4,) + g.shape[2:], BF16) for g in g4s],
        compiler_params=_cparams(("arbitrary",)),
    )(cidx, *g4s, *gots)


def _chip_copies(ps, lands, send, recv):
    x, y, c, jme, sib, chips = _place()
    return [_remote(ps[a].at[2 * chip[0] + chip[1]], lands[a].at[jme], send, recv, a * 3 + k, (*chip, c))
            for a in range(len(ps)) for k, chip in enumerate(chips)]


def rs_chip_start(ps, name):
    n = len(ps)

    def body(*refs):
        ins, lands = refs[:n], refs[n:2 * n]
        send, recv = refs[2 * n], refs[2 * n + 1]
        token = refs[-1]
        for cp in _chip_copies(ins, lands, send, recv):
            cp.start()
        token[...] = jnp.zeros_like(token)

    res = pl.pallas_call(
        body, name=name,
        out_shape=(pltpu.SemaphoreType.DMA((3 * n,)), pltpu.SemaphoreType.DMA((3 * n,)),
                   *[pltpu.HBM(p.shape, p.dtype) for p in ps], *[pltpu.HBM(p.shape, p.dtype) for p in ps],
                   jax.ShapeDtypeStruct((8, 128), F32)),
        in_specs=[HBM] * (2 * n), out_specs=(SEM, SEM, *[HBM] * (2 * n), pl.BlockSpec(memory_space=pltpu.VMEM)),
        input_output_aliases={a: 2 + a for a in range(2 * n)},
        compiler_params=pltpu.CompilerParams(has_side_effects=EFFECT),
    )(*[_hbm(p) for p in ps], *[_hbm(lax.empty(p.shape, p.dtype)) for p in ps])
    return res[0], res[1], list(res[2:2 + n]), list(res[2 + n:2 + 2 * n]), res[-1]


def rs_chip_wait(send, recv, ps, lands, after, name):
    n = len(ps)

    def body(*refs):
        ins, zones = refs[:n], refs[n:2 * n]
        send_r, recv_r = refs[2 * n], refs[2 * n + 1]
        x, y, c, jme, sib, chips = _place()
        for a in range(n):
            for k, chip in enumerate(chips):
                jt = 2 * chip[0] + chip[1]
                cp = _remote(ins[a].at[jt], zones[a].at[jt], send_r, recv_r, a * 3 + k, (*chip, c))
                cp.wait_send()
                cp.wait_recv()

    res = pl.pallas_call(
        body, name=name,
        out_shape=tuple(pltpu.HBM(p.shape, p.dtype) for p in list(ps) + list(lands)),
        in_specs=[HBM] * (2 * n) + [SEM, SEM] + [ANY] * len(after), out_specs=tuple([HBM] * (2 * n)),
        input_output_aliases={a: a for a in range(2 * n)},
        compiler_params=pltpu.CompilerParams(has_side_effects=EFFECT),
    )(*ps, *lands, send, recv, *after)
    return list(res[n:])


def rs_chip_sum(qs, ps, ls, accs, layers, jc):
    n = len(qs)
    tiles = [(min(q.shape[1], SUM_ROWS), q.shape[2]) for q in qs]
    nts = [q.shape[1] // tr for q, (tr, _) in zip(qs, tiles)]

    def at(a, s):
        return jnp.minimum(s, nts[a] - 1)

    def body(jc_ref, *refs):
        jme = jc_ref[0]
        for a in range(n):
            q_ref, p_ref, o_ref = refs[a], refs[n + a], refs[len(refs) - n + a]
            own = p_ref[...].astype(F32)
            v = [jnp.where(jme == j, own, q_ref[j].astype(F32)) for j in range(4)]
            o_ref[...] = ((v[0] + v[1]) + v[2]) + v[3]

    in_specs = [pl.BlockSpec((4,) + tiles[a], lambda s, jr, a=a: (0, at(a, s), 0)) for a in range(n)]
    in_specs += [pl.BlockSpec((None,) + tiles[a], lambda s, jr, a=a: (jr[0], at(a, s), 0)) for a in range(n)]
    args, aliases = [jc, *qs, *ps], {}
    for a in range(n):
        if accs[a] is not None:
            aliases[len(args)] = a
            in_specs.append(ANY)
            args.append(accs[a])
    return pl.pallas_call(
        body, name="rs_chip_sum",
        grid_spec=pltpu.PrefetchScalarGridSpec(
            num_scalar_prefetch=1, grid=(max(nts),), in_specs=in_specs,
            out_specs=[pl.BlockSpec((None, None) + tiles[a], lambda s, jr, a=a: (ls[a], jr[1], at(a, s), 0))
                       for a in range(n)]),
        out_shape=[jax.ShapeDtypeStruct((layers[a], 2) + qs[a].shape[1:], F32) for a in range(n)],
        input_output_aliases=aliases,
        compiler_params=_cparams(("arbitrary",)),
    )(*args)


def rs_pair_gather(rs):
    n = len(rs)

    def body(*refs):
        outs = refs[n:2 * n]
        send, recv = refs[2 * n:]
        x, y, c, jme, sib, chips = _place()
        cps = [_remote(outs[a].at[:, c], outs[a].at[:, c], send, recv, a, sib) for a in range(n)]
        for cp in cps:
            cp.start()
        for a in range(n):
            slot = outs[a].at[:, 1 - c]
            _remote(slot, slot, send, recv, a, sib).wait_recv()
        for cp in cps:
            cp.wait_send()

    return pl.pallas_call(
        body, name="rs_pair_gather", in_specs=[ANY] * n, out_specs=[ANY] * n,
        out_shape=[jax.ShapeDtypeStruct(r.shape, r.dtype) for r in rs],
        input_output_aliases={a: a for a in range(n)},
        scratch_shapes=[pltpu.SemaphoreType.DMA((n,)), pltpu.SemaphoreType.DMA((n,))],
    )(*rs)


def _adamw_math(w, g, m, v):
    m = B1 * m + (1.0 - B1) * g
    v = B2 * v + (1.0 - B2) * (g * g)
    m_hat = m / (1.0 - B1 ** STEP)
    v_hat = v / (1.0 - B2 ** STEP)
    return -LR * (m_hat / (jnp.sqrt(v_hat) + AEPS) + WD * w), m, v


def adamw(w, g, m, v, name, with_grad=False):
    rows, cols = w.shape
    tr = 256 if rows % 256 == 0 else rows
    fn = (lambda wv, gv, mv, vv: (gv,) + _adamw_math(wv, gv, mv, vv)) if with_grad else _adamw_math
    return rw(fn, [(a, 0, cols) for a in (w, g, m, v)], [(cols, F32)] * (4 if with_grad else 3), name, rows, tr=tr)


def adamw_small(ws, gs, ms, vs):
    n = len(ws)

    def body(*refs):
        for a in range(n):
            res = _adamw_math(*[refs[k * n + a][...] for k in range(4)])
            for k in range(3):
                refs[(4 + k) * n + a][...] = res[k]

    res = pl.pallas_call(
        body, name="adamw_small", in_specs=[VM] * (4 * n), out_specs=[VM] * (3 * n),
        out_shape=[jax.ShapeDtypeStruct(w.shape, F32) for _ in range(3) for w in ws],
        compiler_params=pltpu.CompilerParams(vmem_limit_bytes=VMEM_LIMIT),
    )(*ws, *gs, *ms, *vs)
    return [(res[a], res[n + a], res[2 * n + a]) for a in range(n)]


WEIGHTS = ["norm_ab", "w_in_ab", "pool_w", "pool_scale", "w_out_ab", "norm_cd", "w_in_cd", "sgu_ln_g", "sgu_ln_b",
           "sgu_w", "sgu_b", "s5_a_re", "s5_a_im", "s5_log_dt", "s5_b_re", "s5_b_im", "s5_c_re", "s5_c_im", "s5_d",
           "glu_w1", "glu_w2", "w_out_cd", "norm_x", "w_xq", "w_xkv", "w_xo", "mem_norm", "final_norm"]
INPUTS = ["x", "mem"] + WEIGHTS + ["loss_target"] + ["m_" + n for n in WEIGHTS] + ["v_" + n for n in WEIGHTS]
BIG = ["w_in_ab", "w_out_ab", "w_in_cd", "w_out_cd", "w_xq", "w_xkv", "w_xo", "glu_w1", "glu_w2", "pool_w"]
COL_SHARDED = ("w_in_ab", "w_in_cd", "w_xkv")
SMALL = [n for n in WEIGHTS if n not in BIG]
SMALL_SHARDED = {"norm_cd": 256, "sgu_ln_g": 256, "sgu_ln_b": 256, "s5_d": 128}
PACK = 256 * 128


def _pack(arrs):
    flat = jnp.concatenate([a.reshape(-1) for a in arrs])
    pad = (-flat.shape[0]) % PACK
    return jnp.concatenate([flat, jnp.zeros((pad,), flat.dtype)]).reshape(-1, 128)


def _unpack(packed, shapes):
    flat, out, off = packed.reshape(-1), [], 0
    for s in shapes:
        n = 1
        for d in s:
            n *= d
        out.append(flat[off:off + n].reshape(s))
        off += n
    return out


LAYER_KEYS = (("w_in", "w_out", "pool_w", "w_xq", "w_xkv", "w_xo"),
              ("w_in", "w_out", "glu_w1", "glu_w2", "w_xq", "w_xkv", "w_xo"))


def _weight_of(key, layer):
    if key in ("w_xq", "w_xkv", "w_xo"):
        return key, layer, 4
    kind = "ab" if layer % 2 == 0 else "cd"
    return {"w_in": "w_in_" + kind, "w_out": "w_out_" + kind}.get(key, key), layer // 2, 2


def kernel(*args):
    a = dict(zip(INPUTS, args))
    x_i, y_i, c_i = lax.axis_index("x"), lax.axis_index("y"), lax.axis_index("c")
    j = 2 * x_i + y_i

    slab = jnp.concatenate([a["norm_cd"], a["sgu_ln_g"], a["sgu_ln_b"],
                            jnp.pad(a["s5_d"], ((0, 0), (0, 128)))], axis=0)
    gslab = allgather_small(slab)
    P = {n: a[n] for n in SMALL}
    for k, n in enumerate(("norm_cd", "sgu_ln_g", "sgu_ln_b", "s5_d")):
        wd = SMALL_SHARDED[n]
        P[n] = gslab[:, 2 * k:2 * k + 2, :wd].transpose(1, 0, 2).reshape(2, 4 * wd)

    def shards_of(layer):
        keys = sorted(k for k in LAYER_KEYS[layer % 2])
        out = []
        for k in keys:
            n, l, _ = _weight_of(k, layer)
            out.append(a[n][l].reshape(-1, a[n].shape[-1]).astype(BF16))
        return keys, out

    keys0, sh0 = shards_of(0)
    first = keys0.index("w_in")
    g_in, token = allgather_sync([sh0[first].reshape(2, sh0[first].shape[0] // 2, sh0[first].shape[1])])
    w_in0 = g_in[0].reshape(4, -1, g_in[0].shape[-1])
    started = {}
    for layer in (0, 1, 2, 3):
        keys, sh = (keys0, sh0) if layer == 0 else shards_of(layer)
        rest = [(k, s) for k, s in zip(keys, sh) if k != "w_in"]
        parts = [("in", ["w_in"], [sh[keys.index("w_in")]])] * (layer > 0) + [("", *map(list, zip(*rest)))]
        for tag, pk, ps in parts:
            send, recv, ps, lands, token = allgather_start(ps, [token, gslab], "allgather_start_%d%s" % (layer, tag))
            started[(layer, tag)] = (pk, send, recv, ps, lands)
    P["norm_ab"] = P["norm_ab"] + token[0:1, 0:1]

    cidx = jnp.reshape(c_i, (1,)).astype(jnp.int32)
    jc = jnp.stack([j, c_i]).astype(jnp.int32)

    def views(g):
        W = {}
        for k, v in g.items():
            if k in ("w_in", "w_xkv"):
                W[k] = mcs(v)
            elif k == "pool_w":
                W[k] = v.reshape(4, 4, 64, 256).transpose(1, 0, 2, 3).reshape(4, 256, 256)
            elif k not in ("glu_w1", "glu_w2"):
                W[k] = m2(v.reshape(-1, v.shape[-1]))
        if "glu_w1" in g:
            W["w12"] = jnp.concatenate([g["glu_w1"].reshape(512, 512), g["glu_w2"].reshape(512, 512)], axis=1)
        return W

    def arrived(layer, tag, after):
        keys, send, recv, sh, lands = started[(layer, tag)]
        return views(dict(zip(keys, allgather_wait(send, recv, sh, lands, after, "allgather_wait_%d%s" % (layer, tag)))))

    def weights_of(layer, x_in):
        W = views({"w_in": w_in0}) if layer == 0 else arrived(layer, "in", x_in)
        W["more"] = lambda after: arrived(layer, "", after)
        return W

    halves, pending = {}, {}

    def finish_pair(layer, after):
        keys, send, recv, flat, lands = halves.pop(layer)
        flat, got = rs_pair_wait(send, recv, flat, lands, after, "rs_pair_wait_%d" % layer)
        pair = rs_pair_sum(flat, got, cidx)
        send, recv, pair, lands, token = rs_chip_start(pair, "rs_chip_start_%d" % layer)
        pending[layer] = (keys, send, recv, pair, lands)
        return token

    def grads_done(layer, GW):
        keys = sorted(GW)
        flat = [GW[k].reshape(4, 2, GW[k].shape[1] // 2, GW[k].shape[2]) for k in keys]
        send, recv, flat, lands, token = rs_pair_start(flat, "rs_pair_start_%d" % layer)
        halves[layer] = (keys, send, recv, flat, lands)
        if layer + 1 in halves:
            token = token + finish_pair(layer + 1, token)
        return token[0:1, 0:1]

    loss, dx, G = local_step(a["x"][0], a["mem"][0], a["loss_target"][0], P, weights_of, grads_done)
    loss = lax.psum(loss[0, 0], ("x", "y", "c"))
    finish_pair(0, dx)
    outs = {}

    def update_big(names, red):
        for n, g in zip(names, rs_pair_gather([red[n] for n in names])):
            shp = a[n].shape
            g2 = g.reshape(-1, shp[-1])
            upd = adamw(a[n].reshape(g2.shape), g2, a["m_" + n].reshape(g2.shape), a["v_" + n].reshape(g2.shape),
                        "adamw_" + n, with_grad=True)
            outs[n] = tuple(t.reshape(shp) for t in upd)

    def reduce_layer(layer, red, after):
        keys, send, recv, pair, lands = pending[layer]
        lands = rs_chip_wait(send, recv, pair, lands, after, "rs_chip_wait_%d" % layer)
        which = [_weight_of(k, layer) for k in keys]
        sums = rs_chip_sum(lands, pair, [l for _, l, _ in which], [red.get(n) for n, _, _ in which],
                           [layers for _, _, layers in which], jc)
        red.update(zip([n for n, _, _ in which], sums))

    red = {}
    for layer in (3, 2, 1):
        reduce_layer(layer, red, [dx])
    odd_only = [n for n in BIG if n.endswith("_cd") or n.startswith("glu")]
    update_big(odd_only, red)

    gfull = [jnp.stack(G[n]) if isinstance(G[n], list) else G[n] for n in SMALL]
    shapes = [g.shape for g in gfull]
    gsum = _unpack(allreduce_small(_pack(gfull)), shapes)
    gloc = []
    for n, g in zip(SMALL, gsum):
        if n in SMALL_SHARDED:
            g = lax.dynamic_slice_in_dim(g, j * SMALL_SHARDED[n], SMALL_SHARDED[n], axis=1)
        gloc.append(g)
    two = [(-1, a[n].shape[-1]) if a[n].ndim > 1 else (1, a[n].shape[0]) for n in SMALL]
    upds = adamw_small(*[[t.reshape(s) for t, s in zip(ts, two)]
                         for ts in ([a[n] for n in SMALL], gloc, [a["m_" + n] for n in SMALL],
                                    [a["v_" + n] for n in SMALL])])
    for n, g, upd in zip(SMALL, gloc, upds):
        outs[n] = (g,) + tuple(t.reshape(a[n].shape) for t in upd)

    behind = [outs[n][1] for n in odd_only + SMALL[-1:]] + [red[n] for n in BIG if n not in odd_only]
    reduce_layer(0, red, behind)
    update_big([n for n in BIG if n not in odd_only], red)

    res = [loss, dx[None]]
    for part in range(4):
        res += [outs[n][part] for n in WEIGHTS]
    return tuple(res)
```

```python
import math

import jax
import jax.numpy as jnp
from jax import lax
from jax.experimental import pallas as pl
from jax.experimental.pallas import tpu as pltpu

F32, BF16 = jnp.float32, jnp.bfloat16
S, D = 2048, 1024
MEM = 256
EPS = 1e-6
NEG = -1e30
QB = 128
PATTERNS = (1, 4, 16)
NG, NP, NH = 32, 64, 16
NS = NG * NP
LR, B1, B2, AEPS, WD, STEP = 0.001, 0.9, 0.999, 1e-08, 0.01, 10
MESHID = pl.DeviceIdType.MESH
VMEM_LIMIT = 56 * 1024 * 1024


def _cparams(sem):
    return pltpu.CompilerParams(dimension_semantics=sem, vmem_limit_bytes=VMEM_LIMIT)


def _sig(x):
    return 1.0 / (1.0 + jnp.exp(-x))


def _dot(a, b, dims):
    return lax.dot_general(a, b, (dims, ((), ())), preferred_element_type=F32)


def _nn(a, b):
    return _dot(a, b, ((1,), (0,)))


def _nt(a, b):
    return _dot(a, b, ((1,), (1,)))


def _tn(a, b):
    return _dot(a, b, ((0,), (0,)))


_DIMS = {"nn": ((1,), (0,)), "nt": ((1,), (1,)), "tn": ((0,), (0,))}


def _tile(dim, cc=None, cap=1024):
    for t in (2048, 1536, 1024, 768, 512, 384, 256, 128):
        if t <= cap and dim % t == 0 and (cc is None or cc % t == 0):
            return t
    return dim


MM_VMEM = 36 * 1024 * 1024


def _mm_tiles(m, n, k, ccm, ccn, cck, a_bytes, b_bytes, o_bytes):
    caps = [1024, 1024, 2048]
    while True:
        tm, tn, tk = _tile(m, ccm, caps[0]), _tile(n, ccn, caps[1]), _tile(k, cck, caps[2])
        need = 2 * (tm * tk * a_bytes + tk * tn * b_bytes + tm * tn * o_bytes) + (tm * tn * 4 if tk < k else 0)
        if need <= MM_VMEM:
            return tm, tn, tk
        if tk > 1024:
            caps[2] = tk // 2
        elif tn >= tm:
            caps[1] = tn // 2
        else:
            caps[0] = tm // 2


def m2(arr, col_off=0, ncols=None):
    rows, cols = arr.shape
    ncols = cols - col_off if ncols is None else ncols

    def spec(tr, tc, rc):
        assert col_off % tc == 0
        return pl.BlockSpec((tr, tc), lambda *g: (rc(*g)[0], rc(*g)[1] + col_off // tc))
    return (arr, rows, ncols, spec, None if col_off == 0 else col_off)


def mcs(arr):
    cs = arr.shape[2]

    def spec(tr, tc, rc):
        n = cs // tc
        return pl.BlockSpec((None, tr, tc), lambda *g: (rc(*g)[1] // n, rc(*g)[0], rc(*g)[1] % n))
    return (arr, arr.shape[1], 4 * cs, spec, cs)


def out2(rows, cols):
    def spec(tr, tc, rc):
        return pl.BlockSpec((tr, tc), lambda *g: tuple(rc(*g)))
    return ((rows, cols), spec, None)


def outcs(rows, cs):
    def spec(tr, tc, rc):
        n = cs // tc
        return pl.BlockSpec((None, tr, tc), lambda *g: (rc(*g)[1] // n, rc(*g)[0], rc(*g)[1] % n))
    return ((4, rows, cs), spec, cs)


def _both(a, b):
    if a is None:
        return b
    if b is None:
        return a
    return math.gcd(a, b)


def mm(a, b, mode, name, add=None, out=None, out_dtype=F32):
    a_arr, a_r, a_c, a_spec, a_cc = a
    b_arr, b_r, b_c, b_spec, b_cc = b
    if mode == "nn":
        m, k, n = a_r, a_c, b_c
        assert b_r == k
        ccm, cck, ccn = None, a_cc, b_cc
    elif mode == "nt":
        m, k, n = a_r, a_c, b_r
        assert b_c == k
        ccm, cck, ccn = None, _both(a_cc, b_cc), None
    else:
        m, k, n = a_c, a_r, b_c
        assert b_r == k
        ccm, cck, ccn = a_cc, None, b_cc
    out = out2(m, n) if out is None else out
    o_shape, o_spec, o_cc = out
    ccn = _both(ccn, o_cc)
    if add is not None:
        ccn = _both(ccn, add[4])
    o_bytes = jnp.dtype(out_dtype).itemsize + (0 if add is None else add[0].dtype.itemsize)
    tm, tn, tk = _mm_tiles(m, n, k, ccm, ccn, cck, a_arr.dtype.itemsize, b_arr.dtype.itemsize, o_bytes)
    nk = k // tk
    if mode == "nn":
        in_specs = [a_spec(tm, tk, lambda i, j, kk: (i, kk)), b_spec(tk, tn, lambda i, j, kk: (kk, j))]
    elif mode == "nt":
        in_specs = [a_spec(tm, tk, lambda i, j, kk: (i, kk)), b_spec(tn, tk, lambda i, j, kk: (j, kk))]
    else:
        in_specs = [a_spec(tk, tm, lambda i, j, kk: (kk, i)), b_spec(tk, tn, lambda i, j, kk: (kk, j))]
    args = [a_arr, b_arr]
    if add is not None:
        in_specs.append(add[3](tm, tn, lambda i, j, kk: (i, j)))
        args.append(add[0])
    return _mm_call(args, in_specs, o_spec(tm, tn, lambda i, j, kk: (i, j)), jax.ShapeDtypeStruct(o_shape, out_dtype),
                    mode, (m // tm, n // tn, nk), (tm, tn), add is not None, name)


def _mm_call(args, in_specs, out_spec, out_shape, mode, grid, tile, has_add, name):
    dims = _DIMS[mode]
    nk = grid[2]
    tm, tn = tile

    def body(*refs):
        a_ref, b_ref = refs[0], refs[1]
        add_ref = refs[2] if has_add else None
        prod = _dot(a_ref[...].astype(BF16), b_ref[...].astype(BF16), dims)
        if nk == 1:
            o_ref = refs[-1]
            if has_add:
                prod = prod + add_ref[...].astype(F32)
            o_ref[...] = prod.astype(o_ref.dtype)
            return
        o_ref, acc = refs[-2], refs[-1]
        kk = pl.program_id(2)

        @pl.when(kk == 0)
        def _():
            acc[...] = prod

        @pl.when(kk > 0)
        def _():
            acc[...] += prod

        @pl.when(kk == nk - 1)
        def _():
            r = acc[...]
            if has_add:
                r = r + add_ref[...].astype(F32)
            o_ref[...] = r.astype(o_ref.dtype)

    return pl.pallas_call(
        body, name=name, grid=grid, in_specs=in_specs, out_specs=out_spec, out_shape=out_shape,
        scratch_shapes=[pltpu.VMEM((tm, tn), F32)] if nk > 1 else [],
        compiler_params=_cparams(("parallel", "parallel", "arbitrary")),
    )(*args)


def mm_band(a, b, mode, name, grid, blocks, maps, out_shape, add=None, out_dtype=F32):
    in_specs = [pl.BlockSpec(blocks[0], maps[0]), pl.BlockSpec(blocks[1], maps[1])]
    args = [a, b]
    if add is not None:
        in_specs.append(pl.BlockSpec(blocks[2], maps[2]))
        args.append(add)
    return _mm_call(args, in_specs, pl.BlockSpec(blocks[2], maps[2]), jax.ShapeDtypeStruct(out_shape, out_dtype),
                    mode, grid, blocks[2], add is not None, name)


def rw(fn, ins, outs, name, rows, tr=256, consts=(), accs=()):
    n_in, n_c, n_o, n_a = len(ins), len(consts), len(outs), len(accs)
    in_specs = []
    for arr, off, width in ins:
        assert off % width == 0
        in_specs.append(pl.BlockSpec((tr, width), lambda i, o=off // width: (i, o)))
    for c in consts:
        in_specs.append(pl.BlockSpec(c.shape, lambda i: (0, 0)))
    out_specs = [pl.BlockSpec((tr, w), lambda i: (i, 0)) for w, _ in outs]
    out_specs += [pl.BlockSpec(s, lambda i: (0, 0)) for s in accs]
    out_shape = [jax.ShapeDtypeStruct((rows, w), dt) for w, dt in outs]
    out_shape += [jax.ShapeDtypeStruct(s, F32) for s in accs]

    def body(*refs):
        vals = [r[...] for r in refs[:n_in + n_c]]
        o_refs = refs[n_in + n_c:n_in + n_c + n_o]
        a_refs = refs[n_in + n_c + n_o:]
        res = fn(*vals)
        for r, v in zip(o_refs, res[:n_o]):
            r[...] = v.astype(r.dtype)
        if n_a:
            @pl.when(pl.program_id(0) == 0)
            def _():
                for r in a_refs:
                    r[...] = jnp.zeros_like(r)
            for r, v in zip(a_refs, res[n_o:]):
                r[...] += v

    res = pl.pallas_call(
        body, name=name, grid=(rows // tr,), in_specs=in_specs, out_specs=out_specs,
        out_shape=out_shape,
        compiler_params=_cparams(("arbitrary",) if n_a else ("parallel",)),
    )(*[a for a, _, _ in ins], *consts)
    return res


def _rstd(x):
    return lax.rsqrt(jnp.mean(x * x, axis=-1, keepdims=True) + EPS)


def rms_fwd(x, g, name):
    def fn(xv, gv):
        xv = xv.astype(F32)
        return (xv * _rstd(xv) * gv,)
    return rw(fn, [(x, 0, D)], [(D, BF16)], name, x.shape[0], consts=[g])[0]


def _rms_bwd_math(xv, dy, gv):
    r = _rstd(xv)
    dyg = dy * gv
    dx = r * dyg - xv * (r * r * r / D) * jnp.sum(dyg * xv, axis=-1, keepdims=True)
    dg = jnp.sum(dy * xv * r, axis=0, keepdims=True)
    return dx, dg


def rms_bwd(x, dy, dres, g, name):
    def fn(xv, dyv, drv, gv):
        dx, dg = _rms_bwd_math(xv, dyv, gv)
        return dx + drv, dg
    return rw(fn, [(x, 0, D), (dy, 0, D), (dres, 0, D)], [(D, F32)], name, x.shape[0],
              consts=[g], accs=[(1, D)])


def final_loss(x, tgt, g):
    def fn(xv, tv, gv):
        e = xv * _rstd(xv) * gv - tv
        loss = 0.5 * jnp.sum(jnp.sum(e * e, axis=-1, keepdims=True), axis=0, keepdims=True) / D
        dx, dg = _rms_bwd_math(xv, e / D, gv)
        return dx, loss, dg
    return rw(fn, [(x, 0, D), (tgt, 0, D)], [(D, F32)], "final_loss", S, consts=[g],
              accs=[(1, 1), (1, D)])


def _attn_bias(bias_ref):
    ii = lax.broadcasted_iota(jnp.int32, (2 * QB, 2 * QB), 0) % QB
    jj = lax.broadcasted_iota(jnp.int32, (2 * QB, 2 * QB), 1)
    dist = ii + QB - jj
    band = (dist >= 0) & (dist <= QB)
    bias_ref[1] = jnp.where(band, 0.0, NEG)
    bias_ref[0] = jnp.where(band & (jj >= QB), 0.0, NEG)


def _two_heads(x, m0):
    return jnp.concatenate([jnp.where(m0, x, 0.0), jnp.where(m0, 0.0, x)], axis=0)


def _per_head(col, m0):
    return jnp.where(m0, col[:QB], col[QB:])


def _attn_rows(idx, d):
    if d == 1:
        b = idx
        cur = pl.ds(pl.multiple_of(b * QB, QB), QB)
        prev = pl.ds(pl.multiple_of(jnp.maximum(b - 1, 0) * QB, QB), QB)
    else:
        r, b = lax.rem(idx, d), lax.div(idx, d)
        cur = pl.ds(r + b * (QB * d), QB, stride=d)
        prev = pl.ds(r + jnp.maximum(b - 1, 0) * (QB * d), QB, stride=d)
    return cur, prev, b


NBLK = S // QB
GROUP = 16
GROUP_FWD = 16


def _colblk(off):
    return pl.BlockSpec((S, 128), lambda hp: (0, off * 8 + hp))


def attn_fwd(z):
    def body(q_ref, k_ref, v_ref, g_ref, o_ref, l_ref, a_ref, os, ls, bias):
        _attn_bias(bias)
        m0 = lax.broadcasted_iota(jnp.int32, (1, 128), 1) < 64
        for pi, d in enumerate(PATTERNS):
            lone = S // d == QB

            def load(idx, d=d, lone=lone):
                cur, prev, b = _attn_rows(idx, d)
                if lone:
                    return cur, (q_ref[cur, :], None, k_ref[cur, :], None, v_ref[cur, :], bias[1, :, QB:])
                return cur, (q_ref[cur, :], k_ref[prev, :], k_ref[cur, :], v_ref[prev, :], v_ref[cur, :],
                             bias[jnp.minimum(b, 1)])

            def block(q, kp, kc, vp, vc, bs):
                qq = _two_heads(q * 0.125, m0).astype(BF16)
                k = (kc if kp is None else jnp.concatenate([kp, kc], axis=0)).astype(BF16)
                s = _nt(qq, k) + bs
                mx = jnp.max(s, axis=-1, keepdims=True)
                p = jnp.exp(s - mx)
                den = jnp.sum(p, axis=-1, keepdims=True)
                pb = p.astype(BF16)
                vv = _two_heads(vc if vp is None else jnp.concatenate([vp, vc], axis=0), m0).astype(BF16)
                o = _nn(jnp.concatenate([pb[:QB], pb[QB:]], axis=1), vv)
                return o * _per_head(1.0 / den, m0), _per_head(mx + jnp.log(den), m0)

            def step(i, carry, pi=pi):
                loaded = [load(i * GROUP_FWD + u) for u in range(GROUP_FWD)]
                done = [block(*vals) for _, vals in loaded]
                for (cur, _), (o, l) in zip(loaded, done):
                    os[pi, cur, :] = o
                    ls[pi, cur, :] = l
                return carry
            lax.fori_loop(0, NBLK // GROUP_FWD, step, 0)
        l1, l2, l3 = ls[0], ls[1], ls[2]
        mx = jnp.maximum(jnp.maximum(l1, l2), l3)
        e1, e2, e3 = jnp.exp(l1 - mx), jnp.exp(l2 - mx), jnp.exp(l3 - mx)
        tot = e1 + e2 + e3
        o = (os[0] * e1 + os[1] * e2 + os[2] * e3) / tot
        ga = g_ref[...]
        o_ref[...] = o
        l_ref[...] = mx + jnp.log(tot)
        a_ref[...] = (o * (ga * _sig(ga))).astype(a_ref.dtype)

    out = pl.BlockSpec((S, 128), lambda hp: (0, hp))
    return pl.pallas_call(
        body, name="attn_fwd", grid=(8,),
        in_specs=[_colblk(0), _colblk(1), _colblk(2), _colblk(3)], out_specs=[out] * 3,
        out_shape=[jax.ShapeDtypeStruct((S, D), F32), jax.ShapeDtypeStruct((S, D), F32),
                   jax.ShapeDtypeStruct((S, 2 * D), BF16)],
        scratch_shapes=[pltpu.VMEM((3, S, 128), F32), pltpu.VMEM((3, S, 128), F32),
                        pltpu.VMEM((2, 2 * QB, 2 * QB), F32)],
        compiler_params=_cparams(("parallel",)),
    )(z, z, z, z)


def attn_bwd(z, d_cat, o, lse):
    def body(q_ref, k_ref, v_ref, g_ref, da_ref, o_ref, l_ref, dq_ref, dk_ref, dv_ref, dg_ref, do_s, pr_s, bias):
        _attn_bias(bias)
        m0 = lax.broadcasted_iota(jnp.int32, (1, 128), 1) < 64
        ga = g_ref[...]
        sg = _sig(ga)
        da = da_ref[...]
        ov = o_ref[...]
        do = da * (ga * sg)
        dg_ref[...] = da * ov * (sg * (1.0 + ga * (1.0 - sg)))
        do_s[...] = do
        pr_s[...] = do * ov
        dq_ref[...] = jnp.zeros_like(dq_ref)
        dk_ref[...] = jnp.zeros_like(dk_ref)
        dv_ref[...] = jnp.zeros_like(dv_ref)
        for d in PATTERNS:
            lone = S // d == QB

            def load(idx, d=d, lone=lone):
                cur, prev, b = _attn_rows(idx, d)
                if lone:
                    return (cur, None), (q_ref[cur, :], None, k_ref[cur, :], None, v_ref[cur, :],
                                         do_s[cur, :], pr_s[cur, :], l_ref[cur, :], bias[1, :, QB:])
                return (cur, prev), (q_ref[cur, :], k_ref[prev, :], k_ref[cur, :], v_ref[prev, :], v_ref[cur, :],
                                     do_s[cur, :], pr_s[cur, :], l_ref[cur, :], bias[jnp.minimum(b, 1)])

            def block(q, kp, kc, vp, vc, dof, prod, lp, bs):
                qq = _two_heads(q * 0.125, m0).astype(BF16)
                kf = kc if kp is None else jnp.concatenate([kp, kc], axis=0)
                k = kf.astype(BF16)
                v = (vc if vp is None else jnp.concatenate([vp, vc], axis=0)).astype(BF16)
                dd = _two_heads(dof, m0).astype(BF16)
                lh = jnp.max(jnp.concatenate([jnp.where(m0, lp, -jnp.inf), jnp.where(m0, -jnp.inf, lp)], axis=0),
                             axis=-1, keepdims=True)
                delta = jnp.sum(_two_heads(prod, m0), axis=-1, keepdims=True)
                p = jnp.exp(_nt(qq, k) + bs - lh)
                ds = (p * (_nt(dd, v) - delta)).astype(BF16)
                dq = _nn(jnp.concatenate([ds[:QB], ds[QB:]], axis=1), _two_heads(kf, m0).astype(BF16))
                return dq * 0.125, _tn(ds, qq), _tn(p.astype(BF16), dd)

            def step(i, carry):
                loaded = [load(i * GROUP + u) for u in range(GROUP)]
                done = [block(*vals) for _, vals in loaded]
                for ((cur, prev), _), (dq, dk, dv) in zip(loaded, done):
                    dq_ref[cur, :] = dq_ref[cur, :] + dq
                    if prev is not None:
                        dk_ref[prev, :] = dk_ref[prev, :] + dk[:QB]
                        dv_ref[prev, :] = dv_ref[prev, :] + dv[:QB]
                    dk_ref[cur, :] = dk_ref[cur, :] + dk[-QB:]
                    dv_ref[cur, :] = dv_ref[cur, :] + dv[-QB:]
                return carry
            lax.fori_loop(0, NBLK // GROUP, step, 0)

    blk = pl.BlockSpec((S, 128), lambda hp: (0, hp))
    return pl.pallas_call(
        body, name="attn_bwd", grid=(8,),
        in_specs=[_colblk(0), _colblk(1), _colblk(2), _colblk(3), blk, blk, blk], out_specs=[blk] * 4,
        out_shape=[jax.ShapeDtypeStruct((S, D), F32)] * 4,
        scratch_shapes=[pltpu.VMEM((S, 128), F32), pltpu.VMEM((S, 128), F32), pltpu.VMEM((2, 2 * QB, 2 * QB), F32)],
        compiler_params=_cparams(("parallel",)),
    )(z, z, z, z, d_cat, o, lse)


def assemble_dz_even(parts):
    def body(*refs):
        o_ref = refs[-1]
        for j in range(6):
            o_ref[:, j * D:(j + 1) * D] = refs[j][...].astype(o_ref.dtype)
    tr = 256
    blk = pl.BlockSpec((tr, D), lambda i: (i, 0))
    return pl.pallas_call(
        body, name="assemble_dz_even", grid=(S // tr,), in_specs=[blk] * 6,
        out_specs=pl.BlockSpec((tr, 6 * D), lambda i: (i, 0)),
        out_shape=jax.ShapeDtypeStruct((S, 6 * D), BF16),
        compiler_params=_cparams(("parallel",)),
    )(*parts)


def _pool_window(g):
    return jnp.where(g == 0, 2.0, jnp.where(g == 1, 4.0, jnp.where(g == 2, 8.0, 16.0)))


def _pool_sel(g, levels):
    return jnp.where(g == 0, levels[0], jnp.where(g == 1, levels[1], jnp.where(g == 2, levels[2], levels[3])))


def _pool_fwd_math(v, g):
    t = lax.broadcasted_iota(jnp.int32, (S, 1), 0)
    s = v
    levels = []
    for k in (1, 2, 4, 8):
        s = s + jnp.where(t >= k, pltpu.roll(s, k, 0), 0.0)
        levels.append(s)
    cnt = jnp.minimum((t + 1).astype(F32), _pool_window(g))
    return _pool_sel(g, levels) / cnt - v, cnt


def pool_fwd(z, pw, ps, cat):
    def body(v_ref, g_ref, pw_ref, ps_ref, cat_ref, o_ref):
        g = pl.program_id(0)
        pooled, _ = _pool_fwd_math(v_ref[...], g)
        mixed = _nn(pooled.astype(BF16), pw_ref[...].astype(BF16))
        gb = g_ref[...]
        o_ref[...] = (mixed * ps_ref[...] * (gb * _sig(gb))).astype(o_ref.dtype)

    return pl.pallas_call(
        body, name="pool_fwd", grid=(4,),
        in_specs=[pl.BlockSpec((S, 256), lambda g: (0, 16 + g)),
                  pl.BlockSpec((S, 256), lambda g: (0, 20 + g)),
                  pl.BlockSpec((None, 256, 256), lambda g: (g, 0, 0)),
                  pl.BlockSpec((1, 256), lambda g: (0, g)), pl.BlockSpec(memory_space=pl.ANY)],
        out_specs=pl.BlockSpec((S, 256), lambda g: (0, 4 + g)),
        out_shape=jax.ShapeDtypeStruct((S, 2 * D), BF16),
        input_output_aliases={4: 0},
        compiler_params=_cparams(("parallel",)),
    )(z, z, pw, ps, cat)


def pool_bwd(z, d_cat, pw, ps):
    def body(v_ref, g_ref, d_ref, pw_ref, ps_ref, dv_ref, dg_ref, dpw_ref, dps_ref):
        g = pl.program_id(0)
        v = v_ref[...]
        pooled, cnt = _pool_fwd_math(v, g)
        pwb = pw_ref[...].astype(BF16)
        pb = pooled.astype(BF16)
        mixed = _nn(pb, pwb)
        gb = g_ref[...]
        sg = _sig(gb)
        dout = d_ref[...]
        sc = ps_ref[...]
        dg_ref[...] = dout * mixed * sc * (sg * (1.0 + gb * (1.0 - sg)))
        dms = dout * (gb * sg)
        dps_ref[...] = jnp.sum(dms * mixed, axis=0, keepdims=True)
        dmx = (dms * sc).astype(BF16)
        dpw_ref[...] = _tn(pb, dmx)
        dpooled = _nt(dmx, pwb)
        t = lax.broadcasted_iota(jnp.int32, (S, 1), 0)
        s = dpooled / cnt
        levels = []
        for k in (1, 2, 4, 8):
            s = s + jnp.where(t < S - k, pltpu.roll(s, S - k, 0), 0.0)
            levels.append(s)
        dv_ref[...] = _pool_sel(g, levels) - dpooled

    return pl.pallas_call(
        body, name="pool_bwd", grid=(4,),
        in_specs=[pl.BlockSpec((S, 256), lambda g: (0, 16 + g)),
                  pl.BlockSpec((S, 256), lambda g: (0, 20 + g)),
                  pl.BlockSpec((S, 256), lambda g: (0, 4 + g)),
                  pl.BlockSpec((None, 256, 256), lambda g: (g, 0, 0)),
                  pl.BlockSpec((1, 256), lambda g: (0, g))],
        out_specs=[pl.BlockSpec((S, 256), lambda g: (0, g)),
                   pl.BlockSpec((S, 256), lambda g: (0, g)),
                   pl.BlockSpec((None, 256, 256), lambda g: (g, 0, 0)),
                   pl.BlockSpec((1, 256), lambda g: (0, g))],
        out_shape=[jax.ShapeDtypeStruct((S, D), F32), jax.ShapeDtypeStruct((S, D), F32),
                   jax.ShapeDtypeStruct((4, 256, 256), F32), jax.ShapeDtypeStruct((1, D), F32)],
        compiler_params=_cparams(("parallel",)),
    )(z, z, d_cat, pw, ps)


CH = 128


def _sgu_common(v, lng, lnb, w_ref):
    mu = jnp.mean(v, axis=-1, keepdims=True)
    vc = v - mu
    rs = lax.rsqrt(jnp.mean(vc * vc, axis=-1, keepdims=True) + EPS)
    xhat = vc * rs
    vn = (xhat * lng + lnb).astype(BF16)
    ri = lax.broadcasted_iota(jnp.int32, (CH, CH), 0)
    ci = lax.broadcasted_iota(jnp.int32, (CH, CH), 1)
    tril = ri >= ci
    ws = [jnp.where(tril, w_ref[g], 0.0).astype(BF16) for g in range(4)]
    return xhat, rs, vn, tril, ws


def _zspec(off):
    return pl.BlockSpec((CH, D), lambda c: (c, off))


def _full(shape):
    return pl.BlockSpec(shape, lambda c: (0,) * len(shape))


def sgu_fwd(z, lng, lnb, w, bfull):
    def body(u_ref, v_ref, g_ref, lng_ref, lnb_ref, w_ref, b_ref, o_ref):
        _, _, vn, _, ws = _sgu_common(v_ref[...], lng_ref[...], lnb_ref[...], w_ref)
        for g in range(4):
            sl = slice(g * 256, (g + 1) * 256)
            mixed = _nn(ws[g], vn[:, sl]) + b_ref[:, sl]
            gc = g_ref[:, sl]
            o_ref[:, sl] = (u_ref[:, sl] * mixed * (gc * _sig(gc))).astype(o_ref.dtype)

    return pl.pallas_call(
        body, name="sgu_fwd", grid=(S // CH,),
        in_specs=[_zspec(0), _zspec(1), _zspec(2), _full((1, D)), _full((1, D)),
                  _full((4, CH, CH)), _full((CH, D))],
        out_specs=pl.BlockSpec((CH, D), lambda c: (c, 0)),
        out_shape=jax.ShapeDtypeStruct((S, D), BF16),
        compiler_params=_cparams(("parallel",)),
    )(z, z, z, lng, lnb, w, bfull)


def sgu_bwd(z, d_cat, lng, lnb, w, bfull):
    def body(u_ref, v_ref, g_ref, d_ref, lng_ref, lnb_ref, w_ref, b_ref,
             du_ref, dv_ref, dg_ref, dw_ref, db_ref, dlg_ref, dlb_ref):
        @pl.when(pl.program_id(0) == 0)
        def _():
            dw_ref[...] = jnp.zeros_like(dw_ref)
            db_ref[...] = jnp.zeros_like(db_ref)
            dlg_ref[...] = jnp.zeros_like(dlg_ref)
            dlb_ref[...] = jnp.zeros_like(dlb_ref)

        lng = lng_ref[...]
        xhat, rs, vn, tril, ws = _sgu_common(v_ref[...], lng, lnb_ref[...], w_ref)
        lane = lax.broadcasted_iota(jnp.int32, (1, 128), 1)
        db = jnp.zeros((CH, 128), F32)
        dvn_parts = []
        for g in range(4):
            sl = slice(g * 256, (g + 1) * 256)
            mixed = _nn(ws[g], vn[:, sl]) + b_ref[:, sl]
            gc = g_ref[:, sl]
            sg = _sig(gc)
            u = u_ref[:, sl]
            dc = d_ref[:, sl]
            du_ref[:, sl] = dc * mixed * (gc * sg)
            dg_ref[:, sl] = dc * u * mixed * (sg * (1.0 + gc * (1.0 - sg)))
            dmx = dc * u * (gc * sg)
            db = db + jnp.where(lane == g, jnp.sum(dmx, axis=-1, keepdims=True), 0.0)
            dmb = dmx.astype(BF16)
            dw_ref[g] += jnp.where(tril, _nt(dmb, vn[:, sl]), 0.0)
            dvn_parts.append(_tn(ws[g], dmb))
        db_ref[...] += db
        dvn = jnp.concatenate(dvn_parts, axis=1)
        dlb_ref[...] += jnp.sum(dvn, axis=0, keepdims=True)
        dlg_ref[...] += jnp.sum(dvn * xhat, axis=0, keepdims=True)
        dxh = dvn * lng
        dv_ref[...] = rs * (dxh - jnp.mean(dxh, axis=-1, keepdims=True)
                            - xhat * jnp.mean(dxh * xhat, axis=-1, keepdims=True))

    row = pl.BlockSpec((CH, D), lambda c: (c, 0))
    return pl.pallas_call(
        body, name="sgu_bwd", grid=(S // CH,),
        in_specs=[_zspec(0), _zspec(1), _zspec(2), row, _full((1, D)), _full((1, D)),
                  _full((4, CH, CH)), _full((CH, D))],
        out_specs=[row, row, row, _full((4, CH, CH)), _full((CH, 128)), _full((1, D)), _full((1, D))],
        out_shape=[jax.ShapeDtypeStruct((S, D), F32)] * 3
        + [jax.ShapeDtypeStruct((4, CH, CH), F32), jax.ShapeDtypeStruct((CH, 128), F32),
           jax.ShapeDtypeStruct((1, D), F32), jax.ShapeDtypeStruct((1, D), F32)],
        compiler_params=_cparams(("arbitrary",)),
    )(z, z, z, d_cat, lng, lnb, w, bfull)


TB = 256


def _cmul(ar, ai, br, bi):
    return ar * br - ai * bi, ar * bi + ai * br


def _scan_consts(ar, ai, reverse):
    a2 = _cmul(ar, ai, ar, ai)
    a4 = _cmul(*a2, *a2)
    row = lax.broadcasted_iota(jnp.int32, (8, NS), 0)

    def masked(k, p):
        keep = (row < 8 - k) if reverse else (row >= k)
        return jnp.where(keep, p[0], 0.0), jnp.where(keep, p[1], 0.0)
    pr = jnp.zeros((8, NS), F32)
    pi = jnp.zeros((8, NS), F32)
    cr, ci = ar, ai
    for r in range(8):
        sel = row == (7 - r if reverse else r)
        pr = jnp.where(sel, cr, pr)
        pi = jnp.where(sel, ci, pi)
        cr, ci = _cmul(cr, ci, ar, ai)
    return (masked(1, (ar, ai)), masked(2, a2), masked(4, a4)), (pr, pi), row


def scan_fwd(bu, abr, abi):
    def body(bu_ref, ar_ref, ai_ref, h_ref, car, cai):
        @pl.when(pl.program_id(0) == 0)
        def _():
            car[...] = jnp.zeros_like(car)
            cai[...] = jnp.zeros_like(cai)

        pows, (pr, pi), row = _scan_consts(ar_ref[...], ai_ref[...], False)

        def tile(t, carry):
            c_r, c_i = carry
            rows = pl.ds(pl.multiple_of(t * 8, 8), 8)
            xr = bu_ref[rows, 0:NS]
            xi = bu_ref[rows, NS:2 * NS]
            for k, (kr, ki) in zip((1, 2, 4), pows):
                sr = pltpu.roll(xr, k, 0)
                si = pltpu.roll(xi, k, 0)
                xr, xi = xr + kr * sr - ki * si, xi + kr * si + ki * sr
            xr, xi = xr + pr * c_r - pi * c_i, xi + pr * c_i + pi * c_r
            h_ref[rows, 0:NS] = xr
            h_ref[rows, NS:2 * NS] = xi
            return (jnp.broadcast_to(xr[7:8, :], (8, NS)), jnp.broadcast_to(xi[7:8, :], (8, NS)))

        c_r, c_i = lax.fori_loop(0, TB // 8, tile, (car[...], cai[...]))
        car[...] = c_r
        cai[...] = c_i

    return pl.pallas_call(
        body, name="s5_scan_fwd", grid=(S // TB,),
        in_specs=[pl.BlockSpec((TB, 2 * NS), lambda i: (i, 0)),
                  pl.BlockSpec((1, NS), lambda i: (0, 0)), pl.BlockSpec((1, NS), lambda i: (0, 0))],
        out_specs=pl.BlockSpec((TB, 2 * NS), lambda i: (i, 0)),
        out_shape=jax.ShapeDtypeStruct((S, 2 * NS), F32),
        scratch_shapes=[pltpu.VMEM((8, NS), F32), pltpu.VMEM((8, NS), F32)],
        compiler_params=_cparams(("arbitrary",)),
    )(bu, abr, abi)


def scan_bwd(eta, h, abr, abi):
    nt = S // TB

    def body(e_ref, h_ref, ar_ref, ai_ref, l_ref, da_ref, car, cai):
        @pl.when(pl.program_id(0) == 0)
        def _():
            car[...] = jnp.zeros_like(car)
            cai[...] = jnp.zeros_like(cai)
            da_ref[...] = jnp.zeros_like(da_ref)

        pows, (pr, pi), row = _scan_consts(ar_ref[...], -ai_ref[...], True)

        def tile(tt, carry):
            c_r, c_i, acr, aci = carry
            t = TB // 8 - 1 - tt
            rows = pl.ds(pl.multiple_of(t * 8, 8), 8)
            xr = e_ref[rows, 0:NS]
            xi = e_ref[rows, NS:2 * NS]
            for k, (kr, ki) in zip((1, 2, 4), pows):
                sr = pltpu.roll(xr, 8 - k, 0)
                si = pltpu.roll(xi, 8 - k, 0)
                xr, xi = xr + kr * sr - ki * si, xi + kr * si + ki * sr
            xr, xi = xr + pr * c_r - pi * c_i, xi + pr * c_i + pi * c_r
            l_ref[rows, 0:NS] = xr
            l_ref[rows, NS:2 * NS] = xi
            nr = jnp.where(row < 7, pltpu.roll(xr, 7, 0), c_r)
            ni = jnp.where(row < 7, pltpu.roll(xi, 7, 0), c_i)
            hr = h_ref[rows, 0:NS]
            hi = h_ref[rows, NS:2 * NS]
            acr = acr + hr * nr + hi * ni
            aci = aci + hr * ni - hi * nr
            return (jnp.broadcast_to(xr[0:1, :], (8, NS)), jnp.broadcast_to(xi[0:1, :], (8, NS)), acr, aci)

        zero = jnp.zeros((8, NS), F32)
        c_r, c_i, acr, aci = lax.fori_loop(0, TB // 8, tile, (car[...], cai[...], zero, zero))
        car[...] = c_r
        cai[...] = c_i
        da_ref[:, 0:NS] += acr
        da_ref[:, NS:2 * NS] += aci

    rev = pl.BlockSpec((TB, 2 * NS), lambda i: (nt - 1 - i, 0))
    return pl.pallas_call(
        body, name="s5_scan_bwd", grid=(nt,),
        in_specs=[rev, rev, pl.BlockSpec((1, NS), lambda i: (0, 0)), pl.BlockSpec((1, NS), lambda i: (0, 0))],
        out_specs=[rev, pl.BlockSpec((8, 2 * NS), lambda i: (0, 0))],
        out_shape=[jax.ShapeDtypeStruct((S, 2 * NS), F32), jax.ShapeDtypeStruct((8, 2 * NS), F32)],
        scratch_shapes=[pltpu.VMEM((8, NS), F32), pltpu.VMEM((8, NS), F32)],
        compiler_params=_cparams(("arbitrary",)),
    )(eta, h, abr, abi)


GC = 0.7978845608028654
GA = 0.044715


def s5_post(hc, z, dskip):
    def fn(hv, xd, dv):
        y = hv + dv * xd
        return y, 0.5 * y * (1.0 + jnp.tanh(GC * (y + GA * y * y * y)))
    return rw(fn, [(hc, 0, 512), (z, 3072, 512)], [(512, F32), (512, BF16)], "s5_post", S, consts=[dskip])


def s5_post_bwd(dyg, ypre, z, dskip):
    def fn(dy, y, xd, dv):
        th = jnp.tanh(GC * (y + GA * y * y * y))
        dg = 0.5 * (1.0 + th) + 0.5 * y * (1.0 - th * th) * GC * (1.0 + 3.0 * GA * y * y)
        dyp = dy * dg
        return dyp, dyp * dv, jnp.sum(dyp * xd, axis=0, keepdims=True)
    return rw(fn, [(dyg, 0, 512), (ypre, 0, 512), (z, 3072, 512)], [(512, BF16), (512, F32)],
              "s5_post_bwd", S, consts=[dskip], accs=[(1, 512)])


def glu_fwd(t, z, c_out):
    def fn(t1, t2, gd, co):
        return (jnp.concatenate([co, (t1 * _sig(t2) * (gd * _sig(gd))).astype(BF16)], axis=1),)
    return rw(fn, [(t, 0, 512), (t, 512, 512), (z, 3584, 512), (c_out, 0, D)], [(D + 512, BF16)], "glu_fwd", S)[0]


def glu_bwd(t, z, d_cat):
    def fn(t1, t2, gd, dd):
        s2, sg = _sig(t2), _sig(gd)
        sl = gd * sg
        return (jnp.concatenate([dd * s2 * sl, dd * t1 * s2 * (1.0 - s2) * sl], axis=1),
                dd * t1 * s2 * (sg * (1.0 + gd * (1.0 - sg))))
    return rw(fn, [(t, 0, 512), (t, 512, 512), (z, 3584, 512), (d_cat, 1024, 512)],
              [(D, BF16), (512, F32)], "glu_bwd", S)


def assemble_dz_odd(du, dv, dgc, dxd, dgd):
    def body(a, b, c, d, e, o_ref):
        o_ref[:, 0:D] = a[...].astype(BF16)
        o_ref[:, D:2 * D] = b[...].astype(BF16)
        o_ref[:, 2 * D:3 * D] = c[...].astype(BF16)
        o_ref[:, 3 * D:3 * D + 512] = d[...].astype(BF16)
        o_ref[:, 3 * D + 512:4 * D] = e[...].astype(BF16)
    tr = 256
    blk = pl.BlockSpec((tr, D), lambda i: (i, 0))
    half = pl.BlockSpec((tr, 512), lambda i: (i, 0))
    return pl.pallas_call(
        body, name="assemble_dz_odd", grid=(S // tr,), in_specs=[blk, blk, blk, half, half],
        out_specs=pl.BlockSpec((tr, 4 * D), lambda i: (i, 0)),
        out_shape=jax.ShapeDtypeStruct((S, 4 * D), BF16),
        compiler_params=_cparams(("parallel",)),
    )(du, dv, dgc, dxd, dgd)


TQ = 256


def _xattn_probs(qh, kh):
    s = _nt(qh, kh) * 0.0625
    p = jnp.exp(s - jnp.max(s, axis=-1, keepdims=True))
    return p / jnp.sum(p, axis=-1, keepdims=True)


def xattn_fwd(q, kv):
    def body(q_ref, kv_ref, o_ref):
        for h in range(4):
            sl = slice(h * 256, (h + 1) * 256)
            p = _xattn_probs(q_ref[:, sl].astype(BF16), kv_ref[:, sl].astype(BF16))
            vh = kv_ref[:, D + h * 256:D + (h + 1) * 256].astype(BF16)
            o_ref[:, sl] = _nn(p.astype(BF16), vh).astype(o_ref.dtype)

    return pl.pallas_call(
        body, name="xattn_fwd", grid=(S // TQ,),
        in_specs=[pl.BlockSpec((TQ, D), lambda i: (i, 0)), pl.BlockSpec((MEM, 2 * D), lambda i: (0, 0))],
        out_specs=pl.BlockSpec((TQ, D), lambda i: (i, 0)),
        out_shape=jax.ShapeDtypeStruct((S, D), BF16),
        compiler_params=_cparams(("parallel",)),
    )(q, kv)


def xattn_bwd(q, kv, d_o):
    def body(q_ref, kv_ref, do_ref, dq_ref, dkv_ref):
        @pl.when(pl.program_id(0) == 0)
        def _():
            dkv_ref[...] = jnp.zeros_like(dkv_ref)

        for h in range(4):
            sl = slice(h * 256, (h + 1) * 256)
            vs = slice(D + h * 256, D + (h + 1) * 256)
            qh = q_ref[:, sl].astype(BF16)
            kh = kv_ref[:, sl].astype(BF16)
            vh = kv_ref[:, vs].astype(BF16)
            doh = do_ref[:, sl].astype(BF16)
            p = _xattn_probs(qh, kh)
            dp = _nt(doh, vh)
            ds = (p * (dp - jnp.sum(p * dp, axis=-1, keepdims=True)) * 0.0625).astype(BF16)
            dq_ref[:, sl] = _nn(ds, kh).astype(dq_ref.dtype)
            dkv_ref[:, sl] += _tn(ds, qh)
            dkv_ref[:, vs] += _tn(p.astype(BF16), doh)

    return pl.pallas_call(
        body, name="xattn_bwd", grid=(S // TQ,),
        in_specs=[pl.BlockSpec((TQ, D), lambda i: (i, 0)), pl.BlockSpec((MEM, 2 * D), lambda i: (0, 0)),
                  pl.BlockSpec((TQ, D), lambda i: (i, 0))],
        out_specs=[pl.BlockSpec((TQ, D), lambda i: (i, 0)), pl.BlockSpec((MEM, 2 * D), lambda i: (0, 0))],
        out_shape=[jax.ShapeDtypeStruct((S, D), BF16), jax.ShapeDtypeStruct((MEM, 2 * D), F32)],
        compiler_params=_cparams(("arbitrary",)),
    )(q, kv, d_o)


def _s5_disc(a_re, a_im, log_dt, b_re, b_im):
    dt = jnp.exp(log_dt)[:, None]
    mag = jnp.exp(dt * a_re)
    abr = mag * jnp.cos(dt * a_im)
    abi = mag * jnp.sin(dt * a_im)
    nr, ni = abr - 1.0, abi
    inv = 1.0 / (a_re * a_re + a_im * a_im)
    cr = (nr * a_re + ni * a_im) * inv
    ci = (ni * a_re - nr * a_im) * inv
    bbr = cr[..., None] * b_re - ci[..., None] * b_im
    bbi = cr[..., None] * b_im + ci[..., None] * b_re
    return abr, abi, bbr, bbi


VM = pl.BlockSpec(memory_space=pltpu.VMEM)


def s5_embed(bt_re, bt_im, ct_re, ct_im):
    def body(br, bi, cr, ci, b_ref, c_ref):
        b_ref[...] = jnp.zeros_like(b_ref)
        c_ref[...] = jnp.zeros_like(c_ref)
        for g in range(NG):
            rows, cols = slice(g * NH, (g + 1) * NH), slice(g * NP, (g + 1) * NP)
            b_ref[rows, cols] = br[g]
            b_ref[rows, NS + g * NP:NS + (g + 1) * NP] = bi[g]
            c_ref[cols, rows] = cr[g]
            c_ref[NS + g * NP:NS + (g + 1) * NP, rows] = -ci[g]

    return pl.pallas_call(
        body, name="s5_embed", in_specs=[VM] * 4, out_specs=[VM] * 2,
        out_shape=[jax.ShapeDtypeStruct((NG * NH, 2 * NS), F32), jax.ShapeDtypeStruct((2 * NS, NG * NH), F32)],
        compiler_params=pltpu.CompilerParams(vmem_limit_bytes=VMEM_LIMIT),
    )(bt_re, bt_im, ct_re, ct_im)


def s5_extract(gb, gc):
    def body(gb_ref, gc_ref, br, bi, cr, ci):
        for g in range(NG):
            rows, cols = slice(g * NH, (g + 1) * NH), slice(g * NP, (g + 1) * NP)
            br[g] = gb_ref[rows, cols]
            bi[g] = gb_ref[rows, NS + g * NP:NS + (g + 1) * NP]
            cr[g] = gc_ref[cols, rows]
            ci[g] = -gc_ref[NS + g * NP:NS + (g + 1) * NP, rows]

    return pl.pallas_call(
        body, name="s5_extract", in_specs=[VM] * 2, out_specs=[VM] * 4,
        out_shape=[jax.ShapeDtypeStruct((NG, NH, NP), F32)] * 2 + [jax.ShapeDtypeStruct((NG, NP, NH), F32)] * 2,
        compiler_params=pltpu.CompilerParams(vmem_limit_bytes=VMEM_LIMIT),
    )(gb, gc)


HC, HS = NG * NH // 2, NS // 2
TS = 1024


def s5_to_states(x, w, mode, name, z_off=0):
    if mode == "nn":
        wb, wm = (HC, HS), lambda i, j, kk: (j % 2, j)
    else:
        wb, wm = (HS, HC), lambda i, j, kk: (j, j % 2)
    return mm_band(x, w, mode, name, (S // TS, 4, 1), ((TS, HC), wb, (TS, HS)),
                   (lambda i, j, kk: (i, z_off + j % 2), wm, lambda i, j, kk: (i, j)), (S, 2 * NS))


def s5_to_channels(x, w, mode, name, add=None):
    if mode == "nn":
        wb, wm = (HS, HC), lambda i, j, kk: (j + 2 * kk, j)
    else:
        wb, wm = (HC, HS), lambda i, j, kk: (j, j + 2 * kk)
    return mm_band(x, w, mode, name, (S // TS, 2, 2), ((TS, HS), wb, (TS, HC)),
                   (lambda i, j, kk: (i, j + 2 * kk), wm, lambda i, j, kk: (i, j)), (S, NG * NH), add=add)


def s5_outer(a, b, name, states_first, z_off=0):
    if states_first:
        return mm_band(a, b, "tn", name, (4, 1, 1), ((S, HS), (S, HC), (HS, HC)),
                       (lambda i, j, kk: (0, i), lambda i, j, kk: (0, i % 2), lambda i, j, kk: (i, i % 2)),
                       (2 * NS, NG * NH))
    return mm_band(a, b, "tn", name, (1, 4, 1), ((S, HC), (S, HS), (HC, HS)),
                   (lambda i, j, kk: (0, z_off + j % 2), lambda i, j, kk: (0, j), lambda i, j, kk: (j % 2, j)),
                   (NG * NH, 2 * NS))


def _fwd_even(i, x, P, W):
    hn = rms_fwd(x, P["norm_ab"][i:i + 1], "rms_ab_fwd")
    z = mm(m2(hn), W["w_in"], "nn", "in_ab")
    o, lse, cat = attn_fwd(z)
    if "more" in W:
        W.update(W.pop("more")(cat))
    cat = pool_fwd(z, W["pool_w"], P["pool_scale"][i:i + 1], cat)
    x_mid = mm(m2(cat), W["w_out"], "nn", "out_ab", add=m2(x))
    return x_mid, dict(x=x, hn=hn, z=z, o=o, lse=lse, cat=cat)


def _bwd_even(i, dx_mid, sv, P, W, G, GW):
    z = sv["z"]
    d_cat = mm(m2(dx_mid), W["w_out"], "nt", "out_ab_dx")
    GW["w_out"] = mm(m2(sv["cat"]), m2(dx_mid), "tn", "out_ab_dw").reshape(4, 512, D)
    dq, dk, dv, dga = attn_bwd(z, d_cat, sv["o"], sv["lse"])
    dvb, dgb, dpw, dps = pool_bwd(z, d_cat, W["pool_w"], P["pool_scale"][i:i + 1])
    GW["pool_w"] = dpw.reshape(4, 4, 64, 256).transpose(1, 0, 2, 3).reshape(4, 256, 256)
    G["pool_scale"][i] = dps[0]
    d_z = assemble_dz_even((dq, dk, dv, dga, dvb, dgb))
    d_hn = mm(m2(d_z), W["w_in"], "nt", "in_ab_dx")
    GW["w_in"] = mm(m2(sv["hn"]), m2(d_z), "tn", "in_ab_dw", out=outcs(D, 1536))
    return d_hn, P["norm_ab"][i:i + 1], "norm_ab", "rms_ab_bwd"


def _fwd_odd(i, x, P, W):
    hn = rms_fwd(x, P["norm_cd"][i:i + 1], "rms_cd_fwd")
    z = mm(m2(hn), W["w_in"], "nn", "in_cd")
    bfull = jnp.repeat(P["sgu_b"][i].T, 256, axis=1)
    c_out = sgu_fwd(z, P["sgu_ln_g"][i:i + 1], P["sgu_ln_b"][i:i + 1], P["sgu_w"][i], bfull)
    disc, disc_vjp = jax.vjp(_s5_disc, P["s5_a_re"][i], P["s5_a_im"][i], P["s5_log_dt"][i],
                             P["s5_b_re"][i], P["s5_b_im"][i])
    abr, abi, bbr, bbi = disc
    bbd, cbd = s5_embed(bbr.transpose(0, 2, 1), bbi.transpose(0, 2, 1),
                        P["s5_c_re"][i].transpose(0, 2, 1), P["s5_c_im"][i].transpose(0, 2, 1))
    abr, abi = abr.reshape(1, NS), abi.reshape(1, NS)
    bu = s5_to_states(z, bbd, "nn", "s5_bu", z_off=3072 // HC)
    h = scan_fwd(bu, abr, abi)
    hc = s5_to_channels(h, cbd, "nn", "s5_hc")
    dskip = P["s5_d"][i:i + 1]
    ypre, yg = s5_post(hc, z, dskip)
    if "more" in W:
        W.update(W.pop("more")(yg))
    w12 = W["w12"]
    t = mm(m2(yg), m2(w12), "nn", "glu_t")
    cat = glu_fwd(t, z, c_out)
    x_mid = mm(m2(cat), W["w_out"], "nn", "out_cd", add=m2(x))
    return x_mid, dict(x=x, hn=hn, z=z, bfull=bfull, disc_vjp=disc_vjp, bbd=bbd, cbd=cbd, abr=abr,
                       abi=abi, h=h, ypre=ypre, yg=yg, w12=w12, t=t, cat=cat, dskip=dskip)


def _bwd_odd(i, dx_mid, sv, P, W, G, GW):
    z = sv["z"]
    d_cat = mm(m2(dx_mid), W["w_out"], "nt", "out_cd_dx")
    GW["w_out"] = mm(m2(sv["cat"]), m2(dx_mid), "tn", "out_cd_dw").reshape(4, 384, D)
    du, dv, dgc, dws, dbs, dlg, dlb = sgu_bwd(z, d_cat, P["sgu_ln_g"][i:i + 1], P["sgu_ln_b"][i:i + 1],
                                               P["sgu_w"][i], sv["bfull"])
    G["sgu_w"][i], G["sgu_b"][i] = dws, dbs[:, :4].T
    G["sgu_ln_g"][i], G["sgu_ln_b"][i] = dlg[0], dlb[0]
    dt, dgd = glu_bwd(sv["t"], z, d_cat)
    gw12 = mm(m2(sv["yg"]), m2(dt), "tn", "glu_dw")
    GW["glu_w1"] = gw12[:, :512].reshape(4, 128, 512)
    GW["glu_w2"] = gw12[:, 512:].reshape(4, 128, 512)
    dyg = mm(m2(dt), m2(sv["w12"]), "nt", "glu_dx")
    dypre, dxd1, dd = s5_post_bwd(dyg, sv["ypre"], z, sv["dskip"])
    G["s5_d"][i] = dd[0]
    gcbd = s5_outer(sv["h"], dypre, "s5_dc", states_first=True)
    eta = s5_to_states(dypre, sv["cbd"], "nt", "s5_eta")
    lam, dacc = scan_bwd(eta, sv["h"], sv["abr"], sv["abi"])
    gbbd = s5_outer(z, lam, "s5_db", states_first=False, z_off=3072 // HC)
    dxd = s5_to_channels(lam, sv["bbd"], "nt", "s5_dx", add=dxd1)
    dacc = jnp.sum(dacc, axis=0)
    dbt_re, dbt_im, dct_re, dct_im = s5_extract(gbbd, gcbd)
    G["s5_c_re"][i], G["s5_c_im"][i] = dct_re.transpose(0, 2, 1), dct_im.transpose(0, 2, 1)
    d_bbr, d_bbi = dbt_re.transpose(0, 2, 1), dbt_im.transpose(0, 2, 1)
    (G["s5_a_re"][i], G["s5_a_im"][i], G["s5_log_dt"][i], G["s5_b_re"][i], G["s5_b_im"][i]) = sv["disc_vjp"](
        (dacc[:NS].reshape(NG, NP), dacc[NS:].reshape(NG, NP), d_bbr, d_bbi))
    d_z = assemble_dz_odd(du, dv, dgc, dxd, dgd)
    d_hn = mm(m2(d_z), W["w_in"], "nt", "in_cd_dx")
    GW["w_in"] = mm(m2(sv["hn"]), m2(d_z), "tn", "in_cd_dw", out=outcs(D, 1024))
    return d_hn, P["norm_cd"][i:i + 1], "norm_cd", "rms_cd_bwd"


def _fwd_x(l, x, mem_n, P, W):
    hx = rms_fwd(x, P["norm_x"][l:l + 1], "rms_x_fwd")
    q = mm(m2(hx), W["w_xq"], "nn", "xq", out_dtype=BF16)
    kv = mm(m2(mem_n), W["w_xkv"], "nn", "xkv", out_dtype=BF16)
    ox = xattn_fwd(q, kv)
    x_out = mm(m2(ox), W["w_xo"], "nn", "xo", add=m2(x))
    return x_out, dict(x=x, hx=hx, q=q, kv=kv, ox=ox)


def _bwd_x(l, dx_out, sv, mem_n, d_memn, P, W, G, GW):
    d_ox = mm(m2(dx_out), W["w_xo"], "nt", "xo_dx", out_dtype=BF16)
    GW["w_xo"] = mm(m2(sv["ox"]), m2(dx_out), "tn", "xo_dw").reshape(4, 256, D)
    dq, dkv = xattn_bwd(sv["q"], sv["kv"], d_ox)
    GW["w_xq"] = mm(m2(sv["hx"]), m2(dq), "tn", "xq_dw").reshape(4, 256, D)
    d_hx = mm(m2(dq), W["w_xq"], "nt", "xq_dx")
    GW["w_xkv"] = mm(m2(mem_n), m2(dkv), "tn", "xkv_dw", out=outcs(D, 512))
    d_memn = mm(m2(dkv), W["w_xkv"], "nt", "xkv_dx", add=None if d_memn is None else m2(d_memn))
    dx, dg = rms_bwd(sv["x"], d_hx, dx_out, P["norm_x"][l:l + 1], "rms_x_bwd")
    G["norm_x"][l] = dg[0]
    return dx, d_memn


SMALL_LAYERS = (("norm_ab", 2), ("pool_scale", 2), ("norm_cd", 2), ("sgu_ln_g", 2), ("sgu_ln_b", 2), ("sgu_w", 2),
                ("sgu_b", 2), ("s5_a_re", 2), ("s5_a_im", 2), ("s5_log_dt", 2), ("s5_b_re", 2), ("s5_b_im", 2),
                ("s5_c_re", 2), ("s5_c_im", 2), ("s5_d", 2), ("norm_x", 4))


def local_step(x, mem, tgt, P, weights_of, grads_done):
    G = {k: [None] * n for k, n in SMALL_LAYERS}
    mem_g = P["mem_norm"].reshape(1, D)
    mem_n = rms_fwd(mem, mem_g, "rms_mem_fwd")
    saved = []
    for layer in range(4):
        i = layer // 2
        W = weights_of(layer, x)
        x, sv_m = (_fwd_even if layer % 2 == 0 else _fwd_odd)(i, x, P, W)
        x, sv_x = _fwd_x(layer, x, mem_n, P, W)
        saved.append((sv_m, sv_x, W))
    dx, loss, dgf = final_loss(x, tgt, P["final_norm"].reshape(1, D))
    G["final_norm"] = dgf[0]
    d_memn = None
    for layer in reversed(range(4)):
        i = layer // 2
        sv_m, sv_x, W = saved[layer]
        GW = {}
        dx_mid, d_memn = _bwd_x(layer, dx, sv_x, mem_n, d_memn, P, W, G, GW)
        d_hn, g, key, name = (_bwd_even if layer % 2 == 0 else _bwd_odd)(i, dx_mid, sv_m, P, W, G, GW)
        token = grads_done(layer, GW)
        if token is not None:
            g = g + token
        dx, dg = rms_bwd(sv_m["x"], d_hn, dx_mid, g, name)
        G[key][i] = dg[0]
    _, dgm = rms_bwd(mem, d_memn, d_memn, mem_g, "rms_mem_bwd")
    G["mem_norm"] = dgm[0]
    return loss, dx, G


ANY = pl.BlockSpec(memory_space=pl.ANY)


def _place():
    x, y, c = lax.axis_index("x"), lax.axis_index("y"), lax.axis_index("c")
    chips = [(1 - x, y), (x, 1 - y), (1 - x, 1 - y)]
    return x, y, c, 2 * x + y, (x, y, 1 - c), chips


def _remote(src, dst, send, recv, k, dev):
    return pltpu.make_async_remote_copy(src_ref=src, dst_ref=dst, send_sem=send.at[k], recv_sem=recv.at[k],
                                        device_id=dev, device_id_type=MESHID)


HBM = pl.BlockSpec(memory_space=pltpu.HBM)
SEM = pl.BlockSpec(memory_space=pltpu.SEMAPHORE)
EFFECT = pltpu.SideEffectType.DATAFLOW_SIDE_EFFECTING


def _hbm(t):
    return pltpu.with_memory_space_constraint(t, pltpu.HBM)


def allgather_sync(shards):
    n = len(shards)

    def body(*refs):
        ins, outs = refs[:n], refs[n:2 * n]
        token, send, recv = refs[2 * n:]
        x, y, c, jme, sib, chips = _place()
        first, passed = [], []
        for a in range(n):
            cp = _remote(ins[a], outs[a].at[jme], send, recv, a * 7 + 6, sib)
            cp.start()
            first.append(cp)
            for k, chip in enumerate(chips):
                cp = _remote(ins[a].at[c], outs[a].at[jme, c], send, recv, a * 7 + k, (*chip, c))
                cp.start()
                first.append(cp)
        for a in range(n):
            for k, chip in enumerate(chips):
                piece = outs[a].at[2 * chip[0] + chip[1], c]
                _remote(piece, piece, send, recv, a * 7 + k, (*chip, c)).wait_recv()
                fw = _remote(piece, piece, send, recv, a * 7 + 3 + k, sib)
                fw.start()
                passed.append(fw)
        for a in range(n):
            own = outs[a].at[jme]
            _remote(own, own, send, recv, a * 7 + 6, sib).wait_recv()
            for k, chip in enumerate(chips):
                piece = outs[a].at[2 * chip[0] + chip[1], 1 - c]
                _remote(piece, piece, send, recv, a * 7 + 3 + k, sib).wait_recv()
        for cp in first + passed:
            cp.wait_send()
        token[...] = jnp.zeros_like(token)

    res = pl.pallas_call(
        body, name="allgather_sync", in_specs=[ANY] * n,
        out_specs=[ANY] * n + [pl.BlockSpec(memory_space=pltpu.VMEM)],
        out_shape=[jax.ShapeDtypeStruct((4,) + s.shape, s.dtype) for s in shards] + [jax.ShapeDtypeStruct((8, 128), F32)],
        scratch_shapes=[pltpu.SemaphoreType.DMA((7 * n,)), pltpu.SemaphoreType.DMA((7 * n,))],
    )(*shards)
    return list(res[:n]), res[n]


def _gather_copies(ins, lands, send, recv):
    x, y, c, jme, sib, chips = _place()
    devs = [(*chip, c) for chip in chips] + [sib]
    return [_remote(ins[a], lands[a].at[jme], send, recv, a * 4 + k, dev)
            for a in range(len(ins)) for k, dev in enumerate(devs)]


def allgather_start(shards, after, name):
    n, na = len(shards), len(after)

    def body(*refs):
        ins, lands = refs[:n], refs[n:2 * n]
        send, recv = refs[2 * n + na], refs[2 * n + na + 1]
        token = refs[-1]
        for cp in _gather_copies(ins, lands, send, recv):
            cp.start()
        token[...] = jnp.zeros_like(token)

    res = pl.pallas_call(
        body, name=name,
        out_shape=(pltpu.SemaphoreType.DMA((4 * n,)), pltpu.SemaphoreType.DMA((4 * n,)),
                   *[pltpu.HBM(s.shape, s.dtype) for s in shards],
                   *[pltpu.HBM((4,) + s.shape, s.dtype) for s in shards],
                   jax.ShapeDtypeStruct((8, 128), F32)),
        in_specs=[HBM] * (2 * n) + [ANY] * na,
        out_specs=(SEM, SEM, *[HBM] * (2 * n), pl.BlockSpec(memory_space=pltpu.VMEM)),
        input_output_aliases={a: 2 + a for a in range(2 * n)},
        compiler_params=pltpu.CompilerParams(has_side_effects=EFFECT),
    )(*[_hbm(s) for s in shards], *[_hbm(lax.empty((4,) + s.shape, s.dtype)) for s in shards], *after)
    return res[0], res[1], list(res[2:2 + n]), list(res[2 + n:2 + 2 * n]), res[-1]


def allgather_wait(send, recv, shards, lands, after, name):
    n = len(shards)

    def body(*refs):
        ins, zones = refs[:n], refs[n:2 * n]
        send_r, recv_r = refs[2 * n], refs[2 * n + 1]
        x, y, c, jme, sib, chips = _place()
        slots = [2 * chip[0] + chip[1] for chip in chips] + [jme]
        for a in range(n):
            for k, slot in enumerate(slots):
                cp = _remote(ins[a], zones[a].at[slot], send_r, recv_r, a * 4 + k, sib)
                cp.wait_send()
                cp.wait_recv()

    res = pl.pallas_call(
        body, name=name,
        out_shape=tuple(pltpu.HBM(t.shape, t.dtype) for t in list(shards) + list(lands)),
        in_specs=[HBM] * (2 * n) + [SEM, SEM, ANY], out_specs=tuple([HBM] * (2 * n)),
        input_output_aliases={a: a for a in range(2 * n)},
        compiler_params=pltpu.CompilerParams(has_side_effects=EFFECT),
    )(*shards, *lands, send, recv, after)
    return list(res[n:])


def allgather_small(slab):
    def body(in_ref, out_ref, send, recv, lsem):
        x, y, c, jme, sib, chips = _place()
        loc = pltpu.make_async_copy(in_ref, out_ref.at[jme], lsem.at[0])
        loc.start()
        cps = [_remote(in_ref, out_ref.at[jme], send, recv, k, (*chip, c)) for k, chip in enumerate(chips)]
        for cp in cps:
            cp.start()
        for k, chip in enumerate(chips):
            piece = out_ref.at[2 * chip[0] + chip[1]]
            _remote(piece, piece, send, recv, k, (*chip, c)).wait_recv()
        for cp in cps:
            cp.wait_send()
        loc.wait()

    return pl.pallas_call(
        body, name="allgather_small", in_specs=[ANY], out_specs=ANY,
        out_shape=jax.ShapeDtypeStruct((4,) + slab.shape, slab.dtype),
        scratch_shapes=[pltpu.SemaphoreType.DMA((3,)), pltpu.SemaphoreType.DMA((3,)), pltpu.SemaphoreType.DMA((1,))],
    )(slab)


def allreduce_small(v):
    def body(v_ref, o_ref, r0, r1, r2, send, recv):
        x, y, c, jme, sib, chips = _place()
        peers = [sib, (1 - x, y, c), (x, 1 - y, c)]
        o_ref[...] = v_ref[...]
        for k, buf in enumerate((r0, r1, r2)):
            cp = _remote(o_ref, buf, send, recv, k, peers[k])
            cp.start()
            cp.wait()
            o_ref[...] = o_ref[...] + buf[...]

    vm = pl.BlockSpec(memory_space=pltpu.VMEM)
    return pl.pallas_call(
        body, name="allreduce_small", in_specs=[vm], out_specs=vm,
        out_shape=jax.ShapeDtypeStruct(v.shape, v.dtype),
        scratch_shapes=[pltpu.VMEM(v.shape, v.dtype)] * 3 + [pltpu.SemaphoreType.DMA((3,)), pltpu.SemaphoreType.DMA((3,))],
        compiler_params=pltpu.CompilerParams(vmem_limit_bytes=VMEM_LIMIT),
    )(v)


def _pair_copies(gs, lands, send, recv):
    x, y, c, jme, sib, chips = _place()
    return [_remote(gs[a].at[:, 1 - c], lands[a], send, recv, a, sib) for a in range(len(gs))]


def rs_pair_start(gs, name):
    n = len(gs)

    def body(*refs):
        ins, lands = refs[:n], refs[n:2 * n]
        send, recv = refs[2 * n], refs[2 * n + 1]
        token = refs[-1]
        for cp in _pair_copies(ins, lands, send, recv):
            cp.start()
        token[...] = jnp.zeros_like(token)

    shapes = [(4,) + g.shape[2:] for g in gs]
    res = pl.pallas_call(
        body, name=name,
        out_shape=(pltpu.SemaphoreType.DMA((n,)), pltpu.SemaphoreType.DMA((n,)),
                   *[pltpu.HBM(g.shape, g.dtype) for g in gs], *[pltpu.HBM(s, F32) for s in shapes],
                   jax.ShapeDtypeStruct((8, 128), F32)),
        in_specs=[HBM] * (2 * n), out_specs=(SEM, SEM, *[HBM] * (2 * n), pl.BlockSpec(memory_space=pltpu.VMEM)),
        input_output_aliases={a: 2 + a for a in range(2 * n)},
        compiler_params=pltpu.CompilerParams(has_side_effects=EFFECT),
    )(*[_hbm(g) for g in gs], *[_hbm(lax.empty(s, F32)) for s in shapes])
    return res[0], res[1], list(res[2:2 + n]), list(res[2 + n:2 + 2 * n]), res[-1]


def rs_pair_wait(send, recv, gs, lands, after, name):
    n = len(gs)

    def body(*refs):
        ins, zones = refs[:n], refs[n:2 * n]
        for cp in _pair_copies(ins, zones, refs[2 * n], refs[2 * n + 1]):
            cp.wait_send()
            cp.wait_recv()

    res = pl.pallas_call(
        body, name=name,
        out_shape=tuple(pltpu.HBM(t.shape, t.dtype) for t in list(gs) + list(lands)),
        in_specs=[HBM] * (2 * n) + [SEM, SEM, ANY], out_specs=tuple([HBM] * (2 * n)),
        input_output_aliases={a: a for a in range(2 * n)},
        compiler_params=pltpu.CompilerParams(has_side_effects=EFFECT),
    )(*gs, *lands, send, recv, after)
    return list(res[:n]), list(res[n:])


SUM_ROWS = 256


def rs_pair_sum(g4s, gots, cidx):
    n = len(g4s)
    tiles = [(min(g.shape[2], SUM_ROWS), g.shape[3]) for g in g4s]
    nts = [g.shape[2] // tr for g, (tr, _) in zip(g4s, tiles)]

    def at(a, s):
        s = jnp.minimum(s, 4 * nts[a] - 1)
        return s // nts[a], s % nts[a]

    def body(c_ref, *refs):
        for a in range(n):
            refs[2 * n + a][...] = (refs[a][...] + refs[n + a][...]).astype(BF16)

    in_specs = [pl.BlockSpec((None, None) + tiles[a], lambda s, cr, a=a: (at(a, s)[0], cr[0], at(a, s)[1], 0))
                for a in range(n)]
    in_specs += [pl.BlockSpec((None,) + tiles[a], lambda s, cr, a=a: (*at(a, s), 0)) for a in range(n)]
    return pl.pallas_call(
        body, name="rs_pair_sum",
        grid_spec=pltpu.PrefetchScalarGridSpec(
            num_scalar_prefetch=1, grid=(4 * max(nts),), in_specs=in_specs,
            out_specs=[pl.BlockSpec((None,) + tiles[a], lambda s, cr, a=a: (*at(a, s), 0)) for a in range(n)]),
        out_shape=[jax.ShapeDtypeStruct((4,) + g.shape[2:], BF16) for g in g4s],
        compiler_params=_cparams(("arbitrary",)),
    )(cidx, *g4s, *gots)


def _chip_copies(ps, lands, send, recv):
    x, y, c, jme, sib, chips = _place()
    return [_remote(ps[a].at[2 * chip[0] + chip[1]], lands[a].at[jme], send, recv, a * 3 + k, (*chip, c))
            for a in range(len(ps)) for k, chip in enumerate(chips)]


def rs_chip_start(ps, name):
    n = len(ps)

    def body(*refs):
        ins, lands = refs[:n], refs[n:2 * n]
        send, recv = refs[2 * n], refs[2 * n + 1]
        token = refs[-1]
        for cp in _chip_copies(ins, lands, send, recv):
            cp.start()
        token[...] = jnp.zeros_like(token)

    res = pl.pallas_call(
        body, name=name,
        out_shape=(pltpu.SemaphoreType.DMA((3 * n,)), pltpu.SemaphoreType.DMA((3 * n,)),
                   *[pltpu.HBM(p.shape, p.dtype) for p in ps], *[pltpu.HBM(p.shape, p.dtype) for p in ps],
                   jax.ShapeDtypeStruct((8, 128), F32)),
        in_specs=[HBM] * (2 * n), out_specs=(SEM, SEM, *[HBM] * (2 * n), pl.BlockSpec(memory_space=pltpu.VMEM)),
        input_output_aliases={a: 2 + a for a in range(2 * n)},
        compiler_params=pltpu.CompilerParams(has_side_effects=EFFECT),
    )(*[_hbm(p) for p in ps], *[_hbm(lax.empty(p.shape, p.dtype)) for p in ps])
    return res[0], res[1], list(res[2:2 + n]), list(res[2 + n:2 + 2 * n]), res[-1]


def rs_chip_wait(send, recv, ps, lands, after, name):
    n = len(ps)

    def body(*refs):
        ins, zones = refs[:n], refs[n:2 * n]
        send_r, recv_r = refs[2 * n], refs[2 * n + 1]
        x, y, c, jme, sib, chips = _place()
        for a in range(n):
            for k, chip in enumerate(chips):
                jt = 2 * chip[0] + chip[1]
                cp = _remote(ins[a].at[jt], zones[a].at[jt], send_r, recv_r, a * 3 + k, (*chip, c))
                cp.wait_send()
                cp.wait_recv()

    res = pl.pallas_call(
        body, name=name,
        out_shape=tuple(pltpu.HBM(p.shape, p.dtype) for p in list(ps) + list(lands)),
        in_specs=[HBM] * (2 * n) + [SEM, SEM] + [ANY] * len(after), out_specs=tuple([HBM] * (2 * n)),
        input_output_aliases={a: a for a in range(2 * n)},
        compiler_params=pltpu.CompilerParams(has_side_effects=EFFECT),
    )(*ps, *lands, send, recv, *after)
    return list(res[n:])


def rs_chip_sum(qs, ps, ls, accs, layers, jc):
    n = len(qs)
    tiles = [(min(q.shape[1], SUM_ROWS), q.shape[2]) for q in qs]
    nts = [q.shape[1] // tr for q, (tr, _) in zip(qs, tiles)]

    def at(a, s):
        return jnp.minimum(s, nts[a] - 1)

    def body(jc_ref, *refs):
        jme = jc_ref[0]
        for a in range(n):
            q_ref, p_ref, o_ref = refs[a], refs[n + a], refs[len(refs) - n + a]
            own = p_ref[...].astype(F32)
            v = [jnp.where(jme == j, own, q_ref[j].astype(F32)) for j in range(4)]
            o_ref[...] = ((v[0] + v[1]) + v[2]) + v[3]

    in_specs = [pl.BlockSpec((4,) + tiles[a], lambda s, jr, a=a: (0, at(a, s), 0)) for a in range(n)]
    in_specs += [pl.BlockSpec((None,) + tiles[a], lambda s, jr, a=a: (jr[0], at(a, s), 0)) for a in range(n)]
    args, aliases = [jc, *qs, *ps], {}
    for a in range(n):
        if accs[a] is not None:
            aliases[len(args)] = a
            in_specs.append(ANY)
            args.append(accs[a])
    return pl.pallas_call(
        body, name="rs_chip_sum",
        grid_spec=pltpu.PrefetchScalarGridSpec(
            num_scalar_prefetch=1, grid=(max(nts),), in_specs=in_specs,
            out_specs=[pl.BlockSpec((None, None) + tiles[a], lambda s, jr, a=a: (ls[a], jr[1], at(a, s), 0))
                       for a in range(n)]),
        out_shape=[jax.ShapeDtypeStruct((layers[a], 2) + qs[a].shape[1:], F32) for a in range(n)],
        input_output_aliases=aliases,
        compiler_params=_cparams(("arbitrary",)),
    )(*args)


def rs_pair_gather(rs):
    n = len(rs)

    def body(*refs):
        outs = refs[n:2 * n]
        send, recv = refs[2 * n:]
        x, y, c, jme, sib, chips = _place()
        cps = [_remote(outs[a].at[:, c], outs[a].at[:, c], send, recv, a, sib) for a in range(n)]
        for cp in cps:
            cp.start()
        for a in range(n):
            slot = outs[a].at[:, 1 - c]
            _remote(slot, slot, send, recv, a, sib).wait_recv()
        for cp in cps:
            cp.wait_send()

    return pl.pallas_call(
        body, name="rs_pair_gather", in_specs=[ANY] * n, out_specs=[ANY] * n,
        out_shape=[jax.ShapeDtypeStruct(r.shape, r.dtype) for r in rs],
        input_output_aliases={a: a for a in range(n)},
        scratch_shapes=[pltpu.SemaphoreType.DMA((n,)), pltpu.SemaphoreType.DMA((n,))],
    )(*rs)


def _adamw_math(w, g, m, v):
    m = B1 * m + (1.0 - B1) * g
    v = B2 * v + (1.0 - B2) * (g * g)
    m_hat = m / (1.0 - B1 ** STEP)
    v_hat = v / (1.0 - B2 ** STEP)
    return -LR * (m_hat / (jnp.sqrt(v_hat) + AEPS) + WD * w), m, v


def adamw(w, g, m, v, name, with_grad=False):
    rows, cols = w.shape
    tr = 256 if rows % 256 == 0 else rows
    fn = (lambda wv, gv, mv, vv: (gv,) + _adamw_math(wv, gv, mv, vv)) if with_grad else _adamw_math
    return rw(fn, [(a, 0, cols) for a in (w, g, m, v)], [(cols, F32)] * (4 if with_grad else 3), name, rows, tr=tr)


def adamw_small(ws, gs, ms, vs):
    n = len(ws)

    def body(*refs):
        for a in range(n):
            res = _adamw_math(*[refs[k * n + a][...] for k in range(4)])
            for k in range(3):
                refs[(4 + k) * n + a][...] = res[k]

    res = pl.pallas_call(
        body, name="adamw_small", in_specs=[VM] * (4 * n), out_specs=[VM] * (3 * n),
        out_shape=[jax.ShapeDtypeStruct(w.shape, F32) for _ in range(3) for w in ws],
        compiler_params=pltpu.CompilerParams(vmem_limit_bytes=VMEM_LIMIT),
    )(*ws, *gs, *ms, *vs)
    return [(res[a], res[n + a], res[2 * n + a]) for a in range(n)]


WEIGHTS = ["norm_ab", "w_in_ab", "pool_w", "pool_scale", "w_out_ab", "norm_cd", "w_in_cd", "sgu_ln_g", "sgu_ln_b",
           "sgu_w", "sgu_b", "s5_a_re", "s5_a_im", "s5_log_dt", "s5_b_re", "s5_b_im", "s5_c_re", "s5_c_im", "s5_d",
           "glu_w1", "glu_w2", "w_out_cd", "norm_x", "w_xq", "w_xkv", "w_xo", "mem_norm", "final_norm"]
INPUTS = ["x", "mem"] + WEIGHTS + ["loss_target"] + ["m_" + n for n in WEIGHTS] + ["v_" + n for n in WEIGHTS]
BIG = ["w_in_ab", "w_out_ab", "w_in_cd", "w_out_cd", "w_xq", "w_xkv", "w_xo", "glu_w1", "glu_w2", "pool_w"]
COL_SHARDED = ("w_in_ab", "w_in_cd", "w_xkv")
SMALL = [n for n in WEIGHTS if n not in BIG]
SMALL_SHARDED = {"norm_cd": 256, "sgu_ln_g": 256, "sgu_ln_b": 256, "s5_d": 128}
PACK = 256 * 128


def _pack(arrs):
    flat = jnp.concatenate([a.reshape(-1) for a in arrs])
    pad = (-flat.shape[0]) % PACK
    return jnp.concatenate([flat, jnp.zeros((pad,), flat.dtype)]).reshape(-1, 128)


def _unpack(packed, shapes):
    flat, out, off = packed.reshape(-1), [], 0
    for s in shapes:
        n = 1
        for d in s:
            n *= d
        out.append(flat[off:off + n].reshape(s))
        off += n
    return out


LAYER_KEYS = (("w_in", "w_out", "pool_w", "w_xq", "w_xkv", "w_xo"),
              ("w_in", "w_out", "glu_w1", "glu_w2", "w_xq", "w_xkv", "w_xo"))


def _weight_of(key, layer):
    if key in ("w_xq", "w_xkv", "w_xo"):
        return key, layer, 4
    kind = "ab" if layer % 2 == 0 else "cd"
    return {"w_in": "w_in_" + kind, "w_out": "w_out_" + kind}.get(key, key), layer // 2, 2


def kernel(*args):
    a = dict(zip(INPUTS, args))
    x_i, y_i, c_i = lax.axis_index("x"), lax.axis_index("y"), lax.axis_index("c")
    j = 2 * x_i + y_i

    slab = jnp.concatenate([a["norm_cd"], a["sgu_ln_g"], a["sgu_ln_b"],
                            jnp.pad(a["s5_d"], ((0, 0), (0, 128)))], axis=0)
    gslab = allgather_small(slab)
    P = {n: a[n] for n in SMALL}
    for k, n in enumerate(("norm_cd", "sgu_ln_g", "sgu_ln_b", "s5_d")):
        wd = SMALL_SHARDED[n]
        P[n] = gslab[:, 2 * k:2 * k + 2, :wd].transpose(1, 0, 2).reshape(2, 4 * wd)

    def shards_of(layer):
        keys = sorted(k for k in LAYER_KEYS[layer % 2])
        out = []
        for k in keys:
            n, l, _ = _weight_of(k, layer)
            out.append(a[n][l].reshape(-1, a[n].shape[-1]).astype(BF16))
        return keys, out

    keys0, sh0 = shards_of(0)
    first = keys0.index("w_in")
    g_in, token = allgather_sync([sh0[first].reshape(2, sh0[first].shape[0] // 2, sh0[first].shape[1])])
    w_in0 = g_in[0].reshape(4, -1, g_in[0].shape[-1])
    started = {}
    for layer in (0, 1, 2, 3):
        keys, sh = (keys0, sh0) if layer == 0 else shards_of(layer)
        rest = [(k, s) for k, s in zip(keys, sh) if k != "w_in"]
        parts = [("in", ["w_in"], [sh[keys.index("w_in")]])] * (layer > 0) + [("", *map(list, zip(*rest)))]
        for tag, pk, ps in parts:
            send, recv, ps, lands, token = allgather_start(ps, [token, gslab], "allgather_start_%d%s" % (layer, tag))
            started[(layer, tag)] = (pk, send, recv, ps, lands)
    P["norm_ab"] = P["norm_ab"] + token[0:1, 0:1]

    cidx = jnp.reshape(c_i, (1,)).astype(jnp.int32)
    jc = jnp.stack([j, c_i]).astype(jnp.int32)

    def views(g):
        W = {}
        for k, v in g.items():
            if k in ("w_in", "w_xkv"):
                W[k] = mcs(v)
            elif k == "pool_w":
                W[k] = v.reshape(4, 4, 64, 256).transpose(1, 0, 2, 3).reshape(4, 256, 256)
            elif k not in ("glu_w1", "glu_w2"):
                W[k] = m2(v.reshape(-1, v.shape[-1]))
        if "glu_w1" in g:
            W["w12"] = jnp.concatenate([g["glu_w1"].reshape(512, 512), g["glu_w2"].reshape(512, 512)], axis=1)
        return W

    def arrived(layer, tag, after):
        keys, send, recv, sh, lands = started[(layer, tag)]
        return views(dict(zip(keys, allgather_wait(send, recv, sh, lands, after, "allgather_wait_%d%s" % (layer, tag)))))

    def weights_of(layer, x_in):
        W = views({"w_in": w_in0}) if layer == 0 else arrived(layer, "in", x_in)
        W["more"] = lambda after: arrived(layer, "", after)
        return W

    halves, pending = {}, {}

    def finish_pair(layer, after):
        keys, send, recv, flat, lands = halves.pop(layer)
        flat, got = rs_pair_wait(send, recv, flat, lands, after, "rs_pair_wait_%d" % layer)
        pair = rs_pair_sum(flat, got, cidx)
        send, recv, pair, lands, token = rs_chip_start(pair, "rs_chip_start_%d" % layer)
        pending[layer] = (keys, send, recv, pair, lands)
        return token

    def grads_done(layer, GW):
        keys = sorted(GW)
        flat = [GW[k].reshape(4, 2, GW[k].shape[1] // 2, GW[k].shape[2]) for k in keys]
        send, recv, flat, lands, token = rs_pair_start(flat, "rs_pair_start_%d" % layer)
        halves[layer] = (keys, send, recv, flat, lands)
        if layer + 1 in halves:
            token = token + finish_pair(layer + 1, token)
        return token[0:1, 0:1]

    loss, dx, G = local_step(a["x"][0], a["mem"][0], a["loss_target"][0], P, weights_of, grads_done)
    loss = lax.psum(loss[0, 0], ("x", "y", "c"))
    finish_pair(0, dx)
    outs = {}

    def update_big(names, red):
        for n, g in zip(names, rs_pair_gather([red[n] for n in names])):
            shp = a[n].shape
            g2 = g.reshape(-1, shp[-1])
            upd = adamw(a[n].reshape(g2.shape), g2, a["m_" + n].reshape(g2.shape), a["v_" + n].reshape(g2.shape),
                        "adamw_" + n, with_grad=True)
            outs[n] = tuple(t.reshape(shp) for t in upd)

    def reduce_layer(layer, red, after):
        keys, send, recv, pair, lands = pending[layer]
        lands = rs_chip_wait(send, recv, pair, lands, after, "rs_chip_wait_%d" % layer)
        which = [_weight_of(k, layer) for k in keys]
        sums = rs_chip_sum(lands, pair, [l for _, l, _ in which], [red.get(n) for n, _, _ in which],
                           [layers for _, _, layers in which], jc)
        red.update(zip([n for n, _, _ in which], sums))

    red = {}
    for layer in (3, 2, 1):
        reduce_layer(layer, red, [dx])
    odd_only = [n for n in BIG if n.endswith("_cd") or n.startswith("glu")]
    update_big(odd_only, red)

    gfull = [jnp.stack(G[n]) if isinstance(G[n], list) else G[n] for n in SMALL]
    shapes = [g.shape for g in gfull]
    gsum = _unpack(allreduce_small(_pack(gfull)), shapes)
    gloc = []
    for n, g in zip(SMALL, gsum):
        if n in SMALL_SHARDED:
            g = lax.dynamic_slice_in_dim(g, j * SMALL_SHARDED[n], SMALL_SHARDED[n], axis=1)
        gloc.append(g)
    two = [(-1, a[n].shape[-1]) if a[n].ndim > 1 else (1, a[n].shape[0]) for n in SMALL]
    upds = adamw_small(*[[t.reshape(s) for t, s in zip(ts, two)]
                         for ts in ([a[n] for n in SMALL], gloc, [a["m_" + n] for n in SMALL],
                                    [a["v_" + n] for n in SMALL])])
    for n, g, upd in zip(SMALL, gloc, upds):
        outs[n] = (g,) + tuple(t.reshape(a[n].shape) for t in upd)

    behind = [outs[n][1] for n in odd_only + SMALL[-1:]] + [red[n] for n in BIG if n not in odd_only]
    reduce_layer(0, red, behind)
    update_big([n for n in BIG if n not in odd_only], red)

    res = [loss, dx[None]]
    for part in range(4):
        res += [outs[n][part] for n in WEIGHTS]
    return tuple(res)
```

```python
import math

import jax
import jax.numpy as jnp
from jax import lax
from jax.experimental import pallas as pl
from jax.experimental.pallas import tpu as pltpu

F32, BF16 = jnp.float32, jnp.bfloat16
S, D = 2048, 1024
MEM = 256
EPS = 1e-6
NEG = -1e30
QB = 128
PATTERNS = (1, 4, 16)
NG, NP, NH = 32, 64, 16
NS = NG * NP
LR, B1, B2, AEPS, WD, STEP = 0.001, 0.9, 0.999, 1e-08, 0.01, 10
MESHID = pl.DeviceIdType.MESH
VMEM_LIMIT = 56 * 1024 * 1024


def _cparams(sem):
    return pltpu.CompilerParams(dimension_semantics=sem, vmem_limit_bytes=VMEM_LIMIT)


def _sig(x):
    return 1.0 / (1.0 + jnp.exp(-x))


def _dot(a, b, dims):
    return lax.dot_general(a, b, (dims, ((), ())), preferred_element_type=F32)


def _nn(a, b):
    return _dot(a, b, ((1,), (0,)))


def _nt(a, b):
    return _dot(a, b, ((1,), (1,)))


def _tn(a, b):
    return _dot(a, b, ((0,), (0,)))


_DIMS = {"nn": ((1,), (0,)), "nt": ((1,), (1,)), "tn": ((0,), (0,))}


def _tile(dim, cc=None, cap=1024):
    for t in (2048, 1536, 1024, 768, 512, 384, 256, 128):
        if t <= cap and dim % t == 0 and (cc is None or cc % t == 0):
            return t
    return dim


MM_VMEM = 36 * 1024 * 1024


def _mm_tiles(m, n, k, ccm, ccn, cck, a_bytes, b_bytes, o_bytes):
    caps = [1024, 1024, 2048]
    while True:
        tm, tn, tk = _tile(m, ccm, caps[0]), _tile(n, ccn, caps[1]), _tile(k, cck, caps[2])
        need = 2 * (tm * tk * a_bytes + tk * tn * b_bytes + tm * tn * o_bytes) + (tm * tn * 4 if tk < k else 0)
        if need <= MM_VMEM:
            return tm, tn, tk
        if tk > 1024:
            caps[2] = tk // 2
        elif tn >= tm:
            caps[1] = tn // 2
        else:
            caps[0] = tm // 2


def m2(arr, col_off=0, ncols=None):
    rows, cols = arr.shape
    ncols = cols - col_off if ncols is None else ncols

    def spec(tr, tc, rc):
        assert col_off % tc == 0
        return pl.BlockSpec((tr, tc), lambda *g: (rc(*g)[0], rc(*g)[1] + col_off // tc))
    return (arr, rows, ncols, spec, None if col_off == 0 else col_off)


def mcs(arr):
    cs = arr.shape[2]

    def spec(tr, tc, rc):
        n = cs // tc
        return pl.BlockSpec((None, tr, tc), lambda *g: (rc(*g)[1] // n, rc(*g)[0], rc(*g)[1] % n))
    return (arr, arr.shape[1], 4 * cs, spec, cs)


def out2(rows, cols):
    def spec(tr, tc, rc):
        return pl.BlockSpec((tr, tc), lambda *g: tuple(rc(*g)))
    return ((rows, cols), spec, None)


def outcs(rows, cs):
    def spec(tr, tc, rc):
        n = cs // tc
        return pl.BlockSpec((None, tr, tc), lambda *g: (rc(*g)[1] // n, rc(*g)[0], rc(*g)[1] % n))
    return ((4, rows, cs), spec, cs)


def _both(a, b):
    if a is None:
        return b
    if b is None:
        return a
    return math.gcd(a, b)


def mm(a, b, mode, name, add=None, out=None, out_dtype=F32):
    a_arr, a_r, a_c, a_spec, a_cc = a
    b_arr, b_r, b_c, b_spec, b_cc = b
    if mode == "nn":
        m, k, n = a_r, a_c, b_c
        assert b_r == k
        ccm, cck, ccn = None, a_cc, b_cc
    elif mode == "nt":
        m, k, n = a_r, a_c, b_r
        assert b_c == k
        ccm, cck, ccn = None, _both(a_cc, b_cc), None
    else:
        m, k, n = a_c, a_r, b_c
        assert b_r == k
        ccm, cck, ccn = a_cc, None, b_cc
    out = out2(m, n) if out is None else out
    o_shape, o_spec, o_cc = out
    ccn = _both(ccn, o_cc)
    if add is not None:
        ccn = _both(ccn, add[4])
    o_bytes = jnp.dtype(out_dtype).itemsize + (0 if add is None else add[0].dtype.itemsize)
    tm, tn, tk = _mm_tiles(m, n, k, ccm, ccn, cck, a_arr.dtype.itemsize, b_arr.dtype.itemsize, o_bytes)
    nk = k // tk
    if mode == "nn":
        in_specs = [a_spec(tm, tk, lambda i, j, kk: (i, kk)), b_spec(tk, tn, lambda i, j, kk: (kk, j))]
    elif mode == "nt":
        in_specs = [a_spec(tm, tk, lambda i, j, kk: (i, kk)), b_spec(tn, tk, lambda i, j, kk: (j, kk))]
    else:
        in_specs = [a_spec(tk, tm, lambda i, j, kk: (kk, i)), b_spec(tk, tn, lambda i, j, kk: (kk, j))]
    args = [a_arr, b_arr]
    if add is not None:
        in_specs.append(add[3](tm, tn, lambda i, j, kk: (i, j)))
        args.append(add[0])
    return _mm_call(args, in_specs, o_spec(tm, tn, lambda i, j, kk: (i, j)), jax.ShapeDtypeStruct(o_shape, out_dtype),
                    mode, (m // tm, n // tn, nk), (tm, tn), add is not None, name)


def _mm_call(args, in_specs, out_spec, out_shape, mode, grid, tile, has_add, name):
    dims = _DIMS[mode]
    nk = grid[2]
    tm, tn = tile

    def body(*refs):
        a_ref, b_ref = refs[0], refs[1]
        add_ref = refs[2] if has_add else None
        prod = _dot(a_ref[...].astype(BF16), b_ref[...].astype(BF16), dims)
        if nk == 1:
            o_ref = refs[-1]
            if has_add:
                prod = prod + add_ref[...].astype(F32)
            o_ref[...] = prod.astype(o_ref.dtype)
            return
        o_ref, acc = refs[-2], refs[-1]
        kk = pl.program_id(2)

        @pl.when(kk == 0)
        def _():
            acc[...] = prod

        @pl.when(kk > 0)
        def _():
            acc[...] += prod

        @pl.when(kk == nk - 1)
        def _():
            r = acc[...]
            if has_add:
                r = r + add_ref[...].astype(F32)
            o_ref[...] = r.astype(o_ref.dtype)

    return pl.pallas_call(
        body, name=name, grid=grid, in_specs=in_specs, out_specs=out_spec, out_shape=out_shape,
        scratch_shapes=[pltpu.VMEM((tm, tn), F32)] if nk > 1 else [],
        compiler_params=_cparams(("parallel", "parallel", "arbitrary")),
    )(*args)


def mm_band(a, b, mode, name, grid, blocks, maps, out_shape, add=None, out_dtype=F32):
    in_specs = [pl.BlockSpec(blocks[0], maps[0]), pl.BlockSpec(blocks[1], maps[1])]
    args = [a, b]
    if add is not None:
        in_specs.append(pl.BlockSpec(blocks[2], maps[2]))
        args.append(add)
    return _mm_call(args, in_specs, pl.BlockSpec(blocks[2], maps[2]), jax.ShapeDtypeStruct(out_shape, out_dtype),
                    mode, grid, blocks[2], add is not None, name)


def rw(fn, ins, outs, name, rows, tr=256, consts=(), accs=()):
    n_in, n_c, n_o, n_a = len(ins), len(consts), len(outs), len(accs)
    in_specs = []
    for arr, off, width in ins:
        assert off % width == 0
        in_specs.append(pl.BlockSpec((tr, width), lambda i, o=off // width: (i, o)))
    for c in consts:
        in_specs.append(pl.BlockSpec(c.shape, lambda i: (0, 0)))
    out_specs = [pl.BlockSpec((tr, w), lambda i: (i, 0)) for w, _ in outs]
    out_specs += [pl.BlockSpec(s, lambda i: (0, 0)) for s in accs]
    out_shape = [jax.ShapeDtypeStruct((rows, w), dt) for w, dt in outs]
    out_shape += [jax.ShapeDtypeStruct(s, F32) for s in accs]

    def body(*refs):
        vals = [r[...] for r in refs[:n_in + n_c]]
        o_refs = refs[n_in + n_c:n_in + n_c + n_o]
        a_refs = refs[n_in + n_c + n_o:]
        res = fn(*vals)
        for r, v in zip(o_refs, res[:n_o]):
            r[...] = v.astype(r.dtype)
        if n_a:
            @pl.when(pl.program_id(0) == 0)
            def _():
                for r in a_refs:
                    r[...] = jnp.zeros_like(r)
            for r, v in zip(a_refs, res[n_o:]):
                r[...] += v

    res = pl.pallas_call(
        body, name=name, grid=(rows // tr,), in_specs=in_specs, out_specs=out_specs,
        out_shape=out_shape,
        compiler_params=_cparams(("arbitrary",) if n_a else ("parallel",)),
    )(*[a for a, _, _ in ins], *consts)
    return res


def _rstd(x):
    return lax.rsqrt(jnp.mean(x * x, axis=-1, keepdims=True) + EPS)


def rms_fwd(x, g, name):
    def fn(xv, gv):
        xv = xv.astype(F32)
        return (xv * _rstd(xv) * gv,)
    return rw(fn, [(x, 0, D)], [(D, BF16)], name, x.shape[0], consts=[g])[0]


def _rms_bwd_math(xv, dy, gv):
    r = _rstd(xv)
    dyg = dy * gv
    dx = r * dyg - xv * (r * r * r / D) * jnp.sum(dyg * xv, axis=-1, keepdims=True)
    dg = jnp.sum(dy * xv * r, axis=0, keepdims=True)
    return dx, dg


def rms_bwd(x, dy, dres, g, name):
    def fn(xv, dyv, drv, gv):
        dx, dg = _rms_bwd_math(xv, dyv, gv)
        return dx + drv, dg
    return rw(fn, [(x, 0, D), (dy, 0, D), (dres, 0, D)], [(D, F32)], name, x.shape[0],
              consts=[g], accs=[(1, D)])


def final_loss(x, tgt, g):
    def fn(xv, tv, gv):
        e = xv * _rstd(xv) * gv - tv
        loss = 0.5 * jnp.sum(jnp.sum(e * e, axis=-1, keepdims=True), axis=0, keepdims=True) / D
        dx, dg = _rms_bwd_math(xv, e / D, gv)
        return dx, loss, dg
    return rw(fn, [(x, 0, D), (tgt, 0, D)], [(D, F32)], "final_loss", S, consts=[g],
              accs=[(1, 1), (1, D)])


def _attn_bias(bias_ref):
    ii = lax.broadcasted_iota(jnp.int32, (2 * QB, 2 * QB), 0) % QB
    jj = lax.broadcasted_iota(jnp.int32, (2 * QB, 2 * QB), 1)
    dist = ii + QB - jj
    band = (dist >= 0) & (dist <= QB)
    bias_ref[1] = jnp.where(band, 0.0, NEG)
    bias_ref[0] = jnp.where(band & (jj >= QB), 0.0, NEG)


def _two_heads(x, m0):
    return jnp.concatenate([jnp.where(m0, x, 0.0), jnp.where(m0, 0.0, x)], axis=0)


def _per_head(col, m0):
    return jnp.where(m0, col[:QB], col[QB:])


def _attn_rows(idx, d):
    if d == 1:
        b = idx
        cur = pl.ds(pl.multiple_of(b * QB, QB), QB)
        prev = pl.ds(pl.multiple_of(jnp.maximum(b - 1, 0) * QB, QB), QB)
    else:
        r, b = lax.rem(idx, d), lax.div(idx, d)
        cur = pl.ds(r + b * (QB * d), QB, stride=d)
        prev = pl.ds(r + jnp.maximum(b - 1, 0) * (QB * d), QB, stride=d)
    return cur, prev, b


NBLK = S // QB
GROUP = 16
GROUP_FWD = 16


def _colblk(off):
    return pl.BlockSpec((S, 128), lambda hp: (0, off * 8 + hp))


def attn_fwd(z):
    def body(q_ref, k_ref, v_ref, g_ref, o_ref, l_ref, a_ref, os, ls, bias):
        _attn_bias(bias)
        m0 = lax.broadcasted_iota(jnp.int32, (1, 128), 1) < 64
        for pi, d in enumerate(PATTERNS):
            lone = S // d == QB

            def load(idx, d=d, lone=lone):
                cur, prev, b = _attn_rows(idx, d)
                if lone:
                    return cur, (q_ref[cur, :], None, k_ref[cur, :], None, v_ref[cur, :], bias[1, :, QB:])
                return cur, (q_ref[cur, :], k_ref[prev, :], k_ref[cur, :], v_ref[prev, :], v_ref[cur, :],
                             bias[jnp.minimum(b, 1)])

            def block(q, kp, kc, vp, vc, bs):
                qq = _two_heads(q * 0.125, m0).astype(BF16)
                k = (kc if kp is None else jnp.concatenate([kp, kc], axis=0)).astype(BF16)
                s = _nt(qq, k) + bs
                mx = jnp.max(s, axis=-1, keepdims=True)
                p = jnp.exp(s - mx)
                den = jnp.sum(p, axis=-1, keepdims=True)
                pb = p.astype(BF16)
                vv = _two_heads(vc if vp is None else jnp.concatenate([vp, vc], axis=0), m0).astype(BF16)
                o = _nn(jnp.concatenate([pb[:QB], pb[QB:]], axis=1), vv)
                return o * _per_head(1.0 / den, m0), _per_head(mx + jnp.log(den), m0)

            def step(i, carry, pi=pi):
                loaded = [load(i * GROUP_FWD + u) for u in range(GROUP_FWD)]
                done = [block(*vals) for _, vals in loaded]
                for (cur, _), (o, l) in zip(loaded, done):
                    os[pi, cur, :] = o
                    ls[pi, cur, :] = l
                return carry
            lax.fori_loop(0, NBLK // GROUP_FWD, step, 0)
        l1, l2, l3 = ls[0], ls[1], ls[2]
        mx = jnp.maximum(jnp.maximum(l1, l2), l3)
        e1, e2, e3 = jnp.exp(l1 - mx), jnp.exp(l2 - mx), jnp.exp(l3 - mx)
        tot = e1 + e2 + e3
        o = (os[0] * e1 + os[1] * e2 + os[2] * e3) / tot
        ga = g_ref[...]
        o_ref[...] = o
        l_ref[...] = mx + jnp.log(tot)
        a_ref[...] = (o * (ga * _sig(ga))).astype(a_ref.dtype)

    out = pl.BlockSpec((S, 128), lambda hp: (0, hp))
    return pl.pallas_call(
        body, name="attn_fwd", grid=(8,),
        in_specs=[_colblk(0), _colblk(1), _colblk(2), _colblk(3)], out_specs=[out] * 3,
        out_shape=[jax.ShapeDtypeStruct((S, D), F32), jax.ShapeDtypeStruct((S, D), F32),
                   jax.ShapeDtypeStruct((S, 2 * D), BF16)],
        scratch_shapes=[pltpu.VMEM((3, S, 128), F32), pltpu.VMEM((3, S, 128), F32),
                        pltpu.VMEM((2, 2 * QB, 2 * QB), F32)],
        compiler_params=_cparams(("parallel",)),
    )(z, z, z, z)


def attn_bwd(z, d_cat, o, lse):
    def body(q_ref, k_ref, v_ref, g_ref, da_ref, o_ref, l_ref, dq_ref, dk_ref, dv_ref, dg_ref, do_s, pr_s, bias):
        _attn_bias(bias)
        m0 = lax.broadcasted_iota(jnp.int32, (1, 128), 1) < 64
        ga = g_ref[...]
        sg = _sig(ga)
        da = da_ref[...]
        ov = o_ref[...]
        do = da * (ga * sg)
        dg_ref[...] = da * ov * (sg * (1.0 + ga * (1.0 - sg)))
        do_s[...] = do
        pr_s[...] = do * ov
        dq_ref[...] = jnp.zeros_like(dq_ref)
        dk_ref[...] = jnp.zeros_like(dk_ref)
        dv_ref[...] = jnp.zeros_like(dv_ref)
        for d in PATTERNS:
            lone = S // d == QB

            def load(idx, d=d, lone=lone):
                cur, prev, b = _attn_rows(idx, d)
                if lone:
                    return (cur, None), (q_ref[cur, :], None, k_ref[cur, :], None, v_ref[cur, :],
                                         do_s[cur, :], pr_s[cur, :], l_ref[cur, :], bias[1, :, QB:])
                return (cur, prev), (q_ref[cur, :], k_ref[prev, :], k_ref[cur, :], v_ref[prev, :], v_ref[cur, :],
                                     do_s[cur, :], pr_s[cur, :], l_ref[cur, :], bias[jnp.minimum(b, 1)])

            def block(q, kp, kc, vp, vc, dof, prod, lp, bs):
                qq = _two_heads(q * 0.125, m0).astype(BF16)
                kf = kc if kp is None else jnp.concatenate([kp, kc], axis=0)
                k = kf.astype(BF16)
                v = (vc if vp is None else jnp.concatenate([vp, vc], axis=0)).astype(BF16)
                dd = _two_heads(dof, m0).astype(BF16)
                lh = jnp.max(jnp.concatenate([jnp.where(m0, lp, -jnp.inf), jnp.where(m0, -jnp.inf, lp)], axis=0),
                             axis=-1, keepdims=True)
                delta = jnp.sum(_two_heads(prod, m0), axis=-1, keepdims=True)
                p = jnp.exp(_nt(qq, k) + bs - lh)
                ds = (p * (_nt(dd, v) - delta)).astype(BF16)
                dq = _nn(jnp.concatenate([ds[:QB], ds[QB:]], axis=1), _two_heads(kf, m0).astype(BF16))
                return dq * 0.125, _tn(ds, qq), _tn(p.astype(BF16), dd)

            def step(i, carry):
                loaded = [load(i * GROUP + u) for u in range(GROUP)]
                done = [block(*vals) for _, vals in loaded]
                for ((cur, prev), _), (dq, dk, dv) in zip(loaded, done):
                    dq_ref[cur, :] = dq_ref[cur, :] + dq
                    if prev is not None:
                        dk_ref[prev, :] = dk_ref[prev, :] + dk[:QB]
                        dv_ref[prev, :] = dv_ref[prev, :] + dv[:QB]
                    dk_ref[cur, :] = dk_ref[cur, :] + dk[-QB:]
                    dv_ref[cur, :] = dv_ref[cur, :] + dv[-QB:]
                return carry
            lax.fori_loop(0, NBLK // GROUP, step, 0)

    blk = pl.BlockSpec((S, 128), lambda hp: (0, hp))
    return pl.pallas_call(
        body, name="attn_bwd", grid=(8,),
        in_specs=[_colblk(0), _colblk(1), _colblk(2), _colblk(3), blk, blk, blk], out_specs=[blk] * 4,
        out_shape=[jax.ShapeDtypeStruct((S, D), F32)] * 4,
        scratch_shapes=[pltpu.VMEM((S, 128), F32), pltpu.VMEM((S, 128), F32), pltpu.VMEM((2, 2 * QB, 2 * QB), F32)],
        compiler_params=_cparams(("parallel",)),
    )(z, z, z, z, d_cat, o, lse)


def assemble_dz_even(parts):
    def body(*refs):
        o_ref = refs[-1]
        for j in range(6):
            o_ref[:, j * D:(j + 1) * D] = refs[j][...].astype(o_ref.dtype)
    tr = 256
    blk = pl.BlockSpec((tr, D), lambda i: (i, 0))
    return pl.pallas_call(
        body, name="assemble_dz_even", grid=(S // tr,), in_specs=[blk] * 6,
        out_specs=pl.BlockSpec((tr, 6 * D), lambda i: (i, 0)),
        out_shape=jax.ShapeDtypeStruct((S, 6 * D), BF16),
        compiler_params=_cparams(("parallel",)),
    )(*parts)


def _pool_window(g):
    return jnp.where(g == 0, 2.0, jnp.where(g == 1, 4.0, jnp.where(g == 2, 8.0, 16.0)))


def _pool_sel(g, levels):
    return jnp.where(g == 0, levels[0], jnp.where(g == 1, levels[1], jnp.where(g == 2, levels[2], levels[3])))


def _pool_fwd_math(v, g):
    t = lax.broadcasted_iota(jnp.int32, (S, 1), 0)
    s = v
    levels = []
    for k in (1, 2, 4, 8):
        s = s + jnp.where(t >= k, pltpu.roll(s, k, 0), 0.0)
        levels.append(s)
    cnt = jnp.minimum((t + 1).astype(F32), _pool_window(g))
    return _pool_sel(g, levels) / cnt - v, cnt


def pool_fwd(z, pw, ps, cat):
    def body(v_ref, g_ref, pw_ref, ps_ref, cat_ref, o_ref):
        g = pl.program_id(0)
        pooled, _ = _pool_fwd_math(v_ref[...], g)
        mixed = _nn(pooled.astype(BF16), pw_ref[...].astype(BF16))
        gb = g_ref[...]
        o_ref[...] = (mixed * ps_ref[...] * (gb * _sig(gb))).astype(o_ref.dtype)

    return pl.pallas_call(
        body, name="pool_fwd", grid=(4,),
        in_specs=[pl.BlockSpec((S, 256), lambda g: (0, 16 + g)),
                  pl.BlockSpec((S, 256), lambda g: (0, 20 + g)),
                  pl.BlockSpec((None, 256, 256), lambda g: (g, 0, 0)),
                  pl.BlockSpec((1, 256), lambda g: (0, g)), pl.BlockSpec(memory_space=pl.ANY)],
        out_specs=pl.BlockSpec((S, 256), lambda g: (0, 4 + g)),
        out_shape=jax.ShapeDtypeStruct((S, 2 * D), BF16),
        input_output_aliases={4: 0},
        compiler_params=_cparams(("parallel",)),
    )(z, z, pw, ps, cat)


def pool_bwd(z, d_cat, pw, ps):
    def body(v_ref, g_ref, d_ref, pw_ref, ps_ref, dv_ref, dg_ref, dpw_ref, dps_ref):
        g = pl.program_id(0)
        v = v_ref[...]
        pooled, cnt = _pool_fwd_math(v, g)
        pwb = pw_ref[...].astype(BF16)
        pb = pooled.astype(BF16)
        mixed = _nn(pb, pwb)
        gb = g_ref[...]
        sg = _sig(gb)
        dout = d_ref[...]
        sc = ps_ref[...]
        dg_ref[...] = dout * mixed * sc * (sg * (1.0 + gb * (1.0 - sg)))
        dms = dout * (gb * sg)
        dps_ref[...] = jnp.sum(dms * mixed, axis=0, keepdims=True)
        dmx = (dms * sc).astype(BF16)
        dpw_ref[...] = _tn(pb, dmx)
        dpooled = _nt(dmx, pwb)
        t = lax.broadcasted_iota(jnp.int32, (S, 1), 0)
        s = dpooled / cnt
        levels = []
        for k in (1, 2, 4, 8):
            s = s + jnp.where(t < S - k, pltpu.roll(s, S - k, 0), 0.0)
            levels.append(s)
        dv_ref[...] = _pool_sel(g, levels) - dpooled

    return pl.pallas_call(
        body, name="pool_bwd", grid=(4,),
        in_specs=[pl.BlockSpec((S, 256), lambda g: (0, 16 + g)),
                  pl.BlockSpec((S, 256), lambda g: (0, 20 + g)),
                  pl.BlockSpec((S, 256), lambda g: (0, 4 + g)),
                  pl.BlockSpec((None, 256, 256), lambda g: (g, 0, 0)),
                  pl.BlockSpec((1, 256), lambda g: (0, g))],
        out_specs=[pl.BlockSpec((S, 256), lambda g: (0, g)),
                   pl.BlockSpec((S, 256), lambda g: (0, g)),
                   pl.BlockSpec((None, 256, 256), lambda g: (g, 0, 0)),
                   pl.BlockSpec((1, 256), lambda g: (0, g))],
        out_shape=[jax.ShapeDtypeStruct((S, D), F32), jax.ShapeDtypeStruct((S, D), F32),
                   jax.ShapeDtypeStruct((4, 256, 256), F32), jax.ShapeDtypeStruct((1, D), F32)],
        compiler_params=_cparams(("parallel",)),
    )(z, z, d_cat, pw, ps)


CH = 128


def _sgu_common(v, lng, lnb, w_ref):
    mu = jnp.mean(v, axis=-1, keepdims=True)
    vc = v - mu
    rs = lax.rsqrt(jnp.mean(vc * vc, axis=-1, keepdims=True) + EPS)
    xhat = vc * rs
    vn = (xhat * lng + lnb).astype(BF16)
    ri = lax.broadcasted_iota(jnp.int32, (CH, CH), 0)
    ci = lax.broadcasted_iota(jnp.int32, (CH, CH), 1)
    tril = ri >= ci
    ws = [jnp.where(tril, w_ref[g], 0.0).astype(BF16) for g in range(4)]
    return xhat, rs, vn, tril, ws


def _zspec(off):
    return pl.BlockSpec((CH, D), lambda c: (c, off))


def _full(shape):
    return pl.BlockSpec(shape, lambda c: (0,) * len(shape))


def sgu_fwd(z, lng, lnb, w, bfull):
    def body(u_ref, v_ref, g_ref, lng_ref, lnb_ref, w_ref, b_ref, o_ref):
        _, _, vn, _, ws = _sgu_common(v_ref[...], lng_ref[...], lnb_ref[...], w_ref)
        for g in range(4):
            sl = slice(g * 256, (g + 1) * 256)
            mixed = _nn(ws[g], vn[:, sl]) + b_ref[:, sl]
            gc = g_ref[:, sl]
            o_ref[:, sl] = (u_ref[:, sl] * mixed * (gc * _sig(gc))).astype(o_ref.dtype)

    return pl.pallas_call(
        body, name="sgu_fwd", grid=(S // CH,),
        in_specs=[_zspec(0), _zspec(1), _zspec(2), _full((1, D)), _full((1, D)),
                  _full((4, CH, CH)), _full((CH, D))],
        out_specs=pl.BlockSpec((CH, D), lambda c: (c, 0)),
        out_shape=jax.ShapeDtypeStruct((S, D), BF16),
        compiler_params=_cparams(("parallel",)),
    )(z, z, z, lng, lnb, w, bfull)


def sgu_bwd(z, d_cat, lng, lnb, w, bfull):
    def body(u_ref, v_ref, g_ref, d_ref, lng_ref, lnb_ref, w_ref, b_ref,
             du_ref, dv_ref, dg_ref, dw_ref, db_ref, dlg_ref, dlb_ref):
        @pl.when(pl.program_id(0) == 0)
        def _():
            dw_ref[...] = jnp.zeros_like(dw_ref)
            db_ref[...] = jnp.zeros_like(db_ref)
            dlg_ref[...] = jnp.zeros_like(dlg_ref)
            dlb_ref[...] = jnp.zeros_like(dlb_ref)

        lng = lng_ref[...]
        xhat, rs, vn, tril, ws = _sgu_common(v_ref[...], lng, lnb_ref[...], w_ref)
        lane = lax.broadcasted_iota(jnp.int32, (1, 128), 1)
        db = jnp.zeros((CH, 128), F32)
        dvn_parts = []
        for g in range(4):
            sl = slice(g * 256, (g + 1) * 256)
            mixed = _nn(ws[g], vn[:, sl]) + b_ref[:, sl]
            gc = g_ref[:, sl]
            sg = _sig(gc)
            u = u_ref[:, sl]
            dc = d_ref[:, sl]
            du_ref[:, sl] = dc * mixed * (gc * sg)
            dg_ref[:, sl] = dc * u * mixed * (sg * (1.0 + gc * (1.0 - sg)))
            dmx = dc * u * (gc * sg)
            db = db + jnp.where(lane == g, jnp.sum(dmx, axis=-1, keepdims=True), 0.0)
            dmb = dmx.astype(BF16)
            dw_ref[g] += jnp.where(tril, _nt(dmb, vn[:, sl]), 0.0)
            dvn_parts.append(_tn(ws[g], dmb))
        db_ref[...] += db
        dvn = jnp.concatenate(dvn_parts, axis=1)
        dlb_ref[...] += jnp.sum(dvn, axis=0, keepdims=True)
        dlg_ref[...] += jnp.sum(dvn * xhat, axis=0, keepdims=True)
        dxh = dvn * lng
        dv_ref[...] = rs * (dxh - jnp.mean(dxh, axis=-1, keepdims=True)
                            - xhat * jnp.mean(dxh * xhat, axis=-1, keepdims=True))

    row = pl.BlockSpec((CH, D), lambda c: (c, 0))
    return pl.pallas_call(
        body, name="sgu_bwd", grid=(S // CH,),
        in_specs=[_zspec(0), _zspec(1), _zspec(2), row, _full((1, D)), _full((1, D)),
                  _full((4, CH, CH)), _full((CH, D))],
        out_specs=[row, row, row, _full((4, CH, CH)), _full((CH, 128)), _full((1, D)), _full((1, D))],
        out_shape=[jax.ShapeDtypeStruct((S, D), F32)] * 3
        + [jax.ShapeDtypeStruct((4, CH, CH), F32), jax.ShapeDtypeStruct((CH, 128), F32),
           jax.ShapeDtypeStruct((1, D), F32), jax.ShapeDtypeStruct((1, D), F32)],
        compiler_params=_cparams(("arbitrary",)),
    )(z, z, z, d_cat, lng, lnb, w, bfull)


TB = 256


def _cmul(ar, ai, br, bi):
    return ar * br - ai * bi, ar * bi + ai * br


def _scan_consts(ar, ai, reverse):
    a2 = _cmul(ar, ai, ar, ai)
    a4 = _cmul(*a2, *a2)
    row = lax.broadcasted_iota(jnp.int32, (8, NS), 0)

    def masked(k, p):
        keep = (row < 8 - k) if reverse else (row >= k)
        return jnp.where(keep, p[0], 0.0), jnp.where(keep, p[1], 0.0)
    pr = jnp.zeros((8, NS), F32)
    pi = jnp.zeros((8, NS), F32)
    cr, ci = ar, ai
    for r in range(8):
        sel = row == (7 - r if reverse else r)
        pr = jnp.where(sel, cr, pr)
        pi = jnp.where(sel, ci, pi)
        cr, ci = _cmul(cr, ci, ar, ai)
    return (masked(1, (ar, ai)), masked(2, a2), masked(4, a4)), (pr, pi), row


def scan_fwd(bu, abr, abi):
    def body(bu_ref, ar_ref, ai_ref, h_ref, car, cai):
        @pl.when(pl.program_id(0) == 0)
        def _():
            car[...] = jnp.zeros_like(car)
            cai[...] = jnp.zeros_like(cai)

        pows, (pr, pi), row = _scan_consts(ar_ref[...], ai_ref[...], False)

        def tile(t, carry):
            c_r, c_i = carry
            rows = pl.ds(pl.multiple_of(t * 8, 8), 8)
            xr = bu_ref[rows, 0:NS]
            xi = bu_ref[rows, NS:2 * NS]
            for k, (kr, ki) in zip((1, 2, 4), pows):
                sr = pltpu.roll(xr, k, 0)
                si = pltpu.roll(xi, k, 0)
                xr, xi = xr + kr * sr - ki * si, xi + kr * si + ki * sr
            xr, xi = xr + pr * c_r - pi * c_i, xi + pr * c_i + pi * c_r
            h_ref[rows, 0:NS] = xr
            h_ref[rows, NS:2 * NS] = xi
            return (jnp.broadcast_to(xr[7:8, :], (8, NS)), jnp.broadcast_to(xi[7:8, :], (8, NS)))

        c_r, c_i = lax.fori_loop(0, TB // 8, tile, (car[...], cai[...]))
        car[...] = c_r
        cai[...] = c_i

    return pl.pallas_call(
        body, name="s5_scan_fwd", grid=(S // TB,),
        in_specs=[pl.BlockSpec((TB, 2 * NS), lambda i: (i, 0)),
                  pl.BlockSpec((1, NS), lambda i: (0, 0)), pl.BlockSpec((1, NS), lambda i: (0, 0))],
        out_specs=pl.BlockSpec((TB, 2 * NS), lambda i: (i, 0)),
        out_shape=jax.ShapeDtypeStruct((S, 2 * NS), F32),
        scratch_shapes=[pltpu.VMEM((8, NS), F32), pltpu.VMEM((8, NS), F32)],
        compiler_params=_cparams(("arbitrary",)),
    )(bu, abr, abi)


def scan_bwd(eta, h, abr, abi):
    nt = S // TB

    def body(e_ref, h_ref, ar_ref, ai_ref, l_ref, da_ref, car, cai):
        @pl.when(pl.program_id(0) == 0)
        def _():
            car[...] = jnp.zeros_like(car)
            cai[...] = jnp.zeros_like(cai)
            da_ref[...] = jnp.zeros_like(da_ref)

        pows, (pr, pi), row = _scan_consts(ar_ref[...], -ai_ref[...], True)

        def tile(tt, carry):
            c_r, c_i, acr, aci = carry
            t = TB // 8 - 1 - tt
            rows = pl.ds(pl.multiple_of(t * 8, 8), 8)
            xr = e_ref[rows, 0:NS]
            xi = e_ref[rows, NS:2 * NS]
            for k, (kr, ki) in zip((1, 2, 4), pows):
                sr = pltpu.roll(xr, 8 - k, 0)
                si = pltpu.roll(xi, 8 - k, 0)
                xr, xi = xr + kr * sr - ki * si, xi + kr * si + ki * sr
            xr, xi = xr + pr * c_r - pi * c_i, xi + pr * c_i + pi * c_r
            l_ref[rows, 0:NS] = xr
            l_ref[rows, NS:2 * NS] = xi
            nr = jnp.where(row < 7, pltpu.roll(xr, 7, 0), c_r)
            ni = jnp.where(row < 7, pltpu.roll(xi, 7, 0), c_i)
            hr = h_ref[rows, 0:NS]
            hi = h_ref[rows, NS:2 * NS]
            acr = acr + hr * nr + hi * ni
            aci = aci + hr * ni - hi * nr
            return (jnp.broadcast_to(xr[0:1, :], (8, NS)), jnp.broadcast_to(xi[0:1, :], (8, NS)), acr, aci)

        zero = jnp.zeros((8, NS), F32)
        c_r, c_i, acr, aci = lax.fori_loop(0, TB // 8, tile, (car[...], cai[...], zero, zero))
        car[...] = c_r
        cai[...] = c_i
        da_ref[:, 0:NS] += acr
        da_ref[:, NS:2 * NS] += aci

    rev = pl.BlockSpec((TB, 2 * NS), lambda i: (nt - 1 - i, 0))
    return pl.pallas_call(
        body, name="s5_scan_bwd", grid=(nt,),
        in_specs=[rev, rev, pl.BlockSpec((1, NS), lambda i: (0, 0)), pl.BlockSpec((1, NS), lambda i: (0, 0))],
        out_specs=[rev, pl.BlockSpec((8, 2 * NS), lambda i: (0, 0))],
        out_shape=[jax.ShapeDtypeStruct((S, 2 * NS), F32), jax.ShapeDtypeStruct((8, 2 * NS), F32)],
        scratch_shapes=[pltpu.VMEM((8, NS), F32), pltpu.VMEM((8, NS), F32)],
        compiler_params=_cparams(("arbitrary",)),
    )(eta, h, abr, abi)


GC = 0.7978845608028654
GA = 0.044715


def s5_post(hc, z, dskip):
    def fn(hv, xd, dv):
        y = hv + dv * xd
        return y, 0.5 * y * (1.0 + jnp.tanh(GC * (y + GA * y * y * y)))
    return rw(fn, [(hc, 0, 512), (z, 3072, 512)], [(512, F32), (512, BF16)], "s5_post", S, consts=[dskip])


def s5_post_bwd(dyg, ypre, z, dskip):
    def fn(dy, y, xd, dv):
        th = jnp.tanh(GC * (y + GA * y * y * y))
        dg = 0.5 * (1.0 + th) + 0.5 * y * (1.0 - th * th) * GC * (1.0 + 3.0 * GA * y * y)
        dyp = dy * dg
        return dyp, dyp * dv, jnp.sum(dyp * xd, axis=0, keepdims=True)
    return rw(fn, [(dyg, 0, 512), (ypre, 0, 512), (z, 3072, 512)], [(512, BF16), (512, F32)],
              "s5_post_bwd", S, consts=[dskip], accs=[(1, 512)])


def glu_fwd(t, z, c_out):
    def fn(t1, t2, gd, co):
        return (jnp.concatenate([co, (t1 * _sig(t2) * (gd * _sig(gd))).astype(BF16)], axis=1),)
    return rw(fn, [(t, 0, 512), (t, 512, 512), (z, 3584, 512), (c_out, 0, D)], [(D + 512, BF16)], "glu_fwd", S)[0]


def glu_bwd(t, z, d_cat):
    def fn(t1, t2, gd, dd):
        s2, sg = _sig(t2), _sig(gd)
        sl = gd * sg
        return (jnp.concatenate([dd * s2 * sl, dd * t1 * s2 * (1.0 - s2) * sl], axis=1),
                dd * t1 * s2 * (sg * (1.0 + gd * (1.0 - sg))))
    return rw(fn, [(t, 0, 512), (t, 512, 512), (z, 3584, 512), (d_cat, 1024, 512)],
              [(D, BF16), (512, F32)], "glu_bwd", S)


def assemble_dz_odd(du, dv, dgc, dxd, dgd):
    def body(a, b, c, d, e, o_ref):
        o_ref[:, 0:D] = a[...].astype(BF16)
        o_ref[:, D:2 * D] = b[...].astype(BF16)
        o_ref[:, 2 * D:3 * D] = c[...].astype(BF16)
        o_ref[:, 3 * D:3 * D + 512] = d[...].astype(BF16)
        o_ref[:, 3 * D + 512:4 * D] = e[...].astype(BF16)
    tr = 256
    blk = pl.BlockSpec((tr, D), lambda i: (i, 0))
    half = pl.BlockSpec((tr, 512), lambda i: (i, 0))
    return pl.pallas_call(
        body, name="assemble_dz_odd", grid=(S // tr,), in_specs=[blk, blk, blk, half, half],
        out_specs=pl.BlockSpec((tr, 4 * D), lambda i: (i, 0)),
        out_shape=jax.ShapeDtypeStruct((S, 4 * D), BF16),
        compiler_params=_cparams(("parallel",)),
    )(du, dv, dgc, dxd, dgd)


TQ = 256


def _xattn_probs(qh, kh):
    s = _nt(qh, kh) * 0.0625
    p = jnp.exp(s - jnp.max(s, axis=-1, keepdims=True))
    return p / jnp.sum(p, axis=-1, keepdims=True)


def xattn_fwd(q, kv):
    def body(q_ref, kv_ref, o_ref):
        for h in range(4):
            sl = slice(h * 256, (h + 1) * 256)
            p = _xattn_probs(q_ref[:, sl].astype(BF16), kv_ref[:, sl].astype(BF16))
            vh = kv_ref[:, D + h * 256:D + (h + 1) * 256].astype(BF16)
            o_ref[:, sl] = _nn(p.astype(BF16), vh).astype(o_ref.dtype)

    return pl.pallas_call(
        body, name="xattn_fwd", grid=(S // TQ,),
        in_specs=[pl.BlockSpec((TQ, D), lambda i: (i, 0)), pl.BlockSpec((MEM, 2 * D), lambda i: (0, 0))],
        out_specs=pl.BlockSpec((TQ, D), lambda i: (i, 0)),
        out_shape=jax.ShapeDtypeStruct((S, D), BF16),
        compiler_params=_cparams(("parallel",)),
    )(q, kv)


def xattn_bwd(q, kv, d_o):
    def body(q_ref, kv_ref, do_ref, dq_ref, dkv_ref):
        @pl.when(pl.program_id(0) == 0)
        def _():
            dkv_ref[...] = jnp.zeros_like(dkv_ref)

        for h in range(4):
            sl = slice(h * 256, (h + 1) * 256)
            vs = slice(D + h * 256, D + (h + 1) * 256)
            qh = q_ref[:, sl].astype(BF16)
            kh = kv_ref[:, sl].astype(BF16)
            vh = kv_ref[:, vs].astype(BF16)
            doh = do_ref[:, sl].astype(BF16)
            p = _xattn_probs(qh, kh)
            dp = _nt(doh, vh)
            ds = (p * (dp - jnp.sum(p * dp, axis=-1, keepdims=True)) * 0.0625).astype(BF16)
            dq_ref[:, sl] = _nn(ds, kh).astype(dq_ref.dtype)
            dkv_ref[:, sl] += _tn(ds, qh)
            dkv_ref[:, vs] += _tn(p.astype(BF16), doh)

    return pl.pallas_call(
        body, name="xattn_bwd", grid=(S // TQ,),
        in_specs=[pl.BlockSpec((TQ, D), lambda i: (i, 0)), pl.BlockSpec((MEM, 2 * D), lambda i: (0, 0)),
                  pl.BlockSpec((TQ, D), lambda i: (i, 0))],
        out_specs=[pl.BlockSpec((TQ, D), lambda i: (i, 0)), pl.BlockSpec((MEM, 2 * D), lambda i: (0, 0))],
        out_shape=[jax.ShapeDtypeStruct((S, D), BF16), jax.ShapeDtypeStruct((MEM, 2 * D), F32)],
        compiler_params=_cparams(("arbitrary",)),
    )(q, kv, d_o)


def _s5_disc(a_re, a_im, log_dt, b_re, b_im):
    dt = jnp.exp(log_dt)[:, None]
    mag = jnp.exp(dt * a_re)
    abr = mag * jnp.cos(dt * a_im)
    abi = mag * jnp.sin(dt * a_im)
    nr, ni = abr - 1.0, abi
    inv = 1.0 / (a_re * a_re + a_im * a_im)
    cr = (nr * a_re + ni * a_im) * inv
    ci = (ni * a_re - nr * a_im) * inv
    bbr = cr[..., None] * b_re - ci[..., None] * b_im
    bbi = cr[..., None] * b_im + ci[..., None] * b_re
    return abr, abi, bbr, bbi


VM = pl.BlockSpec(memory_space=pltpu.VMEM)


def s5_embed(bt_re, bt_im, ct_re, ct_im):
    def body(br, bi, cr, ci, b_ref, c_ref):
        b_ref[...] = jnp.zeros_like(b_ref)
        c_ref[...] = jnp.zeros_like(c_ref)
        for g in range(NG):
            rows, cols = slice(g * NH, (g + 1) * NH), slice(g * NP, (g + 1) * NP)
            b_ref[rows, cols] = br[g]
            b_ref[rows, NS + g * NP:NS + (g + 1) * NP] = bi[g]
            c_ref[cols, rows] = cr[g]
            c_ref[NS + g * NP:NS + (g + 1) * NP, rows] = -ci[g]

    return pl.pallas_call(
        body, name="s5_embed", in_specs=[VM] * 4, out_specs=[VM] * 2,
        out_shape=[jax.ShapeDtypeStruct((NG * NH, 2 * NS), F32), jax.ShapeDtypeStruct((2 * NS, NG * NH), F32)],
        compiler_params=pltpu.CompilerParams(vmem_limit_bytes=VMEM_LIMIT),
    )(bt_re, bt_im, ct_re, ct_im)


def s5_extract(gb, gc):
    def body(gb_ref, gc_ref, br, bi, cr, ci):
        for g in range(NG):
            rows, cols = slice(g * NH, (g + 1) * NH), slice(g * NP, (g + 1) * NP)
            br[g] = gb_ref[rows, cols]
            bi[g] = gb_ref[rows, NS + g * NP:NS + (g + 1) * NP]
            cr[g] = gc_ref[cols, rows]
            ci[g] = -gc_ref[NS + g * NP:NS + (g + 1) * NP, rows]

    return pl.pallas_call(
        body, name="s5_extract", in_specs=[VM] * 2, out_specs=[VM] * 4,
        out_shape=[jax.ShapeDtypeStruct((NG, NH, NP), F32)] * 2 + [jax.ShapeDtypeStruct((NG, NP, NH), F32)] * 2,
        compiler_params=pltpu.CompilerParams(vmem_limit_bytes=VMEM_LIMIT),
    )(gb, gc)


HC, HS = NG * NH // 2, NS // 2
TS = 1024


def s5_to_states(x, w, mode, name, z_off=0):
    if mode == "nn":
        wb, wm = (HC, HS), lambda i, j, kk: (j % 2, j)
    else:
        wb, wm = (HS, HC), lambda i, j, kk: (j, j % 2)
    return mm_band(x, w, mode, name, (S // TS, 4, 1), ((TS, HC), wb, (TS, HS)),
                   (lambda i, j, kk: (i, z_off + j % 2), wm, lambda i, j, kk: (i, j)), (S, 2 * NS))


def s5_to_channels(x, w, mode, name, add=None):
    if mode == "nn":
        wb, wm = (HS, HC), lambda i, j, kk: (j + 2 * kk, j)
    else:
        wb, wm = (HC, HS), lambda i, j, kk: (j, j + 2 * kk)
    return mm_band(x, w, mode, name, (S // TS, 2, 2), ((TS, HS), wb, (TS, HC)),
                   (lambda i, j, kk: (i, j + 2 * kk), wm, lambda i, j, kk: (i, j)), (S, NG * NH), add=add)


def s5_outer(a, b, name, states_first, z_off=0):
    if states_first:
        return mm_band(a, b, "tn", name, (4, 1, 1), ((S, HS), (S, HC), (HS, HC)),
                       (lambda i, j, kk: (0, i), lambda i, j, kk: (0, i % 2), lambda i, j, kk: (i, i % 2)),
                       (2 * NS, NG * NH))
    return mm_band(a, b, "tn", name, (1, 4, 1), ((S, HC), (S, HS), (HC, HS)),
                   (lambda i, j, kk: (0, z_off + j % 2), lambda i, j, kk: (0, j), lambda i, j, kk: (j % 2, j)),
                   (NG * NH, 2 * NS))


def _fwd_even(i, x, P, W):
    hn = rms_fwd(x, P["norm_ab"][i:i + 1], "rms_ab_fwd")
    z = mm(m2(hn), W["w_in"], "nn", "in_ab")
    o, lse, cat = attn_fwd(z)
    if "more" in W:
        W.update(W.pop("more")(cat))
    cat = pool_fwd(z, W["pool_w"], P["pool_scale"][i:i + 1], cat)
    x_mid = mm(m2(cat), W["w_out"], "nn", "out_ab", add=m2(x))
    return x_mid, dict(x=x, hn=hn, z=z, o=o, lse=lse, cat=cat)


def _bwd_even(i, dx_mid, sv, P, W, G, GW):
    z = sv["z"]
    d_cat = mm(m2(dx_mid), W["w_out"], "nt", "out_ab_dx")
    GW["w_out"] = mm(m2(sv["cat"]), m2(dx_mid), "tn", "out_ab_dw").reshape(4, 512, D)
    dq, dk, dv, dga = attn_bwd(z, d_cat, sv["o"], sv["lse"])
    dvb, dgb, dpw, dps = pool_bwd(z, d_cat, W["pool_w"], P["pool_scale"][i:i + 1])
    GW["pool_w"] = dpw.reshape(4, 4, 64, 256).transpose(1, 0, 2, 3).reshape(4, 256, 256)
    G["pool_scale"][i] = dps[0]
    d_z = assemble_dz_even((dq, dk, dv, dga, dvb, dgb))
    d_hn = mm(m2(d_z), W["w_in"], "nt", "in_ab_dx")
    GW["w_in"] = mm(m2(sv["hn"]), m2(d_z), "tn", "in_ab_dw", out=outcs(D, 1536))
    return d_hn, P["norm_ab"][i:i + 1], "norm_ab", "rms_ab_bwd"


def _fwd_odd(i, x, P, W):
    hn = rms_fwd(x, P["norm_cd"][i:i + 1], "rms_cd_fwd")
    z = mm(m2(hn), W["w_in"], "nn", "in_cd")
    bfull = jnp.repeat(P["sgu_b"][i].T, 256, axis=1)
    c_out = sgu_fwd(z, P["sgu_ln_g"][i:i + 1], P["sgu_ln_b"][i:i + 1], P["sgu_w"][i], bfull)
    disc, disc_vjp = jax.vjp(_s5_disc, P["s5_a_re"][i], P["s5_a_im"][i], P["s5_log_dt"][i],
                             P["s5_b_re"][i], P["s5_b_im"][i])
    abr, abi, bbr, bbi = disc
    bbd, cbd = s5_embed(bbr.transpose(0, 2, 1), bbi.transpose(0, 2, 1),
                        P["s5_c_re"][i].transpose(0, 2, 1), P["s5_c_im"][i].transpose(0, 2, 1))
    abr, abi = abr.reshape(1, NS), abi.reshape(1, NS)
    bu = s5_to_states(z, bbd, "nn", "s5_bu", z_off=3072 // HC)
    h = scan_fwd(bu, abr, abi)
    hc = s5_to_channels(h, cbd, "nn", "s5_hc")
    dskip = P["s5_d"][i:i + 1]
    ypre, yg = s5_post(hc, z, dskip)
    if "more" in W:
        W.update(W.pop("more")(yg))
    w12 = W["w12"]
    t = mm(m2(yg), m2(w12), "nn", "glu_t")
    cat = glu_fwd(t, z, c_out)
    x_mid = mm(m2(cat), W["w_out"], "nn", "out_cd", add=m2(x))
    return x_mid, dict(x=x, hn=hn, z=z, bfull=bfull, disc_vjp=disc_vjp, bbd=bbd, cbd=cbd, abr=abr,
                       abi=abi, h=h, ypre=ypre, yg=yg, w12=w12, t=t, cat=cat, dskip=dskip)


def _bwd_odd(i, dx_mid, sv, P, W, G, GW):
    z = sv["z"]
    d_cat = mm(m2(dx_mid), W["w_out"], "nt", "out_cd_dx")
    GW["w_out"] = mm(m2(sv["cat"]), m2(dx_mid), "tn", "out_cd_dw").reshape(4, 384, D)
    du, dv, dgc, dws, dbs, dlg, dlb = sgu_bwd(z, d_cat, P["sgu_ln_g"][i:i + 1], P["sgu_ln_b"][i:i + 1],
                                               P["sgu_w"][i], sv["bfull"])
    G["sgu_w"][i], G["sgu_b"][i] = dws, dbs[:, :4].T
    G["sgu_ln_g"][i], G["sgu_ln_b"][i] = dlg[0], dlb[0]
    dt, dgd = glu_bwd(sv["t"], z, d_cat)
    gw12 = mm(m2(sv["yg"]), m2(dt), "tn", "glu_dw")
    GW["glu_w1"] = gw12[:, :512].reshape(4, 128, 512)
    GW["glu_w2"] = gw12[:, 512:].reshape(4, 128, 512)
    dyg = mm(m2(dt), m2(sv["w12"]), "nt", "glu_dx")
    dypre, dxd1, dd = s5_post_bwd(dyg, sv["ypre"], z, sv["dskip"])
    G["s5_d"][i] = dd[0]
    gcbd = s5_outer(sv["h"], dypre, "s5_dc", states_first=True)
    eta = s5_to_states(dypre, sv["cbd"], "nt", "s5_eta")
    lam, dacc = scan_bwd(eta, sv["h"], sv["abr"], sv["abi"])
    gbbd = s5_outer(z, lam, "s5_db", states_first=False, z_off=3072 // HC)
    dxd = s5_to_channels(lam, sv["bbd"], "nt", "s5_dx", add=dxd1)
    dacc = jnp.sum(dacc, axis=0)
    dbt_re, dbt_im, dct_re, dct_im = s5_extract(gbbd, gcbd)
    G["s5_c_re"][i], G["s5_c_im"][i] = dct_re.transpose(0, 2, 1), dct_im.transpose(0, 2, 1)
    d_bbr, d_bbi = dbt_re.transpose(0, 2, 1), dbt_im.transpose(0, 2, 1)
    (G["s5_a_re"][i], G["s5_a_im"][i], G["s5_log_dt"][i], G["s5_b_re"][i], G["s5_b_im"][i]) = sv["disc_vjp"](
        (dacc[:NS].reshape(NG, NP), dacc[NS:].reshape(NG, NP), d_bbr, d_bbi))
    d_z = assemble_dz_odd(du, dv, dgc, dxd, dgd)
    d_hn = mm(m2(d_z), W["w_in"], "nt", "in_cd_dx")
    GW["w_in"] = mm(m2(sv["hn"]), m2(d_z), "tn", "in_cd_dw", out=outcs(D, 1024))
    return d_hn, P["norm_cd"][i:i + 1], "norm_cd", "rms_cd_bwd"


def _fwd_x(l, x, mem_n, P, W):
    hx = rms_fwd(x, P["norm_x"][l:l + 1], "rms_x_fwd")
    q = mm(m2(hx), W["w_xq"], "nn", "xq", out_dtype=BF16)
    kv = mm(m2(mem_n), W["w_xkv"], "nn", "xkv", out_dtype=BF16)
    ox = xattn_fwd(q, kv)
    x_out = mm(m2(ox), W["w_xo"], "nn", "xo", add=m2(x))
    return x_out, dict(x=x, hx=hx, q=q, kv=kv, ox=ox)


def _bwd_x(l, dx_out, sv, mem_n, d_memn, P, W, G, GW):
    d_ox = mm(m2(dx_out), W["w_xo"], "nt", "xo_dx", out_dtype=BF16)
    GW["w_xo"] = mm(m2(sv["ox"]), m2(dx_out), "tn", "xo_dw").reshape(4, 256, D)
    dq, dkv = xattn_bwd(sv["q"], sv["kv"], d_ox)
    GW["w_xq"] = mm(m2(sv["hx"]), m2(dq), "tn", "xq_dw").reshape(4, 256, D)
    d_hx = mm(m2(dq), W["w_xq"], "nt", "xq_dx")
    GW["w_xkv"] = mm(m2(mem_n), m2(dkv), "tn", "xkv_dw", out=outcs(D, 512))
    d_memn = mm(m2(dkv), W["w_xkv"], "nt", "xkv_dx", add=None if d_memn is None else m2(d_memn))
    dx, dg = rms_bwd(sv["x"], d_hx, dx_out, P["norm_x"][l:l + 1], "rms_x_bwd")
    G["norm_x"][l] = dg[0]
    return dx, d_memn


SMALL_LAYERS = (("norm_ab", 2), ("pool_scale", 2), ("norm_cd", 2), ("sgu_ln_g", 2), ("sgu_ln_b", 2), ("sgu_w", 2),
                ("sgu_b", 2), ("s5_a_re", 2), ("s5_a_im", 2), ("s5_log_dt", 2), ("s5_b_re", 2), ("s5_b_im", 2),
                ("s5_c_re", 2), ("s5_c_im", 2), ("s5_d", 2), ("norm_x", 4))


def local_step(x, mem, tgt, P, weights_of, grads_done):
    G = {k: [None] * n for k, n in SMALL_LAYERS}
    mem_g = P["mem_norm"].reshape(1, D)
    mem_n = rms_fwd(mem, mem_g, "rms_mem_fwd")
    saved = []
    for layer in range(4):
        i = layer // 2
        W = weights_of(layer, x)
        x, sv_m = (_fwd_even if layer % 2 == 0 else _fwd_odd)(i, x, P, W)
        x, sv_x = _fwd_x(layer, x, mem_n, P, W)
        saved.append((sv_m, sv_x, W))
    dx, loss, dgf = final_loss(x, tgt, P["final_norm"].reshape(1, D))
    G["final_norm"] = dgf[0]
    d_memn = None
    for layer in reversed(range(4)):
        i = layer // 2
        sv_m, sv_x, W = saved[layer]
        GW = {}
        dx_mid, d_memn = _bwd_x(layer, dx, sv_x, mem_n, d_memn, P, W, G, GW)
        d_hn, g, key, name = (_bwd_even if layer % 2 == 0 else _bwd_odd)(i, dx_mid, sv_m, P, W, G, GW)
        token = grads_done(layer, GW)
        if token is not None:
            g = g + token
        dx, dg = rms_bwd(sv_m["x"], d_hn, dx_mid, g, name)
        G[key][i] = dg[0]
    _, dgm = rms_bwd(mem, d_memn, d_memn, mem_g, "rms_mem_bwd")
    G["mem_norm"] = dgm[0]
    return loss, dx, G


ANY = pl.BlockSpec(memory_space=pl.ANY)


def _place():
    x, y, c = lax.axis_index("x"), lax.axis_index("y"), lax.axis_index("c")
    chips = [(1 - x, y), (x, 1 - y), (1 - x, 1 - y)]
    return x, y, c, 2 * x + y, (x, y, 1 - c), chips


def _remote(src, dst, send, recv, k, dev):
    return pltpu.make_async_remote_copy(src_ref=src, dst_ref=dst, send_sem=send.at[k], recv_sem=recv.at[k],
                                        device_id=dev, device_id_type=MESHID)


HBM = pl.BlockSpec(memory_space=pltpu.HBM)
SEM = pl.BlockSpec(memory_space=pltpu.SEMAPHORE)
EFFECT = pltpu.SideEffectType.DATAFLOW_SIDE_EFFECTING


def _hbm(t):
    return pltpu.with_memory_space_constraint(t, pltpu.HBM)


def allgather_sync(shards):
    n = len(shards)

    def body(*refs):
        ins, outs = refs[:n], refs[n:2 * n]
        token, send, recv = refs[2 * n:]
        x, y, c, jme, sib, chips = _place()
        first, passed = [], []
        for a in range(n):
            cp = _remote(ins[a], outs[a].at[jme], send, recv, a * 7 + 6, sib)
            cp.start()
            first.append(cp)
            for k, chip in enumerate(chips):
                cp = _remote(ins[a].at[c], outs[a].at[jme, c], send, recv, a * 7 + k, (*chip, c))
                cp.start()
                first.append(cp)
        for a in range(n):
            for k, chip in enumerate(chips):
                piece = outs[a].at[2 * chip[0] + chip[1], c]
                _remote(piece, piece, send, recv, a * 7 + k, (*chip, c)).wait_recv()
                fw = _remote(piece, piece, send, recv, a * 7 + 3 + k, sib)
                fw.start()
                passed.append(fw)
        for a in range(n):
            own = outs[a].at[jme]
            _remote(own, own, send, recv, a * 7 + 6, sib).wait_recv()
            for k, chip in enumerate(chips):
                piece = outs[a].at[2 * chip[0] + chip[1], 1 - c]
                _remote(piece, piece, send, recv, a * 7 + 3 + k, sib).wait_recv()
        for cp in first + passed:
            cp.wait_send()
        token[...] = jnp.zeros_like(token)

    res = pl.pallas_call(
        body, name="allgather_sync", in_specs=[ANY] * n,
        out_specs=[ANY] * n + [pl.BlockSpec(memory_space=pltpu.VMEM)],
        out_shape=[jax.ShapeDtypeStruct((4,) + s.shape, s.dtype) for s in shards] + [jax.ShapeDtypeStruct((8, 128), F32)],
        scratch_shapes=[pltpu.SemaphoreType.DMA((7 * n,)), pltpu.SemaphoreType.DMA((7 * n,))],
    )(*shards)
    return list(res[:n]), res[n]


def _gather_copies(ins, lands, send, recv):
    x, y, c, jme, sib, chips = _place()
    devs = [(*chip, c) for chip in chips] + [sib]
    return [_remote(ins[a], lands[a].at[jme], send, recv, a * 4 + k, dev)
            for a in range(len(ins)) for k, dev in enumerate(devs)]


def allgather_start(shards, after, name):
    n, na = len(shards), len(after)

    def body(*refs):
        ins, lands = refs[:n], refs[n:2 * n]
        send, recv = refs[2 * n + na], refs[2 * n + na + 1]
        token = refs[-1]
        for cp in _gather_copies(ins, lands, send, recv):
            cp.start()
        token[...] = jnp.zeros_like(token)

    res = pl.pallas_call(
        body, name=name,
        out_shape=(pltpu.SemaphoreType.DMA((4 * n,)), pltpu.SemaphoreType.DMA((4 * n,)),
                   *[pltpu.HBM(s.shape, s.dtype) for s in shards],
                   *[pltpu.HBM((4,) + s.shape, s.dtype) for s in shards],
                   jax.ShapeDtypeStruct((8, 128), F32)),
        in_specs=[HBM] * (2 * n) + [ANY] * na,
        out_specs=(SEM, SEM, *[HBM] * (2 * n), pl.BlockSpec(memory_space=pltpu.VMEM)),
        input_output_aliases={a: 2 + a for a in range(2 * n)},
        compiler_params=pltpu.CompilerParams(has_side_effects=EFFECT),
    )(*[_hbm(s) for s in shards], *[_hbm(lax.empty((4,) + s.shape, s.dtype)) for s in shards], *after)
    return res[0], res[1], list(res[2:2 + n]), list(res[2 + n:2 + 2 * n]), res[-1]


def allgather_wait(send, recv, shards, lands, after, name):
    n = len(shards)

    def body(*refs):
        ins, zones = refs[:n], refs[n:2 * n]
        send_r, recv_r = refs[2 * n], refs[2 * n + 1]
        x, y, c, jme, sib, chips = _place()
        slots = [2 * chip[0] + chip[1] for chip in chips] + [jme]
        for a in range(n):
            for k, slot in enumerate(slots):
                cp = _remote(ins[a], zones[a].at[slot], send_r, recv_r, a * 4 + k, sib)
                cp.wait_send()
                cp.wait_recv()

    res = pl.pallas_call(
        body, name=name,
        out_shape=tuple(pltpu.HBM(t.shape, t.dtype) for t in list(shards) + list(lands)),
        in_specs=[HBM] * (2 * n) + [SEM, SEM, ANY], out_specs=tuple([HBM] * (2 * n)),
        input_output_aliases={a: a for a in range(2 * n)},
        compiler_params=pltpu.CompilerParams(has_side_effects=EFFECT),
    )(*shards, *lands, send, recv, after)
    return list(res[n:])


def allgather_small(slab):
    def body(in_ref, out_ref, send, recv, lsem):
        x, y, c, jme, sib, chips = _place()
        loc = pltpu.make_async_copy(in_ref, out_ref.at[jme], lsem.at[0])
        loc.start()
        cps = [_remote(in_ref, out_ref.at[jme], send, recv, k, (*chip, c)) for k, chip in enumerate(chips)]
        for cp in cps:
            cp.start()
        for k, chip in enumerate(chips):
            piece = out_ref.at[2 * chip[0] + chip[1]]
            _remote(piece, piece, send, recv, k, (*chip, c)).wait_recv()
        for cp in cps:
            cp.wait_send()
        loc.wait()

    return pl.pallas_call(
        body, name="allgather_small", in_specs=[ANY], out_specs=ANY,
        out_shape=jax.ShapeDtypeStruct((4,) + slab.shape, slab.dtype),
        scratch_shapes=[pltpu.SemaphoreType.DMA((3,)), pltpu.SemaphoreType.DMA((3,)), pltpu.SemaphoreType.DMA((1,))],
    )(slab)


def allreduce_small(v):
    def body(v_ref, o_ref, r0, r1, r2, send, recv):
        x, y, c, jme, sib, chips = _place()
        peers = [sib, (1 - x, y, c), (x, 1 - y, c)]
        o_ref[...] = v_ref[...]
        for k, buf in enumerate((r0, r1, r2)):
            cp = _remote(o_ref, buf, send, recv, k, peers[k])
            cp.start()
            cp.wait()
            o_ref[...] = o_ref[...] + buf[...]

    vm = pl.BlockSpec(memory_space=pltpu.VMEM)
    return pl.pallas_call(
        body, name="allreduce_small", in_specs=[vm], out_specs=vm,
        out_shape=jax.ShapeDtypeStruct(v.shape, v.dtype),
        scratch_shapes=[pltpu.VMEM(v.shape, v.dtype)] * 3 + [pltpu.SemaphoreType.DMA((3,)), pltpu.SemaphoreType.DMA((3,))],
        compiler_params=pltpu.CompilerParams(vmem_limit_bytes=VMEM_LIMIT),
    )(v)


def _pair_copies(gs, lands, send, recv):
    x, y, c, jme, sib, chips = _place()
    return [_remote(gs[a].at[:, 1 - c], lands[a], send, recv, a, sib) for a in range(len(gs))]


def rs_pair_start(gs, name):
    n = len(gs)

    def body(*refs):
        ins, lands = refs[:n], refs[n:2 * n]
        send, recv = refs[2 * n], refs[2 * n + 1]
        token = refs[-1]
        for cp in _pair_copies(ins, lands, send, recv):
            cp.start()
        token[...] = jnp.zeros_like(token)

    shapes = [(4,) + g.shape[2:] for g in gs]
    res = pl.pallas_call(
        body, name=name,
        out_shape=(pltpu.SemaphoreType.DMA((n,)), pltpu.SemaphoreType.DMA((n,)),
                   *[pltpu.HBM(g.shape, g.dtype) for g in gs], *[pltpu.HBM(s, F32) for s in shapes],
                   jax.ShapeDtypeStruct((8, 128), F32)),
        in_specs=[HBM] * (2 * n), out_specs=(SEM, SEM, *[HBM] * (2 * n), pl.BlockSpec(memory_space=pltpu.VMEM)),
        input_output_aliases={a: 2 + a for a in range(2 * n)},
        compiler_params=pltpu.CompilerParams(has_side_effects=EFFECT),
    )(*[_hbm(g) for g in gs], *[_hbm(lax.empty(s, F32)) for s in shapes])
    return res[0], res[1], list(res[2:2 + n]), list(res[2 + n:2 + 2 * n]), res[-1]


def rs_pair_wait(send, recv, gs, lands, after, name):
    n = len(gs)

    def body(*refs):
        ins, zones = refs[:n], refs[n:2 * n]
        for cp in _pair_copies(ins, zones, refs[2 * n], refs[2 * n + 1]):
            cp.wait_send()
            cp.wait_recv()

    res = pl.pallas_call(
        body, name=name,
        out_shape=tuple(pltpu.HBM(t.shape, t.dtype) for t in list(gs) + list(lands)),
        in_specs=[HBM] * (2 * n) + [SEM, SEM, ANY], out_specs=tuple([HBM] * (2 * n)),
        input_output_aliases={a: a for a in range(2 * n)},
        compiler_params=pltpu.CompilerParams(has_side_effects=EFFECT),
    )(*gs, *lands, send, recv, after)
    return list(res[:n]), list(res[n:])


SUM_ROWS = 256


def rs_pair_sum(g4s, gots, cidx):
    n = len(g4s)
    tiles = [(min(g.shape[2], SUM_ROWS), g.shape[3]) for g in g4s]
    nts = [g.shape[2] // tr for g, (tr, _) in zip(g4s, tiles)]

    def at(a, s):
        s = jnp.minimum(s, 4 * nts[a] - 1)
        return s // nts[a], s % nts[a]

    def body(c_ref, *refs):
        for a in range(n):
            refs[2 * n + a][...] = (refs[a][...] + refs[n + a][...]).astype(BF16)

    in_specs = [pl.BlockSpec((None, None) + tiles[a], lambda s, cr, a=a: (at(a, s)[0], cr[0], at(a, s)[1], 0))
                for a in range(n)]
    in_specs += [pl.BlockSpec((None,) + tiles[a], lambda s, cr, a=a: (*at(a, s), 0)) for a in range(n)]
    return pl.pallas_call(
        body, name="rs_pair_sum",
        grid_spec=pltpu.PrefetchScalarGridSpec(
            num_scalar_prefetch=1, grid=(4 * max(nts),), in_specs=in_specs,
            out_specs=[pl.BlockSpec((None,) + tiles[a], lambda s, cr, a=a: (*at(a, s), 0)) for a in range(n)]),
        out_shape=[jax.ShapeDtypeStruct((4,) + g.shape[2:], BF16) for g in g4s],
        compiler_params=_cparams(("arbitrary",)),
    )(cidx, *g4s, *gots)


def _chip_copies(ps, lands, send, recv):
    x, y, c, jme, sib, chips = _place()
    return [_remote(ps[a].at[2 * chip[0] + chip[1]], lands[a].at[jme], send, recv, a * 3 + k, (*chip, c))
            for a in range(len(ps)) for k, chip in enumerate(chips)]


def rs_chip_start(ps, name):
    n = len(ps)

    def body(*refs):
        ins, lands = refs[:n], refs[n:2 * n]
        send, recv = refs[2 * n], refs[2 * n + 1]
        token = refs[-1]
        for cp in _chip_copies(ins, lands, send, recv):
            cp.start()
        token[...] = jnp.zeros_like(token)

    res = pl.pallas_call(
        body, name=name,
        out_shape=(pltpu.SemaphoreType.DMA((3 * n,)), pltpu.SemaphoreType.DMA((3 * n,)),
                   *[pltpu.HBM(p.shape, p.dtype) for p in ps], *[pltpu.HBM(p.shape, p.dtype) for p in ps],
                   jax.ShapeDtypeStruct((8, 128), F32)),
        in_specs=[HBM] * (2 * n), out_specs=(SEM, SEM, *[HBM] * (2 * n), pl.BlockSpec(memory_space=pltpu.VMEM)),
        input_output_aliases={a: 2 + a for a in range(2 * n)},
        compiler_params=pltpu.CompilerParams(has_side_effects=EFFECT),
    )(*[_hbm(p) for p in ps], *[_hbm(lax.empty(p.shape, p.dtype)) for p in ps])
    return res[0], res[1], list(res[2:2 + n]), list(res[2 + n:2 + 2 * n]), res[-1]


def rs_chip_wait(send, recv, ps, lands, after, name):
    n = len(ps)

    def body(*refs):
        ins, zones = refs[:n], refs[n:2 * n]
        send_r, recv_r = refs[2 * n], refs[2 * n + 1]
        x, y, c, jme, sib, chips = _place()
        for a in range(n):
            for k, chip in enumerate(chips):
                jt = 2 * chip[0] + chip[1]
                cp = _remote(ins[a].at[jt], zones[a].at[jt], send_r, recv_r, a * 3 + k, (*chip, c))
                cp.wait_send()
                cp.wait_recv()

    res = pl.pallas_call(
        body, name=name,
        out_shape=tuple(pltpu.HBM(p.shape, p.dtype) for p in list(ps) + list(lands)),
        in_specs=[HBM] * (2 * n) + [SEM, SEM] + [ANY] * len(after), out_specs=tuple([HBM] * (2 * n)),
        input_output_aliases={a: a for a in range(2 * n)},
        compiler_params=pltpu.CompilerParams(has_side_effects=EFFECT),
    )(*ps, *lands, send, recv, *after)
    return list(res[:n]), list(res[n:])


def rs_chip_sum(qs, ps, ls, accs, layers, jc):
    n = len(qs)
    tiles = [(min(q.shape[1], SUM_ROWS), q.shape[2]) for q in qs]
    nts = [q.shape[1] // tr for q, (tr, _) in zip(qs, tiles)]

    def at(a, s):
        return jnp.minimum(s, nts[a] - 1)

    def body(jc_ref, *refs):
        jme = jc_ref[0]
        for a in range(n):
            q_ref, p_ref, o_ref = refs[a], refs[n + a], refs[len(refs) - n + a]
            own = p_ref[...].astype(F32)
            v = [jnp.where(jme == j, own, q_ref[j].astype(F32)) for j in range(4)]
            o_ref[...] = ((v[0] + v[1]) + v[2]) + v[3]

    in_specs = [pl.BlockSpec((4,) + tiles[a], lambda s, jr, a=a: (0, at(a, s), 0)) for a in range(n)]
    in_specs += [pl.BlockSpec((None,) + tiles[a], lambda s, jr, a=a: (jr[0], at(a, s), 0)) for a in range(n)]
    args, aliases = [jc, *qs, *ps], {}
    for a in range(n):
        if accs[a] is not None:
            aliases[len(args)] = a
            in_specs.append(ANY)
            args.append(accs[a])
    return pl.pallas_call(
        body, name="rs_chip_sum",
        grid_spec=pltpu.PrefetchScalarGridSpec(
            num_scalar_prefetch=1, grid=(max(nts),), in_specs=in_specs,
            out_specs=[pl.BlockSpec((None, None) + tiles[a], lambda s, jr, a=a: (ls[a], jr[1], at(a, s), 0))
                       for a in range(n)]),
        out_shape=[jax.ShapeDtypeStruct((layers[a], 2) + qs[a].shape[1:], F32) for a in range(n)],
        input_output_aliases=aliases,
        compiler_params=_cparams(("arbitrary",)),
    )(*args)


def rs_pair_gather(rs):
    n = len(rs)

    def body(*refs):
        outs = refs[n:2 * n]
        send, recv = refs[2 * n:]
        x, y, c, jme, sib, chips = _place()
        cps = [_remote(outs[a].at[:, c], outs[a].at[:, c], send, recv, a, sib) for a in range(n)]
        for cp in cps:
            cp.start()
        for a in range(n):
            slot = outs[a].at[:, 1 - c]
            _remote(slot, slot, send, recv, a, sib).wait_recv()
        for cp in cps:
            cp.wait_send()

    return pl.pallas_call(
        body, name="rs_pair_gather", in_specs=[ANY] * n, out_specs=[ANY] * n,
        out_shape=[jax.ShapeDtypeStruct(r.shape, r.dtype) for r in rs],
        input_output_aliases={a: a for a in range(n)},
        scratch_shapes=[pltpu.SemaphoreType.DMA((n,)), pltpu.SemaphoreType.DMA((n,))],
    )(*rs)


def _adamw_math(w, g, m, v):
    m = B1 * m + (1.0 - B1) * g
    v = B2 * v + (1.0 - B2) * (g * g)
    m_hat = m / (1.0 - B1 ** STEP)
    v_hat = v / (1.0 - B2 ** STEP)
    return -LR * (m_hat / (jnp.sqrt(v_hat) + AEPS) + WD * w), m, v


def adamw(w, g, m, v, name, with_grad=False):
    rows, cols = w.shape
    tr = 256 if rows % 256 == 0 else rows
    fn = (lambda wv, gv, mv, vv: (gv,) + _adamw_math(wv, gv, mv, vv)) if with_grad else _adamw_math
    return rw(fn, [(a, 0, cols) for a in (w, g, m, v)], [(cols, F32)] * (4 if with_grad else 3), name, rows, tr=tr)


def adamw_small(ws, gs, ms, vs):
    n = len(ws)

    def body(*refs):
        for a in range(n):
            res = _adamw_math(*[refs[k * n + a][...] for k in range(4)])
            for k in range(3):
                refs[(4 + k) * n + a][...] = res[k]

    res = pl.pallas_call(
        body, name="adamw_small", in_specs=[VM] * (4 * n), out_specs=[VM] * (3 * n),
        out_shape=[jax.ShapeDtypeStruct(w.shape, F32) for _ in range(3) for w in ws],
        compiler_params=pltpu.CompilerParams(vmem_limit_bytes=VMEM_LIMIT),
    )(*ws, *gs, *ms, *vs)
    return [(res[a], res[n + a], res[2 * n + a]) for a in range(n)]


WEIGHTS = ["norm_ab", "w_in_ab", "pool_w", "pool_scale", "w_out_ab", "norm_cd", "w_in_cd", "sgu_ln_g", "sgu_ln_b",
           "sgu_w", "sgu_b", "s5_a_re", "s5_a_im", "s5_log_dt", "s5_b_re", "s5_b_im", "s5_c_re", "s5_c_im", "s5_d",
           "glu_w1", "glu_w2", "w_out_cd", "norm_x", "w_xq", "w_xkv", "w_xo", "mem_norm", "final_norm"]
INPUTS = ["x", "mem"] + WEIGHTS + ["loss_target"] + ["m_" + n for n in WEIGHTS] + ["v_" + n for n in WEIGHTS]
BIG = ["w_in_ab", "w_out_ab", "w_in_cd", "w_out_cd", "w_xq", "w_xkv", "w_xo", "glu_w1", "glu_w2", "pool_w"]
COL_SHARDED = ("w_in_ab", "w_in_cd", "w_xkv")
SMALL = [n for n in WEIGHTS if n not in BIG]
SMALL_SHARDED = {"norm_cd": 256, "sgu_ln_g": 256, "sgu_ln_b": 256, "s5_d": 128}
PACK = 256 * 128


def _pack(arrs):
    flat = jnp.concatenate([a.reshape(-1) for a in arrs])
    pad = (-flat.shape[0]) % PACK
    return jnp.concatenate([flat, jnp.zeros((pad,), flat.dtype)]).reshape(-1, 128)


def _unpack(packed, shapes):
    flat, out, off = packed.reshape(-1), [], 0
    for s in shapes:
        n = 1
        for d in s:
            n *= d
        out.append(flat[off:off + n].reshape(s))
        off += n
    return out


LAYER_KEYS = (("w_in", "w_out", "pool_w", "w_xq", "w_xkv", "w_xo"),
              ("w_in", "w_out", "glu_w1", "glu_w2", "w_xq", "w_xkv", "w_xo"))


def _weight_of(key, layer):
    if key in ("w_xq", "w_xkv", "w_xo"):
        return key, layer, 4
    kind = "ab" if layer % 2 == 0 else "cd"
    return {"w_in": "w_in_" + kind, "w_out": "w_out_" + kind}.get(key, key), layer // 2, 2


def kernel(*args):
    a = dict(zip(INPUTS, args))
    x_i, y_i, c_i = lax.axis_index("x"), lax.axis_index("y"), lax.axis_index("c")
    j = 2 * x_i + y_i

    slab = jnp.concatenate([a["norm_cd"], a["sgu_ln_g"], a["sgu_ln_b"],
                            jnp.pad(a["s5_d"], ((0, 0), (0, 128)))], axis=0)
    gslab = allgather_small(slab)
    P = {n: a[n] for n in SMALL}
    for k, n in enumerate(("norm_cd", "sgu_ln_g", "sgu_ln_b", "s5_d")):
        wd = SMALL_SHARDED[n]
        P[n] = gslab[:, 2 * k:2 * k + 2, :wd].transpose(1, 0, 2).reshape(2, 4 * wd)

    def shards_of(layer):
        keys = sorted(k for k in LAYER_KEYS[layer % 2])
        out = []
        for k in keys:
            n, l, _ = _weight_of(k, layer)
            out.append(a[n][l].reshape(-1, a[n].shape[-1]).astype(BF16))
        return keys, out

    keys0, sh0 = shards_of(0)
    first = keys0.index("w_in")
    g_in, token = allgather_sync([sh0[first].reshape(2, sh0[first].shape[0] // 2, sh0[first].shape[1])])
    w_in0 = g_in[0].reshape(4, -1, g_in[0].shape[-1])
    started = {}
    for layer in (0, 1, 2, 3):
        keys, sh = (keys0, sh0) if layer == 0 else shards_of(layer)
        rest = [(k, s) for k, s in zip(keys, sh) if k != "w_in"]
        parts = [("in", ["w_in"], [sh[keys.index("w_in")]])] * (layer > 0) + [("", *map(list, zip(*rest)))]
        for tag, pk, ps in parts:
            send, recv, ps, lands, token = allgather_start(ps, [token, gslab], "allgather_start_%d%s" % (layer, tag))
            started[(layer, tag)] = (pk, send, recv, ps, lands)
    P["norm_ab"] = P["norm_ab"] + token[0:1, 0:1]

    cidx = jnp.reshape(c_i, (1,)).astype(jnp.int32)
    jc = jnp.stack([j, c_i]).astype(jnp.int32)

    def views(g):
        W = {}
        for k, v in g.items():
            if k in ("w_in", "w_xkv"):
                W[k] = mcs(v)
            elif k == "pool_w":
                W[k] = v.reshape(4, 4, 64, 256).transpose(1, 0, 2, 3).reshape(4, 256, 256)
            elif k not in ("glu_w1", "glu_w2"):
                W[k] = m2(v.reshape(-1, v.shape[-1]))
        if "glu_w1" in g:
            W["w12"] = jnp.concatenate([g["glu_w1"].reshape(512, 512), g["glu_w2"].reshape(512, 512)], axis=1)
        return W

    def arrived(layer, tag, after):
        keys, send, recv, sh, lands = started[(layer, tag)]
        return views(dict(zip(keys, allgather_wait(send, recv, sh, lands, after, "allgather_wait_%d%s" % (layer, tag)))))

    def weights_of(layer, x_in):
        W = views({"w_in": w_in0}) if layer == 0 else arrived(layer, "in", x_in)
        W["more"] = lambda after: arrived(layer, "", after)
        return W

    halves, pending = {}, {}

    def finish_pair(layer, after):
        keys, send, recv, flat, lands = halves.pop(layer)
        flat, got = rs_pair_wait(send, recv, flat, lands, after, "rs_pair_wait_%d" % layer)
        pair = rs_pair_sum(flat, got, cidx)
        send, recv, pair, lands, token = rs_chip_start(pair, "rs_chip_start_%d" % layer)
        pending[layer] = (keys, send, recv, pair, lands)
        return token

    def grads_done(layer, GW):
        keys = sorted(GW)
        flat = [GW[k].reshape(4, 2, GW[k].shape[1] // 2, GW[k].shape[2]) for k in keys]
        send, recv, flat, lands, token = rs_pair_start(flat, "rs_pair_start_%d" % layer)
        halves[layer] = (keys, send, recv, flat, lands)
        if layer + 1 in halves:
            token = token + finish_pair(layer + 1, token)
        return token[0:1, 0:1]

    loss, dx, G = local_step(a["x"][0], a["mem"][0], a["loss_target"][0], P, weights_of, grads_done)
    loss = lax.psum(loss[0, 0], ("x", "y", "c"))
    finish_pair(0, dx)
    outs = {}

    def update_big(names, red):
        for n, g in zip(names, rs_pair_gather([red[n] for n in names])):
            shp = a[n].shape
            g2 = g.reshape(-1, shp[-1])
            upd = adamw(a[n].reshape(g2.shape), g2, a["m_" + n].reshape(g2.shape), a["v_" + n].reshape(g2.shape),
                        "adamw_" + n, with_grad=True)
            outs[n] = tuple(t.reshape(shp) for t in upd)

    def reduce_layer(layer, red, after):
        keys, send, recv, pair, lands = pending[layer]
        pair, lands = rs_chip_wait(send, recv, pair, lands, after, "rs_chip_wait_%d" % layer)
        which = [_weight_of(k, layer) for k in keys]
        sums = rs_chip_sum(lands, pair, [l for _, l, _ in which], [red.get(n) for n, _, _ in which],
                           [layers for _, _, layers in which], jc)
        red.update(zip([n for n, _, _ in which], sums))

    red = {}
    for layer in (3, 2, 1):
        reduce_layer(layer, red, [dx])
    odd_only = [n for n in BIG if n.endswith("_cd") or n.startswith("glu")]
    update_big(odd_only, red)

    gfull = [jnp.stack(G[n]) if isinstance(G[n], list) else G[n] for n in SMALL]
    shapes = [g.shape for g in gfull]
    gsum = _unpack(allreduce_small(_pack(gfull)), shapes)
    gloc = []
    for n, g in zip(SMALL, gsum):
        if n in SMALL_SHARDED:
            g = lax.dynamic_slice_in_dim(g, j * SMALL_SHARDED[n], SMALL_SHARDED[n], axis=1)
        gloc.append(g)
    two = [(-1, a[n].shape[-1]) if a[n].ndim > 1 else (1, a[n].shape[0]) for n in SMALL]
    upds = adamw_small(*[[t.reshape(s) for t, s in zip(ts, two)]
                         for ts in ([a[n] for n in SMALL], gloc, [a["m_" + n] for n in SMALL],
                                    [a["v_" + n] for n in SMALL])])
    for n, g, upd in zip(SMALL, gloc, upds):
        outs[n] = (g,) + tuple(t.reshape(a[n].shape) for t in upd)

    behind = [outs[n][1] for n in odd_only + SMALL[-1:]] + [red[n] for n in BIG if n not in odd_only]
    reduce_layer(0, red, behind)
    update_big([n for n in BIG if n not in odd_only], red)

    res = [loss, dx[None]]
    for part in range(4):
        res += [outs[n][part] for n in WEIGHTS]
    return tuple(res)
```

```python
import math

import jax
import jax.numpy as jnp
from jax import lax
from jax.experimental import pallas as pl
from jax.experimental.pallas import tpu as pltpu

F32, BF16 = jnp.float32, jnp.bfloat16
S, D = 2048, 1024
MEM = 256
EPS = 1e-6
NEG = -1e30
QB = 128
PATTERNS = (1, 4, 16)
NG, NP, NH = 32, 64, 16
NS = NG * NP
LR, B1, B2, AEPS, WD, STEP = 0.001, 0.9, 0.999, 1e-08, 0.01, 10
MESHID = pl.DeviceIdType.MESH
VMEM_LIMIT = 56 * 1024 * 1024


def _cparams(sem):
    return pltpu.CompilerParams(dimension_semantics=sem, vmem_limit_bytes=VMEM_LIMIT)


def _sig(x):
    return 1.0 / (1.0 + jnp.exp(-x))


def _dot(a, b, dims):
    return lax.dot_general(a, b, (dims, ((), ())), preferred_element_type=F32)


def _nn(a, b):
    return _dot(a, b, ((1,), (0,)))


def _nt(a, b):
    return _dot(a, b, ((1,), (1,)))


def _tn(a, b):
    return _dot(a, b, ((0,), (0,)))


_DIMS = {"nn": ((1,), (0,)), "nt": ((1,), (1,)), "tn": ((0,), (0,))}


def _tile(dim, cc=None, cap=1024):
    for t in (2048, 1536, 1024, 768, 512, 384, 256, 128):
        if t <= cap and dim % t == 0 and (cc is None or cc % t == 0):
            return t
    return dim


MM_VMEM = 36 * 1024 * 1024


def _mm_tiles(m, n, k, ccm, ccn, cck, a_bytes, b_bytes, o_bytes):
    caps = [1024, 1024, 2048]
    while True:
        tm, tn, tk = _tile(m, ccm, caps[0]), _tile(n, ccn, caps[1]), _tile(k, cck, caps[2])
        need = 2 * (tm * tk * a_bytes + tk * tn * b_bytes + tm * tn * o_bytes) + (tm * tn * 4 if tk < k else 0)
        if need <= MM_VMEM:
            return tm, tn, tk
        if tk > 1024:
            caps[2] = tk // 2
        elif tn >= tm:
            caps[1] = tn // 2
        else:
            caps[0] = tm // 2


def m2(arr, col_off=0, ncols=None):
    rows, cols = arr.shape
    ncols = cols - col_off if ncols is None else ncols

    def spec(tr, tc, rc):
        assert col_off % tc == 0
        return pl.BlockSpec((tr, tc), lambda *g: (rc(*g)[0], rc(*g)[1] + col_off // tc))
    return (arr, rows, ncols, spec, None if col_off == 0 else col_off)


def mcs(arr):
    cs = arr.shape[2]

    def spec(tr, tc, rc):
        n = cs // tc
        return pl.BlockSpec((None, tr, tc), lambda *g: (rc(*g)[1] // n, rc(*g)[0], rc(*g)[1] % n))
    return (arr, arr.shape[1], 4 * cs, spec, cs)


def out2(rows, cols):
    def spec(tr, tc, rc):
        return pl.BlockSpec((tr, tc), lambda *g: tuple(rc(*g)))
    return ((rows, cols), spec, None)


def outcs(rows, cs):
    def spec(tr, tc, rc):
        n = cs // tc
        return pl.BlockSpec((None, tr, tc), lambda *g: (rc(*g)[1] // n, rc(*g)[0], rc(*g)[1] % n))
    return ((4, rows, cs), spec, cs)


def _both(a, b):
    if a is None:
        return b
    if b is None:
        return a
    return math.gcd(a, b)


def mm(a, b, mode, name, add=None, out=None, out_dtype=F32):
    a_arr, a_r, a_c, a_spec, a_cc = a
    b_arr, b_r, b_c, b_spec, b_cc = b
    if mode == "nn":
        m, k, n = a_r, a_c, b_c
        assert b_r == k
        ccm, cck, ccn = None, a_cc, b_cc
    elif mode == "nt":
        m, k, n = a_r, a_c, b_r
        assert b_c == k
        ccm, cck, ccn = None, _both(a_cc, b_cc), None
    else:
        m, k, n = a_c, a_r, b_c
        assert b_r == k
        ccm, cck, ccn = a_cc, None, b_cc
    out = out2(m, n) if out is None else out
    o_shape, o_spec, o_cc = out
    ccn = _both(ccn, o_cc)
    if add is not None:
        ccn = _both(ccn, add[4])
    o_bytes = jnp.dtype(out_dtype).itemsize + (0 if add is None else add[0].dtype.itemsize)
    tm, tn, tk = _mm_tiles(m, n, k, ccm, ccn, cck, a_arr.dtype.itemsize, b_arr.dtype.itemsize, o_bytes)
    nk = k // tk
    if mode == "nn":
        in_specs = [a_spec(tm, tk, lambda i, j, kk: (i, kk)), b_spec(tk, tn, lambda i, j, kk: (kk, j))]
    elif mode == "nt":
        in_specs = [a_spec(tm, tk, lambda i, j, kk: (i, kk)), b_spec(tn, tk, lambda i, j, kk: (j, kk))]
    else:
        in_specs = [a_spec(tk, tm, lambda i, j, kk: (kk, i)), b_spec(tk, tn, lambda i, j, kk: (kk, j))]
    args = [a_arr, b_arr]
    if add is not None:
        in_specs.append(add[3](tm, tn, lambda i, j, kk: (i, j)))
        args.append(add[0])
    return _mm_call(args, in_specs, o_spec(tm, tn, lambda i, j, kk: (i, j)), jax.ShapeDtypeStruct(o_shape, out_dtype),
                    mode, (m // tm, n // tn, nk), (tm, tn), add is not None, name)


def _mm_call(args, in_specs, out_spec, out_shape, mode, grid, tile, has_add, name):
    dims = _DIMS[mode]
    nk = grid[2]
    tm, tn = tile

    def body(*refs):
        a_ref, b_ref = refs[0], refs[1]
        add_ref = refs[2] if has_add else None
        prod = _dot(a_ref[...].astype(BF16), b_ref[...].astype(BF16), dims)
        if nk == 1:
            o_ref = refs[-1]
            if has_add:
                prod = prod + add_ref[...].astype(F32)
            o_ref[...] = prod.astype(o_ref.dtype)
            return
        o_ref, acc = refs[-2], refs[-1]
        kk = pl.program_id(2)

        @pl.when(kk == 0)
        def _():
            acc[...] = prod

        @pl.when(kk > 0)
        def _():
            acc[...] += prod

        @pl.when(kk == nk - 1)
        def _():
            r = acc[...]
            if has_add:
                r = r + add_ref[...].astype(F32)
            o_ref[...] = r.astype(o_ref.dtype)

    return pl.pallas_call(
        body, name=name, grid=grid, in_specs=in_specs, out_specs=out_spec, out_shape=out_shape,
        scratch_shapes=[pltpu.VMEM((tm, tn), F32)] if nk > 1 else [],
        compiler_params=_cparams(("parallel", "parallel", "arbitrary")),
    )(*args)


def mm_band(a, b, mode, name, grid, blocks, maps, out_shape, add=None, out_dtype=F32):
    in_specs = [pl.BlockSpec(blocks[0], maps[0]), pl.BlockSpec(blocks[1], maps[1])]
    args = [a, b]
    if add is not None:
        in_specs.append(pl.BlockSpec(blocks[2], maps[2]))
        args.append(add)
    return _mm_call(args, in_specs, pl.BlockSpec(blocks[2], maps[2]), jax.ShapeDtypeStruct(out_shape, out_dtype),
                    mode, grid, blocks[2], add is not None, name)


def rw(fn, ins, outs, name, rows, tr=256, consts=(), accs=()):
    n_in, n_c, n_o, n_a = len(ins), len(consts), len(outs), len(accs)
    in_specs = []
    for arr, off, width in ins:
        assert off % width == 0
        in_specs.append(pl.BlockSpec((tr, width), lambda i, o=off // width: (i, o)))
    for c in consts:
        in_specs.append(pl.BlockSpec(c.shape, lambda i: (0, 0)))
    out_specs = [pl.BlockSpec((tr, w), lambda i: (i, 0)) for w, _ in outs]
    out_specs += [pl.BlockSpec(s, lambda i: (0, 0)) for s in accs]
    out_shape = [jax.ShapeDtypeStruct((rows, w), dt) for w, dt in outs]
    out_shape += [jax.ShapeDtypeStruct(s, F32) for s in accs]

    def body(*refs):
        vals = [r[...] for r in refs[:n_in + n_c]]
        o_refs = refs[n_in + n_c:n_in + n_c + n_o]
        a_refs = refs[n_in + n_c + n_o:]
        res = fn(*vals)
        for r, v in zip(o_refs, res[:n_o]):
            r[...] = v.astype(r.dtype)
        if n_a:
            @pl.when(pl.program_id(0) == 0)
            def _():
                for r in a_refs:
                    r[...] = jnp.zeros_like(r)
            for r, v in zip(a_refs, res[n_o:]):
                r[...] += v

    res = pl.pallas_call(
        body, name=name, grid=(rows // tr,), in_specs=in_specs, out_specs=out_specs,
        out_shape=out_shape,
        compiler_params=_cparams(("arbitrary",) if n_a else ("parallel",)),
    )(*[a for a, _, _ in ins], *consts)
    return res


def _rstd(x):
    return lax.rsqrt(jnp.mean(x * x, axis=-1, keepdims=True) + EPS)


def rms_fwd(x, g, name):
    def fn(xv, gv):
        xv = xv.astype(F32)
        return (xv * _rstd(xv) * gv,)
    return rw(fn, [(x, 0, D)], [(D, BF16)], name, x.shape[0], consts=[g])[0]


def _rms_bwd_math(xv, dy, gv):
    r = _rstd(xv)
    dyg = dy * gv
    dx = r * dyg - xv * (r * r * r / D) * jnp.sum(dyg * xv, axis=-1, keepdims=True)
    dg = jnp.sum(dy * xv * r, axis=0, keepdims=True)
    return dx, dg


def rms_bwd(x, dy, dres, g, name):
    def fn(xv, dyv, drv, gv):
        dx, dg = _rms_bwd_math(xv, dyv, gv)
        return dx + drv, dg
    return rw(fn, [(x, 0, D), (dy, 0, D), (dres, 0, D)], [(D, F32)], name, x.shape[0],
              consts=[g], accs=[(1, D)])


def final_loss(x, tgt, g):
    def fn(xv, tv, gv):
        e = xv * _rstd(xv) * gv - tv
        loss = 0.5 * jnp.sum(jnp.sum(e * e, axis=-1, keepdims=True), axis=0, keepdims=True) / D
        dx, dg = _rms_bwd_math(xv, e / D, gv)
        return dx, loss, dg
    return rw(fn, [(x, 0, D), (tgt, 0, D)], [(D, F32)], "final_loss", S, consts=[g],
              accs=[(1, 1), (1, D)])


def _attn_bias(bias_ref):
    ii = lax.broadcasted_iota(jnp.int32, (2 * QB, 2 * QB), 0) % QB
    jj = lax.broadcasted_iota(jnp.int32, (2 * QB, 2 * QB), 1)
    dist = ii + QB - jj
    band = (dist >= 0) & (dist <= QB)
    bias_ref[1] = jnp.where(band, 0.0, NEG)
    bias_ref[0] = jnp.where(band & (jj >= QB), 0.0, NEG)


def _two_heads(x, m0):
    return jnp.concatenate([jnp.where(m0, x, 0.0), jnp.where(m0, 0.0, x)], axis=0)


def _per_head(col, m0):
    return jnp.where(m0, col[:QB], col[QB:])


def _attn_rows(idx, d):
    if d == 1:
        b = idx
        cur = pl.ds(pl.multiple_of(b * QB, QB), QB)
        prev = pl.ds(pl.multiple_of(jnp.maximum(b - 1, 0) * QB, QB), QB)
    else:
        r, b = lax.rem(idx, d), lax.div(idx, d)
        cur = pl.ds(r + b * (QB * d), QB, stride=d)
        prev = pl.ds(r + jnp.maximum(b - 1, 0) * (QB * d), QB, stride=d)
    return cur, prev, b


NBLK = S // QB
GROUP = 16
GROUP_FWD = 16


def _colblk(off):
    return pl.BlockSpec((S, 128), lambda hp: (0, off * 8 + hp))


def attn_fwd(z):
    def body(q_ref, k_ref, v_ref, g_ref, o_ref, l_ref, a_ref, os, ls, bias):
        _attn_bias(bias)
        m0 = lax.broadcasted_iota(jnp.int32, (1, 128), 1) < 64
        for pi, d in enumerate(PATTERNS):
            lone = S // d == QB

            def load(idx, d=d, lone=lone):
                cur, prev, b = _attn_rows(idx, d)
                if lone:
                    return cur, (q_ref[cur, :], None, k_ref[cur, :], None, v_ref[cur, :], bias[1, :, QB:])
                return cur, (q_ref[cur, :], k_ref[prev, :], k_ref[cur, :], v_ref[prev, :], v_ref[cur, :],
                             bias[jnp.minimum(b, 1)])

            def block(q, kp, kc, vp, vc, bs):
                qq = _two_heads(q * 0.125, m0).astype(BF16)
                k = (kc if kp is None else jnp.concatenate([kp, kc], axis=0)).astype(BF16)
                s = _nt(qq, k) + bs
                mx = jnp.max(s, axis=-1, keepdims=True)
                p = jnp.exp(s - mx)
                den = jnp.sum(p, axis=-1, keepdims=True)
                pb = p.astype(BF16)
                vv = _two_heads(vc if vp is None else jnp.concatenate([vp, vc], axis=0), m0).astype(BF16)
                o = _nn(jnp.concatenate([pb[:QB], pb[QB:]], axis=1), vv)
                return o * _per_head(1.0 / den, m0), _per_head(mx + jnp.log(den), m0)

            def step(i, carry, pi=pi):
                loaded = [load(i * GROUP_FWD + u) for u in range(GROUP_FWD)]
                done = [block(*vals) for _, vals in loaded]
                for (cur, _), (o, l) in zip(loaded, done):
                    os[pi, cur, :] = o
                    ls[pi, cur, :] = l
                return carry
            lax.fori_loop(0, NBLK // GROUP_FWD, step, 0)
        l1, l2, l3 = ls[0], ls[1], ls[2]
        mx = jnp.maximum(jnp.maximum(l1, l2), l3)
        e1, e2, e3 = jnp.exp(l1 - mx), jnp.exp(l2 - mx), jnp.exp(l3 - mx)
        tot = e1 + e2 + e3
        o = (os[0] * e1 + os[1] * e2 + os[2] * e3) / tot
        ga = g_ref[...]
        o_ref[...] = o
        l_ref[...] = mx + jnp.log(tot)
        a_ref[...] = (o * (ga * _sig(ga))).astype(a_ref.dtype)

    out = pl.BlockSpec((S, 128), lambda hp: (0, hp))
    return pl.pallas_call(
        body, name="attn_fwd", grid=(8,),
        in_specs=[_colblk(0), _colblk(1), _colblk(2), _colblk(3)], out_specs=[out] * 3,
        out_shape=[jax.ShapeDtypeStruct((S, D), F32), jax.ShapeDtypeStruct((S, D), F32),
                   jax.ShapeDtypeStruct((S, 2 * D), BF16)],
        scratch_shapes=[pltpu.VMEM((3, S, 128), F32), pltpu.VMEM((3, S, 128), F32),
                        pltpu.VMEM((2, 2 * QB, 2 * QB), F32)],
        compiler_params=_cparams(("parallel",)),
    )(z, z, z, z)


def attn_bwd(z, d_cat, o, lse):
    def body(q_ref, k_ref, v_ref, g_ref, da_ref, o_ref, l_ref, dq_ref, dk_ref, dv_ref, dg_ref, do_s, pr_s, bias):
        _attn_bias(bias)
        m0 = lax.broadcasted_iota(jnp.int32, (1, 128), 1) < 64
        ga = g_ref[...]
        sg = _sig(ga)
        da = da_ref[...]
        ov = o_ref[...]
        do = da * (ga * sg)
        dg_ref[...] = da * ov * (sg * (1.0 + ga * (1.0 - sg)))
        do_s[...] = do
        pr_s[...] = do * ov
        dq_ref[...] = jnp.zeros_like(dq_ref)
        dk_ref[...] = jnp.zeros_like(dk_ref)
        dv_ref[...] = jnp.zeros_like(dv_ref)
        for d in PATTERNS:
            lone = S // d == QB

            def load(idx, d=d, lone=lone):
                cur, prev, b = _attn_rows(idx, d)
                if lone:
                    return (cur, None), (q_ref[cur, :], None, k_ref[cur, :], None, v_ref[cur, :],
                                         do_s[cur, :], pr_s[cur, :], l_ref[cur, :], bias[1, :, QB:])
                return (cur, prev), (q_ref[cur, :], k_ref[prev, :], k_ref[cur, :], v_ref[prev, :], v_ref[cur, :],
                                     do_s[cur, :], pr_s[cur, :], l_ref[cur, :], bias[jnp.minimum(b, 1)])

            def block(q, kp, kc, vp, vc, dof, prod, lp, bs):
                qq = _two_heads(q * 0.125, m0).astype(BF16)
                kf = kc if kp is None else jnp.concatenate([kp, kc], axis=0)
                k = kf.astype(BF16)
                v = (vc if vp is None else jnp.concatenate([vp, vc], axis=0)).astype(BF16)
                dd = _two_heads(dof, m0).astype(BF16)
                lh = jnp.max(jnp.concatenate([jnp.where(m0, lp, -jnp.inf), jnp.where(m0, -jnp.inf, lp)], axis=0),
                             axis=-1, keepdims=True)
                delta = jnp.sum(_two_heads(prod, m0), axis=-1, keepdims=True)
                p = jnp.exp(_nt(qq, k) + bs - lh)
                ds = (p * (_nt(dd, v) - delta)).astype(BF16)
                dq = _nn(jnp.concatenate([ds[:QB], ds[QB:]], axis=1), _two_heads(kf, m0).astype(BF16))
                return dq * 0.125, _tn(ds, qq), _tn(p.astype(BF16), dd)

            def step(i, carry):
                loaded = [load(i * GROUP + u) for u in range(GROUP)]
                done = [block(*vals) for _, vals in loaded]
                for ((cur, prev), _), (dq, dk, dv) in zip(loaded, done):
                    dq_ref[cur, :] = dq_ref[cur, :] + dq
                    if prev is not None:
                        dk_ref[prev, :] = dk_ref[prev, :] + dk[:QB]
                        dv_ref[prev, :] = dv_ref[prev, :] + dv[:QB]
                    dk_ref[cur, :] = dk_ref[cur, :] + dk[-QB:]
                    dv_ref[cur, :] = dv_ref[cur, :] + dv[-QB:]
                return carry
            lax.fori_loop(0, NBLK // GROUP, step, 0)

    blk = pl.BlockSpec((S, 128), lambda hp: (0, hp))
    return pl.pallas_call(
        body, name="attn_bwd", grid=(8,),
        in_specs=[_colblk(0), _colblk(1), _colblk(2), _colblk(3), blk, blk, blk], out_specs=[blk] * 4,
        out_shape=[jax.ShapeDtypeStruct((S, D), F32)] * 4,
        scratch_shapes=[pltpu.VMEM((S, 128), F32), pltpu.VMEM((S, 128), F32), pltpu.VMEM((2, 2 * QB, 2 * QB), F32)],
        compiler_params=_cparams(("parallel",)),
    )(z, z, z, z, d_cat, o, lse)


def assemble_dz_even(parts):
    def body(*refs):
        o_ref = refs[-1]
        for j in range(6):
            o_ref[:, j * D:(j + 1) * D] = refs[j][...].astype(o_ref.dtype)
    tr = 256
    blk = pl.BlockSpec((tr, D), lambda i: (i, 0))
    return pl.pallas_call(
        body, name="assemble_dz_even", grid=(S // tr,), in_specs=[blk] * 6,
        out_specs=pl.BlockSpec((tr, 6 * D), lambda i: (i, 0)),
        out_shape=jax.ShapeDtypeStruct((S, 6 * D), BF16),
        compiler_params=_cparams(("parallel",)),
    )(*parts)


def _pool_window(g):
    return jnp.where(g == 0, 2.0, jnp.where(g == 1, 4.0, jnp.where(g == 2, 8.0, 16.0)))


def _pool_sel(g, levels):
    return jnp.where(g == 0, levels[0], jnp.where(g == 1, levels[1], jnp.where(g == 2, levels[2], levels[3])))


def _pool_fwd_math(v, g):
    t = lax.broadcasted_iota(jnp.int32, (S, 1), 0)
    s = v
    levels = []
    for k in (1, 2, 4, 8):
        s = s + jnp.where(t >= k, pltpu.roll(s, k, 0), 0.0)
        levels.append(s)
    cnt = jnp.minimum((t + 1).astype(F32), _pool_window(g))
    return _pool_sel(g, levels) / cnt - v, cnt


def pool_fwd(z, pw, ps, cat):
    def body(v_ref, g_ref, pw_ref, ps_ref, cat_ref, o_ref):
        g = pl.program_id(0)
        pooled, _ = _pool_fwd_math(v_ref[...], g)
        mixed = _nn(pooled.astype(BF16), pw_ref[...].astype(BF16))
        gb = g_ref[...]
        o_ref[...] = (mixed * ps_ref[...] * (gb * _sig(gb))).astype(o_ref.dtype)

    return pl.pallas_call(
        body, name="pool_fwd", grid=(4,),
        in_specs=[pl.BlockSpec((S, 256), lambda g: (0, 16 + g)),
                  pl.BlockSpec((S, 256), lambda g: (0, 20 + g)),
                  pl.BlockSpec((None, 256, 256), lambda g: (g, 0, 0)),
                  pl.BlockSpec((1, 256), lambda g: (0, g)), pl.BlockSpec(memory_space=pl.ANY)],
        out_specs=pl.BlockSpec((S, 256), lambda g: (0, 4 + g)),
        out_shape=jax.ShapeDtypeStruct((S, 2 * D), BF16),
        input_output_aliases={4: 0},
        compiler_params=_cparams(("parallel",)),
    )(z, z, pw, ps, cat)


def pool_bwd(z, d_cat, pw, ps):
    def body(v_ref, g_ref, d_ref, pw_ref, ps_ref, dv_ref, dg_ref, dpw_ref, dps_ref):
        g = pl.program_id(0)
        v = v_ref[...]
        pooled, cnt = _pool_fwd_math(v, g)
        pwb = pw_ref[...].astype(BF16)
        pb = pooled.astype(BF16)
        mixed = _nn(pb, pwb)
        gb = g_ref[...]
        sg = _sig(gb)
        dout = d_ref[...]
        sc = ps_ref[...]
        dg_ref[...] = dout * mixed * sc * (sg * (1.0 + gb * (1.0 - sg)))
        dms = dout * (gb * sg)
        dps_ref[...] = jnp.sum(dms * mixed, axis=0, keepdims=True)
        dmx = (dms * sc).astype(BF16)
        dpw_ref[...] = _tn(pb, dmx)
        dpooled = _nt(dmx, pwb)
        t = lax.broadcasted_iota(jnp.int32, (S, 1), 0)
        s = dpooled / cnt
        levels = []
        for k in (1, 2, 4, 8):
            s = s + jnp.where(t < S - k, pltpu.roll(s, S - k, 0), 0.0)
            levels.append(s)
        dv_ref[...] = _pool_sel(g, levels) - dpooled

    return pl.pallas_call(
        body, name="pool_bwd", grid=(4,),
        in_specs=[pl.BlockSpec((S, 256), lambda g: (0, 16 + g)),
                  pl.BlockSpec((S, 256), lambda g: (0, 20 + g)),
                  pl.BlockSpec((S, 256), lambda g: (0, 4 + g)),
                  pl.BlockSpec((None, 256, 256), lambda g: (g, 0, 0)),
                  pl.BlockSpec((1, 256), lambda g: (0, g))],
        out_specs=[pl.BlockSpec((S, 256), lambda g: (0, g)),
                   pl.BlockSpec((S, 256), lambda g: (0, g)),
                   pl.BlockSpec((None, 256, 256), lambda g: (g, 0, 0)),
                   pl.BlockSpec((1, 256), lambda g: (0, g))],
        out_shape=[jax.ShapeDtypeStruct((S, D), F32), jax.ShapeDtypeStruct((S, D), F32),
                   jax.ShapeDtypeStruct((4, 256, 256), F32), jax.ShapeDtypeStruct((1, D), F32)],
        compiler_params=_cparams(("parallel",)),
    )(z, z, d_cat, pw, ps)


CH = 128


def _sgu_common(v, lng, lnb, w_ref):
    mu = jnp.mean(v, axis=-1, keepdims=True)
    vc = v - mu
    rs = lax.rsqrt(jnp.mean(vc * vc, axis=-1, keepdims=True) + EPS)
    xhat = vc * rs
    vn = (xhat * lng + lnb).astype(BF16)
    ri = lax.broadcasted_iota(jnp.int32, (CH, CH), 0)
    ci = lax.broadcasted_iota(jnp.int32, (CH, CH), 1)
    tril = ri >= ci
    ws = [jnp.where(tril, w_ref[g], 0.0).astype(BF16) for g in range(4)]
    return xhat, rs, vn, tril, ws


def _zspec(off):
    return pl.BlockSpec((CH, D), lambda c: (c, off))


def _full(shape):
    return pl.BlockSpec(shape, lambda c: (0,) * len(shape))


def sgu_fwd(z, lng, lnb, w, bfull):
    def body(u_ref, v_ref, g_ref, lng_ref, lnb_ref, w_ref, b_ref, o_ref):
        _, _, vn, _, ws = _sgu_common(v_ref[...], lng_ref[...], lnb_ref[...], w_ref)
        for g in range(4):
            sl = slice(g * 256, (g + 1) * 256)
            mixed = _nn(ws[g], vn[:, sl]) + b_ref[:, sl]
            gc = g_ref[:, sl]
            o_ref[:, sl] = (u_ref[:, sl] * mixed * (gc * _sig(gc))).astype(o_ref.dtype)

    return pl.pallas_call(
        body, name="sgu_fwd", grid=(S // CH,),
        in_specs=[_zspec(0), _zspec(1), _zspec(2), _full((1, D)), _full((1, D)),
                  _full((4, CH, CH)), _full((CH, D))],
        out_specs=pl.BlockSpec((CH, D), lambda c: (c, 0)),
        out_shape=jax.ShapeDtypeStruct((S, D), BF16),
        compiler_params=_cparams(("parallel",)),
    )(z, z, z, lng, lnb, w, bfull)


def sgu_bwd(z, d_cat, lng, lnb, w, bfull):
    def body(u_ref, v_ref, g_ref, d_ref, lng_ref, lnb_ref, w_ref, b_ref,
             du_ref, dv_ref, dg_ref, dw_ref, db_ref, dlg_ref, dlb_ref):
        @pl.when(pl.program_id(0) == 0)
        def _():
            dw_ref[...] = jnp.zeros_like(dw_ref)
            db_ref[...] = jnp.zeros_like(db_ref)
            dlg_ref[...] = jnp.zeros_like(dlg_ref)
            dlb_ref[...] = jnp.zeros_like(dlb_ref)

        lng = lng_ref[...]
        xhat, rs, vn, tril, ws = _sgu_common(v_ref[...], lng, lnb_ref[...], w_ref)
        lane = lax.broadcasted_iota(jnp.int32, (1, 128), 1)
        db = jnp.zeros((CH, 128), F32)
        dvn_parts = []
        for g in range(4):
            sl = slice(g * 256, (g + 1) * 256)
            mixed = _nn(ws[g], vn[:, sl]) + b_ref[:, sl]
            gc = g_ref[:, sl]
            sg = _sig(gc)
            u = u_ref[:, sl]
            dc = d_ref[:, sl]
            du_ref[:, sl] = dc * mixed * (gc * sg)
            dg_ref[:, sl] = dc * u * mixed * (sg * (1.0 + gc * (1.0 - sg)))
            dmx = dc * u * (gc * sg)
            db = db + jnp.where(lane == g, jnp.sum(dmx, axis=-1, keepdims=True), 0.0)
            dmb = dmx.astype(BF16)
            dw_ref[g] += jnp.where(tril, _nt(dmb, vn[:, sl]), 0.0)
            dvn_parts.append(_tn(ws[g], dmb))
        db_ref[...] += db
        dvn = jnp.concatenate(dvn_parts, axis=1)
        dlb_ref[...] += jnp.sum(dvn, axis=0, keepdims=True)
        dlg_ref[...] += jnp.sum(dvn * xhat, axis=0, keepdims=True)
        dxh = dvn * lng
        dv_ref[...] = rs * (dxh - jnp.mean(dxh, axis=-1, keepdims=True)
                            - xhat * jnp.mean(dxh * xhat, axis=-1, keepdims=True))

    row = pl.BlockSpec((CH, D), lambda c: (c, 0))
    return pl.pallas_call(
        body, name="sgu_bwd", grid=(S // CH,),
        in_specs=[_zspec(0), _zspec(1), _zspec(2), row, _full((1, D)), _full((1, D)),
                  _full((4, CH, CH)), _full((CH, D))],
        out_specs=[row, row, row, _full((4, CH, CH)), _full((CH, 128)), _full((1, D)), _full((1, D))],
        out_shape=[jax.ShapeDtypeStruct((S, D), F32)] * 3
        + [jax.ShapeDtypeStruct((4, CH, CH), F32), jax.ShapeDtypeStruct((CH, 128), F32),
           jax.ShapeDtypeStruct((1, D), F32), jax.ShapeDtypeStruct((1, D), F32)],
        compiler_params=_cparams(("arbitrary",)),
    )(z, z, z, d_cat, lng, lnb, w, bfull)


TB = 256


def _cmul(ar, ai, br, bi):
    return ar * br - ai * bi, ar * bi + ai * br


def _scan_consts(ar, ai, reverse):
    a2 = _cmul(ar, ai, ar, ai)
    a4 = _cmul(*a2, *a2)
    row = lax.broadcasted_iota(jnp.int32, (8, NS), 0)

    def masked(k, p):
        keep = (row < 8 - k) if reverse else (row >= k)
        return jnp.where(keep, p[0], 0.0), jnp.where(keep, p[1], 0.0)
    pr = jnp.zeros((8, NS), F32)
    pi = jnp.zeros((8, NS), F32)
    cr, ci = ar, ai
    for r in range(8):
        sel = row == (7 - r if reverse else r)
        pr = jnp.where(sel, cr, pr)
        pi = jnp.where(sel, ci, pi)
        cr, ci = _cmul(cr, ci, ar, ai)
    return (masked(1, (ar, ai)), masked(2, a2), masked(4, a4)), (pr, pi), row


def scan_fwd(bu, abr, abi):
    def body(bu_ref, ar_ref, ai_ref, h_ref, car, cai):
        @pl.when(pl.program_id(0) == 0)
        def _():
            car[...] = jnp.zeros_like(car)
            cai[...] = jnp.zeros_like(cai)

        pows, (pr, pi), row = _scan_consts(ar_ref[...], ai_ref[...], False)

        def tile(t, carry):
            c_r, c_i = carry
            rows = pl.ds(pl.multiple_of(t * 8, 8), 8)
            xr = bu_ref[rows, 0:NS]
            xi = bu_ref[rows, NS:2 * NS]
            for k, (kr, ki) in zip((1, 2, 4), pows):
                sr = pltpu.roll(xr, k, 0)
                si = pltpu.roll(xi, k, 0)
                xr, xi = xr + kr * sr - ki * si, xi + kr * si + ki * sr
            xr, xi = xr + pr * c_r - pi * c_i, xi + pr * c_i + pi * c_r
            h_ref[rows, 0:NS] = xr
            h_ref[rows, NS:2 * NS] = xi
            return (jnp.broadcast_to(xr[7:8, :], (8, NS)), jnp.broadcast_to(xi[7:8, :], (8, NS)))

        c_r, c_i = lax.fori_loop(0, TB // 8, tile, (car[...], cai[...]))
        car[...] = c_r
        cai[...] = c_i

    return pl.pallas_call(
        body, name="s5_scan_fwd", grid=(S // TB,),
        in_specs=[pl.BlockSpec((TB, 2 * NS), lambda i: (i, 0)),
                  pl.BlockSpec((1, NS), lambda i: (0, 0)), pl.BlockSpec((1, NS), lambda i: (0, 0))],
        out_specs=pl.BlockSpec((TB, 2 * NS), lambda i: (i, 0)),
        out_shape=jax.ShapeDtypeStruct((S, 2 * NS), F32),
        scratch_shapes=[pltpu.VMEM((8, NS), F32), pltpu.VMEM((8, NS), F32)],
        compiler_params=_cparams(("arbitrary",)),
    )(bu, abr, abi)


def scan_bwd(eta, h, abr, abi):
    nt = S // TB

    def body(e_ref, h_ref, ar_ref, ai_ref, l_ref, da_ref, car, cai):
        @pl.when(pl.program_id(0) == 0)
        def _():
            car[...] = jnp.zeros_like(car)
            cai[...] = jnp.zeros_like(cai)
            da_ref[...] = jnp.zeros_like(da_ref)

        pows, (pr, pi), row = _scan_consts(ar_ref[...], -ai_ref[...], True)

        def tile(tt, carry):
            c_r, c_i, acr, aci = carry
            t = TB // 8 - 1 - tt
            rows = pl.ds(pl.multiple_of(t * 8, 8), 8)
            xr = e_ref[rows, 0:NS]
            xi = e_ref[rows, NS:2 * NS]
            for k, (kr, ki) in zip((1, 2, 4), pows):
                sr = pltpu.roll(xr, 8 - k, 0)
                si = pltpu.roll(xi, 8 - k, 0)
                xr, xi = xr + kr * sr - ki * si, xi + kr * si + ki * sr
            xr, xi = xr + pr * c_r - pi * c_i, xi + pr * c_i + pi * c_r
            l_ref[rows, 0:NS] = xr
            l_ref[rows, NS:2 * NS] = xi
            nr = jnp.where(row < 7, pltpu.roll(xr, 7, 0), c_r)
            ni = jnp.where(row < 7, pltpu.roll(xi, 7, 0), c_i)
            hr = h_ref[rows, 0:NS]
            hi = h_ref[rows, NS:2 * NS]
            acr = acr + hr * nr + hi * ni
            aci = aci + hr * ni - hi * nr
            return (jnp.broadcast_to(xr[0:1, :], (8, NS)), jnp.broadcast_to(xi[0:1, :], (8, NS)), acr, aci)

        zero = jnp.zeros((8, NS), F32)
        c_r, c_i, acr, aci = lax.fori_loop(0, TB // 8, tile, (car[...], cai[...], zero, zero))
        car[...] = c_r
        cai[...] = c_i
        da_ref[:, 0:NS] += acr
        da_ref[:, NS:2 * NS] += aci

    rev = pl.BlockSpec((TB, 2 * NS), lambda i: (nt - 1 - i, 0))
    return pl.pallas_call(
        body, name="s5_scan_bwd", grid=(nt,),
        in_specs=[rev, rev, pl.BlockSpec((1, NS), lambda i: (0, 0)), pl.BlockSpec((1, NS), lambda i: (0, 0))],
        out_specs=[rev, pl.BlockSpec((8, 2 * NS), lambda i: (0, 0))],
        out_shape=[jax.ShapeDtypeStruct((S, 2 * NS), F32), jax.ShapeDtypeStruct((8, 2 * NS), F32)],
        scratch_shapes=[pltpu.VMEM((8, NS), F32), pltpu.VMEM((8, NS), F32)],
        compiler_params=_cparams(("arbitrary",)),
    )(eta, h, abr, abi)


GC = 0.7978845608028654
GA = 0.044715


def s5_post(hc, z, dskip):
    def fn(hv, xd, dv):
        y = hv + dv * xd
        return y, 0.5 * y * (1.0 + jnp.tanh(GC * (y + GA * y * y * y)))
    return rw(fn, [(hc, 0, 512), (z, 3072, 512)], [(512, F32), (512, BF16)], "s5_post", S, consts=[dskip])


def s5_post_bwd(dyg, ypre, z, dskip):
    def fn(dy, y, xd, dv):
        th = jnp.tanh(GC * (y + GA * y * y * y))
        dg = 0.5 * (1.0 + th) + 0.5 * y * (1.0 - th * th) * GC * (1.0 + 3.0 * GA * y * y)
        dyp = dy * dg
        return dyp, dyp * dv, jnp.sum(dyp * xd, axis=0, keepdims=True)
    return rw(fn, [(dyg, 0, 512), (ypre, 0, 512), (z, 3072, 512)], [(512, BF16), (512, F32)],
              "s5_post_bwd", S, consts=[dskip], accs=[(1, 512)])


def glu_fwd(t, z, c_out):
    def fn(t1, t2, gd, co):
        return (jnp.concatenate([co, (t1 * _sig(t2) * (gd * _sig(gd))).astype(BF16)], axis=1),)
    return rw(fn, [(t, 0, 512), (t, 512, 512), (z, 3584, 512), (c_out, 0, D)], [(D + 512, BF16)], "glu_fwd", S)[0]


def glu_bwd(t, z, d_cat):
    def fn(t1, t2, gd, dd):
        s2, sg = _sig(t2), _sig(gd)
        sl = gd * sg
        return (jnp.concatenate([dd * s2 * sl, dd * t1 * s2 * (1.0 - s2) * sl], axis=1),
                dd * t1 * s2 * (sg * (1.0 + gd * (1.0 - sg))))
    return rw(fn, [(t, 0, 512), (t, 512, 512), (z, 3584, 512), (d_cat, 1024, 512)],
              [(D, BF16), (512, F32)], "glu_bwd", S)


def assemble_dz_odd(du, dv, dgc, dxd, dgd):
    def body(a, b, c, d, e, o_ref):
        o_ref[:, 0:D] = a[...].astype(BF16)
        o_ref[:, D:2 * D] = b[...].astype(BF16)
        o_ref[:, 2 * D:3 * D] = c[...].astype(BF16)
        o_ref[:, 3 * D:3 * D + 512] = d[...].astype(BF16)
        o_ref[:, 3 * D + 512:4 * D] = e[...].astype(BF16)
    tr = 256
    blk = pl.BlockSpec((tr, D), lambda i: (i, 0))
    half = pl.BlockSpec((tr, 512), lambda i: (i, 0))
    return pl.pallas_call(
        body, name="assemble_dz_odd", grid=(S // tr,), in_specs=[blk, blk, blk, half, half],
        out_specs=pl.BlockSpec((tr, 4 * D), lambda i: (i, 0)),
        out_shape=jax.ShapeDtypeStruct((S, 4 * D), BF16),
        compiler_params=_cparams(("parallel",)),
    )(du, dv, dgc, dxd, dgd)


TQ = 1024


def _xattn_probs(qh, kh):
    s = _nt(qh, kh) * 0.0625
    p = jnp.exp(s - jnp.max(s, axis=-1, keepdims=True))
    return p / jnp.sum(p, axis=-1, keepdims=True)


def xattn_fwd(q, kv):
    def body(q_ref, kv_ref, o_ref):
        outs = []
        for h in range(4):
            sl = slice(h * 256, (h + 1) * 256)
            p = _xattn_probs(q_ref[:, sl].astype(BF16), kv_ref[:, sl].astype(BF16))
            vh = kv_ref[:, D + h * 256:D + (h + 1) * 256].astype(BF16)
            outs.append((sl, _nn(p.astype(BF16), vh)))
        for sl, o in outs:
            o_ref[:, sl] = o.astype(o_ref.dtype)

    return pl.pallas_call(
        body, name="xattn_fwd", grid=(S // TQ,),
        in_specs=[pl.BlockSpec((TQ, D), lambda i: (i, 0)), pl.BlockSpec((MEM, 2 * D), lambda i: (0, 0))],
        out_specs=pl.BlockSpec((TQ, D), lambda i: (i, 0)),
        out_shape=jax.ShapeDtypeStruct((S, D), BF16),
        compiler_params=_cparams(("parallel",)),
    )(q, kv)


def xattn_bwd(q, kv, d_o):
    def body(q_ref, kv_ref, do_ref, dq_ref, dkv_ref):
        @pl.when(pl.program_id(0) == 0)
        def _():
            dkv_ref[...] = jnp.zeros_like(dkv_ref)

        done = []
        for h in range(4):
            sl = slice(h * 256, (h + 1) * 256)
            vs = slice(D + h * 256, D + (h + 1) * 256)
            qh = q_ref[:, sl].astype(BF16)
            kh = kv_ref[:, sl].astype(BF16)
            vh = kv_ref[:, vs].astype(BF16)
            doh = do_ref[:, sl].astype(BF16)
            p = _xattn_probs(qh, kh)
            dp = _nt(doh, vh)
            ds = (p * (dp - jnp.sum(p * dp, axis=-1, keepdims=True)) * 0.0625).astype(BF16)
            done.append((sl, vs, _nn(ds, kh), _tn(ds, qh), _tn(p.astype(BF16), doh)))
        for sl, vs, dq, dk, dv in done:
            dq_ref[:, sl] = dq.astype(dq_ref.dtype)
            dkv_ref[:, sl] += dk
            dkv_ref[:, vs] += dv

    return pl.pallas_call(
        body, name="xattn_bwd", grid=(S // TQ,),
        in_specs=[pl.BlockSpec((TQ, D), lambda i: (i, 0)), pl.BlockSpec((MEM, 2 * D), lambda i: (0, 0)),
                  pl.BlockSpec((TQ, D), lambda i: (i, 0))],
        out_specs=[pl.BlockSpec((TQ, D), lambda i: (i, 0)), pl.BlockSpec((MEM, 2 * D), lambda i: (0, 0))],
        out_shape=[jax.ShapeDtypeStruct((S, D), BF16), jax.ShapeDtypeStruct((MEM, 2 * D), F32)],
        compiler_params=_cparams(("arbitrary",)),
    )(q, kv, d_o)


def _s5_disc(a_re, a_im, log_dt, b_re, b_im):
    dt = jnp.exp(log_dt)[:, None]
    mag = jnp.exp(dt * a_re)
    abr = mag * jnp.cos(dt * a_im)
    abi = mag * jnp.sin(dt * a_im)
    nr, ni = abr - 1.0, abi
    inv = 1.0 / (a_re * a_re + a_im * a_im)
    cr = (nr * a_re + ni * a_im) * inv
    ci = (ni * a_re - nr * a_im) * inv
    bbr = cr[..., None] * b_re - ci[..., None] * b_im
    bbi = cr[..., None] * b_im + ci[..., None] * b_re
    return abr, abi, bbr, bbi


VM = pl.BlockSpec(memory_space=pltpu.VMEM)


def s5_embed(bt_re, bt_im, ct_re, ct_im):
    def body(br, bi, cr, ci, b_ref, c_ref):
        b_ref[...] = jnp.zeros_like(b_ref)
        c_ref[...] = jnp.zeros_like(c_ref)
        for g in range(NG):
            rows, cols = slice(g * NH, (g + 1) * NH), slice(g * NP, (g + 1) * NP)
            b_ref[rows, cols] = br[g]
            b_ref[rows, NS + g * NP:NS + (g + 1) * NP] = bi[g]
            c_ref[cols, rows] = cr[g]
            c_ref[NS + g * NP:NS + (g + 1) * NP, rows] = -ci[g]

    return pl.pallas_call(
        body, name="s5_embed", in_specs=[VM] * 4, out_specs=[VM] * 2,
        out_shape=[jax.ShapeDtypeStruct((NG * NH, 2 * NS), F32), jax.ShapeDtypeStruct((2 * NS, NG * NH), F32)],
        compiler_params=pltpu.CompilerParams(vmem_limit_bytes=VMEM_LIMIT),
    )(bt_re, bt_im, ct_re, ct_im)


def s5_extract(gb, gc):
    def body(gb_ref, gc_ref, br, bi, cr, ci):
        for g in range(NG):
            rows, cols = slice(g * NH, (g + 1) * NH), slice(g * NP, (g + 1) * NP)
            br[g] = gb_ref[rows, cols]
            bi[g] = gb_ref[rows, NS + g * NP:NS + (g + 1) * NP]
            cr[g] = gc_ref[cols, rows]
            ci[g] = -gc_ref[NS + g * NP:NS + (g + 1) * NP, rows]

    return pl.pallas_call(
        body, name="s5_extract", in_specs=[VM] * 2, out_specs=[VM] * 4,
        out_shape=[jax.ShapeDtypeStruct((NG, NH, NP), F32)] * 2 + [jax.ShapeDtypeStruct((NG, NP, NH), F32)] * 2,
        compiler_params=pltpu.CompilerParams(vmem_limit_bytes=VMEM_LIMIT),
    )(gb, gc)


HC, HS = NG * NH // 2, NS // 2
TS = 1024


def s5_to_states(x, w, mode, name, z_off=0):
    if mode == "nn":
        wb, wm = (HC, HS), lambda i, j, kk: (j % 2, j)
    else:
        wb, wm = (HS, HC), lambda i, j, kk: (j, j % 2)
    return mm_band(x, w, mode, name, (S // TS, 4, 1), ((TS, HC), wb, (TS, HS)),
                   (lambda i, j, kk: (i, z_off + j % 2), wm, lambda i, j, kk: (i, j)), (S, 2 * NS))


def s5_to_channels(x, w, mode, name, add=None):
    if mode == "nn":
        wb, wm = (HS, HC), lambda i, j, kk: (j + 2 * kk, j)
    else:
        wb, wm = (HC, HS), lambda i, j, kk: (j, j + 2 * kk)
    return mm_band(x, w, mode, name, (S // TS, 2, 2), ((TS, HS), wb, (TS, HC)),
                   (lambda i, j, kk: (i, j + 2 * kk), wm, lambda i, j, kk: (i, j)), (S, NG * NH), add=add)


def s5_outer(a, b, name, states_first, z_off=0):
    if states_first:
        return mm_band(a, b, "tn", name, (4, 1, 1), ((S, HS), (S, HC), (HS, HC)),
                       (lambda i, j, kk: (0, i), lambda i, j, kk: (0, i % 2), lambda i, j, kk: (i, i % 2)),
                       (2 * NS, NG * NH))
    return mm_band(a, b, "tn", name, (1, 4, 1), ((S, HC), (S, HS), (HC, HS)),
                   (lambda i, j, kk: (0, z_off + j % 2), lambda i, j, kk: (0, j), lambda i, j, kk: (j % 2, j)),
                   (NG * NH, 2 * NS))


def _fwd_even(i, x, P, W):
    hn = rms_fwd(x, P["norm_ab"][i:i + 1], "rms_ab_fwd")
    z = mm(m2(hn), W["w_in"], "nn", "in_ab")
    o, lse, cat = attn_fwd(z)
    if "more" in W:
        W.update(W.pop("more")(cat))
    cat = pool_fwd(z, W["pool_w"], P["pool_scale"][i:i + 1], cat)
    x_mid = mm(m2(cat), W["w_out"], "nn", "out_ab", add=m2(x))
    return x_mid, dict(x=x, hn=hn, z=z, o=o, lse=lse, cat=cat)


def _bwd_even(i, dx_mid, sv, P, W, G, GW):
    z = sv["z"]
    d_cat = mm(m2(dx_mid), W["w_out"], "nt", "out_ab_dx")
    GW["w_out"] = mm(m2(sv["cat"]), m2(dx_mid), "tn", "out_ab_dw").reshape(4, 512, D)
    dq, dk, dv, dga = attn_bwd(z, d_cat, sv["o"], sv["lse"])
    dvb, dgb, dpw, dps = pool_bwd(z, d_cat, W["pool_w"], P["pool_scale"][i:i + 1])
    GW["pool_w"] = dpw.reshape(4, 4, 64, 256).transpose(1, 0, 2, 3).reshape(4, 256, 256)
    G["pool_scale"][i] = dps[0]
    d_z = assemble_dz_even((dq, dk, dv, dga, dvb, dgb))
    d_hn = mm(m2(d_z), W["w_in"], "nt", "in_ab_dx")
    GW["w_in"] = mm(m2(sv["hn"]), m2(d_z), "tn", "in_ab_dw", out=outcs(D, 1536))
    return d_hn, P["norm_ab"][i:i + 1], "norm_ab", "rms_ab_bwd"


def _fwd_odd(i, x, P, W):
    hn = rms_fwd(x, P["norm_cd"][i:i + 1], "rms_cd_fwd")
    z = mm(m2(hn), W["w_in"], "nn", "in_cd")
    bfull = jnp.repeat(P["sgu_b"][i].T, 256, axis=1)
    c_out = sgu_fwd(z, P["sgu_ln_g"][i:i + 1], P["sgu_ln_b"][i:i + 1], P["sgu_w"][i], bfull)
    disc, disc_vjp = jax.vjp(_s5_disc, P["s5_a_re"][i], P["s5_a_im"][i], P["s5_log_dt"][i],
                             P["s5_b_re"][i], P["s5_b_im"][i])
    abr, abi, bbr, bbi = disc
    bbd, cbd = s5_embed(bbr.transpose(0, 2, 1), bbi.transpose(0, 2, 1),
                        P["s5_c_re"][i].transpose(0, 2, 1), P["s5_c_im"][i].transpose(0, 2, 1))
    abr, abi = abr.reshape(1, NS), abi.reshape(1, NS)
    bu = s5_to_states(z, bbd, "nn", "s5_bu", z_off=3072 // HC)
    h = scan_fwd(bu, abr, abi)
    hc = s5_to_channels(h, cbd, "nn", "s5_hc")
    dskip = P["s5_d"][i:i + 1]
    ypre, yg = s5_post(hc, z, dskip)
    if "more" in W:
        W.update(W.pop("more")(yg))
    w12 = W["w12"]
    t = mm(m2(yg), m2(w12), "nn", "glu_t")
    cat = glu_fwd(t, z, c_out)
    x_mid = mm(m2(cat), W["w_out"], "nn", "out_cd", add=m2(x))
    return x_mid, dict(x=x, hn=hn, z=z, bfull=bfull, disc_vjp=disc_vjp, bbd=bbd, cbd=cbd, abr=abr,
                       abi=abi, h=h, ypre=ypre, yg=yg, w12=w12, t=t, cat=cat, dskip=dskip)


def _bwd_odd(i, dx_mid, sv, P, W, G, GW):
    z = sv["z"]
    d_cat = mm(m2(dx_mid), W["w_out"], "nt", "out_cd_dx")
    GW["w_out"] = mm(m2(sv["cat"]), m2(dx_mid), "tn", "out_cd_dw").reshape(4, 384, D)
    du, dv, dgc, dws, dbs, dlg, dlb = sgu_bwd(z, d_cat, P["sgu_ln_g"][i:i + 1], P["sgu_ln_b"][i:i + 1],
                                               P["sgu_w"][i], sv["bfull"])
    G["sgu_w"][i], G["sgu_b"][i] = dws, dbs[:, :4].T
    G["sgu_ln_g"][i], G["sgu_ln_b"][i] = dlg[0], dlb[0]
    dt, dgd = glu_bwd(sv["t"], z, d_cat)
    gw12 = mm(m2(sv["yg"]), m2(dt), "tn", "glu_dw")
    GW["glu_w1"] = gw12[:, :512].reshape(4, 128, 512)
    GW["glu_w2"] = gw12[:, 512:].reshape(4, 128, 512)
    dyg = mm(m2(dt), m2(sv["w12"]), "nt", "glu_dx")
    dypre, dxd1, dd = s5_post_bwd(dyg, sv["ypre"], z, sv["dskip"])
    G["s5_d"][i] = dd[0]
    gcbd = s5_outer(sv["h"], dypre, "s5_dc", states_first=True)
    eta = s5_to_states(dypre, sv["cbd"], "nt", "s5_eta")
    lam, dacc = scan_bwd(eta, sv["h"], sv["abr"], sv["abi"])
    gbbd = s5_outer(z, lam, "s5_db", states_first=False, z_off=3072 // HC)
    dxd = s5_to_channels(lam, sv["bbd"], "nt", "s5_dx", add=dxd1)
    dacc = jnp.sum(dacc, axis=0)
    dbt_re, dbt_im, dct_re, dct_im = s5_extract(gbbd, gcbd)
    G["s5_c_re"][i], G["s5_c_im"][i] = dct_re.transpose(0, 2, 1), dct_im.transpose(0, 2, 1)
    d_bbr, d_bbi = dbt_re.transpose(0, 2, 1), dbt_im.transpose(0, 2, 1)
    (G["s5_a_re"][i], G["s5_a_im"][i], G["s5_log_dt"][i], G["s5_b_re"][i], G["s5_b_im"][i]) = sv["disc_vjp"](
        (dacc[:NS].reshape(NG, NP), dacc[NS:].reshape(NG, NP), d_bbr, d_bbi))
    d_z = assemble_dz_odd(du, dv, dgc, dxd, dgd)
    d_hn = mm(m2(d_z), W["w_in"], "nt", "in_cd_dx")
    GW["w_in"] = mm(m2(sv["hn"]), m2(d_z), "tn", "in_cd_dw", out=outcs(D, 1024))
    return d_hn, P["norm_cd"][i:i + 1], "norm_cd", "rms_cd_bwd"


def _fwd_x(l, x, mem_n, P, W):
    hx = rms_fwd(x, P["norm_x"][l:l + 1], "rms_x_fwd")
    q = mm(m2(hx), W["w_xq"], "nn", "xq", out_dtype=BF16)
    kv = mm(m2(mem_n), W["w_xkv"], "nn", "xkv", out_dtype=BF16)
    ox = xattn_fwd(q, kv)
    x_out = mm(m2(ox), W["w_xo"], "nn", "xo", add=m2(x))
    return x_out, dict(x=x, hx=hx, q=q, kv=kv, ox=ox)


def _bwd_x(l, dx_out, sv, mem_n, d_memn, P, W, G, GW):
    d_ox = mm(m2(dx_out), W["w_xo"], "nt", "xo_dx", out_dtype=BF16)
    GW["w_xo"] = mm(m2(sv["ox"]), m2(dx_out), "tn", "xo_dw").reshape(4, 256, D)
    dq, dkv = xattn_bwd(sv["q"], sv["kv"], d_ox)
    GW["w_xq"] = mm(m2(sv["hx"]), m2(dq), "tn", "xq_dw").reshape(4, 256, D)
    d_hx = mm(m2(dq), W["w_xq"], "nt", "xq_dx")
    GW["w_xkv"] = mm(m2(mem_n), m2(dkv), "tn", "xkv_dw", out=outcs(D, 512))
    d_memn = mm(m2(dkv), W["w_xkv"], "nt", "xkv_dx", add=None if d_memn is None else m2(d_memn))
    dx, dg = rms_bwd(sv["x"], d_hx, dx_out, P["norm_x"][l:l + 1], "rms_x_bwd")
    G["norm_x"][l] = dg[0]
    return dx, d_memn


SMALL_LAYERS = (("norm_ab", 2), ("pool_scale", 2), ("norm_cd", 2), ("sgu_ln_g", 2), ("sgu_ln_b", 2), ("sgu_w", 2),
                ("sgu_b", 2), ("s5_a_re", 2), ("s5_a_im", 2), ("s5_log_dt", 2), ("s5_b_re", 2), ("s5_b_im", 2),
                ("s5_c_re", 2), ("s5_c_im", 2), ("s5_d", 2), ("norm_x", 4))


def local_step(x, mem, tgt, P, weights_of, grads_done):
    G = {k: [None] * n for k, n in SMALL_LAYERS}
    mem_g = P["mem_norm"].reshape(1, D)
    mem_n = rms_fwd(mem, mem_g, "rms_mem_fwd")
    saved = []
    for layer in range(4):
        i = layer // 2
        W = weights_of(layer, x)
        x, sv_m = (_fwd_even if layer % 2 == 0 else _fwd_odd)(i, x, P, W)
        x, sv_x = _fwd_x(layer, x, mem_n, P, W)
        saved.append((sv_m, sv_x, W))
    dx, loss, dgf = final_loss(x, tgt, P["final_norm"].reshape(1, D))
    G["final_norm"] = dgf[0]
    d_memn = None
    for layer in reversed(range(4)):
        i = layer // 2
        sv_m, sv_x, W = saved[layer]
        GW = {}
        dx_mid, d_memn = _bwd_x(layer, dx, sv_x, mem_n, d_memn, P, W, G, GW)
        d_hn, g, key, name = (_bwd_even if layer % 2 == 0 else _bwd_odd)(i, dx_mid, sv_m, P, W, G, GW)
        token = grads_done(layer, GW)
        if token is not None:
            g = g + token
        dx, dg = rms_bwd(sv_m["x"], d_hn, dx_mid, g, name)
        G[key][i] = dg[0]
    _, dgm = rms_bwd(mem, d_memn, d_memn, mem_g, "rms_mem_bwd")
    G["mem_norm"] = dgm[0]
    return loss, dx, G


ANY = pl.BlockSpec(memory_space=pl.ANY)


def _place():
    x, y, c = lax.axis_index("x"), lax.axis_index("y"), lax.axis_index("c")
    chips = [(1 - x, y), (x, 1 - y), (1 - x, 1 - y)]
    return x, y, c, 2 * x + y, (x, y, 1 - c), chips


def _remote(src, dst, send, recv, k, dev):
    return pltpu.make_async_remote_copy(src_ref=src, dst_ref=dst, send_sem=send.at[k], recv_sem=recv.at[k],
                                        device_id=dev, device_id_type=MESHID)


HBM = pl.BlockSpec(memory_space=pltpu.HBM)
SEM = pl.BlockSpec(memory_space=pltpu.SEMAPHORE)
EFFECT = pltpu.SideEffectType.DATAFLOW_SIDE_EFFECTING


def _hbm(t):
    return pltpu.with_memory_space_constraint(t, pltpu.HBM)


def allgather_sync(shards):
    n = len(shards)

    def body(*refs):
        ins, outs = refs[:n], refs[n:2 * n]
        token, send, recv = refs[2 * n:]
        x, y, c, jme, sib, chips = _place()
        first, passed = [], []
        for a in range(n):
            cp = _remote(ins[a], outs[a].at[jme], send, recv, a * 7 + 6, sib)
            cp.start()
            first.append(cp)
            for k, chip in enumerate(chips):
                cp = _remote(ins[a].at[c], outs[a].at[jme, c], send, recv, a * 7 + k, (*chip, c))
                cp.start()
                first.append(cp)
        for a in range(n):
            for k, chip in enumerate(chips):
                piece = outs[a].at[2 * chip[0] + chip[1], c]
                _remote(piece, piece, send, recv, a * 7 + k, (*chip, c)).wait_recv()
                fw = _remote(piece, piece, send, recv, a * 7 + 3 + k, sib)
                fw.start()
                passed.append(fw)
        for a in range(n):
            own = outs[a].at[jme]
            _remote(own, own, send, recv, a * 7 + 6, sib).wait_recv()
            for k, chip in enumerate(chips):
                piece = outs[a].at[2 * chip[0] + chip[1], 1 - c]
                _remote(piece, piece, send, recv, a * 7 + 3 + k, sib).wait_recv()
        for cp in first + passed:
            cp.wait_send()
        token[...] = jnp.zeros_like(token)

    res = pl.pallas_call(
        body, name="allgather_sync", in_specs=[ANY] * n,
        out_specs=[ANY] * n + [pl.BlockSpec(memory_space=pltpu.VMEM)],
        out_shape=[jax.ShapeDtypeStruct((4,) + s.shape, s.dtype) for s in shards] + [jax.ShapeDtypeStruct((8, 128), F32)],
        scratch_shapes=[pltpu.SemaphoreType.DMA((7 * n,)), pltpu.SemaphoreType.DMA((7 * n,))],
    )(*shards)
    return list(res[:n]), res[n]


def _gather_copies(ins, lands, send, recv):
    x, y, c, jme, sib, chips = _place()
    devs = [(*chip, c) for chip in chips] + [sib]
    return [_remote(ins[a], lands[a].at[jme], send, recv, a * 4 + k, dev)
            for a in range(len(ins)) for k, dev in enumerate(devs)]


def allgather_start(shards, after, name):
    n, na = len(shards), len(after)

    def body(*refs):
        ins, lands = refs[:n], refs[n:2 * n]
        send, recv = refs[2 * n + na], refs[2 * n + na + 1]
        token = refs[-1]
        for cp in _gather_copies(ins, lands, send, recv):
            cp.start()
        token[...] = jnp.zeros_like(token)

    res = pl.pallas_call(
        body, name=name,
        out_shape=(pltpu.SemaphoreType.DMA((4 * n,)), pltpu.SemaphoreType.DMA((4 * n,)),
                   *[pltpu.HBM(s.shape, s.dtype) for s in shards],
                   *[pltpu.HBM((4,) + s.shape, s.dtype) for s in shards],
                   jax.ShapeDtypeStruct((8, 128), F32)),
        in_specs=[HBM] * (2 * n) + [ANY] * na,
        out_specs=(SEM, SEM, *[HBM] * (2 * n), pl.BlockSpec(memory_space=pltpu.VMEM)),
        input_output_aliases={a: 2 + a for a in range(2 * n)},
        compiler_params=pltpu.CompilerParams(has_side_effects=EFFECT),
    )(*[_hbm(s) for s in shards], *[_hbm(lax.empty((4,) + s.shape, s.dtype)) for s in shards], *after)
    return res[0], res[1], list(res[2:2 + n]), list(res[2 + n:2 + 2 * n]), res[-1]


def allgather_wait(send, recv, shards, lands, after, name):
    n = len(shards)

    def body(*refs):
        ins, zones = refs[:n], refs[n:2 * n]
        send_r, recv_r = refs[2 * n], refs[2 * n + 1]
        x, y, c, jme, sib, chips = _place()
        slots = [2 * chip[0] + chip[1] for chip in chips] + [jme]
        for a in range(n):
            for k, slot in enumerate(slots):
                cp = _remote(ins[a], zones[a].at[slot], send_r, recv_r, a * 4 + k, sib)
                cp.wait_send()
                cp.wait_recv()

    res = pl.pallas_call(
        body, name=name,
        out_shape=tuple(pltpu.HBM(t.shape, t.dtype) for t in list(shards) + list(lands)),
        in_specs=[HBM] * (2 * n) + [SEM, SEM, ANY], out_specs=tuple([HBM] * (2 * n)),
        input_output_aliases={a: a for a in range(2 * n)},
        compiler_params=pltpu.CompilerParams(has_side_effects=EFFECT),
    )(*shards, *lands, send, recv, after)
    return list(res[n:])


def allgather_small(slab):
    def body(in_ref, out_ref, send, recv, lsem):
        x, y, c, jme, sib, chips = _place()
        loc = pltpu.make_async_copy(in_ref, out_ref.at[jme], lsem.at[0])
        loc.start()
        cps = [_remote(in_ref, out_ref.at[jme], send, recv, k, (*chip, c)) for k, chip in enumerate(chips)]
        for cp in cps:
            cp.start()
        for k, chip in enumerate(chips):
            piece = out_ref.at[2 * chip[0] + chip[1]]
            _remote(piece, piece, send, recv, k, (*chip, c)).wait_recv()
        for cp in cps:
            cp.wait_send()
        loc.wait()

    return pl.pallas_call(
        body, name="allgather_small", in_specs=[ANY], out_specs=ANY,
        out_shape=jax.ShapeDtypeStruct((4,) + slab.shape, slab.dtype),
        scratch_shapes=[pltpu.SemaphoreType.DMA((3,)), pltpu.SemaphoreType.DMA((3,)), pltpu.SemaphoreType.DMA((1,))],
    )(slab)


def allreduce_small(v):
    def body(v_ref, o_ref, r0, r1, r2, send, recv):
        x, y, c, jme, sib, chips = _place()
        peers = [sib, (1 - x, y, c), (x, 1 - y, c)]
        o_ref[...] = v_ref[...]
        for k, buf in enumerate((r0, r1, r2)):
            cp = _remote(o_ref, buf, send, recv, k, peers[k])
            cp.start()
            cp.wait()
            o_ref[...] = o_ref[...] + buf[...]

    vm = pl.BlockSpec(memory_space=pltpu.VMEM)
    return pl.pallas_call(
        body, name="allreduce_small", in_specs=[vm], out_specs=vm,
        out_shape=jax.ShapeDtypeStruct(v.shape, v.dtype),
        scratch_shapes=[pltpu.VMEM(v.shape, v.dtype)] * 3 + [pltpu.SemaphoreType.DMA((3,)), pltpu.SemaphoreType.DMA((3,))],
        compiler_params=pltpu.CompilerParams(vmem_limit_bytes=VMEM_LIMIT),
    )(v)


def _pair_copies(gs, lands, send, recv):
    x, y, c, jme, sib, chips = _place()
    return [_remote(gs[a].at[:, 1 - c], lands[a], send, recv, a, sib) for a in range(len(gs))]


def rs_pair_start(gs, name):
    n = len(gs)

    def body(*refs):
        ins, lands = refs[:n], refs[n:2 * n]
        send, recv = refs[2 * n], refs[2 * n + 1]
        token = refs[-1]
        for cp in _pair_copies(ins, lands, send, recv):
            cp.start()
        token[...] = jnp.zeros_like(token)

    shapes = [(4,) + g.shape[2:] for g in gs]
    res = pl.pallas_call(
        body, name=name,
        out_shape=(pltpu.SemaphoreType.DMA((n,)), pltpu.SemaphoreType.DMA((n,)),
                   *[pltpu.HBM(g.shape, g.dtype) for g in gs], *[pltpu.HBM(s, F32) for s in shapes],
                   jax.ShapeDtypeStruct((8, 128), F32)),
        in_specs=[HBM] * (2 * n), out_specs=(SEM, SEM, *[HBM] * (2 * n), pl.BlockSpec(memory_space=pltpu.VMEM)),
        input_output_aliases={a: 2 + a for a in range(2 * n)},
        compiler_params=pltpu.CompilerParams(has_side_effects=EFFECT),
    )(*[_hbm(g) for g in gs], *[_hbm(lax.empty(s, F32)) for s in shapes])
    return res[0], res[1], list(res[2:2 + n]), list(res[2 + n:2 + 2 * n]), res[-1]


def rs_pair_wait(send, recv, gs, lands, after, name):
    n = len(gs)

    def body(*refs):
        ins, zones = refs[:n], refs[n:2 * n]
        for cp in _pair_copies(ins, zones, refs[2 * n], refs[2 * n + 1]):
            cp.wait_send()
            cp.wait_recv()

    res = pl.pallas_call(
        body, name=name,
        out_shape=tuple(pltpu.HBM(t.shape, t.dtype) for t in list(gs) + list(lands)),
        in_specs=[HBM] * (2 * n) + [SEM, SEM, ANY], out_specs=tuple([HBM] * (2 * n)),
        input_output_aliases={a: a for a in range(2 * n)},
        compiler_params=pltpu.CompilerParams(has_side_effects=EFFECT),
    )(*gs, *lands, send, recv, after)
    return list(res[:n]), list(res[n:])


SUM_ROWS = 256


def rs_pair_sum(g4s, gots, cidx):
    n = len(g4s)
    tiles = [(min(g.shape[2], SUM_ROWS), g.shape[3]) for g in g4s]
    nts = [g.shape[2] // tr for g, (tr, _) in zip(g4s, tiles)]

    def at(a, s):
        s = jnp.minimum(s, 4 * nts[a] - 1)
        return s // nts[a], s % nts[a]

    def body(c_ref, *refs):
        for a in range(n):
            refs[2 * n + a][...] = (refs[a][...] + refs[n + a][...]).astype(BF16)

    in_specs = [pl.BlockSpec((None, None) + tiles[a], lambda s, cr, a=a: (at(a, s)[0], cr[0], at(a, s)[1], 0))
                for a in range(n)]
    in_specs += [pl.BlockSpec((None,) + tiles[a], lambda s, cr, a=a: (*at(a, s), 0)) for a in range(n)]
    return pl.pallas_call(
        body, name="rs_pair_sum",
        grid_spec=pltpu.PrefetchScalarGridSpec(
            num_scalar_prefetch=1, grid=(4 * max(nts),), in_specs=in_specs,
            out_specs=[pl.BlockSpec((None,) + tiles[a], lambda s, cr, a=a: (*at(a, s), 0)) for a in range(n)]),
        out_shape=[jax.ShapeDtypeStruct((4,) + g.shape[2:], BF16) for g in g4s],
        compiler_params=_cparams(("arbitrary",)),
    )(cidx, *g4s, *gots)


def _chip_copies(ps, lands, send, recv):
    x, y, c, jme, sib, chips = _place()
    return [_remote(ps[a].at[2 * chip[0] + chip[1]], lands[a].at[jme], send, recv, a * 3 + k, (*chip, c))
            for a in range(len(ps)) for k, chip in enumerate(chips)]


def rs_chip_start(ps, name):
    n = len(ps)

    def body(*refs):
        ins, lands = refs[:n], refs[n:2 * n]
        send, recv = refs[2 * n], refs[2 * n + 1]
        token = refs[-1]
        for cp in _chip_copies(ins, lands, send, recv):
            cp.start()
        token[...] = jnp.zeros_like(token)

    res = pl.pallas_call(
        body, name=name,
        out_shape=(pltpu.SemaphoreType.DMA((3 * n,)), pltpu.SemaphoreType.DMA((3 * n,)),
                   *[pltpu.HBM(p.shape, p.dtype) for p in ps], *[pltpu.HBM(p.shape, p.dtype) for p in ps],
                   jax.ShapeDtypeStruct((8, 128), F32)),
        in_specs=[HBM] * (2 * n), out_specs=(SEM, SEM, *[HBM] * (2 * n), pl.BlockSpec(memory_space=pltpu.VMEM)),
        input_output_aliases={a: 2 + a for a in range(2 * n)},
        compiler_params=pltpu.CompilerParams(has_side_effects=EFFECT),
    )(*[_hbm(p) for p in ps], *[_hbm(lax.empty(p.shape, p.dtype)) for p in ps])
    return res[0], res[1], list(res[2:2 + n]), list(res[2 + n:2 + 2 * n]), res[-1]


def rs_chip_wait(send, recv, ps, lands, after, name):
    n = len(ps)

    def body(*refs):
        ins, zones = refs[:n], refs[n:2 * n]
        send_r, recv_r = refs[2 * n], refs[2 * n + 1]
        x, y, c, jme, sib, chips = _place()
        for a in range(n):
            for k, chip in enumerate(chips):
                jt = 2 * chip[0] + chip[1]
                cp = _remote(ins[a].at[jt], zones[a].at[jt], send_r, recv_r, a * 3 + k, (*chip, c))
                cp.wait_send()
                cp.wait_recv()

    res = pl.pallas_call(
        body, name=name,
        out_shape=tuple(pltpu.HBM(p.shape, p.dtype) for p in list(ps) + list(lands)),
        in_specs=[HBM] * (2 * n) + [SEM, SEM] + [ANY] * len(after), out_specs=tuple([HBM] * (2 * n)),
        input_output_aliases={a: a for a in range(2 * n)},
        compiler_params=pltpu.CompilerParams(has_side_effects=EFFECT),
    )(*ps, *lands, send, recv, *after)
    return list(res[:n]), list(res[n:])


def rs_chip_sum(qs, ps, ls, accs, layers, jc):
    n = len(qs)
    tiles = [(min(q.shape[1], SUM_ROWS), q.shape[2]) for q in qs]
    nts = [q.shape[1] // tr for q, (tr, _) in zip(qs, tiles)]

    def at(a, s):
        return jnp.minimum(s, nts[a] - 1)

    def body(jc_ref, *refs):
        jme = jc_ref[0]
        for a in range(n):
            q_ref, p_ref, o_ref = refs[a], refs[n + a], refs[len(refs) - n + a]
            own = p_ref[...].astype(F32)
            v = [jnp.where(jme == j, own, q_ref[j].astype(F32)) for j in range(4)]
            o_ref[...] = ((v[0] + v[1]) + v[2]) + v[3]

    in_specs = [pl.BlockSpec((4,) + tiles[a], lambda s, jr, a=a: (0, at(a, s), 0)) for a in range(n)]
    in_specs += [pl.BlockSpec((None,) + tiles[a], lambda s, jr, a=a: (jr[0], at(a, s), 0)) for a in range(n)]
    args, aliases = [jc, *qs, *ps], {}
    for a in range(n):
        if accs[a] is not None:
            aliases[len(args)] = a
            in_specs.append(ANY)
            args.append(accs[a])
    return pl.pallas_call(
        body, name="rs_chip_sum",
        grid_spec=pltpu.PrefetchScalarGridSpec(
            num_scalar_prefetch=1, grid=(max(nts),), in_specs=in_specs,
            out_specs=[pl.BlockSpec((None, None) + tiles[a], lambda s, jr, a=a: (ls[a], jr[1], at(a, s), 0))
                       for a in range(n)]),
        out_shape=[jax.ShapeDtypeStruct((layers[a], 2) + qs[a].shape[1:], F32) for a in range(n)],
        input_output_aliases=aliases,
        compiler_params=_cparams(("arbitrary",)),
    )(*args)


def rs_pair_gather(rs):
    n = len(rs)

    def body(*refs):
        outs = refs[n:2 * n]
        send, recv = refs[2 * n:]
        x, y, c, jme, sib, chips = _place()
        cps = [_remote(outs[a].at[:, c], outs[a].at[:, c], send, recv, a, sib) for a in range(n)]
        for cp in cps:
            cp.start()
        for a in range(n):
            slot = outs[a].at[:, 1 - c]
            _remote(slot, slot, send, recv, a, sib).wait_recv()
        for cp in cps:
            cp.wait_send()

    return pl.pallas_call(
        body, name="rs_pair_gather", in_specs=[ANY] * n, out_specs=[ANY] * n,
        out_shape=[jax.ShapeDtypeStruct(r.shape, r.dtype) for r in rs],
        input_output_aliases={a: a for a in range(n)},
        scratch_shapes=[pltpu.SemaphoreType.DMA((n,)), pltpu.SemaphoreType.DMA((n,))],
    )(*rs)


def _adamw_math(w, g, m, v):
    m = B1 * m + (1.0 - B1) * g
    v = B2 * v + (1.0 - B2) * (g * g)
    m_hat = m / (1.0 - B1 ** STEP)
    v_hat = v / (1.0 - B2 ** STEP)
    return -LR * (m_hat / (jnp.sqrt(v_hat) + AEPS) + WD * w), m, v


def adamw(w, g, m, v, name, with_grad=False):
    rows, cols = w.shape
    tr = 256 if rows % 256 == 0 else rows
    fn = (lambda wv, gv, mv, vv: (gv,) + _adamw_math(wv, gv, mv, vv)) if with_grad else _adamw_math
    return rw(fn, [(a, 0, cols) for a in (w, g, m, v)], [(cols, F32)] * (4 if with_grad else 3), name, rows, tr=tr)


def adamw_small(ws, gs, ms, vs):
    n = len(ws)

    def body(*refs):
        for a in range(n):
            res = _adamw_math(*[refs[k * n + a][...] for k in range(4)])
            for k in range(3):
                refs[(4 + k) * n + a][...] = res[k]

    res = pl.pallas_call(
        body, name="adamw_small", in_specs=[VM] * (4 * n), out_specs=[VM] * (3 * n),
        out_shape=[jax.ShapeDtypeStruct(w.shape, F32) for _ in range(3) for w in ws],
        compiler_params=pltpu.CompilerParams(vmem_limit_bytes=VMEM_LIMIT),
    )(*ws, *gs, *ms, *vs)
    return [(res[a], res[n + a], res[2 * n + a]) for a in range(n)]


WEIGHTS = ["norm_ab", "w_in_ab", "pool_w", "pool_scale", "w_out_ab", "norm_cd", "w_in_cd", "sgu_ln_g", "sgu_ln_b",
           "sgu_w", "sgu_b", "s5_a_re", "s5_a_im", "s5_log_dt", "s5_b_re", "s5_b_im", "s5_c_re", "s5_c_im", "s5_d",
           "glu_w1", "glu_w2", "w_out_cd", "norm_x", "w_xq", "w_xkv", "w_xo", "mem_norm", "final_norm"]
INPUTS = ["x", "mem"] + WEIGHTS + ["loss_target"] + ["m_" + n for n in WEIGHTS] + ["v_" + n for n in WEIGHTS]
BIG = ["w_in_ab", "w_out_ab", "w_in_cd", "w_out_cd", "w_xq", "w_xkv", "w_xo", "glu_w1", "glu_w2", "pool_w"]
COL_SHARDED = ("w_in_ab", "w_in_cd", "w_xkv")
SMALL = [n for n in WEIGHTS if n not in BIG]
SMALL_SHARDED = {"norm_cd": 256, "sgu_ln_g": 256, "sgu_ln_b": 256, "s5_d": 128}
PACK = 256 * 128


def _pack(arrs):
    flat = jnp.concatenate([a.reshape(-1) for a in arrs])
    pad = (-flat.shape[0]) % PACK
    return jnp.concatenate([flat, jnp.zeros((pad,), flat.dtype)]).reshape(-1, 128)


def _unpack(packed, shapes):
    flat, out, off = packed.reshape(-1), [], 0
    for s in shapes:
        n = 1
        for d in s:
            n *= d
        out.append(flat[off:off + n].reshape(s))
        off += n
    return out


LAYER_KEYS = (("w_in", "w_out", "pool_w", "w_xq", "w_xkv", "w_xo"),
              ("w_in", "w_out", "glu_w1", "glu_w2", "w_xq", "w_xkv", "w_xo"))


def _weight_of(key, layer):
    if key in ("w_xq", "w_xkv", "w_xo"):
        return key, layer, 4
    kind = "ab" if layer % 2 == 0 else "cd"
    return {"w_in": "w_in_" + kind, "w_out": "w_out_" + kind}.get(key, key), layer // 2, 2


def kernel(*args):
    a = dict(zip(INPUTS, args))
    x_i, y_i, c_i = lax.axis_index("x"), lax.axis_index("y"), lax.axis_index("c")
    j = 2 * x_i + y_i

    slab = jnp.concatenate([a["norm_cd"], a["sgu_ln_g"], a["sgu_ln_b"],
                            jnp.pad(a["s5_d"], ((0, 0), (0, 128)))], axis=0)
    gslab = allgather_small(slab)
    P = {n: a[n] for n in SMALL}
    for k, n in enumerate(("norm_cd", "sgu_ln_g", "sgu_ln_b", "s5_d")):
        wd = SMALL_SHARDED[n]
        P[n] = gslab[:, 2 * k:2 * k + 2, :wd].transpose(1, 0, 2).reshape(2, 4 * wd)

    def shards_of(layer):
        keys = sorted(k for k in LAYER_KEYS[layer % 2])
        out = []
        for k in keys:
            n, l, _ = _weight_of(k, layer)
            out.append(a[n][l].reshape(-1, a[n].shape[-1]).astype(BF16))
        return keys, out

    keys0, sh0 = shards_of(0)
    first = keys0.index("w_in")
    g_in, token = allgather_sync([sh0[first].reshape(2, sh0[first].shape[0] // 2, sh0[first].shape[1])])
    w_in0 = g_in[0].reshape(4, -1, g_in[0].shape[-1])
    started = {}
    for layer in (0, 1, 2, 3):
        keys, sh = (keys0, sh0) if layer == 0 else shards_of(layer)
        rest = [(k, s) for k, s in zip(keys, sh) if k != "w_in"]
        parts = [("in", ["w_in"], [sh[keys.index("w_in")]])] * (layer > 0) + [("", *map(list, zip(*rest)))]
        for tag, pk, ps in parts:
            send, recv, ps, lands, token = allgather_start(ps, [token, gslab], "allgather_start_%d%s" % (layer, tag))
            started[(layer, tag)] = (pk, send, recv, ps, lands)
    P["norm_ab"] = P["norm_ab"] + token[0:1, 0:1]

    cidx = jnp.reshape(c_i, (1,)).astype(jnp.int32)
    jc = jnp.stack([j, c_i]).astype(jnp.int32)

    def views(g):
        W = {}
        for k, v in g.items():
            if k in ("w_in", "w_xkv"):
                W[k] = mcs(v)
            elif k == "pool_w":
                W[k] = v.reshape(4, 4, 64, 256).transpose(1, 0, 2, 3).reshape(4, 256, 256)
            elif k not in ("glu_w1", "glu_w2"):
                W[k] = m2(v.reshape(-1, v.shape[-1]))
        if "glu_w1" in g:
            W["w12"] = jnp.concatenate([g["glu_w1"].reshape(512, 512), g["glu_w2"].reshape(512, 512)], axis=1)
        return W

    def arrived(layer, tag, after):
        keys, send, recv, sh, lands = started[(layer, tag)]
        return views(dict(zip(keys, allgather_wait(send, recv, sh, lands, after, "allgather_wait_%d%s" % (layer, tag)))))

    def weights_of(layer, x_in):
        W = views({"w_in": w_in0}) if layer == 0 else arrived(layer, "in", x_in)
        W["more"] = lambda after: arrived(layer, "", after)
        return W

    halves, pending = {}, {}

    def finish_pair(layer, after):
        keys, send, recv, flat, lands = halves.pop(layer)
        flat, got = rs_pair_wait(send, recv, flat, lands, after, "rs_pair_wait_%d" % layer)
        pair = rs_pair_sum(flat, got, cidx)
        send, recv, pair, lands, token = rs_chip_start(pair, "rs_chip_start_%d" % layer)
        pending[layer] = (keys, send, recv, pair, lands)
        return token

    def grads_done(layer, GW):
        keys = sorted(GW)
        flat = [GW[k].reshape(4, 2, GW[k].shape[1] // 2, GW[k].shape[2]) for k in keys]
        send, recv, flat, lands, token = rs_pair_start(flat, "rs_pair_start_%d" % layer)
        halves[layer] = (keys, send, recv, flat, lands)
        if layer + 1 in halves:
            token = token + finish_pair(layer + 1, token)
        return token[0:1, 0:1]

    loss, dx, G = local_step(a["x"][0], a["mem"][0], a["loss_target"][0], P, weights_of, grads_done)
    loss = lax.psum(loss[0, 0], ("x", "y", "c"))
    finish_pair(0, dx)
    outs = {}

    def update_big(names, red):
        for n, g in zip(names, rs_pair_gather([red[n] for n in names])):
            shp = a[n].shape
            g2 = g.reshape(-1, shp[-1])
            upd = adamw(a[n].reshape(g2.shape), g2, a["m_" + n].reshape(g2.shape), a["v_" + n].reshape(g2.shape),
                        "adamw_" + n, with_grad=True)
            outs[n] = tuple(t.reshape(shp) for t in upd)

    def reduce_layer(layer, red, after):
        keys, send, recv, pair, lands = pending[layer]
        pair, lands = rs_chip_wait(send, recv, pair, lands, after, "rs_chip_wait_%d" % layer)
        which = [_weight_of(k, layer) for k in keys]
        sums = rs_chip_sum(lands, pair, [l for _, l, _ in which], [red.get(n) for n, _, _ in which],
                           [layers for _, _, layers in which], jc)
        red.update(zip([n for n, _, _ in which], sums))

    red = {}
    for layer in (3, 2, 1):
        reduce_layer(layer, red, [dx])
    odd_only = [n for n in BIG if n.endswith("_cd") or n.startswith("glu")]
    update_big(odd_only, red)

    gfull = [jnp.stack(G[n]) if isinstance(G[n], list) else G[n] for n in SMALL]
    shapes = [g.shape for g in gfull]
    gsum = _unpack(allreduce_small(_pack(gfull)), shapes)
    gloc = []
    for n, g in zip(SMALL, gsum):
        if n in SMALL_SHARDED:
            g = lax.dynamic_slice_in_dim(g, j * SMALL_SHARDED[n], SMALL_SHARDED[n], axis=1)
        gloc.append(g)
    two = [(-1, a[n].shape[-1]) if a[n].ndim > 1 else (1, a[n].shape[0]) for n in SMALL]
    upds = adamw_small(*[[t.reshape(s) for t, s in zip(ts, two)]
                         for ts in ([a[n] for n in SMALL], gloc, [a["m_" + n] for n in SMALL],
                                    [a["v_" + n] for n in SMALL])])
    for n, g, upd in zip(SMALL, gloc, upds):
        outs[n] = (g,) + tuple(t.reshape(a[n].shape) for t in upd)

    behind = [outs[n][1] for n in odd_only + SMALL[-1:]] + [red[n] for n in BIG if n not in odd_only]
    reduce_layer(0, red, behind)
    update_big([n for n in BIG if n not in odd_only], red)

    res = [loss, dx[None]]
    for part in range(4):
        res += [outs[n][part] for n in WEIGHTS]
    return tuple(res)
```

```python
import math

import jax
import jax.numpy as jnp
from jax import lax
from jax.experimental import pallas as pl
from jax.experimental.pallas import tpu as pltpu

F32, BF16 = jnp.float32, jnp.bfloat16
S, D = 2048, 1024
MEM = 256
EPS = 1e-6
NEG = -1e30
QB = 128
PATTERNS = (1, 4, 16)
NG, NP, NH = 32, 64, 16
NS = NG * NP
LR, B1, B2, AEPS, WD, STEP = 0.001, 0.9, 0.999, 1e-08, 0.01, 10
MESHID = pl.DeviceIdType.MESH
VMEM_LIMIT = 56 * 1024 * 1024


def _cparams(sem):
    return pltpu.CompilerParams(dimension_semantics=sem, vmem_limit_bytes=VMEM_LIMIT)


def _sig(x):
    return 1.0 / (1.0 + jnp.exp(-x))


def _dot(a, b, dims):
    return lax.dot_general(a, b, (dims, ((), ())), preferred_element_type=F32)


def _nn(a, b):
    return _dot(a, b, ((1,), (0,)))


def _nt(a, b):
    return _dot(a, b, ((1,), (1,)))


def _tn(a, b):
    return _dot(a, b, ((0,), (0,)))


_DIMS = {"nn": ((1,), (0,)), "nt": ((1,), (1,)), "tn": ((0,), (0,))}


def _tile(dim, cc=None, cap=1024):
    for t in (2048, 1536, 1024, 768, 512, 384, 256, 128):
        if t <= cap and dim % t == 0 and (cc is None or cc % t == 0):
            return t
    return dim


MM_VMEM = 36 * 1024 * 1024


def _mm_tiles(m, n, k, ccm, ccn, cck, a_bytes, b_bytes, o_bytes):
    caps = [1024, 1024, 2048]
    while True:
        tm, tn, tk = _tile(m, ccm, caps[0]), _tile(n, ccn, caps[1]), _tile(k, cck, caps[2])
        need = 2 * (tm * tk * a_bytes + tk * tn * b_bytes + tm * tn * o_bytes) + (tm * tn * 4 if tk < k else 0)
        if need <= MM_VMEM:
            return tm, tn, tk
        if tk > 1024:
            caps[2] = tk // 2
        elif tn >= tm:
            caps[1] = tn // 2
        else:
            caps[0] = tm // 2


def m2(arr, col_off=0, ncols=None):
    rows, cols = arr.shape
    ncols = cols - col_off if ncols is None else ncols

    def spec(tr, tc, rc):
        assert col_off % tc == 0
        return pl.BlockSpec((tr, tc), lambda *g: (rc(*g)[0], rc(*g)[1] + col_off // tc))
    return (arr, rows, ncols, spec, None if col_off == 0 else col_off)


def mcs(arr):
    cs = arr.shape[2]

    def spec(tr, tc, rc):
        n = cs // tc
        return pl.BlockSpec((None, tr, tc), lambda *g: (rc(*g)[1] // n, rc(*g)[0], rc(*g)[1] % n))
    return (arr, arr.shape[1], 4 * cs, spec, cs)


def out2(rows, cols):
    def spec(tr, tc, rc):
        return pl.BlockSpec((tr, tc), lambda *g: tuple(rc(*g)))
    return ((rows, cols), spec, None)


def outcs(rows, cs):
    def spec(tr, tc, rc):
        n = cs // tc
        return pl.BlockSpec((None, tr, tc), lambda *g: (rc(*g)[1] // n, rc(*g)[0], rc(*g)[1] % n))
    return ((4, rows, cs), spec, cs)


def _both(a, b):
    if a is None:
        return b
    if b is None:
        return a
    return math.gcd(a, b)


def mm(a, b, mode, name, add=None, out=None, out_dtype=F32):
    a_arr, a_r, a_c, a_spec, a_cc = a
    b_arr, b_r, b_c, b_spec, b_cc = b
    if mode == "nn":
        m, k, n = a_r, a_c, b_c
        assert b_r == k
        ccm, cck, ccn = None, a_cc, b_cc
    elif mode == "nt":
        m, k, n = a_r, a_c, b_r
        assert b_c == k
        ccm, cck, ccn = None, _both(a_cc, b_cc), None
    else:
        m, k, n = a_c, a_r, b_c
        assert b_r == k
        ccm, cck, ccn = a_cc, None, b_cc
    out = out2(m, n) if out is None else out
    o_shape, o_spec, o_cc = out
    ccn = _both(ccn, o_cc)
    if add is not None:
        ccn = _both(ccn, add[4])
    o_bytes = jnp.dtype(out_dtype).itemsize + (0 if add is None else add[0].dtype.itemsize)
    tm, tn, tk = _mm_tiles(m, n, k, ccm, ccn, cck, a_arr.dtype.itemsize, b_arr.dtype.itemsize, o_bytes)
    nk = k // tk
    if mode == "nn":
        in_specs = [a_spec(tm, tk, lambda i, j, kk: (i, kk)), b_spec(tk, tn, lambda i, j, kk: (kk, j))]
    elif mode == "nt":
        in_specs = [a_spec(tm, tk, lambda i, j, kk: (i, kk)), b_spec(tn, tk, lambda i, j, kk: (j, kk))]
    else:
        in_specs = [a_spec(tk, tm, lambda i, j, kk: (kk, i)), b_spec(tk, tn, lambda i, j, kk: (kk, j))]
    args = [a_arr, b_arr]
    if add is not None:
        in_specs.append(add[3](tm, tn, lambda i, j, kk: (i, j)))
        args.append(add[0])
    return _mm_call(args, in_specs, o_spec(tm, tn, lambda i, j, kk: (i, j)), jax.ShapeDtypeStruct(o_shape, out_dtype),
                    mode, (m // tm, n // tn, nk), (tm, tn), add is not None, name)


def _mm_call(args, in_specs, out_spec, out_shape, mode, grid, tile, has_add, name):
    dims = _DIMS[mode]
    nk = grid[2]
    tm, tn = tile

    def body(*refs):
        a_ref, b_ref = refs[0], refs[1]
        add_ref = refs[2] if has_add else None
        prod = _dot(a_ref[...].astype(BF16), b_ref[...].astype(BF16), dims)
        if nk == 1:
            o_ref = refs[-1]
            if has_add:
                prod = prod + add_ref[...].astype(F32)
            o_ref[...] = prod.astype(o_ref.dtype)
            return
        o_ref, acc = refs[-2], refs[-1]
        kk = pl.program_id(2)

        @pl.when(kk == 0)
        def _():
            acc[...] = prod

        @pl.when(kk > 0)
        def _():
            acc[...] += prod

        @pl.when(kk == nk - 1)
        def _():
            r = acc[...]
            if has_add:
                r = r + add_ref[...].astype(F32)
            o_ref[...] = r.astype(o_ref.dtype)

    return pl.pallas_call(
        body, name=name, grid=grid, in_specs=in_specs, out_specs=out_spec, out_shape=out_shape,
        scratch_shapes=[pltpu.VMEM((tm, tn), F32)] if nk > 1 else [],
        compiler_params=_cparams(("parallel", "parallel", "arbitrary")),
    )(*args)


def mm_band(a, b, mode, name, grid, blocks, maps, out_shape, add=None, out_dtype=F32):
    in_specs = [pl.BlockSpec(blocks[0], maps[0]), pl.BlockSpec(blocks[1], maps[1])]
    args = [a, b]
    if add is not None:
        in_specs.append(pl.BlockSpec(blocks[2], maps[2]))
        args.append(add)
    return _mm_call(args, in_specs, pl.BlockSpec(blocks[2], maps[2]), jax.ShapeDtypeStruct(out_shape, out_dtype),
                    mode, grid, blocks[2], add is not None, name)


def rw(fn, ins, outs, name, rows, tr=None, consts=(), accs=()):
    tr = min(rows, 512) if tr is None else tr
    n_in, n_c, n_o, n_a = len(ins), len(consts), len(outs), len(accs)
    in_specs = []
    for arr, off, width in ins:
        assert off % width == 0
        in_specs.append(pl.BlockSpec((tr, width), lambda i, o=off // width: (i, o)))
    for c in consts:
        in_specs.append(pl.BlockSpec(c.shape, lambda i: (0, 0)))
    out_specs = [pl.BlockSpec((tr, w), lambda i: (i, 0)) for w, _ in outs]
    out_specs += [pl.BlockSpec(s, lambda i: (0, 0)) for s in accs]
    out_shape = [jax.ShapeDtypeStruct((rows, w), dt) for w, dt in outs]
    out_shape += [jax.ShapeDtypeStruct(s, F32) for s in accs]

    def body(*refs):
        vals = [r[...] for r in refs[:n_in + n_c]]
        o_refs = refs[n_in + n_c:n_in + n_c + n_o]
        a_refs = refs[n_in + n_c + n_o:]
        res = fn(*vals)
        for r, v in zip(o_refs, res[:n_o]):
            r[...] = v.astype(r.dtype)
        if n_a:
            @pl.when(pl.program_id(0) == 0)
            def _():
                for r in a_refs:
                    r[...] = jnp.zeros_like(r)
            for r, v in zip(a_refs, res[n_o:]):
                r[...] += v

    res = pl.pallas_call(
        body, name=name, grid=(rows // tr,), in_specs=in_specs, out_specs=out_specs,
        out_shape=out_shape,
        compiler_params=_cparams(("arbitrary",) if n_a else ("parallel",)),
    )(*[a for a, _, _ in ins], *consts)
    return res


def _rstd(x):
    return lax.rsqrt(jnp.mean(x * x, axis=-1, keepdims=True) + EPS)


def rms_fwd(x, g, name):
    def fn(xv, gv):
        xv = xv.astype(F32)
        return (xv * _rstd(xv) * gv,)
    return rw(fn, [(x, 0, D)], [(D, BF16)], name, x.shape[0], consts=[g])[0]


def _rms_bwd_math(xv, dy, gv):
    r = _rstd(xv)
    dyg = dy * gv
    dx = r * dyg - xv * (r * r * r / D) * jnp.sum(dyg * xv, axis=-1, keepdims=True)
    dg = jnp.sum(dy * xv * r, axis=0, keepdims=True)
    return dx, dg


def rms_bwd(x, dy, dres, g, name):
    def fn(xv, dyv, drv, gv):
        dx, dg = _rms_bwd_math(xv, dyv, gv)
        return dx + drv, dg
    return rw(fn, [(x, 0, D), (dy, 0, D), (dres, 0, D)], [(D, F32)], name, x.shape[0],
              consts=[g], accs=[(1, D)])


def final_loss(x, tgt, g):
    def fn(xv, tv, gv):
        e = xv * _rstd(xv) * gv - tv
        loss = 0.5 * jnp.sum(jnp.sum(e * e, axis=-1, keepdims=True), axis=0, keepdims=True) / D
        dx, dg = _rms_bwd_math(xv, e / D, gv)
        return dx, loss, dg
    return rw(fn, [(x, 0, D), (tgt, 0, D)], [(D, F32)], "final_loss", S, consts=[g],
              accs=[(1, 1), (1, D)])


def _attn_bias(bias_ref):
    ii = lax.broadcasted_iota(jnp.int32, (2 * QB, 2 * QB), 0) % QB
    jj = lax.broadcasted_iota(jnp.int32, (2 * QB, 2 * QB), 1)
    dist = ii + QB - jj
    band = (dist >= 0) & (dist <= QB)
    bias_ref[1] = jnp.where(band, 0.0, NEG)
    bias_ref[0] = jnp.where(band & (jj >= QB), 0.0, NEG)


def _two_heads(x, m0):
    return jnp.concatenate([jnp.where(m0, x, 0.0), jnp.where(m0, 0.0, x)], axis=0)


def _per_head(col, m0):
    return jnp.where(m0, col[:QB], col[QB:])


def _attn_rows(idx, d):
    if d == 1:
        b = idx
        cur = pl.ds(pl.multiple_of(b * QB, QB), QB)
        prev = pl.ds(pl.multiple_of(jnp.maximum(b - 1, 0) * QB, QB), QB)
    else:
        r, b = lax.rem(idx, d), lax.div(idx, d)
        cur = pl.ds(r + b * (QB * d), QB, stride=d)
        prev = pl.ds(r + jnp.maximum(b - 1, 0) * (QB * d), QB, stride=d)
    return cur, prev, b


NBLK = S // QB
GROUP = 16
GROUP_FWD = 16


def _colblk(off):
    return pl.BlockSpec((S, 128), lambda hp: (0, off * 8 + hp))


def attn_fwd(z):
    def body(q_ref, k_ref, v_ref, g_ref, o_ref, l_ref, a_ref, os, ls, bias):
        _attn_bias(bias)
        m0 = lax.broadcasted_iota(jnp.int32, (1, 128), 1) < 64
        for pi, d in enumerate(PATTERNS):
            lone = S // d == QB

            def load(idx, d=d, lone=lone):
                cur, prev, b = _attn_rows(idx, d)
                if lone:
                    return cur, (q_ref[cur, :], None, k_ref[cur, :], None, v_ref[cur, :], bias[1, :, QB:])
                return cur, (q_ref[cur, :], k_ref[prev, :], k_ref[cur, :], v_ref[prev, :], v_ref[cur, :],
                             bias[jnp.minimum(b, 1)])

            def block(q, kp, kc, vp, vc, bs):
                qq = _two_heads(q * 0.125, m0).astype(BF16)
                k = (kc if kp is None else jnp.concatenate([kp, kc], axis=0)).astype(BF16)
                s = _nt(qq, k) + bs
                mx = jnp.max(s, axis=-1, keepdims=True)
                p = jnp.exp(s - mx)
                den = jnp.sum(p, axis=-1, keepdims=True)
                pb = p.astype(BF16)
                vv = _two_heads(vc if vp is None else jnp.concatenate([vp, vc], axis=0), m0).astype(BF16)
                o = _nn(jnp.concatenate([pb[:QB], pb[QB:]], axis=1), vv)
                return o * _per_head(1.0 / den, m0), _per_head(mx + jnp.log(den), m0)

            def step(i, carry, pi=pi):
                loaded = [load(i * GROUP_FWD + u) for u in range(GROUP_FWD)]
                done = [block(*vals) for _, vals in loaded]
                for (cur, _), (o, l) in zip(loaded, done):
                    os[pi, cur, :] = o
                    ls[pi, cur, :] = l
                return carry
            lax.fori_loop(0, NBLK // GROUP_FWD, step, 0)
        l1, l2, l3 = ls[0], ls[1], ls[2]
        mx = jnp.maximum(jnp.maximum(l1, l2), l3)
        e1, e2, e3 = jnp.exp(l1 - mx), jnp.exp(l2 - mx), jnp.exp(l3 - mx)
        tot = e1 + e2 + e3
        o = (os[0] * e1 + os[1] * e2 + os[2] * e3) / tot
        ga = g_ref[...]
        o_ref[...] = o
        l_ref[...] = mx + jnp.log(tot)
        a_ref[...] = (o * (ga * _sig(ga))).astype(a_ref.dtype)

    out = pl.BlockSpec((S, 128), lambda hp: (0, hp))
    return pl.pallas_call(
        body, name="attn_fwd", grid=(8,),
        in_specs=[_colblk(0), _colblk(1), _colblk(2), _colblk(3)], out_specs=[out] * 3,
        out_shape=[jax.ShapeDtypeStruct((S, D), F32), jax.ShapeDtypeStruct((S, D), F32),
                   jax.ShapeDtypeStruct((S, 2 * D), BF16)],
        scratch_shapes=[pltpu.VMEM((3, S, 128), F32), pltpu.VMEM((3, S, 128), F32),
                        pltpu.VMEM((2, 2 * QB, 2 * QB), F32)],
        compiler_params=_cparams(("parallel",)),
    )(z, z, z, z)


def attn_bwd(z, d_cat, o, lse):
    def body(q_ref, k_ref, v_ref, g_ref, da_ref, o_ref, l_ref, dq_ref, dk_ref, dv_ref, dg_ref, do_s, pr_s, bias):
        _attn_bias(bias)
        m0 = lax.broadcasted_iota(jnp.int32, (1, 128), 1) < 64
        ga = g_ref[...]
        sg = _sig(ga)
        da = da_ref[...]
        ov = o_ref[...]
        do = da * (ga * sg)
        dg_ref[...] = da * ov * (sg * (1.0 + ga * (1.0 - sg)))
        do_s[...] = do
        pr_s[...] = do * ov
        dq_ref[...] = jnp.zeros_like(dq_ref)
        dk_ref[...] = jnp.zeros_like(dk_ref)
        dv_ref[...] = jnp.zeros_like(dv_ref)
        for d in PATTERNS:
            lone = S // d == QB

            def load(idx, d=d, lone=lone):
                cur, prev, b = _attn_rows(idx, d)
                if lone:
                    return (cur, None), (q_ref[cur, :], None, k_ref[cur, :], None, v_ref[cur, :],
                                         do_s[cur, :], pr_s[cur, :], l_ref[cur, :], bias[1, :, QB:])
                return (cur, prev), (q_ref[cur, :], k_ref[prev, :], k_ref[cur, :], v_ref[prev, :], v_ref[cur, :],
                                     do_s[cur, :], pr_s[cur, :], l_ref[cur, :], bias[jnp.minimum(b, 1)])

            def block(q, kp, kc, vp, vc, dof, prod, lp, bs):
                qq = _two_heads(q * 0.125, m0).astype(BF16)
                kf = kc if kp is None else jnp.concatenate([kp, kc], axis=0)
                k = kf.astype(BF16)
                v = (vc if vp is None else jnp.concatenate([vp, vc], axis=0)).astype(BF16)
                dd = _two_heads(dof, m0).astype(BF16)
                lh = jnp.max(jnp.concatenate([jnp.where(m0, lp, -jnp.inf), jnp.where(m0, -jnp.inf, lp)], axis=0),
                             axis=-1, keepdims=True)
                delta = jnp.sum(_two_heads(prod, m0), axis=-1, keepdims=True)
                p = jnp.exp(_nt(qq, k) + bs - lh)
                ds = (p * (_nt(dd, v) - delta)).astype(BF16)
                dq = _nn(jnp.concatenate([ds[:QB], ds[QB:]], axis=1), _two_heads(kf, m0).astype(BF16))
                return dq * 0.125, _tn(ds, qq), _tn(p.astype(BF16), dd)

            def step(i, carry):
                loaded = [load(i * GROUP + u) for u in range(GROUP)]
                done = [block(*vals) for _, vals in loaded]
                for ((cur, prev), _), (dq, dk, dv) in zip(loaded, done):
                    dq_ref[cur, :] = dq_ref[cur, :] + dq
                    if prev is not None:
                        dk_ref[prev, :] = dk_ref[prev, :] + dk[:QB]
                        dv_ref[prev, :] = dv_ref[prev, :] + dv[:QB]
                    dk_ref[cur, :] = dk_ref[cur, :] + dk[-QB:]
                    dv_ref[cur, :] = dv_ref[cur, :] + dv[-QB:]
                return carry
            lax.fori_loop(0, NBLK // GROUP, step, 0)

    blk = pl.BlockSpec((S, 128), lambda hp: (0, hp))
    return pl.pallas_call(
        body, name="attn_bwd", grid=(8,),
        in_specs=[_colblk(0), _colblk(1), _colblk(2), _colblk(3), blk, blk, blk], out_specs=[blk] * 4,
        out_shape=[jax.ShapeDtypeStruct((S, D), F32)] * 4,
        scratch_shapes=[pltpu.VMEM((S, 128), F32), pltpu.VMEM((S, 128), F32), pltpu.VMEM((2, 2 * QB, 2 * QB), F32)],
        compiler_params=_cparams(("parallel",)),
    )(z, z, z, z, d_cat, o, lse)


def assemble_dz_even(parts):
    def body(*refs):
        o_ref = refs[-1]
        for j in range(6):
            o_ref[:, j * D:(j + 1) * D] = refs[j][...].astype(o_ref.dtype)
    tr = 256
    blk = pl.BlockSpec((tr, D), lambda i: (i, 0))
    return pl.pallas_call(
        body, name="assemble_dz_even", grid=(S // tr,), in_specs=[blk] * 6,
        out_specs=pl.BlockSpec((tr, 6 * D), lambda i: (i, 0)),
        out_shape=jax.ShapeDtypeStruct((S, 6 * D), BF16),
        compiler_params=_cparams(("parallel",)),
    )(*parts)


def _pool_window(g):
    return jnp.where(g == 0, 2.0, jnp.where(g == 1, 4.0, jnp.where(g == 2, 8.0, 16.0)))


def _pool_sel(g, levels):
    return jnp.where(g == 0, levels[0], jnp.where(g == 1, levels[1], jnp.where(g == 2, levels[2], levels[3])))


def _pool_fwd_math(v, g):
    t = lax.broadcasted_iota(jnp.int32, (S, 1), 0)
    s = v
    levels = []
    for k in (1, 2, 4, 8):
        s = s + jnp.where(t >= k, pltpu.roll(s, k, 0), 0.0)
        levels.append(s)
    cnt = jnp.minimum((t + 1).astype(F32), _pool_window(g))
    return _pool_sel(g, levels) / cnt - v, cnt


def pool_fwd(z, pw, ps, cat):
    def body(v_ref, g_ref, pw_ref, ps_ref, cat_ref, o_ref):
        g = pl.program_id(0)
        pooled, _ = _pool_fwd_math(v_ref[...], g)
        mixed = _nn(pooled.astype(BF16), pw_ref[...].astype(BF16))
        gb = g_ref[...]
        o_ref[...] = (mixed * ps_ref[...] * (gb * _sig(gb))).astype(o_ref.dtype)

    return pl.pallas_call(
        body, name="pool_fwd", grid=(4,),
        in_specs=[pl.BlockSpec((S, 256), lambda g: (0, 16 + g)),
                  pl.BlockSpec((S, 256), lambda g: (0, 20 + g)),
                  pl.BlockSpec((None, 256, 256), lambda g: (g, 0, 0)),
                  pl.BlockSpec((1, 256), lambda g: (0, g)), pl.BlockSpec(memory_space=pl.ANY)],
        out_specs=pl.BlockSpec((S, 256), lambda g: (0, 4 + g)),
        out_shape=jax.ShapeDtypeStruct((S, 2 * D), BF16),
        input_output_aliases={4: 0},
        compiler_params=_cparams(("parallel",)),
    )(z, z, pw, ps, cat)


def pool_bwd(z, d_cat, pw, ps):
    def body(v_ref, g_ref, d_ref, pw_ref, ps_ref, dv_ref, dg_ref, dpw_ref, dps_ref):
        g = pl.program_id(0)
        v = v_ref[...]
        pooled, cnt = _pool_fwd_math(v, g)
        pwb = pw_ref[...].astype(BF16)
        pb = pooled.astype(BF16)
        mixed = _nn(pb, pwb)
        gb = g_ref[...]
        sg = _sig(gb)
        dout = d_ref[...]
        sc = ps_ref[...]
        dg_ref[...] = dout * mixed * sc * (sg * (1.0 + gb * (1.0 - sg)))
        dms = dout * (gb * sg)
        dps_ref[...] = jnp.sum(dms * mixed, axis=0, keepdims=True)
        dmx = (dms * sc).astype(BF16)
        dpw_ref[...] = _tn(pb, dmx)
        dpooled = _nt(dmx, pwb)
        t = lax.broadcasted_iota(jnp.int32, (S, 1), 0)
        s = dpooled / cnt
        levels = []
        for k in (1, 2, 4, 8):
            s = s + jnp.where(t < S - k, pltpu.roll(s, S - k, 0), 0.0)
            levels.append(s)
        dv_ref[...] = _pool_sel(g, levels) - dpooled

    return pl.pallas_call(
        body, name="pool_bwd", grid=(4,),
        in_specs=[pl.BlockSpec((S, 256), lambda g: (0, 16 + g)),
                  pl.BlockSpec((S, 256), lambda g: (0, 20 + g)),
                  pl.BlockSpec((S, 256), lambda g: (0, 4 + g)),
                  pl.BlockSpec((None, 256, 256), lambda g: (g, 0, 0)),
                  pl.BlockSpec((1, 256), lambda g: (0, g))],
        out_specs=[pl.BlockSpec((S, 256), lambda g: (0, g)),
                   pl.BlockSpec((S, 256), lambda g: (0, g)),
                   pl.BlockSpec((None, 256, 256), lambda g: (g, 0, 0)),
                   pl.BlockSpec((1, 256), lambda g: (0, g))],
        out_shape=[jax.ShapeDtypeStruct((S, D), F32), jax.ShapeDtypeStruct((S, D), F32),
                   jax.ShapeDtypeStruct((4, 256, 256), F32), jax.ShapeDtypeStruct((1, D), F32)],
        compiler_params=_cparams(("parallel",)),
    )(z, z, d_cat, pw, ps)


CH = 128


def _sgu_common(v, lng, lnb, w_ref):
    mu = jnp.mean(v, axis=-1, keepdims=True)
    vc = v - mu
    rs = lax.rsqrt(jnp.mean(vc * vc, axis=-1, keepdims=True) + EPS)
    xhat = vc * rs
    vn = (xhat * lng + lnb).astype(BF16)
    ri = lax.broadcasted_iota(jnp.int32, (CH, CH), 0)
    ci = lax.broadcasted_iota(jnp.int32, (CH, CH), 1)
    tril = ri >= ci
    ws = [jnp.where(tril, w_ref[g], 0.0).astype(BF16) for g in range(4)]
    return xhat, rs, vn, tril, ws


def _zspec(off):
    return pl.BlockSpec((CH, D), lambda c: (c, off))


def _full(shape):
    return pl.BlockSpec(shape, lambda c: (0,) * len(shape))


def sgu_fwd(z, lng, lnb, w, bfull):
    def body(u_ref, v_ref, g_ref, lng_ref, lnb_ref, w_ref, b_ref, o_ref):
        _, _, vn, _, ws = _sgu_common(v_ref[...], lng_ref[...], lnb_ref[...], w_ref)
        for g in range(4):
            sl = slice(g * 256, (g + 1) * 256)
            mixed = _nn(ws[g], vn[:, sl]) + b_ref[:, sl]
            gc = g_ref[:, sl]
            o_ref[:, sl] = (u_ref[:, sl] * mixed * (gc * _sig(gc))).astype(o_ref.dtype)

    return pl.pallas_call(
        body, name="sgu_fwd", grid=(S // CH,),
        in_specs=[_zspec(0), _zspec(1), _zspec(2), _full((1, D)), _full((1, D)),
                  _full((4, CH, CH)), _full((CH, D))],
        out_specs=pl.BlockSpec((CH, D), lambda c: (c, 0)),
        out_shape=jax.ShapeDtypeStruct((S, D), BF16),
        compiler_params=_cparams(("parallel",)),
    )(z, z, z, lng, lnb, w, bfull)


def sgu_bwd(z, d_cat, lng, lnb, w, bfull):
    def body(u_ref, v_ref, g_ref, d_ref, lng_ref, lnb_ref, w_ref, b_ref,
             du_ref, dv_ref, dg_ref, dw_ref, db_ref, dlg_ref, dlb_ref):
        @pl.when(pl.program_id(0) == 0)
        def _():
            dw_ref[...] = jnp.zeros_like(dw_ref)
            db_ref[...] = jnp.zeros_like(db_ref)
            dlg_ref[...] = jnp.zeros_like(dlg_ref)
            dlb_ref[...] = jnp.zeros_like(dlb_ref)

        lng = lng_ref[...]
        xhat, rs, vn, tril, ws = _sgu_common(v_ref[...], lng, lnb_ref[...], w_ref)
        lane = lax.broadcasted_iota(jnp.int32, (1, 128), 1)
        db = jnp.zeros((CH, 128), F32)
        dvn_parts = []
        for g in range(4):
            sl = slice(g * 256, (g + 1) * 256)
            mixed = _nn(ws[g], vn[:, sl]) + b_ref[:, sl]
            gc = g_ref[:, sl]
            sg = _sig(gc)
            u = u_ref[:, sl]
            dc = d_ref[:, sl]
            du_ref[:, sl] = dc * mixed * (gc * sg)
            dg_ref[:, sl] = dc * u * mixed * (sg * (1.0 + gc * (1.0 - sg)))
            dmx = dc * u * (gc * sg)
            db = db + jnp.where(lane == g, jnp.sum(dmx, axis=-1, keepdims=True), 0.0)
            dmb = dmx.astype(BF16)
            dw_ref[g] += jnp.where(tril, _nt(dmb, vn[:, sl]), 0.0)
            dvn_parts.append(_tn(ws[g], dmb))
        db_ref[...] += db
        dvn = jnp.concatenate(dvn_parts, axis=1)
        dlb_ref[...] += jnp.sum(dvn, axis=0, keepdims=True)
        dlg_ref[...] += jnp.sum(dvn * xhat, axis=0, keepdims=True)
        dxh = dvn * lng
        dv_ref[...] = rs * (dxh - jnp.mean(dxh, axis=-1, keepdims=True)
                            - xhat * jnp.mean(dxh * xhat, axis=-1, keepdims=True))

    row = pl.BlockSpec((CH, D), lambda c: (c, 0))
    return pl.pallas_call(
        body, name="sgu_bwd", grid=(S // CH,),
        in_specs=[_zspec(0), _zspec(1), _zspec(2), row, _full((1, D)), _full((1, D)),
                  _full((4, CH, CH)), _full((CH, D))],
        out_specs=[row, row, row, _full((4, CH, CH)), _full((CH, 128)), _full((1, D)), _full((1, D))],
        out_shape=[jax.ShapeDtypeStruct((S, D), F32)] * 3
        + [jax.ShapeDtypeStruct((4, CH, CH), F32), jax.ShapeDtypeStruct((CH, 128), F32),
           jax.ShapeDtypeStruct((1, D), F32), jax.ShapeDtypeStruct((1, D), F32)],
        compiler_params=_cparams(("arbitrary",)),
    )(z, z, z, d_cat, lng, lnb, w, bfull)


TB = 256


def _cmul(ar, ai, br, bi):
    return ar * br - ai * bi, ar * bi + ai * br


def _scan_consts(ar, ai, reverse):
    a2 = _cmul(ar, ai, ar, ai)
    a4 = _cmul(*a2, *a2)
    row = lax.broadcasted_iota(jnp.int32, (8, NS), 0)

    def masked(k, p):
        keep = (row < 8 - k) if reverse else (row >= k)
        return jnp.where(keep, p[0], 0.0), jnp.where(keep, p[1], 0.0)
    pr = jnp.zeros((8, NS), F32)
    pi = jnp.zeros((8, NS), F32)
    cr, ci = ar, ai
    for r in range(8):
        sel = row == (7 - r if reverse else r)
        pr = jnp.where(sel, cr, pr)
        pi = jnp.where(sel, ci, pi)
        cr, ci = _cmul(cr, ci, ar, ai)
    return (masked(1, (ar, ai)), masked(2, a2), masked(4, a4)), (pr, pi), row


def scan_fwd(bu, abr, abi):
    def body(bu_ref, ar_ref, ai_ref, h_ref, car, cai):
        @pl.when(pl.program_id(0) == 0)
        def _():
            car[...] = jnp.zeros_like(car)
            cai[...] = jnp.zeros_like(cai)

        pows, (pr, pi), row = _scan_consts(ar_ref[...], ai_ref[...], False)

        def tile(t, carry):
            c_r, c_i = carry
            rows = pl.ds(pl.multiple_of(t * 8, 8), 8)
            xr = bu_ref[rows, 0:NS]
            xi = bu_ref[rows, NS:2 * NS]
            for k, (kr, ki) in zip((1, 2, 4), pows):
                sr = pltpu.roll(xr, k, 0)
                si = pltpu.roll(xi, k, 0)
                xr, xi = xr + kr * sr - ki * si, xi + kr * si + ki * sr
            xr, xi = xr + pr * c_r - pi * c_i, xi + pr * c_i + pi * c_r
            h_ref[rows, 0:NS] = xr
            h_ref[rows, NS:2 * NS] = xi
            return (jnp.broadcast_to(xr[7:8, :], (8, NS)), jnp.broadcast_to(xi[7:8, :], (8, NS)))

        c_r, c_i = lax.fori_loop(0, TB // 8, tile, (car[...], cai[...]))
        car[...] = c_r
        cai[...] = c_i

    return pl.pallas_call(
        body, name="s5_scan_fwd", grid=(S // TB,),
        in_specs=[pl.BlockSpec((TB, 2 * NS), lambda i: (i, 0)),
                  pl.BlockSpec((1, NS), lambda i: (0, 0)), pl.BlockSpec((1, NS), lambda i: (0, 0))],
        out_specs=pl.BlockSpec((TB, 2 * NS), lambda i: (i, 0)),
        out_shape=jax.ShapeDtypeStruct((S, 2 * NS), F32),
        scratch_shapes=[pltpu.VMEM((8, NS), F32), pltpu.VMEM((8, NS), F32)],
        compiler_params=_cparams(("arbitrary",)),
    )(bu, abr, abi)


def scan_bwd(eta, h, abr, abi):
    nt = S // TB

    def body(e_ref, h_ref, ar_ref, ai_ref, l_ref, da_ref, car, cai):
        @pl.when(pl.program_id(0) == 0)
        def _():
            car[...] = jnp.zeros_like(car)
            cai[...] = jnp.zeros_like(cai)
            da_ref[...] = jnp.zeros_like(da_ref)

        pows, (pr, pi), row = _scan_consts(ar_ref[...], -ai_ref[...], True)

        def tile(tt, carry):
            c_r, c_i, acr, aci = carry
            t = TB // 8 - 1 - tt
            rows = pl.ds(pl.multiple_of(t * 8, 8), 8)
            xr = e_ref[rows, 0:NS]
            xi = e_ref[rows, NS:2 * NS]
            for k, (kr, ki) in zip((1, 2, 4), pows):
                sr = pltpu.roll(xr, 8 - k, 0)
                si = pltpu.roll(xi, 8 - k, 0)
                xr, xi = xr + kr * sr - ki * si, xi + kr * si + ki * sr
            xr, xi = xr + pr * c_r - pi * c_i, xi + pr * c_i + pi * c_r
            l_ref[rows, 0:NS] = xr
            l_ref[rows, NS:2 * NS] = xi
            nr = jnp.where(row < 7, pltpu.roll(xr, 7, 0), c_r)
            ni = jnp.where(row < 7, pltpu.roll(xi, 7, 0), c_i)
            hr = h_ref[rows, 0:NS]
            hi = h_ref[rows, NS:2 * NS]
            acr = acr + hr * nr + hi * ni
            aci = aci + hr * ni - hi * nr
            return (jnp.broadcast_to(xr[0:1, :], (8, NS)), jnp.broadcast_to(xi[0:1, :], (8, NS)), acr, aci)

        zero = jnp.zeros((8, NS), F32)
        c_r, c_i, acr, aci = lax.fori_loop(0, TB // 8, tile, (car[...], cai[...], zero, zero))
        car[...] = c_r
        cai[...] = c_i
        da_ref[:, 0:NS] += acr
        da_ref[:, NS:2 * NS] += aci

    rev = pl.BlockSpec((TB, 2 * NS), lambda i: (nt - 1 - i, 0))
    return pl.pallas_call(
        body, name="s5_scan_bwd", grid=(nt,),
        in_specs=[rev, rev, pl.BlockSpec((1, NS), lambda i: (0, 0)), pl.BlockSpec((1, NS), lambda i: (0, 0))],
        out_specs=[rev, pl.BlockSpec((8, 2 * NS), lambda i: (0, 0))],
        out_shape=[jax.ShapeDtypeStruct((S, 2 * NS), F32), jax.ShapeDtypeStruct((8, 2 * NS), F32)],
        scratch_shapes=[pltpu.VMEM((8, NS), F32), pltpu.VMEM((8, NS), F32)],
        compiler_params=_cparams(("arbitrary",)),
    )(eta, h, abr, abi)


GC = 0.7978845608028654
GA = 0.044715


def s5_post(hc, z, dskip):
    def fn(hv, xd, dv):
        y = hv + dv * xd
        return y, 0.5 * y * (1.0 + jnp.tanh(GC * (y + GA * y * y * y)))
    return rw(fn, [(hc, 0, 512), (z, 3072, 512)], [(512, F32), (512, BF16)], "s5_post", S, consts=[dskip])


def s5_post_bwd(dyg, ypre, z, dskip):
    def fn(dy, y, xd, dv):
        th = jnp.tanh(GC * (y + GA * y * y * y))
        dg = 0.5 * (1.0 + th) + 0.5 * y * (1.0 - th * th) * GC * (1.0 + 3.0 * GA * y * y)
        dyp = dy * dg
        return dyp, dyp * dv, jnp.sum(dyp * xd, axis=0, keepdims=True)
    return rw(fn, [(dyg, 0, 512), (ypre, 0, 512), (z, 3072, 512)], [(512, BF16), (512, F32)],
              "s5_post_bwd", S, consts=[dskip], accs=[(1, 512)])


def glu_fwd(t, z, c_out):
    def fn(t1, t2, gd, co):
        return (jnp.concatenate([co, (t1 * _sig(t2) * (gd * _sig(gd))).astype(BF16)], axis=1),)
    return rw(fn, [(t, 0, 512), (t, 512, 512), (z, 3584, 512), (c_out, 0, D)], [(D + 512, BF16)], "glu_fwd", S)[0]


def glu_bwd(t, z, d_cat):
    def fn(t1, t2, gd, dd):
        s2, sg = _sig(t2), _sig(gd)
        sl = gd * sg
        return (jnp.concatenate([dd * s2 * sl, dd * t1 * s2 * (1.0 - s2) * sl], axis=1),
                dd * t1 * s2 * (sg * (1.0 + gd * (1.0 - sg))))
    return rw(fn, [(t, 0, 512), (t, 512, 512), (z, 3584, 512), (d_cat, 1024, 512)],
              [(D, BF16), (512, F32)], "glu_bwd", S)


def assemble_dz_odd(du, dv, dgc, dxd, dgd):
    def body(a, b, c, d, e, o_ref):
        o_ref[:, 0:D] = a[...].astype(BF16)
        o_ref[:, D:2 * D] = b[...].astype(BF16)
        o_ref[:, 2 * D:3 * D] = c[...].astype(BF16)
        o_ref[:, 3 * D:3 * D + 512] = d[...].astype(BF16)
        o_ref[:, 3 * D + 512:4 * D] = e[...].astype(BF16)
    tr = 256
    blk = pl.BlockSpec((tr, D), lambda i: (i, 0))
    half = pl.BlockSpec((tr, 512), lambda i: (i, 0))
    return pl.pallas_call(
        body, name="assemble_dz_odd", grid=(S // tr,), in_specs=[blk, blk, blk, half, half],
        out_specs=pl.BlockSpec((tr, 4 * D), lambda i: (i, 0)),
        out_shape=jax.ShapeDtypeStruct((S, 4 * D), BF16),
        compiler_params=_cparams(("parallel",)),
    )(du, dv, dgc, dxd, dgd)


TQ = 1024


def _xattn_probs(qh, kh):
    s = _nt(qh, kh) * 0.0625
    p = jnp.exp(s - jnp.max(s, axis=-1, keepdims=True))
    return p / jnp.sum(p, axis=-1, keepdims=True)


def xattn_fwd(q, kv):
    def body(q_ref, kv_ref, o_ref):
        outs = []
        for h in range(4):
            sl = slice(h * 256, (h + 1) * 256)
            p = _xattn_probs(q_ref[:, sl].astype(BF16), kv_ref[:, sl].astype(BF16))
            vh = kv_ref[:, D + h * 256:D + (h + 1) * 256].astype(BF16)
            outs.append((sl, _nn(p.astype(BF16), vh)))
        for sl, o in outs:
            o_ref[:, sl] = o.astype(o_ref.dtype)

    return pl.pallas_call(
        body, name="xattn_fwd", grid=(S // TQ,),
        in_specs=[pl.BlockSpec((TQ, D), lambda i: (i, 0)), pl.BlockSpec((MEM, 2 * D), lambda i: (0, 0))],
        out_specs=pl.BlockSpec((TQ, D), lambda i: (i, 0)),
        out_shape=jax.ShapeDtypeStruct((S, D), BF16),
        compiler_params=_cparams(("parallel",)),
    )(q, kv)


def xattn_bwd(q, kv, d_o):
    def body(q_ref, kv_ref, do_ref, dq_ref, dkv_ref):
        @pl.when(pl.program_id(0) == 0)
        def _():
            dkv_ref[...] = jnp.zeros_like(dkv_ref)

        done = []
        for h in range(4):
            sl = slice(h * 256, (h + 1) * 256)
            vs = slice(D + h * 256, D + (h + 1) * 256)
            qh = q_ref[:, sl].astype(BF16)
            kh = kv_ref[:, sl].astype(BF16)
            vh = kv_ref[:, vs].astype(BF16)
            doh = do_ref[:, sl].astype(BF16)
            p = _xattn_probs(qh, kh)
            dp = _nt(doh, vh)
            ds = (p * (dp - jnp.sum(p * dp, axis=-1, keepdims=True)) * 0.0625).astype(BF16)
            done.append((sl, vs, _nn(ds, kh), _tn(ds, qh), _tn(p.astype(BF16), doh)))
        for sl, vs, dq, dk, dv in done:
            dq_ref[:, sl] = dq.astype(dq_ref.dtype)
            dkv_ref[:, sl] += dk
            dkv_ref[:, vs] += dv

    return pl.pallas_call(
        body, name="xattn_bwd", grid=(S // TQ,),
        in_specs=[pl.BlockSpec((TQ, D), lambda i: (i, 0)), pl.BlockSpec((MEM, 2 * D), lambda i: (0, 0)),
                  pl.BlockSpec((TQ, D), lambda i: (i, 0))],
        out_specs=[pl.BlockSpec((TQ, D), lambda i: (i, 0)), pl.BlockSpec((MEM, 2 * D), lambda i: (0, 0))],
        out_shape=[jax.ShapeDtypeStruct((S, D), BF16), jax.ShapeDtypeStruct((MEM, 2 * D), F32)],
        compiler_params=_cparams(("arbitrary",)),
    )(q, kv, d_o)


def _s5_disc(a_re, a_im, log_dt, b_re, b_im):
    dt = jnp.exp(log_dt)[:, None]
    mag = jnp.exp(dt * a_re)
    abr = mag * jnp.cos(dt * a_im)
    abi = mag * jnp.sin(dt * a_im)
    nr, ni = abr - 1.0, abi
    inv = 1.0 / (a_re * a_re + a_im * a_im)
    cr = (nr * a_re + ni * a_im) * inv
    ci = (ni * a_re - nr * a_im) * inv
    bbr = cr[..., None] * b_re - ci[..., None] * b_im
    bbi = cr[..., None] * b_im + ci[..., None] * b_re
    return abr, abi, bbr, bbi


VM = pl.BlockSpec(memory_space=pltpu.VMEM)


def s5_embed(bt_re, bt_im, ct_re, ct_im):
    def body(br, bi, cr, ci, b_ref, c_ref):
        b_ref[...] = jnp.zeros_like(b_ref)
        c_ref[...] = jnp.zeros_like(c_ref)
        for g in range(NG):
            rows, cols = slice(g * NH, (g + 1) * NH), slice(g * NP, (g + 1) * NP)
            b_ref[rows, cols] = br[g]
            b_ref[rows, NS + g * NP:NS + (g + 1) * NP] = bi[g]
            c_ref[cols, rows] = cr[g]
            c_ref[NS + g * NP:NS + (g + 1) * NP, rows] = -ci[g]

    return pl.pallas_call(
        body, name="s5_embed", in_specs=[VM] * 4, out_specs=[VM] * 2,
        out_shape=[jax.ShapeDtypeStruct((NG * NH, 2 * NS), F32), jax.ShapeDtypeStruct((2 * NS, NG * NH), F32)],
        compiler_params=pltpu.CompilerParams(vmem_limit_bytes=VMEM_LIMIT),
    )(bt_re, bt_im, ct_re, ct_im)


def s5_extract(gb, gc):
    def body(gb_ref, gc_ref, br, bi, cr, ci):
        for g in range(NG):
            rows, cols = slice(g * NH, (g + 1) * NH), slice(g * NP, (g + 1) * NP)
            br[g] = gb_ref[rows, cols]
            bi[g] = gb_ref[rows, NS + g * NP:NS + (g + 1) * NP]
            cr[g] = gc_ref[cols, rows]
            ci[g] = -gc_ref[NS + g * NP:NS + (g + 1) * NP, rows]

    return pl.pallas_call(
        body, name="s5_extract", in_specs=[VM] * 2, out_specs=[VM] * 4,
        out_shape=[jax.ShapeDtypeStruct((NG, NH, NP), F32)] * 2 + [jax.ShapeDtypeStruct((NG, NP, NH), F32)] * 2,
        compiler_params=pltpu.CompilerParams(vmem_limit_bytes=VMEM_LIMIT),
    )(gb, gc)


HC, HS = NG * NH // 2, NS // 2
TS = 1024


def s5_to_states(x, w, mode, name, z_off=0):
    if mode == "nn":
        wb, wm = (HC, HS), lambda i, j, kk: (j % 2, j)
    else:
        wb, wm = (HS, HC), lambda i, j, kk: (j, j % 2)
    return mm_band(x, w, mode, name, (S // TS, 4, 1), ((TS, HC), wb, (TS, HS)),
                   (lambda i, j, kk: (i, z_off + j % 2), wm, lambda i, j, kk: (i, j)), (S, 2 * NS))


def s5_to_channels(x, w, mode, name, add=None):
    if mode == "nn":
        wb, wm = (HS, HC), lambda i, j, kk: (j + 2 * kk, j)
    else:
        wb, wm = (HC, HS), lambda i, j, kk: (j, j + 2 * kk)
    return mm_band(x, w, mode, name, (S // TS, 2, 2), ((TS, HS), wb, (TS, HC)),
                   (lambda i, j, kk: (i, j + 2 * kk), wm, lambda i, j, kk: (i, j)), (S, NG * NH), add=add)


def s5_outer(a, b, name, states_first, z_off=0):
    if states_first:
        return mm_band(a, b, "tn", name, (4, 1, 1), ((S, HS), (S, HC), (HS, HC)),
                       (lambda i, j, kk: (0, i), lambda i, j, kk: (0, i % 2), lambda i, j, kk: (i, i % 2)),
                       (2 * NS, NG * NH))
    return mm_band(a, b, "tn", name, (1, 4, 1), ((S, HC), (S, HS), (HC, HS)),
                   (lambda i, j, kk: (0, z_off + j % 2), lambda i, j, kk: (0, j), lambda i, j, kk: (j % 2, j)),
                   (NG * NH, 2 * NS))


def _fwd_even(i, x, P, W):
    hn = rms_fwd(x, P["norm_ab"][i:i + 1], "rms_ab_fwd")
    z = mm(m2(hn), W["w_in"], "nn", "in_ab")
    o, lse, cat = attn_fwd(z)
    if "more" in W:
        W.update(W.pop("more")(cat))
    cat = pool_fwd(z, W["pool_w"], P["pool_scale"][i:i + 1], cat)
    x_mid = mm(m2(cat), W["w_out"], "nn", "out_ab", add=m2(x))
    return x_mid, dict(x=x, hn=hn, z=z, o=o, lse=lse, cat=cat)


def _bwd_even(i, dx_mid, sv, P, W, G, GW):
    z = sv["z"]
    d_cat = mm(m2(dx_mid), W["w_out"], "nt", "out_ab_dx")
    GW["w_out"] = mm(m2(sv["cat"]), m2(dx_mid), "tn", "out_ab_dw").reshape(4, 512, D)
    dq, dk, dv, dga = attn_bwd(z, d_cat, sv["o"], sv["lse"])
    dvb, dgb, dpw, dps = pool_bwd(z, d_cat, W["pool_w"], P["pool_scale"][i:i + 1])
    GW["pool_w"] = dpw.reshape(4, 4, 64, 256).transpose(1, 0, 2, 3).reshape(4, 256, 256)
    G["pool_scale"][i] = dps[0]
    d_z = assemble_dz_even((dq, dk, dv, dga, dvb, dgb))
    d_hn = mm(m2(d_z), W["w_in"], "nt", "in_ab_dx")
    GW["w_in"] = mm(m2(sv["hn"]), m2(d_z), "tn", "in_ab_dw", out=outcs(D, 1536))
    return d_hn, P["norm_ab"][i:i + 1], "norm_ab", "rms_ab_bwd"


def _fwd_odd(i, x, P, W):
    hn = rms_fwd(x, P["norm_cd"][i:i + 1], "rms_cd_fwd")
    z = mm(m2(hn), W["w_in"], "nn", "in_cd")
    bfull = jnp.repeat(P["sgu_b"][i].T, 256, axis=1)
    c_out = sgu_fwd(z, P["sgu_ln_g"][i:i + 1], P["sgu_ln_b"][i:i + 1], P["sgu_w"][i], bfull)
    disc, disc_vjp = jax.vjp(_s5_disc, P["s5_a_re"][i], P["s5_a_im"][i], P["s5_log_dt"][i],
                             P["s5_b_re"][i], P["s5_b_im"][i])
    abr, abi, bbr, bbi = disc
    bbd, cbd = s5_embed(bbr.transpose(0, 2, 1), bbi.transpose(0, 2, 1),
                        P["s5_c_re"][i].transpose(0, 2, 1), P["s5_c_im"][i].transpose(0, 2, 1))
    abr, abi = abr.reshape(1, NS), abi.reshape(1, NS)
    bu = s5_to_states(z, bbd, "nn", "s5_bu", z_off=3072 // HC)
    h = scan_fwd(bu, abr, abi)
    hc = s5_to_channels(h, cbd, "nn", "s5_hc")
    dskip = P["s5_d"][i:i + 1]
    ypre, yg = s5_post(hc, z, dskip)
    if "more" in W:
        W.update(W.pop("more")(yg))
    w12 = W["w12"]
    t = mm(m2(yg), m2(w12), "nn", "glu_t")
    cat = glu_fwd(t, z, c_out)
    x_mid = mm(m2(cat), W["w_out"], "nn", "out_cd", add=m2(x))
    return x_mid, dict(x=x, hn=hn, z=z, bfull=bfull, disc_vjp=disc_vjp, bbd=bbd, cbd=cbd, abr=abr,
                       abi=abi, h=h, ypre=ypre, yg=yg, w12=w12, t=t, cat=cat, dskip=dskip)


def _bwd_odd(i, dx_mid, sv, P, W, G, GW):
    z = sv["z"]
    d_cat = mm(m2(dx_mid), W["w_out"], "nt", "out_cd_dx")
    GW["w_out"] = mm(m2(sv["cat"]), m2(dx_mid), "tn", "out_cd_dw").reshape(4, 384, D)
    du, dv, dgc, dws, dbs, dlg, dlb = sgu_bwd(z, d_cat, P["sgu_ln_g"][i:i + 1], P["sgu_ln_b"][i:i + 1],
                                               P["sgu_w"][i], sv["bfull"])
    G["sgu_w"][i], G["sgu_b"][i] = dws, dbs[:, :4].T
    G["sgu_ln_g"][i], G["sgu_ln_b"][i] = dlg[0], dlb[0]
    dt, dgd = glu_bwd(sv["t"], z, d_cat)
    gw12 = mm(m2(sv["yg"]), m2(dt), "tn", "glu_dw")
    GW["glu_w1"] = gw12[:, :512].reshape(4, 128, 512)
    GW["glu_w2"] = gw12[:, 512:].reshape(4, 128, 512)
    dyg = mm(m2(dt), m2(sv["w12"]), "nt", "glu_dx")
    dypre, dxd1, dd = s5_post_bwd(dyg, sv["ypre"], z, sv["dskip"])
    G["s5_d"][i] = dd[0]
    gcbd = s5_outer(sv["h"], dypre, "s5_dc", states_first=True)
    eta = s5_to_states(dypre, sv["cbd"], "nt", "s5_eta")
    lam, dacc = scan_bwd(eta, sv["h"], sv["abr"], sv["abi"])
    gbbd = s5_outer(z, lam, "s5_db", states_first=False, z_off=3072 // HC)
    dxd = s5_to_channels(lam, sv["bbd"], "nt", "s5_dx", add=dxd1)
    dacc = jnp.sum(dacc, axis=0)
    dbt_re, dbt_im, dct_re, dct_im = s5_extract(gbbd, gcbd)
    G["s5_c_re"][i], G["s5_c_im"][i] = dct_re.transpose(0, 2, 1), dct_im.transpose(0, 2, 1)
    d_bbr, d_bbi = dbt_re.transpose(0, 2, 1), dbt_im.transpose(0, 2, 1)
    (G["s5_a_re"][i], G["s5_a_im"][i], G["s5_log_dt"][i], G["s5_b_re"][i], G["s5_b_im"][i]) = sv["disc_vjp"](
        (dacc[:NS].reshape(NG, NP), dacc[NS:].reshape(NG, NP), d_bbr, d_bbi))
    d_z = assemble_dz_odd(du, dv, dgc, dxd, dgd)
    d_hn = mm(m2(d_z), W["w_in"], "nt", "in_cd_dx")
    GW["w_in"] = mm(m2(sv["hn"]), m2(d_z), "tn", "in_cd_dw", out=outcs(D, 1024))
    return d_hn, P["norm_cd"][i:i + 1], "norm_cd", "rms_cd_bwd"


def _fwd_x(l, x, mem_n, P, W):
    hx = rms_fwd(x, P["norm_x"][l:l + 1], "rms_x_fwd")
    q = mm(m2(hx), W["w_xq"], "nn", "xq", out_dtype=BF16)
    kv = mm(m2(mem_n), W["w_xkv"], "nn", "xkv", out_dtype=BF16)
    ox = xattn_fwd(q, kv)
    x_out = mm(m2(ox), W["w_xo"], "nn", "xo", add=m2(x))
    return x_out, dict(x=x, hx=hx, q=q, kv=kv, ox=ox)


def _bwd_x(l, dx_out, sv, mem_n, d_memn, P, W, G, GW):
    d_ox = mm(m2(dx_out), W["w_xo"], "nt", "xo_dx", out_dtype=BF16)
    GW["w_xo"] = mm(m2(sv["ox"]), m2(dx_out), "tn", "xo_dw").reshape(4, 256, D)
    dq, dkv = xattn_bwd(sv["q"], sv["kv"], d_ox)
    GW["w_xq"] = mm(m2(sv["hx"]), m2(dq), "tn", "xq_dw").reshape(4, 256, D)
    d_hx = mm(m2(dq), W["w_xq"], "nt", "xq_dx")
    GW["w_xkv"] = mm(m2(mem_n), m2(dkv), "tn", "xkv_dw", out=outcs(D, 512))
    d_memn = mm(m2(dkv), W["w_xkv"], "nt", "xkv_dx", add=None if d_memn is None else m2(d_memn))
    dx, dg = rms_bwd(sv["x"], d_hx, dx_out, P["norm_x"][l:l + 1], "rms_x_bwd")
    G["norm_x"][l] = dg[0]
    return dx, d_memn


SMALL_LAYERS = (("norm_ab", 2), ("pool_scale", 2), ("norm_cd", 2), ("sgu_ln_g", 2), ("sgu_ln_b", 2), ("sgu_w", 2),
                ("sgu_b", 2), ("s5_a_re", 2), ("s5_a_im", 2), ("s5_log_dt", 2), ("s5_b_re", 2), ("s5_b_im", 2),
                ("s5_c_re", 2), ("s5_c_im", 2), ("s5_d", 2), ("norm_x", 4))


def local_step(x, mem, tgt, P, weights_of, grads_done):
    G = {k: [None] * n for k, n in SMALL_LAYERS}
    mem_g = P["mem_norm"].reshape(1, D)
    mem_n = rms_fwd(mem, mem_g, "rms_mem_fwd")
    saved = []
    for layer in range(4):
        i = layer // 2
        W = weights_of(layer, x)
        x, sv_m = (_fwd_even if layer % 2 == 0 else _fwd_odd)(i, x, P, W)
        x, sv_x = _fwd_x(layer, x, mem_n, P, W)
        saved.append((sv_m, sv_x, W))
    dx, loss, dgf = final_loss(x, tgt, P["final_norm"].reshape(1, D))
    G["final_norm"] = dgf[0]
    d_memn = None
    for layer in reversed(range(4)):
        i = layer // 2
        sv_m, sv_x, W = saved[layer]
        GW = {}
        dx_mid, d_memn = _bwd_x(layer, dx, sv_x, mem_n, d_memn, P, W, G, GW)
        d_hn, g, key, name = (_bwd_even if layer % 2 == 0 else _bwd_odd)(i, dx_mid, sv_m, P, W, G, GW)
        token = grads_done(layer, GW)
        if token is not None:
            g = g + token
        dx, dg = rms_bwd(sv_m["x"], d_hn, dx_mid, g, name)
        G[key][i] = dg[0]
    _, dgm = rms_bwd(mem, d_memn, d_memn, mem_g, "rms_mem_bwd")
    G["mem_norm"] = dgm[0]
    return loss, dx, G


ANY = pl.BlockSpec(memory_space=pl.ANY)


def _place():
    x, y, c = lax.axis_index("x"), lax.axis_index("y"), lax.axis_index("c")
    chips = [(1 - x, y), (x, 1 - y), (1 - x, 1 - y)]
    return x, y, c, 2 * x + y, (x, y, 1 - c), chips


def _remote(src, dst, send, recv, k, dev):
    return pltpu.make_async_remote_copy(src_ref=src, dst_ref=dst, send_sem=send.at[k], recv_sem=recv.at[k],
                                        device_id=dev, device_id_type=MESHID)


HBM = pl.BlockSpec(memory_space=pltpu.HBM)
SEM = pl.BlockSpec(memory_space=pltpu.SEMAPHORE)
EFFECT = pltpu.SideEffectType.DATAFLOW_SIDE_EFFECTING


def _hbm(t):
    return pltpu.with_memory_space_constraint(t, pltpu.HBM)


def allgather_sync(shards):
    n = len(shards)

    def body(*refs):
        ins, outs = refs[:n], refs[n:2 * n]
        token, send, recv = refs[2 * n:]
        x, y, c, jme, sib, chips = _place()
        first, passed = [], []
        for a in range(n):
            cp = _remote(ins[a], outs[a].at[jme], send, recv, a * 7 + 6, sib)
            cp.start()
            first.append(cp)
            for k, chip in enumerate(chips):
                cp = _remote(ins[a].at[c], outs[a].at[jme, c], send, recv, a * 7 + k, (*chip, c))
                cp.start()
                first.append(cp)
        for a in range(n):
            for k, chip in enumerate(chips):
                piece = outs[a].at[2 * chip[0] + chip[1], c]
                _remote(piece, piece, send, recv, a * 7 + k, (*chip, c)).wait_recv()
                fw = _remote(piece, piece, send, recv, a * 7 + 3 + k, sib)
                fw.start()
                passed.append(fw)
        for a in range(n):
            own = outs[a].at[jme]
            _remote(own, own, send, recv, a * 7 + 6, sib).wait_recv()
            for k, chip in enumerate(chips):
                piece = outs[a].at[2 * chip[0] + chip[1], 1 - c]
                _remote(piece, piece, send, recv, a * 7 + 3 + k, sib).wait_recv()
        for cp in first + passed:
            cp.wait_send()
        token[...] = jnp.zeros_like(token)

    res = pl.pallas_call(
        body, name="allgather_sync", in_specs=[ANY] * n,
        out_specs=[ANY] * n + [pl.BlockSpec(memory_space=pltpu.VMEM)],
        out_shape=[jax.ShapeDtypeStruct((4,) + s.shape, s.dtype) for s in shards] + [jax.ShapeDtypeStruct((8, 128), F32)],
        scratch_shapes=[pltpu.SemaphoreType.DMA((7 * n,)), pltpu.SemaphoreType.DMA((7 * n,))],
    )(*shards)
    return list(res[:n]), res[n]


def _gather_copies(ins, lands, send, recv):
    x, y, c, jme, sib, chips = _place()
    devs = [(*chip, c) for chip in chips] + [sib]
    return [_remote(ins[a], lands[a].at[jme], send, recv, a * 4 + k, dev)
            for a in range(len(ins)) for k, dev in enumerate(devs)]


def allgather_start(shards, after, name):
    n, na = len(shards), len(after)

    def body(*refs):
        ins, lands = refs[:n], refs[n:2 * n]
        send, recv = refs[2 * n + na], refs[2 * n + na + 1]
        token = refs[-1]
        for cp in _gather_copies(ins, lands, send, recv):
            cp.start()
        token[...] = jnp.zeros_like(token)

    res = pl.pallas_call(
        body, name=name,
        out_shape=(pltpu.SemaphoreType.DMA((4 * n,)), pltpu.SemaphoreType.DMA((4 * n,)),
                   *[pltpu.HBM(s.shape, s.dtype) for s in shards],
                   *[pltpu.HBM((4,) + s.shape, s.dtype) for s in shards],
                   jax.ShapeDtypeStruct((8, 128), F32)),
        in_specs=[HBM] * (2 * n) + [ANY] * na,
        out_specs=(SEM, SEM, *[HBM] * (2 * n), pl.BlockSpec(memory_space=pltpu.VMEM)),
        input_output_aliases={a: 2 + a for a in range(2 * n)},
        compiler_params=pltpu.CompilerParams(has_side_effects=EFFECT),
    )(*[_hbm(s) for s in shards], *[_hbm(lax.empty((4,) + s.shape, s.dtype)) for s in shards], *after)
    return res[0], res[1], list(res[2:2 + n]), list(res[2 + n:2 + 2 * n]), res[-1]


def allgather_wait(send, recv, shards, lands, after, name):
    n = len(shards)

    def body(*refs):
        ins, zones = refs[:n], refs[n:2 * n]
        send_r, recv_r = refs[2 * n], refs[2 * n + 1]
        x, y, c, jme, sib, chips = _place()
        slots = [2 * chip[0] + chip[1] for chip in chips] + [jme]
        for a in range(n):
            for k, slot in enumerate(slots):
                cp = _remote(ins[a], zones[a].at[slot], send_r, recv_r, a * 4 + k, sib)
                cp.wait_send()
                cp.wait_recv()

    res = pl.pallas_call(
        body, name=name,
        out_shape=tuple(pltpu.HBM(t.shape, t.dtype) for t in list(shards) + list(lands)),
        in_specs=[HBM] * (2 * n) + [SEM, SEM, ANY], out_specs=tuple([HBM] * (2 * n)),
        input_output_aliases={a: a for a in range(2 * n)},
        compiler_params=pltpu.CompilerParams(has_side_effects=EFFECT),
    )(*shards, *lands, send, recv, after)
    return list(res[n:])


def allgather_small(slab):
    def body(in_ref, out_ref, send, recv, lsem):
        x, y, c, jme, sib, chips = _place()
        loc = pltpu.make_async_copy(in_ref, out_ref.at[jme], lsem.at[0])
        loc.start()
        cps = [_remote(in_ref, out_ref.at[jme], send, recv, k, (*chip, c)) for k, chip in enumerate(chips)]
        for cp in cps:
            cp.start()
        for k, chip in enumerate(chips):
            piece = out_ref.at[2 * chip[0] + chip[1]]
            _remote(piece, piece, send, recv, k, (*chip, c)).wait_recv()
        for cp in cps:
            cp.wait_send()
        loc.wait()

    return pl.pallas_call(
        body, name="allgather_small", in_specs=[ANY], out_specs=ANY,
        out_shape=jax.ShapeDtypeStruct((4,) + slab.shape, slab.dtype),
        scratch_shapes=[pltpu.SemaphoreType.DMA((3,)), pltpu.SemaphoreType.DMA((3,)), pltpu.SemaphoreType.DMA((1,))],
    )(slab)


def allreduce_small(v):
    def body(v_ref, o_ref, r0, r1, r2, send, recv):
        x, y, c, jme, sib, chips = _place()
        peers = [sib, (1 - x, y, c), (x, 1 - y, c)]
        o_ref[...] = v_ref[...]
        for k, buf in enumerate((r0, r1, r2)):
            cp = _remote(o_ref, buf, send, recv, k, peers[k])
            cp.start()
            cp.wait()
            o_ref[...] = o_ref[...] + buf[...]

    vm = pl.BlockSpec(memory_space=pltpu.VMEM)
    return pl.pallas_call(
        body, name="allreduce_small", in_specs=[vm], out_specs=vm,
        out_shape=jax.ShapeDtypeStruct(v.shape, v.dtype),
        scratch_shapes=[pltpu.VMEM(v.shape, v.dtype)] * 3 + [pltpu.SemaphoreType.DMA((3,)), pltpu.SemaphoreType.DMA((3,))],
        compiler_params=pltpu.CompilerParams(vmem_limit_bytes=VMEM_LIMIT),
    )(v)


def _pair_copies(gs, lands, send, recv):
    x, y, c, jme, sib, chips = _place()
    return [_remote(gs[a].at[:, 1 - c], lands[a], send, recv, a, sib) for a in range(len(gs))]


def rs_pair_start(gs, name):
    n = len(gs)

    def body(*refs):
        ins, lands = refs[:n], refs[n:2 * n]
        send, recv = refs[2 * n], refs[2 * n + 1]
        token = refs[-1]
        for cp in _pair_copies(ins, lands, send, recv):
            cp.start()
        token[...] = jnp.zeros_like(token)

    shapes = [(4,) + g.shape[2:] for g in gs]
    res = pl.pallas_call(
        body, name=name,
        out_shape=(pltpu.SemaphoreType.DMA((n,)), pltpu.SemaphoreType.DMA((n,)),
                   *[pltpu.HBM(g.shape, g.dtype) for g in gs], *[pltpu.HBM(s, F32) for s in shapes],
                   jax.ShapeDtypeStruct((8, 128), F32)),
        in_specs=[HBM] * (2 * n), out_specs=(SEM, SEM, *[HBM] * (2 * n), pl.BlockSpec(memory_space=pltpu.VMEM)),
        input_output_aliases={a: 2 + a for a in range(2 * n)},
        compiler_params=pltpu.CompilerParams(has_side_effects=EFFECT),
    )(*[_hbm(g) for g in gs], *[_hbm(lax.empty(s, F32)) for s in shapes])
    return res[0], res[1], list(res[2:2 + n]), list(res[2 + n:2 + 2 * n]), res[-1]


def rs_pair_wait(send, recv, gs, lands, after, name):
    n = len(gs)

    def body(*refs):
        ins, zones = refs[:n], refs[n:2 * n]
        for cp in _pair_copies(ins, zones, refs[2 * n], refs[2 * n + 1]):
            cp.wait_send()
            cp.wait_recv()

    res = pl.pallas_call(
        body, name=name,
        out_shape=tuple(pltpu.HBM(t.shape, t.dtype) for t in list(gs) + list(lands)),
        in_specs=[HBM] * (2 * n) + [SEM, SEM, ANY], out_specs=tuple([HBM] * (2 * n)),
        input_output_aliases={a: a for a in range(2 * n)},
        compiler_params=pltpu.CompilerParams(has_side_effects=EFFECT),
    )(*gs, *lands, send, recv, after)
    return list(res[:n]), list(res[n:])


SUM_ROWS = 256


def rs_pair_sum(g4s, gots, cidx):
    n = len(g4s)
    tiles = [(min(g.shape[2], SUM_ROWS), g.shape[3]) for g in g4s]
    nts = [g.shape[2] // tr for g, (tr, _) in zip(g4s, tiles)]

    def at(a, s):
        s = jnp.minimum(s, 4 * nts[a] - 1)
        return s // nts[a], s % nts[a]

    def body(c_ref, *refs):
        for a in range(n):
            refs[2 * n + a][...] = (refs[a][...] + refs[n + a][...]).astype(BF16)

    in_specs = [pl.BlockSpec((None, None) + tiles[a], lambda s, cr, a=a: (at(a, s)[0], cr[0], at(a, s)[1], 0))
                for a in range(n)]
    in_specs += [pl.BlockSpec((None,) + tiles[a], lambda s, cr, a=a: (*at(a, s), 0)) for a in range(n)]
    return pl.pallas_call(
        body, name="rs_pair_sum",
        grid_spec=pltpu.PrefetchScalarGridSpec(
            num_scalar_prefetch=1, grid=(4 * max(nts),), in_specs=in_specs,
            out_specs=[pl.BlockSpec((None,) + tiles[a], lambda s, cr, a=a: (*at(a, s), 0)) for a in range(n)]),
        out_shape=[jax.ShapeDtypeStruct((4,) + g.shape[2:], BF16) for g in g4s],
        compiler_params=_cparams(("arbitrary",)),
    )(cidx, *g4s, *gots)


def _chip_copies(ps, lands, send, recv):
    x, y, c, jme, sib, chips = _place()
    return [_remote(ps[a].at[2 * chip[0] + chip[1]], lands[a].at[jme], send, recv, a * 3 + k, (*chip, c))
            for a in range(len(ps)) for k, chip in enumerate(chips)]


def rs_chip_start(ps, name):
    n = len(ps)

    def body(*refs):
        ins, lands = refs[:n], refs[n:2 * n]
        send, recv = refs[2 * n], refs[2 * n + 1]
        token = refs[-1]
        for cp in _chip_copies(ins, lands, send, recv):
            cp.start()
        token[...] = jnp.zeros_like(token)

    res = pl.pallas_call(
        body, name=name,
        out_shape=(pltpu.SemaphoreType.DMA((3 * n,)), pltpu.SemaphoreType.DMA((3 * n,)),
                   *[pltpu.HBM(p.shape, p.dtype) for p in ps], *[pltpu.HBM(p.shape, p.dtype) for p in ps],
                   jax.ShapeDtypeStruct((8, 128), F32)),
        in_specs=[HBM] * (2 * n), out_specs=(SEM, SEM, *[HBM] * (2 * n), pl.BlockSpec(memory_space=pltpu.VMEM)),
        input_output_aliases={a: 2 + a for a in range(2 * n)},
        compiler_params=pltpu.CompilerParams(has_side_effects=EFFECT),
    )(*[_hbm(p) for p in ps], *[_hbm(lax.empty(p.shape, p.dtype)) for p in ps])
    return res[0], res[1], list(res[2:2 + n]), list(res[2 + n:2 + 2 * n]), res[-1]


def rs_chip_wait(send, recv, ps, lands, after, name):
    n = len(ps)

    def body(*refs):
        ins, zones = refs[:n], refs[n:2 * n]
        send_r, recv_r = refs[2 * n], refs[2 * n + 1]
        x, y, c, jme, sib, chips = _place()
        for a in range(n):
            for k, chip in enumerate(chips):
                jt = 2 * chip[0] + chip[1]
                cp = _remote(ins[a].at[jt], zones[a].at[jt], send_r, recv_r, a * 3 + k, (*chip, c))
                cp.wait_send()
                cp.wait_recv()

    res = pl.pallas_call(
        body, name=name,
        out_shape=tuple(pltpu.HBM(p.shape, p.dtype) for p in list(ps) + list(lands)),
        in_specs=[HBM] * (2 * n) + [SEM, SEM] + [ANY] * len(after), out_specs=tuple([HBM] * (2 * n)),
        input_output_aliases={a: a for a in range(2 * n)},
        compiler_params=pltpu.CompilerParams(has_side_effects=EFFECT),
    )(*ps, *lands, send, recv, *after)
    return list(res[:n]), list(res[n:])


def rs_chip_sum(qs, ps, ls, accs, layers, jc):
    n = len(qs)
    tiles = [(min(q.shape[1], SUM_ROWS), q.shape[2]) for q in qs]
    nts = [q.shape[1] // tr for q, (tr, _) in zip(qs, tiles)]

    def at(a, s):
        return jnp.minimum(s, nts[a] - 1)

    def body(jc_ref, *refs):
        jme = jc_ref[0]
        for a in range(n):
            q_ref, p_ref, o_ref = refs[a], refs[n + a], refs[len(refs) - n + a]
            own = p_ref[...].astype(F32)
            v = [jnp.where(jme == j, own, q_ref[j].astype(F32)) for j in range(4)]
            o_ref[...] = ((v[0] + v[1]) + v[2]) + v[3]

    in_specs = [pl.BlockSpec((4,) + tiles[a], lambda s, jr, a=a: (0, at(a, s), 0)) for a in range(n)]
    in_specs += [pl.BlockSpec((None,) + tiles[a], lambda s, jr, a=a: (jr[0], at(a, s), 0)) for a in range(n)]
    args, aliases = [jc, *qs, *ps], {}
    for a in range(n):
        if accs[a] is not None:
            aliases[len(args)] = a
            in_specs.append(ANY)
            args.append(accs[a])
    return pl.pallas_call(
        body, name="rs_chip_sum",
        grid_spec=pltpu.PrefetchScalarGridSpec(
            num_scalar_prefetch=1, grid=(max(nts),), in_specs=in_specs,
            out_specs=[pl.BlockSpec((None, None) + tiles[a], lambda s, jr, a=a: (ls[a], jr[1], at(a, s), 0))
                       for a in range(n)]),
        out_shape=[jax.ShapeDtypeStruct((layers[a], 2) + qs[a].shape[1:], F32) for a in range(n)],
        input_output_aliases=aliases,
        compiler_params=_cparams(("arbitrary",)),
    )(*args)


def rs_pair_gather(rs):
    n = len(rs)

    def body(*refs):
        outs = refs[n:2 * n]
        send, recv = refs[2 * n:]
        x, y, c, jme, sib, chips = _place()
        cps = [_remote(outs[a].at[:, c], outs[a].at[:, c], send, recv, a, sib) for a in range(n)]
        for cp in cps:
            cp.start()
        for a in range(n):
            slot = outs[a].at[:, 1 - c]
            _remote(slot, slot, send, recv, a, sib).wait_recv()
        for cp in cps:
            cp.wait_send()

    return pl.pallas_call(
        body, name="rs_pair_gather", in_specs=[ANY] * n, out_specs=[ANY] * n,
        out_shape=[jax.ShapeDtypeStruct(r.shape, r.dtype) for r in rs],
        input_output_aliases={a: a for a in range(n)},
        scratch_shapes=[pltpu.SemaphoreType.DMA((n,)), pltpu.SemaphoreType.DMA((n,))],
    )(*rs)


def _adamw_math(w, g, m, v):
    m = B1 * m + (1.0 - B1) * g
    v = B2 * v + (1.0 - B2) * (g * g)
    m_hat = m / (1.0 - B1 ** STEP)
    v_hat = v / (1.0 - B2 ** STEP)
    return -LR * (m_hat / (jnp.sqrt(v_hat) + AEPS) + WD * w), m, v


def adamw(w, g, m, v, name, with_grad=False):
    rows, cols = w.shape
    tr = 256 if rows % 256 == 0 else rows
    fn = (lambda wv, gv, mv, vv: (gv,) + _adamw_math(wv, gv, mv, vv)) if with_grad else _adamw_math
    return rw(fn, [(a, 0, cols) for a in (w, g, m, v)], [(cols, F32)] * (4 if with_grad else 3), name, rows, tr=tr)


def adamw_small(ws, gs, ms, vs):
    n = len(ws)

    def body(*refs):
        for a in range(n):
            res = _adamw_math(*[refs[k * n + a][...] for k in range(4)])
            for k in range(3):
                refs[(4 + k) * n + a][...] = res[k]

    res = pl.pallas_call(
        body, name="adamw_small", in_specs=[VM] * (4 * n), out_specs=[VM] * (3 * n),
        out_shape=[jax.ShapeDtypeStruct(w.shape, F32) for _ in range(3) for w in ws],
        compiler_params=pltpu.CompilerParams(vmem_limit_bytes=VMEM_LIMIT),
    )(*ws, *gs, *ms, *vs)
    return [(res[a], res[n + a], res[2 * n + a]) for a in range(n)]


WEIGHTS = ["norm_ab", "w_in_ab", "pool_w", "pool_scale", "w_out_ab", "norm_cd", "w_in_cd", "sgu_ln_g", "sgu_ln_b",
           "sgu_w", "sgu_b", "s5_a_re", "s5_a_im", "s5_log_dt", "s5_b_re", "s5_b_im", "s5_c_re", "s5_c_im", "s5_d",
           "glu_w1", "glu_w2", "w_out_cd", "norm_x", "w_xq", "w_xkv", "w_xo", "mem_norm", "final_norm"]
INPUTS = ["x", "mem"] + WEIGHTS + ["loss_target"] + ["m_" + n for n in WEIGHTS] + ["v_" + n for n in WEIGHTS]
BIG = ["w_in_ab", "w_out_ab", "w_in_cd", "w_out_cd", "w_xq", "w_xkv", "w_xo", "glu_w1", "glu_w2", "pool_w"]
COL_SHARDED = ("w_in_ab", "w_in_cd", "w_xkv")
SMALL = [n for n in WEIGHTS if n not in BIG]
SMALL_SHARDED = {"norm_cd": 256, "sgu_ln_g": 256, "sgu_ln_b": 256, "s5_d": 128}
PACK = 256 * 128


def _pack(arrs):
    flat = jnp.concatenate([a.reshape(-1) for a in arrs])
    pad = (-flat.shape[0]) % PACK
    return jnp.concatenate([flat, jnp.zeros((pad,), flat.dtype)]).reshape(-1, 128)


def _unpack(packed, shapes):
    flat, out, off = packed.reshape(-1), [], 0
    for s in shapes:
        n = 1
        for d in s:
            n *= d
        out.append(flat[off:off + n].reshape(s))
        off += n
    return out


LAYER_KEYS = (("w_in", "w_out", "pool_w", "w_xq", "w_xkv", "w_xo"),
              ("w_in", "w_out", "glu_w1", "glu_w2", "w_xq", "w_xkv", "w_xo"))


def _weight_of(key, layer):
    if key in ("w_xq", "w_xkv", "w_xo"):
        return key, layer, 4
    kind = "ab" if layer % 2 == 0 else "cd"
    return {"w_in": "w_in_" + kind, "w_out": "w_out_" + kind}.get(key, key), layer // 2, 2


def kernel(*args):
    a = dict(zip(INPUTS, args))
    x_i, y_i, c_i = lax.axis_index("x"), lax.axis_index("y"), lax.axis_index("c")
    j = 2 * x_i + y_i

    slab = jnp.concatenate([a["norm_cd"], a["sgu_ln_g"], a["sgu_ln_b"],
                            jnp.pad(a["s5_d"], ((0, 0), (0, 128)))], axis=0)
    gslab = allgather_small(slab)
    P = {n: a[n] for n in SMALL}
    for k, n in enumerate(("norm_cd", "sgu_ln_g", "sgu_ln_b", "s5_d")):
        wd = SMALL_SHARDED[n]
        P[n] = gslab[:, 2 * k:2 * k + 2, :wd].transpose(1, 0, 2).reshape(2, 4 * wd)

    def shards_of(layer):
        keys = sorted(k for k in LAYER_KEYS[layer % 2])
        out = []
        for k in keys:
            n, l, _ = _weight_of(k, layer)
            out.append(a[n][l].reshape(-1, a[n].shape[-1]).astype(BF16))
        return keys, out

    keys0, sh0 = shards_of(0)
    first = keys0.index("w_in")
    g_in, token = allgather_sync([sh0[first].reshape(2, sh0[first].shape[0] // 2, sh0[first].shape[1])])
    w_in0 = g_in[0].reshape(4, -1, g_in[0].shape[-1])
    started = {}
    for layer in (0, 1, 2, 3):
        keys, sh = (keys0, sh0) if layer == 0 else shards_of(layer)
        rest = [(k, s) for k, s in zip(keys, sh) if k != "w_in"]
        parts = [("in", ["w_in"], [sh[keys.index("w_in")]])] * (layer > 0) + [("", *map(list, zip(*rest)))]
        for tag, pk, ps in parts:
            send, recv, ps, lands, token = allgather_start(ps, [token, gslab], "allgather_start_%d%s" % (layer, tag))
            started[(layer, tag)] = (pk, send, recv, ps, lands)
    P["norm_ab"] = P["norm_ab"] + token[0:1, 0:1]

    cidx = jnp.reshape(c_i, (1,)).astype(jnp.int32)
    jc = jnp.stack([j, c_i]).astype(jnp.int32)

    def views(g):
        W = {}
        for k, v in g.items():
            if k in ("w_in", "w_xkv"):
                W[k] = mcs(v)
            elif k == "pool_w":
                W[k] = v.reshape(4, 4, 64, 256).transpose(1, 0, 2, 3).reshape(4, 256, 256)
            elif k not in ("glu_w1", "glu_w2"):
                W[k] = m2(v.reshape(-1, v.shape[-1]))
        if "glu_w1" in g:
            W["w12"] = jnp.concatenate([g["glu_w1"].reshape(512, 512), g["glu_w2"].reshape(512, 512)], axis=1)
        return W

    def arrived(layer, tag, after):
        keys, send, recv, sh, lands = started[(layer, tag)]
        return views(dict(zip(keys, allgather_wait(send, recv, sh, lands, after, "allgather_wait_%d%s" % (layer, tag)))))

    def weights_of(layer, x_in):
        W = views({"w_in": w_in0}) if layer == 0 else arrived(layer, "in", x_in)
        W["more"] = lambda after: arrived(layer, "", after)
        return W

    halves, pending = {}, {}

    def finish_pair(layer, after):
        keys, send, recv, flat, lands = halves.pop(layer)
        flat, got = rs_pair_wait(send, recv, flat, lands, after, "rs_pair_wait_%d" % layer)
        pair = rs_pair_sum(flat, got, cidx)
        send, recv, pair, lands, token = rs_chip_start(pair, "rs_chip_start_%d" % layer)
        pending[layer] = (keys, send, recv, pair, lands)
        return token

    def grads_done(layer, GW):
        keys = sorted(GW)
        flat = [GW[k].reshape(4, 2, GW[k].shape[1] // 2, GW[k].shape[2]) for k in keys]
        send, recv, flat, lands, token = rs_pair_start(flat, "rs_pair_start_%d" % layer)
        halves[layer] = (keys, send, recv, flat, lands)
        if layer + 1 in halves:
            token = token + finish_pair(layer + 1, token)
        return token[0:1, 0:1]

    loss, dx, G = local_step(a["x"][0], a["mem"][0], a["loss_target"][0], P, weights_of, grads_done)
    loss = lax.psum(loss[0, 0], ("x", "y", "c"))
    finish_pair(0, dx)
    outs = {}

    def update_big(names, red):
        for n, g in zip(names, rs_pair_gather([red[n] for n in names])):
            shp = a[n].shape
            g2 = g.reshape(-1, shp[-1])
            upd = adamw(a[n].reshape(g2.shape), g2, a["m_" + n].reshape(g2.shape), a["v_" + n].reshape(g2.shape),
                        "adamw_" + n, with_grad=True)
            outs[n] = tuple(t.reshape(shp) for t in upd)

    def reduce_layer(layer, red, after):
        keys, send, recv, pair, lands = pending[layer]
        pair, lands = rs_chip_wait(send, recv, pair, lands, after, "rs_chip_wait_%d" % layer)
        which = [_weight_of(k, layer) for k in keys]
        sums = rs_chip_sum(lands, pair, [l for _, l, _ in which], [red.get(n) for n, _, _ in which],
                           [layers for _, _, layers in which], jc)
        red.update(zip([n for n, _, _ in which], sums))

    red = {}
    for layer in (3, 2, 1):
        reduce_layer(layer, red, [dx])
    odd_only = [n for n in BIG if n.endswith("_cd") or n.startswith("glu")]
    update_big(odd_only, red)

    gfull = [jnp.stack(G[n]) if isinstance(G[n], list) else G[n] for n in SMALL]
    shapes = [g.shape for g in gfull]
    gsum = _unpack(allreduce_small(_pack(gfull)), shapes)
    gloc = []
    for n, g in zip(SMALL, gsum):
        if n in SMALL_SHARDED:
            g = lax.dynamic_slice_in_dim(g, j * SMALL_SHARDED[n], SMALL_SHARDED[n], axis=1)
        gloc.append(g)
    two = [(-1, a[n].shape[-1]) if a[n].ndim > 1 else (1, a[n].shape[0]) for n in SMALL]
    upds = adamw_small(*[[t.reshape(s) for t, s in zip(ts, two)]
                         for ts in ([a[n] for n in SMALL], gloc, [a["m_" + n] for n in SMALL],
                                    [a["v_" + n] for n in SMALL])])
    for n, g, upd in zip(SMALL, gloc, upds):
        outs[n] = (g,) + tuple(t.reshape(a[n].shape) for t in upd)

    behind = [outs[n][1] for n in odd_only + SMALL[-1:]] + [red[n] for n in BIG if n not in odd_only]
    reduce_layer(0, red, behind)
    update_big([n for n in BIG if n not in odd_only], red)

    res = [loss, dx[None]]
    for part in range(4):
        res += [outs[n][part] for n in WEIGHTS]
    return tuple(res)
```

```python
import math

import jax
import jax.numpy as jnp
from jax import lax
from jax.experimental import pallas as pl
from jax.experimental.pallas import tpu as pltpu

F32, BF16 = jnp.float32, jnp.bfloat16
S, D = 2048, 1024
MEM = 256
EPS = 1e-6
NEG = -1e30
QB = 128
PATTERNS = (1, 4, 16)
NG, NP, NH = 32, 64, 16
NS = NG * NP
LR, B1, B2, AEPS, WD, STEP = 0.001, 0.9, 0.999, 1e-08, 0.01, 10
MESHID = pl.DeviceIdType.MESH
VMEM_LIMIT = 56 * 1024 * 1024


def _cparams(sem):
    return pltpu.CompilerParams(dimension_semantics=sem, vmem_limit_bytes=VMEM_LIMIT)


def _sig(x):
    return 1.0 / (1.0 + jnp.exp(-x))


def _dot(a, b, dims):
    return lax.dot_general(a, b, (dims, ((), ())), preferred_element_type=F32)


def _nn(a, b):
    return _dot(a, b, ((1,), (0,)))


def _nt(a, b):
    return _dot(a, b, ((1,), (1,)))


def _tn(a, b):
    return _dot(a, b, ((0,), (0,)))


_DIMS = {"nn": ((1,), (0,)), "nt": ((1,), (1,)), "tn": ((0,), (0,))}


def _tile(dim, cc=None, cap=1024):
    for t in (2048, 1536, 1024, 768, 512, 384, 256, 128):
        if t <= cap and dim % t == 0 and (cc is None or cc % t == 0):
            return t
    return dim


MM_VMEM = 36 * 1024 * 1024


def _mm_tiles(m, n, k, ccm, ccn, cck, a_bytes, b_bytes, o_bytes):
    caps = [1024, 1024, 2048]
    while True:
        tm, tn, tk = _tile(m, ccm, caps[0]), _tile(n, ccn, caps[1]), _tile(k, cck, caps[2])
        need = 2 * (tm * tk * a_bytes + tk * tn * b_bytes + tm * tn * o_bytes) + (tm * tn * 4 if tk < k else 0)
        if need <= MM_VMEM:
            return tm, tn, tk
        if tk > 1024:
            caps[2] = tk // 2
        elif tn >= tm:
            caps[1] = tn // 2
        else:
            caps[0] = tm // 2


def m2(arr, col_off=0, ncols=None):
    rows, cols = arr.shape
    ncols = cols - col_off if ncols is None else ncols

    def spec(tr, tc, rc):
        assert col_off % tc == 0
        return pl.BlockSpec((tr, tc), lambda *g: (rc(*g)[0], rc(*g)[1] + col_off // tc))
    return (arr, rows, ncols, spec, None if col_off == 0 else col_off)


def mcs(arr):
    cs = arr.shape[2]

    def spec(tr, tc, rc):
        n = cs // tc
        return pl.BlockSpec((None, tr, tc), lambda *g: (rc(*g)[1] // n, rc(*g)[0], rc(*g)[1] % n))
    return (arr, arr.shape[1], 4 * cs, spec, cs)


def out2(rows, cols):
    def spec(tr, tc, rc):
        return pl.BlockSpec((tr, tc), lambda *g: tuple(rc(*g)))
    return ((rows, cols), spec, None)


def outcs(rows, cs):
    def spec(tr, tc, rc):
        n = cs // tc
        return pl.BlockSpec((None, tr, tc), lambda *g: (rc(*g)[1] // n, rc(*g)[0], rc(*g)[1] % n))
    return ((4, rows, cs), spec, cs)


def _both(a, b):
    if a is None:
        return b
    if b is None:
        return a
    return math.gcd(a, b)


def mm(a, b, mode, name, add=None, out=None, out_dtype=F32):
    a_arr, a_r, a_c, a_spec, a_cc = a
    b_arr, b_r, b_c, b_spec, b_cc = b
    if mode == "nn":
        m, k, n = a_r, a_c, b_c
        assert b_r == k
        ccm, cck, ccn = None, a_cc, b_cc
    elif mode == "nt":
        m, k, n = a_r, a_c, b_r
        assert b_c == k
        ccm, cck, ccn = None, _both(a_cc, b_cc), None
    else:
        m, k, n = a_c, a_r, b_c
        assert b_r == k
        ccm, cck, ccn = a_cc, None, b_cc
    out = out2(m, n) if out is None else out
    o_shape, o_spec, o_cc = out
    ccn = _both(ccn, o_cc)
    if add is not None:
        ccn = _both(ccn, add[4])
    o_bytes = jnp.dtype(out_dtype).itemsize + (0 if add is None else add[0].dtype.itemsize)
    tm, tn, tk = _mm_tiles(m, n, k, ccm, ccn, cck, a_arr.dtype.itemsize, b_arr.dtype.itemsize, o_bytes)
    nk = k // tk
    if mode == "nn":
        in_specs = [a_spec(tm, tk, lambda i, j, kk: (i, kk)), b_spec(tk, tn, lambda i, j, kk: (kk, j))]
    elif mode == "nt":
        in_specs = [a_spec(tm, tk, lambda i, j, kk: (i, kk)), b_spec(tn, tk, lambda i, j, kk: (j, kk))]
    else:
        in_specs = [a_spec(tk, tm, lambda i, j, kk: (kk, i)), b_spec(tk, tn, lambda i, j, kk: (kk, j))]
    args = [a_arr, b_arr]
    if add is not None:
        in_specs.append(add[3](tm, tn, lambda i, j, kk: (i, j)))
        args.append(add[0])
    return _mm_call(args, in_specs, o_spec(tm, tn, lambda i, j, kk: (i, j)), jax.ShapeDtypeStruct(o_shape, out_dtype),
                    mode, (m // tm, n // tn, nk), (tm, tn), add is not None, name)


def _mm_call(args, in_specs, out_spec, out_shape, mode, grid, tile, has_add, name):
    dims = _DIMS[mode]
    nk = grid[2]
    tm, tn = tile

    def body(*refs):
        a_ref, b_ref = refs[0], refs[1]
        add_ref = refs[2] if has_add else None
        prod = _dot(a_ref[...].astype(BF16), b_ref[...].astype(BF16), dims)
        if nk == 1:
            o_ref = refs[-1]
            if has_add:
                prod = prod + add_ref[...].astype(F32)
            o_ref[...] = prod.astype(o_ref.dtype)
            return
        o_ref, acc = refs[-2], refs[-1]
        kk = pl.program_id(2)

        @pl.when(kk == 0)
        def _():
            acc[...] = prod

        @pl.when(kk > 0)
        def _():
            acc[...] += prod

        @pl.when(kk == nk - 1)
        def _():
            r = acc[...]
            if has_add:
                r = r + add_ref[...].astype(F32)
            o_ref[...] = r.astype(o_ref.dtype)

    return pl.pallas_call(
        body, name=name, grid=grid, in_specs=in_specs, out_specs=out_spec, out_shape=out_shape,
        scratch_shapes=[pltpu.VMEM((tm, tn), F32)] if nk > 1 else [],
        compiler_params=_cparams(("parallel", "parallel", "arbitrary")),
    )(*args)


def mm_band(a, b, mode, name, grid, blocks, maps, out_shape, add=None, out_dtype=F32):
    in_specs = [pl.BlockSpec(blocks[0], maps[0]), pl.BlockSpec(blocks[1], maps[1])]
    args = [a, b]
    if add is not None:
        in_specs.append(pl.BlockSpec(blocks[2], maps[2]))
        args.append(add)
    return _mm_call(args, in_specs, pl.BlockSpec(blocks[2], maps[2]), jax.ShapeDtypeStruct(out_shape, out_dtype),
                    mode, grid, blocks[2], add is not None, name)


def rw(fn, ins, outs, name, rows, tr=None, consts=(), accs=()):
    tr = min(rows, 1024) if tr is None else tr
    n_in, n_c, n_o, n_a = len(ins), len(consts), len(outs), len(accs)
    in_specs = []
    for arr, off, width in ins:
        assert off % width == 0
        in_specs.append(pl.BlockSpec((tr, width), lambda i, o=off // width: (i, o)))
    for c in consts:
        in_specs.append(pl.BlockSpec(c.shape, lambda i: (0, 0)))
    out_specs = [pl.BlockSpec((tr, w), lambda i: (i, 0)) for w, _ in outs]
    out_specs += [pl.BlockSpec(s, lambda i: (0, 0)) for s in accs]
    out_shape = [jax.ShapeDtypeStruct((rows, w), dt) for w, dt in outs]
    out_shape += [jax.ShapeDtypeStruct(s, F32) for s in accs]

    def body(*refs):
        vals = [r[...] for r in refs[:n_in + n_c]]
        o_refs = refs[n_in + n_c:n_in + n_c + n_o]
        a_refs = refs[n_in + n_c + n_o:]
        res = fn(*vals)
        for r, v in zip(o_refs, res[:n_o]):
            r[...] = v.astype(r.dtype)
        if n_a:
            @pl.when(pl.program_id(0) == 0)
            def _():
                for r in a_refs:
                    r[...] = jnp.zeros_like(r)
            for r, v in zip(a_refs, res[n_o:]):
                r[...] += v

    res = pl.pallas_call(
        body, name=name, grid=(rows // tr,), in_specs=in_specs, out_specs=out_specs,
        out_shape=out_shape,
        compiler_params=_cparams(("arbitrary",) if n_a else ("parallel",)),
    )(*[a for a, _, _ in ins], *consts)
    return res


def _rstd(x):
    return lax.rsqrt(jnp.mean(x * x, axis=-1, keepdims=True) + EPS)


def rms_fwd(x, g, name):
    def fn(xv, gv):
        xv = xv.astype(F32)
        return (xv * _rstd(xv) * gv,)
    return rw(fn, [(x, 0, D)], [(D, BF16)], name, x.shape[0], consts=[g])[0]


def _rms_bwd_math(xv, dy, gv):
    r = _rstd(xv)
    dyg = dy * gv
    dx = r * dyg - xv * (r * r * r / D) * jnp.sum(dyg * xv, axis=-1, keepdims=True)
    dg = jnp.sum(dy * xv * r, axis=0, keepdims=True)
    return dx, dg


def rms_bwd(x, dy, dres, g, name):
    def fn(xv, dyv, drv, gv):
        dx, dg = _rms_bwd_math(xv, dyv, gv)
        return dx + drv, dg
    return rw(fn, [(x, 0, D), (dy, 0, D), (dres, 0, D)], [(D, F32)], name, x.shape[0],
              consts=[g], accs=[(1, D)])


def final_loss(x, tgt, g):
    def fn(xv, tv, gv):
        e = xv * _rstd(xv) * gv - tv
        loss = 0.5 * jnp.sum(jnp.sum(e * e, axis=-1, keepdims=True), axis=0, keepdims=True) / D
        dx, dg = _rms_bwd_math(xv, e / D, gv)
        return dx, loss, dg
    return rw(fn, [(x, 0, D), (tgt, 0, D)], [(D, F32)], "final_loss", S, consts=[g],
              accs=[(1, 1), (1, D)])


def _attn_bias(bias_ref):
    ii = lax.broadcasted_iota(jnp.int32, (2 * QB, 2 * QB), 0) % QB
    jj = lax.broadcasted_iota(jnp.int32, (2 * QB, 2 * QB), 1)
    dist = ii + QB - jj
    band = (dist >= 0) & (dist <= QB)
    bias_ref[1] = jnp.where(band, 0.0, NEG)
    bias_ref[0] = jnp.where(band & (jj >= QB), 0.0, NEG)


def _two_heads(x, m0):
    return jnp.concatenate([jnp.where(m0, x, 0.0), jnp.where(m0, 0.0, x)], axis=0)


def _per_head(col, m0):
    return jnp.where(m0, col[:QB], col[QB:])


def _attn_rows(idx, d):
    if d == 1:
        b = idx
        cur = pl.ds(pl.multiple_of(b * QB, QB), QB)
        prev = pl.ds(pl.multiple_of(jnp.maximum(b - 1, 0) * QB, QB), QB)
    else:
        r, b = lax.rem(idx, d), lax.div(idx, d)
        cur = pl.ds(r + b * (QB * d), QB, stride=d)
        prev = pl.ds(r + jnp.maximum(b - 1, 0) * (QB * d), QB, stride=d)
    return cur, prev, b


NBLK = S // QB
GROUP = 16
GROUP_FWD = 16


def _colblk(off):
    return pl.BlockSpec((S, 128), lambda hp: (0, off * 8 + hp))


def attn_fwd(z):
    def body(q_ref, k_ref, v_ref, g_ref, o_ref, l_ref, a_ref, os, ls, bias):
        _attn_bias(bias)
        m0 = lax.broadcasted_iota(jnp.int32, (1, 128), 1) < 64
        for pi, d in enumerate(PATTERNS):
            lone = S // d == QB

            def load(idx, d=d, lone=lone):
                cur, prev, b = _attn_rows(idx, d)
                if lone:
                    return cur, (q_ref[cur, :], None, k_ref[cur, :], None, v_ref[cur, :], bias[1, :, QB:])
                return cur, (q_ref[cur, :], k_ref[prev, :], k_ref[cur, :], v_ref[prev, :], v_ref[cur, :],
                             bias[jnp.minimum(b, 1)])

            def block(q, kp, kc, vp, vc, bs):
                qq = _two_heads(q * 0.125, m0).astype(BF16)
                k = (kc if kp is None else jnp.concatenate([kp, kc], axis=0)).astype(BF16)
                s = _nt(qq, k) + bs
                mx = jnp.max(s, axis=-1, keepdims=True)
                p = jnp.exp(s - mx)
                den = jnp.sum(p, axis=-1, keepdims=True)
                pb = p.astype(BF16)
                vv = _two_heads(vc if vp is None else jnp.concatenate([vp, vc], axis=0), m0).astype(BF16)
                o = _nn(jnp.concatenate([pb[:QB], pb[QB:]], axis=1), vv)
                return o * _per_head(1.0 / den, m0), _per_head(mx + jnp.log(den), m0)

            def step(i, carry, pi=pi):
                loaded = [load(i * GROUP_FWD + u) for u in range(GROUP_FWD)]
                done = [block(*vals) for _, vals in loaded]
                for (cur, _), (o, l) in zip(loaded, done):
                    os[pi, cur, :] = o
                    ls[pi, cur, :] = l
                return carry
            lax.fori_loop(0, NBLK // GROUP_FWD, step, 0)
        l1, l2, l3 = ls[0], ls[1], ls[2]
        mx = jnp.maximum(jnp.maximum(l1, l2), l3)
        e1, e2, e3 = jnp.exp(l1 - mx), jnp.exp(l2 - mx), jnp.exp(l3 - mx)
        tot = e1 + e2 + e3
        o = (os[0] * e1 + os[1] * e2 + os[2] * e3) / tot
        ga = g_ref[...]
        o_ref[...] = o
        l_ref[...] = mx + jnp.log(tot)
        a_ref[...] = (o * (ga * _sig(ga))).astype(a_ref.dtype)

    out = pl.BlockSpec((S, 128), lambda hp: (0, hp))
    return pl.pallas_call(
        body, name="attn_fwd", grid=(8,),
        in_specs=[_colblk(0), _colblk(1), _colblk(2), _colblk(3)], out_specs=[out] * 3,
        out_shape=[jax.ShapeDtypeStruct((S, D), F32), jax.ShapeDtypeStruct((S, D), F32),
                   jax.ShapeDtypeStruct((S, 2 * D), BF16)],
        scratch_shapes=[pltpu.VMEM((3, S, 128), F32), pltpu.VMEM((3, S, 128), F32),
                        pltpu.VMEM((2, 2 * QB, 2 * QB), F32)],
        compiler_params=_cparams(("parallel",)),
    )(z, z, z, z)


def attn_bwd(z, d_cat, o, lse):
    def body(q_ref, k_ref, v_ref, g_ref, da_ref, o_ref, l_ref, dq_ref, dk_ref, dv_ref, dg_ref, do_s, pr_s, bias):
        _attn_bias(bias)
        m0 = lax.broadcasted_iota(jnp.int32, (1, 128), 1) < 64
        ga = g_ref[...]
        sg = _sig(ga)
        da = da_ref[...]
        ov = o_ref[...]
        do = da * (ga * sg)
        dg_ref[...] = da * ov * (sg * (1.0 + ga * (1.0 - sg)))
        do_s[...] = do
        pr_s[...] = do * ov
        dq_ref[...] = jnp.zeros_like(dq_ref)
        dk_ref[...] = jnp.zeros_like(dk_ref)
        dv_ref[...] = jnp.zeros_like(dv_ref)
        for d in PATTERNS:
            lone = S // d == QB

            def load(idx, d=d, lone=lone):
                cur, prev, b = _attn_rows(idx, d)
                if lone:
                    return (cur, None), (q_ref[cur, :], None, k_ref[cur, :], None, v_ref[cur, :],
                                         do_s[cur, :], pr_s[cur, :], l_ref[cur, :], bias[1, :, QB:])
                return (cur, prev), (q_ref[cur, :], k_ref[prev, :], k_ref[cur, :], v_ref[prev, :], v_ref[cur, :],
                                     do_s[cur, :], pr_s[cur, :], l_ref[cur, :], bias[jnp.minimum(b, 1)])

            def block(q, kp, kc, vp, vc, dof, prod, lp, bs):
                qq = _two_heads(q * 0.125, m0).astype(BF16)
                kf = kc if kp is None else jnp.concatenate([kp, kc], axis=0)
                k = kf.astype(BF16)
                v = (vc if vp is None else jnp.concatenate([vp, vc], axis=0)).astype(BF16)
                dd = _two_heads(dof, m0).astype(BF16)
                lh = jnp.max(jnp.concatenate([jnp.where(m0, lp, -jnp.inf), jnp.where(m0, -jnp.inf, lp)], axis=0),
                             axis=-1, keepdims=True)
                delta = jnp.sum(_two_heads(prod, m0), axis=-1, keepdims=True)
                p = jnp.exp(_nt(qq, k) + bs - lh)
                ds = (p * (_nt(dd, v) - delta)).astype(BF16)
                dq = _nn(jnp.concatenate([ds[:QB], ds[QB:]], axis=1), _two_heads(kf, m0).astype(BF16))
                return dq * 0.125, _tn(ds, qq), _tn(p.astype(BF16), dd)

            def step(i, carry):
                loaded = [load(i * GROUP + u) for u in range(GROUP)]
                done = [block(*vals) for _, vals in loaded]
                for ((cur, prev), _), (dq, dk, dv) in zip(loaded, done):
                    dq_ref[cur, :] = dq_ref[cur, :] + dq
                    if prev is not None:
                        dk_ref[prev, :] = dk_ref[prev, :] + dk[:QB]
                        dv_ref[prev, :] = dv_ref[prev, :] + dv[:QB]
                    dk_ref[cur, :] = dk_ref[cur, :] + dk[-QB:]
                    dv_ref[cur, :] = dv_ref[cur, :] + dv[-QB:]
                return carry
            lax.fori_loop(0, NBLK // GROUP, step, 0)

    blk = pl.BlockSpec((S, 128), lambda hp: (0, hp))
    return pl.pallas_call(
        body, name="attn_bwd", grid=(8,),
        in_specs=[_colblk(0), _colblk(1), _colblk(2), _colblk(3), blk, blk, blk], out_specs=[blk] * 4,
        out_shape=[jax.ShapeDtypeStruct((S, D), F32)] * 4,
        scratch_shapes=[pltpu.VMEM((S, 128), F32), pltpu.VMEM((S, 128), F32), pltpu.VMEM((2, 2 * QB, 2 * QB), F32)],
        compiler_params=_cparams(("parallel",)),
    )(z, z, z, z, d_cat, o, lse)


def assemble_dz_even(parts):
    def body(*refs):
        o_ref = refs[-1]
        for j in range(6):
            o_ref[:, j * D:(j + 1) * D] = refs[j][...].astype(o_ref.dtype)
    tr = 512
    blk = pl.BlockSpec((tr, D), lambda i: (i, 0))
    return pl.pallas_call(
        body, name="assemble_dz_even", grid=(S // tr,), in_specs=[blk] * 6,
        out_specs=pl.BlockSpec((tr, 6 * D), lambda i: (i, 0)),
        out_shape=jax.ShapeDtypeStruct((S, 6 * D), BF16),
        compiler_params=_cparams(("parallel",)),
    )(*parts)


def _pool_window(g):
    return jnp.where(g == 0, 2.0, jnp.where(g == 1, 4.0, jnp.where(g == 2, 8.0, 16.0)))


def _pool_sel(g, levels):
    return jnp.where(g == 0, levels[0], jnp.where(g == 1, levels[1], jnp.where(g == 2, levels[2], levels[3])))


def _pool_fwd_math(v, g):
    t = lax.broadcasted_iota(jnp.int32, (S, 1), 0)
    s = v
    levels = []
    for k in (1, 2, 4, 8):
        s = s + jnp.where(t >= k, pltpu.roll(s, k, 0), 0.0)
        levels.append(s)
    cnt = jnp.minimum((t + 1).astype(F32), _pool_window(g))
    return _pool_sel(g, levels) / cnt - v, cnt


def pool_fwd(z, pw, ps, cat):
    def body(v_ref, g_ref, pw_ref, ps_ref, cat_ref, o_ref):
        g = pl.program_id(0)
        pooled, _ = _pool_fwd_math(v_ref[...], g)
        mixed = _nn(pooled.astype(BF16), pw_ref[...].astype(BF16))
        gb = g_ref[...]
        o_ref[...] = (mixed * ps_ref[...] * (gb * _sig(gb))).astype(o_ref.dtype)

    return pl.pallas_call(
        body, name="pool_fwd", grid=(4,),
        in_specs=[pl.BlockSpec((S, 256), lambda g: (0, 16 + g)),
                  pl.BlockSpec((S, 256), lambda g: (0, 20 + g)),
                  pl.BlockSpec((None, 256, 256), lambda g: (g, 0, 0)),
                  pl.BlockSpec((1, 256), lambda g: (0, g)), pl.BlockSpec(memory_space=pl.ANY)],
        out_specs=pl.BlockSpec((S, 256), lambda g: (0, 4 + g)),
        out_shape=jax.ShapeDtypeStruct((S, 2 * D), BF16),
        input_output_aliases={4: 0},
        compiler_params=_cparams(("parallel",)),
    )(z, z, pw, ps, cat)


def pool_bwd(z, d_cat, pw, ps):
    def body(v_ref, g_ref, d_ref, pw_ref, ps_ref, dv_ref, dg_ref, dpw_ref, dps_ref):
        g = pl.program_id(0)
        v = v_ref[...]
        pooled, cnt = _pool_fwd_math(v, g)
        pwb = pw_ref[...].astype(BF16)
        pb = pooled.astype(BF16)
        mixed = _nn(pb, pwb)
        gb = g_ref[...]
        sg = _sig(gb)
        dout = d_ref[...]
        sc = ps_ref[...]
        dg_ref[...] = dout * mixed * sc * (sg * (1.0 + gb * (1.0 - sg)))
        dms = dout * (gb * sg)
        dps_ref[...] = jnp.sum(dms * mixed, axis=0, keepdims=True)
        dmx = (dms * sc).astype(BF16)
        dpw_ref[...] = _tn(pb, dmx)
        dpooled = _nt(dmx, pwb)
        t = lax.broadcasted_iota(jnp.int32, (S, 1), 0)
        s = dpooled / cnt
        levels = []
        for k in (1, 2, 4, 8):
            s = s + jnp.where(t < S - k, pltpu.roll(s, S - k, 0), 0.0)
            levels.append(s)
        dv_ref[...] = _pool_sel(g, levels) - dpooled

    return pl.pallas_call(
        body, name="pool_bwd", grid=(4,),
        in_specs=[pl.BlockSpec((S, 256), lambda g: (0, 16 + g)),
                  pl.BlockSpec((S, 256), lambda g: (0, 20 + g)),
                  pl.BlockSpec((S, 256), lambda g: (0, 4 + g)),
                  pl.BlockSpec((None, 256, 256), lambda g: (g, 0, 0)),
                  pl.BlockSpec((1, 256), lambda g: (0, g))],
        out_specs=[pl.BlockSpec((S, 256), lambda g: (0, g)),
                   pl.BlockSpec((S, 256), lambda g: (0, g)),
                   pl.BlockSpec((None, 256, 256), lambda g: (g, 0, 0)),
                   pl.BlockSpec((1, 256), lambda g: (0, g))],
        out_shape=[jax.ShapeDtypeStruct((S, D), F32), jax.ShapeDtypeStruct((S, D), F32),
                   jax.ShapeDtypeStruct((4, 256, 256), F32), jax.ShapeDtypeStruct((1, D), F32)],
        compiler_params=_cparams(("parallel",)),
    )(z, z, d_cat, pw, ps)


CH = 128


def _sgu_common(v, lng, lnb, w_ref):
    mu = jnp.mean(v, axis=-1, keepdims=True)
    vc = v - mu
    rs = lax.rsqrt(jnp.mean(vc * vc, axis=-1, keepdims=True) + EPS)
    xhat = vc * rs
    vn = (xhat * lng + lnb).astype(BF16)
    ri = lax.broadcasted_iota(jnp.int32, (CH, CH), 0)
    ci = lax.broadcasted_iota(jnp.int32, (CH, CH), 1)
    tril = ri >= ci
    ws = [jnp.where(tril, w_ref[g], 0.0).astype(BF16) for g in range(4)]
    return xhat, rs, vn, tril, ws


def _zspec(off):
    return pl.BlockSpec((CH, D), lambda c: (c, off))


def _full(shape):
    return pl.BlockSpec(shape, lambda c: (0,) * len(shape))


def sgu_fwd(z, lng, lnb, w, bfull):
    def body(u_ref, v_ref, g_ref, lng_ref, lnb_ref, w_ref, b_ref, o_ref):
        _, _, vn, _, ws = _sgu_common(v_ref[...], lng_ref[...], lnb_ref[...], w_ref)
        for g in range(4):
            sl = slice(g * 256, (g + 1) * 256)
            mixed = _nn(ws[g], vn[:, sl]) + b_ref[:, sl]
            gc = g_ref[:, sl]
            o_ref[:, sl] = (u_ref[:, sl] * mixed * (gc * _sig(gc))).astype(o_ref.dtype)

    return pl.pallas_call(
        body, name="sgu_fwd", grid=(S // CH,),
        in_specs=[_zspec(0), _zspec(1), _zspec(2), _full((1, D)), _full((1, D)),
                  _full((4, CH, CH)), _full((CH, D))],
        out_specs=pl.BlockSpec((CH, D), lambda c: (c, 0)),
        out_shape=jax.ShapeDtypeStruct((S, D), BF16),
        compiler_params=_cparams(("parallel",)),
    )(z, z, z, lng, lnb, w, bfull)


def sgu_bwd(z, d_cat, lng, lnb, w, bfull):
    def body(u_ref, v_ref, g_ref, d_ref, lng_ref, lnb_ref, w_ref, b_ref,
             du_ref, dv_ref, dg_ref, dw_ref, db_ref, dlg_ref, dlb_ref):
        @pl.when(pl.program_id(0) == 0)
        def _():
            dw_ref[...] = jnp.zeros_like(dw_ref)
            db_ref[...] = jnp.zeros_like(db_ref)
            dlg_ref[...] = jnp.zeros_like(dlg_ref)
            dlb_ref[...] = jnp.zeros_like(dlb_ref)

        lng = lng_ref[...]
        xhat, rs, vn, tril, ws = _sgu_common(v_ref[...], lng, lnb_ref[...], w_ref)
        lane = lax.broadcasted_iota(jnp.int32, (1, 128), 1)
        db = jnp.zeros((CH, 128), F32)
        dvn_parts = []
        for g in range(4):
            sl = slice(g * 256, (g + 1) * 256)
            mixed = _nn(ws[g], vn[:, sl]) + b_ref[:, sl]
            gc = g_ref[:, sl]
            sg = _sig(gc)
            u = u_ref[:, sl]
            dc = d_ref[:, sl]
            du_ref[:, sl] = dc * mixed * (gc * sg)
            dg_ref[:, sl] = dc * u * mixed * (sg * (1.0 + gc * (1.0 - sg)))
            dmx = dc * u * (gc * sg)
            db = db + jnp.where(lane == g, jnp.sum(dmx, axis=-1, keepdims=True), 0.0)
            dmb = dmx.astype(BF16)
            dw_ref[g] += jnp.where(tril, _nt(dmb, vn[:, sl]), 0.0)
            dvn_parts.append(_tn(ws[g], dmb))
        db_ref[...] += db
        dvn = jnp.concatenate(dvn_parts, axis=1)
        dlb_ref[...] += jnp.sum(dvn, axis=0, keepdims=True)
        dlg_ref[...] += jnp.sum(dvn * xhat, axis=0, keepdims=True)
        dxh = dvn * lng
        dv_ref[...] = rs * (dxh - jnp.mean(dxh, axis=-1, keepdims=True)
                            - xhat * jnp.mean(dxh * xhat, axis=-1, keepdims=True))

    row = pl.BlockSpec((CH, D), lambda c: (c, 0))
    return pl.pallas_call(
        body, name="sgu_bwd", grid=(S // CH,),
        in_specs=[_zspec(0), _zspec(1), _zspec(2), row, _full((1, D)), _full((1, D)),
                  _full((4, CH, CH)), _full((CH, D))],
        out_specs=[row, row, row, _full((4, CH, CH)), _full((CH, 128)), _full((1, D)), _full((1, D))],
        out_shape=[jax.ShapeDtypeStruct((S, D), F32)] * 3
        + [jax.ShapeDtypeStruct((4, CH, CH), F32), jax.ShapeDtypeStruct((CH, 128), F32),
           jax.ShapeDtypeStruct((1, D), F32), jax.ShapeDtypeStruct((1, D), F32)],
        compiler_params=_cparams(("arbitrary",)),
    )(z, z, z, d_cat, lng, lnb, w, bfull)


TB = 256


def _cmul(ar, ai, br, bi):
    return ar * br - ai * bi, ar * bi + ai * br


def _scan_consts(ar, ai, reverse):
    a2 = _cmul(ar, ai, ar, ai)
    a4 = _cmul(*a2, *a2)
    row = lax.broadcasted_iota(jnp.int32, (8, NS), 0)

    def masked(k, p):
        keep = (row < 8 - k) if reverse else (row >= k)
        return jnp.where(keep, p[0], 0.0), jnp.where(keep, p[1], 0.0)
    pr = jnp.zeros((8, NS), F32)
    pi = jnp.zeros((8, NS), F32)
    cr, ci = ar, ai
    for r in range(8):
        sel = row == (7 - r if reverse else r)
        pr = jnp.where(sel, cr, pr)
        pi = jnp.where(sel, ci, pi)
        cr, ci = _cmul(cr, ci, ar, ai)
    return (masked(1, (ar, ai)), masked(2, a2), masked(4, a4)), (pr, pi), row


def scan_fwd(bu, abr, abi):
    def body(bu_ref, ar_ref, ai_ref, h_ref, car, cai):
        @pl.when(pl.program_id(0) == 0)
        def _():
            car[...] = jnp.zeros_like(car)
            cai[...] = jnp.zeros_like(cai)

        pows, (pr, pi), row = _scan_consts(ar_ref[...], ai_ref[...], False)

        def tile(t, carry):
            c_r, c_i = carry
            rows = pl.ds(pl.multiple_of(t * 8, 8), 8)
            xr = bu_ref[rows, 0:NS]
            xi = bu_ref[rows, NS:2 * NS]
            for k, (kr, ki) in zip((1, 2, 4), pows):
                sr = pltpu.roll(xr, k, 0)
                si = pltpu.roll(xi, k, 0)
                xr, xi = xr + kr * sr - ki * si, xi + kr * si + ki * sr
            xr, xi = xr + pr * c_r - pi * c_i, xi + pr * c_i + pi * c_r
            h_ref[rows, 0:NS] = xr
            h_ref[rows, NS:2 * NS] = xi
            return (jnp.broadcast_to(xr[7:8, :], (8, NS)), jnp.broadcast_to(xi[7:8, :], (8, NS)))

        c_r, c_i = lax.fori_loop(0, TB // 8, tile, (car[...], cai[...]))
        car[...] = c_r
        cai[...] = c_i

    return pl.pallas_call(
        body, name="s5_scan_fwd", grid=(S // TB,),
        in_specs=[pl.BlockSpec((TB, 2 * NS), lambda i: (i, 0)),
                  pl.BlockSpec((1, NS), lambda i: (0, 0)), pl.BlockSpec((1, NS), lambda i: (0, 0))],
        out_specs=pl.BlockSpec((TB, 2 * NS), lambda i: (i, 0)),
        out_shape=jax.ShapeDtypeStruct((S, 2 * NS), F32),
        scratch_shapes=[pltpu.VMEM((8, NS), F32), pltpu.VMEM((8, NS), F32)],
        compiler_params=_cparams(("arbitrary",)),
    )(bu, abr, abi)


def scan_bwd(eta, h, abr, abi):
    nt = S // TB

    def body(e_ref, h_ref, ar_ref, ai_ref, l_ref, da_ref, car, cai):
        @pl.when(pl.program_id(0) == 0)
        def _():
            car[...] = jnp.zeros_like(car)
            cai[...] = jnp.zeros_like(cai)
            da_ref[...] = jnp.zeros_like(da_ref)

        pows, (pr, pi), row = _scan_consts(ar_ref[...], -ai_ref[...], True)

        def tile(tt, carry):
            c_r, c_i, acr, aci = carry
            t = TB // 8 - 1 - tt
            rows = pl.ds(pl.multiple_of(t * 8, 8), 8)
            xr = e_ref[rows, 0:NS]
            xi = e_ref[rows, NS:2 * NS]
            for k, (kr, ki) in zip((1, 2, 4), pows):
                sr = pltpu.roll(xr, 8 - k, 0)
                si = pltpu.roll(xi, 8 - k, 0)
                xr, xi = xr + kr * sr - ki * si, xi + kr * si + ki * sr
            xr, xi = xr + pr * c_r - pi * c_i, xi + pr * c_i + pi * c_r
            l_ref[rows, 0:NS] = xr
            l_ref[rows, NS:2 * NS] = xi
            nr = jnp.where(row < 7, pltpu.roll(xr, 7, 0), c_r)
            ni = jnp.where(row < 7, pltpu.roll(xi, 7, 0), c_i)
            hr = h_ref[rows, 0:NS]
            hi = h_ref[rows, NS:2 * NS]
            acr = acr + hr * nr + hi * ni
            aci = aci + hr * ni - hi * nr
            return (jnp.broadcast_to(xr[0:1, :], (8, NS)), jnp.broadcast_to(xi[0:1, :], (8, NS)), acr, aci)

        zero = jnp.zeros((8, NS), F32)
        c_r, c_i, acr, aci = lax.fori_loop(0, TB // 8, tile, (car[...], cai[...], zero, zero))
        car[...] = c_r
        cai[...] = c_i
        da_ref[:, 0:NS] += acr
        da_ref[:, NS:2 * NS] += aci

    rev = pl.BlockSpec((TB, 2 * NS), lambda i: (nt - 1 - i, 0))
    return pl.pallas_call(
        body, name="s5_scan_bwd", grid=(nt,),
        in_specs=[rev, rev, pl.BlockSpec((1, NS), lambda i: (0, 0)), pl.BlockSpec((1, NS), lambda i: (0, 0))],
        out_specs=[rev, pl.BlockSpec((8, 2 * NS), lambda i: (0, 0))],
        out_shape=[jax.ShapeDtypeStruct((S, 2 * NS), F32), jax.ShapeDtypeStruct((8, 2 * NS), F32)],
        scratch_shapes=[pltpu.VMEM((8, NS), F32), pltpu.VMEM((8, NS), F32)],
        compiler_params=_cparams(("arbitrary",)),
    )(eta, h, abr, abi)


GC = 0.7978845608028654
GA = 0.044715


def s5_post(hc, z, dskip):
    def fn(hv, xd, dv):
        y = hv + dv * xd
        return y, 0.5 * y * (1.0 + jnp.tanh(GC * (y + GA * y * y * y)))
    return rw(fn, [(hc, 0, 512), (z, 3072, 512)], [(512, F32), (512, BF16)], "s5_post", S, consts=[dskip])


def s5_post_bwd(dyg, ypre, z, dskip):
    def fn(dy, y, xd, dv):
        th = jnp.tanh(GC * (y + GA * y * y * y))
        dg = 0.5 * (1.0 + th) + 0.5 * y * (1.0 - th * th) * GC * (1.0 + 3.0 * GA * y * y)
        dyp = dy * dg
        return dyp, dyp * dv, jnp.sum(dyp * xd, axis=0, keepdims=True)
    return rw(fn, [(dyg, 0, 512), (ypre, 0, 512), (z, 3072, 512)], [(512, BF16), (512, F32)],
              "s5_post_bwd", S, consts=[dskip], accs=[(1, 512)])


def glu_fwd(t, z, c_out):
    def fn(t1, t2, gd, co):
        return (jnp.concatenate([co, (t1 * _sig(t2) * (gd * _sig(gd))).astype(BF16)], axis=1),)
    return rw(fn, [(t, 0, 512), (t, 512, 512), (z, 3584, 512), (c_out, 0, D)], [(D + 512, BF16)], "glu_fwd", S)[0]


def glu_bwd(t, z, d_cat):
    def fn(t1, t2, gd, dd):
        s2, sg = _sig(t2), _sig(gd)
        sl = gd * sg
        return (jnp.concatenate([dd * s2 * sl, dd * t1 * s2 * (1.0 - s2) * sl], axis=1),
                dd * t1 * s2 * (sg * (1.0 + gd * (1.0 - sg))))
    return rw(fn, [(t, 0, 512), (t, 512, 512), (z, 3584, 512), (d_cat, 1024, 512)],
              [(D, BF16), (512, F32)], "glu_bwd", S)


def assemble_dz_odd(du, dv, dgc, dxd, dgd):
    def body(a, b, c, d, e, o_ref):
        o_ref[:, 0:D] = a[...].astype(BF16)
        o_ref[:, D:2 * D] = b[...].astype(BF16)
        o_ref[:, 2 * D:3 * D] = c[...].astype(BF16)
        o_ref[:, 3 * D:3 * D + 512] = d[...].astype(BF16)
        o_ref[:, 3 * D + 512:4 * D] = e[...].astype(BF16)
    tr = 512
    blk = pl.BlockSpec((tr, D), lambda i: (i, 0))
    half = pl.BlockSpec((tr, 512), lambda i: (i, 0))
    return pl.pallas_call(
        body, name="assemble_dz_odd", grid=(S // tr,), in_specs=[blk, blk, blk, half, half],
        out_specs=pl.BlockSpec((tr, 4 * D), lambda i: (i, 0)),
        out_shape=jax.ShapeDtypeStruct((S, 4 * D), BF16),
        compiler_params=_cparams(("parallel",)),
    )(du, dv, dgc, dxd, dgd)


TQ = 1024


def _xattn_probs(qh, kh):
    s = _nt(qh, kh) * 0.0625
    p = jnp.exp(s - jnp.max(s, axis=-1, keepdims=True))
    return p / jnp.sum(p, axis=-1, keepdims=True)


def xattn_fwd(q, kv):
    def body(q_ref, kv_ref, o_ref):
        outs = []
        for h in range(4):
            sl = slice(h * 256, (h + 1) * 256)
            p = _xattn_probs(q_ref[:, sl].astype(BF16), kv_ref[:, sl].astype(BF16))
            vh = kv_ref[:, D + h * 256:D + (h + 1) * 256].astype(BF16)
            outs.append((sl, _nn(p.astype(BF16), vh)))
        for sl, o in outs:
            o_ref[:, sl] = o.astype(o_ref.dtype)

    return pl.pallas_call(
        body, name="xattn_fwd", grid=(S // TQ,),
        in_specs=[pl.BlockSpec((TQ, D), lambda i: (i, 0)), pl.BlockSpec((MEM, 2 * D), lambda i: (0, 0))],
        out_specs=pl.BlockSpec((TQ, D), lambda i: (i, 0)),
        out_shape=jax.ShapeDtypeStruct((S, D), BF16),
        compiler_params=_cparams(("parallel",)),
    )(q, kv)


def xattn_bwd(q, kv, d_o):
    def body(q_ref, kv_ref, do_ref, dq_ref, dkv_ref):
        @pl.when(pl.program_id(0) == 0)
        def _():
            dkv_ref[...] = jnp.zeros_like(dkv_ref)

        done = []
        for h in range(4):
            sl = slice(h * 256, (h + 1) * 256)
            vs = slice(D + h * 256, D + (h + 1) * 256)
            qh = q_ref[:, sl].astype(BF16)
            kh = kv_ref[:, sl].astype(BF16)
            vh = kv_ref[:, vs].astype(BF16)
            doh = do_ref[:, sl].astype(BF16)
            p = _xattn_probs(qh, kh)
            dp = _nt(doh, vh)
            ds = (p * (dp - jnp.sum(p * dp, axis=-1, keepdims=True)) * 0.0625).astype(BF16)
            done.append((sl, vs, _nn(ds, kh), _tn(ds, qh), _tn(p.astype(BF16), doh)))
        for sl, vs, dq, dk, dv in done:
            dq_ref[:, sl] = dq.astype(dq_ref.dtype)
            dkv_ref[:, sl] += dk
            dkv_ref[:, vs] += dv

    return pl.pallas_call(
        body, name="xattn_bwd", grid=(S // TQ,),
        in_specs=[pl.BlockSpec((TQ, D), lambda i: (i, 0)), pl.BlockSpec((MEM, 2 * D), lambda i: (0, 0)),
                  pl.BlockSpec((TQ, D), lambda i: (i, 0))],
        out_specs=[pl.BlockSpec((TQ, D), lambda i: (i, 0)), pl.BlockSpec((MEM, 2 * D), lambda i: (0, 0))],
        out_shape=[jax.ShapeDtypeStruct((S, D), BF16), jax.ShapeDtypeStruct((MEM, 2 * D), F32)],
        compiler_params=_cparams(("arbitrary",)),
    )(q, kv, d_o)


def _s5_disc(a_re, a_im, log_dt, b_re, b_im):
    dt = jnp.exp(log_dt)[:, None]
    mag = jnp.exp(dt * a_re)
    abr = mag * jnp.cos(dt * a_im)
    abi = mag * jnp.sin(dt * a_im)
    nr, ni = abr - 1.0, abi
    inv = 1.0 / (a_re * a_re + a_im * a_im)
    cr = (nr * a_re + ni * a_im) * inv
    ci = (ni * a_re - nr * a_im) * inv
    bbr = cr[..., None] * b_re - ci[..., None] * b_im
    bbi = cr[..., None] * b_im + ci[..., None] * b_re
    return abr, abi, bbr, bbi


VM = pl.BlockSpec(memory_space=pltpu.VMEM)


def s5_embed(bt_re, bt_im, ct_re, ct_im):
    def body(br, bi, cr, ci, b_ref, c_ref):
        b_ref[...] = jnp.zeros_like(b_ref)
        c_ref[...] = jnp.zeros_like(c_ref)
        for g in range(NG):
            rows, cols = slice(g * NH, (g + 1) * NH), slice(g * NP, (g + 1) * NP)
            b_ref[rows, cols] = br[g]
            b_ref[rows, NS + g * NP:NS + (g + 1) * NP] = bi[g]
            c_ref[cols, rows] = cr[g]
            c_ref[NS + g * NP:NS + (g + 1) * NP, rows] = -ci[g]

    return pl.pallas_call(
        body, name="s5_embed", in_specs=[VM] * 4, out_specs=[VM] * 2,
        out_shape=[jax.ShapeDtypeStruct((NG * NH, 2 * NS), F32), jax.ShapeDtypeStruct((2 * NS, NG * NH), F32)],
        compiler_params=pltpu.CompilerParams(vmem_limit_bytes=VMEM_LIMIT),
    )(bt_re, bt_im, ct_re, ct_im)


def s5_extract(gb, gc):
    def body(gb_ref, gc_ref, br, bi, cr, ci):
        for g in range(NG):
            rows, cols = slice(g * NH, (g + 1) * NH), slice(g * NP, (g + 1) * NP)
            br[g] = gb_ref[rows, cols]
            bi[g] = gb_ref[rows, NS + g * NP:NS + (g + 1) * NP]
            cr[g] = gc_ref[cols, rows]
            ci[g] = -gc_ref[NS + g * NP:NS + (g + 1) * NP, rows]

    return pl.pallas_call(
        body, name="s5_extract", in_specs=[VM] * 2, out_specs=[VM] * 4,
        out_shape=[jax.ShapeDtypeStruct((NG, NH, NP), F32)] * 2 + [jax.ShapeDtypeStruct((NG, NP, NH), F32)] * 2,
        compiler_params=pltpu.CompilerParams(vmem_limit_bytes=VMEM_LIMIT),
    )(gb, gc)


HC, HS = NG * NH // 2, NS // 2
TS = 1024


def s5_to_states(x, w, mode, name, z_off=0):
    if mode == "nn":
        wb, wm = (HC, HS), lambda i, j, kk: (j % 2, j)
    else:
        wb, wm = (HS, HC), lambda i, j, kk: (j, j % 2)
    return mm_band(x, w, mode, name, (S // TS, 4, 1), ((TS, HC), wb, (TS, HS)),
                   (lambda i, j, kk: (i, z_off + j % 2), wm, lambda i, j, kk: (i, j)), (S, 2 * NS))


def s5_to_channels(x, w, mode, name, add=None):
    if mode == "nn":
        wb, wm = (HS, HC), lambda i, j, kk: (j + 2 * kk, j)
    else:
        wb, wm = (HC, HS), lambda i, j, kk: (j, j + 2 * kk)
    return mm_band(x, w, mode, name, (S // TS, 2, 2), ((TS, HS), wb, (TS, HC)),
                   (lambda i, j, kk: (i, j + 2 * kk), wm, lambda i, j, kk: (i, j)), (S, NG * NH), add=add)


def s5_outer(a, b, name, states_first, z_off=0):
    if states_first:
        return mm_band(a, b, "tn", name, (4, 1, 1), ((S, HS), (S, HC), (HS, HC)),
                       (lambda i, j, kk: (0, i), lambda i, j, kk: (0, i % 2), lambda i, j, kk: (i, i % 2)),
                       (2 * NS, NG * NH))
    return mm_band(a, b, "tn", name, (1, 4, 1), ((S, HC), (S, HS), (HC, HS)),
                   (lambda i, j, kk: (0, z_off + j % 2), lambda i, j, kk: (0, j), lambda i, j, kk: (j % 2, j)),
                   (NG * NH, 2 * NS))


def _fwd_even(i, x, P, W):
    hn = rms_fwd(x, P["norm_ab"][i:i + 1], "rms_ab_fwd")
    z = mm(m2(hn), W["w_in"], "nn", "in_ab")
    o, lse, cat = attn_fwd(z)
    if "more" in W:
        W.update(W.pop("more")(cat))
    cat = pool_fwd(z, W["pool_w"], P["pool_scale"][i:i + 1], cat)
    x_mid = mm(m2(cat), W["w_out"], "nn", "out_ab", add=m2(x))
    return x_mid, dict(x=x, hn=hn, z=z, o=o, lse=lse, cat=cat)


def _bwd_even(i, dx_mid, sv, P, W, G, GW):
    z = sv["z"]
    d_cat = mm(m2(dx_mid), W["w_out"], "nt", "out_ab_dx")
    GW["w_out"] = mm(m2(sv["cat"]), m2(dx_mid), "tn", "out_ab_dw").reshape(4, 512, D)
    dq, dk, dv, dga = attn_bwd(z, d_cat, sv["o"], sv["lse"])
    dvb, dgb, dpw, dps = pool_bwd(z, d_cat, W["pool_w"], P["pool_scale"][i:i + 1])
    GW["pool_w"] = dpw.reshape(4, 4, 64, 256).transpose(1, 0, 2, 3).reshape(4, 256, 256)
    G["pool_scale"][i] = dps[0]
    d_z = assemble_dz_even((dq, dk, dv, dga, dvb, dgb))
    d_hn = mm(m2(d_z), W["w_in"], "nt", "in_ab_dx")
    GW["w_in"] = mm(m2(sv["hn"]), m2(d_z), "tn", "in_ab_dw", out=outcs(D, 1536))
    return d_hn, P["norm_ab"][i:i + 1], "norm_ab", "rms_ab_bwd"


def _fwd_odd(i, x, P, W):
    hn = rms_fwd(x, P["norm_cd"][i:i + 1], "rms_cd_fwd")
    z = mm(m2(hn), W["w_in"], "nn", "in_cd")
    bfull = jnp.repeat(P["sgu_b"][i].T, 256, axis=1)
    c_out = sgu_fwd(z, P["sgu_ln_g"][i:i + 1], P["sgu_ln_b"][i:i + 1], P["sgu_w"][i], bfull)
    disc, disc_vjp = jax.vjp(_s5_disc, P["s5_a_re"][i], P["s5_a_im"][i], P["s5_log_dt"][i],
                             P["s5_b_re"][i], P["s5_b_im"][i])
    abr, abi, bbr, bbi = disc
    bbd, cbd = s5_embed(bbr.transpose(0, 2, 1), bbi.transpose(0, 2, 1),
                        P["s5_c_re"][i].transpose(0, 2, 1), P["s5_c_im"][i].transpose(0, 2, 1))
    abr, abi = abr.reshape(1, NS), abi.reshape(1, NS)
    bu = s5_to_states(z, bbd, "nn", "s5_bu", z_off=3072 // HC)
    h = scan_fwd(bu, abr, abi)
    hc = s5_to_channels(h, cbd, "nn", "s5_hc")
    dskip = P["s5_d"][i:i + 1]
    ypre, yg = s5_post(hc, z, dskip)
    if "more" in W:
        W.update(W.pop("more")(yg))
    w12 = W["w12"]
    t = mm(m2(yg), m2(w12), "nn", "glu_t")
    cat = glu_fwd(t, z, c_out)
    x_mid = mm(m2(cat), W["w_out"], "nn", "out_cd", add=m2(x))
    return x_mid, dict(x=x, hn=hn, z=z, bfull=bfull, disc_vjp=disc_vjp, bbd=bbd, cbd=cbd, abr=abr,
                       abi=abi, h=h, ypre=ypre, yg=yg, w12=w12, t=t, cat=cat, dskip=dskip)


def _bwd_odd(i, dx_mid, sv, P, W, G, GW):
    z = sv["z"]
    d_cat = mm(m2(dx_mid), W["w_out"], "nt", "out_cd_dx")
    GW["w_out"] = mm(m2(sv["cat"]), m2(dx_mid), "tn", "out_cd_dw").reshape(4, 384, D)
    du, dv, dgc, dws, dbs, dlg, dlb = sgu_bwd(z, d_cat, P["sgu_ln_g"][i:i + 1], P["sgu_ln_b"][i:i + 1],
                                               P["sgu_w"][i], sv["bfull"])
    G["sgu_w"][i], G["sgu_b"][i] = dws, dbs[:, :4].T
    G["sgu_ln_g"][i], G["sgu_ln_b"][i] = dlg[0], dlb[0]
    dt, dgd = glu_bwd(sv["t"], z, d_cat)
    gw12 = mm(m2(sv["yg"]), m2(dt), "tn", "glu_dw")
    GW["glu_w1"] = gw12[:, :512].reshape(4, 128, 512)
    GW["glu_w2"] = gw12[:, 512:].reshape(4, 128, 512)
    dyg = mm(m2(dt), m2(sv["w12"]), "nt", "glu_dx")
    dypre, dxd1, dd = s5_post_bwd(dyg, sv["ypre"], z, sv["dskip"])
    G["s5_d"][i] = dd[0]
    gcbd = s5_outer(sv["h"], dypre, "s5_dc", states_first=True)
    eta = s5_to_states(dypre, sv["cbd"], "nt", "s5_eta")
    lam, dacc = scan_bwd(eta, sv["h"], sv["abr"], sv["abi"])
    gbbd = s5_outer(z, lam, "s5_db", states_first=False, z_off=3072 // HC)
    dxd = s5_to_channels(lam, sv["bbd"], "nt", "s5_dx", add=dxd1)
    dacc = jnp.sum(dacc, axis=0)
    dbt_re, dbt_im, dct_re, dct_im = s5_extract(gbbd, gcbd)
    G["s5_c_re"][i], G["s5_c_im"][i] = dct_re.transpose(0, 2, 1), dct_im.transpose(0, 2, 1)
    d_bbr, d_bbi = dbt_re.transpose(0, 2, 1), dbt_im.transpose(0, 2, 1)
    (G["s5_a_re"][i], G["s5_a_im"][i], G["s5_log_dt"][i], G["s5_b_re"][i], G["s5_b_im"][i]) = sv["disc_vjp"](
        (dacc[:NS].reshape(NG, NP), dacc[NS:].reshape(NG, NP), d_bbr, d_bbi))
    d_z = assemble_dz_odd(du, dv, dgc, dxd, dgd)
    d_hn = mm(m2(d_z), W["w_in"], "nt", "in_cd_dx")
    GW["w_in"] = mm(m2(sv["hn"]), m2(d_z), "tn", "in_cd_dw", out=outcs(D, 1024))
    return d_hn, P["norm_cd"][i:i + 1], "norm_cd", "rms_cd_bwd"


def _fwd_x(l, x, mem_n, P, W):
    hx = rms_fwd(x, P["norm_x"][l:l + 1], "rms_x_fwd")
    q = mm(m2(hx), W["w_xq"], "nn", "xq", out_dtype=BF16)
    kv = mm(m2(mem_n), W["w_xkv"], "nn", "xkv", out_dtype=BF16)
    ox = xattn_fwd(q, kv)
    x_out = mm(m2(ox), W["w_xo"], "nn", "xo", add=m2(x))
    return x_out, dict(x=x, hx=hx, q=q, kv=kv, ox=ox)


def _bwd_x(l, dx_out, sv, mem_n, d_memn, P, W, G, GW):
    d_ox = mm(m2(dx_out), W["w_xo"], "nt", "xo_dx", out_dtype=BF16)
    GW["w_xo"] = mm(m2(sv["ox"]), m2(dx_out), "tn", "xo_dw").reshape(4, 256, D)
    dq, dkv = xattn_bwd(sv["q"], sv["kv"], d_ox)
    GW["w_xq"] = mm(m2(sv["hx"]), m2(dq), "tn", "xq_dw").reshape(4, 256, D)
    d_hx = mm(m2(dq), W["w_xq"], "nt", "xq_dx")
    GW["w_xkv"] = mm(m2(mem_n), m2(dkv), "tn", "xkv_dw", out=outcs(D, 512))
    d_memn = mm(m2(dkv), W["w_xkv"], "nt", "xkv_dx", add=None if d_memn is None else m2(d_memn))
    dx, dg = rms_bwd(sv["x"], d_hx, dx_out, P["norm_x"][l:l + 1], "rms_x_bwd")
    G["norm_x"][l] = dg[0]
    return dx, d_memn


SMALL_LAYERS = (("norm_ab", 2), ("pool_scale", 2), ("norm_cd", 2), ("sgu_ln_g", 2), ("sgu_ln_b", 2), ("sgu_w", 2),
                ("sgu_b", 2), ("s5_a_re", 2), ("s5_a_im", 2), ("s5_log_dt", 2), ("s5_b_re", 2), ("s5_b_im", 2),
                ("s5_c_re", 2), ("s5_c_im", 2), ("s5_d", 2), ("norm_x", 4))


def local_step(x, mem, tgt, P, weights_of, grads_done):
    G = {k: [None] * n for k, n in SMALL_LAYERS}
    mem_g = P["mem_norm"].reshape(1, D)
    mem_n = rms_fwd(mem, mem_g, "rms_mem_fwd")
    saved = []
    for layer in range(4):
        i = layer // 2
        W = weights_of(layer, x)
        x, sv_m = (_fwd_even if layer % 2 == 0 else _fwd_odd)(i, x, P, W)
        x, sv_x = _fwd_x(layer, x, mem_n, P, W)
        saved.append((sv_m, sv_x, W))
    dx, loss, dgf = final_loss(x, tgt, P["final_norm"].reshape(1, D))
    G["final_norm"] = dgf[0]
    d_memn = None
    for layer in reversed(range(4)):
        i = layer // 2
        sv_m, sv_x, W = saved[layer]
        GW = {}
        dx_mid, d_memn = _bwd_x(layer, dx, sv_x, mem_n, d_memn, P, W, G, GW)
        d_hn, g, key, name = (_bwd_even if layer % 2 == 0 else _bwd_odd)(i, dx_mid, sv_m, P, W, G, GW)
        token = grads_done(layer, GW)
        if token is not None:
            g = g + token
        dx, dg = rms_bwd(sv_m["x"], d_hn, dx_mid, g, name)
        G[key][i] = dg[0]
    _, dgm = rms_bwd(mem, d_memn, d_memn, mem_g, "rms_mem_bwd")
    G["mem_norm"] = dgm[0]
    return loss, dx, G


ANY = pl.BlockSpec(memory_space=pl.ANY)


def _place():
    x, y, c = lax.axis_index("x"), lax.axis_index("y"), lax.axis_index("c")
    chips = [(1 - x, y), (x, 1 - y), (1 - x, 1 - y)]
    return x, y, c, 2 * x + y, (x, y, 1 - c), chips


def _remote(src, dst, send, recv, k, dev):
    return pltpu.make_async_remote_copy(src_ref=src, dst_ref=dst, send_sem=send.at[k], recv_sem=recv.at[k],
                                        device_id=dev, device_id_type=MESHID)


HBM = pl.BlockSpec(memory_space=pltpu.HBM)
SEM = pl.BlockSpec(memory_space=pltpu.SEMAPHORE)
EFFECT = pltpu.SideEffectType.DATAFLOW_SIDE_EFFECTING


def _hbm(t):
    return pltpu.with_memory_space_constraint(t, pltpu.HBM)


def allgather_sync(shards):
    n = len(shards)

    def body(*refs):
        ins, outs = refs[:n], refs[n:2 * n]
        token, send, recv = refs[2 * n:]
        x, y, c, jme, sib, chips = _place()
        first, passed = [], []
        for a in range(n):
            cp = _remote(ins[a], outs[a].at[jme], send, recv, a * 7 + 6, sib)
            cp.start()
            first.append(cp)
            for k, chip in enumerate(chips):
                cp = _remote(ins[a].at[c], outs[a].at[jme, c], send, recv, a * 7 + k, (*chip, c))
                cp.start()
                first.append(cp)
        for a in range(n):
            for k, chip in enumerate(chips):
                piece = outs[a].at[2 * chip[0] + chip[1], c]
                _remote(piece, piece, send, recv, a * 7 + k, (*chip, c)).wait_recv()
                fw = _remote(piece, piece, send, recv, a * 7 + 3 + k, sib)
                fw.start()
                passed.append(fw)
        for a in range(n):
            own = outs[a].at[jme]
            _remote(own, own, send, recv, a * 7 + 6, sib).wait_recv()
            for k, chip in enumerate(chips):
                piece = outs[a].at[2 * chip[0] + chip[1], 1 - c]
                _remote(piece, piece, send, recv, a * 7 + 3 + k, sib).wait_recv()
        for cp in first + passed:
            cp.wait_send()
        token[...] = jnp.zeros_like(token)

    res = pl.pallas_call(
        body, name="allgather_sync", in_specs=[ANY] * n,
        out_specs=[ANY] * n + [pl.BlockSpec(memory_space=pltpu.VMEM)],
        out_shape=[jax.ShapeDtypeStruct((4,) + s.shape, s.dtype) for s in shards] + [jax.ShapeDtypeStruct((8, 128), F32)],
        scratch_shapes=[pltpu.SemaphoreType.DMA((7 * n,)), pltpu.SemaphoreType.DMA((7 * n,))],
    )(*shards)
    return list(res[:n]), res[n]


def _gather_copies(ins, lands, send, recv):
    x, y, c, jme, sib, chips = _place()
    devs = [(*chip, c) for chip in chips] + [sib]
    return [_remote(ins[a], lands[a].at[jme], send, recv, a * 4 + k, dev)
            for a in range(len(ins)) for k, dev in enumerate(devs)]


def allgather_start(shards, after, name):
    n, na = len(shards), len(after)

    def body(*refs):
        ins, lands = refs[:n], refs[n:2 * n]
        send, recv = refs[2 * n + na], refs[2 * n + na + 1]
        token = refs[-1]
        for cp in _gather_copies(ins, lands, send, recv):
            cp.start()
        token[...] = jnp.zeros_like(token)

    res = pl.pallas_call(
        body, name=name,
        out_shape=(pltpu.SemaphoreType.DMA((4 * n,)), pltpu.SemaphoreType.DMA((4 * n,)),
                   *[pltpu.HBM(s.shape, s.dtype) for s in shards],
                   *[pltpu.HBM((4,) + s.shape, s.dtype) for s in shards],
                   jax.ShapeDtypeStruct((8, 128), F32)),
        in_specs=[HBM] * (2 * n) + [ANY] * na,
        out_specs=(SEM, SEM, *[HBM] * (2 * n), pl.BlockSpec(memory_space=pltpu.VMEM)),
        input_output_aliases={a: 2 + a for a in range(2 * n)},
        compiler_params=pltpu.CompilerParams(has_side_effects=EFFECT),
    )(*[_hbm(s) for s in shards], *[_hbm(lax.empty((4,) + s.shape, s.dtype)) for s in shards], *after)
    return res[0], res[1], list(res[2:2 + n]), list(res[2 + n:2 + 2 * n]), res[-1]


def allgather_wait(send, recv, shards, lands, after, name):
    n = len(shards)

    def body(*refs):
        ins, zones = refs[:n], refs[n:2 * n]
        send_r, recv_r = refs[2 * n], refs[2 * n + 1]
        x, y, c, jme, sib, chips = _place()
        slots = [2 * chip[0] + chip[1] for chip in chips] + [jme]
        for a in range(n):
            for k, slot in enumerate(slots):
                cp = _remote(ins[a], zones[a].at[slot], send_r, recv_r, a * 4 + k, sib)
                cp.wait_send()
                cp.wait_recv()

    res = pl.pallas_call(
        body, name=name,
        out_shape=tuple(pltpu.HBM(t.shape, t.dtype) for t in list(shards) + list(lands)),
        in_specs=[HBM] * (2 * n) + [SEM, SEM, ANY], out_specs=tuple([HBM] * (2 * n)),
        input_output_aliases={a: a for a in range(2 * n)},
        compiler_params=pltpu.CompilerParams(has_side_effects=EFFECT),
    )(*shards, *lands, send, recv, after)
    return list(res[n:])


def allgather_small(slab):
    def body(in_ref, out_ref, send, recv, lsem):
        x, y, c, jme, sib, chips = _place()
        loc = pltpu.make_async_copy(in_ref, out_ref.at[jme], lsem.at[0])
        loc.start()
        cps = [_remote(in_ref, out_ref.at[jme], send, recv, k, (*chip, c)) for k, chip in enumerate(chips)]
        for cp in cps:
            cp.start()
        for k, chip in enumerate(chips):
            piece = out_ref.at[2 * chip[0] + chip[1]]
            _remote(piece, piece, send, recv, k, (*chip, c)).wait_recv()
        for cp in cps:
            cp.wait_send()
        loc.wait()

    return pl.pallas_call(
        body, name="allgather_small", in_specs=[ANY], out_specs=ANY,
        out_shape=jax.ShapeDtypeStruct((4,) + slab.shape, slab.dtype),
        scratch_shapes=[pltpu.SemaphoreType.DMA((3,)), pltpu.SemaphoreType.DMA((3,)), pltpu.SemaphoreType.DMA((1,))],
    )(slab)


def allreduce_small(v):
    def body(v_ref, o_ref, r0, r1, r2, send, recv):
        x, y, c, jme, sib, chips = _place()
        peers = [sib, (1 - x, y, c), (x, 1 - y, c)]
        o_ref[...] = v_ref[...]
        for k, buf in enumerate((r0, r1, r2)):
            cp = _remote(o_ref, buf, send, recv, k, peers[k])
            cp.start()
            cp.wait()
            o_ref[...] = o_ref[...] + buf[...]

    vm = pl.BlockSpec(memory_space=pltpu.VMEM)
    return pl.pallas_call(
        body, name="allreduce_small", in_specs=[vm], out_specs=vm,
        out_shape=jax.ShapeDtypeStruct(v.shape, v.dtype),
        scratch_shapes=[pltpu.VMEM(v.shape, v.dtype)] * 3 + [pltpu.SemaphoreType.DMA((3,)), pltpu.SemaphoreType.DMA((3,))],
        compiler_params=pltpu.CompilerParams(vmem_limit_bytes=VMEM_LIMIT),
    )(v)


def _pair_copies(gs, lands, send, recv):
    x, y, c, jme, sib, chips = _place()
    return [_remote(gs[a].at[:, 1 - c], lands[a], send, recv, a, sib) for a in range(len(gs))]


def rs_pair_start(gs, name):
    n = len(gs)

    def body(*refs):
        ins, lands = refs[:n], refs[n:2 * n]
        send, recv = refs[2 * n], refs[2 * n + 1]
        token = refs[-1]
        for cp in _pair_copies(ins, lands, send, recv):
            cp.start()
        token[...] = jnp.zeros_like(token)

    shapes = [(4,) + g.shape[2:] for g in gs]
    res = pl.pallas_call(
        body, name=name,
        out_shape=(pltpu.SemaphoreType.DMA((n,)), pltpu.SemaphoreType.DMA((n,)),
                   *[pltpu.HBM(g.shape, g.dtype) for g in gs], *[pltpu.HBM(s, F32) for s in shapes],
                   jax.ShapeDtypeStruct((8, 128), F32)),
        in_specs=[HBM] * (2 * n), out_specs=(SEM, SEM, *[HBM] * (2 * n), pl.BlockSpec(memory_space=pltpu.VMEM)),
        input_output_aliases={a: 2 + a for a in range(2 * n)},
        compiler_params=pltpu.CompilerParams(has_side_effects=EFFECT),
    )(*[_hbm(g) for g in gs], *[_hbm(lax.empty(s, F32)) for s in shapes])
    return res[0], res[1], list(res[2:2 + n]), list(res[2 + n:2 + 2 * n]), res[-1]


def rs_pair_wait(send, recv, gs, lands, after, name):
    n = len(gs)

    def body(*refs):
        ins, zones = refs[:n], refs[n:2 * n]
        for cp in _pair_copies(ins, zones, refs[2 * n], refs[2 * n + 1]):
            cp.wait_send()
            cp.wait_recv()

    res = pl.pallas_call(
        body, name=name,
        out_shape=tuple(pltpu.HBM(t.shape, t.dtype) for t in list(gs) + list(lands)),
        in_specs=[HBM] * (2 * n) + [SEM, SEM, ANY], out_specs=tuple([HBM] * (2 * n)),
        input_output_aliases={a: a for a in range(2 * n)},
        compiler_params=pltpu.CompilerParams(has_side_effects=EFFECT),
    )(*gs, *lands, send, recv, after)
    return list(res[:n]), list(res[n:])


SUM_ROWS = 256


def rs_pair_sum(g4s, gots, cidx):
    n = len(g4s)
    tiles = [(min(g.shape[2], SUM_ROWS), g.shape[3]) for g in g4s]
    nts = [g.shape[2] // tr for g, (tr, _) in zip(g4s, tiles)]

    def at(a, s):
        s = jnp.minimum(s, 4 * nts[a] - 1)
        return s // nts[a], s % nts[a]

    def body(c_ref, *refs):
        for a in range(n):
            refs[2 * n + a][...] = (refs[a][...] + refs[n + a][...]).astype(BF16)

    in_specs = [pl.BlockSpec((None, None) + tiles[a], lambda s, cr, a=a: (at(a, s)[0], cr[0], at(a, s)[1], 0))
                for a in range(n)]
    in_specs += [pl.BlockSpec((None,) + tiles[a], lambda s, cr, a=a: (*at(a, s), 0)) for a in range(n)]
    return pl.pallas_call(
        body, name="rs_pair_sum",
        grid_spec=pltpu.PrefetchScalarGridSpec(
            num_scalar_prefetch=1, grid=(4 * max(nts),), in_specs=in_specs,
            out_specs=[pl.BlockSpec((None,) + tiles[a], lambda s, cr, a=a: (*at(a, s), 0)) for a in range(n)]),
        out_shape=[jax.ShapeDtypeStruct((4,) + g.shape[2:], BF16) for g in g4s],
        compiler_params=_cparams(("arbitrary",)),
    )(cidx, *g4s, *gots)


def _chip_copies(ps, lands, send, recv):
    x, y, c, jme, sib, chips = _place()
    return [_remote(ps[a].at[2 * chip[0] + chip[1]], lands[a].at[jme], send, recv, a * 3 + k, (*chip, c))
            for a in range(len(ps)) for k, chip in enumerate(chips)]


def rs_chip_start(ps, name):
    n = len(ps)

    def body(*refs):
        ins, lands = refs[:n], refs[n:2 * n]
        send, recv = refs[2 * n], refs[2 * n + 1]
        token = refs[-1]
        for cp in _chip_copies(ins, lands, send, recv):
            cp.start()
        token[...] = jnp.zeros_like(token)

    res = pl.pallas_call(
        body, name=name,
        out_shape=(pltpu.SemaphoreType.DMA((3 * n,)), pltpu.SemaphoreType.DMA((3 * n,)),
                   *[pltpu.HBM(p.shape, p.dtype) for p in ps], *[pltpu.HBM(p.shape, p.dtype) for p in ps],
                   jax.ShapeDtypeStruct((8, 128), F32)),
        in_specs=[HBM] * (2 * n), out_specs=(SEM, SEM, *[HBM] * (2 * n), pl.BlockSpec(memory_space=pltpu.VMEM)),
        input_output_aliases={a: 2 + a for a in range(2 * n)},
        compiler_params=pltpu.CompilerParams(has_side_effects=EFFECT),
    )(*[_hbm(p) for p in ps], *[_hbm(lax.empty(p.shape, p.dtype)) for p in ps])
    return res[0], res[1], list(res[2:2 + n]), list(res[2 + n:2 + 2 * n]), res[-1]


def rs_chip_wait(send, recv, ps, lands, after, name):
    n = len(ps)

    def body(*refs):
        ins, zones = refs[:n], refs[n:2 * n]
        send_r, recv_r = refs[2 * n], refs[2 * n + 1]
        x, y, c, jme, sib, chips = _place()
        for a in range(n):
            for k, chip in enumerate(chips):
                jt = 2 * chip[0] + chip[1]
                cp = _remote(ins[a].at[jt], zones[a].at[jt], send_r, recv_r, a * 3 + k, (*chip, c))
                cp.wait_send()
                cp.wait_recv()

    res = pl.pallas_call(
        body, name=name,
        out_shape=tuple(pltpu.HBM(p.shape, p.dtype) for p in list(ps) + list(lands)),
        in_specs=[HBM] * (2 * n) + [SEM, SEM] + [ANY] * len(after), out_specs=tuple([HBM] * (2 * n)),
        input_output_aliases={a: a for a in range(2 * n)},
        compiler_params=pltpu.CompilerParams(has_side_effects=EFFECT),
    )(*ps, *lands, send, recv, *after)
    return list(res[:n]), list(res[n:])


def rs_chip_sum(qs, ps, ls, accs, layers, jc):
    n = len(qs)
    tiles = [(min(q.shape[1], SUM_ROWS), q.shape[2]) for q in qs]
    nts = [q.shape[1] // tr for q, (tr, _) in zip(qs, tiles)]

    def at(a, s):
        return jnp.minimum(s, nts[a] - 1)

    def body(jc_ref, *refs):
        jme = jc_ref[0]
        for a in range(n):
            q_ref, p_ref, o_ref = refs[a], refs[n + a], refs[len(refs) - n + a]
            own = p_ref[...].astype(F32)
            v = [jnp.where(jme == j, own, q_ref[j].astype(F32)) for j in range(4)]
            o_ref[...] = ((v[0] + v[1]) + v[2]) + v[3]

    in_specs = [pl.BlockSpec((4,) + tiles[a], lambda s, jr, a=a: (0, at(a, s), 0)) for a in range(n)]
    in_specs += [pl.BlockSpec((None,) + tiles[a], lambda s, jr, a=a: (jr[0], at(a, s), 0)) for a in range(n)]
    args, aliases = [jc, *qs, *ps], {}
    for a in range(n):
        if accs[a] is not None:
            aliases[len(args)] = a
            in_specs.append(ANY)
            args.append(accs[a])
    return pl.pallas_call(
        body, name="rs_chip_sum",
        grid_spec=pltpu.PrefetchScalarGridSpec(
            num_scalar_prefetch=1, grid=(max(nts),), in_specs=in_specs,
            out_specs=[pl.BlockSpec((None, None) + tiles[a], lambda s, jr, a=a: (ls[a], jr[1], at(a, s), 0))
                       for a in range(n)]),
        out_shape=[jax.ShapeDtypeStruct((layers[a], 2) + qs[a].shape[1:], F32) for a in range(n)],
        input_output_aliases=aliases,
        compiler_params=_cparams(("arbitrary",)),
    )(*args)


def rs_pair_gather(rs):
    n = len(rs)

    def body(*refs):
        outs = refs[n:2 * n]
        send, recv = refs[2 * n:]
        x, y, c, jme, sib, chips = _place()
        cps = [_remote(outs[a].at[:, c], outs[a].at[:, c], send, recv, a, sib) for a in range(n)]
        for cp in cps:
            cp.start()
        for a in range(n):
            slot = outs[a].at[:, 1 - c]
            _remote(slot, slot, send, recv, a, sib).wait_recv()
        for cp in cps:
            cp.wait_send()

    return pl.pallas_call(
        body, name="rs_pair_gather", in_specs=[ANY] * n, out_specs=[ANY] * n,
        out_shape=[jax.ShapeDtypeStruct(r.shape, r.dtype) for r in rs],
        input_output_aliases={a: a for a in range(n)},
        scratch_shapes=[pltpu.SemaphoreType.DMA((n,)), pltpu.SemaphoreType.DMA((n,))],
    )(*rs)


def _adamw_math(w, g, m, v):
    m = B1 * m + (1.0 - B1) * g
    v = B2 * v + (1.0 - B2) * (g * g)
    m_hat = m / (1.0 - B1 ** STEP)
    v_hat = v / (1.0 - B2 ** STEP)
    return -LR * (m_hat / (jnp.sqrt(v_hat) + AEPS) + WD * w), m, v


ADAMW_TILE = 512 * 1024


def adamw(w, g, m, v, name, with_grad=False):
    rows, cols = w.shape
    tr = next((t for t in (1024, 512, 256) if rows % t == 0 and t * cols <= ADAMW_TILE), rows)
    fn =(lambda wv, gv, mv, vv: (gv,) + _adamw_math(wv, gv, mv, vv)) if with_grad else _adamw_math
    return rw(fn, [(a, 0, cols) for a in (w, g, m, v)], [(cols, F32)] * (4 if with_grad else 3), name, rows, tr=tr)


def adamw_small(ws, gs, ms, vs):
    n = len(ws)

    def body(*refs):
        for a in range(n):
            res = _adamw_math(*[refs[k * n + a][...] for k in range(4)])
            for k in range(3):
                refs[(4 + k) * n + a][...] = res[k]

    res = pl.pallas_call(
        body, name="adamw_small", in_specs=[VM] * (4 * n), out_specs=[VM] * (3 * n),
        out_shape=[jax.ShapeDtypeStruct(w.shape, F32) for _ in range(3) for w in ws],
        compiler_params=pltpu.CompilerParams(vmem_limit_bytes=VMEM_LIMIT),
    )(*ws, *gs, *ms, *vs)
    return [(res[a], res[n + a], res[2 * n + a]) for a in range(n)]


WEIGHTS = ["norm_ab", "w_in_ab", "pool_w", "pool_scale", "w_out_ab", "norm_cd", "w_in_cd", "sgu_ln_g", "sgu_ln_b",
           "sgu_w", "sgu_b", "s5_a_re", "s5_a_im", "s5_log_dt", "s5_b_re", "s5_b_im", "s5_c_re", "s5_c_im", "s5_d",
           "glu_w1", "glu_w2", "w_out_cd", "norm_x", "w_xq", "w_xkv", "w_xo", "mem_norm", "final_norm"]
INPUTS = ["x", "mem"] + WEIGHTS + ["loss_target"] + ["m_" + n for n in WEIGHTS] + ["v_" + n for n in WEIGHTS]
BIG = ["w_in_ab", "w_out_ab", "w_in_cd", "w_out_cd", "w_xq", "w_xkv", "w_xo", "glu_w1", "glu_w2", "pool_w"]
COL_SHARDED = ("w_in_ab", "w_in_cd", "w_xkv")
SMALL = [n for n in WEIGHTS if n not in BIG]
SMALL_SHARDED = {"norm_cd": 256, "sgu_ln_g": 256, "sgu_ln_b": 256, "s5_d": 128}
PACK = 256 * 128


def _pack(arrs):
    flat = jnp.concatenate([a.reshape(-1) for a in arrs])
    pad = (-flat.shape[0]) % PACK
    return jnp.concatenate([flat, jnp.zeros((pad,), flat.dtype)]).reshape(-1, 128)


def _unpack(packed, shapes):
    flat, out, off = packed.reshape(-1), [], 0
    for s in shapes:
        n = 1
        for d in s:
            n *= d
        out.append(flat[off:off + n].reshape(s))
        off += n
    return out


LAYER_KEYS = (("w_in", "w_out", "pool_w", "w_xq", "w_xkv", "w_xo"),
              ("w_in", "w_out", "glu_w1", "glu_w2", "w_xq", "w_xkv", "w_xo"))


def _weight_of(key, layer):
    if key in ("w_xq", "w_xkv", "w_xo"):
        return key, layer, 4
    kind = "ab" if layer % 2 == 0 else "cd"
    return {"w_in": "w_in_" + kind, "w_out": "w_out_" + kind}.get(key, key), layer // 2, 2


def kernel(*args):
    a = dict(zip(INPUTS, args))
    x_i, y_i, c_i = lax.axis_index("x"), lax.axis_index("y"), lax.axis_index("c")
    j = 2 * x_i + y_i

    slab = jnp.concatenate([a["norm_cd"], a["sgu_ln_g"], a["sgu_ln_b"],
                            jnp.pad(a["s5_d"], ((0, 0), (0, 128)))], axis=0)
    gslab = allgather_small(slab)
    P = {n: a[n] for n in SMALL}
    for k, n in enumerate(("norm_cd", "sgu_ln_g", "sgu_ln_b", "s5_d")):
        wd = SMALL_SHARDED[n]
        P[n] = gslab[:, 2 * k:2 * k + 2, :wd].transpose(1, 0, 2).reshape(2, 4 * wd)

    def shards_of(layer):
        keys = sorted(k for k in LAYER_KEYS[layer % 2])
        out = []
        for k in keys:
            n, l, _ = _weight_of(k, layer)
            out.append(a[n][l].reshape(-1, a[n].shape[-1]).astype(BF16))
        return keys, out

    keys0, sh0 = shards_of(0)
    first = keys0.index("w_in")
    g_in, token = allgather_sync([sh0[first].reshape(2, sh0[first].shape[0] // 2, sh0[first].shape[1])])
    w_in0 = g_in[0].reshape(4, -1, g_in[0].shape[-1])
    started = {}
    for layer in (0, 1, 2, 3):
        keys, sh = (keys0, sh0) if layer == 0 else shards_of(layer)
        rest = [(k, s) for k, s in zip(keys, sh) if k != "w_in"]
        parts = [("in", ["w_in"], [sh[keys.index("w_in")]])] * (layer > 0) + [("", *map(list, zip(*rest)))]
        for tag, pk, ps in parts:
            send, recv, ps, lands, token = allgather_start(ps, [token, gslab], "allgather_start_%d%s" % (layer, tag))
            started[(layer, tag)] = (pk, send, recv, ps, lands)
    P["norm_ab"] = P["norm_ab"] + token[0:1, 0:1]

    cidx = jnp.reshape(c_i, (1,)).astype(jnp.int32)
    jc = jnp.stack([j, c_i]).astype(jnp.int32)

    def views(g):
        W = {}
        for k, v in g.items():
            if k in ("w_in", "w_xkv"):
                W[k] = mcs(v)
            elif k == "pool_w":
                W[k] = v.reshape(4, 4, 64, 256).transpose(1, 0, 2, 3).reshape(4, 256, 256)
            elif k not in ("glu_w1", "glu_w2"):
                W[k] = m2(v.reshape(-1, v.shape[-1]))
        if "glu_w1" in g:
            W["w12"] = jnp.concatenate([g["glu_w1"].reshape(512, 512), g["glu_w2"].reshape(512, 512)], axis=1)
        return W

    def arrived(layer, tag, after):
        keys, send, recv, sh, lands = started[(layer, tag)]
        return views(dict(zip(keys, allgather_wait(send, recv, sh, lands, after, "allgather_wait_%d%s" % (layer, tag)))))

    def weights_of(layer, x_in):
        W = views({"w_in": w_in0}) if layer == 0 else arrived(layer, "in", x_in)
        W["more"] = lambda after: arrived(layer, "", after)
        return W

    halves, pending = {}, {}

    def finish_pair(layer, after):
        keys, send, recv, flat, lands = halves.pop(layer)
        flat, got = rs_pair_wait(send, recv, flat, lands, after, "rs_pair_wait_%d" % layer)
        pair = rs_pair_sum(flat, got, cidx)
        send, recv, pair, lands, token = rs_chip_start(pair, "rs_chip_start_%d" % layer)
        pending[layer] = (keys, send, recv, pair, lands)
        return token

    def grads_done(layer, GW):
        keys = sorted(GW)
        flat = [GW[k].reshape(4, 2, GW[k].shape[1] // 2, GW[k].shape[2]) for k in keys]
        send, recv, flat, lands, token = rs_pair_start(flat, "rs_pair_start_%d" % layer)
        halves[layer] = (keys, send, recv, flat, lands)
        if layer + 1 in halves:
            token = token + finish_pair(layer + 1, token)
        return token[0:1, 0:1]

    loss, dx, G = local_step(a["x"][0], a["mem"][0], a["loss_target"][0], P, weights_of, grads_done)
    loss = lax.psum(loss[0, 0], ("x", "y", "c"))
    finish_pair(0, dx)
    outs = {}

    def update_big(names, red):
        for n, g in zip(names, rs_pair_gather([red[n] for n in names])):
            shp = a[n].shape
            g2 = g.reshape(-1, shp[-1])
            upd = adamw(a[n].reshape(g2.shape), g2, a["m_" + n].reshape(g2.shape), a["v_" + n].reshape(g2.shape),
                        "adamw_" + n, with_grad=True)
            outs[n] = tuple(t.reshape(shp) for t in upd)

    def reduce_layer(layer, red, after):
        keys, send, recv, pair, lands = pending[layer]
        pair, lands = rs_chip_wait(send, recv, pair, lands, after, "rs_chip_wait_%d" % layer)
        which = [_weight_of(k, layer) for k in keys]
        sums = rs_chip_sum(lands, pair, [l for _, l, _ in which], [red.get(n) for n, _, _ in which],
                           [layers for _, _, layers in which], jc)
        red.update(zip([n for n, _, _ in which], sums))

    red = {}
    for layer in (3, 2, 1):
        reduce_layer(layer, red, [dx])
    odd_only = [n for n in BIG if n.endswith("_cd") or n.startswith("glu")]
    update_big(odd_only, red)

    gfull = [jnp.stack(G[n]) if isinstance(G[n], list) else G[n] for n in SMALL]
    shapes = [g.shape for g in gfull]
    gsum = _unpack(allreduce_small(_pack(gfull)), shapes)
    gloc = []
    for n, g in zip(SMALL, gsum):
        if n in SMALL_SHARDED:
            g = lax.dynamic_slice_in_dim(g, j * SMALL_SHARDED[n], SMALL_SHARDED[n], axis=1)
        gloc.append(g)
    two = [(-1, a[n].shape[-1]) if a[n].ndim > 1 else (1, a[n].shape[0]) for n in SMALL]
    upds = adamw_small(*[[t.reshape(s) for t, s in zip(ts, two)]
                         for ts in ([a[n] for n in SMALL], gloc, [a["m_" + n] for n in SMALL],
                                    [a["v_" + n] for n in SMALL])])
    for n, g, upd in zip(SMALL, gloc, upds):
        outs[n] = (g,) + tuple(t.reshape(a[n].shape) for t in upd)

    behind = [outs[n][1] for n in odd_only + SMALL[-1:]] + [red[n] for n in BIG if n not in odd_only]
    reduce_layer(0, red, behind)
    update_big([n for n in BIG if n not in odd_only], red)

    res = [loss, dx[None]]
    for part in range(4):
        res += [outs[n][part] for n in WEIGHTS]
    return tuple(res)
```

```python
import math

import jax
import jax.numpy as jnp
from jax import lax
from jax.experimental import pallas as pl
from jax.experimental.pallas import tpu as pltpu

F32, BF16 = jnp.float32, jnp.bfloat16
S, D = 2048, 1024
MEM = 256
EPS = 1e-6
NEG = -1e30
QB = 128
PATTERNS = (1, 4, 16)
NG, NP, NH = 32, 64, 16
NS = NG * NP
LR, B1, B2, AEPS, WD, STEP = 0.001, 0.9, 0.999, 1e-08, 0.01, 10
MESHID = pl.DeviceIdType.MESH
VMEM_LIMIT = 56 * 1024 * 1024


def _cparams(sem):
    return pltpu.CompilerParams(dimension_semantics=sem, vmem_limit_bytes=VMEM_LIMIT)


def _sig(x):
    return 1.0 / (1.0 + jnp.exp(-x))


def _dot(a, b, dims):
    return lax.dot_general(a, b, (dims, ((), ())), preferred_element_type=F32)


def _nn(a, b):
    return _dot(a, b, ((1,), (0,)))


def _nt(a, b):
    return _dot(a, b, ((1,), (1,)))


def _tn(a, b):
    return _dot(a, b, ((0,), (0,)))


_DIMS = {"nn": ((1,), (0,)), "nt": ((1,), (1,)), "tn": ((0,), (0,))}


def _tile(dim, cc=None, cap=1024):
    for t in (2048, 1536, 1024, 768, 512, 384, 256, 128):
        if t <= cap and dim % t == 0 and (cc is None or cc % t == 0):
            return t
    return dim


MM_VMEM = 36 * 1024 * 1024


def _mm_tiles(m, n, k, ccm, ccn, cck, a_bytes, b_bytes, o_bytes):
    caps = [1024, 1024, 2048]
    while True:
        tm, tn, tk = _tile(m, ccm, caps[0]), _tile(n, ccn, caps[1]), _tile(k, cck, caps[2])
        need = 2 * (tm * tk * a_bytes + tk * tn * b_bytes + tm * tn * o_bytes) + (tm * tn * 4 if tk < k else 0)
        if need <= MM_VMEM:
            return tm, tn, tk
        if tk > 1024:
            caps[2] = tk // 2
        elif tn >= tm:
            caps[1] = tn // 2
        else:
            caps[0] = tm // 2


def m2(arr, col_off=0, ncols=None):
    rows, cols = arr.shape
    ncols = cols - col_off if ncols is None else ncols

    def spec(tr, tc, rc):
        assert col_off % tc == 0
        return pl.BlockSpec((tr, tc), lambda *g: (rc(*g)[0], rc(*g)[1] + col_off // tc))
    return (arr, rows, ncols, spec, None if col_off == 0 else col_off)


def mcs(arr):
    cs = arr.shape[2]

    def spec(tr, tc, rc):
        n = cs // tc
        return pl.BlockSpec((None, tr, tc), lambda *g: (rc(*g)[1] // n, rc(*g)[0], rc(*g)[1] % n))
    return (arr, arr.shape[1], 4 * cs, spec, cs)


def out2(rows, cols):
    def spec(tr, tc, rc):
        return pl.BlockSpec((tr, tc), lambda *g: tuple(rc(*g)))
    return ((rows, cols), spec, None)


def outcs(rows, cs):
    def spec(tr, tc, rc):
        n = cs // tc
        return pl.BlockSpec((None, tr, tc), lambda *g: (rc(*g)[1] // n, rc(*g)[0], rc(*g)[1] % n))
    return ((4, rows, cs), spec, cs)


def _both(a, b):
    if a is None:
        return b
    if b is None:
        return a
    return math.gcd(a, b)


def mm(a, b, mode, name, add=None, out=None, out_dtype=F32):
    a_arr, a_r, a_c, a_spec, a_cc = a
    b_arr, b_r, b_c, b_spec, b_cc = b
    if mode == "nn":
        m, k, n = a_r, a_c, b_c
        assert b_r == k
        ccm, cck, ccn = None, a_cc, b_cc
    elif mode == "nt":
        m, k, n = a_r, a_c, b_r
        assert b_c == k
        ccm, cck, ccn = None, _both(a_cc, b_cc), None
    else:
        m, k, n = a_c, a_r, b_c
        assert b_r == k
        ccm, cck, ccn = a_cc, None, b_cc
    out = out2(m, n) if out is None else out
    o_shape, o_spec, o_cc = out
    ccn = _both(ccn, o_cc)
    if add is not None:
        ccn = _both(ccn, add[4])
    o_bytes = jnp.dtype(out_dtype).itemsize + (0 if add is None else add[0].dtype.itemsize)
    tm, tn, tk = _mm_tiles(m, n, k, ccm, ccn, cck, a_arr.dtype.itemsize, b_arr.dtype.itemsize, o_bytes)
    nk = k // tk
    if mode == "nn":
        in_specs = [a_spec(tm, tk, lambda i, j, kk: (i, kk)), b_spec(tk, tn, lambda i, j, kk: (kk, j))]
    elif mode == "nt":
        in_specs = [a_spec(tm, tk, lambda i, j, kk: (i, kk)), b_spec(tn, tk, lambda i, j, kk: (j, kk))]
    else:
        in_specs = [a_spec(tk, tm, lambda i, j, kk: (kk, i)), b_spec(tk, tn, lambda i, j, kk: (kk, j))]
    args = [a_arr, b_arr]
    if add is not None:
        in_specs.append(add[3](tm, tn, lambda i, j, kk: (i, j)))
        args.append(add[0])
    return _mm_call(args, in_specs, o_spec(tm, tn, lambda i, j, kk: (i, j)), jax.ShapeDtypeStruct(o_shape, out_dtype),
                    mode, (m // tm, n // tn, nk), (tm, tn), add is not None, name)


def _mm_call(args, in_specs, out_spec, out_shape, mode, grid, tile, has_add, name):
    dims = _DIMS[mode]
    nk = grid[2]
    tm, tn = tile

    def body(*refs):
        a_ref, b_ref = refs[0], refs[1]
        add_ref = refs[2] if has_add else None
        prod = _dot(a_ref[...].astype(BF16), b_ref[...].astype(BF16), dims)
        if nk == 1:
            o_ref = refs[-1]
            if has_add:
                prod = prod + add_ref[...].astype(F32)
            o_ref[...] = prod.astype(o_ref.dtype)
            return
        o_ref, acc = refs[-2], refs[-1]
        kk = pl.program_id(2)

        @pl.when(kk == 0)
        def _():
            acc[...] = prod

        @pl.when(kk > 0)
        def _():
            acc[...] += prod

        @pl.when(kk == nk - 1)
        def _():
            r = acc[...]
            if has_add:
                r = r + add_ref[...].astype(F32)
            o_ref[...] = r.astype(o_ref.dtype)

    return pl.pallas_call(
        body, name=name, grid=grid, in_specs=in_specs, out_specs=out_spec, out_shape=out_shape,
        scratch_shapes=[pltpu.VMEM((tm, tn), F32)] if nk > 1 else [],
        compiler_params=_cparams(("parallel", "parallel", "arbitrary")),
    )(*args)


def mm_band(a, b, mode, name, grid, blocks, maps, out_shape, add=None, out_dtype=F32):
    in_specs = [pl.BlockSpec(blocks[0], maps[0]), pl.BlockSpec(blocks[1], maps[1])]
    args = [a, b]
    if add is not None:
        in_specs.append(pl.BlockSpec(blocks[2], maps[2]))
        args.append(add)
    return _mm_call(args, in_specs, pl.BlockSpec(blocks[2], maps[2]), jax.ShapeDtypeStruct(out_shape, out_dtype),
                    mode, grid, blocks[2], add is not None, name)


def rw(fn, ins, outs, name, rows, tr=None, consts=(), accs=()):
    tr = min(rows, 1024) if tr is None else tr
    n_in, n_c, n_o, n_a = len(ins), len(consts), len(outs), len(accs)
    in_specs = []
    for arr, off, width in ins:
        assert off % width == 0
        in_specs.append(pl.BlockSpec((tr, width), lambda i, o=off // width: (i, o)))
    for c in consts:
        in_specs.append(pl.BlockSpec(c.shape, lambda i: (0, 0)))
    out_specs = [pl.BlockSpec((tr, w), lambda i: (i, 0)) for w, _ in outs]
    out_specs += [pl.BlockSpec(s, lambda i: (0, 0)) for s in accs]
    out_shape = [jax.ShapeDtypeStruct((rows, w), dt) for w, dt in outs]
    out_shape += [jax.ShapeDtypeStruct(s, F32) for s in accs]

    def body(*refs):
        vals = [r[...] for r in refs[:n_in + n_c]]
        o_refs = refs[n_in + n_c:n_in + n_c + n_o]
        a_refs = refs[n_in + n_c + n_o:]
        res = fn(*vals)
        for r, v in zip(o_refs, res[:n_o]):
            r[...] = v.astype(r.dtype)
        if n_a:
            @pl.when(pl.program_id(0) == 0)
            def _():
                for r in a_refs:
                    r[...] = jnp.zeros_like(r)
            for r, v in zip(a_refs, res[n_o:]):
                r[...] += v

    res = pl.pallas_call(
        body, name=name, grid=(rows // tr,), in_specs=in_specs, out_specs=out_specs,
        out_shape=out_shape,
        compiler_params=_cparams(("arbitrary",) if n_a else ("parallel",)),
    )(*[a for a, _, _ in ins], *consts)
    return res


def _rstd(x):
    return lax.rsqrt(jnp.mean(x * x, axis=-1, keepdims=True) + EPS)


def rms_fwd(x, g, name):
    def fn(xv, gv):
        xv = xv.astype(F32)
        return (xv * _rstd(xv) * gv,)
    return rw(fn, [(x, 0, D)], [(D, BF16)], name, x.shape[0], consts=[g])[0]


def _rms_bwd_math(xv, dy, gv):
    r = _rstd(xv)
    dyg = dy * gv
    dx = r * dyg - xv * (r * r * r / D) * jnp.sum(dyg * xv, axis=-1, keepdims=True)
    dg = jnp.sum(dy * xv * r, axis=0, keepdims=True)
    return dx, dg


def rms_bwd(x, dy, dres, g, name):
    def fn(xv, dyv, drv, gv):
        dx, dg = _rms_bwd_math(xv, dyv, gv)
        return dx + drv, dg
    return rw(fn, [(x, 0, D), (dy, 0, D), (dres, 0, D)], [(D, F32)], name, x.shape[0],
              consts=[g], accs=[(1, D)])


def final_loss(x, tgt, g):
    def fn(xv, tv, gv):
        e = xv * _rstd(xv) * gv - tv
        loss = 0.5 * jnp.sum(jnp.sum(e * e, axis=-1, keepdims=True), axis=0, keepdims=True) / D
        dx, dg = _rms_bwd_math(xv, e / D, gv)
        return dx, loss, dg
    return rw(fn, [(x, 0, D), (tgt, 0, D)], [(D, F32)], "final_loss", S, consts=[g],
              accs=[(1, 1), (1, D)])


def _attn_bias(bias_ref):
    ii = lax.broadcasted_iota(jnp.int32, (2 * QB, 2 * QB), 0) % QB
    jj = lax.broadcasted_iota(jnp.int32, (2 * QB, 2 * QB), 1)
    dist = ii + QB - jj
    band = (dist >= 0) & (dist <= QB)
    bias_ref[1] = jnp.where(band, 0.0, NEG)
    bias_ref[0] = jnp.where(band & (jj >= QB), 0.0, NEG)


def _two_heads(x, m0):
    return jnp.concatenate([jnp.where(m0, x, 0.0), jnp.where(m0, 0.0, x)], axis=0)


def _per_head(col, m0):
    return jnp.where(m0, col[:QB], col[QB:])


def _attn_rows(idx, d):
    if d == 1:
        b = idx
        cur = pl.ds(pl.multiple_of(b * QB, QB), QB)
        prev = pl.ds(pl.multiple_of(jnp.maximum(b - 1, 0) * QB, QB), QB)
    else:
        r, b = lax.rem(idx, d), lax.div(idx, d)
        cur = pl.ds(r + b * (QB * d), QB, stride=d)
        prev = pl.ds(r + jnp.maximum(b - 1, 0) * (QB * d), QB, stride=d)
    return cur, prev, b


NBLK = S // QB
GROUP = 16
GROUP_FWD = 16


def _colblk(off):
    return pl.BlockSpec((S, 128), lambda hp: (0, off * 8 + hp))


def attn_fwd(z):
    def body(q_ref, k_ref, v_ref, g_ref, o_ref, l_ref, a_ref, os, ls, bias):
        _attn_bias(bias)
        m0 = lax.broadcasted_iota(jnp.int32, (1, 128), 1) < 64
        for pi, d in enumerate(PATTERNS):
            lone = S // d == QB

            def load(idx, d=d, lone=lone):
                cur, prev, b = _attn_rows(idx, d)
                if lone:
                    return cur, (q_ref[cur, :], None, k_ref[cur, :], None, v_ref[cur, :], bias[1, :, QB:])
                return cur, (q_ref[cur, :], k_ref[prev, :], k_ref[cur, :], v_ref[prev, :], v_ref[cur, :],
                             bias[jnp.minimum(b, 1)])

            def block(q, kp, kc, vp, vc, bs):
                qq = _two_heads(q * 0.125, m0).astype(BF16)
                k = (kc if kp is None else jnp.concatenate([kp, kc], axis=0)).astype(BF16)
                s = _nt(qq, k) + bs
                mx = jnp.max(s, axis=-1, keepdims=True)
                p = jnp.exp(s - mx)
                den = jnp.sum(p, axis=-1, keepdims=True)
                pb = p.astype(BF16)
                vv = _two_heads(vc if vp is None else jnp.concatenate([vp, vc], axis=0), m0).astype(BF16)
                o = _nn(jnp.concatenate([pb[:QB], pb[QB:]], axis=1), vv)
                return o * _per_head(1.0 / den, m0), _per_head(mx + jnp.log(den), m0)

            def step(i, carry, pi=pi):
                loaded = [load(i * GROUP_FWD + u) for u in range(GROUP_FWD)]
                done = [block(*vals) for _, vals in loaded]
                for (cur, _), (o, l) in zip(loaded, done):
                    os[pi, cur, :] = o
                    ls[pi, cur, :] = l
                return carry
            lax.fori_loop(0, NBLK // GROUP_FWD, step, 0)
        l1, l2, l3 = ls[0], ls[1], ls[2]
        mx = jnp.maximum(jnp.maximum(l1, l2), l3)
        e1, e2, e3 = jnp.exp(l1 - mx), jnp.exp(l2 - mx), jnp.exp(l3 - mx)
        tot = e1 + e2 + e3
        o = (os[0] * e1 + os[1] * e2 + os[2] * e3) / tot
        ga = g_ref[...]
        o_ref[...] = o
        l_ref[...] = mx + jnp.log(tot)
        a_ref[...] = (o * (ga * _sig(ga))).astype(a_ref.dtype)

    out = pl.BlockSpec((S, 128), lambda hp: (0, hp))
    return pl.pallas_call(
        body, name="attn_fwd", grid=(8,),
        in_specs=[_colblk(0), _colblk(1), _colblk(2), _colblk(3)], out_specs=[out] * 3,
        out_shape=[jax.ShapeDtypeStruct((S, D), F32), jax.ShapeDtypeStruct((S, D), F32),
                   jax.ShapeDtypeStruct((S, 2 * D), BF16)],
        scratch_shapes=[pltpu.VMEM((3, S, 128), F32), pltpu.VMEM((3, S, 128), F32),
                        pltpu.VMEM((2, 2 * QB, 2 * QB), F32)],
        compiler_params=_cparams(("parallel",)),
    )(z, z, z, z)


def attn_bwd(z, d_cat, o, lse):
    def body(q_ref, k_ref, v_ref, g_ref, da_ref, o_ref, l_ref, dq_ref, dk_ref, dv_ref, dg_ref, do_s, pr_s, bias):
        _attn_bias(bias)
        m0 = lax.broadcasted_iota(jnp.int32, (1, 128), 1) < 64
        ga = g_ref[...]
        sg = _sig(ga)
        da = da_ref[...]
        ov = o_ref[...]
        do = da * (ga * sg)
        dg_ref[...] = da * ov * (sg * (1.0 + ga * (1.0 - sg)))
        do_s[...] = do
        pr_s[...] = do * ov
        dq_ref[...] = jnp.zeros_like(dq_ref)
        dk_ref[...] = jnp.zeros_like(dk_ref)
        dv_ref[...] = jnp.zeros_like(dv_ref)
        for d in PATTERNS:
            lone = S // d == QB

            def load(idx, d=d, lone=lone):
                cur, prev, b = _attn_rows(idx, d)
                if lone:
                    return (cur, None), (q_ref[cur, :], None, k_ref[cur, :], None, v_ref[cur, :],
                                         do_s[cur, :], pr_s[cur, :], l_ref[cur, :], bias[1, :, QB:])
                return (cur, prev), (q_ref[cur, :], k_ref[prev, :], k_ref[cur, :], v_ref[prev, :], v_ref[cur, :],
                                     do_s[cur, :], pr_s[cur, :], l_ref[cur, :], bias[jnp.minimum(b, 1)])

            def block(q, kp, kc, vp, vc, dof, prod, lp, bs):
                qq = _two_heads(q * 0.125, m0).astype(BF16)
                kf = kc if kp is None else jnp.concatenate([kp, kc], axis=0)
                k = kf.astype(BF16)
                v = (vc if vp is None else jnp.concatenate([vp, vc], axis=0)).astype(BF16)
                dd = _two_heads(dof, m0).astype(BF16)
                lh = jnp.max(jnp.concatenate([jnp.where(m0, lp, -jnp.inf), jnp.where(m0, -jnp.inf, lp)], axis=0),
                             axis=-1, keepdims=True)
                delta = jnp.sum(_two_heads(prod, m0), axis=-1, keepdims=True)
                p = jnp.exp(_nt(qq, k) + bs - lh)
                ds = (p * (_nt(dd, v) - delta)).astype(BF16)
                dq = _nn(jnp.concatenate([ds[:QB], ds[QB:]], axis=1), _two_heads(kf, m0).astype(BF16))
                return dq * 0.125, _tn(ds, qq), _tn(p.astype(BF16), dd)

            def step(i, carry):
                loaded = [load(i * GROUP + u) for u in range(GROUP)]
                done = [block(*vals) for _, vals in loaded]
                for ((cur, prev), _), (dq, dk, dv) in zip(loaded, done):
                    dq_ref[cur, :] = dq_ref[cur, :] + dq
                    if prev is not None:
                        dk_ref[prev, :] = dk_ref[prev, :] + dk[:QB]
                        dv_ref[prev, :] = dv_ref[prev, :] + dv[:QB]
                    dk_ref[cur, :] = dk_ref[cur, :] + dk[-QB:]
                    dv_ref[cur, :] = dv_ref[cur, :] + dv[-QB:]
                return carry
            lax.fori_loop(0, NBLK // GROUP, step, 0)

    blk = pl.BlockSpec((S, 128), lambda hp: (0, hp))
    return pl.pallas_call(
        body, name="attn_bwd", grid=(8,),
        in_specs=[_colblk(0), _colblk(1), _colblk(2), _colblk(3), blk, blk, blk], out_specs=[blk] * 4,
        out_shape=[jax.ShapeDtypeStruct((S, D), F32)] * 4,
        scratch_shapes=[pltpu.VMEM((S, 128), F32), pltpu.VMEM((S, 128), F32), pltpu.VMEM((2, 2 * QB, 2 * QB), F32)],
        compiler_params=_cparams(("parallel",)),
    )(z, z, z, z, d_cat, o, lse)


def assemble_dz_even(parts):
    def body(*refs):
        o_ref = refs[-1]
        for j in range(6):
            o_ref[:, j * D:(j + 1) * D] = refs[j][...].astype(o_ref.dtype)
    tr = 512
    blk = pl.BlockSpec((tr, D), lambda i: (i, 0))
    return pl.pallas_call(
        body, name="assemble_dz_even", grid=(S // tr,), in_specs=[blk] * 6,
        out_specs=pl.BlockSpec((tr, 6 * D), lambda i: (i, 0)),
        out_shape=jax.ShapeDtypeStruct((S, 6 * D), BF16),
        compiler_params=_cparams(("parallel",)),
    )(*parts)


def _pool_window(g):
    return jnp.where(g == 0, 2.0, jnp.where(g == 1, 4.0, jnp.where(g == 2, 8.0, 16.0)))


def _pool_sel(g, levels):
    return jnp.where(g == 0, levels[0], jnp.where(g == 1, levels[1], jnp.where(g == 2, levels[2], levels[3])))


def _pool_fwd_math(v, g):
    t = lax.broadcasted_iota(jnp.int32, (S, 1), 0)
    s = v
    levels = []
    for k in (1, 2, 4, 8):
        s = s + jnp.where(t >= k, pltpu.roll(s, k, 0), 0.0)
        levels.append(s)
    cnt = jnp.minimum((t + 1).astype(F32), _pool_window(g))
    return _pool_sel(g, levels) / cnt - v, cnt


def pool_fwd(z, pw, ps, cat):
    def body(v_ref, g_ref, pw_ref, ps_ref, cat_ref, o_ref):
        g = pl.program_id(0)
        pooled, _ = _pool_fwd_math(v_ref[...], g)
        mixed = _nn(pooled.astype(BF16), pw_ref[...].astype(BF16))
        gb = g_ref[...]
        o_ref[...] = (mixed * ps_ref[...] * (gb * _sig(gb))).astype(o_ref.dtype)

    return pl.pallas_call(
        body, name="pool_fwd", grid=(4,),
        in_specs=[pl.BlockSpec((S, 256), lambda g: (0, 16 + g)),
                  pl.BlockSpec((S, 256), lambda g: (0, 20 + g)),
                  pl.BlockSpec((None, 256, 256), lambda g: (g, 0, 0)),
                  pl.BlockSpec((1, 256), lambda g: (0, g)), pl.BlockSpec(memory_space=pl.ANY)],
        out_specs=pl.BlockSpec((S, 256), lambda g: (0, 4 + g)),
        out_shape=jax.ShapeDtypeStruct((S, 2 * D), BF16),
        input_output_aliases={4: 0},
        compiler_params=_cparams(("parallel",)),
    )(z, z, pw, ps, cat)


def pool_bwd(z, d_cat, pw, ps):
    def body(v_ref, g_ref, d_ref, pw_ref, ps_ref, dv_ref, dg_ref, dpw_ref, dps_ref):
        g = pl.program_id(0)
        v = v_ref[...]
        pooled, cnt = _pool_fwd_math(v, g)
        pwb = pw_ref[...].astype(BF16)
        pb = pooled.astype(BF16)
        mixed = _nn(pb, pwb)
        gb = g_ref[...]
        sg = _sig(gb)
        dout = d_ref[...]
        sc = ps_ref[...]
        dg_ref[...] = dout * mixed * sc * (sg * (1.0 + gb * (1.0 - sg)))
        dms = dout * (gb * sg)
        dps_ref[...] = jnp.sum(dms * mixed, axis=0, keepdims=True)
        dmx = (dms * sc).astype(BF16)
        dpw_ref[...] = _tn(pb, dmx)
        dpooled = _nt(dmx, pwb)
        t = lax.broadcasted_iota(jnp.int32, (S, 1), 0)
        s = dpooled / cnt
        levels = []
        for k in (1, 2, 4, 8):
            s = s + jnp.where(t < S - k, pltpu.roll(s, S - k, 0), 0.0)
            levels.append(s)
        dv_ref[...] = _pool_sel(g, levels) - dpooled

    return pl.pallas_call(
        body, name="pool_bwd", grid=(4,),
        in_specs=[pl.BlockSpec((S, 256), lambda g: (0, 16 + g)),
                  pl.BlockSpec((S, 256), lambda g: (0, 20 + g)),
                  pl.BlockSpec((S, 256), lambda g: (0, 4 + g)),
                  pl.BlockSpec((None, 256, 256), lambda g: (g, 0, 0)),
                  pl.BlockSpec((1, 256), lambda g: (0, g))],
        out_specs=[pl.BlockSpec((S, 256), lambda g: (0, g)),
                   pl.BlockSpec((S, 256), lambda g: (0, g)),
                   pl.BlockSpec((None, 256, 256), lambda g: (g, 0, 0)),
                   pl.BlockSpec((1, 256), lambda g: (0, g))],
        out_shape=[jax.ShapeDtypeStruct((S, D), F32), jax.ShapeDtypeStruct((S, D), F32),
                   jax.ShapeDtypeStruct((4, 256, 256), F32), jax.ShapeDtypeStruct((1, D), F32)],
        compiler_params=_cparams(("parallel",)),
    )(z, z, d_cat, pw, ps)


CH = 128


def _sgu_common(v, lng, lnb, w_ref):
    mu = jnp.mean(v, axis=-1, keepdims=True)
    vc = v - mu
    rs = lax.rsqrt(jnp.mean(vc * vc, axis=-1, keepdims=True) + EPS)
    xhat = vc * rs
    vn = (xhat * lng + lnb).astype(BF16)
    ri = lax.broadcasted_iota(jnp.int32, (CH, CH), 0)
    ci = lax.broadcasted_iota(jnp.int32, (CH, CH), 1)
    tril = ri >= ci
    ws = [jnp.where(tril, w_ref[g], 0.0).astype(BF16) for g in range(4)]
    return xhat, rs, vn, tril, ws


def _zspec(off):
    return pl.BlockSpec((CH, D), lambda c: (c, off))


def _full(shape):
    return pl.BlockSpec(shape, lambda c: (0,) * len(shape))


def sgu_fwd(z, lng, lnb, w, bfull):
    def body(u_ref, v_ref, g_ref, lng_ref, lnb_ref, w_ref, b_ref, o_ref):
        _, _, vn, _, ws = _sgu_common(v_ref[...], lng_ref[...], lnb_ref[...], w_ref)
        for g in range(4):
            sl = slice(g * 256, (g + 1) * 256)
            mixed = _nn(ws[g], vn[:, sl]) + b_ref[:, sl]
            gc = g_ref[:, sl]
            o_ref[:, sl] = (u_ref[:, sl] * mixed * (gc * _sig(gc))).astype(o_ref.dtype)

    return pl.pallas_call(
        body, name="sgu_fwd", grid=(S // CH,),
        in_specs=[_zspec(0), _zspec(1), _zspec(2), _full((1, D)), _full((1, D)),
                  _full((4, CH, CH)), _full((CH, D))],
        out_specs=pl.BlockSpec((CH, D), lambda c: (c, 0)),
        out_shape=jax.ShapeDtypeStruct((S, D), BF16),
        compiler_params=_cparams(("parallel",)),
    )(z, z, z, lng, lnb, w, bfull)


def sgu_bwd(z, d_cat, lng, lnb, w, bfull):
    def body(u_ref, v_ref, g_ref, d_ref, lng_ref, lnb_ref, w_ref, b_ref,
             du_ref, dv_ref, dg_ref, dw_ref, db_ref, dlg_ref, dlb_ref):
        @pl.when(pl.program_id(0) == 0)
        def _():
            dw_ref[...] = jnp.zeros_like(dw_ref)
            db_ref[...] = jnp.zeros_like(db_ref)
            dlg_ref[...] = jnp.zeros_like(dlg_ref)
            dlb_ref[...] = jnp.zeros_like(dlb_ref)

        lng = lng_ref[...]
        xhat, rs, vn, tril, ws = _sgu_common(v_ref[...], lng, lnb_ref[...], w_ref)
        lane = lax.broadcasted_iota(jnp.int32, (1, 128), 1)
        db = jnp.zeros((CH, 128), F32)
        dvn_parts = []
        for g in range(4):
            sl = slice(g * 256, (g + 1) * 256)
            mixed = _nn(ws[g], vn[:, sl]) + b_ref[:, sl]
            gc = g_ref[:, sl]
            sg = _sig(gc)
            u = u_ref[:, sl]
            dc = d_ref[:, sl]
            du_ref[:, sl] = dc * mixed * (gc * sg)
            dg_ref[:, sl] = dc * u * mixed * (sg * (1.0 + gc * (1.0 - sg)))
            dmx = dc * u * (gc * sg)
            db = db + jnp.where(lane == g, jnp.sum(dmx, axis=-1, keepdims=True), 0.0)
            dmb = dmx.astype(BF16)
            dw_ref[g] += jnp.where(tril, _nt(dmb, vn[:, sl]), 0.0)
            dvn_parts.append(_tn(ws[g], dmb))
        db_ref[...] += db
        dvn = jnp.concatenate(dvn_parts, axis=1)
        dlb_ref[...] += jnp.sum(dvn, axis=0, keepdims=True)
        dlg_ref[...] += jnp.sum(dvn * xhat, axis=0, keepdims=True)
        dxh = dvn * lng
        dv_ref[...] = rs * (dxh - jnp.mean(dxh, axis=-1, keepdims=True)
                            - xhat * jnp.mean(dxh * xhat, axis=-1, keepdims=True))

    row = pl.BlockSpec((CH, D), lambda c: (c, 0))
    return pl.pallas_call(
        body, name="sgu_bwd", grid=(S // CH,),
        in_specs=[_zspec(0), _zspec(1), _zspec(2), row, _full((1, D)), _full((1, D)),
                  _full((4, CH, CH)), _full((CH, D))],
        out_specs=[row, row, row, _full((4, CH, CH)), _full((CH, 128)), _full((1, D)), _full((1, D))],
        out_shape=[jax.ShapeDtypeStruct((S, D), F32)] * 3
        + [jax.ShapeDtypeStruct((4, CH, CH), F32), jax.ShapeDtypeStruct((CH, 128), F32),
           jax.ShapeDtypeStruct((1, D), F32), jax.ShapeDtypeStruct((1, D), F32)],
        compiler_params=_cparams(("arbitrary",)),
    )(z, z, z, d_cat, lng, lnb, w, bfull)


TB = 256


def _cmul(ar, ai, br, bi):
    return ar * br - ai * bi, ar * bi + ai * br


def _scan_consts(ar, ai, reverse):
    a2 = _cmul(ar, ai, ar, ai)
    a4 = _cmul(*a2, *a2)
    row = lax.broadcasted_iota(jnp.int32, (8, NS), 0)

    def masked(k, p):
        keep = (row < 8 - k) if reverse else (row >= k)
        return jnp.where(keep, p[0], 0.0), jnp.where(keep, p[1], 0.0)
    pr = jnp.zeros((8, NS), F32)
    pi = jnp.zeros((8, NS), F32)
    cr, ci = ar, ai
    for r in range(8):
        sel = row == (7 - r if reverse else r)
        pr = jnp.where(sel, cr, pr)
        pi = jnp.where(sel, ci, pi)
        cr, ci = _cmul(cr, ci, ar, ai)
    return (masked(1, (ar, ai)), masked(2, a2), masked(4, a4)), (pr, pi), row


def scan_fwd(bu, abr, abi):
    def body(bu_ref, ar_ref, ai_ref, h_ref, car, cai):
        @pl.when(pl.program_id(0) == 0)
        def _():
            car[...] = jnp.zeros_like(car)
            cai[...] = jnp.zeros_like(cai)

        pows, (pr, pi), row = _scan_consts(ar_ref[...], ai_ref[...], False)

        def tile(t, carry):
            c_r, c_i = carry
            rows = pl.ds(pl.multiple_of(t * 8, 8), 8)
            xr = bu_ref[rows, 0:NS]
            xi = bu_ref[rows, NS:2 * NS]
            for k, (kr, ki) in zip((1, 2, 4), pows):
                sr = pltpu.roll(xr, k, 0)
                si = pltpu.roll(xi, k, 0)
                xr, xi = xr + kr * sr - ki * si, xi + kr * si + ki * sr
            xr, xi = xr + pr * c_r - pi * c_i, xi + pr * c_i + pi * c_r
            h_ref[rows, 0:NS] = xr
            h_ref[rows, NS:2 * NS] = xi
            return (jnp.broadcast_to(xr[7:8, :], (8, NS)), jnp.broadcast_to(xi[7:8, :], (8, NS)))

        c_r, c_i = lax.fori_loop(0, TB // 8, tile, (car[...], cai[...]))
        car[...] = c_r
        cai[...] = c_i

    return pl.pallas_call(
        body, name="s5_scan_fwd", grid=(S // TB,),
        in_specs=[pl.BlockSpec((TB, 2 * NS), lambda i: (i, 0)),
                  pl.BlockSpec((1, NS), lambda i: (0, 0)), pl.BlockSpec((1, NS), lambda i: (0, 0))],
        out_specs=pl.BlockSpec((TB, 2 * NS), lambda i: (i, 0)),
        out_shape=jax.ShapeDtypeStruct((S, 2 * NS), F32),
        scratch_shapes=[pltpu.VMEM((8, NS), F32), pltpu.VMEM((8, NS), F32)],
        compiler_params=_cparams(("arbitrary",)),
    )(bu, abr, abi)


def scan_bwd(eta, h, abr, abi):
    nt = S // TB

    def body(e_ref, h_ref, ar_ref, ai_ref, l_ref, da_ref, car, cai):
        @pl.when(pl.program_id(0) == 0)
        def _():
            car[...] = jnp.zeros_like(car)
            cai[...] = jnp.zeros_like(cai)
            da_ref[...] = jnp.zeros_like(da_ref)

        pows, (pr, pi), row = _scan_consts(ar_ref[...], -ai_ref[...], True)

        def tile(tt, carry):
            c_r, c_i, acr, aci = carry
            t = TB // 8 - 1 - tt
            rows = pl.ds(pl.multiple_of(t * 8, 8), 8)
            xr = e_ref[rows, 0:NS]
            xi = e_ref[rows, NS:2 * NS]
            for k, (kr, ki) in zip((1, 2, 4), pows):
                sr = pltpu.roll(xr, 8 - k, 0)
                si = pltpu.roll(xi, 8 - k, 0)
                xr, xi = xr + kr * sr - ki * si, xi + kr * si + ki * sr
            xr, xi = xr + pr * c_r - pi * c_i, xi + pr * c_i + pi * c_r
            l_ref[rows, 0:NS] = xr
            l_ref[rows, NS:2 * NS] = xi
            nr = jnp.where(row < 7, pltpu.roll(xr, 7, 0), c_r)
            ni = jnp.where(row < 7, pltpu.roll(xi, 7, 0), c_i)
            hr = h_ref[rows, 0:NS]
            hi = h_ref[rows, NS:2 * NS]
            acr = acr + hr * nr + hi * ni
            aci = aci + hr * ni - hi * nr
            return (jnp.broadcast_to(xr[0:1, :], (8, NS)), jnp.broadcast_to(xi[0:1, :], (8, NS)), acr, aci)

        zero = jnp.zeros((8, NS), F32)
        c_r, c_i, acr, aci = lax.fori_loop(0, TB // 8, tile, (car[...], cai[...], zero, zero))
        car[...] = c_r
        cai[...] = c_i
        da_ref[:, 0:NS] += acr
        da_ref[:, NS:2 * NS] += aci

    rev = pl.BlockSpec((TB, 2 * NS), lambda i: (nt - 1 - i, 0))
    return pl.pallas_call(
        body, name="s5_scan_bwd", grid=(nt,),
        in_specs=[rev, rev, pl.BlockSpec((1, NS), lambda i: (0, 0)), pl.BlockSpec((1, NS), lambda i: (0, 0))],
        out_specs=[rev, pl.BlockSpec((8, 2 * NS), lambda i: (0, 0))],
        out_shape=[jax.ShapeDtypeStruct((S, 2 * NS), F32), jax.ShapeDtypeStruct((8, 2 * NS), F32)],
        scratch_shapes=[pltpu.VMEM((8, NS), F32), pltpu.VMEM((8, NS), F32)],
        compiler_params=_cparams(("arbitrary",)),
    )(eta, h, abr, abi)


GC = 0.7978845608028654
GA = 0.044715


def s5_post(hc, z, dskip):
    def fn(hv, xd, dv):
        y = hv + dv * xd
        return y, 0.5 * y * (1.0 + jnp.tanh(GC * (y + GA * y * y * y)))
    return rw(fn, [(hc, 0, 512), (z, 3072, 512)], [(512, F32), (512, BF16)], "s5_post", S, consts=[dskip])


def s5_post_bwd(dyg, ypre, z, dskip):
    def fn(dy, y, xd, dv):
        th = jnp.tanh(GC * (y + GA * y * y * y))
        dg = 0.5 * (1.0 + th) + 0.5 * y * (1.0 - th * th) * GC * (1.0 + 3.0 * GA * y * y)
        dyp = dy * dg
        return dyp, dyp * dv, jnp.sum(dyp * xd, axis=0, keepdims=True)
    return rw(fn, [(dyg, 0, 512), (ypre, 0, 512), (z, 3072, 512)], [(512, BF16), (512, F32)],
              "s5_post_bwd", S, consts=[dskip], accs=[(1, 512)])


def glu_fwd(t, z, c_out):
    def fn(t1, t2, gd, co):
        return (jnp.concatenate([co, (t1 * _sig(t2) * (gd * _sig(gd))).astype(BF16)], axis=1),)
    return rw(fn, [(t, 0, 512), (t, 512, 512), (z, 3584, 512), (c_out, 0, D)], [(D + 512, BF16)], "glu_fwd", S)[0]


def glu_bwd(t, z, d_cat):
    def fn(t1, t2, gd, dd):
        s2, sg = _sig(t2), _sig(gd)
        sl = gd * sg
        return (jnp.concatenate([dd * s2 * sl, dd * t1 * s2 * (1.0 - s2) * sl], axis=1),
                dd * t1 * s2 * (sg * (1.0 + gd * (1.0 - sg))))
    return rw(fn, [(t, 0, 512), (t, 512, 512), (z, 3584, 512), (d_cat, 1024, 512)],
              [(D, BF16), (512, F32)], "glu_bwd", S)


def assemble_dz_odd(du, dv, dgc, dxd, dgd):
    def body(a, b, c, d, e, o_ref):
        o_ref[:, 0:D] = a[...].astype(BF16)
        o_ref[:, D:2 * D] = b[...].astype(BF16)
        o_ref[:, 2 * D:3 * D] = c[...].astype(BF16)
        o_ref[:, 3 * D:3 * D + 512] = d[...].astype(BF16)
        o_ref[:, 3 * D + 512:4 * D] = e[...].astype(BF16)
    tr = 512
    blk = pl.BlockSpec((tr, D), lambda i: (i, 0))
    half = pl.BlockSpec((tr, 512), lambda i: (i, 0))
    return pl.pallas_call(
        body, name="assemble_dz_odd", grid=(S // tr,), in_specs=[blk, blk, blk, half, half],
        out_specs=pl.BlockSpec((tr, 4 * D), lambda i: (i, 0)),
        out_shape=jax.ShapeDtypeStruct((S, 4 * D), BF16),
        compiler_params=_cparams(("parallel",)),
    )(du, dv, dgc, dxd, dgd)


TQ = 1024


def _xattn_probs(qh, kh):
    s = _nt(qh, kh) * 0.0625
    p = jnp.exp(s - jnp.max(s, axis=-1, keepdims=True))
    return p / jnp.sum(p, axis=-1, keepdims=True)


def xattn_fwd(q, kv):
    def body(q_ref, kv_ref, o_ref):
        outs = []
        for h in range(4):
            sl = slice(h * 256, (h + 1) * 256)
            p = _xattn_probs(q_ref[:, sl].astype(BF16), kv_ref[:, sl].astype(BF16))
            vh = kv_ref[:, D + h * 256:D + (h + 1) * 256].astype(BF16)
            outs.append((sl, _nn(p.astype(BF16), vh)))
        for sl, o in outs:
            o_ref[:, sl] = o.astype(o_ref.dtype)

    return pl.pallas_call(
        body, name="xattn_fwd", grid=(S // TQ,),
        in_specs=[pl.BlockSpec((TQ, D), lambda i: (i, 0)), pl.BlockSpec((MEM, 2 * D), lambda i: (0, 0))],
        out_specs=pl.BlockSpec((TQ, D), lambda i: (i, 0)),
        out_shape=jax.ShapeDtypeStruct((S, D), BF16),
        compiler_params=_cparams(("parallel",)),
    )(q, kv)


def xattn_bwd(q, kv, d_o):
    def body(q_ref, kv_ref, do_ref, dq_ref, dkv_ref):
        @pl.when(pl.program_id(0) == 0)
        def _():
            dkv_ref[...] = jnp.zeros_like(dkv_ref)

        done = []
        for h in range(4):
            sl = slice(h * 256, (h + 1) * 256)
            vs = slice(D + h * 256, D + (h + 1) * 256)
            qh = q_ref[:, sl].astype(BF16)
            kh = kv_ref[:, sl].astype(BF16)
            vh = kv_ref[:, vs].astype(BF16)
            doh = do_ref[:, sl].astype(BF16)
            p = _xattn_probs(qh, kh)
            dp = _nt(doh, vh)
            ds = (p * (dp - jnp.sum(p * dp, axis=-1, keepdims=True)) * 0.0625).astype(BF16)
            done.append((sl, vs, _nn(ds, kh), _tn(ds, qh), _tn(p.astype(BF16), doh)))
        for sl, vs, dq, dk, dv in done:
            dq_ref[:, sl] = dq.astype(dq_ref.dtype)
            dkv_ref[:, sl] += dk
            dkv_ref[:, vs] += dv

    return pl.pallas_call(
        body, name="xattn_bwd", grid=(S // TQ,),
        in_specs=[pl.BlockSpec((TQ, D), lambda i: (i, 0)), pl.BlockSpec((MEM, 2 * D), lambda i: (0, 0)),
                  pl.BlockSpec((TQ, D), lambda i: (i, 0))],
        out_specs=[pl.BlockSpec((TQ, D), lambda i: (i, 0)), pl.BlockSpec((MEM, 2 * D), lambda i: (0, 0))],
        out_shape=[jax.ShapeDtypeStruct((S, D), BF16), jax.ShapeDtypeStruct((MEM, 2 * D), F32)],
        compiler_params=_cparams(("arbitrary",)),
    )(q, kv, d_o)


def _s5_disc(a_re, a_im, log_dt, b_re, b_im):
    dt = jnp.exp(log_dt)[:, None]
    mag = jnp.exp(dt * a_re)
    abr = mag * jnp.cos(dt * a_im)
    abi = mag * jnp.sin(dt * a_im)
    nr, ni = abr - 1.0, abi
    inv = 1.0 / (a_re * a_re + a_im * a_im)
    cr = (nr * a_re + ni * a_im) * inv
    ci = (ni * a_re - nr * a_im) * inv
    bbr = cr[..., None] * b_re - ci[..., None] * b_im
    bbi = cr[..., None] * b_im + ci[..., None] * b_re
    return abr, abi, bbr, bbi


VM = pl.BlockSpec(memory_space=pltpu.VMEM)


def s5_embed(bt_re, bt_im, ct_re, ct_im):
    def body(br, bi, cr, ci, b_ref, c_ref):
        b_ref[...] = jnp.zeros_like(b_ref)
        c_ref[...] = jnp.zeros_like(c_ref)
        for g in range(NG):
            rows, cols = slice(g * NH, (g + 1) * NH), slice(g * NP, (g + 1) * NP)
            b_ref[rows, cols] = br[g]
            b_ref[rows, NS + g * NP:NS + (g + 1) * NP] = bi[g]
            c_ref[cols, rows] = cr[g]
            c_ref[NS + g * NP:NS + (g + 1) * NP, rows] = -ci[g]

    return pl.pallas_call(
        body, name="s5_embed", in_specs=[VM] * 4, out_specs=[VM] * 2,
        out_shape=[jax.ShapeDtypeStruct((NG * NH, 2 * NS), F32), jax.ShapeDtypeStruct((2 * NS, NG * NH), F32)],
        compiler_params=pltpu.CompilerParams(vmem_limit_bytes=VMEM_LIMIT),
    )(bt_re, bt_im, ct_re, ct_im)


def s5_extract(gb, gc):
    def body(gb_ref, gc_ref, br, bi, cr, ci):
        for g in range(NG):
            rows, cols = slice(g * NH, (g + 1) * NH), slice(g * NP, (g + 1) * NP)
            br[g] = gb_ref[rows, cols]
            bi[g] = gb_ref[rows, NS + g * NP:NS + (g + 1) * NP]
            cr[g] = gc_ref[cols, rows]
            ci[g] = -gc_ref[NS + g * NP:NS + (g + 1) * NP, rows]

    return pl.pallas_call(
        body, name="s5_extract", in_specs=[VM] * 2, out_specs=[VM] * 4,
        out_shape=[jax.ShapeDtypeStruct((NG, NH, NP), F32)] * 2 + [jax.ShapeDtypeStruct((NG, NP, NH), F32)] * 2,
        compiler_params=pltpu.CompilerParams(vmem_limit_bytes=VMEM_LIMIT),
    )(gb, gc)


HC, HS = NG * NH // 2, NS // 2
TS = 1024


def s5_to_states(x, w, mode, name, z_off=0):
    if mode == "nn":
        wb, wm = (HC, HS), lambda i, j, kk: (j % 2, j)
    else:
        wb, wm = (HS, HC), lambda i, j, kk: (j, j % 2)
    return mm_band(x, w, mode, name, (S // TS, 4, 1), ((TS, HC), wb, (TS, HS)),
                   (lambda i, j, kk: (i, z_off + j % 2), wm, lambda i, j, kk: (i, j)), (S, 2 * NS))


def s5_to_channels(x, w, mode, name, add=None):
    if mode == "nn":
        wb, wm = (HS, HC), lambda i, j, kk: (j + 2 * kk, j)
    else:
        wb, wm = (HC, HS), lambda i, j, kk: (j, j + 2 * kk)
    return mm_band(x, w, mode, name, (S // TS, 2, 2), ((TS, HS), wb, (TS, HC)),
                   (lambda i, j, kk: (i, j + 2 * kk), wm, lambda i, j, kk: (i, j)), (S, NG * NH), add=add)


def s5_outer(a, b, name, states_first, z_off=0):
    if states_first:
        return mm_band(a, b, "tn", name, (4, 1, 1), ((S, HS), (S, HC), (HS, HC)),
                       (lambda i, j, kk: (0, i), lambda i, j, kk: (0, i % 2), lambda i, j, kk: (i, i % 2)),
                       (2 * NS, NG * NH))
    return mm_band(a, b, "tn", name, (1, 4, 1), ((S, HC), (S, HS), (HC, HS)),
                   (lambda i, j, kk: (0, z_off + j % 2), lambda i, j, kk: (0, j), lambda i, j, kk: (j % 2, j)),
                   (NG * NH, 2 * NS))


def _fwd_even(i, x, P, W):
    hn = rms_fwd(x, P["norm_ab"][i:i + 1], "rms_ab_fwd")
    z = mm(m2(hn), W["w_in"], "nn", "in_ab")
    o, lse, cat = attn_fwd(z)
    if "more" in W:
        W.update(W.pop("more")(cat))
    cat = pool_fwd(z, W["pool_w"], P["pool_scale"][i:i + 1], cat)
    x_mid = mm(m2(cat), W["w_out"], "nn", "out_ab", add=m2(x))
    return x_mid, dict(x=x, hn=hn, z=z, o=o, lse=lse, cat=cat)


def _bwd_even(i, dx_mid, sv, P, W, G, GW):
    z = sv["z"]
    d_cat = mm(m2(dx_mid), W["w_out"], "nt", "out_ab_dx")
    GW["w_out"] = mm(m2(sv["cat"]), m2(dx_mid), "tn", "out_ab_dw").reshape(4, 512, D)
    dq, dk, dv, dga = attn_bwd(z, d_cat, sv["o"], sv["lse"])
    dvb, dgb, dpw, dps = pool_bwd(z, d_cat, W["pool_w"], P["pool_scale"][i:i + 1])
    GW["pool_w"] = dpw.reshape(4, 4, 64, 256).transpose(1, 0, 2, 3).reshape(4, 256, 256)
    G["pool_scale"][i] = dps[0]
    d_z = assemble_dz_even((dq, dk, dv, dga, dvb, dgb))
    d_hn = mm(m2(d_z), W["w_in"], "nt", "in_ab_dx")
    GW["w_in"] = mm(m2(sv["hn"]), m2(d_z), "tn", "in_ab_dw", out=outcs(D, 1536))
    return d_hn, P["norm_ab"][i:i + 1], "norm_ab", "rms_ab_bwd"


def _fwd_odd(i, x, P, W):
    hn = rms_fwd(x, P["norm_cd"][i:i + 1], "rms_cd_fwd")
    z = mm(m2(hn), W["w_in"], "nn", "in_cd")
    bfull = jnp.repeat(P["sgu_b"][i].T, 256, axis=1)
    c_out = sgu_fwd(z, P["sgu_ln_g"][i:i + 1], P["sgu_ln_b"][i:i + 1], P["sgu_w"][i], bfull)
    disc, disc_vjp = jax.vjp(_s5_disc, P["s5_a_re"][i], P["s5_a_im"][i], P["s5_log_dt"][i],
                             P["s5_b_re"][i], P["s5_b_im"][i])
    abr, abi, bbr, bbi = disc
    bbd, cbd = s5_embed(bbr.transpose(0, 2, 1), bbi.transpose(0, 2, 1),
                        P["s5_c_re"][i].transpose(0, 2, 1), P["s5_c_im"][i].transpose(0, 2, 1))
    abr, abi = abr.reshape(1, NS), abi.reshape(1, NS)
    bu = s5_to_states(z, bbd, "nn", "s5_bu", z_off=3072 // HC)
    h = scan_fwd(bu, abr, abi)
    hc = s5_to_channels(h, cbd, "nn", "s5_hc")
    dskip = P["s5_d"][i:i + 1]
    ypre, yg = s5_post(hc, z, dskip)
    if "more" in W:
        W.update(W.pop("more")(yg))
    w12 = W["w12"]
    t = mm(m2(yg), m2(w12), "nn", "glu_t")
    cat = glu_fwd(t, z, c_out)
    x_mid = mm(m2(cat), W["w_out"], "nn", "out_cd", add=m2(x))
    return x_mid, dict(x=x, hn=hn, z=z, bfull=bfull, disc_vjp=disc_vjp, bbd=bbd, cbd=cbd, abr=abr,
                       abi=abi, h=h, ypre=ypre, yg=yg, w12=w12, t=t, cat=cat, dskip=dskip)


def _bwd_odd(i, dx_mid, sv, P, W, G, GW):
    z = sv["z"]
    d_cat = mm(m2(dx_mid), W["w_out"], "nt", "out_cd_dx")
    GW["w_out"] = mm(m2(sv["cat"]), m2(dx_mid), "tn", "out_cd_dw").reshape(4, 384, D)
    du, dv, dgc, dws, dbs, dlg, dlb = sgu_bwd(z, d_cat, P["sgu_ln_g"][i:i + 1], P["sgu_ln_b"][i:i + 1],
                                               P["sgu_w"][i], sv["bfull"])
    G["sgu_w"][i], G["sgu_b"][i] = dws, dbs[:, :4].T
    G["sgu_ln_g"][i], G["sgu_ln_b"][i] = dlg[0], dlb[0]
    dt, dgd = glu_bwd(sv["t"], z, d_cat)
    gw12 = mm(m2(sv["yg"]), m2(dt), "tn", "glu_dw")
    GW["glu_w1"] = gw12[:, :512].reshape(4, 128, 512)
    GW["glu_w2"] = gw12[:, 512:].reshape(4, 128, 512)
    dyg = mm(m2(dt), m2(sv["w12"]), "nt", "glu_dx")
    dypre, dxd1, dd = s5_post_bwd(dyg, sv["ypre"], z, sv["dskip"])
    G["s5_d"][i] = dd[0]
    gcbd = s5_outer(sv["h"], dypre, "s5_dc", states_first=True)
    eta = s5_to_states(dypre, sv["cbd"], "nt", "s5_eta")
    lam, dacc = scan_bwd(eta, sv["h"], sv["abr"], sv["abi"])
    gbbd = s5_outer(z, lam, "s5_db", states_first=False, z_off=3072 // HC)
    dxd = s5_to_channels(lam, sv["bbd"], "nt", "s5_dx", add=dxd1)
    dacc = jnp.sum(dacc, axis=0)
    dbt_re, dbt_im, dct_re, dct_im = s5_extract(gbbd, gcbd)
    G["s5_c_re"][i], G["s5_c_im"][i] = dct_re.transpose(0, 2, 1), dct_im.transpose(0, 2, 1)
    d_bbr, d_bbi = dbt_re.transpose(0, 2, 1), dbt_im.transpose(0, 2, 1)
    (G["s5_a_re"][i], G["s5_a_im"][i], G["s5_log_dt"][i], G["s5_b_re"][i], G["s5_b_im"][i]) = sv["disc_vjp"](
        (dacc[:NS].reshape(NG, NP), dacc[NS:].reshape(NG, NP), d_bbr, d_bbi))
    d_z = assemble_dz_odd(du, dv, dgc, dxd, dgd)
    d_hn = mm(m2(d_z), W["w_in"], "nt", "in_cd_dx")
    GW["w_in"] = mm(m2(sv["hn"]), m2(d_z), "tn", "in_cd_dw", out=outcs(D, 1024))
    return d_hn, P["norm_cd"][i:i + 1], "norm_cd", "rms_cd_bwd"


def _fwd_x(l, x, mem_n, P, W):
    hx = rms_fwd(x, P["norm_x"][l:l + 1], "rms_x_fwd")
    q = mm(m2(hx), W["w_xq"], "nn", "xq", out_dtype=BF16)
    kv = mm(m2(mem_n), W["w_xkv"], "nn", "xkv", out_dtype=BF16)
    ox = xattn_fwd(q, kv)
    x_out = mm(m2(ox), W["w_xo"], "nn", "xo", add=m2(x))
    return x_out, dict(x=x, hx=hx, q=q, kv=kv, ox=ox)


def _bwd_x(l, dx_out, sv, mem_n, d_memn, P, W, G, GW):
    d_ox = mm(m2(dx_out), W["w_xo"], "nt", "xo_dx", out_dtype=BF16)
    GW["w_xo"] = mm(m2(sv["ox"]), m2(dx_out), "tn", "xo_dw").reshape(4, 256, D)
    dq, dkv = xattn_bwd(sv["q"], sv["kv"], d_ox)
    GW["w_xq"] = mm(m2(sv["hx"]), m2(dq), "tn", "xq_dw").reshape(4, 256, D)
    d_hx = mm(m2(dq), W["w_xq"], "nt", "xq_dx")
    GW["w_xkv"] = mm(m2(mem_n), m2(dkv), "tn", "xkv_dw", out=outcs(D, 512))
    d_memn = mm(m2(dkv), W["w_xkv"], "nt", "xkv_dx", add=None if d_memn is None else m2(d_memn))
    dx, dg = rms_bwd(sv["x"], d_hx, dx_out, P["norm_x"][l:l + 1], "rms_x_bwd")
    G["norm_x"][l] = dg[0]
    return dx, d_memn


SMALL_LAYERS = (("norm_ab", 2), ("pool_scale", 2), ("norm_cd", 2), ("sgu_ln_g", 2), ("sgu_ln_b", 2), ("sgu_w", 2),
                ("sgu_b", 2), ("s5_a_re", 2), ("s5_a_im", 2), ("s5_log_dt", 2), ("s5_b_re", 2), ("s5_b_im", 2),
                ("s5_c_re", 2), ("s5_c_im", 2), ("s5_d", 2), ("norm_x", 4))


def local_step(x, mem, tgt, P, weights_of, grads_done):
    G = {k: [None] * n for k, n in SMALL_LAYERS}
    mem_g = P["mem_norm"].reshape(1, D)
    mem_n = rms_fwd(mem, mem_g, "rms_mem_fwd")
    saved = []
    for layer in range(4):
        i = layer // 2
        W = weights_of(layer, x)
        x, sv_m = (_fwd_even if layer % 2 == 0 else _fwd_odd)(i, x, P, W)
        x, sv_x = _fwd_x(layer, x, mem_n, P, W)
        saved.append((sv_m, sv_x, W))
    dx, loss, dgf = final_loss(x, tgt, P["final_norm"].reshape(1, D))
    G["final_norm"] = dgf[0]
    d_memn = None
    for layer in reversed(range(4)):
        i = layer // 2
        sv_m, sv_x, W = saved[layer]
        GW = {}
        dx_mid, d_memn = _bwd_x(layer, dx, sv_x, mem_n, d_memn, P, W, G, GW)
        d_hn, g, key, name = (_bwd_even if layer % 2 == 0 else _bwd_odd)(i, dx_mid, sv_m, P, W, G, GW)
        token = grads_done(layer, GW)
        if token is not None:
            g = g + token
        dx, dg = rms_bwd(sv_m["x"], d_hn, dx_mid, g, name)
        G[key][i] = dg[0]
    _, dgm = rms_bwd(mem, d_memn, d_memn, mem_g, "rms_mem_bwd")
    G["mem_norm"] = dgm[0]
    return loss, dx, G


ANY = pl.BlockSpec(memory_space=pl.ANY)


def _place():
    x, y, c = lax.axis_index("x"), lax.axis_index("y"), lax.axis_index("c")
    chips = [(1 - x, y), (x, 1 - y), (1 - x, 1 - y)]
    return x, y, c, 2 * x + y, (x, y, 1 - c), chips


def _remote(src, dst, send, recv, k, dev):
    return pltpu.make_async_remote_copy(src_ref=src, dst_ref=dst, send_sem=send.at[k], recv_sem=recv.at[k],
                                        device_id=dev, device_id_type=MESHID)


HBM = pl.BlockSpec(memory_space=pltpu.HBM)
SEM = pl.BlockSpec(memory_space=pltpu.SEMAPHORE)
EFFECT = pltpu.SideEffectType.DATAFLOW_SIDE_EFFECTING


def _hbm(t):
    return pltpu.with_memory_space_constraint(t, pltpu.HBM)


def allgather_sync(shards):
    n = len(shards)

    def body(*refs):
        ins, outs = refs[:n], refs[n:2 * n]
        token, send, recv = refs[2 * n:]
        x, y, c, jme, sib, chips = _place()
        first, passed = [], []
        for a in range(n):
            cp = _remote(ins[a], outs[a].at[jme], send, recv, a * 7 + 6, sib)
            cp.start()
            first.append(cp)
            for k, chip in enumerate(chips):
                cp = _remote(ins[a].at[c], outs[a].at[jme, c], send, recv, a * 7 + k, (*chip, c))
                cp.start()
                first.append(cp)
        for a in range(n):
            for k, chip in enumerate(chips):
                piece = outs[a].at[2 * chip[0] + chip[1], c]
                _remote(piece, piece, send, recv, a * 7 + k, (*chip, c)).wait_recv()
                fw = _remote(piece, piece, send, recv, a * 7 + 3 + k, sib)
                fw.start()
                passed.append(fw)
        for a in range(n):
            own = outs[a].at[jme]
            _remote(own, own, send, recv, a * 7 + 6, sib).wait_recv()
            for k, chip in enumerate(chips):
                piece = outs[a].at[2 * chip[0] + chip[1], 1 - c]
                _remote(piece, piece, send, recv, a * 7 + 3 + k, sib).wait_recv()
        for cp in first + passed:
            cp.wait_send()
        token[...] = jnp.zeros_like(token)

    res = pl.pallas_call(
        body, name="allgather_sync", in_specs=[ANY] * n,
        out_specs=[ANY] * n + [pl.BlockSpec(memory_space=pltpu.VMEM)],
        out_shape=[jax.ShapeDtypeStruct((4,) + s.shape, s.dtype) for s in shards] + [jax.ShapeDtypeStruct((8, 128), F32)],
        scratch_shapes=[pltpu.SemaphoreType.DMA((7 * n,)), pltpu.SemaphoreType.DMA((7 * n,))],
    )(*shards)
    return list(res[:n]), res[n]


def _gather_copies(ins, lands, send, recv):
    x, y, c, jme, sib, chips = _place()
    devs = [(*chip, c) for chip in chips] + [sib]
    return [_remote(ins[a], lands[a].at[jme], send, recv, a * 4 + k, dev)
            for a in range(len(ins)) for k, dev in enumerate(devs)]


def allgather_start(shards, after, name):
    n, na = len(shards), len(after)

    def body(*refs):
        ins, lands = refs[:n], refs[n:2 * n]
        send, recv = refs[2 * n + na], refs[2 * n + na + 1]
        token = refs[-1]
        for cp in _gather_copies(ins, lands, send, recv):
            cp.start()
        token[...] = jnp.zeros_like(token)

    res = pl.pallas_call(
        body, name=name,
        out_shape=(pltpu.SemaphoreType.DMA((4 * n,)), pltpu.SemaphoreType.DMA((4 * n,)),
                   *[pltpu.HBM(s.shape, s.dtype) for s in shards],
                   *[pltpu.HBM((4,) + s.shape, s.dtype) for s in shards],
                   jax.ShapeDtypeStruct((8, 128), F32)),
        in_specs=[HBM] * (2 * n) + [ANY] * na,
        out_specs=(SEM, SEM, *[HBM] * (2 * n), pl.BlockSpec(memory_space=pltpu.VMEM)),
        input_output_aliases={a: 2 + a for a in range(2 * n)},
        compiler_params=pltpu.CompilerParams(has_side_effects=EFFECT),
    )(*[_hbm(s) for s in shards], *[_hbm(lax.empty((4,) + s.shape, s.dtype)) for s in shards], *after)
    return res[0], res[1], list(res[2:2 + n]), list(res[2 + n:2 + 2 * n]), res[-1]


def allgather_wait(send, recv, shards, lands, after, name):
    n = len(shards)

    def body(*refs):
        ins, zones = refs[:n], refs[n:2 * n]
        send_r, recv_r = refs[2 * n], refs[2 * n + 1]
        x, y, c, jme, sib, chips = _place()
        slots = [2 * chip[0] + chip[1] for chip in chips] + [jme]
        for a in range(n):
            for k, slot in enumerate(slots):
                cp = _remote(ins[a], zones[a].at[slot], send_r, recv_r, a * 4 + k, sib)
                cp.wait_send()
                cp.wait_recv()

    res = pl.pallas_call(
        body, name=name,
        out_shape=tuple(pltpu.HBM(t.shape, t.dtype) for t in list(shards) + list(lands)),
        in_specs=[HBM] * (2 * n) + [SEM, SEM, ANY], out_specs=tuple([HBM] * (2 * n)),
        input_output_aliases={a: a for a in range(2 * n)},
        compiler_params=pltpu.CompilerParams(has_side_effects=EFFECT),
    )(*shards, *lands, send, recv, after)
    return list(res[n:])


def allgather_small(slab):
    def body(in_ref, out_ref, send, recv, lsem):
        x, y, c, jme, sib, chips = _place()
        loc = pltpu.make_async_copy(in_ref, out_ref.at[jme], lsem.at[0])
        loc.start()
        cps = [_remote(in_ref, out_ref.at[jme], send, recv, k, (*chip, c)) for k, chip in enumerate(chips)]
        for cp in cps:
            cp.start()
        for k, chip in enumerate(chips):
            piece = out_ref.at[2 * chip[0] + chip[1]]
            _remote(piece, piece, send, recv, k, (*chip, c)).wait_recv()
        for cp in cps:
            cp.wait_send()
        loc.wait()

    return pl.pallas_call(
        body, name="allgather_small", in_specs=[ANY], out_specs=ANY,
        out_shape=jax.ShapeDtypeStruct((4,) + slab.shape, slab.dtype),
        scratch_shapes=[pltpu.SemaphoreType.DMA((3,)), pltpu.SemaphoreType.DMA((3,)), pltpu.SemaphoreType.DMA((1,))],
    )(slab)


def allreduce_small(v):
    hr = v.shape[0] // 2

    def body(v_ref, o_ref, r0, r1, r2, send, recv):
        x, y, c, jme, sib, chips = _place()
        mine = pl.ds(pl.multiple_of(c * hr, 8), hr)
        other = pl.ds(pl.multiple_of((1 - c) * hr, 8), hr)
        cp = _remote(v_ref.at[other], r0, send, recv, 0, sib)
        cp.start()
        cp.wait()
        o_ref[mine, :] = v_ref[mine, :] + r0[...]
        for k, (buf, peer) in enumerate(((r1, (1 - x, y, c)), (r2, (x, 1 - y, c))), start=1):
            cp = _remote(o_ref.at[mine], buf, send, recv, k, peer)
            cp.start()
            cp.wait()
            o_ref[mine, :] = o_ref[mine, :] + buf[...]
        cp = _remote(o_ref.at[mine], o_ref.at[mine], send, recv, 3, sib)
        cp.start()
        cp.wait_send()
        _remote(o_ref.at[other], o_ref.at[other], send, recv, 3, sib).wait_recv()

    vm = pl.BlockSpec(memory_space=pltpu.VMEM)
    half = pltpu.VMEM((hr, v.shape[1]), v.dtype)
    return pl.pallas_call(
        body, name="allreduce_small", in_specs=[vm], out_specs=vm,
        out_shape=jax.ShapeDtypeStruct(v.shape, v.dtype),
        scratch_shapes=[half] * 3 + [pltpu.SemaphoreType.DMA((4,)), pltpu.SemaphoreType.DMA((4,))],
        compiler_params=pltpu.CompilerParams(vmem_limit_bytes=VMEM_LIMIT),
    )(v)


def _pair_copies(gs, lands, send, recv):
    x, y, c, jme, sib, chips = _place()
    return [_remote(gs[a].at[:, 1 - c], lands[a], send, recv, a, sib) for a in range(len(gs))]


def rs_pair_start(gs, name):
    n = len(gs)

    def body(*refs):
        ins, lands = refs[:n], refs[n:2 * n]
        send, recv = refs[2 * n], refs[2 * n + 1]
        token = refs[-1]
        for cp in _pair_copies(ins, lands, send, recv):
            cp.start()
        token[...] = jnp.zeros_like(token)

    shapes = [(4,) + g.shape[2:] for g in gs]
    res = pl.pallas_call(
        body, name=name,
        out_shape=(pltpu.SemaphoreType.DMA((n,)), pltpu.SemaphoreType.DMA((n,)),
                   *[pltpu.HBM(g.shape, g.dtype) for g in gs], *[pltpu.HBM(s, F32) for s in shapes],
                   jax.ShapeDtypeStruct((8, 128), F32)),
        in_specs=[HBM] * (2 * n), out_specs=(SEM, SEM, *[HBM] * (2 * n), pl.BlockSpec(memory_space=pltpu.VMEM)),
        input_output_aliases={a: 2 + a for a in range(2 * n)},
        compiler_params=pltpu.CompilerParams(has_side_effects=EFFECT),
    )(*[_hbm(g) for g in gs], *[_hbm(lax.empty(s, F32)) for s in shapes])
    return res[0], res[1], list(res[2:2 + n]), list(res[2 + n:2 + 2 * n]), res[-1]


def rs_pair_wait(send, recv, gs, lands, after, name):
    n = len(gs)

    def body(*refs):
        ins, zones = refs[:n], refs[n:2 * n]
        for cp in _pair_copies(ins, zones, refs[2 * n], refs[2 * n + 1]):
            cp.wait_send()
            cp.wait_recv()

    res = pl.pallas_call(
        body, name=name,
        out_shape=tuple(pltpu.HBM(t.shape, t.dtype) for t in list(gs) + list(lands)),
        in_specs=[HBM] * (2 * n) + [SEM, SEM, ANY], out_specs=tuple([HBM] * (2 * n)),
        input_output_aliases={a: a for a in range(2 * n)},
        compiler_params=pltpu.CompilerParams(has_side_effects=EFFECT),
    )(*gs, *lands, send, recv, after)
    return list(res[:n]), list(res[n:])


SUM_ROWS = 256


def rs_pair_sum(g4s, gots, cidx):
    n = len(g4s)
    tiles = [(min(g.shape[2], SUM_ROWS), g.shape[3]) for g in g4s]
    nts = [g.shape[2] // tr for g, (tr, _) in zip(g4s, tiles)]

    def at(a, s):
        s = jnp.minimum(s, 4 * nts[a] - 1)
        return s // nts[a], s % nts[a]

    def body(c_ref, *refs):
        for a in range(n):
            refs[2 * n + a][...] = (refs[a][...] + refs[n + a][...]).astype(BF16)

    in_specs = [pl.BlockSpec((None, None) + tiles[a], lambda s, cr, a=a: (at(a, s)[0], cr[0], at(a, s)[1], 0))
                for a in range(n)]
    in_specs += [pl.BlockSpec((None,) + tiles[a], lambda s, cr, a=a: (*at(a, s), 0)) for a in range(n)]
    return pl.pallas_call(
        body, name="rs_pair_sum",
        grid_spec=pltpu.PrefetchScalarGridSpec(
            num_scalar_prefetch=1, grid=(4 * max(nts),), in_specs=in_specs,
            out_specs=[pl.BlockSpec((None,) + tiles[a], lambda s, cr, a=a: (*at(a, s), 0)) for a in range(n)]),
        out_shape=[jax.ShapeDtypeStruct((4,) + g.shape[2:], BF16) for g in g4s],
        compiler_params=_cparams(("arbitrary",)),
    )(cidx, *g4s, *gots)


def _chip_copies(ps, lands, send, recv):
    x, y, c, jme, sib, chips = _place()
    return [_remote(ps[a].at[2 * chip[0] + chip[1]], lands[a].at[jme], send, recv, a * 3 + k, (*chip, c))
            for a in range(len(ps)) for k, chip in enumerate(chips)]


def rs_chip_start(ps, name):
    n = len(ps)

    def body(*refs):
        ins, lands = refs[:n], refs[n:2 * n]
        send, recv = refs[2 * n], refs[2 * n + 1]
        token = refs[-1]
        for cp in _chip_copies(ins, lands, send, recv):
            cp.start()
        token[...] = jnp.zeros_like(token)

    res = pl.pallas_call(
        body, name=name,
        out_shape=(pltpu.SemaphoreType.DMA((3 * n,)), pltpu.SemaphoreType.DMA((3 * n,)),
                   *[pltpu.HBM(p.shape, p.dtype) for p in ps], *[pltpu.HBM(p.shape, p.dtype) for p in ps],
                   jax.ShapeDtypeStruct((8, 128), F32)),
        in_specs=[HBM] * (2 * n), out_specs=(SEM, SEM, *[HBM] * (2 * n), pl.BlockSpec(memory_space=pltpu.VMEM)),
        input_output_aliases={a: 2 + a for a in range(2 * n)},
        compiler_params=pltpu.CompilerParams(has_side_effects=EFFECT),
    )(*[_hbm(p) for p in ps], *[_hbm(lax.empty(p.shape, p.dtype)) for p in ps])
    return res[0], res[1], list(res[2:2 + n]), list(res[2 + n:2 + 2 * n]), res[-1]


def rs_chip_wait(send, recv, ps, lands, after, name):
    n = len(ps)

    def body(*refs):
        ins, zones = refs[:n], refs[n:2 * n]
        send_r, recv_r = refs[2 * n], refs[2 * n + 1]
        x, y, c, jme, sib, chips = _place()
        for a in range(n):
            for k, chip in enumerate(chips):
                jt = 2 * chip[0] + chip[1]
                cp = _remote(ins[a].at[jt], zones[a].at[jt], send_r, recv_r, a * 3 + k, (*chip, c))
                cp.wait_send()
                cp.wait_recv()

    res = pl.pallas_call(
        body, name=name,
        out_shape=tuple(pltpu.HBM(p.shape, p.dtype) for p in list(ps) + list(lands)),
        in_specs=[HBM] * (2 * n) + [SEM, SEM] + [ANY] * len(after), out_specs=tuple([HBM] * (2 * n)),
        input_output_aliases={a: a for a in range(2 * n)},
        compiler_params=pltpu.CompilerParams(has_side_effects=EFFECT),
    )(*ps, *lands, send, recv, *after)
    return list(res[:n]), list(res[n:])


def rs_chip_sum(qs, ps, ls, accs, layers, jc):
    n = len(qs)
    tiles = [(min(q.shape[1], SUM_ROWS), q.shape[2]) for q in qs]
    nts = [q.shape[1] // tr for q, (tr, _) in zip(qs, tiles)]

    def at(a, s):
        return jnp.minimum(s, nts[a] - 1)

    def body(jc_ref, *refs):
        jme = jc_ref[0]
        for a in range(n):
            q_ref, p_ref, o_ref = refs[a], refs[n + a], refs[len(refs) - n + a]
            own = p_ref[...].astype(F32)
            v = [jnp.where(jme == j, own, q_ref[j].astype(F32)) for j in range(4)]
            o_ref[...] = ((v[0] + v[1]) + v[2]) + v[3]

    in_specs = [pl.BlockSpec((4,) + tiles[a], lambda s, jr, a=a: (0, at(a, s), 0)) for a in range(n)]
    in_specs += [pl.BlockSpec((None,) + tiles[a], lambda s, jr, a=a: (jr[0], at(a, s), 0)) for a in range(n)]
    args, aliases = [jc, *qs, *ps], {}
    for a in range(n):
        if accs[a] is not None:
            aliases[len(args)] = a
            in_specs.append(ANY)
            args.append(accs[a])
    return pl.pallas_call(
        body, name="rs_chip_sum",
        grid_spec=pltpu.PrefetchScalarGridSpec(
            num_scalar_prefetch=1, grid=(max(nts),), in_specs=in_specs,
            out_specs=[pl.BlockSpec((None, None) + tiles[a], lambda s, jr, a=a: (ls[a], jr[1], at(a, s), 0))
                       for a in range(n)]),
        out_shape=[jax.ShapeDtypeStruct((layers[a], 2) + qs[a].shape[1:], F32) for a in range(n)],
        input_output_aliases=aliases,
        compiler_params=_cparams(("arbitrary",)),
    )(*args)


def rs_pair_gather(rs):
    n = len(rs)

    def body(*refs):
        outs = refs[n:2 * n]
        send, recv = refs[2 * n:]
        x, y, c, jme, sib, chips = _place()
        cps = [_remote(outs[a].at[:, c], outs[a].at[:, c], send, recv, a, sib) for a in range(n)]
        for cp in cps:
            cp.start()
        for a in range(n):
            slot = outs[a].at[:, 1 - c]
            _remote(slot, slot, send, recv, a, sib).wait_recv()
        for cp in cps:
            cp.wait_send()

    return pl.pallas_call(
        body, name="rs_pair_gather", in_specs=[ANY] * n, out_specs=[ANY] * n,
        out_shape=[jax.ShapeDtypeStruct(r.shape, r.dtype) for r in rs],
        input_output_aliases={a: a for a in range(n)},
        scratch_shapes=[pltpu.SemaphoreType.DMA((n,)), pltpu.SemaphoreType.DMA((n,))],
    )(*rs)


def _adamw_math(w, g, m, v):
    m = B1 * m + (1.0 - B1) * g
    v = B2 * v + (1.0 - B2) * (g * g)
    m_hat = m / (1.0 - B1 ** STEP)
    v_hat = v / (1.0 - B2 ** STEP)
    return -LR * (m_hat / (jnp.sqrt(v_hat) + AEPS) + WD * w), m, v


ADAMW_TILE = 512 * 1024


def adamw(w, g, m, v, name, with_grad=False):
    rows, cols = w.shape
    tr = next((t for t in (1024, 512, 256) if rows % t == 0 and t * cols <= ADAMW_TILE), rows)
    fn =(lambda wv, gv, mv, vv: (gv,) + _adamw_math(wv, gv, mv, vv)) if with_grad else _adamw_math
    return rw(fn, [(a, 0, cols) for a in (w, g, m, v)], [(cols, F32)] * (4 if with_grad else 3), name, rows, tr=tr)


def adamw_small(ws, gs, ms, vs):
    n = len(ws)

    def body(*refs):
        for a in range(n):
            res = _adamw_math(*[refs[k * n + a][...] for k in range(4)])
            for k in range(3):
                refs[(4 + k) * n + a][...] = res[k]

    res = pl.pallas_call(
        body, name="adamw_small", in_specs=[VM] * (4 * n), out_specs=[VM] * (3 * n),
        out_shape=[jax.ShapeDtypeStruct(w.shape, F32) for _ in range(3) for w in ws],
        compiler_params=pltpu.CompilerParams(vmem_limit_bytes=VMEM_LIMIT),
    )(*ws, *gs, *ms, *vs)
    return [(res[a], res[n + a], res[2 * n + a]) for a in range(n)]


WEIGHTS = ["norm_ab", "w_in_ab", "pool_w", "pool_scale", "w_out_ab", "norm_cd", "w_in_cd", "sgu_ln_g", "sgu_ln_b",
           "sgu_w", "sgu_b", "s5_a_re", "s5_a_im", "s5_log_dt", "s5_b_re", "s5_b_im", "s5_c_re", "s5_c_im", "s5_d",
           "glu_w1", "glu_w2", "w_out_cd", "norm_x", "w_xq", "w_xkv", "w_xo", "mem_norm", "final_norm"]
INPUTS = ["x", "mem"] + WEIGHTS + ["loss_target"] + ["m_" + n for n in WEIGHTS] + ["v_" + n for n in WEIGHTS]
BIG = ["w_in_ab", "w_out_ab", "w_in_cd", "w_out_cd", "w_xq", "w_xkv", "w_xo", "glu_w1", "glu_w2", "pool_w"]
COL_SHARDED = ("w_in_ab", "w_in_cd", "w_xkv")
SMALL = [n for n in WEIGHTS if n not in BIG]
SMALL_SHARDED = {"norm_cd": 256, "sgu_ln_g": 256, "sgu_ln_b": 256, "s5_d": 128}
PACK = 256 * 128


def _pack(arrs):
    flat = jnp.concatenate([a.reshape(-1) for a in arrs])
    pad = (-flat.shape[0]) % PACK
    return jnp.concatenate([flat, jnp.zeros((pad,), flat.dtype)]).reshape(-1, 128)


def _unpack(packed, shapes):
    flat, out, off = packed.reshape(-1), [], 0
    for s in shapes:
        n = 1
        for d in s:
            n *= d
        out.append(flat[off:off + n].reshape(s))
        off += n
    return out


LAYER_KEYS = (("w_in", "w_out", "pool_w", "w_xq", "w_xkv", "w_xo"),
              ("w_in", "w_out", "glu_w1", "glu_w2", "w_xq", "w_xkv", "w_xo"))


def _weight_of(key, layer):
    if key in ("w_xq", "w_xkv", "w_xo"):
        return key, layer, 4
    kind = "ab" if layer % 2 == 0 else "cd"
    return {"w_in": "w_in_" + kind, "w_out": "w_out_" + kind}.get(key, key), layer // 2, 2


def kernel(*args):
    a = dict(zip(INPUTS, args))
    x_i, y_i, c_i = lax.axis_index("x"), lax.axis_index("y"), lax.axis_index("c")
    j = 2 * x_i + y_i

    slab = jnp.concatenate([a["norm_cd"], a["sgu_ln_g"], a["sgu_ln_b"],
                            jnp.pad(a["s5_d"], ((0, 0), (0, 128)))], axis=0)
    gslab = allgather_small(slab)
    P = {n: a[n] for n in SMALL}
    for k, n in enumerate(("norm_cd", "sgu_ln_g", "sgu_ln_b", "s5_d")):
        wd = SMALL_SHARDED[n]
        P[n] = gslab[:, 2 * k:2 * k + 2, :wd].transpose(1, 0, 2).reshape(2, 4 * wd)

    def shards_of(layer):
        keys = sorted(k for k in LAYER_KEYS[layer % 2])
        out = []
        for k in keys:
            n, l, _ = _weight_of(k, layer)
            out.append(a[n][l].reshape(-1, a[n].shape[-1]).astype(BF16))
        return keys, out

    keys0, sh0 = shards_of(0)
    first = keys0.index("w_in")
    g_in, token = allgather_sync([sh0[first].reshape(2, sh0[first].shape[0] // 2, sh0[first].shape[1])])
    w_in0 = g_in[0].reshape(4, -1, g_in[0].shape[-1])
    started = {}
    for layer in (0, 1, 2, 3):
        keys, sh = (keys0, sh0) if layer == 0 else shards_of(layer)
        rest = [(k, s) for k, s in zip(keys, sh) if k != "w_in"]
        parts = [("in", ["w_in"], [sh[keys.index("w_in")]])] * (layer > 0) + [("", *map(list, zip(*rest)))]
        for tag, pk, ps in parts:
            send, recv, ps, lands, token = allgather_start(ps, [token, gslab], "allgather_start_%d%s" % (layer, tag))
            started[(layer, tag)] = (pk, send, recv, ps, lands)
    P["norm_ab"] = P["norm_ab"] + token[0:1, 0:1]

    cidx = jnp.reshape(c_i, (1,)).astype(jnp.int32)
    jc = jnp.stack([j, c_i]).astype(jnp.int32)

    def views(g):
        W = {}
        for k, v in g.items():
            if k in ("w_in", "w_xkv"):
                W[k] = mcs(v)
            elif k == "pool_w":
                W[k] = v.reshape(4, 4, 64, 256).transpose(1, 0, 2, 3).reshape(4, 256, 256)
            elif k not in ("glu_w1", "glu_w2"):
                W[k] = m2(v.reshape(-1, v.shape[-1]))
        if "glu_w1" in g:
            W["w12"] = jnp.concatenate([g["glu_w1"].reshape(512, 512), g["glu_w2"].reshape(512, 512)], axis=1)
        return W

    def arrived(layer, tag, after):
        keys, send, recv, sh, lands = started[(layer, tag)]
        return views(dict(zip(keys, allgather_wait(send, recv, sh, lands, after, "allgather_wait_%d%s" % (layer, tag)))))

    def weights_of(layer, x_in):
        W = views({"w_in": w_in0}) if layer == 0 else arrived(layer, "in", x_in)
        W["more"] = lambda after: arrived(layer, "", after)
        return W

    halves, pending = {}, {}

    def finish_pair(layer, after):
        keys, send, recv, flat, lands = halves.pop(layer)
        flat, got = rs_pair_wait(send, recv, flat, lands, after, "rs_pair_wait_%d" % layer)
        pair = rs_pair_sum(flat, got, cidx)
        send, recv, pair, lands, token = rs_chip_start(pair, "rs_chip_start_%d" % layer)
        pending[layer] = (keys, send, recv, pair, lands)
        return token

    def grads_done(layer, GW):
        keys = sorted(GW)
        flat = [GW[k].reshape(4, 2, GW[k].shape[1] // 2, GW[k].shape[2]) for k in keys]
        send, recv, flat, lands, token = rs_pair_start(flat, "rs_pair_start_%d" % layer)
        halves[layer] = (keys, send, recv, flat, lands)
        if layer + 1 in halves:
            token = token + finish_pair(layer + 1, token)
        return token[0:1, 0:1]

    loss, dx, G = local_step(a["x"][0], a["mem"][0], a["loss_target"][0], P, weights_of, grads_done)
    loss = lax.psum(loss[0, 0], ("x", "y", "c"))
    finish_pair(0, dx)
    outs = {}

    def update_big(names, red):
        for n, g in zip(names, rs_pair_gather([red[n] for n in names])):
            shp = a[n].shape
            g2 = g.reshape(-1, shp[-1])
            upd = adamw(a[n].reshape(g2.shape), g2, a["m_" + n].reshape(g2.shape), a["v_" + n].reshape(g2.shape),
                        "adamw_" + n, with_grad=True)
            outs[n] = tuple(t.reshape(shp) for t in upd)

    def reduce_layer(layer, red, after):
        keys, send, recv, pair, lands = pending[layer]
        pair, lands = rs_chip_wait(send, recv, pair, lands, after, "rs_chip_wait_%d" % layer)
        which = [_weight_of(k, layer) for k in keys]
        sums = rs_chip_sum(lands, pair, [l for _, l, _ in which], [red.get(n) for n, _, _ in which],
                           [layers for _, _, layers in which], jc)
        red.update(zip([n for n, _, _ in which], sums))

    red = {}
    for layer in (3, 2, 1):
        reduce_layer(layer, red, [dx])
    odd_only = [n for n in BIG if n.endswith("_cd") or n.startswith("glu")]
    update_big(odd_only, red)

    gfull = [jnp.stack(G[n]) if isinstance(G[n], list) else G[n] for n in SMALL]
    shapes = [g.shape for g in gfull]
    gsum = _unpack(allreduce_small(_pack(gfull)), shapes)
    gloc = []
    for n, g in zip(SMALL, gsum):
        if n in SMALL_SHARDED:
            g = lax.dynamic_slice_in_dim(g, j * SMALL_SHARDED[n], SMALL_SHARDED[n], axis=1)
        gloc.append(g)
    two = [(-1, a[n].shape[-1]) if a[n].ndim > 1 else (1, a[n].shape[0]) for n in SMALL]
    upds = adamw_small(*[[t.reshape(s) for t, s in zip(ts, two)]
                         for ts in ([a[n] for n in SMALL], gloc, [a["m_" + n] for n in SMALL],
                                    [a["v_" + n] for n in SMALL])])
    for n, g, upd in zip(SMALL, gloc, upds):
        outs[n] = (g,) + tuple(t.reshape(a[n].shape) for t in upd)

    behind = [outs[n][1] for n in odd_only + SMALL[-1:]] + [red[n] for n in BIG if n not in odd_only]
    reduce_layer(0, red, behind)
    update_big([n for n in BIG if n not in odd_only], red)

    res = [loss, dx[None]]
    for part in range(4):
        res += [outs[n][part] for n in WEIGHTS]
    return tuple(res)
```

```python
import math

import jax
import jax.numpy as jnp
from jax import lax
from jax.experimental import pallas as pl
from jax.experimental.pallas import tpu as pltpu

F32, BF16 = jnp.float32, jnp.bfloat16
S, D = 2048, 1024
MEM = 256
EPS = 1e-6
NEG = -1e30
QB = 128
PATTERNS = (1, 4, 16)
NG, NP, NH = 32, 64, 16
NS = NG * NP
LR, B1, B2, AEPS, WD, STEP = 0.001, 0.9, 0.999, 1e-08, 0.01, 10
MESHID = pl.DeviceIdType.MESH
VMEM_LIMIT = 56 * 1024 * 1024


def _cparams(sem):
    return pltpu.CompilerParams(dimension_semantics=sem, vmem_limit_bytes=VMEM_LIMIT)


def _sig(x):
    return 1.0 / (1.0 + jnp.exp(-x))


def _dot(a, b, dims):
    return lax.dot_general(a, b, (dims, ((), ())), preferred_element_type=F32)


def _nn(a, b):
    return _dot(a, b, ((1,), (0,)))


def _nt(a, b):
    return _dot(a, b, ((1,), (1,)))


def _tn(a, b):
    return _dot(a, b, ((0,), (0,)))


_DIMS = {"nn": ((1,), (0,)), "nt": ((1,), (1,)), "tn": ((0,), (0,))}


def _tile(dim, cc=None, cap=1024):
    for t in (2048, 1536, 1024, 768, 512, 384, 256, 128):
        if t <= cap and dim % t == 0 and (cc is None or cc % t == 0):
            return t
    return dim


MM_VMEM = 36 * 1024 * 1024


def _mm_tiles(m, n, k, ccm, ccn, cck, a_bytes, b_bytes, o_bytes):
    caps = [1024, 1024, 2048]
    while True:
        tm, tn, tk = _tile(m, ccm, caps[0]), _tile(n, ccn, caps[1]), _tile(k, cck, caps[2])
        need = 2 * (tm * tk * a_bytes + tk * tn * b_bytes + tm * tn * o_bytes) + (tm * tn * 4 if tk < k else 0)
        if need <= MM_VMEM:
            return tm, tn, tk
        if tk > 1024:
            caps[2] = tk // 2
        elif tn >= tm:
            caps[1] = tn // 2
        else:
            caps[0] = tm // 2


def m2(arr, col_off=0, ncols=None):
    rows, cols = arr.shape
    ncols = cols - col_off if ncols is None else ncols

    def spec(tr, tc, rc):
        assert col_off % tc == 0
        return pl.BlockSpec((tr, tc), lambda *g: (rc(*g)[0], rc(*g)[1] + col_off // tc))
    return (arr, rows, ncols, spec, None if col_off == 0 else col_off)


def mcs(arr):
    cs = arr.shape[2]

    def spec(tr, tc, rc):
        n = cs // tc
        return pl.BlockSpec((None, tr, tc), lambda *g: (rc(*g)[1] // n, rc(*g)[0], rc(*g)[1] % n))
    return (arr, arr.shape[1], 4 * cs, spec, cs)


def out2(rows, cols):
    def spec(tr, tc, rc):
        return pl.BlockSpec((tr, tc), lambda *g: tuple(rc(*g)))
    return ((rows, cols), spec, None)


def outcs(rows, cs):
    def spec(tr, tc, rc):
        n = cs // tc
        return pl.BlockSpec((None, tr, tc), lambda *g: (rc(*g)[1] // n, rc(*g)[0], rc(*g)[1] % n))
    return ((4, rows, cs), spec, cs)


def _both(a, b):
    if a is None:
        return b
    if b is None:
        return a
    return math.gcd(a, b)


def mm(a, b, mode, name, add=None, out=None, out_dtype=F32):
    a_arr, a_r, a_c, a_spec, a_cc = a
    b_arr, b_r, b_c, b_spec, b_cc = b
    if mode == "nn":
        m, k, n = a_r, a_c, b_c
        assert b_r == k
        ccm, cck, ccn = None, a_cc, b_cc
    elif mode == "nt":
        m, k, n = a_r, a_c, b_r
        assert b_c == k
        ccm, cck, ccn = None, _both(a_cc, b_cc), None
    else:
        m, k, n = a_c, a_r, b_c
        assert b_r == k
        ccm, cck, ccn = a_cc, None, b_cc
    out = out2(m, n) if out is None else out
    o_shape, o_spec, o_cc = out
    ccn = _both(ccn, o_cc)
    if add is not None:
        ccn = _both(ccn, add[4])
    o_bytes = jnp.dtype(out_dtype).itemsize + (0 if add is None else add[0].dtype.itemsize)
    tm, tn, tk = _mm_tiles(m, n, k, ccm, ccn, cck, a_arr.dtype.itemsize, b_arr.dtype.itemsize, o_bytes)
    nk = k // tk
    if mode == "nn":
        in_specs = [a_spec(tm, tk, lambda i, j, kk: (i, kk)), b_spec(tk, tn, lambda i, j, kk: (kk, j))]
    elif mode == "nt":
        in_specs = [a_spec(tm, tk, lambda i, j, kk: (i, kk)), b_spec(tn, tk, lambda i, j, kk: (j, kk))]
    else:
        in_specs = [a_spec(tk, tm, lambda i, j, kk: (kk, i)), b_spec(tk, tn, lambda i, j, kk: (kk, j))]
    args = [a_arr, b_arr]
    if add is not None:
        in_specs.append(add[3](tm, tn, lambda i, j, kk: (i, j)))
        args.append(add[0])
    return _mm_call(args, in_specs, o_spec(tm, tn, lambda i, j, kk: (i, j)), jax.ShapeDtypeStruct(o_shape, out_dtype),
                    mode, (m // tm, n // tn, nk), (tm, tn), add is not None, name)


def _mm_call(args, in_specs, out_spec, out_shape, mode, grid, tile, has_add, name):
    dims = _DIMS[mode]
    nk = grid[2]
    tm, tn = tile

    def body(*refs):
        a_ref, b_ref = refs[0], refs[1]
        add_ref = refs[2] if has_add else None
        prod = _dot(a_ref[...].astype(BF16), b_ref[...].astype(BF16), dims)
        if nk == 1:
            o_ref = refs[-1]
            if has_add:
                prod = prod + add_ref[...].astype(F32)
            o_ref[...] = prod.astype(o_ref.dtype)
            return
        o_ref, acc = refs[-2], refs[-1]
        kk = pl.program_id(2)

        @pl.when(kk == 0)
        def _():
            acc[...] = prod

        @pl.when(kk > 0)
        def _():
            acc[...] += prod

        @pl.when(kk == nk - 1)
        def _():
            r = acc[...]
            if has_add:
                r = r + add_ref[...].astype(F32)
            o_ref[...] = r.astype(o_ref.dtype)

    return pl.pallas_call(
        body, name=name, grid=grid, in_specs=in_specs, out_specs=out_spec, out_shape=out_shape,
        scratch_shapes=[pltpu.VMEM((tm, tn), F32)] if nk > 1 else [],
        compiler_params=_cparams(("parallel", "parallel", "arbitrary")),
    )(*args)


def mm_band(a, b, mode, name, grid, blocks, maps, out_shape, add=None, out_dtype=F32):
    in_specs = [pl.BlockSpec(blocks[0], maps[0]), pl.BlockSpec(blocks[1], maps[1])]
    args = [a, b]
    if add is not None:
        in_specs.append(pl.BlockSpec(blocks[2], maps[2]))
        args.append(add)
    return _mm_call(args, in_specs, pl.BlockSpec(blocks[2], maps[2]), jax.ShapeDtypeStruct(out_shape, out_dtype),
                    mode, grid, blocks[2], add is not None, name)


def rw(fn, ins, outs, name, rows, tr=None, consts=(), accs=()):
    tr = min(rows, 1024) if tr is None else tr
    n_in, n_c, n_o, n_a = len(ins), len(consts), len(outs), len(accs)
    in_specs = []
    for arr, off, width in ins:
        assert off % width == 0
        in_specs.append(pl.BlockSpec((tr, width), lambda i, o=off // width: (i, o)))
    for c in consts:
        in_specs.append(pl.BlockSpec(c.shape, lambda i: (0, 0)))
    out_specs = [pl.BlockSpec((tr, w), lambda i: (i, 0)) for w, _ in outs]
    out_specs += [pl.BlockSpec(s, lambda i: (0, 0)) for s in accs]
    out_shape = [jax.ShapeDtypeStruct((rows, w), dt) for w, dt in outs]
    out_shape += [jax.ShapeDtypeStruct(s, F32) for s in accs]

    def body(*refs):
        vals = [r[...] for r in refs[:n_in + n_c]]
        o_refs = refs[n_in + n_c:n_in + n_c + n_o]
        a_refs = refs[n_in + n_c + n_o:]
        res = fn(*vals)
        for r, v in zip(o_refs, res[:n_o]):
            r[...] = v.astype(r.dtype)
        if n_a:
            @pl.when(pl.program_id(0) == 0)
            def _():
                for r in a_refs:
                    r[...] = jnp.zeros_like(r)
            for r, v in zip(a_refs, res[n_o:]):
                r[...] += v

    res = pl.pallas_call(
        body, name=name, grid=(rows // tr,), in_specs=in_specs, out_specs=out_specs,
        out_shape=out_shape,
        compiler_params=_cparams(("arbitrary",) if n_a else ("parallel",)),
    )(*[a for a, _, _ in ins], *consts)
    return res


def _rstd(x):
    return lax.rsqrt(jnp.mean(x * x, axis=-1, keepdims=True) + EPS)


def rms_fwd(x, g, name):
    def fn(xv, gv):
        xv = xv.astype(F32)
        return (xv * _rstd(xv) * gv,)
    return rw(fn, [(x, 0, D)], [(D, BF16)], name, x.shape[0], consts=[g])[0]


def _rms_bwd_math(xv, dy, gv):
    r = _rstd(xv)
    dyg = dy * gv
    dx = r * dyg - xv * (r * r * r / D) * jnp.sum(dyg * xv, axis=-1, keepdims=True)
    dg = jnp.sum(dy * xv * r, axis=0, keepdims=True)
    return dx, dg


def rms_bwd(x, dy, dres, g, name):
    def fn(xv, dyv, drv, gv):
        dx, dg = _rms_bwd_math(xv, dyv, gv)
        return dx + drv, dg
    return rw(fn, [(x, 0, D), (dy, 0, D), (dres, 0, D)], [(D, F32)], name, x.shape[0],
              consts=[g], accs=[(1, D)])


def final_loss(x, tgt, g):
    def fn(xv, tv, gv):
        e = xv * _rstd(xv) * gv - tv
        loss = 0.5 * jnp.sum(jnp.sum(e * e, axis=-1, keepdims=True), axis=0, keepdims=True) / D
        dx, dg = _rms_bwd_math(xv, e / D, gv)
        return dx, loss, dg
    return rw(fn, [(x, 0, D), (tgt, 0, D)], [(D, F32)], "final_loss", S, consts=[g],
              accs=[(1, 1), (1, D)])


def _attn_bias(bias_ref):
    ii = lax.broadcasted_iota(jnp.int32, (2 * QB, 2 * QB), 0) % QB
    jj = lax.broadcasted_iota(jnp.int32, (2 * QB, 2 * QB), 1)
    dist = ii + QB - jj
    band = (dist >= 0) & (dist <= QB)
    bias_ref[1] = jnp.where(band, 0.0, NEG)
    bias_ref[0] = jnp.where(band & (jj >= QB), 0.0, NEG)


def _two_heads(x, m0):
    return jnp.concatenate([jnp.where(m0, x, 0.0), jnp.where(m0, 0.0, x)], axis=0)


def _per_head(col, m0):
    return jnp.where(m0, col[:QB], col[QB:])


def _attn_rows(idx, d):
    if d == 1:
        b = idx
        cur = pl.ds(pl.multiple_of(b * QB, QB), QB)
        prev = pl.ds(pl.multiple_of(jnp.maximum(b - 1, 0) * QB, QB), QB)
    else:
        r, b = lax.rem(idx, d), lax.div(idx, d)
        cur = pl.ds(r + b * (QB * d), QB, stride=d)
        prev = pl.ds(r + jnp.maximum(b - 1, 0) * (QB * d), QB, stride=d)
    return cur, prev, b


NBLK = S // QB
GROUP = 16
GROUP_FWD = 16


def _colblk(off):
    return pl.BlockSpec((S, 128), lambda hp: (0, off * 8 + hp))


def attn_fwd(z):
    def body(q_ref, k_ref, v_ref, g_ref, o_ref, l_ref, a_ref, os, ls, bias):
        _attn_bias(bias)
        m0 = lax.broadcasted_iota(jnp.int32, (1, 128), 1) < 64
        for pi, d in enumerate(PATTERNS):
            lone = S // d == QB

            def load(idx, d=d, lone=lone):
                cur, prev, b = _attn_rows(idx, d)
                if lone:
                    return cur, (q_ref[cur, :], None, k_ref[cur, :], None, v_ref[cur, :], bias[1, :, QB:])
                return cur, (q_ref[cur, :], k_ref[prev, :], k_ref[cur, :], v_ref[prev, :], v_ref[cur, :],
                             bias[jnp.minimum(b, 1)])

            def block(q, kp, kc, vp, vc, bs):
                qq = _two_heads(q * 0.125, m0).astype(BF16)
                k = (kc if kp is None else jnp.concatenate([kp, kc], axis=0)).astype(BF16)
                s = _nt(qq, k) + bs
                mx = jnp.max(s, axis=-1, keepdims=True)
                p = jnp.exp(s - mx)
                den = jnp.sum(p, axis=-1, keepdims=True)
                pb = p.astype(BF16)
                vv = _two_heads(vc if vp is None else jnp.concatenate([vp, vc], axis=0), m0).astype(BF16)
                o = _nn(jnp.concatenate([pb[:QB], pb[QB:]], axis=1), vv)
                return o * _per_head(1.0 / den, m0), _per_head(mx + jnp.log(den), m0)

            def step(i, carry, pi=pi):
                loaded = [load(i * GROUP_FWD + u) for u in range(GROUP_FWD)]
                done = [block(*vals) for _, vals in loaded]
                for (cur, _), (o, l) in zip(loaded, done):
                    os[pi, cur, :] = o
                    ls[pi, cur, :] = l
                return carry
            lax.fori_loop(0, NBLK // GROUP_FWD, step, 0)
        l1, l2, l3 = ls[0], ls[1], ls[2]
        mx = jnp.maximum(jnp.maximum(l1, l2), l3)
        e1, e2, e3 = jnp.exp(l1 - mx), jnp.exp(l2 - mx), jnp.exp(l3 - mx)
        tot = e1 + e2 + e3
        o = (os[0] * e1 + os[1] * e2 + os[2] * e3) / tot
        ga = g_ref[...]
        o_ref[...] = o
        l_ref[...] = mx + jnp.log(tot)
        a_ref[...] = (o * (ga * _sig(ga))).astype(a_ref.dtype)

    out = pl.BlockSpec((S, 128), lambda hp: (0, hp))
    return pl.pallas_call(
        body, name="attn_fwd", grid=(8,),
        in_specs=[_colblk(0), _colblk(1), _colblk(2), _colblk(3)], out_specs=[out] * 3,
        out_shape=[jax.ShapeDtypeStruct((S, D), F32), jax.ShapeDtypeStruct((S, D), F32),
                   jax.ShapeDtypeStruct((S, 2 * D), BF16)],
        scratch_shapes=[pltpu.VMEM((3, S, 128), F32), pltpu.VMEM((3, S, 128), F32),
                        pltpu.VMEM((2, 2 * QB, 2 * QB), F32)],
        compiler_params=_cparams(("parallel",)),
    )(z, z, z, z)


def attn_bwd(z, d_cat, o, lse):
    def body(q_ref, k_ref, v_ref, g_ref, da_ref, o_ref, l_ref, dq_ref, dk_ref, dv_ref, dg_ref, do_s, pr_s, bias):
        _attn_bias(bias)
        m0 = lax.broadcasted_iota(jnp.int32, (1, 128), 1) < 64
        ga = g_ref[...]
        sg = _sig(ga)
        da = da_ref[...]
        ov = o_ref[...]
        do = da * (ga * sg)
        dg_ref[...] = da * ov * (sg * (1.0 + ga * (1.0 - sg)))
        do_s[...] = do
        pr_s[...] = do * ov
        dq_ref[...] = jnp.zeros_like(dq_ref)
        dk_ref[...] = jnp.zeros_like(dk_ref)
        dv_ref[...] = jnp.zeros_like(dv_ref)
        for d in PATTERNS:
            lone = S // d == QB

            def load(idx, d=d, lone=lone):
                cur, prev, b = _attn_rows(idx, d)
                if lone:
                    return (cur, None), (q_ref[cur, :], None, k_ref[cur, :], None, v_ref[cur, :],
                                         do_s[cur, :], pr_s[cur, :], l_ref[cur, :], bias[1, :, QB:])
                return (cur, prev), (q_ref[cur, :], k_ref[prev, :], k_ref[cur, :], v_ref[prev, :], v_ref[cur, :],
                                     do_s[cur, :], pr_s[cur, :], l_ref[cur, :], bias[jnp.minimum(b, 1)])

            def block(q, kp, kc, vp, vc, dof, prod, lp, bs):
                qq = _two_heads(q * 0.125, m0).astype(BF16)
                kf = kc if kp is None else jnp.concatenate([kp, kc], axis=0)
                k = kf.astype(BF16)
                v = (vc if vp is None else jnp.concatenate([vp, vc], axis=0)).astype(BF16)
                dd = _two_heads(dof, m0).astype(BF16)
                lh = jnp.max(jnp.concatenate([jnp.where(m0, lp, -jnp.inf), jnp.where(m0, -jnp.inf, lp)], axis=0),
                             axis=-1, keepdims=True)
                delta = jnp.sum(_two_heads(prod, m0), axis=-1, keepdims=True)
                p = jnp.exp(_nt(qq, k) + bs - lh)
                ds = (p * (_nt(dd, v) - delta)).astype(BF16)
                dq = _nn(jnp.concatenate([ds[:QB], ds[QB:]], axis=1), _two_heads(kf, m0).astype(BF16))
                return dq * 0.125, _tn(ds, qq), _tn(p.astype(BF16), dd)

            def step(i, carry):
                loaded = [load(i * GROUP + u) for u in range(GROUP)]
                done = [block(*vals) for _, vals in loaded]
                for ((cur, prev), _), (dq, dk, dv) in zip(loaded, done):
                    dq_ref[cur, :] = dq_ref[cur, :] + dq
                    if prev is not None:
                        dk_ref[prev, :] = dk_ref[prev, :] + dk[:QB]
                        dv_ref[prev, :] = dv_ref[prev, :] + dv[:QB]
                    dk_ref[cur, :] = dk_ref[cur, :] + dk[-QB:]
                    dv_ref[cur, :] = dv_ref[cur, :] + dv[-QB:]
                return carry
            lax.fori_loop(0, NBLK // GROUP, step, 0)

    blk = pl.BlockSpec((S, 128), lambda hp: (0, hp))
    return pl.pallas_call(
        body, name="attn_bwd", grid=(8,),
        in_specs=[_colblk(0), _colblk(1), _colblk(2), _colblk(3), blk, blk, blk], out_specs=[blk] * 4,
        out_shape=[jax.ShapeDtypeStruct((S, D), F32)] * 4,
        scratch_shapes=[pltpu.VMEM((S, 128), F32), pltpu.VMEM((S, 128), F32), pltpu.VMEM((2, 2 * QB, 2 * QB), F32)],
        compiler_params=_cparams(("parallel",)),
    )(z, z, z, z, d_cat, o, lse)


def assemble_dz_even(parts):
    def body(*refs):
        o_ref = refs[-1]
        for j in range(6):
            o_ref[:, j * D:(j + 1) * D] = refs[j][...].astype(o_ref.dtype)
    tr = 512
    blk = pl.BlockSpec((tr, D), lambda i: (i, 0))
    return pl.pallas_call(
        body, name="assemble_dz_even", grid=(S // tr,), in_specs=[blk] * 6,
        out_specs=pl.BlockSpec((tr, 6 * D), lambda i: (i, 0)),
        out_shape=jax.ShapeDtypeStruct((S, 6 * D), BF16),
        compiler_params=_cparams(("parallel",)),
    )(*parts)


def _pool_window(g):
    return jnp.where(g == 0, 2.0, jnp.where(g == 1, 4.0, jnp.where(g == 2, 8.0, 16.0)))


def _pool_sel(g, levels):
    return jnp.where(g == 0, levels[0], jnp.where(g == 1, levels[1], jnp.where(g == 2, levels[2], levels[3])))


def _pool_fwd_math(v, g):
    t = lax.broadcasted_iota(jnp.int32, (S, 1), 0)
    s = v
    levels = []
    for k in (1, 2, 4, 8):
        s = s + jnp.where(t >= k, pltpu.roll(s, k, 0), 0.0)
        levels.append(s)
    cnt = jnp.minimum((t + 1).astype(F32), _pool_window(g))
    return _pool_sel(g, levels) / cnt - v, cnt


def pool_fwd(z, pw, ps, cat):
    def body(v_ref, g_ref, pw_ref, ps_ref, cat_ref, o_ref):
        g = pl.program_id(0)
        pooled, _ = _pool_fwd_math(v_ref[...], g)
        mixed = _nn(pooled.astype(BF16), pw_ref[...].astype(BF16))
        gb = g_ref[...]
        o_ref[...] = (mixed * ps_ref[...] * (gb * _sig(gb))).astype(o_ref.dtype)

    return pl.pallas_call(
        body, name="pool_fwd", grid=(4,),
        in_specs=[pl.BlockSpec((S, 256), lambda g: (0, 16 + g)),
                  pl.BlockSpec((S, 256), lambda g: (0, 20 + g)),
                  pl.BlockSpec((None, 256, 256), lambda g: (g, 0, 0)),
                  pl.BlockSpec((1, 256), lambda g: (0, g)), pl.BlockSpec(memory_space=pl.ANY)],
        out_specs=pl.BlockSpec((S, 256), lambda g: (0, 4 + g)),
        out_shape=jax.ShapeDtypeStruct((S, 2 * D), BF16),
        input_output_aliases={4: 0},
        compiler_params=_cparams(("parallel",)),
    )(z, z, pw, ps, cat)


def pool_bwd(z, d_cat, pw, ps):
    def body(v_ref, g_ref, d_ref, pw_ref, ps_ref, dv_ref, dg_ref, dpw_ref, dps_ref):
        g = pl.program_id(0)
        v = v_ref[...]
        pooled, cnt = _pool_fwd_math(v, g)
        pwb = pw_ref[...].astype(BF16)
        pb = pooled.astype(BF16)
        mixed = _nn(pb, pwb)
        gb = g_ref[...]
        sg = _sig(gb)
        dout = d_ref[...]
        sc = ps_ref[...]
        dg_ref[...] = dout * mixed * sc * (sg * (1.0 + gb * (1.0 - sg)))
        dms = dout * (gb * sg)
        dps_ref[...] = jnp.sum(dms * mixed, axis=0, keepdims=True)
        dmx = (dms * sc).astype(BF16)
        dpw_ref[...] = _tn(pb, dmx)
        dpooled = _nt(dmx, pwb)
        t = lax.broadcasted_iota(jnp.int32, (S, 1), 0)
        s = dpooled / cnt
        levels = []
        for k in (1, 2, 4, 8):
            s = s + jnp.where(t < S - k, pltpu.roll(s, S - k, 0), 0.0)
            levels.append(s)
        dv_ref[...] = _pool_sel(g, levels) - dpooled

    return pl.pallas_call(
        body, name="pool_bwd", grid=(4,),
        in_specs=[pl.BlockSpec((S, 256), lambda g: (0, 16 + g)),
                  pl.BlockSpec((S, 256), lambda g: (0, 20 + g)),
                  pl.BlockSpec((S, 256), lambda g: (0, 4 + g)),
                  pl.BlockSpec((None, 256, 256), lambda g: (g, 0, 0)),
                  pl.BlockSpec((1, 256), lambda g: (0, g))],
        out_specs=[pl.BlockSpec((S, 256), lambda g: (0, g)),
                   pl.BlockSpec((S, 256), lambda g: (0, g)),
                   pl.BlockSpec((None, 256, 256), lambda g: (g, 0, 0)),
                   pl.BlockSpec((1, 256), lambda g: (0, g))],
        out_shape=[jax.ShapeDtypeStruct((S, D), F32), jax.ShapeDtypeStruct((S, D), F32),
                   jax.ShapeDtypeStruct((4, 256, 256), F32), jax.ShapeDtypeStruct((1, D), F32)],
        compiler_params=_cparams(("parallel",)),
    )(z, z, d_cat, pw, ps)


CH = 128


def _sgu_common(v, lng, lnb, w_ref):
    mu = jnp.mean(v, axis=-1, keepdims=True)
    vc = v - mu
    rs = lax.rsqrt(jnp.mean(vc * vc, axis=-1, keepdims=True) + EPS)
    xhat = vc * rs
    vn = (xhat * lng + lnb).astype(BF16)
    ri = lax.broadcasted_iota(jnp.int32, (CH, CH), 0)
    ci = lax.broadcasted_iota(jnp.int32, (CH, CH), 1)
    tril = ri >= ci
    ws = [jnp.where(tril, w_ref[g], 0.0).astype(BF16) for g in range(4)]
    return xhat, rs, vn, tril, ws


def _zspec(off):
    return pl.BlockSpec((CH, D), lambda c: (c, off))


def _full(shape):
    return pl.BlockSpec(shape, lambda c: (0,) * len(shape))


def sgu_fwd(z, lng, lnb, w, bfull):
    def body(u_ref, v_ref, g_ref, lng_ref, lnb_ref, w_ref, b_ref, o_ref):
        _, _, vn, _, ws = _sgu_common(v_ref[...], lng_ref[...], lnb_ref[...], w_ref)
        for g in range(4):
            sl = slice(g * 256, (g + 1) * 256)
            mixed = _nn(ws[g], vn[:, sl]) + b_ref[:, sl]
            gc = g_ref[:, sl]
            o_ref[:, sl] = (u_ref[:, sl] * mixed * (gc * _sig(gc))).astype(o_ref.dtype)

    return pl.pallas_call(
        body, name="sgu_fwd", grid=(S // CH,),
        in_specs=[_zspec(0), _zspec(1), _zspec(2), _full((1, D)), _full((1, D)),
                  _full((4, CH, CH)), _full((CH, D))],
        out_specs=pl.BlockSpec((CH, D), lambda c: (c, 0)),
        out_shape=jax.ShapeDtypeStruct((S, D), BF16),
        compiler_params=_cparams(("parallel",)),
    )(z, z, z, lng, lnb, w, bfull)


def sgu_bwd(z, d_cat, lng, lnb, w, bfull):
    def body(u_ref, v_ref, g_ref, d_ref, lng_ref, lnb_ref, w_ref, b_ref,
             du_ref, dv_ref, dg_ref, dw_ref, db_ref, dlg_ref, dlb_ref):
        @pl.when(pl.program_id(0) == 0)
        def _():
            dw_ref[...] = jnp.zeros_like(dw_ref)
            db_ref[...] = jnp.zeros_like(db_ref)
            dlg_ref[...] = jnp.zeros_like(dlg_ref)
            dlb_ref[...] = jnp.zeros_like(dlb_ref)

        lng = lng_ref[...]
        xhat, rs, vn, tril, ws = _sgu_common(v_ref[...], lng, lnb_ref[...], w_ref)
        lane = lax.broadcasted_iota(jnp.int32, (1, 128), 1)
        db = jnp.zeros((CH, 128), F32)
        dvn_parts = []
        for g in range(4):
            sl = slice(g * 256, (g + 1) * 256)
            mixed = _nn(ws[g], vn[:, sl]) + b_ref[:, sl]
            gc = g_ref[:, sl]
            sg = _sig(gc)
            u = u_ref[:, sl]
            dc = d_ref[:, sl]
            du_ref[:, sl] = dc * mixed * (gc * sg)
            dg_ref[:, sl] = dc * u * mixed * (sg * (1.0 + gc * (1.0 - sg)))
            dmx = dc * u * (gc * sg)
            db = db + jnp.where(lane == g, jnp.sum(dmx, axis=-1, keepdims=True), 0.0)
            dmb = dmx.astype(BF16)
            dw_ref[g] += jnp.where(tril, _nt(dmb, vn[:, sl]), 0.0)
            dvn_parts.append(_tn(ws[g], dmb))
        db_ref[...] += db
        dvn = jnp.concatenate(dvn_parts, axis=1)
        dlb_ref[...] += jnp.sum(dvn, axis=0, keepdims=True)
        dlg_ref[...] += jnp.sum(dvn * xhat, axis=0, keepdims=True)
        dxh = dvn * lng
        dv_ref[...] = rs * (dxh - jnp.mean(dxh, axis=-1, keepdims=True)
                            - xhat * jnp.mean(dxh * xhat, axis=-1, keepdims=True))

    row = pl.BlockSpec((CH, D), lambda c: (c, 0))
    return pl.pallas_call(
        body, name="sgu_bwd", grid=(S // CH,),
        in_specs=[_zspec(0), _zspec(1), _zspec(2), row, _full((1, D)), _full((1, D)),
                  _full((4, CH, CH)), _full((CH, D))],
        out_specs=[row, row, row, _full((4, CH, CH)), _full((CH, 128)), _full((1, D)), _full((1, D))],
        out_shape=[jax.ShapeDtypeStruct((S, D), F32)] * 3
        + [jax.ShapeDtypeStruct((4, CH, CH), F32), jax.ShapeDtypeStruct((CH, 128), F32),
           jax.ShapeDtypeStruct((1, D), F32), jax.ShapeDtypeStruct((1, D), F32)],
        compiler_params=_cparams(("arbitrary",)),
    )(z, z, z, d_cat, lng, lnb, w, bfull)


TB = 256


def _cmul(ar, ai, br, bi):
    return ar * br - ai * bi, ar * bi + ai * br


def _scan_consts(ar, ai, reverse):
    a2 = _cmul(ar, ai, ar, ai)
    a4 = _cmul(*a2, *a2)
    row = lax.broadcasted_iota(jnp.int32, (8, NS), 0)

    def masked(k, p):
        keep = (row < 8 - k) if reverse else (row >= k)
        return jnp.where(keep, p[0], 0.0), jnp.where(keep, p[1], 0.0)
    pr = jnp.zeros((8, NS), F32)
    pi = jnp.zeros((8, NS), F32)
    cr, ci = ar, ai
    for r in range(8):
        sel = row == (7 - r if reverse else r)
        pr = jnp.where(sel, cr, pr)
        pi = jnp.where(sel, ci, pi)
        cr, ci = _cmul(cr, ci, ar, ai)
    return (masked(1, (ar, ai)), masked(2, a2), masked(4, a4)), (pr, pi), row


def scan_fwd(bu, abr, abi):
    def body(bu_ref, ar_ref, ai_ref, h_ref, car, cai):
        @pl.when(pl.program_id(0) == 0)
        def _():
            car[...] = jnp.zeros_like(car)
            cai[...] = jnp.zeros_like(cai)

        pows, (pr, pi), row = _scan_consts(ar_ref[...], ai_ref[...], False)

        def tile(t, carry):
            c_r, c_i = carry
            rows = pl.ds(pl.multiple_of(t * 8, 8), 8)
            xr = bu_ref[rows, 0:NS]
            xi = bu_ref[rows, NS:2 * NS]
            for k, (kr, ki) in zip((1, 2, 4), pows):
                sr = pltpu.roll(xr, k, 0)
                si = pltpu.roll(xi, k, 0)
                xr, xi = xr + kr * sr - ki * si, xi + kr * si + ki * sr
            xr, xi = xr + pr * c_r - pi * c_i, xi + pr * c_i + pi * c_r
            h_ref[rows, 0:NS] = xr
            h_ref[rows, NS:2 * NS] = xi
            return (jnp.broadcast_to(xr[7:8, :], (8, NS)), jnp.broadcast_to(xi[7:8, :], (8, NS)))

        c_r, c_i = lax.fori_loop(0, TB // 8, tile, (car[...], cai[...]))
        car[...] = c_r
        cai[...] = c_i

    return pl.pallas_call(
        body, name="s5_scan_fwd", grid=(S // TB,),
        in_specs=[pl.BlockSpec((TB, 2 * NS), lambda i: (i, 0)),
                  pl.BlockSpec((1, NS), lambda i: (0, 0)), pl.BlockSpec((1, NS), lambda i: (0, 0))],
        out_specs=pl.BlockSpec((TB, 2 * NS), lambda i: (i, 0)),
        out_shape=jax.ShapeDtypeStruct((S, 2 * NS), F32),
        scratch_shapes=[pltpu.VMEM((8, NS), F32), pltpu.VMEM((8, NS), F32)],
        compiler_params=_cparams(("arbitrary",)),
    )(bu, abr, abi)


def scan_bwd(eta, h, abr, abi):
    nt = S // TB

    def body(e_ref, h_ref, ar_ref, ai_ref, l_ref, da_ref, car, cai):
        @pl.when(pl.program_id(0) == 0)
        def _():
            car[...] = jnp.zeros_like(car)
            cai[...] = jnp.zeros_like(cai)
            da_ref[...] = jnp.zeros_like(da_ref)

        pows, (pr, pi), row = _scan_consts(ar_ref[...], -ai_ref[...], True)

        def tile(tt, carry):
            c_r, c_i, acr, aci = carry
            t = TB // 8 - 1 - tt
            rows = pl.ds(pl.multiple_of(t * 8, 8), 8)
            xr = e_ref[rows, 0:NS]
            xi = e_ref[rows, NS:2 * NS]
            for k, (kr, ki) in zip((1, 2, 4), pows):
                sr = pltpu.roll(xr, 8 - k, 0)
                si = pltpu.roll(xi, 8 - k, 0)
                xr, xi = xr + kr * sr - ki * si, xi + kr * si + ki * sr
            xr, xi = xr + pr * c_r - pi * c_i, xi + pr * c_i + pi * c_r
            l_ref[rows, 0:NS] = xr
            l_ref[rows, NS:2 * NS] = xi
            nr = jnp.where(row < 7, pltpu.roll(xr, 7, 0), c_r)
            ni = jnp.where(row < 7, pltpu.roll(xi, 7, 0), c_i)
            hr = h_ref[rows, 0:NS]
            hi = h_ref[rows, NS:2 * NS]
            acr = acr + hr * nr + hi * ni
            aci = aci + hr * ni - hi * nr
            return (jnp.broadcast_to(xr[0:1, :], (8, NS)), jnp.broadcast_to(xi[0:1, :], (8, NS)), acr, aci)

        zero = jnp.zeros((8, NS), F32)
        c_r, c_i, acr, aci = lax.fori_loop(0, TB // 8, tile, (car[...], cai[...], zero, zero))
        car[...] = c_r
        cai[...] = c_i
        da_ref[:, 0:NS] += acr
        da_ref[:, NS:2 * NS] += aci

    rev = pl.BlockSpec((TB, 2 * NS), lambda i: (nt - 1 - i, 0))
    return pl.pallas_call(
        body, name="s5_scan_bwd", grid=(nt,),
        in_specs=[rev, rev, pl.BlockSpec((1, NS), lambda i: (0, 0)), pl.BlockSpec((1, NS), lambda i: (0, 0))],
        out_specs=[rev, pl.BlockSpec((8, 2 * NS), lambda i: (0, 0))],
        out_shape=[jax.ShapeDtypeStruct((S, 2 * NS), F32), jax.ShapeDtypeStruct((8, 2 * NS), F32)],
        scratch_shapes=[pltpu.VMEM((8, NS), F32), pltpu.VMEM((8, NS), F32)],
        compiler_params=_cparams(("arbitrary",)),
    )(eta, h, abr, abi)


GC = 0.7978845608028654
GA = 0.044715


def s5_post(hc, z, dskip):
    def fn(hv, xd, dv):
        y = hv + dv * xd
        return y, 0.5 * y * (1.0 + jnp.tanh(GC * (y + GA * y * y * y)))
    return rw(fn, [(hc, 0, 512), (z, 3072, 512)], [(512, F32), (512, BF16)], "s5_post", S, consts=[dskip])


def s5_post_bwd(dyg, ypre, z, dskip):
    def fn(dy, y, xd, dv):
        th = jnp.tanh(GC * (y + GA * y * y * y))
        dg = 0.5 * (1.0 + th) + 0.5 * y * (1.0 - th * th) * GC * (1.0 + 3.0 * GA * y * y)
        dyp = dy * dg
        return dyp, dyp * dv, jnp.sum(dyp * xd, axis=0, keepdims=True)
    return rw(fn, [(dyg, 0, 512), (ypre, 0, 512), (z, 3072, 512)], [(512, BF16), (512, F32)],
              "s5_post_bwd", S, consts=[dskip], accs=[(1, 512)])


def glu_fwd(t, z, c_out):
    def fn(t1, t2, gd, co):
        return (jnp.concatenate([co, (t1 * _sig(t2) * (gd * _sig(gd))).astype(BF16)], axis=1),)
    return rw(fn, [(t, 0, 512), (t, 512, 512), (z, 3584, 512), (c_out, 0, D)], [(D + 512, BF16)], "glu_fwd", S)[0]


def glu_bwd(t, z, d_cat):
    def fn(t1, t2, gd, dd):
        s2, sg = _sig(t2), _sig(gd)
        sl = gd * sg
        return (jnp.concatenate([dd * s2 * sl, dd * t1 * s2 * (1.0 - s2) * sl], axis=1),
                dd * t1 * s2 * (sg * (1.0 + gd * (1.0 - sg))))
    return rw(fn, [(t, 0, 512), (t, 512, 512), (z, 3584, 512), (d_cat, 1024, 512)],
              [(D, BF16), (512, F32)], "glu_bwd", S)


def assemble_dz_odd(du, dv, dgc, dxd, dgd):
    def body(a, b, c, d, e, o_ref):
        o_ref[:, 0:D] = a[...].astype(BF16)
        o_ref[:, D:2 * D] = b[...].astype(BF16)
        o_ref[:, 2 * D:3 * D] = c[...].astype(BF16)
        o_ref[:, 3 * D:3 * D + 512] = d[...].astype(BF16)
        o_ref[:, 3 * D + 512:4 * D] = e[...].astype(BF16)
    tr = 512
    blk = pl.BlockSpec((tr, D), lambda i: (i, 0))
    half = pl.BlockSpec((tr, 512), lambda i: (i, 0))
    return pl.pallas_call(
        body, name="assemble_dz_odd", grid=(S // tr,), in_specs=[blk, blk, blk, half, half],
        out_specs=pl.BlockSpec((tr, 4 * D), lambda i: (i, 0)),
        out_shape=jax.ShapeDtypeStruct((S, 4 * D), BF16),
        compiler_params=_cparams(("parallel",)),
    )(du, dv, dgc, dxd, dgd)


TQ = 1024


def _xattn_probs(qh, kh):
    s = _nt(qh, kh) * 0.0625
    p = jnp.exp(s - jnp.max(s, axis=-1, keepdims=True))
    return p / jnp.sum(p, axis=-1, keepdims=True)


def xattn_fwd(q, kv):
    def body(q_ref, kv_ref, o_ref):
        outs = []
        for h in range(4):
            sl = slice(h * 256, (h + 1) * 256)
            p = _xattn_probs(q_ref[:, sl].astype(BF16), kv_ref[:, sl].astype(BF16))
            vh = kv_ref[:, D + h * 256:D + (h + 1) * 256].astype(BF16)
            outs.append((sl, _nn(p.astype(BF16), vh)))
        for sl, o in outs:
            o_ref[:, sl] = o.astype(o_ref.dtype)

    return pl.pallas_call(
        body, name="xattn_fwd", grid=(S // TQ,),
        in_specs=[pl.BlockSpec((TQ, D), lambda i: (i, 0)), pl.BlockSpec((MEM, 2 * D), lambda i: (0, 0))],
        out_specs=pl.BlockSpec((TQ, D), lambda i: (i, 0)),
        out_shape=jax.ShapeDtypeStruct((S, D), BF16),
        compiler_params=_cparams(("parallel",)),
    )(q, kv)


def xattn_bwd(q, kv, d_o):
    def body(q_ref, kv_ref, do_ref, dq_ref, dkv_ref):
        @pl.when(pl.program_id(0) == 0)
        def _():
            dkv_ref[...] = jnp.zeros_like(dkv_ref)

        done = []
        for h in range(4):
            sl = slice(h * 256, (h + 1) * 256)
            vs = slice(D + h * 256, D + (h + 1) * 256)
            qh = q_ref[:, sl].astype(BF16)
            kh = kv_ref[:, sl].astype(BF16)
            vh = kv_ref[:, vs].astype(BF16)
            doh = do_ref[:, sl].astype(BF16)
            p = _xattn_probs(qh, kh)
            dp = _nt(doh, vh)
            ds = (p * (dp - jnp.sum(p * dp, axis=-1, keepdims=True)) * 0.0625).astype(BF16)
            done.append((sl, vs, _nn(ds, kh), _tn(ds, qh), _tn(p.astype(BF16), doh)))
        for sl, vs, dq, dk, dv in done:
            dq_ref[:, sl] = dq.astype(dq_ref.dtype)
            dkv_ref[:, sl] += dk
            dkv_ref[:, vs] += dv

    return pl.pallas_call(
        body, name="xattn_bwd", grid=(S // TQ,),
        in_specs=[pl.BlockSpec((TQ, D), lambda i: (i, 0)), pl.BlockSpec((MEM, 2 * D), lambda i: (0, 0)),
                  pl.BlockSpec((TQ, D), lambda i: (i, 0))],
        out_specs=[pl.BlockSpec((TQ, D), lambda i: (i, 0)), pl.BlockSpec((MEM, 2 * D), lambda i: (0, 0))],
        out_shape=[jax.ShapeDtypeStruct((S, D), BF16), jax.ShapeDtypeStruct((MEM, 2 * D), F32)],
        compiler_params=_cparams(("arbitrary",)),
    )(q, kv, d_o)


def _s5_disc(a_re, a_im, log_dt, b_re, b_im):
    dt = jnp.exp(log_dt)[:, None]
    mag = jnp.exp(dt * a_re)
    abr = mag * jnp.cos(dt * a_im)
    abi = mag * jnp.sin(dt * a_im)
    nr, ni = abr - 1.0, abi
    inv = 1.0 / (a_re * a_re + a_im * a_im)
    cr = (nr * a_re + ni * a_im) * inv
    ci = (ni * a_re - nr * a_im) * inv
    bbr = cr[..., None] * b_re - ci[..., None] * b_im
    bbi = cr[..., None] * b_im + ci[..., None] * b_re
    return abr, abi, bbr, bbi


VM = pl.BlockSpec(memory_space=pltpu.VMEM)


def s5_embed(bt_re, bt_im, ct_re, ct_im):
    def body(br, bi, cr, ci, b_ref, c_ref):
        b_ref[...] = jnp.zeros_like(b_ref)
        c_ref[...] = jnp.zeros_like(c_ref)
        for g in range(NG):
            rows, cols = slice(g * NH, (g + 1) * NH), slice(g * NP, (g + 1) * NP)
            b_ref[rows, cols] = br[g]
            b_ref[rows, NS + g * NP:NS + (g + 1) * NP] = bi[g]
            c_ref[cols, rows] = cr[g]
            c_ref[NS + g * NP:NS + (g + 1) * NP, rows] = -ci[g]

    return pl.pallas_call(
        body, name="s5_embed", in_specs=[VM] * 4, out_specs=[VM] * 2,
        out_shape=[jax.ShapeDtypeStruct((NG * NH, 2 * NS), F32), jax.ShapeDtypeStruct((2 * NS, NG * NH), F32)],
        compiler_params=pltpu.CompilerParams(vmem_limit_bytes=VMEM_LIMIT),
    )(bt_re, bt_im, ct_re, ct_im)


def s5_extract(gb, gc):
    def body(gb_ref, gc_ref, br, bi, cr, ci):
        for g in range(NG):
            rows, cols = slice(g * NH, (g + 1) * NH), slice(g * NP, (g + 1) * NP)
            br[g] = gb_ref[rows, cols]
            bi[g] = gb_ref[rows, NS + g * NP:NS + (g + 1) * NP]
            cr[g] = gc_ref[cols, rows]
            ci[g] = -gc_ref[NS + g * NP:NS + (g + 1) * NP, rows]

    return pl.pallas_call(
        body, name="s5_extract", in_specs=[VM] * 2, out_specs=[VM] * 4,
        out_shape=[jax.ShapeDtypeStruct((NG, NH, NP), F32)] * 2 + [jax.ShapeDtypeStruct((NG, NP, NH), F32)] * 2,
        compiler_params=pltpu.CompilerParams(vmem_limit_bytes=VMEM_LIMIT),
    )(gb, gc)


HC, HS = NG * NH // 2, NS // 2
TS = 1024


def s5_to_states(x, w, mode, name, z_off=0):
    if mode == "nn":
        wb, wm = (HC, HS), lambda i, j, kk: (j % 2, j)
    else:
        wb, wm = (HS, HC), lambda i, j, kk: (j, j % 2)
    return mm_band(x, w, mode, name, (S // TS, 4, 1), ((TS, HC), wb, (TS, HS)),
                   (lambda i, j, kk: (i, z_off + j % 2), wm, lambda i, j, kk: (i, j)), (S, 2 * NS))


def s5_to_channels(x, w, mode, name, add=None):
    if mode == "nn":
        wb, wm = (HS, HC), lambda i, j, kk: (j + 2 * kk, j)
    else:
        wb, wm = (HC, HS), lambda i, j, kk: (j, j + 2 * kk)
    return mm_band(x, w, mode, name, (S // TS, 2, 2), ((TS, HS), wb, (TS, HC)),
                   (lambda i, j, kk: (i, j + 2 * kk), wm, lambda i, j, kk: (i, j)), (S, NG * NH), add=add)


def s5_outer(a, b, name, states_first, z_off=0):
    if states_first:
        return mm_band(a, b, "tn", name, (4, 1, 1), ((S, HS), (S, HC), (HS, HC)),
                       (lambda i, j, kk: (0, i), lambda i, j, kk: (0, i % 2), lambda i, j, kk: (i, i % 2)),
                       (2 * NS, NG * NH))
    return mm_band(a, b, "tn", name, (1, 4, 1), ((S, HC), (S, HS), (HC, HS)),
                   (lambda i, j, kk: (0, z_off + j % 2), lambda i, j, kk: (0, j), lambda i, j, kk: (j % 2, j)),
                   (NG * NH, 2 * NS))


def _fwd_even(i, x, P, W):
    hn = rms_fwd(x, P["norm_ab"][i:i + 1], "rms_ab_fwd")
    z = mm(m2(hn), W["w_in"], "nn", "in_ab")
    o, lse, cat = attn_fwd(z)
    if "more" in W:
        W.update(W.pop("more")(cat))
    cat = pool_fwd(z, W["pool_w"], P["pool_scale"][i:i + 1], cat)
    x_mid = mm(m2(cat), W["w_out"], "nn", "out_ab", add=m2(x))
    return x_mid, dict(x=x, hn=hn, z=z, o=o, lse=lse, cat=cat)


def _bwd_even(i, dx_mid, sv, P, W, G, GW):
    z = sv["z"]
    d_cat = mm(m2(dx_mid), W["w_out"], "nt", "out_ab_dx")
    GW["w_out"] = mm(m2(sv["cat"]), m2(dx_mid), "tn", "out_ab_dw").reshape(4, 512, D)
    dq, dk, dv, dga = attn_bwd(z, d_cat, sv["o"], sv["lse"])
    dvb, dgb, dpw, dps = pool_bwd(z, d_cat, W["pool_w"], P["pool_scale"][i:i + 1])
    GW["pool_w"] = dpw.reshape(4, 4, 64, 256).transpose(1, 0, 2, 3).reshape(4, 256, 256)
    G["pool_scale"][i] = dps[0]
    d_z = assemble_dz_even((dq, dk, dv, dga, dvb, dgb))
    d_hn = mm(m2(d_z), W["w_in"], "nt", "in_ab_dx")
    GW["w_in"] = mm(m2(sv["hn"]), m2(d_z), "tn", "in_ab_dw", out=outcs(D, 1536))
    return d_hn, P["norm_ab"][i:i + 1], "norm_ab", "rms_ab_bwd"


def _fwd_odd(i, x, P, W):
    hn = rms_fwd(x, P["norm_cd"][i:i + 1], "rms_cd_fwd")
    z = mm(m2(hn), W["w_in"], "nn", "in_cd")
    bfull = jnp.repeat(P["sgu_b"][i].T, 256, axis=1)
    c_out = sgu_fwd(z, P["sgu_ln_g"][i:i + 1], P["sgu_ln_b"][i:i + 1], P["sgu_w"][i], bfull)
    disc, disc_vjp = jax.vjp(_s5_disc, P["s5_a_re"][i], P["s5_a_im"][i], P["s5_log_dt"][i],
                             P["s5_b_re"][i], P["s5_b_im"][i])
    abr, abi, bbr, bbi = disc
    bbd, cbd = s5_embed(bbr.transpose(0, 2, 1), bbi.transpose(0, 2, 1),
                        P["s5_c_re"][i].transpose(0, 2, 1), P["s5_c_im"][i].transpose(0, 2, 1))
    abr, abi = abr.reshape(1, NS), abi.reshape(1, NS)
    bu = s5_to_states(z, bbd, "nn", "s5_bu", z_off=3072 // HC)
    h = scan_fwd(bu, abr, abi)
    hc = s5_to_channels(h, cbd, "nn", "s5_hc")
    dskip = P["s5_d"][i:i + 1]
    ypre, yg = s5_post(hc, z, dskip)
    if "more" in W:
        W.update(W.pop("more")(yg))
    w12 = W["w12"]
    t = mm(m2(yg), m2(w12), "nn", "glu_t")
    cat = glu_fwd(t, z, c_out)
    x_mid = mm(m2(cat), W["w_out"], "nn", "out_cd", add=m2(x))
    return x_mid, dict(x=x, hn=hn, z=z, bfull=bfull, disc_vjp=disc_vjp, bbd=bbd, cbd=cbd, abr=abr,
                       abi=abi, h=h, ypre=ypre, yg=yg, w12=w12, t=t, cat=cat, dskip=dskip)


def _bwd_odd(i, dx_mid, sv, P, W, G, GW):
    z = sv["z"]
    d_cat = mm(m2(dx_mid), W["w_out"], "nt", "out_cd_dx")
    GW["w_out"] = mm(m2(sv["cat"]), m2(dx_mid), "tn", "out_cd_dw").reshape(4, 384, D)
    du, dv, dgc, dws, dbs, dlg, dlb = sgu_bwd(z, d_cat, P["sgu_ln_g"][i:i + 1], P["sgu_ln_b"][i:i + 1],
                                               P["sgu_w"][i], sv["bfull"])
    G["sgu_w"][i], G["sgu_b"][i] = dws, dbs[:, :4].T
    G["sgu_ln_g"][i], G["sgu_ln_b"][i] = dlg[0], dlb[0]
    dt, dgd = glu_bwd(sv["t"], z, d_cat)
    gw12 = mm(m2(sv["yg"]), m2(dt), "tn", "glu_dw")
    GW["glu_w1"] = gw12[:, :512].reshape(4, 128, 512)
    GW["glu_w2"] = gw12[:, 512:].reshape(4, 128, 512)
    dyg = mm(m2(dt), m2(sv["w12"]), "nt", "glu_dx")
    dypre, dxd1, dd = s5_post_bwd(dyg, sv["ypre"], z, sv["dskip"])
    G["s5_d"][i] = dd[0]
    gcbd = s5_outer(sv["h"], dypre, "s5_dc", states_first=True)
    eta = s5_to_states(dypre, sv["cbd"], "nt", "s5_eta")
    lam, dacc = scan_bwd(eta, sv["h"], sv["abr"], sv["abi"])
    gbbd = s5_outer(z, lam, "s5_db", states_first=False, z_off=3072 // HC)
    dxd = s5_to_channels(lam, sv["bbd"], "nt", "s5_dx", add=dxd1)
    dacc = jnp.sum(dacc, axis=0)
    dbt_re, dbt_im, dct_re, dct_im = s5_extract(gbbd, gcbd)
    G["s5_c_re"][i], G["s5_c_im"][i] = dct_re.transpose(0, 2, 1), dct_im.transpose(0, 2, 1)
    d_bbr, d_bbi = dbt_re.transpose(0, 2, 1), dbt_im.transpose(0, 2, 1)
    (G["s5_a_re"][i], G["s5_a_im"][i], G["s5_log_dt"][i], G["s5_b_re"][i], G["s5_b_im"][i]) = sv["disc_vjp"](
        (dacc[:NS].reshape(NG, NP), dacc[NS:].reshape(NG, NP), d_bbr, d_bbi))
    d_z = assemble_dz_odd(du, dv, dgc, dxd, dgd)
    d_hn = mm(m2(d_z), W["w_in"], "nt", "in_cd_dx")
    GW["w_in"] = mm(m2(sv["hn"]), m2(d_z), "tn", "in_cd_dw", out=outcs(D, 1024))
    return d_hn, P["norm_cd"][i:i + 1], "norm_cd", "rms_cd_bwd"


def _fwd_x(l, x, mem_n, P, W):
    hx = rms_fwd(x, P["norm_x"][l:l + 1], "rms_x_fwd")
    q = mm(m2(hx), W["w_xq"], "nn", "xq", out_dtype=BF16)
    kv = mm(m2(mem_n), W["w_xkv"], "nn", "xkv", out_dtype=BF16)
    ox = xattn_fwd(q, kv)
    x_out = mm(m2(ox), W["w_xo"], "nn", "xo", add=m2(x))
    return x_out, dict(x=x, hx=hx, q=q, kv=kv, ox=ox)


def _bwd_x(l, dx_out, sv, mem_n, d_memn, P, W, G, GW):
    d_ox = mm(m2(dx_out), W["w_xo"], "nt", "xo_dx", out_dtype=BF16)
    GW["w_xo"] = mm(m2(sv["ox"]), m2(dx_out), "tn", "xo_dw").reshape(4, 256, D)
    dq, dkv = xattn_bwd(sv["q"], sv["kv"], d_ox)
    GW["w_xq"] = mm(m2(sv["hx"]), m2(dq), "tn", "xq_dw").reshape(4, 256, D)
    d_hx = mm(m2(dq), W["w_xq"], "nt", "xq_dx")
    GW["w_xkv"] = mm(m2(mem_n), m2(dkv), "tn", "xkv_dw", out=outcs(D, 512))
    d_memn = mm(m2(dkv), W["w_xkv"], "nt", "xkv_dx", add=None if d_memn is None else m2(d_memn))
    dx, dg = rms_bwd(sv["x"], d_hx, dx_out, P["norm_x"][l:l + 1], "rms_x_bwd")
    G["norm_x"][l] = dg[0]
    return dx, d_memn


SMALL_LAYERS = (("norm_ab", 2), ("pool_scale", 2), ("norm_cd", 2), ("sgu_ln_g", 2), ("sgu_ln_b", 2), ("sgu_w", 2),
                ("sgu_b", 2), ("s5_a_re", 2), ("s5_a_im", 2), ("s5_log_dt", 2), ("s5_b_re", 2), ("s5_b_im", 2),
                ("s5_c_re", 2), ("s5_c_im", 2), ("s5_d", 2), ("norm_x", 4))


def local_step(x, mem, tgt, P, weights_of, grads_done):
    G = {k: [None] * n for k, n in SMALL_LAYERS}
    mem_g = P["mem_norm"].reshape(1, D)
    mem_n = rms_fwd(mem, mem_g, "rms_mem_fwd")
    saved = []
    for layer in range(4):
        i = layer // 2
        W = weights_of(layer, x)
        x, sv_m = (_fwd_even if layer % 2 == 0 else _fwd_odd)(i, x, P, W)
        x, sv_x = _fwd_x(layer, x, mem_n, P, W)
        saved.append((sv_m, sv_x, W))
    dx, loss, dgf = final_loss(x, tgt, P["final_norm"].reshape(1, D))
    G["final_norm"] = dgf[0]
    d_memn = None
    for layer in reversed(range(4)):
        i = layer // 2
        sv_m, sv_x, W = saved[layer]
        GW = {}
        dx_mid, d_memn = _bwd_x(layer, dx, sv_x, mem_n, d_memn, P, W, G, GW)
        d_hn, g, key, name = (_bwd_even if layer % 2 == 0 else _bwd_odd)(i, dx_mid, sv_m, P, W, G, GW)
        token = grads_done(layer, GW)
        if token is not None:
            g = g + token
        dx, dg = rms_bwd(sv_m["x"], d_hn, dx_mid, g, name)
        G[key][i] = dg[0]
    _, dgm = rms_bwd(mem, d_memn, d_memn, mem_g, "rms_mem_bwd")
    G["mem_norm"] = dgm[0]
    return loss, dx, G


ANY = pl.BlockSpec(memory_space=pl.ANY)


def _place():
    x, y, c = lax.axis_index("x"), lax.axis_index("y"), lax.axis_index("c")
    chips = [(1 - x, y), (x, 1 - y), (1 - x, 1 - y)]
    return x, y, c, 2 * x + y, (x, y, 1 - c), chips


def _remote(src, dst, send, recv, k, dev):
    return pltpu.make_async_remote_copy(src_ref=src, dst_ref=dst, send_sem=send.at[k], recv_sem=recv.at[k],
                                        device_id=dev, device_id_type=MESHID)


HBM = pl.BlockSpec(memory_space=pltpu.HBM)
SEM = pl.BlockSpec(memory_space=pltpu.SEMAPHORE)
EFFECT = pltpu.SideEffectType.DATAFLOW_SIDE_EFFECTING


def _hbm(t):
    return pltpu.with_memory_space_constraint(t, pltpu.HBM)


def allgather_sync(shards):
    n = len(shards)

    def body(*refs):
        ins, outs = refs[:n], refs[n:2 * n]
        token, send, recv = refs[2 * n:]
        x, y, c, jme, sib, chips = _place()
        first, passed = [], []
        for a in range(n):
            cp = _remote(ins[a], outs[a].at[jme], send, recv, a * 7 + 6, sib)
            cp.start()
            first.append(cp)
            for k, chip in enumerate(chips):
                cp = _remote(ins[a].at[c], outs[a].at[jme, c], send, recv, a * 7 + k, (*chip, c))
                cp.start()
                first.append(cp)
        for a in range(n):
            for k, chip in enumerate(chips):
                piece = outs[a].at[2 * chip[0] + chip[1], c]
                _remote(piece, piece, send, recv, a * 7 + k, (*chip, c)).wait_recv()
                fw = _remote(piece, piece, send, recv, a * 7 + 3 + k, sib)
                fw.start()
                passed.append(fw)
        for a in range(n):
            own = outs[a].at[jme]
            _remote(own, own, send, recv, a * 7 + 6, sib).wait_recv()
            for k, chip in enumerate(chips):
                piece = outs[a].at[2 * chip[0] + chip[1], 1 - c]
                _remote(piece, piece, send, recv, a * 7 + 3 + k, sib).wait_recv()
        for cp in first + passed:
            cp.wait_send()
        token[...] = jnp.zeros_like(token)

    res = pl.pallas_call(
        body, name="allgather_sync", in_specs=[ANY] * n,
        out_specs=[ANY] * n + [pl.BlockSpec(memory_space=pltpu.VMEM)],
        out_shape=[jax.ShapeDtypeStruct((4,) + s.shape, s.dtype) for s in shards] + [jax.ShapeDtypeStruct((8, 128), F32)],
        scratch_shapes=[pltpu.SemaphoreType.DMA((7 * n,)), pltpu.SemaphoreType.DMA((7 * n,))],
    )(*shards)
    return list(res[:n]), res[n]


def _gather_copies(ins, lands, send, recv):
    x, y, c, jme, sib, chips = _place()
    devs = [(*chip, c) for chip in chips] + [sib]
    return [_remote(ins[a], lands[a].at[jme], send, recv, a * 4 + k, dev)
            for a in range(len(ins)) for k, dev in enumerate(devs)]


def allgather_start(shards, after, name):
    n, na = len(shards), len(after)

    def body(*refs):
        ins, lands = refs[:n], refs[n:2 * n]
        send, recv = refs[2 * n + na], refs[2 * n + na + 1]
        token = refs[-1]
        for cp in _gather_copies(ins, lands, send, recv):
            cp.start()
        token[...] = jnp.zeros_like(token)

    res = pl.pallas_call(
        body, name=name,
        out_shape=(pltpu.SemaphoreType.DMA((4 * n,)), pltpu.SemaphoreType.DMA((4 * n,)),
                   *[pltpu.HBM(s.shape, s.dtype) for s in shards],
                   *[pltpu.HBM((4,) + s.shape, s.dtype) for s in shards],
                   jax.ShapeDtypeStruct((8, 128), F32)),
        in_specs=[HBM] * (2 * n) + [ANY] * na,
        out_specs=(SEM, SEM, *[HBM] * (2 * n), pl.BlockSpec(memory_space=pltpu.VMEM)),
        input_output_aliases={a: 2 + a for a in range(2 * n)},
        compiler_params=pltpu.CompilerParams(has_side_effects=EFFECT),
    )(*[_hbm(s) for s in shards], *[_hbm(lax.empty((4,) + s.shape, s.dtype)) for s in shards], *after)
    return res[0], res[1], list(res[2:2 + n]), list(res[2 + n:2 + 2 * n]), res[-1]


def allgather_wait(send, recv, shards, lands, after, name):
    n = len(shards)

    def body(*refs):
        ins, zones = refs[:n], refs[n:2 * n]
        send_r, recv_r = refs[2 * n], refs[2 * n + 1]
        x, y, c, jme, sib, chips = _place()
        slots = [2 * chip[0] + chip[1] for chip in chips] + [jme]
        for a in range(n):
            for k, slot in enumerate(slots):
                cp = _remote(ins[a], zones[a].at[slot], send_r, recv_r, a * 4 + k, sib)
                cp.wait_send()
                cp.wait_recv()

    res = pl.pallas_call(
        body, name=name,
        out_shape=tuple(pltpu.HBM(t.shape, t.dtype) for t in list(shards) + list(lands)),
        in_specs=[HBM] * (2 * n) + [SEM, SEM, ANY], out_specs=tuple([HBM] * (2 * n)),
        input_output_aliases={a: a for a in range(2 * n)},
        compiler_params=pltpu.CompilerParams(has_side_effects=EFFECT),
    )(*shards, *lands, send, recv, after)
    return list(res[n:])


def allgather_small(slab):
    def body(in_ref, out_ref, send, recv, lsem):
        x, y, c, jme, sib, chips = _place()
        loc = pltpu.make_async_copy(in_ref, out_ref.at[jme], lsem.at[0])
        loc.start()
        cps = [_remote(in_ref, out_ref.at[jme], send, recv, k, (*chip, c)) for k, chip in enumerate(chips)]
        for cp in cps:
            cp.start()
        for k, chip in enumerate(chips):
            piece = out_ref.at[2 * chip[0] + chip[1]]
            _remote(piece, piece, send, recv, k, (*chip, c)).wait_recv()
        for cp in cps:
            cp.wait_send()
        loc.wait()

    return pl.pallas_call(
        body, name="allgather_small", in_specs=[ANY], out_specs=ANY,
        out_shape=jax.ShapeDtypeStruct((4,) + slab.shape, slab.dtype),
        scratch_shapes=[pltpu.SemaphoreType.DMA((3,)), pltpu.SemaphoreType.DMA((3,)), pltpu.SemaphoreType.DMA((1,))],
    )(slab)


def allreduce_small(v):
    hr = v.shape[0] // 2

    def body(v_ref, o_ref, r0, r1, r2, send, recv):
        x, y, c, jme, sib, chips = _place()
        mine = pl.ds(pl.multiple_of(c * hr, 8), hr)
        other = pl.ds(pl.multiple_of((1 - c) * hr, 8), hr)
        cp = _remote(v_ref.at[other], r0, send, recv, 0, sib)
        cp.start()
        cp.wait()
        o_ref[mine, :] = v_ref[mine, :] + r0[...]
        for k, (buf, peer) in enumerate(((r1, (1 - x, y, c)), (r2, (x, 1 - y, c))), start=1):
            cp = _remote(o_ref.at[mine], buf, send, recv, k, peer)
            cp.start()
            cp.wait()
            o_ref[mine, :] = o_ref[mine, :] + buf[...]
        cp = _remote(o_ref.at[mine], o_ref.at[mine], send, recv, 3, sib)
        cp.start()
        cp.wait_send()
        _remote(o_ref.at[other], o_ref.at[other], send, recv, 3, sib).wait_recv()

    vm = pl.BlockSpec(memory_space=pltpu.VMEM)
    half = pltpu.VMEM((hr, v.shape[1]), v.dtype)
    return pl.pallas_call(
        body, name="allreduce_small", in_specs=[vm], out_specs=vm,
        out_shape=jax.ShapeDtypeStruct(v.shape, v.dtype),
        scratch_shapes=[half] * 3 + [pltpu.SemaphoreType.DMA((4,)), pltpu.SemaphoreType.DMA((4,))],
        compiler_params=pltpu.CompilerParams(vmem_limit_bytes=VMEM_LIMIT),
    )(v)


def _pair_copies(gs, lands, send, recv):
    x, y, c, jme, sib, chips = _place()
    return [_remote(gs[a].at[:, 1 - c], lands[a], send, recv, a, sib) for a in range(len(gs))]


def rs_pair_start(gs, name):
    n = len(gs)

    def body(*refs):
        ins, lands = refs[:n], refs[n:2 * n]
        send, recv = refs[2 * n], refs[2 * n + 1]
        token = refs[-1]
        for cp in _pair_copies(ins, lands, send, recv):
            cp.start()
        token[...] = jnp.zeros_like(token)

    shapes = [(4,) + g.shape[2:] for g in gs]
    res = pl.pallas_call(
        body, name=name,
        out_shape=(pltpu.SemaphoreType.DMA((n,)), pltpu.SemaphoreType.DMA((n,)),
                   *[pltpu.HBM(g.shape, g.dtype) for g in gs], *[pltpu.HBM(s, F32) for s in shapes],
                   jax.ShapeDtypeStruct((8, 128), F32)),
        in_specs=[HBM] * (2 * n), out_specs=(SEM, SEM, *[HBM] * (2 * n), pl.BlockSpec(memory_space=pltpu.VMEM)),
        input_output_aliases={a: 2 + a for a in range(2 * n)},
        compiler_params=pltpu.CompilerParams(has_side_effects=EFFECT),
    )(*[_hbm(g) for g in gs], *[_hbm(lax.empty(s, F32)) for s in shapes])
    return res[0], res[1], list(res[2:2 + n]), list(res[2 + n:2 + 2 * n]), res[-1]


def rs_pair_wait(send, recv, gs, lands, after, name):
    n = len(gs)

    def body(*refs):
        ins, zones = refs[:n], refs[n:2 * n]
        for cp in _pair_copies(ins, zones, refs[2 * n], refs[2 * n + 1]):
            cp.wait_send()
            cp.wait_recv()

    res = pl.pallas_call(
        body, name=name,
        out_shape=tuple(pltpu.HBM(t.shape, t.dtype) for t in list(gs) + list(lands)),
        in_specs=[HBM] * (2 * n) + [SEM, SEM, ANY], out_specs=tuple([HBM] * (2 * n)),
        input_output_aliases={a: a for a in range(2 * n)},
        compiler_params=pltpu.CompilerParams(has_side_effects=EFFECT),
    )(*gs, *lands, send, recv, after)
    return list(res[:n]), list(res[n:])


SUM_ROWS = 256


def rs_pair_sum(g4s, gots, cidx):
    n = len(g4s)
    tiles = [(min(g.shape[2], SUM_ROWS), g.shape[3]) for g in g4s]
    nts = [g.shape[2] // tr for g, (tr, _) in zip(g4s, tiles)]

    def at(a, s):
        s = jnp.minimum(s, 4 * nts[a] - 1)
        return s // nts[a], s % nts[a]

    def body(c_ref, *refs):
        for a in range(n):
            refs[2 * n + a][...] = (refs[a][...] + refs[n + a][...]).astype(BF16)

    in_specs = [pl.BlockSpec((None, None) + tiles[a], lambda s, cr, a=a: (at(a, s)[0], cr[0], at(a, s)[1], 0))
                for a in range(n)]
    in_specs += [pl.BlockSpec((None,) + tiles[a], lambda s, cr, a=a: (*at(a, s), 0)) for a in range(n)]
    return pl.pallas_call(
        body, name="rs_pair_sum",
        grid_spec=pltpu.PrefetchScalarGridSpec(
            num_scalar_prefetch=1, grid=(4 * max(nts),), in_specs=in_specs,
            out_specs=[pl.BlockSpec((None,) + tiles[a], lambda s, cr, a=a: (*at(a, s), 0)) for a in range(n)]),
        out_shape=[jax.ShapeDtypeStruct((4,) + g.shape[2:], BF16) for g in g4s],
        compiler_params=_cparams(("arbitrary",)),
    )(cidx, *g4s, *gots)


def _chip_copies(ps, lands, send, recv):
    x, y, c, jme, sib, chips = _place()
    return [_remote(ps[a].at[2 * chip[0] + chip[1]], lands[a].at[jme], send, recv, a * 3 + k, (*chip, c))
            for a in range(len(ps)) for k, chip in enumerate(chips)]


def rs_chip_start(ps, name):
    n = len(ps)

    def body(*refs):
        ins, lands = refs[:n], refs[n:2 * n]
        send, recv = refs[2 * n], refs[2 * n + 1]
        token = refs[-1]
        for cp in _chip_copies(ins, lands, send, recv):
            cp.start()
        token[...] = jnp.zeros_like(token)

    res = pl.pallas_call(
        body, name=name,
        out_shape=(pltpu.SemaphoreType.DMA((3 * n,)), pltpu.SemaphoreType.DMA((3 * n,)),
                   *[pltpu.HBM(p.shape, p.dtype) for p in ps], *[pltpu.HBM(p.shape, p.dtype) for p in ps],
                   jax.ShapeDtypeStruct((8, 128), F32)),
        in_specs=[HBM] * (2 * n), out_specs=(SEM, SEM, *[HBM] * (2 * n), pl.BlockSpec(memory_space=pltpu.VMEM)),
        input_output_aliases={a: 2 + a for a in range(2 * n)},
        compiler_params=pltpu.CompilerParams(has_side_effects=EFFECT),
    )(*[_hbm(p) for p in ps], *[_hbm(lax.empty(p.shape, p.dtype)) for p in ps])
    return res[0], res[1], list(res[2:2 + n]), list(res[2 + n:2 + 2 * n]), res[-1]


def rs_chip_wait(send, recv, ps, lands, after, name):
    n = len(ps)

    def body(*refs):
        ins, zones = refs[:n], refs[n:2 * n]
        send_r, recv_r = refs[2 * n], refs[2 * n + 1]
        x, y, c, jme, sib, chips = _place()
        for a in range(n):
            for k, chip in enumerate(chips):
                jt = 2 * chip[0] + chip[1]
                cp = _remote(ins[a].at[jt], zones[a].at[jt], send_r, recv_r, a * 3 + k, (*chip, c))
                cp.wait_send()
                cp.wait_recv()

    res = pl.pallas_call(
        body, name=name,
        out_shape=tuple(pltpu.HBM(p.shape, p.dtype) for p in list(ps) + list(lands)),
        in_specs=[HBM] * (2 * n) + [SEM, SEM] + [ANY] * len(after), out_specs=tuple([HBM] * (2 * n)),
        input_output_aliases={a: a for a in range(2 * n)},
        compiler_params=pltpu.CompilerParams(has_side_effects=EFFECT),
    )(*ps, *lands, send, recv, *after)
    return list(res[:n]), list(res[n:])


def rs_chip_sum(qs, ps, ls, accs, layers, jc):
    n = len(qs)
    tiles = [(min(q.shape[1], SUM_ROWS), q.shape[2]) for q in qs]
    nts = [q.shape[1] // tr for q, (tr, _) in zip(qs, tiles)]

    def at(a, s):
        return jnp.minimum(s, nts[a] - 1)

    def body(jc_ref, *refs):
        jme = jc_ref[0]
        for a in range(n):
            q_ref, p_ref, o_ref = refs[a], refs[n + a], refs[len(refs) - n + a]
            own = p_ref[...].astype(F32)
            v = [jnp.where(jme == j, own, q_ref[j].astype(F32)) for j in range(4)]
            o_ref[...] = ((v[0] + v[1]) + v[2]) + v[3]

    in_specs = [pl.BlockSpec((4,) + tiles[a], lambda s, jr, a=a: (0, at(a, s), 0)) for a in range(n)]
    in_specs += [pl.BlockSpec((None,) + tiles[a], lambda s, jr, a=a: (jr[0], at(a, s), 0)) for a in range(n)]
    args, aliases = [jc, *qs, *ps], {}
    for a in range(n):
        if accs[a] is not None:
            aliases[len(args)] = a
            in_specs.append(ANY)
            args.append(accs[a])
    return pl.pallas_call(
        body, name="rs_chip_sum",
        grid_spec=pltpu.PrefetchScalarGridSpec(
            num_scalar_prefetch=1, grid=(max(nts),), in_specs=in_specs,
            out_specs=[pl.BlockSpec((None, None) + tiles[a], lambda s, jr, a=a: (ls[a], jr[1], at(a, s), 0))
                       for a in range(n)]),
        out_shape=[jax.ShapeDtypeStruct((layers[a], 2) + qs[a].shape[1:], F32) for a in range(n)],
        input_output_aliases=aliases,
        compiler_params=_cparams(("arbitrary",)),
    )(*args)


def rs_pair_gather(rs):
    n = len(rs)

    def body(*refs):
        outs = refs[n:2 * n]
        send, recv = refs[2 * n:]
        x, y, c, jme, sib, chips = _place()
        cps = [_remote(outs[a].at[:, c], outs[a].at[:, c], send, recv, a, sib) for a in range(n)]
        for cp in cps:
            cp.start()
        for a in range(n):
            slot = outs[a].at[:, 1 - c]
            _remote(slot, slot, send, recv, a, sib).wait_recv()
        for cp in cps:
            cp.wait_send()

    return pl.pallas_call(
        body, name="rs_pair_gather", in_specs=[ANY] * n, out_specs=[ANY] * n,
        out_shape=[jax.ShapeDtypeStruct(r.shape, r.dtype) for r in rs],
        input_output_aliases={a: a for a in range(n)},
        scratch_shapes=[pltpu.SemaphoreType.DMA((n,)), pltpu.SemaphoreType.DMA((n,))],
    )(*rs)


def _adamw_math(w, g, m, v):
    m = B1 * m + (1.0 - B1) * g
    v = B2 * v + (1.0 - B2) * (g * g)
    m_hat = m / (1.0 - B1 ** STEP)
    v_hat = v / (1.0 - B2 ** STEP)
    return -LR * (m_hat / (jnp.sqrt(v_hat) + AEPS) + WD * w), m, v


ADAMW_TILE = 512 * 1024


def adamw(w, g, m, v, name, with_grad=False):
    rows, cols = w.shape
    tr = next((t for t in (1024, 512, 256) if rows % t == 0 and t * cols <= ADAMW_TILE), rows)
    fn =(lambda wv, gv, mv, vv: (gv,) + _adamw_math(wv, gv, mv, vv)) if with_grad else _adamw_math
    return rw(fn, [(a, 0, cols) for a in (w, g, m, v)], [(cols, F32)] * (4 if with_grad else 3), name, rows, tr=tr)


def adamw_small(ws, gs, ms, vs):
    n = len(ws)

    def body(*refs):
        for a in range(n):
            res = _adamw_math(*[refs[k * n + a][...] for k in range(4)])
            for k in range(3):
                refs[(4 + k) * n + a][...] = res[k]

    res = pl.pallas_call(
        body, name="adamw_small", in_specs=[VM] * (4 * n), out_specs=[VM] * (3 * n),
        out_shape=[jax.ShapeDtypeStruct(w.shape, F32) for _ in range(3) for w in ws],
        compiler_params=pltpu.CompilerParams(vmem_limit_bytes=VMEM_LIMIT),
    )(*ws, *gs, *ms, *vs)
    return [(res[a], res[n + a], res[2 * n + a]) for a in range(n)]


WEIGHTS = ["norm_ab", "w_in_ab", "pool_w", "pool_scale", "w_out_ab", "norm_cd", "w_in_cd", "sgu_ln_g", "sgu_ln_b",
           "sgu_w", "sgu_b", "s5_a_re", "s5_a_im", "s5_log_dt", "s5_b_re", "s5_b_im", "s5_c_re", "s5_c_im", "s5_d",
           "glu_w1", "glu_w2", "w_out_cd", "norm_x", "w_xq", "w_xkv", "w_xo", "mem_norm", "final_norm"]
INPUTS = ["x", "mem"] + WEIGHTS + ["loss_target"] + ["m_" + n for n in WEIGHTS] + ["v_" + n for n in WEIGHTS]
BIG = ["w_in_ab", "w_out_ab", "w_in_cd", "w_out_cd", "w_xq", "w_xkv", "w_xo", "glu_w1", "glu_w2", "pool_w"]
COL_SHARDED = ("w_in_ab", "w_in_cd", "w_xkv")
SMALL = [n for n in WEIGHTS if n not in BIG]
SMALL_SHARDED = {"norm_cd": 256, "sgu_ln_g": 256, "sgu_ln_b": 256, "s5_d": 128}
PACK = 256 * 128


def _pack(arrs):
    flat = jnp.concatenate([a.reshape(-1) for a in arrs])
    pad = (-flat.shape[0]) % PACK
    return jnp.concatenate([flat, jnp.zeros((pad,), flat.dtype)]).reshape(-1, 128)


def _unpack(packed, shapes):
    flat, out, off = packed.reshape(-1), [], 0
    for s in shapes:
        n = 1
        for d in s:
            n *= d
        out.append(flat[off:off + n].reshape(s))
        off += n
    return out


LAYER_KEYS = (("w_in", "w_out", "pool_w", "w_xq", "w_xkv", "w_xo"),
              ("w_in", "w_out", "glu_w1", "glu_w2", "w_xq", "w_xkv", "w_xo"))


def _weight_of(key, layer):
    if key in ("w_xq", "w_xkv", "w_xo"):
        return key, layer, 4
    kind = "ab" if layer % 2 == 0 else "cd"
    return {"w_in": "w_in_" + kind, "w_out": "w_out_" + kind}.get(key, key), layer // 2, 2


def kernel(*args):
    a = dict(zip(INPUTS, args))
    x_i, y_i, c_i = lax.axis_index("x"), lax.axis_index("y"), lax.axis_index("c")
    j = 2 * x_i + y_i

    slab = jnp.concatenate([a["norm_cd"], a["sgu_ln_g"], a["sgu_ln_b"],
                            jnp.pad(a["s5_d"], ((0, 0), (0, 128)))], axis=0)
    gslab = allgather_small(slab)
    P = {n: a[n] for n in SMALL}
    for k, n in enumerate(("norm_cd", "sgu_ln_g", "sgu_ln_b", "s5_d")):
        wd = SMALL_SHARDED[n]
        P[n] = gslab[:, 2 * k:2 * k + 2, :wd].transpose(1, 0, 2).reshape(2, 4 * wd)

    def shards_of(layer):
        keys = sorted(k for k in LAYER_KEYS[layer % 2])
        out = []
        for k in keys:
            n, l, _ = _weight_of(k, layer)
            out.append(a[n][l].reshape(-1, a[n].shape[-1]).astype(BF16))
        return keys, out

    keys0, sh0 = shards_of(0)
    first = keys0.index("w_in")
    g_in, token = allgather_sync([sh0[first].reshape(2, sh0[first].shape[0] // 2, sh0[first].shape[1])])
    w_in0 = g_in[0].reshape(4, -1, g_in[0].shape[-1])
    started = {}
    for layer in (0, 1, 2, 3):
        keys, sh = (keys0, sh0) if layer == 0 else shards_of(layer)
        rest = [(k, s) for k, s in zip(keys, sh) if k != "w_in"]
        parts = [("in", ["w_in"], [sh[keys.index("w_in")]])] * (layer > 0) + [("", *map(list, zip(*rest)))]
        for tag, pk, ps in parts:
            send, recv, ps, lands, token = allgather_start(ps, [token, gslab], "allgather_start_%d%s" % (layer, tag))
            started[(layer, tag)] = (pk, send, recv, ps, lands)
    P["norm_ab"] = P["norm_ab"] + token[0:1, 0:1]

    cidx = jnp.reshape(c_i, (1,)).astype(jnp.int32)
    jc = jnp.stack([j, c_i]).astype(jnp.int32)

    def views(g):
        W = {}
        for k, v in g.items():
            if k in ("w_in", "w_xkv"):
                W[k] = mcs(v)
            elif k == "pool_w":
                W[k] = v.reshape(4, 4, 64, 256).transpose(1, 0, 2, 3).reshape(4, 256, 256)
            elif k not in ("glu_w1", "glu_w2"):
                W[k] = m2(v.reshape(-1, v.shape[-1]))
        if "glu_w1" in g:
            W["w12"] = jnp.concatenate([g["glu_w1"].reshape(512, 512), g["glu_w2"].reshape(512, 512)], axis=1)
        return W

    def arrived(layer, tag, after):
        keys, send, recv, sh, lands = started[(layer, tag)]
        return views(dict(zip(keys, allgather_wait(send, recv, sh, lands, after, "allgather_wait_%d%s" % (layer, tag)))))

    def weights_of(layer, x_in):
        W = views({"w_in": w_in0}) if layer == 0 else arrived(layer, "in", x_in)
        W["more"] = lambda after: arrived(layer, "", after)
        return W

    halves, pending = {}, {}

    def finish_pair(layer, after):
        keys, send, recv, flat, lands = halves.pop(layer)
        flat, got = rs_pair_wait(send, recv, flat, lands, after, "rs_pair_wait_%d" % layer)
        pair = rs_pair_sum(flat, got, cidx)
        send, recv, pair, lands, token = rs_chip_start(pair, "rs_chip_start_%d" % layer)
        pending[layer] = (keys, send, recv, pair, lands)
        return token

    def grads_done(layer, GW):
        keys = sorted(GW)
        flat = [GW[k].reshape(4, 2, GW[k].shape[1] // 2, GW[k].shape[2]) for k in keys]
        send, recv, flat, lands, token = rs_pair_start(flat, "rs_pair_start_%d" % layer)
        halves[layer] = (keys, send, recv, flat, lands)
        if layer + 1 in halves:
            token = token + finish_pair(layer + 1, token)
        return token[0:1, 0:1]

    loss, dx, G = local_step(a["x"][0], a["mem"][0], a["loss_target"][0], P, weights_of, grads_done)
    loss = lax.psum(loss[0, 0], ("x", "y", "c"))
    finish_pair(0, dx)
    outs = {}

    def update_big(names, red):
        for n, g in zip(names, rs_pair_gather([red[n] for n in names])):
            shp = a[n].shape
            g2 = g.reshape(-1, shp[-1])
            upd = adamw(a[n].reshape(g2.shape), g2, a["m_" + n].reshape(g2.shape), a["v_" + n].reshape(g2.shape),
                        "adamw_" + n, with_grad=True)
            outs[n] = tuple(t.reshape(shp) for t in upd)

    def reduce_layer(layer, red, after):
        keys, send, recv, pair, lands = pending[layer]
        pair, lands = rs_chip_wait(send, recv, pair, lands, after, "rs_chip_wait_%d" % layer)
        which = [_weight_of(k, layer) for k in keys]
        sums = rs_chip_sum(lands, pair, [l for _, l, _ in which], [red.get(n) for n, _, _ in which],
                           [layers for _, _, layers in which], jc)
        red.update(zip([n for n, _, _ in which], sums))

    red = {}
    for layer in (3, 2, 1):
        reduce_layer(layer, red, [dx])
    odd_only = [n for n in BIG if n.endswith("_cd") or n.startswith("glu")]
    update_big(odd_only, red)

    pieces = [g for n in SMALL for g in (G[n] if isinstance(G[n], list) else [G[n]])]
    shapes = [((len(G[n]),) + G[n][0].shape) if isinstance(G[n], list) else G[n].shape for n in SMALL]
    gsum = _unpack(allreduce_small(_pack(pieces)), shapes)
    gloc = []
    for n, g in zip(SMALL, gsum):
        if n in SMALL_SHARDED:
            g = lax.dynamic_slice_in_dim(g, j * SMALL_SHARDED[n], SMALL_SHARDED[n], axis=1)
        gloc.append(g)
    two = [(-1, a[n].shape[-1]) if a[n].ndim > 1 else (1, a[n].shape[0]) for n in SMALL]
    upds = adamw_small(*[[t.reshape(s) for t, s in zip(ts, two)]
                         for ts in ([a[n] for n in SMALL], gloc, [a["m_" + n] for n in SMALL],
                                    [a["v_" + n] for n in SMALL])])
    for n, g, upd in zip(SMALL, gloc, upds):
        outs[n] = (g,) + tuple(t.reshape(a[n].shape) for t in upd)

    behind = [outs[n][1] for n in odd_only + SMALL[-1:]] + [red[n] for n in BIG if n not in odd_only]
    reduce_layer(0, red, behind)
    update_big([n for n in BIG if n not in odd_only], red)

    res = [loss, dx[None]]
    for part in range(4):
        res += [outs[n][part] for n in WEIGHTS]
    return tuple(res)
```

```python
import math

import jax
import jax.numpy as jnp
from jax import lax
from jax.experimental import pallas as pl
from jax.experimental.pallas import tpu as pltpu

F32, BF16 = jnp.float32, jnp.bfloat16
S, D = 2048, 1024
MEM = 256
EPS = 1e-6
NEG = -1e30
QB = 128
PATTERNS = (1, 4, 16)
NG, NP, NH = 32, 64, 16
NS = NG * NP
LR, B1, B2, AEPS, WD, STEP = 0.001, 0.9, 0.999, 1e-08, 0.01, 10
MESHID = pl.DeviceIdType.MESH
VMEM_LIMIT = 56 * 1024 * 1024


def _cparams(sem):
    return pltpu.CompilerParams(dimension_semantics=sem, vmem_limit_bytes=VMEM_LIMIT)


def _sig(x):
    return 1.0 / (1.0 + jnp.exp(-x))


def _dot(a, b, dims):
    return lax.dot_general(a, b, (dims, ((), ())), preferred_element_type=F32)


def _nn(a, b):
    return _dot(a, b, ((1,), (0,)))


def _nt(a, b):
    return _dot(a, b, ((1,), (1,)))


def _tn(a, b):
    return _dot(a, b, ((0,), (0,)))


_DIMS = {"nn": ((1,), (0,)), "nt": ((1,), (1,)), "tn": ((0,), (0,))}


def _tile(dim, cc=None, cap=1024):
    for t in (2048, 1536, 1024, 768, 512, 384, 256, 128):
        if t <= cap and dim % t == 0 and (cc is None or cc % t == 0):
            return t
    return dim


MM_VMEM = 36 * 1024 * 1024


def _mm_tiles(m, n, k, ccm, ccn, cck, a_bytes, b_bytes, o_bytes):
    caps = [1024, 1024, 2048]
    while True:
        tm, tn, tk = _tile(m, ccm, caps[0]), _tile(n, ccn, caps[1]), _tile(k, cck, caps[2])
        need = 2 * (tm * tk * a_bytes + tk * tn * b_bytes + tm * tn * o_bytes) + (tm * tn * 4 if tk < k else 0)
        if need <= MM_VMEM:
            return tm, tn, tk
        if tk > 1024:
            caps[2] = tk // 2
        elif tn >= tm:
            caps[1] = tn // 2
        else:
            caps[0] = tm // 2


def m2(arr, col_off=0, ncols=None):
    rows, cols = arr.shape
    ncols = cols - col_off if ncols is None else ncols

    def spec(tr, tc, rc):
        assert col_off % tc == 0
        return pl.BlockSpec((tr, tc), lambda *g: (rc(*g)[0], rc(*g)[1] + col_off // tc))
    return (arr, rows, ncols, spec, None if col_off == 0 else col_off)


def mcs(arr):
    cs = arr.shape[2]

    def spec(tr, tc, rc):
        n = cs // tc
        return pl.BlockSpec((None, tr, tc), lambda *g: (rc(*g)[1] // n, rc(*g)[0], rc(*g)[1] % n))
    return (arr, arr.shape[1], 4 * cs, spec, cs)


def out2(rows, cols):
    def spec(tr, tc, rc):
        return pl.BlockSpec((tr, tc), lambda *g: tuple(rc(*g)))
    return ((rows, cols), spec, None)


def outcs(rows, cs):
    def spec(tr, tc, rc):
        n = cs // tc
        return pl.BlockSpec((None, tr, tc), lambda *g: (rc(*g)[1] // n, rc(*g)[0], rc(*g)[1] % n))
    return ((4, rows, cs), spec, cs)


def _both(a, b):
    if a is None:
        return b
    if b is None:
        return a
    return math.gcd(a, b)


def mm(a, b, mode, name, add=None, out=None, out_dtype=F32):
    a_arr, a_r, a_c, a_spec, a_cc = a
    b_arr, b_r, b_c, b_spec, b_cc = b
    if mode == "nn":
        m, k, n = a_r, a_c, b_c
        assert b_r == k
        ccm, cck, ccn = None, a_cc, b_cc
    elif mode == "nt":
        m, k, n = a_r, a_c, b_r
        assert b_c == k
        ccm, cck, ccn = None, _both(a_cc, b_cc), None
    else:
        m, k, n = a_c, a_r, b_c
        assert b_r == k
        ccm, cck, ccn = a_cc, None, b_cc
    out = out2(m, n) if out is None else out
    o_shape, o_spec, o_cc = out
    ccn = _both(ccn, o_cc)
    if add is not None:
        ccn = _both(ccn, add[4])
    o_bytes = jnp.dtype(out_dtype).itemsize + (0 if add is None else add[0].dtype.itemsize)
    tm, tn, tk = _mm_tiles(m, n, k, ccm, ccn, cck, a_arr.dtype.itemsize, b_arr.dtype.itemsize, o_bytes)
    nk = k // tk
    if mode == "nn":
        in_specs = [a_spec(tm, tk, lambda i, j, kk: (i, kk)), b_spec(tk, tn, lambda i, j, kk: (kk, j))]
    elif mode == "nt":
        in_specs = [a_spec(tm, tk, lambda i, j, kk: (i, kk)), b_spec(tn, tk, lambda i, j, kk: (j, kk))]
    else:
        in_specs = [a_spec(tk, tm, lambda i, j, kk: (kk, i)), b_spec(tk, tn, lambda i, j, kk: (kk, j))]
    args = [a_arr, b_arr]
    if add is not None:
        in_specs.append(add[3](tm, tn, lambda i, j, kk: (i, j)))
        args.append(add[0])
    return _mm_call(args, in_specs, o_spec(tm, tn, lambda i, j, kk: (i, j)), jax.ShapeDtypeStruct(o_shape, out_dtype),
                    mode, (m // tm, n // tn, nk), (tm, tn), add is not None, name)


def _mm_call(args, in_specs, out_spec, out_shape, mode, grid, tile, has_add, name):
    dims = _DIMS[mode]
    nk = grid[2]
    tm, tn = tile

    def body(*refs):
        a_ref, b_ref = refs[0], refs[1]
        add_ref = refs[2] if has_add else None
        prod = _dot(a_ref[...].astype(BF16), b_ref[...].astype(BF16), dims)
        if nk == 1:
            o_ref = refs[-1]
            if has_add:
                prod = prod + add_ref[...].astype(F32)
            o_ref[...] = prod.astype(o_ref.dtype)
            return
        o_ref, acc = refs[-2], refs[-1]
        kk = pl.program_id(2)

        @pl.when(kk == 0)
        def _():
            acc[...] = prod

        @pl.when(kk > 0)
        def _():
            acc[...] += prod

        @pl.when(kk == nk - 1)
        def _():
            r = acc[...]
            if has_add:
                r = r + add_ref[...].astype(F32)
            o_ref[...] = r.astype(o_ref.dtype)

    return pl.pallas_call(
        body, name=name, grid=grid, in_specs=in_specs, out_specs=out_spec, out_shape=out_shape,
        scratch_shapes=[pltpu.VMEM((tm, tn), F32)] if nk > 1 else [],
        compiler_params=_cparams(("parallel", "parallel", "arbitrary")),
    )(*args)


def mm_band(a, b, mode, name, grid, blocks, maps, out_shape, add=None, out_dtype=F32):
    in_specs = [pl.BlockSpec(blocks[0], maps[0]), pl.BlockSpec(blocks[1], maps[1])]
    args = [a, b]
    if add is not None:
        in_specs.append(pl.BlockSpec(blocks[2], maps[2]))
        args.append(add)
    return _mm_call(args, in_specs, pl.BlockSpec(blocks[2], maps[2]), jax.ShapeDtypeStruct(out_shape, out_dtype),
                    mode, grid, blocks[2], add is not None, name)


def rw(fn, ins, outs, name, rows, tr=None, consts=(), accs=()):
    tr = min(rows, 1024) if tr is None else tr
    n_in, n_c, n_o, n_a = len(ins), len(consts), len(outs), len(accs)
    in_specs = []
    for arr, off, width in ins:
        assert off % width == 0
        in_specs.append(pl.BlockSpec((tr, width), lambda i, o=off // width: (i, o)))
    for c in consts:
        in_specs.append(pl.BlockSpec(c.shape, lambda i: (0, 0)))
    out_specs = [pl.BlockSpec((tr, w), lambda i: (i, 0)) for w, _ in outs]
    out_specs += [pl.BlockSpec(s, lambda i: (0, 0)) for s in accs]
    out_shape = [jax.ShapeDtypeStruct((rows, w), dt) for w, dt in outs]
    out_shape += [jax.ShapeDtypeStruct(s, F32) for s in accs]

    def body(*refs):
        vals = [r[...] for r in refs[:n_in + n_c]]
        o_refs = refs[n_in + n_c:n_in + n_c + n_o]
        a_refs = refs[n_in + n_c + n_o:]
        res = fn(*vals)
        for r, v in zip(o_refs, res[:n_o]):
            r[...] = v.astype(r.dtype)
        if n_a:
            @pl.when(pl.program_id(0) == 0)
            def _():
                for r in a_refs:
                    r[...] = jnp.zeros_like(r)
            for r, v in zip(a_refs, res[n_o:]):
                r[...] += v

    res = pl.pallas_call(
        body, name=name, grid=(rows // tr,), in_specs=in_specs, out_specs=out_specs,
        out_shape=out_shape,
        compiler_params=_cparams(("arbitrary",) if n_a else ("parallel",)),
    )(*[a for a, _, _ in ins], *consts)
    return res


def _rstd(x):
    return lax.rsqrt(jnp.mean(x * x, axis=-1, keepdims=True) + EPS)


def rms_fwd(x, g, name):
    def fn(xv, gv):
        xv = xv.astype(F32)
        return (xv * _rstd(xv) * gv,)
    return rw(fn, [(x, 0, D)], [(D, BF16)], name, x.shape[0], consts=[g])[0]


def _rms_bwd_math(xv, dy, gv):
    r = _rstd(xv)
    dyg = dy * gv
    dx = r * dyg - xv * (r * r * r / D) * jnp.sum(dyg * xv, axis=-1, keepdims=True)
    dg = jnp.sum(dy * xv * r, axis=0, keepdims=True)
    return dx, dg


def rms_bwd(x, dy, dres, g, name):
    def fn(xv, dyv, drv, gv):
        dx, dg = _rms_bwd_math(xv, dyv, gv)
        return dx + drv, dg
    return rw(fn, [(x, 0, D), (dy, 0, D), (dres, 0, D)], [(D, F32)], name, x.shape[0],
              consts=[g], accs=[(1, D)])


def final_loss(x, tgt, g):
    def fn(xv, tv, gv):
        e = xv * _rstd(xv) * gv - tv
        loss = 0.5 * jnp.sum(jnp.sum(e * e, axis=-1, keepdims=True), axis=0, keepdims=True) / D
        dx, dg = _rms_bwd_math(xv, e / D, gv)
        return dx, loss, dg
    return rw(fn, [(x, 0, D), (tgt, 0, D)], [(D, F32)], "final_loss", S, consts=[g],
              accs=[(1, 1), (1, D)])


def _attn_bias(bias_ref):
    ii = lax.broadcasted_iota(jnp.int32, (2 * QB, 2 * QB), 0) % QB
    jj = lax.broadcasted_iota(jnp.int32, (2 * QB, 2 * QB), 1)
    dist = ii + QB - jj
    band = (dist >= 0) & (dist <= QB)
    bias_ref[1] = jnp.where(band, 0.0, NEG)
    bias_ref[0] = jnp.where(band & (jj >= QB), 0.0, NEG)


def _two_heads(x, m0):
    return jnp.concatenate([jnp.where(m0, x, 0.0), jnp.where(m0, 0.0, x)], axis=0)


def _per_head(col, m0):
    return jnp.where(m0, col[:QB], col[QB:])


def _attn_rows(idx, d):
    if d == 1:
        b = idx
        cur = pl.ds(pl.multiple_of(b * QB, QB), QB)
        prev = pl.ds(pl.multiple_of(jnp.maximum(b - 1, 0) * QB, QB), QB)
    else:
        r, b = lax.rem(idx, d), lax.div(idx, d)
        cur = pl.ds(r + b * (QB * d), QB, stride=d)
        prev = pl.ds(r + jnp.maximum(b - 1, 0) * (QB * d), QB, stride=d)
    return cur, prev, b


NBLK = S // QB
GROUP = 16
GROUP_FWD = 16


def _colblk(off):
    return pl.BlockSpec((S, 128), lambda hp: (0, off * 8 + hp))


def attn_fwd(z):
    def body(q_ref, k_ref, v_ref, g_ref, o_ref, l_ref, a_ref, os, ls, bias):
        _attn_bias(bias)
        m0 = lax.broadcasted_iota(jnp.int32, (1, 128), 1) < 64
        for pi, d in enumerate(PATTERNS):
            lone = S // d == QB

            def load(idx, d=d, lone=lone):
                cur, prev, b = _attn_rows(idx, d)
                if lone:
                    return cur, (q_ref[cur, :], None, k_ref[cur, :], None, v_ref[cur, :], bias[1, :, QB:])
                return cur, (q_ref[cur, :], k_ref[prev, :], k_ref[cur, :], v_ref[prev, :], v_ref[cur, :],
                             bias[jnp.minimum(b, 1)])

            def block(q, kp, kc, vp, vc, bs):
                qq = _two_heads(q * 0.125, m0).astype(BF16)
                k = (kc if kp is None else jnp.concatenate([kp, kc], axis=0)).astype(BF16)
                s = _nt(qq, k) + bs
                mx = jnp.max(s, axis=-1, keepdims=True)
                p = jnp.exp(s - mx)
                den = jnp.sum(p, axis=-1, keepdims=True)
                pb = p.astype(BF16)
                vv = _two_heads(vc if vp is None else jnp.concatenate([vp, vc], axis=0), m0).astype(BF16)
                o = _nn(jnp.concatenate([pb[:QB], pb[QB:]], axis=1), vv)
                return o * _per_head(1.0 / den, m0), _per_head(mx + jnp.log(den), m0)

            def step(i, carry, pi=pi):
                loaded = [load(i * GROUP_FWD + u) for u in range(GROUP_FWD)]
                done = [block(*vals) for _, vals in loaded]
                for (cur, _), (o, l) in zip(loaded, done):
                    os[pi, cur, :] = o
                    ls[pi, cur, :] = l
                return carry
            lax.fori_loop(0, NBLK // GROUP_FWD, step, 0)
        l1, l2, l3 = ls[0], ls[1], ls[2]
        mx = jnp.maximum(jnp.maximum(l1, l2), l3)
        e1, e2, e3 = jnp.exp(l1 - mx), jnp.exp(l2 - mx), jnp.exp(l3 - mx)
        tot = e1 + e2 + e3
        o = (os[0] * e1 + os[1] * e2 + os[2] * e3) / tot
        ga = g_ref[...]
        o_ref[...] = o
        l_ref[...] = mx + jnp.log(tot)
        a_ref[...] = (o * (ga * _sig(ga))).astype(a_ref.dtype)

    out = pl.BlockSpec((S, 128), lambda hp: (0, hp))
    return pl.pallas_call(
        body, name="attn_fwd", grid=(8,),
        in_specs=[_colblk(0), _colblk(1), _colblk(2), _colblk(3)], out_specs=[out] * 3,
        out_shape=[jax.ShapeDtypeStruct((S, D), F32), jax.ShapeDtypeStruct((S, D), F32),
                   jax.ShapeDtypeStruct((S, 2 * D), BF16)],
        scratch_shapes=[pltpu.VMEM((3, S, 128), F32), pltpu.VMEM((3, S, 128), F32),
                        pltpu.VMEM((2, 2 * QB, 2 * QB), F32)],
        compiler_params=_cparams(("parallel",)),
    )(z, z, z, z)


def attn_bwd(z, d_cat, o, lse):
    def body(q_ref, k_ref, v_ref, g_ref, da_ref, o_ref, l_ref, dq_ref, dk_ref, dv_ref, dg_ref, do_s, pr_s, bias):
        _attn_bias(bias)
        m0 = lax.broadcasted_iota(jnp.int32, (1, 128), 1) < 64
        ga = g_ref[...]
        sg = _sig(ga)
        da = da_ref[...]
        ov = o_ref[...]
        do = da * (ga * sg)
        dg_ref[...] = da * ov * (sg * (1.0 + ga * (1.0 - sg)))
        do_s[...] = do
        pr_s[...] = do * ov
        dq_ref[...] = jnp.zeros_like(dq_ref)
        dk_ref[...] = jnp.zeros_like(dk_ref)
        dv_ref[...] = jnp.zeros_like(dv_ref)
        for d in PATTERNS:
            lone = S // d == QB

            def load(idx, d=d, lone=lone):
                cur, prev, b = _attn_rows(idx, d)
                if lone:
                    return (cur, None), (q_ref[cur, :], None, k_ref[cur, :], None, v_ref[cur, :],
                                         do_s[cur, :], pr_s[cur, :], l_ref[cur, :], bias[1, :, QB:])
                return (cur, prev), (q_ref[cur, :], k_ref[prev, :], k_ref[cur, :], v_ref[prev, :], v_ref[cur, :],
                                     do_s[cur, :], pr_s[cur, :], l_ref[cur, :], bias[jnp.minimum(b, 1)])

            def block(q, kp, kc, vp, vc, dof, prod, lp, bs):
                qq = _two_heads(q * 0.125, m0).astype(BF16)
                kf = kc if kp is None else jnp.concatenate([kp, kc], axis=0)
                k = kf.astype(BF16)
                v = (vc if vp is None else jnp.concatenate([vp, vc], axis=0)).astype(BF16)
                dd = _two_heads(dof, m0).astype(BF16)
                lh = jnp.max(jnp.concatenate([jnp.where(m0, lp, -jnp.inf), jnp.where(m0, -jnp.inf, lp)], axis=0),
                             axis=-1, keepdims=True)
                delta = jnp.sum(_two_heads(prod, m0), axis=-1, keepdims=True)
                p = jnp.exp(_nt(qq, k) + bs - lh)
                ds = (p * (_nt(dd, v) - delta)).astype(BF16)
                dq = _nn(jnp.concatenate([ds[:QB], ds[QB:]], axis=1), _two_heads(kf, m0).astype(BF16))
                return dq * 0.125, _tn(ds, qq), _tn(p.astype(BF16), dd)

            def step(i, carry):
                loaded = [load(i * GROUP + u) for u in range(GROUP)]
                done = [block(*vals) for _, vals in loaded]
                for ((cur, prev), _), (dq, dk, dv) in zip(loaded, done):
                    dq_ref[cur, :] = dq_ref[cur, :] + dq
                    if prev is not None:
                        dk_ref[prev, :] = dk_ref[prev, :] + dk[:QB]
                        dv_ref[prev, :] = dv_ref[prev, :] + dv[:QB]
                    dk_ref[cur, :] = dk_ref[cur, :] + dk[-QB:]
                    dv_ref[cur, :] = dv_ref[cur, :] + dv[-QB:]
                return carry
            lax.fori_loop(0, NBLK // GROUP, step, 0)

    blk = pl.BlockSpec((S, 128), lambda hp: (0, hp))
    return pl.pallas_call(
        body, name="attn_bwd", grid=(8,),
        in_specs=[_colblk(0), _colblk(1), _colblk(2), _colblk(3), blk, blk, blk], out_specs=[blk] * 4,
        out_shape=[jax.ShapeDtypeStruct((S, D), F32)] * 4,
        scratch_shapes=[pltpu.VMEM((S, 128), F32), pltpu.VMEM((S, 128), F32), pltpu.VMEM((2, 2 * QB, 2 * QB), F32)],
        compiler_params=_cparams(("parallel",)),
    )(z, z, z, z, d_cat, o, lse)


def assemble_dz_even(parts):
    def body(*refs):
        o_ref = refs[-1]
        for j in range(6):
            o_ref[:, j * D:(j + 1) * D] = refs[j][...].astype(o_ref.dtype)
    tr = 512
    blk = pl.BlockSpec((tr, D), lambda i: (i, 0))
    return pl.pallas_call(
        body, name="assemble_dz_even", grid=(S // tr,), in_specs=[blk] * 6,
        out_specs=pl.BlockSpec((tr, 6 * D), lambda i: (i, 0)),
        out_shape=jax.ShapeDtypeStruct((S, 6 * D), BF16),
        compiler_params=_cparams(("parallel",)),
    )(*parts)


def _pool_window(g):
    return jnp.where(g == 0, 2.0, jnp.where(g == 1, 4.0, jnp.where(g == 2, 8.0, 16.0)))


def _pool_sel(g, levels):
    return jnp.where(g == 0, levels[0], jnp.where(g == 1, levels[1], jnp.where(g == 2, levels[2], levels[3])))


def _pool_fwd_math(v, g):
    t = lax.broadcasted_iota(jnp.int32, (S, 1), 0)
    s = v
    levels = []
    for k in (1, 2, 4, 8):
        s = s + jnp.where(t >= k, pltpu.roll(s, k, 0), 0.0)
        levels.append(s)
    cnt = jnp.minimum((t + 1).astype(F32), _pool_window(g))
    return _pool_sel(g, levels) / cnt - v, cnt


def pool_fwd(z, pw, ps, cat):
    def body(v_ref, g_ref, pw_ref, ps_ref, cat_ref, o_ref):
        g = pl.program_id(0)
        pooled, _ = _pool_fwd_math(v_ref[...], g)
        mixed = _nn(pooled.astype(BF16), pw_ref[...].astype(BF16))
        gb = g_ref[...]
        o_ref[...] = (mixed * ps_ref[...] * (gb * _sig(gb))).astype(o_ref.dtype)

    return pl.pallas_call(
        body, name="pool_fwd", grid=(4,),
        in_specs=[pl.BlockSpec((S, 256), lambda g: (0, 16 + g)),
                  pl.BlockSpec((S, 256), lambda g: (0, 20 + g)),
                  pl.BlockSpec((None, 256, 256), lambda g: (g, 0, 0)),
                  pl.BlockSpec((1, 256), lambda g: (0, g)), pl.BlockSpec(memory_space=pl.ANY)],
        out_specs=pl.BlockSpec((S, 256), lambda g: (0, 4 + g)),
        out_shape=jax.ShapeDtypeStruct((S, 2 * D), BF16),
        input_output_aliases={4: 0},
        compiler_params=_cparams(("parallel",)),
    )(z, z, pw, ps, cat)


def pool_bwd(z, d_cat, pw, ps):
    def body(v_ref, g_ref, d_ref, pw_ref, ps_ref, dv_ref, dg_ref, dpw_ref, dps_ref):
        g = pl.program_id(0)
        v = v_ref[...]
        pooled, cnt = _pool_fwd_math(v, g)
        pwb = pw_ref[...].astype(BF16)
        pb = pooled.astype(BF16)
        mixed = _nn(pb, pwb)
        gb = g_ref[...]
        sg = _sig(gb)
        dout = d_ref[...]
        sc = ps_ref[...]
        dg_ref[...] = dout * mixed * sc * (sg * (1.0 + gb * (1.0 - sg)))
        dms = dout * (gb * sg)
        dps_ref[...] = jnp.sum(dms * mixed, axis=0, keepdims=True)
        dmx = (dms * sc).astype(BF16)
        dpw_ref[...] = _tn(pb, dmx)
        dpooled = _nt(dmx, pwb)
        t = lax.broadcasted_iota(jnp.int32, (S, 1), 0)
        s = dpooled / cnt
        levels = []
        for k in (1, 2, 4, 8):
            s = s + jnp.where(t < S - k, pltpu.roll(s, S - k, 0), 0.0)
            levels.append(s)
        dv_ref[...] = _pool_sel(g, levels) - dpooled

    return pl.pallas_call(
        body, name="pool_bwd", grid=(4,),
        in_specs=[pl.BlockSpec((S, 256), lambda g: (0, 16 + g)),
                  pl.BlockSpec((S, 256), lambda g: (0, 20 + g)),
                  pl.BlockSpec((S, 256), lambda g: (0, 4 + g)),
                  pl.BlockSpec((None, 256, 256), lambda g: (g, 0, 0)),
                  pl.BlockSpec((1, 256), lambda g: (0, g))],
        out_specs=[pl.BlockSpec((S, 256), lambda g: (0, g)),
                   pl.BlockSpec((S, 256), lambda g: (0, g)),
                   pl.BlockSpec((None, 256, 256), lambda g: (g, 0, 0)),
                   pl.BlockSpec((1, 256), lambda g: (0, g))],
        out_shape=[jax.ShapeDtypeStruct((S, D), F32), jax.ShapeDtypeStruct((S, D), F32),
                   jax.ShapeDtypeStruct((4, 256, 256), F32), jax.ShapeDtypeStruct((1, D), F32)],
        compiler_params=_cparams(("parallel",)),
    )(z, z, d_cat, pw, ps)


CH = 128


def _sgu_common(v, lng, lnb, w_ref):
    mu = jnp.mean(v, axis=-1, keepdims=True)
    vc = v - mu
    rs = lax.rsqrt(jnp.mean(vc * vc, axis=-1, keepdims=True) + EPS)
    xhat = vc * rs
    vn = (xhat * lng + lnb).astype(BF16)
    ri = lax.broadcasted_iota(jnp.int32, (CH, CH), 0)
    ci = lax.broadcasted_iota(jnp.int32, (CH, CH), 1)
    tril = ri >= ci
    ws = [jnp.where(tril, w_ref[g], 0.0).astype(BF16) for g in range(4)]
    return xhat, rs, vn, tril, ws


def _zspec(off):
    return pl.BlockSpec((CH, D), lambda c: (c, off))


def _full(shape):
    return pl.BlockSpec(shape, lambda c: (0,) * len(shape))


def sgu_fwd(z, lng, lnb, w, bfull):
    def body(u_ref, v_ref, g_ref, lng_ref, lnb_ref, w_ref, b_ref, o_ref):
        _, _, vn, _, ws = _sgu_common(v_ref[...], lng_ref[...], lnb_ref[...], w_ref)
        for g in range(4):
            sl = slice(g * 256, (g + 1) * 256)
            mixed = _nn(ws[g], vn[:, sl]) + b_ref[:, sl]
            gc = g_ref[:, sl]
            o_ref[:, sl] = (u_ref[:, sl] * mixed * (gc * _sig(gc))).astype(o_ref.dtype)

    return pl.pallas_call(
        body, name="sgu_fwd", grid=(S // CH,),
        in_specs=[_zspec(0), _zspec(1), _zspec(2), _full((1, D)), _full((1, D)),
                  _full((4, CH, CH)), _full((CH, D))],
        out_specs=pl.BlockSpec((CH, D), lambda c: (c, 0)),
        out_shape=jax.ShapeDtypeStruct((S, D), BF16),
        compiler_params=_cparams(("parallel",)),
    )(z, z, z, lng, lnb, w, bfull)


def sgu_bwd(z, d_cat, lng, lnb, w, bfull):
    def body(u_ref, v_ref, g_ref, d_ref, lng_ref, lnb_ref, w_ref, b_ref,
             du_ref, dv_ref, dg_ref, dw_ref, db_ref, dlg_ref, dlb_ref):
        @pl.when(pl.program_id(0) == 0)
        def _():
            dw_ref[...] = jnp.zeros_like(dw_ref)
            db_ref[...] = jnp.zeros_like(db_ref)
            dlg_ref[...] = jnp.zeros_like(dlg_ref)
            dlb_ref[...] = jnp.zeros_like(dlb_ref)

        lng = lng_ref[...]
        xhat, rs, vn, tril, ws = _sgu_common(v_ref[...], lng, lnb_ref[...], w_ref)
        lane = lax.broadcasted_iota(jnp.int32, (1, 128), 1)
        db = jnp.zeros((CH, 128), F32)
        dvn_parts = []
        for g in range(4):
            sl = slice(g * 256, (g + 1) * 256)
            mixed = _nn(ws[g], vn[:, sl]) + b_ref[:, sl]
            gc = g_ref[:, sl]
            sg = _sig(gc)
            u = u_ref[:, sl]
            dc = d_ref[:, sl]
            du_ref[:, sl] = dc * mixed * (gc * sg)
            dg_ref[:, sl] = dc * u * mixed * (sg * (1.0 + gc * (1.0 - sg)))
            dmx = dc * u * (gc * sg)
            db = db + jnp.where(lane == g, jnp.sum(dmx, axis=-1, keepdims=True), 0.0)
            dmb = dmx.astype(BF16)
            dw_ref[g] += jnp.where(tril, _nt(dmb, vn[:, sl]), 0.0)
            dvn_parts.append(_tn(ws[g], dmb))
        db_ref[...] += db
        dvn = jnp.concatenate(dvn_parts, axis=1)
        dlb_ref[...] += jnp.sum(dvn, axis=0, keepdims=True)
        dlg_ref[...] += jnp.sum(dvn * xhat, axis=0, keepdims=True)
        dxh = dvn * lng
        dv_ref[...] = rs * (dxh - jnp.mean(dxh, axis=-1, keepdims=True)
                            - xhat * jnp.mean(dxh * xhat, axis=-1, keepdims=True))

    row = pl.BlockSpec((CH, D), lambda c: (c, 0))
    return pl.pallas_call(
        body, name="sgu_bwd", grid=(S // CH,),
        in_specs=[_zspec(0), _zspec(1), _zspec(2), row, _full((1, D)), _full((1, D)),
                  _full((4, CH, CH)), _full((CH, D))],
        out_specs=[row, row, row, _full((4, CH, CH)), _full((CH, 128)), _full((1, D)), _full((1, D))],
        out_shape=[jax.ShapeDtypeStruct((S, D), F32)] * 3
        + [jax.ShapeDtypeStruct((4, CH, CH), F32), jax.ShapeDtypeStruct((CH, 128), F32),
           jax.ShapeDtypeStruct((1, D), F32), jax.ShapeDtypeStruct((1, D), F32)],
        compiler_params=_cparams(("arbitrary",)),
    )(z, z, z, d_cat, lng, lnb, w, bfull)


TB = 256


def _cmul(ar, ai, br, bi):
    return ar * br - ai * bi, ar * bi + ai * br


def _scan_consts(ar, ai, reverse):
    a2 = _cmul(ar, ai, ar, ai)
    a4 = _cmul(*a2, *a2)
    row = lax.broadcasted_iota(jnp.int32, (8, NS), 0)

    def masked(k, p):
        keep = (row < 8 - k) if reverse else (row >= k)
        return jnp.where(keep, p[0], 0.0), jnp.where(keep, p[1], 0.0)
    pr = jnp.zeros((8, NS), F32)
    pi = jnp.zeros((8, NS), F32)
    cr, ci = ar, ai
    for r in range(8):
        sel = row == (7 - r if reverse else r)
        pr = jnp.where(sel, cr, pr)
        pi = jnp.where(sel, ci, pi)
        cr, ci = _cmul(cr, ci, ar, ai)
    return (masked(1, (ar, ai)), masked(2, a2), masked(4, a4)), (pr, pi), row


def scan_fwd(bu, abr, abi):
    def body(bu_ref, ar_ref, ai_ref, h_ref, car, cai):
        @pl.when(pl.program_id(0) == 0)
        def _():
            car[...] = jnp.zeros_like(car)
            cai[...] = jnp.zeros_like(cai)

        pows, (pr, pi), row = _scan_consts(ar_ref[...], ai_ref[...], False)

        def tile(t, carry):
            c_r, c_i = carry
            rows = pl.ds(pl.multiple_of(t * 8, 8), 8)
            xr = bu_ref[rows, 0:NS]
            xi = bu_ref[rows, NS:2 * NS]
            for k, (kr, ki) in zip((1, 2, 4), pows):
                sr = pltpu.roll(xr, k, 0)
                si = pltpu.roll(xi, k, 0)
                xr, xi = xr + kr * sr - ki * si, xi + kr * si + ki * sr
            xr, xi = xr + pr * c_r - pi * c_i, xi + pr * c_i + pi * c_r
            h_ref[rows, 0:NS] = xr
            h_ref[rows, NS:2 * NS] = xi
            return (jnp.broadcast_to(xr[7:8, :], (8, NS)), jnp.broadcast_to(xi[7:8, :], (8, NS)))

        c_r, c_i = lax.fori_loop(0, TB // 8, tile, (car[...], cai[...]))
        car[...] = c_r
        cai[...] = c_i

    return pl.pallas_call(
        body, name="s5_scan_fwd", grid=(S // TB,),
        in_specs=[pl.BlockSpec((TB, 2 * NS), lambda i: (i, 0)),
                  pl.BlockSpec((1, NS), lambda i: (0, 0)), pl.BlockSpec((1, NS), lambda i: (0, 0))],
        out_specs=pl.BlockSpec((TB, 2 * NS), lambda i: (i, 0)),
        out_shape=jax.ShapeDtypeStruct((S, 2 * NS), F32),
        scratch_shapes=[pltpu.VMEM((8, NS), F32), pltpu.VMEM((8, NS), F32)],
        compiler_params=_cparams(("arbitrary",)),
    )(bu, abr, abi)


def scan_bwd(eta, h, abr, abi):
    nt = S // TB

    def body(e_ref, h_ref, ar_ref, ai_ref, l_ref, da_ref, car, cai):
        @pl.when(pl.program_id(0) == 0)
        def _():
            car[...] = jnp.zeros_like(car)
            cai[...] = jnp.zeros_like(cai)
            da_ref[...] = jnp.zeros_like(da_ref)

        pows, (pr, pi), row = _scan_consts(ar_ref[...], -ai_ref[...], True)

        def tile(tt, carry):
            c_r, c_i, acr, aci = carry
            t = TB // 8 - 1 - tt
            rows = pl.ds(pl.multiple_of(t * 8, 8), 8)
            xr = e_ref[rows, 0:NS]
            xi = e_ref[rows, NS:2 * NS]
            for k, (kr, ki) in zip((1, 2, 4), pows):
                sr = pltpu.roll(xr, 8 - k, 0)
                si = pltpu.roll(xi, 8 - k, 0)
                xr, xi = xr + kr * sr - ki * si, xi + kr * si + ki * sr
            xr, xi = xr + pr * c_r - pi * c_i, xi + pr * c_i + pi * c_r
            l_ref[rows, 0:NS] = xr
            l_ref[rows, NS:2 * NS] = xi
            nr = jnp.where(row < 7, pltpu.roll(xr, 7, 0), c_r)
            ni = jnp.where(row < 7, pltpu.roll(xi, 7, 0), c_i)
            hr = h_ref[rows, 0:NS]
            hi = h_ref[rows, NS:2 * NS]
            acr = acr + hr * nr + hi * ni
            aci = aci + hr * ni - hi * nr
            return (jnp.broadcast_to(xr[0:1, :], (8, NS)), jnp.broadcast_to(xi[0:1, :], (8, NS)), acr, aci)

        zero = jnp.zeros((8, NS), F32)
        c_r, c_i, acr, aci = lax.fori_loop(0, TB // 8, tile, (car[...], cai[...], zero, zero))
        car[...] = c_r
        cai[...] = c_i
        da_ref[:, 0:NS] += acr
        da_ref[:, NS:2 * NS] += aci

    rev = pl.BlockSpec((TB, 2 * NS), lambda i: (nt - 1 - i, 0))
    return pl.pallas_call(
        body, name="s5_scan_bwd", grid=(nt,),
        in_specs=[rev, rev, pl.BlockSpec((1, NS), lambda i: (0, 0)), pl.BlockSpec((1, NS), lambda i: (0, 0))],
        out_specs=[rev, pl.BlockSpec((8, 2 * NS), lambda i: (0, 0))],
        out_shape=[jax.ShapeDtypeStruct((S, 2 * NS), F32), jax.ShapeDtypeStruct((8, 2 * NS), F32)],
        scratch_shapes=[pltpu.VMEM((8, NS), F32), pltpu.VMEM((8, NS), F32)],
        compiler_params=_cparams(("arbitrary",)),
    )(eta, h, abr, abi)


GC = 0.7978845608028654
GA = 0.044715


def s5_post(hc, z, dskip):
    def fn(hv, xd, dv):
        y = hv + dv * xd
        return y, 0.5 * y * (1.0 + jnp.tanh(GC * (y + GA * y * y * y)))
    return rw(fn, [(hc, 0, 512), (z, 3072, 512)], [(512, F32), (512, BF16)], "s5_post", S, consts=[dskip])


def s5_post_bwd(dyg, ypre, z, dskip):
    def fn(dy, y, xd, dv):
        th = jnp.tanh(GC * (y + GA * y * y * y))
        dg = 0.5 * (1.0 + th) + 0.5 * y * (1.0 - th * th) * GC * (1.0 + 3.0 * GA * y * y)
        dyp = dy * dg
        return dyp, dyp * dv, jnp.sum(dyp * xd, axis=0, keepdims=True)
    return rw(fn, [(dyg, 0, 512), (ypre, 0, 512), (z, 3072, 512)], [(512, BF16), (512, F32)],
              "s5_post_bwd", S, consts=[dskip], accs=[(1, 512)])


def glu_fwd(t, z, c_out):
    def fn(t1, t2, gd, co):
        return (jnp.concatenate([co, (t1 * _sig(t2) * (gd * _sig(gd))).astype(BF16)], axis=1),)
    return rw(fn, [(t, 0, 512), (t, 512, 512), (z, 3584, 512), (c_out, 0, D)], [(D + 512, BF16)], "glu_fwd", S)[0]


def glu_bwd(t, z, d_cat):
    def fn(t1, t2, gd, dd):
        s2, sg = _sig(t2), _sig(gd)
        sl = gd * sg
        return (jnp.concatenate([dd * s2 * sl, dd * t1 * s2 * (1.0 - s2) * sl], axis=1),
                dd * t1 * s2 * (sg * (1.0 + gd * (1.0 - sg))))
    return rw(fn, [(t, 0, 512), (t, 512, 512), (z, 3584, 512), (d_cat, 1024, 512)],
              [(D, BF16), (512, F32)], "glu_bwd", S)


def assemble_dz_odd(du, dv, dgc, dxd, dgd):
    def body(a, b, c, d, e, o_ref):
        o_ref[:, 0:D] = a[...].astype(BF16)
        o_ref[:, D:2 * D] = b[...].astype(BF16)
        o_ref[:, 2 * D:3 * D] = c[...].astype(BF16)
        o_ref[:, 3 * D:3 * D + 512] = d[...].astype(BF16)
        o_ref[:, 3 * D + 512:4 * D] = e[...].astype(BF16)
    tr = 512
    blk = pl.BlockSpec((tr, D), lambda i: (i, 0))
    half = pl.BlockSpec((tr, 512), lambda i: (i, 0))
    return pl.pallas_call(
        body, name="assemble_dz_odd", grid=(S // tr,), in_specs=[blk, blk, blk, half, half],
        out_specs=pl.BlockSpec((tr, 4 * D), lambda i: (i, 0)),
        out_shape=jax.ShapeDtypeStruct((S, 4 * D), BF16),
        compiler_params=_cparams(("parallel",)),
    )(du, dv, dgc, dxd, dgd)


TQ = 1024


def _xattn_probs(qh, kh):
    s = _nt(qh, kh) * 0.0625
    p = jnp.exp(s - jnp.max(s, axis=-1, keepdims=True))
    return p / jnp.sum(p, axis=-1, keepdims=True)


def xattn_fwd(q, kv):
    def body(q_ref, kv_ref, o_ref):
        outs = []
        for h in range(4):
            sl = slice(h * 256, (h + 1) * 256)
            p = _xattn_probs(q_ref[:, sl].astype(BF16), kv_ref[:, sl].astype(BF16))
            vh = kv_ref[:, D + h * 256:D + (h + 1) * 256].astype(BF16)
            outs.append((sl, _nn(p.astype(BF16), vh)))
        for sl, o in outs:
            o_ref[:, sl] = o.astype(o_ref.dtype)

    return pl.pallas_call(
        body, name="xattn_fwd", grid=(S // TQ,),
        in_specs=[pl.BlockSpec((TQ, D), lambda i: (i, 0)), pl.BlockSpec((MEM, 2 * D), lambda i: (0, 0))],
        out_specs=pl.BlockSpec((TQ, D), lambda i: (i, 0)),
        out_shape=jax.ShapeDtypeStruct((S, D), BF16),
        compiler_params=_cparams(("parallel",)),
    )(q, kv)


def xattn_bwd(q, kv, d_o):
    def body(q_ref, kv_ref, do_ref, dq_ref, dkv_ref):
        @pl.when(pl.program_id(0) == 0)
        def _():
            dkv_ref[...] = jnp.zeros_like(dkv_ref)

        done = []
        for h in range(4):
            sl = slice(h * 256, (h + 1) * 256)
            vs = slice(D + h * 256, D + (h + 1) * 256)
            qh = q_ref[:, sl].astype(BF16)
            kh = kv_ref[:, sl].astype(BF16)
            vh = kv_ref[:, vs].astype(BF16)
            doh = do_ref[:, sl].astype(BF16)
            p = _xattn_probs(qh, kh)
            dp = _nt(doh, vh)
            ds = (p * (dp - jnp.sum(p * dp, axis=-1, keepdims=True)) * 0.0625).astype(BF16)
            done.append((sl, vs, _nn(ds, kh), _tn(ds, qh), _tn(p.astype(BF16), doh)))
        for sl, vs, dq, dk, dv in done:
            dq_ref[:, sl] = dq.astype(dq_ref.dtype)
            dkv_ref[:, sl] += dk
            dkv_ref[:, vs] += dv

    return pl.pallas_call(
        body, name="xattn_bwd", grid=(S // TQ,),
        in_specs=[pl.BlockSpec((TQ, D), lambda i: (i, 0)), pl.BlockSpec((MEM, 2 * D), lambda i: (0, 0)),
                  pl.BlockSpec((TQ, D), lambda i: (i, 0))],
        out_specs=[pl.BlockSpec((TQ, D), lambda i: (i, 0)), pl.BlockSpec((MEM, 2 * D), lambda i: (0, 0))],
        out_shape=[jax.ShapeDtypeStruct((S, D), BF16), jax.ShapeDtypeStruct((MEM, 2 * D), F32)],
        compiler_params=_cparams(("arbitrary",)),
    )(q, kv, d_o)


def _s5_disc(a_re, a_im, log_dt, b_re, b_im):
    dt = jnp.exp(log_dt)[:, None]
    mag = jnp.exp(dt * a_re)
    abr = mag * jnp.cos(dt * a_im)
    abi = mag * jnp.sin(dt * a_im)
    nr, ni = abr - 1.0, abi
    inv = 1.0 / (a_re * a_re + a_im * a_im)
    cr = (nr * a_re + ni * a_im) * inv
    ci = (ni * a_re - nr * a_im) * inv
    bbr = cr[..., None] * b_re - ci[..., None] * b_im
    bbi = cr[..., None] * b_im + ci[..., None] * b_re
    return abr, abi, bbr, bbi


VM = pl.BlockSpec(memory_space=pltpu.VMEM)


def s5_embed(bt_re, bt_im, ct_re, ct_im):
    def body(br, bi, cr, ci, b_ref, c_ref):
        b_ref[...] = jnp.zeros_like(b_ref)
        c_ref[...] = jnp.zeros_like(c_ref)
        for g in range(NG):
            rows, cols = slice(g * NH, (g + 1) * NH), slice(g * NP, (g + 1) * NP)
            b_ref[rows, cols] = br[g]
            b_ref[rows, NS + g * NP:NS + (g + 1) * NP] = bi[g]
            c_ref[cols, rows] = cr[g]
            c_ref[NS + g * NP:NS + (g + 1) * NP, rows] = -ci[g]

    return pl.pallas_call(
        body, name="s5_embed", in_specs=[VM] * 4, out_specs=[VM] * 2,
        out_shape=[jax.ShapeDtypeStruct((NG * NH, 2 * NS), F32), jax.ShapeDtypeStruct((2 * NS, NG * NH), F32)],
        compiler_params=pltpu.CompilerParams(vmem_limit_bytes=VMEM_LIMIT),
    )(bt_re, bt_im, ct_re, ct_im)


def s5_extract(gb, gc):
    def body(gb_ref, gc_ref, br, bi, cr, ci):
        for g in range(NG):
            rows, cols = slice(g * NH, (g + 1) * NH), slice(g * NP, (g + 1) * NP)
            br[g] = gb_ref[rows, cols]
            bi[g] = gb_ref[rows, NS + g * NP:NS + (g + 1) * NP]
            cr[g] = gc_ref[cols, rows]
            ci[g] = -gc_ref[NS + g * NP:NS + (g + 1) * NP, rows]

    return pl.pallas_call(
        body, name="s5_extract", in_specs=[VM] * 2, out_specs=[VM] * 4,
        out_shape=[jax.ShapeDtypeStruct((NG, NH, NP), F32)] * 2 + [jax.ShapeDtypeStruct((NG, NP, NH), F32)] * 2,
        compiler_params=pltpu.CompilerParams(vmem_limit_bytes=VMEM_LIMIT),
    )(gb, gc)


HC, HS = NG * NH // 2, NS // 2
TS = 1024


def s5_to_states(x, w, mode, name, z_off=0):
    if mode == "nn":
        wb, wm = (HC, HS), lambda i, j, kk: (j % 2, j)
    else:
        wb, wm = (HS, HC), lambda i, j, kk: (j, j % 2)
    return mm_band(x, w, mode, name, (S // TS, 4, 1), ((TS, HC), wb, (TS, HS)),
                   (lambda i, j, kk: (i, z_off + j % 2), wm, lambda i, j, kk: (i, j)), (S, 2 * NS))


def s5_to_channels(x, w, mode, name, add=None):
    if mode == "nn":
        wb, wm = (HS, HC), lambda i, j, kk: (j + 2 * kk, j)
    else:
        wb, wm = (HC, HS), lambda i, j, kk: (j, j + 2 * kk)
    return mm_band(x, w, mode, name, (S // TS, 2, 2), ((TS, HS), wb, (TS, HC)),
                   (lambda i, j, kk: (i, j + 2 * kk), wm, lambda i, j, kk: (i, j)), (S, NG * NH), add=add)


def s5_outer(a, b, name, states_first, z_off=0):
    if states_first:
        return mm_band(a, b, "tn", name, (4, 1, 1), ((S, HS), (S, HC), (HS, HC)),
                       (lambda i, j, kk: (0, i), lambda i, j, kk: (0, i % 2), lambda i, j, kk: (i, i % 2)),
                       (2 * NS, NG * NH))
    return mm_band(a, b, "tn", name, (1, 4, 1), ((S, HC), (S, HS), (HC, HS)),
                   (lambda i, j, kk: (0, z_off + j % 2), lambda i, j, kk: (0, j), lambda i, j, kk: (j % 2, j)),
                   (NG * NH, 2 * NS))


def _fwd_even(i, x, P, W):
    hn = rms_fwd(x, P["norm_ab"][i:i + 1], "rms_ab_fwd")
    z = mm(m2(hn), W["w_in"], "nn", "in_ab")
    o, lse, cat = attn_fwd(z)
    if "more" in W:
        W.update(W.pop("more")(cat))
    cat = pool_fwd(z, W["pool_w"], P["pool_scale"][i:i + 1], cat)
    x_mid = mm(m2(cat), W["w_out"], "nn", "out_ab", add=m2(x))
    return x_mid, dict(x=x, hn=hn, z=z, o=o, lse=lse, cat=cat)


def _bwd_even(i, dx_mid, sv, P, W, G, GW):
    z = sv["z"]
    d_cat = mm(m2(dx_mid), W["w_out"], "nt", "out_ab_dx")
    GW["w_out"] = mm(m2(sv["cat"]), m2(dx_mid), "tn", "out_ab_dw").reshape(4, 512, D)
    dq, dk, dv, dga = attn_bwd(z, d_cat, sv["o"], sv["lse"])
    dvb, dgb, dpw, dps = pool_bwd(z, d_cat, W["pool_w"], P["pool_scale"][i:i + 1])
    GW["pool_w"] = dpw.reshape(4, 4, 64, 256).transpose(1, 0, 2, 3).reshape(4, 256, 256)
    G["pool_scale"][i] = dps[0]
    d_z = assemble_dz_even((dq, dk, dv, dga, dvb, dgb))
    d_hn = mm(m2(d_z), W["w_in"], "nt", "in_ab_dx")
    GW["w_in"] = mm(m2(sv["hn"]), m2(d_z), "tn", "in_ab_dw", out=outcs(D, 1536))
    return d_hn, P["norm_ab"][i:i + 1], "norm_ab", "rms_ab_bwd"


def _fwd_odd(i, x, P, W):
    hn = rms_fwd(x, P["norm_cd"][i:i + 1], "rms_cd_fwd")
    z = mm(m2(hn), W["w_in"], "nn", "in_cd")
    bfull = jnp.repeat(P["sgu_b"][i].T, 256, axis=1)
    c_out = sgu_fwd(z, P["sgu_ln_g"][i:i + 1], P["sgu_ln_b"][i:i + 1], P["sgu_w"][i], bfull)
    disc, disc_vjp = jax.vjp(_s5_disc, P["s5_a_re"][i], P["s5_a_im"][i], P["s5_log_dt"][i],
                             P["s5_b_re"][i], P["s5_b_im"][i])
    abr, abi, bbr, bbi = disc
    bbd, cbd = s5_embed(bbr.transpose(0, 2, 1), bbi.transpose(0, 2, 1),
                        P["s5_c_re"][i].transpose(0, 2, 1), P["s5_c_im"][i].transpose(0, 2, 1))
    abr, abi = abr.reshape(1, NS), abi.reshape(1, NS)
    bu = s5_to_states(z, bbd, "nn", "s5_bu", z_off=3072 // HC)
    h = scan_fwd(bu, abr, abi)
    hc = s5_to_channels(h, cbd, "nn", "s5_hc")
    dskip = P["s5_d"][i:i + 1]
    ypre, yg = s5_post(hc, z, dskip)
    if "more" in W:
        W.update(W.pop("more")(yg))
    w12 = W["w12"]
    t = mm(m2(yg), m2(w12), "nn", "glu_t")
    cat = glu_fwd(t, z, c_out)
    x_mid = mm(m2(cat), W["w_out"], "nn", "out_cd", add=m2(x))
    return x_mid, dict(x=x, hn=hn, z=z, bfull=bfull, disc_vjp=disc_vjp, bbd=bbd, cbd=cbd, abr=abr,
                       abi=abi, h=h, ypre=ypre, yg=yg, w12=w12, t=t, cat=cat, dskip=dskip)


def _bwd_odd(i, dx_mid, sv, P, W, G, GW):
    z = sv["z"]
    d_cat = mm(m2(dx_mid), W["w_out"], "nt", "out_cd_dx")
    GW["w_out"] = mm(m2(sv["cat"]), m2(dx_mid), "tn", "out_cd_dw").reshape(4, 384, D)
    du, dv, dgc, dws, dbs, dlg, dlb = sgu_bwd(z, d_cat, P["sgu_ln_g"][i:i + 1], P["sgu_ln_b"][i:i + 1],
                                               P["sgu_w"][i], sv["bfull"])
    G["sgu_w"][i], G["sgu_b"][i] = dws, dbs[:, :4].T
    G["sgu_ln_g"][i], G["sgu_ln_b"][i] = dlg[0], dlb[0]
    dt, dgd = glu_bwd(sv["t"], z, d_cat)
    gw12 = mm(m2(sv["yg"]), m2(dt), "tn", "glu_dw")
    GW["glu_w1"] = gw12[:, :512].reshape(4, 128, 512)
    GW["glu_w2"] = gw12[:, 512:].reshape(4, 128, 512)
    dyg = mm(m2(dt), m2(sv["w12"]), "nt", "glu_dx")
    dypre, dxd1, dd = s5_post_bwd(dyg, sv["ypre"], z, sv["dskip"])
    G["s5_d"][i] = dd[0]
    gcbd = s5_outer(sv["h"], dypre, "s5_dc", states_first=True)
    eta = s5_to_states(dypre, sv["cbd"], "nt", "s5_eta")
    lam, dacc = scan_bwd(eta, sv["h"], sv["abr"], sv["abi"])
    gbbd = s5_outer(z, lam, "s5_db", states_first=False, z_off=3072 // HC)
    dxd = s5_to_channels(lam, sv["bbd"], "nt", "s5_dx", add=dxd1)
    dacc = jnp.sum(dacc, axis=0)
    dbt_re, dbt_im, dct_re, dct_im = s5_extract(gbbd, gcbd)
    G["s5_c_re"][i], G["s5_c_im"][i] = dct_re.transpose(0, 2, 1), dct_im.transpose(0, 2, 1)
    d_bbr, d_bbi = dbt_re.transpose(0, 2, 1), dbt_im.transpose(0, 2, 1)
    (G["s5_a_re"][i], G["s5_a_im"][i], G["s5_log_dt"][i], G["s5_b_re"][i], G["s5_b_im"][i]) = sv["disc_vjp"](
        (dacc[:NS].reshape(NG, NP), dacc[NS:].reshape(NG, NP), d_bbr, d_bbi))
    d_z = assemble_dz_odd(du, dv, dgc, dxd, dgd)
    d_hn = mm(m2(d_z), W["w_in"], "nt", "in_cd_dx")
    GW["w_in"] = mm(m2(sv["hn"]), m2(d_z), "tn", "in_cd_dw", out=outcs(D, 1024))
    return d_hn, P["norm_cd"][i:i + 1], "norm_cd", "rms_cd_bwd"


def _fwd_x(l, x, mem_n, P, W):
    hx = rms_fwd(x, P["norm_x"][l:l + 1], "rms_x_fwd")
    q = mm(m2(hx), W["w_xq"], "nn", "xq", out_dtype=BF16)
    kv = mm(m2(mem_n), W["w_xkv"], "nn", "xkv", out_dtype=BF16)
    ox = xattn_fwd(q, kv)
    x_out = mm(m2(ox), W["w_xo"], "nn", "xo", add=m2(x))
    return x_out, dict(x=x, hx=hx, q=q, kv=kv, ox=ox)


def _bwd_x(l, dx_out, sv, mem_n, d_memn, P, W, G, GW):
    d_ox = mm(m2(dx_out), W["w_xo"], "nt", "xo_dx", out_dtype=BF16)
    GW["w_xo"] = mm(m2(sv["ox"]), m2(dx_out), "tn", "xo_dw").reshape(4, 256, D)
    dq, dkv = xattn_bwd(sv["q"], sv["kv"], d_ox)
    GW["w_xq"] = mm(m2(sv["hx"]), m2(dq), "tn", "xq_dw").reshape(4, 256, D)
    d_hx = mm(m2(dq), W["w_xq"], "nt", "xq_dx")
    GW["w_xkv"] = mm(m2(mem_n), m2(dkv), "tn", "xkv_dw", out=outcs(D, 512))
    d_memn = mm(m2(dkv), W["w_xkv"], "nt", "xkv_dx", add=None if d_memn is None else m2(d_memn))
    dx, dg = rms_bwd(sv["x"], d_hx, dx_out, P["norm_x"][l:l + 1], "rms_x_bwd")
    G["norm_x"][l] = dg[0]
    return dx, d_memn


SMALL_LAYERS = (("norm_ab", 2), ("pool_scale", 2), ("norm_cd", 2), ("sgu_ln_g", 2), ("sgu_ln_b", 2), ("sgu_w", 2),
                ("sgu_b", 2), ("s5_a_re", 2), ("s5_a_im", 2), ("s5_log_dt", 2), ("s5_b_re", 2), ("s5_b_im", 2),
                ("s5_c_re", 2), ("s5_c_im", 2), ("s5_d", 2), ("norm_x", 4))


def local_step(x, mem, tgt, P, weights_of, grads_done):
    G = {k: [None] * n for k, n in SMALL_LAYERS}
    mem_g = P["mem_norm"].reshape(1, D)
    mem_n = rms_fwd(mem, mem_g, "rms_mem_fwd")
    saved = []
    for layer in range(4):
        i = layer // 2
        W = weights_of(layer, x)
        x, sv_m = (_fwd_even if layer % 2 == 0 else _fwd_odd)(i, x, P, W)
        x, sv_x = _fwd_x(layer, x, mem_n, P, W)
        saved.append((sv_m, sv_x, W))
    dx, loss, dgf = final_loss(x, tgt, P["final_norm"].reshape(1, D))
    G["final_norm"] = dgf[0]
    d_memn = None
    for layer in reversed(range(4)):
        i = layer // 2
        sv_m, sv_x, W = saved[layer]
        GW = {}
        dx_mid, d_memn = _bwd_x(layer, dx, sv_x, mem_n, d_memn, P, W, G, GW)
        d_hn, g, key, name = (_bwd_even if layer % 2 == 0 else _bwd_odd)(i, dx_mid, sv_m, P, W, G, GW)
        token = grads_done(layer, GW)
        if token is not None:
            g = g + token
        dx, dg = rms_bwd(sv_m["x"], d_hn, dx_mid, g, name)
        G[key][i] = dg[0]
    _, dgm = rms_bwd(mem, d_memn, d_memn, mem_g, "rms_mem_bwd")
    G["mem_norm"] = dgm[0]
    return loss, dx, G


ANY = pl.BlockSpec(memory_space=pl.ANY)


def _place():
    x, y, c = lax.axis_index("x"), lax.axis_index("y"), lax.axis_index("c")
    chips = [(1 - x, y), (x, 1 - y), (1 - x, 1 - y)]
    return x, y, c, 2 * x + y, (x, y, 1 - c), chips


def _remote(src, dst, send, recv, k, dev):
    return pltpu.make_async_remote_copy(src_ref=src, dst_ref=dst, send_sem=send.at[k], recv_sem=recv.at[k],
                                        device_id=dev, device_id_type=MESHID)


HBM = pl.BlockSpec(memory_space=pltpu.HBM)
SEM = pl.BlockSpec(memory_space=pltpu.SEMAPHORE)
EFFECT = pltpu.SideEffectType.DATAFLOW_SIDE_EFFECTING


def _hbm(t):
    return pltpu.with_memory_space_constraint(t, pltpu.HBM)


def _first_copies(shard, land, send, recv, arriving):
    x, y, c, jme, sib, chips = _place()
    out = []
    for k, chip in enumerate(chips):
        slot = 2 * chip[0] + chip[1] if arriving else jme
        out.append(_remote(shard.at[c], land.at[slot, c], send, recv, k, (*chip, c)))
    out.append(_remote(shard, land.at[jme], send, recv, 3, sib))
    return out


def allgather_first_start(shard, after, name):
    def body(s_ref, l_ref, *rest):
        send, recv, token = rest[len(after)], rest[len(after) + 1], rest[-1]
        for cp in _first_copies(s_ref, l_ref, send, recv, False):
            cp.start()
        token[...] = jnp.zeros_like(token)

    land = (4,) + shard.shape
    return pl.pallas_call(
        body, name=name,
        out_shape=(pltpu.SemaphoreType.DMA((4,)), pltpu.SemaphoreType.DMA((4,)), pltpu.HBM(shard.shape, shard.dtype),
                   pltpu.HBM(land, shard.dtype), jax.ShapeDtypeStruct((8, 128), F32)),
        in_specs=[HBM, HBM] + [ANY] * len(after),
        out_specs=(SEM, SEM, HBM, HBM, pl.BlockSpec(memory_space=pltpu.VMEM)),
        input_output_aliases={0: 2, 1: 3},
        compiler_params=pltpu.CompilerParams(has_side_effects=EFFECT),
    )(_hbm(shard), _hbm(lax.empty(land, shard.dtype)), *after)


def allgather_first_wait(send, recv, shard, land, after, name):
    def body(s_ref, l_ref, send_r, recv_r, *rest):
        for cp in _first_copies(s_ref, l_ref, send_r, recv_r, True):
            cp.wait_send()
            cp.wait_recv()

    res = pl.pallas_call(
        body, name=name, out_shape=(pltpu.HBM(shard.shape, shard.dtype), pltpu.HBM(land.shape, land.dtype)),
        in_specs=[HBM, HBM, SEM, SEM] + [ANY] * len(after), out_specs=(HBM, HBM),
        input_output_aliases={0: 0, 1: 1},
        compiler_params=pltpu.CompilerParams(has_side_effects=EFFECT),
    )(shard, land, send, recv, *after)
    return res[1]


def allgather_forward(land):
    def body(l_in, l_ref, send, recv):
        x, y, c, jme, sib, chips = _place()
        cps = []
        for k, chip in enumerate(chips):
            piece = l_ref.at[2 * chip[0] + chip[1], c]
            cp = _remote(piece, piece, send, recv, k, sib)
            cp.start()
            cps.append(cp)
        for k, chip in enumerate(chips):
            piece = l_ref.at[2 * chip[0] + chip[1], 1 - c]
            _remote(piece, piece, send, recv, k, sib).wait_recv()
        for cp in cps:
            cp.wait_send()

    return pl.pallas_call(
        body, name="allgather_forward", in_specs=[ANY], out_specs=ANY,
        out_shape=jax.ShapeDtypeStruct(land.shape, land.dtype), input_output_aliases={0: 0},
        scratch_shapes=[pltpu.SemaphoreType.DMA((3,)), pltpu.SemaphoreType.DMA((3,))],
    )(land)


def _gather_copies(ins, lands, send, recv):
    x, y, c, jme, sib, chips = _place()
    devs = [(*chip, c) for chip in chips] + [sib]
    return [_remote(ins[a], lands[a].at[jme], send, recv, a * 4 + k, dev)
            for a in range(len(ins)) for k, dev in enumerate(devs)]


def allgather_start(shards, after, name):
    n, na = len(shards), len(after)

    def body(*refs):
        ins, lands = refs[:n], refs[n:2 * n]
        send, recv = refs[2 * n + na], refs[2 * n + na + 1]
        token = refs[-1]
        for cp in _gather_copies(ins, lands, send, recv):
            cp.start()
        token[...] = jnp.zeros_like(token)

    res = pl.pallas_call(
        body, name=name,
        out_shape=(pltpu.SemaphoreType.DMA((4 * n,)), pltpu.SemaphoreType.DMA((4 * n,)),
                   *[pltpu.HBM(s.shape, s.dtype) for s in shards],
                   *[pltpu.HBM((4,) + s.shape, s.dtype) for s in shards],
                   jax.ShapeDtypeStruct((8, 128), F32)),
        in_specs=[HBM] * (2 * n) + [ANY] * na,
        out_specs=(SEM, SEM, *[HBM] * (2 * n), pl.BlockSpec(memory_space=pltpu.VMEM)),
        input_output_aliases={a: 2 + a for a in range(2 * n)},
        compiler_params=pltpu.CompilerParams(has_side_effects=EFFECT),
    )(*[_hbm(s) for s in shards], *[_hbm(lax.empty((4,) + s.shape, s.dtype)) for s in shards], *after)
    return res[0], res[1], list(res[2:2 + n]), list(res[2 + n:2 + 2 * n]), res[-1]


def allgather_wait(send, recv, shards, lands, after, name):
    n = len(shards)

    def body(*refs):
        ins, zones = refs[:n], refs[n:2 * n]
        send_r, recv_r = refs[2 * n], refs[2 * n + 1]
        x, y, c, jme, sib, chips = _place()
        slots = [2 * chip[0] + chip[1] for chip in chips] + [jme]
        for a in range(n):
            for k, slot in enumerate(slots):
                cp = _remote(ins[a], zones[a].at[slot], send_r, recv_r, a * 4 + k, sib)
                cp.wait_send()
                cp.wait_recv()

    res = pl.pallas_call(
        body, name=name,
        out_shape=tuple(pltpu.HBM(t.shape, t.dtype) for t in list(shards) + list(lands)),
        in_specs=[HBM] * (2 * n) + [SEM, SEM, ANY], out_specs=tuple([HBM] * (2 * n)),
        input_output_aliases={a: a for a in range(2 * n)},
        compiler_params=pltpu.CompilerParams(has_side_effects=EFFECT),
    )(*shards, *lands, send, recv, after)
    return list(res[n:])


def allgather_small(slab):
    def body(in_ref, out_ref, send, recv, lsem):
        x, y, c, jme, sib, chips = _place()
        loc = pltpu.make_async_copy(in_ref, out_ref.at[jme], lsem.at[0])
        loc.start()
        cps = [_remote(in_ref, out_ref.at[jme], send, recv, k, (*chip, c)) for k, chip in enumerate(chips)]
        for cp in cps:
            cp.start()
        for k, chip in enumerate(chips):
            piece = out_ref.at[2 * chip[0] + chip[1]]
            _remote(piece, piece, send, recv, k, (*chip, c)).wait_recv()
        for cp in cps:
            cp.wait_send()
        loc.wait()

    return pl.pallas_call(
        body, name="allgather_small", in_specs=[ANY], out_specs=ANY,
        out_shape=jax.ShapeDtypeStruct((4,) + slab.shape, slab.dtype),
        scratch_shapes=[pltpu.SemaphoreType.DMA((3,)), pltpu.SemaphoreType.DMA((3,)), pltpu.SemaphoreType.DMA((1,))],
    )(slab)


def allreduce_small(v):
    hr = v.shape[0] // 2

    def body(v_ref, o_ref, r0, r1, r2, send, recv):
        x, y, c, jme, sib, chips = _place()
        mine = pl.ds(pl.multiple_of(c * hr, 8), hr)
        other = pl.ds(pl.multiple_of((1 - c) * hr, 8), hr)
        cp = _remote(v_ref.at[other], r0, send, recv, 0, sib)
        cp.start()
        cp.wait()
        o_ref[mine, :] = v_ref[mine, :] + r0[...]
        for k, (buf, peer) in enumerate(((r1, (1 - x, y, c)), (r2, (x, 1 - y, c))), start=1):
            cp = _remote(o_ref.at[mine], buf, send, recv, k, peer)
            cp.start()
            cp.wait()
            o_ref[mine, :] = o_ref[mine, :] + buf[...]
        cp = _remote(o_ref.at[mine], o_ref.at[mine], send, recv, 3, sib)
        cp.start()
        cp.wait_send()
        _remote(o_ref.at[other], o_ref.at[other], send, recv, 3, sib).wait_recv()

    vm = pl.BlockSpec(memory_space=pltpu.VMEM)
    half = pltpu.VMEM((hr, v.shape[1]), v.dtype)
    return pl.pallas_call(
        body, name="allreduce_small", in_specs=[vm], out_specs=vm,
        out_shape=jax.ShapeDtypeStruct(v.shape, v.dtype),
        scratch_shapes=[half] * 3 + [pltpu.SemaphoreType.DMA((4,)), pltpu.SemaphoreType.DMA((4,))],
        compiler_params=pltpu.CompilerParams(vmem_limit_bytes=VMEM_LIMIT),
    )(v)


def _pair_copies(gs, lands, send, recv):
    x, y, c, jme, sib, chips = _place()
    return [_remote(gs[a].at[:, 1 - c], lands[a], send, recv, a, sib) for a in range(len(gs))]


def rs_pair_start(gs, name):
    n = len(gs)

    def body(*refs):
        ins, lands = refs[:n], refs[n:2 * n]
        send, recv = refs[2 * n], refs[2 * n + 1]
        token = refs[-1]
        for cp in _pair_copies(ins, lands, send, recv):
            cp.start()
        token[...] = jnp.zeros_like(token)

    shapes = [(4,) + g.shape[2:] for g in gs]
    res = pl.pallas_call(
        body, name=name,
        out_shape=(pltpu.SemaphoreType.DMA((n,)), pltpu.SemaphoreType.DMA((n,)),
                   *[pltpu.HBM(g.shape, g.dtype) for g in gs], *[pltpu.HBM(s, F32) for s in shapes],
                   jax.ShapeDtypeStruct((8, 128), F32)),
        in_specs=[HBM] * (2 * n), out_specs=(SEM, SEM, *[HBM] * (2 * n), pl.BlockSpec(memory_space=pltpu.VMEM)),
        input_output_aliases={a: 2 + a for a in range(2 * n)},
        compiler_params=pltpu.CompilerParams(has_side_effects=EFFECT),
    )(*[_hbm(g) for g in gs], *[_hbm(lax.empty(s, F32)) for s in shapes])
    return res[0], res[1], list(res[2:2 + n]), list(res[2 + n:2 + 2 * n]), res[-1]


def rs_pair_wait(send, recv, gs, lands, after, name):
    n = len(gs)

    def body(*refs):
        ins, zones = refs[:n], refs[n:2 * n]
        for cp in _pair_copies(ins, zones, refs[2 * n], refs[2 * n + 1]):
            cp.wait_send()
            cp.wait_recv()

    res = pl.pallas_call(
        body, name=name,
        out_shape=tuple(pltpu.HBM(t.shape, t.dtype) for t in list(gs) + list(lands)),
        in_specs=[HBM] * (2 * n) + [SEM, SEM, ANY], out_specs=tuple([HBM] * (2 * n)),
        input_output_aliases={a: a for a in range(2 * n)},
        compiler_params=pltpu.CompilerParams(has_side_effects=EFFECT),
    )(*gs, *lands, send, recv, after)
    return list(res[:n]), list(res[n:])


SUM_ROWS = 256


def rs_pair_sum(g4s, gots, cidx):
    n = len(g4s)
    tiles = [(min(g.shape[2], SUM_ROWS), g.shape[3]) for g in g4s]
    nts = [g.shape[2] // tr for g, (tr, _) in zip(g4s, tiles)]

    def at(a, s):
        s = jnp.minimum(s, 4 * nts[a] - 1)
        return s // nts[a], s % nts[a]

    def body(c_ref, *refs):
        for a in range(n):
            refs[2 * n + a][...] = (refs[a][...] + refs[n + a][...]).astype(BF16)

    in_specs = [pl.BlockSpec((None, None) + tiles[a], lambda s, cr, a=a: (at(a, s)[0], cr[0], at(a, s)[1], 0))
                for a in range(n)]
    in_specs += [pl.BlockSpec((None,) + tiles[a], lambda s, cr, a=a: (*at(a, s), 0)) for a in range(n)]
    return pl.pallas_call(
        body, name="rs_pair_sum",
        grid_spec=pltpu.PrefetchScalarGridSpec(
            num_scalar_prefetch=1, grid=(4 * max(nts),), in_specs=in_specs,
            out_specs=[pl.BlockSpec((None,) + tiles[a], lambda s, cr, a=a: (*at(a, s), 0)) for a in range(n)]),
        out_shape=[jax.ShapeDtypeStruct((4,) + g.shape[2:], BF16) for g in g4s],
        compiler_params=_cparams(("arbitrary",)),
    )(cidx, *g4s, *gots)


def _chip_copies(ps, lands, send, recv):
    x, y, c, jme, sib, chips = _place()
    return [_remote(ps[a].at[2 * chip[0] + chip[1]], lands[a].at[jme], send, recv, a * 3 + k, (*chip, c))
            for a in range(len(ps)) for k, chip in enumerate(chips)]


def rs_chip_start(ps, name):
    n = len(ps)

    def body(*refs):
        ins, lands = refs[:n], refs[n:2 * n]
        send, recv = refs[2 * n], refs[2 * n + 1]
        token = refs[-1]
        for cp in _chip_copies(ins, lands, send, recv):
            cp.start()
        token[...] = jnp.zeros_like(token)

    res = pl.pallas_call(
        body, name=name,
        out_shape=(pltpu.SemaphoreType.DMA((3 * n,)), pltpu.SemaphoreType.DMA((3 * n,)),
                   *[pltpu.HBM(p.shape, p.dtype) for p in ps], *[pltpu.HBM(p.shape, p.dtype) for p in ps],
                   jax.ShapeDtypeStruct((8, 128), F32)),
        in_specs=[HBM] * (2 * n), out_specs=(SEM, SEM, *[HBM] * (2 * n), pl.BlockSpec(memory_space=pltpu.VMEM)),
        input_output_aliases={a: 2 + a for a in range(2 * n)},
        compiler_params=pltpu.CompilerParams(has_side_effects=EFFECT),
    )(*[_hbm(p) for p in ps], *[_hbm(lax.empty(p.shape, p.dtype)) for p in ps])
    return res[0], res[1], list(res[2:2 + n]), list(res[2 + n:2 + 2 * n]), res[-1]


def rs_chip_wait(send, recv, ps, lands, after, name):
    n = len(ps)

    def body(*refs):
        ins, zones = refs[:n], refs[n:2 * n]
        send_r, recv_r = refs[2 * n], refs[2 * n + 1]
        x, y, c, jme, sib, chips = _place()
        for a in range(n):
            for k, chip in enumerate(chips):
                jt = 2 * chip[0] + chip[1]
                cp = _remote(ins[a].at[jt], zones[a].at[jt], send_r, recv_r, a * 3 + k, (*chip, c))
                cp.wait_send()
                cp.wait_recv()

    res = pl.pallas_call(
        body, name=name,
        out_shape=tuple(pltpu.HBM(p.shape, p.dtype) for p in list(ps) + list(lands)),
        in_specs=[HBM] * (2 * n) + [SEM, SEM] + [ANY] * len(after), out_specs=tuple([HBM] * (2 * n)),
        input_output_aliases={a: a for a in range(2 * n)},
        compiler_params=pltpu.CompilerParams(has_side_effects=EFFECT),
    )(*ps, *lands, send, recv, *after)
    return list(res[:n]), list(res[n:])


def rs_chip_sum(qs, ps, ls, accs, layers, jc):
    n = len(qs)
    tiles = [(min(q.shape[1], SUM_ROWS), q.shape[2]) for q in qs]
    nts = [q.shape[1] // tr for q, (tr, _) in zip(qs, tiles)]

    def at(a, s):
        return jnp.minimum(s, nts[a] - 1)

    def body(jc_ref, *refs):
        jme = jc_ref[0]
        for a in range(n):
            q_ref, p_ref, o_ref = refs[a], refs[n + a], refs[len(refs) - n + a]
            own = p_ref[...].astype(F32)
            v = [jnp.where(jme == j, own, q_ref[j].astype(F32)) for j in range(4)]
            o_ref[...] = ((v[0] + v[1]) + v[2]) + v[3]

    in_specs = [pl.BlockSpec((4,) + tiles[a], lambda s, jr, a=a: (0, at(a, s), 0)) for a in range(n)]
    in_specs += [pl.BlockSpec((None,) + tiles[a], lambda s, jr, a=a: (jr[0], at(a, s), 0)) for a in range(n)]
    args, aliases = [jc, *qs, *ps], {}
    for a in range(n):
        if accs[a] is not None:
            aliases[len(args)] = a
            in_specs.append(ANY)
            args.append(accs[a])
    return pl.pallas_call(
        body, name="rs_chip_sum",
        grid_spec=pltpu.PrefetchScalarGridSpec(
            num_scalar_prefetch=1, grid=(max(nts),), in_specs=in_specs,
            out_specs=[pl.BlockSpec((None, None) + tiles[a], lambda s, jr, a=a: (ls[a], jr[1], at(a, s), 0))
                       for a in range(n)]),
        out_shape=[jax.ShapeDtypeStruct((layers[a], 2) + qs[a].shape[1:], F32) for a in range(n)],
        input_output_aliases=aliases,
        compiler_params=_cparams(("arbitrary",)),
    )(*args)


def rs_pair_gather(rs):
    n = len(rs)

    def body(*refs):
        outs = refs[n:2 * n]
        send, recv = refs[2 * n:]
        x, y, c, jme, sib, chips = _place()
        cps = [_remote(outs[a].at[:, c], outs[a].at[:, c], send, recv, a, sib) for a in range(n)]
        for cp in cps:
            cp.start()
        for a in range(n):
            slot = outs[a].at[:, 1 - c]
            _remote(slot, slot, send, recv, a, sib).wait_recv()
        for cp in cps:
            cp.wait_send()

    return pl.pallas_call(
        body, name="rs_pair_gather", in_specs=[ANY] * n, out_specs=[ANY] * n,
        out_shape=[jax.ShapeDtypeStruct(r.shape, r.dtype) for r in rs],
        input_output_aliases={a: a for a in range(n)},
        scratch_shapes=[pltpu.SemaphoreType.DMA((n,)), pltpu.SemaphoreType.DMA((n,))],
    )(*rs)


def _adamw_math(w, g, m, v):
    m = B1 * m + (1.0 - B1) * g
    v = B2 * v + (1.0 - B2) * (g * g)
    m_hat = m / (1.0 - B1 ** STEP)
    v_hat = v / (1.0 - B2 ** STEP)
    return -LR * (m_hat / (jnp.sqrt(v_hat) + AEPS) + WD * w), m, v


ADAMW_TILE = 512 * 1024


def adamw(w, g, m, v, name, with_grad=False):
    rows, cols = w.shape
    tr = next((t for t in (1024, 512, 256) if rows % t == 0 and t * cols <= ADAMW_TILE), rows)
    fn =(lambda wv, gv, mv, vv: (gv,) + _adamw_math(wv, gv, mv, vv)) if with_grad else _adamw_math
    return rw(fn, [(a, 0, cols) for a in (w, g, m, v)], [(cols, F32)] * (4 if with_grad else 3), name, rows, tr=tr)


def adamw_small(ws, gs, ms, vs):
    n = len(ws)

    def body(*refs):
        for a in range(n):
            res = _adamw_math(*[refs[k * n + a][...] for k in range(4)])
            for k in range(3):
                refs[(4 + k) * n + a][...] = res[k]

    res = pl.pallas_call(
        body, name="adamw_small", in_specs=[VM] * (4 * n), out_specs=[VM] * (3 * n),
        out_shape=[jax.ShapeDtypeStruct(w.shape, F32) for _ in range(3) for w in ws],
        compiler_params=pltpu.CompilerParams(vmem_limit_bytes=VMEM_LIMIT),
    )(*ws, *gs, *ms, *vs)
    return [(res[a], res[n + a], res[2 * n + a]) for a in range(n)]


WEIGHTS = ["norm_ab", "w_in_ab", "pool_w", "pool_scale", "w_out_ab", "norm_cd", "w_in_cd", "sgu_ln_g", "sgu_ln_b",
           "sgu_w", "sgu_b", "s5_a_re", "s5_a_im", "s5_log_dt", "s5_b_re", "s5_b_im", "s5_c_re", "s5_c_im", "s5_d",
           "glu_w1", "glu_w2", "w_out_cd", "norm_x", "w_xq", "w_xkv", "w_xo", "mem_norm", "final_norm"]
INPUTS = ["x", "mem"] + WEIGHTS + ["loss_target"] + ["m_" + n for n in WEIGHTS] + ["v_" + n for n in WEIGHTS]
BIG = ["w_in_ab", "w_out_ab", "w_in_cd", "w_out_cd", "w_xq", "w_xkv", "w_xo", "glu_w1", "glu_w2", "pool_w"]
COL_SHARDED = ("w_in_ab", "w_in_cd", "w_xkv")
SMALL = [n for n in WEIGHTS if n not in BIG]
SMALL_SHARDED = {"norm_cd": 256, "sgu_ln_g": 256, "sgu_ln_b": 256, "s5_d": 128}
PACK = 256 * 128


def _pack(arrs):
    flat = jnp.concatenate([a.reshape(-1) for a in arrs])
    pad = (-flat.shape[0]) % PACK
    return jnp.concatenate([flat, jnp.zeros((pad,), flat.dtype)]).reshape(-1, 128)


def _unpack(packed, shapes):
    flat, out, off = packed.reshape(-1), [], 0
    for s in shapes:
        n = 1
        for d in s:
            n *= d
        out.append(flat[off:off + n].reshape(s))
        off += n
    return out


LAYER_KEYS = (("w_in", "w_out", "pool_w", "w_xq", "w_xkv", "w_xo"),
              ("w_in", "w_out", "glu_w1", "glu_w2", "w_xq", "w_xkv", "w_xo"))


def _weight_of(key, layer):
    if key in ("w_xq", "w_xkv", "w_xo"):
        return key, layer, 4
    kind = "ab" if layer % 2 == 0 else "cd"
    return {"w_in": "w_in_" + kind, "w_out": "w_out_" + kind}.get(key, key), layer // 2, 2


def kernel(*args):
    a = dict(zip(INPUTS, args))
    x_i, y_i, c_i = lax.axis_index("x"), lax.axis_index("y"), lax.axis_index("c")
    j = 2 * x_i + y_i

    slab = jnp.concatenate([a["norm_cd"], a["sgu_ln_g"], a["sgu_ln_b"],
                            jnp.pad(a["s5_d"], ((0, 0), (0, 128)))], axis=0)
    gslab = allgather_small(slab)
    P = {n: a[n] for n in SMALL}
    for k, n in enumerate(("norm_cd", "sgu_ln_g", "sgu_ln_b", "s5_d")):
        wd = SMALL_SHARDED[n]
        P[n] = gslab[:, 2 * k:2 * k + 2, :wd].transpose(1, 0, 2).reshape(2, 4 * wd)

    def shards_of(layer):
        keys = sorted(k for k in LAYER_KEYS[layer % 2])
        out = []
        for k in keys:
            n, l, _ = _weight_of(k, layer)
            out.append(a[n][l].reshape(-1, a[n].shape[-1]).astype(BF16))
        return keys, out

    keys0, sh0 = shards_of(0)
    first = keys0.index("w_in")
    f_send, f_recv, f_shard, f_land, token = allgather_first_start(
        sh0[first].reshape(2, sh0[first].shape[0] // 2, sh0[first].shape[1]), [gslab], "allgather_start_0in")
    started = {}
    for layer in (0, 1, 2, 3):
        keys, sh = (keys0, sh0) if layer == 0 else shards_of(layer)
        rest = [(k, s) for k, s in zip(keys, sh) if k != "w_in"]
        parts = [("in", ["w_in"], [sh[keys.index("w_in")]])] * (layer > 0) + [("", *map(list, zip(*rest)))]
        for tag, pk, ps in parts:
            send, recv, ps, lands, token = allgather_start(ps, [token, gslab], "allgather_start_%d%s" % (layer, tag))
            started[(layer, tag)] = (pk, send, recv, ps, lands)
    g_in = allgather_forward(allgather_first_wait(f_send, f_recv, f_shard, f_land, [token], "allgather_wait_0in"))
    w_in0 = g_in.reshape(4, -1, g_in.shape[-1])

    cidx = jnp.reshape(c_i, (1,)).astype(jnp.int32)
    jc = jnp.stack([j, c_i]).astype(jnp.int32)

    def views(g):
        W = {}
        for k, v in g.items():
            if k in ("w_in", "w_xkv"):
                W[k] = mcs(v)
            elif k == "pool_w":
                W[k] = v.reshape(4, 4, 64, 256).transpose(1, 0, 2, 3).reshape(4, 256, 256)
            elif k not in ("glu_w1", "glu_w2"):
                W[k] = m2(v.reshape(-1, v.shape[-1]))
        if "glu_w1" in g:
            W["w12"] = jnp.concatenate([g["glu_w1"].reshape(512, 512), g["glu_w2"].reshape(512, 512)], axis=1)
        return W

    def arrived(layer, tag, after):
        keys, send, recv, sh, lands = started[(layer, tag)]
        return views(dict(zip(keys, allgather_wait(send, recv, sh, lands, after, "allgather_wait_%d%s" % (layer, tag)))))

    def weights_of(layer, x_in):
        W = views({"w_in": w_in0}) if layer == 0 else arrived(layer, "in", x_in)
        W["more"] = lambda after: arrived(layer, "", after)
        return W

    halves, pending = {}, {}

    def finish_pair(layer, after):
        keys, send, recv, flat, lands = halves.pop(layer)
        flat, got = rs_pair_wait(send, recv, flat, lands, after, "rs_pair_wait_%d" % layer)
        pair = rs_pair_sum(flat, got, cidx)
        send, recv, pair, lands, token = rs_chip_start(pair, "rs_chip_start_%d" % layer)
        pending[layer] = (keys, send, recv, pair, lands)
        return token

    def grads_done(layer, GW):
        keys = sorted(GW)
        flat = [GW[k].reshape(4, 2, GW[k].shape[1] // 2, GW[k].shape[2]) for k in keys]
        send, recv, flat, lands, token = rs_pair_start(flat, "rs_pair_start_%d" % layer)
        halves[layer] = (keys, send, recv, flat, lands)
        if layer + 1 in halves:
            token = token + finish_pair(layer + 1, token)
        return token[0:1, 0:1]

    loss, dx, G = local_step(a["x"][0], a["mem"][0], a["loss_target"][0], P, weights_of, grads_done)
    loss = lax.psum(loss[0, 0], ("x", "y", "c"))
    finish_pair(0, dx)
    outs = {}

    def update_big(names, red):
        for n, g in zip(names, rs_pair_gather([red[n] for n in names])):
            shp = a[n].shape
            g2 = g.reshape(-1, shp[-1])
            upd = adamw(a[n].reshape(g2.shape), g2, a["m_" + n].reshape(g2.shape), a["v_" + n].reshape(g2.shape),
                        "adamw_" + n, with_grad=True)
            outs[n] = tuple(t.reshape(shp) for t in upd)

    def reduce_layer(layer, red, after):
        keys, send, recv, pair, lands = pending[layer]
        pair, lands = rs_chip_wait(send, recv, pair, lands, after, "rs_chip_wait_%d" % layer)
        which = [_weight_of(k, layer) for k in keys]
        sums = rs_chip_sum(lands, pair, [l for _, l, _ in which], [red.get(n) for n, _, _ in which],
                           [layers for _, _, layers in which], jc)
        red.update(zip([n for n, _, _ in which], sums))

    red = {}
    for layer in (3, 2, 1):
        reduce_layer(layer, red, [dx])
    odd_only = [n for n in BIG if n.endswith("_cd") or n.startswith("glu")]
    update_big(odd_only, red)

    gfull = [jnp.stack(G[n]) if isinstance(G[n], list) else G[n] for n in SMALL]
    shapes = [g.shape for g in gfull]
    gsum = _unpack(allreduce_small(_pack(gfull)), shapes)
    gloc = []
    for n, g in zip(SMALL, gsum):
        if n in SMALL_SHARDED:
            g = lax.dynamic_slice_in_dim(g, j * SMALL_SHARDED[n], SMALL_SHARDED[n], axis=1)
        gloc.append(g)
    two = [(-1, a[n].shape[-1]) if a[n].ndim > 1 else (1, a[n].shape[0]) for n in SMALL]
    upds = adamw_small(*[[t.reshape(s) for t, s in zip(ts, two)]
                         for ts in ([a[n] for n in SMALL], gloc, [a["m_" + n] for n in SMALL],
                                    [a["v_" + n] for n in SMALL])])
    for n, g, upd in zip(SMALL, gloc, upds):
        outs[n] = (g,) + tuple(t.reshape(a[n].shape) for t in upd)

    behind = [outs[n][1] for n in odd_only + SMALL[-1:]] + [red[n] for n in BIG if n not in odd_only]
    reduce_layer(0, red, behind)
    update_big([n for n in BIG if n not in odd_only], red)

    res = [loss, dx[None]]
    for part in range(4):
        res += [outs[n][part] for n in WEIGHTS]
    return tuple(res)
```

```python
import math

import jax
import jax.numpy as jnp
from jax import lax
from jax.experimental import pallas as pl
from jax.experimental.pallas import tpu as pltpu

F32, BF16 = jnp.float32, jnp.bfloat16
S, D = 2048, 1024
MEM = 256
EPS = 1e-6
NEG = -1e30
QB = 128
PATTERNS = (1, 4, 16)
NG, NP, NH = 32, 64, 16
NS = NG * NP
LR, B1, B2, AEPS, WD, STEP = 0.001, 0.9, 0.999, 1e-08, 0.01, 10
MESHID = pl.DeviceIdType.MESH
VMEM_LIMIT = 56 * 1024 * 1024


def _cparams(sem):
    return pltpu.CompilerParams(dimension_semantics=sem, vmem_limit_bytes=VMEM_LIMIT)


def _sig(x):
    return 1.0 / (1.0 + jnp.exp(-x))


def _dot(a, b, dims):
    return lax.dot_general(a, b, (dims, ((), ())), preferred_element_type=F32)


def _nn(a, b):
    return _dot(a, b, ((1,), (0,)))


def _nt(a, b):
    return _dot(a, b, ((1,), (1,)))


def _tn(a, b):
    return _dot(a, b, ((0,), (0,)))


_DIMS = {"nn": ((1,), (0,)), "nt": ((1,), (1,)), "tn": ((0,), (0,))}


def _tile(dim, cc=None, cap=1024):
    for t in (2048, 1536, 1024, 768, 512, 384, 256, 128):
        if t <= cap and dim % t == 0 and (cc is None or cc % t == 0):
            return t
    return dim


MM_VMEM = 36 * 1024 * 1024


def _mm_tiles(m, n, k, ccm, ccn, cck, a_bytes, b_bytes, o_bytes):
    caps = [1024, 1024, 2048]
    while True:
        tm, tn, tk = _tile(m, ccm, caps[0]), _tile(n, ccn, caps[1]), _tile(k, cck, caps[2])
        need = 2 * (tm * tk * a_bytes + tk * tn * b_bytes + tm * tn * o_bytes) + (tm * tn * 4 if tk < k else 0)
        if need <= MM_VMEM:
            return tm, tn, tk
        if tk > 1024:
            caps[2] = tk // 2
        elif tn >= tm:
            caps[1] = tn // 2
        else:
            caps[0] = tm // 2


def m2(arr, col_off=0, ncols=None):
    rows, cols = arr.shape
    ncols = cols - col_off if ncols is None else ncols

    def spec(tr, tc, rc):
        assert col_off % tc == 0
        return pl.BlockSpec((tr, tc), lambda *g: (rc(*g)[0], rc(*g)[1] + col_off // tc))
    return (arr, rows, ncols, spec, None if col_off == 0 else col_off)


def mcs(arr):
    cs = arr.shape[2]

    def spec(tr, tc, rc):
        n = cs // tc
        return pl.BlockSpec((None, tr, tc), lambda *g: (rc(*g)[1] // n, rc(*g)[0], rc(*g)[1] % n))
    return (arr, arr.shape[1], 4 * cs, spec, cs)


def out2(rows, cols):
    def spec(tr, tc, rc):
        return pl.BlockSpec((tr, tc), lambda *g: tuple(rc(*g)))
    return ((rows, cols), spec, None)


def outcs(rows, cs):
    def spec(tr, tc, rc):
        n = cs // tc
        return pl.BlockSpec((None, tr, tc), lambda *g: (rc(*g)[1] // n, rc(*g)[0], rc(*g)[1] % n))
    return ((4, rows, cs), spec, cs)


def _both(a, b):
    if a is None:
        return b
    if b is None:
        return a
    return math.gcd(a, b)


def mm(a, b, mode, name, add=None, out=None, out_dtype=F32):
    a_arr, a_r, a_c, a_spec, a_cc = a
    b_arr, b_r, b_c, b_spec, b_cc = b
    if mode == "nn":
        m, k, n = a_r, a_c, b_c
        assert b_r == k
        ccm, cck, ccn = None, a_cc, b_cc
    elif mode == "nt":
        m, k, n = a_r, a_c, b_r
        assert b_c == k
        ccm, cck, ccn = None, _both(a_cc, b_cc), None
    else:
        m, k, n = a_c, a_r, b_c
        assert b_r == k
        ccm, cck, ccn = a_cc, None, b_cc
    out = out2(m, n) if out is None else out
    o_shape, o_spec, o_cc = out
    ccn = _both(ccn, o_cc)
    if add is not None:
        ccn = _both(ccn, add[4])
    o_bytes = jnp.dtype(out_dtype).itemsize + (0 if add is None else add[0].dtype.itemsize)
    tm, tn, tk = _mm_tiles(m, n, k, ccm, ccn, cck, a_arr.dtype.itemsize, b_arr.dtype.itemsize, o_bytes)
    nk = k // tk
    if mode == "nn":
        in_specs = [a_spec(tm, tk, lambda i, j, kk: (i, kk)), b_spec(tk, tn, lambda i, j, kk: (kk, j))]
    elif mode == "nt":
        in_specs = [a_spec(tm, tk, lambda i, j, kk: (i, kk)), b_spec(tn, tk, lambda i, j, kk: (j, kk))]
    else:
        in_specs = [a_spec(tk, tm, lambda i, j, kk: (kk, i)), b_spec(tk, tn, lambda i, j, kk: (kk, j))]
    args = [a_arr, b_arr]
    if add is not None:
        in_specs.append(add[3](tm, tn, lambda i, j, kk: (i, j)))
        args.append(add[0])
    return _mm_call(args, in_specs, o_spec(tm, tn, lambda i, j, kk: (i, j)), jax.ShapeDtypeStruct(o_shape, out_dtype),
                    mode, (m // tm, n // tn, nk), (tm, tn), add is not None, name)


def _mm_call(args, in_specs, out_spec, out_shape, mode, grid, tile, has_add, name):
    dims = _DIMS[mode]
    nk = grid[2]
    tm, tn = tile

    def body(*refs):
        a_ref, b_ref = refs[0], refs[1]
        add_ref = refs[2] if has_add else None
        prod = _dot(a_ref[...].astype(BF16), b_ref[...].astype(BF16), dims)
        if nk == 1:
            o_ref = refs[-1]
            if has_add:
                prod = prod + add_ref[...].astype(F32)
            o_ref[...] = prod.astype(o_ref.dtype)
            return
        o_ref, acc = refs[-2], refs[-1]
        kk = pl.program_id(2)

        @pl.when(kk == 0)
        def _():
            acc[...] = prod

        @pl.when(kk > 0)
        def _():
            acc[...] += prod

        @pl.when(kk == nk - 1)
        def _():
            r = acc[...]
            if has_add:
                r = r + add_ref[...].astype(F32)
            o_ref[...] = r.astype(o_ref.dtype)

    return pl.pallas_call(
        body, name=name, grid=grid, in_specs=in_specs, out_specs=out_spec, out_shape=out_shape,
        scratch_shapes=[pltpu.VMEM((tm, tn), F32)] if nk > 1 else [],
        compiler_params=_cparams(("parallel", "parallel", "arbitrary")),
    )(*args)


def mm_band(a, b, mode, name, grid, blocks, maps, out_shape, add=None, out_dtype=F32):
    in_specs = [pl.BlockSpec(blocks[0], maps[0]), pl.BlockSpec(blocks[1], maps[1])]
    args = [a, b]
    if add is not None:
        in_specs.append(pl.BlockSpec(blocks[2], maps[2]))
        args.append(add)
    return _mm_call(args, in_specs, pl.BlockSpec(blocks[2], maps[2]), jax.ShapeDtypeStruct(out_shape, out_dtype),
                    mode, grid, blocks[2], add is not None, name)


def rw(fn, ins, outs, name, rows, tr=None, consts=(), accs=()):
    tr = min(rows, 1024) if tr is None else tr
    n_in, n_c, n_o, n_a = len(ins), len(consts), len(outs), len(accs)
    in_specs = []
    for arr, off, width in ins:
        assert off % width == 0
        in_specs.append(pl.BlockSpec((tr, width), lambda i, o=off // width: (i, o)))
    for c in consts:
        in_specs.append(pl.BlockSpec(c.shape, lambda i: (0, 0)))
    out_specs = [pl.BlockSpec((tr, w), lambda i: (i, 0)) for w, _ in outs]
    out_specs += [pl.BlockSpec(s, lambda i: (0, 0)) for s in accs]
    out_shape = [jax.ShapeDtypeStruct((rows, w), dt) for w, dt in outs]
    out_shape += [jax.ShapeDtypeStruct(s, F32) for s in accs]

    def body(*refs):
        vals = [r[...] for r in refs[:n_in + n_c]]
        o_refs = refs[n_in + n_c:n_in + n_c + n_o]
        a_refs = refs[n_in + n_c + n_o:]
        res = fn(*vals)
        for r, v in zip(o_refs, res[:n_o]):
            r[...] = v.astype(r.dtype)
        if n_a:
            @pl.when(pl.program_id(0) == 0)
            def _():
                for r in a_refs:
                    r[...] = jnp.zeros_like(r)
            for r, v in zip(a_refs, res[n_o:]):
                r[...] += v

    res = pl.pallas_call(
        body, name=name, grid=(rows // tr,), in_specs=in_specs, out_specs=out_specs,
        out_shape=out_shape,
        compiler_params=_cparams(("arbitrary",) if n_a else ("parallel",)),
    )(*[a for a, _, _ in ins], *consts)
    return res


def _rstd(x):
    return lax.rsqrt(jnp.mean(x * x, axis=-1, keepdims=True) + EPS)


def rms_fwd(x, g, name):
    def fn(xv, gv):
        xv = xv.astype(F32)
        return (xv * _rstd(xv) * gv,)
    return rw(fn, [(x, 0, D)], [(D, BF16)], name, x.shape[0], consts=[g])[0]


def _rms_bwd_math(xv, dy, gv):
    r = _rstd(xv)
    dyg = dy * gv
    dx = r * dyg - xv * (r * r * r / D) * jnp.sum(dyg * xv, axis=-1, keepdims=True)
    dg = jnp.sum(dy * xv * r, axis=0, keepdims=True)
    return dx, dg


def rms_bwd(x, dy, dres, g, name):
    def fn(xv, dyv, drv, gv):
        dx, dg = _rms_bwd_math(xv, dyv, gv)
        return dx + drv, dg
    return rw(fn, [(x, 0, D), (dy, 0, D), (dres, 0, D)], [(D, F32)], name, x.shape[0],
              consts=[g], accs=[(1, D)])


def final_loss(x, tgt, g):
    def fn(xv, tv, gv):
        e = xv * _rstd(xv) * gv - tv
        loss = 0.5 * jnp.sum(jnp.sum(e * e, axis=-1, keepdims=True), axis=0, keepdims=True) / D
        dx, dg = _rms_bwd_math(xv, e / D, gv)
        return dx, loss, dg
    return rw(fn, [(x, 0, D), (tgt, 0, D)], [(D, F32)], "final_loss", S, consts=[g],
              accs=[(1, 1), (1, D)])


def _attn_bias(bias_ref):
    ii = lax.broadcasted_iota(jnp.int32, (2 * QB, 2 * QB), 0) % QB
    jj = lax.broadcasted_iota(jnp.int32, (2 * QB, 2 * QB), 1)
    dist = ii + QB - jj
    band = (dist >= 0) & (dist <= QB)
    bias_ref[1] = jnp.where(band, 0.0, NEG)
    bias_ref[0] = jnp.where(band & (jj >= QB), 0.0, NEG)


def _two_heads(x, m0):
    return jnp.concatenate([jnp.where(m0, x, 0.0), jnp.where(m0, 0.0, x)], axis=0)


def _per_head(col, m0):
    return jnp.where(m0, col[:QB], col[QB:])


def _attn_rows(idx, d):
    if d == 1:
        b = idx
        cur = pl.ds(pl.multiple_of(b * QB, QB), QB)
        prev = pl.ds(pl.multiple_of(jnp.maximum(b - 1, 0) * QB, QB), QB)
    else:
        r, b = lax.rem(idx, d), lax.div(idx, d)
        cur = pl.ds(r + b * (QB * d), QB, stride=d)
        prev = pl.ds(r + jnp.maximum(b - 1, 0) * (QB * d), QB, stride=d)
    return cur, prev, b


NBLK = S // QB
GROUP = 16
GROUP_FWD = 16


def _colblk(off):
    return pl.BlockSpec((S, 128), lambda hp: (0, off * 8 + hp))


def attn_fwd(z):
    def body(q_ref, k_ref, v_ref, g_ref, o_ref, l_ref, a_ref, os, ls, bias):
        _attn_bias(bias)
        m0 = lax.broadcasted_iota(jnp.int32, (1, 128), 1) < 64
        for pi, d in enumerate(PATTERNS):
            lone = S // d == QB

            def load(idx, d=d, lone=lone):
                cur, prev, b = _attn_rows(idx, d)
                if lone:
                    return cur, (q_ref[cur, :], None, k_ref[cur, :], None, v_ref[cur, :], bias[1, :, QB:])
                return cur, (q_ref[cur, :], k_ref[prev, :], k_ref[cur, :], v_ref[prev, :], v_ref[cur, :],
                             bias[jnp.minimum(b, 1)])

            def block(q, kp, kc, vp, vc, bs):
                qq = _two_heads(q * 0.125, m0).astype(BF16)
                k = (kc if kp is None else jnp.concatenate([kp, kc], axis=0)).astype(BF16)
                s = _nt(qq, k) + bs
                mx = jnp.max(s, axis=-1, keepdims=True)
                p = jnp.exp(s - mx)
                den = jnp.sum(p, axis=-1, keepdims=True)
                pb = p.astype(BF16)
                vv = _two_heads(vc if vp is None else jnp.concatenate([vp, vc], axis=0), m0).astype(BF16)
                o = _nn(jnp.concatenate([pb[:QB], pb[QB:]], axis=1), vv)
                return o * _per_head(1.0 / den, m0), _per_head(mx + jnp.log(den), m0)

            def step(i, carry, pi=pi):
                loaded = [load(i * GROUP_FWD + u) for u in range(GROUP_FWD)]
                done = [block(*vals) for _, vals in loaded]
                for (cur, _), (o, l) in zip(loaded, done):
                    os[pi, cur, :] = o
                    ls[pi, cur, :] = l
                return carry
            lax.fori_loop(0, NBLK // GROUP_FWD, step, 0)
        l1, l2, l3 = ls[0], ls[1], ls[2]
        mx = jnp.maximum(jnp.maximum(l1, l2), l3)
        e1, e2, e3 = jnp.exp(l1 - mx), jnp.exp(l2 - mx), jnp.exp(l3 - mx)
        tot = e1 + e2 + e3
        o = (os[0] * e1 + os[1] * e2 + os[2] * e3) / tot
        ga = g_ref[...]
        o_ref[...] = o
        l_ref[...] = mx + jnp.log(tot)
        a_ref[...] = (o * (ga * _sig(ga))).astype(a_ref.dtype)

    out = pl.BlockSpec((S, 128), lambda hp: (0, hp))
    return pl.pallas_call(
        body, name="attn_fwd", grid=(8,),
        in_specs=[_colblk(0), _colblk(1), _colblk(2), _colblk(3)], out_specs=[out] * 3,
        out_shape=[jax.ShapeDtypeStruct((S, D), F32), jax.ShapeDtypeStruct((S, D), F32),
                   jax.ShapeDtypeStruct((S, 2 * D), BF16)],
        scratch_shapes=[pltpu.VMEM((3, S, 128), F32), pltpu.VMEM((3, S, 128), F32),
                        pltpu.VMEM((2, 2 * QB, 2 * QB), F32)],
        compiler_params=_cparams(("parallel",)),
    )(z, z, z, z)


def attn_bwd(z, d_cat, o, lse):
    def body(q_ref, k_ref, v_ref, g_ref, da_ref, o_ref, l_ref, dq_ref, dk_ref, dv_ref, dg_ref, do_s, pr_s, bias):
        _attn_bias(bias)
        m0 = lax.broadcasted_iota(jnp.int32, (1, 128), 1) < 64
        ga = g_ref[...]
        sg = _sig(ga)
        da = da_ref[...]
        ov = o_ref[...]
        do = da * (ga * sg)
        dg_ref[...] = da * ov * (sg * (1.0 + ga * (1.0 - sg)))
        do_s[...] = do
        pr_s[...] = do * ov
        dq_ref[...] = jnp.zeros_like(dq_ref)
        dk_ref[...] = jnp.zeros_like(dk_ref)
        dv_ref[...] = jnp.zeros_like(dv_ref)
        for d in PATTERNS:
            lone = S // d == QB

            def load(idx, d=d, lone=lone):
                cur, prev, b = _attn_rows(idx, d)
                if lone:
                    return (cur, None), (q_ref[cur, :], None, k_ref[cur, :], None, v_ref[cur, :],
                                         do_s[cur, :], pr_s[cur, :], l_ref[cur, :], bias[1, :, QB:])
                return (cur, prev), (q_ref[cur, :], k_ref[prev, :], k_ref[cur, :], v_ref[prev, :], v_ref[cur, :],
                                     do_s[cur, :], pr_s[cur, :], l_ref[cur, :], bias[jnp.minimum(b, 1)])

            def block(q, kp, kc, vp, vc, dof, prod, lp, bs):
                qq = _two_heads(q * 0.125, m0).astype(BF16)
                kf = kc if kp is None else jnp.concatenate([kp, kc], axis=0)
                k = kf.astype(BF16)
                v = (vc if vp is None else jnp.concatenate([vp, vc], axis=0)).astype(BF16)
                dd = _two_heads(dof, m0).astype(BF16)
                lh = jnp.max(jnp.concatenate([jnp.where(m0, lp, -jnp.inf), jnp.where(m0, -jnp.inf, lp)], axis=0),
                             axis=-1, keepdims=True)
                delta = jnp.sum(_two_heads(prod, m0), axis=-1, keepdims=True)
                p = jnp.exp(_nt(qq, k) + bs - lh)
                ds = (p * (_nt(dd, v) - delta)).astype(BF16)
                dq = _nn(jnp.concatenate([ds[:QB], ds[QB:]], axis=1), _two_heads(kf, m0).astype(BF16))
                return dq * 0.125, _tn(ds, qq), _tn(p.astype(BF16), dd)

            def step(i, carry):
                loaded = [load(i * GROUP + u) for u in range(GROUP)]
                done = [block(*vals) for _, vals in loaded]
                for ((cur, prev), _), (dq, dk, dv) in zip(loaded, done):
                    dq_ref[cur, :] = dq_ref[cur, :] + dq
                    if prev is not None:
                        dk_ref[prev, :] = dk_ref[prev, :] + dk[:QB]
                        dv_ref[prev, :] = dv_ref[prev, :] + dv[:QB]
                    dk_ref[cur, :] = dk_ref[cur, :] + dk[-QB:]
                    dv_ref[cur, :] = dv_ref[cur, :] + dv[-QB:]
                return carry
            lax.fori_loop(0, NBLK // GROUP, step, 0)

    blk = pl.BlockSpec((S, 128), lambda hp: (0, hp))
    return pl.pallas_call(
        body, name="attn_bwd", grid=(8,),
        in_specs=[_colblk(0), _colblk(1), _colblk(2), _colblk(3), blk, blk, blk], out_specs=[blk] * 4,
        out_shape=[jax.ShapeDtypeStruct((S, D), F32)] * 4,
        scratch_shapes=[pltpu.VMEM((S, 128), F32), pltpu.VMEM((S, 128), F32), pltpu.VMEM((2, 2 * QB, 2 * QB), F32)],
        compiler_params=_cparams(("parallel",)),
    )(z, z, z, z, d_cat, o, lse)


def assemble_dz_even(parts):
    def body(*refs):
        o_ref = refs[-1]
        for j in range(6):
            o_ref[:, j * D:(j + 1) * D] = refs[j][...].astype(o_ref.dtype)
    tr = 512
    blk = pl.BlockSpec((tr, D), lambda i: (i, 0))
    return pl.pallas_call(
        body, name="assemble_dz_even", grid=(S // tr,), in_specs=[blk] * 6,
        out_specs=pl.BlockSpec((tr, 6 * D), lambda i: (i, 0)),
        out_shape=jax.ShapeDtypeStruct((S, 6 * D), BF16),
        compiler_params=_cparams(("parallel",)),
    )(*parts)


def _pool_window(g):
    return jnp.where(g == 0, 2.0, jnp.where(g == 1, 4.0, jnp.where(g == 2, 8.0, 16.0)))


def _pool_sel(g, levels):
    return jnp.where(g == 0, levels[0], jnp.where(g == 1, levels[1], jnp.where(g == 2, levels[2], levels[3])))


def _pool_fwd_math(v, g):
    t = lax.broadcasted_iota(jnp.int32, (S, 1), 0)
    s = v
    levels = []
    for k in (1, 2, 4, 8):
        s = s + jnp.where(t >= k, pltpu.roll(s, k, 0), 0.0)
        levels.append(s)
    cnt = jnp.minimum((t + 1).astype(F32), _pool_window(g))
    return _pool_sel(g, levels) / cnt - v, cnt


def pool_fwd(z, pw, ps, cat):
    def body(v_ref, g_ref, pw_ref, ps_ref, cat_ref, o_ref):
        g = pl.program_id(0)
        pooled, _ = _pool_fwd_math(v_ref[...], g)
        mixed = _nn(pooled.astype(BF16), pw_ref[...].astype(BF16))
        gb = g_ref[...]
        o_ref[...] = (mixed * ps_ref[...] * (gb * _sig(gb))).astype(o_ref.dtype)

    return pl.pallas_call(
        body, name="pool_fwd", grid=(4,),
        in_specs=[pl.BlockSpec((S, 256), lambda g: (0, 16 + g)),
                  pl.BlockSpec((S, 256), lambda g: (0, 20 + g)),
                  pl.BlockSpec((None, 256, 256), lambda g: (g, 0, 0)),
                  pl.BlockSpec((1, 256), lambda g: (0, g)), pl.BlockSpec(memory_space=pl.ANY)],
        out_specs=pl.BlockSpec((S, 256), lambda g: (0, 4 + g)),
        out_shape=jax.ShapeDtypeStruct((S, 2 * D), BF16),
        input_output_aliases={4: 0},
        compiler_params=_cparams(("parallel",)),
    )(z, z, pw, ps, cat)


def pool_bwd(z, d_cat, pw, ps):
    def body(v_ref, g_ref, d_ref, pw_ref, ps_ref, dv_ref, dg_ref, dpw_ref, dps_ref):
        g = pl.program_id(0)
        v = v_ref[...]
        pooled, cnt = _pool_fwd_math(v, g)
        pwb = pw_ref[...].astype(BF16)
        pb = pooled.astype(BF16)
        mixed = _nn(pb, pwb)
        gb = g_ref[...]
        sg = _sig(gb)
        dout = d_ref[...]
        sc = ps_ref[...]
        dg_ref[...] = dout * mixed * sc * (sg * (1.0 + gb * (1.0 - sg)))
        dms = dout * (gb * sg)
        dps_ref[...] = jnp.sum(dms * mixed, axis=0, keepdims=True)
        dmx = (dms * sc).astype(BF16)
        dpw_ref[...] = _tn(pb, dmx)
        dpooled = _nt(dmx, pwb)
        t = lax.broadcasted_iota(jnp.int32, (S, 1), 0)
        s = dpooled / cnt
        levels = []
        for k in (1, 2, 4, 8):
            s = s + jnp.where(t < S - k, pltpu.roll(s, S - k, 0), 0.0)
            levels.append(s)
        dv_ref[...] = _pool_sel(g, levels) - dpooled

    return pl.pallas_call(
        body, name="pool_bwd", grid=(4,),
        in_specs=[pl.BlockSpec((S, 256), lambda g: (0, 16 + g)),
                  pl.BlockSpec((S, 256), lambda g: (0, 20 + g)),
                  pl.BlockSpec((S, 256), lambda g: (0, 4 + g)),
                  pl.BlockSpec((None, 256, 256), lambda g: (g, 0, 0)),
                  pl.BlockSpec((1, 256), lambda g: (0, g))],
        out_specs=[pl.BlockSpec((S, 256), lambda g: (0, g)),
                   pl.BlockSpec((S, 256), lambda g: (0, g)),
                   pl.BlockSpec((None, 256, 256), lambda g: (g, 0, 0)),
                   pl.BlockSpec((1, 256), lambda g: (0, g))],
        out_shape=[jax.ShapeDtypeStruct((S, D), F32), jax.ShapeDtypeStruct((S, D), F32),
                   jax.ShapeDtypeStruct((4, 256, 256), F32), jax.ShapeDtypeStruct((1, D), F32)],
        compiler_params=_cparams(("parallel",)),
    )(z, z, d_cat, pw, ps)


CH = 128


def _sgu_common(v, lng, lnb, w_ref):
    mu = jnp.mean(v, axis=-1, keepdims=True)
    vc = v - mu
    rs = lax.rsqrt(jnp.mean(vc * vc, axis=-1, keepdims=True) + EPS)
    xhat = vc * rs
    vn = (xhat * lng + lnb).astype(BF16)
    ri = lax.broadcasted_iota(jnp.int32, (CH, CH), 0)
    ci = lax.broadcasted_iota(jnp.int32, (CH, CH), 1)
    tril = ri >= ci
    ws = [jnp.where(tril, w_ref[g], 0.0).astype(BF16) for g in range(4)]
    return xhat, rs, vn, tril, ws


def _zspec(off):
    return pl.BlockSpec((CH, D), lambda c: (c, off))


def _full(shape):
    return pl.BlockSpec(shape, lambda c: (0,) * len(shape))


def sgu_fwd(z, lng, lnb, w, bfull):
    def body(u_ref, v_ref, g_ref, lng_ref, lnb_ref, w_ref, b_ref, o_ref):
        _, _, vn, _, ws = _sgu_common(v_ref[...], lng_ref[...], lnb_ref[...], w_ref)
        for g in range(4):
            sl = slice(g * 256, (g + 1) * 256)
            mixed = _nn(ws[g], vn[:, sl]) + b_ref[:, sl]
            gc = g_ref[:, sl]
            o_ref[:, sl] = (u_ref[:, sl] * mixed * (gc * _sig(gc))).astype(o_ref.dtype)

    return pl.pallas_call(
        body, name="sgu_fwd", grid=(S // CH,),
        in_specs=[_zspec(0), _zspec(1), _zspec(2), _full((1, D)), _full((1, D)),
                  _full((4, CH, CH)), _full((CH, D))],
        out_specs=pl.BlockSpec((CH, D), lambda c: (c, 0)),
        out_shape=jax.ShapeDtypeStruct((S, D), BF16),
        compiler_params=_cparams(("parallel",)),
    )(z, z, z, lng, lnb, w, bfull)


def sgu_bwd(z, d_cat, lng, lnb, w, bfull):
    def body(u_ref, v_ref, g_ref, d_ref, lng_ref, lnb_ref, w_ref, b_ref,
             du_ref, dv_ref, dg_ref, dw_ref, db_ref, dlg_ref, dlb_ref):
        @pl.when(pl.program_id(0) == 0)
        def _():
            dw_ref[...] = jnp.zeros_like(dw_ref)
            db_ref[...] = jnp.zeros_like(db_ref)
            dlg_ref[...] = jnp.zeros_like(dlg_ref)
            dlb_ref[...] = jnp.zeros_like(dlb_ref)

        lng = lng_ref[...]
        xhat, rs, vn, tril, ws = _sgu_common(v_ref[...], lng, lnb_ref[...], w_ref)
        lane = lax.broadcasted_iota(jnp.int32, (1, 128), 1)
        db = jnp.zeros((CH, 128), F32)
        dvn_parts = []
        for g in range(4):
            sl = slice(g * 256, (g + 1) * 256)
            mixed = _nn(ws[g], vn[:, sl]) + b_ref[:, sl]
            gc = g_ref[:, sl]
            sg = _sig(gc)
            u = u_ref[:, sl]
            dc = d_ref[:, sl]
            du_ref[:, sl] = dc * mixed * (gc * sg)
            dg_ref[:, sl] = dc * u * mixed * (sg * (1.0 + gc * (1.0 - sg)))
            dmx = dc * u * (gc * sg)
            db = db + jnp.where(lane == g, jnp.sum(dmx, axis=-1, keepdims=True), 0.0)
            dmb = dmx.astype(BF16)
            dw_ref[g] += jnp.where(tril, _nt(dmb, vn[:, sl]), 0.0)
            dvn_parts.append(_tn(ws[g], dmb))
        db_ref[...] += db
        dvn = jnp.concatenate(dvn_parts, axis=1)
        dlb_ref[...] += jnp.sum(dvn, axis=0, keepdims=True)
        dlg_ref[...] += jnp.sum(dvn * xhat, axis=0, keepdims=True)
        dxh = dvn * lng
        dv_ref[...] = rs * (dxh - jnp.mean(dxh, axis=-1, keepdims=True)
                            - xhat * jnp.mean(dxh * xhat, axis=-1, keepdims=True))

    row = pl.BlockSpec((CH, D), lambda c: (c, 0))
    return pl.pallas_call(
        body, name="sgu_bwd", grid=(S // CH,),
        in_specs=[_zspec(0), _zspec(1), _zspec(2), row, _full((1, D)), _full((1, D)),
                  _full((4, CH, CH)), _full((CH, D))],
        out_specs=[row, row, row, _full((4, CH, CH)), _full((CH, 128)), _full((1, D)), _full((1, D))],
        out_shape=[jax.ShapeDtypeStruct((S, D), F32)] * 3
        + [jax.ShapeDtypeStruct((4, CH, CH), F32), jax.ShapeDtypeStruct((CH, 128), F32),
           jax.ShapeDtypeStruct((1, D), F32), jax.ShapeDtypeStruct((1, D), F32)],
        compiler_params=_cparams(("arbitrary",)),
    )(z, z, z, d_cat, lng, lnb, w, bfull)


TB = 256


def _cmul(ar, ai, br, bi):
    return ar * br - ai * bi, ar * bi + ai * br


def _scan_consts(ar, ai, reverse):
    a2 = _cmul(ar, ai, ar, ai)
    a4 = _cmul(*a2, *a2)
    row = lax.broadcasted_iota(jnp.int32, (8, NS), 0)

    def masked(k, p):
        keep = (row < 8 - k) if reverse else (row >= k)
        return jnp.where(keep, p[0], 0.0), jnp.where(keep, p[1], 0.0)
    pr = jnp.zeros((8, NS), F32)
    pi = jnp.zeros((8, NS), F32)
    cr, ci = ar, ai
    for r in range(8):
        sel = row == (7 - r if reverse else r)
        pr = jnp.where(sel, cr, pr)
        pi = jnp.where(sel, ci, pi)
        cr, ci = _cmul(cr, ci, ar, ai)
    return (masked(1, (ar, ai)), masked(2, a2), masked(4, a4)), (pr, pi), row


def scan_fwd(bu, abr, abi):
    def body(bu_ref, ar_ref, ai_ref, h_ref, car, cai):
        @pl.when(pl.program_id(0) == 0)
        def _():
            car[...] = jnp.zeros_like(car)
            cai[...] = jnp.zeros_like(cai)

        pows, (pr, pi), row = _scan_consts(ar_ref[...], ai_ref[...], False)

        def tile(t, carry):
            c_r, c_i = carry
            rows = pl.ds(pl.multiple_of(t * 8, 8), 8)
            xr = bu_ref[rows, 0:NS]
            xi = bu_ref[rows, NS:2 * NS]
            for k, (kr, ki) in zip((1, 2, 4), pows):
                sr = pltpu.roll(xr, k, 0)
                si = pltpu.roll(xi, k, 0)
                xr, xi = xr + kr * sr - ki * si, xi + kr * si + ki * sr
            xr, xi = xr + pr * c_r - pi * c_i, xi + pr * c_i + pi * c_r
            h_ref[rows, 0:NS] = xr
            h_ref[rows, NS:2 * NS] = xi
            return (jnp.broadcast_to(xr[7:8, :], (8, NS)), jnp.broadcast_to(xi[7:8, :], (8, NS)))

        c_r, c_i = lax.fori_loop(0, TB // 8, tile, (car[...], cai[...]))
        car[...] = c_r
        cai[...] = c_i

    return pl.pallas_call(
        body, name="s5_scan_fwd", grid=(S // TB,),
        in_specs=[pl.BlockSpec((TB, 2 * NS), lambda i: (i, 0)),
                  pl.BlockSpec((1, NS), lambda i: (0, 0)), pl.BlockSpec((1, NS), lambda i: (0, 0))],
        out_specs=pl.BlockSpec((TB, 2 * NS), lambda i: (i, 0)),
        out_shape=jax.ShapeDtypeStruct((S, 2 * NS), F32),
        scratch_shapes=[pltpu.VMEM((8, NS), F32), pltpu.VMEM((8, NS), F32)],
        compiler_params=_cparams(("arbitrary",)),
    )(bu, abr, abi)


def scan_bwd(eta, h, abr, abi):
    nt = S // TB

    def body(e_ref, h_ref, ar_ref, ai_ref, l_ref, da_ref, car, cai):
        @pl.when(pl.program_id(0) == 0)
        def _():
            car[...] = jnp.zeros_like(car)
            cai[...] = jnp.zeros_like(cai)
            da_ref[...] = jnp.zeros_like(da_ref)

        pows, (pr, pi), row = _scan_consts(ar_ref[...], -ai_ref[...], True)

        def tile(tt, carry):
            c_r, c_i, acr, aci = carry
            t = TB // 8 - 1 - tt
            rows = pl.ds(pl.multiple_of(t * 8, 8), 8)
            xr = e_ref[rows, 0:NS]
            xi = e_ref[rows, NS:2 * NS]
            for k, (kr, ki) in zip((1, 2, 4), pows):
                sr = pltpu.roll(xr, 8 - k, 0)
                si = pltpu.roll(xi, 8 - k, 0)
                xr, xi = xr + kr * sr - ki * si, xi + kr * si + ki * sr
            xr, xi = xr + pr * c_r - pi * c_i, xi + pr * c_i + pi * c_r
            l_ref[rows, 0:NS] = xr
            l_ref[rows, NS:2 * NS] = xi
            nr = jnp.where(row < 7, pltpu.roll(xr, 7, 0), c_r)
            ni = jnp.where(row < 7, pltpu.roll(xi, 7, 0), c_i)
            hr = h_ref[rows, 0:NS]
            hi = h_ref[rows, NS:2 * NS]
            acr = acr + hr * nr + hi * ni
            aci = aci + hr * ni - hi * nr
            return (jnp.broadcast_to(xr[0:1, :], (8, NS)), jnp.broadcast_to(xi[0:1, :], (8, NS)), acr, aci)

        zero = jnp.zeros((8, NS), F32)
        c_r, c_i, acr, aci = lax.fori_loop(0, TB // 8, tile, (car[...], cai[...], zero, zero))
        car[...] = c_r
        cai[...] = c_i
        da_ref[:, 0:NS] += acr
        da_ref[:, NS:2 * NS] += aci

    rev = pl.BlockSpec((TB, 2 * NS), lambda i: (nt - 1 - i, 0))
    return pl.pallas_call(
        body, name="s5_scan_bwd", grid=(nt,),
        in_specs=[rev, rev, pl.BlockSpec((1, NS), lambda i: (0, 0)), pl.BlockSpec((1, NS), lambda i: (0, 0))],
        out_specs=[rev, pl.BlockSpec((8, 2 * NS), lambda i: (0, 0))],
        out_shape=[jax.ShapeDtypeStruct((S, 2 * NS), F32), jax.ShapeDtypeStruct((8, 2 * NS), F32)],
        scratch_shapes=[pltpu.VMEM((8, NS), F32), pltpu.VMEM((8, NS), F32)],
        compiler_params=_cparams(("arbitrary",)),
    )(eta, h, abr, abi)


GC = 0.7978845608028654
GA = 0.044715


def s5_post(hc, z, dskip):
    def fn(hv, xd, dv):
        y = hv + dv * xd
        return y, 0.5 * y * (1.0 + jnp.tanh(GC * (y + GA * y * y * y)))
    return rw(fn, [(hc, 0, 512), (z, 3072, 512)], [(512, F32), (512, BF16)], "s5_post", S, consts=[dskip])


def s5_post_bwd(dyg, ypre, z, dskip):
    def fn(dy, y, xd, dv):
        th = jnp.tanh(GC * (y + GA * y * y * y))
        dg = 0.5 * (1.0 + th) + 0.5 * y * (1.0 - th * th) * GC * (1.0 + 3.0 * GA * y * y)
        dyp = dy * dg
        return dyp, dyp * dv, jnp.sum(dyp * xd, axis=0, keepdims=True)
    return rw(fn, [(dyg, 0, 512), (ypre, 0, 512), (z, 3072, 512)], [(512, BF16), (512, F32)],
              "s5_post_bwd", S, consts=[dskip], accs=[(1, 512)])


def glu_fwd(t, z, c_out):
    def fn(t1, t2, gd, co):
        return (jnp.concatenate([co, (t1 * _sig(t2) * (gd * _sig(gd))).astype(BF16)], axis=1),)
    return rw(fn, [(t, 0, 512), (t, 512, 512), (z, 3584, 512), (c_out, 0, D)], [(D + 512, BF16)], "glu_fwd", S)[0]


def glu_bwd(t, z, d_cat):
    def fn(t1, t2, gd, dd):
        s2, sg = _sig(t2), _sig(gd)
        sl = gd * sg
        return (jnp.concatenate([dd * s2 * sl, dd * t1 * s2 * (1.0 - s2) * sl], axis=1),
                dd * t1 * s2 * (sg * (1.0 + gd * (1.0 - sg))))
    return rw(fn, [(t, 0, 512), (t, 512, 512), (z, 3584, 512), (d_cat, 1024, 512)],
              [(D, BF16), (512, F32)], "glu_bwd", S)


def assemble_dz_odd(du, dv, dgc, dxd, dgd):
    def body(a, b, c, d, e, o_ref):
        o_ref[:, 0:D] = a[...].astype(BF16)
        o_ref[:, D:2 * D] = b[...].astype(BF16)
        o_ref[:, 2 * D:3 * D] = c[...].astype(BF16)
        o_ref[:, 3 * D:3 * D + 512] = d[...].astype(BF16)
        o_ref[:, 3 * D + 512:4 * D] = e[...].astype(BF16)
    tr = 512
    blk = pl.BlockSpec((tr, D), lambda i: (i, 0))
    half = pl.BlockSpec((tr, 512), lambda i: (i, 0))
    return pl.pallas_call(
        body, name="assemble_dz_odd", grid=(S // tr,), in_specs=[blk, blk, blk, half, half],
        out_specs=pl.BlockSpec((tr, 4 * D), lambda i: (i, 0)),
        out_shape=jax.ShapeDtypeStruct((S, 4 * D), BF16),
        compiler_params=_cparams(("parallel",)),
    )(du, dv, dgc, dxd, dgd)


TQ = 1024


def _xattn_probs(qh, kh):
    s = _nt(qh, kh) * 0.0625
    p = jnp.exp(s - jnp.max(s, axis=-1, keepdims=True))
    return p / jnp.sum(p, axis=-1, keepdims=True)


def xattn_fwd(q, kv):
    def body(q_ref, kv_ref, o_ref):
        outs = []
        for h in range(4):
            sl = slice(h * 256, (h + 1) * 256)
            p = _xattn_probs(q_ref[:, sl].astype(BF16), kv_ref[:, sl].astype(BF16))
            vh = kv_ref[:, D + h * 256:D + (h + 1) * 256].astype(BF16)
            outs.append((sl, _nn(p.astype(BF16), vh)))
        for sl, o in outs:
            o_ref[:, sl] = o.astype(o_ref.dtype)

    return pl.pallas_call(
        body, name="xattn_fwd", grid=(S // TQ,),
        in_specs=[pl.BlockSpec((TQ, D), lambda i: (i, 0)), pl.BlockSpec((MEM, 2 * D), lambda i: (0, 0))],
        out_specs=pl.BlockSpec((TQ, D), lambda i: (i, 0)),
        out_shape=jax.ShapeDtypeStruct((S, D), BF16),
        compiler_params=_cparams(("parallel",)),
    )(q, kv)


def xattn_bwd(q, kv, d_o):
    def body(q_ref, kv_ref, do_ref, dq_ref, dkv_ref):
        @pl.when(pl.program_id(0) == 0)
        def _():
            dkv_ref[...] = jnp.zeros_like(dkv_ref)

        done = []
        for h in range(4):
            sl = slice(h * 256, (h + 1) * 256)
            vs = slice(D + h * 256, D + (h + 1) * 256)
            qh = q_ref[:, sl].astype(BF16)
            kh = kv_ref[:, sl].astype(BF16)
            vh = kv_ref[:, vs].astype(BF16)
            doh = do_ref[:, sl].astype(BF16)
            p = _xattn_probs(qh, kh)
            dp = _nt(doh, vh)
            ds = (p * (dp - jnp.sum(p * dp, axis=-1, keepdims=True)) * 0.0625).astype(BF16)
            done.append((sl, vs, _nn(ds, kh), _tn(ds, qh), _tn(p.astype(BF16), doh)))
        for sl, vs, dq, dk, dv in done:
            dq_ref[:, sl] = dq.astype(dq_ref.dtype)
            dkv_ref[:, sl] += dk
            dkv_ref[:, vs] += dv

    return pl.pallas_call(
        body, name="xattn_bwd", grid=(S // TQ,),
        in_specs=[pl.BlockSpec((TQ, D), lambda i: (i, 0)), pl.BlockSpec((MEM, 2 * D), lambda i: (0, 0)),
                  pl.BlockSpec((TQ, D), lambda i: (i, 0))],
        out_specs=[pl.BlockSpec((TQ, D), lambda i: (i, 0)), pl.BlockSpec((MEM, 2 * D), lambda i: (0, 0))],
        out_shape=[jax.ShapeDtypeStruct((S, D), BF16), jax.ShapeDtypeStruct((MEM, 2 * D), F32)],
        compiler_params=_cparams(("arbitrary",)),
    )(q, kv, d_o)


def _s5_disc(a_re, a_im, log_dt, b_re, b_im):
    dt = jnp.exp(log_dt)[:, None]
    mag = jnp.exp(dt * a_re)
    abr = mag * jnp.cos(dt * a_im)
    abi = mag * jnp.sin(dt * a_im)
    nr, ni = abr - 1.0, abi
    inv = 1.0 / (a_re * a_re + a_im * a_im)
    cr = (nr * a_re + ni * a_im) * inv
    ci = (ni * a_re - nr * a_im) * inv
    bbr = cr[..., None] * b_re - ci[..., None] * b_im
    bbi = cr[..., None] * b_im + ci[..., None] * b_re
    return abr, abi, bbr, bbi


VM = pl.BlockSpec(memory_space=pltpu.VMEM)


def s5_embed(bt_re, bt_im, ct_re, ct_im):
    def body(br, bi, cr, ci, b_ref, c_ref):
        b_ref[...] = jnp.zeros_like(b_ref)
        c_ref[...] = jnp.zeros_like(c_ref)
        for g in range(NG):
            rows, cols = slice(g * NH, (g + 1) * NH), slice(g * NP, (g + 1) * NP)
            b_ref[rows, cols] = br[g]
            b_ref[rows, NS + g * NP:NS + (g + 1) * NP] = bi[g]
            c_ref[cols, rows] = cr[g]
            c_ref[NS + g * NP:NS + (g + 1) * NP, rows] = -ci[g]

    return pl.pallas_call(
        body, name="s5_embed", in_specs=[VM] * 4, out_specs=[VM] * 2,
        out_shape=[jax.ShapeDtypeStruct((NG * NH, 2 * NS), F32), jax.ShapeDtypeStruct((2 * NS, NG * NH), F32)],
        compiler_params=pltpu.CompilerParams(vmem_limit_bytes=VMEM_LIMIT),
    )(bt_re, bt_im, ct_re, ct_im)


def s5_extract(gb, gc):
    def body(gb_ref, gc_ref, br, bi, cr, ci):
        for g in range(NG):
            rows, cols = slice(g * NH, (g + 1) * NH), slice(g * NP, (g + 1) * NP)
            br[g] = gb_ref[rows, cols]
            bi[g] = gb_ref[rows, NS + g * NP:NS + (g + 1) * NP]
            cr[g] = gc_ref[cols, rows]
            ci[g] = -gc_ref[NS + g * NP:NS + (g + 1) * NP, rows]

    return pl.pallas_call(
        body, name="s5_extract", in_specs=[VM] * 2, out_specs=[VM] * 4,
        out_shape=[jax.ShapeDtypeStruct((NG, NH, NP), F32)] * 2 + [jax.ShapeDtypeStruct((NG, NP, NH), F32)] * 2,
        compiler_params=pltpu.CompilerParams(vmem_limit_bytes=VMEM_LIMIT),
    )(gb, gc)


HC, HS = NG * NH // 2, NS // 2
TS = 1024


def s5_to_states(x, w, mode, name, z_off=0):
    if mode == "nn":
        wb, wm = (HC, HS), lambda i, j, kk: (j % 2, j)
    else:
        wb, wm = (HS, HC), lambda i, j, kk: (j, j % 2)
    return mm_band(x, w, mode, name, (S // TS, 4, 1), ((TS, HC), wb, (TS, HS)),
                   (lambda i, j, kk: (i, z_off + j % 2), wm, lambda i, j, kk: (i, j)), (S, 2 * NS))


def s5_to_channels(x, w, mode, name, add=None):
    if mode == "nn":
        wb, wm = (HS, HC), lambda i, j, kk: (j + 2 * kk, j)
    else:
        wb, wm = (HC, HS), lambda i, j, kk: (j, j + 2 * kk)
    return mm_band(x, w, mode, name, (S // TS, 2, 2), ((TS, HS), wb, (TS, HC)),
                   (lambda i, j, kk: (i, j + 2 * kk), wm, lambda i, j, kk: (i, j)), (S, NG * NH), add=add)


def s5_outer(a, b, name, states_first, z_off=0):
    if states_first:
        return mm_band(a, b, "tn", name, (4, 1, 1), ((S, HS), (S, HC), (HS, HC)),
                       (lambda i, j, kk: (0, i), lambda i, j, kk: (0, i % 2), lambda i, j, kk: (i, i % 2)),
                       (2 * NS, NG * NH))
    return mm_band(a, b, "tn", name, (1, 4, 1), ((S, HC), (S, HS), (HC, HS)),
                   (lambda i, j, kk: (0, z_off + j % 2), lambda i, j, kk: (0, j), lambda i, j, kk: (j % 2, j)),
                   (NG * NH, 2 * NS))


def _fwd_even(i, x, P, W):
    hn = rms_fwd(x, P["norm_ab"][i:i + 1], "rms_ab_fwd")
    z = mm(m2(hn), W["w_in"], "nn", "in_ab")
    o, lse, cat = attn_fwd(z)
    if "more" in W:
        W.update(W.pop("more")(cat))
    cat = pool_fwd(z, W["pool_w"], P["pool_scale"][i:i + 1], cat)
    x_mid = mm(m2(cat), W["w_out"], "nn", "out_ab", add=m2(x))
    return x_mid, dict(x=x, hn=hn, z=z, o=o, lse=lse, cat=cat)


def _bwd_even(i, dx_mid, sv, P, W, G, GW):
    z = sv["z"]
    d_cat = mm(m2(dx_mid), W["w_out"], "nt", "out_ab_dx")
    GW["w_out"] = mm(m2(sv["cat"]), m2(dx_mid), "tn", "out_ab_dw").reshape(4, 512, D)
    dq, dk, dv, dga = attn_bwd(z, d_cat, sv["o"], sv["lse"])
    dvb, dgb, dpw, dps = pool_bwd(z, d_cat, W["pool_w"], P["pool_scale"][i:i + 1])
    GW["pool_w"] = dpw.reshape(4, 4, 64, 256).transpose(1, 0, 2, 3).reshape(4, 256, 256)
    G["pool_scale"][i] = dps[0]
    d_z = assemble_dz_even((dq, dk, dv, dga, dvb, dgb))
    d_hn = mm(m2(d_z), W["w_in"], "nt", "in_ab_dx")
    GW["w_in"] = mm(m2(sv["hn"]), m2(d_z), "tn", "in_ab_dw", out=outcs(D, 1536))
    return d_hn, P["norm_ab"][i:i + 1], "norm_ab", "rms_ab_bwd"


def _fwd_odd(i, x, P, W):
    hn = rms_fwd(x, P["norm_cd"][i:i + 1], "rms_cd_fwd")
    z = mm(m2(hn), W["w_in"], "nn", "in_cd")
    bfull = jnp.repeat(P["sgu_b"][i].T, 256, axis=1)
    c_out = sgu_fwd(z, P["sgu_ln_g"][i:i + 1], P["sgu_ln_b"][i:i + 1], P["sgu_w"][i], bfull)
    disc, disc_vjp = jax.vjp(_s5_disc, P["s5_a_re"][i], P["s5_a_im"][i], P["s5_log_dt"][i],
                             P["s5_b_re"][i], P["s5_b_im"][i])
    abr, abi, bbr, bbi = disc
    bbd, cbd = s5_embed(bbr.transpose(0, 2, 1), bbi.transpose(0, 2, 1),
                        P["s5_c_re"][i].transpose(0, 2, 1), P["s5_c_im"][i].transpose(0, 2, 1))
    abr, abi = abr.reshape(1, NS), abi.reshape(1, NS)
    bu = s5_to_states(z, bbd, "nn", "s5_bu", z_off=3072 // HC)
    h = scan_fwd(bu, abr, abi)
    hc = s5_to_channels(h, cbd, "nn", "s5_hc")
    dskip = P["s5_d"][i:i + 1]
    ypre, yg = s5_post(hc, z, dskip)
    if "more" in W:
        W.update(W.pop("more")(yg))
    w12 = W["w12"]
    t = mm(m2(yg), m2(w12), "nn", "glu_t")
    cat = glu_fwd(t, z, c_out)
    x_mid = mm(m2(cat), W["w_out"], "nn", "out_cd", add=m2(x))
    return x_mid, dict(x=x, hn=hn, z=z, bfull=bfull, disc_vjp=disc_vjp, bbd=bbd, cbd=cbd, abr=abr,
                       abi=abi, h=h, ypre=ypre, yg=yg, w12=w12, t=t, cat=cat, dskip=dskip)


def _bwd_odd(i, dx_mid, sv, P, W, G, GW):
    z = sv["z"]
    d_cat = mm(m2(dx_mid), W["w_out"], "nt", "out_cd_dx")
    GW["w_out"] = mm(m2(sv["cat"]), m2(dx_mid), "tn", "out_cd_dw").reshape(4, 384, D)
    du, dv, dgc, dws, dbs, dlg, dlb = sgu_bwd(z, d_cat, P["sgu_ln_g"][i:i + 1], P["sgu_ln_b"][i:i + 1],
                                               P["sgu_w"][i], sv["bfull"])
    G["sgu_w"][i], G["sgu_b"][i] = dws, dbs[:, :4].T
    G["sgu_ln_g"][i], G["sgu_ln_b"][i] = dlg[0], dlb[0]
    dt, dgd = glu_bwd(sv["t"], z, d_cat)
    gw12 = mm(m2(sv["yg"]), m2(dt), "tn", "glu_dw")
    GW["glu_w1"] = gw12[:, :512].reshape(4, 128, 512)
    GW["glu_w2"] = gw12[:, 512:].reshape(4, 128, 512)
    dyg = mm(m2(dt), m2(sv["w12"]), "nt", "glu_dx")
    dypre, dxd1, dd = s5_post_bwd(dyg, sv["ypre"], z, sv["dskip"])
    G["s5_d"][i] = dd[0]
    gcbd = s5_outer(sv["h"], dypre, "s5_dc", states_first=True)
    eta = s5_to_states(dypre, sv["cbd"], "nt", "s5_eta")
    lam, dacc = scan_bwd(eta, sv["h"], sv["abr"], sv["abi"])
    gbbd = s5_outer(z, lam, "s5_db", states_first=False, z_off=3072 // HC)
    dxd = s5_to_channels(lam, sv["bbd"], "nt", "s5_dx", add=dxd1)
    dacc = jnp.sum(dacc, axis=0)
    dbt_re, dbt_im, dct_re, dct_im = s5_extract(gbbd, gcbd)
    G["s5_c_re"][i], G["s5_c_im"][i] = dct_re.transpose(0, 2, 1), dct_im.transpose(0, 2, 1)
    d_bbr, d_bbi = dbt_re.transpose(0, 2, 1), dbt_im.transpose(0, 2, 1)
    (G["s5_a_re"][i], G["s5_a_im"][i], G["s5_log_dt"][i], G["s5_b_re"][i], G["s5_b_im"][i]) = sv["disc_vjp"](
        (dacc[:NS].reshape(NG, NP), dacc[NS:].reshape(NG, NP), d_bbr, d_bbi))
    d_z = assemble_dz_odd(du, dv, dgc, dxd, dgd)
    d_hn = mm(m2(d_z), W["w_in"], "nt", "in_cd_dx")
    GW["w_in"] = mm(m2(sv["hn"]), m2(d_z), "tn", "in_cd_dw", out=outcs(D, 1024))
    return d_hn, P["norm_cd"][i:i + 1], "norm_cd", "rms_cd_bwd"


def _fwd_x(l, x, mem_n, P, W):
    if "more_x" in W:
        W.update(W.pop("more_x")(x))
    hx = rms_fwd(x, P["norm_x"][l:l + 1], "rms_x_fwd")
    q = mm(m2(hx), W["w_xq"], "nn", "xq", out_dtype=BF16)
    kv = mm(m2(mem_n), W["w_xkv"], "nn", "xkv", out_dtype=BF16)
    ox = xattn_fwd(q, kv)
    x_out = mm(m2(ox), W["w_xo"], "nn", "xo", add=m2(x))
    return x_out, dict(x=x, hx=hx, q=q, kv=kv, ox=ox)


def _bwd_x(l, dx_out, sv, mem_n, d_memn, P, W, G, GW):
    d_ox = mm(m2(dx_out), W["w_xo"], "nt", "xo_dx", out_dtype=BF16)
    GW["w_xo"] = mm(m2(sv["ox"]), m2(dx_out), "tn", "xo_dw").reshape(4, 256, D)
    dq, dkv = xattn_bwd(sv["q"], sv["kv"], d_ox)
    GW["w_xq"] = mm(m2(sv["hx"]), m2(dq), "tn", "xq_dw").reshape(4, 256, D)
    d_hx = mm(m2(dq), W["w_xq"], "nt", "xq_dx")
    GW["w_xkv"] = mm(m2(mem_n), m2(dkv), "tn", "xkv_dw", out=outcs(D, 512))
    d_memn = mm(m2(dkv), W["w_xkv"], "nt", "xkv_dx", add=None if d_memn is None else m2(d_memn))
    dx, dg = rms_bwd(sv["x"], d_hx, dx_out, P["norm_x"][l:l + 1], "rms_x_bwd")
    G["norm_x"][l] = dg[0]
    return dx, d_memn


SMALL_LAYERS = (("norm_ab", 2), ("pool_scale", 2), ("norm_cd", 2), ("sgu_ln_g", 2), ("sgu_ln_b", 2), ("sgu_w", 2),
                ("sgu_b", 2), ("s5_a_re", 2), ("s5_a_im", 2), ("s5_log_dt", 2), ("s5_b_re", 2), ("s5_b_im", 2),
                ("s5_c_re", 2), ("s5_c_im", 2), ("s5_d", 2), ("norm_x", 4))


def local_step(x, mem, tgt, P, weights_of, grads_done):
    G = {k: [None] * n for k, n in SMALL_LAYERS}
    mem_g = P["mem_norm"].reshape(1, D)
    mem_n = rms_fwd(mem, mem_g, "rms_mem_fwd")
    saved = []
    for layer in range(4):
        i = layer // 2
        W = weights_of(layer, x)
        x, sv_m = (_fwd_even if layer % 2 == 0 else _fwd_odd)(i, x, P, W)
        x, sv_x = _fwd_x(layer, x, mem_n, P, W)
        saved.append((sv_m, sv_x, W))
    dx, loss, dgf = final_loss(x, tgt, P["final_norm"].reshape(1, D))
    G["final_norm"] = dgf[0]
    d_memn = None
    for layer in reversed(range(4)):
        i = layer // 2
        sv_m, sv_x, W = saved[layer]
        GW = {}
        dx_mid, d_memn = _bwd_x(layer, dx, sv_x, mem_n, d_memn, P, W, G, GW)
        d_hn, g, key, name = (_bwd_even if layer % 2 == 0 else _bwd_odd)(i, dx_mid, sv_m, P, W, G, GW)
        token = grads_done(layer, GW)
        if token is not None:
            g = g + token
        dx, dg = rms_bwd(sv_m["x"], d_hn, dx_mid, g, name)
        G[key][i] = dg[0]
    _, dgm = rms_bwd(mem, d_memn, d_memn, mem_g, "rms_mem_bwd")
    G["mem_norm"] = dgm[0]
    return loss, dx, G


ANY = pl.BlockSpec(memory_space=pl.ANY)


def _place():
    x, y, c = lax.axis_index("x"), lax.axis_index("y"), lax.axis_index("c")
    chips = [(1 - x, y), (x, 1 - y), (1 - x, 1 - y)]
    return x, y, c, 2 * x + y, (x, y, 1 - c), chips


def _remote(src, dst, send, recv, k, dev):
    return pltpu.make_async_remote_copy(src_ref=src, dst_ref=dst, send_sem=send.at[k], recv_sem=recv.at[k],
                                        device_id=dev, device_id_type=MESHID)


HBM = pl.BlockSpec(memory_space=pltpu.HBM)
SEM = pl.BlockSpec(memory_space=pltpu.SEMAPHORE)
EFFECT = pltpu.SideEffectType.DATAFLOW_SIDE_EFFECTING


def _hbm(t):
    return pltpu.with_memory_space_constraint(t, pltpu.HBM)


def _first_copies(shard, land, send, recv, arriving):
    x, y, c, jme, sib, chips = _place()
    out = []
    for k, chip in enumerate(chips):
        slot = 2 * chip[0] + chip[1] if arriving else jme
        out.append(_remote(shard.at[c], land.at[slot, c], send, recv, k, (*chip, c)))
    out.append(_remote(shard, land.at[jme], send, recv, 3, sib))
    return out


def allgather_first_start(shard, after, name):
    def body(s_ref, l_ref, *rest):
        send, recv, token = rest[len(after)], rest[len(after) + 1], rest[-1]
        for cp in _first_copies(s_ref, l_ref, send, recv, False):
            cp.start()
        token[...] = jnp.zeros_like(token)

    land = (4,) + shard.shape
    return pl.pallas_call(
        body, name=name,
        out_shape=(pltpu.SemaphoreType.DMA((4,)), pltpu.SemaphoreType.DMA((4,)), pltpu.HBM(shard.shape, shard.dtype),
                   pltpu.HBM(land, shard.dtype), jax.ShapeDtypeStruct((8, 128), F32)),
        in_specs=[HBM, HBM] + [ANY] * len(after),
        out_specs=(SEM, SEM, HBM, HBM, pl.BlockSpec(memory_space=pltpu.VMEM)),
        input_output_aliases={0: 2, 1: 3},
        compiler_params=pltpu.CompilerParams(has_side_effects=EFFECT),
    )(_hbm(shard), _hbm(lax.empty(land, shard.dtype)), *after)


def allgather_first_wait(send, recv, shard, land, after, name):
    def body(s_ref, l_ref, send_r, recv_r, *rest):
        for cp in _first_copies(s_ref, l_ref, send_r, recv_r, True):
            cp.wait_send()
            cp.wait_recv()

    res = pl.pallas_call(
        body, name=name, out_shape=(pltpu.HBM(shard.shape, shard.dtype), pltpu.HBM(land.shape, land.dtype)),
        in_specs=[HBM, HBM, SEM, SEM] + [ANY] * len(after), out_specs=(HBM, HBM),
        input_output_aliases={0: 0, 1: 1},
        compiler_params=pltpu.CompilerParams(has_side_effects=EFFECT),
    )(shard, land, send, recv, *after)
    return res[1]


def allgather_forward(land):
    def body(l_in, l_ref, send, recv):
        x, y, c, jme, sib, chips = _place()
        cps = []
        for k, chip in enumerate(chips):
            piece = l_ref.at[2 * chip[0] + chip[1], c]
            cp = _remote(piece, piece, send, recv, k, sib)
            cp.start()
            cps.append(cp)
        for k, chip in enumerate(chips):
            piece = l_ref.at[2 * chip[0] + chip[1], 1 - c]
            _remote(piece, piece, send, recv, k, sib).wait_recv()
        for cp in cps:
            cp.wait_send()

    return pl.pallas_call(
        body, name="allgather_forward", in_specs=[ANY], out_specs=ANY,
        out_shape=jax.ShapeDtypeStruct(land.shape, land.dtype), input_output_aliases={0: 0},
        scratch_shapes=[pltpu.SemaphoreType.DMA((3,)), pltpu.SemaphoreType.DMA((3,))],
    )(land)


def _gather_copies(ins, lands, send, recv):
    x, y, c, jme, sib, chips = _place()
    devs = [(*chip, c) for chip in chips] + [sib]
    return [_remote(ins[a], lands[a].at[jme], send, recv, a * 4 + k, dev)
            for a in range(len(ins)) for k, dev in enumerate(devs)]


def allgather_start(shards, after, name):
    n, na = len(shards), len(after)

    def body(*refs):
        ins, lands = refs[:n], refs[n:2 * n]
        send, recv = refs[2 * n + na], refs[2 * n + na + 1]
        token = refs[-1]
        for cp in _gather_copies(ins, lands, send, recv):
            cp.start()
        token[...] = jnp.zeros_like(token)

    res = pl.pallas_call(
        body, name=name,
        out_shape=(pltpu.SemaphoreType.DMA((4 * n,)), pltpu.SemaphoreType.DMA((4 * n,)),
                   *[pltpu.HBM(s.shape, s.dtype) for s in shards],
                   *[pltpu.HBM((4,) + s.shape, s.dtype) for s in shards],
                   jax.ShapeDtypeStruct((8, 128), F32)),
        in_specs=[HBM] * (2 * n) + [ANY] * na,
        out_specs=(SEM, SEM, *[HBM] * (2 * n), pl.BlockSpec(memory_space=pltpu.VMEM)),
        input_output_aliases={a: 2 + a for a in range(2 * n)},
        compiler_params=pltpu.CompilerParams(has_side_effects=EFFECT),
    )(*[_hbm(s) for s in shards], *[_hbm(lax.empty((4,) + s.shape, s.dtype)) for s in shards], *after)
    return res[0], res[1], list(res[2:2 + n]), list(res[2 + n:2 + 2 * n]), res[-1]


def allgather_wait(send, recv, shards, lands, after, name):
    n = len(shards)

    def body(*refs):
        ins, zones = refs[:n], refs[n:2 * n]
        send_r, recv_r = refs[2 * n], refs[2 * n + 1]
        x, y, c, jme, sib, chips = _place()
        slots = [2 * chip[0] + chip[1] for chip in chips] + [jme]
        for a in range(n):
            for k, slot in enumerate(slots):
                cp = _remote(ins[a], zones[a].at[slot], send_r, recv_r, a * 4 + k, sib)
                cp.wait_send()
                cp.wait_recv()

    res = pl.pallas_call(
        body, name=name,
        out_shape=tuple(pltpu.HBM(t.shape, t.dtype) for t in list(shards) + list(lands)),
        in_specs=[HBM] * (2 * n) + [SEM, SEM, ANY], out_specs=tuple([HBM] * (2 * n)),
        input_output_aliases={a: a for a in range(2 * n)},
        compiler_params=pltpu.CompilerParams(has_side_effects=EFFECT),
    )(*shards, *lands, send, recv, after)
    return list(res[n:])


def allgather_small(slab):
    def body(in_ref, out_ref, send, recv, lsem):
        x, y, c, jme, sib, chips = _place()
        loc = pltpu.make_async_copy(in_ref, out_ref.at[jme], lsem.at[0])
        loc.start()
        cps = [_remote(in_ref, out_ref.at[jme], send, recv, k, (*chip, c)) for k, chip in enumerate(chips)]
        for cp in cps:
            cp.start()
        for k, chip in enumerate(chips):
            piece = out_ref.at[2 * chip[0] + chip[1]]
            _remote(piece, piece, send, recv, k, (*chip, c)).wait_recv()
        for cp in cps:
            cp.wait_send()
        loc.wait()

    return pl.pallas_call(
        body, name="allgather_small", in_specs=[ANY], out_specs=ANY,
        out_shape=jax.ShapeDtypeStruct((4,) + slab.shape, slab.dtype),
        scratch_shapes=[pltpu.SemaphoreType.DMA((3,)), pltpu.SemaphoreType.DMA((3,)), pltpu.SemaphoreType.DMA((1,))],
    )(slab)


def allreduce_small(v):
    hr = v.shape[0] // 2

    def body(v_ref, o_ref, r0, r1, r2, send, recv):
        x, y, c, jme, sib, chips = _place()
        mine = pl.ds(pl.multiple_of(c * hr, 8), hr)
        other = pl.ds(pl.multiple_of((1 - c) * hr, 8), hr)
        cp = _remote(v_ref.at[other], r0, send, recv, 0, sib)
        cp.start()
        cp.wait()
        o_ref[mine, :] = v_ref[mine, :] + r0[...]
        for k, (buf, peer) in enumerate(((r1, (1 - x, y, c)), (r2, (x, 1 - y, c))), start=1):
            cp = _remote(o_ref.at[mine], buf, send, recv, k, peer)
            cp.start()
            cp.wait()
            o_ref[mine, :] = o_ref[mine, :] + buf[...]
        cp = _remote(o_ref.at[mine], o_ref.at[mine], send, recv, 3, sib)
        cp.start()
        cp.wait_send()
        _remote(o_ref.at[other], o_ref.at[other], send, recv, 3, sib).wait_recv()

    vm = pl.BlockSpec(memory_space=pltpu.VMEM)
    half = pltpu.VMEM((hr, v.shape[1]), v.dtype)
    return pl.pallas_call(
        body, name="allreduce_small", in_specs=[vm], out_specs=vm,
        out_shape=jax.ShapeDtypeStruct(v.shape, v.dtype),
        scratch_shapes=[half] * 3 + [pltpu.SemaphoreType.DMA((4,)), pltpu.SemaphoreType.DMA((4,))],
        compiler_params=pltpu.CompilerParams(vmem_limit_bytes=VMEM_LIMIT),
    )(v)


def _pair_copies(gs, lands, send, recv):
    x, y, c, jme, sib, chips = _place()
    return [_remote(gs[a].at[:, 1 - c], lands[a], send, recv, a, sib) for a in range(len(gs))]


def rs_pair_start(gs, name):
    n = len(gs)

    def body(*refs):
        ins, lands = refs[:n], refs[n:2 * n]
        send, recv = refs[2 * n], refs[2 * n + 1]
        token = refs[-1]
        for cp in _pair_copies(ins, lands, send, recv):
            cp.start()
        token[...] = jnp.zeros_like(token)

    shapes = [(4,) + g.shape[2:] for g in gs]
    res = pl.pallas_call(
        body, name=name,
        out_shape=(pltpu.SemaphoreType.DMA((n,)), pltpu.SemaphoreType.DMA((n,)),
                   *[pltpu.HBM(g.shape, g.dtype) for g in gs], *[pltpu.HBM(s, F32) for s in shapes],
                   jax.ShapeDtypeStruct((8, 128), F32)),
        in_specs=[HBM] * (2 * n), out_specs=(SEM, SEM, *[HBM] * (2 * n), pl.BlockSpec(memory_space=pltpu.VMEM)),
        input_output_aliases={a: 2 + a for a in range(2 * n)},
        compiler_params=pltpu.CompilerParams(has_side_effects=EFFECT),
    )(*[_hbm(g) for g in gs], *[_hbm(lax.empty(s, F32)) for s in shapes])
    return res[0], res[1], list(res[2:2 + n]), list(res[2 + n:2 + 2 * n]), res[-1]


def rs_pair_wait(send, recv, gs, lands, after, name):
    n = len(gs)

    def body(*refs):
        ins, zones = refs[:n], refs[n:2 * n]
        for cp in _pair_copies(ins, zones, refs[2 * n], refs[2 * n + 1]):
            cp.wait_send()
            cp.wait_recv()

    res = pl.pallas_call(
        body, name=name,
        out_shape=tuple(pltpu.HBM(t.shape, t.dtype) for t in list(gs) + list(lands)),
        in_specs=[HBM] * (2 * n) + [SEM, SEM, ANY], out_specs=tuple([HBM] * (2 * n)),
        input_output_aliases={a: a for a in range(2 * n)},
        compiler_params=pltpu.CompilerParams(has_side_effects=EFFECT),
    )(*gs, *lands, send, recv, after)
    return list(res[:n]), list(res[n:])


SUM_ROWS = 256


def rs_pair_sum(g4s, gots, cidx):
    n = len(g4s)
    tiles = [(min(g.shape[2], SUM_ROWS), g.shape[3]) for g in g4s]
    nts = [g.shape[2] // tr for g, (tr, _) in zip(g4s, tiles)]

    def at(a, s):
        s = jnp.minimum(s, 4 * nts[a] - 1)
        return s // nts[a], s % nts[a]

    def body(c_ref, *refs):
        for a in range(n):
            refs[2 * n + a][...] = (refs[a][...] + refs[n + a][...]).astype(BF16)

    in_specs = [pl.BlockSpec((None, None) + tiles[a], lambda s, cr, a=a: (at(a, s)[0], cr[0], at(a, s)[1], 0))
                for a in range(n)]
    in_specs += [pl.BlockSpec((None,) + tiles[a], lambda s, cr, a=a: (*at(a, s), 0)) for a in range(n)]
    return pl.pallas_call(
        body, name="rs_pair_sum",
        grid_spec=pltpu.PrefetchScalarGridSpec(
            num_scalar_prefetch=1, grid=(4 * max(nts),), in_specs=in_specs,
            out_specs=[pl.BlockSpec((None,) + tiles[a], lambda s, cr, a=a: (*at(a, s), 0)) for a in range(n)]),
        out_shape=[jax.ShapeDtypeStruct((4,) + g.shape[2:], BF16) for g in g4s],
        compiler_params=_cparams(("arbitrary",)),
    )(cidx, *g4s, *gots)


def _chip_copies(ps, lands, send, recv):
    x, y, c, jme, sib, chips = _place()
    return [_remote(ps[a].at[2 * chip[0] + chip[1]], lands[a].at[jme], send, recv, a * 3 + k, (*chip, c))
            for a in range(len(ps)) for k, chip in enumerate(chips)]


def rs_chip_start(ps, name):
    n = len(ps)

    def body(*refs):
        ins, lands = refs[:n], refs[n:2 * n]
        send, recv = refs[2 * n], refs[2 * n + 1]
        token = refs[-1]
        for cp in _chip_copies(ins, lands, send, recv):
            cp.start()
        token[...] = jnp.zeros_like(token)

    res = pl.pallas_call(
        body, name=name,
        out_shape=(pltpu.SemaphoreType.DMA((3 * n,)), pltpu.SemaphoreType.DMA((3 * n,)),
                   *[pltpu.HBM(p.shape, p.dtype) for p in ps], *[pltpu.HBM(p.shape, p.dtype) for p in ps],
                   jax.ShapeDtypeStruct((8, 128), F32)),
        in_specs=[HBM] * (2 * n), out_specs=(SEM, SEM, *[HBM] * (2 * n), pl.BlockSpec(memory_space=pltpu.VMEM)),
        input_output_aliases={a: 2 + a for a in range(2 * n)},
        compiler_params=pltpu.CompilerParams(has_side_effects=EFFECT),
    )(*[_hbm(p) for p in ps], *[_hbm(lax.empty(p.shape, p.dtype)) for p in ps])
    return res[0], res[1], list(res[2:2 + n]), list(res[2 + n:2 + 2 * n]), res[-1]


def rs_chip_wait(send, recv, ps, lands, after, name):
    n = len(ps)

    def body(*refs):
        ins, zones = refs[:n], refs[n:2 * n]
        send_r, recv_r = refs[2 * n], refs[2 * n + 1]
        x, y, c, jme, sib, chips = _place()
        for a in range(n):
            for k, chip in enumerate(chips):
                jt = 2 * chip[0] + chip[1]
                cp = _remote(ins[a].at[jt], zones[a].at[jt], send_r, recv_r, a * 3 + k, (*chip, c))
                cp.wait_send()
                cp.wait_recv()

    res = pl.pallas_call(
        body, name=name,
        out_shape=tuple(pltpu.HBM(p.shape, p.dtype) for p in list(ps) + list(lands)),
        in_specs=[HBM] * (2 * n) + [SEM, SEM] + [ANY] * len(after), out_specs=tuple([HBM] * (2 * n)),
        input_output_aliases={a: a for a in range(2 * n)},
        compiler_params=pltpu.CompilerParams(has_side_effects=EFFECT),
    )(*ps, *lands, send, recv, *after)
    return list(res[:n]), list(res[n:])


def rs_chip_sum(qs, ps, ls, accs, layers, jc):
    n = len(qs)
    tiles = [(min(q.shape[1], SUM_ROWS), q.shape[2]) for q in qs]
    nts = [q.shape[1] // tr for q, (tr, _) in zip(qs, tiles)]

    def at(a, s):
        return jnp.minimum(s, nts[a] - 1)

    def body(jc_ref, *refs):
        jme = jc_ref[0]
        for a in range(n):
            q_ref, p_ref, o_ref = refs[a], refs[n + a], refs[len(refs) - n + a]
            own = p_ref[...].astype(F32)
            v = [jnp.where(jme == j, own, q_ref[j].astype(F32)) for j in range(4)]
            o_ref[...] = ((v[0] + v[1]) + v[2]) + v[3]

    in_specs = [pl.BlockSpec((4,) + tiles[a], lambda s, jr, a=a: (0, at(a, s), 0)) for a in range(n)]
    in_specs += [pl.BlockSpec((None,) + tiles[a], lambda s, jr, a=a: (jr[0], at(a, s), 0)) for a in range(n)]
    args, aliases = [jc, *qs, *ps], {}
    for a in range(n):
        if accs[a] is not None:
            aliases[len(args)] = a
            in_specs.append(ANY)
            args.append(accs[a])
    return pl.pallas_call(
        body, name="rs_chip_sum",
        grid_spec=pltpu.PrefetchScalarGridSpec(
            num_scalar_prefetch=1, grid=(max(nts),), in_specs=in_specs,
            out_specs=[pl.BlockSpec((None, None) + tiles[a], lambda s, jr, a=a: (ls[a], jr[1], at(a, s), 0))
                       for a in range(n)]),
        out_shape=[jax.ShapeDtypeStruct((layers[a], 2) + qs[a].shape[1:], F32) for a in range(n)],
        input_output_aliases=aliases,
        compiler_params=_cparams(("arbitrary",)),
    )(*args)


def rs_pair_gather(rs):
    n = len(rs)

    def body(*refs):
        outs = refs[n:2 * n]
        send, recv = refs[2 * n:]
        x, y, c, jme, sib, chips = _place()
        cps = [_remote(outs[a].at[:, c], outs[a].at[:, c], send, recv, a, sib) for a in range(n)]
        for cp in cps:
            cp.start()
        for a in range(n):
            slot = outs[a].at[:, 1 - c]
            _remote(slot, slot, send, recv, a, sib).wait_recv()
        for cp in cps:
            cp.wait_send()

    return pl.pallas_call(
        body, name="rs_pair_gather", in_specs=[ANY] * n, out_specs=[ANY] * n,
        out_shape=[jax.ShapeDtypeStruct(r.shape, r.dtype) for r in rs],
        input_output_aliases={a: a for a in range(n)},
        scratch_shapes=[pltpu.SemaphoreType.DMA((n,)), pltpu.SemaphoreType.DMA((n,))],
    )(*rs)


def _adamw_math(w, g, m, v):
    m = B1 * m + (1.0 - B1) * g
    v = B2 * v + (1.0 - B2) * (g * g)
    m_hat = m / (1.0 - B1 ** STEP)
    v_hat = v / (1.0 - B2 ** STEP)
    return -LR * (m_hat / (jnp.sqrt(v_hat) + AEPS) + WD * w), m, v


ADAMW_TILE = 512 * 1024


def adamw(w, g, m, v, name, with_grad=False):
    rows, cols = w.shape
    tr = next((t for t in (1024, 512, 256) if rows % t == 0 and t * cols <= ADAMW_TILE), rows)
    fn =(lambda wv, gv, mv, vv: (gv,) + _adamw_math(wv, gv, mv, vv)) if with_grad else _adamw_math
    return rw(fn, [(a, 0, cols) for a in (w, g, m, v)], [(cols, F32)] * (4 if with_grad else 3), name, rows, tr=tr)


def adamw_small(ws, gs, ms, vs):
    n = len(ws)

    def body(*refs):
        for a in range(n):
            res = _adamw_math(*[refs[k * n + a][...] for k in range(4)])
            for k in range(3):
                refs[(4 + k) * n + a][...] = res[k]

    res = pl.pallas_call(
        body, name="adamw_small", in_specs=[VM] * (4 * n), out_specs=[VM] * (3 * n),
        out_shape=[jax.ShapeDtypeStruct(w.shape, F32) for _ in range(3) for w in ws],
        compiler_params=pltpu.CompilerParams(vmem_limit_bytes=VMEM_LIMIT),
    )(*ws, *gs, *ms, *vs)
    return [(res[a], res[n + a], res[2 * n + a]) for a in range(n)]


WEIGHTS = ["norm_ab", "w_in_ab", "pool_w", "pool_scale", "w_out_ab", "norm_cd", "w_in_cd", "sgu_ln_g", "sgu_ln_b",
           "sgu_w", "sgu_b", "s5_a_re", "s5_a_im", "s5_log_dt", "s5_b_re", "s5_b_im", "s5_c_re", "s5_c_im", "s5_d",
           "glu_w1", "glu_w2", "w_out_cd", "norm_x", "w_xq", "w_xkv", "w_xo", "mem_norm", "final_norm"]
INPUTS = ["x", "mem"] + WEIGHTS + ["loss_target"] + ["m_" + n for n in WEIGHTS] + ["v_" + n for n in WEIGHTS]
BIG = ["w_in_ab", "w_out_ab", "w_in_cd", "w_out_cd", "w_xq", "w_xkv", "w_xo", "glu_w1", "glu_w2", "pool_w"]
COL_SHARDED = ("w_in_ab", "w_in_cd", "w_xkv")
SMALL = [n for n in WEIGHTS if n not in BIG]
SMALL_SHARDED = {"norm_cd": 256, "sgu_ln_g": 256, "sgu_ln_b": 256, "s5_d": 128}
PACK = 256 * 128


def _pack(arrs):
    flat = jnp.concatenate([a.reshape(-1) for a in arrs])
    pad = (-flat.shape[0]) % PACK
    return jnp.concatenate([flat, jnp.zeros((pad,), flat.dtype)]).reshape(-1, 128)


def _unpack(packed, shapes):
    flat, out, off = packed.reshape(-1), [], 0
    for s in shapes:
        n = 1
        for d in s:
            n *= d
        out.append(flat[off:off + n].reshape(s))
        off += n
    return out


LAYER_KEYS = (("w_in", "w_out", "pool_w", "w_xq", "w_xkv", "w_xo"),
              ("w_in", "w_out", "glu_w1", "glu_w2", "w_xq", "w_xkv", "w_xo"))


def _weight_of(key, layer):
    if key in ("w_xq", "w_xkv", "w_xo"):
        return key, layer, 4
    kind = "ab" if layer % 2 == 0 else "cd"
    return {"w_in": "w_in_" + kind, "w_out": "w_out_" + kind}.get(key, key), layer // 2, 2


def kernel(*args):
    a = dict(zip(INPUTS, args))
    x_i, y_i, c_i = lax.axis_index("x"), lax.axis_index("y"), lax.axis_index("c")
    j = 2 * x_i + y_i

    slab = jnp.concatenate([a["norm_cd"], a["sgu_ln_g"], a["sgu_ln_b"],
                            jnp.pad(a["s5_d"], ((0, 0), (0, 128)))], axis=0)
    gslab = allgather_small(slab)
    P = {n: a[n] for n in SMALL}
    for k, n in enumerate(("norm_cd", "sgu_ln_g", "sgu_ln_b", "s5_d")):
        wd = SMALL_SHARDED[n]
        P[n] = gslab[:, 2 * k:2 * k + 2, :wd].transpose(1, 0, 2).reshape(2, 4 * wd)

    def shards_of(layer):
        keys = sorted(k for k in LAYER_KEYS[layer % 2])
        out = []
        for k in keys:
            n, l, _ = _weight_of(k, layer)
            out.append(a[n][l].reshape(-1, a[n].shape[-1]).astype(BF16))
        return keys, out

    keys0, sh0 = shards_of(0)
    first = keys0.index("w_in")
    f_send, f_recv, f_shard, f_land, token = allgather_first_start(
        sh0[first].reshape(2, sh0[first].shape[0] // 2, sh0[first].shape[1]), [gslab], "allgather_start_0in")
    started = {}
    for layer in (0, 1, 2, 3):
        keys, sh = (keys0, sh0) if layer == 0 else shards_of(layer)
        mix = [(k, s) for k, s in zip(keys, sh) if k != "w_in" and not k.startswith("w_x")]
        xat = [(k, s) for k, s in zip(keys, sh) if k.startswith("w_x")]
        parts = [("in", ["w_in"], [sh[keys.index("w_in")]])] * (layer > 0)
        parts += [("", *map(list, zip(*mix))), ("x", *map(list, zip(*xat)))]
        for tag, pk, ps in parts:
            send, recv, ps, lands, token = allgather_start(ps, [token, gslab], "allgather_start_%d%s" % (layer, tag))
            started[(layer, tag)] = (pk, send, recv, ps, lands)
    g_in = allgather_forward(allgather_first_wait(f_send, f_recv, f_shard, f_land, [token], "allgather_wait_0in"))
    w_in0 = g_in.reshape(4, -1, g_in.shape[-1])

    cidx = jnp.reshape(c_i, (1,)).astype(jnp.int32)
    jc = jnp.stack([j, c_i]).astype(jnp.int32)

    def views(g):
        W = {}
        for k, v in g.items():
            if k in ("w_in", "w_xkv"):
                W[k] = mcs(v)
            elif k == "pool_w":
                W[k] = v.reshape(4, 4, 64, 256).transpose(1, 0, 2, 3).reshape(4, 256, 256)
            elif k not in ("glu_w1", "glu_w2"):
                W[k] = m2(v.reshape(-1, v.shape[-1]))
        if "glu_w1" in g:
            W["w12"] = jnp.concatenate([g["glu_w1"].reshape(512, 512), g["glu_w2"].reshape(512, 512)], axis=1)
        return W

    def arrived(layer, tag, after):
        keys, send, recv, sh, lands = started[(layer, tag)]
        return views(dict(zip(keys, allgather_wait(send, recv, sh, lands, after, "allgather_wait_%d%s" % (layer, tag)))))

    def weights_of(layer, x_in):
        W = views({"w_in": w_in0}) if layer == 0 else arrived(layer, "in", x_in)
        W["more"] = lambda after: arrived(layer, "", after)
        W["more_x"] = lambda after: arrived(layer, "x", after)
        return W

    halves, pending = {}, {}

    def finish_pair(layer, after):
        keys, send, recv, flat, lands = halves.pop(layer)
        flat, got = rs_pair_wait(send, recv, flat, lands, after, "rs_pair_wait_%d" % layer)
        pair = rs_pair_sum(flat, got, cidx)
        send, recv, pair, lands, token = rs_chip_start(pair, "rs_chip_start_%d" % layer)
        pending[layer] = (keys, send, recv, pair, lands)
        return token

    def grads_done(layer, GW):
        keys = sorted(GW)
        flat = [GW[k].reshape(4, 2, GW[k].shape[1] // 2, GW[k].shape[2]) for k in keys]
        send, recv, flat, lands, token = rs_pair_start(flat, "rs_pair_start_%d" % layer)
        halves[layer] = (keys, send, recv, flat, lands)
        if layer + 1 in halves:
            token = token + finish_pair(layer + 1, token)
        return token[0:1, 0:1]

    loss, dx, G = local_step(a["x"][0], a["mem"][0], a["loss_target"][0], P, weights_of, grads_done)
    loss = lax.psum(loss[0, 0], ("x", "y", "c"))
    finish_pair(0, dx)
    outs = {}

    def update_big(names, red):
        for n, g in zip(names, rs_pair_gather([red[n] for n in names])):
            shp = a[n].shape
            g2 = g.reshape(-1, shp[-1])
            upd = adamw(a[n].reshape(g2.shape), g2, a["m_" + n].reshape(g2.shape), a["v_" + n].reshape(g2.shape),
                        "adamw_" + n, with_grad=True)
            outs[n] = tuple(t.reshape(shp) for t in upd)

    def reduce_layer(layer, red, after):
        keys, send, recv, pair, lands = pending[layer]
        pair, lands = rs_chip_wait(send, recv, pair, lands, after, "rs_chip_wait_%d" % layer)
        which = [_weight_of(k, layer) for k in keys]
        sums = rs_chip_sum(lands, pair, [l for _, l, _ in which], [red.get(n) for n, _, _ in which],
                           [layers for _, _, layers in which], jc)
        red.update(zip([n for n, _, _ in which], sums))

    red = {}
    for layer in (3, 2, 1):
        reduce_layer(layer, red, [dx])
    odd_only = [n for n in BIG if n.endswith("_cd") or n.startswith("glu")]
    update_big(odd_only, red)

    gfull = [jnp.stack(G[n]) if isinstance(G[n], list) else G[n] for n in SMALL]
    shapes = [g.shape for g in gfull]
    gsum = _unpack(allreduce_small(_pack(gfull)), shapes)
    gloc = []
    for n, g in zip(SMALL, gsum):
        if n in SMALL_SHARDED:
            g = lax.dynamic_slice_in_dim(g, j * SMALL_SHARDED[n], SMALL_SHARDED[n], axis=1)
        gloc.append(g)
    two = [(-1, a[n].shape[-1]) if a[n].ndim > 1 else (1, a[n].shape[0]) for n in SMALL]
    upds = adamw_small(*[[t.reshape(s) for t, s in zip(ts, two)]
                         for ts in ([a[n] for n in SMALL], gloc, [a["m_" + n] for n in SMALL],
                                    [a["v_" + n] for n in SMALL])])
    for n, g, upd in zip(SMALL, gloc, upds):
        outs[n] = (g,) + tuple(t.reshape(a[n].shape) for t in upd)

    behind = [outs[n][1] for n in odd_only + SMALL[-1:]] + [red[n] for n in BIG if n not in odd_only]
    reduce_layer(0, red, behind)
    update_big([n for n in BIG if n not in odd_only], red)

    res = [loss, dx[None]]
    for part in range(4):
        res += [outs[n][part] for n in WEIGHTS]
    return tuple(res)
```

```python
import math

import jax
import jax.numpy as jnp
from jax import lax
from jax.experimental import pallas as pl
from jax.experimental.pallas import tpu as pltpu

F32, BF16 = jnp.float32, jnp.bfloat16
S, D = 2048, 1024
MEM = 256
EPS = 1e-6
NEG = -1e30
QB = 128
PATTERNS = (1, 4, 16)
NG, NP, NH = 32, 64, 16
NS = NG * NP
LR, B1, B2, AEPS, WD, STEP = 0.001, 0.9, 0.999, 1e-08, 0.01, 10
MESHID = pl.DeviceIdType.MESH
VMEM_LIMIT = 56 * 1024 * 1024


def _cparams(sem):
    return pltpu.CompilerParams(dimension_semantics=sem, vmem_limit_bytes=VMEM_LIMIT)


def _sig(x):
    return 1.0 / (1.0 + jnp.exp(-x))


def _dot(a, b, dims):
    return lax.dot_general(a, b, (dims, ((), ())), preferred_element_type=F32)


def _nn(a, b):
    return _dot(a, b, ((1,), (0,)))


def _nt(a, b):
    return _dot(a, b, ((1,), (1,)))


def _tn(a, b):
    return _dot(a, b, ((0,), (0,)))


_DIMS = {"nn": ((1,), (0,)), "nt": ((1,), (1,)), "tn": ((0,), (0,))}


def _tile(dim, cc=None, cap=1024):
    for t in (2048, 1536, 1024, 768, 512, 384, 256, 128):
        if t <= cap and dim % t == 0 and (cc is None or cc % t == 0):
            return t
    return dim


MM_VMEM = 36 * 1024 * 1024


def _mm_tiles(m, n, k, ccm, ccn, cck, a_bytes, b_bytes, o_bytes):
    caps = [1024, 1024, 2048]
    while True:
        tm, tn, tk = _tile(m, ccm, caps[0]), _tile(n, ccn, caps[1]), _tile(k, cck, caps[2])
        need = 2 * (tm * tk * a_bytes + tk * tn * b_bytes + tm * tn * o_bytes) + (tm * tn * 4 if tk < k else 0)
        if need <= MM_VMEM:
            return tm, tn, tk
        if tk > 1024:
            caps[2] = tk // 2
        elif tn >= tm:
            caps[1] = tn // 2
        else:
            caps[0] = tm // 2


def m2(arr, col_off=0, ncols=None):
    rows, cols = arr.shape
    ncols = cols - col_off if ncols is None else ncols

    def spec(tr, tc, rc):
        assert col_off % tc == 0
        return pl.BlockSpec((tr, tc), lambda *g: (rc(*g)[0], rc(*g)[1] + col_off // tc))
    return (arr, rows, ncols, spec, None if col_off == 0 else col_off)


def mcs(arr):
    cs = arr.shape[2]

    def spec(tr, tc, rc):
        n = cs // tc
        return pl.BlockSpec((None, tr, tc), lambda *g: (rc(*g)[1] // n, rc(*g)[0], rc(*g)[1] % n))
    return (arr, arr.shape[1], 4 * cs, spec, cs)


def out2(rows, cols):
    def spec(tr, tc, rc):
        return pl.BlockSpec((tr, tc), lambda *g: tuple(rc(*g)))
    return ((rows, cols), spec, None)


def outcs(rows, cs):
    def spec(tr, tc, rc):
        n = cs // tc
        return pl.BlockSpec((None, tr, tc), lambda *g: (rc(*g)[1] // n, rc(*g)[0], rc(*g)[1] % n))
    return ((4, rows, cs), spec, cs)


def _both(a, b):
    if a is None:
        return b
    if b is None:
        return a
    return math.gcd(a, b)


def mm(a, b, mode, name, add=None, out=None, out_dtype=F32):
    a_arr, a_r, a_c, a_spec, a_cc = a
    b_arr, b_r, b_c, b_spec, b_cc = b
    if mode == "nn":
        m, k, n = a_r, a_c, b_c
        assert b_r == k
        ccm, cck, ccn = None, a_cc, b_cc
    elif mode == "nt":
        m, k, n = a_r, a_c, b_r
        assert b_c == k
        ccm, cck, ccn = None, _both(a_cc, b_cc), None
    else:
        m, k, n = a_c, a_r, b_c
        assert b_r == k
        ccm, cck, ccn = a_cc, None, b_cc
    out = out2(m, n) if out is None else out
    o_shape, o_spec, o_cc = out
    ccn = _both(ccn, o_cc)
    if add is not None:
        ccn = _both(ccn, add[4])
    o_bytes = jnp.dtype(out_dtype).itemsize + (0 if add is None else add[0].dtype.itemsize)
    tm, tn, tk = _mm_tiles(m, n, k, ccm, ccn, cck, a_arr.dtype.itemsize, b_arr.dtype.itemsize, o_bytes)
    nk = k // tk
    if mode == "nn":
        in_specs = [a_spec(tm, tk, lambda i, j, kk: (i, kk)), b_spec(tk, tn, lambda i, j, kk: (kk, j))]
    elif mode == "nt":
        in_specs = [a_spec(tm, tk, lambda i, j, kk: (i, kk)), b_spec(tn, tk, lambda i, j, kk: (j, kk))]
    else:
        in_specs = [a_spec(tk, tm, lambda i, j, kk: (kk, i)), b_spec(tk, tn, lambda i, j, kk: (kk, j))]
    args = [a_arr, b_arr]
    if add is not None:
        in_specs.append(add[3](tm, tn, lambda i, j, kk: (i, j)))
        args.append(add[0])
    return _mm_call(args, in_specs, o_spec(tm, tn, lambda i, j, kk: (i, j)), jax.ShapeDtypeStruct(o_shape, out_dtype),
                    mode, (m // tm, n // tn, nk), (tm, tn), add is not None, name)


def _mm_call(args, in_specs, out_spec, out_shape, mode, grid, tile, has_add, name):
    dims = _DIMS[mode]
    nk = grid[2]
    tm, tn = tile

    def body(*refs):
        a_ref, b_ref = refs[0], refs[1]
        add_ref = refs[2] if has_add else None
        prod = _dot(a_ref[...].astype(BF16), b_ref[...].astype(BF16), dims)
        if nk == 1:
            o_ref = refs[-1]
            if has_add:
                prod = prod + add_ref[...].astype(F32)
            o_ref[...] = prod.astype(o_ref.dtype)
            return
        o_ref, acc = refs[-2], refs[-1]
        kk = pl.program_id(2)

        @pl.when(kk == 0)
        def _():
            acc[...] = prod

        @pl.when(kk > 0)
        def _():
            acc[...] += prod

        @pl.when(kk == nk - 1)
        def _():
            r = acc[...]
            if has_add:
                r = r + add_ref[...].astype(F32)
            o_ref[...] = r.astype(o_ref.dtype)

    return pl.pallas_call(
        body, name=name, grid=grid, in_specs=in_specs, out_specs=out_spec, out_shape=out_shape,
        scratch_shapes=[pltpu.VMEM((tm, tn), F32)] if nk > 1 else [],
        compiler_params=_cparams(("parallel", "parallel", "arbitrary")),
    )(*args)


def mm_band(a, b, mode, name, grid, blocks, maps, out_shape, add=None, out_dtype=F32):
    in_specs = [pl.BlockSpec(blocks[0], maps[0]), pl.BlockSpec(blocks[1], maps[1])]
    args = [a, b]
    if add is not None:
        in_specs.append(pl.BlockSpec(blocks[2], maps[2]))
        args.append(add)
    return _mm_call(args, in_specs, pl.BlockSpec(blocks[2], maps[2]), jax.ShapeDtypeStruct(out_shape, out_dtype),
                    mode, grid, blocks[2], add is not None, name)


def rw(fn, ins, outs, name, rows, tr=None, consts=(), accs=()):
    tr = min(rows, 1024) if tr is None else tr
    n_in, n_c, n_o, n_a = len(ins), len(consts), len(outs), len(accs)
    in_specs = []
    for arr, off, width in ins:
        assert off % width == 0
        in_specs.append(pl.BlockSpec((tr, width), lambda i, o=off // width: (i, o)))
    for c in consts:
        in_specs.append(pl.BlockSpec(c.shape, lambda i: (0, 0)))
    out_specs = [pl.BlockSpec((tr, w), lambda i: (i, 0)) for w, _ in outs]
    out_specs += [pl.BlockSpec(s, lambda i: (0, 0)) for s in accs]
    out_shape = [jax.ShapeDtypeStruct((rows, w), dt) for w, dt in outs]
    out_shape += [jax.ShapeDtypeStruct(s, F32) for s in accs]

    def body(*refs):
        vals = [r[...] for r in refs[:n_in + n_c]]
        o_refs = refs[n_in + n_c:n_in + n_c + n_o]
        a_refs = refs[n_in + n_c + n_o:]
        res = fn(*vals)
        for r, v in zip(o_refs, res[:n_o]):
            r[...] = v.astype(r.dtype)
        if n_a:
            @pl.when(pl.program_id(0) == 0)
            def _():
                for r in a_refs:
                    r[...] = jnp.zeros_like(r)
            for r, v in zip(a_refs, res[n_o:]):
                r[...] += v

    res = pl.pallas_call(
        body, name=name, grid=(rows // tr,), in_specs=in_specs, out_specs=out_specs,
        out_shape=out_shape,
        compiler_params=_cparams(("arbitrary",) if n_a else ("parallel",)),
    )(*[a for a, _, _ in ins], *consts)
    return res


def _rstd(x):
    return lax.rsqrt(jnp.mean(x * x, axis=-1, keepdims=True) + EPS)


def rms_fwd(x, g, name):
    def fn(xv, gv):
        xv = xv.astype(F32)
        return (xv * _rstd(xv) * gv,)
    return rw(fn, [(x, 0, D)], [(D, BF16)], name, x.shape[0], consts=[g])[0]


def _rms_bwd_math(xv, dy, gv):
    r = _rstd(xv)
    dyg = dy * gv
    dx = r * dyg - xv * (r * r * r / D) * jnp.sum(dyg * xv, axis=-1, keepdims=True)
    dg = jnp.sum(dy * xv * r, axis=0, keepdims=True)
    return dx, dg


def rms_bwd(x, dy, dres, g, name):
    def fn(xv, dyv, drv, gv):
        dx, dg = _rms_bwd_math(xv, dyv, gv)
        return dx + drv, dg
    return rw(fn, [(x, 0, D), (dy, 0, D), (dres, 0, D)], [(D, F32)], name, x.shape[0],
              consts=[g], accs=[(1, D)])


def final_loss(x, tgt, g):
    def fn(xv, tv, gv):
        e = xv * _rstd(xv) * gv - tv
        loss = 0.5 * jnp.sum(jnp.sum(e * e, axis=-1, keepdims=True), axis=0, keepdims=True) / D
        dx, dg = _rms_bwd_math(xv, e / D, gv)
        return dx, loss, dg
    return rw(fn, [(x, 0, D), (tgt, 0, D)], [(D, F32)], "final_loss", S, consts=[g],
              accs=[(1, 1), (1, D)])


def _attn_bias(bias_ref):
    ii = lax.broadcasted_iota(jnp.int32, (2 * QB, 2 * QB), 0) % QB
    jj = lax.broadcasted_iota(jnp.int32, (2 * QB, 2 * QB), 1)
    dist = ii + QB - jj
    band = (dist >= 0) & (dist <= QB)
    bias_ref[1] = jnp.where(band, 0.0, NEG)
    bias_ref[0] = jnp.where(band & (jj >= QB), 0.0, NEG)


def _two_heads(x, m0):
    return jnp.concatenate([jnp.where(m0, x, 0.0), jnp.where(m0, 0.0, x)], axis=0)


def _per_head(col, m0):
    return jnp.where(m0, col[:QB], col[QB:])


def _attn_rows(idx, d):
    if d == 1:
        b = idx
        cur = pl.ds(pl.multiple_of(b * QB, QB), QB)
        prev = pl.ds(pl.multiple_of(jnp.maximum(b - 1, 0) * QB, QB), QB)
    else:
        r, b = lax.rem(idx, d), lax.div(idx, d)
        cur = pl.ds(r + b * (QB * d), QB, stride=d)
        prev = pl.ds(r + jnp.maximum(b - 1, 0) * (QB * d), QB, stride=d)
    return cur, prev, b


NBLK = S // QB
GROUP = 16
GROUP_FWD = 16


def _colblk(off):
    return pl.BlockSpec((S, 128), lambda hp: (0, off * 8 + hp))


def attn_fwd(z):
    def body(q_ref, k_ref, v_ref, g_ref, o_ref, l_ref, a_ref, os, ls, bias):
        _attn_bias(bias)
        m0 = lax.broadcasted_iota(jnp.int32, (1, 128), 1) < 64
        for pi, d in enumerate(PATTERNS):
            lone = S // d == QB

            def load(idx, d=d, lone=lone):
                cur, prev, b = _attn_rows(idx, d)
                if lone:
                    return cur, (q_ref[cur, :], None, k_ref[cur, :], None, v_ref[cur, :], bias[1, :, QB:])
                return cur, (q_ref[cur, :], k_ref[prev, :], k_ref[cur, :], v_ref[prev, :], v_ref[cur, :],
                             bias[jnp.minimum(b, 1)])

            def block(q, kp, kc, vp, vc, bs):
                qq = _two_heads(q * 0.125, m0).astype(BF16)
                k = (kc if kp is None else jnp.concatenate([kp, kc], axis=0)).astype(BF16)
                s = _nt(qq, k) + bs
                mx = jnp.max(s, axis=-1, keepdims=True)
                p = jnp.exp(s - mx)
                den = jnp.sum(p, axis=-1, keepdims=True)
                pb = p.astype(BF16)
                vv = _two_heads(vc if vp is None else jnp.concatenate([vp, vc], axis=0), m0).astype(BF16)
                o = _nn(jnp.concatenate([pb[:QB], pb[QB:]], axis=1), vv)
                return o * _per_head(1.0 / den, m0), _per_head(mx + jnp.log(den), m0)

            def step(i, carry, pi=pi):
                loaded = [load(i * GROUP_FWD + u) for u in range(GROUP_FWD)]
                done = [block(*vals) for _, vals in loaded]
                for (cur, _), (o, l) in zip(loaded, done):
                    os[pi, cur, :] = o
                    ls[pi, cur, :] = l
                return carry
            lax.fori_loop(0, NBLK // GROUP_FWD, step, 0)
        l1, l2, l3 = ls[0], ls[1], ls[2]
        mx = jnp.maximum(jnp.maximum(l1, l2), l3)
        e1, e2, e3 = jnp.exp(l1 - mx), jnp.exp(l2 - mx), jnp.exp(l3 - mx)
        tot = e1 + e2 + e3
        o = (os[0] * e1 + os[1] * e2 + os[2] * e3) / tot
        ga = g_ref[...]
        o_ref[...] = o
        l_ref[...] = mx + jnp.log(tot)
        a_ref[...] = (o * (ga * _sig(ga))).astype(a_ref.dtype)

    out = pl.BlockSpec((S, 128), lambda hp: (0, hp))
    return pl.pallas_call(
        body, name="attn_fwd", grid=(8,),
        in_specs=[_colblk(0), _colblk(1), _colblk(2), _colblk(3)], out_specs=[out] * 3,
        out_shape=[jax.ShapeDtypeStruct((S, D), F32), jax.ShapeDtypeStruct((S, D), F32),
                   jax.ShapeDtypeStruct((S, 2 * D), BF16)],
        scratch_shapes=[pltpu.VMEM((3, S, 128), F32), pltpu.VMEM((3, S, 128), F32),
                        pltpu.VMEM((2, 2 * QB, 2 * QB), F32)],
        compiler_params=_cparams(("parallel",)),
    )(z, z, z, z)


def attn_bwd(z, d_cat, o, lse):
    def body(q_ref, k_ref, v_ref, g_ref, da_ref, o_ref, l_ref, dq_ref, dk_ref, dv_ref, dg_ref, do_s, pr_s, bias):
        _attn_bias(bias)
        m0 = lax.broadcasted_iota(jnp.int32, (1, 128), 1) < 64
        ga = g_ref[...]
        sg = _sig(ga)
        da = da_ref[...]
        ov = o_ref[...]
        do = da * (ga * sg)
        dg_ref[...] = da * ov * (sg * (1.0 + ga * (1.0 - sg)))
        do_s[...] = do
        pr_s[...] = do * ov
        dq_ref[...] = jnp.zeros_like(dq_ref)
        dk_ref[...] = jnp.zeros_like(dk_ref)
        dv_ref[...] = jnp.zeros_like(dv_ref)
        for d in PATTERNS:
            lone = S // d == QB

            def load(idx, d=d, lone=lone):
                cur, prev, b = _attn_rows(idx, d)
                if lone:
                    return (cur, None), (q_ref[cur, :], None, k_ref[cur, :], None, v_ref[cur, :],
                                         do_s[cur, :], pr_s[cur, :], l_ref[cur, :], bias[1, :, QB:])
                return (cur, prev), (q_ref[cur, :], k_ref[prev, :], k_ref[cur, :], v_ref[prev, :], v_ref[cur, :],
                                     do_s[cur, :], pr_s[cur, :], l_ref[cur, :], bias[jnp.minimum(b, 1)])

            def block(q, kp, kc, vp, vc, dof, prod, lp, bs):
                qq = _two_heads(q * 0.125, m0).astype(BF16)
                kf = kc if kp is None else jnp.concatenate([kp, kc], axis=0)
                k = kf.astype(BF16)
                v = (vc if vp is None else jnp.concatenate([vp, vc], axis=0)).astype(BF16)
                dd = _two_heads(dof, m0).astype(BF16)
                lh = jnp.max(jnp.concatenate([jnp.where(m0, lp, -jnp.inf), jnp.where(m0, -jnp.inf, lp)], axis=0),
                             axis=-1, keepdims=True)
                delta = jnp.sum(_two_heads(prod, m0), axis=-1, keepdims=True)
                p = jnp.exp(_nt(qq, k) + bs - lh)
                ds = (p * (_nt(dd, v) - delta)).astype(BF16)
                dq = _nn(jnp.concatenate([ds[:QB], ds[QB:]], axis=1), _two_heads(kf, m0).astype(BF16))
                return dq * 0.125, _tn(ds, qq), _tn(p.astype(BF16), dd)

            def step(i, carry):
                loaded = [load(i * GROUP + u) for u in range(GROUP)]
                done = [block(*vals) for _, vals in loaded]
                for ((cur, prev), _), (dq, dk, dv) in zip(loaded, done):
                    dq_ref[cur, :] = dq_ref[cur, :] + dq
                    if prev is not None:
                        dk_ref[prev, :] = dk_ref[prev, :] + dk[:QB]
                        dv_ref[prev, :] = dv_ref[prev, :] + dv[:QB]
                    dk_ref[cur, :] = dk_ref[cur, :] + dk[-QB:]
                    dv_ref[cur, :] = dv_ref[cur, :] + dv[-QB:]
                return carry
            lax.fori_loop(0, NBLK // GROUP, step, 0)

    blk = pl.BlockSpec((S, 128), lambda hp: (0, hp))
    return pl.pallas_call(
        body, name="attn_bwd", grid=(8,),
        in_specs=[_colblk(0), _colblk(1), _colblk(2), _colblk(3), blk, blk, blk], out_specs=[blk] * 4,
        out_shape=[jax.ShapeDtypeStruct((S, D), F32)] * 4,
        scratch_shapes=[pltpu.VMEM((S, 128), F32), pltpu.VMEM((S, 128), F32), pltpu.VMEM((2, 2 * QB, 2 * QB), F32)],
        compiler_params=_cparams(("parallel",)),
    )(z, z, z, z, d_cat, o, lse)


def assemble_dz_even(parts):
    def body(*refs):
        o_ref = refs[-1]
        for j in range(6):
            o_ref[:, j * D:(j + 1) * D] = refs[j][...].astype(o_ref.dtype)
    tr = 512
    blk = pl.BlockSpec((tr, D), lambda i: (i, 0))
    return pl.pallas_call(
        body, name="assemble_dz_even", grid=(S // tr,), in_specs=[blk] * 6,
        out_specs=pl.BlockSpec((tr, 6 * D), lambda i: (i, 0)),
        out_shape=jax.ShapeDtypeStruct((S, 6 * D), BF16),
        compiler_params=_cparams(("parallel",)),
    )(*parts)


def _pool_window(g):
    return jnp.where(g == 0, 2.0, jnp.where(g == 1, 4.0, jnp.where(g == 2, 8.0, 16.0)))


def _pool_sel(g, levels):
    return jnp.where(g == 0, levels[0], jnp.where(g == 1, levels[1], jnp.where(g == 2, levels[2], levels[3])))


def _pool_fwd_math(v, g):
    t = lax.broadcasted_iota(jnp.int32, (S, 1), 0)
    s = v
    levels = []
    for k in (1, 2, 4, 8):
        s = s + jnp.where(t >= k, pltpu.roll(s, k, 0), 0.0)
        levels.append(s)
    cnt = jnp.minimum((t + 1).astype(F32), _pool_window(g))
    return _pool_sel(g, levels) / cnt - v, cnt


def pool_fwd(z, pw, ps, cat):
    def body(v_ref, g_ref, pw_ref, ps_ref, cat_ref, o_ref):
        g = pl.program_id(0)
        pooled, _ = _pool_fwd_math(v_ref[...], g)
        mixed = _nn(pooled.astype(BF16), pw_ref[...].astype(BF16))
        gb = g_ref[...]
        o_ref[...] = (mixed * ps_ref[...] * (gb * _sig(gb))).astype(o_ref.dtype)

    return pl.pallas_call(
        body, name="pool_fwd", grid=(4,),
        in_specs=[pl.BlockSpec((S, 256), lambda g: (0, 16 + g)),
                  pl.BlockSpec((S, 256), lambda g: (0, 20 + g)),
                  pl.BlockSpec((None, 256, 256), lambda g: (g, 0, 0)),
                  pl.BlockSpec((1, 256), lambda g: (0, g)), pl.BlockSpec(memory_space=pl.ANY)],
        out_specs=pl.BlockSpec((S, 256), lambda g: (0, 4 + g)),
        out_shape=jax.ShapeDtypeStruct((S, 2 * D), BF16),
        input_output_aliases={4: 0},
        compiler_params=_cparams(("parallel",)),
    )(z, z, pw, ps, cat)


def pool_bwd(z, d_cat, pw, ps):
    def body(v_ref, g_ref, d_ref, pw_ref, ps_ref, dv_ref, dg_ref, dpw_ref, dps_ref):
        g = pl.program_id(0)
        v = v_ref[...]
        pooled, cnt = _pool_fwd_math(v, g)
        pwb = pw_ref[...].astype(BF16)
        pb = pooled.astype(BF16)
        mixed = _nn(pb, pwb)
        gb = g_ref[...]
        sg = _sig(gb)
        dout = d_ref[...]
        sc = ps_ref[...]
        dg_ref[...] = dout * mixed * sc * (sg * (1.0 + gb * (1.0 - sg)))
        dms = dout * (gb * sg)
        dps_ref[...] = jnp.sum(dms * mixed, axis=0, keepdims=True)
        dmx = (dms * sc).astype(BF16)
        dpw_ref[...] = _tn(pb, dmx)
        dpooled = _nt(dmx, pwb)
        t = lax.broadcasted_iota(jnp.int32, (S, 1), 0)
        s = dpooled / cnt
        levels = []
        for k in (1, 2, 4, 8):
            s = s + jnp.where(t < S - k, pltpu.roll(s, S - k, 0), 0.0)
            levels.append(s)
        dv_ref[...] = _pool_sel(g, levels) - dpooled

    return pl.pallas_call(
        body, name="pool_bwd", grid=(4,),
        in_specs=[pl.BlockSpec((S, 256), lambda g: (0, 16 + g)),
                  pl.BlockSpec((S, 256), lambda g: (0, 20 + g)),
                  pl.BlockSpec((S, 256), lambda g: (0, 4 + g)),
                  pl.BlockSpec((None, 256, 256), lambda g: (g, 0, 0)),
                  pl.BlockSpec((1, 256), lambda g: (0, g))],
        out_specs=[pl.BlockSpec((S, 256), lambda g: (0, g)),
                   pl.BlockSpec((S, 256), lambda g: (0, g)),
                   pl.BlockSpec((None, 256, 256), lambda g: (g, 0, 0)),
                   pl.BlockSpec((1, 256), lambda g: (0, g))],
        out_shape=[jax.ShapeDtypeStruct((S, D), F32), jax.ShapeDtypeStruct((S, D), F32),
                   jax.ShapeDtypeStruct((4, 256, 256), F32), jax.ShapeDtypeStruct((1, D), F32)],
        compiler_params=_cparams(("parallel",)),
    )(z, z, d_cat, pw, ps)


CH = 128


def _sgu_common(v, lng, lnb, w_ref):
    mu = jnp.mean(v, axis=-1, keepdims=True)
    vc = v - mu
    rs = lax.rsqrt(jnp.mean(vc * vc, axis=-1, keepdims=True) + EPS)
    xhat = vc * rs
    vn = (xhat * lng + lnb).astype(BF16)
    ri = lax.broadcasted_iota(jnp.int32, (CH, CH), 0)
    ci = lax.broadcasted_iota(jnp.int32, (CH, CH), 1)
    tril = ri >= ci
    ws = [jnp.where(tril, w_ref[g], 0.0).astype(BF16) for g in range(4)]
    return xhat, rs, vn, tril, ws


def _zspec(off):
    return pl.BlockSpec((CH, D), lambda c: (c, off))


def _full(shape):
    return pl.BlockSpec(shape, lambda c: (0,) * len(shape))


def sgu_fwd(z, lng, lnb, w, bfull):
    def body(u_ref, v_ref, g_ref, lng_ref, lnb_ref, w_ref, b_ref, o_ref):
        _, _, vn, _, ws = _sgu_common(v_ref[...], lng_ref[...], lnb_ref[...], w_ref)
        for g in range(4):
            sl = slice(g * 256, (g + 1) * 256)
            mixed = _nn(ws[g], vn[:, sl]) + b_ref[:, sl]
            gc = g_ref[:, sl]
            o_ref[:, sl] = (u_ref[:, sl] * mixed * (gc * _sig(gc))).astype(o_ref.dtype)

    return pl.pallas_call(
        body, name="sgu_fwd", grid=(S // CH,),
        in_specs=[_zspec(0), _zspec(1), _zspec(2), _full((1, D)), _full((1, D)),
                  _full((4, CH, CH)), _full((CH, D))],
        out_specs=pl.BlockSpec((CH, D), lambda c: (c, 0)),
        out_shape=jax.ShapeDtypeStruct((S, D), BF16),
        compiler_params=_cparams(("parallel",)),
    )(z, z, z, lng, lnb, w, bfull)


def sgu_bwd(z, d_cat, lng, lnb, w, bfull):
    def body(u_ref, v_ref, g_ref, d_ref, lng_ref, lnb_ref, w_ref, b_ref,
             du_ref, dv_ref, dg_ref, dw_ref, db_ref, dlg_ref, dlb_ref):
        @pl.when(pl.program_id(0) == 0)
        def _():
            dw_ref[...] = jnp.zeros_like(dw_ref)
            db_ref[...] = jnp.zeros_like(db_ref)
            dlg_ref[...] = jnp.zeros_like(dlg_ref)
            dlb_ref[...] = jnp.zeros_like(dlb_ref)

        lng = lng_ref[...]
        xhat, rs, vn, tril, ws = _sgu_common(v_ref[...], lng, lnb_ref[...], w_ref)
        lane = lax.broadcasted_iota(jnp.int32, (1, 128), 1)
        db = jnp.zeros((CH, 128), F32)
        dvn_parts = []
        for g in range(4):
            sl = slice(g * 256, (g + 1) * 256)
            mixed = _nn(ws[g], vn[:, sl]) + b_ref[:, sl]
            gc = g_ref[:, sl]
            sg = _sig(gc)
            u = u_ref[:, sl]
            dc = d_ref[:, sl]
            du_ref[:, sl] = dc * mixed * (gc * sg)
            dg_ref[:, sl] = dc * u * mixed * (sg * (1.0 + gc * (1.0 - sg)))
            dmx = dc * u * (gc * sg)
            db = db + jnp.where(lane == g, jnp.sum(dmx, axis=-1, keepdims=True), 0.0)
            dmb = dmx.astype(BF16)
            dw_ref[g] += jnp.where(tril, _nt(dmb, vn[:, sl]), 0.0)
            dvn_parts.append(_tn(ws[g], dmb))
        db_ref[...] += db
        dvn = jnp.concatenate(dvn_parts, axis=1)
        dlb_ref[...] += jnp.sum(dvn, axis=0, keepdims=True)
        dlg_ref[...] += jnp.sum(dvn * xhat, axis=0, keepdims=True)
        dxh = dvn * lng
        dv_ref[...] = rs * (dxh - jnp.mean(dxh, axis=-1, keepdims=True)
                            - xhat * jnp.mean(dxh * xhat, axis=-1, keepdims=True))

    row = pl.BlockSpec((CH, D), lambda c: (c, 0))
    return pl.pallas_call(
        body, name="sgu_bwd", grid=(S // CH,),
        in_specs=[_zspec(0), _zspec(1), _zspec(2), row, _full((1, D)), _full((1, D)),
                  _full((4, CH, CH)), _full((CH, D))],
        out_specs=[row, row, row, _full((4, CH, CH)), _full((CH, 128)), _full((1, D)), _full((1, D))],
        out_shape=[jax.ShapeDtypeStruct((S, D), F32)] * 3
        + [jax.ShapeDtypeStruct((4, CH, CH), F32), jax.ShapeDtypeStruct((CH, 128), F32),
           jax.ShapeDtypeStruct((1, D), F32), jax.ShapeDtypeStruct((1, D), F32)],
        compiler_params=_cparams(("arbitrary",)),
    )(z, z, z, d_cat, lng, lnb, w, bfull)


TB = 256


def _cmul(ar, ai, br, bi):
    return ar * br - ai * bi, ar * bi + ai * br


def _scan_consts(ar, ai, reverse):
    a2 = _cmul(ar, ai, ar, ai)
    a4 = _cmul(*a2, *a2)
    row = lax.broadcasted_iota(jnp.int32, (8, NS), 0)

    def masked(k, p):
        keep = (row < 8 - k) if reverse else (row >= k)
        return jnp.where(keep, p[0], 0.0), jnp.where(keep, p[1], 0.0)
    pr = jnp.zeros((8, NS), F32)
    pi = jnp.zeros((8, NS), F32)
    cr, ci = ar, ai
    for r in range(8):
        sel = row == (7 - r if reverse else r)
        pr = jnp.where(sel, cr, pr)
        pi = jnp.where(sel, ci, pi)
        cr, ci = _cmul(cr, ci, ar, ai)
    return (masked(1, (ar, ai)), masked(2, a2), masked(4, a4)), (pr, pi), row


def scan_fwd(bu, abr, abi):
    def body(bu_ref, ar_ref, ai_ref, h_ref, car, cai):
        @pl.when(pl.program_id(0) == 0)
        def _():
            car[...] = jnp.zeros_like(car)
            cai[...] = jnp.zeros_like(cai)

        pows, (pr, pi), row = _scan_consts(ar_ref[...], ai_ref[...], False)

        def tile(t, carry):
            c_r, c_i = carry
            rows = pl.ds(pl.multiple_of(t * 8, 8), 8)
            xr = bu_ref[rows, 0:NS]
            xi = bu_ref[rows, NS:2 * NS]
            for k, (kr, ki) in zip((1, 2, 4), pows):
                sr = pltpu.roll(xr, k, 0)
                si = pltpu.roll(xi, k, 0)
                xr, xi = xr + kr * sr - ki * si, xi + kr * si + ki * sr
            xr, xi = xr + pr * c_r - pi * c_i, xi + pr * c_i + pi * c_r
            h_ref[rows, 0:NS] = xr
            h_ref[rows, NS:2 * NS] = xi
            return (jnp.broadcast_to(xr[7:8, :], (8, NS)), jnp.broadcast_to(xi[7:8, :], (8, NS)))

        c_r, c_i = lax.fori_loop(0, TB // 8, tile, (car[...], cai[...]))
        car[...] = c_r
        cai[...] = c_i

    return pl.pallas_call(
        body, name="s5_scan_fwd", grid=(S // TB,),
        in_specs=[pl.BlockSpec((TB, 2 * NS), lambda i: (i, 0)),
                  pl.BlockSpec((1, NS), lambda i: (0, 0)), pl.BlockSpec((1, NS), lambda i: (0, 0))],
        out_specs=pl.BlockSpec((TB, 2 * NS), lambda i: (i, 0)),
        out_shape=jax.ShapeDtypeStruct((S, 2 * NS), F32),
        scratch_shapes=[pltpu.VMEM((8, NS), F32), pltpu.VMEM((8, NS), F32)],
        compiler_params=_cparams(("arbitrary",)),
    )(bu, abr, abi)


def scan_bwd(eta, h, abr, abi):
    nt = S // TB

    def body(e_ref, h_ref, ar_ref, ai_ref, l_ref, da_ref, car, cai):
        @pl.when(pl.program_id(0) == 0)
        def _():
            car[...] = jnp.zeros_like(car)
            cai[...] = jnp.zeros_like(cai)
            da_ref[...] = jnp.zeros_like(da_ref)

        pows, (pr, pi), row = _scan_consts(ar_ref[...], -ai_ref[...], True)

        def tile(tt, carry):
            c_r, c_i, acr, aci = carry
            t = TB // 8 - 1 - tt
            rows = pl.ds(pl.multiple_of(t * 8, 8), 8)
            xr = e_ref[rows, 0:NS]
            xi = e_ref[rows, NS:2 * NS]
            for k, (kr, ki) in zip((1, 2, 4), pows):
                sr = pltpu.roll(xr, 8 - k, 0)
                si = pltpu.roll(xi, 8 - k, 0)
                xr, xi = xr + kr * sr - ki * si, xi + kr * si + ki * sr
            xr, xi = xr + pr * c_r - pi * c_i, xi + pr * c_i + pi * c_r
            l_ref[rows, 0:NS] = xr
            l_ref[rows, NS:2 * NS] = xi
            nr = jnp.where(row < 7, pltpu.roll(xr, 7, 0), c_r)
            ni = jnp.where(row < 7, pltpu.roll(xi, 7, 0), c_i)
            hr = h_ref[rows, 0:NS]
            hi = h_ref[rows, NS:2 * NS]
            acr = acr + hr * nr + hi * ni
            aci = aci + hr * ni - hi * nr
            return (jnp.broadcast_to(xr[0:1, :], (8, NS)), jnp.broadcast_to(xi[0:1, :], (8, NS)), acr, aci)

        zero = jnp.zeros((8, NS), F32)
        c_r, c_i, acr, aci = lax.fori_loop(0, TB // 8, tile, (car[...], cai[...], zero, zero))
        car[...] = c_r
        cai[...] = c_i
        da_ref[:, 0:NS] += acr
        da_ref[:, NS:2 * NS] += aci

    rev = pl.BlockSpec((TB, 2 * NS), lambda i: (nt - 1 - i, 0))
    return pl.pallas_call(
        body, name="s5_scan_bwd", grid=(nt,),
        in_specs=[rev, rev, pl.BlockSpec((1, NS), lambda i: (0, 0)), pl.BlockSpec((1, NS), lambda i: (0, 0))],
        out_specs=[rev, pl.BlockSpec((8, 2 * NS), lambda i: (0, 0))],
        out_shape=[jax.ShapeDtypeStruct((S, 2 * NS), F32), jax.ShapeDtypeStruct((8, 2 * NS), F32)],
        scratch_shapes=[pltpu.VMEM((8, NS), F32), pltpu.VMEM((8, NS), F32)],
        compiler_params=_cparams(("arbitrary",)),
    )(eta, h, abr, abi)


GC = 0.7978845608028654
GA = 0.044715


def s5_post(hc, z, dskip):
    def fn(hv, xd, dv):
        y = hv + dv * xd
        return y, 0.5 * y * (1.0 + jnp.tanh(GC * (y + GA * y * y * y)))
    return rw(fn, [(hc, 0, 512), (z, 3072, 512)], [(512, F32), (512, BF16)], "s5_post", S, consts=[dskip])


def s5_post_bwd(dyg, ypre, z, dskip):
    def fn(dy, y, xd, dv):
        th = jnp.tanh(GC * (y + GA * y * y * y))
        dg = 0.5 * (1.0 + th) + 0.5 * y * (1.0 - th * th) * GC * (1.0 + 3.0 * GA * y * y)
        dyp = dy * dg
        return dyp, dyp * dv, jnp.sum(dyp * xd, axis=0, keepdims=True)
    return rw(fn, [(dyg, 0, 512), (ypre, 0, 512), (z, 3072, 512)], [(512, BF16), (512, F32)],
              "s5_post_bwd", S, consts=[dskip], accs=[(1, 512)])


def glu_fwd(t, z, c_out):
    def fn(t1, t2, gd, co):
        return (jnp.concatenate([co, (t1 * _sig(t2) * (gd * _sig(gd))).astype(BF16)], axis=1),)
    return rw(fn, [(t, 0, 512), (t, 512, 512), (z, 3584, 512), (c_out, 0, D)], [(D + 512, BF16)], "glu_fwd", S)[0]


def glu_bwd(t, z, d_cat):
    def fn(t1, t2, gd, dd):
        s2, sg = _sig(t2), _sig(gd)
        sl = gd * sg
        return (jnp.concatenate([dd * s2 * sl, dd * t1 * s2 * (1.0 - s2) * sl], axis=1),
                dd * t1 * s2 * (sg * (1.0 + gd * (1.0 - sg))))
    return rw(fn, [(t, 0, 512), (t, 512, 512), (z, 3584, 512), (d_cat, 1024, 512)],
              [(D, BF16), (512, F32)], "glu_bwd", S)


def assemble_dz_odd(du, dv, dgc, dxd, dgd):
    def body(a, b, c, d, e, o_ref):
        o_ref[:, 0:D] = a[...].astype(BF16)
        o_ref[:, D:2 * D] = b[...].astype(BF16)
        o_ref[:, 2 * D:3 * D] = c[...].astype(BF16)
        o_ref[:, 3 * D:3 * D + 512] = d[...].astype(BF16)
        o_ref[:, 3 * D + 512:4 * D] = e[...].astype(BF16)
    tr = 512
    blk = pl.BlockSpec((tr, D), lambda i: (i, 0))
    half = pl.BlockSpec((tr, 512), lambda i: (i, 0))
    return pl.pallas_call(
        body, name="assemble_dz_odd", grid=(S // tr,), in_specs=[blk, blk, blk, half, half],
        out_specs=pl.BlockSpec((tr, 4 * D), lambda i: (i, 0)),
        out_shape=jax.ShapeDtypeStruct((S, 4 * D), BF16),
        compiler_params=_cparams(("parallel",)),
    )(du, dv, dgc, dxd, dgd)


TQ = 1024


def _xattn_probs(qh, kh):
    s = _nt(qh, kh) * 0.0625
    p = jnp.exp(s - jnp.max(s, axis=-1, keepdims=True))
    return p / jnp.sum(p, axis=-1, keepdims=True)


def xattn_fwd(q, kv):
    def body(q_ref, kv_ref, o_ref):
        outs = []
        for h in range(4):
            sl = slice(h * 256, (h + 1) * 256)
            p = _xattn_probs(q_ref[:, sl].astype(BF16), kv_ref[:, sl].astype(BF16))
            vh = kv_ref[:, D + h * 256:D + (h + 1) * 256].astype(BF16)
            outs.append((sl, _nn(p.astype(BF16), vh)))
        for sl, o in outs:
            o_ref[:, sl] = o.astype(o_ref.dtype)

    return pl.pallas_call(
        body, name="xattn_fwd", grid=(S // TQ,),
        in_specs=[pl.BlockSpec((TQ, D), lambda i: (i, 0)), pl.BlockSpec((MEM, 2 * D), lambda i: (0, 0))],
        out_specs=pl.BlockSpec((TQ, D), lambda i: (i, 0)),
        out_shape=jax.ShapeDtypeStruct((S, D), BF16),
        compiler_params=_cparams(("parallel",)),
    )(q, kv)


def xattn_bwd(q, kv, d_o):
    def body(q_ref, kv_ref, do_ref, dq_ref, dkv_ref):
        @pl.when(pl.program_id(0) == 0)
        def _():
            dkv_ref[...] = jnp.zeros_like(dkv_ref)

        done = []
        for h in range(4):
            sl = slice(h * 256, (h + 1) * 256)
            vs = slice(D + h * 256, D + (h + 1) * 256)
            qh = q_ref[:, sl].astype(BF16)
            kh = kv_ref[:, sl].astype(BF16)
            vh = kv_ref[:, vs].astype(BF16)
            doh = do_ref[:, sl].astype(BF16)
            p = _xattn_probs(qh, kh)
            dp = _nt(doh, vh)
            ds = (p * (dp - jnp.sum(p * dp, axis=-1, keepdims=True)) * 0.0625).astype(BF16)
            done.append((sl, vs, _nn(ds, kh), _tn(ds, qh), _tn(p.astype(BF16), doh)))
        for sl, vs, dq, dk, dv in done:
            dq_ref[:, sl] = dq.astype(dq_ref.dtype)
            dkv_ref[:, sl] += dk
            dkv_ref[:, vs] += dv

    return pl.pallas_call(
        body, name="xattn_bwd", grid=(S // TQ,),
        in_specs=[pl.BlockSpec((TQ, D), lambda i: (i, 0)), pl.BlockSpec((MEM, 2 * D), lambda i: (0, 0)),
                  pl.BlockSpec((TQ, D), lambda i: (i, 0))],
        out_specs=[pl.BlockSpec((TQ, D), lambda i: (i, 0)), pl.BlockSpec((MEM, 2 * D), lambda i: (0, 0))],
        out_shape=[jax.ShapeDtypeStruct((S, D), BF16), jax.ShapeDtypeStruct((MEM, 2 * D), F32)],
        compiler_params=_cparams(("arbitrary",)),
    )(q, kv, d_o)


def _s5_disc(a_re, a_im, log_dt, b_re, b_im):
    dt = jnp.exp(log_dt)[:, None]
    mag = jnp.exp(dt * a_re)
    abr = mag * jnp.cos(dt * a_im)
    abi = mag * jnp.sin(dt * a_im)
    nr, ni = abr - 1.0, abi
    inv = 1.0 / (a_re * a_re + a_im * a_im)
    cr = (nr * a_re + ni * a_im) * inv
    ci = (ni * a_re - nr * a_im) * inv
    bbr = cr[..., None] * b_re - ci[..., None] * b_im
    bbi = cr[..., None] * b_im + ci[..., None] * b_re
    return abr, abi, bbr, bbi


VM = pl.BlockSpec(memory_space=pltpu.VMEM)


def s5_embed(bt_re, bt_im, ct_re, ct_im):
    def body(br, bi, cr, ci, b_ref, c_ref):
        b_ref[...] = jnp.zeros_like(b_ref)
        c_ref[...] = jnp.zeros_like(c_ref)
        for g in range(NG):
            rows, cols = slice(g * NH, (g + 1) * NH), slice(g * NP, (g + 1) * NP)
            b_ref[rows, cols] = br[g]
            b_ref[rows, NS + g * NP:NS + (g + 1) * NP] = bi[g]
            c_ref[cols, rows] = cr[g]
            c_ref[NS + g * NP:NS + (g + 1) * NP, rows] = -ci[g]

    return pl.pallas_call(
        body, name="s5_embed", in_specs=[VM] * 4, out_specs=[VM] * 2,
        out_shape=[jax.ShapeDtypeStruct((NG * NH, 2 * NS), F32), jax.ShapeDtypeStruct((2 * NS, NG * NH), F32)],
        compiler_params=pltpu.CompilerParams(vmem_limit_bytes=VMEM_LIMIT),
    )(bt_re, bt_im, ct_re, ct_im)


def s5_extract(gb, gc):
    def body(gb_ref, gc_ref, br, bi, cr, ci):
        for g in range(NG):
            rows, cols = slice(g * NH, (g + 1) * NH), slice(g * NP, (g + 1) * NP)
            br[g] = gb_ref[rows, cols]
            bi[g] = gb_ref[rows, NS + g * NP:NS + (g + 1) * NP]
            cr[g] = gc_ref[cols, rows]
            ci[g] = -gc_ref[NS + g * NP:NS + (g + 1) * NP, rows]

    return pl.pallas_call(
        body, name="s5_extract", in_specs=[VM] * 2, out_specs=[VM] * 4,
        out_shape=[jax.ShapeDtypeStruct((NG, NH, NP), F32)] * 2 + [jax.ShapeDtypeStruct((NG, NP, NH), F32)] * 2,
        compiler_params=pltpu.CompilerParams(vmem_limit_bytes=VMEM_LIMIT),
    )(gb, gc)


HC, HS = NG * NH // 2, NS // 2
TS = 1024


def s5_to_states(x, w, mode, name, z_off=0):
    if mode == "nn":
        wb, wm = (HC, HS), lambda i, j, kk: (j % 2, j)
    else:
        wb, wm = (HS, HC), lambda i, j, kk: (j, j % 2)
    return mm_band(x, w, mode, name, (S // TS, 4, 1), ((TS, HC), wb, (TS, HS)),
                   (lambda i, j, kk: (i, z_off + j % 2), wm, lambda i, j, kk: (i, j)), (S, 2 * NS))


def s5_to_channels(x, w, mode, name, add=None):
    if mode == "nn":
        wb, wm = (HS, HC), lambda i, j, kk: (j + 2 * kk, j)
    else:
        wb, wm = (HC, HS), lambda i, j, kk: (j, j + 2 * kk)
    return mm_band(x, w, mode, name, (S // TS, 2, 2), ((TS, HS), wb, (TS, HC)),
                   (lambda i, j, kk: (i, j + 2 * kk), wm, lambda i, j, kk: (i, j)), (S, NG * NH), add=add)


def s5_outer(a, b, name, states_first, z_off=0):
    if states_first:
        return mm_band(a, b, "tn", name, (4, 1, 1), ((S, HS), (S, HC), (HS, HC)),
                       (lambda i, j, kk: (0, i), lambda i, j, kk: (0, i % 2), lambda i, j, kk: (i, i % 2)),
                       (2 * NS, NG * NH))
    return mm_band(a, b, "tn", name, (1, 4, 1), ((S, HC), (S, HS), (HC, HS)),
                   (lambda i, j, kk: (0, z_off + j % 2), lambda i, j, kk: (0, j), lambda i, j, kk: (j % 2, j)),
                   (NG * NH, 2 * NS))


def _fwd_even(i, x, P, W):
    hn = rms_fwd(x, P["norm_ab"][i:i + 1], "rms_ab_fwd")
    z = mm(m2(hn), W["w_in"], "nn", "in_ab")
    o, lse, cat = attn_fwd(z)
    if "more" in W:
        W.update(W.pop("more")(cat))
    cat = pool_fwd(z, W["pool_w"], P["pool_scale"][i:i + 1], cat)
    x_mid = mm(m2(cat), W["w_out"], "nn", "out_ab", add=m2(x))
    return x_mid, dict(x=x, hn=hn, z=z, o=o, lse=lse, cat=cat)


def _bwd_even(i, dx_mid, sv, P, W, G, GW):
    z = sv["z"]
    d_cat = mm(m2(dx_mid), W["w_out"], "nt", "out_ab_dx")
    GW["w_out"] = mm(m2(sv["cat"]), m2(dx_mid), "tn", "out_ab_dw").reshape(4, 512, D)
    dq, dk, dv, dga = attn_bwd(z, d_cat, sv["o"], sv["lse"])
    dvb, dgb, dpw, dps = pool_bwd(z, d_cat, W["pool_w"], P["pool_scale"][i:i + 1])
    GW["pool_w"] = dpw.reshape(4, 4, 64, 256).transpose(1, 0, 2, 3).reshape(4, 256, 256)
    G["pool_scale"][i] = dps[0]
    d_z = assemble_dz_even((dq, dk, dv, dga, dvb, dgb))
    d_hn = mm(m2(d_z), W["w_in"], "nt", "in_ab_dx")
    GW["w_in"] = mm(m2(sv["hn"]), m2(d_z), "tn", "in_ab_dw", out=outcs(D, 1536))
    return d_hn, P["norm_ab"][i:i + 1], "norm_ab", "rms_ab_bwd"


def _fwd_odd(i, x, P, W):
    hn = rms_fwd(x, P["norm_cd"][i:i + 1], "rms_cd_fwd")
    z = mm(m2(hn), W["w_in"], "nn", "in_cd")
    bfull = jnp.repeat(P["sgu_b"][i].T, 256, axis=1)
    c_out = sgu_fwd(z, P["sgu_ln_g"][i:i + 1], P["sgu_ln_b"][i:i + 1], P["sgu_w"][i], bfull)
    disc, disc_vjp = jax.vjp(_s5_disc, P["s5_a_re"][i], P["s5_a_im"][i], P["s5_log_dt"][i],
                             P["s5_b_re"][i], P["s5_b_im"][i])
    abr, abi, bbr, bbi = disc
    bbd, cbd = s5_embed(bbr.transpose(0, 2, 1), bbi.transpose(0, 2, 1),
                        P["s5_c_re"][i].transpose(0, 2, 1), P["s5_c_im"][i].transpose(0, 2, 1))
    abr, abi = abr.reshape(1, NS), abi.reshape(1, NS)
    bu = s5_to_states(z, bbd, "nn", "s5_bu", z_off=3072 // HC)
    h = scan_fwd(bu, abr, abi)
    hc = s5_to_channels(h, cbd, "nn", "s5_hc")
    dskip = P["s5_d"][i:i + 1]
    ypre, yg = s5_post(hc, z, dskip)
    if "more" in W:
        W.update(W.pop("more")(yg))
    w12 = W["w12"]
    t = mm(m2(yg), m2(w12), "nn", "glu_t")
    cat = glu_fwd(t, z, c_out)
    x_mid = mm(m2(cat), W["w_out"], "nn", "out_cd", add=m2(x))
    return x_mid, dict(x=x, hn=hn, z=z, bfull=bfull, disc_vjp=disc_vjp, bbd=bbd, cbd=cbd, abr=abr,
                       abi=abi, h=h, ypre=ypre, yg=yg, w12=w12, t=t, cat=cat, dskip=dskip)


def _bwd_odd(i, dx_mid, sv, P, W, G, GW):
    z = sv["z"]
    d_cat = mm(m2(dx_mid), W["w_out"], "nt", "out_cd_dx")
    GW["w_out"] = mm(m2(sv["cat"]), m2(dx_mid), "tn", "out_cd_dw").reshape(4, 384, D)
    du, dv, dgc, dws, dbs, dlg, dlb = sgu_bwd(z, d_cat, P["sgu_ln_g"][i:i + 1], P["sgu_ln_b"][i:i + 1],
                                               P["sgu_w"][i], sv["bfull"])
    G["sgu_w"][i], G["sgu_b"][i] = dws, dbs[:, :4].T
    G["sgu_ln_g"][i], G["sgu_ln_b"][i] = dlg[0], dlb[0]
    dt, dgd = glu_bwd(sv["t"], z, d_cat)
    gw12 = mm(m2(sv["yg"]), m2(dt), "tn", "glu_dw")
    GW["glu_w1"] = gw12[:, :512].reshape(4, 128, 512)
    GW["glu_w2"] = gw12[:, 512:].reshape(4, 128, 512)
    dyg = mm(m2(dt), m2(sv["w12"]), "nt", "glu_dx")
    dypre, dxd1, dd = s5_post_bwd(dyg, sv["ypre"], z, sv["dskip"])
    G["s5_d"][i] = dd[0]
    gcbd = s5_outer(sv["h"], dypre, "s5_dc", states_first=True)
    eta = s5_to_states(dypre, sv["cbd"], "nt", "s5_eta")
    lam, dacc = scan_bwd(eta, sv["h"], sv["abr"], sv["abi"])
    gbbd = s5_outer(z, lam, "s5_db", states_first=False, z_off=3072 // HC)
    dxd = s5_to_channels(lam, sv["bbd"], "nt", "s5_dx", add=dxd1)
    dacc = jnp.sum(dacc, axis=0)
    dbt_re, dbt_im, dct_re, dct_im = s5_extract(gbbd, gcbd)
    G["s5_c_re"][i], G["s5_c_im"][i] = dct_re.transpose(0, 2, 1), dct_im.transpose(0, 2, 1)
    d_bbr, d_bbi = dbt_re.transpose(0, 2, 1), dbt_im.transpose(0, 2, 1)
    (G["s5_a_re"][i], G["s5_a_im"][i], G["s5_log_dt"][i], G["s5_b_re"][i], G["s5_b_im"][i]) = sv["disc_vjp"](
        (dacc[:NS].reshape(NG, NP), dacc[NS:].reshape(NG, NP), d_bbr, d_bbi))
    d_z = assemble_dz_odd(du, dv, dgc, dxd, dgd)
    d_hn = mm(m2(d_z), W["w_in"], "nt", "in_cd_dx")
    GW["w_in"] = mm(m2(sv["hn"]), m2(d_z), "tn", "in_cd_dw", out=outcs(D, 1024))
    return d_hn, P["norm_cd"][i:i + 1], "norm_cd", "rms_cd_bwd"


def _fwd_x(l, x, mem_n, P, W):
    if "more_x" in W:
        W.update(W.pop("more_x")(x))
    hx = rms_fwd(x, P["norm_x"][l:l + 1], "rms_x_fwd")
    q = mm(m2(hx), W["w_xq"], "nn", "xq", out_dtype=BF16)
    kv = mm(m2(mem_n), W["w_xkv"], "nn", "xkv", out_dtype=BF16)
    ox = xattn_fwd(q, kv)
    x_out = mm(m2(ox), W["w_xo"], "nn", "xo", add=m2(x))
    return x_out, dict(x=x, hx=hx, q=q, kv=kv, ox=ox)


def _bwd_x(l, dx_out, sv, mem_n, d_memn, P, W, G, GW):
    d_ox = mm(m2(dx_out), W["w_xo"], "nt", "xo_dx", out_dtype=BF16)
    GW["w_xo"] = mm(m2(sv["ox"]), m2(dx_out), "tn", "xo_dw").reshape(4, 256, D)
    dq, dkv = xattn_bwd(sv["q"], sv["kv"], d_ox)
    GW["w_xq"] = mm(m2(sv["hx"]), m2(dq), "tn", "xq_dw").reshape(4, 256, D)
    d_hx = mm(m2(dq), W["w_xq"], "nt", "xq_dx")
    GW["w_xkv"] = mm(m2(mem_n), m2(dkv), "tn", "xkv_dw", out=outcs(D, 512))
    d_memn = mm(m2(dkv), W["w_xkv"], "nt", "xkv_dx", add=None if d_memn is None else m2(d_memn))
    dx, dg = rms_bwd(sv["x"], d_hx, dx_out, P["norm_x"][l:l + 1], "rms_x_bwd")
    G["norm_x"][l] = dg[0]
    return dx, d_memn


SMALL_LAYERS = (("norm_ab", 2), ("pool_scale", 2), ("norm_cd", 2), ("sgu_ln_g", 2), ("sgu_ln_b", 2), ("sgu_w", 2),
                ("sgu_b", 2), ("s5_a_re", 2), ("s5_a_im", 2), ("s5_log_dt", 2), ("s5_b_re", 2), ("s5_b_im", 2),
                ("s5_c_re", 2), ("s5_c_im", 2), ("s5_d", 2), ("norm_x", 4))


def local_step(x, mem, tgt, P, weights_of, grads_done):
    G = {k: [None] * n for k, n in SMALL_LAYERS}
    mem_g = P["mem_norm"].reshape(1, D)
    mem_n = rms_fwd(mem, mem_g, "rms_mem_fwd")
    saved = []
    for layer in range(4):
        i = layer // 2
        W = weights_of(layer, x)
        x, sv_m = (_fwd_even if layer % 2 == 0 else _fwd_odd)(i, x, P, W)
        x, sv_x = _fwd_x(layer, x, mem_n, P, W)
        saved.append((sv_m, sv_x, W))
    dx, loss, dgf = final_loss(x, tgt, P["final_norm"].reshape(1, D))
    G["final_norm"] = dgf[0]
    d_memn = None
    for layer in reversed(range(4)):
        i = layer // 2
        sv_m, sv_x, W = saved[layer]
        GW = {}
        dx_mid, d_memn = _bwd_x(layer, dx, sv_x, mem_n, d_memn, P, W, G, GW)
        d_hn, g, key, name = (_bwd_even if layer % 2 == 0 else _bwd_odd)(i, dx_mid, sv_m, P, W, G, GW)
        token = grads_done(layer, GW)
        if token is not None:
            g = g + token
        dx, dg = rms_bwd(sv_m["x"], d_hn, dx_mid, g, name)
        G[key][i] = dg[0]
    _, dgm = rms_bwd(mem, d_memn, d_memn, mem_g, "rms_mem_bwd")
    G["mem_norm"] = dgm[0]
    return loss, dx, G


ANY = pl.BlockSpec(memory_space=pl.ANY)


def _place():
    x, y, c = lax.axis_index("x"), lax.axis_index("y"), lax.axis_index("c")
    chips = [(1 - x, y), (x, 1 - y), (1 - x, 1 - y)]
    return x, y, c, 2 * x + y, (x, y, 1 - c), chips


def _remote(src, dst, send, recv, k, dev):
    return pltpu.make_async_remote_copy(src_ref=src, dst_ref=dst, send_sem=send.at[k], recv_sem=recv.at[k],
                                        device_id=dev, device_id_type=MESHID)


HBM = pl.BlockSpec(memory_space=pltpu.HBM)
SEM = pl.BlockSpec(memory_space=pltpu.SEMAPHORE)
EFFECT = pltpu.SideEffectType.DATAFLOW_SIDE_EFFECTING


def _hbm(t):
    return pltpu.with_memory_space_constraint(t, pltpu.HBM)


def _first_copies(shard, land, send, recv, arriving):
    x, y, c, jme, sib, chips = _place()
    out = []
    for k, chip in enumerate(chips):
        slot = 2 * chip[0] + chip[1] if arriving else jme
        out.append(_remote(shard.at[c], land.at[slot, c], send, recv, k, (*chip, c)))
    out.append(_remote(shard, land.at[jme], send, recv, 3, sib))
    return out


def allgather_first_start(shard, after, name):
    def body(s_ref, l_ref, *rest):
        send, recv, token = rest[len(after)], rest[len(after) + 1], rest[-1]
        for cp in _first_copies(s_ref, l_ref, send, recv, False):
            cp.start()
        token[...] = jnp.zeros_like(token)

    land = (4,) + shard.shape
    return pl.pallas_call(
        body, name=name,
        out_shape=(pltpu.SemaphoreType.DMA((4,)), pltpu.SemaphoreType.DMA((4,)), pltpu.HBM(shard.shape, shard.dtype),
                   pltpu.HBM(land, shard.dtype), jax.ShapeDtypeStruct((8, 128), F32)),
        in_specs=[HBM, HBM] + [ANY] * len(after),
        out_specs=(SEM, SEM, HBM, HBM, pl.BlockSpec(memory_space=pltpu.VMEM)),
        input_output_aliases={0: 2, 1: 3},
        compiler_params=pltpu.CompilerParams(has_side_effects=EFFECT),
    )(_hbm(shard), _hbm(lax.empty(land, shard.dtype)), *after)


def allgather_first_wait(send, recv, shard, land, after, name):
    def body(s_ref, l_ref, send_r, recv_r, *rest):
        for cp in _first_copies(s_ref, l_ref, send_r, recv_r, True):
            cp.wait_send()
            cp.wait_recv()

    res = pl.pallas_call(
        body, name=name, out_shape=(pltpu.HBM(shard.shape, shard.dtype), pltpu.HBM(land.shape, land.dtype)),
        in_specs=[HBM, HBM, SEM, SEM] + [ANY] * len(after), out_specs=(HBM, HBM),
        input_output_aliases={0: 0, 1: 1},
        compiler_params=pltpu.CompilerParams(has_side_effects=EFFECT),
    )(shard, land, send, recv, *after)
    return res[1]


def allgather_forward(land):
    def body(l_in, l_ref, send, recv):
        x, y, c, jme, sib, chips = _place()
        cps = []
        for k, chip in enumerate(chips):
            piece = l_ref.at[2 * chip[0] + chip[1], c]
            cp = _remote(piece, piece, send, recv, k, sib)
            cp.start()
            cps.append(cp)
        for k, chip in enumerate(chips):
            piece = l_ref.at[2 * chip[0] + chip[1], 1 - c]
            _remote(piece, piece, send, recv, k, sib).wait_recv()
        for cp in cps:
            cp.wait_send()

    return pl.pallas_call(
        body, name="allgather_forward", in_specs=[ANY], out_specs=ANY,
        out_shape=jax.ShapeDtypeStruct(land.shape, land.dtype), input_output_aliases={0: 0},
        scratch_shapes=[pltpu.SemaphoreType.DMA((3,)), pltpu.SemaphoreType.DMA((3,))],
    )(land)


def _gather_copies(ins, lands, send, recv):
    x, y, c, jme, sib, chips = _place()
    devs = [(*chip, c) for chip in chips] + [sib]
    return [_remote(ins[a], lands[a].at[jme], send, recv, a * 4 + k, dev)
            for a in range(len(ins)) for k, dev in enumerate(devs)]


def allgather_start(shards, after, name):
    n, na = len(shards), len(after)

    def body(*refs):
        ins, lands = refs[:n], refs[n:2 * n]
        send, recv = refs[2 * n + na], refs[2 * n + na + 1]
        token = refs[-1]
        for cp in _gather_copies(ins, lands, send, recv):
            cp.start()
        token[...] = jnp.zeros_like(token)

    res = pl.pallas_call(
        body, name=name,
        out_shape=(pltpu.SemaphoreType.DMA((4 * n,)), pltpu.SemaphoreType.DMA((4 * n,)),
                   *[pltpu.HBM(s.shape, s.dtype) for s in shards],
                   *[pltpu.HBM((4,) + s.shape, s.dtype) for s in shards],
                   jax.ShapeDtypeStruct((8, 128), F32)),
        in_specs=[HBM] * (2 * n) + [ANY] * na,
        out_specs=(SEM, SEM, *[HBM] * (2 * n), pl.BlockSpec(memory_space=pltpu.VMEM)),
        input_output_aliases={a: 2 + a for a in range(2 * n)},
        compiler_params=pltpu.CompilerParams(has_side_effects=EFFECT),
    )(*[_hbm(s) for s in shards], *[_hbm(lax.empty((4,) + s.shape, s.dtype)) for s in shards], *after)
    return res[0], res[1], list(res[2:2 + n]), list(res[2 + n:2 + 2 * n]), res[-1]


def allgather_wait(send, recv, shards, lands, after, name):
    n = len(shards)

    def body(*refs):
        ins, zones = refs[:n], refs[n:2 * n]
        send_r, recv_r = refs[2 * n], refs[2 * n + 1]
        x, y, c, jme, sib, chips = _place()
        slots = [2 * chip[0] + chip[1] for chip in chips] + [jme]
        for a in range(n):
            for k, slot in enumerate(slots):
                cp = _remote(ins[a], zones[a].at[slot], send_r, recv_r, a * 4 + k, sib)
                cp.wait_send()
                cp.wait_recv()

    res = pl.pallas_call(
        body, name=name,
        out_shape=tuple(pltpu.HBM(t.shape, t.dtype) for t in list(shards) + list(lands)),
        in_specs=[HBM] * (2 * n) + [SEM, SEM, ANY], out_specs=tuple([HBM] * (2 * n)),
        input_output_aliases={a: a for a in range(2 * n)},
        compiler_params=pltpu.CompilerParams(has_side_effects=EFFECT),
    )(*shards, *lands, send, recv, after)
    return list(res[n:])


def allgather_small(slab):
    def body(in_ref, out_ref, send, recv, lsem):
        x, y, c, jme, sib, chips = _place()
        loc = pltpu.make_async_copy(in_ref, out_ref.at[jme], lsem.at[0])
        loc.start()
        cps = [_remote(in_ref, out_ref.at[jme], send, recv, k, (*chip, c)) for k, chip in enumerate(chips)]
        for cp in cps:
            cp.start()
        for k, chip in enumerate(chips):
            piece = out_ref.at[2 * chip[0] + chip[1]]
            _remote(piece, piece, send, recv, k, (*chip, c)).wait_recv()
        for cp in cps:
            cp.wait_send()
        loc.wait()

    return pl.pallas_call(
        body, name="allgather_small", in_specs=[ANY], out_specs=ANY,
        out_shape=jax.ShapeDtypeStruct((4,) + slab.shape, slab.dtype),
        scratch_shapes=[pltpu.SemaphoreType.DMA((3,)), pltpu.SemaphoreType.DMA((3,)), pltpu.SemaphoreType.DMA((1,))],
    )(slab)


def allreduce_small(v):
    hr = v.shape[0] // 2

    def body(v_ref, o_ref, r0, r1, r2, send, recv):
        x, y, c, jme, sib, chips = _place()
        mine = pl.ds(pl.multiple_of(c * hr, 8), hr)
        other = pl.ds(pl.multiple_of((1 - c) * hr, 8), hr)
        cp = _remote(v_ref.at[other], r0, send, recv, 0, sib)
        cp.start()
        cp.wait()
        o_ref[mine, :] = v_ref[mine, :] + r0[...]
        for k, (buf, peer) in enumerate(((r1, (1 - x, y, c)), (r2, (x, 1 - y, c))), start=1):
            cp = _remote(o_ref.at[mine], buf, send, recv, k, peer)
            cp.start()
            cp.wait()
            o_ref[mine, :] = o_ref[mine, :] + buf[...]
        cp = _remote(o_ref.at[mine], o_ref.at[mine], send, recv, 3, sib)
        cp.start()
        cp.wait_send()
        _remote(o_ref.at[other], o_ref.at[other], send, recv, 3, sib).wait_recv()

    vm = pl.BlockSpec(memory_space=pltpu.VMEM)
    half = pltpu.VMEM((hr, v.shape[1]), v.dtype)
    return pl.pallas_call(
        body, name="allreduce_small", in_specs=[vm], out_specs=vm,
        out_shape=jax.ShapeDtypeStruct(v.shape, v.dtype),
        scratch_shapes=[half] * 3 + [pltpu.SemaphoreType.DMA((4,)), pltpu.SemaphoreType.DMA((4,))],
        compiler_params=pltpu.CompilerParams(vmem_limit_bytes=VMEM_LIMIT),
    )(v)


def _pair_copies(gs, lands, send, recv):
    x, y, c, jme, sib, chips = _place()
    return [_remote(gs[a].at[:, 1 - c], lands[a], send, recv, a, sib) for a in range(len(gs))]


def rs_pair_start(gs, name):
    n = len(gs)

    def body(*refs):
        ins, lands = refs[:n], refs[n:2 * n]
        send, recv = refs[2 * n], refs[2 * n + 1]
        token = refs[-1]
        for cp in _pair_copies(ins, lands, send, recv):
            cp.start()
        token[...] = jnp.zeros_like(token)

    shapes = [(4,) + g.shape[2:] for g in gs]
    res = pl.pallas_call(
        body, name=name,
        out_shape=(pltpu.SemaphoreType.DMA((n,)), pltpu.SemaphoreType.DMA((n,)),
                   *[pltpu.HBM(g.shape, g.dtype) for g in gs], *[pltpu.HBM(s, F32) for s in shapes],
                   jax.ShapeDtypeStruct((8, 128), F32)),
        in_specs=[HBM] * (2 * n), out_specs=(SEM, SEM, *[HBM] * (2 * n), pl.BlockSpec(memory_space=pltpu.VMEM)),
        input_output_aliases={a: 2 + a for a in range(2 * n)},
        compiler_params=pltpu.CompilerParams(has_side_effects=EFFECT),
    )(*[_hbm(g) for g in gs], *[_hbm(lax.empty(s, F32)) for s in shapes])
    return res[0], res[1], list(res[2:2 + n]), list(res[2 + n:2 + 2 * n]), res[-1]


def rs_pair_wait(send, recv, gs, lands, after, name):
    n = len(gs)

    def body(*refs):
        ins, zones = refs[:n], refs[n:2 * n]
        for cp in _pair_copies(ins, zones, refs[2 * n], refs[2 * n + 1]):
            cp.wait_send()
            cp.wait_recv()

    res = pl.pallas_call(
        body, name=name,
        out_shape=tuple(pltpu.HBM(t.shape, t.dtype) for t in list(gs) + list(lands)),
        in_specs=[HBM] * (2 * n) + [SEM, SEM, ANY], out_specs=tuple([HBM] * (2 * n)),
        input_output_aliases={a: a for a in range(2 * n)},
        compiler_params=pltpu.CompilerParams(has_side_effects=EFFECT),
    )(*gs, *lands, send, recv, after)
    return list(res[:n]), list(res[n:])


SUM_ROWS = 256


def rs_pair_sum(g4s, gots, cidx):
    n = len(g4s)
    tiles = [(min(g.shape[2], SUM_ROWS), g.shape[3]) for g in g4s]
    nts = [g.shape[2] // tr for g, (tr, _) in zip(g4s, tiles)]

    def at(a, s):
        s = jnp.minimum(s, 4 * nts[a] - 1)
        return s // nts[a], s % nts[a]

    def body(c_ref, *refs):
        for a in range(n):
            refs[2 * n + a][...] = (refs[a][...] + refs[n + a][...]).astype(BF16)

    in_specs = [pl.BlockSpec((None, None) + tiles[a], lambda s, cr, a=a: (at(a, s)[0], cr[0], at(a, s)[1], 0))
                for a in range(n)]
    in_specs += [pl.BlockSpec((None,) + tiles[a], lambda s, cr, a=a: (*at(a, s), 0)) for a in range(n)]
    return pl.pallas_call(
        body, name="rs_pair_sum",
        grid_spec=pltpu.PrefetchScalarGridSpec(
            num_scalar_prefetch=1, grid=(4 * max(nts),), in_specs=in_specs,
            out_specs=[pl.BlockSpec((None,) + tiles[a], lambda s, cr, a=a: (*at(a, s), 0)) for a in range(n)]),
        out_shape=[jax.ShapeDtypeStruct((4,) + g.shape[2:], BF16) for g in g4s],
        compiler_params=_cparams(("arbitrary",)),
    )(cidx, *g4s, *gots)


def _chip_copies(ps, lands, send, recv):
    x, y, c, jme, sib, chips = _place()
    return [_remote(ps[a].at[2 * chip[0] + chip[1]], lands[a].at[jme], send, recv, a * 3 + k, (*chip, c))
            for a in range(len(ps)) for k, chip in enumerate(chips)]


def rs_chip_start(ps, name):
    n = len(ps)

    def body(*refs):
        ins, lands = refs[:n], refs[n:2 * n]
        send, recv = refs[2 * n], refs[2 * n + 1]
        token = refs[-1]
        for cp in _chip_copies(ins, lands, send, recv):
            cp.start()
        token[...] = jnp.zeros_like(token)

    res = pl.pallas_call(
        body, name=name,
        out_shape=(pltpu.SemaphoreType.DMA((3 * n,)), pltpu.SemaphoreType.DMA((3 * n,)),
                   *[pltpu.HBM(p.shape, p.dtype) for p in ps], *[pltpu.HBM(p.shape, p.dtype) for p in ps],
                   jax.ShapeDtypeStruct((8, 128), F32)),
        in_specs=[HBM] * (2 * n), out_specs=(SEM, SEM, *[HBM] * (2 * n), pl.BlockSpec(memory_space=pltpu.VMEM)),
        input_output_aliases={a: 2 + a for a in range(2 * n)},
        compiler_params=pltpu.CompilerParams(has_side_effects=EFFECT),
    )(*[_hbm(p) for p in ps], *[_hbm(lax.empty(p.shape, p.dtype)) for p in ps])
    return res[0], res[1], list(res[2:2 + n]), list(res[2 + n:2 + 2 * n]), res[-1]


def rs_chip_wait(send, recv, ps, lands, after, name):
    n = len(ps)

    def body(*refs):
        ins, zones = refs[:n], refs[n:2 * n]
        send_r, recv_r = refs[2 * n], refs[2 * n + 1]
        x, y, c, jme, sib, chips = _place()
        for a in range(n):
            for k, chip in enumerate(chips):
                jt = 2 * chip[0] + chip[1]
                cp = _remote(ins[a].at[jt], zones[a].at[jt], send_r, recv_r, a * 3 + k, (*chip, c))
                cp.wait_send()
                cp.wait_recv()

    res = pl.pallas_call(
        body, name=name,
        out_shape=tuple(pltpu.HBM(p.shape, p.dtype) for p in list(ps) + list(lands)),
        in_specs=[HBM] * (2 * n) + [SEM, SEM] + [ANY] * len(after), out_specs=tuple([HBM] * (2 * n)),
        input_output_aliases={a: a for a in range(2 * n)},
        compiler_params=pltpu.CompilerParams(has_side_effects=EFFECT),
    )(*ps, *lands, send, recv, *after)
    return list(res[:n]), list(res[n:])


def rs_chip_sum(qs, ps, ls, accs, layers, jc):
    n = len(qs)
    tiles = [(min(q.shape[1], SUM_ROWS), q.shape[2]) for q in qs]
    nts = [q.shape[1] // tr for q, (tr, _) in zip(qs, tiles)]

    def at(a, s):
        return jnp.minimum(s, nts[a] - 1)

    def body(jc_ref, *refs):
        jme = jc_ref[0]
        for a in range(n):
            q_ref, p_ref, o_ref = refs[a], refs[n + a], refs[len(refs) - n + a]
            own = p_ref[...].astype(F32)
            v = [jnp.where(jme == j, own, q_ref[j].astype(F32)) for j in range(4)]
            o_ref[...] = ((v[0] + v[1]) + v[2]) + v[3]

    in_specs = [pl.BlockSpec((4,) + tiles[a], lambda s, jr, a=a: (0, at(a, s), 0)) for a in range(n)]
    in_specs += [pl.BlockSpec((None,) + tiles[a], lambda s, jr, a=a: (jr[0], at(a, s), 0)) for a in range(n)]
    args, aliases = [jc, *qs, *ps], {}
    for a in range(n):
        if accs[a] is not None:
            aliases[len(args)] = a
            in_specs.append(ANY)
            args.append(accs[a])
    return pl.pallas_call(
        body, name="rs_chip_sum",
        grid_spec=pltpu.PrefetchScalarGridSpec(
            num_scalar_prefetch=1, grid=(max(nts),), in_specs=in_specs,
            out_specs=[pl.BlockSpec((None, None) + tiles[a], lambda s, jr, a=a: (ls[a], jr[1], at(a, s), 0))
                       for a in range(n)]),
        out_shape=[jax.ShapeDtypeStruct((layers[a], 2) + qs[a].shape[1:], F32) for a in range(n)],
        input_output_aliases=aliases,
        compiler_params=_cparams(("arbitrary",)),
    )(*args)


def rs_pair_gather(rs):
    n = len(rs)

    def body(*refs):
        outs = refs[n:2 * n]
        send, recv = refs[2 * n:]
        x, y, c, jme, sib, chips = _place()
        cps = [_remote(outs[a].at[:, c], outs[a].at[:, c], send, recv, a, sib) for a in range(n)]
        for cp in cps:
            cp.start()
        for a in range(n):
            slot = outs[a].at[:, 1 - c]
            _remote(slot, slot, send, recv, a, sib).wait_recv()
        for cp in cps:
            cp.wait_send()

    return pl.pallas_call(
        body, name="rs_pair_gather", in_specs=[ANY] * n, out_specs=[ANY] * n,
        out_shape=[jax.ShapeDtypeStruct(r.shape, r.dtype) for r in rs],
        input_output_aliases={a: a for a in range(n)},
        scratch_shapes=[pltpu.SemaphoreType.DMA((n,)), pltpu.SemaphoreType.DMA((n,))],
    )(*rs)


def _adamw_math(w, g, m, v):
    m = B1 * m + (1.0 - B1) * g
    v = B2 * v + (1.0 - B2) * (g * g)
    m_hat = m / (1.0 - B1 ** STEP)
    v_hat = v / (1.0 - B2 ** STEP)
    return -LR * (m_hat / (jnp.sqrt(v_hat) + AEPS) + WD * w), m, v


ADAMW_TILE = 512 * 1024


def adamw(w, g, m, v, name, with_grad=False):
    rows, cols = w.shape
    tr = next((t for t in (1024, 512, 256) if rows % t == 0 and t * cols <= ADAMW_TILE), rows)
    fn =(lambda wv, gv, mv, vv: (gv,) + _adamw_math(wv, gv, mv, vv)) if with_grad else _adamw_math
    return rw(fn, [(a, 0, cols) for a in (w, g, m, v)], [(cols, F32)] * (4 if with_grad else 3), name, rows, tr=tr)


def adamw_small(ws, gs, ms, vs):
    n = len(ws)

    def body(*refs):
        for a in range(n):
            res = _adamw_math(*[refs[k * n + a][...] for k in range(4)])
            for k in range(3):
                refs[(4 + k) * n + a][...] = res[k]

    res = pl.pallas_call(
        body, name="adamw_small", in_specs=[VM] * (4 * n), out_specs=[VM] * (3 * n),
        out_shape=[jax.ShapeDtypeStruct(w.shape, F32) for _ in range(3) for w in ws],
        compiler_params=pltpu.CompilerParams(vmem_limit_bytes=VMEM_LIMIT),
    )(*ws, *gs, *ms, *vs)
    return [(res[a], res[n + a], res[2 * n + a]) for a in range(n)]


WEIGHTS = ["norm_ab", "w_in_ab", "pool_w", "pool_scale", "w_out_ab", "norm_cd", "w_in_cd", "sgu_ln_g", "sgu_ln_b",
           "sgu_w", "sgu_b", "s5_a_re", "s5_a_im", "s5_log_dt", "s5_b_re", "s5_b_im", "s5_c_re", "s5_c_im", "s5_d",
           "glu_w1", "glu_w2", "w_out_cd", "norm_x", "w_xq", "w_xkv", "w_xo", "mem_norm", "final_norm"]
INPUTS = ["x", "mem"] + WEIGHTS + ["loss_target"] + ["m_" + n for n in WEIGHTS] + ["v_" + n for n in WEIGHTS]
BIG = ["w_in_ab", "w_out_ab", "w_in_cd", "w_out_cd", "w_xq", "w_xkv", "w_xo", "glu_w1", "glu_w2", "pool_w"]
COL_SHARDED = ("w_in_ab", "w_in_cd", "w_xkv")
SMALL = [n for n in WEIGHTS if n not in BIG]
SMALL_SHARDED = {"norm_cd": 256, "sgu_ln_g": 256, "sgu_ln_b": 256, "s5_d": 128}
PACK = 256 * 128


def _pack(arrs):
    flat = jnp.concatenate([a.reshape(-1) for a in arrs])
    pad = (-flat.shape[0]) % PACK
    return jnp.concatenate([flat, jnp.zeros((pad,), flat.dtype)]).reshape(-1, 128)


def _unpack(packed, shapes):
    flat, out, off = packed.reshape(-1), [], 0
    for s in shapes:
        n = 1
        for d in s:
            n *= d
        out.append(flat[off:off + n].reshape(s))
        off += n
    return out


LAYER_KEYS = (("w_in", "w_out", "pool_w", "w_xq", "w_xkv", "w_xo"),
              ("w_in", "w_out", "glu_w1", "glu_w2", "w_xq", "w_xkv", "w_xo"))


def _weight_of(key, layer):
    if key in ("w_xq", "w_xkv", "w_xo"):
        return key, layer, 4
    kind = "ab" if layer % 2 == 0 else "cd"
    return {"w_in": "w_in_" + kind, "w_out": "w_out_" + kind}.get(key, key), layer // 2, 2


def kernel(*args):
    a = dict(zip(INPUTS, args))
    x_i, y_i, c_i = lax.axis_index("x"), lax.axis_index("y"), lax.axis_index("c")
    j = 2 * x_i + y_i

    slab = jnp.concatenate([a["norm_cd"], a["sgu_ln_g"], a["sgu_ln_b"],
                            jnp.pad(a["s5_d"], ((0, 0), (0, 128)))], axis=0)
    gslab = allgather_small(slab)
    P = {n: a[n] for n in SMALL}
    for k, n in enumerate(("norm_cd", "sgu_ln_g", "sgu_ln_b", "s5_d")):
        wd = SMALL_SHARDED[n]
        P[n] = gslab[:, 2 * k:2 * k + 2, :wd].transpose(1, 0, 2).reshape(2, 4 * wd)

    def shards_of(layer):
        keys = sorted(k for k in LAYER_KEYS[layer % 2])
        out = []
        for k in keys:
            n, l, _ = _weight_of(k, layer)
            out.append(a[n][l].reshape(-1, a[n].shape[-1]).astype(BF16))
        return keys, out

    keys0, sh0 = shards_of(0)
    first = keys0.index("w_in")
    first_of = {0: allgather_first_start(
        sh0[first].reshape(2, sh0[first].shape[0] // 2, sh0[first].shape[1]), [gslab], "allgather_start_0in")}
    token = first_of[0][-1]
    started = {}
    for layer in (0, 1, 2, 3):
        keys, sh = (keys0, sh0) if layer == 0 else shards_of(layer)
        mix = [(k, s) for k, s in zip(keys, sh) if k != "w_in" and not k.startswith("w_x")]
        xat = [(k, s) for k, s in zip(keys, sh) if k.startswith("w_x")]
        if layer > 0:
            w_in = sh[keys.index("w_in")]
            first_of[layer] = allgather_first_start(w_in.reshape(2, w_in.shape[0] // 2, w_in.shape[1]), [token, gslab],
                                                    "allgather_start_%din" % layer)
            token = first_of[layer][-1]
        for tag, pk, ps in (("", *map(list, zip(*mix))), ("x", *map(list, zip(*xat)))):
            send, recv, ps, lands, token = allgather_start(ps, [token, gslab], "allgather_start_%d%s" % (layer, tag))
            started[(layer, tag)] = (pk, send, recv, ps, lands)

    def first_arrived(layer, after):
        f_send, f_recv, f_shard, f_land, _ = first_of[layer]
        g = allgather_forward(allgather_first_wait(f_send, f_recv, f_shard, f_land, after, "allgather_wait_%din" % layer))
        return views({"w_in": g.reshape(4, -1, g.shape[-1])})


    cidx = jnp.reshape(c_i, (1,)).astype(jnp.int32)
    jc = jnp.stack([j, c_i]).astype(jnp.int32)

    def views(g):
        W = {}
        for k, v in g.items():
            if k in ("w_in", "w_xkv"):
                W[k] = mcs(v)
            elif k == "pool_w":
                W[k] = v.reshape(4, 4, 64, 256).transpose(1, 0, 2, 3).reshape(4, 256, 256)
            elif k not in ("glu_w1", "glu_w2"):
                W[k] = m2(v.reshape(-1, v.shape[-1]))
        if "glu_w1" in g:
            W["w12"] = jnp.concatenate([g["glu_w1"].reshape(512, 512), g["glu_w2"].reshape(512, 512)], axis=1)
        return W

    w_in0 = first_arrived(0, [token])

    def arrived(layer, tag, after):
        keys, send, recv, sh, lands = started[(layer, tag)]
        return views(dict(zip(keys, allgather_wait(send, recv, sh, lands, after, "allgather_wait_%d%s" % (layer, tag)))))

    def weights_of(layer, x_in):
        W = dict(w_in0) if layer == 0 else first_arrived(layer, [x_in])
        W["more"] = lambda after: arrived(layer, "", after)
        W["more_x"] = lambda after: arrived(layer, "x", after)
        return W

    halves, pending = {}, {}

    def finish_pair(layer, after):
        keys, send, recv, flat, lands = halves.pop(layer)
        flat, got = rs_pair_wait(send, recv, flat, lands, after, "rs_pair_wait_%d" % layer)
        pair = rs_pair_sum(flat, got, cidx)
        send, recv, pair, lands, token = rs_chip_start(pair, "rs_chip_start_%d" % layer)
        pending[layer] = (keys, send, recv, pair, lands)
        return token

    def grads_done(layer, GW):
        keys = sorted(GW)
        flat = [GW[k].reshape(4, 2, GW[k].shape[1] // 2, GW[k].shape[2]) for k in keys]
        send, recv, flat, lands, token = rs_pair_start(flat, "rs_pair_start_%d" % layer)
        halves[layer] = (keys, send, recv, flat, lands)
        if layer + 1 in halves:
            token = token + finish_pair(layer + 1, token)
        return token[0:1, 0:1]

    loss, dx, G = local_step(a["x"][0], a["mem"][0], a["loss_target"][0], P, weights_of, grads_done)
    loss = lax.psum(loss[0, 0], ("x", "y", "c"))
    finish_pair(0, dx)
    outs = {}

    def update_big(names, red):
        for n, g in zip(names, rs_pair_gather([red[n] for n in names])):
            shp = a[n].shape
            g2 = g.reshape(-1, shp[-1])
            upd = adamw(a[n].reshape(g2.shape), g2, a["m_" + n].reshape(g2.shape), a["v_" + n].reshape(g2.shape),
                        "adamw_" + n, with_grad=True)
            outs[n] = tuple(t.reshape(shp) for t in upd)

    def reduce_layer(layer, red, after):
        keys, send, recv, pair, lands = pending[layer]
        pair, lands = rs_chip_wait(send, recv, pair, lands, after, "rs_chip_wait_%d" % layer)
        which = [_weight_of(k, layer) for k in keys]
        sums = rs_chip_sum(lands, pair, [l for _, l, _ in which], [red.get(n) for n, _, _ in which],
                           [layers for _, _, layers in which], jc)
        red.update(zip([n for n, _, _ in which], sums))

    red = {}
    for layer in (3, 2, 1):
        reduce_layer(layer, red, [dx])
    odd_only = [n for n in BIG if n.endswith("_cd") or n.startswith("glu")]
    update_big(odd_only, red)

    gfull = [jnp.stack(G[n]) if isinstance(G[n], list) else G[n] for n in SMALL]
    shapes = [g.shape for g in gfull]
    gsum = _unpack(allreduce_small(_pack(gfull)), shapes)
    gloc = []
    for n, g in zip(SMALL, gsum):
        if n in SMALL_SHARDED:
            g = lax.dynamic_slice_in_dim(g, j * SMALL_SHARDED[n], SMALL_SHARDED[n], axis=1)
        gloc.append(g)
    two = [(-1, a[n].shape[-1]) if a[n].ndim > 1 else (1, a[n].shape[0]) for n in SMALL]
    upds = adamw_small(*[[t.reshape(s) for t, s in zip(ts, two)]
                         for ts in ([a[n] for n in SMALL], gloc, [a["m_" + n] for n in SMALL],
                                    [a["v_" + n] for n in SMALL])])
    for n, g, upd in zip(SMALL, gloc, upds):
        outs[n] = (g,) + tuple(t.reshape(a[n].shape) for t in upd)

    behind = [outs[n][1] for n in odd_only + SMALL[-1:]] + [red[n] for n in BIG if n not in odd_only]
    reduce_layer(0, red, behind)
    update_big([n for n in BIG if n not in odd_only], red)

    res = [loss, dx[None]]
    for part in range(4):
        res += [outs[n][part] for n in WEIGHTS]
    return tuple(res)
```

```python
import math

import jax
import jax.numpy as jnp
from jax import lax
from jax.experimental import pallas as pl
from jax.experimental.pallas import tpu as pltpu

F32, BF16 = jnp.float32, jnp.bfloat16
S, D = 2048, 1024
MEM = 256
EPS = 1e-6
NEG = -1e30
QB = 128
PATTERNS = (1, 4, 16)
NG, NP, NH = 32, 64, 16
NS = NG * NP
LR, B1, B2, AEPS, WD, STEP = 0.001, 0.9, 0.999, 1e-08, 0.01, 10
MESHID = pl.DeviceIdType.MESH
VMEM_LIMIT = 56 * 1024 * 1024


def _cparams(sem):
    return pltpu.CompilerParams(dimension_semantics=sem, vmem_limit_bytes=VMEM_LIMIT)


def _sig(x):
    return 1.0 / (1.0 + jnp.exp(-x))


def _dot(a, b, dims):
    return lax.dot_general(a, b, (dims, ((), ())), preferred_element_type=F32)


def _nn(a, b):
    return _dot(a, b, ((1,), (0,)))


def _nt(a, b):
    return _dot(a, b, ((1,), (1,)))


def _tn(a, b):
    return _dot(a, b, ((0,), (0,)))


_DIMS = {"nn": ((1,), (0,)), "nt": ((1,), (1,)), "tn": ((0,), (0,))}


def _tile(dim, cc=None, cap=1024):
    for t in (2048, 1536, 1024, 768, 512, 384, 256, 128):
        if t <= cap and dim % t == 0 and (cc is None or cc % t == 0):
            return t
    return dim


MM_VMEM = 36 * 1024 * 1024


def _mm_tiles(m, n, k, ccm, ccn, cck, a_bytes, b_bytes, o_bytes):
    caps = [1024, 1024, 2048]
    while True:
        tm, tn, tk = _tile(m, ccm, caps[0]), _tile(n, ccn, caps[1]), _tile(k, cck, caps[2])
        need = 2 * (tm * tk * a_bytes + tk * tn * b_bytes + tm * tn * o_bytes) + (tm * tn * 4 if tk < k else 0)
        if need <= MM_VMEM:
            return tm, tn, tk
        if tk > 1024:
            caps[2] = tk // 2
        elif tn >= tm:
            caps[1] = tn // 2
        else:
            caps[0] = tm // 2


def m2(arr, col_off=0, ncols=None):
    rows, cols = arr.shape
    ncols = cols - col_off if ncols is None else ncols

    def spec(tr, tc, rc):
        assert col_off % tc == 0
        return pl.BlockSpec((tr, tc), lambda *g: (rc(*g)[0], rc(*g)[1] + col_off // tc))
    return (arr, rows, ncols, spec, None if col_off == 0 else col_off)


def mcs(arr):
    cs = arr.shape[2]

    def spec(tr, tc, rc):
        n = cs // tc
        return pl.BlockSpec((None, tr, tc), lambda *g: (rc(*g)[1] // n, rc(*g)[0], rc(*g)[1] % n))
    return (arr, arr.shape[1], 4 * cs, spec, cs)


def out2(rows, cols):
    def spec(tr, tc, rc):
        return pl.BlockSpec((tr, tc), lambda *g: tuple(rc(*g)))
    return ((rows, cols), spec, None)


def outcs(rows, cs):
    def spec(tr, tc, rc):
        n = cs // tc
        return pl.BlockSpec((None, tr, tc), lambda *g: (rc(*g)[1] // n, rc(*g)[0], rc(*g)[1] % n))
    return ((4, rows, cs), spec, cs)


def _both(a, b):
    if a is None:
        return b
    if b is None:
        return a
    return math.gcd(a, b)


def mm(a, b, mode, name, add=None, out=None, out_dtype=F32):
    a_arr, a_r, a_c, a_spec, a_cc = a
    b_arr, b_r, b_c, b_spec, b_cc = b
    if mode == "nn":
        m, k, n = a_r, a_c, b_c
        assert b_r == k
        ccm, cck, ccn = None, a_cc, b_cc
    elif mode == "nt":
        m, k, n = a_r, a_c, b_r
        assert b_c == k
        ccm, cck, ccn = None, _both(a_cc, b_cc), None
    else:
        m, k, n = a_c, a_r, b_c
        assert b_r == k
        ccm, cck, ccn = a_cc, None, b_cc
    out = out2(m, n) if out is None else out
    o_shape, o_spec, o_cc = out
    ccn = _both(ccn, o_cc)
    if add is not None:
        ccn = _both(ccn, add[4])
    o_bytes = jnp.dtype(out_dtype).itemsize + (0 if add is None else add[0].dtype.itemsize)
    tm, tn, tk = _mm_tiles(m, n, k, ccm, ccn, cck, a_arr.dtype.itemsize, b_arr.dtype.itemsize, o_bytes)
    nk = k // tk
    if mode == "nn":
        in_specs = [a_spec(tm, tk, lambda i, j, kk: (i, kk)), b_spec(tk, tn, lambda i, j, kk: (kk, j))]
    elif mode == "nt":
        in_specs = [a_spec(tm, tk, lambda i, j, kk: (i, kk)), b_spec(tn, tk, lambda i, j, kk: (j, kk))]
    else:
        in_specs = [a_spec(tk, tm, lambda i, j, kk: (kk, i)), b_spec(tk, tn, lambda i, j, kk: (kk, j))]
    args = [a_arr, b_arr]
    if add is not None:
        in_specs.append(add[3](tm, tn, lambda i, j, kk: (i, j)))
        args.append(add[0])
    return _mm_call(args, in_specs, o_spec(tm, tn, lambda i, j, kk: (i, j)), jax.ShapeDtypeStruct(o_shape, out_dtype),
                    mode, (m // tm, n // tn, nk), (tm, tn), add is not None, name)


def _mm_call(args, in_specs, out_spec, out_shape, mode, grid, tile, has_add, name):
    dims = _DIMS[mode]
    nk = grid[2]
    tm, tn = tile

    def body(*refs):
        a_ref, b_ref = refs[0], refs[1]
        add_ref = refs[2] if has_add else None
        prod = _dot(a_ref[...].astype(BF16), b_ref[...].astype(BF16), dims)
        if nk == 1:
            o_ref = refs[-1]
            if has_add:
                prod = prod + add_ref[...].astype(F32)
            o_ref[...] = prod.astype(o_ref.dtype)
            return
        o_ref, acc = refs[-2], refs[-1]
        kk = pl.program_id(2)

        @pl.when(kk == 0)
        def _():
            acc[...] = prod

        @pl.when(kk > 0)
        def _():
            acc[...] += prod

        @pl.when(kk == nk - 1)
        def _():
            r = acc[...]
            if has_add:
                r = r + add_ref[...].astype(F32)
            o_ref[...] = r.astype(o_ref.dtype)

    return pl.pallas_call(
        body, name=name, grid=grid, in_specs=in_specs, out_specs=out_spec, out_shape=out_shape,
        scratch_shapes=[pltpu.VMEM((tm, tn), F32)] if nk > 1 else [],
        compiler_params=_cparams(("parallel", "parallel", "arbitrary")),
    )(*args)


def mm_band(a, b, mode, name, grid, blocks, maps, out_shape, add=None, out_dtype=F32):
    in_specs = [pl.BlockSpec(blocks[0], maps[0]), pl.BlockSpec(blocks[1], maps[1])]
    args = [a, b]
    if add is not None:
        in_specs.append(pl.BlockSpec(blocks[2], maps[2]))
        args.append(add)
    return _mm_call(args, in_specs, pl.BlockSpec(blocks[2], maps[2]), jax.ShapeDtypeStruct(out_shape, out_dtype),
                    mode, grid, blocks[2], add is not None, name)


def rw(fn, ins, outs, name, rows, tr=None, consts=(), accs=()):
    tr = min(rows, 1024) if tr is None else tr
    n_in, n_c, n_o, n_a = len(ins), len(consts), len(outs), len(accs)
    in_specs = []
    for arr, off, width in ins:
        assert off % width == 0
        in_specs.append(pl.BlockSpec((tr, width), lambda i, o=off // width: (i, o)))
    for c in consts:
        in_specs.append(pl.BlockSpec(c.shape, lambda i: (0, 0)))
    out_specs = [pl.BlockSpec((tr, w), lambda i: (i, 0)) for w, _ in outs]
    out_specs += [pl.BlockSpec(s, lambda i: (0, 0)) for s in accs]
    out_shape = [jax.ShapeDtypeStruct((rows, w), dt) for w, dt in outs]
    out_shape += [jax.ShapeDtypeStruct(s, F32) for s in accs]

    def body(*refs):
        vals = [r[...] for r in refs[:n_in + n_c]]
        o_refs = refs[n_in + n_c:n_in + n_c + n_o]
        a_refs = refs[n_in + n_c + n_o:]
        res = fn(*vals)
        for r, v in zip(o_refs, res[:n_o]):
            r[...] = v.astype(r.dtype)
        if n_a:
            @pl.when(pl.program_id(0) == 0)
            def _():
                for r in a_refs:
                    r[...] = jnp.zeros_like(r)
            for r, v in zip(a_refs, res[n_o:]):
                r[...] += v

    res = pl.pallas_call(
        body, name=name, grid=(rows // tr,), in_specs=in_specs, out_specs=out_specs,
        out_shape=out_shape,
        compiler_params=_cparams(("arbitrary",) if n_a else ("parallel",)),
    )(*[a for a, _, _ in ins], *consts)
    return res


def _rstd(x):
    return lax.rsqrt(jnp.mean(x * x, axis=-1, keepdims=True) + EPS)


def rms_fwd(x, g, name):
    def fn(xv, gv):
        xv = xv.astype(F32)
        return (xv * _rstd(xv) * gv,)
    return rw(fn, [(x, 0, D)], [(D, BF16)], name, x.shape[0], consts=[g])[0]


def _rms_bwd_math(xv, dy, gv):
    r = _rstd(xv)
    dyg = dy * gv
    dx = r * dyg - xv * (r * r * r / D) * jnp.sum(dyg * xv, axis=-1, keepdims=True)
    dg = jnp.sum(dy * xv * r, axis=0, keepdims=True)
    return dx, dg


def rms_bwd(x, dy, dres, g, name):
    def fn(xv, dyv, drv, gv):
        dx, dg = _rms_bwd_math(xv, dyv, gv)
        return dx + drv, dg
    return rw(fn, [(x, 0, D), (dy, 0, D), (dres, 0, D)], [(D, F32)], name, x.shape[0],
              consts=[g], accs=[(1, D)])


def final_loss(x, tgt, g):
    def fn(xv, tv, gv):
        e = xv * _rstd(xv) * gv - tv
        loss = 0.5 * jnp.sum(jnp.sum(e * e, axis=-1, keepdims=True), axis=0, keepdims=True) / D
        dx, dg = _rms_bwd_math(xv, e / D, gv)
        return dx, loss, dg
    return rw(fn, [(x, 0, D), (tgt, 0, D)], [(D, F32)], "final_loss", S, consts=[g],
              accs=[(1, 1), (1, D)])


def _attn_bias(bias_ref):
    ii = lax.broadcasted_iota(jnp.int32, (2 * QB, 2 * QB), 0) % QB
    jj = lax.broadcasted_iota(jnp.int32, (2 * QB, 2 * QB), 1)
    dist = ii + QB - jj
    band = (dist >= 0) & (dist <= QB)
    bias_ref[1] = jnp.where(band, 0.0, NEG)
    bias_ref[0] = jnp.where(band & (jj >= QB), 0.0, NEG)


def _two_heads(x, m0):
    return jnp.concatenate([jnp.where(m0, x, 0.0), jnp.where(m0, 0.0, x)], axis=0)


def _per_head(col, m0):
    return jnp.where(m0, col[:QB], col[QB:])


def _attn_rows(idx, d):
    if d == 1:
        b = idx
        cur = pl.ds(pl.multiple_of(b * QB, QB), QB)
        prev = pl.ds(pl.multiple_of(jnp.maximum(b - 1, 0) * QB, QB), QB)
    else:
        r, b = lax.rem(idx, d), lax.div(idx, d)
        cur = pl.ds(r + b * (QB * d), QB, stride=d)
        prev = pl.ds(r + jnp.maximum(b - 1, 0) * (QB * d), QB, stride=d)
    return cur, prev, b


NBLK = S // QB
GROUP = 16
GROUP_FWD = 16


def _colblk(off):
    return pl.BlockSpec((S, 128), lambda hp: (0, off * 8 + hp))


def attn_fwd(z):
    def body(q_ref, k_ref, v_ref, g_ref, o_ref, l_ref, a_ref, os, ls, bias):
        _attn_bias(bias)
        m0 = lax.broadcasted_iota(jnp.int32, (1, 128), 1) < 64
        for pi, d in enumerate(PATTERNS):
            lone = S // d == QB

            def load(idx, d=d, lone=lone):
                cur, prev, b = _attn_rows(idx, d)
                if lone:
                    return cur, (q_ref[cur, :], None, k_ref[cur, :], None, v_ref[cur, :], bias[1, :, QB:])
                return cur, (q_ref[cur, :], k_ref[prev, :], k_ref[cur, :], v_ref[prev, :], v_ref[cur, :],
                             bias[jnp.minimum(b, 1)])

            def block(q, kp, kc, vp, vc, bs):
                qq = _two_heads(q * 0.125, m0).astype(BF16)
                k = (kc if kp is None else jnp.concatenate([kp, kc], axis=0)).astype(BF16)
                s = _nt(qq, k) + bs
                mx = jnp.max(s, axis=-1, keepdims=True)
                p = jnp.exp(s - mx)
                den = jnp.sum(p, axis=-1, keepdims=True)
                pb = p.astype(BF16)
                vv = _two_heads(vc if vp is None else jnp.concatenate([vp, vc], axis=0), m0).astype(BF16)
                o = _nn(jnp.concatenate([pb[:QB], pb[QB:]], axis=1), vv)
                return o * _per_head(1.0 / den, m0), _per_head(mx + jnp.log(den), m0)

            def step(i, carry, pi=pi):
                loaded = [load(i * GROUP_FWD + u) for u in range(GROUP_FWD)]
                done = [block(*vals) for _, vals in loaded]
                for (cur, _), (o, l) in zip(loaded, done):
                    os[pi, cur, :] = o
                    ls[pi, cur, :] = l
                return carry
            lax.fori_loop(0, NBLK // GROUP_FWD, step, 0)
        l1, l2, l3 = ls[0], ls[1], ls[2]
        mx = jnp.maximum(jnp.maximum(l1, l2), l3)
        e1, e2, e3 = jnp.exp(l1 - mx), jnp.exp(l2 - mx), jnp.exp(l3 - mx)
        tot = e1 + e2 + e3
        o = (os[0] * e1 + os[1] * e2 + os[2] * e3) / tot
        ga = g_ref[...]
        o_ref[...] = o
        l_ref[...] = mx + jnp.log(tot)
        a_ref[...] = (o * (ga * _sig(ga))).astype(a_ref.dtype)

    out = pl.BlockSpec((S, 128), lambda hp: (0, hp))
    return pl.pallas_call(
        body, name="attn_fwd", grid=(8,),
        in_specs=[_colblk(0), _colblk(1), _colblk(2), _colblk(3)], out_specs=[out] * 3,
        out_shape=[jax.ShapeDtypeStruct((S, D), F32), jax.ShapeDtypeStruct((S, D), F32),
                   jax.ShapeDtypeStruct((S, 2 * D), BF16)],
        scratch_shapes=[pltpu.VMEM((3, S, 128), F32), pltpu.VMEM((3, S, 128), F32),
                        pltpu.VMEM((2, 2 * QB, 2 * QB), F32)],
        compiler_params=_cparams(("parallel",)),
    )(z, z, z, z)


def attn_bwd(z, d_cat, o, lse):
    def body(q_ref, k_ref, v_ref, g_ref, da_ref, o_ref, l_ref, dq_ref, dk_ref, dv_ref, dg_ref, do_s, pr_s, bias):
        _attn_bias(bias)
        m0 = lax.broadcasted_iota(jnp.int32, (1, 128), 1) < 64
        ga = g_ref[...]
        sg = _sig(ga)
        da = da_ref[...]
        ov = o_ref[...]
        do = da * (ga * sg)
        dg_ref[...] = da * ov * (sg * (1.0 + ga * (1.0 - sg)))
        do_s[...] = do
        pr_s[...] = do * ov
        dq_ref[...] = jnp.zeros_like(dq_ref)
        dk_ref[...] = jnp.zeros_like(dk_ref)
        dv_ref[...] = jnp.zeros_like(dv_ref)
        for d in PATTERNS:
            lone = S // d == QB

            def load(idx, d=d, lone=lone):
                cur, prev, b = _attn_rows(idx, d)
                if lone:
                    return (cur, None), (q_ref[cur, :], None, k_ref[cur, :], None, v_ref[cur, :],
                                         do_s[cur, :], pr_s[cur, :], l_ref[cur, :], bias[1, :, QB:])
                return (cur, prev), (q_ref[cur, :], k_ref[prev, :], k_ref[cur, :], v_ref[prev, :], v_ref[cur, :],
                                     do_s[cur, :], pr_s[cur, :], l_ref[cur, :], bias[jnp.minimum(b, 1)])

            def block(q, kp, kc, vp, vc, dof, prod, lp, bs):
                qq = _two_heads(q * 0.125, m0).astype(BF16)
                kf = kc if kp is None else jnp.concatenate([kp, kc], axis=0)
                k = kf.astype(BF16)
                v = (vc if vp is None else jnp.concatenate([vp, vc], axis=0)).astype(BF16)
                dd = _two_heads(dof, m0).astype(BF16)
                lh = jnp.max(jnp.concatenate([jnp.where(m0, lp, -jnp.inf), jnp.where(m0, -jnp.inf, lp)], axis=0),
                             axis=-1, keepdims=True)
                delta = jnp.sum(_two_heads(prod, m0), axis=-1, keepdims=True)
                p = jnp.exp(_nt(qq, k) + bs - lh)
                ds = (p * (_nt(dd, v) - delta)).astype(BF16)
                dq = _nn(jnp.concatenate([ds[:QB], ds[QB:]], axis=1), _two_heads(kf, m0).astype(BF16))
                return dq * 0.125, _tn(ds, qq), _tn(p.astype(BF16), dd)

            def step(i, carry):
                loaded = [load(i * GROUP + u) for u in range(GROUP)]
                done = [block(*vals) for _, vals in loaded]
                for ((cur, prev), _), (dq, dk, dv) in zip(loaded, done):
                    dq_ref[cur, :] = dq_ref[cur, :] + dq
                    if prev is not None:
                        dk_ref[prev, :] = dk_ref[prev, :] + dk[:QB]
                        dv_ref[prev, :] = dv_ref[prev, :] + dv[:QB]
                    dk_ref[cur, :] = dk_ref[cur, :] + dk[-QB:]
                    dv_ref[cur, :] = dv_ref[cur, :] + dv[-QB:]
                return carry
            lax.fori_loop(0, NBLK // GROUP, step, 0)

    blk = pl.BlockSpec((S, 128), lambda hp: (0, hp))
    return pl.pallas_call(
        body, name="attn_bwd", grid=(8,),
        in_specs=[_colblk(0), _colblk(1), _colblk(2), _colblk(3), blk, blk, blk], out_specs=[blk] * 4,
        out_shape=[jax.ShapeDtypeStruct((S, D), F32)] * 4,
        scratch_shapes=[pltpu.VMEM((S, 128), F32), pltpu.VMEM((S, 128), F32), pltpu.VMEM((2, 2 * QB, 2 * QB), F32)],
        compiler_params=_cparams(("parallel",)),
    )(z, z, z, z, d_cat, o, lse)


def assemble_dz_even(parts):
    def body(*refs):
        o_ref = refs[-1]
        for j in range(6):
            o_ref[:, j * D:(j + 1) * D] = refs[j][...].astype(o_ref.dtype)
    tr = 512
    blk = pl.BlockSpec((tr, D), lambda i: (i, 0))
    return pl.pallas_call(
        body, name="assemble_dz_even", grid=(S // tr,), in_specs=[blk] * 6,
        out_specs=pl.BlockSpec((tr, 6 * D), lambda i: (i, 0)),
        out_shape=jax.ShapeDtypeStruct((S, 6 * D), BF16),
        compiler_params=_cparams(("parallel",)),
    )(*parts)


def _pool_window(g):
    return jnp.where(g == 0, 2.0, jnp.where(g == 1, 4.0, jnp.where(g == 2, 8.0, 16.0)))


def _pool_sel(g, levels):
    return jnp.where(g == 0, levels[0], jnp.where(g == 1, levels[1], jnp.where(g == 2, levels[2], levels[3])))


def _pool_fwd_math(v, g):
    t = lax.broadcasted_iota(jnp.int32, (S, 1), 0)
    s = v
    levels = []
    for k in (1, 2, 4, 8):
        s = s + jnp.where(t >= k, pltpu.roll(s, k, 0), 0.0)
        levels.append(s)
    cnt = jnp.minimum((t + 1).astype(F32), _pool_window(g))
    return _pool_sel(g, levels) / cnt - v, cnt


def pool_fwd(z, pw, ps, cat):
    def body(v_ref, g_ref, pw_ref, ps_ref, cat_ref, o_ref):
        g = pl.program_id(0)
        pooled, _ = _pool_fwd_math(v_ref[...], g)
        mixed = _nn(pooled.astype(BF16), pw_ref[...].astype(BF16))
        gb = g_ref[...]
        o_ref[...] = (mixed * ps_ref[...] * (gb * _sig(gb))).astype(o_ref.dtype)

    return pl.pallas_call(
        body, name="pool_fwd", grid=(4,),
        in_specs=[pl.BlockSpec((S, 256), lambda g: (0, 16 + g)),
                  pl.BlockSpec((S, 256), lambda g: (0, 20 + g)),
                  pl.BlockSpec((None, 256, 256), lambda g: (g, 0, 0)),
                  pl.BlockSpec((1, 256), lambda g: (0, g)), pl.BlockSpec(memory_space=pl.ANY)],
        out_specs=pl.BlockSpec((S, 256), lambda g: (0, 4 + g)),
        out_shape=jax.ShapeDtypeStruct((S, 2 * D), BF16),
        input_output_aliases={4: 0},
        compiler_params=_cparams(("parallel",)),
    )(z, z, pw, ps, cat)


def pool_bwd(z, d_cat, pw, ps):
    def body(v_ref, g_ref, d_ref, pw_ref, ps_ref, dv_ref, dg_ref, dpw_ref, dps_ref):
        g = pl.program_id(0)
        v = v_ref[...]
        pooled, cnt = _pool_fwd_math(v, g)
        pwb = pw_ref[...].astype(BF16)
        pb = pooled.astype(BF16)
        mixed = _nn(pb, pwb)
        gb = g_ref[...]
        sg = _sig(gb)
        dout = d_ref[...]
        sc = ps_ref[...]
        dg_ref[...] = dout * mixed * sc * (sg * (1.0 + gb * (1.0 - sg)))
        dms = dout * (gb * sg)
        dps_ref[...] = jnp.sum(dms * mixed, axis=0, keepdims=True)
        dmx = (dms * sc).astype(BF16)
        dpw_ref[...] = _tn(pb, dmx)
        dpooled = _nt(dmx, pwb)
        t = lax.broadcasted_iota(jnp.int32, (S, 1), 0)
        s = dpooled / cnt
        levels = []
        for k in (1, 2, 4, 8):
            s = s + jnp.where(t < S - k, pltpu.roll(s, S - k, 0), 0.0)
            levels.append(s)
        dv_ref[...] = _pool_sel(g, levels) - dpooled

    return pl.pallas_call(
        body, name="pool_bwd", grid=(4,),
        in_specs=[pl.BlockSpec((S, 256), lambda g: (0, 16 + g)),
                  pl.BlockSpec((S, 256), lambda g: (0, 20 + g)),
                  pl.BlockSpec((S, 256), lambda g: (0, 4 + g)),
                  pl.BlockSpec((None, 256, 256), lambda g: (g, 0, 0)),
                  pl.BlockSpec((1, 256), lambda g: (0, g))],
        out_specs=[pl.BlockSpec((S, 256), lambda g: (0, g)),
                   pl.BlockSpec((S, 256), lambda g: (0, g)),
                   pl.BlockSpec((None, 256, 256), lambda g: (g, 0, 0)),
                   pl.BlockSpec((1, 256), lambda g: (0, g))],
        out_shape=[jax.ShapeDtypeStruct((S, D), F32), jax.ShapeDtypeStruct((S, D), F32),
                   jax.ShapeDtypeStruct((4, 256, 256), F32), jax.ShapeDtypeStruct((1, D), F32)],
        compiler_params=_cparams(("parallel",)),
    )(z, z, d_cat, pw, ps)


CH = 128


def _sgu_common(v, lng, lnb, w_ref):
    mu = jnp.mean(v, axis=-1, keepdims=True)
    vc = v - mu
    rs = lax.rsqrt(jnp.mean(vc * vc, axis=-1, keepdims=True) + EPS)
    xhat = vc * rs
    vn = (xhat * lng + lnb).astype(BF16)
    ri = lax.broadcasted_iota(jnp.int32, (CH, CH), 0)
    ci = lax.broadcasted_iota(jnp.int32, (CH, CH), 1)
    tril = ri >= ci
    ws = [jnp.where(tril, w_ref[g], 0.0).astype(BF16) for g in range(4)]
    return xhat, rs, vn, tril, ws


def _zspec(off):
    return pl.BlockSpec((CH, D), lambda c: (c, off))


def _full(shape):
    return pl.BlockSpec(shape, lambda c: (0,) * len(shape))


def sgu_fwd(z, lng, lnb, w, bfull):
    def body(u_ref, v_ref, g_ref, lng_ref, lnb_ref, w_ref, b_ref, o_ref):
        _, _, vn, _, ws = _sgu_common(v_ref[...], lng_ref[...], lnb_ref[...], w_ref)
        for g in range(4):
            sl = slice(g * 256, (g + 1) * 256)
            mixed = _nn(ws[g], vn[:, sl]) + b_ref[:, sl]
            gc = g_ref[:, sl]
            o_ref[:, sl] = (u_ref[:, sl] * mixed * (gc * _sig(gc))).astype(o_ref.dtype)

    return pl.pallas_call(
        body, name="sgu_fwd", grid=(S // CH,),
        in_specs=[_zspec(0), _zspec(1), _zspec(2), _full((1, D)), _full((1, D)),
                  _full((4, CH, CH)), _full((CH, D))],
        out_specs=pl.BlockSpec((CH, D), lambda c: (c, 0)),
        out_shape=jax.ShapeDtypeStruct((S, D), BF16),
        compiler_params=_cparams(("parallel",)),
    )(z, z, z, lng, lnb, w, bfull)


def sgu_bwd(z, d_cat, lng, lnb, w, bfull):
    def body(u_ref, v_ref, g_ref, d_ref, lng_ref, lnb_ref, w_ref, b_ref,
             du_ref, dv_ref, dg_ref, dw_ref, db_ref, dlg_ref, dlb_ref):
        @pl.when(pl.program_id(0) == 0)
        def _():
            dw_ref[...] = jnp.zeros_like(dw_ref)
            db_ref[...] = jnp.zeros_like(db_ref)
            dlg_ref[...] = jnp.zeros_like(dlg_ref)
            dlb_ref[...] = jnp.zeros_like(dlb_ref)

        lng = lng_ref[...]
        xhat, rs, vn, tril, ws = _sgu_common(v_ref[...], lng, lnb_ref[...], w_ref)
        lane = lax.broadcasted_iota(jnp.int32, (1, 128), 1)
        db = jnp.zeros((CH, 128), F32)
        dvn_parts = []
        for g in range(4):
            sl = slice(g * 256, (g + 1) * 256)
            mixed = _nn(ws[g], vn[:, sl]) + b_ref[:, sl]
            gc = g_ref[:, sl]
            sg = _sig(gc)
            u = u_ref[:, sl]
            dc = d_ref[:, sl]
            du_ref[:, sl] = dc * mixed * (gc * sg)
            dg_ref[:, sl] = dc * u * mixed * (sg * (1.0 + gc * (1.0 - sg)))
            dmx = dc * u * (gc * sg)
            db = db + jnp.where(lane == g, jnp.sum(dmx, axis=-1, keepdims=True), 0.0)
            dmb = dmx.astype(BF16)
            dw_ref[g] += jnp.where(tril, _nt(dmb, vn[:, sl]), 0.0)
            dvn_parts.append(_tn(ws[g], dmb))
        db_ref[...] += db
        dvn = jnp.concatenate(dvn_parts, axis=1)
        dlb_ref[...] += jnp.sum(dvn, axis=0, keepdims=True)
        dlg_ref[...] += jnp.sum(dvn * xhat, axis=0, keepdims=True)
        dxh = dvn * lng
        dv_ref[...] = rs * (dxh - jnp.mean(dxh, axis=-1, keepdims=True)
                            - xhat * jnp.mean(dxh * xhat, axis=-1, keepdims=True))

    row = pl.BlockSpec((CH, D), lambda c: (c, 0))
    return pl.pallas_call(
        body, name="sgu_bwd", grid=(S // CH,),
        in_specs=[_zspec(0), _zspec(1), _zspec(2), row, _full((1, D)), _full((1, D)),
                  _full((4, CH, CH)), _full((CH, D))],
        out_specs=[row, row, row, _full((4, CH, CH)), _full((CH, 128)), _full((1, D)), _full((1, D))],
        out_shape=[jax.ShapeDtypeStruct((S, D), F32)] * 3
        + [jax.ShapeDtypeStruct((4, CH, CH), F32), jax.ShapeDtypeStruct((CH, 128), F32),
           jax.ShapeDtypeStruct((1, D), F32), jax.ShapeDtypeStruct((1, D), F32)],
        compiler_params=_cparams(("arbitrary",)),
    )(z, z, z, d_cat, lng, lnb, w, bfull)


TB = 256


def _cmul(ar, ai, br, bi):
    return ar * br - ai * bi, ar * bi + ai * br


def _scan_consts(ar, ai, reverse):
    a2 = _cmul(ar, ai, ar, ai)
    a4 = _cmul(*a2, *a2)
    row = lax.broadcasted_iota(jnp.int32, (8, NS), 0)

    def masked(k, p):
        keep = (row < 8 - k) if reverse else (row >= k)
        return jnp.where(keep, p[0], 0.0), jnp.where(keep, p[1], 0.0)
    pr = jnp.zeros((8, NS), F32)
    pi = jnp.zeros((8, NS), F32)
    cr, ci = ar, ai
    for r in range(8):
        sel = row == (7 - r if reverse else r)
        pr = jnp.where(sel, cr, pr)
        pi = jnp.where(sel, ci, pi)
        cr, ci = _cmul(cr, ci, ar, ai)
    return (masked(1, (ar, ai)), masked(2, a2), masked(4, a4)), (pr, pi), row


def scan_fwd(bu, abr, abi):
    def body(bu_ref, ar_ref, ai_ref, h_ref, car, cai):
        @pl.when(pl.program_id(0) == 0)
        def _():
            car[...] = jnp.zeros_like(car)
            cai[...] = jnp.zeros_like(cai)

        pows, (pr, pi), row = _scan_consts(ar_ref[...], ai_ref[...], False)

        def tile(t, carry):
            c_r, c_i = carry
            rows = pl.ds(pl.multiple_of(t * 8, 8), 8)
            xr = bu_ref[rows, 0:NS]
            xi = bu_ref[rows, NS:2 * NS]
            for k, (kr, ki) in zip((1, 2, 4), pows):
                sr = pltpu.roll(xr, k, 0)
                si = pltpu.roll(xi, k, 0)
                xr, xi = xr + kr * sr - ki * si, xi + kr * si + ki * sr
            xr, xi = xr + pr * c_r - pi * c_i, xi + pr * c_i + pi * c_r
            h_ref[rows, 0:NS] = xr
            h_ref[rows, NS:2 * NS] = xi
            return (jnp.broadcast_to(xr[7:8, :], (8, NS)), jnp.broadcast_to(xi[7:8, :], (8, NS)))

        c_r, c_i = lax.fori_loop(0, TB // 8, tile, (car[...], cai[...]))
        car[...] = c_r
        cai[...] = c_i

    return pl.pallas_call(
        body, name="s5_scan_fwd", grid=(S // TB,),
        in_specs=[pl.BlockSpec((TB, 2 * NS), lambda i: (i, 0)),
                  pl.BlockSpec((1, NS), lambda i: (0, 0)), pl.BlockSpec((1, NS), lambda i: (0, 0))],
        out_specs=pl.BlockSpec((TB, 2 * NS), lambda i: (i, 0)),
        out_shape=jax.ShapeDtypeStruct((S, 2 * NS), F32),
        scratch_shapes=[pltpu.VMEM((8, NS), F32), pltpu.VMEM((8, NS), F32)],
        compiler_params=_cparams(("arbitrary",)),
    )(bu, abr, abi)


def scan_bwd(eta, h, abr, abi):
    nt = S // TB

    def body(e_ref, h_ref, ar_ref, ai_ref, l_ref, da_ref, car, cai):
        @pl.when(pl.program_id(0) == 0)
        def _():
            car[...] = jnp.zeros_like(car)
            cai[...] = jnp.zeros_like(cai)
            da_ref[...] = jnp.zeros_like(da_ref)

        pows, (pr, pi), row = _scan_consts(ar_ref[...], -ai_ref[...], True)

        def tile(tt, carry):
            c_r, c_i, acr, aci = carry
            t = TB // 8 - 1 - tt
            rows = pl.ds(pl.multiple_of(t * 8, 8), 8)
            xr = e_ref[rows, 0:NS]
            xi = e_ref[rows, NS:2 * NS]
            for k, (kr, ki) in zip((1, 2, 4), pows):
                sr = pltpu.roll(xr, 8 - k, 0)
                si = pltpu.roll(xi, 8 - k, 0)
                xr, xi = xr + kr * sr - ki * si, xi + kr * si + ki * sr
            xr, xi = xr + pr * c_r - pi * c_i, xi + pr * c_i + pi * c_r
            l_ref[rows, 0:NS] = xr
            l_ref[rows, NS:2 * NS] = xi
            nr = jnp.where(row < 7, pltpu.roll(xr, 7, 0), c_r)
            ni = jnp.where(row < 7, pltpu.roll(xi, 7, 0), c_i)
            hr = h_ref[rows, 0:NS]
            hi = h_ref[rows, NS:2 * NS]
            acr = acr + hr * nr + hi * ni
            aci = aci + hr * ni - hi * nr
            return (jnp.broadcast_to(xr[0:1, :], (8, NS)), jnp.broadcast_to(xi[0:1, :], (8, NS)), acr, aci)

        zero = jnp.zeros((8, NS), F32)
        c_r, c_i, acr, aci = lax.fori_loop(0, TB // 8, tile, (car[...], cai[...], zero, zero))
        car[...] = c_r
        cai[...] = c_i
        da_ref[:, 0:NS] += acr
        da_ref[:, NS:2 * NS] += aci

    rev = pl.BlockSpec((TB, 2 * NS), lambda i: (nt - 1 - i, 0))
    return pl.pallas_call(
        body, name="s5_scan_bwd", grid=(nt,),
        in_specs=[rev, rev, pl.BlockSpec((1, NS), lambda i: (0, 0)), pl.BlockSpec((1, NS), lambda i: (0, 0))],
        out_specs=[rev, pl.BlockSpec((8, 2 * NS), lambda i: (0, 0))],
        out_shape=[jax.ShapeDtypeStruct((S, 2 * NS), F32), jax.ShapeDtypeStruct((8, 2 * NS), F32)],
        scratch_shapes=[pltpu.VMEM((8, NS), F32), pltpu.VMEM((8, NS), F32)],
        compiler_params=_cparams(("arbitrary",)),
    )(eta, h, abr, abi)


GC = 0.7978845608028654
GA = 0.044715


def s5_post(hc, z, dskip):
    def fn(hv, xd, dv):
        y = hv + dv * xd
        return y, 0.5 * y * (1.0 + jnp.tanh(GC * (y + GA * y * y * y)))
    return rw(fn, [(hc, 0, 512), (z, 3072, 512)], [(512, F32), (512, BF16)], "s5_post", S, consts=[dskip])


def s5_post_bwd(dyg, ypre, z, dskip):
    def fn(dy, y, xd, dv):
        th = jnp.tanh(GC * (y + GA * y * y * y))
        dg = 0.5 * (1.0 + th) + 0.5 * y * (1.0 - th * th) * GC * (1.0 + 3.0 * GA * y * y)
        dyp = dy * dg
        return dyp, dyp * dv, jnp.sum(dyp * xd, axis=0, keepdims=True)
    return rw(fn, [(dyg, 0, 512), (ypre, 0, 512), (z, 3072, 512)], [(512, BF16), (512, F32)],
              "s5_post_bwd", S, consts=[dskip], accs=[(1, 512)])


def glu_fwd(t, z, c_out):
    def fn(t1, t2, gd, co):
        return (jnp.concatenate([co, (t1 * _sig(t2) * (gd * _sig(gd))).astype(BF16)], axis=1),)
    return rw(fn, [(t, 0, 512), (t, 512, 512), (z, 3584, 512), (c_out, 0, D)], [(D + 512, BF16)], "glu_fwd", S)[0]


def glu_bwd(t, z, d_cat):
    def fn(t1, t2, gd, dd):
        s2, sg = _sig(t2), _sig(gd)
        sl = gd * sg
        return (jnp.concatenate([dd * s2 * sl, dd * t1 * s2 * (1.0 - s2) * sl], axis=1),
                dd * t1 * s2 * (sg * (1.0 + gd * (1.0 - sg))))
    return rw(fn, [(t, 0, 512), (t, 512, 512), (z, 3584, 512), (d_cat, 1024, 512)],
              [(D, BF16), (512, F32)], "glu_bwd", S)


def assemble_dz_odd(du, dv, dgc, dxd, dgd):
    def body(a, b, c, d, e, o_ref):
        o_ref[:, 0:D] = a[...].astype(BF16)
        o_ref[:, D:2 * D] = b[...].astype(BF16)
        o_ref[:, 2 * D:3 * D] = c[...].astype(BF16)
        o_ref[:, 3 * D:3 * D + 512] = d[...].astype(BF16)
        o_ref[:, 3 * D + 512:4 * D] = e[...].astype(BF16)
    tr = 512
    blk = pl.BlockSpec((tr, D), lambda i: (i, 0))
    half = pl.BlockSpec((tr, 512), lambda i: (i, 0))
    return pl.pallas_call(
        body, name="assemble_dz_odd", grid=(S // tr,), in_specs=[blk, blk, blk, half, half],
        out_specs=pl.BlockSpec((tr, 4 * D), lambda i: (i, 0)),
        out_shape=jax.ShapeDtypeStruct((S, 4 * D), BF16),
        compiler_params=_cparams(("parallel",)),
    )(du, dv, dgc, dxd, dgd)


TQ = 1024


def _xattn_probs(qh, kh):
    s = _nt(qh, kh) * 0.0625
    p = jnp.exp(s - jnp.max(s, axis=-1, keepdims=True))
    return p / jnp.sum(p, axis=-1, keepdims=True)


def xattn_fwd(q, kv):
    def body(q_ref, kv_ref, o_ref):
        outs = []
        for h in range(4):
            sl = slice(h * 256, (h + 1) * 256)
            p = _xattn_probs(q_ref[:, sl].astype(BF16), kv_ref[:, sl].astype(BF16))
            vh = kv_ref[:, D + h * 256:D + (h + 1) * 256].astype(BF16)
            outs.append((sl, _nn(p.astype(BF16), vh)))
        for sl, o in outs:
            o_ref[:, sl] = o.astype(o_ref.dtype)

    return pl.pallas_call(
        body, name="xattn_fwd", grid=(S // TQ,),
        in_specs=[pl.BlockSpec((TQ, D), lambda i: (i, 0)), pl.BlockSpec((MEM, 2 * D), lambda i: (0, 0))],
        out_specs=pl.BlockSpec((TQ, D), lambda i: (i, 0)),
        out_shape=jax.ShapeDtypeStruct((S, D), BF16),
        compiler_params=_cparams(("parallel",)),
    )(q, kv)


def xattn_bwd(q, kv, d_o):
    def body(q_ref, kv_ref, do_ref, dq_ref, dkv_ref):
        @pl.when(pl.program_id(0) == 0)
        def _():
            dkv_ref[...] = jnp.zeros_like(dkv_ref)

        done = []
        for h in range(4):
            sl = slice(h * 256, (h + 1) * 256)
            vs = slice(D + h * 256, D + (h + 1) * 256)
            qh = q_ref[:, sl].astype(BF16)
            kh = kv_ref[:, sl].astype(BF16)
            vh = kv_ref[:, vs].astype(BF16)
            doh = do_ref[:, sl].astype(BF16)
            p = _xattn_probs(qh, kh)
            dp = _nt(doh, vh)
            ds = (p * (dp - jnp.sum(p * dp, axis=-1, keepdims=True)) * 0.0625).astype(BF16)
            done.append((sl, vs, _nn(ds, kh), _tn(ds, qh), _tn(p.astype(BF16), doh)))
        for sl, vs, dq, dk, dv in done:
            dq_ref[:, sl] = dq.astype(dq_ref.dtype)
            dkv_ref[:, sl] += dk
            dkv_ref[:, vs] += dv

    return pl.pallas_call(
        body, name="xattn_bwd", grid=(S // TQ,),
        in_specs=[pl.BlockSpec((TQ, D), lambda i: (i, 0)), pl.BlockSpec((MEM, 2 * D), lambda i: (0, 0)),
                  pl.BlockSpec((TQ, D), lambda i: (i, 0))],
        out_specs=[pl.BlockSpec((TQ, D), lambda i: (i, 0)), pl.BlockSpec((MEM, 2 * D), lambda i: (0, 0))],
        out_shape=[jax.ShapeDtypeStruct((S, D), BF16), jax.ShapeDtypeStruct((MEM, 2 * D), F32)],
        compiler_params=_cparams(("arbitrary",)),
    )(q, kv, d_o)


def _s5_disc(a_re, a_im, log_dt, b_re, b_im):
    dt = jnp.exp(log_dt)[:, None]
    mag = jnp.exp(dt * a_re)
    abr = mag * jnp.cos(dt * a_im)
    abi = mag * jnp.sin(dt * a_im)
    nr, ni = abr - 1.0, abi
    inv = 1.0 / (a_re * a_re + a_im * a_im)
    cr = (nr * a_re + ni * a_im) * inv
    ci = (ni * a_re - nr * a_im) * inv
    bbr = cr[..., None] * b_re - ci[..., None] * b_im
    bbi = cr[..., None] * b_im + ci[..., None] * b_re
    return abr, abi, bbr, bbi


VM = pl.BlockSpec(memory_space=pltpu.VMEM)


def s5_embed(bt_re, bt_im, ct_re, ct_im):
    def body(br, bi, cr, ci, b_ref, c_ref):
        b_ref[...] = jnp.zeros_like(b_ref)
        c_ref[...] = jnp.zeros_like(c_ref)
        for g in range(NG):
            rows, cols = slice(g * NH, (g + 1) * NH), slice(g * NP, (g + 1) * NP)
            b_ref[rows, cols] = br[g]
            b_ref[rows, NS + g * NP:NS + (g + 1) * NP] = bi[g]
            c_ref[cols, rows] = cr[g]
            c_ref[NS + g * NP:NS + (g + 1) * NP, rows] = -ci[g]

    return pl.pallas_call(
        body, name="s5_embed", in_specs=[VM] * 4, out_specs=[VM] * 2,
        out_shape=[jax.ShapeDtypeStruct((NG * NH, 2 * NS), F32), jax.ShapeDtypeStruct((2 * NS, NG * NH), F32)],
        compiler_params=pltpu.CompilerParams(vmem_limit_bytes=VMEM_LIMIT),
    )(bt_re, bt_im, ct_re, ct_im)


def s5_extract(gb, gc):
    def body(gb_ref, gc_ref, br, bi, cr, ci):
        for g in range(NG):
            rows, cols = slice(g * NH, (g + 1) * NH), slice(g * NP, (g + 1) * NP)
            br[g] = gb_ref[rows, cols]
            bi[g] = gb_ref[rows, NS + g * NP:NS + (g + 1) * NP]
            cr[g] = gc_ref[cols, rows]
            ci[g] = -gc_ref[NS + g * NP:NS + (g + 1) * NP, rows]

    return pl.pallas_call(
        body, name="s5_extract", in_specs=[VM] * 2, out_specs=[VM] * 4,
        out_shape=[jax.ShapeDtypeStruct((NG, NH, NP), F32)] * 2 + [jax.ShapeDtypeStruct((NG, NP, NH), F32)] * 2,
        compiler_params=pltpu.CompilerParams(vmem_limit_bytes=VMEM_LIMIT),
    )(gb, gc)


HC, HS = NG * NH // 2, NS // 2
TS = 1024


def s5_to_states(x, w, mode, name, z_off=0):
    if mode == "nn":
        wb, wm = (HC, HS), lambda i, j, kk: (j % 2, j)
    else:
        wb, wm = (HS, HC), lambda i, j, kk: (j, j % 2)
    return mm_band(x, w, mode, name, (S // TS, 4, 1), ((TS, HC), wb, (TS, HS)),
                   (lambda i, j, kk: (i, z_off + j % 2), wm, lambda i, j, kk: (i, j)), (S, 2 * NS))


def s5_to_channels(x, w, mode, name, add=None):
    if mode == "nn":
        wb, wm = (HS, HC), lambda i, j, kk: (j + 2 * kk, j)
    else:
        wb, wm = (HC, HS), lambda i, j, kk: (j, j + 2 * kk)
    return mm_band(x, w, mode, name, (S // TS, 2, 2), ((TS, HS), wb, (TS, HC)),
                   (lambda i, j, kk: (i, j + 2 * kk), wm, lambda i, j, kk: (i, j)), (S, NG * NH), add=add)


def s5_outer(a, b, name, states_first, z_off=0):
    if states_first:
        return mm_band(a, b, "tn", name, (4, 1, 1), ((S, HS), (S, HC), (HS, HC)),
                       (lambda i, j, kk: (0, i), lambda i, j, kk: (0, i % 2), lambda i, j, kk: (i, i % 2)),
                       (2 * NS, NG * NH))
    return mm_band(a, b, "tn", name, (1, 4, 1), ((S, HC), (S, HS), (HC, HS)),
                   (lambda i, j, kk: (0, z_off + j % 2), lambda i, j, kk: (0, j), lambda i, j, kk: (j % 2, j)),
                   (NG * NH, 2 * NS))


def _fwd_even(i, x, P, W):
    hn = rms_fwd(x, P["norm_ab"][i:i + 1], "rms_ab_fwd")
    z = mm(m2(hn), W["w_in"], "nn", "in_ab")
    o, lse, cat = attn_fwd(z)
    if "more" in W:
        W.update(W.pop("more")(cat))
    cat = pool_fwd(z, W["pool_w"], P["pool_scale"][i:i + 1], cat)
    x_mid = mm(m2(cat), W["w_out"], "nn", "out_ab", add=m2(x))
    return x_mid, dict(x=x, hn=hn, z=z, o=o, lse=lse, cat=cat)


def _bwd_even(i, dx_mid, sv, P, W, G, GW):
    z = sv["z"]
    d_cat = mm(m2(dx_mid), W["w_out"], "nt", "out_ab_dx")
    GW["w_out"] = mm(m2(sv["cat"]), m2(dx_mid), "tn", "out_ab_dw").reshape(4, 512, D)
    dq, dk, dv, dga = attn_bwd(z, d_cat, sv["o"], sv["lse"])
    dvb, dgb, dpw, dps = pool_bwd(z, d_cat, W["pool_w"], P["pool_scale"][i:i + 1])
    GW["pool_w"] = dpw.reshape(4, 4, 64, 256).transpose(1, 0, 2, 3).reshape(4, 256, 256)
    G["pool_scale"][i] = dps[0]
    d_z = assemble_dz_even((dq, dk, dv, dga, dvb, dgb))
    d_hn = mm(m2(d_z), W["w_in"], "nt", "in_ab_dx")
    GW["w_in"] = mm(m2(sv["hn"]), m2(d_z), "tn", "in_ab_dw", out=outcs(D, 1536))
    return d_hn, P["norm_ab"][i:i + 1], "norm_ab", "rms_ab_bwd"


def _fwd_odd(i, x, P, W):
    hn = rms_fwd(x, P["norm_cd"][i:i + 1], "rms_cd_fwd")
    z = mm(m2(hn), W["w_in"], "nn", "in_cd")
    bfull = jnp.repeat(P["sgu_b"][i].T, 256, axis=1)
    c_out = sgu_fwd(z, P["sgu_ln_g"][i:i + 1], P["sgu_ln_b"][i:i + 1], P["sgu_w"][i], bfull)
    disc, disc_vjp = jax.vjp(_s5_disc, P["s5_a_re"][i], P["s5_a_im"][i], P["s5_log_dt"][i],
                             P["s5_b_re"][i], P["s5_b_im"][i])
    abr, abi, bbr, bbi = disc
    bbd, cbd = s5_embed(bbr.transpose(0, 2, 1), bbi.transpose(0, 2, 1),
                        P["s5_c_re"][i].transpose(0, 2, 1), P["s5_c_im"][i].transpose(0, 2, 1))
    abr, abi = abr.reshape(1, NS), abi.reshape(1, NS)
    bu = s5_to_states(z, bbd, "nn", "s5_bu", z_off=3072 // HC)
    h = scan_fwd(bu, abr, abi)
    hc = s5_to_channels(h, cbd, "nn", "s5_hc")
    dskip = P["s5_d"][i:i + 1]
    ypre, yg = s5_post(hc, z, dskip)
    if "more" in W:
        W.update(W.pop("more")(yg))
    w12 = W["w12"]
    t = mm(m2(yg), m2(w12), "nn", "glu_t")
    cat = glu_fwd(t, z, c_out)
    x_mid = mm(m2(cat), W["w_out"], "nn", "out_cd", add=m2(x))
    return x_mid, dict(x=x, hn=hn, z=z, bfull=bfull, disc_vjp=disc_vjp, bbd=bbd, cbd=cbd, abr=abr,
                       abi=abi, h=h, ypre=ypre, yg=yg, w12=w12, t=t, cat=cat, dskip=dskip)


def _bwd_odd(i, dx_mid, sv, P, W, G, GW):
    z = sv["z"]
    d_cat = mm(m2(dx_mid), W["w_out"], "nt", "out_cd_dx")
    GW["w_out"] = mm(m2(sv["cat"]), m2(dx_mid), "tn", "out_cd_dw").reshape(4, 384, D)
    du, dv, dgc, dws, dbs, dlg, dlb = sgu_bwd(z, d_cat, P["sgu_ln_g"][i:i + 1], P["sgu_ln_b"][i:i + 1],
                                               P["sgu_w"][i], sv["bfull"])
    G["sgu_w"][i], G["sgu_b"][i] = dws, dbs[:, :4].T
    G["sgu_ln_g"][i], G["sgu_ln_b"][i] = dlg[0], dlb[0]
    dt, dgd = glu_bwd(sv["t"], z, d_cat)
    gw12 = mm(m2(sv["yg"]), m2(dt), "tn", "glu_dw")
    GW["glu_w1"] = gw12[:, :512].reshape(4, 128, 512)
    GW["glu_w2"] = gw12[:, 512:].reshape(4, 128, 512)
    dyg = mm(m2(dt), m2(sv["w12"]), "nt", "glu_dx")
    dypre, dxd1, dd = s5_post_bwd(dyg, sv["ypre"], z, sv["dskip"])
    G["s5_d"][i] = dd[0]
    gcbd = s5_outer(sv["h"], dypre, "s5_dc", states_first=True)
    eta = s5_to_states(dypre, sv["cbd"], "nt", "s5_eta")
    lam, dacc = scan_bwd(eta, sv["h"], sv["abr"], sv["abi"])
    gbbd = s5_outer(z, lam, "s5_db", states_first=False, z_off=3072 // HC)
    dxd = s5_to_channels(lam, sv["bbd"], "nt", "s5_dx", add=dxd1)
    dacc = jnp.sum(dacc, axis=0)
    dbt_re, dbt_im, dct_re, dct_im = s5_extract(gbbd, gcbd)
    G["s5_c_re"][i], G["s5_c_im"][i] = dct_re.transpose(0, 2, 1), dct_im.transpose(0, 2, 1)
    d_bbr, d_bbi = dbt_re.transpose(0, 2, 1), dbt_im.transpose(0, 2, 1)
    (G["s5_a_re"][i], G["s5_a_im"][i], G["s5_log_dt"][i], G["s5_b_re"][i], G["s5_b_im"][i]) = sv["disc_vjp"](
        (dacc[:NS].reshape(NG, NP), dacc[NS:].reshape(NG, NP), d_bbr, d_bbi))
    d_z = assemble_dz_odd(du, dv, dgc, dxd, dgd)
    d_hn = mm(m2(d_z), W["w_in"], "nt", "in_cd_dx")
    GW["w_in"] = mm(m2(sv["hn"]), m2(d_z), "tn", "in_cd_dw", out=outcs(D, 1024))
    return d_hn, P["norm_cd"][i:i + 1], "norm_cd", "rms_cd_bwd"


def _fwd_x(l, x, mem_n, P, W):
    if "more_x" in W:
        W.update(W.pop("more_x")(x))
    hx = rms_fwd(x, P["norm_x"][l:l + 1], "rms_x_fwd")
    q = mm(m2(hx), W["w_xq"], "nn", "xq", out_dtype=BF16)
    kv = mm(m2(mem_n), W["w_xkv"], "nn", "xkv", out_dtype=BF16)
    ox = xattn_fwd(q, kv)
    x_out = mm(m2(ox), W["w_xo"], "nn", "xo", add=m2(x))
    return x_out, dict(x=x, hx=hx, q=q, kv=kv, ox=ox)


def _bwd_x(l, dx_out, sv, mem_n, d_memn, P, W, G, GW):
    d_ox = mm(m2(dx_out), W["w_xo"], "nt", "xo_dx", out_dtype=BF16)
    GW["w_xo"] = mm(m2(sv["ox"]), m2(dx_out), "tn", "xo_dw").reshape(4, 256, D)
    dq, dkv = xattn_bwd(sv["q"], sv["kv"], d_ox)
    GW["w_xq"] = mm(m2(sv["hx"]), m2(dq), "tn", "xq_dw").reshape(4, 256, D)
    d_hx = mm(m2(dq), W["w_xq"], "nt", "xq_dx")
    GW["w_xkv"] = mm(m2(mem_n), m2(dkv), "tn", "xkv_dw", out=outcs(D, 512))
    d_memn = mm(m2(dkv), W["w_xkv"], "nt", "xkv_dx", add=None if d_memn is None else m2(d_memn))
    dx, dg = rms_bwd(sv["x"], d_hx, dx_out, P["norm_x"][l:l + 1], "rms_x_bwd")
    G["norm_x"][l] = dg[0]
    return dx, d_memn


SMALL_LAYERS = (("norm_ab", 2), ("pool_scale", 2), ("norm_cd", 2), ("sgu_ln_g", 2), ("sgu_ln_b", 2), ("sgu_w", 2),
                ("sgu_b", 2), ("s5_a_re", 2), ("s5_a_im", 2), ("s5_log_dt", 2), ("s5_b_re", 2), ("s5_b_im", 2),
                ("s5_c_re", 2), ("s5_c_im", 2), ("s5_d", 2), ("norm_x", 4))


def local_step(x, mem, tgt, P, weights_of, grads_done):
    G = {k: [None] * n for k, n in SMALL_LAYERS}
    mem_g = P["mem_norm"].reshape(1, D)
    mem_n = rms_fwd(mem, mem_g, "rms_mem_fwd")
    saved = []
    for layer in range(4):
        i = layer // 2
        W = weights_of(layer, x)
        x, sv_m = (_fwd_even if layer % 2 == 0 else _fwd_odd)(i, x, P, W)
        x, sv_x = _fwd_x(layer, x, mem_n, P, W)
        saved.append((sv_m, sv_x, W))
    dx, loss, dgf = final_loss(x, tgt, P["final_norm"].reshape(1, D))
    G["final_norm"] = dgf[0]
    d_memn = None
    for layer in reversed(range(4)):
        i = layer // 2
        sv_m, sv_x, W = saved[layer]
        GW = {}
        dx_mid, d_memn = _bwd_x(layer, dx, sv_x, mem_n, d_memn, P, W, G, GW)
        d_hn, g, key, name = (_bwd_even if layer % 2 == 0 else _bwd_odd)(i, dx_mid, sv_m, P, W, G, GW)
        token = grads_done(layer, GW)
        if token is not None:
            g = g + token
        dx, dg = rms_bwd(sv_m["x"], d_hn, dx_mid, g, name)
        G[key][i] = dg[0]
    _, dgm = rms_bwd(mem, d_memn, d_memn, mem_g, "rms_mem_bwd")
    G["mem_norm"] = dgm[0]
    return loss, dx, G


ANY = pl.BlockSpec(memory_space=pl.ANY)


def _place():
    x, y, c = lax.axis_index("x"), lax.axis_index("y"), lax.axis_index("c")
    chips = [(1 - x, y), (x, 1 - y), (1 - x, 1 - y)]
    return x, y, c, 2 * x + y, (x, y, 1 - c), chips


def _remote(src, dst, send, recv, k, dev):
    return pltpu.make_async_remote_copy(src_ref=src, dst_ref=dst, send_sem=send.at[k], recv_sem=recv.at[k],
                                        device_id=dev, device_id_type=MESHID)


HBM = pl.BlockSpec(memory_space=pltpu.HBM)
SEM = pl.BlockSpec(memory_space=pltpu.SEMAPHORE)
EFFECT = pltpu.SideEffectType.DATAFLOW_SIDE_EFFECTING


def _hbm(t):
    return pltpu.with_memory_space_constraint(t, pltpu.HBM)


def _first_copies(shard, land, send, recv, arriving):
    x, y, c, jme, sib, chips = _place()
    out = []
    for k, chip in enumerate(chips):
        slot = 2 * chip[0] + chip[1] if arriving else jme
        out.append(_remote(shard.at[c], land.at[slot, c], send, recv, k, (*chip, c)))
    out.append(_remote(shard, land.at[jme], send, recv, 3, sib))
    return out


def allgather_first_start(shard, after, name):
    def body(s_ref, l_ref, *rest):
        send, recv, token = rest[len(after)], rest[len(after) + 1], rest[-1]
        for cp in _first_copies(s_ref, l_ref, send, recv, False):
            cp.start()
        token[...] = jnp.zeros_like(token)

    land = (4,) + shard.shape
    return pl.pallas_call(
        body, name=name,
        out_shape=(pltpu.SemaphoreType.DMA((4,)), pltpu.SemaphoreType.DMA((4,)), pltpu.HBM(shard.shape, shard.dtype),
                   pltpu.HBM(land, shard.dtype), jax.ShapeDtypeStruct((8, 128), F32)),
        in_specs=[HBM, HBM] + [ANY] * len(after),
        out_specs=(SEM, SEM, HBM, HBM, pl.BlockSpec(memory_space=pltpu.VMEM)),
        input_output_aliases={0: 2, 1: 3},
        compiler_params=pltpu.CompilerParams(has_side_effects=EFFECT),
    )(_hbm(shard), _hbm(lax.empty(land, shard.dtype)), *after)


def allgather_first_wait(send, recv, shard, land, after, name):
    def body(s_ref, l_ref, send_r, recv_r, *rest):
        for cp in _first_copies(s_ref, l_ref, send_r, recv_r, True):
            cp.wait_send()
            cp.wait_recv()

    res = pl.pallas_call(
        body, name=name, out_shape=(pltpu.HBM(shard.shape, shard.dtype), pltpu.HBM(land.shape, land.dtype)),
        in_specs=[HBM, HBM, SEM, SEM] + [ANY] * len(after), out_specs=(HBM, HBM),
        input_output_aliases={0: 0, 1: 1},
        compiler_params=pltpu.CompilerParams(has_side_effects=EFFECT),
    )(shard, land, send, recv, *after)
    return res[1]


def allgather_forward(land):
    def body(l_in, l_ref, send, recv):
        x, y, c, jme, sib, chips = _place()
        cps = []
        for k, chip in enumerate(chips):
            piece = l_ref.at[2 * chip[0] + chip[1], c]
            cp = _remote(piece, piece, send, recv, k, sib)
            cp.start()
            cps.append(cp)
        for k, chip in enumerate(chips):
            piece = l_ref.at[2 * chip[0] + chip[1], 1 - c]
            _remote(piece, piece, send, recv, k, sib).wait_recv()
        for cp in cps:
            cp.wait_send()

    return pl.pallas_call(
        body, name="allgather_forward", in_specs=[ANY], out_specs=ANY,
        out_shape=jax.ShapeDtypeStruct(land.shape, land.dtype), input_output_aliases={0: 0},
        scratch_shapes=[pltpu.SemaphoreType.DMA((3,)), pltpu.SemaphoreType.DMA((3,))],
    )(land)


def _gather_copies(ins, lands, send, recv):
    x, y, c, jme, sib, chips = _place()
    devs = [(*chip, c) for chip in chips] + [sib]
    return [_remote(ins[a], lands[a].at[jme], send, recv, a * 4 + k, dev)
            for a in range(len(ins)) for k, dev in enumerate(devs)]


def allgather_start(shards, after, name):
    n, na = len(shards), len(after)

    def body(*refs):
        ins, lands = refs[:n], refs[n:2 * n]
        send, recv = refs[2 * n + na], refs[2 * n + na + 1]
        token = refs[-1]
        for cp in _gather_copies(ins, lands, send, recv):
            cp.start()
        token[...] = jnp.zeros_like(token)

    res = pl.pallas_call(
        body, name=name,
        out_shape=(pltpu.SemaphoreType.DMA((4 * n,)), pltpu.SemaphoreType.DMA((4 * n,)),
                   *[pltpu.HBM(s.shape, s.dtype) for s in shards],
                   *[pltpu.HBM((4,) + s.shape, s.dtype) for s in shards],
                   jax.ShapeDtypeStruct((8, 128), F32)),
        in_specs=[HBM] * (2 * n) + [ANY] * na,
        out_specs=(SEM, SEM, *[HBM] * (2 * n), pl.BlockSpec(memory_space=pltpu.VMEM)),
        input_output_aliases={a: 2 + a for a in range(2 * n)},
        compiler_params=pltpu.CompilerParams(has_side_effects=EFFECT),
    )(*[_hbm(s) for s in shards], *[_hbm(lax.empty((4,) + s.shape, s.dtype)) for s in shards], *after)
    return res[0], res[1], list(res[2:2 + n]), list(res[2 + n:2 + 2 * n]), res[-1]


def allgather_wait(send, recv, shards, lands, after, name):
    n = len(shards)

    def body(*refs):
        ins, zones = refs[:n], refs[n:2 * n]
        send_r, recv_r = refs[2 * n], refs[2 * n + 1]
        x, y, c, jme, sib, chips = _place()
        slots = [2 * chip[0] + chip[1] for chip in chips] + [jme]
        for a in range(n):
            for k, slot in enumerate(slots):
                cp = _remote(ins[a], zones[a].at[slot], send_r, recv_r, a * 4 + k, sib)
                cp.wait_send()
                cp.wait_recv()

    res = pl.pallas_call(
        body, name=name,
        out_shape=tuple(pltpu.HBM(t.shape, t.dtype) for t in list(shards) + list(lands)),
        in_specs=[HBM] * (2 * n) + [SEM, SEM, ANY], out_specs=tuple([HBM] * (2 * n)),
        input_output_aliases={a: a for a in range(2 * n)},
        compiler_params=pltpu.CompilerParams(has_side_effects=EFFECT),
    )(*shards, *lands, send, recv, after)
    return list(res[n:])


def allgather_small(slab):
    def body(in_ref, out_ref, send, recv, lsem):
        x, y, c, jme, sib, chips = _place()
        loc = pltpu.make_async_copy(in_ref, out_ref.at[jme], lsem.at[0])
        loc.start()
        cps = [_remote(in_ref, out_ref.at[jme], send, recv, k, (*chip, c)) for k, chip in enumerate(chips)]
        for cp in cps:
            cp.start()
        for k, chip in enumerate(chips):
            piece = out_ref.at[2 * chip[0] + chip[1]]
            _remote(piece, piece, send, recv, k, (*chip, c)).wait_recv()
        for cp in cps:
            cp.wait_send()
        loc.wait()

    return pl.pallas_call(
        body, name="allgather_small", in_specs=[ANY], out_specs=ANY,
        out_shape=jax.ShapeDtypeStruct((4,) + slab.shape, slab.dtype),
        scratch_shapes=[pltpu.SemaphoreType.DMA((3,)), pltpu.SemaphoreType.DMA((3,)), pltpu.SemaphoreType.DMA((1,))],
    )(slab)


def allreduce_small(v):
    hr = v.shape[0] // 2

    def body(v_ref, o_ref, r0, r1, r2, send, recv):
        x, y, c, jme, sib, chips = _place()
        mine = pl.ds(pl.multiple_of(c * hr, 8), hr)
        other = pl.ds(pl.multiple_of((1 - c) * hr, 8), hr)
        cp = _remote(v_ref.at[other], r0, send, recv, 0, sib)
        cp.start()
        cp.wait()
        o_ref[mine, :] = v_ref[mine, :] + r0[...]
        for k, (buf, peer) in enumerate(((r1, (1 - x, y, c)), (r2, (x, 1 - y, c))), start=1):
            cp = _remote(o_ref.at[mine], buf, send, recv, k, peer)
            cp.start()
            cp.wait()
            o_ref[mine, :] = o_ref[mine, :] + buf[...]
        cp = _remote(o_ref.at[mine], o_ref.at[mine], send, recv, 3, sib)
        cp.start()
        cp.wait_send()
        _remote(o_ref.at[other], o_ref.at[other], send, recv, 3, sib).wait_recv()

    vm = pl.BlockSpec(memory_space=pltpu.VMEM)
    half = pltpu.VMEM((hr, v.shape[1]), v.dtype)
    return pl.pallas_call(
        body, name="allreduce_small", in_specs=[vm], out_specs=vm,
        out_shape=jax.ShapeDtypeStruct(v.shape, v.dtype),
        scratch_shapes=[half] * 3 + [pltpu.SemaphoreType.DMA((4,)), pltpu.SemaphoreType.DMA((4,))],
        compiler_params=pltpu.CompilerParams(vmem_limit_bytes=VMEM_LIMIT),
    )(v)


def _pair_copies(gs, lands, send, recv):
    x, y, c, jme, sib, chips = _place()
    return [_remote(gs[a].at[:, 1 - c], lands[a], send, recv, a, sib) for a in range(len(gs))]


def rs_pair_start(gs, name):
    n = len(gs)

    def body(*refs):
        ins, lands = refs[:n], refs[n:2 * n]
        send, recv = refs[2 * n], refs[2 * n + 1]
        token = refs[-1]
        for cp in _pair_copies(ins, lands, send, recv):
            cp.start()
        token[...] = jnp.zeros_like(token)

    shapes = [(4,) + g.shape[2:] for g in gs]
    res = pl.pallas_call(
        body, name=name,
        out_shape=(pltpu.SemaphoreType.DMA((n,)), pltpu.SemaphoreType.DMA((n,)),
                   *[pltpu.HBM(g.shape, g.dtype) for g in gs], *[pltpu.HBM(s, F32) for s in shapes],
                   jax.ShapeDtypeStruct((8, 128), F32)),
        in_specs=[HBM] * (2 * n), out_specs=(SEM, SEM, *[HBM] * (2 * n), pl.BlockSpec(memory_space=pltpu.VMEM)),
        input_output_aliases={a: 2 + a for a in range(2 * n)},
        compiler_params=pltpu.CompilerParams(has_side_effects=EFFECT),
    )(*[_hbm(g) for g in gs], *[_hbm(lax.empty(s, F32)) for s in shapes])
    return res[0], res[1], list(res[2:2 + n]), list(res[2 + n:2 + 2 * n]), res[-1]


def rs_pair_wait(send, recv, gs, lands, after, name):
    n = len(gs)

    def body(*refs):
        ins, zones = refs[:n], refs[n:2 * n]
        for cp in _pair_copies(ins, zones, refs[2 * n], refs[2 * n + 1]):
            cp.wait_send()
            cp.wait_recv()

    res = pl.pallas_call(
        body, name=name,
        out_shape=tuple(pltpu.HBM(t.shape, t.dtype) for t in list(gs) + list(lands)),
        in_specs=[HBM] * (2 * n) + [SEM, SEM, ANY], out_specs=tuple([HBM] * (2 * n)),
        input_output_aliases={a: a for a in range(2 * n)},
        compiler_params=pltpu.CompilerParams(has_side_effects=EFFECT),
    )(*gs, *lands, send, recv, after)
    return list(res[:n]), list(res[n:])


SUM_ROWS = 512


def rs_pair_sum(g4s, gots, cidx):
    n = len(g4s)
    tiles = [(min(g.shape[2], SUM_ROWS), g.shape[3]) for g in g4s]
    nts = [g.shape[2] // tr for g, (tr, _) in zip(g4s, tiles)]

    def at(a, s):
        s = jnp.minimum(s, 4 * nts[a] - 1)
        return s // nts[a], s % nts[a]

    def body(c_ref, *refs):
        for a in range(n):
            refs[2 * n + a][...] = (refs[a][...] + refs[n + a][...]).astype(BF16)

    in_specs = [pl.BlockSpec((None, None) + tiles[a], lambda s, cr, a=a: (at(a, s)[0], cr[0], at(a, s)[1], 0))
                for a in range(n)]
    in_specs += [pl.BlockSpec((None,) + tiles[a], lambda s, cr, a=a: (*at(a, s), 0)) for a in range(n)]
    return pl.pallas_call(
        body, name="rs_pair_sum",
        grid_spec=pltpu.PrefetchScalarGridSpec(
            num_scalar_prefetch=1, grid=(4 * max(nts),), in_specs=in_specs,
            out_specs=[pl.BlockSpec((None,) + tiles[a], lambda s, cr, a=a: (*at(a, s), 0)) for a in range(n)]),
        out_shape=[jax.ShapeDtypeStruct((4,) + g.shape[2:], BF16) for g in g4s],
        compiler_params=_cparams(("arbitrary",)),
    )(cidx, *g4s, *gots)


def _chip_copies(ps, lands, send, recv):
    x, y, c, jme, sib, chips = _place()
    return [_remote(ps[a].at[2 * chip[0] + chip[1]], lands[a].at[jme], send, recv, a * 3 + k, (*chip, c))
            for a in range(len(ps)) for k, chip in enumerate(chips)]


def rs_chip_start(ps, name):
    n = len(ps)

    def body(*refs):
        ins, lands = refs[:n], refs[n:2 * n]
        send, recv = refs[2 * n], refs[2 * n + 1]
        token = refs[-1]
        for cp in _chip_copies(ins, lands, send, recv):
            cp.start()
        token[...] = jnp.zeros_like(token)

    res = pl.pallas_call(
        body, name=name,
        out_shape=(pltpu.SemaphoreType.DMA((3 * n,)), pltpu.SemaphoreType.DMA((3 * n,)),
                   *[pltpu.HBM(p.shape, p.dtype) for p in ps], *[pltpu.HBM(p.shape, p.dtype) for p in ps],
                   jax.ShapeDtypeStruct((8, 128), F32)),
        in_specs=[HBM] * (2 * n), out_specs=(SEM, SEM, *[HBM] * (2 * n), pl.BlockSpec(memory_space=pltpu.VMEM)),
        input_output_aliases={a: 2 + a for a in range(2 * n)},
        compiler_params=pltpu.CompilerParams(has_side_effects=EFFECT),
    )(*[_hbm(p) for p in ps], *[_hbm(lax.empty(p.shape, p.dtype)) for p in ps])
    return res[0], res[1], list(res[2:2 + n]), list(res[2 + n:2 + 2 * n]), res[-1]


def rs_chip_wait(send, recv, ps, lands, after, name):
    n = len(ps)

    def body(*refs):
        ins, zones = refs[:n], refs[n:2 * n]
        send_r, recv_r = refs[2 * n], refs[2 * n + 1]
        x, y, c, jme, sib, chips = _place()
        for a in range(n):
            for k, chip in enumerate(chips):
                jt = 2 * chip[0] + chip[1]
                cp = _remote(ins[a].at[jt], zones[a].at[jt], send_r, recv_r, a * 3 + k, (*chip, c))
                cp.wait_send()
                cp.wait_recv()

    res = pl.pallas_call(
        body, name=name,
        out_shape=tuple(pltpu.HBM(p.shape, p.dtype) for p in list(ps) + list(lands)),
        in_specs=[HBM] * (2 * n) + [SEM, SEM] + [ANY] * len(after), out_specs=tuple([HBM] * (2 * n)),
        input_output_aliases={a: a for a in range(2 * n)},
        compiler_params=pltpu.CompilerParams(has_side_effects=EFFECT),
    )(*ps, *lands, send, recv, *after)
    return list(res[:n]), list(res[n:])


def rs_chip_sum(qs, ps, ls, accs, layers, jc):
    n = len(qs)
    tiles = [(min(q.shape[1], SUM_ROWS), q.shape[2]) for q in qs]
    nts = [q.shape[1] // tr for q, (tr, _) in zip(qs, tiles)]

    def at(a, s):
        return jnp.minimum(s, nts[a] - 1)

    def body(jc_ref, *refs):
        jme = jc_ref[0]
        for a in range(n):
            q_ref, p_ref, o_ref = refs[a], refs[n + a], refs[len(refs) - n + a]
            own = p_ref[...].astype(F32)
            v = [jnp.where(jme == j, own, q_ref[j].astype(F32)) for j in range(4)]
            o_ref[...] = ((v[0] + v[1]) + v[2]) + v[3]

    in_specs = [pl.BlockSpec((4,) + tiles[a], lambda s, jr, a=a: (0, at(a, s), 0)) for a in range(n)]
    in_specs += [pl.BlockSpec((None,) + tiles[a], lambda s, jr, a=a: (jr[0], at(a, s), 0)) for a in range(n)]
    args, aliases = [jc, *qs, *ps], {}
    for a in range(n):
        if accs[a] is not None:
            aliases[len(args)] = a
            in_specs.append(ANY)
            args.append(accs[a])
    return pl.pallas_call(
        body, name="rs_chip_sum",
        grid_spec=pltpu.PrefetchScalarGridSpec(
            num_scalar_prefetch=1, grid=(max(nts),), in_specs=in_specs,
            out_specs=[pl.BlockSpec((None, None) + tiles[a], lambda s, jr, a=a: (ls[a], jr[1], at(a, s), 0))
                       for a in range(n)]),
        out_shape=[jax.ShapeDtypeStruct((layers[a], 2) + qs[a].shape[1:], F32) for a in range(n)],
        input_output_aliases=aliases,
        compiler_params=_cparams(("arbitrary",)),
    )(*args)


def rs_pair_gather(rs):
    n = len(rs)

    def body(*refs):
        outs = refs[n:2 * n]
        send, recv = refs[2 * n:]
        x, y, c, jme, sib, chips = _place()
        cps = [_remote(outs[a].at[:, c], outs[a].at[:, c], send, recv, a, sib) for a in range(n)]
        for cp in cps:
            cp.start()
        for a in range(n):
            slot = outs[a].at[:, 1 - c]
            _remote(slot, slot, send, recv, a, sib).wait_recv()
        for cp in cps:
            cp.wait_send()

    return pl.pallas_call(
        body, name="rs_pair_gather", in_specs=[ANY] * n, out_specs=[ANY] * n,
        out_shape=[jax.ShapeDtypeStruct(r.shape, r.dtype) for r in rs],
        input_output_aliases={a: a for a in range(n)},
        scratch_shapes=[pltpu.SemaphoreType.DMA((n,)), pltpu.SemaphoreType.DMA((n,))],
    )(*rs)


def _adamw_math(w, g, m, v):
    m = B1 * m + (1.0 - B1) * g
    v = B2 * v + (1.0 - B2) * (g * g)
    m_hat = m / (1.0 - B1 ** STEP)
    v_hat = v / (1.0 - B2 ** STEP)
    return -LR * (m_hat / (jnp.sqrt(v_hat) + AEPS) + WD * w), m, v


ADAMW_TILE = 512 * 1024


def adamw(w, g, m, v, name, with_grad=False):
    rows, cols = w.shape
    tr = next((t for t in (1024, 512, 256) if rows % t == 0 and t * cols <= ADAMW_TILE), rows)
    fn =(lambda wv, gv, mv, vv: (gv,) + _adamw_math(wv, gv, mv, vv)) if with_grad else _adamw_math
    return rw(fn, [(a, 0, cols) for a in (w, g, m, v)], [(cols, F32)] * (4 if with_grad else 3), name, rows, tr=tr)


def adamw_small(ws, gs, ms, vs):
    n = len(ws)

    def body(*refs):
        for a in range(n):
            res = _adamw_math(*[refs[k * n + a][...] for k in range(4)])
            for k in range(3):
                refs[(4 + k) * n + a][...] = res[k]

    res = pl.pallas_call(
        body, name="adamw_small", in_specs=[VM] * (4 * n), out_specs=[VM] * (3 * n),
        out_shape=[jax.ShapeDtypeStruct(w.shape, F32) for _ in range(3) for w in ws],
        compiler_params=pltpu.CompilerParams(vmem_limit_bytes=VMEM_LIMIT),
    )(*ws, *gs, *ms, *vs)
    return [(res[a], res[n + a], res[2 * n + a]) for a in range(n)]


WEIGHTS = ["norm_ab", "w_in_ab", "pool_w", "pool_scale", "w_out_ab", "norm_cd", "w_in_cd", "sgu_ln_g", "sgu_ln_b",
           "sgu_w", "sgu_b", "s5_a_re", "s5_a_im", "s5_log_dt", "s5_b_re", "s5_b_im", "s5_c_re", "s5_c_im", "s5_d",
           "glu_w1", "glu_w2", "w_out_cd", "norm_x", "w_xq", "w_xkv", "w_xo", "mem_norm", "final_norm"]
INPUTS = ["x", "mem"] + WEIGHTS + ["loss_target"] + ["m_" + n for n in WEIGHTS] + ["v_" + n for n in WEIGHTS]
BIG = ["w_in_ab", "w_out_ab", "w_in_cd", "w_out_cd", "w_xq", "w_xkv", "w_xo", "glu_w1", "glu_w2", "pool_w"]
COL_SHARDED = ("w_in_ab", "w_in_cd", "w_xkv")
SMALL = [n for n in WEIGHTS if n not in BIG]
SMALL_SHARDED = {"norm_cd": 256, "sgu_ln_g": 256, "sgu_ln_b": 256, "s5_d": 128}
PACK = 256 * 128


def _pack(arrs):
    flat = jnp.concatenate([a.reshape(-1) for a in arrs])
    pad = (-flat.shape[0]) % PACK
    return jnp.concatenate([flat, jnp.zeros((pad,), flat.dtype)]).reshape(-1, 128)


def _unpack(packed, shapes):
    flat, out, off = packed.reshape(-1), [], 0
    for s in shapes:
        n = 1
        for d in s:
            n *= d
        out.append(flat[off:off + n].reshape(s))
        off += n
    return out


LAYER_KEYS = (("w_in", "w_out", "pool_w", "w_xq", "w_xkv", "w_xo"),
              ("w_in", "w_out", "glu_w1", "glu_w2", "w_xq", "w_xkv", "w_xo"))


def _weight_of(key, layer):
    if key in ("w_xq", "w_xkv", "w_xo"):
        return key, layer, 4
    kind = "ab" if layer % 2 == 0 else "cd"
    return {"w_in": "w_in_" + kind, "w_out": "w_out_" + kind}.get(key, key), layer // 2, 2


def kernel(*args):
    a = dict(zip(INPUTS, args))
    x_i, y_i, c_i = lax.axis_index("x"), lax.axis_index("y"), lax.axis_index("c")
    j = 2 * x_i + y_i

    slab = jnp.concatenate([a["norm_cd"], a["sgu_ln_g"], a["sgu_ln_b"],
                            jnp.pad(a["s5_d"], ((0, 0), (0, 128)))], axis=0)
    gslab = allgather_small(slab)
    P = {n: a[n] for n in SMALL}
    for k, n in enumerate(("norm_cd", "sgu_ln_g", "sgu_ln_b", "s5_d")):
        wd = SMALL_SHARDED[n]
        P[n] = gslab[:, 2 * k:2 * k + 2, :wd].transpose(1, 0, 2).reshape(2, 4 * wd)

    def shards_of(layer):
        keys = sorted(k for k in LAYER_KEYS[layer % 2])
        out = []
        for k in keys:
            n, l, _ = _weight_of(k, layer)
            out.append(a[n][l].reshape(-1, a[n].shape[-1]).astype(BF16))
        return keys, out

    keys0, sh0 = shards_of(0)
    first = keys0.index("w_in")
    first_of = {0: allgather_first_start(
        sh0[first].reshape(2, sh0[first].shape[0] // 2, sh0[first].shape[1]), [gslab], "allgather_start_0in")}
    token = first_of[0][-1]
    started = {}
    for layer in (0, 1, 2, 3):
        keys, sh = (keys0, sh0) if layer == 0 else shards_of(layer)
        mix = [(k, s) for k, s in zip(keys, sh) if k != "w_in" and not k.startswith("w_x")]
        xat = [(k, s) for k, s in zip(keys, sh) if k.startswith("w_x")]
        if layer > 0:
            w_in = sh[keys.index("w_in")]
            first_of[layer] = allgather_first_start(w_in.reshape(2, w_in.shape[0] // 2, w_in.shape[1]), [token, gslab],
                                                    "allgather_start_%din" % layer)
            token = first_of[layer][-1]
        for tag, pk, ps in (("", *map(list, zip(*mix))), ("x", *map(list, zip(*xat)))):
            send, recv, ps, lands, token = allgather_start(ps, [token, gslab], "allgather_start_%d%s" % (layer, tag))
            started[(layer, tag)] = (pk, send, recv, ps, lands)

    def first_arrived(layer, after):
        f_send, f_recv, f_shard, f_land, _ = first_of[layer]
        g = allgather_forward(allgather_first_wait(f_send, f_recv, f_shard, f_land, after, "allgather_wait_%din" % layer))
        return views({"w_in": g.reshape(4, -1, g.shape[-1])})


    cidx = jnp.reshape(c_i, (1,)).astype(jnp.int32)
    jc = jnp.stack([j, c_i]).astype(jnp.int32)

    def views(g):
        W = {}
        for k, v in g.items():
            if k in ("w_in", "w_xkv"):
                W[k] = mcs(v)
            elif k == "pool_w":
                W[k] = v.reshape(4, 4, 64, 256).transpose(1, 0, 2, 3).reshape(4, 256, 256)
            elif k not in ("glu_w1", "glu_w2"):
                W[k] = m2(v.reshape(-1, v.shape[-1]))
        if "glu_w1" in g:
            W["w12"] = jnp.concatenate([g["glu_w1"].reshape(512, 512), g["glu_w2"].reshape(512, 512)], axis=1)
        return W

    w_in0 = first_arrived(0, [token])

    def arrived(layer, tag, after):
        keys, send, recv, sh, lands = started[(layer, tag)]
        return views(dict(zip(keys, allgather_wait(send, recv, sh, lands, after, "allgather_wait_%d%s" % (layer, tag)))))

    def weights_of(layer, x_in):
        W = dict(w_in0) if layer == 0 else first_arrived(layer, [x_in])
        W["more"] = lambda after: arrived(layer, "", after)
        W["more_x"] = lambda after: arrived(layer, "x", after)
        return W

    halves, pending = {}, {}

    def finish_pair(layer, after):
        keys, send, recv, flat, lands = halves.pop(layer)
        flat, got = rs_pair_wait(send, recv, flat, lands, after, "rs_pair_wait_%d" % layer)
        pair = rs_pair_sum(flat, got, cidx)
        send, recv, pair, lands, token = rs_chip_start(pair, "rs_chip_start_%d" % layer)
        pending[layer] = (keys, send, recv, pair, lands)
        return token

    def grads_done(layer, GW):
        keys = sorted(GW)
        flat = [GW[k].reshape(4, 2, GW[k].shape[1] // 2, GW[k].shape[2]) for k in keys]
        send, recv, flat, lands, token = rs_pair_start(flat, "rs_pair_start_%d" % layer)
        halves[layer] = (keys, send, recv, flat, lands)
        if layer + 1 in halves:
            token = token + finish_pair(layer + 1, token)
        return token[0:1, 0:1]

    loss, dx, G = local_step(a["x"][0], a["mem"][0], a["loss_target"][0], P, weights_of, grads_done)
    loss = lax.psum(loss[0, 0], ("x", "y", "c"))
    finish_pair(0, dx)
    outs = {}

    def update_big(names, red):
        for n, g in zip(names, rs_pair_gather([red[n] for n in names])):
            shp = a[n].shape
            g2 = g.reshape(-1, shp[-1])
            upd = adamw(a[n].reshape(g2.shape), g2, a["m_" + n].reshape(g2.shape), a["v_" + n].reshape(g2.shape),
                        "adamw_" + n, with_grad=True)
            outs[n] = tuple(t.reshape(shp) for t in upd)

    def reduce_layer(layer, red, after):
        keys, send, recv, pair, lands = pending[layer]
        pair, lands = rs_chip_wait(send, recv, pair, lands, after, "rs_chip_wait_%d" % layer)
        which = [_weight_of(k, layer) for k in keys]
        sums = rs_chip_sum(lands, pair, [l for _, l, _ in which], [red.get(n) for n, _, _ in which],
                           [layers for _, _, layers in which], jc)
        red.update(zip([n for n, _, _ in which], sums))

    red = {}
    for layer in (3, 2, 1):
        reduce_layer(layer, red, [dx])
    odd_only = [n for n in BIG if n.endswith("_cd") or n.startswith("glu")]
    update_big(odd_only, red)

    gfull = [jnp.stack(G[n]) if isinstance(G[n], list) else G[n] for n in SMALL]
    shapes = [g.shape for g in gfull]
    gsum = _unpack(allreduce_small(_pack(gfull)), shapes)
    gloc = []
    for n, g in zip(SMALL, gsum):
        if n in SMALL_SHARDED:
            g = lax.dynamic_slice_in_dim(g, j * SMALL_SHARDED[n], SMALL_SHARDED[n], axis=1)
        gloc.append(g)
    two = [(-1, a[n].shape[-1]) if a[n].ndim > 1 else (1, a[n].shape[0]) for n in SMALL]
    upds = adamw_small(*[[t.reshape(s) for t, s in zip(ts, two)]
                         for ts in ([a[n] for n in SMALL], gloc, [a["m_" + n] for n in SMALL],
                                    [a["v_" + n] for n in SMALL])])
    for n, g, upd in zip(SMALL, gloc, upds):
        outs[n] = (g,) + tuple(t.reshape(a[n].shape) for t in upd)

    behind = [outs[n][1] for n in odd_only + SMALL[-1:]] + [red[n] for n in BIG if n not in odd_only]
    reduce_layer(0, red, behind)
    update_big([n for n in BIG if n not in odd_only], red)

    res = [loss, dx[None]]
    for part in range(4):
        res += [outs[n][part] for n in WEIGHTS]
    return tuple(res)
```

```python
import math

import jax
import jax.numpy as jnp
from jax import lax
from jax.experimental import pallas as pl
from jax.experimental.pallas import tpu as pltpu

F32, BF16 = jnp.float32, jnp.bfloat16
S, D = 2048, 1024
MEM = 256
EPS = 1e-6
NEG = -1e30
QB = 128
PATTERNS = (1, 4, 16)
NG, NP, NH = 32, 64, 16
NS = NG * NP
LR, B1, B2, AEPS, WD, STEP = 0.001, 0.9, 0.999, 1e-08, 0.01, 10
MESHID = pl.DeviceIdType.MESH
VMEM_LIMIT = 56 * 1024 * 1024


def _cparams(sem):
    return pltpu.CompilerParams(dimension_semantics=sem, vmem_limit_bytes=VMEM_LIMIT)


def _sig(x):
    return 1.0 / (1.0 + jnp.exp(-x))


def _dot(a, b, dims):
    return lax.dot_general(a, b, (dims, ((), ())), preferred_element_type=F32)


def _nn(a, b):
    return _dot(a, b, ((1,), (0,)))


def _nt(a, b):
    return _dot(a, b, ((1,), (1,)))


def _tn(a, b):
    return _dot(a, b, ((0,), (0,)))


_DIMS = {"nn": ((1,), (0,)), "nt": ((1,), (1,)), "tn": ((0,), (0,))}


def _tile(dim, cc=None, cap=1024):
    for t in (2048, 1536, 1024, 768, 512, 384, 256, 128):
        if t <= cap and dim % t == 0 and (cc is None or cc % t == 0):
            return t
    return dim


MM_VMEM = 36 * 1024 * 1024


def _mm_tiles(m, n, k, ccm, ccn, cck, a_bytes, b_bytes, o_bytes):
    caps = [1024, 1536, 2048]
    while True:
        tm, tn, tk = _tile(m, ccm, caps[0]), _tile(n, ccn, caps[1]), _tile(k, cck, caps[2])
        need = 2 * (tm * tk * a_bytes + tk * tn * b_bytes + tm * tn * o_bytes) + (tm * tn * 4 if tk < k else 0)
        if need <= MM_VMEM:
            return tm, tn, tk
        if tk > 1024:
            caps[2] = tk // 2
        elif tn >= tm:
            caps[1] = tn // 2
        else:
            caps[0] = tm // 2


def m2(arr, col_off=0, ncols=None):
    rows, cols = arr.shape
    ncols = cols - col_off if ncols is None else ncols

    def spec(tr, tc, rc):
        assert col_off % tc == 0
        return pl.BlockSpec((tr, tc), lambda *g: (rc(*g)[0], rc(*g)[1] + col_off // tc))
    return (arr, rows, ncols, spec, None if col_off == 0 else col_off)


def mcs(arr):
    cs = arr.shape[2]

    def spec(tr, tc, rc):
        n = cs // tc
        return pl.BlockSpec((None, tr, tc), lambda *g: (rc(*g)[1] // n, rc(*g)[0], rc(*g)[1] % n))
    return (arr, arr.shape[1], 4 * cs, spec, cs)


def out2(rows, cols):
    def spec(tr, tc, rc):
        return pl.BlockSpec((tr, tc), lambda *g: tuple(rc(*g)))
    return ((rows, cols), spec, None)


def outcs(rows, cs):
    def spec(tr, tc, rc):
        n = cs // tc
        return pl.BlockSpec((None, tr, tc), lambda *g: (rc(*g)[1] // n, rc(*g)[0], rc(*g)[1] % n))
    return ((4, rows, cs), spec, cs)


def _both(a, b):
    if a is None:
        return b
    if b is None:
        return a
    return math.gcd(a, b)


def mm(a, b, mode, name, add=None, out=None, out_dtype=F32):
    a_arr, a_r, a_c, a_spec, a_cc = a
    b_arr, b_r, b_c, b_spec, b_cc = b
    if mode == "nn":
        m, k, n = a_r, a_c, b_c
        assert b_r == k
        ccm, cck, ccn = None, a_cc, b_cc
    elif mode == "nt":
        m, k, n = a_r, a_c, b_r
        assert b_c == k
        ccm, cck, ccn = None, _both(a_cc, b_cc), None
    else:
        m, k, n = a_c, a_r, b_c
        assert b_r == k
        ccm, cck, ccn = a_cc, None, b_cc
    out = out2(m, n) if out is None else out
    o_shape, o_spec, o_cc = out
    ccn = _both(ccn, o_cc)
    if add is not None:
        ccn = _both(ccn, add[4])
    o_bytes = jnp.dtype(out_dtype).itemsize + (0 if add is None else add[0].dtype.itemsize)
    tm, tn, tk = _mm_tiles(m, n, k, ccm, ccn, cck, a_arr.dtype.itemsize, b_arr.dtype.itemsize, o_bytes)
    nk = k // tk
    if mode == "nn":
        in_specs = [a_spec(tm, tk, lambda i, j, kk: (i, kk)), b_spec(tk, tn, lambda i, j, kk: (kk, j))]
    elif mode == "nt":
        in_specs = [a_spec(tm, tk, lambda i, j, kk: (i, kk)), b_spec(tn, tk, lambda i, j, kk: (j, kk))]
    else:
        in_specs = [a_spec(tk, tm, lambda i, j, kk: (kk, i)), b_spec(tk, tn, lambda i, j, kk: (kk, j))]
    args = [a_arr, b_arr]
    if add is not None:
        in_specs.append(add[3](tm, tn, lambda i, j, kk: (i, j)))
        args.append(add[0])
    return _mm_call(args, in_specs, o_spec(tm, tn, lambda i, j, kk: (i, j)), jax.ShapeDtypeStruct(o_shape, out_dtype),
                    mode, (m // tm, n // tn, nk), (tm, tn), add is not None, name)


def _mm_call(args, in_specs, out_spec, out_shape, mode, grid, tile, has_add, name):
    dims = _DIMS[mode]
    nk = grid[2]
    tm, tn = tile

    def body(*refs):
        a_ref, b_ref = refs[0], refs[1]
        add_ref = refs[2] if has_add else None
        prod = _dot(a_ref[...].astype(BF16), b_ref[...].astype(BF16), dims)
        if nk == 1:
            o_ref = refs[-1]
            if has_add:
                prod = prod + add_ref[...].astype(F32)
            o_ref[...] = prod.astype(o_ref.dtype)
            return
        o_ref, acc = refs[-2], refs[-1]
        kk = pl.program_id(2)

        @pl.when(kk == 0)
        def _():
            acc[...] = prod

        @pl.when(kk > 0)
        def _():
            acc[...] += prod

        @pl.when(kk == nk - 1)
        def _():
            r = acc[...]
            if has_add:
                r = r + add_ref[...].astype(F32)
            o_ref[...] = r.astype(o_ref.dtype)

    return pl.pallas_call(
        body, name=name, grid=grid, in_specs=in_specs, out_specs=out_spec, out_shape=out_shape,
        scratch_shapes=[pltpu.VMEM((tm, tn), F32)] if nk > 1 else [],
        compiler_params=_cparams(("parallel", "parallel", "arbitrary")),
    )(*args)


def mm_band(a, b, mode, name, grid, blocks, maps, out_shape, add=None, out_dtype=F32):
    in_specs = [pl.BlockSpec(blocks[0], maps[0]), pl.BlockSpec(blocks[1], maps[1])]
    args = [a, b]
    if add is not None:
        in_specs.append(pl.BlockSpec(blocks[2], maps[2]))
        args.append(add)
    return _mm_call(args, in_specs, pl.BlockSpec(blocks[2], maps[2]), jax.ShapeDtypeStruct(out_shape, out_dtype),
                    mode, grid, blocks[2], add is not None, name)


def rw(fn, ins, outs, name, rows, tr=None, consts=(), accs=()):
    tr = min(rows, 1024) if tr is None else tr
    n_in, n_c, n_o, n_a = len(ins), len(consts), len(outs), len(accs)
    in_specs = []
    for arr, off, width in ins:
        assert off % width == 0
        in_specs.append(pl.BlockSpec((tr, width), lambda i, o=off // width: (i, o)))
    for c in consts:
        in_specs.append(pl.BlockSpec(c.shape, lambda i: (0, 0)))
    out_specs = [pl.BlockSpec((tr, w), lambda i: (i, 0)) for w, _ in outs]
    out_specs += [pl.BlockSpec(s, lambda i: (0, 0)) for s in accs]
    out_shape = [jax.ShapeDtypeStruct((rows, w), dt) for w, dt in outs]
    out_shape += [jax.ShapeDtypeStruct(s, F32) for s in accs]

    def body(*refs):
        vals = [r[...] for r in refs[:n_in + n_c]]
        o_refs = refs[n_in + n_c:n_in + n_c + n_o]
        a_refs = refs[n_in + n_c + n_o:]
        res = fn(*vals)
        for r, v in zip(o_refs, res[:n_o]):
            r[...] = v.astype(r.dtype)
        if n_a:
            @pl.when(pl.program_id(0) == 0)
            def _():
                for r in a_refs:
                    r[...] = jnp.zeros_like(r)
            for r, v in zip(a_refs, res[n_o:]):
                r[...] += v

    res = pl.pallas_call(
        body, name=name, grid=(rows // tr,), in_specs=in_specs, out_specs=out_specs,
        out_shape=out_shape,
        compiler_params=_cparams(("arbitrary",) if n_a else ("parallel",)),
    )(*[a for a, _, _ in ins], *consts)
    return res


def _rstd(x):
    return lax.rsqrt(jnp.mean(x * x, axis=-1, keepdims=True) + EPS)


def rms_fwd(x, g, name):
    def fn(xv, gv):
        xv = xv.astype(F32)
        return (xv * _rstd(xv) * gv,)
    return rw(fn, [(x, 0, D)], [(D, BF16)], name, x.shape[0], consts=[g])[0]


def _rms_bwd_math(xv, dy, gv):
    r = _rstd(xv)
    dyg = dy * gv
    dx = r * dyg - xv * (r * r * r / D) * jnp.sum(dyg * xv, axis=-1, keepdims=True)
    dg = jnp.sum(dy * xv * r, axis=0, keepdims=True)
    return dx, dg


def rms_bwd(x, dy, dres, g, name):
    def fn(xv, dyv, drv, gv):
        dx, dg = _rms_bwd_math(xv, dyv, gv)
        return dx + drv, dg
    return rw(fn, [(x, 0, D), (dy, 0, D), (dres, 0, D)], [(D, F32)], name, x.shape[0],
              consts=[g], accs=[(1, D)])


def final_loss(x, tgt, g):
    def fn(xv, tv, gv):
        e = xv * _rstd(xv) * gv - tv
        loss = 0.5 * jnp.sum(jnp.sum(e * e, axis=-1, keepdims=True), axis=0, keepdims=True) / D
        dx, dg = _rms_bwd_math(xv, e / D, gv)
        return dx, loss, dg
    return rw(fn, [(x, 0, D), (tgt, 0, D)], [(D, F32)], "final_loss", S, consts=[g],
              accs=[(1, 1), (1, D)])


def _attn_bias(bias_ref):
    ii = lax.broadcasted_iota(jnp.int32, (2 * QB, 2 * QB), 0) % QB
    jj = lax.broadcasted_iota(jnp.int32, (2 * QB, 2 * QB), 1)
    dist = ii + QB - jj
    band = (dist >= 0) & (dist <= QB)
    bias_ref[1] = jnp.where(band, 0.0, NEG)
    bias_ref[0] = jnp.where(band & (jj >= QB), 0.0, NEG)


def _two_heads(x, m0):
    return jnp.concatenate([jnp.where(m0, x, 0.0), jnp.where(m0, 0.0, x)], axis=0)


def _per_head(col, m0):
    return jnp.where(m0, col[:QB], col[QB:])


def _attn_rows(idx, d):
    if d == 1:
        b = idx
        cur = pl.ds(pl.multiple_of(b * QB, QB), QB)
        prev = pl.ds(pl.multiple_of(jnp.maximum(b - 1, 0) * QB, QB), QB)
    else:
        r, b = lax.rem(idx, d), lax.div(idx, d)
        cur = pl.ds(r + b * (QB * d), QB, stride=d)
        prev = pl.ds(r + jnp.maximum(b - 1, 0) * (QB * d), QB, stride=d)
    return cur, prev, b


NBLK = S // QB
GROUP = 16
GROUP_FWD = 16


def _colblk(off):
    return pl.BlockSpec((S, 128), lambda hp: (0, off * 8 + hp))


def attn_fwd(z):
    def body(q_ref, k_ref, v_ref, g_ref, o_ref, l_ref, a_ref, os, ls, bias):
        _attn_bias(bias)
        m0 = lax.broadcasted_iota(jnp.int32, (1, 128), 1) < 64
        for pi, d in enumerate(PATTERNS):
            lone = S // d == QB

            def load(idx, d=d, lone=lone):
                cur, prev, b = _attn_rows(idx, d)
                if lone:
                    return cur, (q_ref[cur, :], None, k_ref[cur, :], None, v_ref[cur, :], bias[1, :, QB:])
                return cur, (q_ref[cur, :], k_ref[prev, :], k_ref[cur, :], v_ref[prev, :], v_ref[cur, :],
                             bias[jnp.minimum(b, 1)])

            def block(q, kp, kc, vp, vc, bs):
                qq = _two_heads(q * 0.125, m0).astype(BF16)
                k = (kc if kp is None else jnp.concatenate([kp, kc], axis=0)).astype(BF16)
                s = _nt(qq, k) + bs
                mx = jnp.max(s, axis=-1, keepdims=True)
                p = jnp.exp(s - mx)
                den = jnp.sum(p, axis=-1, keepdims=True)
                pb = p.astype(BF16)
                vv = _two_heads(vc if vp is None else jnp.concatenate([vp, vc], axis=0), m0).astype(BF16)
                o = _nn(jnp.concatenate([pb[:QB], pb[QB:]], axis=1), vv)
                return o * _per_head(1.0 / den, m0), _per_head(mx + jnp.log(den), m0)

            def step(i, carry, pi=pi):
                loaded = [load(i * GROUP_FWD + u) for u in range(GROUP_FWD)]
                done = [block(*vals) for _, vals in loaded]
                for (cur, _), (o, l) in zip(loaded, done):
                    os[pi, cur, :] = o
                    ls[pi, cur, :] = l
                return carry
            lax.fori_loop(0, NBLK // GROUP_FWD, step, 0)
        l1, l2, l3 = ls[0], ls[1], ls[2]
        mx = jnp.maximum(jnp.maximum(l1, l2), l3)
        e1, e2, e3 = jnp.exp(l1 - mx), jnp.exp(l2 - mx), jnp.exp(l3 - mx)
        tot = e1 + e2 + e3
        o = (os[0] * e1 + os[1] * e2 + os[2] * e3) / tot
        ga = g_ref[...]
        o_ref[...] = o
        l_ref[...] = mx + jnp.log(tot)
        a_ref[...] = (o * (ga * _sig(ga))).astype(a_ref.dtype)

    out = pl.BlockSpec((S, 128), lambda hp: (0, hp))
    return pl.pallas_call(
        body, name="attn_fwd", grid=(8,),
        in_specs=[_colblk(0), _colblk(1), _colblk(2), _colblk(3)], out_specs=[out] * 3,
        out_shape=[jax.ShapeDtypeStruct((S, D), F32), jax.ShapeDtypeStruct((S, D), F32),
                   jax.ShapeDtypeStruct((S, 2 * D), BF16)],
        scratch_shapes=[pltpu.VMEM((3, S, 128), F32), pltpu.VMEM((3, S, 128), F32),
                        pltpu.VMEM((2, 2 * QB, 2 * QB), F32)],
        compiler_params=_cparams(("parallel",)),
    )(z, z, z, z)


def attn_bwd(z, d_cat, o, lse):
    def body(q_ref, k_ref, v_ref, g_ref, da_ref, o_ref, l_ref, dq_ref, dk_ref, dv_ref, dg_ref, do_s, pr_s, bias):
        _attn_bias(bias)
        m0 = lax.broadcasted_iota(jnp.int32, (1, 128), 1) < 64
        ga = g_ref[...]
        sg = _sig(ga)
        da = da_ref[...]
        ov = o_ref[...]
        do = da * (ga * sg)
        dg_ref[...] = da * ov * (sg * (1.0 + ga * (1.0 - sg)))
        do_s[...] = do
        pr_s[...] = do * ov
        dq_ref[...] = jnp.zeros_like(dq_ref)
        dk_ref[...] = jnp.zeros_like(dk_ref)
        dv_ref[...] = jnp.zeros_like(dv_ref)
        for d in PATTERNS:
            lone = S // d == QB

            def load(idx, d=d, lone=lone):
                cur, prev, b = _attn_rows(idx, d)
                if lone:
                    return (cur, None), (q_ref[cur, :], None, k_ref[cur, :], None, v_ref[cur, :],
                                         do_s[cur, :], pr_s[cur, :], l_ref[cur, :], bias[1, :, QB:])
                return (cur, prev), (q_ref[cur, :], k_ref[prev, :], k_ref[cur, :], v_ref[prev, :], v_ref[cur, :],
                                     do_s[cur, :], pr_s[cur, :], l_ref[cur, :], bias[jnp.minimum(b, 1)])

            def block(q, kp, kc, vp, vc, dof, prod, lp, bs):
                qq = _two_heads(q * 0.125, m0).astype(BF16)
                kf = kc if kp is None else jnp.concatenate([kp, kc], axis=0)
                k = kf.astype(BF16)
                v = (vc if vp is None else jnp.concatenate([vp, vc], axis=0)).astype(BF16)
                dd = _two_heads(dof, m0).astype(BF16)
                lh = jnp.max(jnp.concatenate([jnp.where(m0, lp, -jnp.inf), jnp.where(m0, -jnp.inf, lp)], axis=0),
                             axis=-1, keepdims=True)
                delta = jnp.sum(_two_heads(prod, m0), axis=-1, keepdims=True)
                p = jnp.exp(_nt(qq, k) + bs - lh)
                ds = (p * (_nt(dd, v) - delta)).astype(BF16)
                dq = _nn(jnp.concatenate([ds[:QB], ds[QB:]], axis=1), _two_heads(kf, m0).astype(BF16))
                return dq * 0.125, _tn(ds, qq), _tn(p.astype(BF16), dd)

            def step(i, carry):
                loaded = [load(i * GROUP + u) for u in range(GROUP)]
                done = [block(*vals) for _, vals in loaded]
                for ((cur, prev), _), (dq, dk, dv) in zip(loaded, done):
                    dq_ref[cur, :] = dq_ref[cur, :] + dq
                    if prev is not None:
                        dk_ref[prev, :] = dk_ref[prev, :] + dk[:QB]
                        dv_ref[prev, :] = dv_ref[prev, :] + dv[:QB]
                    dk_ref[cur, :] = dk_ref[cur, :] + dk[-QB:]
                    dv_ref[cur, :] = dv_ref[cur, :] + dv[-QB:]
                return carry
            lax.fori_loop(0, NBLK // GROUP, step, 0)

    blk = pl.BlockSpec((S, 128), lambda hp: (0, hp))
    return pl.pallas_call(
        body, name="attn_bwd", grid=(8,),
        in_specs=[_colblk(0), _colblk(1), _colblk(2), _colblk(3), blk, blk, blk], out_specs=[blk] * 4,
        out_shape=[jax.ShapeDtypeStruct((S, D), F32)] * 4,
        scratch_shapes=[pltpu.VMEM((S, 128), F32), pltpu.VMEM((S, 128), F32), pltpu.VMEM((2, 2 * QB, 2 * QB), F32)],
        compiler_params=_cparams(("parallel",)),
    )(z, z, z, z, d_cat, o, lse)


def assemble_dz_even(parts):
    def body(*refs):
        o_ref = refs[-1]
        for j in range(6):
            o_ref[:, j * D:(j + 1) * D] = refs[j][...].astype(o_ref.dtype)
    tr = 512
    blk = pl.BlockSpec((tr, D), lambda i: (i, 0))
    return pl.pallas_call(
        body, name="assemble_dz_even", grid=(S // tr,), in_specs=[blk] * 6,
        out_specs=pl.BlockSpec((tr, 6 * D), lambda i: (i, 0)),
        out_shape=jax.ShapeDtypeStruct((S, 6 * D), BF16),
        compiler_params=_cparams(("parallel",)),
    )(*parts)


def _pool_window(g):
    return jnp.where(g == 0, 2.0, jnp.where(g == 1, 4.0, jnp.where(g == 2, 8.0, 16.0)))


def _pool_sel(g, levels):
    return jnp.where(g == 0, levels[0], jnp.where(g == 1, levels[1], jnp.where(g == 2, levels[2], levels[3])))


def _pool_fwd_math(v, g):
    t = lax.broadcasted_iota(jnp.int32, (S, 1), 0)
    s = v
    levels = []
    for k in (1, 2, 4, 8):
        s = s + jnp.where(t >= k, pltpu.roll(s, k, 0), 0.0)
        levels.append(s)
    cnt = jnp.minimum((t + 1).astype(F32), _pool_window(g))
    return _pool_sel(g, levels) / cnt - v, cnt


def pool_fwd(z, pw, ps, cat):
    def body(v_ref, g_ref, pw_ref, ps_ref, cat_ref, o_ref):
        g = pl.program_id(0)
        pooled, _ = _pool_fwd_math(v_ref[...], g)
        mixed = _nn(pooled.astype(BF16), pw_ref[...].astype(BF16))
        gb = g_ref[...]
        o_ref[...] = (mixed * ps_ref[...] * (gb * _sig(gb))).astype(o_ref.dtype)

    return pl.pallas_call(
        body, name="pool_fwd", grid=(4,),
        in_specs=[pl.BlockSpec((S, 256), lambda g: (0, 16 + g)),
                  pl.BlockSpec((S, 256), lambda g: (0, 20 + g)),
                  pl.BlockSpec((None, 256, 256), lambda g: (g, 0, 0)),
                  pl.BlockSpec((1, 256), lambda g: (0, g)), pl.BlockSpec(memory_space=pl.ANY)],
        out_specs=pl.BlockSpec((S, 256), lambda g: (0, 4 + g)),
        out_shape=jax.ShapeDtypeStruct((S, 2 * D), BF16),
        input_output_aliases={4: 0},
        compiler_params=_cparams(("parallel",)),
    )(z, z, pw, ps, cat)


def pool_bwd(z, d_cat, pw, ps):
    def body(v_ref, g_ref, d_ref, pw_ref, ps_ref, dv_ref, dg_ref, dpw_ref, dps_ref):
        g = pl.program_id(0)
        v = v_ref[...]
        pooled, cnt = _pool_fwd_math(v, g)
        pwb = pw_ref[...].astype(BF16)
        pb = pooled.astype(BF16)
        mixed = _nn(pb, pwb)
        gb = g_ref[...]
        sg = _sig(gb)
        dout = d_ref[...]
        sc = ps_ref[...]
        dg_ref[...] = dout * mixed * sc * (sg * (1.0 + gb * (1.0 - sg)))
        dms = dout * (gb * sg)
        dps_ref[...] = jnp.sum(dms * mixed, axis=0, keepdims=True)
        dmx = (dms * sc).astype(BF16)
        dpw_ref[...] = _tn(pb, dmx)
        dpooled = _nt(dmx, pwb)
        t = lax.broadcasted_iota(jnp.int32, (S, 1), 0)
        s = dpooled / cnt
        levels = []
        for k in (1, 2, 4, 8):
            s = s + jnp.where(t < S - k, pltpu.roll(s, S - k, 0), 0.0)
            levels.append(s)
        dv_ref[...] = _pool_sel(g, levels) - dpooled

    return pl.pallas_call(
        body, name="pool_bwd", grid=(4,),
        in_specs=[pl.BlockSpec((S, 256), lambda g: (0, 16 + g)),
                  pl.BlockSpec((S, 256), lambda g: (0, 20 + g)),
                  pl.BlockSpec((S, 256), lambda g: (0, 4 + g)),
                  pl.BlockSpec((None, 256, 256), lambda g: (g, 0, 0)),
                  pl.BlockSpec((1, 256), lambda g: (0, g))],
        out_specs=[pl.BlockSpec((S, 256), lambda g: (0, g)),
                   pl.BlockSpec((S, 256), lambda g: (0, g)),
                   pl.BlockSpec((None, 256, 256), lambda g: (g, 0, 0)),
                   pl.BlockSpec((1, 256), lambda g: (0, g))],
        out_shape=[jax.ShapeDtypeStruct((S, D), F32), jax.ShapeDtypeStruct((S, D), F32),
                   jax.ShapeDtypeStruct((4, 256, 256), F32), jax.ShapeDtypeStruct((1, D), F32)],
        compiler_params=_cparams(("parallel",)),
    )(z, z, d_cat, pw, ps)


CH = 128


def _sgu_common(v, lng, lnb, w_ref):
    mu = jnp.mean(v, axis=-1, keepdims=True)
    vc = v - mu
    rs = lax.rsqrt(jnp.mean(vc * vc, axis=-1, keepdims=True) + EPS)
    xhat = vc * rs
    vn = (xhat * lng + lnb).astype(BF16)
    ri = lax.broadcasted_iota(jnp.int32, (CH, CH), 0)
    ci = lax.broadcasted_iota(jnp.int32, (CH, CH), 1)
    tril = ri >= ci
    ws = [jnp.where(tril, w_ref[g], 0.0).astype(BF16) for g in range(4)]
    return xhat, rs, vn, tril, ws


def _zspec(off):
    return pl.BlockSpec((CH, D), lambda c: (c, off))


def _full(shape):
    return pl.BlockSpec(shape, lambda c: (0,) * len(shape))


def sgu_fwd(z, lng, lnb, w, bfull):
    def body(u_ref, v_ref, g_ref, lng_ref, lnb_ref, w_ref, b_ref, o_ref):
        _, _, vn, _, ws = _sgu_common(v_ref[...], lng_ref[...], lnb_ref[...], w_ref)
        for g in range(4):
            sl = slice(g * 256, (g + 1) * 256)
            mixed = _nn(ws[g], vn[:, sl]) + b_ref[:, sl]
            gc = g_ref[:, sl]
            o_ref[:, sl] = (u_ref[:, sl] * mixed * (gc * _sig(gc))).astype(o_ref.dtype)

    return pl.pallas_call(
        body, name="sgu_fwd", grid=(S // CH,),
        in_specs=[_zspec(0), _zspec(1), _zspec(2), _full((1, D)), _full((1, D)),
                  _full((4, CH, CH)), _full((CH, D))],
        out_specs=pl.BlockSpec((CH, D), lambda c: (c, 0)),
        out_shape=jax.ShapeDtypeStruct((S, D), BF16),
        compiler_params=_cparams(("parallel",)),
    )(z, z, z, lng, lnb, w, bfull)


def sgu_bwd(z, d_cat, lng, lnb, w, bfull):
    def body(u_ref, v_ref, g_ref, d_ref, lng_ref, lnb_ref, w_ref, b_ref,
             du_ref, dv_ref, dg_ref, dw_ref, db_ref, dlg_ref, dlb_ref):
        @pl.when(pl.program_id(0) == 0)
        def _():
            dw_ref[...] = jnp.zeros_like(dw_ref)
            db_ref[...] = jnp.zeros_like(db_ref)
            dlg_ref[...] = jnp.zeros_like(dlg_ref)
            dlb_ref[...] = jnp.zeros_like(dlb_ref)

        lng = lng_ref[...]
        xhat, rs, vn, tril, ws = _sgu_common(v_ref[...], lng, lnb_ref[...], w_ref)
        lane = lax.broadcasted_iota(jnp.int32, (1, 128), 1)
        db = jnp.zeros((CH, 128), F32)
        dvn_parts = []
        for g in range(4):
            sl = slice(g * 256, (g + 1) * 256)
            mixed = _nn(ws[g], vn[:, sl]) + b_ref[:, sl]
            gc = g_ref[:, sl]
            sg = _sig(gc)
            u = u_ref[:, sl]
            dc = d_ref[:, sl]
            du_ref[:, sl] = dc * mixed * (gc * sg)
            dg_ref[:, sl] = dc * u * mixed * (sg * (1.0 + gc * (1.0 - sg)))
            dmx = dc * u * (gc * sg)
            db = db + jnp.where(lane == g, jnp.sum(dmx, axis=-1, keepdims=True), 0.0)
            dmb = dmx.astype(BF16)
            dw_ref[g] += jnp.where(tril, _nt(dmb, vn[:, sl]), 0.0)
            dvn_parts.append(_tn(ws[g], dmb))
        db_ref[...] += db
        dvn = jnp.concatenate(dvn_parts, axis=1)
        dlb_ref[...] += jnp.sum(dvn, axis=0, keepdims=True)
        dlg_ref[...] += jnp.sum(dvn * xhat, axis=0, keepdims=True)
        dxh = dvn * lng
        dv_ref[...] = rs * (dxh - jnp.mean(dxh, axis=-1, keepdims=True)
                            - xhat * jnp.mean(dxh * xhat, axis=-1, keepdims=True))

    row = pl.BlockSpec((CH, D), lambda c: (c, 0))
    return pl.pallas_call(
        body, name="sgu_bwd", grid=(S // CH,),
        in_specs=[_zspec(0), _zspec(1), _zspec(2), row, _full((1, D)), _full((1, D)),
                  _full((4, CH, CH)), _full((CH, D))],
        out_specs=[row, row, row, _full((4, CH, CH)), _full((CH, 128)), _full((1, D)), _full((1, D))],
        out_shape=[jax.ShapeDtypeStruct((S, D), F32)] * 3
        + [jax.ShapeDtypeStruct((4, CH, CH), F32), jax.ShapeDtypeStruct((CH, 128), F32),
           jax.ShapeDtypeStruct((1, D), F32), jax.ShapeDtypeStruct((1, D), F32)],
        compiler_params=_cparams(("arbitrary",)),
    )(z, z, z, d_cat, lng, lnb, w, bfull)


TB = 256


def _cmul(ar, ai, br, bi):
    return ar * br - ai * bi, ar * bi + ai * br


def _scan_consts(ar, ai, reverse):
    a2 = _cmul(ar, ai, ar, ai)
    a4 = _cmul(*a2, *a2)
    row = lax.broadcasted_iota(jnp.int32, (8, NS), 0)

    def masked(k, p):
        keep = (row < 8 - k) if reverse else (row >= k)
        return jnp.where(keep, p[0], 0.0), jnp.where(keep, p[1], 0.0)
    pr = jnp.zeros((8, NS), F32)
    pi = jnp.zeros((8, NS), F32)
    cr, ci = ar, ai
    for r in range(8):
        sel = row == (7 - r if reverse else r)
        pr = jnp.where(sel, cr, pr)
        pi = jnp.where(sel, ci, pi)
        cr, ci = _cmul(cr, ci, ar, ai)
    return (masked(1, (ar, ai)), masked(2, a2), masked(4, a4)), (pr, pi), row


def scan_fwd(bu, abr, abi):
    def body(bu_ref, ar_ref, ai_ref, h_ref, car, cai):
        @pl.when(pl.program_id(0) == 0)
        def _():
            car[...] = jnp.zeros_like(car)
            cai[...] = jnp.zeros_like(cai)

        pows, (pr, pi), row = _scan_consts(ar_ref[...], ai_ref[...], False)

        def tile(t, carry):
            c_r, c_i = carry
            rows = pl.ds(pl.multiple_of(t * 8, 8), 8)
            xr = bu_ref[rows, 0:NS]
            xi = bu_ref[rows, NS:2 * NS]
            for k, (kr, ki) in zip((1, 2, 4), pows):
                sr = pltpu.roll(xr, k, 0)
                si = pltpu.roll(xi, k, 0)
                xr, xi = xr + kr * sr - ki * si, xi + kr * si + ki * sr
            xr, xi = xr + pr * c_r - pi * c_i, xi + pr * c_i + pi * c_r
            h_ref[rows, 0:NS] = xr
            h_ref[rows, NS:2 * NS] = xi
            return (jnp.broadcast_to(xr[7:8, :], (8, NS)), jnp.broadcast_to(xi[7:8, :], (8, NS)))

        c_r, c_i = lax.fori_loop(0, TB // 8, tile, (car[...], cai[...]))
        car[...] = c_r
        cai[...] = c_i

    return pl.pallas_call(
        body, name="s5_scan_fwd", grid=(S // TB,),
        in_specs=[pl.BlockSpec((TB, 2 * NS), lambda i: (i, 0)),
                  pl.BlockSpec((1, NS), lambda i: (0, 0)), pl.BlockSpec((1, NS), lambda i: (0, 0))],
        out_specs=pl.BlockSpec((TB, 2 * NS), lambda i: (i, 0)),
        out_shape=jax.ShapeDtypeStruct((S, 2 * NS), F32),
        scratch_shapes=[pltpu.VMEM((8, NS), F32), pltpu.VMEM((8, NS), F32)],
        compiler_params=_cparams(("arbitrary",)),
    )(bu, abr, abi)


def scan_bwd(eta, h, abr, abi):
    nt = S // TB

    def body(e_ref, h_ref, ar_ref, ai_ref, l_ref, da_ref, car, cai):
        @pl.when(pl.program_id(0) == 0)
        def _():
            car[...] = jnp.zeros_like(car)
            cai[...] = jnp.zeros_like(cai)
            da_ref[...] = jnp.zeros_like(da_ref)

        pows, (pr, pi), row = _scan_consts(ar_ref[...], -ai_ref[...], True)

        def tile(tt, carry):
            c_r, c_i, acr, aci = carry
            t = TB // 8 - 1 - tt
            rows = pl.ds(pl.multiple_of(t * 8, 8), 8)
            xr = e_ref[rows, 0:NS]
            xi = e_ref[rows, NS:2 * NS]
            for k, (kr, ki) in zip((1, 2, 4), pows):
                sr = pltpu.roll(xr, 8 - k, 0)
                si = pltpu.roll(xi, 8 - k, 0)
                xr, xi = xr + kr * sr - ki * si, xi + kr * si + ki * sr
            xr, xi = xr + pr * c_r - pi * c_i, xi + pr * c_i + pi * c_r
            l_ref[rows, 0:NS] = xr
            l_ref[rows, NS:2 * NS] = xi
            nr = jnp.where(row < 7, pltpu.roll(xr, 7, 0), c_r)
            ni = jnp.where(row < 7, pltpu.roll(xi, 7, 0), c_i)
            hr = h_ref[rows, 0:NS]
            hi = h_ref[rows, NS:2 * NS]
            acr = acr + hr * nr + hi * ni
            aci = aci + hr * ni - hi * nr
            return (jnp.broadcast_to(xr[0:1, :], (8, NS)), jnp.broadcast_to(xi[0:1, :], (8, NS)), acr, aci)

        zero = jnp.zeros((8, NS), F32)
        c_r, c_i, acr, aci = lax.fori_loop(0, TB // 8, tile, (car[...], cai[...], zero, zero))
        car[...] = c_r
        cai[...] = c_i
        da_ref[:, 0:NS] += acr
        da_ref[:, NS:2 * NS] += aci

    rev = pl.BlockSpec((TB, 2 * NS), lambda i: (nt - 1 - i, 0))
    return pl.pallas_call(
        body, name="s5_scan_bwd", grid=(nt,),
        in_specs=[rev, rev, pl.BlockSpec((1, NS), lambda i: (0, 0)), pl.BlockSpec((1, NS), lambda i: (0, 0))],
        out_specs=[rev, pl.BlockSpec((8, 2 * NS), lambda i: (0, 0))],
        out_shape=[jax.ShapeDtypeStruct((S, 2 * NS), F32), jax.ShapeDtypeStruct((8, 2 * NS), F32)],
        scratch_shapes=[pltpu.VMEM((8, NS), F32), pltpu.VMEM((8, NS), F32)],
        compiler_params=_cparams(("arbitrary",)),
    )(eta, h, abr, abi)


GC = 0.7978845608028654
GA = 0.044715


def s5_post(hc, z, dskip):
    def fn(hv, xd, dv):
        y = hv + dv * xd
        return y, 0.5 * y * (1.0 + jnp.tanh(GC * (y + GA * y * y * y)))
    return rw(fn, [(hc, 0, 512), (z, 3072, 512)], [(512, F32), (512, BF16)], "s5_post", S, consts=[dskip])


def s5_post_bwd(dyg, ypre, z, dskip):
    def fn(dy, y, xd, dv):
        th = jnp.tanh(GC * (y + GA * y * y * y))
        dg = 0.5 * (1.0 + th) + 0.5 * y * (1.0 - th * th) * GC * (1.0 + 3.0 * GA * y * y)
        dyp = dy * dg
        return dyp, dyp * dv, jnp.sum(dyp * xd, axis=0, keepdims=True)
    return rw(fn, [(dyg, 0, 512), (ypre, 0, 512), (z, 3072, 512)], [(512, BF16), (512, F32)],
              "s5_post_bwd", S, consts=[dskip], accs=[(1, 512)])


def glu_fwd(t, z, c_out):
    def fn(t1, t2, gd, co):
        return (jnp.concatenate([co, (t1 * _sig(t2) * (gd * _sig(gd))).astype(BF16)], axis=1),)
    return rw(fn, [(t, 0, 512), (t, 512, 512), (z, 3584, 512), (c_out, 0, D)], [(D + 512, BF16)], "glu_fwd", S)[0]


def glu_bwd(t, z, d_cat):
    def fn(t1, t2, gd, dd):
        s2, sg = _sig(t2), _sig(gd)
        sl = gd * sg
        return (jnp.concatenate([dd * s2 * sl, dd * t1 * s2 * (1.0 - s2) * sl], axis=1),
                dd * t1 * s2 * (sg * (1.0 + gd * (1.0 - sg))))
    return rw(fn, [(t, 0, 512), (t, 512, 512), (z, 3584, 512), (d_cat, 1024, 512)],
              [(D, BF16), (512, F32)], "glu_bwd", S)


def assemble_dz_odd(du, dv, dgc, dxd, dgd):
    def body(a, b, c, d, e, o_ref):
        o_ref[:, 0:D] = a[...].astype(BF16)
        o_ref[:, D:2 * D] = b[...].astype(BF16)
        o_ref[:, 2 * D:3 * D] = c[...].astype(BF16)
        o_ref[:, 3 * D:3 * D + 512] = d[...].astype(BF16)
        o_ref[:, 3 * D + 512:4 * D] = e[...].astype(BF16)
    tr = 512
    blk = pl.BlockSpec((tr, D), lambda i: (i, 0))
    half = pl.BlockSpec((tr, 512), lambda i: (i, 0))
    return pl.pallas_call(
        body, name="assemble_dz_odd", grid=(S // tr,), in_specs=[blk, blk, blk, half, half],
        out_specs=pl.BlockSpec((tr, 4 * D), lambda i: (i, 0)),
        out_shape=jax.ShapeDtypeStruct((S, 4 * D), BF16),
        compiler_params=_cparams(("parallel",)),
    )(du, dv, dgc, dxd, dgd)


TQ = 1024


def _xattn_probs(qh, kh):
    s = _nt(qh, kh) * 0.0625
    p = jnp.exp(s - jnp.max(s, axis=-1, keepdims=True))
    return p / jnp.sum(p, axis=-1, keepdims=True)


def xattn_fwd(q, kv):
    def body(q_ref, kv_ref, o_ref):
        outs = []
        for h in range(4):
            sl = slice(h * 256, (h + 1) * 256)
            p = _xattn_probs(q_ref[:, sl].astype(BF16), kv_ref[:, sl].astype(BF16))
            vh = kv_ref[:, D + h * 256:D + (h + 1) * 256].astype(BF16)
            outs.append((sl, _nn(p.astype(BF16), vh)))
        for sl, o in outs:
            o_ref[:, sl] = o.astype(o_ref.dtype)

    return pl.pallas_call(
        body, name="xattn_fwd", grid=(S // TQ,),
        in_specs=[pl.BlockSpec((TQ, D), lambda i: (i, 0)), pl.BlockSpec((MEM, 2 * D), lambda i: (0, 0))],
        out_specs=pl.BlockSpec((TQ, D), lambda i: (i, 0)),
        out_shape=jax.ShapeDtypeStruct((S, D), BF16),
        compiler_params=_cparams(("parallel",)),
    )(q, kv)


def xattn_bwd(q, kv, d_o):
    def body(q_ref, kv_ref, do_ref, dq_ref, dkv_ref):
        @pl.when(pl.program_id(0) == 0)
        def _():
            dkv_ref[...] = jnp.zeros_like(dkv_ref)

        done = []
        for h in range(4):
            sl = slice(h * 256, (h + 1) * 256)
            vs = slice(D + h * 256, D + (h + 1) * 256)
            qh = q_ref[:, sl].astype(BF16)
            kh = kv_ref[:, sl].astype(BF16)
            vh = kv_ref[:, vs].astype(BF16)
            doh = do_ref[:, sl].astype(BF16)
            p = _xattn_probs(qh, kh)
            dp = _nt(doh, vh)
            ds = (p * (dp - jnp.sum(p * dp, axis=-1, keepdims=True)) * 0.0625).astype(BF16)
            done.append((sl, vs, _nn(ds, kh), _tn(ds, qh), _tn(p.astype(BF16), doh)))
        for sl, vs, dq, dk, dv in done:
            dq_ref[:, sl] = dq.astype(dq_ref.dtype)
            dkv_ref[:, sl] += dk
            dkv_ref[:, vs] += dv

    return pl.pallas_call(
        body, name="xattn_bwd", grid=(S // TQ,),
        in_specs=[pl.BlockSpec((TQ, D), lambda i: (i, 0)), pl.BlockSpec((MEM, 2 * D), lambda i: (0, 0)),
                  pl.BlockSpec((TQ, D), lambda i: (i, 0))],
        out_specs=[pl.BlockSpec((TQ, D), lambda i: (i, 0)), pl.BlockSpec((MEM, 2 * D), lambda i: (0, 0))],
        out_shape=[jax.ShapeDtypeStruct((S, D), BF16), jax.ShapeDtypeStruct((MEM, 2 * D), F32)],
        compiler_params=_cparams(("arbitrary",)),
    )(q, kv, d_o)


def _s5_disc(a_re, a_im, log_dt, b_re, b_im):
    dt = jnp.exp(log_dt)[:, None]
    mag = jnp.exp(dt * a_re)
    abr = mag * jnp.cos(dt * a_im)
    abi = mag * jnp.sin(dt * a_im)
    nr, ni = abr - 1.0, abi
    inv = 1.0 / (a_re * a_re + a_im * a_im)
    cr = (nr * a_re + ni * a_im) * inv
    ci = (ni * a_re - nr * a_im) * inv
    bbr = cr[..., None] * b_re - ci[..., None] * b_im
    bbi = cr[..., None] * b_im + ci[..., None] * b_re
    return abr, abi, bbr, bbi


VM = pl.BlockSpec(memory_space=pltpu.VMEM)


def s5_embed(bt_re, bt_im, ct_re, ct_im):
    def body(br, bi, cr, ci, b_ref, c_ref):
        b_ref[...] = jnp.zeros_like(b_ref)
        c_ref[...] = jnp.zeros_like(c_ref)
        for g in range(NG):
            rows, cols = slice(g * NH, (g + 1) * NH), slice(g * NP, (g + 1) * NP)
            b_ref[rows, cols] = br[g]
            b_ref[rows, NS + g * NP:NS + (g + 1) * NP] = bi[g]
            c_ref[cols, rows] = cr[g]
            c_ref[NS + g * NP:NS + (g + 1) * NP, rows] = -ci[g]

    return pl.pallas_call(
        body, name="s5_embed", in_specs=[VM] * 4, out_specs=[VM] * 2,
        out_shape=[jax.ShapeDtypeStruct((NG * NH, 2 * NS), F32), jax.ShapeDtypeStruct((2 * NS, NG * NH), F32)],
        compiler_params=pltpu.CompilerParams(vmem_limit_bytes=VMEM_LIMIT),
    )(bt_re, bt_im, ct_re, ct_im)


def s5_extract(gb, gc):
    def body(gb_ref, gc_ref, br, bi, cr, ci):
        for g in range(NG):
            rows, cols = slice(g * NH, (g + 1) * NH), slice(g * NP, (g + 1) * NP)
            br[g] = gb_ref[rows, cols]
            bi[g] = gb_ref[rows, NS + g * NP:NS + (g + 1) * NP]
            cr[g] = gc_ref[cols, rows]
            ci[g] = -gc_ref[NS + g * NP:NS + (g + 1) * NP, rows]

    return pl.pallas_call(
        body, name="s5_extract", in_specs=[VM] * 2, out_specs=[VM] * 4,
        out_shape=[jax.ShapeDtypeStruct((NG, NH, NP), F32)] * 2 + [jax.ShapeDtypeStruct((NG, NP, NH), F32)] * 2,
        compiler_params=pltpu.CompilerParams(vmem_limit_bytes=VMEM_LIMIT),
    )(gb, gc)


HC, HS = NG * NH // 2, NS // 2
TS = 1024


def s5_to_states(x, w, mode, name, z_off=0):
    if mode == "nn":
        wb, wm = (HC, HS), lambda i, j, kk: (j % 2, j)
    else:
        wb, wm = (HS, HC), lambda i, j, kk: (j, j % 2)
    return mm_band(x, w, mode, name, (S // TS, 4, 1), ((TS, HC), wb, (TS, HS)),
                   (lambda i, j, kk: (i, z_off + j % 2), wm, lambda i, j, kk: (i, j)), (S, 2 * NS))


def s5_to_channels(x, w, mode, name, add=None):
    if mode == "nn":
        wb, wm = (HS, HC), lambda i, j, kk: (j + 2 * kk, j)
    else:
        wb, wm = (HC, HS), lambda i, j, kk: (j, j + 2 * kk)
    return mm_band(x, w, mode, name, (S // TS, 2, 2), ((TS, HS), wb, (TS, HC)),
                   (lambda i, j, kk: (i, j + 2 * kk), wm, lambda i, j, kk: (i, j)), (S, NG * NH), add=add)


def s5_outer(a, b, name, states_first, z_off=0):
    if states_first:
        return mm_band(a, b, "tn", name, (4, 1, 1), ((S, HS), (S, HC), (HS, HC)),
                       (lambda i, j, kk: (0, i), lambda i, j, kk: (0, i % 2), lambda i, j, kk: (i, i % 2)),
                       (2 * NS, NG * NH))
    return mm_band(a, b, "tn", name, (1, 4, 1), ((S, HC), (S, HS), (HC, HS)),
                   (lambda i, j, kk: (0, z_off + j % 2), lambda i, j, kk: (0, j), lambda i, j, kk: (j % 2, j)),
                   (NG * NH, 2 * NS))


def _fwd_even(i, x, P, W):
    hn = rms_fwd(x, P["norm_ab"][i:i + 1], "rms_ab_fwd")
    z = mm(m2(hn), W["w_in"], "nn", "in_ab")
    o, lse, cat = attn_fwd(z)
    if "more" in W:
        W.update(W.pop("more")(cat))
    cat = pool_fwd(z, W["pool_w"], P["pool_scale"][i:i + 1], cat)
    x_mid = mm(m2(cat), W["w_out"], "nn", "out_ab", add=m2(x))
    return x_mid, dict(x=x, hn=hn, z=z, o=o, lse=lse, cat=cat)


def _bwd_even(i, dx_mid, sv, P, W, G, GW):
    z = sv["z"]
    d_cat = mm(m2(dx_mid), W["w_out"], "nt", "out_ab_dx")
    GW["w_out"] = mm(m2(sv["cat"]), m2(dx_mid), "tn", "out_ab_dw").reshape(4, 512, D)
    dq, dk, dv, dga = attn_bwd(z, d_cat, sv["o"], sv["lse"])
    dvb, dgb, dpw, dps = pool_bwd(z, d_cat, W["pool_w"], P["pool_scale"][i:i + 1])
    GW["pool_w"] = dpw.reshape(4, 4, 64, 256).transpose(1, 0, 2, 3).reshape(4, 256, 256)
    G["pool_scale"][i] = dps[0]
    d_z = assemble_dz_even((dq, dk, dv, dga, dvb, dgb))
    d_hn = mm(m2(d_z), W["w_in"], "nt", "in_ab_dx")
    GW["w_in"] = mm(m2(sv["hn"]), m2(d_z), "tn", "in_ab_dw", out=outcs(D, 1536))
    return d_hn, P["norm_ab"][i:i + 1], "norm_ab", "rms_ab_bwd"


def _fwd_odd(i, x, P, W):
    hn = rms_fwd(x, P["norm_cd"][i:i + 1], "rms_cd_fwd")
    z = mm(m2(hn), W["w_in"], "nn", "in_cd")
    bfull = jnp.repeat(P["sgu_b"][i].T, 256, axis=1)
    c_out = sgu_fwd(z, P["sgu_ln_g"][i:i + 1], P["sgu_ln_b"][i:i + 1], P["sgu_w"][i], bfull)
    disc, disc_vjp = jax.vjp(_s5_disc, P["s5_a_re"][i], P["s5_a_im"][i], P["s5_log_dt"][i],
                             P["s5_b_re"][i], P["s5_b_im"][i])
    abr, abi, bbr, bbi = disc
    bbd, cbd = s5_embed(bbr.transpose(0, 2, 1), bbi.transpose(0, 2, 1),
                        P["s5_c_re"][i].transpose(0, 2, 1), P["s5_c_im"][i].transpose(0, 2, 1))
    abr, abi = abr.reshape(1, NS), abi.reshape(1, NS)
    bu = s5_to_states(z, bbd, "nn", "s5_bu", z_off=3072 // HC)
    h = scan_fwd(bu, abr, abi)
    hc = s5_to_channels(h, cbd, "nn", "s5_hc")
    dskip = P["s5_d"][i:i + 1]
    ypre, yg = s5_post(hc, z, dskip)
    if "more" in W:
        W.update(W.pop("more")(yg))
    w12 = W["w12"]
    t = mm(m2(yg), m2(w12), "nn", "glu_t")
    cat = glu_fwd(t, z, c_out)
    x_mid = mm(m2(cat), W["w_out"], "nn", "out_cd", add=m2(x))
    return x_mid, dict(x=x, hn=hn, z=z, bfull=bfull, disc_vjp=disc_vjp, bbd=bbd, cbd=cbd, abr=abr,
                       abi=abi, h=h, ypre=ypre, yg=yg, w12=w12, t=t, cat=cat, dskip=dskip)


def _bwd_odd(i, dx_mid, sv, P, W, G, GW):
    z = sv["z"]
    d_cat = mm(m2(dx_mid), W["w_out"], "nt", "out_cd_dx")
    GW["w_out"] = mm(m2(sv["cat"]), m2(dx_mid), "tn", "out_cd_dw").reshape(4, 384, D)
    du, dv, dgc, dws, dbs, dlg, dlb = sgu_bwd(z, d_cat, P["sgu_ln_g"][i:i + 1], P["sgu_ln_b"][i:i + 1],
                                               P["sgu_w"][i], sv["bfull"])
    G["sgu_w"][i], G["sgu_b"][i] = dws, dbs[:, :4].T
    G["sgu_ln_g"][i], G["sgu_ln_b"][i] = dlg[0], dlb[0]
    dt, dgd = glu_bwd(sv["t"], z, d_cat)
    gw12 = mm(m2(sv["yg"]), m2(dt), "tn", "glu_dw")
    GW["glu_w1"] = gw12[:, :512].reshape(4, 128, 512)
    GW["glu_w2"] = gw12[:, 512:].reshape(4, 128, 512)
    dyg = mm(m2(dt), m2(sv["w12"]), "nt", "glu_dx")
    dypre, dxd1, dd = s5_post_bwd(dyg, sv["ypre"], z, sv["dskip"])
    G["s5_d"][i] = dd[0]
    gcbd = s5_outer(sv["h"], dypre, "s5_dc", states_first=True)
    eta = s5_to_states(dypre, sv["cbd"], "nt", "s5_eta")
    lam, dacc = scan_bwd(eta, sv["h"], sv["abr"], sv["abi"])
    gbbd = s5_outer(z, lam, "s5_db", states_first=False, z_off=3072 // HC)
    dxd = s5_to_channels(lam, sv["bbd"], "nt", "s5_dx", add=dxd1)
    dacc = jnp.sum(dacc, axis=0)
    dbt_re, dbt_im, dct_re, dct_im = s5_extract(gbbd, gcbd)
    G["s5_c_re"][i], G["s5_c_im"][i] = dct_re.transpose(0, 2, 1), dct_im.transpose(0, 2, 1)
    d_bbr, d_bbi = dbt_re.transpose(0, 2, 1), dbt_im.transpose(0, 2, 1)
    (G["s5_a_re"][i], G["s5_a_im"][i], G["s5_log_dt"][i], G["s5_b_re"][i], G["s5_b_im"][i]) = sv["disc_vjp"](
        (dacc[:NS].reshape(NG, NP), dacc[NS:].reshape(NG, NP), d_bbr, d_bbi))
    d_z = assemble_dz_odd(du, dv, dgc, dxd, dgd)
    d_hn = mm(m2(d_z), W["w_in"], "nt", "in_cd_dx")
    GW["w_in"] = mm(m2(sv["hn"]), m2(d_z), "tn", "in_cd_dw", out=outcs(D, 1024))
    return d_hn, P["norm_cd"][i:i + 1], "norm_cd", "rms_cd_bwd"


def _fwd_x(l, x, mem_n, P, W):
    if "more_x" in W:
        W.update(W.pop("more_x")(x))
    hx = rms_fwd(x, P["norm_x"][l:l + 1], "rms_x_fwd")
    q = mm(m2(hx), W["w_xq"], "nn", "xq", out_dtype=BF16)
    kv = mm(m2(mem_n), W["w_xkv"], "nn", "xkv", out_dtype=BF16)
    ox = xattn_fwd(q, kv)
    x_out = mm(m2(ox), W["w_xo"], "nn", "xo", add=m2(x))
    return x_out, dict(x=x, hx=hx, q=q, kv=kv, ox=ox)


def _bwd_x(l, dx_out, sv, mem_n, d_memn, P, W, G, GW):
    d_ox = mm(m2(dx_out), W["w_xo"], "nt", "xo_dx", out_dtype=BF16)
    GW["w_xo"] = mm(m2(sv["ox"]), m2(dx_out), "tn", "xo_dw").reshape(4, 256, D)
    dq, dkv = xattn_bwd(sv["q"], sv["kv"], d_ox)
    GW["w_xq"] = mm(m2(sv["hx"]), m2(dq), "tn", "xq_dw").reshape(4, 256, D)
    d_hx = mm(m2(dq), W["w_xq"], "nt", "xq_dx")
    GW["w_xkv"] = mm(m2(mem_n), m2(dkv), "tn", "xkv_dw", out=outcs(D, 512))
    d_memn = mm(m2(dkv), W["w_xkv"], "nt", "xkv_dx", add=None if d_memn is None else m2(d_memn))
    dx, dg = rms_bwd(sv["x"], d_hx, dx_out, P["norm_x"][l:l + 1], "rms_x_bwd")
    G["norm_x"][l] = dg[0]
    return dx, d_memn


SMALL_LAYERS = (("norm_ab", 2), ("pool_scale", 2), ("norm_cd", 2), ("sgu_ln_g", 2), ("sgu_ln_b", 2), ("sgu_w", 2),
                ("sgu_b", 2), ("s5_a_re", 2), ("s5_a_im", 2), ("s5_log_dt", 2), ("s5_b_re", 2), ("s5_b_im", 2),
                ("s5_c_re", 2), ("s5_c_im", 2), ("s5_d", 2), ("norm_x", 4))


def local_step(x, mem, tgt, P, weights_of, grads_done):
    G = {k: [None] * n for k, n in SMALL_LAYERS}
    mem_g = P["mem_norm"].reshape(1, D)
    mem_n = rms_fwd(mem, mem_g, "rms_mem_fwd")
    saved = []
    for layer in range(4):
        i = layer // 2
        W = weights_of(layer, x)
        x, sv_m = (_fwd_even if layer % 2 == 0 else _fwd_odd)(i, x, P, W)
        x, sv_x = _fwd_x(layer, x, mem_n, P, W)
        saved.append((sv_m, sv_x, W))
    dx, loss, dgf = final_loss(x, tgt, P["final_norm"].reshape(1, D))
    G["final_norm"] = dgf[0]
    d_memn = None
    for layer in reversed(range(4)):
        i = layer // 2
        sv_m, sv_x, W = saved[layer]
        GW = {}
        dx_mid, d_memn = _bwd_x(layer, dx, sv_x, mem_n, d_memn, P, W, G, GW)
        d_hn, g, key, name = (_bwd_even if layer % 2 == 0 else _bwd_odd)(i, dx_mid, sv_m, P, W, G, GW)
        token = grads_done(layer, GW)
        if token is not None:
            g = g + token
        dx, dg = rms_bwd(sv_m["x"], d_hn, dx_mid, g, name)
        G[key][i] = dg[0]
    _, dgm = rms_bwd(mem, d_memn, d_memn, mem_g, "rms_mem_bwd")
    G["mem_norm"] = dgm[0]
    return loss, dx, G


ANY = pl.BlockSpec(memory_space=pl.ANY)


def _place():
    x, y, c = lax.axis_index("x"), lax.axis_index("y"), lax.axis_index("c")
    chips = [(1 - x, y), (x, 1 - y), (1 - x, 1 - y)]
    return x, y, c, 2 * x + y, (x, y, 1 - c), chips


def _remote(src, dst, send, recv, k, dev):
    return pltpu.make_async_remote_copy(src_ref=src, dst_ref=dst, send_sem=send.at[k], recv_sem=recv.at[k],
                                        device_id=dev, device_id_type=MESHID)


HBM = pl.BlockSpec(memory_space=pltpu.HBM)
SEM = pl.BlockSpec(memory_space=pltpu.SEMAPHORE)
EFFECT = pltpu.SideEffectType.DATAFLOW_SIDE_EFFECTING


def _hbm(t):
    return pltpu.with_memory_space_constraint(t, pltpu.HBM)


def _first_copies(shard, land, send, recv, arriving):
    x, y, c, jme, sib, chips = _place()
    out = []
    for k, chip in enumerate(chips):
        slot = 2 * chip[0] + chip[1] if arriving else jme
        out.append(_remote(shard.at[c], land.at[slot, c], send, recv, k, (*chip, c)))
    out.append(_remote(shard, land.at[jme], send, recv, 3, sib))
    return out


def allgather_first_start(shard, after, name):
    def body(s_ref, l_ref, *rest):
        send, recv, token = rest[len(after)], rest[len(after) + 1], rest[-1]
        for cp in _first_copies(s_ref, l_ref, send, recv, False):
            cp.start()
        token[...] = jnp.zeros_like(token)

    land = (4,) + shard.shape
    return pl.pallas_call(
        body, name=name,
        out_shape=(pltpu.SemaphoreType.DMA((4,)), pltpu.SemaphoreType.DMA((4,)), pltpu.HBM(shard.shape, shard.dtype),
                   pltpu.HBM(land, shard.dtype), jax.ShapeDtypeStruct((8, 128), F32)),
        in_specs=[HBM, HBM] + [ANY] * len(after),
        out_specs=(SEM, SEM, HBM, HBM, pl.BlockSpec(memory_space=pltpu.VMEM)),
        input_output_aliases={0: 2, 1: 3},
        compiler_params=pltpu.CompilerParams(has_side_effects=EFFECT),
    )(_hbm(shard), _hbm(lax.empty(land, shard.dtype)), *after)


def allgather_first_wait(send, recv, shard, land, after, name):
    def body(s_ref, l_ref, send_r, recv_r, *rest):
        for cp in _first_copies(s_ref, l_ref, send_r, recv_r, True):
            cp.wait_send()
            cp.wait_recv()

    res = pl.pallas_call(
        body, name=name, out_shape=(pltpu.HBM(shard.shape, shard.dtype), pltpu.HBM(land.shape, land.dtype)),
        in_specs=[HBM, HBM, SEM, SEM] + [ANY] * len(after), out_specs=(HBM, HBM),
        input_output_aliases={0: 0, 1: 1},
        compiler_params=pltpu.CompilerParams(has_side_effects=EFFECT),
    )(shard, land, send, recv, *after)
    return res[1]


def allgather_forward(land):
    def body(l_in, l_ref, send, recv):
        x, y, c, jme, sib, chips = _place()
        cps = []
        for k, chip in enumerate(chips):
            piece = l_ref.at[2 * chip[0] + chip[1], c]
            cp = _remote(piece, piece, send, recv, k, sib)
            cp.start()
            cps.append(cp)
        for k, chip in enumerate(chips):
            piece = l_ref.at[2 * chip[0] + chip[1], 1 - c]
            _remote(piece, piece, send, recv, k, sib).wait_recv()
        for cp in cps:
            cp.wait_send()

    return pl.pallas_call(
        body, name="allgather_forward", in_specs=[ANY], out_specs=ANY,
        out_shape=jax.ShapeDtypeStruct(land.shape, land.dtype), input_output_aliases={0: 0},
        scratch_shapes=[pltpu.SemaphoreType.DMA((3,)), pltpu.SemaphoreType.DMA((3,))],
    )(land)


def _gather_copies(ins, lands, send, recv):
    x, y, c, jme, sib, chips = _place()
    devs = [(*chip, c) for chip in chips] + [sib]
    return [_remote(ins[a], lands[a].at[jme], send, recv, a * 4 + k, dev)
            for a in range(len(ins)) for k, dev in enumerate(devs)]


def allgather_start(shards, after, name):
    n, na = len(shards), len(after)

    def body(*refs):
        ins, lands = refs[:n], refs[n:2 * n]
        send, recv = refs[2 * n + na], refs[2 * n + na + 1]
        token = refs[-1]
        for cp in _gather_copies(ins, lands, send, recv):
            cp.start()
        token[...] = jnp.zeros_like(token)

    res = pl.pallas_call(
        body, name=name,
        out_shape=(pltpu.SemaphoreType.DMA((4 * n,)), pltpu.SemaphoreType.DMA((4 * n,)),
                   *[pltpu.HBM(s.shape, s.dtype) for s in shards],
                   *[pltpu.HBM((4,) + s.shape, s.dtype) for s in shards],
                   jax.ShapeDtypeStruct((8, 128), F32)),
        in_specs=[HBM] * (2 * n) + [ANY] * na,
        out_specs=(SEM, SEM, *[HBM] * (2 * n), pl.BlockSpec(memory_space=pltpu.VMEM)),
        input_output_aliases={a: 2 + a for a in range(2 * n)},
        compiler_params=pltpu.CompilerParams(has_side_effects=EFFECT),
    )(*[_hbm(s) for s in shards], *[_hbm(lax.empty((4,) + s.shape, s.dtype)) for s in shards], *after)
    return res[0], res[1], list(res[2:2 + n]), list(res[2 + n:2 + 2 * n]), res[-1]


def allgather_wait(send, recv, shards, lands, after, name):
    n = len(shards)

    def body(*refs):
        ins, zones = refs[:n], refs[n:2 * n]
        send_r, recv_r = refs[2 * n], refs[2 * n + 1]
        x, y, c, jme, sib, chips = _place()
        slots = [2 * chip[0] + chip[1] for chip in chips] + [jme]
        for a in range(n):
            for k, slot in enumerate(slots):
                cp = _remote(ins[a], zones[a].at[slot], send_r, recv_r, a * 4 + k, sib)
                cp.wait_send()
                cp.wait_recv()

    res = pl.pallas_call(
        body, name=name,
        out_shape=tuple(pltpu.HBM(t.shape, t.dtype) for t in list(shards) + list(lands)),
        in_specs=[HBM] * (2 * n) + [SEM, SEM, ANY], out_specs=tuple([HBM] * (2 * n)),
        input_output_aliases={a: a for a in range(2 * n)},
        compiler_params=pltpu.CompilerParams(has_side_effects=EFFECT),
    )(*shards, *lands, send, recv, after)
    return list(res[n:])


def allgather_small(slab):
    def body(in_ref, out_ref, send, recv, lsem):
        x, y, c, jme, sib, chips = _place()
        loc = pltpu.make_async_copy(in_ref, out_ref.at[jme], lsem.at[0])
        loc.start()
        cps = [_remote(in_ref, out_ref.at[jme], send, recv, k, (*chip, c)) for k, chip in enumerate(chips)]
        for cp in cps:
            cp.start()
        for k, chip in enumerate(chips):
            piece = out_ref.at[2 * chip[0] + chip[1]]
            _remote(piece, piece, send, recv, k, (*chip, c)).wait_recv()
        for cp in cps:
            cp.wait_send()
        loc.wait()

    return pl.pallas_call(
        body, name="allgather_small", in_specs=[ANY], out_specs=ANY,
        out_shape=jax.ShapeDtypeStruct((4,) + slab.shape, slab.dtype),
        scratch_shapes=[pltpu.SemaphoreType.DMA((3,)), pltpu.SemaphoreType.DMA((3,)), pltpu.SemaphoreType.DMA((1,))],
    )(slab)


def allreduce_small(v):
    hr = v.shape[0] // 2

    def body(v_ref, o_ref, r0, r1, r2, send, recv):
        x, y, c, jme, sib, chips = _place()
        mine = pl.ds(pl.multiple_of(c * hr, 8), hr)
        other = pl.ds(pl.multiple_of((1 - c) * hr, 8), hr)
        cp = _remote(v_ref.at[other], r0, send, recv, 0, sib)
        cp.start()
        cp.wait()
        o_ref[mine, :] = v_ref[mine, :] + r0[...]
        for k, (buf, peer) in enumerate(((r1, (1 - x, y, c)), (r2, (x, 1 - y, c))), start=1):
            cp = _remote(o_ref.at[mine], buf, send, recv, k, peer)
            cp.start()
            cp.wait()
            o_ref[mine, :] = o_ref[mine, :] + buf[...]
        cp = _remote(o_ref.at[mine], o_ref.at[mine], send, recv, 3, sib)
        cp.start()
        cp.wait_send()
        _remote(o_ref.at[other], o_ref.at[other], send, recv, 3, sib).wait_recv()

    vm = pl.BlockSpec(memory_space=pltpu.VMEM)
    half = pltpu.VMEM((hr, v.shape[1]), v.dtype)
    return pl.pallas_call(
        body, name="allreduce_small", in_specs=[vm], out_specs=vm,
        out_shape=jax.ShapeDtypeStruct(v.shape, v.dtype),
        scratch_shapes=[half] * 3 + [pltpu.SemaphoreType.DMA((4,)), pltpu.SemaphoreType.DMA((4,))],
        compiler_params=pltpu.CompilerParams(vmem_limit_bytes=VMEM_LIMIT),
    )(v)


def _pair_copies(gs, lands, send, recv):
    x, y, c, jme, sib, chips = _place()
    return [_remote(gs[a].at[:, 1 - c], lands[a], send, recv, a, sib) for a in range(len(gs))]


def rs_pair_start(gs, name):
    n = len(gs)

    def body(*refs):
        ins, lands = refs[:n], refs[n:2 * n]
        send, recv = refs[2 * n], refs[2 * n + 1]
        token = refs[-1]
        for cp in _pair_copies(ins, lands, send, recv):
            cp.start()
        token[...] = jnp.zeros_like(token)

    shapes = [(4,) + g.shape[2:] for g in gs]
    res = pl.pallas_call(
        body, name=name,
        out_shape=(pltpu.SemaphoreType.DMA((n,)), pltpu.SemaphoreType.DMA((n,)),
                   *[pltpu.HBM(g.shape, g.dtype) for g in gs], *[pltpu.HBM(s, F32) for s in shapes],
                   jax.ShapeDtypeStruct((8, 128), F32)),
        in_specs=[HBM] * (2 * n), out_specs=(SEM, SEM, *[HBM] * (2 * n), pl.BlockSpec(memory_space=pltpu.VMEM)),
        input_output_aliases={a: 2 + a for a in range(2 * n)},
        compiler_params=pltpu.CompilerParams(has_side_effects=EFFECT),
    )(*[_hbm(g) for g in gs], *[_hbm(lax.empty(s, F32)) for s in shapes])
    return res[0], res[1], list(res[2:2 + n]), list(res[2 + n:2 + 2 * n]), res[-1]


def rs_pair_wait(send, recv, gs, lands, after, name):
    n = len(gs)

    def body(*refs):
        ins, zones = refs[:n], refs[n:2 * n]
        for cp in _pair_copies(ins, zones, refs[2 * n], refs[2 * n + 1]):
            cp.wait_send()
            cp.wait_recv()

    res = pl.pallas_call(
        body, name=name,
        out_shape=tuple(pltpu.HBM(t.shape, t.dtype) for t in list(gs) + list(lands)),
        in_specs=[HBM] * (2 * n) + [SEM, SEM, ANY], out_specs=tuple([HBM] * (2 * n)),
        input_output_aliases={a: a for a in range(2 * n)},
        compiler_params=pltpu.CompilerParams(has_side_effects=EFFECT),
    )(*gs, *lands, send, recv, after)
    return list(res[:n]), list(res[n:])


SUM_ROWS = 256


def rs_pair_sum(g4s, gots, cidx):
    n = len(g4s)
    tiles = [(min(g.shape[2], SUM_ROWS), g.shape[3]) for g in g4s]
    nts = [g.shape[2] // tr for g, (tr, _) in zip(g4s, tiles)]

    def at(a, s):
        s = jnp.minimum(s, 4 * nts[a] - 1)
        return s // nts[a], s % nts[a]

    def body(c_ref, *refs):
        for a in range(n):
            refs[2 * n + a][...] = (refs[a][...] + refs[n + a][...]).astype(BF16)

    in_specs = [pl.BlockSpec((None, None) + tiles[a], lambda s, cr, a=a: (at(a, s)[0], cr[0], at(a, s)[1], 0))
                for a in range(n)]
    in_specs += [pl.BlockSpec((None,) + tiles[a], lambda s, cr, a=a: (*at(a, s), 0)) for a in range(n)]
    return pl.pallas_call(
        body, name="rs_pair_sum",
        grid_spec=pltpu.PrefetchScalarGridSpec(
            num_scalar_prefetch=1, grid=(4 * max(nts),), in_specs=in_specs,
            out_specs=[pl.BlockSpec((None,) + tiles[a], lambda s, cr, a=a: (*at(a, s), 0)) for a in range(n)]),
        out_shape=[jax.ShapeDtypeStruct((4,) + g.shape[2:], BF16) for g in g4s],
        compiler_params=_cparams(("arbitrary",)),
    )(cidx, *g4s, *gots)


def _chip_copies(ps, lands, send, recv):
    x, y, c, jme, sib, chips = _place()
    return [_remote(ps[a].at[2 * chip[0] + chip[1]], lands[a].at[jme], send, recv, a * 3 + k, (*chip, c))
            for a in range(len(ps)) for k, chip in enumerate(chips)]


def rs_chip_start(ps, name):
    n = len(ps)

    def body(*refs):
        ins, lands = refs[:n], refs[n:2 * n]
        send, recv = refs[2 * n], refs[2 * n + 1]
        token = refs[-1]
        for cp in _chip_copies(ins, lands, send, recv):
            cp.start()
        token[...] = jnp.zeros_like(token)

    res = pl.pallas_call(
        body, name=name,
        out_shape=(pltpu.SemaphoreType.DMA((3 * n,)), pltpu.SemaphoreType.DMA((3 * n,)),
                   *[pltpu.HBM(p.shape, p.dtype) for p in ps], *[pltpu.HBM(p.shape, p.dtype) for p in ps],
                   jax.ShapeDtypeStruct((8, 128), F32)),
        in_specs=[HBM] * (2 * n), out_specs=(SEM, SEM, *[HBM] * (2 * n), pl.BlockSpec(memory_space=pltpu.VMEM)),
        input_output_aliases={a: 2 + a for a in range(2 * n)},
        compiler_params=pltpu.CompilerParams(has_side_effects=EFFECT),
    )(*[_hbm(p) for p in ps], *[_hbm(lax.empty(p.shape, p.dtype)) for p in ps])
    return res[0], res[1], list(res[2:2 + n]), list(res[2 + n:2 + 2 * n]), res[-1]


def rs_chip_wait(send, recv, ps, lands, after, name):
    n = len(ps)

    def body(*refs):
        ins, zones = refs[:n], refs[n:2 * n]
        send_r, recv_r = refs[2 * n], refs[2 * n + 1]
        x, y, c, jme, sib, chips = _place()
        for a in range(n):
            for k, chip in enumerate(chips):
                jt = 2 * chip[0] + chip[1]
                cp = _remote(ins[a].at[jt], zones[a].at[jt], send_r, recv_r, a * 3 + k, (*chip, c))
                cp.wait_send()
                cp.wait_recv()

    res = pl.pallas_call(
        body, name=name,
        out_shape=tuple(pltpu.HBM(p.shape, p.dtype) for p in list(ps) + list(lands)),
        in_specs=[HBM] * (2 * n) + [SEM, SEM] + [ANY] * len(after), out_specs=tuple([HBM] * (2 * n)),
        input_output_aliases={a: a for a in range(2 * n)},
        compiler_params=pltpu.CompilerParams(has_side_effects=EFFECT),
    )(*ps, *lands, send, recv, *after)
    return list(res[:n]), list(res[n:])


def rs_chip_sum(qs, ps, ls, accs, layers, jc):
    n = len(qs)
    tiles = [(min(q.shape[1], SUM_ROWS), q.shape[2]) for q in qs]
    nts = [q.shape[1] // tr for q, (tr, _) in zip(qs, tiles)]

    def at(a, s):
        return jnp.minimum(s, nts[a] - 1)

    def body(jc_ref, *refs):
        jme = jc_ref[0]
        for a in range(n):
            q_ref, p_ref, o_ref = refs[a], refs[n + a], refs[len(refs) - n + a]
            own = p_ref[...].astype(F32)
            v = [jnp.where(jme == j, own, q_ref[j].astype(F32)) for j in range(4)]
            o_ref[...] = ((v[0] + v[1]) + v[2]) + v[3]

    in_specs = [pl.BlockSpec((4,) + tiles[a], lambda s, jr, a=a: (0, at(a, s), 0)) for a in range(n)]
    in_specs += [pl.BlockSpec((None,) + tiles[a], lambda s, jr, a=a: (jr[0], at(a, s), 0)) for a in range(n)]
    args, aliases = [jc, *qs, *ps], {}
    for a in range(n):
        if accs[a] is not None:
            aliases[len(args)] = a
            in_specs.append(ANY)
            args.append(accs[a])
    return pl.pallas_call(
        body, name="rs_chip_sum",
        grid_spec=pltpu.PrefetchScalarGridSpec(
            num_scalar_prefetch=1, grid=(max(nts),), in_specs=in_specs,
            out_specs=[pl.BlockSpec((None, None) + tiles[a], lambda s, jr, a=a: (ls[a], jr[1], at(a, s), 0))
                       for a in range(n)]),
        out_shape=[jax.ShapeDtypeStruct((layers[a], 2) + qs[a].shape[1:], F32) for a in range(n)],
        input_output_aliases=aliases,
        compiler_params=_cparams(("arbitrary",)),
    )(*args)


def rs_pair_gather(rs):
    n = len(rs)

    def body(*refs):
        outs = refs[n:2 * n]
        send, recv = refs[2 * n:]
        x, y, c, jme, sib, chips = _place()
        cps = [_remote(outs[a].at[:, c], outs[a].at[:, c], send, recv, a, sib) for a in range(n)]
        for cp in cps:
            cp.start()
        for a in range(n):
            slot = outs[a].at[:, 1 - c]
            _remote(slot, slot, send, recv, a, sib).wait_recv()
        for cp in cps:
            cp.wait_send()

    return pl.pallas_call(
        body, name="rs_pair_gather", in_specs=[ANY] * n, out_specs=[ANY] * n,
        out_shape=[jax.ShapeDtypeStruct(r.shape, r.dtype) for r in rs],
        input_output_aliases={a: a for a in range(n)},
        scratch_shapes=[pltpu.SemaphoreType.DMA((n,)), pltpu.SemaphoreType.DMA((n,))],
    )(*rs)


def _adamw_math(w, g, m, v):
    m = B1 * m + (1.0 - B1) * g
    v = B2 * v + (1.0 - B2) * (g * g)
    m_hat = m / (1.0 - B1 ** STEP)
    v_hat = v / (1.0 - B2 ** STEP)
    return -LR * (m_hat / (jnp.sqrt(v_hat) + AEPS) + WD * w), m, v


ADAMW_TILE = 512 * 1024


def adamw(w, g, m, v, name, with_grad=False):
    rows, cols = w.shape
    tr = next((t for t in (1024, 512, 256) if rows % t == 0 and t * cols <= ADAMW_TILE), rows)
    fn =(lambda wv, gv, mv, vv: (gv,) + _adamw_math(wv, gv, mv, vv)) if with_grad else _adamw_math
    return rw(fn, [(a, 0, cols) for a in (w, g, m, v)], [(cols, F32)] * (4 if with_grad else 3), name, rows, tr=tr)


def adamw_small(ws, gs, ms, vs):
    n = len(ws)

    def body(*refs):
        for a in range(n):
            res = _adamw_math(*[refs[k * n + a][...] for k in range(4)])
            for k in range(3):
                refs[(4 + k) * n + a][...] = res[k]

    res = pl.pallas_call(
        body, name="adamw_small", in_specs=[VM] * (4 * n), out_specs=[VM] * (3 * n),
        out_shape=[jax.ShapeDtypeStruct(w.shape, F32) for _ in range(3) for w in ws],
        compiler_params=pltpu.CompilerParams(vmem_limit_bytes=VMEM_LIMIT),
    )(*ws, *gs, *ms, *vs)
    return [(res[a], res[n + a], res[2 * n + a]) for a in range(n)]


WEIGHTS = ["norm_ab", "w_in_ab", "pool_w", "pool_scale", "w_out_ab", "norm_cd", "w_in_cd", "sgu_ln_g", "sgu_ln_b",
           "sgu_w", "sgu_b", "s5_a_re", "s5_a_im", "s5_log_dt", "s5_b_re", "s5_b_im", "s5_c_re", "s5_c_im", "s5_d",
           "glu_w1", "glu_w2", "w_out_cd", "norm_x", "w_xq", "w_xkv", "w_xo", "mem_norm", "final_norm"]
INPUTS = ["x", "mem"] + WEIGHTS + ["loss_target"] + ["m_" + n for n in WEIGHTS] + ["v_" + n for n in WEIGHTS]
BIG = ["w_in_ab", "w_out_ab", "w_in_cd", "w_out_cd", "w_xq", "w_xkv", "w_xo", "glu_w1", "glu_w2", "pool_w"]
COL_SHARDED = ("w_in_ab", "w_in_cd", "w_xkv")
SMALL = [n for n in WEIGHTS if n not in BIG]
SMALL_SHARDED = {"norm_cd": 256, "sgu_ln_g": 256, "sgu_ln_b": 256, "s5_d": 128}
PACK = 256 * 128


def _pack(arrs):
    flat = jnp.concatenate([a.reshape(-1) for a in arrs])
    pad = (-flat.shape[0]) % PACK
    return jnp.concatenate([flat, jnp.zeros((pad,), flat.dtype)]).reshape(-1, 128)


def _unpack(packed, shapes):
    flat, out, off = packed.reshape(-1), [], 0
    for s in shapes:
        n = 1
        for d in s:
            n *= d
        out.append(flat[off:off + n].reshape(s))
        off += n
    return out


LAYER_KEYS = (("w_in", "w_out", "pool_w", "w_xq", "w_xkv", "w_xo"),
              ("w_in", "w_out", "glu_w1", "glu_w2", "w_xq", "w_xkv", "w_xo"))


def _weight_of(key, layer):
    if key in ("w_xq", "w_xkv", "w_xo"):
        return key, layer, 4
    kind = "ab" if layer % 2 == 0 else "cd"
    return {"w_in": "w_in_" + kind, "w_out": "w_out_" + kind}.get(key, key), layer // 2, 2


def kernel(*args):
    a = dict(zip(INPUTS, args))
    x_i, y_i, c_i = lax.axis_index("x"), lax.axis_index("y"), lax.axis_index("c")
    j = 2 * x_i + y_i

    slab = jnp.concatenate([a["norm_cd"], a["sgu_ln_g"], a["sgu_ln_b"],
                            jnp.pad(a["s5_d"], ((0, 0), (0, 128)))], axis=0)
    gslab = allgather_small(slab)
    P = {n: a[n] for n in SMALL}
    for k, n in enumerate(("norm_cd", "sgu_ln_g", "sgu_ln_b", "s5_d")):
        wd = SMALL_SHARDED[n]
        P[n] = gslab[:, 2 * k:2 * k + 2, :wd].transpose(1, 0, 2).reshape(2, 4 * wd)

    def shards_of(layer):
        keys = sorted(k for k in LAYER_KEYS[layer % 2])
        out = []
        for k in keys:
            n, l, _ = _weight_of(k, layer)
            out.append(a[n][l].reshape(-1, a[n].shape[-1]).astype(BF16))
        return keys, out

    keys0, sh0 = shards_of(0)
    first = keys0.index("w_in")
    first_of = {0: allgather_first_start(
        sh0[first].reshape(2, sh0[first].shape[0] // 2, sh0[first].shape[1]), [gslab], "allgather_start_0in")}
    token = first_of[0][-1]
    started = {}
    for layer in (0, 1, 2, 3):
        keys, sh = (keys0, sh0) if layer == 0 else shards_of(layer)
        mix = [(k, s) for k, s in zip(keys, sh) if k != "w_in" and not k.startswith("w_x")]
        xat = [(k, s) for k, s in zip(keys, sh) if k.startswith("w_x")]
        if layer > 0:
            w_in = sh[keys.index("w_in")]
            first_of[layer] = allgather_first_start(w_in.reshape(2, w_in.shape[0] // 2, w_in.shape[1]), [token, gslab],
                                                    "allgather_start_%din" % layer)
            token = first_of[layer][-1]
        for tag, pk, ps in (("", *map(list, zip(*mix))), ("x", *map(list, zip(*xat)))):
            send, recv, ps, lands, token = allgather_start(ps, [token, gslab], "allgather_start_%d%s" % (layer, tag))
            started[(layer, tag)] = (pk, send, recv, ps, lands)

    def first_arrived(layer, after):
        f_send, f_recv, f_shard, f_land, _ = first_of[layer]
        g = allgather_forward(allgather_first_wait(f_send, f_recv, f_shard, f_land, after, "allgather_wait_%din" % layer))
        return views({"w_in": g.reshape(4, -1, g.shape[-1])})


    cidx = jnp.reshape(c_i, (1,)).astype(jnp.int32)
    jc = jnp.stack([j, c_i]).astype(jnp.int32)

    def views(g):
        W = {}
        for k, v in g.items():
            if k in ("w_in", "w_xkv"):
                W[k] = mcs(v)
            elif k == "pool_w":
                W[k] = v.reshape(4, 4, 64, 256).transpose(1, 0, 2, 3).reshape(4, 256, 256)
            elif k not in ("glu_w1", "glu_w2"):
                W[k] = m2(v.reshape(-1, v.shape[-1]))
        if "glu_w1" in g:
            W["w12"] = jnp.concatenate([g["glu_w1"].reshape(512, 512), g["glu_w2"].reshape(512, 512)], axis=1)
        return W

    w_in0 = first_arrived(0, [token])

    def arrived(layer, tag, after):
        keys, send, recv, sh, lands = started[(layer, tag)]
        return views(dict(zip(keys, allgather_wait(send, recv, sh, lands, after, "allgather_wait_%d%s" % (layer, tag)))))

    def weights_of(layer, x_in):
        W = dict(w_in0) if layer == 0 else first_arrived(layer, [x_in])
        W["more"] = lambda after: arrived(layer, "", after)
        W["more_x"] = lambda after: arrived(layer, "x", after)
        return W

    halves, pending = {}, {}

    def finish_pair(layer, after):
        keys, send, recv, flat, lands = halves.pop(layer)
        flat, got = rs_pair_wait(send, recv, flat, lands, after, "rs_pair_wait_%d" % layer)
        pair = rs_pair_sum(flat, got, cidx)
        send, recv, pair, lands, token = rs_chip_start(pair, "rs_chip_start_%d" % layer)
        pending[layer] = (keys, send, recv, pair, lands)
        return token

    def grads_done(layer, GW):
        keys = sorted(GW)
        flat = [GW[k].reshape(4, 2, GW[k].shape[1] // 2, GW[k].shape[2]) for k in keys]
        send, recv, flat, lands, token = rs_pair_start(flat, "rs_pair_start_%d" % layer)
        halves[layer] = (keys, send, recv, flat, lands)
        if layer + 1 in halves:
            token = token + finish_pair(layer + 1, token)
        return token[0:1, 0:1]

    loss, dx, G = local_step(a["x"][0], a["mem"][0], a["loss_target"][0], P, weights_of, grads_done)
    loss = lax.psum(loss[0, 0], ("x", "y", "c"))
    finish_pair(0, dx)
    outs = {}

    def update_big(names, red):
        for n, g in zip(names, rs_pair_gather([red[n] for n in names])):
            shp = a[n].shape
            g2 = g.reshape(-1, shp[-1])
            upd = adamw(a[n].reshape(g2.shape), g2, a["m_" + n].reshape(g2.shape), a["v_" + n].reshape(g2.shape),
                        "adamw_" + n, with_grad=True)
            outs[n] = tuple(t.reshape(shp) for t in upd)

    def reduce_layer(layer, red, after):
        keys, send, recv, pair, lands = pending[layer]
        pair, lands = rs_chip_wait(send, recv, pair, lands, after, "rs_chip_wait_%d" % layer)
        which = [_weight_of(k, layer) for k in keys]
        sums = rs_chip_sum(lands, pair, [l for _, l, _ in which], [red.get(n) for n, _, _ in which],
                           [layers for _, _, layers in which], jc)
        red.update(zip([n for n, _, _ in which], sums))

    red = {}
    for layer in (3, 2, 1):
        reduce_layer(layer, red, [dx])
    odd_only = [n for n in BIG if n.endswith("_cd") or n.startswith("glu")]
    update_big(odd_only, red)

    gfull = [jnp.stack(G[n]) if isinstance(G[n], list) else G[n] for n in SMALL]
    shapes = [g.shape for g in gfull]
    gsum = _unpack(allreduce_small(_pack(gfull)), shapes)
    gloc = []
    for n, g in zip(SMALL, gsum):
        if n in SMALL_SHARDED:
            g = lax.dynamic_slice_in_dim(g, j * SMALL_SHARDED[n], SMALL_SHARDED[n], axis=1)
        gloc.append(g)
    two = [(-1, a[n].shape[-1]) if a[n].ndim > 1 else (1, a[n].shape[0]) for n in SMALL]
    upds = adamw_small(*[[t.reshape(s) for t, s in zip(ts, two)]
                         for ts in ([a[n] for n in SMALL], gloc, [a["m_" + n] for n in SMALL],
                                    [a["v_" + n] for n in SMALL])])
    for n, g, upd in zip(SMALL, gloc, upds):
        outs[n] = (g,) + tuple(t.reshape(a[n].shape) for t in upd)

    behind = [outs[n][1] for n in odd_only + SMALL[-1:]] + [red[n] for n in BIG if n not in odd_only]
    reduce_layer(0, red, behind)
    update_big([n for n in BIG if n not in odd_only], red)

    res = [loss, dx[None]]
    for part in range(4):
        res += [outs[n][part] for n in WEIGHTS]
    return tuple(res)
```
